```python
import math
import jax, jax.numpy as jnp
from jax import lax
import numpy as np

D_MODEL = 2048
BATCH = 16
SEQ = 2048
DEPTH = 1

CHUNK = 64
Q_BLOCK = 128
MAX_OFFSET = 65536

MLA_HEADS = 8
Q_LORA = 512
KV_LORA = 256
NOPE_DIM = 128
ROPE_DIM = 64
V_DIM = 128
ROPE_THETA = 10000.0

FOX_HEADS = 8
FOX_HEAD_DIM = 128

D_FF = 5632
CONV_WIDTH = 3

N_BRANCHES = 2
EPS = 1e-6
NEG_INF = -1e30

SPLITS = (
    Q_LORA,
    KV_LORA,
    ROPE_DIM,
    FOX_HEADS * FOX_HEAD_DIM,
    FOX_HEADS * FOX_HEAD_DIM,
    FOX_HEADS * FOX_HEAD_DIM,
    FOX_HEADS,
    N_BRANCHES * D_MODEL,
)
D_IN = sum(SPLITS)

kernel_name = "hybrid_mla_fox_convffn_block"


def rmsnorm(x, g):
    xf = x.astype(jnp.float32)
    y = xf * lax.rsqrt(jnp.mean(xf * xf, axis=-1, keepdims=True) + EPS)
    return (y * g.astype(jnp.float32)).astype(x.dtype)


def rope(x, cos, sin):
    half = x.shape[-1] // 2
    x1, x2 = x[..., :half], x[..., half:]
    return jnp.concatenate([x1 * cos - x2 * sin, x2 * cos + x1 * sin], axis=-1)


def rope_tables(positions, dtype):
    inv_freq = 1.0 / (ROPE_THETA ** (jnp.arange(0, ROPE_DIM, 2, dtype=jnp.float32) / ROPE_DIM))
    ang = positions.astype(jnp.float32)[..., None] * inv_freq
    return jnp.cos(ang).astype(dtype), jnp.sin(ang).astype(dtype)


def block_attention(q, k, v, causal_unit, log_decay=None):
    S = q.shape[1]
    scale = q.shape[-1] ** -0.5
    pos = jnp.arange(S)
    outs = []
    for i in range(S // Q_BLOCK):
        start, end = i * Q_BLOCK, (i + 1) * Q_BLOCK
        s = jnp.einsum('bqhd,bkhd->bhqk', q[:, start:end], k[:, :end]).astype(jnp.float32) * scale
        if log_decay is not None:
            s = s + log_decay[:, :, start:end, None] - log_decay[:, :, None, :end]
        visible = (pos[None, :end] // causal_unit) <= (pos[start:end, None] // causal_unit)
        s = jnp.where(visible, s, NEG_INF)
        p = jax.nn.softmax(s, axis=-1)
        outs.append(jnp.einsum('bhqk,bkhd->bqhd', p.astype(v.dtype), v[:, :end]))
    return jnp.concatenate(outs, axis=1)


def _fwd_setup_inputs(seed: int = 0) -> dict:
    key = jax.random.key(seed)
    ks = jax.random.split(key, 24)
    f32 = jnp.float32
    nrm = lambda k, shape, fan_in: jax.random.normal(k, shape, f32) * (fan_in ** -0.5)
    gain = lambda k, n: 1.0 + 0.05 * jax.random.normal(k, (n,), f32)
    offset = jax.random.randint(ks[1], (BATCH, 1), 0, MAX_OFFSET, dtype=jnp.int32)
    positions = offset + jnp.arange(SEQ, dtype=jnp.int32)[None, :]
    return {
        "x": jax.random.normal(ks[0], (BATCH, SEQ, D_MODEL), f32),
        "positions": positions,
        "pre_mix_norm": gain(ks[2], D_MODEL),
        "w_in": nrm(ks[3], (D_MODEL, D_IN), D_MODEL),
        "q_a_norm": gain(ks[4], Q_LORA),
        "w_uq": nrm(ks[5], (Q_LORA, MLA_HEADS * (NOPE_DIM + ROPE_DIM)), Q_LORA),
        "kv_a_norm": gain(ks[6], KV_LORA),
        "w_ukv": nrm(ks[7], (KV_LORA, MLA_HEADS * (NOPE_DIM + V_DIM)), KV_LORA),
        "b_forget": jax.random.uniform(ks[8], (FOX_HEADS,), f32, 1.0, 4.0),
        "b_gate": 0.02 * jax.random.normal(ks[9], (N_BRANCHES * D_MODEL,), f32),
        "w_branch_mla": nrm(ks[10], (MLA_HEADS * V_DIM, D_MODEL), MLA_HEADS * V_DIM),
        "w_branch_fox": nrm(ks[11], (FOX_HEADS * FOX_HEAD_DIM, D_MODEL), FOX_HEADS * FOX_HEAD_DIM),
        "w_out": nrm(ks[12], (D_MODEL, D_MODEL), D_MODEL),
        "post_mix_norm": gain(ks[13], D_MODEL),
        "pre_ffn_norm": gain(ks[14], D_MODEL),
        "w_up": nrm(ks[15], (D_MODEL, 2 * D_FF), D_MODEL),
        "conv_w": nrm(ks[16], (CONV_WIDTH, 2 * D_FF), CONV_WIDTH),
        "conv_b": 0.02 * jax.random.normal(ks[17], (2 * D_FF,), f32),
        "w_down": nrm(ks[18], (D_FF, D_MODEL), D_FF),
        "post_ffn_norm": gain(ks[19], D_MODEL),
    }


def _fwd_reference(x, positions, pre_mix_norm, w_in, q_a_norm, w_uq, kv_a_norm, w_ukv,
              b_forget, b_gate, w_branch_mla, w_branch_fox, w_out, post_mix_norm,
              pre_ffn_norm, w_up, conv_w, conv_b, w_down, post_ffn_norm):
    B, S, _ = x.shape
    cos, sin = rope_tables(positions, x.dtype)
    for _layer in range(DEPTH):
        h = rmsnorm(x, pre_mix_norm)
        proj = h @ w_in
        cuts = np.cumsum(SPLITS)[:-1].tolist()
        q_lat, kv_lat, k_pe, fq, fk, fv, f_logit, g_logit = jnp.split(proj, cuts, axis=-1)

        q = (rmsnorm(q_lat, q_a_norm) @ w_uq).reshape(B, S, MLA_HEADS, NOPE_DIM + ROPE_DIM)
        q_nope, q_pe = q[..., :NOPE_DIM], q[..., NOPE_DIM:]
        q_pe = rope(q_pe, cos[:, :, None, :], sin[:, :, None, :])
        kv = (rmsnorm(kv_lat, kv_a_norm) @ w_ukv).reshape(B, S, MLA_HEADS, NOPE_DIM + V_DIM)
        k_nope, v_mla = kv[..., :NOPE_DIM], kv[..., NOPE_DIM:]
        k_pe = rope(k_pe, cos, sin)[:, :, None, :]
        q_mla = jnp.concatenate([q_nope, q_pe], axis=-1)
        k_mla = jnp.concatenate([k_nope, jnp.broadcast_to(k_pe, (B, S, MLA_HEADS, ROPE_DIM))], axis=-1)
        o_mla = block_attention(q_mla, k_mla, v_mla, CHUNK).reshape(B, S, MLA_HEADS * V_DIM)

        log_f = jax.nn.log_sigmoid(f_logit.astype(jnp.float32) + b_forget.astype(jnp.float32))
        c = jnp.transpose(jnp.cumsum(log_f, axis=1), (0, 2, 1))
        shp = (B, S, FOX_HEADS, FOX_HEAD_DIM)
        o_fox = block_attention(fq.reshape(shp), fk.reshape(shp), fv.reshape(shp), 1, c)
        o_fox = o_fox.reshape(B, S, FOX_HEADS * FOX_HEAD_DIM)

        gates = jax.nn.sigmoid((g_logit + b_gate).astype(jnp.float32)).astype(x.dtype)
        g_mla, g_fox = gates[..., :D_MODEL], gates[..., D_MODEL:]
        merged = g_mla * (o_mla @ w_branch_mla) + g_fox * (o_fox @ w_branch_fox)
        x = x + rmsnorm(merged @ w_out, post_mix_norm)

        h2 = rmsnorm(x, pre_ffn_norm)
        u = h2 @ w_up
        u_pad = jnp.pad(u, ((0, 0), (CONV_WIDTH - 1, 0), (0, 0)))
        u = sum(conv_w[j] * u_pad[:, j:j + S] for j in range(CONV_WIDTH)) + conv_b
        gate, val = u[..., :D_FF], u[..., D_FF:]
        ff = (jax.nn.gelu(gate, approximate=True) * val) @ w_down
        x = x + rmsnorm(ff, post_ffn_norm)
    return x


import jax as _jax
import jax.numpy as _jnp

TWIN_FORMAT = 'train_step'
FWD_PARAMS = ['x', 'positions', 'pre_mix_norm', 'w_in', 'q_a_norm', 'w_uq', 'kv_a_norm', 'w_ukv', 'b_forget', 'b_gate', 'w_branch_mla', 'w_branch_fox', 'w_out', 'post_mix_norm', 'pre_ffn_norm', 'w_up', 'conv_w', 'conv_b', 'w_down', 'post_ffn_norm']
TWIN_WEIGHTS = ['pre_mix_norm', 'w_in', 'q_a_norm', 'w_uq', 'kv_a_norm', 'w_ukv', 'b_forget', 'b_gate', 'w_branch_mla', 'w_branch_fox', 'w_out', 'post_mix_norm', 'pre_ffn_norm', 'w_up', 'conv_w', 'conv_b', 'w_down', 'post_ffn_norm']
TWIN_DIFF_INPUT = 'x'
TWIN_INPUTS = ['x', 'positions', 'pre_mix_norm', 'w_in', 'q_a_norm', 'w_uq', 'kv_a_norm', 'w_ukv', 'b_forget', 'b_gate', 'w_branch_mla', 'w_branch_fox', 'w_out', 'post_mix_norm', 'pre_ffn_norm', 'w_up', 'conv_w', 'conv_b', 'w_down', 'post_ffn_norm', 'loss_target', 'm_pre_mix_norm', 'm_w_in', 'm_q_a_norm', 'm_w_uq', 'm_kv_a_norm', 'm_w_ukv', 'm_b_forget', 'm_b_gate', 'm_w_branch_mla', 'm_w_branch_fox', 'm_w_out', 'm_post_mix_norm', 'm_pre_ffn_norm', 'm_w_up', 'm_conv_w', 'm_conv_b', 'm_w_down', 'm_post_ffn_norm', 'v_pre_mix_norm', 'v_w_in', 'v_q_a_norm', 'v_w_uq', 'v_kv_a_norm', 'v_w_ukv', 'v_b_forget', 'v_b_gate', 'v_w_branch_mla', 'v_w_branch_fox', 'v_w_out', 'v_post_mix_norm', 'v_pre_ffn_norm', 'v_w_up', 'v_conv_w', 'v_conv_b', 'v_w_down', 'v_post_ffn_norm']
TWIN_OUTPUTS = ['loss', 'grad_x', 'grad_pre_mix_norm', 'grad_w_in', 'grad_q_a_norm', 'grad_w_uq', 'grad_kv_a_norm', 'grad_w_ukv', 'grad_b_forget', 'grad_b_gate', 'grad_w_branch_mla', 'grad_w_branch_fox', 'grad_w_out', 'grad_post_mix_norm', 'grad_pre_ffn_norm', 'grad_w_up', 'grad_conv_w', 'grad_conv_b', 'grad_w_down', 'grad_post_ffn_norm', 'delta_pre_mix_norm', 'delta_w_in', 'delta_q_a_norm', 'delta_w_uq', 'delta_kv_a_norm', 'delta_w_ukv', 'delta_b_forget', 'delta_b_gate', 'delta_w_branch_mla', 'delta_w_branch_fox', 'delta_w_out', 'delta_post_mix_norm', 'delta_pre_ffn_norm', 'delta_w_up', 'delta_conv_w', 'delta_conv_b', 'delta_w_down', 'delta_post_ffn_norm', 'new_m_pre_mix_norm', 'new_m_w_in', 'new_m_q_a_norm', 'new_m_w_uq', 'new_m_kv_a_norm', 'new_m_w_ukv', 'new_m_b_forget', 'new_m_b_gate', 'new_m_w_branch_mla', 'new_m_w_branch_fox', 'new_m_w_out', 'new_m_post_mix_norm', 'new_m_pre_ffn_norm', 'new_m_w_up', 'new_m_conv_w', 'new_m_conv_b', 'new_m_w_down', 'new_m_post_ffn_norm', 'new_v_pre_mix_norm', 'new_v_w_in', 'new_v_q_a_norm', 'new_v_w_uq', 'new_v_kv_a_norm', 'new_v_w_ukv', 'new_v_b_forget', 'new_v_b_gate', 'new_v_w_branch_mla', 'new_v_w_branch_fox', 'new_v_w_out', 'new_v_post_mix_norm', 'new_v_pre_ffn_norm', 'new_v_w_up', 'new_v_conv_w', 'new_v_conv_b', 'new_v_w_down', 'new_v_post_ffn_norm']
TWIN_LEAF_KINDS = {'loss': 'loss', 'grad_x': 'grad_x', 'grad_pre_mix_norm': 'grad_w', 'grad_w_in': 'grad_w', 'grad_q_a_norm': 'grad_w', 'grad_w_uq': 'grad_w', 'grad_kv_a_norm': 'grad_w', 'grad_w_ukv': 'grad_w', 'grad_b_forget': 'grad_w', 'grad_b_gate': 'grad_w', 'grad_w_branch_mla': 'grad_w', 'grad_w_branch_fox': 'grad_w', 'grad_w_out': 'grad_w', 'grad_post_mix_norm': 'grad_w', 'grad_pre_ffn_norm': 'grad_w', 'grad_w_up': 'grad_w', 'grad_conv_w': 'grad_w', 'grad_conv_b': 'grad_w', 'grad_w_down': 'grad_w', 'grad_post_ffn_norm': 'grad_w', 'delta_pre_mix_norm': 'delta_w', 'delta_w_in': 'delta_w', 'delta_q_a_norm': 'delta_w', 'delta_w_uq': 'delta_w', 'delta_kv_a_norm': 'delta_w', 'delta_w_ukv': 'delta_w', 'delta_b_forget': 'delta_w', 'delta_b_gate': 'delta_w', 'delta_w_branch_mla': 'delta_w', 'delta_w_branch_fox': 'delta_w', 'delta_w_out': 'delta_w', 'delta_post_mix_norm': 'delta_w', 'delta_pre_ffn_norm': 'delta_w', 'delta_w_up': 'delta_w', 'delta_conv_w': 'delta_w', 'delta_conv_b': 'delta_w', 'delta_w_down': 'delta_w', 'delta_post_ffn_norm': 'delta_w', 'new_m_pre_mix_norm': 'new_m', 'new_m_w_in': 'new_m', 'new_m_q_a_norm': 'new_m', 'new_m_w_uq': 'new_m', 'new_m_kv_a_norm': 'new_m', 'new_m_w_ukv': 'new_m', 'new_m_b_forget': 'new_m', 'new_m_b_gate': 'new_m', 'new_m_w_branch_mla': 'new_m', 'new_m_w_branch_fox': 'new_m', 'new_m_w_out': 'new_m', 'new_m_post_mix_norm': 'new_m', 'new_m_pre_ffn_norm': 'new_m', 'new_m_w_up': 'new_m', 'new_m_conv_w': 'new_m', 'new_m_conv_b': 'new_m', 'new_m_w_down': 'new_m', 'new_m_post_ffn_norm': 'new_m', 'new_v_pre_mix_norm': 'new_v', 'new_v_w_in': 'new_v', 'new_v_q_a_norm': 'new_v', 'new_v_w_uq': 'new_v', 'new_v_kv_a_norm': 'new_v', 'new_v_w_ukv': 'new_v', 'new_v_b_forget': 'new_v', 'new_v_b_gate': 'new_v', 'new_v_w_branch_mla': 'new_v', 'new_v_w_branch_fox': 'new_v', 'new_v_w_out': 'new_v', 'new_v_post_mix_norm': 'new_v', 'new_v_pre_ffn_norm': 'new_v', 'new_v_w_up': 'new_v', 'new_v_conv_w': 'new_v', 'new_v_conv_b': 'new_v', 'new_v_w_down': 'new_v', 'new_v_post_ffn_norm': 'new_v'}


def _forward(args):
    return _fwd_reference(*[args[k] for k in FWD_PARAMS])


def _output_shape():
    out = _jax.eval_shape(lambda: _forward(_fwd_setup_inputs(0)))
    return out.shape, out.dtype

N_MICROBATCH = 1
ADAM_LR = 0.001
ADAM_B1 = 0.9
ADAM_B2 = 0.999
ADAM_EPS = 1e-08
ADAM_WD = 0.01
ADAM_STEP = 10
PER_EXAMPLE_BATCH_AXIS = {'x': 0, 'positions': 0, 'loss_target': 0}
SHARED_INPUTS = []
_WEIGHT_DTYPES = {'pre_mix_norm': _jnp.float32, 'w_in': _jnp.float32, 'q_a_norm': _jnp.float32, 'w_uq': _jnp.float32, 'kv_a_norm': _jnp.float32, 'w_ukv': _jnp.float32, 'b_forget': _jnp.float32, 'b_gate': _jnp.float32, 'w_branch_mla': _jnp.float32, 'w_branch_fox': _jnp.float32, 'w_out': _jnp.float32, 'post_mix_norm': _jnp.float32, 'pre_ffn_norm': _jnp.float32, 'w_up': _jnp.float32, 'conv_w': _jnp.float32, 'conv_b': _jnp.float32, 'w_down': _jnp.float32, 'post_ffn_norm': _jnp.float32}
MOMENT_SCALE = {'pre_mix_norm': 3.472186e-01, 'w_in': 1.768401e-01, 'q_a_norm': 1.476963e-01, 'w_uq': 8.456292e-02, 'kv_a_norm': 3.018970e-01, 'w_ukv': 1.016801e-01, 'b_forget': 2.496556e+00, 'b_gate': 7.504119e-02, 'w_branch_mla': 8.103084e-02, 'w_branch_fox': 2.591809e-01, 'w_out': 2.624253e-01, 'post_mix_norm': 1.601289e+01, 'pre_ffn_norm': 2.558628e-01, 'w_up': 1.051901e-01, 'conv_w': 1.101183e-01, 'conv_b': 1.537615e-01, 'w_down': 1.900735e-01, 'post_ffn_norm': 1.599544e+01}


def _to_microbatches(a, axis):
    t = _jnp.moveaxis(a, axis, 0)
    t = t.reshape((N_MICROBATCH, t.shape[0] // N_MICROBATCH) + t.shape[1:])
    return _jnp.moveaxis(t, 1, axis + 1)


def setup_inputs(seed: int = 0) -> dict:
    inp = _fwd_setup_inputs(seed)
    key = _jax.random.fold_in(_jax.random.key(seed), 7919)
    shape, _ = _output_shape()
    out = dict(inp)
    out["loss_target"] = _jax.random.normal(_jax.random.fold_in(key, 0), shape, _jnp.float32)
    for i, name in enumerate(TWIN_WEIGHTS):
        w = inp[name].astype(_jnp.float32)
        if MOMENT_SCALE is None:
            s = _jnp.sqrt(_jnp.mean(_jnp.square(w)) + 1e-30)
        else:
            s = MOMENT_SCALE[name]
        km, kv = _jax.random.split(_jax.random.fold_in(key, i + 1))
        out[name] = w
        out["m_" + name] = s * _jax.random.normal(km, w.shape, _jnp.float32)
        out["v_" + name] = (s * s) * _jax.random.uniform(kv, w.shape, _jnp.float32, 0.5, 1.5)
    if N_MICROBATCH > 1:
        for name, axis in PER_EXAMPLE_BATCH_AXIS.items():
            out[name] = _to_microbatches(out[name], axis)
    return {'x': out['x'], 'positions': out['positions'], 'pre_mix_norm': out['pre_mix_norm'], 'w_in': out['w_in'], 'q_a_norm': out['q_a_norm'], 'w_uq': out['w_uq'], 'kv_a_norm': out['kv_a_norm'], 'w_ukv': out['w_ukv'], 'b_forget': out['b_forget'], 'b_gate': out['b_gate'], 'w_branch_mla': out['w_branch_mla'], 'w_branch_fox': out['w_branch_fox'], 'w_out': out['w_out'], 'post_mix_norm': out['post_mix_norm'], 'pre_ffn_norm': out['pre_ffn_norm'], 'w_up': out['w_up'], 'conv_w': out['conv_w'], 'conv_b': out['conv_b'], 'w_down': out['w_down'], 'post_ffn_norm': out['post_ffn_norm'], 'loss_target': out['loss_target'], 'm_pre_mix_norm': out['m_pre_mix_norm'], 'm_w_in': out['m_w_in'], 'm_q_a_norm': out['m_q_a_norm'], 'm_w_uq': out['m_w_uq'], 'm_kv_a_norm': out['m_kv_a_norm'], 'm_w_ukv': out['m_w_ukv'], 'm_b_forget': out['m_b_forget'], 'm_b_gate': out['m_b_gate'], 'm_w_branch_mla': out['m_w_branch_mla'], 'm_w_branch_fox': out['m_w_branch_fox'], 'm_w_out': out['m_w_out'], 'm_post_mix_norm': out['m_post_mix_norm'], 'm_pre_ffn_norm': out['m_pre_ffn_norm'], 'm_w_up': out['m_w_up'], 'm_conv_w': out['m_conv_w'], 'm_conv_b': out['m_conv_b'], 'm_w_down': out['m_w_down'], 'm_post_ffn_norm': out['m_post_ffn_norm'], 'v_pre_mix_norm': out['v_pre_mix_norm'], 'v_w_in': out['v_w_in'], 'v_q_a_norm': out['v_q_a_norm'], 'v_w_uq': out['v_w_uq'], 'v_kv_a_norm': out['v_kv_a_norm'], 'v_w_ukv': out['v_w_ukv'], 'v_b_forget': out['v_b_forget'], 'v_b_gate': out['v_b_gate'], 'v_w_branch_mla': out['v_w_branch_mla'], 'v_w_branch_fox': out['v_w_branch_fox'], 'v_w_out': out['v_w_out'], 'v_post_mix_norm': out['v_post_mix_norm'], 'v_pre_ffn_norm': out['v_pre_ffn_norm'], 'v_w_up': out['v_w_up'], 'v_conv_w': out['v_conv_w'], 'v_conv_b': out['v_conv_b'], 'v_w_down': out['v_w_down'], 'v_post_ffn_norm': out['v_post_ffn_norm']}


def _loss(weights, diff, rest, loss_target):
    with _jax.named_scope("forward"):
        args = {**rest, TWIN_DIFF_INPUT: diff, **{k: w.astype(_WEIGHT_DTYPES[k]) for k, w in weights.items()}}
        y = _forward(args)
    with _jax.named_scope("loss_head"):
        err = _jnp.square(y.astype(_jnp.float32) - loss_target)
        return 0.5 * _jnp.sum(_jnp.mean(err, axis=-1)) if err.ndim else 0.5 * err


def _adamw(w, g, m, v):
    m = ADAM_B1 * m + (1.0 - ADAM_B1) * g
    v = ADAM_B2 * v + (1.0 - ADAM_B2) * _jnp.square(g)
    m_hat = m / (1.0 - ADAM_B1 ** ADAM_STEP)
    v_hat = v / (1.0 - ADAM_B2 ** ADAM_STEP)
    delta = -ADAM_LR * (m_hat / (_jnp.sqrt(v_hat) + ADAM_EPS) + ADAM_WD * w)
    return delta, m, v


def reference(x, positions, pre_mix_norm, w_in, q_a_norm, w_uq, kv_a_norm, w_ukv, b_forget, b_gate, w_branch_mla, w_branch_fox, w_out, post_mix_norm, pre_ffn_norm, w_up, conv_w, conv_b, w_down, post_ffn_norm, loss_target, m_pre_mix_norm, m_w_in, m_q_a_norm, m_w_uq, m_kv_a_norm, m_w_ukv, m_b_forget, m_b_gate, m_w_branch_mla, m_w_branch_fox, m_w_out, m_post_mix_norm, m_pre_ffn_norm, m_w_up, m_conv_w, m_conv_b, m_w_down, m_post_ffn_norm, v_pre_mix_norm, v_w_in, v_q_a_norm, v_w_uq, v_kv_a_norm, v_w_ukv, v_b_forget, v_b_gate, v_w_branch_mla, v_w_branch_fox, v_w_out, v_post_mix_norm, v_pre_ffn_norm, v_w_up, v_conv_w, v_conv_b, v_w_down, v_post_ffn_norm):
    given = dict(x=x, positions=positions, pre_mix_norm=pre_mix_norm, w_in=w_in, q_a_norm=q_a_norm, w_uq=w_uq, kv_a_norm=kv_a_norm, w_ukv=w_ukv, b_forget=b_forget, b_gate=b_gate, w_branch_mla=w_branch_mla, w_branch_fox=w_branch_fox, w_out=w_out, post_mix_norm=post_mix_norm, pre_ffn_norm=pre_ffn_norm, w_up=w_up, conv_w=conv_w, conv_b=conv_b, w_down=w_down, post_ffn_norm=post_ffn_norm, loss_target=loss_target, m_pre_mix_norm=m_pre_mix_norm, m_w_in=m_w_in, m_q_a_norm=m_q_a_norm, m_w_uq=m_w_uq, m_kv_a_norm=m_kv_a_norm, m_w_ukv=m_w_ukv, m_b_forget=m_b_forget, m_b_gate=m_b_gate, m_w_branch_mla=m_w_branch_mla, m_w_branch_fox=m_w_branch_fox, m_w_out=m_w_out, m_post_mix_norm=m_post_mix_norm, m_pre_ffn_norm=m_pre_ffn_norm, m_w_up=m_w_up, m_conv_w=m_conv_w, m_conv_b=m_conv_b, m_w_down=m_w_down, m_post_ffn_norm=m_post_ffn_norm, v_pre_mix_norm=v_pre_mix_norm, v_w_in=v_w_in, v_q_a_norm=v_q_a_norm, v_w_uq=v_w_uq, v_kv_a_norm=v_kv_a_norm, v_w_ukv=v_w_ukv, v_b_forget=v_b_forget, v_b_gate=v_b_gate, v_w_branch_mla=v_w_branch_mla, v_w_branch_fox=v_w_branch_fox, v_w_out=v_w_out, v_post_mix_norm=v_post_mix_norm, v_pre_ffn_norm=v_pre_ffn_norm, v_w_up=v_w_up, v_conv_w=v_conv_w, v_conv_b=v_conv_b, v_w_down=v_w_down, v_post_ffn_norm=v_post_ffn_norm)
    weights = {n: given[n] for n in TWIN_WEIGHTS}
    shared = {n: given[n] for n in SHARED_INPUTS}
    per_example = {n: given[n] for n in ['x', 'positions']}
    grad_fn = _jax.value_and_grad(_loss, argnums=(0, 1))

    def one_microbatch(ex, loss_target):
        ex = dict(ex)
        diff = ex.pop(TWIN_DIFF_INPUT)
        return grad_fn(weights, diff, {**shared, **ex}, loss_target)

    if N_MICROBATCH == 1:
        loss, (grad_w, grad_x) = one_microbatch(per_example, given["loss_target"])
    else:
        def body(carry, xs):
            loss_sum, grad_sum = carry
            l_k, (gw_k, gx_k) = one_microbatch(xs[0], xs[1])
            with _jax.named_scope("update"):
                return (loss_sum + l_k, _jax.tree.map(_jnp.add, grad_sum, gw_k)), gx_k

        init = (_jnp.zeros((), _jnp.float32), _jax.tree.map(_jnp.zeros_like, weights))
        (loss, grad_w), grad_x = _jax.lax.scan(body, init, (per_example, given["loss_target"]))
    with _jax.named_scope("update"):
        delta_w, new_m, new_v = {}, {}, {}
        for n in TWIN_WEIGHTS:
            delta_w[n], new_m[n], new_v[n] = _adamw(weights[n], grad_w[n], given["m_" + n], given["v_" + n])
    return (loss, grad_x, *[grad_w[n] for n in TWIN_WEIGHTS], *[delta_w[n] for n in TWIN_WEIGHTS],
            *[new_m[n] for n in TWIN_WEIGHTS], *[new_v[n] for n in TWIN_WEIGHTS])
```

```python
import functools
import math

import jax
import jax.numpy as jnp
from jax import lax
from jax.experimental import pallas as pl
from jax.experimental.pallas import tpu as pltpu

F32 = jnp.float32
BF16 = jnp.bfloat16

N_DEV = 8
HEADS = 8
NOPE = 128
ROPE = 64
HALF_ROPE = ROPE // 2
VDIM = 128
Q_LORA = 512
KV_LORA = 256
FOX_DIM = 128
ATT_DK = 256
MLA_UNIT = 64
ROPE_THETA = 10000.0
EPS = 1e-6
NEG_INF = -1e30
LANES = 128

ADAM_LR = 0.001
ADAM_B1 = 0.9
ADAM_B2 = 0.999
ADAM_EPS = 1e-08
ADAM_WD = 0.01
ADAM_STEP = 10

VMEM_LIMIT_BYTES = 56 * 1024 * 1024
ROW_TILE = 256
ATT_TILE = 512
MM_TILE = 1024

MESH_AXES = ("x", "y", "c")


def _tile(n, pref, align=LANES):
    if n <= pref:
        return n
    t = (pref // align) * align
    while t >= align:
        if n % t == 0:
            return t
        t -= align
    return n


def _params(sem=None):
    return pltpu.CompilerParams(dimension_semantics=sem, vmem_limit_bytes=VMEM_LIMIT_BYTES)


def _sds(shape, dtype):
    return jax.ShapeDtypeStruct(shape, dtype)


def _matmul(a, b, *, mode, name, out_dtype=F32, out_blocks=None, tm=MM_TILE, tn=MM_TILE, tk=None):
    a_blk = a.ndim == 3
    b_blk = b.ndim == 3
    if mode == "nn":
        M, K = a.shape
        N = b.shape[0] * b.shape[2] if b_blk else b.shape[1]
        dims = (((1,), (0,)), ((), ()))
    elif mode == "nt":
        M = a.shape[1] if a_blk else a.shape[0]
        K = a.shape[0] * a.shape[2] if a_blk else a.shape[1]
        N = b.shape[1] if b_blk else b.shape[0]
        dims = (((1,), (1,)), ((), ()))
    else:
        K, M = a.shape
        N = b.shape[0] * b.shape[2] if b_blk else b.shape[1]
        dims = (((0,), (0,)), ((), ()))

    tm = _tile(M, tm)
    tn = _tile(N, tn)
    if mode == "nt" and (a_blk or b_blk):
        tk = a.shape[2] if a_blk else b.shape[2]
    else:
        tk = _tile(K, K if tk is None else tk)
    if mode != "nt" and b_blk:
        tn = _tile(b.shape[2], tn)
    if out_blocks is not None:
        tn = _tile(out_blocks, tn)
    nk = K // tk
    grid = (M // tm, N // tn, nk)

    if mode == "nn":
        a_spec = pl.BlockSpec((tm, tk), lambda i, j, k: (i, k))
        if b_blk:
            rb = b.shape[2] // tn
            b_spec = pl.BlockSpec((None, tk, tn), lambda i, j, k: (j // rb, k, j % rb))
        else:
            b_spec = pl.BlockSpec((tk, tn), lambda i, j, k: (k, j))
    elif mode == "nt":
        if a_blk:
            a_spec = pl.BlockSpec((None, tm, tk), lambda i, j, k: (k, i, 0))
        else:
            a_spec = pl.BlockSpec((tm, tk), lambda i, j, k: (i, k))
        if b_blk:
            b_spec = pl.BlockSpec((None, tn, tk), lambda i, j, k: (k, j, 0))
        else:
            b_spec = pl.BlockSpec((tn, tk), lambda i, j, k: (j, k))
    else:
        a_spec = pl.BlockSpec((tk, tm), lambda i, j, k: (k, i))
        if b_blk:
            rb = b.shape[2] // tn
            b_spec = pl.BlockSpec((None, tk, tn), lambda i, j, k: (j // rb, k, j % rb))
        else:
            b_spec = pl.BlockSpec((tk, tn), lambda i, j, k: (k, j))

    if out_blocks is None:
        o_spec = pl.BlockSpec((tm, tn), lambda i, j, k: (i, j))
        o_shape = _sds((M, N), out_dtype)
    else:
        ro = out_blocks // tn
        o_spec = pl.BlockSpec((None, tm, tn), lambda i, j, k: (j // ro, i, j % ro))
        o_shape = _sds((N // out_blocks, M, out_blocks), out_dtype)

    def body(a_ref, b_ref, o_ref, *scratch):
        part = lax.dot_general(a_ref[...], b_ref[...], dims, preferred_element_type=F32)
        if nk == 1:
            o_ref[...] = part.astype(o_ref.dtype)
        else:
            acc_ref = scratch[0]
            k = pl.program_id(2)

            @pl.when(k == 0)
            def _():
                acc_ref[...] = part

            @pl.when(k > 0)
            def _():
                acc_ref[...] += part

            @pl.when(k == nk - 1)
            def _():
                o_ref[...] = acc_ref[...].astype(o_ref.dtype)

    scratch = [] if nk == 1 else [pltpu.VMEM((tm, tn), F32)]
    return pl.pallas_call(
        body,
        name=name,
        grid=grid,
        in_specs=[a_spec, b_spec],
        out_specs=o_spec,
        out_shape=o_shape,
        scratch_shapes=scratch,
        compiler_params=_params(("parallel", "parallel", "arbitrary")),
    )(a, b)


def _rms(x):
    return lax.rsqrt(jnp.mean(x * x, axis=-1, keepdims=True) + EPS)


def _rms_bwd(dy, x, g):
    r = _rms(x)
    n = x * r
    dn = dy * g
    dx = r * (dn - n * jnp.mean(dn * n, axis=-1, keepdims=True))
    return dx, dy * n


def _sigmoid(x):
    return 1.0 / (1.0 + jnp.exp(-x))


def _rope_rot(t):
    return pltpu.roll(t, HALF_ROPE, 1) - pltpu.roll(t, LANES - HALF_ROPE, 1)


def _lane(shape):
    return lax.broadcasted_iota(jnp.int32, shape, 1)


def _split3(x):
    hi = x.astype(BF16).astype(F32)
    r1 = x - hi
    mid = r1.astype(BF16).astype(F32)
    lo = (r1 - mid).astype(BF16).astype(F32)
    return hi, mid, lo


def _cumsum_rows(x, reverse):
    S = x.shape[0]
    bs = min(256, S)
    nb = S // bs
    r = lax.broadcasted_iota(jnp.int32, (bs, bs), 0)
    c = lax.broadcasted_iota(jnp.int32, (bs, bs), 1)
    tri = jnp.where((c >= r) if reverse else (c <= r), 1.0, 0.0).astype(BF16)
    edge = lax.broadcasted_iota(jnp.int32, (bs, x.shape[1]), 0) == (0 if reverse else bs - 1)
    carry = jnp.zeros((1, x.shape[1]), F32)
    outs = [None] * nb
    for bi in (range(nb - 1, -1, -1) if reverse else range(nb)):
        xb = x[bi * bs:(bi + 1) * bs, :]
        acc = carry
        for term in _split3(xb):
            acc = acc + jnp.dot(tri, term.astype(BF16), preferred_element_type=F32)
        outs[bi] = acc
        carry = jnp.sum(jnp.where(edge, acc, 0.0), axis=0, keepdims=True)
    return jnp.concatenate(outs, axis=0) if nb > 1 else outs[0]


def _gelu_parts(x):
    c0 = math.sqrt(2.0 / math.pi)
    inner = c0 * (x + 0.044715 * (x * x * x))
    t = jnp.tanh(inner)
    g = 0.5 * x * (1.0 + t)
    dg = 0.5 * (1.0 + t) + 0.5 * x * (1.0 - t * t) * (c0 * (1.0 + 3.0 * 0.044715 * (x * x)))
    return g, dg


def _accumulate(ref, value, first):
    @pl.when(first)
    def _():
        ref[...] = value

    @pl.when(jnp.logical_not(first))
    def _():
        ref[...] += value


def _cast_bf16(w, name):
    R, C = w.shape
    tr = _tile(R, 512, 16)

    def body(w_ref, o_ref):
        o_ref[...] = w_ref[...].astype(BF16)

    return pl.pallas_call(
        body, name=name, grid=(R // tr,),
        in_specs=[pl.BlockSpec((tr, C), lambda i: (i, 0))],
        out_specs=pl.BlockSpec((tr, C), lambda i: (i, 0)),
        out_shape=_sds((R, C), BF16), compiler_params=_params(("parallel",)),
    )(w)


def _cast_pieces_bf16(g, name):
    G, R, C = g.shape
    tr = _tile(R, 512, 16)

    def body(g_ref, o_ref):
        o_ref[...] = g_ref[...].astype(BF16)

    return pl.pallas_call(
        body, name=name, grid=(G, R // tr),
        in_specs=[pl.BlockSpec((None, tr, C), lambda p, i: (p, i, 0))],
        out_specs=pl.BlockSpec((None, tr, C), lambda p, i: (p, i, 0)),
        out_shape=_sds((G, R, C), BF16), compiler_params=_params(("parallel", "parallel")),
    )(g)


def _exchange(arrays, *, scatter, name):
    n = len(arrays)
    out_shapes = [_sds(a.shape if scatter else (N_DEV,) + a.shape, a.dtype) for a in arrays]

    def body(*refs):
        ins, outs = refs[:n], refs[n:2 * n]
        send_sems, recv_sems, local_sems = refs[2 * n:]
        x, y, c = (lax.axis_index(ax) for ax in MESH_AXES)
        me = 4 * x + 2 * y + c

        def peer_of(r):
            px = 1 - x if r & 4 else x
            py = 1 - y if r & 2 else y
            pc = 1 - c if r & 1 else c
            return (px, py, pc), 4 * px + 2 * py + pc

        local, sends, recvs = [], [], []
        for w in range(n):
            src = ins[w].at[me] if scatter else ins[w]
            cp = pltpu.make_async_copy(src, outs[w].at[me], local_sems.at[w])
            cp.start()
            local.append(cp)
            for r in range(1, N_DEV):
                peer, pidx = peer_of(r)
                src = ins[w].at[pidx] if scatter else ins[w]
                cp = pltpu.make_async_remote_copy(
                    src_ref=src, dst_ref=outs[w].at[me],
                    send_sem=send_sems.at[w, r - 1], recv_sem=recv_sems.at[w, r - 1],
                    device_id=peer, device_id_type=pl.DeviceIdType.MESH)
                cp.start()
                sends.append(cp)
                recvs.append(pltpu.make_async_remote_copy(
                    src_ref=src, dst_ref=outs[w].at[pidx],
                    send_sem=send_sems.at[w, r - 1], recv_sem=recv_sems.at[w, r - 1],
                    device_id=peer, device_id_type=pl.DeviceIdType.MESH))
        for cp in recvs:
            cp.wait_recv()
        for cp in sends:
            cp.wait_send()
        for cp in local:
            cp.wait()

    any_spec = pl.BlockSpec(memory_space=pl.ANY)
    return pl.pallas_call(
        body, name=name,
        in_specs=[any_spec] * n, out_specs=[any_spec] * n, out_shape=out_shapes,
        scratch_shapes=[pltpu.SemaphoreType.DMA((n, N_DEV - 1)), pltpu.SemaphoreType.DMA((n, N_DEV - 1)),
                        pltpu.SemaphoreType.DMA((n,))],
        compiler_params=pltpu.CompilerParams(has_side_effects=True),
    )(*arrays)


def _prenorm(x, g):
    T, D = x.shape
    tm = _tile(T, ROW_TILE, 16)

    def body(x_ref, g_ref, h_ref):
        xv = x_ref[...]
        h_ref[...] = (xv * _rms(xv) * g_ref[...]).astype(BF16)

    row = pl.BlockSpec((tm, D), lambda i: (i, 0))
    return pl.pallas_call(
        body, name="prenorm", grid=(T // tm,),
        in_specs=[row, pl.BlockSpec((1, D), lambda i: (0, 0))], out_specs=row,
        out_shape=_sds((T, D), BF16), compiler_params=_params(("parallel",)),
    )(x, g)


def _split_prep(proj, pos, invf, gq, gkv, bfor, lay):
    T = proj.shape[0]
    tm = _tile(T, ROW_TILE, 16)

    def body(q_ref, kv_ref, kpe_ref, fl_ref, pos_ref, invf_ref, gq_ref, gkv_ref, bf_ref,
             qn_ref, kvn_ref, kper_ref, logf_ref, cos_ref, sin_ref):
        ql = q_ref[...]
        qn_ref[...] = (ql * _rms(ql) * gq_ref[...]).astype(BF16)
        kl = kv_ref[...]
        kvn_ref[...] = (kl * _rms(kl) * gkv_ref[...]).astype(BF16)
        ang = pos_ref[...].astype(F32) * invf_ref[...]
        valid = _lane(ang.shape) < ROPE
        cs = jnp.where(valid, jnp.cos(ang), 0.0)
        sn = jnp.where(valid, jnp.sin(ang), 0.0)
        cos_ref[...] = cs
        sin_ref[...] = sn
        kp = jnp.where(valid, kpe_ref[...], 0.0)
        kper_ref[...] = (kp * cs + _rope_rot(kp) * sn).astype(BF16)
        z = fl_ref[...] + bf_ref[...]
        logf_ref[...] = jnp.minimum(z, 0.0) - jnp.log(1.0 + jnp.exp(-jnp.abs(z)))

    def col(width, off):
        return pl.BlockSpec((tm, width), lambda i: (i, off // width))

    def vec(width):
        return pl.BlockSpec((1, width), lambda i: (0, 0))

    def out(width):
        return pl.BlockSpec((tm, width), lambda i: (i, 0))

    return pl.pallas_call(
        body, name="split_prep", grid=(T // tm,),
        in_specs=[col(Q_LORA, lay["q"]), col(KV_LORA, lay["kv"]), col(LANES, lay["kpe"]), col(LANES, lay["fl"]),
                  pl.BlockSpec((tm, 1), lambda i: (i, 0)), vec(LANES), vec(Q_LORA), vec(KV_LORA), vec(LANES)],
        out_specs=[out(Q_LORA), out(KV_LORA), out(LANES), out(LANES), out(LANES), out(LANES)],
        out_shape=[_sds((T, Q_LORA), BF16), _sds((T, KV_LORA), BF16), _sds((T, LANES), BF16),
                   _sds((T, LANES), F32), _sds((T, LANES), F32), _sds((T, LANES), F32)],
        compiler_params=_params(("parallel",)),
    )(proj, proj, proj, proj, pos, invf, gq, gkv, bfor)


def _mla_prep(qraw, kvraw, kper, cosT, sinT):
    H, T, _ = qraw.shape
    tm = _tile(T, ROW_TILE, 16)

    def body(q_ref, kv_ref, kpe_ref, cos_ref, sin_ref, qo_ref, ko_ref, vo_ref):
        q = q_ref[...]
        pe = q[:, NOPE:]
        pe = jnp.where(_lane(pe.shape) < ROPE, pe, 0.0)
        qo_ref[:, :NOPE] = q[:, :NOPE].astype(BF16)
        qo_ref[:, NOPE:] = (pe * cos_ref[...] + _rope_rot(pe) * sin_ref[...]).astype(BF16)
        kv = kv_ref[...]
        ko_ref[:, :NOPE] = kv[:, :NOPE].astype(BF16)
        ko_ref[:, NOPE:] = kpe_ref[...]
        vo_ref[...] = kv[:, NOPE:].astype(BF16)

    head = pl.BlockSpec((None, tm, ATT_DK), lambda h, i: (h, i, 0))
    tok = pl.BlockSpec((tm, LANES), lambda h, i: (i, 0))
    return pl.pallas_call(
        body, name="mla_prep", grid=(H, T // tm),
        in_specs=[head, head, tok, tok, tok],
        out_specs=[head, head, pl.BlockSpec((None, tm, VDIM), lambda h, i: (h, i, 0))],
        out_shape=[_sds((H, T, ATT_DK), BF16), _sds((H, T, ATT_DK), BF16), _sds((H, T, VDIM), BF16)],
        compiler_params=_params(("parallel", "parallel")),
    )(qraw, kvraw, kper, cosT, sinT)


def _fox_cumsum(logf, B, S, inv_scale):
    T = logf.shape[0]

    def body(l_ref, c_ref):
        c_ref[...] = _cumsum_rows(l_ref[...], reverse=False) * inv_scale

    seq = pl.BlockSpec((S, LANES), lambda b: (b, 0))
    return pl.pallas_call(
        body, name="fox_cumsum", grid=(B,), in_specs=[seq], out_specs=seq,
        out_shape=_sds((T, LANES), F32), compiler_params=_params(("parallel",)),
    )(logf)


def _fox_prep(proj, cs, lay):
    T = proj.shape[0]
    tm = _tile(T, ROW_TILE, 16)

    def body(q_ref, k_ref, v_ref, cs_ref, qo_ref, ko_ref, vo_ref):
        h = pl.program_id(0)
        cv = cs_ref[...]
        lane = _lane(cv.shape)
        ccol = jnp.sum(jnp.where(lane == h, cv, 0.0), axis=1, keepdims=True)
        hi, mid, lo = _split3(ccol)
        one = jnp.where(lane < 6, 1.0, 0.0)
        augq = jnp.where(lane == 0, hi, jnp.where(lane == 1, mid, jnp.where(lane == 2, lo, one)))
        augk = jnp.where(lane < 3, 1.0, jnp.where(lane == 3, -hi, jnp.where(lane == 4, -mid, jnp.where(lane == 5, -lo, 0.0))))
        qo_ref[:, :FOX_DIM] = q_ref[...].astype(BF16)
        qo_ref[:, FOX_DIM:] = augq.astype(BF16)
        ko_ref[:, :FOX_DIM] = k_ref[...].astype(BF16)
        ko_ref[:, FOX_DIM:] = augk.astype(BF16)
        vo_ref[...] = v_ref[...].astype(BF16)

    def col(off):
        return pl.BlockSpec((tm, FOX_DIM), lambda h, i: (i, off // FOX_DIM + h))

    head = pl.BlockSpec((None, tm, ATT_DK), lambda h, i: (h, i, 0))
    return pl.pallas_call(
        body, name="fox_prep", grid=(HEADS, T // tm),
        in_specs=[col(lay["fq"]), col(lay["fk"]), col(lay["fv"]), pl.BlockSpec((tm, LANES), lambda h, i: (i, 0))],
        out_specs=[head, head, pl.BlockSpec((None, tm, VDIM), lambda h, i: (h, i, 0))],
        out_shape=[_sds((HEADS, T, ATT_DK), BF16), _sds((HEADS, T, ATT_DK), BF16), _sds((HEADS, T, VDIM), BF16)],
        compiler_params=_params(("parallel", "parallel")),
    )(proj, proj, proj, cs)


def _visible(tq, tk, unit):
    r = lax.broadcasted_iota(jnp.int32, (tq, tk), 0)
    c = lax.broadcasted_iota(jnp.int32, (tq, tk), 1)
    sh = int(math.log2(unit))
    return lax.shift_right_logical(c, sh) <= lax.shift_right_logical(r, sh)


def _attn_fwd(q, k, v, *, B, S, unit, scale, name):
    H, T, DK = q.shape
    DV = v.shape[2]
    tq = _tile(S, ATT_TILE)
    nq = S // tq
    NT = (((1,), (1,)), ((), ()))

    def body(q_ref, k_ref, v_ref, o_ref, lse_ref, m_sc, l_sc, acc_sc):
        i, j = pl.program_id(1), pl.program_id(2)

        @pl.when(j == 0)
        def _():
            m_sc[...] = jnp.full(m_sc.shape, NEG_INF, F32)
            l_sc[...] = jnp.zeros(l_sc.shape, F32)
            acc_sc[...] = jnp.zeros(acc_sc.shape, F32)

        def step(diagonal):
            s = lax.dot_general(q_ref[...], k_ref[...], NT, preferred_element_type=F32) * scale
            if diagonal:
                s = jnp.where(_visible(tq, tq, unit), s, NEG_INF)
            m_prev = m_sc[...]
            m_new = jnp.maximum(m_prev, jnp.max(s, axis=1, keepdims=True))
            alpha = jnp.exp(m_prev - m_new)
            p = jnp.exp(s - m_new)
            l_sc[...] = alpha * l_sc[...] + jnp.sum(p, axis=1, keepdims=True)
            acc_sc[...] = alpha * acc_sc[...] + jnp.dot(p.astype(BF16), v_ref[...], preferred_element_type=F32)
            m_sc[...] = m_new

        @pl.when(j < i)
        def _():
            step(False)

        @pl.when(j == i)
        def _():
            step(True)
            l = l_sc[...]
            o_ref[...] = (acc_sc[...] / l).astype(BF16)
            lse_ref[...] = jnp.broadcast_to(m_sc[...] + jnp.log(l), lse_ref.shape)

    def qmap(g, i, j):
        return (g % H, (g // H) * nq + i, 0)

    def kmap(g, i, j):
        return (g % H, (g // H) * nq + jnp.minimum(j, i), 0)

    return pl.pallas_call(
        body, name=name, grid=(B * H, nq, nq),
        in_specs=[pl.BlockSpec((None, tq, DK), qmap), pl.BlockSpec((None, tq, DK), kmap),
                  pl.BlockSpec((None, tq, DV), kmap)],
        out_specs=[pl.BlockSpec((tq, DV), lambda g, i, j: ((g // H) * nq + i, g % H)),
                   pl.BlockSpec((None, tq, LANES), qmap)],
        out_shape=[_sds((T, H * DV), BF16), _sds((H, T, LANES), F32)],
        scratch_shapes=[pltpu.VMEM((tq, 1), F32), pltpu.VMEM((tq, 1), F32), pltpu.VMEM((tq, DV), F32)],
        compiler_params=_params(("parallel", "parallel", "arbitrary")),
    )(q, k, v)


def _attn_bwd(q, k, v, o, do, lse, *, B, S, unit, scale, name):
    H, T, DK = q.shape
    DV = v.shape[2]
    tq = _tile(S, ATT_TILE)
    nq = S // tq
    NT = (((1,), (1,)), ((), ()))
    TN = (((0,), (0,)), ((), ()))

    def body(q_ref, k_ref, v_ref, o_ref, do_ref, lse_ref, dq_ref, dk_ref, dv_ref):
        j, i = pl.program_id(1), pl.program_id(2)

        @pl.when(jnp.logical_and(j == 0, i == 0))
        def _():
            dq_ref[...] = jnp.zeros(dq_ref.shape, F32)

        @pl.when(i == 0)
        def _():
            dk_ref[...] = jnp.zeros(dk_ref.shape, F32)
            dv_ref[...] = jnp.zeros(dv_ref.shape, F32)

        def step(diagonal):
            qv, kv_, dov = q_ref[...], k_ref[...], do_ref[...]
            s = lax.dot_general(qv, kv_, NT, preferred_element_type=F32) * scale
            if diagonal:
                s = jnp.where(_visible(tq, tq, unit), s, NEG_INF)
            p = jnp.exp(s - jnp.tile(lse_ref[...], (1, tq // LANES)))
            dp = lax.dot_general(dov, v_ref[...], NT, preferred_element_type=F32)
            delta = jnp.sum(dov.astype(F32) * o_ref[...].astype(F32), axis=1, keepdims=True)
            ds = (p * (dp - delta) * scale).astype(BF16)
            dv_ref[...] += lax.dot_general(p.astype(BF16), dov, TN, preferred_element_type=F32)
            dk_ref[...] += lax.dot_general(ds, qv, TN, preferred_element_type=F32)
            rows = pl.ds(pl.multiple_of(i * tq, tq), tq)
            dq_ref[rows, :] += jnp.dot(ds, kv_, preferred_element_type=F32)

        @pl.when(i > j)
        def _():
            step(False)

        @pl.when(i == j)
        def _():
            step(True)

    def qmap(g, j, i):
        return (g % H, (g // H) * nq + jnp.maximum(i, j), 0)

    def kmap(g, j, i):
        return (g % H, (g // H) * nq + j, 0)

    def omap(g, j, i):
        return ((g // H) * nq + jnp.maximum(i, j), g % H)

    return pl.pallas_call(
        body, name=name, grid=(B * H, nq, nq),
        in_specs=[pl.BlockSpec((None, tq, DK), qmap), pl.BlockSpec((None, tq, DK), kmap),
                  pl.BlockSpec((None, tq, DV), kmap), pl.BlockSpec((tq, DV), omap), pl.BlockSpec((tq, DV), omap),
                  pl.BlockSpec((None, tq, LANES), qmap)],
        out_specs=[pl.BlockSpec((None, S, DK), lambda g, j, i: (g % H, g // H, 0)),
                   pl.BlockSpec((None, tq, DK), kmap), pl.BlockSpec((None, tq, DV), kmap)],
        out_shape=[_sds((H, T, DK), F32), _sds((H, T, DK), F32), _sds((H, T, DV), F32)],
        compiler_params=_params(("parallel", "arbitrary", "arbitrary")),
    )(q, k, v, o, do, lse)


def _gate_merge(am, af, proj, bgate, lay, D):
    T = am.shape[0]
    tm = _tile(T, ROW_TILE, 16)
    tn = _tile(D, 1024)

    def body(am_ref, af_ref, gm_ref, gf_ref, bm_ref, bf_ref, o_ref):
        sm = _sigmoid(gm_ref[...] + bm_ref[...])
        sf = _sigmoid(gf_ref[...] + bf_ref[...])
        o_ref[...] = (sm * am_ref[...] + sf * af_ref[...]).astype(BF16)

    og = lay["g"] // tn
    blk = pl.BlockSpec((tm, tn), lambda i, j: (i, j))
    return pl.pallas_call(
        body, name="gate_merge", grid=(T // tm, D // tn),
        in_specs=[blk, blk, pl.BlockSpec((tm, tn), lambda i, j: (i, og + j)),
                  pl.BlockSpec((tm, tn), lambda i, j: (i, og + D // tn + j)),
                  pl.BlockSpec((1, tn), lambda i, j: (0, j)), pl.BlockSpec((1, tn), lambda i, j: (0, D // tn + j))],
        out_specs=blk, out_shape=_sds((T, D), BF16), compiler_params=_params(("parallel", "parallel")),
    )(am, af, proj, proj, bgate, bgate)


def _mid(x, y1, g_pm, g_ffn):
    T, D = x.shape
    tm = _tile(T, ROW_TILE, 16)

    def body(x_ref, y_ref, gp_ref, gf_ref, x1_ref, h2_ref):
        y = y_ref[...]
        x1 = x_ref[...] + y * _rms(y) * gp_ref[...]
        x1_ref[...] = x1
        h2_ref[...] = (x1 * _rms(x1) * gf_ref[...]).astype(BF16)

    row = pl.BlockSpec((tm, D), lambda i: (i, 0))
    vec = pl.BlockSpec((1, D), lambda i: (0, 0))
    return pl.pallas_call(
        body, name="mid", grid=(T // tm,), in_specs=[row, row, vec, vec], out_specs=[row, row],
        out_shape=[_sds((T, D), F32), _sds((T, D), BF16)], compiler_params=_params(("parallel",)),
    )(x, y1, g_pm, g_ffn)


def _conv3(u, w_ref, bias):
    row = lax.broadcasted_iota(jnp.int32, u.shape, 0)
    u1 = jnp.where(row >= 1, pltpu.roll(u, 1, 0), 0.0)
    u2 = jnp.where(row >= 2, pltpu.roll(u, 2, 0), 0.0)
    return w_ref[0:1, :] * u2 + w_ref[1:2, :] * u1 + w_ref[2:3, :] * u + bias, u1, u2


def _convffn_fwd(u, cw, cb, B, S, F):
    T = u.shape[0]
    tn = _tile(F, 256)
    nf = F // tn

    def body(ug_ref, uv_ref, wg_ref, wv_ref, bg_ref, bv_ref, a_ref):
        g, _, _ = _conv3(ug_ref[...], wg_ref, bg_ref[...])
        val, _, _ = _conv3(uv_ref[...], wv_ref, bv_ref[...])
        a_ref[...] = (_gelu_parts(g)[0] * val).astype(BF16)

    def seq(off):
        return pl.BlockSpec((S, tn), lambda b, j: (b, off + j))

    def par(rows, off):
        return pl.BlockSpec((rows, tn), lambda b, j: (0, off + j))

    return pl.pallas_call(
        body, name="convffn_fwd", grid=(B, nf),
        in_specs=[seq(0), seq(nf), par(3, 0), par(3, nf), par(1, 0), par(1, nf)],
        out_specs=seq(0), out_shape=_sds((T, F), BF16), compiler_params=_params(("parallel", "parallel")),
    )(u, u, cw, cw, cb, cb)


def _convffn_bwd(u, dact, cw, cb, B, S, F):
    T = u.shape[0]
    tn = _tile(F, 256)
    nf = F // tn

    def body(ug_ref, uv_ref, da_ref, wg_ref, wv_ref, bg_ref, bv_ref, dug_ref, duv_ref, dpg_ref, dpv_ref):
        b = pl.program_id(1)
        ug, uv, da = ug_ref[...], uv_ref[...], da_ref[...]
        g, ug1, ug2 = _conv3(ug, wg_ref, bg_ref[...])
        val, uv1, uv2 = _conv3(uv, wv_ref, bv_ref[...])
        gel, dgel = _gelu_parts(g)
        dg = da * val * dgel
        dval = da * gel
        row = lax.broadcasted_iota(jnp.int32, ug.shape, 0)

        def back(d, w_ref):
            d1 = jnp.where(row < S - 1, pltpu.roll(d, S - 1, 0), 0.0)
            d2 = jnp.where(row < S - 2, pltpu.roll(d, S - 2, 0), 0.0)
            return w_ref[2:3, :] * d + w_ref[1:2, :] * d1 + w_ref[0:1, :] * d2

        dug_ref[...] = back(dg, wg_ref).astype(BF16)
        duv_ref[...] = back(dval, wv_ref).astype(BF16)

        def sums(d, u0, u1, u2):
            r8 = lax.broadcasted_iota(jnp.int32, (8, d.shape[1]), 0)
            out = jnp.zeros((8, d.shape[1]), F32)
            for k, t in enumerate((d * u2, d * u1, d * u0, d)):
                out = jnp.where(r8 == k, jnp.sum(t, axis=0, keepdims=True), out)
            return out

        _accumulate(dpg_ref, sums(dg, ug, ug1, ug2), b == 0)
        _accumulate(dpv_ref, sums(dval, uv, uv1, uv2), b == 0)

    def seq(off):
        return pl.BlockSpec((S, tn), lambda j, b: (b, off + j))

    def par(rows, off):
        return pl.BlockSpec((rows, tn), lambda j, b: (0, off + j))

    return pl.pallas_call(
        body, name="convffn_bwd", grid=(nf, B),
        in_specs=[seq(0), seq(nf), seq(0), par(3, 0), par(3, nf), par(1, 0), par(1, nf)],
        out_specs=[seq(0), seq(0), par(8, 0), par(8, 0)],
        out_shape=[_sds((T, F), BF16), _sds((T, F), BF16), _sds((8, F), F32), _sds((8, F), F32)],
        compiler_params=_params(("parallel", "arbitrary")),
    )(u, u, dact, cw, cw, cb, cb)


def _tail(ff, x1, tgt, g):
    T, D = ff.shape
    tm = _tile(T, ROW_TILE, 16)

    def body(ff_ref, x1_ref, t_ref, g_ref, dy_ref, dff_ref, loss_ref, dg_ref):
        i = pl.program_id(0)
        f = ff_ref[...]
        gv = g_ref[...]
        r = _rms(f)
        n = f * r
        e = (x1_ref[...] + n * gv) - t_ref[...]
        dy = e * (1.0 / D)
        dy_ref[...] = dy
        dn = dy * gv
        dff_ref[...] = (r * (dn - n * jnp.mean(dn * n, axis=-1, keepdims=True))).astype(BF16)
        part = 0.5 * jnp.sum(jnp.mean(e * e, axis=-1, keepdims=True), axis=0, keepdims=True)
        _accumulate(loss_ref, jnp.broadcast_to(part, loss_ref.shape), i == 0)
        _accumulate(dg_ref, jnp.sum(dy * n, axis=0, keepdims=True), i == 0)

    row = pl.BlockSpec((tm, D), lambda i: (i, 0))
    vec = pl.BlockSpec((1, D), lambda i: (0, 0))
    return pl.pallas_call(
        body, name="tail", grid=(T // tm,), in_specs=[row, row, row, vec],
        out_specs=[row, row, pl.BlockSpec((8, LANES), lambda i: (0, 0)), vec],
        out_shape=[_sds((T, D), F32), _sds((T, D), BF16), _sds((8, LANES), F32), _sds((1, D), F32)],
        compiler_params=_params(("arbitrary",)),
    )(ff, x1, tgt, g)


def _mid_bwd(dy, dh2, x1, y1, g_ffn, g_pm):
    T, D = dy.shape
    tm = _tile(T, ROW_TILE, 16)

    def body(dy_ref, dh_ref, x1_ref, y1_ref, gf_ref, gp_ref, dx1_ref, dy1_ref, dgf_ref, dgp_ref):
        i = pl.program_id(0)
        dh = dh_ref[...]
        d2, dgf = _rms_bwd(dh, x1_ref[...], gf_ref[...])
        dx1 = dy_ref[...] + d2
        dx1_ref[...] = dx1
        d1, dgp = _rms_bwd(dx1, y1_ref[...], gp_ref[...])
        dy1_ref[...] = d1.astype(BF16)
        _accumulate(dgf_ref, jnp.sum(dgf, axis=0, keepdims=True), i == 0)
        _accumulate(dgp_ref, jnp.sum(dgp, axis=0, keepdims=True), i == 0)

    row = pl.BlockSpec((tm, D), lambda i: (i, 0))
    vec = pl.BlockSpec((1, D), lambda i: (0, 0))
    return pl.pallas_call(
        body, name="mid_bwd", grid=(T // tm,), in_specs=[row, row, row, row, vec, vec],
        out_specs=[row, row, vec, vec],
        out_shape=[_sds((T, D), F32), _sds((T, D), BF16), _sds((1, D), F32), _sds((1, D), F32)],
        compiler_params=_params(("arbitrary",)),
    )(dy, dh2, x1, y1, g_ffn, g_pm)


def _gate_bwd(dm, am, af, proj, bgate, lay, D):
    T = dm.shape[0]
    tm = _tile(T, ROW_TILE, 16)
    tn = _tile(D, 512)

    def body(dm_ref, am_ref, af_ref, gm_ref, gf_ref, bm_ref, bf_ref,
             dam_ref, daf_ref, dgm_ref, dgf_ref, dbm_ref, dbf_ref):
        i = pl.program_id(1)
        d = dm_ref[...]
        sm = _sigmoid(gm_ref[...] + bm_ref[...])
        sf = _sigmoid(gf_ref[...] + bf_ref[...])
        dam_ref[...] = (d * sm).astype(BF16)
        daf_ref[...] = (d * sf).astype(BF16)
        dgm = d * am_ref[...] * (sm * (1.0 - sm))
        dgf = d * af_ref[...] * (sf * (1.0 - sf))
        dgm_ref[...] = dgm.astype(BF16)
        dgf_ref[...] = dgf.astype(BF16)
        _accumulate(dbm_ref, jnp.sum(dgm, axis=0, keepdims=True), i == 0)
        _accumulate(dbf_ref, jnp.sum(dgf, axis=0, keepdims=True), i == 0)

    og = lay["g"] // tn
    blk = pl.BlockSpec((tm, tn), lambda j, i: (i, j))
    vec = pl.BlockSpec((1, tn), lambda j, i: (0, j))
    return pl.pallas_call(
        body, name="gate_bwd", grid=(D // tn, T // tm),
        in_specs=[blk, blk, blk, pl.BlockSpec((tm, tn), lambda j, i: (i, og + j)),
                  pl.BlockSpec((tm, tn), lambda j, i: (i, og + D // tn + j)),
                  vec, pl.BlockSpec((1, tn), lambda j, i: (0, D // tn + j))],
        out_specs=[blk, blk, blk, blk, vec, vec],
        out_shape=[_sds((T, D), BF16)] * 4 + [_sds((1, D), F32)] * 2,
        compiler_params=_params(("parallel", "arbitrary")),
    )(dm, am, af, proj, proj, bgate, bgate)


def _mla_bwd_prep(dq, dk, dv, cosT, sinT):
    H, T, _ = dq.shape
    tm = _tile(T, ROW_TILE, 16)

    def body(dq_ref, dk_ref, dv_ref, cos_ref, sin_ref, dqr_ref, dkv_ref, dkpe_ref):
        h = pl.program_id(1)
        cs, sn = cos_ref[...], sin_ref[...]
        valid = _lane(cs.shape) < ROPE

        def unrope(d):
            d = jnp.where(valid, d, 0.0)
            return d * cs - _rope_rot(d) * sn

        dqv = dq_ref[...]
        dqr_ref[:, :NOPE] = dqv[:, :NOPE].astype(BF16)
        dqr_ref[:, NOPE:] = unrope(dqv[:, NOPE:]).astype(BF16)
        dkv_ = dk_ref[...]
        dkv_ref[:, :NOPE] = dkv_[:, :NOPE].astype(BF16)
        dkv_ref[:, NOPE:] = dv_ref[...].astype(BF16)
        _accumulate(dkpe_ref, unrope(dkv_[:, NOPE:]), h == 0)

    head = pl.BlockSpec((None, tm, ATT_DK), lambda i, h: (h, i, 0))
    tok = pl.BlockSpec((tm, LANES), lambda i, h: (i, 0))
    return pl.pallas_call(
        body, name="mla_bwd_prep", grid=(T // tm, H),
        in_specs=[head, head, pl.BlockSpec((None, tm, VDIM), lambda i, h: (h, i, 0)), tok, tok],
        out_specs=[head, head, tok],
        out_shape=[_sds((H, T, ATT_DK), BF16), _sds((H, T, ATT_DK), BF16), _sds((T, LANES), F32)],
        compiler_params=_params(("parallel", "arbitrary")),
    )(dq, dk, dv, cosT, sinT)


def _fox_bwd_prep(dq, dk, proj, bfor, lay, B, S, inv_scale):
    H, T, _ = dq.shape

    def body(dq_ref, dk_ref, fl_ref, bf_ref, dfl_ref, dbf_ref, dc_sc):
        b, h = pl.program_id(0), pl.program_id(1)
        lane = _lane(dc_sc.shape)
        col = jnp.sum(jnp.where(lane == 0, dq_ref[...], 0.0) - jnp.where(lane == 3, dk_ref[...], 0.0),
                      axis=1, keepdims=True)

        @pl.when(h == 0)
        def _():
            dc_sc[...] = jnp.zeros(dc_sc.shape, F32)

        dc_sc[...] = jnp.where(lane == h, col, dc_sc[...])

        @pl.when(h == H - 1)
        def _():
            dlogf = _cumsum_rows(dc_sc[...] * inv_scale, reverse=True)
            z = fl_ref[...] + bf_ref[...]
            dz = jnp.where(lane < H, dlogf * (1.0 / (1.0 + jnp.exp(z))), 0.0)
            dfl_ref[...] = dz
            _accumulate(dbf_ref, jnp.sum(dz, axis=0, keepdims=True), b == 0)

    aug = pl.BlockSpec((None, S, LANES), lambda b, h: (h, b, 1))
    seq = pl.BlockSpec((S, LANES), lambda b, h: (b, 0))
    vec = pl.BlockSpec((1, LANES), lambda b, h: (0, 0))
    return pl.pallas_call(
        body, name="fox_bwd_prep", grid=(B, H),
        in_specs=[aug, aug, pl.BlockSpec((S, LANES), lambda b, h: (b, lay["fl"] // LANES)), vec],
        out_specs=[seq, vec], out_shape=[_sds((T, LANES), F32), _sds((1, LANES), F32)],
        scratch_shapes=[pltpu.VMEM((S, LANES), F32)],
        compiler_params=_params(("arbitrary", "arbitrary")),
    )(dq, dk, proj, bfor)


def _heads_to_cols(dq, dk, dv):
    H, T, _ = dq.shape
    tm = _tile(T, ROW_TILE, 16)

    def body(a_ref, b_ref, c_ref, ao_ref, bo_ref, co_ref):
        ao_ref[...] = a_ref[...].astype(BF16)
        bo_ref[...] = b_ref[...].astype(BF16)
        co_ref[...] = c_ref[...].astype(BF16)

    src = pl.BlockSpec((None, tm, FOX_DIM), lambda i, h: (h, i, 0))
    dst = pl.BlockSpec((tm, FOX_DIM), lambda i, h: (i, h))
    return pl.pallas_call(
        body, name="heads_to_cols", grid=(T // tm, H), in_specs=[src, src, src], out_specs=[dst, dst, dst],
        out_shape=[_sds((T, H * FOX_DIM), BF16)] * 3, compiler_params=_params(("parallel", "parallel")),
    )(dq, dk, dv)


def _lat_bwd(dqn, dkvn, proj, gq, gkv, lay):
    T = dqn.shape[0]
    tm = _tile(T, ROW_TILE, 16)

    def body(dq_ref, dkv_ref, q_ref, kv_ref, gq_ref, gkv_ref, dql_ref, dkl_ref, dgq_ref, dgkv_ref):
        i = pl.program_id(0)
        dql, dgq = _rms_bwd(dq_ref[...], q_ref[...], gq_ref[...])
        dkl, dgkv = _rms_bwd(dkv_ref[...], kv_ref[...], gkv_ref[...])
        dql_ref[...] = dql.astype(BF16)
        dkl_ref[...] = dkl.astype(BF16)
        _accumulate(dgq_ref, jnp.sum(dgq, axis=0, keepdims=True), i == 0)
        _accumulate(dgkv_ref, jnp.sum(dgkv, axis=0, keepdims=True), i == 0)

    def blk(width, off=0):
        return pl.BlockSpec((tm, width), lambda i: (i, off // width))

    def vec(width):
        return pl.BlockSpec((1, width), lambda i: (0, 0))

    return pl.pallas_call(
        body, name="lat_bwd", grid=(T // tm,),
        in_specs=[blk(Q_LORA), blk(KV_LORA), blk(Q_LORA, lay["q"]), blk(KV_LORA, lay["kv"]), vec(Q_LORA), vec(KV_LORA)],
        out_specs=[blk(Q_LORA), blk(KV_LORA), vec(Q_LORA), vec(KV_LORA)],
        out_shape=[_sds((T, Q_LORA), BF16), _sds((T, KV_LORA), BF16), _sds((1, Q_LORA), F32), _sds((1, KV_LORA), F32)],
        compiler_params=_params(("arbitrary",)),
    )(dqn, dkvn, proj, proj, gq, gkv)


def _final_dx(dx1, dh, x, g):
    T, D = x.shape
    tm = _tile(T, ROW_TILE, 16)

    def body(dx1_ref, dh_ref, x_ref, g_ref, dx_ref, dg_ref):
        i = pl.program_id(0)
        d, dg = _rms_bwd(dh_ref[...], x_ref[...], g_ref[...])
        dx_ref[...] = dx1_ref[...] + d
        _accumulate(dg_ref, jnp.sum(dg, axis=0, keepdims=True), i == 0)

    row = pl.BlockSpec((tm, D), lambda i: (i, 0))
    vec = pl.BlockSpec((1, D), lambda i: (0, 0))
    return pl.pallas_call(
        body, name="final_dx", grid=(T // tm,), in_specs=[row, row, row, vec], out_specs=[row, vec],
        out_shape=[_sds((T, D), F32), _sds((1, D), F32)], compiler_params=_params(("arbitrary",)),
    )(dx1, dh, x, g)


def _adamw_math(w, g, m, v):
    m = ADAM_B1 * m + (1.0 - ADAM_B1) * g
    v = ADAM_B2 * v + (1.0 - ADAM_B2) * (g * g)
    m_hat = m / (1.0 - ADAM_B1 ** ADAM_STEP)
    v_hat = v / (1.0 - ADAM_B2 ** ADAM_STEP)
    delta = -ADAM_LR * (m_hat / (jnp.sqrt(v_hat) + ADAM_EPS) + ADAM_WD * w)
    return delta, m, v


def _sum_adamw(pieces, w, m, v, name):
    R, C = w.shape
    tr = _tile(R, 256, 16)

    def body(p_ref, w_ref, m_ref, v_ref, g_ref, d_ref, mo_ref, vo_ref):
        g = p_ref[0].astype(F32)
        for q in range(1, N_DEV):
            g = g + p_ref[q].astype(F32)
        g_ref[...] = g
        d_ref[...], mo_ref[...], vo_ref[...] = _adamw_math(w_ref[...], g, m_ref[...], v_ref[...])

    blk = pl.BlockSpec((tr, C), lambda i: (i, 0))
    return pl.pallas_call(
        body, name=name, grid=(R // tr,),
        in_specs=[pl.BlockSpec((N_DEV, tr, C), lambda i: (0, i, 0)), blk, blk, blk],
        out_specs=[blk] * 4, out_shape=[_sds((R, C), F32)] * 4, compiler_params=_params(("parallel",)),
    )(pieces, w, m, v)


def _layout(D):
    lay = {"q": 0, "kv": Q_LORA, "kpe": Q_LORA + KV_LORA}
    lay["fq"] = lay["kpe"] + LANES
    lay["fk"] = lay["fq"] + HEADS * FOX_DIM
    lay["fv"] = lay["fk"] + HEADS * FOX_DIM
    lay["fl"] = lay["fv"] + HEADS * FOX_DIM
    lay["g"] = lay["fl"] + LANES
    lay["end"] = lay["g"] + 2 * D
    return lay


def kernel(x, positions, pre_mix_norm, w_in, q_a_norm, w_uq, kv_a_norm, w_ukv, b_forget, b_gate, w_branch_mla, w_branch_fox, w_out, post_mix_norm, pre_ffn_norm, w_up, conv_w, conv_b, w_down, post_ffn_norm, loss_target, m_pre_mix_norm, m_w_in, m_q_a_norm, m_w_uq, m_kv_a_norm, m_w_ukv, m_b_forget, m_b_gate, m_w_branch_mla, m_w_branch_fox, m_w_out, m_post_mix_norm, m_pre_ffn_norm, m_w_up, m_conv_w, m_conv_b, m_w_down, m_post_ffn_norm, v_pre_mix_norm, v_w_in, v_q_a_norm, v_w_uq, v_kv_a_norm, v_w_ukv, v_b_forget, v_b_gate, v_w_branch_mla, v_w_branch_fox, v_w_out, v_post_mix_norm, v_pre_ffn_norm, v_w_up, v_conv_w, v_conv_b, v_w_down, v_post_ffn_norm):
    B, S, D = x.shape
    T = B * S
    F = conv_b.shape[0] // 2
    lay = _layout(D)
    n_in = w_in.shape[1]
    d_in = N_DEV * n_in
    seg_a = Q_LORA + KV_LORA + ROPE
    seg_b = 3 * HEADS * FOX_DIM + HEADS
    mla_scale = (NOPE + ROPE) ** -0.5
    fox_scale = FOX_DIM ** -0.5

    def row(vec, width=None):
        vec = vec.reshape(1, -1)
        if width is not None and vec.shape[1] < width:
            vec = jnp.pad(vec, ((0, 0), (0, width - vec.shape[1])))
        return vec

    big = [w_in, w_uq, w_ukv, w_branch_mla, w_branch_fox, w_out, w_up, w_down]
    names = ["w_in", "w_uq", "w_ukv", "w_branch_mla", "w_branch_fox", "w_out", "w_up", "w_down"]
    shards = [_cast_bf16(w, "cast_" + n) for w, n in zip(big, names)]
    gathered = _exchange(shards + [conv_w], scatter=False, name="gather_weights")
    win_g, wuq_g, wukv_g, wbm_g, wbf_g, wout_g, wup_g, wdown_g, cw_g = gathered

    win_full = jnp.transpose(win_g, (1, 0, 2)).reshape(D, d_in)
    w_perm = jnp.concatenate(
        [win_full[:, :seg_a], jnp.zeros((D, LANES - ROPE), BF16), win_full[:, seg_a:seg_a + seg_b],
         jnp.zeros((D, LANES - HEADS), BF16), win_full[:, seg_a + seg_b:]], axis=1)
    wuq_pad = jnp.pad(wuq_g, ((0, 0), (0, 0), (0, ATT_DK - NOPE - ROPE)))
    wbm = jnp.transpose(wbm_g, (1, 0, 2)).reshape(HEADS * VDIM, D)
    wbf = jnp.transpose(wbf_g, (1, 0, 2)).reshape(HEADS * FOX_DIM, D)
    wout = wout_g.reshape(D, D)
    wdown = wdown_g.reshape(F, D)
    cw_full = jnp.transpose(cw_g, (1, 0, 2)).reshape(3, 2 * F)
    cb_full = row(conv_b)

    x2 = x.reshape(T, D)
    tgt = loss_target.reshape(T, D)
    pos = positions.reshape(T, 1)
    inv_freq = 1.0 / (ROPE_THETA ** (jnp.arange(0, ROPE, 2, dtype=F32) / ROPE))
    invf = row(jnp.concatenate([inv_freq, inv_freq]), LANES)
    g_pre, g_q, g_kv = row(pre_mix_norm), row(q_a_norm), row(kv_a_norm)
    g_pm, g_ffn, g_pf = row(post_mix_norm), row(pre_ffn_norm), row(post_ffn_norm)
    bfor = row(b_forget, LANES)
    bgate = row(b_gate)

    h = _prenorm(x2, g_pre)
    proj = _matmul(h, w_perm, mode="nn", name="mm_proj")
    qn, kvn, kper, logf, cosT, sinT = _split_prep(proj, pos, invf, g_q, g_kv, bfor, lay)
    qraw = _matmul(qn, wuq_pad, mode="nn", name="mm_q", out_blocks=ATT_DK)
    kvraw = _matmul(kvn, wukv_g, mode="nn", name="mm_kv", out_blocks=NOPE + VDIM)
    q_mla, k_mla, v_mla = _mla_prep(qraw, kvraw, kper, cosT, sinT)
    o_mla, lse_mla = _attn_fwd(q_mla, k_mla, v_mla, B=B, S=S, unit=MLA_UNIT, scale=mla_scale, name="attn_mla_fwd")
    cs = _fox_cumsum(logf, B, S, 1.0 / fox_scale)
    q_fox, k_fox, v_fox = _fox_prep(proj, cs, lay)
    o_fox, lse_fox = _attn_fwd(q_fox, k_fox, v_fox, B=B, S=S, unit=1, scale=fox_scale, name="attn_fox_fwd")
    a_m = _matmul(o_mla, wbm, mode="nn", name="mm_branch_mla")
    a_f = _matmul(o_fox, wbf, mode="nn", name="mm_branch_fox")
    merged = _gate_merge(a_m, a_f, proj, bgate, lay, D)
    y1 = _matmul(merged, wout, mode="nn", name="mm_out")
    x1, h2 = _mid(x2, y1, g_pm, g_ffn)
    n_up = wup_g.shape[2]
    u = _matmul(h2, wup_g, mode="nn", name="mm_up", tn=n_up)
    act = _convffn_fwd(u, cw_full, cb_full, B, S, F)
    ff = _matmul(act, wdown, mode="nn", name="mm_down", tk=F // 2)
    dy, dff, loss_part, dg_pf = _tail(ff, x1, tgt, g_pf)

    dact = _matmul(dff, wdown, mode="nt", name="mm_dact")
    dw_down = _matmul(act, dff, mode="tn", name="mm_dw_down", tm=512)
    du_g, du_v, dcp_g, dcp_v = _convffn_bwd(u, dact, cw_full, cb_full, B, S, F)
    du = jnp.concatenate([du_g, du_v], axis=1)
    dh2 = _matmul(du, wup_g, mode="nt", name="mm_dh2")
    dw_up = _matmul(h2, du, mode="tn", name="mm_dw_up", out_blocks=n_up, tm=512, tn=n_up)
    dx1, dy1, dg_ffn, dg_pm = _mid_bwd(dy, dh2, x1, y1, g_ffn, g_pm)
    dmerged = _matmul(dy1, wout, mode="nt", name="mm_dmerged")
    dw_out = _matmul(merged, dy1, mode="tn", name="mm_dw_out")
    da_m, da_f, dgl_m, dgl_f, dbg_m, dbg_f = _gate_bwd(dmerged, a_m, a_f, proj, bgate, lay, D)
    do_mla = _matmul(da_m, wbm, mode="nt", name="mm_do_mla", out_dtype=BF16)
    do_fox = _matmul(da_f, wbf, mode="nt", name="mm_do_fox", out_dtype=BF16)
    dw_bm = _matmul(o_mla, da_m, mode="tn", name="mm_dw_branch_mla", out_blocks=D // N_DEV)
    dw_bf = _matmul(o_fox, da_f, mode="tn", name="mm_dw_branch_fox", out_blocks=D // N_DEV)
    dq_m, dk_m, dv_m = _attn_bwd(q_mla, k_mla, v_mla, o_mla, do_mla, lse_mla, B=B, S=S, unit=MLA_UNIT,
                                 scale=mla_scale, name="attn_mla_bwd")
    dq_f, dk_f, dv_f = _attn_bwd(q_fox, k_fox, v_fox, o_fox, do_fox, lse_fox, B=B, S=S, unit=1,
                                 scale=fox_scale, name="attn_fox_bwd")
    dqraw, dkvraw, dkpe = _mla_bwd_prep(dq_m, dk_m, dv_m, cosT, sinT)
    dqn = _matmul(dqraw, wuq_pad, mode="nt", name="mm_dqn")
    dw_uq = _matmul(qn, dqraw, mode="tn", name="mm_dw_uq", out_blocks=ATT_DK)
    dkvn = _matmul(dkvraw, wukv_g, mode="nt", name="mm_dkvn")
    dw_ukv = _matmul(kvn, dkvraw, mode="tn", name="mm_dw_ukv", out_blocks=NOPE + VDIM)
    dqlat, dkvlat, dg_q, dg_kv = _lat_bwd(dqn, dkvn, proj, g_q, g_kv, lay)
    dfl, dbfor = _fox_bwd_prep(dq_f, dk_f, proj, bfor, lay, B, S, 1.0 / fox_scale)
    dfq, dfk, dfv = _heads_to_cols(dq_f, dk_f, dv_f)
    dproj = jnp.concatenate([dqlat, dkvlat, dkpe.astype(BF16), dfq, dfk, dfv, dfl.astype(BF16), dgl_m, dgl_f], axis=1)
    dh = _matmul(dproj, w_perm, mode="nt", name="mm_dh", tk=2048)
    dw_perm = _matmul(h, dproj, mode="tn", name="mm_dw_in")
    grad_x, dg_pre = _final_dx(dx1, dh, x2, g_pre)

    dw_in_full = jnp.concatenate(
        [dw_perm[:, :seg_a], dw_perm[:, lay["fq"]:lay["fq"] + seg_b], dw_perm[:, lay["g"]:]], axis=1)
    dw_in_p = jnp.transpose(dw_in_full.reshape(D, N_DEV, n_in), (1, 0, 2))
    dcw = jnp.concatenate([dcp_g[0:3], dcp_v[0:3]], axis=1)
    dcw_p = jnp.transpose(dcw.reshape(3, N_DEV, (2 * F) // N_DEV), (1, 0, 2))
    grads_p = [dw_in_p, dw_uq[:, :, :NOPE + ROPE], dw_ukv, dw_bm, dw_bf, dw_out.reshape(N_DEV, D // N_DEV, D),
               dw_up, dw_down.reshape(N_DEV, F // N_DEV, D)]
    sends = [_cast_pieces_bf16(g, "castg_" + n) for g, n in zip(grads_p, names)]
    received = _exchange(sends + [dcw_p], scatter=True, name="scatter_grads")
    ms = [m_w_in, m_w_uq, m_w_ukv, m_w_branch_mla, m_w_branch_fox, m_w_out, m_w_up, m_w_down, m_conv_w]
    vs = [v_w_in, v_w_uq, v_w_ukv, v_w_branch_mla, v_w_branch_fox, v_w_out, v_w_up, v_w_down, v_conv_w]
    big_out = {}
    for n, p, w, m, v in zip(names + ["conv_w"], received, big + [conv_w], ms, vs):
        big_out[n] = _sum_adamw(p, w, m, v, "adamw_" + n)

    widths = [D, Q_LORA, KV_LORA, LANES, 2 * D, D, D, 2 * F, D]
    small_names = ["pre_mix_norm", "q_a_norm", "kv_a_norm", "b_forget", "b_gate", "post_mix_norm", "pre_ffn_norm",
                   "conv_b", "post_ffn_norm"]
    true_w = [D, Q_LORA, KV_LORA, HEADS, 2 * D, D, D, 2 * F, D]
    dcb = jnp.concatenate([dcp_g[3:4], dcp_v[3:4]], axis=1)
    part = jnp.concatenate([dg_pre, dg_q, dg_kv, dbfor, dbg_m, dbg_f, dg_pm, dg_ffn, dcb, dg_pf], axis=1)

    def pack(vals):
        return jnp.concatenate([row(a, wd) for a, wd in zip(vals, widths)], axis=1)

    sw = pack([pre_mix_norm, q_a_norm, kv_a_norm, b_forget, b_gate, post_mix_norm, pre_ffn_norm, conv_b, post_ffn_norm])
    sm = pack([m_pre_mix_norm, m_q_a_norm, m_kv_a_norm, m_b_forget, m_b_gate, m_post_mix_norm, m_pre_ffn_norm,
               m_conv_b, m_post_ffn_norm])
    sv = pack([v_pre_mix_norm, v_q_a_norm, v_kv_a_norm, v_b_forget, v_b_gate, v_post_mix_norm, v_pre_ffn_norm,
               v_conv_b, v_post_ffn_norm])
    (parts_all,) = _exchange([part], scatter=False, name="gather_small")
    sg, sd, smo, svo = _sum_adamw(parts_all, sw, sm, sv, "adamw_small")
    small_out = {}
    off = 0
    for n, wd, tw in zip(small_names, widths, true_w):
        small_out[n] = tuple(a[0, off:off + tw] for a in (sg, sd, smo, svo))
        off += wd

    loss = lax.psum(loss_part[0, 0], MESH_AXES)
    order = ["pre_mix_norm", "w_in", "q_a_norm", "w_uq", "kv_a_norm", "w_ukv", "b_forget", "b_gate", "w_branch_mla",
             "w_branch_fox", "w_out", "post_mix_norm", "pre_ffn_norm", "w_up", "conv_w", "conv_b", "w_down",
             "post_ffn_norm"]
    res = {**big_out, **small_out}
    outs = [loss, grad_x.reshape(B, S, D)]
    for kind in range(4):
        outs += [res[n][kind] for n in order]
    return tuple(outs)
```

```python
import math

import jax
import jax.numpy as jnp
from jax import lax
from jax.experimental import pallas as pl
from jax.experimental.pallas import tpu as pltpu

F32 = jnp.float32
BF16 = jnp.bfloat16

N_DEV = 8
N_CHIP = 4
HEADS = 8
NOPE = 128
ROPE = 64
HALF_ROPE = ROPE // 2
VDIM = 128
Q_LORA = 512
KV_LORA = 256
FOX_DIM = 128
ATT_DK = 256
MLA_UNIT = 64
ROPE_THETA = 10000.0
EPS = 1e-6
NEG_INF = -1e30
LANES = 128

ADAM_LR = 0.001
ADAM_B1 = 0.9
ADAM_B2 = 0.999
ADAM_EPS = 1e-08
ADAM_WD = 0.01
ADAM_STEP = 10

VMEM_LIMIT_BYTES = 56 * 1024 * 1024
ROW_TILE = 256
ATT_TILE = 512
MM_TILE = 1024

MESH_AXES = ("x", "y", "c")
ANY = pl.BlockSpec(memory_space=pl.ANY)


def _tile(n, pref, align=LANES):
    if n <= pref:
        return n
    t = (pref // align) * align
    while t >= align:
        if n % t == 0:
            return t
        t -= align
    return n


def _sds(shape, dtype):
    return jax.ShapeDtypeStruct(shape, dtype)


def _coords():
    x, y, c = (lax.axis_index(ax) for ax in MESH_AXES)
    return x, y, c


def _chip_rel(x, y, r):
    return (1 - x if r & 2 else x), (1 - y if r & 1 else y)


def _rcopy(src, dst, sems, w, k, dev):
    return pltpu.make_async_remote_copy(src_ref=src, dst_ref=dst, send_sem=sems[0].at[w, k], recv_sem=sems[1].at[w, k],
                                        device_id=dev, device_id_type=pl.DeviceIdType.MESH)


class _GatherPlan:
    def __init__(self, blocks, mid_frac=0.5):
        self.ins = list(blocks)
        self.out_shapes = [_sds((N_DEV,) + b.shape, b.dtype) for b in blocks]
        n = len(blocks)
        self.scratch = [pltpu.SemaphoreType.DMA((n, 7)), pltpu.SemaphoreType.DMA((n, 7)), pltpu.SemaphoreType.DMA((n,))]
        self.mid_frac = mid_frac

    def first(self, ins, outs, sems):
        x, y, c = _coords()
        me = 4 * x + 2 * y + c
        for w in range(len(ins)):
            pltpu.make_async_copy(ins[w], outs[w].at[me], sems[2].at[w]).start()
            _rcopy(ins[w], outs[w].at[me], sems, w, 0, (x, y, 1 - c)).start()
            for r in (1, 2, 3):
                px, py = _chip_rel(x, y, r)
                _rcopy(ins[w], outs[w].at[me], sems, w, r, (px, py, c)).start()

    def mid(self, ins, outs, sems):
        x, y, c = _coords()
        for w in range(len(ins)):
            for r in (1, 2, 3):
                px, py = _chip_rel(x, y, r)
                blk = outs[w].at[4 * px + 2 * py + c]
                _rcopy(ins[w], blk, sems, w, r, (px, py, c)).wait_recv()
                _rcopy(blk, blk, sems, w, 3 + r, (x, y, 1 - c)).start()

    def last(self, ins, outs, sems):
        x, y, c = _coords()
        me = 4 * x + 2 * y + c
        sib = (x, y, 1 - c)
        for w in range(len(ins)):
            _rcopy(ins[w], outs[w].at[4 * x + 2 * y + 1 - c], sems, w, 0, sib).wait_recv()
            for r in (1, 2, 3):
                px, py = _chip_rel(x, y, r)
                blk = outs[w].at[4 * px + 2 * py + 1 - c]
                _rcopy(blk, blk, sems, w, 3 + r, sib).wait_recv()
            for k in range(7):
                _rcopy(ins[w], outs[w].at[me], sems, w, k, sib).wait_send()
            pltpu.make_async_copy(ins[w], outs[w].at[me], sems[2].at[w]).wait()


class _DirectGatherPlan:
    mid = None

    def __init__(self, blocks):
        self.ins = list(blocks)
        self.out_shapes = [_sds((N_DEV,) + b.shape, b.dtype) for b in blocks]
        n = len(blocks)
        self.scratch = [pltpu.SemaphoreType.DMA((n, 7)), pltpu.SemaphoreType.DMA((n, 7)), pltpu.SemaphoreType.DMA((n,))]

    @staticmethod
    def _peer(x, y, c, r):
        return (1 - x if r & 4 else x), (1 - y if r & 2 else y), (1 - c if r & 1 else c)

    def first(self, ins, outs, sems):
        x, y, c = _coords()
        me = 4 * x + 2 * y + c
        for w in range(len(ins)):
            pltpu.make_async_copy(ins[w], outs[w].at[me], sems[2].at[w]).start()
            for r in range(1, N_DEV):
                _rcopy(ins[w], outs[w].at[me], sems, w, r - 1, self._peer(x, y, c, r)).start()

    def last(self, ins, outs, sems):
        x, y, c = _coords()
        me = 4 * x + 2 * y + c
        for w in range(len(ins)):
            for r in range(1, N_DEV):
                px, py, pc = self._peer(x, y, c, r)
                cp = _rcopy(ins[w], outs[w].at[4 * px + 2 * py + pc], sems, w, r - 1, (px, py, pc))
                cp.wait_recv()
                cp.wait_send()
            pltpu.make_async_copy(ins[w], outs[w].at[me], sems[2].at[w]).wait()


class _PairScatterPlan:
    mid = None

    def __init__(self, pieces):
        self.ins = list(pieces)
        self.out_shapes = [_sds((N_CHIP,) + p.shape[1:], p.dtype) for p in pieces]
        n = len(pieces)
        self.scratch = [pltpu.SemaphoreType.DMA((n, N_CHIP)), pltpu.SemaphoreType.DMA((n, N_CHIP))]

    def _copies(self, ins, outs, sems):
        x, y, c = _coords()
        return [_rcopy(ins[w].at[2 * q + 1 - c], outs[w].at[q], sems, w, q, (x, y, 1 - c))
                for w in range(len(ins)) for q in range(N_CHIP)]

    def first(self, ins, outs, sems):
        for cp in self._copies(ins, outs, sems):
            cp.start()

    def last(self, ins, outs, sems):
        for cp in self._copies(ins, outs, sems):
            cp.wait_recv()
            cp.wait_send()


class _ChipScatterPlan:
    mid = None

    def __init__(self, sums):
        self.ins = list(sums)
        self.out_shapes = [_sds(s.shape, s.dtype) for s in sums]
        n = len(sums)
        self.scratch = [pltpu.SemaphoreType.DMA((n, 3)), pltpu.SemaphoreType.DMA((n, 3))]

    def _copies(self, ins, outs, sems):
        x, y, c = _coords()
        cps = []
        for w in range(len(ins)):
            for r in (1, 2, 3):
                px, py = _chip_rel(x, y, r)
                cps.append(_rcopy(ins[w].at[r - 1], outs[w].at[r - 1], sems, w, r - 1, (px, py, c)))
        return cps

    def first(self, ins, outs, sems):
        for cp in self._copies(ins, outs, sems):
            cp.start()

    def last(self, ins, outs, sems):
        for cp in self._copies(ins, outs, sems):
            cp.wait_recv()
            cp.wait_send()


class _Comm:
    def __init__(self, plans):
        self.plans = list(plans)
        self.ins = [a for p in self.plans for a in p.ins]
        self.out_shapes = [s for p in self.plans for s in p.out_shapes]
        self.scratch = [s for p in self.plans for s in p.scratch]

    def _parts(self, ins, outs, sems):
        i = o = s = 0
        for p in self.plans:
            yield p, ins[i:i + len(p.ins)], outs[o:o + len(p.out_shapes)], sems[s:s + len(p.scratch)]
            i, o, s = i + len(p.ins), o + len(p.out_shapes), s + len(p.scratch)

    def begin(self, step, nsteps, ins, outs, sems):
        @pl.when(step == 0)
        def _():
            for p, pi, po, ps in self._parts(ins, outs, sems):
                p.first(pi, po, ps)

        for p, pi, po, ps in self._parts(ins, outs, sems):
            if p.mid is not None:
                @pl.when(step == min(nsteps - 1, int(p.mid_frac * nsteps)))
                def _(p=p, pi=pi, po=po, ps=ps):
                    p.mid(pi, po, ps)

    def end(self, step, nsteps, ins, outs, sems):
        @pl.when(step == nsteps - 1)
        def _():
            for p, pi, po, ps in self._parts(ins, outs, sems):
                p.last(pi, po, ps)


def _call(body, args, *, name, grid, in_specs, out_specs, out_shape, scratch_shapes=(), sem=None, comm=None):
    in_specs, out_specs, out_shape, scratch_shapes = list(in_specs), list(out_specs), list(out_shape), list(scratch_shapes)
    if comm is None:
        res = pl.pallas_call(
            body, name=name, grid=grid, in_specs=in_specs, out_specs=out_specs, out_shape=out_shape,
            scratch_shapes=scratch_shapes,
            compiler_params=pltpu.CompilerParams(dimension_semantics=sem, vmem_limit_bytes=VMEM_LIMIT_BYTES),
        )(*args)
        return list(res), []
    n_in, n_out, n_sc = len(in_specs), len(out_specs), len(scratch_shapes)
    n_ci, n_co = len(comm.ins), len(comm.out_shapes)
    nsteps = math.prod(grid)

    def hosted(*refs):
        ins, cins = refs[:n_in], refs[n_in:n_in + n_ci]
        o0 = n_in + n_ci
        outs, couts = refs[o0:o0 + n_out], refs[o0 + n_out:o0 + n_out + n_co]
        s0 = o0 + n_out + n_co
        scr, csems = refs[s0:s0 + n_sc], refs[s0 + n_sc:]
        step = jnp.int32(0)
        for d in range(len(grid)):
            step = step * grid[d] + pl.program_id(d)
        comm.begin(step, nsteps, cins, couts, csems)
        body(*ins, *outs, *scr)
        comm.end(step, nsteps, cins, couts, csems)

    res = pl.pallas_call(
        hosted, name=name, grid=grid, in_specs=in_specs + [ANY] * n_ci, out_specs=out_specs + [ANY] * n_co,
        out_shape=out_shape + comm.out_shapes, scratch_shapes=scratch_shapes + comm.scratch,
        compiler_params=pltpu.CompilerParams(dimension_semantics=("arbitrary",) * len(grid),
                                             vmem_limit_bytes=VMEM_LIMIT_BYTES, has_side_effects=True),
    )(*args, *comm.ins)
    return list(res[:n_out]), list(res[n_out:])


def _exchange_alone(comm, name):
    def body():
        pass

    return _call(body, [], name=name, grid=(), in_specs=[], out_specs=[], out_shape=[], comm=comm)[1]


def _matmul(a, b, *, mode, name, out_dtype=F32, out_blocks=None, tm=None, tn=None, tk=None, comm=None):
    tm = MM_TILE if tm is None else tm
    tn = MM_TILE if tn is None else tn
    a_blk = a.ndim == 3
    b_blk = b.ndim == 3
    if mode == "nn":
        M, K = a.shape
        N = b.shape[0] * b.shape[2] if b_blk else b.shape[1]
        dims = (((1,), (0,)), ((), ()))
    elif mode == "nt":
        M = a.shape[1] if a_blk else a.shape[0]
        K = a.shape[0] * a.shape[2] if a_blk else a.shape[1]
        N = b.shape[1] if b_blk else b.shape[0]
        dims = (((1,), (1,)), ((), ()))
    else:
        K, M = a.shape
        N = b.shape[0] * b.shape[2] if b_blk else b.shape[1]
        dims = (((0,), (0,)), ((), ()))

    tm = _tile(M, tm)
    tn = _tile(N, tn)
    if mode == "nt" and (a_blk or b_blk):
        tk = a.shape[2] if a_blk else b.shape[2]
    else:
        tk = _tile(K, K if tk is None else tk)
    if mode != "nt" and b_blk:
        tn = _tile(b.shape[2], tn)
    if out_blocks is not None:
        tn = _tile(out_blocks, tn)
    nk = K // tk
    grid = (M // tm, N // tn, nk)

    if mode == "nn":
        a_spec = pl.BlockSpec((tm, tk), lambda i, j, k: (i, k))
        if b_blk:
            rb = b.shape[2] // tn
            b_spec = pl.BlockSpec((None, tk, tn), lambda i, j, k: (j // rb, k, j % rb))
        else:
            b_spec = pl.BlockSpec((tk, tn), lambda i, j, k: (k, j))
    elif mode == "nt":
        if a_blk:
            a_spec = pl.BlockSpec((None, tm, tk), lambda i, j, k: (k, i, 0))
        else:
            a_spec = pl.BlockSpec((tm, tk), lambda i, j, k: (i, k))
        if b_blk:
            b_spec = pl.BlockSpec((None, tn, tk), lambda i, j, k: (k, j, 0))
        else:
            b_spec = pl.BlockSpec((tn, tk), lambda i, j, k: (j, k))
    else:
        a_spec = pl.BlockSpec((tk, tm), lambda i, j, k: (k, i))
        if b_blk:
            rb = b.shape[2] // tn
            b_spec = pl.BlockSpec((None, tk, tn), lambda i, j, k: (j // rb, k, j % rb))
        else:
            b_spec = pl.BlockSpec((tk, tn), lambda i, j, k: (k, j))

    if out_blocks is None:
        o_spec = pl.BlockSpec((tm, tn), lambda i, j, k: (i, j))
        o_shape = _sds((M, N), out_dtype)
    else:
        ro = out_blocks // tn
        o_spec = pl.BlockSpec((None, tm, tn), lambda i, j, k: (j // ro, i, j % ro))
        o_shape = _sds((N // out_blocks, M, out_blocks), out_dtype)

    def body(a_ref, b_ref, o_ref, *scratch):
        part = lax.dot_general(a_ref[...], b_ref[...], dims, preferred_element_type=F32)
        if nk == 1:
            o_ref[...] = part.astype(o_ref.dtype)
        else:
            acc_ref = scratch[0]
            k = pl.program_id(2)

            @pl.when(k == 0)
            def _():
                acc_ref[...] = part

            @pl.when(k > 0)
            def _():
                acc_ref[...] += part

            @pl.when(k == nk - 1)
            def _():
                o_ref[...] = acc_ref[...].astype(o_ref.dtype)

    scratch = [] if nk == 1 else [pltpu.VMEM((tm, tn), F32)]
    outs, landed = _call(body, [a, b], name=name, grid=grid, in_specs=[a_spec, b_spec], out_specs=[o_spec],
                         out_shape=[o_shape], scratch_shapes=scratch, sem=("parallel", "parallel", "arbitrary"), comm=comm)
    return outs[0] if comm is None else (outs[0], landed)


def _rms(x):
    return lax.rsqrt(jnp.mean(x * x, axis=-1, keepdims=True) + EPS)


def _rms_bwd(dy, x, g):
    r = _rms(x)
    n = x * r
    dn = dy * g
    dx = r * (dn - n * jnp.mean(dn * n, axis=-1, keepdims=True))
    return dx, dy * n


def _sigmoid(x):
    return 1.0 / (1.0 + jnp.exp(-x))


def _rope_rot(t):
    return pltpu.roll(t, HALF_ROPE, 1) - pltpu.roll(t, LANES - HALF_ROPE, 1)


def _lane(shape):
    return lax.broadcasted_iota(jnp.int32, shape, 1)


def _split3(x):
    hi = x.astype(BF16).astype(F32)
    r1 = x - hi
    mid = r1.astype(BF16).astype(F32)
    lo = (r1 - mid).astype(BF16).astype(F32)
    return hi, mid, lo


def _cumsum_rows(x, reverse):
    S = x.shape[0]
    bs = min(256, S)
    nb = S // bs
    r = lax.broadcasted_iota(jnp.int32, (bs, bs), 0)
    c = lax.broadcasted_iota(jnp.int32, (bs, bs), 1)
    tri = jnp.where((c >= r) if reverse else (c <= r), 1.0, 0.0).astype(BF16)
    edge = lax.broadcasted_iota(jnp.int32, (bs, x.shape[1]), 0) == (0 if reverse else bs - 1)
    carry = jnp.zeros((1, x.shape[1]), F32)
    outs = [None] * nb
    for bi in (range(nb - 1, -1, -1) if reverse else range(nb)):
        xb = x[bi * bs:(bi + 1) * bs, :]
        acc = carry
        for term in _split3(xb):
            acc = acc + jnp.dot(tri, term.astype(BF16), preferred_element_type=F32)
        outs[bi] = acc
        carry = jnp.sum(jnp.where(edge, acc, 0.0), axis=0, keepdims=True)
    return jnp.concatenate(outs, axis=0) if nb > 1 else outs[0]


def _gelu_parts(x):
    c0 = math.sqrt(2.0 / math.pi)
    inner = c0 * (x + 0.044715 * (x * x * x))
    t = jnp.tanh(inner)
    g = 0.5 * x * (1.0 + t)
    dg = 0.5 * (1.0 + t) + 0.5 * x * (1.0 - t * t) * (c0 * (1.0 + 3.0 * 0.044715 * (x * x)))
    return g, dg


def _accumulate(ref, value, first):
    @pl.when(first)
    def _():
        ref[...] = value

    @pl.when(jnp.logical_not(first))
    def _():
        ref[...] += value


def _cast_bf16(w, name):
    R, C = w.shape
    tr = _tile(R, 512, 16)

    def body(w_ref, o_ref):
        o_ref[...] = w_ref[...].astype(BF16)

    blk = pl.BlockSpec((tr, C), lambda i: (i, 0))
    return _call(body, [w], name=name, grid=(R // tr,), in_specs=[blk], out_specs=[blk],
                 out_shape=[_sds((R, C), BF16)], sem=("parallel",))[0][0]


def _prenorm(x, g):
    T, D = x.shape
    tm = _tile(T, ROW_TILE, 16)

    def body(x_ref, g_ref, h_ref):
        xv = x_ref[...]
        h_ref[...] = (xv * _rms(xv) * g_ref[...]).astype(BF16)

    row = pl.BlockSpec((tm, D), lambda i: (i, 0))
    return _call(body, [x, g], name="prenorm", grid=(T // tm,),
                 in_specs=[row, pl.BlockSpec((1, D), lambda i: (0, 0))], out_specs=[row],
                 out_shape=[_sds((T, D), BF16)], sem=("parallel",))[0][0]


def _split_prep(proj, pos, invf, gq, gkv, bfor, lay):
    T = proj.shape[0]
    tm = _tile(T, ROW_TILE, 16)

    def body(q_ref, kv_ref, kpe_ref, fl_ref, pos_ref, invf_ref, gq_ref, gkv_ref, bf_ref,
             qn_ref, kvn_ref, kper_ref, logf_ref, cos_ref, sin_ref):
        ql = q_ref[...]
        qn_ref[...] = (ql * _rms(ql) * gq_ref[...]).astype(BF16)
        kl = kv_ref[...]
        kvn_ref[...] = (kl * _rms(kl) * gkv_ref[...]).astype(BF16)
        ang = pos_ref[...].astype(F32) * invf_ref[...]
        valid = _lane(ang.shape) < ROPE
        cs = jnp.where(valid, jnp.cos(ang), 0.0)
        sn = jnp.where(valid, jnp.sin(ang), 0.0)
        cos_ref[...] = cs
        sin_ref[...] = sn
        kp = jnp.where(valid, kpe_ref[...], 0.0)
        kper_ref[...] = (kp * cs + _rope_rot(kp) * sn).astype(BF16)
        z = fl_ref[...] + bf_ref[...]
        logf_ref[...] = jnp.minimum(z, 0.0) - jnp.log(1.0 + jnp.exp(-jnp.abs(z)))

    def col(width, off):
        return pl.BlockSpec((tm, width), lambda i: (i, off // width))

    def vec(width):
        return pl.BlockSpec((1, width), lambda i: (0, 0))

    def out(width):
        return pl.BlockSpec((tm, width), lambda i: (i, 0))

    return _call(
        body, [proj, proj, proj, proj, pos, invf, gq, gkv, bfor], name="split_prep", grid=(T // tm,),
        in_specs=[col(Q_LORA, lay["q"]), col(KV_LORA, lay["kv"]), col(LANES, lay["kpe"]), col(LANES, lay["fl"]),
                  pl.BlockSpec((tm, 1), lambda i: (i, 0)), vec(LANES), vec(Q_LORA), vec(KV_LORA), vec(LANES)],
        out_specs=[out(Q_LORA), out(KV_LORA), out(LANES), out(LANES), out(LANES), out(LANES)],
        out_shape=[_sds((T, Q_LORA), BF16), _sds((T, KV_LORA), BF16), _sds((T, LANES), BF16),
                   _sds((T, LANES), F32), _sds((T, LANES), F32), _sds((T, LANES), F32)],
        sem=("parallel",))[0]


def _mla_prep(qraw, kvraw, kper, cosT, sinT):
    H, T, _ = qraw.shape
    tm = _tile(T, ROW_TILE, 16)

    def body(q_ref, kv_ref, kpe_ref, cos_ref, sin_ref, qo_ref, ko_ref, vo_ref):
        q = q_ref[...]
        pe = q[:, NOPE:]
        pe = jnp.where(_lane(pe.shape) < ROPE, pe, 0.0)
        qo_ref[:, :NOPE] = q[:, :NOPE].astype(BF16)
        qo_ref[:, NOPE:] = (pe * cos_ref[...] + _rope_rot(pe) * sin_ref[...]).astype(BF16)
        kv = kv_ref[...]
        ko_ref[:, :NOPE] = kv[:, :NOPE].astype(BF16)
        ko_ref[:, NOPE:] = kpe_ref[...]
        vo_ref[...] = kv[:, NOPE:].astype(BF16)

    head = pl.BlockSpec((None, tm, ATT_DK), lambda h, i: (h, i, 0))
    tok = pl.BlockSpec((tm, LANES), lambda h, i: (i, 0))
    return _call(
        body, [qraw, kvraw, kper, cosT, sinT], name="mla_prep", grid=(H, T // tm),
        in_specs=[head, head, tok, tok, tok],
        out_specs=[head, head, pl.BlockSpec((None, tm, VDIM), lambda h, i: (h, i, 0))],
        out_shape=[_sds((H, T, ATT_DK), BF16), _sds((H, T, ATT_DK), BF16), _sds((H, T, VDIM), BF16)],
        sem=("parallel", "parallel"))[0]


def _fox_cumsum(logf, B, S, inv_scale):
    T = logf.shape[0]

    def body(l_ref, c_ref):
        c_ref[...] = _cumsum_rows(l_ref[...], reverse=False) * inv_scale

    seq = pl.BlockSpec((S, LANES), lambda b: (b, 0))
    return _call(body, [logf], name="fox_cumsum", grid=(B,), in_specs=[seq], out_specs=[seq],
                 out_shape=[_sds((T, LANES), F32)], sem=("parallel",))[0][0]


def _fox_prep(proj, cs, lay):
    T = proj.shape[0]
    tm = _tile(T, ROW_TILE, 16)

    def body(q_ref, k_ref, v_ref, cs_ref, qo_ref, ko_ref, vo_ref):
        h = pl.program_id(0)
        cv = cs_ref[...]
        lane = _lane(cv.shape)
        ccol = jnp.sum(jnp.where(lane == h, cv, 0.0), axis=1, keepdims=True)
        hi, mid, lo = _split3(ccol)
        one = jnp.where(lane < 6, 1.0, 0.0)
        augq = jnp.where(lane == 0, hi, jnp.where(lane == 1, mid, jnp.where(lane == 2, lo, one)))
        augk = jnp.where(lane < 3, 1.0, jnp.where(lane == 3, -hi, jnp.where(lane == 4, -mid, jnp.where(lane == 5, -lo, 0.0))))
        qo_ref[:, :FOX_DIM] = q_ref[...].astype(BF16)
        qo_ref[:, FOX_DIM:] = augq.astype(BF16)
        ko_ref[:, :FOX_DIM] = k_ref[...].astype(BF16)
        ko_ref[:, FOX_DIM:] = augk.astype(BF16)
        vo_ref[...] = v_ref[...].astype(BF16)

    def col(off):
        return pl.BlockSpec((tm, FOX_DIM), lambda h, i: (i, off // FOX_DIM + h))

    head = pl.BlockSpec((None, tm, ATT_DK), lambda h, i: (h, i, 0))
    return _call(
        body, [proj, proj, proj, cs], name="fox_prep", grid=(HEADS, T // tm),
        in_specs=[col(lay["fq"]), col(lay["fk"]), col(lay["fv"]), pl.BlockSpec((tm, LANES), lambda h, i: (i, 0))],
        out_specs=[head, head, pl.BlockSpec((None, tm, VDIM), lambda h, i: (h, i, 0))],
        out_shape=[_sds((HEADS, T, ATT_DK), BF16), _sds((HEADS, T, ATT_DK), BF16), _sds((HEADS, T, VDIM), BF16)],
        sem=("parallel", "parallel"))[0]


def _visible(tq, tk, unit):
    r = lax.broadcasted_iota(jnp.int32, (tq, tk), 0)
    c = lax.broadcasted_iota(jnp.int32, (tq, tk), 1)
    sh = int(math.log2(unit))
    return lax.shift_right_logical(c, sh) <= lax.shift_right_logical(r, sh)


def _attn_fwd(q, k, v, *, B, S, unit, scale, name, comm=None):
    H, T, DK = q.shape
    DV = v.shape[2]
    tq = _tile(S, ATT_TILE)
    nq = S // tq
    NT = (((1,), (1,)), ((), ()))

    def body(q_ref, k_ref, v_ref, o_ref, lse_ref, m_sc, l_sc, acc_sc):
        i, j = pl.program_id(1), pl.program_id(2)

        @pl.when(j == 0)
        def _():
            m_sc[...] = jnp.full(m_sc.shape, NEG_INF, F32)
            l_sc[...] = jnp.zeros(l_sc.shape, F32)
            acc_sc[...] = jnp.zeros(acc_sc.shape, F32)

        def step(diagonal):
            s = lax.dot_general(q_ref[...], k_ref[...], NT, preferred_element_type=F32) * scale
            if diagonal:
                s = jnp.where(_visible(tq, tq, unit), s, NEG_INF)
            m_prev = m_sc[...]
            m_new = jnp.maximum(m_prev, jnp.max(s, axis=1, keepdims=True))
            alpha = jnp.exp(m_prev - m_new)
            p = jnp.exp(s - m_new)
            l_sc[...] = alpha * l_sc[...] + jnp.sum(p, axis=1, keepdims=True)
            acc_sc[...] = alpha * acc_sc[...] + jnp.dot(p.astype(BF16), v_ref[...], preferred_element_type=F32)
            m_sc[...] = m_new

        @pl.when(j < i)
        def _():
            step(False)

        @pl.when(j == i)
        def _():
            step(True)
            l = l_sc[...]
            o_ref[...] = (acc_sc[...] / l).astype(BF16)
            lse_ref[...] = jnp.broadcast_to(m_sc[...] + jnp.log(l), lse_ref.shape)

    def qmap(g, i, j):
        return (g % H, (g // H) * nq + i, 0)

    def kmap(g, i, j):
        return (g % H, (g // H) * nq + jnp.minimum(j, i), 0)

    outs, landed = _call(
        body, [q, k, v], name=name, grid=(B * H, nq, nq),
        in_specs=[pl.BlockSpec((None, tq, DK), qmap), pl.BlockSpec((None, tq, DK), kmap),
                  pl.BlockSpec((None, tq, DV), kmap)],
        out_specs=[pl.BlockSpec((tq, DV), lambda g, i, j: ((g // H) * nq + i, g % H)),
                   pl.BlockSpec((None, tq, LANES), qmap)],
        out_shape=[_sds((T, H * DV), BF16), _sds((H, T, LANES), F32)],
        scratch_shapes=[pltpu.VMEM((tq, 1), F32), pltpu.VMEM((tq, 1), F32), pltpu.VMEM((tq, DV), F32)],
        sem=("parallel", "parallel", "arbitrary"), comm=comm)
    return outs[0], outs[1], landed


def _attn_bwd(q, k, v, o, do, lse, *, B, S, unit, scale, name, comm=None):
    H, T, DK = q.shape
    DV = v.shape[2]
    tq = _tile(S, ATT_TILE)
    nq = S // tq
    NT = (((1,), (1,)), ((), ()))
    TN = (((0,), (0,)), ((), ()))

    def body(q_ref, k_ref, v_ref, o_ref, do_ref, lse_ref, dq_ref, dk_ref, dv_ref):
        j, i = pl.program_id(1), pl.program_id(2)

        @pl.when(jnp.logical_and(j == 0, i == 0))
        def _():
            dq_ref[...] = jnp.zeros(dq_ref.shape, F32)

        @pl.when(i == 0)
        def _():
            dk_ref[...] = jnp.zeros(dk_ref.shape, F32)
            dv_ref[...] = jnp.zeros(dv_ref.shape, F32)

        def step(diagonal):
            qv, kv_, dov = q_ref[...], k_ref[...], do_ref[...]
            s = lax.dot_general(qv, kv_, NT, preferred_element_type=F32) * scale
            if diagonal:
                s = jnp.where(_visible(tq, tq, unit), s, NEG_INF)
            p = jnp.exp(s - jnp.tile(lse_ref[...], (1, tq // LANES)))
            dp = lax.dot_general(dov, v_ref[...], NT, preferred_element_type=F32)
            delta = jnp.sum(dov.astype(F32) * o_ref[...].astype(F32), axis=1, keepdims=True)
            ds = (p * (dp - delta) * scale).astype(BF16)
            dv_ref[...] += lax.dot_general(p.astype(BF16), dov, TN, preferred_element_type=F32)
            dk_ref[...] += lax.dot_general(ds, qv, TN, preferred_element_type=F32)
            rows = pl.ds(pl.multiple_of(i * tq, tq), tq)
            dq_ref[rows, :] += jnp.dot(ds, kv_, preferred_element_type=F32)

        @pl.when(i > j)
        def _():
            step(False)

        @pl.when(i == j)
        def _():
            step(True)

    def qmap(g, j, i):
        return (g % H, (g // H) * nq + jnp.maximum(i, j), 0)

    def kmap(g, j, i):
        return (g % H, (g // H) * nq + j, 0)

    def omap(g, j, i):
        return ((g // H) * nq + jnp.maximum(i, j), g % H)

    outs, landed = _call(
        body, [q, k, v, o, do, lse], name=name, grid=(B * H, nq, nq),
        in_specs=[pl.BlockSpec((None, tq, DK), qmap), pl.BlockSpec((None, tq, DK), kmap),
                  pl.BlockSpec((None, tq, DV), kmap), pl.BlockSpec((tq, DV), omap), pl.BlockSpec((tq, DV), omap),
                  pl.BlockSpec((None, tq, LANES), qmap)],
        out_specs=[pl.BlockSpec((None, S, DK), lambda g, j, i: (g % H, g // H, 0)),
                   pl.BlockSpec((None, tq, DK), kmap), pl.BlockSpec((None, tq, DV), kmap)],
        out_shape=[_sds((H, T, DK), F32), _sds((H, T, DK), F32), _sds((H, T, DV), F32)],
        sem=("parallel", "arbitrary", "arbitrary"), comm=comm)
    return outs[0], outs[1], outs[2], landed


def _gate_merge(am, af, proj, bgate, lay, D):
    T = am.shape[0]
    tm = _tile(T, ROW_TILE, 16)
    tn = _tile(D, 1024)

    def body(am_ref, af_ref, gm_ref, gf_ref, bm_ref, bf_ref, o_ref):
        sm = _sigmoid(gm_ref[...] + bm_ref[...])
        sf = _sigmoid(gf_ref[...] + bf_ref[...])
        o_ref[...] = (sm * am_ref[...] + sf * af_ref[...]).astype(BF16)

    og = lay["g"] // tn
    blk = pl.BlockSpec((tm, tn), lambda i, j: (i, j))
    return _call(
        body, [am, af, proj, proj, bgate, bgate], name="gate_merge", grid=(T // tm, D // tn),
        in_specs=[blk, blk, pl.BlockSpec((tm, tn), lambda i, j: (i, og + j)),
                  pl.BlockSpec((tm, tn), lambda i, j: (i, og + D // tn + j)),
                  pl.BlockSpec((1, tn), lambda i, j: (0, j)), pl.BlockSpec((1, tn), lambda i, j: (0, D // tn + j))],
        out_specs=[blk], out_shape=[_sds((T, D), BF16)], sem=("parallel", "parallel"))[0][0]


def _mid(x, y1, g_pm, g_ffn):
    T, D = x.shape
    tm = _tile(T, ROW_TILE, 16)

    def body(x_ref, y_ref, gp_ref, gf_ref, x1_ref, h2_ref):
        y = y_ref[...]
        x1 = x_ref[...] + y * _rms(y) * gp_ref[...]
        x1_ref[...] = x1
        h2_ref[...] = (x1 * _rms(x1) * gf_ref[...]).astype(BF16)

    row = pl.BlockSpec((tm, D), lambda i: (i, 0))
    vec = pl.BlockSpec((1, D), lambda i: (0, 0))
    return _call(body, [x, y1, g_pm, g_ffn], name="mid", grid=(T // tm,), in_specs=[row, row, vec, vec],
                 out_specs=[row, row], out_shape=[_sds((T, D), F32), _sds((T, D), BF16)], sem=("parallel",))[0]


def _conv3(u, w_ref, bias):
    row = lax.broadcasted_iota(jnp.int32, u.shape, 0)
    u1 = jnp.where(row >= 1, pltpu.roll(u, 1, 0), 0.0)
    u2 = jnp.where(row >= 2, pltpu.roll(u, 2, 0), 0.0)
    return w_ref[0:1, :] * u2 + w_ref[1:2, :] * u1 + w_ref[2:3, :] * u + bias, u1, u2


def _convffn_fwd(u, cw, cb, B, S, F):
    T = u.shape[0]
    tn = _tile(F, 256)
    nf = F // tn

    def body(ug_ref, uv_ref, wg_ref, wv_ref, bg_ref, bv_ref, a_ref):
        g, _, _ = _conv3(ug_ref[...], wg_ref, bg_ref[...])
        val, _, _ = _conv3(uv_ref[...], wv_ref, bv_ref[...])
        a_ref[...] = (_gelu_parts(g)[0] * val).astype(BF16)

    def seq(off):
        return pl.BlockSpec((S, tn), lambda b, j: (b, off + j))

    def par(rows, off):
        return pl.BlockSpec((rows, tn), lambda b, j: (0, off + j))

    return _call(body, [u, u, cw, cw, cb, cb], name="convffn_fwd", grid=(B, nf),
                 in_specs=[seq(0), seq(nf), par(3, 0), par(3, nf), par(1, 0), par(1, nf)],
                 out_specs=[seq(0)], out_shape=[_sds((T, F), BF16)], sem=("parallel", "parallel"))[0][0]


def _convffn_bwd(u, dact, cw, cb, B, S, F, comm=None):
    T = u.shape[0]
    tn = _tile(F, 256)
    nf = F // tn

    def body(ug_ref, uv_ref, da_ref, wg_ref, wv_ref, bg_ref, bv_ref, dug_ref, duv_ref, dpg_ref, dpv_ref):
        b = pl.program_id(1)
        ug, uv, da = ug_ref[...], uv_ref[...], da_ref[...]
        g, ug1, ug2 = _conv3(ug, wg_ref, bg_ref[...])
        val, uv1, uv2 = _conv3(uv, wv_ref, bv_ref[...])
        gel, dgel = _gelu_parts(g)
        dg = da * val * dgel
        dval = da * gel
        row = lax.broadcasted_iota(jnp.int32, ug.shape, 0)

        def back(d, w_ref):
            d1 = jnp.where(row < S - 1, pltpu.roll(d, S - 1, 0), 0.0)
            d2 = jnp.where(row < S - 2, pltpu.roll(d, S - 2, 0), 0.0)
            return w_ref[2:3, :] * d + w_ref[1:2, :] * d1 + w_ref[0:1, :] * d2

        dug_ref[...] = back(dg, wg_ref).astype(BF16)
        duv_ref[...] = back(dval, wv_ref).astype(BF16)

        def sums(d, u0, u1, u2):
            r8 = lax.broadcasted_iota(jnp.int32, (8, d.shape[1]), 0)
            out = jnp.zeros((8, d.shape[1]), F32)
            for k, t in enumerate((d * u2, d * u1, d * u0, d)):
                out = jnp.where(r8 == k, jnp.sum(t, axis=0, keepdims=True), out)
            return out

        _accumulate(dpg_ref, sums(dg, ug, ug1, ug2), b == 0)
        _accumulate(dpv_ref, sums(dval, uv, uv1, uv2), b == 0)

    def seq(off):
        return pl.BlockSpec((S, tn), lambda j, b: (b, off + j))

    def par(rows, off):
        return pl.BlockSpec((rows, tn), lambda j, b: (0, off + j))

    outs, landed = _call(
        body, [u, u, dact, cw, cw, cb, cb], name="convffn_bwd", grid=(nf, B),
        in_specs=[seq(0), seq(nf), seq(0), par(3, 0), par(3, nf), par(1, 0), par(1, nf)],
        out_specs=[seq(0), seq(0), par(8, 0), par(8, 0)],
        out_shape=[_sds((T, F), BF16), _sds((T, F), BF16), _sds((8, F), F32), _sds((8, F), F32)],
        sem=("parallel", "arbitrary"), comm=comm)
    return outs, landed


def _tail(ff, x1, tgt, g):
    T, D = ff.shape
    tm = _tile(T, ROW_TILE, 16)

    def body(ff_ref, x1_ref, t_ref, g_ref, dy_ref, dff_ref, loss_ref, dg_ref):
        i = pl.program_id(0)
        f = ff_ref[...]
        gv = g_ref[...]
        r = _rms(f)
        n = f * r
        e = (x1_ref[...] + n * gv) - t_ref[...]
        dy = e * (1.0 / D)
        dy_ref[...] = dy
        dn = dy * gv
        dff_ref[...] = (r * (dn - n * jnp.mean(dn * n, axis=-1, keepdims=True))).astype(BF16)
        part = 0.5 * jnp.sum(jnp.mean(e * e, axis=-1, keepdims=True), axis=0, keepdims=True)
        _accumulate(loss_ref, jnp.broadcast_to(part, loss_ref.shape), i == 0)
        _accumulate(dg_ref, jnp.sum(dy * n, axis=0, keepdims=True), i == 0)

    row = pl.BlockSpec((tm, D), lambda i: (i, 0))
    vec = pl.BlockSpec((1, D), lambda i: (0, 0))
    return _call(body, [ff, x1, tgt, g], name="tail", grid=(T // tm,), in_specs=[row, row, row, vec],
                 out_specs=[row, row, pl.BlockSpec((8, LANES), lambda i: (0, 0)), vec],
                 out_shape=[_sds((T, D), F32), _sds((T, D), BF16), _sds((8, LANES), F32), _sds((1, D), F32)],
                 sem=("arbitrary",))[0]


def _mid_bwd(dy, dh2, x1, y1, g_ffn, g_pm, comm=None):
    T, D = dy.shape
    tm = _tile(T, ROW_TILE, 16)

    def body(dy_ref, dh_ref, x1_ref, y1_ref, gf_ref, gp_ref, dx1_ref, dy1_ref, dgf_ref, dgp_ref):
        i = pl.program_id(0)
        dh = dh_ref[...]
        d2, dgf = _rms_bwd(dh, x1_ref[...], gf_ref[...])
        dx1 = dy_ref[...] + d2
        dx1_ref[...] = dx1
        d1, dgp = _rms_bwd(dx1, y1_ref[...], gp_ref[...])
        dy1_ref[...] = d1.astype(BF16)
        _accumulate(dgf_ref, jnp.sum(dgf, axis=0, keepdims=True), i == 0)
        _accumulate(dgp_ref, jnp.sum(dgp, axis=0, keepdims=True), i == 0)

    row = pl.BlockSpec((tm, D), lambda i: (i, 0))
    vec = pl.BlockSpec((1, D), lambda i: (0, 0))
    return _call(body, [dy, dh2, x1, y1, g_ffn, g_pm], name="mid_bwd", grid=(T // tm,),
                 in_specs=[row, row, row, row, vec, vec], out_specs=[row, row, vec, vec],
                 out_shape=[_sds((T, D), F32), _sds((T, D), BF16), _sds((1, D), F32), _sds((1, D), F32)],
                 sem=("arbitrary",), comm=comm)


def _gate_bwd(dm, am, af, proj, bgate, lay, D):
    T = dm.shape[0]
    tm = _tile(T, ROW_TILE, 16)
    tn = _tile(D, 512)

    def body(dm_ref, am_ref, af_ref, gm_ref, gf_ref, bm_ref, bf_ref,
             dam_ref, daf_ref, dgm_ref, dgf_ref, dbm_ref, dbf_ref):
        i = pl.program_id(1)
        d = dm_ref[...]
        sm = _sigmoid(gm_ref[...] + bm_ref[...])
        sf = _sigmoid(gf_ref[...] + bf_ref[...])
        dam_ref[...] = (d * sm).astype(BF16)
        daf_ref[...] = (d * sf).astype(BF16)
        dgm = d * am_ref[...] * (sm * (1.0 - sm))
        dgf = d * af_ref[...] * (sf * (1.0 - sf))
        dgm_ref[...] = dgm.astype(BF16)
        dgf_ref[...] = dgf.astype(BF16)
        _accumulate(dbm_ref, jnp.sum(dgm, axis=0, keepdims=True), i == 0)
        _accumulate(dbf_ref, jnp.sum(dgf, axis=0, keepdims=True), i == 0)

    og = lay["g"] // tn
    blk = pl.BlockSpec((tm, tn), lambda j, i: (i, j))
    vec = pl.BlockSpec((1, tn), lambda j, i: (0, j))
    return _call(
        body, [dm, am, af, proj, proj, bgate, bgate], name="gate_bwd", grid=(D // tn, T // tm),
        in_specs=[blk, blk, blk, pl.BlockSpec((tm, tn), lambda j, i: (i, og + j)),
                  pl.BlockSpec((tm, tn), lambda j, i: (i, og + D // tn + j)),
                  vec, pl.BlockSpec((1, tn), lambda j, i: (0, D // tn + j))],
        out_specs=[blk, blk, blk, blk, vec, vec],
        out_shape=[_sds((T, D), BF16)] * 4 + [_sds((1, D), F32)] * 2, sem=("parallel", "arbitrary"))[0]


def _mla_bwd_prep(dq, dk, dv, cosT, sinT):
    H, T, _ = dq.shape
    tm = _tile(T, ROW_TILE, 16)

    def body(dq_ref, dk_ref, dv_ref, cos_ref, sin_ref, dqr_ref, dkv_ref, dkpe_ref):
        h = pl.program_id(1)
        cs, sn = cos_ref[...], sin_ref[...]
        valid = _lane(cs.shape) < ROPE

        def unrope(d):
            d = jnp.where(valid, d, 0.0)
            return d * cs - _rope_rot(d) * sn

        dqv = dq_ref[...]
        dqr_ref[:, :NOPE] = dqv[:, :NOPE].astype(BF16)
        dqr_ref[:, NOPE:] = unrope(dqv[:, NOPE:]).astype(BF16)
        dkv_ = dk_ref[...]
        dkv_ref[:, :NOPE] = dkv_[:, :NOPE].astype(BF16)
        dkv_ref[:, NOPE:] = dv_ref[...].astype(BF16)
        _accumulate(dkpe_ref, unrope(dkv_[:, NOPE:]), h == 0)

    head = pl.BlockSpec((None, tm, ATT_DK), lambda i, h: (h, i, 0))
    tok = pl.BlockSpec((tm, LANES), lambda i, h: (i, 0))
    return _call(
        body, [dq, dk, dv, cosT, sinT], name="mla_bwd_prep", grid=(T // tm, H),
        in_specs=[head, head, pl.BlockSpec((None, tm, VDIM), lambda i, h: (h, i, 0)), tok, tok],
        out_specs=[head, head, tok],
        out_shape=[_sds((H, T, ATT_DK), BF16), _sds((H, T, ATT_DK), BF16), _sds((T, LANES), F32)],
        sem=("parallel", "arbitrary"))[0]


def _fox_bwd_prep(dq, dk, proj, bfor, lay, B, S, inv_scale):
    H, T, _ = dq.shape

    def body(dq_ref, dk_ref, fl_ref, bf_ref, dfl_ref, dbf_ref, dc_sc):
        b, h = pl.program_id(0), pl.program_id(1)
        lane = _lane(dc_sc.shape)
        col = jnp.sum(jnp.where(lane == 0, dq_ref[...], 0.0) - jnp.where(lane == 3, dk_ref[...], 0.0),
                      axis=1, keepdims=True)

        @pl.when(h == 0)
        def _():
            dc_sc[...] = jnp.zeros(dc_sc.shape, F32)

        dc_sc[...] = jnp.where(lane == h, col, dc_sc[...])

        @pl.when(h == H - 1)
        def _():
            dlogf = _cumsum_rows(dc_sc[...] * inv_scale, reverse=True)
            z = fl_ref[...] + bf_ref[...]
            dz = jnp.where(lane < H, dlogf * (1.0 / (1.0 + jnp.exp(z))), 0.0)
            dfl_ref[...] = dz
            _accumulate(dbf_ref, jnp.sum(dz, axis=0, keepdims=True), b == 0)

    aug = pl.BlockSpec((None, S, LANES), lambda b, h: (h, b, 1))
    seq = pl.BlockSpec((S, LANES), lambda b, h: (b, 0))
    vec = pl.BlockSpec((1, LANES), lambda b, h: (0, 0))
    return _call(
        body, [dq, dk, proj, bfor], name="fox_bwd_prep", grid=(B, H),
        in_specs=[aug, aug, pl.BlockSpec((S, LANES), lambda b, h: (b, lay["fl"] // LANES)), vec],
        out_specs=[seq, vec], out_shape=[_sds((T, LANES), F32), _sds((1, LANES), F32)],
        scratch_shapes=[pltpu.VMEM((S, LANES), F32)], sem=("arbitrary", "arbitrary"))[0]


def _heads_to_cols(dq, dk, dv):
    H, T, _ = dq.shape
    tm = _tile(T, ROW_TILE, 16)

    def body(a_ref, b_ref, c_ref, ao_ref, bo_ref, co_ref):
        ao_ref[...] = a_ref[...].astype(BF16)
        bo_ref[...] = b_ref[...].astype(BF16)
        co_ref[...] = c_ref[...].astype(BF16)

    src = pl.BlockSpec((None, tm, FOX_DIM), lambda i, h: (h, i, 0))
    dst = pl.BlockSpec((tm, FOX_DIM), lambda i, h: (i, h))
    return _call(body, [dq, dk, dv], name="heads_to_cols", grid=(T // tm, H), in_specs=[src, src, src],
                 out_specs=[dst, dst, dst], out_shape=[_sds((T, H * FOX_DIM), BF16)] * 3,
                 sem=("parallel", "parallel"))[0]


def _lat_bwd(dqn, dkvn, proj, gq, gkv, lay):
    T = dqn.shape[0]
    tm = _tile(T, ROW_TILE, 16)

    def body(dq_ref, dkv_ref, q_ref, kv_ref, gq_ref, gkv_ref, dql_ref, dkl_ref, dgq_ref, dgkv_ref):
        i = pl.program_id(0)
        dql, dgq = _rms_bwd(dq_ref[...], q_ref[...], gq_ref[...])
        dkl, dgkv = _rms_bwd(dkv_ref[...], kv_ref[...], gkv_ref[...])
        dql_ref[...] = dql.astype(BF16)
        dkl_ref[...] = dkl.astype(BF16)
        _accumulate(dgq_ref, jnp.sum(dgq, axis=0, keepdims=True), i == 0)
        _accumulate(dgkv_ref, jnp.sum(dgkv, axis=0, keepdims=True), i == 0)

    def blk(width, off=0):
        return pl.BlockSpec((tm, width), lambda i: (i, off // width))

    def vec(width):
        return pl.BlockSpec((1, width), lambda i: (0, 0))

    return _call(
        body, [dqn, dkvn, proj, proj, gq, gkv], name="lat_bwd", grid=(T // tm,),
        in_specs=[blk(Q_LORA), blk(KV_LORA), blk(Q_LORA, lay["q"]), blk(KV_LORA, lay["kv"]), vec(Q_LORA), vec(KV_LORA)],
        out_specs=[blk(Q_LORA), blk(KV_LORA), vec(Q_LORA), vec(KV_LORA)],
        out_shape=[_sds((T, Q_LORA), BF16), _sds((T, KV_LORA), BF16), _sds((1, Q_LORA), F32), _sds((1, KV_LORA), F32)],
        sem=("arbitrary",))[0]


def _final_dx(dx1, dh, x, g, comm=None):
    T, D = x.shape
    tm = _tile(T, ROW_TILE, 16)

    def body(dx1_ref, dh_ref, x_ref, g_ref, dx_ref, dg_ref):
        i = pl.program_id(0)
        d, dg = _rms_bwd(dh_ref[...], x_ref[...], g_ref[...])
        dx_ref[...] = dx1_ref[...] + d
        _accumulate(dg_ref, jnp.sum(dg, axis=0, keepdims=True), i == 0)

    row = pl.BlockSpec((tm, D), lambda i: (i, 0))
    vec = pl.BlockSpec((1, D), lambda i: (0, 0))
    return _call(body, [dx1, dh, x, g], name="final_dx", grid=(T // tm,), in_specs=[row, row, row, vec],
                 out_specs=[row, vec], out_shape=[_sds((T, D), F32), _sds((1, D), F32)], sem=("arbitrary",), comm=comm)


def _chip_sum(pieces, paired, qc, name):
    G, R, C = pieces.shape
    tr = _tile(R, 256, 16)

    def body(qc_ref, g_ref, p_ref, keep_ref, send_ref):
        s = pl.program_id(1)
        tot = g_ref[...] + p_ref[...]

        @pl.when(s == 0)
        def _():
            keep_ref[...] = tot

        @pl.when(s > 0)
        def _():
            send_ref[...] = tot.astype(send_ref.dtype)

    grid_spec = pltpu.PrefetchScalarGridSpec(
        num_scalar_prefetch=1, grid=(R // tr, N_CHIP),
        in_specs=[pl.BlockSpec((None, tr, C), lambda i, s, qc: (2 * (qc[0] ^ s) + qc[1], i, 0)),
                  pl.BlockSpec((None, tr, C), lambda i, s, qc: (qc[0] ^ s, i, 0))],
        out_specs=[pl.BlockSpec((tr, C), lambda i, s, qc: (i, 0)),
                   pl.BlockSpec((None, tr, C), lambda i, s, qc: (jnp.maximum(s - 1, 0), i, 0))])
    send_dtype = BF16 if R >= 16 else pieces.dtype
    return pl.pallas_call(
        body, name=name, grid_spec=grid_spec,
        out_shape=[_sds((R, C), F32), _sds((3, R, C), send_dtype)],
        compiler_params=pltpu.CompilerParams(dimension_semantics=("arbitrary", "arbitrary"),
                                             vmem_limit_bytes=VMEM_LIMIT_BYTES),
    )(qc, pieces, paired)


def _adamw_math(w, g, m, v):
    m = ADAM_B1 * m + (1.0 - ADAM_B1) * g
    v = ADAM_B2 * v + (1.0 - ADAM_B2) * (g * g)
    m_hat = m / (1.0 - ADAM_B1 ** ADAM_STEP)
    v_hat = v / (1.0 - ADAM_B2 ** ADAM_STEP)
    delta = -ADAM_LR * (m_hat / (jnp.sqrt(v_hat) + ADAM_EPS) + ADAM_WD * w)
    return delta, m, v


def _sum_adamw(keep, pieces, w, m, v, name):
    R, C = w.shape
    P = pieces.shape[0]
    tr = _tile(R, 256, 16)

    def body(*refs):
        if keep is None:
            p_ref, w_ref, m_ref, v_ref, g_ref, d_ref, mo_ref, vo_ref = refs
            g = p_ref[0].astype(F32)
            rest = range(1, P)
        else:
            k_ref, p_ref, w_ref, m_ref, v_ref, g_ref, d_ref, mo_ref, vo_ref = refs
            g = k_ref[...]
            rest = range(P)
        for q in rest:
            g = g + p_ref[q].astype(F32)
        g_ref[...] = g
        d_ref[...], mo_ref[...], vo_ref[...] = _adamw_math(w_ref[...], g, m_ref[...], v_ref[...])

    blk = pl.BlockSpec((tr, C), lambda i: (i, 0))
    pblk = pl.BlockSpec((P, tr, C), lambda i: (0, i, 0))
    args = [pieces, w, m, v] if keep is None else [keep, pieces, w, m, v]
    specs = [pblk, blk, blk, blk] if keep is None else [blk, pblk, blk, blk, blk]
    return _call(body, args, name=name, grid=(R // tr,), in_specs=specs, out_specs=[blk] * 4,
                 out_shape=[_sds((R, C), F32)] * 4, sem=("parallel",))[0]


def _layout(D):
    lay = {"q": 0, "kv": Q_LORA, "kpe": Q_LORA + KV_LORA}
    lay["fq"] = lay["kpe"] + LANES
    lay["fk"] = lay["fq"] + HEADS * FOX_DIM
    lay["fv"] = lay["fk"] + HEADS * FOX_DIM
    lay["fl"] = lay["fv"] + HEADS * FOX_DIM
    lay["g"] = lay["fl"] + LANES
    lay["end"] = lay["g"] + 2 * D
    return lay


def kernel(x, positions, pre_mix_norm, w_in, q_a_norm, w_uq, kv_a_norm, w_ukv, b_forget, b_gate, w_branch_mla, w_branch_fox, w_out, post_mix_norm, pre_ffn_norm, w_up, conv_w, conv_b, w_down, post_ffn_norm, loss_target, m_pre_mix_norm, m_w_in, m_q_a_norm, m_w_uq, m_kv_a_norm, m_w_ukv, m_b_forget, m_b_gate, m_w_branch_mla, m_w_branch_fox, m_w_out, m_post_mix_norm, m_pre_ffn_norm, m_w_up, m_conv_w, m_conv_b, m_w_down, m_post_ffn_norm, v_pre_mix_norm, v_w_in, v_q_a_norm, v_w_uq, v_kv_a_norm, v_w_ukv, v_b_forget, v_b_gate, v_w_branch_mla, v_w_branch_fox, v_w_out, v_post_mix_norm, v_pre_ffn_norm, v_w_up, v_conv_w, v_conv_b, v_w_down, v_post_ffn_norm):
    B, S, D = x.shape
    T = B * S
    F = conv_b.shape[0] // 2
    lay = _layout(D)
    n_in = w_in.shape[1]
    d_in = N_DEV * n_in
    seg_a = Q_LORA + KV_LORA + ROPE
    seg_b = 3 * HEADS * FOX_DIM + HEADS
    mla_scale = (NOPE + ROPE) ** -0.5
    fox_scale = FOX_DIM ** -0.5
    ax, ay, ac = (lax.axis_index(a) for a in MESH_AXES)
    qc = jnp.stack([2 * ax + ay, ac]).astype(jnp.int32)

    def row(vec, width=None):
        vec = vec.reshape(1, -1)
        if width is not None and vec.shape[1] < width:
            vec = jnp.pad(vec, ((0, 0), (0, width - vec.shape[1])))
        return vec

    win_s = _cast_bf16(w_in, "cast_w_in")
    (win_g,) = _exchange_alone(_Comm([_GatherPlan([win_s], mid_frac=0.0)]), "gather_w_in")
    small_s = [_cast_bf16(w, "cast_" + n) for w, n in
               [(w_uq, "w_uq"), (w_ukv, "w_ukv"), (w_branch_mla, "w_branch_mla"), (w_branch_fox, "w_branch_fox"), (w_out, "w_out")]]
    wup_s = _cast_bf16(w_up, "cast_w_up")
    wdown_s = _cast_bf16(w_down, "cast_w_down")

    win_full = jnp.transpose(win_g, (1, 0, 2)).reshape(D, d_in)
    w_perm = jnp.concatenate(
        [win_full[:, :seg_a], jnp.zeros((D, LANES - ROPE), BF16), win_full[:, seg_a:seg_a + seg_b],
         jnp.zeros((D, LANES - HEADS), BF16), win_full[:, seg_a + seg_b:]], axis=1)

    x2 = x.reshape(T, D)
    tgt = loss_target.reshape(T, D)
    pos = positions.reshape(T, 1)
    inv_freq = 1.0 / (ROPE_THETA ** (jnp.arange(0, ROPE, 2, dtype=F32) / ROPE))
    invf = row(jnp.concatenate([inv_freq, inv_freq]), LANES)
    g_pre, g_q, g_kv = row(pre_mix_norm), row(q_a_norm), row(kv_a_norm)
    g_pm, g_ffn, g_pf = row(post_mix_norm), row(pre_ffn_norm), row(post_ffn_norm)
    bfor = row(b_forget, LANES)
    bgate = row(b_gate)
    cb_full = row(conv_b)

    h = _prenorm(x2, g_pre)
    proj, (wuq_g, wukv_g, wbm_g, wbf_g, wout_g, cw_g) = _matmul(
        h, w_perm, mode="nn", name="mm_proj", comm=_Comm([_GatherPlan(small_s + [conv_w])]))
    wuq_pad = jnp.pad(wuq_g, ((0, 0), (0, 0), (0, ATT_DK - NOPE - ROPE)))
    wbm = jnp.transpose(wbm_g, (1, 0, 2)).reshape(HEADS * VDIM, D)
    wbf = jnp.transpose(wbf_g, (1, 0, 2)).reshape(HEADS * FOX_DIM, D)
    wout = wout_g.reshape(D, D)
    cw_full = jnp.transpose(cw_g, (1, 0, 2)).reshape(3, 2 * F)

    qn, kvn, kper, logf, cosT, sinT = _split_prep(proj, pos, invf, g_q, g_kv, bfor, lay)
    qraw = _matmul(qn, wuq_pad, mode="nn", name="mm_q", out_blocks=ATT_DK)
    kvraw = _matmul(kvn, wukv_g, mode="nn", name="mm_kv", out_blocks=NOPE + VDIM)
    q_mla, k_mla, v_mla = _mla_prep(qraw, kvraw, kper, cosT, sinT)
    o_mla, lse_mla, (wup_g,) = _attn_fwd(q_mla, k_mla, v_mla, B=B, S=S, unit=MLA_UNIT, scale=mla_scale,
                                         name="attn_mla_fwd", comm=_Comm([_GatherPlan([wup_s], mid_frac=0.75)]))
    cs = _fox_cumsum(logf, B, S, 1.0 / fox_scale)
    q_fox, k_fox, v_fox = _fox_prep(proj, cs, lay)
    o_fox, lse_fox, (wdown_g,) = _attn_fwd(q_fox, k_fox, v_fox, B=B, S=S, unit=1, scale=fox_scale,
                                           name="attn_fox_fwd", comm=_Comm([_GatherPlan([wdown_s], mid_frac=0.6)]))
    wdown = wdown_g.reshape(F, D)
    n_up = wup_g.shape[2]
    a_m = _matmul(o_mla, wbm, mode="nn", name="mm_branch_mla")
    a_f = _matmul(o_fox, wbf, mode="nn", name="mm_branch_fox")
    merged = _gate_merge(a_m, a_f, proj, bgate, lay, D)
    y1 = _matmul(merged, wout, mode="nn", name="mm_out")
    x1, h2 = _mid(x2, y1, g_pm, g_ffn)
    u = _matmul(h2, wup_g, mode="nn", name="mm_up", tn=n_up)
    act = _convffn_fwd(u, cw_full, cb_full, B, S, F)
    ff = _matmul(act, wdown, mode="nn", name="mm_down", tk=F // 2)
    dy, dff, loss_part, dg_pf = _tail(ff, x1, tgt, g_pf)

    def pair_plan(gs):
        return _Comm([_PairScatterPlan(gs)])

    def chip_plan(gs):
        return _Comm([_ChipScatterPlan(gs)])

    dact = _matmul(dff, wdown, mode="nt", name="mm_dact")
    dw_down = _matmul(act, dff, mode="tn", name="mm_dw_down", tm=512).reshape(N_DEV, F // N_DEV, D)
    (du_g, du_v, dcp_g, dcp_v), (pa_down,) = _convffn_bwd(u, dact, cw_full, cb_full, B, S, F, comm=pair_plan([dw_down]))
    keep_down, sb_down = _chip_sum(dw_down, pa_down, qc, "chipsum_w_down")
    du = jnp.concatenate([du_g, du_v], axis=1)
    dh2, (rb_down,) = _matmul(du, wup_g, mode="nt", name="mm_dh2", comm=chip_plan([sb_down]))
    dw_up = _matmul(h2, du, mode="tn", name="mm_dw_up", out_blocks=n_up, tm=512, tn=n_up)
    (dx1, dy1, dg_ffn, dg_pm), (pa_up,) = _mid_bwd(dy, dh2, x1, y1, g_ffn, g_pm, comm=pair_plan([dw_up]))
    keep_up, sb_up = _chip_sum(dw_up, pa_up, qc, "chipsum_w_up")
    dmerged = _matmul(dy1, wout, mode="nt", name="mm_dmerged")
    dw_out = _matmul(merged, dy1, mode="tn", name="mm_dw_out").reshape(N_DEV, D // N_DEV, D)
    da_m, da_f, dgl_m, dgl_f, dbg_m, dbg_f = _gate_bwd(dmerged, a_m, a_f, proj, bgate, lay, D)
    do_mla = _matmul(da_m, wbm, mode="nt", name="mm_do_mla", out_dtype=BF16)
    do_fox = _matmul(da_f, wbf, mode="nt", name="mm_do_fox", out_dtype=BF16)
    dw_bm = _matmul(o_mla, da_m, mode="tn", name="mm_dw_branch_mla", out_blocks=D // N_DEV)
    dw_bf = _matmul(o_fox, da_f, mode="tn", name="mm_dw_branch_fox", out_blocks=D // N_DEV)
    mix = [dw_out, dw_bm, dw_bf]
    dq_m, dk_m, dv_m, landed = _attn_bwd(
        q_mla, k_mla, v_mla, o_mla, do_mla, lse_mla, B=B, S=S, unit=MLA_UNIT, scale=mla_scale, name="attn_mla_bwd",
        comm=_Comm([_ChipScatterPlan([sb_up]), _PairScatterPlan(mix)]))
    rb_up, pa_mix = landed[0], landed[1:]
    mix_sums = [_chip_sum(g, p, qc, "chipsum_" + n) for g, p, n in zip(mix, pa_mix, ["w_out", "w_branch_mla", "w_branch_fox"])]
    dq_f, dk_f, dv_f, rb_mix = _attn_bwd(
        q_fox, k_fox, v_fox, o_fox, do_fox, lse_fox, B=B, S=S, unit=1, scale=fox_scale, name="attn_fox_bwd",
        comm=chip_plan([s[1] for s in mix_sums]))
    dqraw, dkvraw, dkpe = _mla_bwd_prep(dq_m, dk_m, dv_m, cosT, sinT)
    dqn = _matmul(dqraw, wuq_pad, mode="nt", name="mm_dqn")
    dw_uq = _matmul(qn, dqraw, mode="tn", name="mm_dw_uq", out_blocks=ATT_DK)[:, :, :NOPE + ROPE]
    dkvn = _matmul(dkvraw, wukv_g, mode="nt", name="mm_dkvn")
    dw_ukv = _matmul(kvn, dkvraw, mode="tn", name="mm_dw_ukv", out_blocks=NOPE + VDIM)
    dqlat, dkvlat, dg_q, dg_kv = _lat_bwd(dqn, dkvn, proj, g_q, g_kv, lay)
    dfl, dbfor = _fox_bwd_prep(dq_f, dk_f, proj, bfor, lay, B, S, 1.0 / fox_scale)
    dfq, dfk, dfv = _heads_to_cols(dq_f, dk_f, dv_f)
    dproj = jnp.concatenate([dqlat, dkvlat, dkpe.astype(BF16), dfq, dfk, dfv, dfl.astype(BF16), dgl_m, dgl_f], axis=1)
    dw_perm = _matmul(h, dproj, mode="tn", name="mm_dw_in")
    dw_in_full = jnp.concatenate(
        [dw_perm[:, :seg_a], dw_perm[:, lay["fq"]:lay["fq"] + seg_b], dw_perm[:, lay["g"]:]], axis=1)
    dw_in = jnp.transpose(dw_in_full.reshape(D, N_DEV, n_in), (1, 0, 2))
    dcw = jnp.transpose(jnp.concatenate([dcp_g[0:3], dcp_v[0:3]], axis=1).reshape(3, N_DEV, (2 * F) // N_DEV), (1, 0, 2))
    late = [dw_in, dw_uq, dw_ukv, dcw]
    dh, pa_late = _matmul(dproj, w_perm, mode="nt", name="mm_dh", tk=2048, comm=pair_plan(late))
    late_sums = [_chip_sum(g, p, qc, "chipsum_" + n) for g, p, n in zip(late, pa_late, ["w_in", "w_uq", "w_ukv", "conv_w"])]
    (grad_x, dg_pre), rb_late = _final_dx(dx1, dh, x2, g_pre, comm=chip_plan([s[1] for s in late_sums]))

    big_out = {}

    def finish(n, keep, pieces, w, m, v):
        big_out[n] = _sum_adamw(keep, pieces, w, m, v, "adamw_" + n)

    finish("w_down", keep_down, rb_down, w_down, m_w_down, v_w_down)
    finish("w_up", keep_up, rb_up, w_up, m_w_up, v_w_up)
    finish("w_out", mix_sums[0][0], rb_mix[0], w_out, m_w_out, v_w_out)
    finish("w_branch_mla", mix_sums[1][0], rb_mix[1], w_branch_mla, m_w_branch_mla, v_w_branch_mla)
    finish("w_branch_fox", mix_sums[2][0], rb_mix[2], w_branch_fox, m_w_branch_fox, v_w_branch_fox)
    finish("w_in", late_sums[0][0], rb_late[0], w_in, m_w_in, v_w_in)
    finish("w_uq", late_sums[1][0], rb_late[1], w_uq, m_w_uq, v_w_uq)
    finish("w_ukv", late_sums[2][0], rb_late[2], w_ukv, m_w_ukv, v_w_ukv)
    finish("conv_w", late_sums[3][0], rb_late[3], conv_w, m_conv_w, v_conv_w)

    widths = [D, Q_LORA, KV_LORA, LANES, 2 * D, D, D, 2 * F, D]
    small_names = ["pre_mix_norm", "q_a_norm", "kv_a_norm", "b_forget", "b_gate", "post_mix_norm", "pre_ffn_norm",
                   "conv_b", "post_ffn_norm"]
    true_w = [D, Q_LORA, KV_LORA, HEADS, 2 * D, D, D, 2 * F, D]
    dcb = jnp.concatenate([dcp_g[3:4], dcp_v[3:4]], axis=1)
    part = jnp.concatenate([dg_pre, dg_q, dg_kv, dbfor, dbg_m, dbg_f, dg_pm, dg_ffn, dcb, dg_pf], axis=1)

    def pack(vals):
        return jnp.concatenate([row(a, wd) for a, wd in zip(vals, widths)], axis=1)

    sw = pack([pre_mix_norm, q_a_norm, kv_a_norm, b_forget, b_gate, post_mix_norm, pre_ffn_norm, conv_b, post_ffn_norm])
    sm = pack([m_pre_mix_norm, m_q_a_norm, m_kv_a_norm, m_b_forget, m_b_gate, m_post_mix_norm, m_pre_ffn_norm,
               m_conv_b, m_post_ffn_norm])
    sv = pack([v_pre_mix_norm, v_q_a_norm, v_kv_a_norm, v_b_forget, v_b_gate, v_post_mix_norm, v_pre_ffn_norm,
               v_conv_b, v_post_ffn_norm])
    (parts_all,) = _exchange_alone(_Comm([_DirectGatherPlan([part])]), "gather_small")
    sg, sd, smo, svo = _sum_adamw(None, parts_all, sw, sm, sv, "adamw_small")
    small_out = {}
    off = 0
    for n, wd, tw in zip(small_names, widths, true_w):
        small_out[n] = tuple(a[0, off:off + tw] for a in (sg, sd, smo, svo))
        off += wd

    loss = lax.psum(loss_part[0, 0], MESH_AXES)
    order = ["pre_mix_norm", "w_in", "q_a_norm", "w_uq", "kv_a_norm", "w_ukv", "b_forget", "b_gate", "w_branch_mla",
             "w_branch_fox", "w_out", "post_mix_norm", "pre_ffn_norm", "w_up", "conv_w", "conv_b", "w_down",
             "post_ffn_norm"]
    res = {**big_out, **small_out}
    outs = [loss, grad_x.reshape(B, S, D)]
    for kind in range(4):
        outs += [res[n][kind] for n in order]
    return tuple(outs)
```

```python
import math

import jax
import jax.numpy as jnp
from jax import lax
from jax.experimental import pallas as pl
from jax.experimental.pallas import tpu as pltpu

F32 = jnp.float32
BF16 = jnp.bfloat16

N_DEV = 8
N_CHIP = 4
HEADS = 8
NOPE = 128
ROPE = 64
HALF_ROPE = ROPE // 2
VDIM = 128
Q_LORA = 512
KV_LORA = 256
FOX_DIM = 128
ATT_DK = 256
MLA_UNIT = 64
ROPE_THETA = 10000.0
EPS = 1e-6
NEG_INF = -1e30
LANES = 128
LOG2_E = 1.4426950408889634

ADAM_LR = 0.001
ADAM_B1 = 0.9
ADAM_B2 = 0.999
ADAM_EPS = 1e-08
ADAM_WD = 0.01
ADAM_STEP = 10

VMEM_LIMIT_BYTES = 56 * 1024 * 1024
ROW_TILE = 256
ATT_TILE = 512
ATT_SUB = 256
ATT_AHEAD = 3
MM_TILE = 1024

MESH_AXES = ("x", "y", "c")
ANY = pl.BlockSpec(memory_space=pl.ANY)


def _tile(n, pref, align=LANES):
    if n <= pref:
        return n
    t = (pref // align) * align
    while t >= align:
        if n % t == 0:
            return t
        t -= align
    return n


def _sds(shape, dtype):
    return jax.ShapeDtypeStruct(shape, dtype)


def _coords():
    x, y, c = (lax.axis_index(ax) for ax in MESH_AXES)
    return x, y, c


def _chip_rel(x, y, r):
    return (1 - x if r & 2 else x), (1 - y if r & 1 else y)


def _rcopy(src, dst, sems, w, k, dev):
    return pltpu.make_async_remote_copy(src_ref=src, dst_ref=dst, send_sem=sems[0].at[w, k], recv_sem=sems[1].at[w, k],
                                        device_id=dev, device_id_type=pl.DeviceIdType.MESH)


class _GatherPlan:
    def __init__(self, blocks, mid_frac=0.5):
        self.ins = list(blocks)
        self.out_shapes = [_sds((N_DEV,) + b.shape, b.dtype) for b in blocks]
        n = len(blocks)
        self.scratch = [pltpu.SemaphoreType.DMA((n, 7)), pltpu.SemaphoreType.DMA((n, 7)), pltpu.SemaphoreType.DMA((n,))]
        self.mid_frac = mid_frac

    def first(self, ins, outs, sems):
        x, y, c = _coords()
        me = 4 * x + 2 * y + c
        for w in range(len(ins)):
            pltpu.make_async_copy(ins[w], outs[w].at[me], sems[2].at[w]).start()
            _rcopy(ins[w], outs[w].at[me], sems, w, 0, (x, y, 1 - c)).start()
            for r in (1, 2, 3):
                px, py = _chip_rel(x, y, r)
                _rcopy(ins[w], outs[w].at[me], sems, w, r, (px, py, c)).start()

    def mid(self, ins, outs, sems):
        x, y, c = _coords()
        for w in range(len(ins)):
            for r in (1, 2, 3):
                px, py = _chip_rel(x, y, r)
                blk = outs[w].at[4 * px + 2 * py + c]
                _rcopy(ins[w], blk, sems, w, r, (px, py, c)).wait_recv()
                _rcopy(blk, blk, sems, w, 3 + r, (x, y, 1 - c)).start()

    def last(self, ins, outs, sems):
        x, y, c = _coords()
        me = 4 * x + 2 * y + c
        sib = (x, y, 1 - c)
        for w in range(len(ins)):
            _rcopy(ins[w], outs[w].at[4 * x + 2 * y + 1 - c], sems, w, 0, sib).wait_recv()
            for r in (1, 2, 3):
                px, py = _chip_rel(x, y, r)
                blk = outs[w].at[4 * px + 2 * py + 1 - c]
                _rcopy(blk, blk, sems, w, 3 + r, sib).wait_recv()
            for k in range(7):
                _rcopy(ins[w], outs[w].at[me], sems, w, k, sib).wait_send()
            pltpu.make_async_copy(ins[w], outs[w].at[me], sems[2].at[w]).wait()


class _DirectGatherPlan:
    mid = None

    def __init__(self, blocks):
        self.ins = list(blocks)
        self.out_shapes = [_sds((N_DEV,) + b.shape, b.dtype) for b in blocks]
        n = len(blocks)
        self.scratch = [pltpu.SemaphoreType.DMA((n, 7)), pltpu.SemaphoreType.DMA((n, 7)), pltpu.SemaphoreType.DMA((n,))]

    @staticmethod
    def _peer(x, y, c, r):
        return (1 - x if r & 4 else x), (1 - y if r & 2 else y), (1 - c if r & 1 else c)

    def first(self, ins, outs, sems):
        x, y, c = _coords()
        me = 4 * x + 2 * y + c
        for w in range(len(ins)):
            pltpu.make_async_copy(ins[w], outs[w].at[me], sems[2].at[w]).start()
            for r in range(1, N_DEV):
                _rcopy(ins[w], outs[w].at[me], sems, w, r - 1, self._peer(x, y, c, r)).start()

    def last(self, ins, outs, sems):
        x, y, c = _coords()
        me = 4 * x + 2 * y + c
        for w in range(len(ins)):
            for r in range(1, N_DEV):
                px, py, pc = self._peer(x, y, c, r)
                cp = _rcopy(ins[w], outs[w].at[4 * px + 2 * py + pc], sems, w, r - 1, (px, py, pc))
                cp.wait_recv()
                cp.wait_send()
            pltpu.make_async_copy(ins[w], outs[w].at[me], sems[2].at[w]).wait()


class _PairScatterPlan:
    mid = None

    def __init__(self, pieces):
        self.ins = list(pieces)
        self.out_shapes = [_sds((N_CHIP,) + p.shape[1:], p.dtype) for p in pieces]
        n = len(pieces)
        self.scratch = [pltpu.SemaphoreType.DMA((n, N_CHIP)), pltpu.SemaphoreType.DMA((n, N_CHIP))]

    def _copies(self, ins, outs, sems):
        x, y, c = _coords()
        return [_rcopy(ins[w].at[2 * q + 1 - c], outs[w].at[q], sems, w, q, (x, y, 1 - c))
                for w in range(len(ins)) for q in range(N_CHIP)]

    def first(self, ins, outs, sems):
        for cp in self._copies(ins, outs, sems):
            cp.start()

    def last(self, ins, outs, sems):
        for cp in self._copies(ins, outs, sems):
            cp.wait_recv()
            cp.wait_send()


class _ChipScatterPlan:
    mid = None

    def __init__(self, sums):
        self.ins = list(sums)
        self.out_shapes = [_sds(s.shape, s.dtype) for s in sums]
        n = len(sums)
        self.scratch = [pltpu.SemaphoreType.DMA((n, 3)), pltpu.SemaphoreType.DMA((n, 3))]

    def _copies(self, ins, outs, sems):
        x, y, c = _coords()
        cps = []
        for w in range(len(ins)):
            for r in (1, 2, 3):
                px, py = _chip_rel(x, y, r)
                cps.append(_rcopy(ins[w].at[r - 1], outs[w].at[r - 1], sems, w, r - 1, (px, py, c)))
        return cps

    def first(self, ins, outs, sems):
        for cp in self._copies(ins, outs, sems):
            cp.start()

    def last(self, ins, outs, sems):
        for cp in self._copies(ins, outs, sems):
            cp.wait_recv()
            cp.wait_send()


class _Comm:
    def __init__(self, plans):
        self.plans = list(plans)
        self.ins = [a for p in self.plans for a in p.ins]
        self.out_shapes = [s for p in self.plans for s in p.out_shapes]
        self.scratch = [s for p in self.plans for s in p.scratch]

    def _parts(self, ins, outs, sems):
        i = o = s = 0
        for p in self.plans:
            yield p, ins[i:i + len(p.ins)], outs[o:o + len(p.out_shapes)], sems[s:s + len(p.scratch)]
            i, o, s = i + len(p.ins), o + len(p.out_shapes), s + len(p.scratch)

    def begin(self, step, nsteps, ins, outs, sems):
        @pl.when(step == 0)
        def _():
            for p, pi, po, ps in self._parts(ins, outs, sems):
                p.first(pi, po, ps)

        for p, pi, po, ps in self._parts(ins, outs, sems):
            if p.mid is not None:
                @pl.when(step == min(nsteps - 1, int(p.mid_frac * nsteps)))
                def _(p=p, pi=pi, po=po, ps=ps):
                    p.mid(pi, po, ps)

    def end(self, step, nsteps, ins, outs, sems):
        @pl.when(step == nsteps - 1)
        def _():
            for p, pi, po, ps in self._parts(ins, outs, sems):
                p.last(pi, po, ps)


def _call(body, args, *, name, grid, in_specs, out_specs, out_shape, scratch_shapes=(), sem=None, comm=None):
    in_specs, out_specs, out_shape, scratch_shapes = list(in_specs), list(out_specs), list(out_shape), list(scratch_shapes)
    if comm is None:
        res = pl.pallas_call(
            body, name=name, grid=grid, in_specs=in_specs, out_specs=out_specs, out_shape=out_shape,
            scratch_shapes=scratch_shapes,
            compiler_params=pltpu.CompilerParams(dimension_semantics=sem, vmem_limit_bytes=VMEM_LIMIT_BYTES),
        )(*args)
        return list(res), []
    n_in, n_out, n_sc = len(in_specs), len(out_specs), len(scratch_shapes)
    n_ci, n_co = len(comm.ins), len(comm.out_shapes)
    nsteps = math.prod(grid)

    def hosted(*refs):
        ins, cins = refs[:n_in], refs[n_in:n_in + n_ci]
        o0 = n_in + n_ci
        outs, couts = refs[o0:o0 + n_out], refs[o0 + n_out:o0 + n_out + n_co]
        s0 = o0 + n_out + n_co
        scr, csems = refs[s0:s0 + n_sc], refs[s0 + n_sc:]
        step = jnp.int32(0)
        for d in range(len(grid)):
            step = step * grid[d] + pl.program_id(d)
        comm.begin(step, nsteps, cins, couts, csems)
        body(*ins, *outs, *scr)
        comm.end(step, nsteps, cins, couts, csems)

    res = pl.pallas_call(
        hosted, name=name, grid=grid, in_specs=in_specs + [ANY] * n_ci, out_specs=out_specs + [ANY] * n_co,
        out_shape=out_shape + comm.out_shapes, scratch_shapes=scratch_shapes + comm.scratch,
        compiler_params=pltpu.CompilerParams(dimension_semantics=("arbitrary",) * len(grid),
                                             vmem_limit_bytes=VMEM_LIMIT_BYTES, has_side_effects=True),
    )(*args, *comm.ins)
    return list(res[:n_out]), list(res[n_out:])


def _exchange_alone(comm, name):
    def body():
        pass

    return _call(body, [], name=name, grid=(), in_specs=[], out_specs=[], out_shape=[], comm=comm)[1]


def _matmul(a, b, *, mode, name, out_dtype=F32, out_blocks=None, tm=None, tn=None, tk=None, comm=None):
    tm = MM_TILE if tm is None else tm
    tn = MM_TILE if tn is None else tn
    a_blk = a.ndim == 3
    b_blk = b.ndim == 3
    if mode == "nn":
        M, K = a.shape
        N = b.shape[0] * b.shape[2] if b_blk else b.shape[1]
        dims = (((1,), (0,)), ((), ()))
    elif mode == "nt":
        M = a.shape[1] if a_blk else a.shape[0]
        K = a.shape[0] * a.shape[2] if a_blk else a.shape[1]
        N = b.shape[1] if b_blk else b.shape[0]
        dims = (((1,), (1,)), ((), ()))
    else:
        K, M = a.shape
        N = b.shape[0] * b.shape[2] if b_blk else b.shape[1]
        dims = (((0,), (0,)), ((), ()))

    tm = _tile(M, tm)
    tn = _tile(N, tn)
    if mode == "nt" and (a_blk or b_blk):
        tk = a.shape[2] if a_blk else b.shape[2]
    else:
        tk = _tile(K, K if tk is None else tk)
    if mode != "nt" and b_blk:
        tn = _tile(b.shape[2], tn)
    if out_blocks is not None:
        tn = _tile(out_blocks, tn)
    nk = K // tk
    grid = (M // tm, N // tn, nk)

    if mode == "nn":
        a_spec = pl.BlockSpec((tm, tk), lambda i, j, k: (i, k))
        if b_blk:
            rb = b.shape[2] // tn
            b_spec = pl.BlockSpec((None, tk, tn), lambda i, j, k: (j // rb, k, j % rb))
        else:
            b_spec = pl.BlockSpec((tk, tn), lambda i, j, k: (k, j))
    elif mode == "nt":
        if a_blk:
            a_spec = pl.BlockSpec((None, tm, tk), lambda i, j, k: (k, i, 0))
        else:
            a_spec = pl.BlockSpec((tm, tk), lambda i, j, k: (i, k))
        if b_blk:
            b_spec = pl.BlockSpec((None, tn, tk), lambda i, j, k: (k, j, 0))
        else:
            b_spec = pl.BlockSpec((tn, tk), lambda i, j, k: (j, k))
    else:
        a_spec = pl.BlockSpec((tk, tm), lambda i, j, k: (k, i))
        if b_blk:
            rb = b.shape[2] // tn
            b_spec = pl.BlockSpec((None, tk, tn), lambda i, j, k: (j // rb, k, j % rb))
        else:
            b_spec = pl.BlockSpec((tk, tn), lambda i, j, k: (k, j))

    if out_blocks is None:
        o_spec = pl.BlockSpec((tm, tn), lambda i, j, k: (i, j))
        o_shape = _sds((M, N), out_dtype)
    else:
        ro = out_blocks // tn
        o_spec = pl.BlockSpec((None, tm, tn), lambda i, j, k: (j // ro, i, j % ro))
        o_shape = _sds((N // out_blocks, M, out_blocks), out_dtype)

    def body(a_ref, b_ref, o_ref, *scratch):
        part = lax.dot_general(a_ref[...], b_ref[...], dims, preferred_element_type=F32)
        if nk == 1:
            o_ref[...] = part.astype(o_ref.dtype)
        else:
            acc_ref = scratch[0]
            k = pl.program_id(2)

            @pl.when(k == 0)
            def _():
                acc_ref[...] = part

            @pl.when(k > 0)
            def _():
                acc_ref[...] += part

            @pl.when(k == nk - 1)
            def _():
                o_ref[...] = acc_ref[...].astype(o_ref.dtype)

    scratch = [] if nk == 1 else [pltpu.VMEM((tm, tn), F32)]
    outs, landed = _call(body, [a, b], name=name, grid=grid, in_specs=[a_spec, b_spec], out_specs=[o_spec],
                         out_shape=[o_shape], scratch_shapes=scratch, sem=("parallel", "parallel", "arbitrary"), comm=comm)
    return outs[0] if comm is None else (outs[0], landed)


def _rms(x):
    return lax.rsqrt(jnp.mean(x * x, axis=-1, keepdims=True) + EPS)


def _rms_bwd(dy, x, g):
    r = _rms(x)
    n = x * r
    dn = dy * g
    dx = r * (dn - n * jnp.mean(dn * n, axis=-1, keepdims=True))
    return dx, dy * n


def _sigmoid(x):
    return 1.0 / (1.0 + jnp.exp(-x))


def _rope_rot(t):
    return pltpu.roll(t, HALF_ROPE, 1) - pltpu.roll(t, LANES - HALF_ROPE, 1)


def _lane(shape):
    return lax.broadcasted_iota(jnp.int32, shape, 1)


def _split3(x):
    hi = x.astype(BF16).astype(F32)
    r1 = x - hi
    mid = r1.astype(BF16).astype(F32)
    lo = (r1 - mid).astype(BF16).astype(F32)
    return hi, mid, lo


def _cumsum_rows(x, reverse):
    S = x.shape[0]
    bs = min(256, S)
    nb = S // bs
    r = lax.broadcasted_iota(jnp.int32, (bs, bs), 0)
    c = lax.broadcasted_iota(jnp.int32, (bs, bs), 1)
    tri = jnp.where((c >= r) if reverse else (c <= r), 1.0, 0.0).astype(BF16)
    edge = lax.broadcasted_iota(jnp.int32, (bs, x.shape[1]), 0) == (0 if reverse else bs - 1)
    carry = jnp.zeros((1, x.shape[1]), F32)
    outs = [None] * nb
    for bi in (range(nb - 1, -1, -1) if reverse else range(nb)):
        xb = x[bi * bs:(bi + 1) * bs, :]
        acc = carry
        for term in _split3(xb):
            acc = acc + jnp.dot(tri, term.astype(BF16), preferred_element_type=F32)
        outs[bi] = acc
        carry = jnp.sum(jnp.where(edge, acc, 0.0), axis=0, keepdims=True)
    return jnp.concatenate(outs, axis=0) if nb > 1 else outs[0]


def _gelu_parts(x):
    c0 = math.sqrt(2.0 / math.pi)
    inner = c0 * (x + 0.044715 * (x * x * x))
    t = jnp.tanh(inner)
    g = 0.5 * x * (1.0 + t)
    dg = 0.5 * (1.0 + t) + 0.5 * x * (1.0 - t * t) * (c0 * (1.0 + 3.0 * 0.044715 * (x * x)))
    return g, dg


def _accumulate(ref, value, first):
    @pl.when(first)
    def _():
        ref[...] = value

    @pl.when(jnp.logical_not(first))
    def _():
        ref[...] += value


def _cast_bf16(w, name):
    R, C = w.shape
    tr = _tile(R, 512, 16)

    def body(w_ref, o_ref):
        o_ref[...] = w_ref[...].astype(BF16)

    blk = pl.BlockSpec((tr, C), lambda i: (i, 0))
    return _call(body, [w], name=name, grid=(R // tr,), in_specs=[blk], out_specs=[blk],
                 out_shape=[_sds((R, C), BF16)], sem=("parallel",))[0][0]


def _prenorm(x, g):
    T, D = x.shape
    tm = _tile(T, ROW_TILE, 16)

    def body(x_ref, g_ref, h_ref):
        xv = x_ref[...]
        h_ref[...] = (xv * _rms(xv) * g_ref[...]).astype(BF16)

    row = pl.BlockSpec((tm, D), lambda i: (i, 0))
    return _call(body, [x, g], name="prenorm", grid=(T // tm,),
                 in_specs=[row, pl.BlockSpec((1, D), lambda i: (0, 0))], out_specs=[row],
                 out_shape=[_sds((T, D), BF16)], sem=("parallel",))[0][0]


def _split_prep(proj, pos, invf, gq, gkv, bfor, lay):
    T = proj.shape[0]
    tm = _tile(T, ROW_TILE, 16)

    def body(q_ref, kv_ref, kpe_ref, fl_ref, pos_ref, invf_ref, gq_ref, gkv_ref, bf_ref,
             qn_ref, kvn_ref, kper_ref, logf_ref, cos_ref, sin_ref):
        ql = q_ref[...]
        qn_ref[...] = (ql * _rms(ql) * gq_ref[...]).astype(BF16)
        kl = kv_ref[...]
        kvn_ref[...] = (kl * _rms(kl) * gkv_ref[...]).astype(BF16)
        ang = pos_ref[...].astype(F32) * invf_ref[...]
        valid = _lane(ang.shape) < ROPE
        cs = jnp.where(valid, jnp.cos(ang), 0.0)
        sn = jnp.where(valid, jnp.sin(ang), 0.0)
        cos_ref[...] = cs
        sin_ref[...] = sn
        kp = jnp.where(valid, kpe_ref[...], 0.0)
        kper_ref[...] = (kp * cs + _rope_rot(kp) * sn).astype(BF16)
        z = fl_ref[...] + bf_ref[...]
        logf_ref[...] = jnp.minimum(z, 0.0) - jnp.log(1.0 + jnp.exp(-jnp.abs(z)))

    def col(width, off):
        return pl.BlockSpec((tm, width), lambda i: (i, off // width))

    def vec(width):
        return pl.BlockSpec((1, width), lambda i: (0, 0))

    def out(width):
        return pl.BlockSpec((tm, width), lambda i: (i, 0))

    return _call(
        body, [proj, proj, proj, proj, pos, invf, gq, gkv, bfor], name="split_prep", grid=(T // tm,),
        in_specs=[col(Q_LORA, lay["q"]), col(KV_LORA, lay["kv"]), col(LANES, lay["kpe"]), col(LANES, lay["fl"]),
                  pl.BlockSpec((tm, 1), lambda i: (i, 0)), vec(LANES), vec(Q_LORA), vec(KV_LORA), vec(LANES)],
        out_specs=[out(Q_LORA), out(KV_LORA), out(LANES), out(LANES), out(LANES), out(LANES)],
        out_shape=[_sds((T, Q_LORA), BF16), _sds((T, KV_LORA), BF16), _sds((T, LANES), BF16),
                   _sds((T, LANES), F32), _sds((T, LANES), F32), _sds((T, LANES), F32)],
        sem=("parallel",))[0]


def _mla_prep(qraw, kvraw, kper, cosT, sinT):
    H, T, _ = qraw.shape
    tm = _tile(T, ROW_TILE, 16)

    def body(q_ref, kv_ref, kpe_ref, cos_ref, sin_ref, qo_ref, ko_ref, vo_ref):
        q = q_ref[...]
        pe = q[:, NOPE:]
        pe = jnp.where(_lane(pe.shape) < ROPE, pe, 0.0)
        qo_ref[:, :NOPE] = q[:, :NOPE].astype(BF16)
        qo_ref[:, NOPE:] = (pe * cos_ref[...] + _rope_rot(pe) * sin_ref[...]).astype(BF16)
        kv = kv_ref[...]
        ko_ref[:, :NOPE] = kv[:, :NOPE].astype(BF16)
        ko_ref[:, NOPE:] = kpe_ref[...]
        vo_ref[...] = kv[:, NOPE:].astype(BF16)

    head = pl.BlockSpec((None, tm, ATT_DK), lambda h, i: (h, i, 0))
    tok = pl.BlockSpec((tm, LANES), lambda h, i: (i, 0))
    return _call(
        body, [qraw, kvraw, kper, cosT, sinT], name="mla_prep", grid=(H, T // tm),
        in_specs=[head, head, tok, tok, tok],
        out_specs=[head, head, pl.BlockSpec((None, tm, VDIM), lambda h, i: (h, i, 0))],
        out_shape=[_sds((H, T, ATT_DK), BF16), _sds((H, T, ATT_DK), BF16), _sds((H, T, VDIM), BF16)],
        sem=("parallel", "parallel"))[0]


def _fox_cumsum(logf, B, S, inv_scale):
    T = logf.shape[0]

    def body(l_ref, c_ref):
        c_ref[...] = _cumsum_rows(l_ref[...], reverse=False) * inv_scale

    seq = pl.BlockSpec((S, LANES), lambda b: (b, 0))
    return _call(body, [logf], name="fox_cumsum", grid=(B,), in_specs=[seq], out_specs=[seq],
                 out_shape=[_sds((T, LANES), F32)], sem=("parallel",))[0][0]


def _fox_prep(proj, cs, lay):
    T = proj.shape[0]
    tm = _tile(T, ROW_TILE, 16)

    def body(q_ref, k_ref, v_ref, cs_ref, qo_ref, ko_ref, vo_ref):
        h = pl.program_id(0)
        cv = cs_ref[...]
        lane = _lane(cv.shape)
        ccol = jnp.sum(jnp.where(lane == h, cv, 0.0), axis=1, keepdims=True)
        hi, mid, lo = _split3(ccol)
        one = jnp.where(lane < 6, 1.0, 0.0)
        augq = jnp.where(lane == 0, hi, jnp.where(lane == 1, mid, jnp.where(lane == 2, lo, one)))
        augk = jnp.where(lane < 3, 1.0, jnp.where(lane == 3, -hi, jnp.where(lane == 4, -mid, jnp.where(lane == 5, -lo, 0.0))))
        qo_ref[:, :FOX_DIM] = q_ref[...].astype(BF16)
        qo_ref[:, FOX_DIM:] = augq.astype(BF16)
        ko_ref[:, :FOX_DIM] = k_ref[...].astype(BF16)
        ko_ref[:, FOX_DIM:] = augk.astype(BF16)
        vo_ref[...] = v_ref[...].astype(BF16)

    def col(off):
        return pl.BlockSpec((tm, FOX_DIM), lambda h, i: (i, off // FOX_DIM + h))

    head = pl.BlockSpec((None, tm, ATT_DK), lambda h, i: (h, i, 0))
    return _call(
        body, [proj, proj, proj, cs], name="fox_prep", grid=(HEADS, T // tm),
        in_specs=[col(lay["fq"]), col(lay["fk"]), col(lay["fv"]), pl.BlockSpec((tm, LANES), lambda h, i: (i, 0))],
        out_specs=[head, head, pl.BlockSpec((None, tm, VDIM), lambda h, i: (h, i, 0))],
        out_shape=[_sds((HEADS, T, ATT_DK), BF16), _sds((HEADS, T, ATT_DK), BF16), _sds((HEADS, T, VDIM), BF16)],
        sem=("parallel", "parallel"))[0]


def _visible(tq, tk, unit):
    r = lax.broadcasted_iota(jnp.int32, (tq, tk), 0)
    c = lax.broadcasted_iota(jnp.int32, (tq, tk), 1)
    sh = int(math.log2(unit))
    return lax.shift_right_logical(c, sh) <= lax.shift_right_logical(r, sh)


def _attn_fwd(streams, *, B, S, name, comm=None):
    n = len(streams)
    H, T, DK = streams[0][0].shape
    DV = streams[0][2].shape[2]
    tq = _tile(S, ATT_TILE)
    nq = S // tq
    sub = min(ATT_SUB, tq)
    NT = (((1,), (1,)), ((), ()))

    def body(*refs):
        ins, outs, (m_sc, acc_sc) = refs[:3 * n], refs[3 * n:5 * n], refs[5 * n:]
        i, j = pl.program_id(1), pl.program_id(2)

        @pl.when(j == 0)
        def _():
            m_sc[...] = jnp.full(m_sc.shape, NEG_INF, F32)
            acc_sc[...] = jnp.zeros(acc_sc.shape, F32)

        def step(diagonal):
            work = [(t, r) for r in range(tq // sub) for t in range(n)]

            def scores(t, r):
                q_ref, k_ref, _ = ins[3 * t:3 * t + 3]
                kc = (r + 1) * sub if diagonal else tq
                s = lax.dot_general(q_ref[r * sub:(r + 1) * sub, :], k_ref[0:kc, :], NT, preferred_element_type=F32)
                return s * (streams[t][4] * LOG2_E)

            ahead = [scores(*work[w]) for w in range(min(ATT_AHEAD, len(work)))]
            for w, (t, r) in enumerate(work):
                s = ahead.pop(0)
                if w + ATT_AHEAD < len(work):
                    ahead.append(scores(*work[w + ATT_AHEAD]))
                v_ref = ins[3 * t + 2]
                kc = s.shape[1]
                rows = slice(r * sub, (r + 1) * sub)
                if diagonal:
                    own = jnp.where(_visible(sub, sub, streams[t][3]), s[:, kc - sub:], NEG_INF)
                    s = own if kc == sub else jnp.concatenate([s[:, :kc - sub], own], axis=1)
                m_prev = m_sc[t, rows, :]
                mx = s[:, 0:LANES]
                for g in range(1, kc // LANES):
                    mx = jnp.maximum(mx, s[:, g * LANES:(g + 1) * LANES])
                m_new = jnp.maximum(m_prev, jnp.max(mx, axis=1, keepdims=True))
                alpha = jnp.exp2(m_prev - m_new)
                p = jnp.exp2(s - jnp.tile(m_new, (1, kc // LANES))).astype(BF16)
                v_aug = jnp.concatenate([v_ref[0:kc, :], jnp.ones((kc, LANES), BF16)], axis=1)
                acc_sc[t, rows, :] = jnp.tile(alpha, (1, 2)) * acc_sc[t, rows, :] + jnp.dot(
                    p, v_aug, preferred_element_type=F32)
                m_sc[t, rows, :] = m_new

        @pl.when(j < i)
        def _():
            step(False)

        @pl.when(j == i)
        def _():
            step(True)
            for t in range(n):
                o_ref, lse_ref = outs[2 * t:2 * t + 2]
                l = acc_sc[t, :, DV:]
                o_ref[...] = (acc_sc[t, :, :DV] / l).astype(BF16)
                lse_ref[...] = m_sc[t] + jnp.log2(l)

    def qmap(g, i, j):
        return (g % H, (g // H) * nq + i, 0)

    def kmap(g, i, j):
        return (g % H, (g // H) * nq + jnp.minimum(j, i), 0)

    args = [a for st in streams for a in st[:3]]
    outs, landed = _call(
        body, args, name=name, grid=(B * H, nq, nq),
        in_specs=[pl.BlockSpec((None, tq, DK), qmap), pl.BlockSpec((None, tq, DK), kmap),
                  pl.BlockSpec((None, tq, DV), kmap)] * n,
        out_specs=[pl.BlockSpec((tq, DV), lambda g, i, j: ((g // H) * nq + i, g % H)),
                   pl.BlockSpec((None, tq, LANES), qmap)] * n,
        out_shape=[_sds((T, H * DV), BF16), _sds((H, T, LANES), F32)] * n,
        scratch_shapes=[pltpu.VMEM((n, tq, LANES), F32), pltpu.VMEM((n, tq, DV + LANES), F32)],
        sem=("parallel", "parallel", "arbitrary"), comm=comm)
    return [(outs[2 * t], outs[2 * t + 1]) for t in range(n)], landed


def _attn_bwd(streams, *, B, S, name, comm=None):
    n = len(streams)
    H, T, DK = streams[0][0].shape
    DV = streams[0][2].shape[2]
    tq = _tile(S, ATT_TILE)
    nq = S // tq
    NT = (((1,), (1,)), ((), ()))
    TN = (((0,), (0,)), ((), ()))

    def body(*refs):
        ins, outs = refs[:6 * n], refs[6 * n:]
        j, i = pl.program_id(1), pl.program_id(2)

        @pl.when(jnp.logical_and(j == 0, i == 0))
        def _():
            for t in range(n):
                outs[3 * t][...] = jnp.zeros(outs[3 * t].shape, F32)

        @pl.when(i == 0)
        def _():
            for t in range(n):
                outs[3 * t + 1][...] = jnp.zeros(outs[3 * t + 1].shape, F32)
                outs[3 * t + 2][...] = jnp.zeros(outs[3 * t + 2].shape, F32)

        def step(diagonal):
            for t, st in enumerate(streams):
                unit, scale = st[6], st[7]
                q_ref, k_ref, v_ref, o_ref, do_ref, lse_ref = ins[6 * t:6 * t + 6]
                dq_ref, dk_ref, dv_ref = outs[3 * t:3 * t + 3]
                qv, kv_, dov = q_ref[...], k_ref[...], do_ref[...]
                s = lax.dot_general(qv, kv_, NT, preferred_element_type=F32) * (scale * LOG2_E)
                if diagonal:
                    s = jnp.where(_visible(tq, tq, unit), s, NEG_INF)
                p = jnp.exp2(s - jnp.tile(lse_ref[...], (1, tq // LANES)))
                dp = lax.dot_general(dov, v_ref[...], NT, preferred_element_type=F32)
                delta = jnp.sum(dov.astype(F32) * o_ref[...].astype(F32), axis=1, keepdims=True)
                ds = (p * (dp - delta) * scale).astype(BF16)
                dv_ref[...] += lax.dot_general(p.astype(BF16), dov, TN, preferred_element_type=F32)
                dk_ref[...] += lax.dot_general(ds, qv, TN, preferred_element_type=F32)
                rows = pl.ds(pl.multiple_of(i * tq, tq), tq)
                dq_ref[rows, :] += jnp.dot(ds, kv_, preferred_element_type=F32)

        @pl.when(i > j)
        def _():
            step(False)

        @pl.when(i == j)
        def _():
            step(True)

    def qmap(g, j, i):
        return (g % H, (g // H) * nq + jnp.maximum(i, j), 0)

    def kmap(g, j, i):
        return (g % H, (g // H) * nq + j, 0)

    def omap(g, j, i):
        return ((g // H) * nq + jnp.maximum(i, j), g % H)

    args = [a for st in streams for a in st[:6]]
    outs, landed = _call(
        body, args, name=name, grid=(B * H, nq, nq),
        in_specs=[pl.BlockSpec((None, tq, DK), qmap), pl.BlockSpec((None, tq, DK), kmap),
                  pl.BlockSpec((None, tq, DV), kmap), pl.BlockSpec((tq, DV), omap), pl.BlockSpec((tq, DV), omap),
                  pl.BlockSpec((None, tq, LANES), qmap)] * n,
        out_specs=[pl.BlockSpec((None, S, DK), lambda g, j, i: (g % H, g // H, 0)),
                   pl.BlockSpec((None, tq, DK), kmap), pl.BlockSpec((None, tq, DV), kmap)] * n,
        out_shape=[_sds((H, T, DK), F32), _sds((H, T, DK), F32), _sds((H, T, DV), F32)] * n,
        sem=("parallel", "arbitrary", "arbitrary"), comm=comm)
    return [tuple(outs[3 * t:3 * t + 3]) for t in range(n)], landed


def _gate_merge(am, af, proj, bgate, lay, D):
    T = am.shape[0]
    tm = _tile(T, ROW_TILE, 16)
    tn = _tile(D, 1024)

    def body(am_ref, af_ref, gm_ref, gf_ref, bm_ref, bf_ref, o_ref):
        sm = _sigmoid(gm_ref[...] + bm_ref[...])
        sf = _sigmoid(gf_ref[...] + bf_ref[...])
        o_ref[...] = (sm * am_ref[...] + sf * af_ref[...]).astype(BF16)

    og = lay["g"] // tn
    blk = pl.BlockSpec((tm, tn), lambda i, j: (i, j))
    return _call(
        body, [am, af, proj, proj, bgate, bgate], name="gate_merge", grid=(T // tm, D // tn),
        in_specs=[blk, blk, pl.BlockSpec((tm, tn), lambda i, j: (i, og + j)),
                  pl.BlockSpec((tm, tn), lambda i, j: (i, og + D // tn + j)),
                  pl.BlockSpec((1, tn), lambda i, j: (0, j)), pl.BlockSpec((1, tn), lambda i, j: (0, D // tn + j))],
        out_specs=[blk], out_shape=[_sds((T, D), BF16)], sem=("parallel", "parallel"))[0][0]


def _mid(x, y1, g_pm, g_ffn):
    T, D = x.shape
    tm = _tile(T, ROW_TILE, 16)

    def body(x_ref, y_ref, gp_ref, gf_ref, x1_ref, h2_ref):
        y = y_ref[...]
        x1 = x_ref[...] + y * _rms(y) * gp_ref[...]
        x1_ref[...] = x1
        h2_ref[...] = (x1 * _rms(x1) * gf_ref[...]).astype(BF16)

    row = pl.BlockSpec((tm, D), lambda i: (i, 0))
    vec = pl.BlockSpec((1, D), lambda i: (0, 0))
    return _call(body, [x, y1, g_pm, g_ffn], name="mid", grid=(T // tm,), in_specs=[row, row, vec, vec],
                 out_specs=[row, row], out_shape=[_sds((T, D), F32), _sds((T, D), BF16)], sem=("parallel",))[0]


def _conv3(u, w_ref, bias):
    row = lax.broadcasted_iota(jnp.int32, u.shape, 0)
    u1 = jnp.where(row >= 1, pltpu.roll(u, 1, 0), 0.0)
    u2 = jnp.where(row >= 2, pltpu.roll(u, 2, 0), 0.0)
    return w_ref[0:1, :] * u2 + w_ref[1:2, :] * u1 + w_ref[2:3, :] * u + bias, u1, u2


def _convffn_fwd(u, cw, cb, B, S, F):
    T = u.shape[0]
    tn = _tile(F, 256)
    nf = F // tn

    def body(ug_ref, uv_ref, wg_ref, wv_ref, bg_ref, bv_ref, a_ref):
        g, _, _ = _conv3(ug_ref[...], wg_ref, bg_ref[...])
        val, _, _ = _conv3(uv_ref[...], wv_ref, bv_ref[...])
        a_ref[...] = (_gelu_parts(g)[0] * val).astype(BF16)

    def seq(off):
        return pl.BlockSpec((S, tn), lambda b, j: (b, off + j))

    def par(rows, off):
        return pl.BlockSpec((rows, tn), lambda b, j: (0, off + j))

    return _call(body, [u, u, cw, cw, cb, cb], name="convffn_fwd", grid=(B, nf),
                 in_specs=[seq(0), seq(nf), par(3, 0), par(3, nf), par(1, 0), par(1, nf)],
                 out_specs=[seq(0)], out_shape=[_sds((T, F), BF16)], sem=("parallel", "parallel"))[0][0]


def _convffn_bwd(u, dact, cw, cb, B, S, F, comm=None):
    T = u.shape[0]
    tn = _tile(F, 256)
    nf = F // tn

    def body(ug_ref, uv_ref, da_ref, wg_ref, wv_ref, bg_ref, bv_ref, dug_ref, duv_ref, dpg_ref, dpv_ref):
        b = pl.program_id(1)
        ug, uv, da = ug_ref[...], uv_ref[...], da_ref[...]
        g, ug1, ug2 = _conv3(ug, wg_ref, bg_ref[...])
        val, uv1, uv2 = _conv3(uv, wv_ref, bv_ref[...])
        gel, dgel = _gelu_parts(g)
        dg = da * val * dgel
        dval = da * gel
        row = lax.broadcasted_iota(jnp.int32, ug.shape, 0)

        def back(d, w_ref):
            d1 = jnp.where(row < S - 1, pltpu.roll(d, S - 1, 0), 0.0)
            d2 = jnp.where(row < S - 2, pltpu.roll(d, S - 2, 0), 0.0)
            return w_ref[2:3, :] * d + w_ref[1:2, :] * d1 + w_ref[0:1, :] * d2

        dug_ref[...] = back(dg, wg_ref).astype(BF16)
        duv_ref[...] = back(dval, wv_ref).astype(BF16)

        def sums(d, u0, u1, u2):
            r8 = lax.broadcasted_iota(jnp.int32, (8, d.shape[1]), 0)
            out = jnp.zeros((8, d.shape[1]), F32)
            for k, t in enumerate((d * u2, d * u1, d * u0, d)):
                out = jnp.where(r8 == k, jnp.sum(t, axis=0, keepdims=True), out)
            return out

        _accumulate(dpg_ref, sums(dg, ug, ug1, ug2), b == 0)
        _accumulate(dpv_ref, sums(dval, uv, uv1, uv2), b == 0)

    def seq(off):
        return pl.BlockSpec((S, tn), lambda j, b: (b, off + j))

    def par(rows, off):
        return pl.BlockSpec((rows, tn), lambda j, b: (0, off + j))

    outs, landed = _call(
        body, [u, u, dact, cw, cw, cb, cb], name="convffn_bwd", grid=(nf, B),
        in_specs=[seq(0), seq(nf), seq(0), par(3, 0), par(3, nf), par(1, 0), par(1, nf)],
        out_specs=[seq(0), seq(0), par(8, 0), par(8, 0)],
        out_shape=[_sds((T, F), BF16), _sds((T, F), BF16), _sds((8, F), F32), _sds((8, F), F32)],
        sem=("parallel", "arbitrary"), comm=comm)
    return outs, landed


def _tail(ff, x1, tgt, g):
    T, D = ff.shape
    tm = _tile(T, ROW_TILE, 16)

    def body(ff_ref, x1_ref, t_ref, g_ref, dy_ref, dff_ref, loss_ref, dg_ref):
        i = pl.program_id(0)
        f = ff_ref[...]
        gv = g_ref[...]
        r = _rms(f)
        n = f * r
        e = (x1_ref[...] + n * gv) - t_ref[...]
        dy = e * (1.0 / D)
        dy_ref[...] = dy
        dn = dy * gv
        dff_ref[...] = (r * (dn - n * jnp.mean(dn * n, axis=-1, keepdims=True))).astype(BF16)
        part = 0.5 * jnp.sum(jnp.mean(e * e, axis=-1, keepdims=True), axis=0, keepdims=True)
        _accumulate(loss_ref, jnp.broadcast_to(part, loss_ref.shape), i == 0)
        _accumulate(dg_ref, jnp.sum(dy * n, axis=0, keepdims=True), i == 0)

    row = pl.BlockSpec((tm, D), lambda i: (i, 0))
    vec = pl.BlockSpec((1, D), lambda i: (0, 0))
    return _call(body, [ff, x1, tgt, g], name="tail", grid=(T // tm,), in_specs=[row, row, row, vec],
                 out_specs=[row, row, pl.BlockSpec((8, LANES), lambda i: (0, 0)), vec],
                 out_shape=[_sds((T, D), F32), _sds((T, D), BF16), _sds((8, LANES), F32), _sds((1, D), F32)],
                 sem=("arbitrary",))[0]


def _mid_bwd(dy, dh2, x1, y1, g_ffn, g_pm, comm=None):
    T, D = dy.shape
    tm = _tile(T, ROW_TILE, 16)

    def body(dy_ref, dh_ref, x1_ref, y1_ref, gf_ref, gp_ref, dx1_ref, dy1_ref, dgf_ref, dgp_ref):
        i = pl.program_id(0)
        dh = dh_ref[...]
        d2, dgf = _rms_bwd(dh, x1_ref[...], gf_ref[...])
        dx1 = dy_ref[...] + d2
        dx1_ref[...] = dx1
        d1, dgp = _rms_bwd(dx1, y1_ref[...], gp_ref[...])
        dy1_ref[...] = d1.astype(BF16)
        _accumulate(dgf_ref, jnp.sum(dgf, axis=0, keepdims=True), i == 0)
        _accumulate(dgp_ref, jnp.sum(dgp, axis=0, keepdims=True), i == 0)

    row = pl.BlockSpec((tm, D), lambda i: (i, 0))
    vec = pl.BlockSpec((1, D), lambda i: (0, 0))
    return _call(body, [dy, dh2, x1, y1, g_ffn, g_pm], name="mid_bwd", grid=(T // tm,),
                 in_specs=[row, row, row, row, vec, vec], out_specs=[row, row, vec, vec],
                 out_shape=[_sds((T, D), F32), _sds((T, D), BF16), _sds((1, D), F32), _sds((1, D), F32)],
                 sem=("arbitrary",), comm=comm)


def _gate_bwd(dm, am, af, proj, bgate, lay, D):
    T = dm.shape[0]
    tm = _tile(T, ROW_TILE, 16)
    tn = _tile(D, 512)

    def body(dm_ref, am_ref, af_ref, gm_ref, gf_ref, bm_ref, bf_ref,
             dam_ref, daf_ref, dgm_ref, dgf_ref, dbm_ref, dbf_ref):
        i = pl.program_id(1)
        d = dm_ref[...]
        sm = _sigmoid(gm_ref[...] + bm_ref[...])
        sf = _sigmoid(gf_ref[...] + bf_ref[...])
        dam_ref[...] = (d * sm).astype(BF16)
        daf_ref[...] = (d * sf).astype(BF16)
        dgm = d * am_ref[...] * (sm * (1.0 - sm))
        dgf = d * af_ref[...] * (sf * (1.0 - sf))
        dgm_ref[...] = dgm.astype(BF16)
        dgf_ref[...] = dgf.astype(BF16)
        _accumulate(dbm_ref, jnp.sum(dgm, axis=0, keepdims=True), i == 0)
        _accumulate(dbf_ref, jnp.sum(dgf, axis=0, keepdims=True), i == 0)

    og = lay["g"] // tn
    blk = pl.BlockSpec((tm, tn), lambda j, i: (i, j))
    vec = pl.BlockSpec((1, tn), lambda j, i: (0, j))
    return _call(
        body, [dm, am, af, proj, proj, bgate, bgate], name="gate_bwd", grid=(D // tn, T // tm),
        in_specs=[blk, blk, blk, pl.BlockSpec((tm, tn), lambda j, i: (i, og + j)),
                  pl.BlockSpec((tm, tn), lambda j, i: (i, og + D // tn + j)),
                  vec, pl.BlockSpec((1, tn), lambda j, i: (0, D // tn + j))],
        out_specs=[blk, blk, blk, blk, vec, vec],
        out_shape=[_sds((T, D), BF16)] * 4 + [_sds((1, D), F32)] * 2, sem=("parallel", "arbitrary"))[0]


def _mla_bwd_prep(dq, dk, dv, cosT, sinT):
    H, T, _ = dq.shape
    tm = _tile(T, ROW_TILE, 16)

    def body(dq_ref, dk_ref, dv_ref, cos_ref, sin_ref, dqr_ref, dkv_ref, dkpe_ref):
        h = pl.program_id(1)
        cs, sn = cos_ref[...], sin_ref[...]
        valid = _lane(cs.shape) < ROPE

        def unrope(d):
            d = jnp.where(valid, d, 0.0)
            return d * cs - _rope_rot(d) * sn

        dqv = dq_ref[...]
        dqr_ref[:, :NOPE] = dqv[:, :NOPE].astype(BF16)
        dqr_ref[:, NOPE:] = unrope(dqv[:, NOPE:]).astype(BF16)
        dkv_ = dk_ref[...]
        dkv_ref[:, :NOPE] = dkv_[:, :NOPE].astype(BF16)
        dkv_ref[:, NOPE:] = dv_ref[...].astype(BF16)
        _accumulate(dkpe_ref, unrope(dkv_[:, NOPE:]), h == 0)

    head = pl.BlockSpec((None, tm, ATT_DK), lambda i, h: (h, i, 0))
    tok = pl.BlockSpec((tm, LANES), lambda i, h: (i, 0))
    return _call(
        body, [dq, dk, dv, cosT, sinT], name="mla_bwd_prep", grid=(T // tm, H),
        in_specs=[head, head, pl.BlockSpec((None, tm, VDIM), lambda i, h: (h, i, 0)), tok, tok],
        out_specs=[head, head, tok],
        out_shape=[_sds((H, T, ATT_DK), BF16), _sds((H, T, ATT_DK), BF16), _sds((T, LANES), F32)],
        sem=("parallel", "arbitrary"))[0]


def _fox_bwd_prep(dq, dk, proj, bfor, lay, B, S, inv_scale):
    H, T, _ = dq.shape

    def body(dq_ref, dk_ref, fl_ref, bf_ref, dfl_ref, dbf_ref, dc_sc):
        b, h = pl.program_id(0), pl.program_id(1)
        lane = _lane(dc_sc.shape)
        col = jnp.sum(jnp.where(lane == 0, dq_ref[...], 0.0) - jnp.where(lane == 3, dk_ref[...], 0.0),
                      axis=1, keepdims=True)

        @pl.when(h == 0)
        def _():
            dc_sc[...] = jnp.zeros(dc_sc.shape, F32)

        dc_sc[...] = jnp.where(lane == h, col, dc_sc[...])

        @pl.when(h == H - 1)
        def _():
            dlogf = _cumsum_rows(dc_sc[...] * inv_scale, reverse=True)
            z = fl_ref[...] + bf_ref[...]
            dz = jnp.where(lane < H, dlogf * (1.0 / (1.0 + jnp.exp(z))), 0.0)
            dfl_ref[...] = dz
            _accumulate(dbf_ref, jnp.sum(dz, axis=0, keepdims=True), b == 0)

    aug = pl.BlockSpec((None, S, LANES), lambda b, h: (h, b, 1))
    seq = pl.BlockSpec((S, LANES), lambda b, h: (b, 0))
    vec = pl.BlockSpec((1, LANES), lambda b, h: (0, 0))
    return _call(
        body, [dq, dk, proj, bfor], name="fox_bwd_prep", grid=(B, H),
        in_specs=[aug, aug, pl.BlockSpec((S, LANES), lambda b, h: (b, lay["fl"] // LANES)), vec],
        out_specs=[seq, vec], out_shape=[_sds((T, LANES), F32), _sds((1, LANES), F32)],
        scratch_shapes=[pltpu.VMEM((S, LANES), F32)], sem=("arbitrary", "arbitrary"))[0]


def _heads_to_cols(dq, dk, dv):
    H, T, _ = dq.shape
    tm = _tile(T, ROW_TILE, 16)

    def body(a_ref, b_ref, c_ref, ao_ref, bo_ref, co_ref):
        ao_ref[...] = a_ref[...].astype(BF16)
        bo_ref[...] = b_ref[...].astype(BF16)
        co_ref[...] = c_ref[...].astype(BF16)

    src = pl.BlockSpec((None, tm, FOX_DIM), lambda i, h: (h, i, 0))
    dst = pl.BlockSpec((tm, FOX_DIM), lambda i, h: (i, h))
    return _call(body, [dq, dk, dv], name="heads_to_cols", grid=(T // tm, H), in_specs=[src, src, src],
                 out_specs=[dst, dst, dst], out_shape=[_sds((T, H * FOX_DIM), BF16)] * 3,
                 sem=("parallel", "parallel"))[0]


def _lat_bwd(dqn, dkvn, proj, gq, gkv, lay):
    T = dqn.shape[0]
    tm = _tile(T, ROW_TILE, 16)

    def body(dq_ref, dkv_ref, q_ref, kv_ref, gq_ref, gkv_ref, dql_ref, dkl_ref, dgq_ref, dgkv_ref):
        i = pl.program_id(0)
        dql, dgq = _rms_bwd(dq_ref[...], q_ref[...], gq_ref[...])
        dkl, dgkv = _rms_bwd(dkv_ref[...], kv_ref[...], gkv_ref[...])
        dql_ref[...] = dql.astype(BF16)
        dkl_ref[...] = dkl.astype(BF16)
        _accumulate(dgq_ref, jnp.sum(dgq, axis=0, keepdims=True), i == 0)
        _accumulate(dgkv_ref, jnp.sum(dgkv, axis=0, keepdims=True), i == 0)

    def blk(width, off=0):
        return pl.BlockSpec((tm, width), lambda i: (i, off // width))

    def vec(width):
        return pl.BlockSpec((1, width), lambda i: (0, 0))

    return _call(
        body, [dqn, dkvn, proj, proj, gq, gkv], name="lat_bwd", grid=(T // tm,),
        in_specs=[blk(Q_LORA), blk(KV_LORA), blk(Q_LORA, lay["q"]), blk(KV_LORA, lay["kv"]), vec(Q_LORA), vec(KV_LORA)],
        out_specs=[blk(Q_LORA), blk(KV_LORA), vec(Q_LORA), vec(KV_LORA)],
        out_shape=[_sds((T, Q_LORA), BF16), _sds((T, KV_LORA), BF16), _sds((1, Q_LORA), F32), _sds((1, KV_LORA), F32)],
        sem=("arbitrary",))[0]


def _final_dx(dx1, dh, x, g, comm=None):
    T, D = x.shape
    tm = _tile(T, ROW_TILE, 16)

    def body(dx1_ref, dh_ref, x_ref, g_ref, dx_ref, dg_ref):
        i = pl.program_id(0)
        d, dg = _rms_bwd(dh_ref[...], x_ref[...], g_ref[...])
        dx_ref[...] = dx1_ref[...] + d
        _accumulate(dg_ref, jnp.sum(dg, axis=0, keepdims=True), i == 0)

    row = pl.BlockSpec((tm, D), lambda i: (i, 0))
    vec = pl.BlockSpec((1, D), lambda i: (0, 0))
    return _call(body, [dx1, dh, x, g], name="final_dx", grid=(T // tm,), in_specs=[row, row, row, vec],
                 out_specs=[row, vec], out_shape=[_sds((T, D), F32), _sds((1, D), F32)], sem=("arbitrary",), comm=comm)


def _chip_sum(pieces, paired, qc, name):
    G, R, C = pieces.shape
    tr = _tile(R, 256, 16)

    def body(qc_ref, g_ref, p_ref, keep_ref, send_ref):
        s = pl.program_id(1)
        tot = g_ref[...] + p_ref[...]

        @pl.when(s == 0)
        def _():
            keep_ref[...] = tot

        @pl.when(s > 0)
        def _():
            send_ref[...] = tot.astype(send_ref.dtype)

    grid_spec = pltpu.PrefetchScalarGridSpec(
        num_scalar_prefetch=1, grid=(R // tr, N_CHIP),
        in_specs=[pl.BlockSpec((None, tr, C), lambda i, s, qc: (2 * (qc[0] ^ s) + qc[1], i, 0)),
                  pl.BlockSpec((None, tr, C), lambda i, s, qc: (qc[0] ^ s, i, 0))],
        out_specs=[pl.BlockSpec((tr, C), lambda i, s, qc: (i, 0)),
                   pl.BlockSpec((None, tr, C), lambda i, s, qc: (jnp.maximum(s - 1, 0), i, 0))])
    send_dtype = BF16 if R >= 16 else pieces.dtype
    return pl.pallas_call(
        body, name=name, grid_spec=grid_spec,
        out_shape=[_sds((R, C), F32), _sds((3, R, C), send_dtype)],
        compiler_params=pltpu.CompilerParams(dimension_semantics=("arbitrary", "arbitrary"),
                                             vmem_limit_bytes=VMEM_LIMIT_BYTES),
    )(qc, pieces, paired)


def _adamw_math(w, g, m, v):
    m = ADAM_B1 * m + (1.0 - ADAM_B1) * g
    v = ADAM_B2 * v + (1.0 - ADAM_B2) * (g * g)
    m_hat = m / (1.0 - ADAM_B1 ** ADAM_STEP)
    v_hat = v / (1.0 - ADAM_B2 ** ADAM_STEP)
    delta = -ADAM_LR * (m_hat / (jnp.sqrt(v_hat) + ADAM_EPS) + ADAM_WD * w)
    return delta, m, v


def _sum_adamw(keep, pieces, w, m, v, name):
    R, C = w.shape
    P = pieces.shape[0]
    tr = _tile(R, 256, 16)

    def body(*refs):
        if keep is None:
            p_ref, w_ref, m_ref, v_ref, g_ref, d_ref, mo_ref, vo_ref = refs
            g = p_ref[0].astype(F32)
            rest = range(1, P)
        else:
            k_ref, p_ref, w_ref, m_ref, v_ref, g_ref, d_ref, mo_ref, vo_ref = refs
            g = k_ref[...]
            rest = range(P)
        for q in rest:
            g = g + p_ref[q].astype(F32)
        g_ref[...] = g
        d_ref[...], mo_ref[...], vo_ref[...] = _adamw_math(w_ref[...], g, m_ref[...], v_ref[...])

    blk = pl.BlockSpec((tr, C), lambda i: (i, 0))
    pblk = pl.BlockSpec((P, tr, C), lambda i: (0, i, 0))
    args = [pieces, w, m, v] if keep is None else [keep, pieces, w, m, v]
    specs = [pblk, blk, blk, blk] if keep is None else [blk, pblk, blk, blk, blk]
    return _call(body, args, name=name, grid=(R // tr,), in_specs=specs, out_specs=[blk] * 4,
                 out_shape=[_sds((R, C), F32)] * 4, sem=("parallel",))[0]


def _layout(D):
    lay = {"q": 0, "kv": Q_LORA, "kpe": Q_LORA + KV_LORA}
    lay["fq"] = lay["kpe"] + LANES
    lay["fk"] = lay["fq"] + HEADS * FOX_DIM
    lay["fv"] = lay["fk"] + HEADS * FOX_DIM
    lay["fl"] = lay["fv"] + HEADS * FOX_DIM
    lay["g"] = lay["fl"] + LANES
    lay["end"] = lay["g"] + 2 * D
    return lay


def kernel(x, positions, pre_mix_norm, w_in, q_a_norm, w_uq, kv_a_norm, w_ukv, b_forget, b_gate, w_branch_mla, w_branch_fox, w_out, post_mix_norm, pre_ffn_norm, w_up, conv_w, conv_b, w_down, post_ffn_norm, loss_target, m_pre_mix_norm, m_w_in, m_q_a_norm, m_w_uq, m_kv_a_norm, m_w_ukv, m_b_forget, m_b_gate, m_w_branch_mla, m_w_branch_fox, m_w_out, m_post_mix_norm, m_pre_ffn_norm, m_w_up, m_conv_w, m_conv_b, m_w_down, m_post_ffn_norm, v_pre_mix_norm, v_w_in, v_q_a_norm, v_w_uq, v_kv_a_norm, v_w_ukv, v_b_forget, v_b_gate, v_w_branch_mla, v_w_branch_fox, v_w_out, v_post_mix_norm, v_pre_ffn_norm, v_w_up, v_conv_w, v_conv_b, v_w_down, v_post_ffn_norm):
    B, S, D = x.shape
    T = B * S
    F = conv_b.shape[0] // 2
    lay = _layout(D)
    n_in = w_in.shape[1]
    d_in = N_DEV * n_in
    seg_a = Q_LORA + KV_LORA + ROPE
    seg_b = 3 * HEADS * FOX_DIM + HEADS
    mla_scale = (NOPE + ROPE) ** -0.5
    fox_scale = FOX_DIM ** -0.5
    ax, ay, ac = (lax.axis_index(a) for a in MESH_AXES)
    qc = jnp.stack([2 * ax + ay, ac]).astype(jnp.int32)

    def row(vec, width=None):
        vec = vec.reshape(1, -1)
        if width is not None and vec.shape[1] < width:
            vec = jnp.pad(vec, ((0, 0), (0, width - vec.shape[1])))
        return vec

    win_s = _cast_bf16(w_in, "cast_w_in")
    (win_g,) = _exchange_alone(_Comm([_GatherPlan([win_s], mid_frac=0.0)]), "gather_w_in")
    small_s = [_cast_bf16(w, "cast_" + n) for w, n in
               [(w_uq, "w_uq"), (w_ukv, "w_ukv"), (w_branch_mla, "w_branch_mla"), (w_branch_fox, "w_branch_fox"), (w_out, "w_out")]]
    wup_s = _cast_bf16(w_up, "cast_w_up")
    wdown_s = _cast_bf16(w_down, "cast_w_down")

    win_full = jnp.transpose(win_g, (1, 0, 2)).reshape(D, d_in)
    w_perm = jnp.concatenate(
        [win_full[:, :seg_a], jnp.zeros((D, LANES - ROPE), BF16), win_full[:, seg_a:seg_a + seg_b],
         jnp.zeros((D, LANES - HEADS), BF16), win_full[:, seg_a + seg_b:]], axis=1)

    x2 = x.reshape(T, D)
    tgt = loss_target.reshape(T, D)
    pos = positions.reshape(T, 1)
    inv_freq = 1.0 / (ROPE_THETA ** (jnp.arange(0, ROPE, 2, dtype=F32) / ROPE))
    invf = row(jnp.concatenate([inv_freq, inv_freq]), LANES)
    g_pre, g_q, g_kv = row(pre_mix_norm), row(q_a_norm), row(kv_a_norm)
    g_pm, g_ffn, g_pf = row(post_mix_norm), row(pre_ffn_norm), row(post_ffn_norm)
    bfor = row(b_forget, LANES)
    bgate = row(b_gate)
    cb_full = row(conv_b)

    h = _prenorm(x2, g_pre)
    proj, (wuq_g, wukv_g, wbm_g, wbf_g, wout_g, cw_g) = _matmul(
        h, w_perm, mode="nn", name="mm_proj", comm=_Comm([_GatherPlan(small_s + [conv_w])]))
    wuq_pad = jnp.pad(wuq_g, ((0, 0), (0, 0), (0, ATT_DK - NOPE - ROPE)))
    wbm = jnp.transpose(wbm_g, (1, 0, 2)).reshape(HEADS * VDIM, D)
    wbf = jnp.transpose(wbf_g, (1, 0, 2)).reshape(HEADS * FOX_DIM, D)
    wout = wout_g.reshape(D, D)
    cw_full = jnp.transpose(cw_g, (1, 0, 2)).reshape(3, 2 * F)

    qn, kvn, kper, logf, cosT, sinT = _split_prep(proj, pos, invf, g_q, g_kv, bfor, lay)
    qraw = _matmul(qn, wuq_pad, mode="nn", name="mm_q", out_blocks=ATT_DK)
    kvraw = _matmul(kvn, wukv_g, mode="nn", name="mm_kv", out_blocks=NOPE + VDIM)
    q_mla, k_mla, v_mla = _mla_prep(qraw, kvraw, kper, cosT, sinT)
    cs = _fox_cumsum(logf, B, S, 1.0 / fox_scale)
    q_fox, k_fox, v_fox = _fox_prep(proj, cs, lay)
    ((o_mla, lse_mla), (o_fox, lse_fox)), (wup_g,) = _attn_fwd(
        [(q_mla, k_mla, v_mla, MLA_UNIT, mla_scale), (q_fox, k_fox, v_fox, 1, fox_scale)], B=B, S=S,
        name="attn_fwd", comm=_Comm([_GatherPlan([wup_s], mid_frac=0.7)]))
    n_up = wup_g.shape[2]
    a_m = _matmul(o_mla, wbm, mode="nn", name="mm_branch_mla")
    a_f = _matmul(o_fox, wbf, mode="nn", name="mm_branch_fox")
    merged = _gate_merge(a_m, a_f, proj, bgate, lay, D)
    y1 = _matmul(merged, wout, mode="nn", name="mm_out")
    x1, h2 = _mid(x2, y1, g_pm, g_ffn)
    u, (wdown_g,) = _matmul(h2, wup_g, mode="nn", name="mm_up", tn=n_up, comm=_Comm([_GatherPlan([wdown_s], mid_frac=0.6)]))
    wdown = wdown_g.reshape(F, D)
    act = _convffn_fwd(u, cw_full, cb_full, B, S, F)
    ff = _matmul(act, wdown, mode="nn", name="mm_down", tk=F // 2)
    dy, dff, loss_part, dg_pf = _tail(ff, x1, tgt, g_pf)

    def pair_plan(gs):
        return _Comm([_PairScatterPlan(gs)])

    def chip_plan(gs):
        return _Comm([_ChipScatterPlan(gs)])

    dact = _matmul(dff, wdown, mode="nt", name="mm_dact")
    dw_down = _matmul(act, dff, mode="tn", name="mm_dw_down", tm=512).reshape(N_DEV, F // N_DEV, D)
    (du_g, du_v, dcp_g, dcp_v), (pa_down,) = _convffn_bwd(u, dact, cw_full, cb_full, B, S, F, comm=pair_plan([dw_down]))
    keep_down, sb_down = _chip_sum(dw_down, pa_down, qc, "chipsum_w_down")
    du = jnp.concatenate([du_g, du_v], axis=1)
    dh2, (rb_down,) = _matmul(du, wup_g, mode="nt", name="mm_dh2", comm=chip_plan([sb_down]))
    dw_up = _matmul(h2, du, mode="tn", name="mm_dw_up", out_blocks=n_up, tm=512, tn=n_up)
    (dx1, dy1, dg_ffn, dg_pm), (pa_up,) = _mid_bwd(dy, dh2, x1, y1, g_ffn, g_pm, comm=pair_plan([dw_up]))
    keep_up, sb_up = _chip_sum(dw_up, pa_up, qc, "chipsum_w_up")
    dmerged = _matmul(dy1, wout, mode="nt", name="mm_dmerged")
    dw_out = _matmul(merged, dy1, mode="tn", name="mm_dw_out").reshape(N_DEV, D // N_DEV, D)
    da_m, da_f, dgl_m, dgl_f, dbg_m, dbg_f = _gate_bwd(dmerged, a_m, a_f, proj, bgate, lay, D)
    dw_bm = _matmul(o_mla, da_m, mode="tn", name="mm_dw_branch_mla", out_blocks=D // N_DEV)
    dw_bf = _matmul(o_fox, da_f, mode="tn", name="mm_dw_branch_fox", out_blocks=D // N_DEV)
    mix = [dw_out, dw_bm, dw_bf]
    do_mla, pa_mix = _matmul(da_m, wbm, mode="nt", name="mm_do_mla", out_dtype=BF16, comm=pair_plan(mix))
    do_fox = _matmul(da_f, wbf, mode="nt", name="mm_do_fox", out_dtype=BF16)
    mix_sums = [_chip_sum(g, p, qc, "chipsum_" + n) for g, p, n in zip(mix, pa_mix, ["w_out", "w_branch_mla", "w_branch_fox"])]
    ((dq_m, dk_m, dv_m), (dq_f, dk_f, dv_f)), landed = _attn_bwd(
        [(q_mla, k_mla, v_mla, o_mla, do_mla, lse_mla, MLA_UNIT, mla_scale),
         (q_fox, k_fox, v_fox, o_fox, do_fox, lse_fox, 1, fox_scale)], B=B, S=S, name="attn_bwd",
        comm=_Comm([_ChipScatterPlan([sb_up] + [s[1] for s in mix_sums])]))
    rb_up, rb_mix = landed[0], landed[1:]
    dqraw, dkvraw, dkpe = _mla_bwd_prep(dq_m, dk_m, dv_m, cosT, sinT)
    dqn = _matmul(dqraw, wuq_pad, mode="nt", name="mm_dqn")
    dw_uq = _matmul(qn, dqraw, mode="tn", name="mm_dw_uq", out_blocks=ATT_DK)[:, :, :NOPE + ROPE]
    dkvn = _matmul(dkvraw, wukv_g, mode="nt", name="mm_dkvn")
    dw_ukv = _matmul(kvn, dkvraw, mode="tn", name="mm_dw_ukv", out_blocks=NOPE + VDIM)
    dqlat, dkvlat, dg_q, dg_kv = _lat_bwd(dqn, dkvn, proj, g_q, g_kv, lay)
    dfl, dbfor = _fox_bwd_prep(dq_f, dk_f, proj, bfor, lay, B, S, 1.0 / fox_scale)
    dfq, dfk, dfv = _heads_to_cols(dq_f, dk_f, dv_f)
    dproj = jnp.concatenate([dqlat, dkvlat, dkpe.astype(BF16), dfq, dfk, dfv, dfl.astype(BF16), dgl_m, dgl_f], axis=1)
    dw_perm = _matmul(h, dproj, mode="tn", name="mm_dw_in")
    dw_in_full = jnp.concatenate(
        [dw_perm[:, :seg_a], dw_perm[:, lay["fq"]:lay["fq"] + seg_b], dw_perm[:, lay["g"]:]], axis=1)
    dw_in = jnp.transpose(dw_in_full.reshape(D, N_DEV, n_in), (1, 0, 2))
    dcw = jnp.transpose(jnp.concatenate([dcp_g[0:3], dcp_v[0:3]], axis=1).reshape(3, N_DEV, (2 * F) // N_DEV), (1, 0, 2))
    late = [dw_in, dw_uq, dw_ukv, dcw]
    dh, pa_late = _matmul(dproj, w_perm, mode="nt", name="mm_dh", tk=2048, comm=pair_plan(late))
    late_sums = [_chip_sum(g, p, qc, "chipsum_" + n) for g, p, n in zip(late, pa_late, ["w_in", "w_uq", "w_ukv", "conv_w"])]
    (grad_x, dg_pre), rb_late = _final_dx(dx1, dh, x2, g_pre, comm=chip_plan([s[1] for s in late_sums]))

    big_out = {}

    def finish(n, keep, pieces, w, m, v):
        big_out[n] = _sum_adamw(keep, pieces, w, m, v, "adamw_" + n)

    finish("w_down", keep_down, rb_down, w_down, m_w_down, v_w_down)
    finish("w_up", keep_up, rb_up, w_up, m_w_up, v_w_up)
    finish("w_out", mix_sums[0][0], rb_mix[0], w_out, m_w_out, v_w_out)
    finish("w_branch_mla", mix_sums[1][0], rb_mix[1], w_branch_mla, m_w_branch_mla, v_w_branch_mla)
    finish("w_branch_fox", mix_sums[2][0], rb_mix[2], w_branch_fox, m_w_branch_fox, v_w_branch_fox)
    finish("w_in", late_sums[0][0], rb_late[0], w_in, m_w_in, v_w_in)
    finish("w_uq", late_sums[1][0], rb_late[1], w_uq, m_w_uq, v_w_uq)
    finish("w_ukv", late_sums[2][0], rb_late[2], w_ukv, m_w_ukv, v_w_ukv)
    finish("conv_w", late_sums[3][0], rb_late[3], conv_w, m_conv_w, v_conv_w)

    widths = [D, Q_LORA, KV_LORA, LANES, 2 * D, D, D, 2 * F, D]
    small_names = ["pre_mix_norm", "q_a_norm", "kv_a_norm", "b_forget", "b_gate", "post_mix_norm", "pre_ffn_norm",
                   "conv_b", "post_ffn_norm"]
    true_w = [D, Q_LORA, KV_LORA, HEADS, 2 * D, D, D, 2 * F, D]
    dcb = jnp.concatenate([dcp_g[3:4], dcp_v[3:4]], axis=1)
    part = jnp.concatenate([dg_pre, dg_q, dg_kv, dbfor, dbg_m, dbg_f, dg_pm, dg_ffn, dcb, dg_pf], axis=1)

    def pack(vals):
        return jnp.concatenate([row(a, wd) for a, wd in zip(vals, widths)], axis=1)

    sw = pack([pre_mix_norm, q_a_norm, kv_a_norm, b_forget, b_gate, post_mix_norm, pre_ffn_norm, conv_b, post_ffn_norm])
    sm = pack([m_pre_mix_norm, m_q_a_norm, m_kv_a_norm, m_b_forget, m_b_gate, m_post_mix_norm, m_pre_ffn_norm,
               m_conv_b, m_post_ffn_norm])
    sv = pack([v_pre_mix_norm, v_q_a_norm, v_kv_a_norm, v_b_forget, v_b_gate, v_post_mix_norm, v_pre_ffn_norm,
               v_conv_b, v_post_ffn_norm])
    (parts_all,) = _exchange_alone(_Comm([_DirectGatherPlan([part])]), "gather_small")
    sg, sd, smo, svo = _sum_adamw(None, parts_all, sw, sm, sv, "adamw_small")
    small_out = {}
    off = 0
    for n, wd, tw in zip(small_names, widths, true_w):
        small_out[n] = tuple(a[0, off:off + tw] for a in (sg, sd, smo, svo))
        off += wd

    loss = lax.psum(loss_part[0, 0], MESH_AXES)
    order = ["pre_mix_norm", "w_in", "q_a_norm", "w_uq", "kv_a_norm", "w_ukv", "b_forget", "b_gate", "w_branch_mla",
             "w_branch_fox", "w_out", "post_mix_norm", "pre_ffn_norm", "w_up", "conv_w", "conv_b", "w_down",
             "post_ffn_norm"]
    res = {**big_out, **small_out}
    outs = [loss, grad_x.reshape(B, S, D)]
    for kind in range(4):
        outs += [res[n][kind] for n in order]
    return tuple(outs)
```

```python
import math

import jax
import jax.numpy as jnp
from jax import lax
from jax.experimental import pallas as pl
from jax.experimental.pallas import tpu as pltpu

F32 = jnp.float32
BF16 = jnp.bfloat16

N_DEV = 8
N_CHIP = 4
HEADS = 8
NOPE = 128
ROPE = 64
HALF_ROPE = ROPE // 2
VDIM = 128
Q_LORA = 512
KV_LORA = 256
FOX_DIM = 128
ATT_DK = 256
MLA_UNIT = 64
ROPE_THETA = 10000.0
EPS = 1e-6
NEG_INF = -1e30
LANES = 128
LOG2_E = 1.4426950408889634

ADAM_LR = 0.001
ADAM_B1 = 0.9
ADAM_B2 = 0.999
ADAM_EPS = 1e-08
ADAM_WD = 0.01
ADAM_STEP = 10

VMEM_LIMIT_BYTES = 56 * 1024 * 1024
ROW_TILE = 256
HEAD_ROW_TILE = 1024
ATT_TILE = 512
ATT_SUB = 256
ATT_AHEAD = 3
MM_TILE = 1024

MESH_AXES = ("x", "y", "c")
ANY = pl.BlockSpec(memory_space=pl.ANY)


def _tile(n, pref, align=LANES):
    if n <= pref:
        return n
    t = (pref // align) * align
    while t >= align:
        if n % t == 0:
            return t
        t -= align
    return n


def _sds(shape, dtype):
    return jax.ShapeDtypeStruct(shape, dtype)


def _coords():
    x, y, c = (lax.axis_index(ax) for ax in MESH_AXES)
    return x, y, c


def _chip_rel(x, y, r):
    return (1 - x if r & 2 else x), (1 - y if r & 1 else y)


def _rcopy(src, dst, sems, w, k, dev):
    return pltpu.make_async_remote_copy(src_ref=src, dst_ref=dst, send_sem=sems[0].at[w, k], recv_sem=sems[1].at[w, k],
                                        device_id=dev, device_id_type=pl.DeviceIdType.MESH)


class _GatherPlan:
    def __init__(self, blocks, mid_frac=0.5):
        self.ins = list(blocks)
        self.out_shapes = [_sds((N_DEV,) + b.shape, b.dtype) for b in blocks]
        n = len(blocks)
        self.scratch = [pltpu.SemaphoreType.DMA((n, 7)), pltpu.SemaphoreType.DMA((n, 7)), pltpu.SemaphoreType.DMA((n,))]
        self.mid_frac = mid_frac

    def first(self, ins, outs, sems):
        x, y, c = _coords()
        me = 4 * x + 2 * y + c
        for w in range(len(ins)):
            pltpu.make_async_copy(ins[w], outs[w].at[me], sems[2].at[w]).start()
            _rcopy(ins[w], outs[w].at[me], sems, w, 0, (x, y, 1 - c)).start()
            for r in (1, 2, 3):
                px, py = _chip_rel(x, y, r)
                _rcopy(ins[w], outs[w].at[me], sems, w, r, (px, py, c)).start()

    def mid(self, ins, outs, sems):
        x, y, c = _coords()
        for w in range(len(ins)):
            for r in (1, 2, 3):
                px, py = _chip_rel(x, y, r)
                blk = outs[w].at[4 * px + 2 * py + c]
                _rcopy(ins[w], blk, sems, w, r, (px, py, c)).wait_recv()
                _rcopy(blk, blk, sems, w, 3 + r, (x, y, 1 - c)).start()

    def last(self, ins, outs, sems):
        x, y, c = _coords()
        me = 4 * x + 2 * y + c
        sib = (x, y, 1 - c)
        for w in range(len(ins)):
            _rcopy(ins[w], outs[w].at[4 * x + 2 * y + 1 - c], sems, w, 0, sib).wait_recv()
            for r in (1, 2, 3):
                px, py = _chip_rel(x, y, r)
                blk = outs[w].at[4 * px + 2 * py + 1 - c]
                _rcopy(blk, blk, sems, w, 3 + r, sib).wait_recv()
            for k in range(7):
                _rcopy(ins[w], outs[w].at[me], sems, w, k, sib).wait_send()
            pltpu.make_async_copy(ins[w], outs[w].at[me], sems[2].at[w]).wait()


class _GatherOwnPlan:
    mid = None

    def __init__(self, blocks):
        self.ins = list(blocks)
        self.out_shapes = [_sds((N_DEV,) + b.shape, b.dtype) for b in blocks]
        n = len(blocks)
        self.scratch = [pltpu.SemaphoreType.DMA((n, 4)), pltpu.SemaphoreType.DMA((n, 4)), pltpu.SemaphoreType.DMA((n,))]

    def first(self, ins, outs, sems):
        x, y, c = _coords()
        me = 4 * x + 2 * y + c
        for w in range(len(ins)):
            pltpu.make_async_copy(ins[w], outs[w].at[me], sems[2].at[w]).start()
            _rcopy(ins[w], outs[w].at[me], sems, w, 0, (x, y, 1 - c)).start()
            for r in (1, 2, 3):
                px, py = _chip_rel(x, y, r)
                _rcopy(ins[w], outs[w].at[me], sems, w, r, (px, py, c)).start()

    def last(self, ins, outs, sems):
        x, y, c = _coords()
        me = 4 * x + 2 * y + c
        for w in range(len(ins)):
            cp = _rcopy(ins[w], outs[w].at[4 * x + 2 * y + 1 - c], sems, w, 0, (x, y, 1 - c))
            cp.wait_recv()
            cp.wait_send()
            for r in (1, 2, 3):
                px, py = _chip_rel(x, y, r)
                cp = _rcopy(ins[w], outs[w].at[4 * px + 2 * py + c], sems, w, r, (px, py, c))
                cp.wait_recv()
                cp.wait_send()
            pltpu.make_async_copy(ins[w], outs[w].at[me], sems[2].at[w]).wait()


class _GatherPassPlan:
    mid = None

    def __init__(self, gathered):
        self.ins = list(gathered)
        self.out_shapes = [_sds(g.shape, g.dtype) for g in gathered]
        self.aliases = [(i, i) for i in range(len(gathered))]
        n = len(gathered)
        self.scratch = [pltpu.SemaphoreType.DMA((n, 3)), pltpu.SemaphoreType.DMA((n, 3))]

    def first(self, ins, outs, sems):
        x, y, c = _coords()
        for w in range(len(ins)):
            for r in (1, 2, 3):
                px, py = _chip_rel(x, y, r)
                blk = 4 * px + 2 * py + c
                _rcopy(ins[w].at[blk], outs[w].at[blk], sems, w, r - 1, (x, y, 1 - c)).start()

    def last(self, ins, outs, sems):
        x, y, c = _coords()
        for w in range(len(ins)):
            for r in (1, 2, 3):
                px, py = _chip_rel(x, y, r)
                blk = 4 * px + 2 * py + 1 - c
                cp = _rcopy(ins[w].at[blk], outs[w].at[blk], sems, w, r - 1, (x, y, 1 - c))
                cp.wait_recv()
                cp.wait_send()


class _DirectGatherPlan:
    mid = None

    def __init__(self, blocks):
        self.ins = list(blocks)
        self.out_shapes = [_sds((N_DEV,) + b.shape, b.dtype) for b in blocks]
        n = len(blocks)
        self.scratch = [pltpu.SemaphoreType.DMA((n, 7)), pltpu.SemaphoreType.DMA((n, 7)), pltpu.SemaphoreType.DMA((n,))]

    @staticmethod
    def _peer(x, y, c, r):
        return (1 - x if r & 4 else x), (1 - y if r & 2 else y), (1 - c if r & 1 else c)

    def first(self, ins, outs, sems):
        x, y, c = _coords()
        me = 4 * x + 2 * y + c
        for w in range(len(ins)):
            pltpu.make_async_copy(ins[w], outs[w].at[me], sems[2].at[w]).start()
            for r in range(1, N_DEV):
                _rcopy(ins[w], outs[w].at[me], sems, w, r - 1, self._peer(x, y, c, r)).start()

    def last(self, ins, outs, sems):
        x, y, c = _coords()
        me = 4 * x + 2 * y + c
        for w in range(len(ins)):
            for r in range(1, N_DEV):
                px, py, pc = self._peer(x, y, c, r)
                cp = _rcopy(ins[w], outs[w].at[4 * px + 2 * py + pc], sems, w, r - 1, (px, py, pc))
                cp.wait_recv()
                cp.wait_send()
            pltpu.make_async_copy(ins[w], outs[w].at[me], sems[2].at[w]).wait()


class _PairScatterPlan:
    mid = None

    def __init__(self, pieces):
        self.ins = list(pieces)
        self.out_shapes = [_sds((N_CHIP,) + p.shape[1:], p.dtype) for p in pieces]
        n = len(pieces)
        self.scratch = [pltpu.SemaphoreType.DMA((n, N_CHIP)), pltpu.SemaphoreType.DMA((n, N_CHIP))]

    def _copies(self, ins, outs, sems):
        x, y, c = _coords()
        return [_rcopy(ins[w].at[2 * q + 1 - c], outs[w].at[q], sems, w, q, (x, y, 1 - c))
                for w in range(len(ins)) for q in range(N_CHIP)]

    def first(self, ins, outs, sems):
        for cp in self._copies(ins, outs, sems):
            cp.start()

    def last(self, ins, outs, sems):
        for cp in self._copies(ins, outs, sems):
            cp.wait_recv()
            cp.wait_send()


class _ChipScatterPlan:
    mid = None

    def __init__(self, sums):
        self.ins = list(sums)
        self.out_shapes = [_sds(s.shape, s.dtype) for s in sums]
        n = len(sums)
        self.scratch = [pltpu.SemaphoreType.DMA((n, 3)), pltpu.SemaphoreType.DMA((n, 3))]

    def _copies(self, ins, outs, sems):
        x, y, c = _coords()
        cps = []
        for w in range(len(ins)):
            for r in (1, 2, 3):
                px, py = _chip_rel(x, y, r)
                cps.append(_rcopy(ins[w].at[r - 1], outs[w].at[r - 1], sems, w, r - 1, (px, py, c)))
        return cps

    def first(self, ins, outs, sems):
        for cp in self._copies(ins, outs, sems):
            cp.start()

    def last(self, ins, outs, sems):
        for cp in self._copies(ins, outs, sems):
            cp.wait_recv()
            cp.wait_send()


class _Comm:
    def __init__(self, plans):
        self.plans = list(plans)
        self.ins = [a for p in self.plans for a in p.ins]
        self.out_shapes = [s for p in self.plans for s in p.out_shapes]
        self.scratch = [s for p in self.plans for s in p.scratch]
        self.aliases = []
        i = o = 0
        for p in self.plans:
            self.aliases += [(i + a, o + b) for a, b in getattr(p, "aliases", [])]
            i, o = i + len(p.ins), o + len(p.out_shapes)

    def _parts(self, ins, outs, sems):
        i = o = s = 0
        for p in self.plans:
            yield p, ins[i:i + len(p.ins)], outs[o:o + len(p.out_shapes)], sems[s:s + len(p.scratch)]
            i, o, s = i + len(p.ins), o + len(p.out_shapes), s + len(p.scratch)

    def begin(self, step, nsteps, ins, outs, sems):
        @pl.when(step == 0)
        def _():
            for p, pi, po, ps in self._parts(ins, outs, sems):
                p.first(pi, po, ps)

        for p, pi, po, ps in self._parts(ins, outs, sems):
            if p.mid is not None:
                @pl.when(step == min(nsteps - 1, int(p.mid_frac * nsteps)))
                def _(p=p, pi=pi, po=po, ps=ps):
                    p.mid(pi, po, ps)

    def end(self, step, nsteps, ins, outs, sems):
        @pl.when(step == nsteps - 1)
        def _():
            for p, pi, po, ps in self._parts(ins, outs, sems):
                p.last(pi, po, ps)


def _call(body, args, *, name, grid, in_specs, out_specs, out_shape, scratch_shapes=(), sem=None, comm=None):
    in_specs, out_specs, out_shape, scratch_shapes = list(in_specs), list(out_specs), list(out_shape), list(scratch_shapes)
    if comm is None:
        res = pl.pallas_call(
            body, name=name, grid=grid, in_specs=in_specs, out_specs=out_specs, out_shape=out_shape,
            scratch_shapes=scratch_shapes,
            compiler_params=pltpu.CompilerParams(dimension_semantics=sem, vmem_limit_bytes=VMEM_LIMIT_BYTES),
        )(*args)
        return list(res), []
    n_in, n_out, n_sc = len(in_specs), len(out_specs), len(scratch_shapes)
    n_ci, n_co = len(comm.ins), len(comm.out_shapes)
    nsteps = math.prod(grid)

    def hosted(*refs):
        ins, cins = refs[:n_in], refs[n_in:n_in + n_ci]
        o0 = n_in + n_ci
        outs, couts = refs[o0:o0 + n_out], refs[o0 + n_out:o0 + n_out + n_co]
        s0 = o0 + n_out + n_co
        scr, csems = refs[s0:s0 + n_sc], refs[s0 + n_sc:]
        step = jnp.int32(0)
        for d in range(len(grid)):
            step = step * grid[d] + pl.program_id(d)
        comm.begin(step, nsteps, cins, couts, csems)
        body(*ins, *outs, *scr)
        comm.end(step, nsteps, cins, couts, csems)

    res = pl.pallas_call(
        hosted, name=name, grid=grid, in_specs=in_specs + [ANY] * n_ci, out_specs=out_specs + [ANY] * n_co,
        out_shape=out_shape + comm.out_shapes, scratch_shapes=scratch_shapes + comm.scratch,
        input_output_aliases={n_in + a: n_out + b for a, b in comm.aliases},
        compiler_params=pltpu.CompilerParams(dimension_semantics=("arbitrary",) * len(grid),
                                             vmem_limit_bytes=VMEM_LIMIT_BYTES, has_side_effects=True),
    )(*args, *comm.ins)
    return list(res[:n_out]), list(res[n_out:])


def _exchange_alone(comm, name):
    def body():
        pass

    return _call(body, [], name=name, grid=(), in_specs=[], out_specs=[], out_shape=[], comm=comm)[1]


def _matmul(a, b, *, mode, name, out_dtype=F32, out_blocks=None, tm=None, tn=None, tk=None, comm=None):
    tm = MM_TILE if tm is None else tm
    tn = MM_TILE if tn is None else tn
    a_blk = a.ndim == 3
    b_blk = b.ndim == 3
    if mode == "nn":
        M, K = a.shape
        N = b.shape[0] * b.shape[2] if b_blk else b.shape[1]
        dims = (((1,), (0,)), ((), ()))
    elif mode == "nt":
        M = a.shape[1] if a_blk else a.shape[0]
        K = a.shape[0] * a.shape[2] if a_blk else a.shape[1]
        N = b.shape[1] if b_blk else b.shape[0]
        dims = (((1,), (1,)), ((), ()))
    else:
        K, M = a.shape
        N = b.shape[0] * b.shape[2] if b_blk else b.shape[1]
        dims = (((0,), (0,)), ((), ()))

    tm = _tile(M, tm)
    tn = _tile(N, tn)
    if mode == "nt" and (a_blk or b_blk):
        tk = a.shape[2] if a_blk else b.shape[2]
    else:
        tk = _tile(K, K if tk is None else tk)
    if mode != "nt" and b_blk:
        tn = _tile(b.shape[2], tn)
    if out_blocks is not None:
        tn = _tile(out_blocks, tn)
    nk = K // tk
    grid = (M // tm, N // tn, nk)

    if mode == "nn":
        a_spec = pl.BlockSpec((tm, tk), lambda i, j, k: (i, k))
        if b_blk:
            rb = b.shape[2] // tn
            b_spec = pl.BlockSpec((None, tk, tn), lambda i, j, k: (j // rb, k, j % rb))
        else:
            b_spec = pl.BlockSpec((tk, tn), lambda i, j, k: (k, j))
    elif mode == "nt":
        if a_blk:
            a_spec = pl.BlockSpec((None, tm, tk), lambda i, j, k: (k, i, 0))
        else:
            a_spec = pl.BlockSpec((tm, tk), lambda i, j, k: (i, k))
        if b_blk:
            b_spec = pl.BlockSpec((None, tn, tk), lambda i, j, k: (k, j, 0))
        else:
            b_spec = pl.BlockSpec((tn, tk), lambda i, j, k: (j, k))
    else:
        a_spec = pl.BlockSpec((tk, tm), lambda i, j, k: (k, i))
        if b_blk:
            rb = b.shape[2] // tn
            b_spec = pl.BlockSpec((None, tk, tn), lambda i, j, k: (j // rb, k, j % rb))
        else:
            b_spec = pl.BlockSpec((tk, tn), lambda i, j, k: (k, j))

    if out_blocks is None:
        o_spec = pl.BlockSpec((tm, tn), lambda i, j, k: (i, j))
        o_shape = _sds((M, N), out_dtype)
    else:
        ro = out_blocks // tn
        o_spec = pl.BlockSpec((None, tm, tn), lambda i, j, k: (j // ro, i, j % ro))
        o_shape = _sds((N // out_blocks, M, out_blocks), out_dtype)

    def body(a_ref, b_ref, o_ref, *scratch):
        part = lax.dot_general(a_ref[...], b_ref[...], dims, preferred_element_type=F32)
        if nk == 1:
            o_ref[...] = part.astype(o_ref.dtype)
        else:
            acc_ref = scratch[0]
            k = pl.program_id(2)

            @pl.when(k == 0)
            def _():
                acc_ref[...] = part

            @pl.when(k > 0)
            def _():
                acc_ref[...] += part

            @pl.when(k == nk - 1)
            def _():
                o_ref[...] = acc_ref[...].astype(o_ref.dtype)

    scratch = [] if nk == 1 else [pltpu.VMEM((tm, tn), F32)]
    outs, landed = _call(body, [a, b], name=name, grid=grid, in_specs=[a_spec, b_spec], out_specs=[o_spec],
                         out_shape=[o_shape], scratch_shapes=scratch, sem=("parallel", "parallel", "arbitrary"), comm=comm)
    return outs[0] if comm is None else (outs[0], landed)


def _rms(x):
    return lax.rsqrt(jnp.mean(x * x, axis=-1, keepdims=True) + EPS)


def _rms_bwd(dy, x, g):
    r = _rms(x)
    n = x * r
    dn = dy * g
    dx = r * (dn - n * jnp.mean(dn * n, axis=-1, keepdims=True))
    return dx, dy * n


def _sigmoid(x):
    return 1.0 / (1.0 + jnp.exp(-x))


def _rope_rot(t):
    return pltpu.roll(t, HALF_ROPE, 1) - pltpu.roll(t, LANES - HALF_ROPE, 1)


def _lane(shape):
    return lax.broadcasted_iota(jnp.int32, shape, 1)


def _split3(x):
    hi = x.astype(BF16).astype(F32)
    r1 = x - hi
    mid = r1.astype(BF16).astype(F32)
    lo = (r1 - mid).astype(BF16).astype(F32)
    return hi, mid, lo


def _cumsum_rows(x, reverse):
    S = x.shape[0]
    bs = min(256, S)
    nb = S // bs
    r = lax.broadcasted_iota(jnp.int32, (bs, bs), 0)
    c = lax.broadcasted_iota(jnp.int32, (bs, bs), 1)
    tri = jnp.where((c >= r) if reverse else (c <= r), 1.0, 0.0).astype(BF16)
    edge = lax.broadcasted_iota(jnp.int32, (bs, x.shape[1]), 0) == (0 if reverse else bs - 1)
    carry = jnp.zeros((1, x.shape[1]), F32)
    outs = [None] * nb
    for bi in (range(nb - 1, -1, -1) if reverse else range(nb)):
        xb = x[bi * bs:(bi + 1) * bs, :]
        acc = carry
        for term in _split3(xb):
            acc = acc + jnp.dot(tri, term.astype(BF16), preferred_element_type=F32)
        outs[bi] = acc
        carry = jnp.sum(jnp.where(edge, acc, 0.0), axis=0, keepdims=True)
    return jnp.concatenate(outs, axis=0) if nb > 1 else outs[0]


def _gelu_parts(x):
    c0 = math.sqrt(2.0 / math.pi)
    inner = c0 * (x + 0.044715 * (x * x * x))
    t = jnp.tanh(inner)
    g = 0.5 * x * (1.0 + t)
    dg = 0.5 * (1.0 + t) + 0.5 * x * (1.0 - t * t) * (c0 * (1.0 + 3.0 * 0.044715 * (x * x)))
    return g, dg


def _accumulate(ref, value, first):
    @pl.when(first)
    def _():
        ref[...] = value

    @pl.when(jnp.logical_not(first))
    def _():
        ref[...] += value


def _cast_bf16(w, name):
    R, C = w.shape
    tr = _tile(R, 512, 16)

    def body(w_ref, o_ref):
        o_ref[...] = w_ref[...].astype(BF16)

    blk = pl.BlockSpec((tr, C), lambda i: (i, 0))
    return _call(body, [w], name=name, grid=(R // tr,), in_specs=[blk], out_specs=[blk],
                 out_shape=[_sds((R, C), BF16)], sem=("parallel",))[0][0]


def _prenorm(x, g):
    T, D = x.shape
    tm = _tile(T, ROW_TILE, 16)

    def body(x_ref, g_ref, h_ref):
        xv = x_ref[...]
        h_ref[...] = (xv * _rms(xv) * g_ref[...]).astype(BF16)

    row = pl.BlockSpec((tm, D), lambda i: (i, 0))
    return _call(body, [x, g], name="prenorm", grid=(T // tm,),
                 in_specs=[row, pl.BlockSpec((1, D), lambda i: (0, 0))], out_specs=[row],
                 out_shape=[_sds((T, D), BF16)], sem=("parallel",))[0][0]


def _split_prep(proj, pos, invf, gq, gkv, bfor, lay, comm=None):
    T = proj.shape[0]
    tm = _tile(T, ROW_TILE, 16)

    def body(q_ref, kv_ref, kpe_ref, fl_ref, pos_ref, invf_ref, gq_ref, gkv_ref, bf_ref,
             qn_ref, kvn_ref, kper_ref, logf_ref, cos_ref, sin_ref):
        ql = q_ref[...]
        qn_ref[...] = (ql * _rms(ql) * gq_ref[...]).astype(BF16)
        kl = kv_ref[...]
        kvn_ref[...] = (kl * _rms(kl) * gkv_ref[...]).astype(BF16)
        ang = pos_ref[...].astype(F32) * invf_ref[...]
        valid = _lane(ang.shape) < ROPE
        cs = jnp.where(valid, jnp.cos(ang), 0.0)
        sn = jnp.where(valid, jnp.sin(ang), 0.0)
        cos_ref[...] = cs
        sin_ref[...] = sn
        kp = jnp.where(valid, kpe_ref[...], 0.0)
        kper_ref[...] = (kp * cs + _rope_rot(kp) * sn).astype(BF16)
        z = fl_ref[...] + bf_ref[...]
        logf_ref[...] = jnp.minimum(z, 0.0) - jnp.log(1.0 + jnp.exp(-jnp.abs(z)))

    def col(width, off):
        return pl.BlockSpec((tm, width), lambda i: (i, off // width))

    def vec(width):
        return pl.BlockSpec((1, width), lambda i: (0, 0))

    def out(width):
        return pl.BlockSpec((tm, width), lambda i: (i, 0))

    return _call(
        body, [proj, proj, proj, proj, pos, invf, gq, gkv, bfor], name="split_prep", grid=(T // tm,),
        in_specs=[col(Q_LORA, lay["q"]), col(KV_LORA, lay["kv"]), col(LANES, lay["kpe"]), col(LANES, lay["fl"]),
                  pl.BlockSpec((tm, 1), lambda i: (i, 0)), vec(LANES), vec(Q_LORA), vec(KV_LORA), vec(LANES)],
        out_specs=[out(Q_LORA), out(KV_LORA), out(LANES), out(LANES), out(LANES), out(LANES)],
        out_shape=[_sds((T, Q_LORA), BF16), _sds((T, KV_LORA), BF16), _sds((T, LANES), BF16),
                   _sds((T, LANES), F32), _sds((T, LANES), F32), _sds((T, LANES), F32)],
        sem=("parallel",), comm=comm)


def _mla_prep(qraw, kvraw, kper, cosT, sinT):
    H, T, _ = qraw.shape
    tm = _tile(T, HEAD_ROW_TILE, 16)

    def body(q_ref, kv_ref, kpe_ref, cos_ref, sin_ref, qo_ref, ko_ref, vo_ref):
        q = q_ref[...]
        pe = q[:, NOPE:]
        pe = jnp.where(_lane(pe.shape) < ROPE, pe, 0.0)
        qo_ref[:, :NOPE] = q[:, :NOPE].astype(BF16)
        qo_ref[:, NOPE:] = (pe * cos_ref[...] + _rope_rot(pe) * sin_ref[...]).astype(BF16)
        kv = kv_ref[...]
        ko_ref[:, :NOPE] = kv[:, :NOPE].astype(BF16)
        ko_ref[:, NOPE:] = kpe_ref[...]
        vo_ref[...] = kv[:, NOPE:].astype(BF16)

    head = pl.BlockSpec((None, tm, ATT_DK), lambda h, i: (h, i, 0))
    tok = pl.BlockSpec((tm, LANES), lambda h, i: (i, 0))
    return _call(
        body, [qraw, kvraw, kper, cosT, sinT], name="mla_prep", grid=(H, T // tm),
        in_specs=[head, head, tok, tok, tok],
        out_specs=[head, head, pl.BlockSpec((None, tm, VDIM), lambda h, i: (h, i, 0))],
        out_shape=[_sds((H, T, ATT_DK), BF16), _sds((H, T, ATT_DK), BF16), _sds((H, T, VDIM), BF16)],
        sem=("parallel", "parallel"))[0]


def _fox_cumsum(logf, B, S, inv_scale):
    T = logf.shape[0]

    def body(l_ref, c_ref):
        c_ref[...] = _cumsum_rows(l_ref[...], reverse=False) * inv_scale

    seq = pl.BlockSpec((S, LANES), lambda b: (b, 0))
    return _call(body, [logf], name="fox_cumsum", grid=(B,), in_specs=[seq], out_specs=[seq],
                 out_shape=[_sds((T, LANES), F32)], sem=("parallel",))[0][0]


def _fox_prep(proj, cs, lay):
    T = proj.shape[0]
    tm = _tile(T, HEAD_ROW_TILE, 16)

    def body(q_ref, k_ref, v_ref, cs_ref, qo_ref, ko_ref, vo_ref):
        h = pl.program_id(0)
        cv = cs_ref[...]
        lane = _lane(cv.shape)
        ccol = jnp.sum(jnp.where(lane == h, cv, 0.0), axis=1, keepdims=True)
        hi, mid, lo = _split3(ccol)
        one = jnp.where(lane < 6, 1.0, 0.0)
        augq = jnp.where(lane == 0, hi, jnp.where(lane == 1, mid, jnp.where(lane == 2, lo, one)))
        augk = jnp.where(lane < 3, 1.0, jnp.where(lane == 3, -hi, jnp.where(lane == 4, -mid, jnp.where(lane == 5, -lo, 0.0))))
        qo_ref[:, :FOX_DIM] = q_ref[...].astype(BF16)
        qo_ref[:, FOX_DIM:] = augq.astype(BF16)
        ko_ref[:, :FOX_DIM] = k_ref[...].astype(BF16)
        ko_ref[:, FOX_DIM:] = augk.astype(BF16)
        vo_ref[...] = v_ref[...].astype(BF16)

    def col(off):
        return pl.BlockSpec((tm, FOX_DIM), lambda h, i: (i, off // FOX_DIM + h))

    head = pl.BlockSpec((None, tm, ATT_DK), lambda h, i: (h, i, 0))
    return _call(
        body, [proj, proj, proj, cs], name="fox_prep", grid=(HEADS, T // tm),
        in_specs=[col(lay["fq"]), col(lay["fk"]), col(lay["fv"]), pl.BlockSpec((tm, LANES), lambda h, i: (i, 0))],
        out_specs=[head, head, pl.BlockSpec((None, tm, VDIM), lambda h, i: (h, i, 0))],
        out_shape=[_sds((HEADS, T, ATT_DK), BF16), _sds((HEADS, T, ATT_DK), BF16), _sds((HEADS, T, VDIM), BF16)],
        sem=("parallel", "parallel"))[0]


def _visible(tq, tk, unit):
    r = lax.broadcasted_iota(jnp.int32, (tq, tk), 0)
    c = lax.broadcasted_iota(jnp.int32, (tq, tk), 1)
    sh = int(math.log2(unit))
    return lax.shift_right_logical(c, sh) <= lax.shift_right_logical(r, sh)


def _attn_fwd(streams, *, B, S, name, comm=None):
    n = len(streams)
    H, T, DK = streams[0][0].shape
    DV = streams[0][2].shape[2]
    tq = _tile(S, ATT_TILE)
    nq = S // tq
    sub = min(ATT_SUB, tq)
    NT = (((1,), (1,)), ((), ()))

    def body(*refs):
        ins, outs, (m_sc, acc_sc) = refs[:3 * n], refs[3 * n:5 * n], refs[5 * n:]
        i, j = pl.program_id(1), pl.program_id(2)

        @pl.when(j == 0)
        def _():
            m_sc[...] = jnp.full(m_sc.shape, NEG_INF, F32)
            acc_sc[...] = jnp.zeros(acc_sc.shape, F32)

        def step(diagonal):
            work = [(t, r) for r in range(tq // sub) for t in range(n)]

            def scores(t, r):
                q_ref, k_ref, _ = ins[3 * t:3 * t + 3]
                kc = (r + 1) * sub if diagonal else tq
                s = lax.dot_general(q_ref[r * sub:(r + 1) * sub, :], k_ref[0:kc, :], NT, preferred_element_type=F32)
                return s * (streams[t][4] * LOG2_E)

            ahead = [scores(*work[w]) for w in range(min(ATT_AHEAD, len(work)))]
            for w, (t, r) in enumerate(work):
                s = ahead.pop(0)
                if w + ATT_AHEAD < len(work):
                    ahead.append(scores(*work[w + ATT_AHEAD]))
                v_ref = ins[3 * t + 2]
                kc = s.shape[1]
                rows = slice(r * sub, (r + 1) * sub)
                if diagonal:
                    own = jnp.where(_visible(sub, sub, streams[t][3]), s[:, kc - sub:], NEG_INF)
                    s = own if kc == sub else jnp.concatenate([s[:, :kc - sub], own], axis=1)
                m_prev = m_sc[t, rows, :]
                mx = s[:, 0:LANES]
                for g in range(1, kc // LANES):
                    mx = jnp.maximum(mx, s[:, g * LANES:(g + 1) * LANES])
                m_new = jnp.maximum(m_prev, jnp.max(mx, axis=1, keepdims=True))
                alpha = jnp.exp2(m_prev - m_new)
                p = jnp.exp2(s - jnp.tile(m_new, (1, kc // LANES))).astype(BF16)
                v_aug = jnp.concatenate([v_ref[0:kc, :], jnp.ones((kc, LANES), BF16)], axis=1)
                acc_sc[t, rows, :] = jnp.tile(alpha, (1, 2)) * acc_sc[t, rows, :] + jnp.dot(
                    p, v_aug, preferred_element_type=F32)
                m_sc[t, rows, :] = m_new

        @pl.when(j < i)
        def _():
            step(False)

        @pl.when(j == i)
        def _():
            step(True)
            for t in range(n):
                o_ref, lse_ref = outs[2 * t:2 * t + 2]
                l = acc_sc[t, :, DV:]
                o_ref[...] = (acc_sc[t, :, :DV] / l).astype(BF16)
                lse_ref[...] = m_sc[t] + jnp.log2(l)

    def qmap(g, i, j):
        return (g % H, (g // H) * nq + i, 0)

    def kmap(g, i, j):
        return (g % H, (g // H) * nq + jnp.minimum(j, i), 0)

    args = [a for st in streams for a in st[:3]]
    outs, landed = _call(
        body, args, name=name, grid=(B * H, nq, nq),
        in_specs=[pl.BlockSpec((None, tq, DK), qmap), pl.BlockSpec((None, tq, DK), kmap),
                  pl.BlockSpec((None, tq, DV), kmap)] * n,
        out_specs=[pl.BlockSpec((tq, DV), lambda g, i, j: ((g // H) * nq + i, g % H)),
                   pl.BlockSpec((None, tq, LANES), qmap)] * n,
        out_shape=[_sds((T, H * DV), BF16), _sds((H, T, LANES), F32)] * n,
        scratch_shapes=[pltpu.VMEM((n, tq, LANES), F32), pltpu.VMEM((n, tq, DV + LANES), F32)],
        sem=("parallel", "parallel", "arbitrary"), comm=comm)
    return [(outs[2 * t], outs[2 * t + 1]) for t in range(n)], landed


def _attn_bwd(streams, *, B, S, name, comm=None):
    n = len(streams)
    H, T, DK = streams[0][0].shape
    DV = streams[0][2].shape[2]
    tq = _tile(S, ATT_TILE)
    nq = S // tq
    NT = (((1,), (1,)), ((), ()))
    TN = (((0,), (0,)), ((), ()))

    def body(*refs):
        ins, outs = refs[:6 * n], refs[6 * n:]
        j, i = pl.program_id(1), pl.program_id(2)

        @pl.when(jnp.logical_and(j == 0, i == 0))
        def _():
            for t in range(n):
                outs[3 * t][...] = jnp.zeros(outs[3 * t].shape, F32)

        @pl.when(i == 0)
        def _():
            for t in range(n):
                outs[3 * t + 1][...] = jnp.zeros(outs[3 * t + 1].shape, F32)
                outs[3 * t + 2][...] = jnp.zeros(outs[3 * t + 2].shape, F32)

        def step(diagonal):
            for t, st in enumerate(streams):
                unit, scale = st[6], st[7]
                q_ref, k_ref, v_ref, o_ref, do_ref, lse_ref = ins[6 * t:6 * t + 6]
                dq_ref, dk_ref, dv_ref = outs[3 * t:3 * t + 3]
                qv, kv_, dov = q_ref[...], k_ref[...], do_ref[...]
                s = lax.dot_general(qv, kv_, NT, preferred_element_type=F32) * (scale * LOG2_E)
                if diagonal:
                    s = jnp.where(_visible(tq, tq, unit), s, NEG_INF)
                p = jnp.exp2(s - jnp.tile(lse_ref[...], (1, tq // LANES)))
                dp = lax.dot_general(dov, v_ref[...], NT, preferred_element_type=F32)
                delta = jnp.sum(dov.astype(F32) * o_ref[...].astype(F32), axis=1, keepdims=True)
                ds = (p * (dp - delta) * scale).astype(BF16)
                dv_ref[...] += lax.dot_general(p.astype(BF16), dov, TN, preferred_element_type=F32)
                dk_ref[...] += lax.dot_general(ds, qv, TN, preferred_element_type=F32)
                rows = pl.ds(pl.multiple_of(i * tq, tq), tq)
                dq_ref[rows, :] += jnp.dot(ds, kv_, preferred_element_type=F32)

        @pl.when(i > j)
        def _():
            step(False)

        @pl.when(i == j)
        def _():
            step(True)

    def qmap(g, j, i):
        return (g % H, (g // H) * nq + jnp.maximum(i, j), 0)

    def kmap(g, j, i):
        return (g % H, (g // H) * nq + j, 0)

    def omap(g, j, i):
        return ((g // H) * nq + jnp.maximum(i, j), g % H)

    args = [a for st in streams for a in st[:6]]
    outs, landed = _call(
        body, args, name=name, grid=(B * H, nq, nq),
        in_specs=[pl.BlockSpec((None, tq, DK), qmap), pl.BlockSpec((None, tq, DK), kmap),
                  pl.BlockSpec((None, tq, DV), kmap), pl.BlockSpec((tq, DV), omap), pl.BlockSpec((tq, DV), omap),
                  pl.BlockSpec((None, tq, LANES), qmap)] * n,
        out_specs=[pl.BlockSpec((None, S, DK), lambda g, j, i: (g % H, g // H, 0)),
                   pl.BlockSpec((None, tq, DK), kmap), pl.BlockSpec((None, tq, DV), kmap)] * n,
        out_shape=[_sds((H, T, DK), F32), _sds((H, T, DK), F32), _sds((H, T, DV), F32)] * n,
        sem=("parallel", "arbitrary", "arbitrary"), comm=comm)
    return [tuple(outs[3 * t:3 * t + 3]) for t in range(n)], landed


def _gate_merge(am, af, proj, bgate, lay, D, comm=None):
    T = am.shape[0]
    tm = _tile(T, ROW_TILE, 16)
    tn = _tile(D, 1024)

    def body(am_ref, af_ref, gm_ref, gf_ref, bm_ref, bf_ref, o_ref):
        sm = _sigmoid(gm_ref[...] + bm_ref[...])
        sf = _sigmoid(gf_ref[...] + bf_ref[...])
        o_ref[...] = (sm * am_ref[...] + sf * af_ref[...]).astype(BF16)

    og = lay["g"] // tn
    blk = pl.BlockSpec((tm, tn), lambda i, j: (i, j))
    return _call(
        body, [am, af, proj, proj, bgate, bgate], name="gate_merge", grid=(T // tm, D // tn),
        in_specs=[blk, blk, pl.BlockSpec((tm, tn), lambda i, j: (i, og + j)),
                  pl.BlockSpec((tm, tn), lambda i, j: (i, og + D // tn + j)),
                  pl.BlockSpec((1, tn), lambda i, j: (0, j)), pl.BlockSpec((1, tn), lambda i, j: (0, D // tn + j))],
        out_specs=[blk], out_shape=[_sds((T, D), BF16)], sem=("parallel", "parallel"), comm=comm)


def _mid(x, y1, g_pm, g_ffn):
    T, D = x.shape
    tm = _tile(T, ROW_TILE, 16)

    def body(x_ref, y_ref, gp_ref, gf_ref, x1_ref, h2_ref):
        y = y_ref[...]
        x1 = x_ref[...] + y * _rms(y) * gp_ref[...]
        x1_ref[...] = x1
        h2_ref[...] = (x1 * _rms(x1) * gf_ref[...]).astype(BF16)

    row = pl.BlockSpec((tm, D), lambda i: (i, 0))
    vec = pl.BlockSpec((1, D), lambda i: (0, 0))
    return _call(body, [x, y1, g_pm, g_ffn], name="mid", grid=(T // tm,), in_specs=[row, row, vec, vec],
                 out_specs=[row, row], out_shape=[_sds((T, D), F32), _sds((T, D), BF16)], sem=("parallel",))[0]


def _conv3(u, w_ref, bias):
    row = lax.broadcasted_iota(jnp.int32, u.shape, 0)
    u1 = jnp.where(row >= 1, pltpu.roll(u, 1, 0), 0.0)
    u2 = jnp.where(row >= 2, pltpu.roll(u, 2, 0), 0.0)
    return w_ref[0:1, :] * u2 + w_ref[1:2, :] * u1 + w_ref[2:3, :] * u + bias, u1, u2


def _convffn_fwd(u, cw, cb, B, S, F, comm=None):
    T = u.shape[0]
    tn = _tile(F, 256)
    nf = F // tn

    def body(ug_ref, uv_ref, wg_ref, wv_ref, bg_ref, bv_ref, a_ref):
        g, _, _ = _conv3(ug_ref[...], wg_ref, bg_ref[...])
        val, _, _ = _conv3(uv_ref[...], wv_ref, bv_ref[...])
        a_ref[...] = (_gelu_parts(g)[0] * val).astype(BF16)

    def seq(off):
        return pl.BlockSpec((S, tn), lambda b, j: (b, off + j))

    def par(rows, off):
        return pl.BlockSpec((rows, tn), lambda b, j: (0, off + j))

    return _call(body, [u, u, cw, cw, cb, cb], name="convffn_fwd", grid=(B, nf),
                 in_specs=[seq(0), seq(nf), par(3, 0), par(3, nf), par(1, 0), par(1, nf)],
                 out_specs=[seq(0)], out_shape=[_sds((T, F), BF16)], sem=("parallel", "parallel"), comm=comm)


def _convffn_bwd(u, dact, cw, cb, B, S, F, comm=None):
    T = u.shape[0]
    tn = _tile(F, 256)
    nf = F // tn

    def body(ug_ref, uv_ref, da_ref, wg_ref, wv_ref, bg_ref, bv_ref, dug_ref, duv_ref, dpg_ref, dpv_ref):
        b = pl.program_id(1)
        ug, uv, da = ug_ref[...], uv_ref[...], da_ref[...]
        g, ug1, ug2 = _conv3(ug, wg_ref, bg_ref[...])
        val, uv1, uv2 = _conv3(uv, wv_ref, bv_ref[...])
        gel, dgel = _gelu_parts(g)
        dg = da * val * dgel
        dval = da * gel
        row = lax.broadcasted_iota(jnp.int32, ug.shape, 0)

        def back(d, w_ref):
            d1 = jnp.where(row < S - 1, pltpu.roll(d, S - 1, 0), 0.0)
            d2 = jnp.where(row < S - 2, pltpu.roll(d, S - 2, 0), 0.0)
            return w_ref[2:3, :] * d + w_ref[1:2, :] * d1 + w_ref[0:1, :] * d2

        dug_ref[...] = back(dg, wg_ref).astype(BF16)
        duv_ref[...] = back(dval, wv_ref).astype(BF16)

        def sums(d, u0, u1, u2):
            r8 = lax.broadcasted_iota(jnp.int32, (8, d.shape[1]), 0)
            out = jnp.zeros((8, d.shape[1]), F32)
            for k, t in enumerate((d * u2, d * u1, d * u0, d)):
                out = jnp.where(r8 == k, jnp.sum(t, axis=0, keepdims=True), out)
            return out

        _accumulate(dpg_ref, sums(dg, ug, ug1, ug2), b == 0)
        _accumulate(dpv_ref, sums(dval, uv, uv1, uv2), b == 0)

    def seq(off):
        return pl.BlockSpec((S, tn), lambda j, b: (b, off + j))

    def par(rows, off):
        return pl.BlockSpec((rows, tn), lambda j, b: (0, off + j))

    outs, landed = _call(
        body, [u, u, dact, cw, cw, cb, cb], name="convffn_bwd", grid=(nf, B),
        in_specs=[seq(0), seq(nf), seq(0), par(3, 0), par(3, nf), par(1, 0), par(1, nf)],
        out_specs=[seq(0), seq(0), par(8, 0), par(8, 0)],
        out_shape=[_sds((T, F), BF16), _sds((T, F), BF16), _sds((8, F), F32), _sds((8, F), F32)],
        sem=("parallel", "arbitrary"), comm=comm)
    return outs, landed


def _tail(ff, x1, tgt, g):
    T, D = ff.shape
    tm = _tile(T, ROW_TILE, 16)

    def body(ff_ref, x1_ref, t_ref, g_ref, dy_ref, dff_ref, loss_ref, dg_ref):
        i = pl.program_id(0)
        f = ff_ref[...]
        gv = g_ref[...]
        r = _rms(f)
        n = f * r
        e = (x1_ref[...] + n * gv) - t_ref[...]
        dy = e * (1.0 / D)
        dy_ref[...] = dy
        dn = dy * gv
        dff_ref[...] = (r * (dn - n * jnp.mean(dn * n, axis=-1, keepdims=True))).astype(BF16)
        part = 0.5 * jnp.sum(jnp.mean(e * e, axis=-1, keepdims=True), axis=0, keepdims=True)
        _accumulate(loss_ref, jnp.broadcast_to(part, loss_ref.shape), i == 0)
        _accumulate(dg_ref, jnp.sum(dy * n, axis=0, keepdims=True), i == 0)

    row = pl.BlockSpec((tm, D), lambda i: (i, 0))
    vec = pl.BlockSpec((1, D), lambda i: (0, 0))
    return _call(body, [ff, x1, tgt, g], name="tail", grid=(T // tm,), in_specs=[row, row, row, vec],
                 out_specs=[row, row, pl.BlockSpec((8, LANES), lambda i: (0, 0)), vec],
                 out_shape=[_sds((T, D), F32), _sds((T, D), BF16), _sds((8, LANES), F32), _sds((1, D), F32)],
                 sem=("arbitrary",))[0]


def _mid_bwd(dy, dh2, x1, y1, g_ffn, g_pm, comm=None):
    T, D = dy.shape
    tm = _tile(T, ROW_TILE, 16)

    def body(dy_ref, dh_ref, x1_ref, y1_ref, gf_ref, gp_ref, dx1_ref, dy1_ref, dgf_ref, dgp_ref):
        i = pl.program_id(0)
        dh = dh_ref[...]
        d2, dgf = _rms_bwd(dh, x1_ref[...], gf_ref[...])
        dx1 = dy_ref[...] + d2
        dx1_ref[...] = dx1
        d1, dgp = _rms_bwd(dx1, y1_ref[...], gp_ref[...])
        dy1_ref[...] = d1.astype(BF16)
        _accumulate(dgf_ref, jnp.sum(dgf, axis=0, keepdims=True), i == 0)
        _accumulate(dgp_ref, jnp.sum(dgp, axis=0, keepdims=True), i == 0)

    row = pl.BlockSpec((tm, D), lambda i: (i, 0))
    vec = pl.BlockSpec((1, D), lambda i: (0, 0))
    return _call(body, [dy, dh2, x1, y1, g_ffn, g_pm], name="mid_bwd", grid=(T // tm,),
                 in_specs=[row, row, row, row, vec, vec], out_specs=[row, row, vec, vec],
                 out_shape=[_sds((T, D), F32), _sds((T, D), BF16), _sds((1, D), F32), _sds((1, D), F32)],
                 sem=("arbitrary",), comm=comm)


def _gate_bwd(dm, am, af, proj, bgate, lay, D, comm=None):
    T = dm.shape[0]
    tm = _tile(T, ROW_TILE, 16)
    tn = _tile(D, 512)

    def body(dm_ref, am_ref, af_ref, gm_ref, gf_ref, bm_ref, bf_ref,
             dam_ref, daf_ref, dgm_ref, dgf_ref, dbm_ref, dbf_ref):
        i = pl.program_id(1)
        d = dm_ref[...]
        sm = _sigmoid(gm_ref[...] + bm_ref[...])
        sf = _sigmoid(gf_ref[...] + bf_ref[...])
        dam_ref[...] = (d * sm).astype(BF16)
        daf_ref[...] = (d * sf).astype(BF16)
        dgm = d * am_ref[...] * (sm * (1.0 - sm))
        dgf = d * af_ref[...] * (sf * (1.0 - sf))
        dgm_ref[...] = dgm.astype(BF16)
        dgf_ref[...] = dgf.astype(BF16)
        _accumulate(dbm_ref, jnp.sum(dgm, axis=0, keepdims=True), i == 0)
        _accumulate(dbf_ref, jnp.sum(dgf, axis=0, keepdims=True), i == 0)

    og = lay["g"] // tn
    blk = pl.BlockSpec((tm, tn), lambda j, i: (i, j))
    vec = pl.BlockSpec((1, tn), lambda j, i: (0, j))
    return _call(
        body, [dm, am, af, proj, proj, bgate, bgate], name="gate_bwd", grid=(D // tn, T // tm),
        in_specs=[blk, blk, blk, pl.BlockSpec((tm, tn), lambda j, i: (i, og + j)),
                  pl.BlockSpec((tm, tn), lambda j, i: (i, og + D // tn + j)),
                  vec, pl.BlockSpec((1, tn), lambda j, i: (0, D // tn + j))],
        out_specs=[blk, blk, blk, blk, vec, vec],
        out_shape=[_sds((T, D), BF16)] * 4 + [_sds((1, D), F32)] * 2, sem=("parallel", "arbitrary"), comm=comm)


def _mla_bwd_prep(dq, dk, dv, cosT, sinT, comm=None):
    H, T, _ = dq.shape
    tm = _tile(T, HEAD_ROW_TILE, 16)

    def body(dq_ref, dk_ref, dv_ref, cos_ref, sin_ref, dqr_ref, dkv_ref, dkpe_ref):
        h = pl.program_id(1)
        cs, sn = cos_ref[...], sin_ref[...]
        valid = _lane(cs.shape) < ROPE

        def unrope(d):
            d = jnp.where(valid, d, 0.0)
            return d * cs - _rope_rot(d) * sn

        dqv = dq_ref[...]
        dqr_ref[:, :NOPE] = dqv[:, :NOPE].astype(BF16)
        dqr_ref[:, NOPE:] = unrope(dqv[:, NOPE:]).astype(BF16)
        dkv_ = dk_ref[...]
        dkv_ref[:, :NOPE] = dkv_[:, :NOPE].astype(BF16)
        dkv_ref[:, NOPE:] = dv_ref[...].astype(BF16)
        _accumulate(dkpe_ref, unrope(dkv_[:, NOPE:]), h == 0)

    head = pl.BlockSpec((None, tm, ATT_DK), lambda i, h: (h, i, 0))
    tok = pl.BlockSpec((tm, LANES), lambda i, h: (i, 0))
    return _call(
        body, [dq, dk, dv, cosT, sinT], name="mla_bwd_prep", grid=(T // tm, H),
        in_specs=[head, head, pl.BlockSpec((None, tm, VDIM), lambda i, h: (h, i, 0)), tok, tok],
        out_specs=[head, head, tok],
        out_shape=[_sds((H, T, ATT_DK), BF16), _sds((H, T, ATT_DK), BF16), _sds((T, LANES), F32)],
        sem=("parallel", "arbitrary"), comm=comm)


def _fox_bwd_prep(dq, dk, proj, bfor, lay, B, S, inv_scale):
    H, T, _ = dq.shape

    def body(dq_ref, dk_ref, fl_ref, bf_ref, dfl_ref, dbf_ref, dc_sc):
        b, h = pl.program_id(0), pl.program_id(1)
        lane = _lane(dc_sc.shape)
        col = jnp.sum(jnp.where(lane == 0, dq_ref[...], 0.0) - jnp.where(lane == 3, dk_ref[...], 0.0),
                      axis=1, keepdims=True)

        @pl.when(h == 0)
        def _():
            dc_sc[...] = jnp.zeros(dc_sc.shape, F32)

        dc_sc[...] = jnp.where(lane == h, col, dc_sc[...])

        @pl.when(h == H - 1)
        def _():
            dlogf = _cumsum_rows(dc_sc[...] * inv_scale, reverse=True)
            z = fl_ref[...] + bf_ref[...]
            dz = jnp.where(lane < H, dlogf * (1.0 / (1.0 + jnp.exp(z))), 0.0)
            dfl_ref[...] = dz
            _accumulate(dbf_ref, jnp.sum(dz, axis=0, keepdims=True), b == 0)

    aug = pl.BlockSpec((None, S, LANES), lambda b, h: (h, b, 1))
    seq = pl.BlockSpec((S, LANES), lambda b, h: (b, 0))
    vec = pl.BlockSpec((1, LANES), lambda b, h: (0, 0))
    return _call(
        body, [dq, dk, proj, bfor], name="fox_bwd_prep", grid=(B, H),
        in_specs=[aug, aug, pl.BlockSpec((S, LANES), lambda b, h: (b, lay["fl"] // LANES)), vec],
        out_specs=[seq, vec], out_shape=[_sds((T, LANES), F32), _sds((1, LANES), F32)],
        scratch_shapes=[pltpu.VMEM((S, LANES), F32)], sem=("arbitrary", "arbitrary"))[0]


def _heads_to_cols(dq, dk, dv):
    H, T, _ = dq.shape
    tm = _tile(T, HEAD_ROW_TILE, 16)

    def body(a_ref, b_ref, c_ref, ao_ref, bo_ref, co_ref):
        ao_ref[...] = a_ref[...].astype(BF16)
        bo_ref[...] = b_ref[...].astype(BF16)
        co_ref[...] = c_ref[...].astype(BF16)

    src = pl.BlockSpec((None, tm, FOX_DIM), lambda i, h: (h, i, 0))
    dst = pl.BlockSpec((tm, FOX_DIM), lambda i, h: (i, h))
    return _call(body, [dq, dk, dv], name="heads_to_cols", grid=(T // tm, H), in_specs=[src, src, src],
                 out_specs=[dst, dst, dst], out_shape=[_sds((T, H * FOX_DIM), BF16)] * 3,
                 sem=("parallel", "parallel"))[0]


def _lat_bwd(dqn, dkvn, proj, gq, gkv, lay):
    T = dqn.shape[0]
    tm = _tile(T, ROW_TILE, 16)

    def body(dq_ref, dkv_ref, q_ref, kv_ref, gq_ref, gkv_ref, dql_ref, dkl_ref, dgq_ref, dgkv_ref):
        i = pl.program_id(0)
        dql, dgq = _rms_bwd(dq_ref[...], q_ref[...], gq_ref[...])
        dkl, dgkv = _rms_bwd(dkv_ref[...], kv_ref[...], gkv_ref[...])
        dql_ref[...] = dql.astype(BF16)
        dkl_ref[...] = dkl.astype(BF16)
        _accumulate(dgq_ref, jnp.sum(dgq, axis=0, keepdims=True), i == 0)
        _accumulate(dgkv_ref, jnp.sum(dgkv, axis=0, keepdims=True), i == 0)

    def blk(width, off=0):
        return pl.BlockSpec((tm, width), lambda i: (i, off // width))

    def vec(width):
        return pl.BlockSpec((1, width), lambda i: (0, 0))

    return _call(
        body, [dqn, dkvn, proj, proj, gq, gkv], name="lat_bwd", grid=(T // tm,),
        in_specs=[blk(Q_LORA), blk(KV_LORA), blk(Q_LORA, lay["q"]), blk(KV_LORA, lay["kv"]), vec(Q_LORA), vec(KV_LORA)],
        out_specs=[blk(Q_LORA), blk(KV_LORA), vec(Q_LORA), vec(KV_LORA)],
        out_shape=[_sds((T, Q_LORA), BF16), _sds((T, KV_LORA), BF16), _sds((1, Q_LORA), F32), _sds((1, KV_LORA), F32)],
        sem=("arbitrary",))[0]


def _final_dx(dx1, dh, x, g, comm=None):
    T, D = x.shape
    tm = _tile(T, ROW_TILE, 16)

    def body(dx1_ref, dh_ref, x_ref, g_ref, dx_ref, dg_ref):
        i = pl.program_id(0)
        d, dg = _rms_bwd(dh_ref[...], x_ref[...], g_ref[...])
        dx_ref[...] = dx1_ref[...] + d
        _accumulate(dg_ref, jnp.sum(dg, axis=0, keepdims=True), i == 0)

    row = pl.BlockSpec((tm, D), lambda i: (i, 0))
    vec = pl.BlockSpec((1, D), lambda i: (0, 0))
    return _call(body, [dx1, dh, x, g], name="final_dx", grid=(T // tm,), in_specs=[row, row, row, vec],
                 out_specs=[row, vec], out_shape=[_sds((T, D), F32), _sds((1, D), F32)], sem=("arbitrary",), comm=comm)


def _chip_sum(pieces, paired, qc, name):
    G, R, C = pieces.shape
    tr = _tile(R, 256, 16)

    def body(qc_ref, g_ref, p_ref, keep_ref, send_ref):
        s = pl.program_id(1)
        tot = g_ref[...] + p_ref[...]

        @pl.when(s == 0)
        def _():
            keep_ref[...] = tot

        @pl.when(s > 0)
        def _():
            send_ref[...] = tot.astype(send_ref.dtype)

    grid_spec = pltpu.PrefetchScalarGridSpec(
        num_scalar_prefetch=1, grid=(R // tr, N_CHIP),
        in_specs=[pl.BlockSpec((None, tr, C), lambda i, s, qc: (2 * (qc[0] ^ s) + qc[1], i, 0)),
                  pl.BlockSpec((None, tr, C), lambda i, s, qc: (qc[0] ^ s, i, 0))],
        out_specs=[pl.BlockSpec((tr, C), lambda i, s, qc: (i, 0)),
                   pl.BlockSpec((None, tr, C), lambda i, s, qc: (jnp.maximum(s - 1, 0), i, 0))])
    send_dtype = BF16 if R >= 16 else pieces.dtype
    return pl.pallas_call(
        body, name=name, grid_spec=grid_spec,
        out_shape=[_sds((R, C), F32), _sds((3, R, C), send_dtype)],
        compiler_params=pltpu.CompilerParams(dimension_semantics=("arbitrary", "arbitrary"),
                                             vmem_limit_bytes=VMEM_LIMIT_BYTES),
    )(qc, pieces, paired)


def _adamw_math(w, g, m, v):
    m = ADAM_B1 * m + (1.0 - ADAM_B1) * g
    v = ADAM_B2 * v + (1.0 - ADAM_B2) * (g * g)
    m_hat = m / (1.0 - ADAM_B1 ** ADAM_STEP)
    v_hat = v / (1.0 - ADAM_B2 ** ADAM_STEP)
    delta = -ADAM_LR * (m_hat / (jnp.sqrt(v_hat) + ADAM_EPS) + ADAM_WD * w)
    return delta, m, v


def _sum_adamw(keep, pieces, w, m, v, name):
    R, C = w.shape
    P = pieces.shape[0]
    tr = _tile(R, 256, 16)

    def body(*refs):
        if keep is None:
            p_ref, w_ref, m_ref, v_ref, g_ref, d_ref, mo_ref, vo_ref = refs
            g = p_ref[0].astype(F32)
            rest = range(1, P)
        else:
            k_ref, p_ref, w_ref, m_ref, v_ref, g_ref, d_ref, mo_ref, vo_ref = refs
            g = k_ref[...]
            rest = range(P)
        for q in rest:
            g = g + p_ref[q].astype(F32)
        g_ref[...] = g
        d_ref[...], mo_ref[...], vo_ref[...] = _adamw_math(w_ref[...], g, m_ref[...], v_ref[...])

    blk = pl.BlockSpec((tr, C), lambda i: (i, 0))
    pblk = pl.BlockSpec((P, tr, C), lambda i: (0, i, 0))
    args = [pieces, w, m, v] if keep is None else [keep, pieces, w, m, v]
    specs = [pblk, blk, blk, blk] if keep is None else [blk, pblk, blk, blk, blk]
    return _call(body, args, name=name, grid=(R // tr,), in_specs=specs, out_specs=[blk] * 4,
                 out_shape=[_sds((R, C), F32)] * 4, sem=("parallel",))[0]


def _layout(D):
    lay = {"q": 0, "kv": Q_LORA, "kpe": Q_LORA + KV_LORA}
    lay["fq"] = lay["kpe"] + LANES
    lay["fk"] = lay["fq"] + HEADS * FOX_DIM
    lay["fv"] = lay["fk"] + HEADS * FOX_DIM
    lay["fl"] = lay["fv"] + HEADS * FOX_DIM
    lay["g"] = lay["fl"] + LANES
    lay["end"] = lay["g"] + 2 * D
    return lay


def kernel(x, positions, pre_mix_norm, w_in, q_a_norm, w_uq, kv_a_norm, w_ukv, b_forget, b_gate, w_branch_mla, w_branch_fox, w_out, post_mix_norm, pre_ffn_norm, w_up, conv_w, conv_b, w_down, post_ffn_norm, loss_target, m_pre_mix_norm, m_w_in, m_q_a_norm, m_w_uq, m_kv_a_norm, m_w_ukv, m_b_forget, m_b_gate, m_w_branch_mla, m_w_branch_fox, m_w_out, m_post_mix_norm, m_pre_ffn_norm, m_w_up, m_conv_w, m_conv_b, m_w_down, m_post_ffn_norm, v_pre_mix_norm, v_w_in, v_q_a_norm, v_w_uq, v_kv_a_norm, v_w_ukv, v_b_forget, v_b_gate, v_w_branch_mla, v_w_branch_fox, v_w_out, v_post_mix_norm, v_pre_ffn_norm, v_w_up, v_conv_w, v_conv_b, v_w_down, v_post_ffn_norm):
    B, S, D = x.shape
    T = B * S
    F = conv_b.shape[0] // 2
    lay = _layout(D)
    n_in = w_in.shape[1]
    d_in = N_DEV * n_in
    seg_a = Q_LORA + KV_LORA + ROPE
    seg_b = 3 * HEADS * FOX_DIM + HEADS
    mla_scale = (NOPE + ROPE) ** -0.5
    fox_scale = FOX_DIM ** -0.5
    ax, ay, ac = (lax.axis_index(a) for a in MESH_AXES)
    qc = jnp.stack([2 * ax + ay, ac]).astype(jnp.int32)

    def row(vec, width=None):
        vec = vec.reshape(1, -1)
        if width is not None and vec.shape[1] < width:
            vec = jnp.pad(vec, ((0, 0), (0, width - vec.shape[1])))
        return vec

    win_s = _cast_bf16(w_in, "cast_w_in")
    (win_g,) = _exchange_alone(_Comm([_GatherPlan([win_s], mid_frac=0.0)]), "gather_w_in")
    small_s = [_cast_bf16(w, "cast_" + n) for w, n in
               [(w_uq, "w_uq"), (w_ukv, "w_ukv"), (w_branch_mla, "w_branch_mla"), (w_branch_fox, "w_branch_fox"), (w_out, "w_out")]]
    wup_s = _cast_bf16(w_up, "cast_w_up")
    wdown_s = _cast_bf16(w_down, "cast_w_down")

    win_full = jnp.transpose(win_g, (1, 0, 2)).reshape(D, d_in)
    w_perm = jnp.concatenate(
        [win_full[:, :seg_a], jnp.zeros((D, LANES - ROPE), BF16), win_full[:, seg_a:seg_a + seg_b],
         jnp.zeros((D, LANES - HEADS), BF16), win_full[:, seg_a + seg_b:]], axis=1)

    x2 = x.reshape(T, D)
    tgt = loss_target.reshape(T, D)
    pos = positions.reshape(T, 1)
    inv_freq = 1.0 / (ROPE_THETA ** (jnp.arange(0, ROPE, 2, dtype=F32) / ROPE))
    invf = row(jnp.concatenate([inv_freq, inv_freq]), LANES)
    g_pre, g_q, g_kv = row(pre_mix_norm), row(q_a_norm), row(kv_a_norm)
    g_pm, g_ffn, g_pf = row(post_mix_norm), row(pre_ffn_norm), row(post_ffn_norm)
    bfor = row(b_forget, LANES)
    bgate = row(b_gate)
    cb_full = row(conv_b)

    def own_plan(blocks):
        return _Comm([_GatherOwnPlan(blocks)])

    def pass_plan(gathered):
        return _Comm([_GatherPassPlan(gathered)])

    def pair_plan(gs):
        return _Comm([_PairScatterPlan(gs)])

    def chip_plan(gs):
        return _Comm([_ChipScatterPlan(gs)])

    h = _prenorm(x2, g_pre)
    proj, small_g = _matmul(h, w_perm, mode="nn", name="mm_proj", comm=own_plan(small_s + [conv_w]))
    (qn, kvn, kper, logf, cosT, sinT), (wuq_g, wukv_g, wbm_g, wbf_g, wout_g, cw_g) = _split_prep(
        proj, pos, invf, g_q, g_kv, bfor, lay, comm=pass_plan(small_g))
    wuq_pad = jnp.pad(wuq_g, ((0, 0), (0, 0), (0, ATT_DK - NOPE - ROPE)))
    wbm = jnp.transpose(wbm_g, (1, 0, 2)).reshape(HEADS * VDIM, D)
    wbf = jnp.transpose(wbf_g, (1, 0, 2)).reshape(HEADS * FOX_DIM, D)
    wout = wout_g.reshape(D, D)
    cw_full = jnp.transpose(cw_g, (1, 0, 2)).reshape(3, 2 * F)

    qraw = _matmul(qn, wuq_pad, mode="nn", name="mm_q", out_blocks=ATT_DK)
    kvraw = _matmul(kvn, wukv_g, mode="nn", name="mm_kv", out_blocks=NOPE + VDIM)
    q_mla, k_mla, v_mla = _mla_prep(qraw, kvraw, kper, cosT, sinT)
    cs = _fox_cumsum(logf, B, S, 1.0 / fox_scale)
    q_fox, k_fox, v_fox = _fox_prep(proj, cs, lay)
    ((o_mla, lse_mla), (o_fox, lse_fox)), wup_half = _attn_fwd(
        [(q_mla, k_mla, v_mla, MLA_UNIT, mla_scale), (q_fox, k_fox, v_fox, 1, fox_scale)], B=B, S=S,
        name="attn_fwd", comm=own_plan([wup_s]))
    a_m = _matmul(o_mla, wbm, mode="nn", name="mm_branch_mla")
    a_f = _matmul(o_fox, wbf, mode="nn", name="mm_branch_fox")
    (merged,), (wup_g,) = _gate_merge(a_m, a_f, proj, bgate, lay, D, comm=pass_plan(wup_half))
    n_up = wup_g.shape[2]
    y1 = _matmul(merged, wout, mode="nn", name="mm_out")
    x1, h2 = _mid(x2, y1, g_pm, g_ffn)
    u, wdown_half = _matmul(h2, wup_g, mode="nn", name="mm_up", tn=n_up, comm=own_plan([wdown_s]))
    (act,), (wdown_g,) = _convffn_fwd(u, cw_full, cb_full, B, S, F, comm=pass_plan(wdown_half))
    wdown = wdown_g.reshape(F, D)
    ff = _matmul(act, wdown, mode="nn", name="mm_down", tk=F // 2)
    dy, dff, loss_part, dg_pf = _tail(ff, x1, tgt, g_pf)

    dact = _matmul(dff, wdown, mode="nt", name="mm_dact")
    dw_down = _matmul(act, dff, mode="tn", name="mm_dw_down", tm=512).reshape(N_DEV, F // N_DEV, D)
    (du_g, du_v, dcp_g, dcp_v), (pa_down,) = _convffn_bwd(u, dact, cw_full, cb_full, B, S, F, comm=pair_plan([dw_down]))
    keep_down, sb_down = _chip_sum(dw_down, pa_down, qc, "chipsum_w_down")
    du = jnp.concatenate([du_g, du_v], axis=1)
    dh2, (rb_down,) = _matmul(du, wup_g, mode="nt", name="mm_dh2", comm=chip_plan([sb_down]))
    dw_up = _matmul(h2, du, mode="tn", name="mm_dw_up", out_blocks=n_up, tm=512, tn=n_up)
    (dx1, dy1, dg_ffn, dg_pm), _ = _mid_bwd(dy, dh2, x1, y1, g_ffn, g_pm)
    dmerged = _matmul(dy1, wout, mode="nt", name="mm_dmerged")
    dw_out = _matmul(merged, dy1, mode="tn", name="mm_dw_out").reshape(N_DEV, D // N_DEV, D)
    (da_m, da_f, dgl_m, dgl_f, dbg_m, dbg_f), (pa_up,) = _gate_bwd(dmerged, a_m, a_f, proj, bgate, lay, D,
                                                                   comm=pair_plan([dw_up]))
    keep_up, sb_up = _chip_sum(dw_up, pa_up, qc, "chipsum_w_up")
    dw_bm = _matmul(o_mla, da_m, mode="tn", name="mm_dw_branch_mla", out_blocks=D // N_DEV)
    dw_bf = _matmul(o_fox, da_f, mode="tn", name="mm_dw_branch_fox", out_blocks=D // N_DEV)
    mix = [dw_out, dw_bm, dw_bf]
    do_mla, pa_mix = _matmul(da_m, wbm, mode="nt", name="mm_do_mla", out_dtype=BF16, comm=pair_plan(mix))
    do_fox = _matmul(da_f, wbf, mode="nt", name="mm_do_fox", out_dtype=BF16)
    mix_sums = [_chip_sum(g, p, qc, "chipsum_" + n) for g, p, n in zip(mix, pa_mix, ["w_out", "w_branch_mla", "w_branch_fox"])]
    ((dq_m, dk_m, dv_m), (dq_f, dk_f, dv_f)), (rb_up,) = _attn_bwd(
        [(q_mla, k_mla, v_mla, o_mla, do_mla, lse_mla, MLA_UNIT, mla_scale),
         (q_fox, k_fox, v_fox, o_fox, do_fox, lse_fox, 1, fox_scale)], B=B, S=S, name="attn_bwd",
        comm=chip_plan([sb_up]))
    (dqraw, dkvraw, dkpe), rb_mix = _mla_bwd_prep(dq_m, dk_m, dv_m, cosT, sinT, comm=chip_plan([s[1] for s in mix_sums]))
    dqn = _matmul(dqraw, wuq_pad, mode="nt", name="mm_dqn")
    dw_uq = _matmul(qn, dqraw, mode="tn", name="mm_dw_uq", out_blocks=ATT_DK)[:, :, :NOPE + ROPE]
    dkvn = _matmul(dkvraw, wukv_g, mode="nt", name="mm_dkvn")
    dw_ukv = _matmul(kvn, dkvraw, mode="tn", name="mm_dw_ukv", out_blocks=NOPE + VDIM)
    dqlat, dkvlat, dg_q, dg_kv = _lat_bwd(dqn, dkvn, proj, g_q, g_kv, lay)
    dfl, dbfor = _fox_bwd_prep(dq_f, dk_f, proj, bfor, lay, B, S, 1.0 / fox_scale)
    dfq, dfk, dfv = _heads_to_cols(dq_f, dk_f, dv_f)
    dproj = jnp.concatenate([dqlat, dkvlat, dkpe.astype(BF16), dfq, dfk, dfv, dfl.astype(BF16), dgl_m, dgl_f], axis=1)
    dw_perm = _matmul(h, dproj, mode="tn", name="mm_dw_in")
    dw_in_full = jnp.concatenate(
        [dw_perm[:, :seg_a], dw_perm[:, lay["fq"]:lay["fq"] + seg_b], dw_perm[:, lay["g"]:]], axis=1)
    dw_in = jnp.transpose(dw_in_full.reshape(D, N_DEV, n_in), (1, 0, 2))
    dcw = jnp.transpose(jnp.concatenate([dcp_g[0:3], dcp_v[0:3]], axis=1).reshape(3, N_DEV, (2 * F) // N_DEV), (1, 0, 2))
    late = [dw_in, dw_uq, dw_ukv, dcw]
    pa_late = _exchange_alone(pair_plan(late), "pair_late")
    late_sums = [_chip_sum(g, p, qc, "chipsum_" + n) for g, p, n in zip(late, pa_late, ["w_in", "w_uq", "w_ukv", "conv_w"])]
    dh, rb_late = _matmul(dproj, w_perm, mode="nt", name="mm_dh", tk=2048, comm=chip_plan([s[1] for s in late_sums]))
    (grad_x, dg_pre), _ = _final_dx(dx1, dh, x2, g_pre)

    big_out = {}

    def finish(n, keep, pieces, w, m, v):
        big_out[n] = _sum_adamw(keep, pieces, w, m, v, "adamw_" + n)

    finish("w_down", keep_down, rb_down, w_down, m_w_down, v_w_down)
    finish("w_up", keep_up, rb_up, w_up, m_w_up, v_w_up)
    finish("w_out", mix_sums[0][0], rb_mix[0], w_out, m_w_out, v_w_out)
    finish("w_branch_mla", mix_sums[1][0], rb_mix[1], w_branch_mla, m_w_branch_mla, v_w_branch_mla)
    finish("w_branch_fox", mix_sums[2][0], rb_mix[2], w_branch_fox, m_w_branch_fox, v_w_branch_fox)
    finish("w_in", late_sums[0][0], rb_late[0], w_in, m_w_in, v_w_in)
    finish("w_uq", late_sums[1][0], rb_late[1], w_uq, m_w_uq, v_w_uq)
    finish("w_ukv", late_sums[2][0], rb_late[2], w_ukv, m_w_ukv, v_w_ukv)
    finish("conv_w", late_sums[3][0], rb_late[3], conv_w, m_conv_w, v_conv_w)

    widths = [D, Q_LORA, KV_LORA, LANES, 2 * D, D, D, 2 * F, D]
    small_names = ["pre_mix_norm", "q_a_norm", "kv_a_norm", "b_forget", "b_gate", "post_mix_norm", "pre_ffn_norm",
                   "conv_b", "post_ffn_norm"]
    true_w = [D, Q_LORA, KV_LORA, HEADS, 2 * D, D, D, 2 * F, D]
    dcb = jnp.concatenate([dcp_g[3:4], dcp_v[3:4]], axis=1)
    part = jnp.concatenate([dg_pre, dg_q, dg_kv, dbfor, dbg_m, dbg_f, dg_pm, dg_ffn, dcb, dg_pf], axis=1)

    def pack(vals):
        return jnp.concatenate([row(a, wd) for a, wd in zip(vals, widths)], axis=1)

    sw = pack([pre_mix_norm, q_a_norm, kv_a_norm, b_forget, b_gate, post_mix_norm, pre_ffn_norm, conv_b, post_ffn_norm])
    sm = pack([m_pre_mix_norm, m_q_a_norm, m_kv_a_norm, m_b_forget, m_b_gate, m_post_mix_norm, m_pre_ffn_norm,
               m_conv_b, m_post_ffn_norm])
    sv = pack([v_pre_mix_norm, v_q_a_norm, v_kv_a_norm, v_b_forget, v_b_gate, v_post_mix_norm, v_pre_ffn_norm,
               v_conv_b, v_post_ffn_norm])
    (parts_all,) = _exchange_alone(_Comm([_DirectGatherPlan([part])]), "gather_small")
    sg, sd, smo, svo = _sum_adamw(None, parts_all, sw, sm, sv, "adamw_small")
    small_out = {}
    off = 0
    for n, wd, tw in zip(small_names, widths, true_w):
        small_out[n] = tuple(a[0, off:off + tw] for a in (sg, sd, smo, svo))
        off += wd

    loss = lax.psum(loss_part[0, 0], MESH_AXES)
    order = ["pre_mix_norm", "w_in", "q_a_norm", "w_uq", "kv_a_norm", "w_ukv", "b_forget", "b_gate", "w_branch_mla",
             "w_branch_fox", "w_out", "post_mix_norm", "pre_ffn_norm", "w_up", "conv_w", "conv_b", "w_down",
             "post_ffn_norm"]
    res = {**big_out, **small_out}
    outs = [loss, grad_x.reshape(B, S, D)]
    for kind in range(4):
        outs += [res[n][kind] for n in order]
    return tuple(outs)
```

```python
import math

import jax
import jax.numpy as jnp
from jax import lax
from jax.experimental import pallas as pl
from jax.experimental.pallas import tpu as pltpu

F32 = jnp.float32
BF16 = jnp.bfloat16

N_DEV = 8
N_CHIP = 4
HEADS = 8
NOPE = 128
ROPE = 64
HALF_ROPE = ROPE // 2
VDIM = 128
Q_LORA = 512
KV_LORA = 256
FOX_DIM = 128
ATT_DK = 256
MLA_UNIT = 64
ROPE_THETA = 10000.0
EPS = 1e-6
NEG_INF = -1e30
LANES = 128
LOG2_E = 1.4426950408889634

ADAM_LR = 0.001
ADAM_B1 = 0.9
ADAM_B2 = 0.999
ADAM_EPS = 1e-08
ADAM_WD = 0.01
ADAM_STEP = 10

VMEM_LIMIT_BYTES = 56 * 1024 * 1024
ROW_TILE = 256
HEAD_ROW_TILE = 1024
ATT_TILE = 512
ATT_SUB = 256
ATT_AHEAD = 3
MM_TILE = 1024

MESH_AXES = ("x", "y", "c")
ANY = pl.BlockSpec(memory_space=pl.ANY)


def _tile(n, pref, align=LANES):
    if n <= pref:
        return n
    t = (pref // align) * align
    while t >= align:
        if n % t == 0:
            return t
        t -= align
    return n


def _sds(shape, dtype):
    return jax.ShapeDtypeStruct(shape, dtype)


def _coords():
    x, y, c = (lax.axis_index(ax) for ax in MESH_AXES)
    return x, y, c


def _chip_rel(x, y, r):
    return (1 - x if r & 2 else x), (1 - y if r & 1 else y)


def _rcopy(src, dst, sems, w, k, dev):
    return pltpu.make_async_remote_copy(src_ref=src, dst_ref=dst, send_sem=sems[0].at[w, k], recv_sem=sems[1].at[w, k],
                                        device_id=dev, device_id_type=pl.DeviceIdType.MESH)


class _GatherPlan:
    def __init__(self, blocks, mid_frac=0.5):
        self.ins = list(blocks)
        self.out_shapes = [_sds((N_DEV,) + b.shape, b.dtype) for b in blocks]
        n = len(blocks)
        self.scratch = [pltpu.SemaphoreType.DMA((n, 7)), pltpu.SemaphoreType.DMA((n, 7)), pltpu.SemaphoreType.DMA((n,))]
        self.mid_frac = mid_frac

    def first(self, ins, outs, sems):
        x, y, c = _coords()
        me = 4 * x + 2 * y + c
        for w in range(len(ins)):
            pltpu.make_async_copy(ins[w], outs[w].at[me], sems[2].at[w]).start()
            _rcopy(ins[w], outs[w].at[me], sems, w, 0, (x, y, 1 - c)).start()
            for r in (1, 2, 3):
                px, py = _chip_rel(x, y, r)
                _rcopy(ins[w], outs[w].at[me], sems, w, r, (px, py, c)).start()

    def mid(self, ins, outs, sems):
        x, y, c = _coords()
        for w in range(len(ins)):
            for r in (1, 2, 3):
                px, py = _chip_rel(x, y, r)
                blk = outs[w].at[4 * px + 2 * py + c]
                _rcopy(ins[w], blk, sems, w, r, (px, py, c)).wait_recv()
                _rcopy(blk, blk, sems, w, 3 + r, (x, y, 1 - c)).start()

    def last(self, ins, outs, sems):
        x, y, c = _coords()
        me = 4 * x + 2 * y + c
        sib = (x, y, 1 - c)
        for w in range(len(ins)):
            _rcopy(ins[w], outs[w].at[4 * x + 2 * y + 1 - c], sems, w, 0, sib).wait_recv()
            for r in (1, 2, 3):
                px, py = _chip_rel(x, y, r)
                blk = outs[w].at[4 * px + 2 * py + 1 - c]
                _rcopy(blk, blk, sems, w, 3 + r, sib).wait_recv()
            for k in range(7):
                _rcopy(ins[w], outs[w].at[me], sems, w, k, sib).wait_send()
            pltpu.make_async_copy(ins[w], outs[w].at[me], sems[2].at[w]).wait()


class _GatherOwnPlan:
    mid = None

    def __init__(self, blocks):
        self.ins = list(blocks)
        self.out_shapes = [_sds((N_DEV,) + b.shape, b.dtype) for b in blocks]
        n = len(blocks)
        self.scratch = [pltpu.SemaphoreType.DMA((n, 4)), pltpu.SemaphoreType.DMA((n, 4)), pltpu.SemaphoreType.DMA((n,))]

    def first(self, ins, outs, sems):
        x, y, c = _coords()
        me = 4 * x + 2 * y + c
        for w in range(len(ins)):
            pltpu.make_async_copy(ins[w], outs[w].at[me], sems[2].at[w]).start()
            _rcopy(ins[w], outs[w].at[me], sems, w, 0, (x, y, 1 - c)).start()
            for r in (1, 2, 3):
                px, py = _chip_rel(x, y, r)
                _rcopy(ins[w], outs[w].at[me], sems, w, r, (px, py, c)).start()

    def last(self, ins, outs, sems):
        x, y, c = _coords()
        me = 4 * x + 2 * y + c
        for w in range(len(ins)):
            cp = _rcopy(ins[w], outs[w].at[4 * x + 2 * y + 1 - c], sems, w, 0, (x, y, 1 - c))
            cp.wait_recv()
            cp.wait_send()
            for r in (1, 2, 3):
                px, py = _chip_rel(x, y, r)
                cp = _rcopy(ins[w], outs[w].at[4 * px + 2 * py + c], sems, w, r, (px, py, c))
                cp.wait_recv()
                cp.wait_send()
            pltpu.make_async_copy(ins[w], outs[w].at[me], sems[2].at[w]).wait()


class _GatherPassPlan:
    mid = None

    def __init__(self, gathered):
        self.ins = list(gathered)
        self.out_shapes = [_sds(g.shape, g.dtype) for g in gathered]
        self.aliases = [(i, i) for i in range(len(gathered))]
        n = len(gathered)
        self.scratch = [pltpu.SemaphoreType.DMA((n, 3)), pltpu.SemaphoreType.DMA((n, 3))]

    def first(self, ins, outs, sems):
        x, y, c = _coords()
        for w in range(len(ins)):
            for r in (1, 2, 3):
                px, py = _chip_rel(x, y, r)
                blk = 4 * px + 2 * py + c
                _rcopy(ins[w].at[blk], outs[w].at[blk], sems, w, r - 1, (x, y, 1 - c)).start()

    def last(self, ins, outs, sems):
        x, y, c = _coords()
        for w in range(len(ins)):
            for r in (1, 2, 3):
                px, py = _chip_rel(x, y, r)
                blk = 4 * px + 2 * py + 1 - c
                cp = _rcopy(ins[w].at[blk], outs[w].at[blk], sems, w, r - 1, (x, y, 1 - c))
                cp.wait_recv()
                cp.wait_send()


class _DirectGatherPlan:
    mid = None

    def __init__(self, blocks):
        self.ins = list(blocks)
        self.out_shapes = [_sds((N_DEV,) + b.shape, b.dtype) for b in blocks]
        n = len(blocks)
        self.scratch = [pltpu.SemaphoreType.DMA((n, 7)), pltpu.SemaphoreType.DMA((n, 7)), pltpu.SemaphoreType.DMA((n,))]

    @staticmethod
    def _peer(x, y, c, r):
        return (1 - x if r & 4 else x), (1 - y if r & 2 else y), (1 - c if r & 1 else c)

    def first(self, ins, outs, sems):
        x, y, c = _coords()
        me = 4 * x + 2 * y + c
        for w in range(len(ins)):
            pltpu.make_async_copy(ins[w], outs[w].at[me], sems[2].at[w]).start()
            for r in range(1, N_DEV):
                _rcopy(ins[w], outs[w].at[me], sems, w, r - 1, self._peer(x, y, c, r)).start()

    def last(self, ins, outs, sems):
        x, y, c = _coords()
        me = 4 * x + 2 * y + c
        for w in range(len(ins)):
            for r in range(1, N_DEV):
                px, py, pc = self._peer(x, y, c, r)
                cp = _rcopy(ins[w], outs[w].at[4 * px + 2 * py + pc], sems, w, r - 1, (px, py, pc))
                cp.wait_recv()
                cp.wait_send()
            pltpu.make_async_copy(ins[w], outs[w].at[me], sems[2].at[w]).wait()


class _PairScatterPlan:
    mid = None

    def __init__(self, pieces):
        self.ins = list(pieces)
        self.out_shapes = [_sds((N_CHIP,) + p.shape[1:], p.dtype) for p in pieces]
        n = len(pieces)
        self.scratch = [pltpu.SemaphoreType.DMA((n, N_CHIP)), pltpu.SemaphoreType.DMA((n, N_CHIP))]

    def _copies(self, ins, outs, sems):
        x, y, c = _coords()
        return [_rcopy(ins[w].at[2 * q + 1 - c], outs[w].at[q], sems, w, q, (x, y, 1 - c))
                for w in range(len(ins)) for q in range(N_CHIP)]

    def first(self, ins, outs, sems):
        for cp in self._copies(ins, outs, sems):
            cp.start()

    def last(self, ins, outs, sems):
        for cp in self._copies(ins, outs, sems):
            cp.wait_recv()
            cp.wait_send()


class _ChipScatterPlan:
    mid = None

    def __init__(self, sums):
        self.ins = list(sums)
        self.out_shapes = [_sds(s.shape, s.dtype) for s in sums]
        n = len(sums)
        self.scratch = [pltpu.SemaphoreType.DMA((n, 3)), pltpu.SemaphoreType.DMA((n, 3))]

    def _copies(self, ins, outs, sems):
        x, y, c = _coords()
        cps = []
        for w in range(len(ins)):
            for r in (1, 2, 3):
                px, py = _chip_rel(x, y, r)
                cps.append(_rcopy(ins[w].at[r - 1], outs[w].at[r - 1], sems, w, r - 1, (px, py, c)))
        return cps

    def first(self, ins, outs, sems):
        for cp in self._copies(ins, outs, sems):
            cp.start()

    def last(self, ins, outs, sems):
        for cp in self._copies(ins, outs, sems):
            cp.wait_recv()
            cp.wait_send()


class _Comm:
    def __init__(self, plans):
        self.plans = list(plans)
        self.ins = [a for p in self.plans for a in p.ins]
        self.out_shapes = [s for p in self.plans for s in p.out_shapes]
        self.scratch = [s for p in self.plans for s in p.scratch]
        self.aliases = []
        i = o = 0
        for p in self.plans:
            self.aliases += [(i + a, o + b) for a, b in getattr(p, "aliases", [])]
            i, o = i + len(p.ins), o + len(p.out_shapes)

    def _parts(self, ins, outs, sems):
        i = o = s = 0
        for p in self.plans:
            yield p, ins[i:i + len(p.ins)], outs[o:o + len(p.out_shapes)], sems[s:s + len(p.scratch)]
            i, o, s = i + len(p.ins), o + len(p.out_shapes), s + len(p.scratch)

    def begin(self, step, nsteps, ins, outs, sems):
        @pl.when(step == 0)
        def _():
            for p, pi, po, ps in self._parts(ins, outs, sems):
                p.first(pi, po, ps)

        for p, pi, po, ps in self._parts(ins, outs, sems):
            if p.mid is not None:
                @pl.when(step == min(nsteps - 1, int(p.mid_frac * nsteps)))
                def _(p=p, pi=pi, po=po, ps=ps):
                    p.mid(pi, po, ps)

    def end(self, step, nsteps, ins, outs, sems):
        @pl.when(step == nsteps - 1)
        def _():
            for p, pi, po, ps in self._parts(ins, outs, sems):
                p.last(pi, po, ps)


def _call(body, args, *, name, grid, in_specs, out_specs, out_shape, scratch_shapes=(), sem=None, comm=None):
    in_specs, out_specs, out_shape, scratch_shapes = list(in_specs), list(out_specs), list(out_shape), list(scratch_shapes)
    if comm is None:
        res = pl.pallas_call(
            body, name=name, grid=grid, in_specs=in_specs, out_specs=out_specs, out_shape=out_shape,
            scratch_shapes=scratch_shapes,
            compiler_params=pltpu.CompilerParams(dimension_semantics=sem, vmem_limit_bytes=VMEM_LIMIT_BYTES),
        )(*args)
        return list(res), []
    n_in, n_out, n_sc = len(in_specs), len(out_specs), len(scratch_shapes)
    n_ci, n_co = len(comm.ins), len(comm.out_shapes)
    nsteps = math.prod(grid)

    def hosted(*refs):
        ins, cins = refs[:n_in], refs[n_in:n_in + n_ci]
        o0 = n_in + n_ci
        outs, couts = refs[o0:o0 + n_out], refs[o0 + n_out:o0 + n_out + n_co]
        s0 = o0 + n_out + n_co
        scr, csems = refs[s0:s0 + n_sc], refs[s0 + n_sc:]
        step = jnp.int32(0)
        for d in range(len(grid)):
            step = step * grid[d] + pl.program_id(d)
        comm.begin(step, nsteps, cins, couts, csems)
        body(*ins, *outs, *scr)
        comm.end(step, nsteps, cins, couts, csems)

    res = pl.pallas_call(
        hosted, name=name, grid=grid, in_specs=in_specs + [ANY] * n_ci, out_specs=out_specs + [ANY] * n_co,
        out_shape=out_shape + comm.out_shapes, scratch_shapes=scratch_shapes + comm.scratch,
        input_output_aliases={n_in + a: n_out + b for a, b in comm.aliases},
        compiler_params=pltpu.CompilerParams(dimension_semantics=("arbitrary",) * len(grid),
                                             vmem_limit_bytes=VMEM_LIMIT_BYTES, has_side_effects=True),
    )(*args, *comm.ins)
    return list(res[:n_out]), list(res[n_out:])


def _exchange_alone(comm, name):
    def body():
        pass

    return _call(body, [], name=name, grid=(), in_specs=[], out_specs=[], out_shape=[], comm=comm)[1]


def _matmul(a, b, *, mode, name, out_dtype=F32, out_blocks=None, tm=None, tn=None, tk=None, comm=None):
    tm = MM_TILE if tm is None else tm
    tn = MM_TILE if tn is None else tn
    a_blk = a.ndim == 3
    b_blk = b.ndim == 3
    if mode == "nn":
        M, K = a.shape
        N = b.shape[0] * b.shape[2] if b_blk else b.shape[1]
        dims = (((1,), (0,)), ((), ()))
    elif mode == "nt":
        M = a.shape[1] if a_blk else a.shape[0]
        K = a.shape[0] * a.shape[2] if a_blk else a.shape[1]
        N = b.shape[1] if b_blk else b.shape[0]
        dims = (((1,), (1,)), ((), ()))
    else:
        K, M = a.shape
        N = b.shape[0] * b.shape[2] if b_blk else b.shape[1]
        dims = (((0,), (0,)), ((), ()))

    tm = _tile(M, tm)
    tn = _tile(N, tn)
    if mode == "nt" and (a_blk or b_blk):
        tk = a.shape[2] if a_blk else b.shape[2]
    else:
        tk = _tile(K, K if tk is None else tk)
    if mode != "nt" and b_blk:
        tn = _tile(b.shape[2], tn)
    if out_blocks is not None:
        tn = _tile(out_blocks, tn)
    nk = K // tk
    grid = (M // tm, N // tn, nk)

    if mode == "nn":
        a_spec = pl.BlockSpec((tm, tk), lambda i, j, k: (i, k))
        if b_blk:
            rb = b.shape[2] // tn
            b_spec = pl.BlockSpec((None, tk, tn), lambda i, j, k: (j // rb, k, j % rb))
        else:
            b_spec = pl.BlockSpec((tk, tn), lambda i, j, k: (k, j))
    elif mode == "nt":
        if a_blk:
            a_spec = pl.BlockSpec((None, tm, tk), lambda i, j, k: (k, i, 0))
        else:
            a_spec = pl.BlockSpec((tm, tk), lambda i, j, k: (i, k))
        if b_blk:
            b_spec = pl.BlockSpec((None, tn, tk), lambda i, j, k: (k, j, 0))
        else:
            b_spec = pl.BlockSpec((tn, tk), lambda i, j, k: (j, k))
    else:
        a_spec = pl.BlockSpec((tk, tm), lambda i, j, k: (k, i))
        if b_blk:
            rb = b.shape[2] // tn
            b_spec = pl.BlockSpec((None, tk, tn), lambda i, j, k: (j // rb, k, j % rb))
        else:
            b_spec = pl.BlockSpec((tk, tn), lambda i, j, k: (k, j))

    if out_blocks is None:
        o_spec = pl.BlockSpec((tm, tn), lambda i, j, k: (i, j))
        o_shape = _sds((M, N), out_dtype)
    else:
        ro = out_blocks // tn
        o_spec = pl.BlockSpec((None, tm, tn), lambda i, j, k: (j // ro, i, j % ro))
        o_shape = _sds((N // out_blocks, M, out_blocks), out_dtype)

    direct = nk == 1 or out_dtype == F32

    def body(a_ref, b_ref, o_ref, *scratch):
        if nk == 1:
            o_ref[...] = lax.dot_general(a_ref[...], b_ref[...], dims, preferred_element_type=F32).astype(o_ref.dtype)
            return
        acc_ref = o_ref if direct else scratch[0]
        k = pl.program_id(2)

        @pl.when(k == 0)
        def _():
            acc_ref[...] = jnp.zeros(acc_ref.shape, F32)

        acc_ref[...] += lax.dot_general(a_ref[...], b_ref[...], dims, preferred_element_type=F32)
        if not direct:
            @pl.when(k == nk - 1)
            def _():
                o_ref[...] = acc_ref[...].astype(o_ref.dtype)

    scratch = [] if direct else [pltpu.VMEM((tm, tn), F32)]
    outs, landed = _call(body, [a, b], name=name, grid=grid, in_specs=[a_spec, b_spec], out_specs=[o_spec],
                         out_shape=[o_shape], scratch_shapes=scratch, sem=("parallel", "parallel", "arbitrary"), comm=comm)
    return outs[0] if comm is None else (outs[0], landed)


def _rms(x):
    return lax.rsqrt(jnp.mean(x * x, axis=-1, keepdims=True) + EPS)


def _rms_bwd(dy, x, g):
    r = _rms(x)
    n = x * r
    dn = dy * g
    dx = r * (dn - n * jnp.mean(dn * n, axis=-1, keepdims=True))
    return dx, dy * n


def _sigmoid(x):
    return 1.0 / (1.0 + jnp.exp(-x))


def _rope_rot(t):
    return pltpu.roll(t, HALF_ROPE, 1) - pltpu.roll(t, LANES - HALF_ROPE, 1)


def _lane(shape):
    return lax.broadcasted_iota(jnp.int32, shape, 1)


def _split3(x):
    hi = x.astype(BF16).astype(F32)
    r1 = x - hi
    mid = r1.astype(BF16).astype(F32)
    lo = (r1 - mid).astype(BF16).astype(F32)
    return hi, mid, lo


def _cumsum_rows(x, reverse):
    S = x.shape[0]
    bs = min(256, S)
    nb = S // bs
    r = lax.broadcasted_iota(jnp.int32, (bs, bs), 0)
    c = lax.broadcasted_iota(jnp.int32, (bs, bs), 1)
    tri = jnp.where((c >= r) if reverse else (c <= r), 1.0, 0.0).astype(BF16)
    edge = lax.broadcasted_iota(jnp.int32, (bs, x.shape[1]), 0) == (0 if reverse else bs - 1)
    carry = jnp.zeros((1, x.shape[1]), F32)
    outs = [None] * nb
    for bi in (range(nb - 1, -1, -1) if reverse else range(nb)):
        xb = x[bi * bs:(bi + 1) * bs, :]
        acc = carry
        for term in _split3(xb):
            acc = acc + jnp.dot(tri, term.astype(BF16), preferred_element_type=F32)
        outs[bi] = acc
        carry = jnp.sum(jnp.where(edge, acc, 0.0), axis=0, keepdims=True)
    return jnp.concatenate(outs, axis=0) if nb > 1 else outs[0]


def _gelu_parts(x):
    c0 = math.sqrt(2.0 / math.pi)
    inner = c0 * (x + 0.044715 * (x * x * x))
    t = jnp.tanh(inner)
    g = 0.5 * x * (1.0 + t)
    dg = 0.5 * (1.0 + t) + 0.5 * x * (1.0 - t * t) * (c0 * (1.0 + 3.0 * 0.044715 * (x * x)))
    return g, dg


def _accumulate(ref, value, first):
    @pl.when(first)
    def _():
        ref[...] = value

    @pl.when(jnp.logical_not(first))
    def _():
        ref[...] += value


def _cast_bf16(w, name):
    R, C = w.shape
    tr = _tile(R, 512, 16)

    def body(w_ref, o_ref):
        o_ref[...] = w_ref[...].astype(BF16)

    blk = pl.BlockSpec((tr, C), lambda i: (i, 0))
    return _call(body, [w], name=name, grid=(R // tr,), in_specs=[blk], out_specs=[blk],
                 out_shape=[_sds((R, C), BF16)], sem=("parallel",))[0][0]


def _concat_cols(parts, name):
    T = parts[0].shape[0]
    widths = [p.shape[1] for p in parts]
    tm = _tile(T, ROW_TILE, 16)

    def body(*refs):
        o_ref = refs[-1]
        off = 0
        for p_ref, w in zip(refs[:-1], widths):
            o_ref[:, off:off + w] = p_ref[...].astype(BF16)
            off += w

    return _call(body, parts, name=name, grid=(T // tm,),
                 in_specs=[pl.BlockSpec((tm, w), lambda i: (i, 0)) for w in widths],
                 out_specs=[pl.BlockSpec((tm, sum(widths)), lambda i: (i, 0))],
                 out_shape=[_sds((T, sum(widths)), BF16)], sem=("parallel",))[0][0]


def _prenorm(x, g):
    T, D = x.shape
    tm = _tile(T, ROW_TILE, 16)

    def body(x_ref, g_ref, h_ref):
        xv = x_ref[...]
        h_ref[...] = (xv * _rms(xv) * g_ref[...]).astype(BF16)

    row = pl.BlockSpec((tm, D), lambda i: (i, 0))
    return _call(body, [x, g], name="prenorm", grid=(T // tm,),
                 in_specs=[row, pl.BlockSpec((1, D), lambda i: (0, 0))], out_specs=[row],
                 out_shape=[_sds((T, D), BF16)], sem=("parallel",))[0][0]


def _split_prep(proj, pos, invf, gq, gkv, bfor, lay, comm=None):
    T = proj.shape[0]
    tm = _tile(T, ROW_TILE, 16)

    def body(q_ref, kv_ref, kpe_ref, fl_ref, pos_ref, invf_ref, gq_ref, gkv_ref, bf_ref,
             qn_ref, kvn_ref, kper_ref, logf_ref, cos_ref, sin_ref):
        ql = q_ref[...]
        qn_ref[...] = (ql * _rms(ql) * gq_ref[...]).astype(BF16)
        kl = kv_ref[...]
        kvn_ref[...] = (kl * _rms(kl) * gkv_ref[...]).astype(BF16)
        ang = pos_ref[...].astype(F32) * invf_ref[...]
        valid = _lane(ang.shape) < ROPE
        cs = jnp.where(valid, jnp.cos(ang), 0.0)
        sn = jnp.where(valid, jnp.sin(ang), 0.0)
        cos_ref[...] = cs
        sin_ref[...] = sn
        kp = jnp.where(valid, kpe_ref[...], 0.0)
        kper_ref[...] = (kp * cs + _rope_rot(kp) * sn).astype(BF16)
        z = fl_ref[...] + bf_ref[...]
        logf_ref[...] = jnp.minimum(z, 0.0) - jnp.log(1.0 + jnp.exp(-jnp.abs(z)))

    def col(width, off):
        return pl.BlockSpec((tm, width), lambda i: (i, off // width))

    def vec(width):
        return pl.BlockSpec((1, width), lambda i: (0, 0))

    def out(width):
        return pl.BlockSpec((tm, width), lambda i: (i, 0))

    return _call(
        body, [proj, proj, proj, proj, pos, invf, gq, gkv, bfor], name="split_prep", grid=(T // tm,),
        in_specs=[col(Q_LORA, lay["q"]), col(KV_LORA, lay["kv"]), col(LANES, lay["kpe"]), col(LANES, lay["fl"]),
                  pl.BlockSpec((tm, 1), lambda i: (i, 0)), vec(LANES), vec(Q_LORA), vec(KV_LORA), vec(LANES)],
        out_specs=[out(Q_LORA), out(KV_LORA), out(LANES), out(LANES), out(LANES), out(LANES)],
        out_shape=[_sds((T, Q_LORA), BF16), _sds((T, KV_LORA), BF16), _sds((T, LANES), BF16),
                   _sds((T, LANES), F32), _sds((T, LANES), F32), _sds((T, LANES), F32)],
        sem=("parallel",), comm=comm)


def _mla_prep(qraw, kvraw, kper, cosT, sinT):
    H, T, _ = qraw.shape
    tm = _tile(T, HEAD_ROW_TILE, 16)

    def body(q_ref, kv_ref, kpe_ref, cos_ref, sin_ref, qo_ref, ko_ref, vo_ref):
        q = q_ref[...]
        pe = q[:, NOPE:]
        pe = jnp.where(_lane(pe.shape) < ROPE, pe, 0.0)
        qo_ref[:, :NOPE] = q[:, :NOPE].astype(BF16)
        qo_ref[:, NOPE:] = (pe * cos_ref[...] + _rope_rot(pe) * sin_ref[...]).astype(BF16)
        kv = kv_ref[...]
        ko_ref[:, :NOPE] = kv[:, :NOPE].astype(BF16)
        ko_ref[:, NOPE:] = kpe_ref[...]
        vo_ref[...] = kv[:, NOPE:].astype(BF16)

    head = pl.BlockSpec((None, tm, ATT_DK), lambda h, i: (h, i, 0))
    tok = pl.BlockSpec((tm, LANES), lambda h, i: (i, 0))
    return _call(
        body, [qraw, kvraw, kper, cosT, sinT], name="mla_prep", grid=(H, T // tm),
        in_specs=[head, head, tok, tok, tok],
        out_specs=[head, head, pl.BlockSpec((None, tm, VDIM), lambda h, i: (h, i, 0))],
        out_shape=[_sds((H, T, ATT_DK), BF16), _sds((H, T, ATT_DK), BF16), _sds((H, T, VDIM), BF16)],
        sem=("parallel", "parallel"))[0]


def _fox_cumsum(logf, B, S, inv_scale):
    T = logf.shape[0]

    def body(l_ref, c_ref):
        c_ref[...] = _cumsum_rows(l_ref[...], reverse=False) * inv_scale

    seq = pl.BlockSpec((S, LANES), lambda b: (b, 0))
    return _call(body, [logf], name="fox_cumsum", grid=(B,), in_specs=[seq], out_specs=[seq],
                 out_shape=[_sds((T, LANES), F32)], sem=("parallel",))[0][0]


def _fox_prep(proj, cs, lay):
    T = proj.shape[0]
    tm = _tile(T, HEAD_ROW_TILE, 16)

    def body(q_ref, k_ref, v_ref, cs_ref, qo_ref, ko_ref, vo_ref):
        h = pl.program_id(0)
        cv = cs_ref[...]
        lane = _lane(cv.shape)
        ccol = jnp.sum(jnp.where(lane == h, cv, 0.0), axis=1, keepdims=True)
        hi, mid, lo = _split3(ccol)
        one = jnp.where(lane < 6, 1.0, 0.0)
        augq = jnp.where(lane == 0, hi, jnp.where(lane == 1, mid, jnp.where(lane == 2, lo, one)))
        augk = jnp.where(lane < 3, 1.0, jnp.where(lane == 3, -hi, jnp.where(lane == 4, -mid, jnp.where(lane == 5, -lo, 0.0))))
        qo_ref[:, :FOX_DIM] = q_ref[...].astype(BF16)
        qo_ref[:, FOX_DIM:] = augq.astype(BF16)
        ko_ref[:, :FOX_DIM] = k_ref[...].astype(BF16)
        ko_ref[:, FOX_DIM:] = augk.astype(BF16)
        vo_ref[...] = v_ref[...].astype(BF16)

    def col(off):
        return pl.BlockSpec((tm, FOX_DIM), lambda h, i: (i, off // FOX_DIM + h))

    head = pl.BlockSpec((None, tm, ATT_DK), lambda h, i: (h, i, 0))
    return _call(
        body, [proj, proj, proj, cs], name="fox_prep", grid=(HEADS, T // tm),
        in_specs=[col(lay["fq"]), col(lay["fk"]), col(lay["fv"]), pl.BlockSpec((tm, LANES), lambda h, i: (i, 0))],
        out_specs=[head, head, pl.BlockSpec((None, tm, VDIM), lambda h, i: (h, i, 0))],
        out_shape=[_sds((HEADS, T, ATT_DK), BF16), _sds((HEADS, T, ATT_DK), BF16), _sds((HEADS, T, VDIM), BF16)],
        sem=("parallel", "parallel"))[0]


def _visible(tq, tk, unit):
    r = lax.broadcasted_iota(jnp.int32, (tq, tk), 0)
    c = lax.broadcasted_iota(jnp.int32, (tq, tk), 1)
    sh = int(math.log2(unit))
    return lax.shift_right_logical(c, sh) <= lax.shift_right_logical(r, sh)


def _attn_fwd(streams, *, B, S, name, comm=None):
    n = len(streams)
    H, T, DK = streams[0][0].shape
    DV = streams[0][2].shape[2]
    tq = _tile(S, ATT_TILE)
    nq = S // tq
    sub = min(ATT_SUB, tq)
    NT = (((1,), (1,)), ((), ()))

    def body(*refs):
        ins, outs, (m_sc, acc_sc) = refs[:3 * n], refs[3 * n:5 * n], refs[5 * n:]
        i, j = pl.program_id(1), pl.program_id(2)

        @pl.when(j == 0)
        def _():
            m_sc[...] = jnp.full(m_sc.shape, NEG_INF, F32)
            acc_sc[...] = jnp.zeros(acc_sc.shape, F32)

        def step(diagonal):
            work = [(t, r) for r in range(tq // sub) for t in range(n)]

            def scores(t, r):
                q_ref, k_ref, _ = ins[3 * t:3 * t + 3]
                kc = (r + 1) * sub if diagonal else tq
                s = lax.dot_general(q_ref[r * sub:(r + 1) * sub, :], k_ref[0:kc, :], NT, preferred_element_type=F32)
                return s * (streams[t][4] * LOG2_E)

            ahead = [scores(*work[w]) for w in range(min(ATT_AHEAD, len(work)))]
            for w, (t, r) in enumerate(work):
                s = ahead.pop(0)
                if w + ATT_AHEAD < len(work):
                    ahead.append(scores(*work[w + ATT_AHEAD]))
                v_ref = ins[3 * t + 2]
                kc = s.shape[1]
                rows = slice(r * sub, (r + 1) * sub)
                if diagonal:
                    own = jnp.where(_visible(sub, sub, streams[t][3]), s[:, kc - sub:], NEG_INF)
                    s = own if kc == sub else jnp.concatenate([s[:, :kc - sub], own], axis=1)
                m_prev = m_sc[t, rows, :]
                mx = s[:, 0:LANES]
                for g in range(1, kc // LANES):
                    mx = jnp.maximum(mx, s[:, g * LANES:(g + 1) * LANES])
                m_new = jnp.maximum(m_prev, jnp.max(mx, axis=1, keepdims=True))
                alpha = jnp.exp2(m_prev - m_new)
                p = jnp.exp2(s - jnp.tile(m_new, (1, kc // LANES))).astype(BF16)
                v_aug = jnp.concatenate([v_ref[0:kc, :], jnp.ones((kc, LANES), BF16)], axis=1)
                acc_sc[t, rows, :] = jnp.tile(alpha, (1, 2)) * acc_sc[t, rows, :] + jnp.dot(
                    p, v_aug, preferred_element_type=F32)
                m_sc[t, rows, :] = m_new

        @pl.when(j < i)
        def _():
            step(False)

        @pl.when(j == i)
        def _():
            step(True)
            for t in range(n):
                o_ref, lse_ref = outs[2 * t:2 * t + 2]
                l = acc_sc[t, :, DV:]
                o_ref[...] = (acc_sc[t, :, :DV] / l).astype(BF16)
                lse_ref[...] = m_sc[t] + jnp.log2(l)

    def qmap(g, i, j):
        return (g % H, (g // H) * nq + i, 0)

    def kmap(g, i, j):
        return (g % H, (g // H) * nq + jnp.minimum(j, i), 0)

    args = [a for st in streams for a in st[:3]]
    outs, landed = _call(
        body, args, name=name, grid=(B * H, nq, nq),
        in_specs=[pl.BlockSpec((None, tq, DK), qmap), pl.BlockSpec((None, tq, DK), kmap),
                  pl.BlockSpec((None, tq, DV), kmap)] * n,
        out_specs=[pl.BlockSpec((tq, DV), lambda g, i, j: ((g // H) * nq + i, g % H)),
                   pl.BlockSpec((None, tq, LANES), qmap)] * n,
        out_shape=[_sds((T, H * DV), BF16), _sds((H, T, LANES), F32)] * n,
        scratch_shapes=[pltpu.VMEM((n, tq, LANES), F32), pltpu.VMEM((n, tq, DV + LANES), F32)],
        sem=("parallel", "parallel", "arbitrary"), comm=comm)
    return [(outs[2 * t], outs[2 * t + 1]) for t in range(n)], landed


def _attn_bwd(streams, *, B, S, name, comm=None):
    n = len(streams)
    H, T, DK = streams[0][0].shape
    DV = streams[0][2].shape[2]
    tq = _tile(S, ATT_TILE)
    nq = S // tq
    sub = min(ATT_SUB, tq)
    NT = (((1,), (1,)), ((), ()))
    TN = (((0,), (0,)), ((), ()))

    def body(*refs):
        ins, outs = refs[:6 * n], refs[6 * n:]
        j, i = pl.program_id(1), pl.program_id(2)

        @pl.when(jnp.logical_and(j == 0, i == 0))
        def _():
            for t in range(n):
                outs[3 * t][...] = jnp.zeros(outs[3 * t].shape, F32)

        @pl.when(i == 0)
        def _():
            for t in range(n):
                outs[3 * t + 1][...] = jnp.zeros(outs[3 * t + 1].shape, F32)
                outs[3 * t + 2][...] = jnp.zeros(outs[3 * t + 2].shape, F32)

        def step(diagonal):
            work = [(t, r) for r in range(tq // sub) for t in range(n)]

            def kcols(r):
                return (r + 1) * sub if diagonal else tq

            def scores(t, r):
                q_ref, k_ref, v_ref, _, do_ref, _ = ins[6 * t:6 * t + 6]
                rows, kc = slice(r * sub, (r + 1) * sub), kcols(r)
                s = lax.dot_general(q_ref[rows, :], k_ref[0:kc, :], NT, preferred_element_type=F32)
                dp = lax.dot_general(do_ref[rows, :], v_ref[0:kc, :], NT, preferred_element_type=F32)
                return s * (streams[t][7] * LOG2_E), dp

            def probs(t, r, s, dp):
                _, _, _, o_ref, do_ref, lse_ref = ins[6 * t:6 * t + 6]
                rows, kc = slice(r * sub, (r + 1) * sub), kcols(r)
                if diagonal:
                    own = jnp.where(_visible(sub, sub, streams[t][6]), s[:, kc - sub:], NEG_INF)
                    s = own if kc == sub else jnp.concatenate([s[:, :kc - sub], own], axis=1)
                p = jnp.exp2(s - jnp.tile(lse_ref[rows, :], (1, kc // LANES)))
                delta = jnp.sum(do_ref[rows, :].astype(F32) * o_ref[rows, :].astype(F32), axis=1, keepdims=True)
                return p.astype(BF16), (p * (dp - delta) * streams[t][7]).astype(BF16)

            def grads(t, r, p, ds):
                q_ref, k_ref, _, _, do_ref, _ = ins[6 * t:6 * t + 6]
                dq_ref, dk_ref, dv_ref = outs[3 * t:3 * t + 3]
                rows, kc = slice(r * sub, (r + 1) * sub), kcols(r)
                dv_ref[0:kc, :] += lax.dot_general(p, do_ref[rows, :], TN, preferred_element_type=F32)
                dk_ref[0:kc, :] += lax.dot_general(ds, q_ref[rows, :], TN, preferred_element_type=F32)
                qrows = pl.ds(pl.multiple_of(i * tq + r * sub, sub), sub)
                dq_ref[qrows, :] += jnp.dot(ds, k_ref[0:kc, :], preferred_element_type=F32)

            nw = len(work)
            sc = {w: scores(*work[w]) for w in range(min(2, nw))}
            pr = {0: probs(*work[0], *sc.pop(0))}
            for w in range(nw):
                if w + 2 < nw:
                    sc[w + 2] = scores(*work[w + 2])
                if w + 1 < nw:
                    pr[w + 1] = probs(*work[w + 1], *sc.pop(w + 1))
                grads(*work[w], *pr.pop(w))

        @pl.when(i > j)
        def _():
            step(False)

        @pl.when(i == j)
        def _():
            step(True)

    def qmap(g, j, i):
        return (g % H, (g // H) * nq + jnp.maximum(i, j), 0)

    def kmap(g, j, i):
        return (g % H, (g // H) * nq + j, 0)

    def omap(g, j, i):
        return ((g // H) * nq + jnp.maximum(i, j), g % H)

    args = [a for st in streams for a in st[:6]]
    outs, landed = _call(
        body, args, name=name, grid=(B * H, nq, nq),
        in_specs=[pl.BlockSpec((None, tq, DK), qmap), pl.BlockSpec((None, tq, DK), kmap),
                  pl.BlockSpec((None, tq, DV), kmap), pl.BlockSpec((tq, DV), omap), pl.BlockSpec((tq, DV), omap),
                  pl.BlockSpec((None, tq, LANES), qmap)] * n,
        out_specs=[pl.BlockSpec((None, S, DK), lambda g, j, i: (g % H, g // H, 0)),
                   pl.BlockSpec((None, tq, DK), kmap), pl.BlockSpec((None, tq, DV), kmap)] * n,
        out_shape=[_sds((H, T, DK), F32), _sds((H, T, DK), F32), _sds((H, T, DV), F32)] * n,
        sem=("parallel", "arbitrary", "arbitrary"), comm=comm)
    return [tuple(outs[3 * t:3 * t + 3]) for t in range(n)], landed


def _gate_merge(am, af, proj, bgate, lay, D, comm=None):
    T = am.shape[0]
    tm = _tile(T, ROW_TILE, 16)
    tn = _tile(D, 1024)

    def body(am_ref, af_ref, gm_ref, gf_ref, bm_ref, bf_ref, o_ref):
        sm = _sigmoid(gm_ref[...] + bm_ref[...])
        sf = _sigmoid(gf_ref[...] + bf_ref[...])
        o_ref[...] = (sm * am_ref[...] + sf * af_ref[...]).astype(BF16)

    og = lay["g"] // tn
    blk = pl.BlockSpec((tm, tn), lambda i, j: (i, j))
    return _call(
        body, [am, af, proj, proj, bgate, bgate], name="gate_merge", grid=(T // tm, D // tn),
        in_specs=[blk, blk, pl.BlockSpec((tm, tn), lambda i, j: (i, og + j)),
                  pl.BlockSpec((tm, tn), lambda i, j: (i, og + D // tn + j)),
                  pl.BlockSpec((1, tn), lambda i, j: (0, j)), pl.BlockSpec((1, tn), lambda i, j: (0, D // tn + j))],
        out_specs=[blk], out_shape=[_sds((T, D), BF16)], sem=("parallel", "parallel"), comm=comm)


def _mid(x, y1, g_pm, g_ffn):
    T, D = x.shape
    tm = _tile(T, ROW_TILE, 16)

    def body(x_ref, y_ref, gp_ref, gf_ref, x1_ref, h2_ref):
        y = y_ref[...]
        x1 = x_ref[...] + y * _rms(y) * gp_ref[...]
        x1_ref[...] = x1
        h2_ref[...] = (x1 * _rms(x1) * gf_ref[...]).astype(BF16)

    row = pl.BlockSpec((tm, D), lambda i: (i, 0))
    vec = pl.BlockSpec((1, D), lambda i: (0, 0))
    return _call(body, [x, y1, g_pm, g_ffn], name="mid", grid=(T // tm,), in_specs=[row, row, vec, vec],
                 out_specs=[row, row], out_shape=[_sds((T, D), F32), _sds((T, D), BF16)], sem=("parallel",))[0]


def _conv3(u, w_ref, bias):
    row = lax.broadcasted_iota(jnp.int32, u.shape, 0)
    u1 = jnp.where(row >= 1, pltpu.roll(u, 1, 0), 0.0)
    u2 = jnp.where(row >= 2, pltpu.roll(u, 2, 0), 0.0)
    return w_ref[0:1, :] * u2 + w_ref[1:2, :] * u1 + w_ref[2:3, :] * u + bias, u1, u2


def _convffn_fwd(u, cw, cb, B, S, F, comm=None):
    T = u.shape[0]
    tn = _tile(F, 256)
    nf = F // tn

    def body(ug_ref, uv_ref, wg_ref, wv_ref, bg_ref, bv_ref, a_ref):
        g, _, _ = _conv3(ug_ref[...], wg_ref, bg_ref[...])
        val, _, _ = _conv3(uv_ref[...], wv_ref, bv_ref[...])
        a_ref[...] = (_gelu_parts(g)[0] * val).astype(BF16)

    def seq(off):
        return pl.BlockSpec((S, tn), lambda b, j: (b, off + j))

    def par(rows, off):
        return pl.BlockSpec((rows, tn), lambda b, j: (0, off + j))

    return _call(body, [u, u, cw, cw, cb, cb], name="convffn_fwd", grid=(B, nf),
                 in_specs=[seq(0), seq(nf), par(3, 0), par(3, nf), par(1, 0), par(1, nf)],
                 out_specs=[seq(0)], out_shape=[_sds((T, F), BF16)], sem=("parallel", "parallel"), comm=comm)


def _convffn_bwd(u, dact, cw, cb, B, S, F, comm=None):
    T = u.shape[0]
    tn = _tile(F, 256)
    nf = F // tn

    def body(ug_ref, uv_ref, da_ref, wg_ref, wv_ref, bg_ref, bv_ref, dug_ref, duv_ref, dpg_ref, dpv_ref):
        b = pl.program_id(1)
        ug, uv, da = ug_ref[...], uv_ref[...], da_ref[...]
        g, ug1, ug2 = _conv3(ug, wg_ref, bg_ref[...])
        val, uv1, uv2 = _conv3(uv, wv_ref, bv_ref[...])
        gel, dgel = _gelu_parts(g)
        dg = da * val * dgel
        dval = da * gel
        row = lax.broadcasted_iota(jnp.int32, ug.shape, 0)

        def back(d, w_ref):
            d1 = jnp.where(row < S - 1, pltpu.roll(d, S - 1, 0), 0.0)
            d2 = jnp.where(row < S - 2, pltpu.roll(d, S - 2, 0), 0.0)
            return w_ref[2:3, :] * d + w_ref[1:2, :] * d1 + w_ref[0:1, :] * d2

        dug_ref[...] = back(dg, wg_ref).astype(BF16)
        duv_ref[...] = back(dval, wv_ref).astype(BF16)

        def sums(d, u0, u1, u2):
            r8 = lax.broadcasted_iota(jnp.int32, (8, d.shape[1]), 0)
            out = jnp.zeros((8, d.shape[1]), F32)
            for k, t in enumerate((d * u2, d * u1, d * u0, d)):
                out = jnp.where(r8 == k, jnp.sum(t, axis=0, keepdims=True), out)
            return out

        _accumulate(dpg_ref, sums(dg, ug, ug1, ug2), b == 0)
        _accumulate(dpv_ref, sums(dval, uv, uv1, uv2), b == 0)

    def seq(off):
        return pl.BlockSpec((S, tn), lambda j, b: (b, off + j))

    def par(rows, off):
        return pl.BlockSpec((rows, tn), lambda j, b: (0, off + j))

    outs, landed = _call(
        body, [u, u, dact, cw, cw, cb, cb], name="convffn_bwd", grid=(nf, B),
        in_specs=[seq(0), seq(nf), seq(0), par(3, 0), par(3, nf), par(1, 0), par(1, nf)],
        out_specs=[seq(0), seq(0), par(8, 0), par(8, 0)],
        out_shape=[_sds((T, F), BF16), _sds((T, F), BF16), _sds((8, F), F32), _sds((8, F), F32)],
        sem=("parallel", "arbitrary"), comm=comm)
    return outs, landed


def _tail(ff, x1, tgt, g):
    T, D = ff.shape
    tm = _tile(T, ROW_TILE, 16)

    def body(ff_ref, x1_ref, t_ref, g_ref, dy_ref, dff_ref, loss_ref, dg_ref):
        i = pl.program_id(0)
        f = ff_ref[...]
        gv = g_ref[...]
        r = _rms(f)
        n = f * r
        e = (x1_ref[...] + n * gv) - t_ref[...]
        dy = e * (1.0 / D)
        dy_ref[...] = dy
        dn = dy * gv
        dff_ref[...] = (r * (dn - n * jnp.mean(dn * n, axis=-1, keepdims=True))).astype(BF16)
        part = 0.5 * jnp.sum(jnp.mean(e * e, axis=-1, keepdims=True), axis=0, keepdims=True)
        _accumulate(loss_ref, jnp.broadcast_to(part, loss_ref.shape), i == 0)
        _accumulate(dg_ref, jnp.sum(dy * n, axis=0, keepdims=True), i == 0)

    row = pl.BlockSpec((tm, D), lambda i: (i, 0))
    vec = pl.BlockSpec((1, D), lambda i: (0, 0))
    return _call(body, [ff, x1, tgt, g], name="tail", grid=(T // tm,), in_specs=[row, row, row, vec],
                 out_specs=[row, row, pl.BlockSpec((8, LANES), lambda i: (0, 0)), vec],
                 out_shape=[_sds((T, D), F32), _sds((T, D), BF16), _sds((8, LANES), F32), _sds((1, D), F32)],
                 sem=("arbitrary",))[0]


def _mid_bwd(dy, dh2, x1, y1, g_ffn, g_pm, comm=None):
    T, D = dy.shape
    tm = _tile(T, ROW_TILE, 16)

    def body(dy_ref, dh_ref, x1_ref, y1_ref, gf_ref, gp_ref, dx1_ref, dy1_ref, dgf_ref, dgp_ref):
        i = pl.program_id(0)
        dh = dh_ref[...]
        d2, dgf = _rms_bwd(dh, x1_ref[...], gf_ref[...])
        dx1 = dy_ref[...] + d2
        dx1_ref[...] = dx1
        d1, dgp = _rms_bwd(dx1, y1_ref[...], gp_ref[...])
        dy1_ref[...] = d1.astype(BF16)
        _accumulate(dgf_ref, jnp.sum(dgf, axis=0, keepdims=True), i == 0)
        _accumulate(dgp_ref, jnp.sum(dgp, axis=0, keepdims=True), i == 0)

    row = pl.BlockSpec((tm, D), lambda i: (i, 0))
    vec = pl.BlockSpec((1, D), lambda i: (0, 0))
    return _call(body, [dy, dh2, x1, y1, g_ffn, g_pm], name="mid_bwd", grid=(T // tm,),
                 in_specs=[row, row, row, row, vec, vec], out_specs=[row, row, vec, vec],
                 out_shape=[_sds((T, D), F32), _sds((T, D), BF16), _sds((1, D), F32), _sds((1, D), F32)],
                 sem=("arbitrary",), comm=comm)


def _gate_bwd(dm, am, af, proj, bgate, lay, D, comm=None):
    T = dm.shape[0]
    tm = _tile(T, ROW_TILE, 16)
    tn = _tile(D, 512)

    def body(dm_ref, am_ref, af_ref, gm_ref, gf_ref, bm_ref, bf_ref,
             dam_ref, daf_ref, dgm_ref, dgf_ref, dbm_ref, dbf_ref):
        i = pl.program_id(1)
        d = dm_ref[...]
        sm = _sigmoid(gm_ref[...] + bm_ref[...])
        sf = _sigmoid(gf_ref[...] + bf_ref[...])
        dam_ref[...] = (d * sm).astype(BF16)
        daf_ref[...] = (d * sf).astype(BF16)
        dgm = d * am_ref[...] * (sm * (1.0 - sm))
        dgf = d * af_ref[...] * (sf * (1.0 - sf))
        dgm_ref[...] = dgm.astype(BF16)
        dgf_ref[...] = dgf.astype(BF16)
        _accumulate(dbm_ref, jnp.sum(dgm, axis=0, keepdims=True), i == 0)
        _accumulate(dbf_ref, jnp.sum(dgf, axis=0, keepdims=True), i == 0)

    og = lay["g"] // tn
    blk = pl.BlockSpec((tm, tn), lambda j, i: (i, j))
    vec = pl.BlockSpec((1, tn), lambda j, i: (0, j))
    return _call(
        body, [dm, am, af, proj, proj, bgate, bgate], name="gate_bwd", grid=(D // tn, T // tm),
        in_specs=[blk, blk, blk, pl.BlockSpec((tm, tn), lambda j, i: (i, og + j)),
                  pl.BlockSpec((tm, tn), lambda j, i: (i, og + D // tn + j)),
                  vec, pl.BlockSpec((1, tn), lambda j, i: (0, D // tn + j))],
        out_specs=[blk, blk, blk, blk, vec, vec],
        out_shape=[_sds((T, D), BF16)] * 4 + [_sds((1, D), F32)] * 2, sem=("parallel", "arbitrary"), comm=comm)


def _mla_bwd_prep(dq, dk, dv, cosT, sinT, comm=None):
    H, T, _ = dq.shape
    tm = _tile(T, HEAD_ROW_TILE, 16)

    def body(dq_ref, dk_ref, dv_ref, cos_ref, sin_ref, dqr_ref, dkv_ref, dkpe_ref):
        h = pl.program_id(1)
        cs, sn = cos_ref[...], sin_ref[...]
        valid = _lane(cs.shape) < ROPE

        def unrope(d):
            d = jnp.where(valid, d, 0.0)
            return d * cs - _rope_rot(d) * sn

        dqv = dq_ref[...]
        dqr_ref[:, :NOPE] = dqv[:, :NOPE].astype(BF16)
        dqr_ref[:, NOPE:] = unrope(dqv[:, NOPE:]).astype(BF16)
        dkv_ = dk_ref[...]
        dkv_ref[:, :NOPE] = dkv_[:, :NOPE].astype(BF16)
        dkv_ref[:, NOPE:] = dv_ref[...].astype(BF16)
        _accumulate(dkpe_ref, unrope(dkv_[:, NOPE:]), h == 0)

    head = pl.BlockSpec((None, tm, ATT_DK), lambda i, h: (h, i, 0))
    tok = pl.BlockSpec((tm, LANES), lambda i, h: (i, 0))
    return _call(
        body, [dq, dk, dv, cosT, sinT], name="mla_bwd_prep", grid=(T // tm, H),
        in_specs=[head, head, pl.BlockSpec((None, tm, VDIM), lambda i, h: (h, i, 0)), tok, tok],
        out_specs=[head, head, tok],
        out_shape=[_sds((H, T, ATT_DK), BF16), _sds((H, T, ATT_DK), BF16), _sds((T, LANES), F32)],
        sem=("parallel", "arbitrary"), comm=comm)


def _fox_bwd_prep(dq, dk, proj, bfor, lay, B, S, inv_scale):
    H, T, _ = dq.shape

    def body(dq_ref, dk_ref, fl_ref, bf_ref, dfl_ref, dbf_ref, dc_sc):
        b, h = pl.program_id(0), pl.program_id(1)
        lane = _lane(dc_sc.shape)
        col = jnp.sum(jnp.where(lane == 0, dq_ref[...], 0.0) - jnp.where(lane == 3, dk_ref[...], 0.0),
                      axis=1, keepdims=True)

        @pl.when(h == 0)
        def _():
            dc_sc[...] = jnp.zeros(dc_sc.shape, F32)

        dc_sc[...] = jnp.where(lane == h, col, dc_sc[...])

        @pl.when(h == H - 1)
        def _():
            dlogf = _cumsum_rows(dc_sc[...] * inv_scale, reverse=True)
            z = fl_ref[...] + bf_ref[...]
            dz = jnp.where(lane < H, dlogf * (1.0 / (1.0 + jnp.exp(z))), 0.0)
            dfl_ref[...] = dz
            _accumulate(dbf_ref, jnp.sum(dz, axis=0, keepdims=True), b == 0)

    aug = pl.BlockSpec((None, S, LANES), lambda b, h: (h, b, 1))
    seq = pl.BlockSpec((S, LANES), lambda b, h: (b, 0))
    vec = pl.BlockSpec((1, LANES), lambda b, h: (0, 0))
    return _call(
        body, [dq, dk, proj, bfor], name="fox_bwd_prep", grid=(B, H),
        in_specs=[aug, aug, pl.BlockSpec((S, LANES), lambda b, h: (b, lay["fl"] // LANES)), vec],
        out_specs=[seq, vec], out_shape=[_sds((T, LANES), F32), _sds((1, LANES), F32)],
        scratch_shapes=[pltpu.VMEM((S, LANES), F32)], sem=("arbitrary", "arbitrary"))[0]


def _heads_to_cols(dq, dk, dv):
    H, T, _ = dq.shape
    tm = _tile(T, HEAD_ROW_TILE, 16)

    def body(a_ref, b_ref, c_ref, ao_ref, bo_ref, co_ref):
        ao_ref[...] = a_ref[...].astype(BF16)
        bo_ref[...] = b_ref[...].astype(BF16)
        co_ref[...] = c_ref[...].astype(BF16)

    src = pl.BlockSpec((None, tm, FOX_DIM), lambda i, h: (h, i, 0))
    dst = pl.BlockSpec((tm, FOX_DIM), lambda i, h: (i, h))
    return _call(body, [dq, dk, dv], name="heads_to_cols", grid=(T // tm, H), in_specs=[src, src, src],
                 out_specs=[dst, dst, dst], out_shape=[_sds((T, H * FOX_DIM), BF16)] * 3,
                 sem=("parallel", "parallel"))[0]


def _lat_bwd(dqn, dkvn, proj, gq, gkv, lay):
    T = dqn.shape[0]
    tm = _tile(T, ROW_TILE, 16)

    def body(dq_ref, dkv_ref, q_ref, kv_ref, gq_ref, gkv_ref, dql_ref, dkl_ref, dgq_ref, dgkv_ref):
        i = pl.program_id(0)
        dql, dgq = _rms_bwd(dq_ref[...], q_ref[...], gq_ref[...])
        dkl, dgkv = _rms_bwd(dkv_ref[...], kv_ref[...], gkv_ref[...])
        dql_ref[...] = dql.astype(BF16)
        dkl_ref[...] = dkl.astype(BF16)
        _accumulate(dgq_ref, jnp.sum(dgq, axis=0, keepdims=True), i == 0)
        _accumulate(dgkv_ref, jnp.sum(dgkv, axis=0, keepdims=True), i == 0)

    def blk(width, off=0):
        return pl.BlockSpec((tm, width), lambda i: (i, off // width))

    def vec(width):
        return pl.BlockSpec((1, width), lambda i: (0, 0))

    return _call(
        body, [dqn, dkvn, proj, proj, gq, gkv], name="lat_bwd", grid=(T // tm,),
        in_specs=[blk(Q_LORA), blk(KV_LORA), blk(Q_LORA, lay["q"]), blk(KV_LORA, lay["kv"]), vec(Q_LORA), vec(KV_LORA)],
        out_specs=[blk(Q_LORA), blk(KV_LORA), vec(Q_LORA), vec(KV_LORA)],
        out_shape=[_sds((T, Q_LORA), BF16), _sds((T, KV_LORA), BF16), _sds((1, Q_LORA), F32), _sds((1, KV_LORA), F32)],
        sem=("arbitrary",))[0]


def _final_dx(dx1, dh, x, g, comm=None):
    T, D = x.shape
    tm = _tile(T, ROW_TILE, 16)

    def body(dx1_ref, dh_ref, x_ref, g_ref, dx_ref, dg_ref):
        i = pl.program_id(0)
        d, dg = _rms_bwd(dh_ref[...], x_ref[...], g_ref[...])
        dx_ref[...] = dx1_ref[...] + d
        _accumulate(dg_ref, jnp.sum(dg, axis=0, keepdims=True), i == 0)

    row = pl.BlockSpec((tm, D), lambda i: (i, 0))
    vec = pl.BlockSpec((1, D), lambda i: (0, 0))
    return _call(body, [dx1, dh, x, g], name="final_dx", grid=(T // tm,), in_specs=[row, row, row, vec],
                 out_specs=[row, vec], out_shape=[_sds((T, D), F32), _sds((1, D), F32)], sem=("arbitrary",), comm=comm)


def _chip_sum(pieces, paired, qc, name):
    G, R, C = pieces.shape
    tr = _tile(R, 256, 16)

    def body(qc_ref, g_ref, p_ref, keep_ref, send_ref):
        s = pl.program_id(1)
        tot = g_ref[...] + p_ref[...]

        @pl.when(s == 0)
        def _():
            keep_ref[...] = tot

        @pl.when(s > 0)
        def _():
            send_ref[...] = tot.astype(send_ref.dtype)

    grid_spec = pltpu.PrefetchScalarGridSpec(
        num_scalar_prefetch=1, grid=(R // tr, N_CHIP),
        in_specs=[pl.BlockSpec((None, tr, C), lambda i, s, qc: (2 * (qc[0] ^ s) + qc[1], i, 0)),
                  pl.BlockSpec((None, tr, C), lambda i, s, qc: (qc[0] ^ s, i, 0))],
        out_specs=[pl.BlockSpec((tr, C), lambda i, s, qc: (i, 0)),
                   pl.BlockSpec((None, tr, C), lambda i, s, qc: (jnp.maximum(s - 1, 0), i, 0))])
    send_dtype = BF16 if R >= 16 else pieces.dtype
    return pl.pallas_call(
        body, name=name, grid_spec=grid_spec,
        out_shape=[_sds((R, C), F32), _sds((3, R, C), send_dtype)],
        compiler_params=pltpu.CompilerParams(dimension_semantics=("arbitrary", "arbitrary"),
                                             vmem_limit_bytes=VMEM_LIMIT_BYTES),
    )(qc, pieces, paired)


def _adamw_math(w, g, m, v):
    m = ADAM_B1 * m + (1.0 - ADAM_B1) * g
    v = ADAM_B2 * v + (1.0 - ADAM_B2) * (g * g)
    m_hat = m / (1.0 - ADAM_B1 ** ADAM_STEP)
    v_hat = v / (1.0 - ADAM_B2 ** ADAM_STEP)
    delta = -ADAM_LR * (m_hat / (jnp.sqrt(v_hat) + ADAM_EPS) + ADAM_WD * w)
    return delta, m, v


def _sum_adamw(keep, pieces, w, m, v, name):
    R, C = w.shape
    P = pieces.shape[0]
    tr = _tile(R, 256, 16)

    def body(*refs):
        if keep is None:
            p_ref, w_ref, m_ref, v_ref, g_ref, d_ref, mo_ref, vo_ref = refs
            g = p_ref[0].astype(F32)
            rest = range(1, P)
        else:
            k_ref, p_ref, w_ref, m_ref, v_ref, g_ref, d_ref, mo_ref, vo_ref = refs
            g = k_ref[...]
            rest = range(P)
        for q in rest:
            g = g + p_ref[q].astype(F32)
        g_ref[...] = g
        d_ref[...], mo_ref[...], vo_ref[...] = _adamw_math(w_ref[...], g, m_ref[...], v_ref[...])

    blk = pl.BlockSpec((tr, C), lambda i: (i, 0))
    pblk = pl.BlockSpec((P, tr, C), lambda i: (0, i, 0))
    args = [pieces, w, m, v] if keep is None else [keep, pieces, w, m, v]
    specs = [pblk, blk, blk, blk] if keep is None else [blk, pblk, blk, blk, blk]
    return _call(body, args, name=name, grid=(R // tr,), in_specs=specs, out_specs=[blk] * 4,
                 out_shape=[_sds((R, C), F32)] * 4, sem=("parallel",))[0]


def _layout(D):
    lay = {"q": 0, "kv": Q_LORA, "kpe": Q_LORA + KV_LORA}
    lay["fq"] = lay["kpe"] + LANES
    lay["fk"] = lay["fq"] + HEADS * FOX_DIM
    lay["fv"] = lay["fk"] + HEADS * FOX_DIM
    lay["fl"] = lay["fv"] + HEADS * FOX_DIM
    lay["g"] = lay["fl"] + LANES
    lay["end"] = lay["g"] + 2 * D
    return lay


def kernel(x, positions, pre_mix_norm, w_in, q_a_norm, w_uq, kv_a_norm, w_ukv, b_forget, b_gate, w_branch_mla, w_branch_fox, w_out, post_mix_norm, pre_ffn_norm, w_up, conv_w, conv_b, w_down, post_ffn_norm, loss_target, m_pre_mix_norm, m_w_in, m_q_a_norm, m_w_uq, m_kv_a_norm, m_w_ukv, m_b_forget, m_b_gate, m_w_branch_mla, m_w_branch_fox, m_w_out, m_post_mix_norm, m_pre_ffn_norm, m_w_up, m_conv_w, m_conv_b, m_w_down, m_post_ffn_norm, v_pre_mix_norm, v_w_in, v_q_a_norm, v_w_uq, v_kv_a_norm, v_w_ukv, v_b_forget, v_b_gate, v_w_branch_mla, v_w_branch_fox, v_w_out, v_post_mix_norm, v_pre_ffn_norm, v_w_up, v_conv_w, v_conv_b, v_w_down, v_post_ffn_norm):
    B, S, D = x.shape
    T = B * S
    F = conv_b.shape[0] // 2
    lay = _layout(D)
    n_in = w_in.shape[1]
    d_in = N_DEV * n_in
    seg_a = Q_LORA + KV_LORA + ROPE
    seg_b = 3 * HEADS * FOX_DIM + HEADS
    mla_scale = (NOPE + ROPE) ** -0.5
    fox_scale = FOX_DIM ** -0.5
    ax, ay, ac = (lax.axis_index(a) for a in MESH_AXES)
    qc = jnp.stack([2 * ax + ay, ac]).astype(jnp.int32)

    def row(vec, width=None):
        vec = vec.reshape(1, -1)
        if width is not None and vec.shape[1] < width:
            vec = jnp.pad(vec, ((0, 0), (0, width - vec.shape[1])))
        return vec

    win_s = _cast_bf16(w_in, "cast_w_in")
    (win_g,) = _exchange_alone(_Comm([_GatherPlan([win_s], mid_frac=0.0)]), "gather_w_in")
    small_s = [_cast_bf16(w, "cast_" + n) for w, n in
               [(w_uq, "w_uq"), (w_ukv, "w_ukv"), (w_branch_mla, "w_branch_mla"), (w_branch_fox, "w_branch_fox"), (w_out, "w_out")]]
    wup_s = _cast_bf16(w_up, "cast_w_up")
    wdown_s = _cast_bf16(w_down, "cast_w_down")

    win_full = jnp.transpose(win_g, (1, 0, 2)).reshape(D, d_in)
    w_perm = jnp.concatenate(
        [win_full[:, :seg_a], jnp.zeros((D, LANES - ROPE), BF16), win_full[:, seg_a:seg_a + seg_b],
         jnp.zeros((D, LANES - HEADS), BF16), win_full[:, seg_a + seg_b:]], axis=1)

    x2 = x.reshape(T, D)
    tgt = loss_target.reshape(T, D)
    pos = positions.reshape(T, 1)
    inv_freq = 1.0 / (ROPE_THETA ** (jnp.arange(0, ROPE, 2, dtype=F32) / ROPE))
    invf = row(jnp.concatenate([inv_freq, inv_freq]), LANES)
    g_pre, g_q, g_kv = row(pre_mix_norm), row(q_a_norm), row(kv_a_norm)
    g_pm, g_ffn, g_pf = row(post_mix_norm), row(pre_ffn_norm), row(post_ffn_norm)
    bfor = row(b_forget, LANES)
    bgate = row(b_gate)
    cb_full = row(conv_b)

    def own_plan(blocks):
        return _Comm([_GatherOwnPlan(blocks)])

    def pass_plan(gathered):
        return _Comm([_GatherPassPlan(gathered)])

    def pair_plan(gs):
        return _Comm([_PairScatterPlan(gs)])

    def chip_plan(gs):
        return _Comm([_ChipScatterPlan(gs)])

    h = _prenorm(x2, g_pre)
    proj, small_g = _matmul(h, w_perm, mode="nn", name="mm_proj", comm=own_plan(small_s + [conv_w]))
    (qn, kvn, kper, logf, cosT, sinT), (wuq_g, wukv_g, wbm_g, wbf_g, wout_g, cw_g) = _split_prep(
        proj, pos, invf, g_q, g_kv, bfor, lay, comm=pass_plan(small_g))
    wuq_pad = jnp.pad(wuq_g, ((0, 0), (0, 0), (0, ATT_DK - NOPE - ROPE)))
    wbm = jnp.transpose(wbm_g, (1, 0, 2)).reshape(HEADS * VDIM, D)
    wbf = jnp.transpose(wbf_g, (1, 0, 2)).reshape(HEADS * FOX_DIM, D)
    wout = wout_g.reshape(D, D)
    cw_full = jnp.transpose(cw_g, (1, 0, 2)).reshape(3, 2 * F)

    qraw = _matmul(qn, wuq_pad, mode="nn", name="mm_q", out_blocks=ATT_DK)
    kvraw = _matmul(kvn, wukv_g, mode="nn", name="mm_kv", out_blocks=NOPE + VDIM)
    q_mla, k_mla, v_mla = _mla_prep(qraw, kvraw, kper, cosT, sinT)
    cs = _fox_cumsum(logf, B, S, 1.0 / fox_scale)
    q_fox, k_fox, v_fox = _fox_prep(proj, cs, lay)
    ((o_mla, lse_mla), (o_fox, lse_fox)), wup_half = _attn_fwd(
        [(q_mla, k_mla, v_mla, MLA_UNIT, mla_scale), (q_fox, k_fox, v_fox, 1, fox_scale)], B=B, S=S,
        name="attn_fwd", comm=own_plan([wup_s]))
    a_m = _matmul(o_mla, wbm, mode="nn", name="mm_branch_mla")
    a_f = _matmul(o_fox, wbf, mode="nn", name="mm_branch_fox")
    (merged,), (wup_g,) = _gate_merge(a_m, a_f, proj, bgate, lay, D, comm=pass_plan(wup_half))
    n_up = wup_g.shape[2]
    y1 = _matmul(merged, wout, mode="nn", name="mm_out")
    x1, h2 = _mid(x2, y1, g_pm, g_ffn)
    u, wdown_half = _matmul(h2, wup_g, mode="nn", name="mm_up", tn=n_up, comm=own_plan([wdown_s]))
    (act,), (wdown_g,) = _convffn_fwd(u, cw_full, cb_full, B, S, F, comm=pass_plan(wdown_half))
    wdown = wdown_g.reshape(F, D)
    ff = _matmul(act, wdown, mode="nn", name="mm_down", tk=F // 2)
    dy, dff, loss_part, dg_pf = _tail(ff, x1, tgt, g_pf)

    dact = _matmul(dff, wdown, mode="nt", name="mm_dact")
    dw_down = _matmul(act, dff, mode="tn", name="mm_dw_down", tm=512).reshape(N_DEV, F // N_DEV, D)
    (du_g, du_v, dcp_g, dcp_v), (pa_down,) = _convffn_bwd(u, dact, cw_full, cb_full, B, S, F, comm=pair_plan([dw_down]))
    keep_down, sb_down = _chip_sum(dw_down, pa_down, qc, "chipsum_w_down")
    du = _concat_cols([du_g, du_v], "concat_du")
    dh2, (rb_down,) = _matmul(du, wup_g, mode="nt", name="mm_dh2", comm=chip_plan([sb_down]))
    dw_up = _matmul(h2, du, mode="tn", name="mm_dw_up", out_blocks=n_up, tm=512, tn=n_up)
    (dx1, dy1, dg_ffn, dg_pm), _ = _mid_bwd(dy, dh2, x1, y1, g_ffn, g_pm)
    dmerged = _matmul(dy1, wout, mode="nt", name="mm_dmerged")
    dw_out = _matmul(merged, dy1, mode="tn", name="mm_dw_out").reshape(N_DEV, D // N_DEV, D)
    (da_m, da_f, dgl_m, dgl_f, dbg_m, dbg_f), (pa_up,) = _gate_bwd(dmerged, a_m, a_f, proj, bgate, lay, D,
                                                                   comm=pair_plan([dw_up]))
    keep_up, sb_up = _chip_sum(dw_up, pa_up, qc, "chipsum_w_up")
    dw_bm = _matmul(o_mla, da_m, mode="tn", name="mm_dw_branch_mla", out_blocks=D // N_DEV)
    dw_bf = _matmul(o_fox, da_f, mode="tn", name="mm_dw_branch_fox", out_blocks=D // N_DEV)
    mix = [dw_out, dw_bm, dw_bf]
    do_mla, pa_mix = _matmul(da_m, wbm, mode="nt", name="mm_do_mla", out_dtype=BF16, comm=pair_plan(mix))
    do_fox = _matmul(da_f, wbf, mode="nt", name="mm_do_fox", out_dtype=BF16)
    mix_sums = [_chip_sum(g, p, qc, "chipsum_" + n) for g, p, n in zip(mix, pa_mix, ["w_out", "w_branch_mla", "w_branch_fox"])]
    ((dq_m, dk_m, dv_m), (dq_f, dk_f, dv_f)), (rb_up,) = _attn_bwd(
        [(q_mla, k_mla, v_mla, o_mla, do_mla, lse_mla, MLA_UNIT, mla_scale),
         (q_fox, k_fox, v_fox, o_fox, do_fox, lse_fox, 1, fox_scale)], B=B, S=S, name="attn_bwd",
        comm=chip_plan([sb_up]))
    (dqraw, dkvraw, dkpe), rb_mix = _mla_bwd_prep(dq_m, dk_m, dv_m, cosT, sinT, comm=chip_plan([s[1] for s in mix_sums]))
    dqn = _matmul(dqraw, wuq_pad, mode="nt", name="mm_dqn")
    dw_uq = _matmul(qn, dqraw, mode="tn", name="mm_dw_uq", out_blocks=ATT_DK)[:, :, :NOPE + ROPE]
    dkvn = _matmul(dkvraw, wukv_g, mode="nt", name="mm_dkvn")
    dw_ukv = _matmul(kvn, dkvraw, mode="tn", name="mm_dw_ukv", out_blocks=NOPE + VDIM)
    dqlat, dkvlat, dg_q, dg_kv = _lat_bwd(dqn, dkvn, proj, g_q, g_kv, lay)
    dfl, dbfor = _fox_bwd_prep(dq_f, dk_f, proj, bfor, lay, B, S, 1.0 / fox_scale)
    dfq, dfk, dfv = _heads_to_cols(dq_f, dk_f, dv_f)
    dproj = _concat_cols([dqlat, dkvlat, dkpe, dfq, dfk, dfv, dfl, dgl_m, dgl_f], "concat_dproj")
    dw_perm = _matmul(h, dproj, mode="tn", name="mm_dw_in")
    dw_in_full = jnp.concatenate(
        [dw_perm[:, :seg_a], dw_perm[:, lay["fq"]:lay["fq"] + seg_b], dw_perm[:, lay["g"]:]], axis=1)
    dw_in = jnp.transpose(dw_in_full.reshape(D, N_DEV, n_in), (1, 0, 2))
    dcw = jnp.transpose(jnp.concatenate([dcp_g[0:3], dcp_v[0:3]], axis=1).reshape(3, N_DEV, (2 * F) // N_DEV), (1, 0, 2))
    late = [dw_in, dw_uq, dw_ukv, dcw]
    pa_late = _exchange_alone(pair_plan(late), "pair_late")
    late_sums = [_chip_sum(g, p, qc, "chipsum_" + n) for g, p, n in zip(late, pa_late, ["w_in", "w_uq", "w_ukv", "conv_w"])]
    dh, rb_late = _matmul(dproj, w_perm, mode="nt", name="mm_dh", tk=2048, comm=chip_plan([s[1] for s in late_sums]))
    (grad_x, dg_pre), _ = _final_dx(dx1, dh, x2, g_pre)

    big_out = {}

    def finish(n, keep, pieces, w, m, v):
        big_out[n] = _sum_adamw(keep, pieces, w, m, v, "adamw_" + n)

    finish("w_down", keep_down, rb_down, w_down, m_w_down, v_w_down)
    finish("w_up", keep_up, rb_up, w_up, m_w_up, v_w_up)
    finish("w_out", mix_sums[0][0], rb_mix[0], w_out, m_w_out, v_w_out)
    finish("w_branch_mla", mix_sums[1][0], rb_mix[1], w_branch_mla, m_w_branch_mla, v_w_branch_mla)
    finish("w_branch_fox", mix_sums[2][0], rb_mix[2], w_branch_fox, m_w_branch_fox, v_w_branch_fox)
    finish("w_in", late_sums[0][0], rb_late[0], w_in, m_w_in, v_w_in)
    finish("w_uq", late_sums[1][0], rb_late[1], w_uq, m_w_uq, v_w_uq)
    finish("w_ukv", late_sums[2][0], rb_late[2], w_ukv, m_w_ukv, v_w_ukv)
    finish("conv_w", late_sums[3][0], rb_late[3], conv_w, m_conv_w, v_conv_w)

    widths = [D, Q_LORA, KV_LORA, LANES, 2 * D, D, D, 2 * F, D]
    small_names = ["pre_mix_norm", "q_a_norm", "kv_a_norm", "b_forget", "b_gate", "post_mix_norm", "pre_ffn_norm",
                   "conv_b", "post_ffn_norm"]
    true_w = [D, Q_LORA, KV_LORA, HEADS, 2 * D, D, D, 2 * F, D]
    dcb = jnp.concatenate([dcp_g[3:4], dcp_v[3:4]], axis=1)
    part = jnp.concatenate([dg_pre, dg_q, dg_kv, dbfor, dbg_m, dbg_f, dg_pm, dg_ffn, dcb, dg_pf], axis=1)

    def pack(vals):
        return jnp.concatenate([row(a, wd) for a, wd in zip(vals, widths)], axis=1)

    sw = pack([pre_mix_norm, q_a_norm, kv_a_norm, b_forget, b_gate, post_mix_norm, pre_ffn_norm, conv_b, post_ffn_norm])
    sm = pack([m_pre_mix_norm, m_q_a_norm, m_kv_a_norm, m_b_forget, m_b_gate, m_post_mix_norm, m_pre_ffn_norm,
               m_conv_b, m_post_ffn_norm])
    sv = pack([v_pre_mix_norm, v_q_a_norm, v_kv_a_norm, v_b_forget, v_b_gate, v_post_mix_norm, v_pre_ffn_norm,
               v_conv_b, v_post_ffn_norm])
    (parts_all,) = _exchange_alone(_Comm([_DirectGatherPlan([part])]), "gather_small")
    sg, sd, smo, svo = _sum_adamw(None, parts_all, sw, sm, sv, "adamw_small")
    small_out = {}
    off = 0
    for n, wd, tw in zip(small_names, widths, true_w):
        small_out[n] = tuple(a[0, off:off + tw] for a in (sg, sd, smo, svo))
        off += wd

    loss = lax.psum(loss_part[0, 0], MESH_AXES)
    order = ["pre_mix_norm", "w_in", "q_a_norm", "w_uq", "kv_a_norm", "w_ukv", "b_forget", "b_gate", "w_branch_mla",
             "w_branch_fox", "w_out", "post_mix_norm", "pre_ffn_norm", "w_up", "conv_w", "conv_b", "w_down",
             "post_ffn_norm"]
    res = {**big_out, **small_out}
    outs = [loss, grad_x.reshape(B, S, D)]
    for kind in range(4):
        outs += [res[n][kind] for n in order]
    return tuple(outs)
```

```python
import math

import jax
import jax.numpy as jnp
from jax import lax
from jax.experimental import pallas as pl
from jax.experimental.pallas import tpu as pltpu

F32 = jnp.float32
BF16 = jnp.bfloat16

N_DEV = 8
N_CHIP = 4
HEADS = 8
NOPE = 128
ROPE = 64
HALF_ROPE = ROPE // 2
VDIM = 128
Q_LORA = 512
KV_LORA = 256
FOX_DIM = 128
ATT_DK = 256
MLA_UNIT = 64
ROPE_THETA = 10000.0
EPS = 1e-6
NEG_INF = -1e30
LANES = 128
LOG2_E = 1.4426950408889634

ADAM_LR = 0.001
ADAM_B1 = 0.9
ADAM_B2 = 0.999
ADAM_EPS = 1e-08
ADAM_WD = 0.01
ADAM_STEP = 10

VMEM_LIMIT_BYTES = 56 * 1024 * 1024
ROW_TILE = 256
HEAD_ROW_TILE = 1024
ATT_TILE = 512
ATT_SUB = 256
ATT_AHEAD = 3
MM_TILE = 1024

MESH_AXES = ("x", "y", "c")
ANY = pl.BlockSpec(memory_space=pl.ANY)


def _tile(n, pref, align=LANES):
    if n <= pref:
        return n
    t = (pref // align) * align
    while t >= align:
        if n % t == 0:
            return t
        t -= align
    return n


def _sds(shape, dtype):
    return jax.ShapeDtypeStruct(shape, dtype)


def _coords():
    x, y, c = (lax.axis_index(ax) for ax in MESH_AXES)
    return x, y, c


def _chip_rel(x, y, r):
    return (1 - x if r & 2 else x), (1 - y if r & 1 else y)


def _rcopy(src, dst, sems, w, k, dev):
    return pltpu.make_async_remote_copy(src_ref=src, dst_ref=dst, send_sem=sems[0].at[w, k], recv_sem=sems[1].at[w, k],
                                        device_id=dev, device_id_type=pl.DeviceIdType.MESH)


class _GatherPlan:
    def __init__(self, blocks, mid_frac=0.5):
        self.ins = list(blocks)
        self.out_shapes = [_sds((N_DEV,) + b.shape, b.dtype) for b in blocks]
        n = len(blocks)
        self.scratch = [pltpu.SemaphoreType.DMA((n, 7)), pltpu.SemaphoreType.DMA((n, 7)), pltpu.SemaphoreType.DMA((n,))]
        self.mid_frac = mid_frac

    def first(self, ins, outs, sems):
        x, y, c = _coords()
        me = 4 * x + 2 * y + c
        for w in range(len(ins)):
            pltpu.make_async_copy(ins[w], outs[w].at[me], sems[2].at[w]).start()
            _rcopy(ins[w], outs[w].at[me], sems, w, 0, (x, y, 1 - c)).start()
            for r in (1, 2, 3):
                px, py = _chip_rel(x, y, r)
                _rcopy(ins[w], outs[w].at[me], sems, w, r, (px, py, c)).start()

    def mid(self, ins, outs, sems):
        x, y, c = _coords()
        for w in range(len(ins)):
            for r in (1, 2, 3):
                px, py = _chip_rel(x, y, r)
                blk = outs[w].at[4 * px + 2 * py + c]
                _rcopy(ins[w], blk, sems, w, r, (px, py, c)).wait_recv()
                _rcopy(blk, blk, sems, w, 3 + r, (x, y, 1 - c)).start()

    def last(self, ins, outs, sems):
        x, y, c = _coords()
        me = 4 * x + 2 * y + c
        sib = (x, y, 1 - c)
        for w in range(len(ins)):
            _rcopy(ins[w], outs[w].at[4 * x + 2 * y + 1 - c], sems, w, 0, sib).wait_recv()
            for r in (1, 2, 3):
                px, py = _chip_rel(x, y, r)
                blk = outs[w].at[4 * px + 2 * py + 1 - c]
                _rcopy(blk, blk, sems, w, 3 + r, sib).wait_recv()
            for k in range(7):
                _rcopy(ins[w], outs[w].at[me], sems, w, k, sib).wait_send()
            pltpu.make_async_copy(ins[w], outs[w].at[me], sems[2].at[w]).wait()


class _GatherOwnPlan:
    mid = None

    def __init__(self, blocks):
        self.ins = list(blocks)
        self.out_shapes = [_sds((N_DEV,) + b.shape, b.dtype) for b in blocks]
        n = len(blocks)
        self.scratch = [pltpu.SemaphoreType.DMA((n, 4)), pltpu.SemaphoreType.DMA((n, 4)), pltpu.SemaphoreType.DMA((n,))]

    def first(self, ins, outs, sems):
        x, y, c = _coords()
        me = 4 * x + 2 * y + c
        for w in range(len(ins)):
            pltpu.make_async_copy(ins[w], outs[w].at[me], sems[2].at[w]).start()
            _rcopy(ins[w], outs[w].at[me], sems, w, 0, (x, y, 1 - c)).start()
            for r in (1, 2, 3):
                px, py = _chip_rel(x, y, r)
                _rcopy(ins[w], outs[w].at[me], sems, w, r, (px, py, c)).start()

    def last(self, ins, outs, sems):
        x, y, c = _coords()
        me = 4 * x + 2 * y + c
        for w in range(len(ins)):
            cp = _rcopy(ins[w], outs[w].at[4 * x + 2 * y + 1 - c], sems, w, 0, (x, y, 1 - c))
            cp.wait_recv()
            cp.wait_send()
            for r in (1, 2, 3):
                px, py = _chip_rel(x, y, r)
                cp = _rcopy(ins[w], outs[w].at[4 * px + 2 * py + c], sems, w, r, (px, py, c))
                cp.wait_recv()
                cp.wait_send()
            pltpu.make_async_copy(ins[w], outs[w].at[me], sems[2].at[w]).wait()


class _GatherPassPlan:
    mid = None

    def __init__(self, gathered):
        self.ins = list(gathered)
        self.out_shapes = [_sds(g.shape, g.dtype) for g in gathered]
        self.aliases = [(i, i) for i in range(len(gathered))]
        n = len(gathered)
        self.scratch = [pltpu.SemaphoreType.DMA((n, 3)), pltpu.SemaphoreType.DMA((n, 3))]

    def first(self, ins, outs, sems):
        x, y, c = _coords()
        for w in range(len(ins)):
            for r in (1, 2, 3):
                px, py = _chip_rel(x, y, r)
                blk = 4 * px + 2 * py + c
                _rcopy(ins[w].at[blk], outs[w].at[blk], sems, w, r - 1, (x, y, 1 - c)).start()

    def last(self, ins, outs, sems):
        x, y, c = _coords()
        for w in range(len(ins)):
            for r in (1, 2, 3):
                px, py = _chip_rel(x, y, r)
                blk = 4 * px + 2 * py + 1 - c
                cp = _rcopy(ins[w].at[blk], outs[w].at[blk], sems, w, r - 1, (x, y, 1 - c))
                cp.wait_recv()
                cp.wait_send()


class _DirectGatherPlan:
    mid = None

    def __init__(self, blocks):
        self.ins = list(blocks)
        self.out_shapes = [_sds((N_DEV,) + b.shape, b.dtype) for b in blocks]
        n = len(blocks)
        self.scratch = [pltpu.SemaphoreType.DMA((n, 7)), pltpu.SemaphoreType.DMA((n, 7)), pltpu.SemaphoreType.DMA((n,))]

    @staticmethod
    def _peer(x, y, c, r):
        return (1 - x if r & 4 else x), (1 - y if r & 2 else y), (1 - c if r & 1 else c)

    def first(self, ins, outs, sems):
        x, y, c = _coords()
        me = 4 * x + 2 * y + c
        for w in range(len(ins)):
            pltpu.make_async_copy(ins[w], outs[w].at[me], sems[2].at[w]).start()
            for r in range(1, N_DEV):
                _rcopy(ins[w], outs[w].at[me], sems, w, r - 1, self._peer(x, y, c, r)).start()

    def last(self, ins, outs, sems):
        x, y, c = _coords()
        me = 4 * x + 2 * y + c
        for w in range(len(ins)):
            for r in range(1, N_DEV):
                px, py, pc = self._peer(x, y, c, r)
                cp = _rcopy(ins[w], outs[w].at[4 * px + 2 * py + pc], sems, w, r - 1, (px, py, pc))
                cp.wait_recv()
                cp.wait_send()
            pltpu.make_async_copy(ins[w], outs[w].at[me], sems[2].at[w]).wait()


class _PairScatterPlan:
    mid = None

    def __init__(self, pieces):
        self.ins = list(pieces)
        self.out_shapes = [_sds((N_CHIP,) + p.shape[1:], p.dtype) for p in pieces]
        n = len(pieces)
        self.scratch = [pltpu.SemaphoreType.DMA((n, N_CHIP)), pltpu.SemaphoreType.DMA((n, N_CHIP))]

    def _copies(self, ins, outs, sems):
        x, y, c = _coords()
        return [_rcopy(ins[w].at[2 * q + 1 - c], outs[w].at[q], sems, w, q, (x, y, 1 - c))
                for w in range(len(ins)) for q in range(N_CHIP)]

    def first(self, ins, outs, sems):
        for cp in self._copies(ins, outs, sems):
            cp.start()

    def last(self, ins, outs, sems):
        for cp in self._copies(ins, outs, sems):
            cp.wait_recv()
            cp.wait_send()


class _ChipScatterPlan:
    mid = None

    def __init__(self, sums):
        self.ins = list(sums)
        self.out_shapes = [_sds(s.shape, s.dtype) for s in sums]
        n = len(sums)
        self.scratch = [pltpu.SemaphoreType.DMA((n, 3)), pltpu.SemaphoreType.DMA((n, 3))]

    def _copies(self, ins, outs, sems):
        x, y, c = _coords()
        cps = []
        for w in range(len(ins)):
            for r in (1, 2, 3):
                px, py = _chip_rel(x, y, r)
                cps.append(_rcopy(ins[w].at[r - 1], outs[w].at[r - 1], sems, w, r - 1, (px, py, c)))
        return cps

    def first(self, ins, outs, sems):
        for cp in self._copies(ins, outs, sems):
            cp.start()

    def last(self, ins, outs, sems):
        for cp in self._copies(ins, outs, sems):
            cp.wait_recv()
            cp.wait_send()


class _Comm:
    def __init__(self, plans):
        self.plans = list(plans)
        self.ins = [a for p in self.plans for a in p.ins]
        self.out_shapes = [s for p in self.plans for s in p.out_shapes]
        self.scratch = [s for p in self.plans for s in p.scratch]
        self.aliases = []
        i = o = 0
        for p in self.plans:
            self.aliases += [(i + a, o + b) for a, b in getattr(p, "aliases", [])]
            i, o = i + len(p.ins), o + len(p.out_shapes)

    def _parts(self, ins, outs, sems):
        i = o = s = 0
        for p in self.plans:
            yield p, ins[i:i + len(p.ins)], outs[o:o + len(p.out_shapes)], sems[s:s + len(p.scratch)]
            i, o, s = i + len(p.ins), o + len(p.out_shapes), s + len(p.scratch)

    def begin(self, step, nsteps, ins, outs, sems):
        @pl.when(step == 0)
        def _():
            for p, pi, po, ps in self._parts(ins, outs, sems):
                p.first(pi, po, ps)

        for p, pi, po, ps in self._parts(ins, outs, sems):
            if p.mid is not None:
                @pl.when(step == min(nsteps - 1, int(p.mid_frac * nsteps)))
                def _(p=p, pi=pi, po=po, ps=ps):
                    p.mid(pi, po, ps)

    def end(self, step, nsteps, ins, outs, sems):
        @pl.when(step == nsteps - 1)
        def _():
            for p, pi, po, ps in self._parts(ins, outs, sems):
                p.last(pi, po, ps)


def _call(body, args, *, name, grid, in_specs, out_specs, out_shape, scratch_shapes=(), sem=None, comm=None):
    in_specs, out_specs, out_shape, scratch_shapes = list(in_specs), list(out_specs), list(out_shape), list(scratch_shapes)
    if comm is None:
        res = pl.pallas_call(
            body, name=name, grid=grid, in_specs=in_specs, out_specs=out_specs, out_shape=out_shape,
            scratch_shapes=scratch_shapes,
            compiler_params=pltpu.CompilerParams(dimension_semantics=sem, vmem_limit_bytes=VMEM_LIMIT_BYTES),
        )(*args)
        return list(res), []
    n_in, n_out, n_sc = len(in_specs), len(out_specs), len(scratch_shapes)
    n_ci, n_co = len(comm.ins), len(comm.out_shapes)
    nsteps = math.prod(grid)

    def hosted(*refs):
        ins, cins = refs[:n_in], refs[n_in:n_in + n_ci]
        o0 = n_in + n_ci
        outs, couts = refs[o0:o0 + n_out], refs[o0 + n_out:o0 + n_out + n_co]
        s0 = o0 + n_out + n_co
        scr, csems = refs[s0:s0 + n_sc], refs[s0 + n_sc:]
        step = jnp.int32(0)
        for d in range(len(grid)):
            step = step * grid[d] + pl.program_id(d)
        comm.begin(step, nsteps, cins, couts, csems)
        body(*ins, *outs, *scr)
        comm.end(step, nsteps, cins, couts, csems)

    res = pl.pallas_call(
        hosted, name=name, grid=grid, in_specs=in_specs + [ANY] * n_ci, out_specs=out_specs + [ANY] * n_co,
        out_shape=out_shape + comm.out_shapes, scratch_shapes=scratch_shapes + comm.scratch,
        input_output_aliases={n_in + a: n_out + b for a, b in comm.aliases},
        compiler_params=pltpu.CompilerParams(dimension_semantics=("arbitrary",) * len(grid),
                                             vmem_limit_bytes=VMEM_LIMIT_BYTES, has_side_effects=True),
    )(*args, *comm.ins)
    return list(res[:n_out]), list(res[n_out:])


def _exchange_alone(comm, name):
    def body():
        pass

    return _call(body, [], name=name, grid=(), in_specs=[], out_specs=[], out_shape=[], comm=comm)[1]


def _matmul(a, b, *, mode, name, out_dtype=F32, out_blocks=None, tm=None, tn=None, tk=None, comm=None):
    tm = MM_TILE if tm is None else tm
    tn = MM_TILE if tn is None else tn
    a_blk = a.ndim == 3
    b_blk = b.ndim == 3
    if mode == "nn":
        M, K = a.shape
        N = b.shape[0] * b.shape[2] if b_blk else b.shape[1]
        dims = (((1,), (0,)), ((), ()))
    elif mode == "nt":
        M = a.shape[1] if a_blk else a.shape[0]
        K = a.shape[0] * a.shape[2] if a_blk else a.shape[1]
        N = b.shape[1] if b_blk else b.shape[0]
        dims = (((1,), (1,)), ((), ()))
    else:
        K, M = a.shape
        N = b.shape[0] * b.shape[2] if b_blk else b.shape[1]
        dims = (((0,), (0,)), ((), ()))

    tm = _tile(M, tm)
    tn = _tile(N, tn)
    if mode == "nt" and (a_blk or b_blk):
        tk = a.shape[2] if a_blk else b.shape[2]
    else:
        tk = _tile(K, K if tk is None else tk)
    if mode != "nt" and b_blk:
        tn = _tile(b.shape[2], tn)
    if out_blocks is not None:
        tn = _tile(out_blocks, tn)
    nk = K // tk
    grid = (M // tm, N // tn, nk)

    if mode == "nn":
        a_spec = pl.BlockSpec((tm, tk), lambda i, j, k: (i, k))
        if b_blk:
            rb = b.shape[2] // tn
            b_spec = pl.BlockSpec((None, tk, tn), lambda i, j, k: (j // rb, k, j % rb))
        else:
            b_spec = pl.BlockSpec((tk, tn), lambda i, j, k: (k, j))
    elif mode == "nt":
        if a_blk:
            a_spec = pl.BlockSpec((None, tm, tk), lambda i, j, k: (k, i, 0))
        else:
            a_spec = pl.BlockSpec((tm, tk), lambda i, j, k: (i, k))
        if b_blk:
            b_spec = pl.BlockSpec((None, tn, tk), lambda i, j, k: (k, j, 0))
        else:
            b_spec = pl.BlockSpec((tn, tk), lambda i, j, k: (j, k))
    else:
        a_spec = pl.BlockSpec((tk, tm), lambda i, j, k: (k, i))
        if b_blk:
            rb = b.shape[2] // tn
            b_spec = pl.BlockSpec((None, tk, tn), lambda i, j, k: (j // rb, k, j % rb))
        else:
            b_spec = pl.BlockSpec((tk, tn), lambda i, j, k: (k, j))

    if out_blocks is None:
        o_spec = pl.BlockSpec((tm, tn), lambda i, j, k: (i, j))
        o_shape = _sds((M, N), out_dtype)
    else:
        ro = out_blocks // tn
        o_spec = pl.BlockSpec((None, tm, tn), lambda i, j, k: (j // ro, i, j % ro))
        o_shape = _sds((N // out_blocks, M, out_blocks), out_dtype)

    direct = nk == 1 or out_dtype == F32

    def body(a_ref, b_ref, o_ref, *scratch):
        if nk == 1:
            o_ref[...] = lax.dot_general(a_ref[...], b_ref[...], dims, preferred_element_type=F32).astype(o_ref.dtype)
            return
        acc_ref = o_ref if direct else scratch[0]
        k = pl.program_id(2)

        @pl.when(k == 0)
        def _():
            acc_ref[...] = jnp.zeros(acc_ref.shape, F32)

        acc_ref[...] += lax.dot_general(a_ref[...], b_ref[...], dims, preferred_element_type=F32)
        if not direct:
            @pl.when(k == nk - 1)
            def _():
                o_ref[...] = acc_ref[...].astype(o_ref.dtype)

    scratch = [] if direct else [pltpu.VMEM((tm, tn), F32)]
    outs, landed = _call(body, [a, b], name=name, grid=grid, in_specs=[a_spec, b_spec], out_specs=[o_spec],
                         out_shape=[o_shape], scratch_shapes=scratch, sem=("parallel", "parallel", "arbitrary"), comm=comm)
    return outs[0] if comm is None else (outs[0], landed)


def _rms(x):
    return lax.rsqrt(jnp.mean(x * x, axis=-1, keepdims=True) + EPS)


def _rms_bwd(dy, x, g):
    r = _rms(x)
    n = x * r
    dn = dy * g
    dx = r * (dn - n * jnp.mean(dn * n, axis=-1, keepdims=True))
    return dx, dy * n


def _sigmoid(x):
    return 1.0 / (1.0 + jnp.exp(-x))


def _rope_rot(t):
    return pltpu.roll(t, HALF_ROPE, 1) - pltpu.roll(t, LANES - HALF_ROPE, 1)


def _lane(shape):
    return lax.broadcasted_iota(jnp.int32, shape, 1)


def _split3(x):
    hi = x.astype(BF16).astype(F32)
    r1 = x - hi
    mid = r1.astype(BF16).astype(F32)
    lo = (r1 - mid).astype(BF16).astype(F32)
    return hi, mid, lo


def _cumsum_rows(x, reverse):
    S = x.shape[0]
    bs = min(256, S)
    nb = S // bs
    r = lax.broadcasted_iota(jnp.int32, (bs, bs), 0)
    c = lax.broadcasted_iota(jnp.int32, (bs, bs), 1)
    tri = jnp.where((c >= r) if reverse else (c <= r), 1.0, 0.0).astype(BF16)
    edge = lax.broadcasted_iota(jnp.int32, (bs, x.shape[1]), 0) == (0 if reverse else bs - 1)
    carry = jnp.zeros((1, x.shape[1]), F32)
    outs = [None] * nb
    for bi in (range(nb - 1, -1, -1) if reverse else range(nb)):
        xb = x[bi * bs:(bi + 1) * bs, :]
        acc = carry
        for term in _split3(xb):
            acc = acc + jnp.dot(tri, term.astype(BF16), preferred_element_type=F32)
        outs[bi] = acc
        carry = jnp.sum(jnp.where(edge, acc, 0.0), axis=0, keepdims=True)
    return jnp.concatenate(outs, axis=0) if nb > 1 else outs[0]


def _gelu_parts(x):
    c0 = math.sqrt(2.0 / math.pi)
    inner = c0 * (x + 0.044715 * (x * x * x))
    t = jnp.tanh(inner)
    g = 0.5 * x * (1.0 + t)
    dg = 0.5 * (1.0 + t) + 0.5 * x * (1.0 - t * t) * (c0 * (1.0 + 3.0 * 0.044715 * (x * x)))
    return g, dg


def _accumulate(ref, value, first):
    @pl.when(first)
    def _():
        ref[...] = value

    @pl.when(jnp.logical_not(first))
    def _():
        ref[...] += value


def _cast_bf16(w, name):
    R, C = w.shape
    tr = _tile(R, 512, 16)

    def body(w_ref, o_ref):
        o_ref[...] = w_ref[...].astype(BF16)

    blk = pl.BlockSpec((tr, C), lambda i: (i, 0))
    return _call(body, [w], name=name, grid=(R // tr,), in_specs=[blk], out_specs=[blk],
                 out_shape=[_sds((R, C), BF16)], sem=("parallel",))[0][0]


def _concat_cols(parts, name):
    T = parts[0].shape[0]
    widths = [p.shape[1] for p in parts]
    tm = _tile(T, ROW_TILE, 16)

    def body(*refs):
        o_ref = refs[-1]
        off = 0
        for p_ref, w in zip(refs[:-1], widths):
            o_ref[:, off:off + w] = p_ref[...].astype(BF16)
            off += w

    return _call(body, parts, name=name, grid=(T // tm,),
                 in_specs=[pl.BlockSpec((tm, w), lambda i: (i, 0)) for w in widths],
                 out_specs=[pl.BlockSpec((tm, sum(widths)), lambda i: (i, 0))],
                 out_shape=[_sds((T, sum(widths)), BF16)], sem=("parallel",))[0][0]


def _prenorm(x, g, comm=None):
    T, D = x.shape
    tm = _tile(T, ROW_TILE, 16)

    def body(x_ref, g_ref, h_ref):
        xv = x_ref[...]
        h_ref[...] = (xv * _rms(xv) * g_ref[...]).astype(BF16)

    row = pl.BlockSpec((tm, D), lambda i: (i, 0))
    (h,), landed = _call(body, [x, g], name="prenorm", grid=(T // tm,),
                         in_specs=[row, pl.BlockSpec((1, D), lambda i: (0, 0))], out_specs=[row],
                         out_shape=[_sds((T, D), BF16)], sem=("parallel",), comm=comm)
    return h, landed


def _split_prep(proj, pos, invf, gq, gkv, bfor, lay, comm=None):
    T = proj.shape[0]
    tm = _tile(T, ROW_TILE, 16)

    def body(q_ref, kv_ref, kpe_ref, fl_ref, pos_ref, invf_ref, gq_ref, gkv_ref, bf_ref,
             qn_ref, kvn_ref, kper_ref, logf_ref, cos_ref, sin_ref):
        ql = q_ref[...]
        qn_ref[...] = (ql * _rms(ql) * gq_ref[...]).astype(BF16)
        kl = kv_ref[...]
        kvn_ref[...] = (kl * _rms(kl) * gkv_ref[...]).astype(BF16)
        ang = pos_ref[...].astype(F32) * invf_ref[...]
        valid = _lane(ang.shape) < ROPE
        cs = jnp.where(valid, jnp.cos(ang), 0.0)
        sn = jnp.where(valid, jnp.sin(ang), 0.0)
        cos_ref[...] = cs
        sin_ref[...] = sn
        kp = jnp.where(valid, kpe_ref[...], 0.0)
        kper_ref[...] = (kp * cs + _rope_rot(kp) * sn).astype(BF16)
        z = fl_ref[...] + bf_ref[...]
        logf_ref[...] = jnp.minimum(z, 0.0) - jnp.log(1.0 + jnp.exp(-jnp.abs(z)))

    def col(width, off):
        return pl.BlockSpec((tm, width), lambda i: (i, off // width))

    def vec(width):
        return pl.BlockSpec((1, width), lambda i: (0, 0))

    def out(width):
        return pl.BlockSpec((tm, width), lambda i: (i, 0))

    return _call(
        body, [proj, proj, proj, proj, pos, invf, gq, gkv, bfor], name="split_prep", grid=(T // tm,),
        in_specs=[col(Q_LORA, lay["q"]), col(KV_LORA, lay["kv"]), col(LANES, lay["kpe"]), col(LANES, lay["fl"]),
                  pl.BlockSpec((tm, 1), lambda i: (i, 0)), vec(LANES), vec(Q_LORA), vec(KV_LORA), vec(LANES)],
        out_specs=[out(Q_LORA), out(KV_LORA), out(LANES), out(LANES), out(LANES), out(LANES)],
        out_shape=[_sds((T, Q_LORA), BF16), _sds((T, KV_LORA), BF16), _sds((T, LANES), BF16),
                   _sds((T, LANES), F32), _sds((T, LANES), F32), _sds((T, LANES), F32)],
        sem=("parallel",), comm=comm)


def _mla_prep(qraw, kvraw, kper, cosT, sinT):
    H, T, _ = qraw.shape
    tm = _tile(T, HEAD_ROW_TILE, 16)

    def body(q_ref, kv_ref, kpe_ref, cos_ref, sin_ref, qo_ref, ko_ref, vo_ref):
        q = q_ref[...]
        pe = q[:, NOPE:]
        pe = jnp.where(_lane(pe.shape) < ROPE, pe, 0.0)
        qo_ref[:, :NOPE] = q[:, :NOPE].astype(BF16)
        qo_ref[:, NOPE:] = (pe * cos_ref[...] + _rope_rot(pe) * sin_ref[...]).astype(BF16)
        kv = kv_ref[...]
        ko_ref[:, :NOPE] = kv[:, :NOPE].astype(BF16)
        ko_ref[:, NOPE:] = kpe_ref[...]
        vo_ref[...] = kv[:, NOPE:].astype(BF16)

    head = pl.BlockSpec((None, tm, ATT_DK), lambda h, i: (h, i, 0))
    tok = pl.BlockSpec((tm, LANES), lambda h, i: (i, 0))
    return _call(
        body, [qraw, kvraw, kper, cosT, sinT], name="mla_prep", grid=(H, T // tm),
        in_specs=[head, head, tok, tok, tok],
        out_specs=[head, head, pl.BlockSpec((None, tm, VDIM), lambda h, i: (h, i, 0))],
        out_shape=[_sds((H, T, ATT_DK), BF16), _sds((H, T, ATT_DK), BF16), _sds((H, T, VDIM), BF16)],
        sem=("parallel", "parallel"))[0]


def _fox_cumsum(logf, B, S, inv_scale):
    T = logf.shape[0]

    def body(l_ref, c_ref):
        c_ref[...] = _cumsum_rows(l_ref[...], reverse=False) * inv_scale

    seq = pl.BlockSpec((S, LANES), lambda b: (b, 0))
    return _call(body, [logf], name="fox_cumsum", grid=(B,), in_specs=[seq], out_specs=[seq],
                 out_shape=[_sds((T, LANES), F32)], sem=("parallel",))[0][0]


def _fox_prep(proj, cs, lay):
    T = proj.shape[0]
    tm = _tile(T, HEAD_ROW_TILE, 16)

    def body(q_ref, k_ref, v_ref, cs_ref, qo_ref, ko_ref, vo_ref):
        h = pl.program_id(0)
        cv = cs_ref[...]
        lane = _lane(cv.shape)
        ccol = jnp.sum(jnp.where(lane == h, cv, 0.0), axis=1, keepdims=True)
        hi, mid, lo = _split3(ccol)
        one = jnp.where(lane < 6, 1.0, 0.0)
        augq = jnp.where(lane == 0, hi, jnp.where(lane == 1, mid, jnp.where(lane == 2, lo, one)))
        augk = jnp.where(lane < 3, 1.0, jnp.where(lane == 3, -hi, jnp.where(lane == 4, -mid, jnp.where(lane == 5, -lo, 0.0))))
        qo_ref[:, :FOX_DIM] = q_ref[...].astype(BF16)
        qo_ref[:, FOX_DIM:] = augq.astype(BF16)
        ko_ref[:, :FOX_DIM] = k_ref[...].astype(BF16)
        ko_ref[:, FOX_DIM:] = augk.astype(BF16)
        vo_ref[...] = v_ref[...].astype(BF16)

    def col(off):
        return pl.BlockSpec((tm, FOX_DIM), lambda h, i: (i, off // FOX_DIM + h))

    head = pl.BlockSpec((None, tm, ATT_DK), lambda h, i: (h, i, 0))
    return _call(
        body, [proj, proj, proj, cs], name="fox_prep", grid=(HEADS, T // tm),
        in_specs=[col(lay["fq"]), col(lay["fk"]), col(lay["fv"]), pl.BlockSpec((tm, LANES), lambda h, i: (i, 0))],
        out_specs=[head, head, pl.BlockSpec((None, tm, VDIM), lambda h, i: (h, i, 0))],
        out_shape=[_sds((HEADS, T, ATT_DK), BF16), _sds((HEADS, T, ATT_DK), BF16), _sds((HEADS, T, VDIM), BF16)],
        sem=("parallel", "parallel"))[0]


def _visible(tq, tk, unit):
    r = lax.broadcasted_iota(jnp.int32, (tq, tk), 0)
    c = lax.broadcasted_iota(jnp.int32, (tq, tk), 1)
    sh = int(math.log2(unit))
    return lax.shift_right_logical(c, sh) <= lax.shift_right_logical(r, sh)


def _attn_fwd(streams, *, B, S, name, comm=None):
    n = len(streams)
    H, T, DK = streams[0][0].shape
    DV = streams[0][2].shape[2]
    tq = _tile(S, ATT_TILE)
    nq = S // tq
    sub = min(ATT_SUB, tq)
    NT = (((1,), (1,)), ((), ()))

    def body(*refs):
        ins, outs, (m_sc, acc_sc) = refs[:3 * n], refs[3 * n:5 * n], refs[5 * n:]
        i, j = pl.program_id(1), pl.program_id(2)

        @pl.when(j == 0)
        def _():
            m_sc[...] = jnp.full(m_sc.shape, NEG_INF, F32)
            acc_sc[...] = jnp.zeros(acc_sc.shape, F32)

        def step(diagonal):
            work = [(t, r) for r in range(tq // sub) for t in range(n)]

            def scores(t, r):
                q_ref, k_ref, _ = ins[3 * t:3 * t + 3]
                kc = (r + 1) * sub if diagonal else tq
                s = lax.dot_general(q_ref[r * sub:(r + 1) * sub, :], k_ref[0:kc, :], NT, preferred_element_type=F32)
                return s * (streams[t][4] * LOG2_E)

            ahead = [scores(*work[w]) for w in range(min(ATT_AHEAD, len(work)))]
            for w, (t, r) in enumerate(work):
                s = ahead.pop(0)
                if w + ATT_AHEAD < len(work):
                    ahead.append(scores(*work[w + ATT_AHEAD]))
                v_ref = ins[3 * t + 2]
                kc = s.shape[1]
                rows = slice(r * sub, (r + 1) * sub)
                if diagonal:
                    own = jnp.where(_visible(sub, sub, streams[t][3]), s[:, kc - sub:], NEG_INF)
                    s = own if kc == sub else jnp.concatenate([s[:, :kc - sub], own], axis=1)
                m_prev = m_sc[t, rows, :]
                mx = s[:, 0:LANES]
                for g in range(1, kc // LANES):
                    mx = jnp.maximum(mx, s[:, g * LANES:(g + 1) * LANES])
                m_new = jnp.maximum(m_prev, jnp.max(mx, axis=1, keepdims=True))
                alpha = jnp.exp2(m_prev - m_new)
                p = jnp.exp2(s - jnp.tile(m_new, (1, kc // LANES))).astype(BF16)
                v_aug = jnp.concatenate([v_ref[0:kc, :], jnp.ones((kc, LANES), BF16)], axis=1)
                acc_sc[t, rows, :] = jnp.tile(alpha, (1, 2)) * acc_sc[t, rows, :] + jnp.dot(
                    p, v_aug, preferred_element_type=F32)
                m_sc[t, rows, :] = m_new

        @pl.when(j < i)
        def _():
            step(False)

        @pl.when(j == i)
        def _():
            step(True)
            for t in range(n):
                o_ref, lse_ref = outs[2 * t:2 * t + 2]
                l = acc_sc[t, :, DV:]
                o_ref[...] = (acc_sc[t, :, :DV] / l).astype(BF16)
                lse_ref[...] = m_sc[t] + jnp.log2(l)

    def qmap(g, i, j):
        return (g % H, (g // H) * nq + i, 0)

    def kmap(g, i, j):
        return (g % H, (g // H) * nq + jnp.minimum(j, i), 0)

    args = [a for st in streams for a in st[:3]]
    outs, landed = _call(
        body, args, name=name, grid=(B * H, nq, nq),
        in_specs=[pl.BlockSpec((None, tq, DK), qmap), pl.BlockSpec((None, tq, DK), kmap),
                  pl.BlockSpec((None, tq, DV), kmap)] * n,
        out_specs=[pl.BlockSpec((tq, DV), lambda g, i, j: ((g // H) * nq + i, g % H)),
                   pl.BlockSpec((None, tq, LANES), qmap)] * n,
        out_shape=[_sds((T, H * DV), BF16), _sds((H, T, LANES), F32)] * n,
        scratch_shapes=[pltpu.VMEM((n, tq, LANES), F32), pltpu.VMEM((n, tq, DV + LANES), F32)],
        sem=("parallel", "parallel", "arbitrary"), comm=comm)
    return [(outs[2 * t], outs[2 * t + 1]) for t in range(n)], landed


def _attn_bwd(streams, *, B, S, name, comm=None):
    n = len(streams)
    H, T, DK = streams[0][0].shape
    DV = streams[0][2].shape[2]
    tq = _tile(S, ATT_TILE)
    nq = S // tq
    sub = min(ATT_SUB, tq)
    NT = (((1,), (1,)), ((), ()))
    TN = (((0,), (0,)), ((), ()))

    def body(*refs):
        ins, outs = refs[:6 * n], refs[6 * n:]
        j, i = pl.program_id(1), pl.program_id(2)

        @pl.when(jnp.logical_and(j == 0, i == 0))
        def _():
            for t in range(n):
                outs[3 * t][...] = jnp.zeros(outs[3 * t].shape, F32)

        @pl.when(i == 0)
        def _():
            for t in range(n):
                outs[3 * t + 1][...] = jnp.zeros(outs[3 * t + 1].shape, F32)
                outs[3 * t + 2][...] = jnp.zeros(outs[3 * t + 2].shape, F32)

        def step(diagonal):
            work = [(t, r) for r in range(tq // sub) for t in range(n)]

            def kcols(r):
                return (r + 1) * sub if diagonal else tq

            def scores(t, r):
                q_ref, k_ref, v_ref, _, do_ref, _ = ins[6 * t:6 * t + 6]
                rows, kc = slice(r * sub, (r + 1) * sub), kcols(r)
                s = lax.dot_general(q_ref[rows, :], k_ref[0:kc, :], NT, preferred_element_type=F32)
                dp = lax.dot_general(do_ref[rows, :], v_ref[0:kc, :], NT, preferred_element_type=F32)
                return s * (streams[t][7] * LOG2_E), dp

            def probs(t, r, s, dp):
                _, _, _, o_ref, do_ref, lse_ref = ins[6 * t:6 * t + 6]
                rows, kc = slice(r * sub, (r + 1) * sub), kcols(r)
                if diagonal:
                    own = jnp.where(_visible(sub, sub, streams[t][6]), s[:, kc - sub:], NEG_INF)
                    s = own if kc == sub else jnp.concatenate([s[:, :kc - sub], own], axis=1)
                p = jnp.exp2(s - jnp.tile(lse_ref[rows, :], (1, kc // LANES)))
                delta = jnp.sum(do_ref[rows, :].astype(F32) * o_ref[rows, :].astype(F32), axis=1, keepdims=True)
                return p.astype(BF16), (p * (dp - delta) * streams[t][7]).astype(BF16)

            def grads(t, r, p, ds):
                q_ref, k_ref, _, _, do_ref, _ = ins[6 * t:6 * t + 6]
                dq_ref, dk_ref, dv_ref = outs[3 * t:3 * t + 3]
                rows, kc = slice(r * sub, (r + 1) * sub), kcols(r)
                dv_ref[0:kc, :] += lax.dot_general(p, do_ref[rows, :], TN, preferred_element_type=F32)
                dk_ref[0:kc, :] += lax.dot_general(ds, q_ref[rows, :], TN, preferred_element_type=F32)
                qrows = pl.ds(pl.multiple_of(i * tq + r * sub, sub), sub)
                dq_ref[qrows, :] += jnp.dot(ds, k_ref[0:kc, :], preferred_element_type=F32)

            nw = len(work)
            sc = {w: scores(*work[w]) for w in range(min(2, nw))}
            pr = {0: probs(*work[0], *sc.pop(0))}
            for w in range(nw):
                if w + 2 < nw:
                    sc[w + 2] = scores(*work[w + 2])
                if w + 1 < nw:
                    pr[w + 1] = probs(*work[w + 1], *sc.pop(w + 1))
                grads(*work[w], *pr.pop(w))

        @pl.when(i > j)
        def _():
            step(False)

        @pl.when(i == j)
        def _():
            step(True)

    def qmap(g, j, i):
        return (g % H, (g // H) * nq + jnp.maximum(i, j), 0)

    def kmap(g, j, i):
        return (g % H, (g // H) * nq + j, 0)

    def omap(g, j, i):
        return ((g // H) * nq + jnp.maximum(i, j), g % H)

    args = [a for st in streams for a in st[:6]]
    outs, landed = _call(
        body, args, name=name, grid=(B * H, nq, nq),
        in_specs=[pl.BlockSpec((None, tq, DK), qmap), pl.BlockSpec((None, tq, DK), kmap),
                  pl.BlockSpec((None, tq, DV), kmap), pl.BlockSpec((tq, DV), omap), pl.BlockSpec((tq, DV), omap),
                  pl.BlockSpec((None, tq, LANES), qmap)] * n,
        out_specs=[pl.BlockSpec((None, S, DK), lambda g, j, i: (g % H, g // H, 0)),
                   pl.BlockSpec((None, tq, DK), kmap), pl.BlockSpec((None, tq, DV), kmap)] * n,
        out_shape=[_sds((H, T, DK), F32), _sds((H, T, DK), F32), _sds((H, T, DV), F32)] * n,
        sem=("parallel", "arbitrary", "arbitrary"), comm=comm)
    return [tuple(outs[3 * t:3 * t + 3]) for t in range(n)], landed


def _gate_merge(am, af, proj, bgate, lay, D, comm=None):
    T = am.shape[0]
    tm = _tile(T, ROW_TILE, 16)
    tn = _tile(D, 1024)

    def body(am_ref, af_ref, gm_ref, gf_ref, bm_ref, bf_ref, o_ref):
        sm = _sigmoid(gm_ref[...] + bm_ref[...])
        sf = _sigmoid(gf_ref[...] + bf_ref[...])
        o_ref[...] = (sm * am_ref[...] + sf * af_ref[...]).astype(BF16)

    og = lay["g"] // tn
    blk = pl.BlockSpec((tm, tn), lambda i, j: (i, j))
    return _call(
        body, [am, af, proj, proj, bgate, bgate], name="gate_merge", grid=(T // tm, D // tn),
        in_specs=[blk, blk, pl.BlockSpec((tm, tn), lambda i, j: (i, og + j)),
                  pl.BlockSpec((tm, tn), lambda i, j: (i, og + D // tn + j)),
                  pl.BlockSpec((1, tn), lambda i, j: (0, j)), pl.BlockSpec((1, tn), lambda i, j: (0, D // tn + j))],
        out_specs=[blk], out_shape=[_sds((T, D), BF16)], sem=("parallel", "parallel"), comm=comm)


def _mid(x, y1, g_pm, g_ffn):
    T, D = x.shape
    tm = _tile(T, ROW_TILE, 16)

    def body(x_ref, y_ref, gp_ref, gf_ref, x1_ref, h2_ref):
        y = y_ref[...]
        x1 = x_ref[...] + y * _rms(y) * gp_ref[...]
        x1_ref[...] = x1
        h2_ref[...] = (x1 * _rms(x1) * gf_ref[...]).astype(BF16)

    row = pl.BlockSpec((tm, D), lambda i: (i, 0))
    vec = pl.BlockSpec((1, D), lambda i: (0, 0))
    return _call(body, [x, y1, g_pm, g_ffn], name="mid", grid=(T // tm,), in_specs=[row, row, vec, vec],
                 out_specs=[row, row], out_shape=[_sds((T, D), F32), _sds((T, D), BF16)], sem=("parallel",))[0]


def _conv3(u, w_ref, bias):
    row = lax.broadcasted_iota(jnp.int32, u.shape, 0)
    u1 = jnp.where(row >= 1, pltpu.roll(u, 1, 0), 0.0)
    u2 = jnp.where(row >= 2, pltpu.roll(u, 2, 0), 0.0)
    return w_ref[0:1, :] * u2 + w_ref[1:2, :] * u1 + w_ref[2:3, :] * u + bias, u1, u2


def _convffn_fwd(u, cw, cb, B, S, F, comm=None):
    T = u.shape[0]
    tn = _tile(F, 256)
    nf = F // tn

    def body(ug_ref, uv_ref, wg_ref, wv_ref, bg_ref, bv_ref, a_ref):
        g, _, _ = _conv3(ug_ref[...], wg_ref, bg_ref[...])
        val, _, _ = _conv3(uv_ref[...], wv_ref, bv_ref[...])
        a_ref[...] = (_gelu_parts(g)[0] * val).astype(BF16)

    def seq(off):
        return pl.BlockSpec((S, tn), lambda b, j: (b, off + j))

    def par(rows, off):
        return pl.BlockSpec((rows, tn), lambda b, j: (0, off + j))

    return _call(body, [u, u, cw, cw, cb, cb], name="convffn_fwd", grid=(B, nf),
                 in_specs=[seq(0), seq(nf), par(3, 0), par(3, nf), par(1, 0), par(1, nf)],
                 out_specs=[seq(0)], out_shape=[_sds((T, F), BF16)], sem=("parallel", "parallel"), comm=comm)


def _convffn_bwd(u, dact, cw, cb, B, S, F, comm=None):
    T = u.shape[0]
    tn = _tile(F, 256)
    nf = F // tn

    def body(ug_ref, uv_ref, da_ref, wg_ref, wv_ref, bg_ref, bv_ref, dug_ref, duv_ref, dpg_ref, dpv_ref):
        b = pl.program_id(1)
        ug, uv, da = ug_ref[...], uv_ref[...], da_ref[...]
        g, ug1, ug2 = _conv3(ug, wg_ref, bg_ref[...])
        val, uv1, uv2 = _conv3(uv, wv_ref, bv_ref[...])
        gel, dgel = _gelu_parts(g)
        dg = da * val * dgel
        dval = da * gel
        row = lax.broadcasted_iota(jnp.int32, ug.shape, 0)

        def back(d, w_ref):
            d1 = jnp.where(row < S - 1, pltpu.roll(d, S - 1, 0), 0.0)
            d2 = jnp.where(row < S - 2, pltpu.roll(d, S - 2, 0), 0.0)
            return w_ref[2:3, :] * d + w_ref[1:2, :] * d1 + w_ref[0:1, :] * d2

        dug_ref[...] = back(dg, wg_ref).astype(BF16)
        duv_ref[...] = back(dval, wv_ref).astype(BF16)

        def sums(d, u0, u1, u2):
            r8 = lax.broadcasted_iota(jnp.int32, (8, d.shape[1]), 0)
            out = jnp.zeros((8, d.shape[1]), F32)
            for k, t in enumerate((d * u2, d * u1, d * u0, d)):
                out = jnp.where(r8 == k, jnp.sum(t, axis=0, keepdims=True), out)
            return out

        _accumulate(dpg_ref, sums(dg, ug, ug1, ug2), b == 0)
        _accumulate(dpv_ref, sums(dval, uv, uv1, uv2), b == 0)

    def seq(off):
        return pl.BlockSpec((S, tn), lambda j, b: (b, off + j))

    def par(rows, off):
        return pl.BlockSpec((rows, tn), lambda j, b: (0, off + j))

    outs, landed = _call(
        body, [u, u, dact, cw, cw, cb, cb], name="convffn_bwd", grid=(nf, B),
        in_specs=[seq(0), seq(nf), seq(0), par(3, 0), par(3, nf), par(1, 0), par(1, nf)],
        out_specs=[seq(0), seq(0), par(8, 0), par(8, 0)],
        out_shape=[_sds((T, F), BF16), _sds((T, F), BF16), _sds((8, F), F32), _sds((8, F), F32)],
        sem=("parallel", "arbitrary"), comm=comm)
    return outs, landed


def _tail(ff, x1, tgt, g):
    T, D = ff.shape
    tm = _tile(T, ROW_TILE, 16)

    def body(ff_ref, x1_ref, t_ref, g_ref, dy_ref, dff_ref, loss_ref, dg_ref):
        i = pl.program_id(0)
        f = ff_ref[...]
        gv = g_ref[...]
        r = _rms(f)
        n = f * r
        e = (x1_ref[...] + n * gv) - t_ref[...]
        dy = e * (1.0 / D)
        dy_ref[...] = dy
        dn = dy * gv
        dff_ref[...] = (r * (dn - n * jnp.mean(dn * n, axis=-1, keepdims=True))).astype(BF16)
        part = 0.5 * jnp.sum(jnp.mean(e * e, axis=-1, keepdims=True), axis=0, keepdims=True)
        _accumulate(loss_ref, jnp.broadcast_to(part, loss_ref.shape), i == 0)
        _accumulate(dg_ref, jnp.sum(dy * n, axis=0, keepdims=True), i == 0)

    row = pl.BlockSpec((tm, D), lambda i: (i, 0))
    vec = pl.BlockSpec((1, D), lambda i: (0, 0))
    return _call(body, [ff, x1, tgt, g], name="tail", grid=(T // tm,), in_specs=[row, row, row, vec],
                 out_specs=[row, row, pl.BlockSpec((8, LANES), lambda i: (0, 0)), vec],
                 out_shape=[_sds((T, D), F32), _sds((T, D), BF16), _sds((8, LANES), F32), _sds((1, D), F32)],
                 sem=("arbitrary",))[0]


def _mid_bwd(dy, dh2, x1, y1, g_ffn, g_pm, comm=None):
    T, D = dy.shape
    tm = _tile(T, ROW_TILE, 16)

    def body(dy_ref, dh_ref, x1_ref, y1_ref, gf_ref, gp_ref, dx1_ref, dy1_ref, dgf_ref, dgp_ref):
        i = pl.program_id(0)
        dh = dh_ref[...]
        d2, dgf = _rms_bwd(dh, x1_ref[...], gf_ref[...])
        dx1 = dy_ref[...] + d2
        dx1_ref[...] = dx1
        d1, dgp = _rms_bwd(dx1, y1_ref[...], gp_ref[...])
        dy1_ref[...] = d1.astype(BF16)
        _accumulate(dgf_ref, jnp.sum(dgf, axis=0, keepdims=True), i == 0)
        _accumulate(dgp_ref, jnp.sum(dgp, axis=0, keepdims=True), i == 0)

    row = pl.BlockSpec((tm, D), lambda i: (i, 0))
    vec = pl.BlockSpec((1, D), lambda i: (0, 0))
    return _call(body, [dy, dh2, x1, y1, g_ffn, g_pm], name="mid_bwd", grid=(T // tm,),
                 in_specs=[row, row, row, row, vec, vec], out_specs=[row, row, vec, vec],
                 out_shape=[_sds((T, D), F32), _sds((T, D), BF16), _sds((1, D), F32), _sds((1, D), F32)],
                 sem=("arbitrary",), comm=comm)


def _gate_bwd(dm, am, af, proj, bgate, lay, D, comm=None):
    T = dm.shape[0]
    tm = _tile(T, ROW_TILE, 16)
    tn = _tile(D, 512)

    def body(dm_ref, am_ref, af_ref, gm_ref, gf_ref, bm_ref, bf_ref,
             dam_ref, daf_ref, dgm_ref, dgf_ref, dbm_ref, dbf_ref):
        i = pl.program_id(1)
        d = dm_ref[...]
        sm = _sigmoid(gm_ref[...] + bm_ref[...])
        sf = _sigmoid(gf_ref[...] + bf_ref[...])
        dam_ref[...] = (d * sm).astype(BF16)
        daf_ref[...] = (d * sf).astype(BF16)
        dgm = d * am_ref[...] * (sm * (1.0 - sm))
        dgf = d * af_ref[...] * (sf * (1.0 - sf))
        dgm_ref[...] = dgm.astype(BF16)
        dgf_ref[...] = dgf.astype(BF16)
        _accumulate(dbm_ref, jnp.sum(dgm, axis=0, keepdims=True), i == 0)
        _accumulate(dbf_ref, jnp.sum(dgf, axis=0, keepdims=True), i == 0)

    og = lay["g"] // tn
    blk = pl.BlockSpec((tm, tn), lambda j, i: (i, j))
    vec = pl.BlockSpec((1, tn), lambda j, i: (0, j))
    return _call(
        body, [dm, am, af, proj, proj, bgate, bgate], name="gate_bwd", grid=(D // tn, T // tm),
        in_specs=[blk, blk, blk, pl.BlockSpec((tm, tn), lambda j, i: (i, og + j)),
                  pl.BlockSpec((tm, tn), lambda j, i: (i, og + D // tn + j)),
                  vec, pl.BlockSpec((1, tn), lambda j, i: (0, D // tn + j))],
        out_specs=[blk, blk, blk, blk, vec, vec],
        out_shape=[_sds((T, D), BF16)] * 4 + [_sds((1, D), F32)] * 2, sem=("parallel", "arbitrary"), comm=comm)


def _mla_bwd_prep(dq, dk, dv, cosT, sinT, comm=None):
    H, T, _ = dq.shape
    tm = _tile(T, HEAD_ROW_TILE, 16)

    def body(dq_ref, dk_ref, dv_ref, cos_ref, sin_ref, dqr_ref, dkv_ref, dkpe_ref):
        h = pl.program_id(1)
        cs, sn = cos_ref[...], sin_ref[...]
        valid = _lane(cs.shape) < ROPE

        def unrope(d):
            d = jnp.where(valid, d, 0.0)
            return d * cs - _rope_rot(d) * sn

        dqv = dq_ref[...]
        dqr_ref[:, :NOPE] = dqv[:, :NOPE].astype(BF16)
        dqr_ref[:, NOPE:] = unrope(dqv[:, NOPE:]).astype(BF16)
        dkv_ = dk_ref[...]
        dkv_ref[:, :NOPE] = dkv_[:, :NOPE].astype(BF16)
        dkv_ref[:, NOPE:] = dv_ref[...].astype(BF16)
        _accumulate(dkpe_ref, unrope(dkv_[:, NOPE:]), h == 0)

    head = pl.BlockSpec((None, tm, ATT_DK), lambda i, h: (h, i, 0))
    tok = pl.BlockSpec((tm, LANES), lambda i, h: (i, 0))
    return _call(
        body, [dq, dk, dv, cosT, sinT], name="mla_bwd_prep", grid=(T // tm, H),
        in_specs=[head, head, pl.BlockSpec((None, tm, VDIM), lambda i, h: (h, i, 0)), tok, tok],
        out_specs=[head, head, tok],
        out_shape=[_sds((H, T, ATT_DK), BF16), _sds((H, T, ATT_DK), BF16), _sds((T, LANES), F32)],
        sem=("parallel", "arbitrary"), comm=comm)


def _fox_bwd_prep(dq, dk, proj, bfor, lay, B, S, inv_scale):
    H, T, _ = dq.shape

    def body(dq_ref, dk_ref, fl_ref, bf_ref, dfl_ref, dbf_ref, dc_sc):
        b, h = pl.program_id(0), pl.program_id(1)
        lane = _lane(dc_sc.shape)
        col = jnp.sum(jnp.where(lane == 0, dq_ref[...], 0.0) - jnp.where(lane == 3, dk_ref[...], 0.0),
                      axis=1, keepdims=True)

        @pl.when(h == 0)
        def _():
            dc_sc[...] = jnp.zeros(dc_sc.shape, F32)

        dc_sc[...] = jnp.where(lane == h, col, dc_sc[...])

        @pl.when(h == H - 1)
        def _():
            dlogf = _cumsum_rows(dc_sc[...] * inv_scale, reverse=True)
            z = fl_ref[...] + bf_ref[...]
            dz = jnp.where(lane < H, dlogf * (1.0 / (1.0 + jnp.exp(z))), 0.0)
            dfl_ref[...] = dz
            _accumulate(dbf_ref, jnp.sum(dz, axis=0, keepdims=True), b == 0)

    aug = pl.BlockSpec((None, S, LANES), lambda b, h: (h, b, 1))
    seq = pl.BlockSpec((S, LANES), lambda b, h: (b, 0))
    vec = pl.BlockSpec((1, LANES), lambda b, h: (0, 0))
    return _call(
        body, [dq, dk, proj, bfor], name="fox_bwd_prep", grid=(B, H),
        in_specs=[aug, aug, pl.BlockSpec((S, LANES), lambda b, h: (b, lay["fl"] // LANES)), vec],
        out_specs=[seq, vec], out_shape=[_sds((T, LANES), F32), _sds((1, LANES), F32)],
        scratch_shapes=[pltpu.VMEM((S, LANES), F32)], sem=("arbitrary", "arbitrary"))[0]


def _heads_to_cols(dq, dk, dv):
    H, T, _ = dq.shape
    tm = _tile(T, HEAD_ROW_TILE, 16)

    def body(a_ref, b_ref, c_ref, ao_ref, bo_ref, co_ref):
        ao_ref[...] = a_ref[...].astype(BF16)
        bo_ref[...] = b_ref[...].astype(BF16)
        co_ref[...] = c_ref[...].astype(BF16)

    src = pl.BlockSpec((None, tm, FOX_DIM), lambda i, h: (h, i, 0))
    dst = pl.BlockSpec((tm, FOX_DIM), lambda i, h: (i, h))
    return _call(body, [dq, dk, dv], name="heads_to_cols", grid=(T // tm, H), in_specs=[src, src, src],
                 out_specs=[dst, dst, dst], out_shape=[_sds((T, H * FOX_DIM), BF16)] * 3,
                 sem=("parallel", "parallel"))[0]


def _lat_bwd(dqn, dkvn, proj, gq, gkv, lay):
    T = dqn.shape[0]
    tm = _tile(T, ROW_TILE, 16)

    def body(dq_ref, dkv_ref, q_ref, kv_ref, gq_ref, gkv_ref, dql_ref, dkl_ref, dgq_ref, dgkv_ref):
        i = pl.program_id(0)
        dql, dgq = _rms_bwd(dq_ref[...], q_ref[...], gq_ref[...])
        dkl, dgkv = _rms_bwd(dkv_ref[...], kv_ref[...], gkv_ref[...])
        dql_ref[...] = dql.astype(BF16)
        dkl_ref[...] = dkl.astype(BF16)
        _accumulate(dgq_ref, jnp.sum(dgq, axis=0, keepdims=True), i == 0)
        _accumulate(dgkv_ref, jnp.sum(dgkv, axis=0, keepdims=True), i == 0)

    def blk(width, off=0):
        return pl.BlockSpec((tm, width), lambda i: (i, off // width))

    def vec(width):
        return pl.BlockSpec((1, width), lambda i: (0, 0))

    return _call(
        body, [dqn, dkvn, proj, proj, gq, gkv], name="lat_bwd", grid=(T // tm,),
        in_specs=[blk(Q_LORA), blk(KV_LORA), blk(Q_LORA, lay["q"]), blk(KV_LORA, lay["kv"]), vec(Q_LORA), vec(KV_LORA)],
        out_specs=[blk(Q_LORA), blk(KV_LORA), vec(Q_LORA), vec(KV_LORA)],
        out_shape=[_sds((T, Q_LORA), BF16), _sds((T, KV_LORA), BF16), _sds((1, Q_LORA), F32), _sds((1, KV_LORA), F32)],
        sem=("arbitrary",))[0]


def _final_dx(dx1, dh, x, g, comm=None):
    T, D = x.shape
    tm = _tile(T, ROW_TILE, 16)

    def body(dx1_ref, dh_ref, x_ref, g_ref, dx_ref, dg_ref):
        i = pl.program_id(0)
        d, dg = _rms_bwd(dh_ref[...], x_ref[...], g_ref[...])
        dx_ref[...] = dx1_ref[...] + d
        _accumulate(dg_ref, jnp.sum(dg, axis=0, keepdims=True), i == 0)

    row = pl.BlockSpec((tm, D), lambda i: (i, 0))
    vec = pl.BlockSpec((1, D), lambda i: (0, 0))
    return _call(body, [dx1, dh, x, g], name="final_dx", grid=(T // tm,), in_specs=[row, row, row, vec],
                 out_specs=[row, vec], out_shape=[_sds((T, D), F32), _sds((1, D), F32)], sem=("arbitrary",), comm=comm)


def _chip_sum(pieces, paired, qc, name):
    G, R, C = pieces.shape
    tr = _tile(R, 256, 16)

    def body(qc_ref, g_ref, p_ref, keep_ref, send_ref):
        s = pl.program_id(1)
        tot = g_ref[...] + p_ref[...]

        @pl.when(s == 0)
        def _():
            keep_ref[...] = tot

        @pl.when(s > 0)
        def _():
            send_ref[...] = tot.astype(send_ref.dtype)

    grid_spec = pltpu.PrefetchScalarGridSpec(
        num_scalar_prefetch=1, grid=(R // tr, N_CHIP),
        in_specs=[pl.BlockSpec((None, tr, C), lambda i, s, qc: (2 * (qc[0] ^ s) + qc[1], i, 0)),
                  pl.BlockSpec((None, tr, C), lambda i, s, qc: (qc[0] ^ s, i, 0))],
        out_specs=[pl.BlockSpec((tr, C), lambda i, s, qc: (i, 0)),
                   pl.BlockSpec((None, tr, C), lambda i, s, qc: (jnp.maximum(s - 1, 0), i, 0))])
    send_dtype = BF16 if R >= 16 else pieces.dtype
    return pl.pallas_call(
        body, name=name, grid_spec=grid_spec,
        out_shape=[_sds((R, C), F32), _sds((3, R, C), send_dtype)],
        compiler_params=pltpu.CompilerParams(dimension_semantics=("arbitrary", "arbitrary"),
                                             vmem_limit_bytes=VMEM_LIMIT_BYTES),
    )(qc, pieces, paired)


def _adamw_math(w, g, m, v):
    m = ADAM_B1 * m + (1.0 - ADAM_B1) * g
    v = ADAM_B2 * v + (1.0 - ADAM_B2) * (g * g)
    m_hat = m / (1.0 - ADAM_B1 ** ADAM_STEP)
    v_hat = v / (1.0 - ADAM_B2 ** ADAM_STEP)
    delta = -ADAM_LR * (m_hat / (jnp.sqrt(v_hat) + ADAM_EPS) + ADAM_WD * w)
    return delta, m, v


def _sum_adamw(keep, pieces, w, m, v, name):
    R, C = w.shape
    P = pieces.shape[0]
    tr = _tile(R, 256, 16)

    def body(*refs):
        if keep is None:
            p_ref, w_ref, m_ref, v_ref, g_ref, d_ref, mo_ref, vo_ref = refs
            g = p_ref[0].astype(F32)
            rest = range(1, P)
        else:
            k_ref, p_ref, w_ref, m_ref, v_ref, g_ref, d_ref, mo_ref, vo_ref = refs
            g = k_ref[...]
            rest = range(P)
        for q in rest:
            g = g + p_ref[q].astype(F32)
        g_ref[...] = g
        d_ref[...], mo_ref[...], vo_ref[...] = _adamw_math(w_ref[...], g, m_ref[...], v_ref[...])

    blk = pl.BlockSpec((tr, C), lambda i: (i, 0))
    pblk = pl.BlockSpec((P, tr, C), lambda i: (0, i, 0))
    args = [pieces, w, m, v] if keep is None else [keep, pieces, w, m, v]
    specs = [pblk, blk, blk, blk] if keep is None else [blk, pblk, blk, blk, blk]
    return _call(body, args, name=name, grid=(R // tr,), in_specs=specs, out_specs=[blk] * 4,
                 out_shape=[_sds((R, C), F32)] * 4, sem=("parallel",))[0]


def _layout(D):
    lay = {"q": 0, "kv": Q_LORA, "kpe": Q_LORA + KV_LORA}
    lay["fq"] = lay["kpe"] + LANES
    lay["fk"] = lay["fq"] + HEADS * FOX_DIM
    lay["fv"] = lay["fk"] + HEADS * FOX_DIM
    lay["fl"] = lay["fv"] + HEADS * FOX_DIM
    lay["g"] = lay["fl"] + LANES
    lay["end"] = lay["g"] + 2 * D
    return lay


def kernel(x, positions, pre_mix_norm, w_in, q_a_norm, w_uq, kv_a_norm, w_ukv, b_forget, b_gate, w_branch_mla, w_branch_fox, w_out, post_mix_norm, pre_ffn_norm, w_up, conv_w, conv_b, w_down, post_ffn_norm, loss_target, m_pre_mix_norm, m_w_in, m_q_a_norm, m_w_uq, m_kv_a_norm, m_w_ukv, m_b_forget, m_b_gate, m_w_branch_mla, m_w_branch_fox, m_w_out, m_post_mix_norm, m_pre_ffn_norm, m_w_up, m_conv_w, m_conv_b, m_w_down, m_post_ffn_norm, v_pre_mix_norm, v_w_in, v_q_a_norm, v_w_uq, v_kv_a_norm, v_w_ukv, v_b_forget, v_b_gate, v_w_branch_mla, v_w_branch_fox, v_w_out, v_post_mix_norm, v_pre_ffn_norm, v_w_up, v_conv_w, v_conv_b, v_w_down, v_post_ffn_norm):
    B, S, D = x.shape
    T = B * S
    F = conv_b.shape[0] // 2
    lay = _layout(D)
    n_in = w_in.shape[1]
    d_in = N_DEV * n_in
    seg_a = Q_LORA + KV_LORA + ROPE
    seg_b = 3 * HEADS * FOX_DIM + HEADS
    mla_scale = (NOPE + ROPE) ** -0.5
    fox_scale = FOX_DIM ** -0.5
    ax, ay, ac = (lax.axis_index(a) for a in MESH_AXES)
    qc = jnp.stack([2 * ax + ay, ac]).astype(jnp.int32)

    def row(vec, width=None):
        vec = vec.reshape(1, -1)
        if width is not None and vec.shape[1] < width:
            vec = jnp.pad(vec, ((0, 0), (0, width - vec.shape[1])))
        return vec

    x2 = x.reshape(T, D)
    win_s = _cast_bf16(w_in, "cast_w_in")
    h, (win_g,) = _prenorm(x2, row(pre_mix_norm), comm=_Comm([_GatherPlan([win_s], mid_frac=0.3)]))
    small_s = [_cast_bf16(w, "cast_" + n) for w, n in
               [(w_uq, "w_uq"), (w_ukv, "w_ukv"), (w_branch_mla, "w_branch_mla"), (w_branch_fox, "w_branch_fox"), (w_out, "w_out")]]
    wup_s = _cast_bf16(w_up, "cast_w_up")
    wdown_s = _cast_bf16(w_down, "cast_w_down")

    def shard_cols(lo, hi):
        out = []
        for g in range(lo // n_in, (hi - 1) // n_in + 1):
            out.append(win_g[g][:, max(lo, g * n_in) - g * n_in:min(hi, (g + 1) * n_in) - g * n_in])
        return out

    w_perm = jnp.concatenate(
        shard_cols(0, seg_a) + [jnp.zeros((D, LANES - ROPE), BF16)] + shard_cols(seg_a, seg_a + seg_b)
        + [jnp.zeros((D, LANES - HEADS), BF16)] + shard_cols(seg_a + seg_b, d_in), axis=1)

    tgt = loss_target.reshape(T, D)
    pos = positions.reshape(T, 1)
    inv_freq = 1.0 / (ROPE_THETA ** (jnp.arange(0, ROPE, 2, dtype=F32) / ROPE))
    invf = row(jnp.concatenate([inv_freq, inv_freq]), LANES)
    g_pre, g_q, g_kv = row(pre_mix_norm), row(q_a_norm), row(kv_a_norm)
    g_pm, g_ffn, g_pf = row(post_mix_norm), row(pre_ffn_norm), row(post_ffn_norm)
    bfor = row(b_forget, LANES)
    bgate = row(b_gate)
    cb_full = row(conv_b)

    def own_plan(blocks):
        return _Comm([_GatherOwnPlan(blocks)])

    def pass_plan(gathered):
        return _Comm([_GatherPassPlan(gathered)])

    def pair_plan(gs):
        return _Comm([_PairScatterPlan(gs)])

    def chip_plan(gs):
        return _Comm([_ChipScatterPlan(gs)])

    proj, small_g = _matmul(h, w_perm, mode="nn", name="mm_proj", comm=own_plan(small_s + [conv_w]))
    (qn, kvn, kper, logf, cosT, sinT), (wuq_g, wukv_g, wbm_g, wbf_g, wout_g, cw_g) = _split_prep(
        proj, pos, invf, g_q, g_kv, bfor, lay, comm=pass_plan(small_g))
    wuq_pad = jnp.pad(wuq_g, ((0, 0), (0, 0), (0, ATT_DK - NOPE - ROPE)))
    wbm = jnp.transpose(wbm_g, (1, 0, 2)).reshape(HEADS * VDIM, D)
    wbf = jnp.transpose(wbf_g, (1, 0, 2)).reshape(HEADS * FOX_DIM, D)
    wout = wout_g.reshape(D, D)
    cw_full = jnp.transpose(cw_g, (1, 0, 2)).reshape(3, 2 * F)

    qraw = _matmul(qn, wuq_pad, mode="nn", name="mm_q", out_blocks=ATT_DK)
    kvraw = _matmul(kvn, wukv_g, mode="nn", name="mm_kv", out_blocks=NOPE + VDIM)
    q_mla, k_mla, v_mla = _mla_prep(qraw, kvraw, kper, cosT, sinT)
    cs = _fox_cumsum(logf, B, S, 1.0 / fox_scale)
    q_fox, k_fox, v_fox = _fox_prep(proj, cs, lay)
    ((o_mla, lse_mla), (o_fox, lse_fox)), wup_half = _attn_fwd(
        [(q_mla, k_mla, v_mla, MLA_UNIT, mla_scale), (q_fox, k_fox, v_fox, 1, fox_scale)], B=B, S=S,
        name="attn_fwd", comm=own_plan([wup_s]))
    a_m = _matmul(o_mla, wbm, mode="nn", name="mm_branch_mla")
    a_f = _matmul(o_fox, wbf, mode="nn", name="mm_branch_fox")
    (merged,), (wup_g,) = _gate_merge(a_m, a_f, proj, bgate, lay, D, comm=pass_plan(wup_half))
    n_up = wup_g.shape[2]
    y1 = _matmul(merged, wout, mode="nn", name="mm_out")
    x1, h2 = _mid(x2, y1, g_pm, g_ffn)
    u, wdown_half = _matmul(h2, wup_g, mode="nn", name="mm_up", tn=n_up, comm=own_plan([wdown_s]))
    (act,), (wdown_g,) = _convffn_fwd(u, cw_full, cb_full, B, S, F, comm=pass_plan(wdown_half))
    wdown = wdown_g.reshape(F, D)
    ff = _matmul(act, wdown, mode="nn", name="mm_down", tk=F // 2)
    dy, dff, loss_part, dg_pf = _tail(ff, x1, tgt, g_pf)

    dact = _matmul(dff, wdown, mode="nt", name="mm_dact")
    dw_down = _matmul(act, dff, mode="tn", name="mm_dw_down", tm=512).reshape(N_DEV, F // N_DEV, D)
    (du_g, du_v, dcp_g, dcp_v), (pa_down,) = _convffn_bwd(u, dact, cw_full, cb_full, B, S, F, comm=pair_plan([dw_down]))
    keep_down, sb_down = _chip_sum(dw_down, pa_down, qc, "chipsum_w_down")
    du = _concat_cols([du_g, du_v], "concat_du")
    dh2, (rb_down,) = _matmul(du, wup_g, mode="nt", name="mm_dh2", comm=chip_plan([sb_down]))
    dw_up = _matmul(h2, du, mode="tn", name="mm_dw_up", out_blocks=n_up, tm=512, tn=n_up)
    (dx1, dy1, dg_ffn, dg_pm), _ = _mid_bwd(dy, dh2, x1, y1, g_ffn, g_pm)
    dmerged = _matmul(dy1, wout, mode="nt", name="mm_dmerged")
    dw_out = _matmul(merged, dy1, mode="tn", name="mm_dw_out").reshape(N_DEV, D // N_DEV, D)
    (da_m, da_f, dgl_m, dgl_f, dbg_m, dbg_f), (pa_up,) = _gate_bwd(dmerged, a_m, a_f, proj, bgate, lay, D,
                                                                   comm=pair_plan([dw_up]))
    keep_up, sb_up = _chip_sum(dw_up, pa_up, qc, "chipsum_w_up")
    dw_bm = _matmul(o_mla, da_m, mode="tn", name="mm_dw_branch_mla", out_blocks=D // N_DEV)
    dw_bf = _matmul(o_fox, da_f, mode="tn", name="mm_dw_branch_fox", out_blocks=D // N_DEV)
    mix = [dw_out, dw_bm, dw_bf]
    do_mla, pa_mix = _matmul(da_m, wbm, mode="nt", name="mm_do_mla", out_dtype=BF16, comm=pair_plan(mix))
    do_fox = _matmul(da_f, wbf, mode="nt", name="mm_do_fox", out_dtype=BF16)
    mix_sums = [_chip_sum(g, p, qc, "chipsum_" + n) for g, p, n in zip(mix, pa_mix, ["w_out", "w_branch_mla", "w_branch_fox"])]
    ((dq_m, dk_m, dv_m), (dq_f, dk_f, dv_f)), (rb_up,) = _attn_bwd(
        [(q_mla, k_mla, v_mla, o_mla, do_mla, lse_mla, MLA_UNIT, mla_scale),
         (q_fox, k_fox, v_fox, o_fox, do_fox, lse_fox, 1, fox_scale)], B=B, S=S, name="attn_bwd",
        comm=chip_plan([sb_up]))
    (dqraw, dkvraw, dkpe), rb_mix = _mla_bwd_prep(dq_m, dk_m, dv_m, cosT, sinT, comm=chip_plan([s[1] for s in mix_sums]))
    dqn = _matmul(dqraw, wuq_pad, mode="nt", name="mm_dqn")
    dw_uq = _matmul(qn, dqraw, mode="tn", name="mm_dw_uq", out_blocks=ATT_DK)[:, :, :NOPE + ROPE]
    dkvn = _matmul(dkvraw, wukv_g, mode="nt", name="mm_dkvn")
    dw_ukv = _matmul(kvn, dkvraw, mode="tn", name="mm_dw_ukv", out_blocks=NOPE + VDIM)
    dqlat, dkvlat, dg_q, dg_kv = _lat_bwd(dqn, dkvn, proj, g_q, g_kv, lay)
    dfl, dbfor = _fox_bwd_prep(dq_f, dk_f, proj, bfor, lay, B, S, 1.0 / fox_scale)
    dfq, dfk, dfv = _heads_to_cols(dq_f, dk_f, dv_f)
    dproj = _concat_cols([dqlat, dkvlat, dkpe, dfq, dfk, dfv, dfl, dgl_m, dgl_f], "concat_dproj")
    dw_perm = _matmul(h, dproj, mode="tn", name="mm_dw_in")
    segs = [(0, seg_a, 0), (seg_a, seg_a + seg_b, lay["fq"] - seg_a), (seg_a + seg_b, d_in, lay["g"] - seg_a - seg_b)]

    def piece(g):
        lo, hi = g * n_in, (g + 1) * n_in
        parts = [dw_perm[:, max(lo, s0) + sh:min(hi, s1) + sh] for s0, s1, sh in segs if max(lo, s0) < min(hi, s1)]
        return parts[0] if len(parts) == 1 else jnp.concatenate(parts, axis=1)

    dw_in = jnp.stack([piece(g) for g in range(N_DEV)])
    dcw = jnp.transpose(jnp.concatenate([dcp_g[0:3], dcp_v[0:3]], axis=1).reshape(3, N_DEV, (2 * F) // N_DEV), (1, 0, 2))
    late = [dw_in, dw_uq, dw_ukv, dcw]
    pa_late = _exchange_alone(pair_plan(late), "pair_late")
    late_sums = [_chip_sum(g, p, qc, "chipsum_" + n) for g, p, n in zip(late, pa_late, ["w_in", "w_uq", "w_ukv", "conv_w"])]
    dh, rb_late = _matmul(dproj, w_perm, mode="nt", name="mm_dh", tk=2048, comm=chip_plan([s[1] for s in late_sums]))
    (grad_x, dg_pre), _ = _final_dx(dx1, dh, x2, g_pre)

    big_out = {}

    def finish(n, keep, pieces, w, m, v):
        big_out[n] = _sum_adamw(keep, pieces, w, m, v, "adamw_" + n)

    finish("w_down", keep_down, rb_down, w_down, m_w_down, v_w_down)
    finish("w_up", keep_up, rb_up, w_up, m_w_up, v_w_up)
    finish("w_out", mix_sums[0][0], rb_mix[0], w_out, m_w_out, v_w_out)
    finish("w_branch_mla", mix_sums[1][0], rb_mix[1], w_branch_mla, m_w_branch_mla, v_w_branch_mla)
    finish("w_branch_fox", mix_sums[2][0], rb_mix[2], w_branch_fox, m_w_branch_fox, v_w_branch_fox)
    finish("w_in", late_sums[0][0], rb_late[0], w_in, m_w_in, v_w_in)
    finish("w_uq", late_sums[1][0], rb_late[1], w_uq, m_w_uq, v_w_uq)
    finish("w_ukv", late_sums[2][0], rb_late[2], w_ukv, m_w_ukv, v_w_ukv)
    finish("conv_w", late_sums[3][0], rb_late[3], conv_w, m_conv_w, v_conv_w)

    widths = [D, Q_LORA, KV_LORA, LANES, 2 * D, D, D, 2 * F, D]
    small_names = ["pre_mix_norm", "q_a_norm", "kv_a_norm", "b_forget", "b_gate", "post_mix_norm", "pre_ffn_norm",
                   "conv_b", "post_ffn_norm"]
    true_w = [D, Q_LORA, KV_LORA, HEADS, 2 * D, D, D, 2 * F, D]
    dcb = jnp.concatenate([dcp_g[3:4], dcp_v[3:4]], axis=1)
    part = jnp.concatenate([dg_pre, dg_q, dg_kv, dbfor, dbg_m, dbg_f, dg_pm, dg_ffn, dcb, dg_pf], axis=1)

    def pack(vals):
        return jnp.concatenate([row(a, wd) for a, wd in zip(vals, widths)], axis=1)

    sw = pack([pre_mix_norm, q_a_norm, kv_a_norm, b_forget, b_gate, post_mix_norm, pre_ffn_norm, conv_b, post_ffn_norm])
    sm = pack([m_pre_mix_norm, m_q_a_norm, m_kv_a_norm, m_b_forget, m_b_gate, m_post_mix_norm, m_pre_ffn_norm,
               m_conv_b, m_post_ffn_norm])
    sv = pack([v_pre_mix_norm, v_q_a_norm, v_kv_a_norm, v_b_forget, v_b_gate, v_post_mix_norm, v_pre_ffn_norm,
               v_conv_b, v_post_ffn_norm])
    (parts_all,) = _exchange_alone(_Comm([_DirectGatherPlan([part])]), "gather_small")
    sg, sd, smo, svo = _sum_adamw(None, parts_all, sw, sm, sv, "adamw_small")
    small_out = {}
    off = 0
    for n, wd, tw in zip(small_names, widths, true_w):
        small_out[n] = tuple(a[0, off:off + tw] for a in (sg, sd, smo, svo))
        off += wd

    loss = lax.psum(loss_part[0, 0], MESH_AXES)
    order = ["pre_mix_norm", "w_in", "q_a_norm", "w_uq", "kv_a_norm", "w_ukv", "b_forget", "b_gate", "w_branch_mla",
             "w_branch_fox", "w_out", "post_mix_norm", "pre_ffn_norm", "w_up", "conv_w", "conv_b", "w_down",
             "post_ffn_norm"]
    res = {**big_out, **small_out}
    outs = [loss, grad_x.reshape(B, S, D)]
    for kind in range(4):
        outs += [res[n][kind] for n in order]
    return tuple(outs)
```

```python
import math

import jax
import jax.numpy as jnp
from jax import lax
from jax.experimental import pallas as pl
from jax.experimental.pallas import tpu as pltpu

F32 = jnp.float32
BF16 = jnp.bfloat16

N_DEV = 8
N_CHIP = 4
HEADS = 8
NOPE = 128
ROPE = 64
HALF_ROPE = ROPE // 2
VDIM = 128
Q_LORA = 512
KV_LORA = 256
FOX_DIM = 128
ATT_DK = 256
MLA_UNIT = 64
ROPE_THETA = 10000.0
EPS = 1e-6
NEG_INF = -1e30
LANES = 128
LOG2_E = 1.4426950408889634

ADAM_LR = 0.001
ADAM_B1 = 0.9
ADAM_B2 = 0.999
ADAM_EPS = 1e-08
ADAM_WD = 0.01
ADAM_STEP = 10

VMEM_LIMIT_BYTES = 56 * 1024 * 1024
ROW_TILE = 256
HEAD_ROW_TILE = 1024
ATT_TILE = 1024
ATT_SUB = 256
ATT_AHEAD = 3
MM_TILE = 1024

MESH_AXES = ("x", "y", "c")
ANY = pl.BlockSpec(memory_space=pl.ANY)


def _tile(n, pref, align=LANES):
    if n <= pref:
        return n
    t = (pref // align) * align
    while t >= align:
        if n % t == 0:
            return t
        t -= align
    return n


def _sds(shape, dtype):
    return jax.ShapeDtypeStruct(shape, dtype)


def _coords():
    x, y, c = (lax.axis_index(ax) for ax in MESH_AXES)
    return x, y, c


def _chip_rel(x, y, r):
    return (1 - x if r & 2 else x), (1 - y if r & 1 else y)


def _rcopy(src, dst, sems, w, k, dev):
    return pltpu.make_async_remote_copy(src_ref=src, dst_ref=dst, send_sem=sems[0].at[w, k], recv_sem=sems[1].at[w, k],
                                        device_id=dev, device_id_type=pl.DeviceIdType.MESH)


class _GatherPlan:
    def __init__(self, blocks, mid_frac=0.5):
        self.ins = list(blocks)
        self.out_shapes = [_sds((N_DEV,) + b.shape, b.dtype) for b in blocks]
        n = len(blocks)
        self.scratch = [pltpu.SemaphoreType.DMA((n, 7)), pltpu.SemaphoreType.DMA((n, 7)), pltpu.SemaphoreType.DMA((n,))]
        self.mid_frac = mid_frac

    def first(self, ins, outs, sems):
        x, y, c = _coords()
        me = 4 * x + 2 * y + c
        for w in range(len(ins)):
            pltpu.make_async_copy(ins[w], outs[w].at[me], sems[2].at[w]).start()
            _rcopy(ins[w], outs[w].at[me], sems, w, 0, (x, y, 1 - c)).start()
            for r in (1, 2, 3):
                px, py = _chip_rel(x, y, r)
                _rcopy(ins[w], outs[w].at[me], sems, w, r, (px, py, c)).start()

    def mid(self, ins, outs, sems):
        x, y, c = _coords()
        for w in range(len(ins)):
            for r in (1, 2, 3):
                px, py = _chip_rel(x, y, r)
                blk = outs[w].at[4 * px + 2 * py + c]
                _rcopy(ins[w], blk, sems, w, r, (px, py, c)).wait_recv()
                _rcopy(blk, blk, sems, w, 3 + r, (x, y, 1 - c)).start()

    def last(self, ins, outs, sems):
        x, y, c = _coords()
        me = 4 * x + 2 * y + c
        sib = (x, y, 1 - c)
        for w in range(len(ins)):
            _rcopy(ins[w], outs[w].at[4 * x + 2 * y + 1 - c], sems, w, 0, sib).wait_recv()
            for r in (1, 2, 3):
                px, py = _chip_rel(x, y, r)
                blk = outs[w].at[4 * px + 2 * py + 1 - c]
                _rcopy(blk, blk, sems, w, 3 + r, sib).wait_recv()
            for k in range(7):
                _rcopy(ins[w], outs[w].at[me], sems, w, k, sib).wait_send()
            pltpu.make_async_copy(ins[w], outs[w].at[me], sems[2].at[w]).wait()


class _GatherOwnPlan:
    mid = None

    def __init__(self, blocks):
        self.ins = list(blocks)
        self.out_shapes = [_sds((N_DEV,) + b.shape, b.dtype) for b in blocks]
        n = len(blocks)
        self.scratch = [pltpu.SemaphoreType.DMA((n, 4)), pltpu.SemaphoreType.DMA((n, 4)), pltpu.SemaphoreType.DMA((n,))]

    def first(self, ins, outs, sems):
        x, y, c = _coords()
        me = 4 * x + 2 * y + c
        for w in range(len(ins)):
            pltpu.make_async_copy(ins[w], outs[w].at[me], sems[2].at[w]).start()
            _rcopy(ins[w], outs[w].at[me], sems, w, 0, (x, y, 1 - c)).start()
            for r in (1, 2, 3):
                px, py = _chip_rel(x, y, r)
                _rcopy(ins[w], outs[w].at[me], sems, w, r, (px, py, c)).start()

    def last(self, ins, outs, sems):
        x, y, c = _coords()
        me = 4 * x + 2 * y + c
        for w in range(len(ins)):
            cp = _rcopy(ins[w], outs[w].at[4 * x + 2 * y + 1 - c], sems, w, 0, (x, y, 1 - c))
            cp.wait_recv()
            cp.wait_send()
            for r in (1, 2, 3):
                px, py = _chip_rel(x, y, r)
                cp = _rcopy(ins[w], outs[w].at[4 * px + 2 * py + c], sems, w, r, (px, py, c))
                cp.wait_recv()
                cp.wait_send()
            pltpu.make_async_copy(ins[w], outs[w].at[me], sems[2].at[w]).wait()


class _GatherPassPlan:
    mid = None

    def __init__(self, gathered):
        self.ins = list(gathered)
        self.out_shapes = [_sds(g.shape, g.dtype) for g in gathered]
        self.aliases = [(i, i) for i in range(len(gathered))]
        n = len(gathered)
        self.scratch = [pltpu.SemaphoreType.DMA((n, 3)), pltpu.SemaphoreType.DMA((n, 3))]

    def first(self, ins, outs, sems):
        x, y, c = _coords()
        for w in range(len(ins)):
            for r in (1, 2, 3):
                px, py = _chip_rel(x, y, r)
                blk = 4 * px + 2 * py + c
                _rcopy(ins[w].at[blk], outs[w].at[blk], sems, w, r - 1, (x, y, 1 - c)).start()

    def last(self, ins, outs, sems):
        x, y, c = _coords()
        for w in range(len(ins)):
            for r in (1, 2, 3):
                px, py = _chip_rel(x, y, r)
                blk = 4 * px + 2 * py + 1 - c
                cp = _rcopy(ins[w].at[blk], outs[w].at[blk], sems, w, r - 1, (x, y, 1 - c))
                cp.wait_recv()
                cp.wait_send()


class _DirectGatherPlan:
    mid = None

    def __init__(self, blocks):
        self.ins = list(blocks)
        self.out_shapes = [_sds((N_DEV,) + b.shape, b.dtype) for b in blocks]
        n = len(blocks)
        self.scratch = [pltpu.SemaphoreType.DMA((n, 7)), pltpu.SemaphoreType.DMA((n, 7)), pltpu.SemaphoreType.DMA((n,))]

    @staticmethod
    def _peer(x, y, c, r):
        return (1 - x if r & 4 else x), (1 - y if r & 2 else y), (1 - c if r & 1 else c)

    def first(self, ins, outs, sems):
        x, y, c = _coords()
        me = 4 * x + 2 * y + c
        for w in range(len(ins)):
            pltpu.make_async_copy(ins[w], outs[w].at[me], sems[2].at[w]).start()
            for r in range(1, N_DEV):
                _rcopy(ins[w], outs[w].at[me], sems, w, r - 1, self._peer(x, y, c, r)).start()

    def last(self, ins, outs, sems):
        x, y, c = _coords()
        me = 4 * x + 2 * y + c
        for w in range(len(ins)):
            for r in range(1, N_DEV):
                px, py, pc = self._peer(x, y, c, r)
                cp = _rcopy(ins[w], outs[w].at[4 * px + 2 * py + pc], sems, w, r - 1, (px, py, pc))
                cp.wait_recv()
                cp.wait_send()
            pltpu.make_async_copy(ins[w], outs[w].at[me], sems[2].at[w]).wait()


class _PairScatterPlan:
    mid = None

    def __init__(self, pieces):
        self.ins = list(pieces)
        self.out_shapes = [_sds((N_CHIP,) + p.shape[1:], p.dtype) for p in pieces]
        n = len(pieces)
        self.scratch = [pltpu.SemaphoreType.DMA((n, N_CHIP)), pltpu.SemaphoreType.DMA((n, N_CHIP))]

    def _copies(self, ins, outs, sems):
        x, y, c = _coords()
        return [_rcopy(ins[w].at[2 * q + 1 - c], outs[w].at[q], sems, w, q, (x, y, 1 - c))
                for w in range(len(ins)) for q in range(N_CHIP)]

    def first(self, ins, outs, sems):
        for cp in self._copies(ins, outs, sems):
            cp.start()

    def last(self, ins, outs, sems):
        for cp in self._copies(ins, outs, sems):
            cp.wait_recv()
            cp.wait_send()


class _ChipScatterPlan:
    mid = None

    def __init__(self, sums):
        self.ins = list(sums)
        self.out_shapes = [_sds(s.shape, s.dtype) for s in sums]
        n = len(sums)
        self.scratch = [pltpu.SemaphoreType.DMA((n, 3)), pltpu.SemaphoreType.DMA((n, 3))]

    def _copies(self, ins, outs, sems):
        x, y, c = _coords()
        cps = []
        for w in range(len(ins)):
            for r in (1, 2, 3):
                px, py = _chip_rel(x, y, r)
                cps.append(_rcopy(ins[w].at[r - 1], outs[w].at[r - 1], sems, w, r - 1, (px, py, c)))
        return cps

    def first(self, ins, outs, sems):
        for cp in self._copies(ins, outs, sems):
            cp.start()

    def last(self, ins, outs, sems):
        for cp in self._copies(ins, outs, sems):
            cp.wait_recv()
            cp.wait_send()


class _Comm:
    def __init__(self, plans):
        self.plans = list(plans)
        self.ins = [a for p in self.plans for a in p.ins]
        self.out_shapes = [s for p in self.plans for s in p.out_shapes]
        self.scratch = [s for p in self.plans for s in p.scratch]
        self.aliases = []
        i = o = 0
        for p in self.plans:
            self.aliases += [(i + a, o + b) for a, b in getattr(p, "aliases", [])]
            i, o = i + len(p.ins), o + len(p.out_shapes)

    def _parts(self, ins, outs, sems):
        i = o = s = 0
        for p in self.plans:
            yield p, ins[i:i + len(p.ins)], outs[o:o + len(p.out_shapes)], sems[s:s + len(p.scratch)]
            i, o, s = i + len(p.ins), o + len(p.out_shapes), s + len(p.scratch)

    def begin(self, step, nsteps, ins, outs, sems):
        @pl.when(step == 0)
        def _():
            for p, pi, po, ps in self._parts(ins, outs, sems):
                p.first(pi, po, ps)

        for p, pi, po, ps in self._parts(ins, outs, sems):
            if p.mid is not None:
                @pl.when(step == min(nsteps - 1, int(p.mid_frac * nsteps)))
                def _(p=p, pi=pi, po=po, ps=ps):
                    p.mid(pi, po, ps)

    def end(self, step, nsteps, ins, outs, sems):
        @pl.when(step == nsteps - 1)
        def _():
            for p, pi, po, ps in self._parts(ins, outs, sems):
                p.last(pi, po, ps)


def _call(body, args, *, name, grid, in_specs, out_specs, out_shape, scratch_shapes=(), sem=None, comm=None):
    in_specs, out_specs, out_shape, scratch_shapes = list(in_specs), list(out_specs), list(out_shape), list(scratch_shapes)
    if comm is None:
        res = pl.pallas_call(
            body, name=name, grid=grid, in_specs=in_specs, out_specs=out_specs, out_shape=out_shape,
            scratch_shapes=scratch_shapes,
            compiler_params=pltpu.CompilerParams(dimension_semantics=sem, vmem_limit_bytes=VMEM_LIMIT_BYTES),
        )(*args)
        return list(res), []
    n_in, n_out, n_sc = len(in_specs), len(out_specs), len(scratch_shapes)
    n_ci, n_co = len(comm.ins), len(comm.out_shapes)
    nsteps = math.prod(grid)

    def hosted(*refs):
        ins, cins = refs[:n_in], refs[n_in:n_in + n_ci]
        o0 = n_in + n_ci
        outs, couts = refs[o0:o0 + n_out], refs[o0 + n_out:o0 + n_out + n_co]
        s0 = o0 + n_out + n_co
        scr, csems = refs[s0:s0 + n_sc], refs[s0 + n_sc:]
        step = jnp.int32(0)
        for d in range(len(grid)):
            step = step * grid[d] + pl.program_id(d)
        comm.begin(step, nsteps, cins, couts, csems)
        body(*ins, *outs, *scr)
        comm.end(step, nsteps, cins, couts, csems)

    res = pl.pallas_call(
        hosted, name=name, grid=grid, in_specs=in_specs + [ANY] * n_ci, out_specs=out_specs + [ANY] * n_co,
        out_shape=out_shape + comm.out_shapes, scratch_shapes=scratch_shapes + comm.scratch,
        input_output_aliases={n_in + a: n_out + b for a, b in comm.aliases},
        compiler_params=pltpu.CompilerParams(dimension_semantics=("arbitrary",) * len(grid),
                                             vmem_limit_bytes=VMEM_LIMIT_BYTES, has_side_effects=True),
    )(*args, *comm.ins)
    return list(res[:n_out]), list(res[n_out:])


def _exchange_alone(comm, name):
    def body():
        pass

    return _call(body, [], name=name, grid=(), in_specs=[], out_specs=[], out_shape=[], comm=comm)[1]


def _matmul(a, b, *, mode, name, out_dtype=F32, out_blocks=None, tm=None, tn=None, tk=None, comm=None):
    tm = MM_TILE if tm is None else tm
    tn = MM_TILE if tn is None else tn
    a_blk = a.ndim == 3
    b_blk = b.ndim == 3
    if mode == "nn":
        M, K = a.shape
        N = b.shape[0] * b.shape[2] if b_blk else b.shape[1]
        dims = (((1,), (0,)), ((), ()))
    elif mode == "nt":
        M = a.shape[1] if a_blk else a.shape[0]
        K = a.shape[0] * a.shape[2] if a_blk else a.shape[1]
        N = b.shape[1] if b_blk else b.shape[0]
        dims = (((1,), (1,)), ((), ()))
    else:
        K, M = a.shape
        N = b.shape[0] * b.shape[2] if b_blk else b.shape[1]
        dims = (((0,), (0,)), ((), ()))

    tm = _tile(M, tm)
    tn = _tile(N, tn)
    if mode == "nt" and (a_blk or b_blk):
        tk = a.shape[2] if a_blk else b.shape[2]
    else:
        tk = _tile(K, K if tk is None else tk)
    if mode != "nt" and b_blk:
        tn = _tile(b.shape[2], tn)
    if out_blocks is not None:
        tn = _tile(out_blocks, tn)
    nk = K // tk
    grid = (M // tm, N // tn, nk)

    if mode == "nn":
        a_spec = pl.BlockSpec((tm, tk), lambda i, j, k: (i, k))
        if b_blk:
            rb = b.shape[2] // tn
            b_spec = pl.BlockSpec((None, tk, tn), lambda i, j, k: (j // rb, k, j % rb))
        else:
            b_spec = pl.BlockSpec((tk, tn), lambda i, j, k: (k, j))
    elif mode == "nt":
        if a_blk:
            a_spec = pl.BlockSpec((None, tm, tk), lambda i, j, k: (k, i, 0))
        else:
            a_spec = pl.BlockSpec((tm, tk), lambda i, j, k: (i, k))
        if b_blk:
            b_spec = pl.BlockSpec((None, tn, tk), lambda i, j, k: (k, j, 0))
        else:
            b_spec = pl.BlockSpec((tn, tk), lambda i, j, k: (j, k))
    else:
        a_spec = pl.BlockSpec((tk, tm), lambda i, j, k: (k, i))
        if b_blk:
            rb = b.shape[2] // tn
            b_spec = pl.BlockSpec((None, tk, tn), lambda i, j, k: (j // rb, k, j % rb))
        else:
            b_spec = pl.BlockSpec((tk, tn), lambda i, j, k: (k, j))

    if out_blocks is None:
        o_spec = pl.BlockSpec((tm, tn), lambda i, j, k: (i, j))
        o_shape = _sds((M, N), out_dtype)
    else:
        ro = out_blocks // tn
        o_spec = pl.BlockSpec((None, tm, tn), lambda i, j, k: (j // ro, i, j % ro))
        o_shape = _sds((N // out_blocks, M, out_blocks), out_dtype)

    direct = nk == 1 or out_dtype == F32

    def body(a_ref, b_ref, o_ref, *scratch):
        if nk == 1:
            o_ref[...] = lax.dot_general(a_ref[...], b_ref[...], dims, preferred_element_type=F32).astype(o_ref.dtype)
            return
        acc_ref = o_ref if direct else scratch[0]
        k = pl.program_id(2)

        @pl.when(k == 0)
        def _():
            acc_ref[...] = jnp.zeros(acc_ref.shape, F32)

        acc_ref[...] += lax.dot_general(a_ref[...], b_ref[...], dims, preferred_element_type=F32)
        if not direct:
            @pl.when(k == nk - 1)
            def _():
                o_ref[...] = acc_ref[...].astype(o_ref.dtype)

    scratch = [] if direct else [pltpu.VMEM((tm, tn), F32)]
    outs, landed = _call(body, [a, b], name=name, grid=grid, in_specs=[a_spec, b_spec], out_specs=[o_spec],
                         out_shape=[o_shape], scratch_shapes=scratch, sem=("parallel", "parallel", "arbitrary"), comm=comm)
    return outs[0] if comm is None else (outs[0], landed)


def _rms(x):
    return lax.rsqrt(jnp.mean(x * x, axis=-1, keepdims=True) + EPS)


def _rms_bwd(dy, x, g):
    r = _rms(x)
    n = x * r
    dn = dy * g
    dx = r * (dn - n * jnp.mean(dn * n, axis=-1, keepdims=True))
    return dx, dy * n


def _sigmoid(x):
    return 1.0 / (1.0 + jnp.exp(-x))


def _rope_rot(t):
    return pltpu.roll(t, HALF_ROPE, 1) - pltpu.roll(t, LANES - HALF_ROPE, 1)


def _lane(shape):
    return lax.broadcasted_iota(jnp.int32, shape, 1)


def _split3(x):
    hi = x.astype(BF16).astype(F32)
    r1 = x - hi
    mid = r1.astype(BF16).astype(F32)
    lo = (r1 - mid).astype(BF16).astype(F32)
    return hi, mid, lo


def _cumsum_rows(x, reverse):
    S = x.shape[0]
    bs = min(256, S)
    nb = S // bs
    r = lax.broadcasted_iota(jnp.int32, (bs, bs), 0)
    c = lax.broadcasted_iota(jnp.int32, (bs, bs), 1)
    tri = jnp.where((c >= r) if reverse else (c <= r), 1.0, 0.0).astype(BF16)
    edge = lax.broadcasted_iota(jnp.int32, (bs, x.shape[1]), 0) == (0 if reverse else bs - 1)
    carry = jnp.zeros((1, x.shape[1]), F32)
    outs = [None] * nb
    for bi in (range(nb - 1, -1, -1) if reverse else range(nb)):
        xb = x[bi * bs:(bi + 1) * bs, :]
        acc = carry
        for term in _split3(xb):
            acc = acc + jnp.dot(tri, term.astype(BF16), preferred_element_type=F32)
        outs[bi] = acc
        carry = jnp.sum(jnp.where(edge, acc, 0.0), axis=0, keepdims=True)
    return jnp.concatenate(outs, axis=0) if nb > 1 else outs[0]


def _gelu_parts(x):
    c0 = math.sqrt(2.0 / math.pi)
    inner = c0 * (x + 0.044715 * (x * x * x))
    t = jnp.tanh(inner)
    g = 0.5 * x * (1.0 + t)
    dg = 0.5 * (1.0 + t) + 0.5 * x * (1.0 - t * t) * (c0 * (1.0 + 3.0 * 0.044715 * (x * x)))
    return g, dg


def _accumulate(ref, value, first):
    @pl.when(first)
    def _():
        ref[...] = value

    @pl.when(jnp.logical_not(first))
    def _():
        ref[...] += value


def _cast_bf16(w, name):
    R, C = w.shape
    tr = _tile(R, 512, 16)

    def body(w_ref, o_ref):
        o_ref[...] = w_ref[...].astype(BF16)

    blk = pl.BlockSpec((tr, C), lambda i: (i, 0))
    return _call(body, [w], name=name, grid=(R // tr,), in_specs=[blk], out_specs=[blk],
                 out_shape=[_sds((R, C), BF16)], sem=("parallel",))[0][0]


def _concat_cols(parts, name):
    T = parts[0].shape[0]
    widths = [p.shape[1] for p in parts]
    tm = _tile(T, ROW_TILE, 16)

    def body(*refs):
        o_ref = refs[-1]
        off = 0
        for p_ref, w in zip(refs[:-1], widths):
            o_ref[:, off:off + w] = p_ref[...].astype(BF16)
            off += w

    return _call(body, parts, name=name, grid=(T // tm,),
                 in_specs=[pl.BlockSpec((tm, w), lambda i: (i, 0)) for w in widths],
                 out_specs=[pl.BlockSpec((tm, sum(widths)), lambda i: (i, 0))],
                 out_shape=[_sds((T, sum(widths)), BF16)], sem=("parallel",))[0][0]


def _prenorm(x, g, comm=None):
    T, D = x.shape
    tm = _tile(T, ROW_TILE, 16)

    def body(x_ref, g_ref, h_ref):
        xv = x_ref[...]
        h_ref[...] = (xv * _rms(xv) * g_ref[...]).astype(BF16)

    row = pl.BlockSpec((tm, D), lambda i: (i, 0))
    (h,), landed = _call(body, [x, g], name="prenorm", grid=(T // tm,),
                         in_specs=[row, pl.BlockSpec((1, D), lambda i: (0, 0))], out_specs=[row],
                         out_shape=[_sds((T, D), BF16)], sem=("parallel",), comm=comm)
    return h, landed


def _split_prep(proj, pos, invf, gq, gkv, bfor, lay, comm=None):
    T = proj.shape[0]
    tm = _tile(T, ROW_TILE, 16)

    def body(q_ref, kv_ref, kpe_ref, fl_ref, pos_ref, invf_ref, gq_ref, gkv_ref, bf_ref,
             qn_ref, kvn_ref, kper_ref, logf_ref, cos_ref, sin_ref):
        ql = q_ref[...]
        qn_ref[...] = (ql * _rms(ql) * gq_ref[...]).astype(BF16)
        kl = kv_ref[...]
        kvn_ref[...] = (kl * _rms(kl) * gkv_ref[...]).astype(BF16)
        ang = pos_ref[...].astype(F32) * invf_ref[...]
        valid = _lane(ang.shape) < ROPE
        cs = jnp.where(valid, jnp.cos(ang), 0.0)
        sn = jnp.where(valid, jnp.sin(ang), 0.0)
        cos_ref[...] = cs
        sin_ref[...] = sn
        kp = jnp.where(valid, kpe_ref[...], 0.0)
        kper_ref[...] = (kp * cs + _rope_rot(kp) * sn).astype(BF16)
        z = fl_ref[...] + bf_ref[...]
        logf_ref[...] = jnp.minimum(z, 0.0) - jnp.log(1.0 + jnp.exp(-jnp.abs(z)))

    def col(width, off):
        return pl.BlockSpec((tm, width), lambda i: (i, off // width))

    def vec(width):
        return pl.BlockSpec((1, width), lambda i: (0, 0))

    def out(width):
        return pl.BlockSpec((tm, width), lambda i: (i, 0))

    return _call(
        body, [proj, proj, proj, proj, pos, invf, gq, gkv, bfor], name="split_prep", grid=(T // tm,),
        in_specs=[col(Q_LORA, lay["q"]), col(KV_LORA, lay["kv"]), col(LANES, lay["kpe"]), col(LANES, lay["fl"]),
                  pl.BlockSpec((tm, 1), lambda i: (i, 0)), vec(LANES), vec(Q_LORA), vec(KV_LORA), vec(LANES)],
        out_specs=[out(Q_LORA), out(KV_LORA), out(LANES), out(LANES), out(LANES), out(LANES)],
        out_shape=[_sds((T, Q_LORA), BF16), _sds((T, KV_LORA), BF16), _sds((T, LANES), BF16),
                   _sds((T, LANES), F32), _sds((T, LANES), F32), _sds((T, LANES), F32)],
        sem=("parallel",), comm=comm)


def _mla_prep(qraw, kvraw, kper, cosT, sinT):
    H, T, _ = qraw.shape
    tm = _tile(T, HEAD_ROW_TILE, 16)

    def body(q_ref, kv_ref, kpe_ref, cos_ref, sin_ref, qo_ref, ko_ref, vo_ref):
        q = q_ref[...]
        pe = q[:, NOPE:]
        pe = jnp.where(_lane(pe.shape) < ROPE, pe, 0.0)
        qo_ref[:, :NOPE] = q[:, :NOPE].astype(BF16)
        qo_ref[:, NOPE:] = (pe * cos_ref[...] + _rope_rot(pe) * sin_ref[...]).astype(BF16)
        kv = kv_ref[...]
        ko_ref[:, :NOPE] = kv[:, :NOPE].astype(BF16)
        ko_ref[:, NOPE:] = kpe_ref[...]
        vo_ref[...] = kv[:, NOPE:].astype(BF16)

    head = pl.BlockSpec((None, tm, ATT_DK), lambda h, i: (h, i, 0))
    tok = pl.BlockSpec((tm, LANES), lambda h, i: (i, 0))
    return _call(
        body, [qraw, kvraw, kper, cosT, sinT], name="mla_prep", grid=(H, T // tm),
        in_specs=[head, head, tok, tok, tok],
        out_specs=[head, head, pl.BlockSpec((None, tm, VDIM), lambda h, i: (h, i, 0))],
        out_shape=[_sds((H, T, ATT_DK), BF16), _sds((H, T, ATT_DK), BF16), _sds((H, T, VDIM), BF16)],
        sem=("parallel", "parallel"))[0]


def _fox_cumsum(logf, B, S, inv_scale):
    T = logf.shape[0]

    def body(l_ref, c_ref):
        c_ref[...] = _cumsum_rows(l_ref[...], reverse=False) * inv_scale

    seq = pl.BlockSpec((S, LANES), lambda b: (b, 0))
    return _call(body, [logf], name="fox_cumsum", grid=(B,), in_specs=[seq], out_specs=[seq],
                 out_shape=[_sds((T, LANES), F32)], sem=("parallel",))[0][0]


def _fox_prep(proj, cs, lay):
    T = proj.shape[0]
    tm = _tile(T, HEAD_ROW_TILE, 16)

    def body(q_ref, k_ref, v_ref, cs_ref, qo_ref, ko_ref, vo_ref):
        h = pl.program_id(0)
        cv = cs_ref[...]
        lane = _lane(cv.shape)
        ccol = jnp.sum(jnp.where(lane == h, cv, 0.0), axis=1, keepdims=True)
        hi, mid, lo = _split3(ccol)
        one = jnp.where(lane < 6, 1.0, 0.0)
        augq = jnp.where(lane == 0, hi, jnp.where(lane == 1, mid, jnp.where(lane == 2, lo, one)))
        augk = jnp.where(lane < 3, 1.0, jnp.where(lane == 3, -hi, jnp.where(lane == 4, -mid, jnp.where(lane == 5, -lo, 0.0))))
        qo_ref[:, :FOX_DIM] = q_ref[...].astype(BF16)
        qo_ref[:, FOX_DIM:] = augq.astype(BF16)
        ko_ref[:, :FOX_DIM] = k_ref[...].astype(BF16)
        ko_ref[:, FOX_DIM:] = augk.astype(BF16)
        vo_ref[...] = v_ref[...].astype(BF16)

    def col(off):
        return pl.BlockSpec((tm, FOX_DIM), lambda h, i: (i, off // FOX_DIM + h))

    head = pl.BlockSpec((None, tm, ATT_DK), lambda h, i: (h, i, 0))
    return _call(
        body, [proj, proj, proj, cs], name="fox_prep", grid=(HEADS, T // tm),
        in_specs=[col(lay["fq"]), col(lay["fk"]), col(lay["fv"]), pl.BlockSpec((tm, LANES), lambda h, i: (i, 0))],
        out_specs=[head, head, pl.BlockSpec((None, tm, VDIM), lambda h, i: (h, i, 0))],
        out_shape=[_sds((HEADS, T, ATT_DK), BF16), _sds((HEADS, T, ATT_DK), BF16), _sds((HEADS, T, VDIM), BF16)],
        sem=("parallel", "parallel"))[0]


def _visible(tq, tk, unit):
    r = lax.broadcasted_iota(jnp.int32, (tq, tk), 0)
    c = lax.broadcasted_iota(jnp.int32, (tq, tk), 1)
    sh = int(math.log2(unit))
    return lax.shift_right_logical(c, sh) <= lax.shift_right_logical(r, sh)


def _attn_fwd(streams, *, B, S, name, comm=None):
    n = len(streams)
    H, T, DK = streams[0][0].shape
    DV = streams[0][2].shape[2]
    tq = _tile(S, ATT_TILE)
    nq = S // tq
    sub = min(ATT_SUB, tq)
    NT = (((1,), (1,)), ((), ()))

    def body(*refs):
        ins, outs, (m_sc, acc_sc) = refs[:3 * n], refs[3 * n:5 * n], refs[5 * n:]
        i, j = pl.program_id(1), pl.program_id(2)

        @pl.when(j == 0)
        def _():
            m_sc[...] = jnp.full(m_sc.shape, NEG_INF, F32)
            acc_sc[...] = jnp.zeros(acc_sc.shape, F32)

        def step(diagonal):
            work = [(t, r) for r in range(tq // sub) for t in range(n)]

            def scores(t, r):
                q_ref, k_ref, _ = ins[3 * t:3 * t + 3]
                kc = (r + 1) * sub if diagonal else tq
                s = lax.dot_general(q_ref[r * sub:(r + 1) * sub, :], k_ref[0:kc, :], NT, preferred_element_type=F32)
                return s * (streams[t][4] * LOG2_E)

            ahead = [scores(*work[w]) for w in range(min(ATT_AHEAD, len(work)))]
            for w, (t, r) in enumerate(work):
                s = ahead.pop(0)
                if w + ATT_AHEAD < len(work):
                    ahead.append(scores(*work[w + ATT_AHEAD]))
                v_ref = ins[3 * t + 2]
                kc = s.shape[1]
                rows = slice(r * sub, (r + 1) * sub)
                if diagonal:
                    own = jnp.where(_visible(sub, sub, streams[t][3]), s[:, kc - sub:], NEG_INF)
                    s = own if kc == sub else jnp.concatenate([s[:, :kc - sub], own], axis=1)
                m_prev = m_sc[t, rows, :]
                mx = s[:, 0:LANES]
                for g in range(1, kc // LANES):
                    mx = jnp.maximum(mx, s[:, g * LANES:(g + 1) * LANES])
                m_new = jnp.maximum(m_prev, jnp.max(mx, axis=1, keepdims=True))
                alpha = jnp.exp2(m_prev - m_new)
                p = jnp.exp2(s - jnp.tile(m_new, (1, kc // LANES))).astype(BF16)
                v_aug = jnp.concatenate([v_ref[0:kc, :], jnp.ones((kc, LANES), BF16)], axis=1)
                acc_sc[t, rows, :] = jnp.tile(alpha, (1, 2)) * acc_sc[t, rows, :] + jnp.dot(
                    p, v_aug, preferred_element_type=F32)
                m_sc[t, rows, :] = m_new

        @pl.when(j < i)
        def _():
            step(False)

        @pl.when(j == i)
        def _():
            step(True)
            for t in range(n):
                o_ref, lse_ref = outs[2 * t:2 * t + 2]
                l = acc_sc[t, :, DV:]
                o_ref[...] = (acc_sc[t, :, :DV] / l).astype(BF16)
                lse_ref[...] = m_sc[t] + jnp.log2(l)

    def qmap(g, i, j):
        return (g % H, (g // H) * nq + i, 0)

    def kmap(g, i, j):
        return (g % H, (g // H) * nq + jnp.minimum(j, i), 0)

    args = [a for st in streams for a in st[:3]]
    outs, landed = _call(
        body, args, name=name, grid=(B * H, nq, nq),
        in_specs=[pl.BlockSpec((None, tq, DK), qmap), pl.BlockSpec((None, tq, DK), kmap),
                  pl.BlockSpec((None, tq, DV), kmap)] * n,
        out_specs=[pl.BlockSpec((tq, DV), lambda g, i, j: ((g // H) * nq + i, g % H)),
                   pl.BlockSpec((None, tq, LANES), qmap)] * n,
        out_shape=[_sds((T, H * DV), BF16), _sds((H, T, LANES), F32)] * n,
        scratch_shapes=[pltpu.VMEM((n, tq, LANES), F32), pltpu.VMEM((n, tq, DV + LANES), F32)],
        sem=("parallel", "parallel", "arbitrary"), comm=comm)
    return [(outs[2 * t], outs[2 * t + 1]) for t in range(n)], landed


def _attn_bwd(streams, *, B, S, name, comm=None):
    n = len(streams)
    H, T, DK = streams[0][0].shape
    DV = streams[0][2].shape[2]
    tq = _tile(S, ATT_TILE)
    nq = S // tq
    sub = min(ATT_SUB, tq)
    NT = (((1,), (1,)), ((), ()))
    TN = (((0,), (0,)), ((), ()))

    def body(*refs):
        ins, outs = refs[:6 * n], refs[6 * n:]
        j, i = pl.program_id(1), pl.program_id(2)

        @pl.when(jnp.logical_and(j == 0, i == 0))
        def _():
            for t in range(n):
                outs[3 * t][...] = jnp.zeros(outs[3 * t].shape, F32)

        @pl.when(i == 0)
        def _():
            for t in range(n):
                outs[3 * t + 1][...] = jnp.zeros(outs[3 * t + 1].shape, F32)
                outs[3 * t + 2][...] = jnp.zeros(outs[3 * t + 2].shape, F32)

        def step(diagonal):
            work = [(t, r) for r in range(tq // sub) for t in range(n)]

            def kcols(r):
                return (r + 1) * sub if diagonal else tq

            def scores(t, r):
                q_ref, k_ref, v_ref, _, do_ref, _ = ins[6 * t:6 * t + 6]
                rows, kc = slice(r * sub, (r + 1) * sub), kcols(r)
                s = lax.dot_general(q_ref[rows, :], k_ref[0:kc, :], NT, preferred_element_type=F32)
                dp = lax.dot_general(do_ref[rows, :], v_ref[0:kc, :], NT, preferred_element_type=F32)
                return s * (streams[t][7] * LOG2_E), dp

            def probs(t, r, s, dp):
                _, _, _, o_ref, do_ref, lse_ref = ins[6 * t:6 * t + 6]
                rows, kc = slice(r * sub, (r + 1) * sub), kcols(r)
                if diagonal:
                    own = jnp.where(_visible(sub, sub, streams[t][6]), s[:, kc - sub:], NEG_INF)
                    s = own if kc == sub else jnp.concatenate([s[:, :kc - sub], own], axis=1)
                p = jnp.exp2(s - jnp.tile(lse_ref[rows, :], (1, kc // LANES)))
                delta = jnp.sum(do_ref[rows, :].astype(F32) * o_ref[rows, :].astype(F32), axis=1, keepdims=True)
                return p.astype(BF16), (p * (dp - delta) * streams[t][7]).astype(BF16)

            def grads(t, r, p, ds):
                q_ref, k_ref, _, _, do_ref, _ = ins[6 * t:6 * t + 6]
                dq_ref, dk_ref, dv_ref = outs[3 * t:3 * t + 3]
                rows, kc = slice(r * sub, (r + 1) * sub), kcols(r)
                dv_ref[0:kc, :] += lax.dot_general(p, do_ref[rows, :], TN, preferred_element_type=F32)
                dk_ref[0:kc, :] += lax.dot_general(ds, q_ref[rows, :], TN, preferred_element_type=F32)
                qrows = pl.ds(pl.multiple_of(i * tq + r * sub, sub), sub)
                dq_ref[qrows, :] += jnp.dot(ds, k_ref[0:kc, :], preferred_element_type=F32)

            nw = len(work)
            sc = {w: scores(*work[w]) for w in range(min(2, nw))}
            pr = {0: probs(*work[0], *sc.pop(0))}
            for w in range(nw):
                if w + 2 < nw:
                    sc[w + 2] = scores(*work[w + 2])
                if w + 1 < nw:
                    pr[w + 1] = probs(*work[w + 1], *sc.pop(w + 1))
                grads(*work[w], *pr.pop(w))

        @pl.when(i > j)
        def _():
            step(False)

        @pl.when(i == j)
        def _():
            step(True)

    def qmap(g, j, i):
        return (g % H, (g // H) * nq + jnp.maximum(i, j), 0)

    def kmap(g, j, i):
        return (g % H, (g // H) * nq + j, 0)

    def omap(g, j, i):
        return ((g // H) * nq + jnp.maximum(i, j), g % H)

    args = [a for st in streams for a in st[:6]]
    outs, landed = _call(
        body, args, name=name, grid=(B * H, nq, nq),
        in_specs=[pl.BlockSpec((None, tq, DK), qmap), pl.BlockSpec((None, tq, DK), kmap),
                  pl.BlockSpec((None, tq, DV), kmap), pl.BlockSpec((tq, DV), omap), pl.BlockSpec((tq, DV), omap),
                  pl.BlockSpec((None, tq, LANES), qmap)] * n,
        out_specs=[pl.BlockSpec((None, S, DK), lambda g, j, i: (g % H, g // H, 0)),
                   pl.BlockSpec((None, tq, DK), kmap), pl.BlockSpec((None, tq, DV), kmap)] * n,
        out_shape=[_sds((H, T, DK), F32), _sds((H, T, DK), F32), _sds((H, T, DV), F32)] * n,
        sem=("parallel", "arbitrary", "arbitrary"), comm=comm)
    return [tuple(outs[3 * t:3 * t + 3]) for t in range(n)], landed


def _gate_merge(am, af, proj, bgate, lay, D, comm=None):
    T = am.shape[0]
    tm = _tile(T, ROW_TILE, 16)
    tn = _tile(D, 1024)

    def body(am_ref, af_ref, gm_ref, gf_ref, bm_ref, bf_ref, o_ref):
        sm = _sigmoid(gm_ref[...] + bm_ref[...])
        sf = _sigmoid(gf_ref[...] + bf_ref[...])
        o_ref[...] = (sm * am_ref[...] + sf * af_ref[...]).astype(BF16)

    og = lay["g"] // tn
    blk = pl.BlockSpec((tm, tn), lambda i, j: (i, j))
    return _call(
        body, [am, af, proj, proj, bgate, bgate], name="gate_merge", grid=(T // tm, D // tn),
        in_specs=[blk, blk, pl.BlockSpec((tm, tn), lambda i, j: (i, og + j)),
                  pl.BlockSpec((tm, tn), lambda i, j: (i, og + D // tn + j)),
                  pl.BlockSpec((1, tn), lambda i, j: (0, j)), pl.BlockSpec((1, tn), lambda i, j: (0, D // tn + j))],
        out_specs=[blk], out_shape=[_sds((T, D), BF16)], sem=("parallel", "parallel"), comm=comm)


def _mid(x, y1, g_pm, g_ffn):
    T, D = x.shape
    tm = _tile(T, ROW_TILE, 16)

    def body(x_ref, y_ref, gp_ref, gf_ref, x1_ref, h2_ref):
        y = y_ref[...]
        x1 = x_ref[...] + y * _rms(y) * gp_ref[...]
        x1_ref[...] = x1
        h2_ref[...] = (x1 * _rms(x1) * gf_ref[...]).astype(BF16)

    row = pl.BlockSpec((tm, D), lambda i: (i, 0))
    vec = pl.BlockSpec((1, D), lambda i: (0, 0))
    return _call(body, [x, y1, g_pm, g_ffn], name="mid", grid=(T // tm,), in_specs=[row, row, vec, vec],
                 out_specs=[row, row], out_shape=[_sds((T, D), F32), _sds((T, D), BF16)], sem=("parallel",))[0]


def _conv3(u, w_ref, bias):
    row = lax.broadcasted_iota(jnp.int32, u.shape, 0)
    u1 = jnp.where(row >= 1, pltpu.roll(u, 1, 0), 0.0)
    u2 = jnp.where(row >= 2, pltpu.roll(u, 2, 0), 0.0)
    return w_ref[0:1, :] * u2 + w_ref[1:2, :] * u1 + w_ref[2:3, :] * u + bias, u1, u2


def _convffn_fwd(u, cw, cb, B, S, F, comm=None):
    T = u.shape[0]
    tn = _tile(F, 256)
    nf = F // tn

    def body(ug_ref, uv_ref, wg_ref, wv_ref, bg_ref, bv_ref, a_ref):
        g, _, _ = _conv3(ug_ref[...], wg_ref, bg_ref[...])
        val, _, _ = _conv3(uv_ref[...], wv_ref, bv_ref[...])
        a_ref[...] = (_gelu_parts(g)[0] * val).astype(BF16)

    def seq(off):
        return pl.BlockSpec((S, tn), lambda b, j: (b, off + j))

    def par(rows, off):
        return pl.BlockSpec((rows, tn), lambda b, j: (0, off + j))

    return _call(body, [u, u, cw, cw, cb, cb], name="convffn_fwd", grid=(B, nf),
                 in_specs=[seq(0), seq(nf), par(3, 0), par(3, nf), par(1, 0), par(1, nf)],
                 out_specs=[seq(0)], out_shape=[_sds((T, F), BF16)], sem=("parallel", "parallel"), comm=comm)


def _convffn_bwd(u, dact, cw, cb, B, S, F, comm=None):
    T = u.shape[0]
    tn = _tile(F, 256)
    nf = F // tn

    def body(ug_ref, uv_ref, da_ref, wg_ref, wv_ref, bg_ref, bv_ref, dug_ref, duv_ref, dpg_ref, dpv_ref):
        b = pl.program_id(1)
        ug, uv, da = ug_ref[...], uv_ref[...], da_ref[...]
        g, ug1, ug2 = _conv3(ug, wg_ref, bg_ref[...])
        val, uv1, uv2 = _conv3(uv, wv_ref, bv_ref[...])
        gel, dgel = _gelu_parts(g)
        dg = da * val * dgel
        dval = da * gel
        row = lax.broadcasted_iota(jnp.int32, ug.shape, 0)

        def back(d, w_ref):
            d1 = jnp.where(row < S - 1, pltpu.roll(d, S - 1, 0), 0.0)
            d2 = jnp.where(row < S - 2, pltpu.roll(d, S - 2, 0), 0.0)
            return w_ref[2:3, :] * d + w_ref[1:2, :] * d1 + w_ref[0:1, :] * d2

        dug_ref[...] = back(dg, wg_ref).astype(BF16)
        duv_ref[...] = back(dval, wv_ref).astype(BF16)

        def sums(d, u0, u1, u2):
            r8 = lax.broadcasted_iota(jnp.int32, (8, d.shape[1]), 0)
            out = jnp.zeros((8, d.shape[1]), F32)
            for k, t in enumerate((d * u2, d * u1, d * u0, d)):
                out = jnp.where(r8 == k, jnp.sum(t, axis=0, keepdims=True), out)
            return out

        _accumulate(dpg_ref, sums(dg, ug, ug1, ug2), b == 0)
        _accumulate(dpv_ref, sums(dval, uv, uv1, uv2), b == 0)

    def seq(off):
        return pl.BlockSpec((S, tn), lambda j, b: (b, off + j))

    def par(rows, off):
        return pl.BlockSpec((rows, tn), lambda j, b: (0, off + j))

    outs, landed = _call(
        body, [u, u, dact, cw, cw, cb, cb], name="convffn_bwd", grid=(nf, B),
        in_specs=[seq(0), seq(nf), seq(0), par(3, 0), par(3, nf), par(1, 0), par(1, nf)],
        out_specs=[seq(0), seq(0), par(8, 0), par(8, 0)],
        out_shape=[_sds((T, F), BF16), _sds((T, F), BF16), _sds((8, F), F32), _sds((8, F), F32)],
        sem=("parallel", "arbitrary"), comm=comm)
    return outs, landed


def _tail(ff, x1, tgt, g):
    T, D = ff.shape
    tm = _tile(T, ROW_TILE, 16)

    def body(ff_ref, x1_ref, t_ref, g_ref, dy_ref, dff_ref, loss_ref, dg_ref):
        i = pl.program_id(0)
        f = ff_ref[...]
        gv = g_ref[...]
        r = _rms(f)
        n = f * r
        e = (x1_ref[...] + n * gv) - t_ref[...]
        dy = e * (1.0 / D)
        dy_ref[...] = dy
        dn = dy * gv
        dff_ref[...] = (r * (dn - n * jnp.mean(dn * n, axis=-1, keepdims=True))).astype(BF16)
        part = 0.5 * jnp.sum(jnp.mean(e * e, axis=-1, keepdims=True), axis=0, keepdims=True)
        _accumulate(loss_ref, jnp.broadcast_to(part, loss_ref.shape), i == 0)
        _accumulate(dg_ref, jnp.sum(dy * n, axis=0, keepdims=True), i == 0)

    row = pl.BlockSpec((tm, D), lambda i: (i, 0))
    vec = pl.BlockSpec((1, D), lambda i: (0, 0))
    return _call(body, [ff, x1, tgt, g], name="tail", grid=(T // tm,), in_specs=[row, row, row, vec],
                 out_specs=[row, row, pl.BlockSpec((8, LANES), lambda i: (0, 0)), vec],
                 out_shape=[_sds((T, D), F32), _sds((T, D), BF16), _sds((8, LANES), F32), _sds((1, D), F32)],
                 sem=("arbitrary",))[0]


def _mid_bwd(dy, dh2, x1, y1, g_ffn, g_pm, comm=None):
    T, D = dy.shape
    tm = _tile(T, ROW_TILE, 16)

    def body(dy_ref, dh_ref, x1_ref, y1_ref, gf_ref, gp_ref, dx1_ref, dy1_ref, dgf_ref, dgp_ref):
        i = pl.program_id(0)
        dh = dh_ref[...]
        d2, dgf = _rms_bwd(dh, x1_ref[...], gf_ref[...])
        dx1 = dy_ref[...] + d2
        dx1_ref[...] = dx1
        d1, dgp = _rms_bwd(dx1, y1_ref[...], gp_ref[...])
        dy1_ref[...] = d1.astype(BF16)
        _accumulate(dgf_ref, jnp.sum(dgf, axis=0, keepdims=True), i == 0)
        _accumulate(dgp_ref, jnp.sum(dgp, axis=0, keepdims=True), i == 0)

    row = pl.BlockSpec((tm, D), lambda i: (i, 0))
    vec = pl.BlockSpec((1, D), lambda i: (0, 0))
    return _call(body, [dy, dh2, x1, y1, g_ffn, g_pm], name="mid_bwd", grid=(T // tm,),
                 in_specs=[row, row, row, row, vec, vec], out_specs=[row, row, vec, vec],
                 out_shape=[_sds((T, D), F32), _sds((T, D), BF16), _sds((1, D), F32), _sds((1, D), F32)],
                 sem=("arbitrary",), comm=comm)


def _gate_bwd(dm, am, af, proj, bgate, lay, D, comm=None):
    T = dm.shape[0]
    tm = _tile(T, ROW_TILE, 16)
    tn = _tile(D, 512)

    def body(dm_ref, am_ref, af_ref, gm_ref, gf_ref, bm_ref, bf_ref,
             dam_ref, daf_ref, dgm_ref, dgf_ref, dbm_ref, dbf_ref):
        i = pl.program_id(1)
        d = dm_ref[...]
        sm = _sigmoid(gm_ref[...] + bm_ref[...])
        sf = _sigmoid(gf_ref[...] + bf_ref[...])
        dam_ref[...] = (d * sm).astype(BF16)
        daf_ref[...] = (d * sf).astype(BF16)
        dgm = d * am_ref[...] * (sm * (1.0 - sm))
        dgf = d * af_ref[...] * (sf * (1.0 - sf))
        dgm_ref[...] = dgm.astype(BF16)
        dgf_ref[...] = dgf.astype(BF16)
        _accumulate(dbm_ref, jnp.sum(dgm, axis=0, keepdims=True), i == 0)
        _accumulate(dbf_ref, jnp.sum(dgf, axis=0, keepdims=True), i == 0)

    og = lay["g"] // tn
    blk = pl.BlockSpec((tm, tn), lambda j, i: (i, j))
    vec = pl.BlockSpec((1, tn), lambda j, i: (0, j))
    return _call(
        body, [dm, am, af, proj, proj, bgate, bgate], name="gate_bwd", grid=(D // tn, T // tm),
        in_specs=[blk, blk, blk, pl.BlockSpec((tm, tn), lambda j, i: (i, og + j)),
                  pl.BlockSpec((tm, tn), lambda j, i: (i, og + D // tn + j)),
                  vec, pl.BlockSpec((1, tn), lambda j, i: (0, D // tn + j))],
        out_specs=[blk, blk, blk, blk, vec, vec],
        out_shape=[_sds((T, D), BF16)] * 4 + [_sds((1, D), F32)] * 2, sem=("parallel", "arbitrary"), comm=comm)


def _mla_bwd_prep(dq, dk, dv, cosT, sinT, comm=None):
    H, T, _ = dq.shape
    tm = _tile(T, HEAD_ROW_TILE, 16)

    def body(dq_ref, dk_ref, dv_ref, cos_ref, sin_ref, dqr_ref, dkv_ref, dkpe_ref):
        h = pl.program_id(1)
        cs, sn = cos_ref[...], sin_ref[...]
        valid = _lane(cs.shape) < ROPE

        def unrope(d):
            d = jnp.where(valid, d, 0.0)
            return d * cs - _rope_rot(d) * sn

        dqv = dq_ref[...]
        dqr_ref[:, :NOPE] = dqv[:, :NOPE].astype(BF16)
        dqr_ref[:, NOPE:] = unrope(dqv[:, NOPE:]).astype(BF16)
        dkv_ = dk_ref[...]
        dkv_ref[:, :NOPE] = dkv_[:, :NOPE].astype(BF16)
        dkv_ref[:, NOPE:] = dv_ref[...].astype(BF16)
        _accumulate(dkpe_ref, unrope(dkv_[:, NOPE:]), h == 0)

    head = pl.BlockSpec((None, tm, ATT_DK), lambda i, h: (h, i, 0))
    tok = pl.BlockSpec((tm, LANES), lambda i, h: (i, 0))
    return _call(
        body, [dq, dk, dv, cosT, sinT], name="mla_bwd_prep", grid=(T // tm, H),
        in_specs=[head, head, pl.BlockSpec((None, tm, VDIM), lambda i, h: (h, i, 0)), tok, tok],
        out_specs=[head, head, tok],
        out_shape=[_sds((H, T, ATT_DK), BF16), _sds((H, T, ATT_DK), BF16), _sds((T, LANES), F32)],
        sem=("parallel", "arbitrary"), comm=comm)


def _fox_bwd_prep(dq, dk, proj, bfor, lay, B, S, inv_scale):
    H, T, _ = dq.shape

    def body(dq_ref, dk_ref, fl_ref, bf_ref, dfl_ref, dbf_ref, dc_sc):
        b, h = pl.program_id(0), pl.program_id(1)
        lane = _lane(dc_sc.shape)
        col = jnp.sum(jnp.where(lane == 0, dq_ref[...], 0.0) - jnp.where(lane == 3, dk_ref[...], 0.0),
                      axis=1, keepdims=True)

        @pl.when(h == 0)
        def _():
            dc_sc[...] = jnp.zeros(dc_sc.shape, F32)

        dc_sc[...] = jnp.where(lane == h, col, dc_sc[...])

        @pl.when(h == H - 1)
        def _():
            dlogf = _cumsum_rows(dc_sc[...] * inv_scale, reverse=True)
            z = fl_ref[...] + bf_ref[...]
            dz = jnp.where(lane < H, dlogf * (1.0 / (1.0 + jnp.exp(z))), 0.0)
            dfl_ref[...] = dz
            _accumulate(dbf_ref, jnp.sum(dz, axis=0, keepdims=True), b == 0)

    aug = pl.BlockSpec((None, S, LANES), lambda b, h: (h, b, 1))
    seq = pl.BlockSpec((S, LANES), lambda b, h: (b, 0))
    vec = pl.BlockSpec((1, LANES), lambda b, h: (0, 0))
    return _call(
        body, [dq, dk, proj, bfor], name="fox_bwd_prep", grid=(B, H),
        in_specs=[aug, aug, pl.BlockSpec((S, LANES), lambda b, h: (b, lay["fl"] // LANES)), vec],
        out_specs=[seq, vec], out_shape=[_sds((T, LANES), F32), _sds((1, LANES), F32)],
        scratch_shapes=[pltpu.VMEM((S, LANES), F32)], sem=("arbitrary", "arbitrary"))[0]


def _heads_to_cols(dq, dk, dv):
    H, T, _ = dq.shape
    tm = _tile(T, HEAD_ROW_TILE, 16)

    def body(a_ref, b_ref, c_ref, ao_ref, bo_ref, co_ref):
        ao_ref[...] = a_ref[...].astype(BF16)
        bo_ref[...] = b_ref[...].astype(BF16)
        co_ref[...] = c_ref[...].astype(BF16)

    src = pl.BlockSpec((None, tm, FOX_DIM), lambda i, h: (h, i, 0))
    dst = pl.BlockSpec((tm, FOX_DIM), lambda i, h: (i, h))
    return _call(body, [dq, dk, dv], name="heads_to_cols", grid=(T // tm, H), in_specs=[src, src, src],
                 out_specs=[dst, dst, dst], out_shape=[_sds((T, H * FOX_DIM), BF16)] * 3,
                 sem=("parallel", "parallel"))[0]


def _lat_bwd(dqn, dkvn, proj, gq, gkv, lay):
    T = dqn.shape[0]
    tm = _tile(T, ROW_TILE, 16)

    def body(dq_ref, dkv_ref, q_ref, kv_ref, gq_ref, gkv_ref, dql_ref, dkl_ref, dgq_ref, dgkv_ref):
        i = pl.program_id(0)
        dql, dgq = _rms_bwd(dq_ref[...], q_ref[...], gq_ref[...])
        dkl, dgkv = _rms_bwd(dkv_ref[...], kv_ref[...], gkv_ref[...])
        dql_ref[...] = dql.astype(BF16)
        dkl_ref[...] = dkl.astype(BF16)
        _accumulate(dgq_ref, jnp.sum(dgq, axis=0, keepdims=True), i == 0)
        _accumulate(dgkv_ref, jnp.sum(dgkv, axis=0, keepdims=True), i == 0)

    def blk(width, off=0):
        return pl.BlockSpec((tm, width), lambda i: (i, off // width))

    def vec(width):
        return pl.BlockSpec((1, width), lambda i: (0, 0))

    return _call(
        body, [dqn, dkvn, proj, proj, gq, gkv], name="lat_bwd", grid=(T // tm,),
        in_specs=[blk(Q_LORA), blk(KV_LORA), blk(Q_LORA, lay["q"]), blk(KV_LORA, lay["kv"]), vec(Q_LORA), vec(KV_LORA)],
        out_specs=[blk(Q_LORA), blk(KV_LORA), vec(Q_LORA), vec(KV_LORA)],
        out_shape=[_sds((T, Q_LORA), BF16), _sds((T, KV_LORA), BF16), _sds((1, Q_LORA), F32), _sds((1, KV_LORA), F32)],
        sem=("arbitrary",))[0]


def _final_dx(dx1, dh, x, g, comm=None):
    T, D = x.shape
    tm = _tile(T, ROW_TILE, 16)

    def body(dx1_ref, dh_ref, x_ref, g_ref, dx_ref, dg_ref):
        i = pl.program_id(0)
        d, dg = _rms_bwd(dh_ref[...], x_ref[...], g_ref[...])
        dx_ref[...] = dx1_ref[...] + d
        _accumulate(dg_ref, jnp.sum(dg, axis=0, keepdims=True), i == 0)

    row = pl.BlockSpec((tm, D), lambda i: (i, 0))
    vec = pl.BlockSpec((1, D), lambda i: (0, 0))
    return _call(body, [dx1, dh, x, g], name="final_dx", grid=(T // tm,), in_specs=[row, row, row, vec],
                 out_specs=[row, vec], out_shape=[_sds((T, D), F32), _sds((1, D), F32)], sem=("arbitrary",), comm=comm)


def _chip_sum(pieces, paired, qc, name):
    G, R, C = pieces.shape
    tr = _tile(R, 256, 16)

    def body(qc_ref, g_ref, p_ref, keep_ref, send_ref):
        s = pl.program_id(1)
        tot = g_ref[...] + p_ref[...]

        @pl.when(s == 0)
        def _():
            keep_ref[...] = tot

        @pl.when(s > 0)
        def _():
            send_ref[...] = tot.astype(send_ref.dtype)

    grid_spec = pltpu.PrefetchScalarGridSpec(
        num_scalar_prefetch=1, grid=(R // tr, N_CHIP),
        in_specs=[pl.BlockSpec((None, tr, C), lambda i, s, qc: (2 * (qc[0] ^ s) + qc[1], i, 0)),
                  pl.BlockSpec((None, tr, C), lambda i, s, qc: (qc[0] ^ s, i, 0))],
        out_specs=[pl.BlockSpec((tr, C), lambda i, s, qc: (i, 0)),
                   pl.BlockSpec((None, tr, C), lambda i, s, qc: (jnp.maximum(s - 1, 0), i, 0))])
    send_dtype = BF16 if R >= 16 else pieces.dtype
    return pl.pallas_call(
        body, name=name, grid_spec=grid_spec,
        out_shape=[_sds((R, C), F32), _sds((3, R, C), send_dtype)],
        compiler_params=pltpu.CompilerParams(dimension_semantics=("arbitrary", "arbitrary"),
                                             vmem_limit_bytes=VMEM_LIMIT_BYTES),
    )(qc, pieces, paired)


def _adamw_math(w, g, m, v):
    m = ADAM_B1 * m + (1.0 - ADAM_B1) * g
    v = ADAM_B2 * v + (1.0 - ADAM_B2) * (g * g)
    m_hat = m / (1.0 - ADAM_B1 ** ADAM_STEP)
    v_hat = v / (1.0 - ADAM_B2 ** ADAM_STEP)
    delta = -ADAM_LR * (m_hat / (jnp.sqrt(v_hat) + ADAM_EPS) + ADAM_WD * w)
    return delta, m, v


def _sum_adamw(keep, pieces, w, m, v, name):
    R, C = w.shape
    P = pieces.shape[0]
    tr = _tile(R, 256, 16)

    def body(*refs):
        if keep is None:
            p_ref, w_ref, m_ref, v_ref, g_ref, d_ref, mo_ref, vo_ref = refs
            g = p_ref[0].astype(F32)
            rest = range(1, P)
        else:
            k_ref, p_ref, w_ref, m_ref, v_ref, g_ref, d_ref, mo_ref, vo_ref = refs
            g = k_ref[...]
            rest = range(P)
        for q in rest:
            g = g + p_ref[q].astype(F32)
        g_ref[...] = g
        d_ref[...], mo_ref[...], vo_ref[...] = _adamw_math(w_ref[...], g, m_ref[...], v_ref[...])

    blk = pl.BlockSpec((tr, C), lambda i: (i, 0))
    pblk = pl.BlockSpec((P, tr, C), lambda i: (0, i, 0))
    args = [pieces, w, m, v] if keep is None else [keep, pieces, w, m, v]
    specs = [pblk, blk, blk, blk] if keep is None else [blk, pblk, blk, blk, blk]
    return _call(body, args, name=name, grid=(R // tr,), in_specs=specs, out_specs=[blk] * 4,
                 out_shape=[_sds((R, C), F32)] * 4, sem=("parallel",))[0]


def _layout(D):
    lay = {"q": 0, "kv": Q_LORA, "kpe": Q_LORA + KV_LORA}
    lay["fq"] = lay["kpe"] + LANES
    lay["fk"] = lay["fq"] + HEADS * FOX_DIM
    lay["fv"] = lay["fk"] + HEADS * FOX_DIM
    lay["fl"] = lay["fv"] + HEADS * FOX_DIM
    lay["g"] = lay["fl"] + LANES
    lay["end"] = lay["g"] + 2 * D
    return lay


def kernel(x, positions, pre_mix_norm, w_in, q_a_norm, w_uq, kv_a_norm, w_ukv, b_forget, b_gate, w_branch_mla, w_branch_fox, w_out, post_mix_norm, pre_ffn_norm, w_up, conv_w, conv_b, w_down, post_ffn_norm, loss_target, m_pre_mix_norm, m_w_in, m_q_a_norm, m_w_uq, m_kv_a_norm, m_w_ukv, m_b_forget, m_b_gate, m_w_branch_mla, m_w_branch_fox, m_w_out, m_post_mix_norm, m_pre_ffn_norm, m_w_up, m_conv_w, m_conv_b, m_w_down, m_post_ffn_norm, v_pre_mix_norm, v_w_in, v_q_a_norm, v_w_uq, v_kv_a_norm, v_w_ukv, v_b_forget, v_b_gate, v_w_branch_mla, v_w_branch_fox, v_w_out, v_post_mix_norm, v_pre_ffn_norm, v_w_up, v_conv_w, v_conv_b, v_w_down, v_post_ffn_norm):
    B, S, D = x.shape
    T = B * S
    F = conv_b.shape[0] // 2
    lay = _layout(D)
    n_in = w_in.shape[1]
    d_in = N_DEV * n_in
    seg_a = Q_LORA + KV_LORA + ROPE
    seg_b = 3 * HEADS * FOX_DIM + HEADS
    mla_scale = (NOPE + ROPE) ** -0.5
    fox_scale = FOX_DIM ** -0.5
    ax, ay, ac = (lax.axis_index(a) for a in MESH_AXES)
    qc = jnp.stack([2 * ax + ay, ac]).astype(jnp.int32)

    def row(vec, width=None):
        vec = vec.reshape(1, -1)
        if width is not None and vec.shape[1] < width:
            vec = jnp.pad(vec, ((0, 0), (0, width - vec.shape[1])))
        return vec

    x2 = x.reshape(T, D)
    win_s = _cast_bf16(w_in, "cast_w_in")
    h, (win_g,) = _prenorm(x2, row(pre_mix_norm), comm=_Comm([_GatherPlan([win_s], mid_frac=0.3)]))
    small_s = [_cast_bf16(w, "cast_" + n) for w, n in
               [(w_uq, "w_uq"), (w_ukv, "w_ukv"), (w_branch_mla, "w_branch_mla"), (w_branch_fox, "w_branch_fox"), (w_out, "w_out")]]
    wup_s = _cast_bf16(w_up, "cast_w_up")
    wdown_s = _cast_bf16(w_down, "cast_w_down")

    def shard_cols(lo, hi):
        out = []
        for g in range(lo // n_in, (hi - 1) // n_in + 1):
            out.append(win_g[g][:, max(lo, g * n_in) - g * n_in:min(hi, (g + 1) * n_in) - g * n_in])
        return out

    w_perm = jnp.concatenate(
        shard_cols(0, seg_a) + [jnp.zeros((D, LANES - ROPE), BF16)] + shard_cols(seg_a, seg_a + seg_b)
        + [jnp.zeros((D, LANES - HEADS), BF16)] + shard_cols(seg_a + seg_b, d_in), axis=1)

    tgt = loss_target.reshape(T, D)
    pos = positions.reshape(T, 1)
    inv_freq = 1.0 / (ROPE_THETA ** (jnp.arange(0, ROPE, 2, dtype=F32) / ROPE))
    invf = row(jnp.concatenate([inv_freq, inv_freq]), LANES)
    g_pre, g_q, g_kv = row(pre_mix_norm), row(q_a_norm), row(kv_a_norm)
    g_pm, g_ffn, g_pf = row(post_mix_norm), row(pre_ffn_norm), row(post_ffn_norm)
    bfor = row(b_forget, LANES)
    bgate = row(b_gate)
    cb_full = row(conv_b)

    def own_plan(blocks):
        return _Comm([_GatherOwnPlan(blocks)])

    def pass_plan(gathered):
        return _Comm([_GatherPassPlan(gathered)])

    def pair_plan(gs):
        return _Comm([_PairScatterPlan(gs)])

    def chip_plan(gs):
        return _Comm([_ChipScatterPlan(gs)])

    proj, small_g = _matmul(h, w_perm, mode="nn", name="mm_proj", comm=own_plan(small_s + [conv_w]))
    (qn, kvn, kper, logf, cosT, sinT), (wuq_g, wukv_g, wbm_g, wbf_g, wout_g, cw_g) = _split_prep(
        proj, pos, invf, g_q, g_kv, bfor, lay, comm=pass_plan(small_g))
    wuq_pad = jnp.pad(wuq_g, ((0, 0), (0, 0), (0, ATT_DK - NOPE - ROPE)))
    wbm = jnp.transpose(wbm_g, (1, 0, 2)).reshape(HEADS * VDIM, D)
    wbf = jnp.transpose(wbf_g, (1, 0, 2)).reshape(HEADS * FOX_DIM, D)
    wout = wout_g.reshape(D, D)
    cw_full = jnp.transpose(cw_g, (1, 0, 2)).reshape(3, 2 * F)

    qraw = _matmul(qn, wuq_pad, mode="nn", name="mm_q", out_blocks=ATT_DK)
    kvraw = _matmul(kvn, wukv_g, mode="nn", name="mm_kv", out_blocks=NOPE + VDIM)
    q_mla, k_mla, v_mla = _mla_prep(qraw, kvraw, kper, cosT, sinT)
    cs = _fox_cumsum(logf, B, S, 1.0 / fox_scale)
    q_fox, k_fox, v_fox = _fox_prep(proj, cs, lay)
    ((o_mla, lse_mla), (o_fox, lse_fox)), wup_half = _attn_fwd(
        [(q_mla, k_mla, v_mla, MLA_UNIT, mla_scale), (q_fox, k_fox, v_fox, 1, fox_scale)], B=B, S=S,
        name="attn_fwd", comm=own_plan([wup_s]))
    a_m = _matmul(o_mla, wbm, mode="nn", name="mm_branch_mla")
    a_f = _matmul(o_fox, wbf, mode="nn", name="mm_branch_fox")
    (merged,), (wup_g,) = _gate_merge(a_m, a_f, proj, bgate, lay, D, comm=pass_plan(wup_half))
    n_up = wup_g.shape[2]
    y1 = _matmul(merged, wout, mode="nn", name="mm_out")
    x1, h2 = _mid(x2, y1, g_pm, g_ffn)
    u, wdown_half = _matmul(h2, wup_g, mode="nn", name="mm_up", tn=n_up, comm=own_plan([wdown_s]))
    (act,), (wdown_g,) = _convffn_fwd(u, cw_full, cb_full, B, S, F, comm=pass_plan(wdown_half))
    wdown = wdown_g.reshape(F, D)
    ff = _matmul(act, wdown, mode="nn", name="mm_down", tk=F // 2)
    dy, dff, loss_part, dg_pf = _tail(ff, x1, tgt, g_pf)

    dact = _matmul(dff, wdown, mode="nt", name="mm_dact", tn=F // 4)
    dw_down = _matmul(act, dff, mode="tn", name="mm_dw_down", tm=F // 4, tn=512).reshape(N_DEV, F // N_DEV, D)
    (du_g, du_v, dcp_g, dcp_v), (pa_down,) = _convffn_bwd(u, dact, cw_full, cb_full, B, S, F, comm=pair_plan([dw_down]))
    keep_down, sb_down = _chip_sum(dw_down, pa_down, qc, "chipsum_w_down")
    du = _concat_cols([du_g, du_v], "concat_du")
    dh2, (rb_down,) = _matmul(du, wup_g, mode="nt", name="mm_dh2", tn=2048, comm=chip_plan([sb_down]))
    dw_up = _matmul(h2, du, mode="tn", name="mm_dw_up", out_blocks=n_up, tm=512, tn=n_up)
    (dx1, dy1, dg_ffn, dg_pm), _ = _mid_bwd(dy, dh2, x1, y1, g_ffn, g_pm)
    dmerged = _matmul(dy1, wout, mode="nt", name="mm_dmerged")
    dw_out = _matmul(merged, dy1, mode="tn", name="mm_dw_out").reshape(N_DEV, D // N_DEV, D)
    (da_m, da_f, dgl_m, dgl_f, dbg_m, dbg_f), (pa_up,) = _gate_bwd(dmerged, a_m, a_f, proj, bgate, lay, D,
                                                                   comm=pair_plan([dw_up]))
    keep_up, sb_up = _chip_sum(dw_up, pa_up, qc, "chipsum_w_up")
    dw_bm = _matmul(o_mla, da_m, mode="tn", name="mm_dw_branch_mla", out_blocks=D // N_DEV)
    dw_bf = _matmul(o_fox, da_f, mode="tn", name="mm_dw_branch_fox", out_blocks=D // N_DEV)
    mix = [dw_out, dw_bm, dw_bf]
    do_mla, pa_mix = _matmul(da_m, wbm, mode="nt", name="mm_do_mla", out_dtype=BF16, comm=pair_plan(mix))
    do_fox = _matmul(da_f, wbf, mode="nt", name="mm_do_fox", out_dtype=BF16)
    mix_sums = [_chip_sum(g, p, qc, "chipsum_" + n) for g, p, n in zip(mix, pa_mix, ["w_out", "w_branch_mla", "w_branch_fox"])]
    ((dq_m, dk_m, dv_m), (dq_f, dk_f, dv_f)), (rb_up,) = _attn_bwd(
        [(q_mla, k_mla, v_mla, o_mla, do_mla, lse_mla, MLA_UNIT, mla_scale),
         (q_fox, k_fox, v_fox, o_fox, do_fox, lse_fox, 1, fox_scale)], B=B, S=S, name="attn_bwd",
        comm=chip_plan([sb_up]))
    (dqraw, dkvraw, dkpe), rb_mix = _mla_bwd_prep(dq_m, dk_m, dv_m, cosT, sinT, comm=chip_plan([s[1] for s in mix_sums]))
    dqn = _matmul(dqraw, wuq_pad, mode="nt", name="mm_dqn")
    dw_uq = _matmul(qn, dqraw, mode="tn", name="mm_dw_uq", out_blocks=ATT_DK)[:, :, :NOPE + ROPE]
    dkvn = _matmul(dkvraw, wukv_g, mode="nt", name="mm_dkvn")
    dw_ukv = _matmul(kvn, dkvraw, mode="tn", name="mm_dw_ukv", out_blocks=NOPE + VDIM)
    dqlat, dkvlat, dg_q, dg_kv = _lat_bwd(dqn, dkvn, proj, g_q, g_kv, lay)
    dfl, dbfor = _fox_bwd_prep(dq_f, dk_f, proj, bfor, lay, B, S, 1.0 / fox_scale)
    dfq, dfk, dfv = _heads_to_cols(dq_f, dk_f, dv_f)
    dproj = _concat_cols([dqlat, dkvlat, dkpe, dfq, dfk, dfv, dfl, dgl_m, dgl_f], "concat_dproj")
    dw_perm = _matmul(h, dproj, mode="tn", name="mm_dw_in")
    segs = [(0, seg_a, 0), (seg_a, seg_a + seg_b, lay["fq"] - seg_a), (seg_a + seg_b, d_in, lay["g"] - seg_a - seg_b)]

    def piece(g):
        lo, hi = g * n_in, (g + 1) * n_in
        parts = [dw_perm[:, max(lo, s0) + sh:min(hi, s1) + sh] for s0, s1, sh in segs if max(lo, s0) < min(hi, s1)]
        return parts[0] if len(parts) == 1 else jnp.concatenate(parts, axis=1)

    dw_in = jnp.stack([piece(g) for g in range(N_DEV)])
    dcw = jnp.transpose(jnp.concatenate([dcp_g[0:3], dcp_v[0:3]], axis=1).reshape(3, N_DEV, (2 * F) // N_DEV), (1, 0, 2))
    late = [dw_in, dw_uq, dw_ukv, dcw]
    pa_late = _exchange_alone(pair_plan(late), "pair_late")
    late_sums = [_chip_sum(g, p, qc, "chipsum_" + n) for g, p, n in zip(late, pa_late, ["w_in", "w_uq", "w_ukv", "conv_w"])]
    dh, rb_late = _matmul(dproj, w_perm, mode="nt", name="mm_dh", tn=2048, tk=2048, comm=chip_plan([s[1] for s in late_sums]))
    (grad_x, dg_pre), _ = _final_dx(dx1, dh, x2, g_pre)

    big_out = {}

    def finish(n, keep, pieces, w, m, v):
        big_out[n] = _sum_adamw(keep, pieces, w, m, v, "adamw_" + n)

    finish("w_down", keep_down, rb_down, w_down, m_w_down, v_w_down)
    finish("w_up", keep_up, rb_up, w_up, m_w_up, v_w_up)
    finish("w_out", mix_sums[0][0], rb_mix[0], w_out, m_w_out, v_w_out)
    finish("w_branch_mla", mix_sums[1][0], rb_mix[1], w_branch_mla, m_w_branch_mla, v_w_branch_mla)
    finish("w_branch_fox", mix_sums[2][0], rb_mix[2], w_branch_fox, m_w_branch_fox, v_w_branch_fox)
    finish("w_in", late_sums[0][0], rb_late[0], w_in, m_w_in, v_w_in)
    finish("w_uq", late_sums[1][0], rb_late[1], w_uq, m_w_uq, v_w_uq)
    finish("w_ukv", late_sums[2][0], rb_late[2], w_ukv, m_w_ukv, v_w_ukv)
    finish("conv_w", late_sums[3][0], rb_late[3], conv_w, m_conv_w, v_conv_w)

    widths = [D, Q_LORA, KV_LORA, LANES, 2 * D, D, D, 2 * F, D]
    small_names = ["pre_mix_norm", "q_a_norm", "kv_a_norm", "b_forget", "b_gate", "post_mix_norm", "pre_ffn_norm",
                   "conv_b", "post_ffn_norm"]
    true_w = [D, Q_LORA, KV_LORA, HEADS, 2 * D, D, D, 2 * F, D]
    dcb = jnp.concatenate([dcp_g[3:4], dcp_v[3:4]], axis=1)
    part = jnp.concatenate([dg_pre, dg_q, dg_kv, dbfor, dbg_m, dbg_f, dg_pm, dg_ffn, dcb, dg_pf], axis=1)

    def pack(vals):
        return jnp.concatenate([row(a, wd) for a, wd in zip(vals, widths)], axis=1)

    sw = pack([pre_mix_norm, q_a_norm, kv_a_norm, b_forget, b_gate, post_mix_norm, pre_ffn_norm, conv_b, post_ffn_norm])
    sm = pack([m_pre_mix_norm, m_q_a_norm, m_kv_a_norm, m_b_forget, m_b_gate, m_post_mix_norm, m_pre_ffn_norm,
               m_conv_b, m_post_ffn_norm])
    sv = pack([v_pre_mix_norm, v_q_a_norm, v_kv_a_norm, v_b_forget, v_b_gate, v_post_mix_norm, v_pre_ffn_norm,
               v_conv_b, v_post_ffn_norm])
    (parts_all,) = _exchange_alone(_Comm([_DirectGatherPlan([part])]), "gather_small")
    sg, sd, smo, svo = _sum_adamw(None, parts_all, sw, sm, sv, "adamw_small")
    small_out = {}
    off = 0
    for n, wd, tw in zip(small_names, widths, true_w):
        small_out[n] = tuple(a[0, off:off + tw] for a in (sg, sd, smo, svo))
        off += wd

    loss = lax.psum(loss_part[0, 0], MESH_AXES)
    order = ["pre_mix_norm", "w_in", "q_a_norm", "w_uq", "kv_a_norm", "w_ukv", "b_forget", "b_gate", "w_branch_mla",
             "w_branch_fox", "w_out", "post_mix_norm", "pre_ffn_norm", "w_up", "conv_w", "conv_b", "w_down",
             "post_ffn_norm"]
    res = {**big_out, **small_out}
    outs = [loss, grad_x.reshape(B, S, D)]
    for kind in range(4):
        outs += [res[n][kind] for n in order]
    return tuple(outs)
```

```python
import math

import jax
import jax.numpy as jnp
from jax import lax
from jax.experimental import pallas as pl
from jax.experimental.pallas import tpu as pltpu

F32 = jnp.float32
BF16 = jnp.bfloat16

N_DEV = 8
N_CHIP = 4
HEADS = 8
NOPE = 128
ROPE = 64
HALF_ROPE = ROPE // 2
VDIM = 128
Q_LORA = 512
KV_LORA = 256
FOX_DIM = 128
ATT_DK = 256
MLA_UNIT = 64
ROPE_THETA = 10000.0
EPS = 1e-6
NEG_INF = -1e30
LANES = 128
LOG2_E = 1.4426950408889634

ADAM_LR = 0.001
ADAM_B1 = 0.9
ADAM_B2 = 0.999
ADAM_EPS = 1e-08
ADAM_WD = 0.01
ADAM_STEP = 10

VMEM_LIMIT_BYTES = 56 * 1024 * 1024
ROW_TILE = 256
HEAD_ROW_TILE = 1024
ATT_TILE = 1024
ATT_SUB = 256
ATT_AHEAD = 3
MM_TILE = 1024

MESH_AXES = ("x", "y", "c")
ANY = pl.BlockSpec(memory_space=pl.ANY)


def _tile(n, pref, align=LANES):
    if n <= pref:
        return n
    t = (pref // align) * align
    while t >= align:
        if n % t == 0:
            return t
        t -= align
    return n


def _sds(shape, dtype):
    return jax.ShapeDtypeStruct(shape, dtype)


def _coords():
    x, y, c = (lax.axis_index(ax) for ax in MESH_AXES)
    return x, y, c


def _chip_rel(x, y, r):
    return (1 - x if r & 2 else x), (1 - y if r & 1 else y)


def _rcopy(src, dst, sems, w, k, dev):
    return pltpu.make_async_remote_copy(src_ref=src, dst_ref=dst, send_sem=sems[0].at[w, k], recv_sem=sems[1].at[w, k],
                                        device_id=dev, device_id_type=pl.DeviceIdType.MESH)


class _GatherPlan:
    def __init__(self, blocks, mid_frac=0.5):
        self.ins = list(blocks)
        self.out_shapes = [_sds((N_DEV,) + b.shape, b.dtype) for b in blocks]
        n = len(blocks)
        self.scratch = [pltpu.SemaphoreType.DMA((n, 7)), pltpu.SemaphoreType.DMA((n, 7)), pltpu.SemaphoreType.DMA((n,))]
        self.mid_frac = mid_frac

    def first(self, ins, outs, sems):
        x, y, c = _coords()
        me = 4 * x + 2 * y + c
        for w in range(len(ins)):
            pltpu.make_async_copy(ins[w], outs[w].at[me], sems[2].at[w]).start()
            _rcopy(ins[w], outs[w].at[me], sems, w, 0, (x, y, 1 - c)).start()
            for r in (1, 2, 3):
                px, py = _chip_rel(x, y, r)
                _rcopy(ins[w], outs[w].at[me], sems, w, r, (px, py, c)).start()

    def mid(self, ins, outs, sems):
        x, y, c = _coords()
        for w in range(len(ins)):
            for r in (1, 2, 3):
                px, py = _chip_rel(x, y, r)
                blk = outs[w].at[4 * px + 2 * py + c]
                _rcopy(ins[w], blk, sems, w, r, (px, py, c)).wait_recv()
                _rcopy(blk, blk, sems, w, 3 + r, (x, y, 1 - c)).start()

    def last(self, ins, outs, sems):
        x, y, c = _coords()
        me = 4 * x + 2 * y + c
        sib = (x, y, 1 - c)
        for w in range(len(ins)):
            _rcopy(ins[w], outs[w].at[4 * x + 2 * y + 1 - c], sems, w, 0, sib).wait_recv()
            for r in (1, 2, 3):
                px, py = _chip_rel(x, y, r)
                blk = outs[w].at[4 * px + 2 * py + 1 - c]
                _rcopy(blk, blk, sems, w, 3 + r, sib).wait_recv()
            for k in range(7):
                _rcopy(ins[w], outs[w].at[me], sems, w, k, sib).wait_send()
            pltpu.make_async_copy(ins[w], outs[w].at[me], sems[2].at[w]).wait()


class _GatherOwnPlan:
    mid = None

    def __init__(self, blocks, rows=None, into=None):
        self.n = len(blocks)
        self.rows = rows
        self.ins = list(blocks) + list(into or [])
        self.out_shapes = [_sds((N_DEV,) + b.shape, b.dtype) for b in blocks]
        self.aliases = [(self.n + i, i) for i in range(len(into or []))]
        n = self.n
        self.scratch = [pltpu.SemaphoreType.DMA((n, 4)), pltpu.SemaphoreType.DMA((n, 4)), pltpu.SemaphoreType.DMA((n,))]

    def _cut(self, ref):
        return ref if self.rows is None else ref.at[pl.ds(self.rows[0], self.rows[1] - self.rows[0])]

    def first(self, ins, outs, sems):
        x, y, c = _coords()
        me = 4 * x + 2 * y + c
        for w in range(self.n):
            src, dst = self._cut(ins[w]), self._cut(outs[w].at[me])
            pltpu.make_async_copy(src, dst, sems[2].at[w]).start()
            _rcopy(src, dst, sems, w, 0, (x, y, 1 - c)).start()
            for r in (1, 2, 3):
                px, py = _chip_rel(x, y, r)
                _rcopy(src, dst, sems, w, r, (px, py, c)).start()

    def last(self, ins, outs, sems):
        x, y, c = _coords()
        me = 4 * x + 2 * y + c
        for w in range(self.n):
            src = self._cut(ins[w])
            cp = _rcopy(src, self._cut(outs[w].at[4 * x + 2 * y + 1 - c]), sems, w, 0, (x, y, 1 - c))
            cp.wait_recv()
            cp.wait_send()
            for r in (1, 2, 3):
                px, py = _chip_rel(x, y, r)
                cp = _rcopy(src, self._cut(outs[w].at[4 * px + 2 * py + c]), sems, w, r, (px, py, c))
                cp.wait_recv()
                cp.wait_send()
            pltpu.make_async_copy(src, self._cut(outs[w].at[me]), sems[2].at[w]).wait()


class _GatherPassPlan:
    mid = None

    def __init__(self, gathered):
        self.ins = list(gathered)
        self.out_shapes = [_sds(g.shape, g.dtype) for g in gathered]
        self.aliases = [(i, i) for i in range(len(gathered))]
        n = len(gathered)
        self.scratch = [pltpu.SemaphoreType.DMA((n, 3)), pltpu.SemaphoreType.DMA((n, 3))]

    def first(self, ins, outs, sems):
        x, y, c = _coords()
        for w in range(len(ins)):
            for r in (1, 2, 3):
                px, py = _chip_rel(x, y, r)
                blk = 4 * px + 2 * py + c
                _rcopy(ins[w].at[blk], outs[w].at[blk], sems, w, r - 1, (x, y, 1 - c)).start()

    def last(self, ins, outs, sems):
        x, y, c = _coords()
        for w in range(len(ins)):
            for r in (1, 2, 3):
                px, py = _chip_rel(x, y, r)
                blk = 4 * px + 2 * py + 1 - c
                cp = _rcopy(ins[w].at[blk], outs[w].at[blk], sems, w, r - 1, (x, y, 1 - c))
                cp.wait_recv()
                cp.wait_send()


class _DirectGatherPlan:
    mid = None

    def __init__(self, blocks):
        self.ins = list(blocks)
        self.out_shapes = [_sds((N_DEV,) + b.shape, b.dtype) for b in blocks]
        n = len(blocks)
        self.scratch = [pltpu.SemaphoreType.DMA((n, 7)), pltpu.SemaphoreType.DMA((n, 7)), pltpu.SemaphoreType.DMA((n,))]

    @staticmethod
    def _peer(x, y, c, r):
        return (1 - x if r & 4 else x), (1 - y if r & 2 else y), (1 - c if r & 1 else c)

    def first(self, ins, outs, sems):
        x, y, c = _coords()
        me = 4 * x + 2 * y + c
        for w in range(len(ins)):
            pltpu.make_async_copy(ins[w], outs[w].at[me], sems[2].at[w]).start()
            for r in range(1, N_DEV):
                _rcopy(ins[w], outs[w].at[me], sems, w, r - 1, self._peer(x, y, c, r)).start()

    def last(self, ins, outs, sems):
        x, y, c = _coords()
        me = 4 * x + 2 * y + c
        for w in range(len(ins)):
            for r in range(1, N_DEV):
                px, py, pc = self._peer(x, y, c, r)
                cp = _rcopy(ins[w], outs[w].at[4 * px + 2 * py + pc], sems, w, r - 1, (px, py, pc))
                cp.wait_recv()
                cp.wait_send()
            pltpu.make_async_copy(ins[w], outs[w].at[me], sems[2].at[w]).wait()


class _PairScatterPlan:
    mid = None

    def __init__(self, pieces):
        self.ins = list(pieces)
        self.out_shapes = [_sds((N_CHIP,) + p.shape[1:], p.dtype) for p in pieces]
        n = len(pieces)
        self.scratch = [pltpu.SemaphoreType.DMA((n, N_CHIP)), pltpu.SemaphoreType.DMA((n, N_CHIP))]

    def _copies(self, ins, outs, sems):
        x, y, c = _coords()
        return [_rcopy(ins[w].at[2 * q + 1 - c], outs[w].at[q], sems, w, q, (x, y, 1 - c))
                for w in range(len(ins)) for q in range(N_CHIP)]

    def first(self, ins, outs, sems):
        for cp in self._copies(ins, outs, sems):
            cp.start()

    def last(self, ins, outs, sems):
        for cp in self._copies(ins, outs, sems):
            cp.wait_recv()
            cp.wait_send()


class _ChipScatterPlan:
    mid = None

    def __init__(self, sums, rows=None, into=None):
        self.n = len(sums)
        self.rows = rows
        self.ins = list(sums) + list(into or [])
        self.out_shapes = [_sds(s.shape, s.dtype) for s in sums]
        self.aliases = [(self.n + i, i) for i in range(len(into or []))]
        self.scratch = [pltpu.SemaphoreType.DMA((self.n, 3)), pltpu.SemaphoreType.DMA((self.n, 3))]

    def _copies(self, ins, outs, sems):
        x, y, c = _coords()
        cps = []
        for w in range(self.n):
            for r in (1, 2, 3):
                px, py = _chip_rel(x, y, r)
                src, dst = ins[w].at[r - 1], outs[w].at[r - 1]
                if self.rows is not None:
                    cut = pl.ds(self.rows[0], self.rows[1] - self.rows[0])
                    src, dst = src.at[cut], dst.at[cut]
                cps.append(_rcopy(src, dst, sems, w, r - 1, (px, py, c)))
        return cps

    def first(self, ins, outs, sems):
        for cp in self._copies(ins, outs, sems):
            cp.start()

    def last(self, ins, outs, sems):
        for cp in self._copies(ins, outs, sems):
            cp.wait_recv()
            cp.wait_send()


class _Comm:
    def __init__(self, plans):
        self.plans = list(plans)
        self.ins = [a for p in self.plans for a in p.ins]
        self.out_shapes = [s for p in self.plans for s in p.out_shapes]
        self.scratch = [s for p in self.plans for s in p.scratch]
        self.aliases = []
        i = o = 0
        for p in self.plans:
            self.aliases += [(i + a, o + b) for a, b in getattr(p, "aliases", [])]
            i, o = i + len(p.ins), o + len(p.out_shapes)

    def _parts(self, ins, outs, sems):
        i = o = s = 0
        for p in self.plans:
            yield p, ins[i:i + len(p.ins)], outs[o:o + len(p.out_shapes)], sems[s:s + len(p.scratch)]
            i, o, s = i + len(p.ins), o + len(p.out_shapes), s + len(p.scratch)

    def begin(self, step, nsteps, ins, outs, sems):
        @pl.when(step == 0)
        def _():
            for p, pi, po, ps in self._parts(ins, outs, sems):
                p.first(pi, po, ps)

        for p, pi, po, ps in self._parts(ins, outs, sems):
            if p.mid is not None:
                @pl.when(step == min(nsteps - 1, int(p.mid_frac * nsteps)))
                def _(p=p, pi=pi, po=po, ps=ps):
                    p.mid(pi, po, ps)

    def end(self, step, nsteps, ins, outs, sems):
        @pl.when(step == nsteps - 1)
        def _():
            for p, pi, po, ps in self._parts(ins, outs, sems):
                p.last(pi, po, ps)


def _call(body, args, *, name, grid, in_specs, out_specs, out_shape, scratch_shapes=(), sem=None, comm=None):
    in_specs, out_specs, out_shape, scratch_shapes = list(in_specs), list(out_specs), list(out_shape), list(scratch_shapes)
    if comm is None:
        res = pl.pallas_call(
            body, name=name, grid=grid, in_specs=in_specs, out_specs=out_specs, out_shape=out_shape,
            scratch_shapes=scratch_shapes,
            compiler_params=pltpu.CompilerParams(dimension_semantics=sem, vmem_limit_bytes=VMEM_LIMIT_BYTES),
        )(*args)
        return list(res), []
    n_in, n_out, n_sc = len(in_specs), len(out_specs), len(scratch_shapes)
    n_ci, n_co = len(comm.ins), len(comm.out_shapes)
    nsteps = math.prod(grid)

    def hosted(*refs):
        ins, cins = refs[:n_in], refs[n_in:n_in + n_ci]
        o0 = n_in + n_ci
        outs, couts = refs[o0:o0 + n_out], refs[o0 + n_out:o0 + n_out + n_co]
        s0 = o0 + n_out + n_co
        scr, csems = refs[s0:s0 + n_sc], refs[s0 + n_sc:]
        step = jnp.int32(0)
        for d in range(len(grid)):
            step = step * grid[d] + pl.program_id(d)
        comm.begin(step, nsteps, cins, couts, csems)
        body(*ins, *outs, *scr)
        comm.end(step, nsteps, cins, couts, csems)

    res = pl.pallas_call(
        hosted, name=name, grid=grid, in_specs=in_specs + [ANY] * n_ci, out_specs=out_specs + [ANY] * n_co,
        out_shape=out_shape + comm.out_shapes, scratch_shapes=scratch_shapes + comm.scratch,
        input_output_aliases={n_in + a: n_out + b for a, b in comm.aliases},
        compiler_params=pltpu.CompilerParams(dimension_semantics=("arbitrary",) * len(grid),
                                             vmem_limit_bytes=VMEM_LIMIT_BYTES, has_side_effects=True),
    )(*args, *comm.ins)
    return list(res[:n_out]), list(res[n_out:])


def _exchange_alone(comm, name):
    def body():
        pass

    return _call(body, [], name=name, grid=(), in_specs=[], out_specs=[], out_shape=[], comm=comm)[1]


def _matmul(a, b, *, mode, name, out_dtype=F32, out_blocks=None, tm=None, tn=None, tk=None, comm=None):
    tm = MM_TILE if tm is None else tm
    tn = MM_TILE if tn is None else tn
    a_blk = a.ndim == 3
    b_blk = b.ndim == 3
    if mode == "nn":
        M, K = a.shape
        N = b.shape[0] * b.shape[2] if b_blk else b.shape[1]
        dims = (((1,), (0,)), ((), ()))
    elif mode == "nt":
        M = a.shape[1] if a_blk else a.shape[0]
        K = a.shape[0] * a.shape[2] if a_blk else a.shape[1]
        N = b.shape[1] if b_blk else b.shape[0]
        dims = (((1,), (1,)), ((), ()))
    else:
        K, M = a.shape
        N = b.shape[0] * b.shape[2] if b_blk else b.shape[1]
        dims = (((0,), (0,)), ((), ()))

    tm = _tile(M, tm)
    tn = _tile(N, tn)
    if mode == "nt" and (a_blk or b_blk):
        tk = a.shape[2] if a_blk else b.shape[2]
    else:
        tk = _tile(K, K if tk is None else tk)
    if mode != "nt" and b_blk:
        tn = _tile(b.shape[2], tn)
    if out_blocks is not None:
        tn = _tile(out_blocks, tn)
    nk = K // tk
    grid = (M // tm, N // tn, nk)

    if mode == "nn":
        a_spec = pl.BlockSpec((tm, tk), lambda i, j, k: (i, k))
        if b_blk:
            rb = b.shape[2] // tn
            b_spec = pl.BlockSpec((None, tk, tn), lambda i, j, k: (j // rb, k, j % rb))
        else:
            b_spec = pl.BlockSpec((tk, tn), lambda i, j, k: (k, j))
    elif mode == "nt":
        if a_blk:
            a_spec = pl.BlockSpec((None, tm, tk), lambda i, j, k: (k, i, 0))
        else:
            a_spec = pl.BlockSpec((tm, tk), lambda i, j, k: (i, k))
        if b_blk:
            b_spec = pl.BlockSpec((None, tn, tk), lambda i, j, k: (k, j, 0))
        else:
            b_spec = pl.BlockSpec((tn, tk), lambda i, j, k: (j, k))
    else:
        a_spec = pl.BlockSpec((tk, tm), lambda i, j, k: (k, i))
        if b_blk:
            rb = b.shape[2] // tn
            b_spec = pl.BlockSpec((None, tk, tn), lambda i, j, k: (j // rb, k, j % rb))
        else:
            b_spec = pl.BlockSpec((tk, tn), lambda i, j, k: (k, j))

    if out_blocks is None:
        o_spec = pl.BlockSpec((tm, tn), lambda i, j, k: (i, j))
        o_shape = _sds((M, N), out_dtype)
    else:
        ro = out_blocks // tn
        o_spec = pl.BlockSpec((None, tm, tn), lambda i, j, k: (j // ro, i, j % ro))
        o_shape = _sds((N // out_blocks, M, out_blocks), out_dtype)

    direct = nk == 1 or out_dtype == F32

    def body(a_ref, b_ref, o_ref, *scratch):
        if nk == 1:
            o_ref[...] = lax.dot_general(a_ref[...], b_ref[...], dims, preferred_element_type=F32).astype(o_ref.dtype)
            return
        acc_ref = o_ref if direct else scratch[0]
        k = pl.program_id(2)

        @pl.when(k == 0)
        def _():
            acc_ref[...] = jnp.zeros(acc_ref.shape, F32)

        acc_ref[...] += lax.dot_general(a_ref[...], b_ref[...], dims, preferred_element_type=F32)
        if not direct:
            @pl.when(k == nk - 1)
            def _():
                o_ref[...] = acc_ref[...].astype(o_ref.dtype)

    scratch = [] if direct else [pltpu.VMEM((tm, tn), F32)]
    outs, landed = _call(body, [a, b], name=name, grid=grid, in_specs=[a_spec, b_spec], out_specs=[o_spec],
                         out_shape=[o_shape], scratch_shapes=scratch, sem=("parallel", "parallel", "arbitrary"), comm=comm)
    return outs[0] if comm is None else (outs[0], landed)


def _rms(x):
    return lax.rsqrt(jnp.mean(x * x, axis=-1, keepdims=True) + EPS)


def _rms_bwd(dy, x, g):
    r = _rms(x)
    n = x * r
    dn = dy * g
    dx = r * (dn - n * jnp.mean(dn * n, axis=-1, keepdims=True))
    return dx, dy * n


def _sigmoid(x):
    return 1.0 / (1.0 + jnp.exp(-x))


def _rope_rot(t):
    return pltpu.roll(t, HALF_ROPE, 1) - pltpu.roll(t, LANES - HALF_ROPE, 1)


def _lane(shape):
    return lax.broadcasted_iota(jnp.int32, shape, 1)


def _split3(x):
    hi = x.astype(BF16).astype(F32)
    r1 = x - hi
    mid = r1.astype(BF16).astype(F32)
    lo = (r1 - mid).astype(BF16).astype(F32)
    return hi, mid, lo


def _cumsum_rows(x, reverse):
    S = x.shape[0]
    bs = min(256, S)
    nb = S // bs
    r = lax.broadcasted_iota(jnp.int32, (bs, bs), 0)
    c = lax.broadcasted_iota(jnp.int32, (bs, bs), 1)
    tri = jnp.where((c >= r) if reverse else (c <= r), 1.0, 0.0).astype(BF16)
    edge = lax.broadcasted_iota(jnp.int32, (bs, x.shape[1]), 0) == (0 if reverse else bs - 1)
    carry = jnp.zeros((1, x.shape[1]), F32)
    outs = [None] * nb
    for bi in (range(nb - 1, -1, -1) if reverse else range(nb)):
        xb = x[bi * bs:(bi + 1) * bs, :]
        acc = carry
        for term in _split3(xb):
            acc = acc + jnp.dot(tri, term.astype(BF16), preferred_element_type=F32)
        outs[bi] = acc
        carry = jnp.sum(jnp.where(edge, acc, 0.0), axis=0, keepdims=True)
    return jnp.concatenate(outs, axis=0) if nb > 1 else outs[0]


def _gelu_parts(x):
    c0 = math.sqrt(2.0 / math.pi)
    inner = c0 * (x + 0.044715 * (x * x * x))
    t = jnp.tanh(inner)
    g = 0.5 * x * (1.0 + t)
    dg = 0.5 * (1.0 + t) + 0.5 * x * (1.0 - t * t) * (c0 * (1.0 + 3.0 * 0.044715 * (x * x)))
    return g, dg


def _accumulate(ref, value, first):
    @pl.when(first)
    def _():
        ref[...] = value

    @pl.when(jnp.logical_not(first))
    def _():
        ref[...] += value


def _cast_bf16(w, name):
    R, C = w.shape
    tr = _tile(R, 512, 16)

    def body(w_ref, o_ref):
        o_ref[...] = w_ref[...].astype(BF16)

    blk = pl.BlockSpec((tr, C), lambda i: (i, 0))
    return _call(body, [w], name=name, grid=(R // tr,), in_specs=[blk], out_specs=[blk],
                 out_shape=[_sds((R, C), BF16)], sem=("parallel",))[0][0]


def _concat_cols(parts, name):
    T = parts[0].shape[0]
    widths = [p.shape[1] for p in parts]
    tm = _tile(T, ROW_TILE, 16)

    def body(*refs):
        o_ref = refs[-1]
        off = 0
        for p_ref, w in zip(refs[:-1], widths):
            o_ref[:, off:off + w] = p_ref[...].astype(BF16)
            off += w

    return _call(body, parts, name=name, grid=(T // tm,),
                 in_specs=[pl.BlockSpec((tm, w), lambda i: (i, 0)) for w in widths],
                 out_specs=[pl.BlockSpec((tm, sum(widths)), lambda i: (i, 0))],
                 out_shape=[_sds((T, sum(widths)), BF16)], sem=("parallel",))[0][0]


def _prenorm(x, g, comm=None):
    T, D = x.shape
    tm = _tile(T, ROW_TILE, 16)

    def body(x_ref, g_ref, h_ref):
        xv = x_ref[...]
        h_ref[...] = (xv * _rms(xv) * g_ref[...]).astype(BF16)

    row = pl.BlockSpec((tm, D), lambda i: (i, 0))
    (h,), landed = _call(body, [x, g], name="prenorm", grid=(T // tm,),
                         in_specs=[row, pl.BlockSpec((1, D), lambda i: (0, 0))], out_specs=[row],
                         out_shape=[_sds((T, D), BF16)], sem=("parallel",), comm=comm)
    return h, landed


def _split_prep(proj, pos, invf, gq, gkv, bfor, lay, comm=None):
    T = proj.shape[0]
    tm = _tile(T, ROW_TILE, 16)

    def body(q_ref, kv_ref, kpe_ref, fl_ref, pos_ref, invf_ref, gq_ref, gkv_ref, bf_ref,
             qn_ref, kvn_ref, kper_ref, logf_ref, cos_ref, sin_ref):
        ql = q_ref[...]
        qn_ref[...] = (ql * _rms(ql) * gq_ref[...]).astype(BF16)
        kl = kv_ref[...]
        kvn_ref[...] = (kl * _rms(kl) * gkv_ref[...]).astype(BF16)
        ang = pos_ref[...].astype(F32) * invf_ref[...]
        valid = _lane(ang.shape) < ROPE
        cs = jnp.where(valid, jnp.cos(ang), 0.0)
        sn = jnp.where(valid, jnp.sin(ang), 0.0)
        cos_ref[...] = cs
        sin_ref[...] = sn
        kp = jnp.where(valid, kpe_ref[...], 0.0)
        kper_ref[...] = (kp * cs + _rope_rot(kp) * sn).astype(BF16)
        z = fl_ref[...] + bf_ref[...]
        logf_ref[...] = jnp.minimum(z, 0.0) - jnp.log(1.0 + jnp.exp(-jnp.abs(z)))

    def col(width, off):
        return pl.BlockSpec((tm, width), lambda i: (i, off // width))

    def vec(width):
        return pl.BlockSpec((1, width), lambda i: (0, 0))

    def out(width):
        return pl.BlockSpec((tm, width), lambda i: (i, 0))

    return _call(
        body, [proj, proj, proj, proj, pos, invf, gq, gkv, bfor], name="split_prep", grid=(T // tm,),
        in_specs=[col(Q_LORA, lay["q"]), col(KV_LORA, lay["kv"]), col(LANES, lay["kpe"]), col(LANES, lay["fl"]),
                  pl.BlockSpec((tm, 1), lambda i: (i, 0)), vec(LANES), vec(Q_LORA), vec(KV_LORA), vec(LANES)],
        out_specs=[out(Q_LORA), out(KV_LORA), out(LANES), out(LANES), out(LANES), out(LANES)],
        out_shape=[_sds((T, Q_LORA), BF16), _sds((T, KV_LORA), BF16), _sds((T, LANES), BF16),
                   _sds((T, LANES), F32), _sds((T, LANES), F32), _sds((T, LANES), F32)],
        sem=("parallel",), comm=comm)


def _mla_prep(qraw, kvraw, kper, cosT, sinT):
    H, T, _ = qraw.shape
    tm = _tile(T, HEAD_ROW_TILE, 16)

    def body(q_ref, kv_ref, kpe_ref, cos_ref, sin_ref, qo_ref, ko_ref, vo_ref):
        q = q_ref[...]
        pe = q[:, NOPE:]
        pe = jnp.where(_lane(pe.shape) < ROPE, pe, 0.0)
        qo_ref[:, :NOPE] = q[:, :NOPE].astype(BF16)
        qo_ref[:, NOPE:] = (pe * cos_ref[...] + _rope_rot(pe) * sin_ref[...]).astype(BF16)
        kv = kv_ref[...]
        ko_ref[:, :NOPE] = kv[:, :NOPE].astype(BF16)
        ko_ref[:, NOPE:] = kpe_ref[...]
        vo_ref[...] = kv[:, NOPE:].astype(BF16)

    head = pl.BlockSpec((None, tm, ATT_DK), lambda h, i: (h, i, 0))
    tok = pl.BlockSpec((tm, LANES), lambda h, i: (i, 0))
    return _call(
        body, [qraw, kvraw, kper, cosT, sinT], name="mla_prep", grid=(H, T // tm),
        in_specs=[head, head, tok, tok, tok],
        out_specs=[head, head, pl.BlockSpec((None, tm, VDIM), lambda h, i: (h, i, 0))],
        out_shape=[_sds((H, T, ATT_DK), BF16), _sds((H, T, ATT_DK), BF16), _sds((H, T, VDIM), BF16)],
        sem=("parallel", "parallel"))[0]


def _fox_cumsum(logf, B, S, inv_scale):
    T = logf.shape[0]

    def body(l_ref, c_ref):
        c_ref[...] = _cumsum_rows(l_ref[...], reverse=False) * inv_scale

    seq = pl.BlockSpec((S, LANES), lambda b: (b, 0))
    return _call(body, [logf], name="fox_cumsum", grid=(B,), in_specs=[seq], out_specs=[seq],
                 out_shape=[_sds((T, LANES), F32)], sem=("parallel",))[0][0]


def _fox_prep(proj, cs, lay):
    T = proj.shape[0]
    tm = _tile(T, HEAD_ROW_TILE, 16)

    def body(q_ref, k_ref, v_ref, cs_ref, qo_ref, ko_ref, vo_ref):
        h = pl.program_id(0)
        cv = cs_ref[...]
        lane = _lane(cv.shape)
        ccol = jnp.sum(jnp.where(lane == h, cv, 0.0), axis=1, keepdims=True)
        hi, mid, lo = _split3(ccol)
        one = jnp.where(lane < 6, 1.0, 0.0)
        augq = jnp.where(lane == 0, hi, jnp.where(lane == 1, mid, jnp.where(lane == 2, lo, one)))
        augk = jnp.where(lane < 3, 1.0, jnp.where(lane == 3, -hi, jnp.where(lane == 4, -mid, jnp.where(lane == 5, -lo, 0.0))))
        qo_ref[:, :FOX_DIM] = q_ref[...].astype(BF16)
        qo_ref[:, FOX_DIM:] = augq.astype(BF16)
        ko_ref[:, :FOX_DIM] = k_ref[...].astype(BF16)
        ko_ref[:, FOX_DIM:] = augk.astype(BF16)
        vo_ref[...] = v_ref[...].astype(BF16)

    def col(off):
        return pl.BlockSpec((tm, FOX_DIM), lambda h, i: (i, off // FOX_DIM + h))

    head = pl.BlockSpec((None, tm, ATT_DK), lambda h, i: (h, i, 0))
    return _call(
        body, [proj, proj, proj, cs], name="fox_prep", grid=(HEADS, T // tm),
        in_specs=[col(lay["fq"]), col(lay["fk"]), col(lay["fv"]), pl.BlockSpec((tm, LANES), lambda h, i: (i, 0))],
        out_specs=[head, head, pl.BlockSpec((None, tm, VDIM), lambda h, i: (h, i, 0))],
        out_shape=[_sds((HEADS, T, ATT_DK), BF16), _sds((HEADS, T, ATT_DK), BF16), _sds((HEADS, T, VDIM), BF16)],
        sem=("parallel", "parallel"))[0]


def _visible(tq, tk, unit):
    r = lax.broadcasted_iota(jnp.int32, (tq, tk), 0)
    c = lax.broadcasted_iota(jnp.int32, (tq, tk), 1)
    sh = int(math.log2(unit))
    return lax.shift_right_logical(c, sh) <= lax.shift_right_logical(r, sh)


def _attn_fwd(streams, *, B, S, name, comm=None):
    n = len(streams)
    H, T, DK = streams[0][0].shape
    DV = streams[0][2].shape[2]
    tq = _tile(S, ATT_TILE)
    nq = S // tq
    sub = min(ATT_SUB, tq)
    NT = (((1,), (1,)), ((), ()))

    def body(*refs):
        ins, outs, (m_sc, acc_sc) = refs[:3 * n], refs[3 * n:5 * n], refs[5 * n:]
        i, j = pl.program_id(1), pl.program_id(2)

        @pl.when(j == 0)
        def _():
            m_sc[...] = jnp.full(m_sc.shape, NEG_INF, F32)
            acc_sc[...] = jnp.zeros(acc_sc.shape, F32)

        def step(diagonal):
            work = [(t, r) for r in range(tq // sub) for t in range(n)]

            def scores(t, r):
                q_ref, k_ref, _ = ins[3 * t:3 * t + 3]
                kc = (r + 1) * sub if diagonal else tq
                s = lax.dot_general(q_ref[r * sub:(r + 1) * sub, :], k_ref[0:kc, :], NT, preferred_element_type=F32)
                return s * (streams[t][4] * LOG2_E)

            ahead = [scores(*work[w]) for w in range(min(ATT_AHEAD, len(work)))]
            for w, (t, r) in enumerate(work):
                s = ahead.pop(0)
                if w + ATT_AHEAD < len(work):
                    ahead.append(scores(*work[w + ATT_AHEAD]))
                v_ref = ins[3 * t + 2]
                kc = s.shape[1]
                rows = slice(r * sub, (r + 1) * sub)
                if diagonal:
                    own = jnp.where(_visible(sub, sub, streams[t][3]), s[:, kc - sub:], NEG_INF)
                    s = own if kc == sub else jnp.concatenate([s[:, :kc - sub], own], axis=1)
                m_prev = m_sc[t, rows, :]
                mx = s[:, 0:LANES]
                for g in range(1, kc // LANES):
                    mx = jnp.maximum(mx, s[:, g * LANES:(g + 1) * LANES])
                m_new = jnp.maximum(m_prev, jnp.max(mx, axis=1, keepdims=True))
                alpha = jnp.exp2(m_prev - m_new)
                p = jnp.exp2(s - jnp.tile(m_new, (1, kc // LANES))).astype(BF16)
                v_aug = jnp.concatenate([v_ref[0:kc, :], jnp.ones((kc, LANES), BF16)], axis=1)
                acc_sc[t, rows, :] = jnp.tile(alpha, (1, 2)) * acc_sc[t, rows, :] + jnp.dot(
                    p, v_aug, preferred_element_type=F32)
                m_sc[t, rows, :] = m_new

        @pl.when(j < i)
        def _():
            step(False)

        @pl.when(j == i)
        def _():
            step(True)
            for t in range(n):
                o_ref, lse_ref = outs[2 * t:2 * t + 2]
                l = acc_sc[t, :, DV:]
                o_ref[...] = (acc_sc[t, :, :DV] / l).astype(BF16)
                lse_ref[...] = m_sc[t] + jnp.log2(l)

    def qmap(g, i, j):
        return (g % H, (g // H) * nq + i, 0)

    def kmap(g, i, j):
        return (g % H, (g // H) * nq + jnp.minimum(j, i), 0)

    args = [a for st in streams for a in st[:3]]
    outs, landed = _call(
        body, args, name=name, grid=(B * H, nq, nq),
        in_specs=[pl.BlockSpec((None, tq, DK), qmap), pl.BlockSpec((None, tq, DK), kmap),
                  pl.BlockSpec((None, tq, DV), kmap)] * n,
        out_specs=[pl.BlockSpec((tq, DV), lambda g, i, j: ((g // H) * nq + i, g % H)),
                   pl.BlockSpec((None, tq, LANES), qmap)] * n,
        out_shape=[_sds((T, H * DV), BF16), _sds((H, T, LANES), F32)] * n,
        scratch_shapes=[pltpu.VMEM((n, tq, LANES), F32), pltpu.VMEM((n, tq, DV + LANES), F32)],
        sem=("parallel", "parallel", "arbitrary"), comm=comm)
    return [(outs[2 * t], outs[2 * t + 1]) for t in range(n)], landed


def _attn_bwd(streams, *, B, S, name, comm=None):
    n = len(streams)
    H, T, DK = streams[0][0].shape
    DV = streams[0][2].shape[2]
    tq = _tile(S, ATT_TILE)
    nq = S // tq
    sub = min(ATT_SUB, tq)
    NT = (((1,), (1,)), ((), ()))
    TN = (((0,), (0,)), ((), ()))

    def body(*refs):
        ins, outs = refs[:6 * n], refs[6 * n:]
        j, i = pl.program_id(1), pl.program_id(2)

        @pl.when(jnp.logical_and(j == 0, i == 0))
        def _():
            for t in range(n):
                outs[3 * t][...] = jnp.zeros(outs[3 * t].shape, F32)

        @pl.when(i == 0)
        def _():
            for t in range(n):
                outs[3 * t + 1][...] = jnp.zeros(outs[3 * t + 1].shape, F32)
                outs[3 * t + 2][...] = jnp.zeros(outs[3 * t + 2].shape, F32)

        def step(diagonal):
            work = [(t, r) for r in range(tq // sub) for t in range(n)]

            def kcols(r):
                return (r + 1) * sub if diagonal else tq

            def scores(t, r):
                q_ref, k_ref, v_ref, _, do_ref, _ = ins[6 * t:6 * t + 6]
                rows, kc = slice(r * sub, (r + 1) * sub), kcols(r)
                s = lax.dot_general(q_ref[rows, :], k_ref[0:kc, :], NT, preferred_element_type=F32)
                dp = lax.dot_general(do_ref[rows, :], v_ref[0:kc, :], NT, preferred_element_type=F32)
                return s * (streams[t][7] * LOG2_E), dp

            def probs(t, r, s, dp):
                _, _, _, o_ref, do_ref, lse_ref = ins[6 * t:6 * t + 6]
                rows, kc = slice(r * sub, (r + 1) * sub), kcols(r)
                if diagonal:
                    own = jnp.where(_visible(sub, sub, streams[t][6]), s[:, kc - sub:], NEG_INF)
                    s = own if kc == sub else jnp.concatenate([s[:, :kc - sub], own], axis=1)
                p = jnp.exp2(s - jnp.tile(lse_ref[rows, :], (1, kc // LANES)))
                delta = jnp.sum(do_ref[rows, :].astype(F32) * o_ref[rows, :].astype(F32), axis=1, keepdims=True)
                return p.astype(BF16), (p * (dp - delta) * streams[t][7]).astype(BF16)

            def grads(t, r, p, ds):
                q_ref, k_ref, _, _, do_ref, _ = ins[6 * t:6 * t + 6]
                dq_ref, dk_ref, dv_ref = outs[3 * t:3 * t + 3]
                rows, kc = slice(r * sub, (r + 1) * sub), kcols(r)
                dv_ref[0:kc, :] += lax.dot_general(p, do_ref[rows, :], TN, preferred_element_type=F32)
                dk_ref[0:kc, :] += lax.dot_general(ds, q_ref[rows, :], TN, preferred_element_type=F32)
                qrows = pl.ds(pl.multiple_of(i * tq + r * sub, sub), sub)
                dq_ref[qrows, :] += jnp.dot(ds, k_ref[0:kc, :], preferred_element_type=F32)

            nw = len(work)
            sc = {w: scores(*work[w]) for w in range(min(2, nw))}
            pr = {0: probs(*work[0], *sc.pop(0))}
            for w in range(nw):
                if w + 2 < nw:
                    sc[w + 2] = scores(*work[w + 2])
                if w + 1 < nw:
                    pr[w + 1] = probs(*work[w + 1], *sc.pop(w + 1))
                grads(*work[w], *pr.pop(w))

        @pl.when(i > j)
        def _():
            step(False)

        @pl.when(i == j)
        def _():
            step(True)

    def qmap(g, j, i):
        return (g % H, (g // H) * nq + jnp.maximum(i, j), 0)

    def kmap(g, j, i):
        return (g % H, (g // H) * nq + j, 0)

    def omap(g, j, i):
        return ((g // H) * nq + jnp.maximum(i, j), g % H)

    args = [a for st in streams for a in st[:6]]
    outs, landed = _call(
        body, args, name=name, grid=(B * H, nq, nq),
        in_specs=[pl.BlockSpec((None, tq, DK), qmap), pl.BlockSpec((None, tq, DK), kmap),
                  pl.BlockSpec((None, tq, DV), kmap), pl.BlockSpec((tq, DV), omap), pl.BlockSpec((tq, DV), omap),
                  pl.BlockSpec((None, tq, LANES), qmap)] * n,
        out_specs=[pl.BlockSpec((None, S, DK), lambda g, j, i: (g % H, g // H, 0)),
                   pl.BlockSpec((None, tq, DK), kmap), pl.BlockSpec((None, tq, DV), kmap)] * n,
        out_shape=[_sds((H, T, DK), F32), _sds((H, T, DK), F32), _sds((H, T, DV), F32)] * n,
        sem=("parallel", "arbitrary", "arbitrary"), comm=comm)
    return [tuple(outs[3 * t:3 * t + 3]) for t in range(n)], landed


def _gate_merge(am, af, proj, bgate, lay, D, comm=None):
    T = am.shape[0]
    tm = _tile(T, ROW_TILE, 16)
    tn = _tile(D, 1024)

    def body(am_ref, af_ref, gm_ref, gf_ref, bm_ref, bf_ref, o_ref):
        sm = _sigmoid(gm_ref[...] + bm_ref[...])
        sf = _sigmoid(gf_ref[...] + bf_ref[...])
        o_ref[...] = (sm * am_ref[...] + sf * af_ref[...]).astype(BF16)

    og = lay["g"] // tn
    blk = pl.BlockSpec((tm, tn), lambda i, j: (i, j))
    return _call(
        body, [am, af, proj, proj, bgate, bgate], name="gate_merge", grid=(T // tm, D // tn),
        in_specs=[blk, blk, pl.BlockSpec((tm, tn), lambda i, j: (i, og + j)),
                  pl.BlockSpec((tm, tn), lambda i, j: (i, og + D // tn + j)),
                  pl.BlockSpec((1, tn), lambda i, j: (0, j)), pl.BlockSpec((1, tn), lambda i, j: (0, D // tn + j))],
        out_specs=[blk], out_shape=[_sds((T, D), BF16)], sem=("parallel", "parallel"), comm=comm)


def _mid(x, y1, g_pm, g_ffn):
    T, D = x.shape
    tm = _tile(T, ROW_TILE, 16)

    def body(x_ref, y_ref, gp_ref, gf_ref, x1_ref, h2_ref):
        y = y_ref[...]
        x1 = x_ref[...] + y * _rms(y) * gp_ref[...]
        x1_ref[...] = x1
        h2_ref[...] = (x1 * _rms(x1) * gf_ref[...]).astype(BF16)

    row = pl.BlockSpec((tm, D), lambda i: (i, 0))
    vec = pl.BlockSpec((1, D), lambda i: (0, 0))
    return _call(body, [x, y1, g_pm, g_ffn], name="mid", grid=(T // tm,), in_specs=[row, row, vec, vec],
                 out_specs=[row, row], out_shape=[_sds((T, D), F32), _sds((T, D), BF16)], sem=("parallel",))[0]


def _conv3(u, w_ref, bias):
    row = lax.broadcasted_iota(jnp.int32, u.shape, 0)
    u1 = jnp.where(row >= 1, pltpu.roll(u, 1, 0), 0.0)
    u2 = jnp.where(row >= 2, pltpu.roll(u, 2, 0), 0.0)
    return w_ref[0:1, :] * u2 + w_ref[1:2, :] * u1 + w_ref[2:3, :] * u + bias, u1, u2


def _convffn_fwd(u, cw, cb, B, S, F, comm=None):
    T = u.shape[0]
    tn = _tile(F, 256)
    nf = F // tn

    def body(ug_ref, uv_ref, wg_ref, wv_ref, bg_ref, bv_ref, a_ref):
        g, _, _ = _conv3(ug_ref[...], wg_ref, bg_ref[...])
        val, _, _ = _conv3(uv_ref[...], wv_ref, bv_ref[...])
        a_ref[...] = (_gelu_parts(g)[0] * val).astype(BF16)

    def seq(off):
        return pl.BlockSpec((S, tn), lambda b, j: (b, off + j))

    def par(rows, off):
        return pl.BlockSpec((rows, tn), lambda b, j: (0, off + j))

    return _call(body, [u, u, cw, cw, cb, cb], name="convffn_fwd", grid=(B, nf),
                 in_specs=[seq(0), seq(nf), par(3, 0), par(3, nf), par(1, 0), par(1, nf)],
                 out_specs=[seq(0)], out_shape=[_sds((T, F), BF16)], sem=("parallel", "parallel"), comm=comm)


def _convffn_bwd(u, dact, cw, cb, B, S, F, comm=None):
    T = u.shape[0]
    tn = _tile(F, 256)
    nf = F // tn

    def body(ug_ref, uv_ref, da_ref, wg_ref, wv_ref, bg_ref, bv_ref, dug_ref, duv_ref, dpg_ref, dpv_ref):
        b = pl.program_id(1)
        ug, uv, da = ug_ref[...], uv_ref[...], da_ref[...]
        g, ug1, ug2 = _conv3(ug, wg_ref, bg_ref[...])
        val, uv1, uv2 = _conv3(uv, wv_ref, bv_ref[...])
        gel, dgel = _gelu_parts(g)
        dg = da * val * dgel
        dval = da * gel
        row = lax.broadcasted_iota(jnp.int32, ug.shape, 0)

        def back(d, w_ref):
            d1 = jnp.where(row < S - 1, pltpu.roll(d, S - 1, 0), 0.0)
            d2 = jnp.where(row < S - 2, pltpu.roll(d, S - 2, 0), 0.0)
            return w_ref[2:3, :] * d + w_ref[1:2, :] * d1 + w_ref[0:1, :] * d2

        dug_ref[...] = back(dg, wg_ref).astype(BF16)
        duv_ref[...] = back(dval, wv_ref).astype(BF16)

        def sums(d, u0, u1, u2):
            r8 = lax.broadcasted_iota(jnp.int32, (8, d.shape[1]), 0)
            out = jnp.zeros((8, d.shape[1]), F32)
            for k, t in enumerate((d * u2, d * u1, d * u0, d)):
                out = jnp.where(r8 == k, jnp.sum(t, axis=0, keepdims=True), out)
            return out

        _accumulate(dpg_ref, sums(dg, ug, ug1, ug2), b == 0)
        _accumulate(dpv_ref, sums(dval, uv, uv1, uv2), b == 0)

    def seq(off):
        return pl.BlockSpec((S, tn), lambda j, b: (b, off + j))

    def par(rows, off):
        return pl.BlockSpec((rows, tn), lambda j, b: (0, off + j))

    outs, landed = _call(
        body, [u, u, dact, cw, cw, cb, cb], name="convffn_bwd", grid=(nf, B),
        in_specs=[seq(0), seq(nf), seq(0), par(3, 0), par(3, nf), par(1, 0), par(1, nf)],
        out_specs=[seq(0), seq(0), par(8, 0), par(8, 0)],
        out_shape=[_sds((T, F), BF16), _sds((T, F), BF16), _sds((8, F), F32), _sds((8, F), F32)],
        sem=("parallel", "arbitrary"), comm=comm)
    return outs, landed


def _tail(ff, x1, tgt, g):
    T, D = ff.shape
    tm = _tile(T, ROW_TILE, 16)

    def body(ff_ref, x1_ref, t_ref, g_ref, dy_ref, dff_ref, loss_ref, dg_ref):
        i = pl.program_id(0)
        f = ff_ref[...]
        gv = g_ref[...]
        r = _rms(f)
        n = f * r
        e = (x1_ref[...] + n * gv) - t_ref[...]
        dy = e * (1.0 / D)
        dy_ref[...] = dy
        dn = dy * gv
        dff_ref[...] = (r * (dn - n * jnp.mean(dn * n, axis=-1, keepdims=True))).astype(BF16)
        part = 0.5 * jnp.sum(jnp.mean(e * e, axis=-1, keepdims=True), axis=0, keepdims=True)
        _accumulate(loss_ref, jnp.broadcast_to(part, loss_ref.shape), i == 0)
        _accumulate(dg_ref, jnp.sum(dy * n, axis=0, keepdims=True), i == 0)

    row = pl.BlockSpec((tm, D), lambda i: (i, 0))
    vec = pl.BlockSpec((1, D), lambda i: (0, 0))
    return _call(body, [ff, x1, tgt, g], name="tail", grid=(T // tm,), in_specs=[row, row, row, vec],
                 out_specs=[row, row, pl.BlockSpec((8, LANES), lambda i: (0, 0)), vec],
                 out_shape=[_sds((T, D), F32), _sds((T, D), BF16), _sds((8, LANES), F32), _sds((1, D), F32)],
                 sem=("arbitrary",))[0]


def _mid_bwd(dy, dh2, x1, y1, g_ffn, g_pm, comm=None):
    T, D = dy.shape
    tm = _tile(T, ROW_TILE, 16)

    def body(dy_ref, dh_ref, x1_ref, y1_ref, gf_ref, gp_ref, dx1_ref, dy1_ref, dgf_ref, dgp_ref):
        i = pl.program_id(0)
        dh = dh_ref[...]
        d2, dgf = _rms_bwd(dh, x1_ref[...], gf_ref[...])
        dx1 = dy_ref[...] + d2
        dx1_ref[...] = dx1
        d1, dgp = _rms_bwd(dx1, y1_ref[...], gp_ref[...])
        dy1_ref[...] = d1.astype(BF16)
        _accumulate(dgf_ref, jnp.sum(dgf, axis=0, keepdims=True), i == 0)
        _accumulate(dgp_ref, jnp.sum(dgp, axis=0, keepdims=True), i == 0)

    row = pl.BlockSpec((tm, D), lambda i: (i, 0))
    vec = pl.BlockSpec((1, D), lambda i: (0, 0))
    return _call(body, [dy, dh2, x1, y1, g_ffn, g_pm], name="mid_bwd", grid=(T // tm,),
                 in_specs=[row, row, row, row, vec, vec], out_specs=[row, row, vec, vec],
                 out_shape=[_sds((T, D), F32), _sds((T, D), BF16), _sds((1, D), F32), _sds((1, D), F32)],
                 sem=("arbitrary",), comm=comm)


def _gate_bwd(dm, am, af, proj, bgate, lay, D, comm=None):
    T = dm.shape[0]
    tm = _tile(T, ROW_TILE, 16)
    tn = _tile(D, 512)

    def body(dm_ref, am_ref, af_ref, gm_ref, gf_ref, bm_ref, bf_ref,
             dam_ref, daf_ref, dgm_ref, dgf_ref, dbm_ref, dbf_ref):
        i = pl.program_id(1)
        d = dm_ref[...]
        sm = _sigmoid(gm_ref[...] + bm_ref[...])
        sf = _sigmoid(gf_ref[...] + bf_ref[...])
        dam_ref[...] = (d * sm).astype(BF16)
        daf_ref[...] = (d * sf).astype(BF16)
        dgm = d * am_ref[...] * (sm * (1.0 - sm))
        dgf = d * af_ref[...] * (sf * (1.0 - sf))
        dgm_ref[...] = dgm.astype(BF16)
        dgf_ref[...] = dgf.astype(BF16)
        _accumulate(dbm_ref, jnp.sum(dgm, axis=0, keepdims=True), i == 0)
        _accumulate(dbf_ref, jnp.sum(dgf, axis=0, keepdims=True), i == 0)

    og = lay["g"] // tn
    blk = pl.BlockSpec((tm, tn), lambda j, i: (i, j))
    vec = pl.BlockSpec((1, tn), lambda j, i: (0, j))
    return _call(
        body, [dm, am, af, proj, proj, bgate, bgate], name="gate_bwd", grid=(D // tn, T // tm),
        in_specs=[blk, blk, blk, pl.BlockSpec((tm, tn), lambda j, i: (i, og + j)),
                  pl.BlockSpec((tm, tn), lambda j, i: (i, og + D // tn + j)),
                  vec, pl.BlockSpec((1, tn), lambda j, i: (0, D // tn + j))],
        out_specs=[blk, blk, blk, blk, vec, vec],
        out_shape=[_sds((T, D), BF16)] * 4 + [_sds((1, D), F32)] * 2, sem=("parallel", "arbitrary"), comm=comm)


def _mla_bwd_prep(dq, dk, dv, cosT, sinT, comm=None):
    H, T, _ = dq.shape
    tm = _tile(T, HEAD_ROW_TILE, 16)

    def body(dq_ref, dk_ref, dv_ref, cos_ref, sin_ref, dqr_ref, dkv_ref, dkpe_ref):
        h = pl.program_id(1)
        cs, sn = cos_ref[...], sin_ref[...]
        valid = _lane(cs.shape) < ROPE

        def unrope(d):
            d = jnp.where(valid, d, 0.0)
            return d * cs - _rope_rot(d) * sn

        dqv = dq_ref[...]
        dqr_ref[:, :NOPE] = dqv[:, :NOPE].astype(BF16)
        dqr_ref[:, NOPE:] = unrope(dqv[:, NOPE:]).astype(BF16)
        dkv_ = dk_ref[...]
        dkv_ref[:, :NOPE] = dkv_[:, :NOPE].astype(BF16)
        dkv_ref[:, NOPE:] = dv_ref[...].astype(BF16)
        _accumulate(dkpe_ref, unrope(dkv_[:, NOPE:]), h == 0)

    head = pl.BlockSpec((None, tm, ATT_DK), lambda i, h: (h, i, 0))
    tok = pl.BlockSpec((tm, LANES), lambda i, h: (i, 0))
    return _call(
        body, [dq, dk, dv, cosT, sinT], name="mla_bwd_prep", grid=(T // tm, H),
        in_specs=[head, head, pl.BlockSpec((None, tm, VDIM), lambda i, h: (h, i, 0)), tok, tok],
        out_specs=[head, head, tok],
        out_shape=[_sds((H, T, ATT_DK), BF16), _sds((H, T, ATT_DK), BF16), _sds((T, LANES), F32)],
        sem=("parallel", "arbitrary"), comm=comm)


def _fox_bwd_prep(dq, dk, proj, bfor, lay, B, S, inv_scale):
    H, T, _ = dq.shape

    def body(dq_ref, dk_ref, fl_ref, bf_ref, dfl_ref, dbf_ref, dc_sc):
        b, h = pl.program_id(0), pl.program_id(1)
        lane = _lane(dc_sc.shape)
        col = jnp.sum(jnp.where(lane == 0, dq_ref[...], 0.0) - jnp.where(lane == 3, dk_ref[...], 0.0),
                      axis=1, keepdims=True)

        @pl.when(h == 0)
        def _():
            dc_sc[...] = jnp.zeros(dc_sc.shape, F32)

        dc_sc[...] = jnp.where(lane == h, col, dc_sc[...])

        @pl.when(h == H - 1)
        def _():
            dlogf = _cumsum_rows(dc_sc[...] * inv_scale, reverse=True)
            z = fl_ref[...] + bf_ref[...]
            dz = jnp.where(lane < H, dlogf * (1.0 / (1.0 + jnp.exp(z))), 0.0)
            dfl_ref[...] = dz
            _accumulate(dbf_ref, jnp.sum(dz, axis=0, keepdims=True), b == 0)

    aug = pl.BlockSpec((None, S, LANES), lambda b, h: (h, b, 1))
    seq = pl.BlockSpec((S, LANES), lambda b, h: (b, 0))
    vec = pl.BlockSpec((1, LANES), lambda b, h: (0, 0))
    return _call(
        body, [dq, dk, proj, bfor], name="fox_bwd_prep", grid=(B, H),
        in_specs=[aug, aug, pl.BlockSpec((S, LANES), lambda b, h: (b, lay["fl"] // LANES)), vec],
        out_specs=[seq, vec], out_shape=[_sds((T, LANES), F32), _sds((1, LANES), F32)],
        scratch_shapes=[pltpu.VMEM((S, LANES), F32)], sem=("arbitrary", "arbitrary"))[0]


def _heads_to_cols(dq, dk, dv):
    H, T, _ = dq.shape
    tm = _tile(T, HEAD_ROW_TILE, 16)

    def body(a_ref, b_ref, c_ref, ao_ref, bo_ref, co_ref):
        ao_ref[...] = a_ref[...].astype(BF16)
        bo_ref[...] = b_ref[...].astype(BF16)
        co_ref[...] = c_ref[...].astype(BF16)

    src = pl.BlockSpec((None, tm, FOX_DIM), lambda i, h: (h, i, 0))
    dst = pl.BlockSpec((tm, FOX_DIM), lambda i, h: (i, h))
    return _call(body, [dq, dk, dv], name="heads_to_cols", grid=(T // tm, H), in_specs=[src, src, src],
                 out_specs=[dst, dst, dst], out_shape=[_sds((T, H * FOX_DIM), BF16)] * 3,
                 sem=("parallel", "parallel"))[0]


def _lat_bwd(dqn, dkvn, proj, gq, gkv, lay):
    T = dqn.shape[0]
    tm = _tile(T, ROW_TILE, 16)

    def body(dq_ref, dkv_ref, q_ref, kv_ref, gq_ref, gkv_ref, dql_ref, dkl_ref, dgq_ref, dgkv_ref):
        i = pl.program_id(0)
        dql, dgq = _rms_bwd(dq_ref[...], q_ref[...], gq_ref[...])
        dkl, dgkv = _rms_bwd(dkv_ref[...], kv_ref[...], gkv_ref[...])
        dql_ref[...] = dql.astype(BF16)
        dkl_ref[...] = dkl.astype(BF16)
        _accumulate(dgq_ref, jnp.sum(dgq, axis=0, keepdims=True), i == 0)
        _accumulate(dgkv_ref, jnp.sum(dgkv, axis=0, keepdims=True), i == 0)

    def blk(width, off=0):
        return pl.BlockSpec((tm, width), lambda i: (i, off // width))

    def vec(width):
        return pl.BlockSpec((1, width), lambda i: (0, 0))

    return _call(
        body, [dqn, dkvn, proj, proj, gq, gkv], name="lat_bwd", grid=(T // tm,),
        in_specs=[blk(Q_LORA), blk(KV_LORA), blk(Q_LORA, lay["q"]), blk(KV_LORA, lay["kv"]), vec(Q_LORA), vec(KV_LORA)],
        out_specs=[blk(Q_LORA), blk(KV_LORA), vec(Q_LORA), vec(KV_LORA)],
        out_shape=[_sds((T, Q_LORA), BF16), _sds((T, KV_LORA), BF16), _sds((1, Q_LORA), F32), _sds((1, KV_LORA), F32)],
        sem=("arbitrary",))[0]


def _final_dx(dx1, dh, x, g, comm=None):
    T, D = x.shape
    tm = _tile(T, ROW_TILE, 16)

    def body(dx1_ref, dh_ref, x_ref, g_ref, dx_ref, dg_ref):
        i = pl.program_id(0)
        d, dg = _rms_bwd(dh_ref[...], x_ref[...], g_ref[...])
        dx_ref[...] = dx1_ref[...] + d
        _accumulate(dg_ref, jnp.sum(dg, axis=0, keepdims=True), i == 0)

    row = pl.BlockSpec((tm, D), lambda i: (i, 0))
    vec = pl.BlockSpec((1, D), lambda i: (0, 0))
    return _call(body, [dx1, dh, x, g], name="final_dx", grid=(T // tm,), in_specs=[row, row, row, vec],
                 out_specs=[row, vec], out_shape=[_sds((T, D), F32), _sds((1, D), F32)], sem=("arbitrary",), comm=comm)


def _chip_sum(pieces, paired, qc, name):
    G, R, C = pieces.shape
    tr = _tile(R, 256, 16)

    def body(qc_ref, g_ref, p_ref, keep_ref, send_ref):
        s = pl.program_id(1)
        tot = g_ref[...] + p_ref[...]

        @pl.when(s == 0)
        def _():
            keep_ref[...] = tot

        @pl.when(s > 0)
        def _():
            send_ref[...] = tot.astype(send_ref.dtype)

    grid_spec = pltpu.PrefetchScalarGridSpec(
        num_scalar_prefetch=1, grid=(R // tr, N_CHIP),
        in_specs=[pl.BlockSpec((None, tr, C), lambda i, s, qc: (2 * (qc[0] ^ s) + qc[1], i, 0)),
                  pl.BlockSpec((None, tr, C), lambda i, s, qc: (qc[0] ^ s, i, 0))],
        out_specs=[pl.BlockSpec((tr, C), lambda i, s, qc: (i, 0)),
                   pl.BlockSpec((None, tr, C), lambda i, s, qc: (jnp.maximum(s - 1, 0), i, 0))])
    send_dtype = BF16 if R >= 16 else pieces.dtype
    return pl.pallas_call(
        body, name=name, grid_spec=grid_spec,
        out_shape=[_sds((R, C), F32), _sds((3, R, C), send_dtype)],
        compiler_params=pltpu.CompilerParams(dimension_semantics=("arbitrary", "arbitrary"),
                                             vmem_limit_bytes=VMEM_LIMIT_BYTES),
    )(qc, pieces, paired)


def _adamw_math(w, g, m, v):
    m = ADAM_B1 * m + (1.0 - ADAM_B1) * g
    v = ADAM_B2 * v + (1.0 - ADAM_B2) * (g * g)
    m_hat = m / (1.0 - ADAM_B1 ** ADAM_STEP)
    v_hat = v / (1.0 - ADAM_B2 ** ADAM_STEP)
    delta = -ADAM_LR * (m_hat / (jnp.sqrt(v_hat) + ADAM_EPS) + ADAM_WD * w)
    return delta, m, v


def _sum_adamw(keep, pieces, w, m, v, name):
    R, C = w.shape
    P = pieces.shape[0]
    tr = _tile(R, 256, 16)

    def body(*refs):
        if keep is None:
            p_ref, w_ref, m_ref, v_ref, g_ref, d_ref, mo_ref, vo_ref = refs
            g = p_ref[0].astype(F32)
            rest = range(1, P)
        else:
            k_ref, p_ref, w_ref, m_ref, v_ref, g_ref, d_ref, mo_ref, vo_ref = refs
            g = k_ref[...]
            rest = range(P)
        for q in rest:
            g = g + p_ref[q].astype(F32)
        g_ref[...] = g
        d_ref[...], mo_ref[...], vo_ref[...] = _adamw_math(w_ref[...], g, m_ref[...], v_ref[...])

    blk = pl.BlockSpec((tr, C), lambda i: (i, 0))
    pblk = pl.BlockSpec((P, tr, C), lambda i: (0, i, 0))
    args = [pieces, w, m, v] if keep is None else [keep, pieces, w, m, v]
    specs = [pblk, blk, blk, blk] if keep is None else [blk, pblk, blk, blk, blk]
    return _call(body, args, name=name, grid=(R // tr,), in_specs=specs, out_specs=[blk] * 4,
                 out_shape=[_sds((R, C), F32)] * 4, sem=("parallel",))[0]


def _layout(D):
    lay = {"q": 0, "kv": Q_LORA, "kpe": Q_LORA + KV_LORA}
    lay["fq"] = lay["kpe"] + LANES
    lay["fk"] = lay["fq"] + HEADS * FOX_DIM
    lay["fv"] = lay["fk"] + HEADS * FOX_DIM
    lay["fl"] = lay["fv"] + HEADS * FOX_DIM
    lay["g"] = lay["fl"] + LANES
    lay["end"] = lay["g"] + 2 * D
    return lay


def kernel(x, positions, pre_mix_norm, w_in, q_a_norm, w_uq, kv_a_norm, w_ukv, b_forget, b_gate, w_branch_mla, w_branch_fox, w_out, post_mix_norm, pre_ffn_norm, w_up, conv_w, conv_b, w_down, post_ffn_norm, loss_target, m_pre_mix_norm, m_w_in, m_q_a_norm, m_w_uq, m_kv_a_norm, m_w_ukv, m_b_forget, m_b_gate, m_w_branch_mla, m_w_branch_fox, m_w_out, m_post_mix_norm, m_pre_ffn_norm, m_w_up, m_conv_w, m_conv_b, m_w_down, m_post_ffn_norm, v_pre_mix_norm, v_w_in, v_q_a_norm, v_w_uq, v_kv_a_norm, v_w_ukv, v_b_forget, v_b_gate, v_w_branch_mla, v_w_branch_fox, v_w_out, v_post_mix_norm, v_pre_ffn_norm, v_w_up, v_conv_w, v_conv_b, v_w_down, v_post_ffn_norm):
    B, S, D = x.shape
    T = B * S
    F = conv_b.shape[0] // 2
    lay = _layout(D)
    n_in = w_in.shape[1]
    d_in = N_DEV * n_in
    seg_a = Q_LORA + KV_LORA + ROPE
    seg_b = 3 * HEADS * FOX_DIM + HEADS
    mla_scale = (NOPE + ROPE) ** -0.5
    fox_scale = FOX_DIM ** -0.5
    ax, ay, ac = (lax.axis_index(a) for a in MESH_AXES)
    qc = jnp.stack([2 * ax + ay, ac]).astype(jnp.int32)

    def row(vec, width=None):
        vec = vec.reshape(1, -1)
        if width is not None and vec.shape[1] < width:
            vec = jnp.pad(vec, ((0, 0), (0, width - vec.shape[1])))
        return vec

    x2 = x.reshape(T, D)
    win_s = _cast_bf16(w_in, "cast_w_in")
    h, (win_g,) = _prenorm(x2, row(pre_mix_norm), comm=_Comm([_GatherPlan([win_s], mid_frac=0.3)]))
    small_s = [_cast_bf16(w, "cast_" + n) for w, n in
               [(w_uq, "w_uq"), (w_ukv, "w_ukv"), (w_branch_mla, "w_branch_mla"), (w_branch_fox, "w_branch_fox"), (w_out, "w_out")]]
    wup_s = _cast_bf16(w_up, "cast_w_up")
    wdown_s = _cast_bf16(w_down, "cast_w_down")

    def shard_cols(lo, hi):
        out = []
        for g in range(lo // n_in, (hi - 1) // n_in + 1):
            out.append(win_g[g][:, max(lo, g * n_in) - g * n_in:min(hi, (g + 1) * n_in) - g * n_in])
        return out

    w_perm = jnp.concatenate(
        shard_cols(0, seg_a) + [jnp.zeros((D, LANES - ROPE), BF16)] + shard_cols(seg_a, seg_a + seg_b)
        + [jnp.zeros((D, LANES - HEADS), BF16)] + shard_cols(seg_a + seg_b, d_in), axis=1)

    tgt = loss_target.reshape(T, D)
    pos = positions.reshape(T, 1)
    inv_freq = 1.0 / (ROPE_THETA ** (jnp.arange(0, ROPE, 2, dtype=F32) / ROPE))
    invf = row(jnp.concatenate([inv_freq, inv_freq]), LANES)
    g_pre, g_q, g_kv = row(pre_mix_norm), row(q_a_norm), row(kv_a_norm)
    g_pm, g_ffn, g_pf = row(post_mix_norm), row(pre_ffn_norm), row(post_ffn_norm)
    bfor = row(b_forget, LANES)
    bgate = row(b_gate)
    cb_full = row(conv_b)

    def own_plan(blocks):
        return _Comm([_GatherOwnPlan(blocks)])

    def pass_plan(gathered):
        return _Comm([_GatherPassPlan(gathered)])

    def pair_plan(gs):
        return _Comm([_PairScatterPlan(gs)])

    def chip_plan(gs):
        return _Comm([_ChipScatterPlan(gs)])

    half_d = D // 2
    proj, landed = _matmul(h, w_perm, mode="nn", name="mm_proj", comm=_Comm(
        [_GatherOwnPlan(small_s + [conv_w]), _GatherOwnPlan([wup_s], rows=(0, half_d))]))
    small_g, wup_part = landed[:-1], landed[-1:]
    (qn, kvn, kper, logf, cosT, sinT), (wuq_g, wukv_g, wbm_g, wbf_g, wout_g, cw_g) = _split_prep(
        proj, pos, invf, g_q, g_kv, bfor, lay, comm=pass_plan(small_g))
    wuq_pad = jnp.pad(wuq_g, ((0, 0), (0, 0), (0, ATT_DK - NOPE - ROPE)))
    wbm = jnp.transpose(wbm_g, (1, 0, 2)).reshape(HEADS * VDIM, D)
    wbf = jnp.transpose(wbf_g, (1, 0, 2)).reshape(HEADS * FOX_DIM, D)
    wout = wout_g.reshape(D, D)
    cw_full = jnp.transpose(cw_g, (1, 0, 2)).reshape(3, 2 * F)

    qraw = _matmul(qn, wuq_pad, mode="nn", name="mm_q", out_blocks=ATT_DK)
    kvraw = _matmul(kvn, wukv_g, mode="nn", name="mm_kv", out_blocks=NOPE + VDIM)
    q_mla, k_mla, v_mla = _mla_prep(qraw, kvraw, kper, cosT, sinT)
    cs = _fox_cumsum(logf, B, S, 1.0 / fox_scale)
    q_fox, k_fox, v_fox = _fox_prep(proj, cs, lay)
    ((o_mla, lse_mla), (o_fox, lse_fox)), wup_half = _attn_fwd(
        [(q_mla, k_mla, v_mla, MLA_UNIT, mla_scale), (q_fox, k_fox, v_fox, 1, fox_scale)], B=B, S=S,
        name="attn_fwd", comm=_Comm([_GatherOwnPlan([wup_s], rows=(half_d, D), into=wup_part)]))
    a_m = _matmul(o_mla, wbm, mode="nn", name="mm_branch_mla")
    a_f = _matmul(o_fox, wbf, mode="nn", name="mm_branch_fox")
    (merged,), (wup_g,) = _gate_merge(a_m, a_f, proj, bgate, lay, D, comm=pass_plan(wup_half))
    n_up = wup_g.shape[2]
    y1 = _matmul(merged, wout, mode="nn", name="mm_out")
    x1, h2 = _mid(x2, y1, g_pm, g_ffn)
    u, wdown_half = _matmul(h2, wup_g, mode="nn", name="mm_up", tn=n_up, comm=own_plan([wdown_s]))
    (act,), (wdown_g,) = _convffn_fwd(u, cw_full, cb_full, B, S, F, comm=pass_plan(wdown_half))
    wdown = wdown_g.reshape(F, D)
    ff = _matmul(act, wdown, mode="nn", name="mm_down", tk=F // 2)
    dy, dff, loss_part, dg_pf = _tail(ff, x1, tgt, g_pf)

    dact = _matmul(dff, wdown, mode="nt", name="mm_dact", tn=F // 4)
    dw_down = _matmul(act, dff, mode="tn", name="mm_dw_down", tm=F // 4, tn=512).reshape(N_DEV, F // N_DEV, D)
    (du_g, du_v, dcp_g, dcp_v), (pa_down,) = _convffn_bwd(u, dact, cw_full, cb_full, B, S, F, comm=pair_plan([dw_down]))
    keep_down, sb_down = _chip_sum(dw_down, pa_down, qc, "chipsum_w_down")
    du = _concat_cols([du_g, du_v], "concat_du")
    dh2, (rb_down,) = _matmul(du, wup_g, mode="nt", name="mm_dh2", tn=2048, comm=chip_plan([sb_down]))
    dw_up = _matmul(h2, du, mode="tn", name="mm_dw_up", out_blocks=n_up, tm=512, tn=n_up)
    (dx1, dy1, dg_ffn, dg_pm), (pa_up,) = _mid_bwd(dy, dh2, x1, y1, g_ffn, g_pm, comm=pair_plan([dw_up]))
    keep_up, sb_up = _chip_sum(dw_up, pa_up, qc, "chipsum_w_up")
    dmerged = _matmul(dy1, wout, mode="nt", name="mm_dmerged")
    dw_out = _matmul(merged, dy1, mode="tn", name="mm_dw_out").reshape(N_DEV, D // N_DEV, D)
    (da_m, da_f, dgl_m, dgl_f, dbg_m, dbg_f), rb_up_part = _gate_bwd(
        dmerged, a_m, a_f, proj, bgate, lay, D, comm=_Comm([_ChipScatterPlan([sb_up], rows=(0, half_d))]))
    dw_bm = _matmul(o_mla, da_m, mode="tn", name="mm_dw_branch_mla", out_blocks=D // N_DEV)
    dw_bf = _matmul(o_fox, da_f, mode="tn", name="mm_dw_branch_fox", out_blocks=D // N_DEV)
    mix = [dw_out, dw_bm, dw_bf]
    do_mla, pa_mix = _matmul(da_m, wbm, mode="nt", name="mm_do_mla", out_dtype=BF16, comm=pair_plan(mix))
    do_fox = _matmul(da_f, wbf, mode="nt", name="mm_do_fox", out_dtype=BF16)
    mix_sums = [_chip_sum(g, p, qc, "chipsum_" + n) for g, p, n in zip(mix, pa_mix, ["w_out", "w_branch_mla", "w_branch_fox"])]
    ((dq_m, dk_m, dv_m), (dq_f, dk_f, dv_f)), (rb_up,) = _attn_bwd(
        [(q_mla, k_mla, v_mla, o_mla, do_mla, lse_mla, MLA_UNIT, mla_scale),
         (q_fox, k_fox, v_fox, o_fox, do_fox, lse_fox, 1, fox_scale)], B=B, S=S, name="attn_bwd",
        comm=_Comm([_ChipScatterPlan([sb_up], rows=(half_d, D), into=rb_up_part)]))
    (dqraw, dkvraw, dkpe), rb_mix = _mla_bwd_prep(dq_m, dk_m, dv_m, cosT, sinT, comm=chip_plan([s[1] for s in mix_sums]))
    dqn = _matmul(dqraw, wuq_pad, mode="nt", name="mm_dqn")
    dw_uq = _matmul(qn, dqraw, mode="tn", name="mm_dw_uq", out_blocks=ATT_DK)[:, :, :NOPE + ROPE]
    dkvn = _matmul(dkvraw, wukv_g, mode="nt", name="mm_dkvn")
    dw_ukv = _matmul(kvn, dkvraw, mode="tn", name="mm_dw_ukv", out_blocks=NOPE + VDIM)
    dqlat, dkvlat, dg_q, dg_kv = _lat_bwd(dqn, dkvn, proj, g_q, g_kv, lay)
    dfl, dbfor = _fox_bwd_prep(dq_f, dk_f, proj, bfor, lay, B, S, 1.0 / fox_scale)
    dfq, dfk, dfv = _heads_to_cols(dq_f, dk_f, dv_f)
    dproj = _concat_cols([dqlat, dkvlat, dkpe, dfq, dfk, dfv, dfl, dgl_m, dgl_f], "concat_dproj")
    dw_perm = _matmul(h, dproj, mode="tn", name="mm_dw_in")
    segs = [(0, seg_a, 0), (seg_a, seg_a + seg_b, lay["fq"] - seg_a), (seg_a + seg_b, d_in, lay["g"] - seg_a - seg_b)]

    def piece(g):
        lo, hi = g * n_in, (g + 1) * n_in
        parts = [dw_perm[:, max(lo, s0) + sh:min(hi, s1) + sh] for s0, s1, sh in segs if max(lo, s0) < min(hi, s1)]
        return parts[0] if len(parts) == 1 else jnp.concatenate(parts, axis=1)

    dw_in = jnp.stack([piece(g) for g in range(N_DEV)])
    dcw = jnp.transpose(jnp.concatenate([dcp_g[0:3], dcp_v[0:3]], axis=1).reshape(3, N_DEV, (2 * F) // N_DEV), (1, 0, 2))
    late = [dw_in, dw_uq, dw_ukv, dcw]
    pa_late = _exchange_alone(pair_plan(late), "pair_late")
    late_sums = [_chip_sum(g, p, qc, "chipsum_" + n) for g, p, n in zip(late, pa_late, ["w_in", "w_uq", "w_ukv", "conv_w"])]
    dh, rb_late = _matmul(dproj, w_perm, mode="nt", name="mm_dh", tn=2048, tk=2048, comm=chip_plan([s[1] for s in late_sums]))
    (grad_x, dg_pre), _ = _final_dx(dx1, dh, x2, g_pre)

    big_out = {}

    def finish(n, keep, pieces, w, m, v):
        big_out[n] = _sum_adamw(keep, pieces, w, m, v, "adamw_" + n)

    finish("w_down", keep_down, rb_down, w_down, m_w_down, v_w_down)
    finish("w_up", keep_up, rb_up, w_up, m_w_up, v_w_up)
    finish("w_out", mix_sums[0][0], rb_mix[0], w_out, m_w_out, v_w_out)
    finish("w_branch_mla", mix_sums[1][0], rb_mix[1], w_branch_mla, m_w_branch_mla, v_w_branch_mla)
    finish("w_branch_fox", mix_sums[2][0], rb_mix[2], w_branch_fox, m_w_branch_fox, v_w_branch_fox)
    finish("w_in", late_sums[0][0], rb_late[0], w_in, m_w_in, v_w_in)
    finish("w_uq", late_sums[1][0], rb_late[1], w_uq, m_w_uq, v_w_uq)
    finish("w_ukv", late_sums[2][0], rb_late[2], w_ukv, m_w_ukv, v_w_ukv)
    finish("conv_w", late_sums[3][0], rb_late[3], conv_w, m_conv_w, v_conv_w)

    widths = [D, Q_LORA, KV_LORA, LANES, 2 * D, D, D, 2 * F, D]
    small_names = ["pre_mix_norm", "q_a_norm", "kv_a_norm", "b_forget", "b_gate", "post_mix_norm", "pre_ffn_norm",
                   "conv_b", "post_ffn_norm"]
    true_w = [D, Q_LORA, KV_LORA, HEADS, 2 * D, D, D, 2 * F, D]
    dcb = jnp.concatenate([dcp_g[3:4], dcp_v[3:4]], axis=1)
    part = jnp.concatenate([dg_pre, dg_q, dg_kv, dbfor, dbg_m, dbg_f, dg_pm, dg_ffn, dcb, dg_pf], axis=1)

    def pack(vals):
        return jnp.concatenate([row(a, wd) for a, wd in zip(vals, widths)], axis=1)

    sw = pack([pre_mix_norm, q_a_norm, kv_a_norm, b_forget, b_gate, post_mix_norm, pre_ffn_norm, conv_b, post_ffn_norm])
    sm = pack([m_pre_mix_norm, m_q_a_norm, m_kv_a_norm, m_b_forget, m_b_gate, m_post_mix_norm, m_pre_ffn_norm,
               m_conv_b, m_post_ffn_norm])
    sv = pack([v_pre_mix_norm, v_q_a_norm, v_kv_a_norm, v_b_forget, v_b_gate, v_post_mix_norm, v_pre_ffn_norm,
               v_conv_b, v_post_ffn_norm])
    (parts_all,) = _exchange_alone(_Comm([_DirectGatherPlan([part])]), "gather_small")
    sg, sd, smo, svo = _sum_adamw(None, parts_all, sw, sm, sv, "adamw_small")
    small_out = {}
    off = 0
    for n, wd, tw in zip(small_names, widths, true_w):
        small_out[n] = tuple(a[0, off:off + tw] for a in (sg, sd, smo, svo))
        off += wd

    loss = lax.psum(loss_part[0, 0], MESH_AXES)
    order = ["pre_mix_norm", "w_in", "q_a_norm", "w_uq", "kv_a_norm", "w_ukv", "b_forget", "b_gate", "w_branch_mla",
             "w_branch_fox", "w_out", "post_mix_norm", "pre_ffn_norm", "w_up", "conv_w", "conv_b", "w_down",
             "post_ffn_norm"]
    res = {**big_out, **small_out}
    outs = [loss, grad_x.reshape(B, S, D)]
    for kind in range(4):
        outs += [res[n][kind] for n in order]
    return tuple(outs)
```

```python
import math

import jax
import jax.numpy as jnp
from jax import lax
from jax.experimental import pallas as pl
from jax.experimental.pallas import tpu as pltpu

F32 = jnp.float32
BF16 = jnp.bfloat16

N_DEV = 8
N_CHIP = 4
HEADS = 8
NOPE = 128
ROPE = 64
HALF_ROPE = ROPE // 2
VDIM = 128
Q_LORA = 512
KV_LORA = 256
FOX_DIM = 128
ATT_DK = 256
MLA_UNIT = 64
ROPE_THETA = 10000.0
EPS = 1e-6
NEG_INF = -1e30
LANES = 128
LOG2_E = 1.4426950408889634

ADAM_LR = 0.001
ADAM_B1 = 0.9
ADAM_B2 = 0.999
ADAM_EPS = 1e-08
ADAM_WD = 0.01
ADAM_STEP = 10

VMEM_LIMIT_BYTES = 56 * 1024 * 1024
ROW_TILE = 256
HEAD_ROW_TILE = 1024
ATT_TILE = 1024
ATT_SUB = 256
ATT_AHEAD = 3
MM_TILE = 1024

MESH_AXES = ("x", "y", "c")
ANY = pl.BlockSpec(memory_space=pl.ANY)


def _tile(n, pref, align=LANES):
    if n <= pref:
        return n
    t = (pref // align) * align
    while t >= align:
        if n % t == 0:
            return t
        t -= align
    return n


def _sds(shape, dtype):
    return jax.ShapeDtypeStruct(shape, dtype)


def _coords():
    x, y, c = (lax.axis_index(ax) for ax in MESH_AXES)
    return x, y, c


def _chip_rel(x, y, r):
    return (1 - x if r & 2 else x), (1 - y if r & 1 else y)


def _rcopy(src, dst, sems, w, k, dev):
    return pltpu.make_async_remote_copy(src_ref=src, dst_ref=dst, send_sem=sems[0].at[w, k], recv_sem=sems[1].at[w, k],
                                        device_id=dev, device_id_type=pl.DeviceIdType.MESH)


class _GatherPlan:
    def __init__(self, blocks, mid_frac=0.5):
        self.ins = list(blocks)
        self.out_shapes = [_sds((N_DEV,) + b.shape, b.dtype) for b in blocks]
        n = len(blocks)
        self.scratch = [pltpu.SemaphoreType.DMA((n, 7)), pltpu.SemaphoreType.DMA((n, 7)), pltpu.SemaphoreType.DMA((n,))]
        self.mid_frac = mid_frac

    def first(self, ins, outs, sems):
        x, y, c = _coords()
        me = 4 * x + 2 * y + c
        for w in range(len(ins)):
            pltpu.make_async_copy(ins[w], outs[w].at[me], sems[2].at[w]).start()
            _rcopy(ins[w], outs[w].at[me], sems, w, 0, (x, y, 1 - c)).start()
            for r in (1, 2, 3):
                px, py = _chip_rel(x, y, r)
                _rcopy(ins[w], outs[w].at[me], sems, w, r, (px, py, c)).start()

    def mid(self, ins, outs, sems):
        x, y, c = _coords()
        for w in range(len(ins)):
            for r in (1, 2, 3):
                px, py = _chip_rel(x, y, r)
                blk = outs[w].at[4 * px + 2 * py + c]
                _rcopy(ins[w], blk, sems, w, r, (px, py, c)).wait_recv()
                _rcopy(blk, blk, sems, w, 3 + r, (x, y, 1 - c)).start()

    def last(self, ins, outs, sems):
        x, y, c = _coords()
        me = 4 * x + 2 * y + c
        sib = (x, y, 1 - c)
        for w in range(len(ins)):
            _rcopy(ins[w], outs[w].at[4 * x + 2 * y + 1 - c], sems, w, 0, sib).wait_recv()
            for r in (1, 2, 3):
                px, py = _chip_rel(x, y, r)
                blk = outs[w].at[4 * px + 2 * py + 1 - c]
                _rcopy(blk, blk, sems, w, 3 + r, sib).wait_recv()
            for k in range(7):
                _rcopy(ins[w], outs[w].at[me], sems, w, k, sib).wait_send()
            pltpu.make_async_copy(ins[w], outs[w].at[me], sems[2].at[w]).wait()


class _GatherOwnPlan:
    mid = None

    def __init__(self, blocks, rows=None, into=None):
        self.n = len(blocks)
        self.rows = rows
        self.ins = list(blocks) + list(into or [])
        self.out_shapes = [_sds((N_DEV,) + b.shape, b.dtype) for b in blocks]
        self.aliases = [(self.n + i, i) for i in range(len(into or []))]
        n = self.n
        self.scratch = [pltpu.SemaphoreType.DMA((n, 4)), pltpu.SemaphoreType.DMA((n, 4)), pltpu.SemaphoreType.DMA((n,))]

    def _cut(self, ref):
        return ref if self.rows is None else ref.at[pl.ds(self.rows[0], self.rows[1] - self.rows[0])]

    def first(self, ins, outs, sems):
        x, y, c = _coords()
        me = 4 * x + 2 * y + c
        for w in range(self.n):
            src, dst = self._cut(ins[w]), self._cut(outs[w].at[me])
            pltpu.make_async_copy(src, dst, sems[2].at[w]).start()
            _rcopy(src, dst, sems, w, 0, (x, y, 1 - c)).start()
            for r in (1, 2, 3):
                px, py = _chip_rel(x, y, r)
                _rcopy(src, dst, sems, w, r, (px, py, c)).start()

    def last(self, ins, outs, sems):
        x, y, c = _coords()
        me = 4 * x + 2 * y + c
        for w in range(self.n):
            src = self._cut(ins[w])
            cp = _rcopy(src, self._cut(outs[w].at[4 * x + 2 * y + 1 - c]), sems, w, 0, (x, y, 1 - c))
            cp.wait_recv()
            cp.wait_send()
            for r in (1, 2, 3):
                px, py = _chip_rel(x, y, r)
                cp = _rcopy(src, self._cut(outs[w].at[4 * px + 2 * py + c]), sems, w, r, (px, py, c))
                cp.wait_recv()
                cp.wait_send()
            pltpu.make_async_copy(src, self._cut(outs[w].at[me]), sems[2].at[w]).wait()


class _GatherPassPlan:
    mid = None

    def __init__(self, gathered):
        self.ins = list(gathered)
        self.out_shapes = [_sds(g.shape, g.dtype) for g in gathered]
        self.aliases = [(i, i) for i in range(len(gathered))]
        n = len(gathered)
        self.scratch = [pltpu.SemaphoreType.DMA((n, 3)), pltpu.SemaphoreType.DMA((n, 3))]

    def first(self, ins, outs, sems):
        x, y, c = _coords()
        for w in range(len(ins)):
            for r in (1, 2, 3):
                px, py = _chip_rel(x, y, r)
                blk = 4 * px + 2 * py + c
                _rcopy(ins[w].at[blk], outs[w].at[blk], sems, w, r - 1, (x, y, 1 - c)).start()

    def last(self, ins, outs, sems):
        x, y, c = _coords()
        for w in range(len(ins)):
            for r in (1, 2, 3):
                px, py = _chip_rel(x, y, r)
                blk = 4 * px + 2 * py + 1 - c
                cp = _rcopy(ins[w].at[blk], outs[w].at[blk], sems, w, r - 1, (x, y, 1 - c))
                cp.wait_recv()
                cp.wait_send()


class _DirectGatherPlan:
    mid = None

    def __init__(self, blocks):
        self.ins = list(blocks)
        self.out_shapes = [_sds((N_DEV,) + b.shape, b.dtype) for b in blocks]
        n = len(blocks)
        self.scratch = [pltpu.SemaphoreType.DMA((n, 7)), pltpu.SemaphoreType.DMA((n, 7)), pltpu.SemaphoreType.DMA((n,))]

    @staticmethod
    def _peer(x, y, c, r):
        return (1 - x if r & 4 else x), (1 - y if r & 2 else y), (1 - c if r & 1 else c)

    def first(self, ins, outs, sems):
        x, y, c = _coords()
        me = 4 * x + 2 * y + c
        for w in range(len(ins)):
            pltpu.make_async_copy(ins[w], outs[w].at[me], sems[2].at[w]).start()
            for r in range(1, N_DEV):
                _rcopy(ins[w], outs[w].at[me], sems, w, r - 1, self._peer(x, y, c, r)).start()

    def last(self, ins, outs, sems):
        x, y, c = _coords()
        me = 4 * x + 2 * y + c
        for w in range(len(ins)):
            for r in range(1, N_DEV):
                px, py, pc = self._peer(x, y, c, r)
                cp = _rcopy(ins[w], outs[w].at[4 * px + 2 * py + pc], sems, w, r - 1, (px, py, pc))
                cp.wait_recv()
                cp.wait_send()
            pltpu.make_async_copy(ins[w], outs[w].at[me], sems[2].at[w]).wait()


class _PairScatterPlan:
    mid = None

    def __init__(self, pieces):
        self.ins = list(pieces)
        self.out_shapes = [_sds((N_CHIP,) + p.shape[1:], p.dtype) for p in pieces]
        n = len(pieces)
        self.scratch = [pltpu.SemaphoreType.DMA((n, N_CHIP)), pltpu.SemaphoreType.DMA((n, N_CHIP))]

    def _copies(self, ins, outs, sems):
        x, y, c = _coords()
        return [_rcopy(ins[w].at[2 * q + 1 - c], outs[w].at[q], sems, w, q, (x, y, 1 - c))
                for w in range(len(ins)) for q in range(N_CHIP)]

    def first(self, ins, outs, sems):
        for cp in self._copies(ins, outs, sems):
            cp.start()

    def last(self, ins, outs, sems):
        for cp in self._copies(ins, outs, sems):
            cp.wait_recv()
            cp.wait_send()


class _ChipScatterPlan:
    mid = None

    def __init__(self, sums, rows=None, into=None):
        self.n = len(sums)
        self.rows = rows
        self.ins = list(sums) + list(into or [])
        self.out_shapes = [_sds(s.shape, s.dtype) for s in sums]
        self.aliases = [(self.n + i, i) for i in range(len(into or []))]
        self.scratch = [pltpu.SemaphoreType.DMA((self.n, 3)), pltpu.SemaphoreType.DMA((self.n, 3))]

    def _copies(self, ins, outs, sems):
        x, y, c = _coords()
        cps = []
        for w in range(self.n):
            for r in (1, 2, 3):
                px, py = _chip_rel(x, y, r)
                src, dst = ins[w].at[r - 1], outs[w].at[r - 1]
                if self.rows is not None:
                    cut = pl.ds(self.rows[0], self.rows[1] - self.rows[0])
                    src, dst = src.at[cut], dst.at[cut]
                cps.append(_rcopy(src, dst, sems, w, r - 1, (px, py, c)))
        return cps

    def first(self, ins, outs, sems):
        for cp in self._copies(ins, outs, sems):
            cp.start()

    def last(self, ins, outs, sems):
        for cp in self._copies(ins, outs, sems):
            cp.wait_recv()
            cp.wait_send()


class _Comm:
    def __init__(self, plans):
        self.plans = list(plans)
        self.ins = [a for p in self.plans for a in p.ins]
        self.out_shapes = [s for p in self.plans for s in p.out_shapes]
        self.scratch = [s for p in self.plans for s in p.scratch]
        self.aliases = []
        i = o = 0
        for p in self.plans:
            self.aliases += [(i + a, o + b) for a, b in getattr(p, "aliases", [])]
            i, o = i + len(p.ins), o + len(p.out_shapes)

    def _parts(self, ins, outs, sems):
        i = o = s = 0
        for p in self.plans:
            yield p, ins[i:i + len(p.ins)], outs[o:o + len(p.out_shapes)], sems[s:s + len(p.scratch)]
            i, o, s = i + len(p.ins), o + len(p.out_shapes), s + len(p.scratch)

    def begin(self, step, nsteps, ins, outs, sems):
        @pl.when(step == 0)
        def _():
            for p, pi, po, ps in self._parts(ins, outs, sems):
                p.first(pi, po, ps)

        for p, pi, po, ps in self._parts(ins, outs, sems):
            if p.mid is not None:
                @pl.when(step == min(nsteps - 1, int(p.mid_frac * nsteps)))
                def _(p=p, pi=pi, po=po, ps=ps):
                    p.mid(pi, po, ps)

    def end(self, step, nsteps, ins, outs, sems):
        @pl.when(step == nsteps - 1)
        def _():
            for p, pi, po, ps in self._parts(ins, outs, sems):
                p.last(pi, po, ps)


def _call(body, args, *, name, grid, in_specs, out_specs, out_shape, scratch_shapes=(), sem=None, comm=None):
    in_specs, out_specs, out_shape, scratch_shapes = list(in_specs), list(out_specs), list(out_shape), list(scratch_shapes)
    if comm is None:
        res = pl.pallas_call(
            body, name=name, grid=grid, in_specs=in_specs, out_specs=out_specs, out_shape=out_shape,
            scratch_shapes=scratch_shapes,
            compiler_params=pltpu.CompilerParams(dimension_semantics=sem, vmem_limit_bytes=VMEM_LIMIT_BYTES),
        )(*args)
        return list(res), []
    n_in, n_out, n_sc = len(in_specs), len(out_specs), len(scratch_shapes)
    n_ci, n_co = len(comm.ins), len(comm.out_shapes)
    nsteps = math.prod(grid)

    def hosted(*refs):
        ins, cins = refs[:n_in], refs[n_in:n_in + n_ci]
        o0 = n_in + n_ci
        outs, couts = refs[o0:o0 + n_out], refs[o0 + n_out:o0 + n_out + n_co]
        s0 = o0 + n_out + n_co
        scr, csems = refs[s0:s0 + n_sc], refs[s0 + n_sc:]
        step = jnp.int32(0)
        for d in range(len(grid)):
            step = step * grid[d] + pl.program_id(d)
        comm.begin(step, nsteps, cins, couts, csems)
        body(*ins, *outs, *scr)
        comm.end(step, nsteps, cins, couts, csems)

    res = pl.pallas_call(
        hosted, name=name, grid=grid, in_specs=in_specs + [ANY] * n_ci, out_specs=out_specs + [ANY] * n_co,
        out_shape=out_shape + comm.out_shapes, scratch_shapes=scratch_shapes + comm.scratch,
        input_output_aliases={n_in + a: n_out + b for a, b in comm.aliases},
        compiler_params=pltpu.CompilerParams(dimension_semantics=("arbitrary",) * len(grid),
                                             vmem_limit_bytes=VMEM_LIMIT_BYTES, has_side_effects=True),
    )(*args, *comm.ins)
    return list(res[:n_out]), list(res[n_out:])


def _exchange_alone(comm, name):
    def body():
        pass

    return _call(body, [], name=name, grid=(), in_specs=[], out_specs=[], out_shape=[], comm=comm)[1]


def _matmul(a, b, *, mode, name, out_dtype=F32, out_blocks=None, tm=None, tn=None, tk=None, comm=None):
    tm = MM_TILE if tm is None else tm
    tn = MM_TILE if tn is None else tn
    a_blk = a.ndim == 3
    b_blk = b.ndim == 3
    if mode == "nn":
        M, K = a.shape
        N = b.shape[0] * b.shape[2] if b_blk else b.shape[1]
        dims = (((1,), (0,)), ((), ()))
    elif mode == "nt":
        M = a.shape[1] if a_blk else a.shape[0]
        K = a.shape[0] * a.shape[2] if a_blk else a.shape[1]
        N = b.shape[1] if b_blk else b.shape[0]
        dims = (((1,), (1,)), ((), ()))
    else:
        K, M = a.shape
        N = b.shape[0] * b.shape[2] if b_blk else b.shape[1]
        dims = (((0,), (0,)), ((), ()))

    tm = _tile(M, tm)
    tn = _tile(N, tn)
    if mode == "nt" and (a_blk or b_blk):
        tk = a.shape[2] if a_blk else b.shape[2]
    else:
        tk = _tile(K, K if tk is None else tk)
    if mode != "nt" and b_blk:
        tn = _tile(b.shape[2], tn)
    if out_blocks is not None:
        tn = _tile(out_blocks, tn)
    nk = K // tk
    grid = (M // tm, N // tn, nk)

    if mode == "nn":
        a_spec = pl.BlockSpec((tm, tk), lambda i, j, k: (i, k))
        if b_blk:
            rb = b.shape[2] // tn
            b_spec = pl.BlockSpec((None, tk, tn), lambda i, j, k: (j // rb, k, j % rb))
        else:
            b_spec = pl.BlockSpec((tk, tn), lambda i, j, k: (k, j))
    elif mode == "nt":
        if a_blk:
            a_spec = pl.BlockSpec((None, tm, tk), lambda i, j, k: (k, i, 0))
        else:
            a_spec = pl.BlockSpec((tm, tk), lambda i, j, k: (i, k))
        if b_blk:
            b_spec = pl.BlockSpec((None, tn, tk), lambda i, j, k: (k, j, 0))
        else:
            b_spec = pl.BlockSpec((tn, tk), lambda i, j, k: (j, k))
    else:
        a_spec = pl.BlockSpec((tk, tm), lambda i, j, k: (k, i))
        if b_blk:
            rb = b.shape[2] // tn
            b_spec = pl.BlockSpec((None, tk, tn), lambda i, j, k: (j // rb, k, j % rb))
        else:
            b_spec = pl.BlockSpec((tk, tn), lambda i, j, k: (k, j))

    if out_blocks is None:
        o_spec = pl.BlockSpec((tm, tn), lambda i, j, k: (i, j))
        o_shape = _sds((M, N), out_dtype)
    else:
        ro = out_blocks // tn
        o_spec = pl.BlockSpec((None, tm, tn), lambda i, j, k: (j // ro, i, j % ro))
        o_shape = _sds((N // out_blocks, M, out_blocks), out_dtype)

    direct = nk == 1 or out_dtype == F32

    def body(a_ref, b_ref, o_ref, *scratch):
        if nk == 1:
            o_ref[...] = lax.dot_general(a_ref[...], b_ref[...], dims, preferred_element_type=F32).astype(o_ref.dtype)
            return
        acc_ref = o_ref if direct else scratch[0]
        k = pl.program_id(2)

        @pl.when(k == 0)
        def _():
            acc_ref[...] = jnp.zeros(acc_ref.shape, F32)

        acc_ref[...] += lax.dot_general(a_ref[...], b_ref[...], dims, preferred_element_type=F32)
        if not direct:
            @pl.when(k == nk - 1)
            def _():
                o_ref[...] = acc_ref[...].astype(o_ref.dtype)

    scratch = [] if direct else [pltpu.VMEM((tm, tn), F32)]
    outs, landed = _call(body, [a, b], name=name, grid=grid, in_specs=[a_spec, b_spec], out_specs=[o_spec],
                         out_shape=[o_shape], scratch_shapes=scratch, sem=("parallel", "parallel", "arbitrary"), comm=comm)
    return outs[0] if comm is None else (outs[0], landed)


def _matmul_halves(a, b, *, mode, name, tm, tn=None, tk=None, comm=None):
    if mode == "nt":
        lo, hi = a
        M, kh = lo.shape
        G, N, kb = b.shape
        half = kh // kb
        tm, tn = _tile(M, tm), _tile(N, N if tn is None else tn)
        dims = (((1,), (1,)), ((), ()))

        def body(lo_ref, hi_ref, b_ref, o_ref):
            k = pl.program_id(2)

            @pl.when(k == 0)
            def _():
                o_ref[...] = jnp.zeros(o_ref.shape, F32)

            @pl.when(k < half)
            def _():
                o_ref[...] += lax.dot_general(lo_ref[...], b_ref[...], dims, preferred_element_type=F32)

            @pl.when(k >= half)
            def _():
                o_ref[...] += lax.dot_general(hi_ref[...], b_ref[...], dims, preferred_element_type=F32)

        outs, landed = _call(
            body, [lo, hi, b], name=name, grid=(M // tm, N // tn, G),
            in_specs=[pl.BlockSpec((tm, kb), lambda i, j, k: (i, jnp.minimum(k, half - 1))),
                      pl.BlockSpec((tm, kb), lambda i, j, k: (i, jnp.maximum(k - half, 0))),
                      pl.BlockSpec((None, tn, kb), lambda i, j, k: (k, j, 0))],
            out_specs=[pl.BlockSpec((tm, tn), lambda i, j, k: (i, j))], out_shape=[_sds((M, N), F32)],
            sem=("parallel", "parallel", "arbitrary"), comm=comm)
    else:
        lo, hi = b
        K, nh = lo.shape
        M = a.shape[1]
        n = tn
        half = nh // n
        tm, tk = _tile(M, tm), _tile(K, K if tk is None else tk)
        nk = K // tk
        dims = (((0,), (0,)), ((), ()))

        def body(a_ref, lo_ref, hi_ref, o_ref):
            j, k = pl.program_id(1), pl.program_id(2)

            @pl.when(k == 0)
            def _():
                o_ref[...] = jnp.zeros(o_ref.shape, F32)

            @pl.when(j < half)
            def _():
                o_ref[...] += lax.dot_general(a_ref[...], lo_ref[...], dims, preferred_element_type=F32)

            @pl.when(j >= half)
            def _():
                o_ref[...] += lax.dot_general(a_ref[...], hi_ref[...], dims, preferred_element_type=F32)

        outs, landed = _call(
            body, [a, lo, hi], name=name, grid=(M // tm, 2 * half, nk),
            in_specs=[pl.BlockSpec((tk, tm), lambda i, j, k: (k, i)),
                      pl.BlockSpec((tk, n), lambda i, j, k: (jnp.where(j < half, k, nk - 1), jnp.minimum(j, half - 1))),
                      pl.BlockSpec((tk, n), lambda i, j, k: (jnp.where(j >= half, k, 0), jnp.maximum(j - half, 0)))],
            out_specs=[pl.BlockSpec((None, tm, n), lambda i, j, k: (j, i, 0))],
            out_shape=[_sds((2 * half, M, n), F32)], sem=("parallel", "parallel", "arbitrary"), comm=comm)
    return outs[0] if comm is None else (outs[0], landed)


def _rms(x):
    return lax.rsqrt(jnp.mean(x * x, axis=-1, keepdims=True) + EPS)


def _rms_bwd(dy, x, g):
    r = _rms(x)
    n = x * r
    dn = dy * g
    dx = r * (dn - n * jnp.mean(dn * n, axis=-1, keepdims=True))
    return dx, dy * n


def _sigmoid(x):
    return 1.0 / (1.0 + jnp.exp(-x))


def _rope_rot(t):
    return pltpu.roll(t, HALF_ROPE, 1) - pltpu.roll(t, LANES - HALF_ROPE, 1)


def _lane(shape):
    return lax.broadcasted_iota(jnp.int32, shape, 1)


def _split3(x):
    hi = x.astype(BF16).astype(F32)
    r1 = x - hi
    mid = r1.astype(BF16).astype(F32)
    lo = (r1 - mid).astype(BF16).astype(F32)
    return hi, mid, lo


def _cumsum_rows(x, reverse):
    S = x.shape[0]
    bs = min(256, S)
    nb = S // bs
    r = lax.broadcasted_iota(jnp.int32, (bs, bs), 0)
    c = lax.broadcasted_iota(jnp.int32, (bs, bs), 1)
    tri = jnp.where((c >= r) if reverse else (c <= r), 1.0, 0.0).astype(BF16)
    edge = lax.broadcasted_iota(jnp.int32, (bs, x.shape[1]), 0) == (0 if reverse else bs - 1)
    carry = jnp.zeros((1, x.shape[1]), F32)
    outs = [None] * nb
    for bi in (range(nb - 1, -1, -1) if reverse else range(nb)):
        xb = x[bi * bs:(bi + 1) * bs, :]
        acc = carry
        for term in _split3(xb):
            acc = acc + jnp.dot(tri, term.astype(BF16), preferred_element_type=F32)
        outs[bi] = acc
        carry = jnp.sum(jnp.where(edge, acc, 0.0), axis=0, keepdims=True)
    return jnp.concatenate(outs, axis=0) if nb > 1 else outs[0]


def _gelu_parts(x):
    c0 = math.sqrt(2.0 / math.pi)
    inner = c0 * (x + 0.044715 * (x * x * x))
    t = jnp.tanh(inner)
    g = 0.5 * x * (1.0 + t)
    dg = 0.5 * (1.0 + t) + 0.5 * x * (1.0 - t * t) * (c0 * (1.0 + 3.0 * 0.044715 * (x * x)))
    return g, dg


def _accumulate(ref, value, first):
    @pl.when(first)
    def _():
        ref[...] = value

    @pl.when(jnp.logical_not(first))
    def _():
        ref[...] += value


def _cast_bf16(w, name):
    R, C = w.shape
    tr = _tile(R, 512, 16)

    def body(w_ref, o_ref):
        o_ref[...] = w_ref[...].astype(BF16)

    blk = pl.BlockSpec((tr, C), lambda i: (i, 0))
    return _call(body, [w], name=name, grid=(R // tr,), in_specs=[blk], out_specs=[blk],
                 out_shape=[_sds((R, C), BF16)], sem=("parallel",))[0][0]


def _concat_cols(parts, name):
    T = parts[0].shape[0]
    widths = [p.shape[1] for p in parts]
    tm = _tile(T, ROW_TILE, 16)

    def body(*refs):
        o_ref = refs[-1]
        off = 0
        for p_ref, w in zip(refs[:-1], widths):
            o_ref[:, off:off + w] = p_ref[...].astype(BF16)
            off += w

    return _call(body, parts, name=name, grid=(T // tm,),
                 in_specs=[pl.BlockSpec((tm, w), lambda i: (i, 0)) for w in widths],
                 out_specs=[pl.BlockSpec((tm, sum(widths)), lambda i: (i, 0))],
                 out_shape=[_sds((T, sum(widths)), BF16)], sem=("parallel",))[0][0]


def _prenorm(x, g, comm=None):
    T, D = x.shape
    tm = _tile(T, ROW_TILE, 16)

    def body(x_ref, g_ref, h_ref):
        xv = x_ref[...]
        h_ref[...] = (xv * _rms(xv) * g_ref[...]).astype(BF16)

    row = pl.BlockSpec((tm, D), lambda i: (i, 0))
    (h,), landed = _call(body, [x, g], name="prenorm", grid=(T // tm,),
                         in_specs=[row, pl.BlockSpec((1, D), lambda i: (0, 0))], out_specs=[row],
                         out_shape=[_sds((T, D), BF16)], sem=("parallel",), comm=comm)
    return h, landed


def _split_prep(proj, pos, invf, gq, gkv, bfor, lay, comm=None):
    T = proj.shape[0]
    tm = _tile(T, ROW_TILE, 16)

    def body(q_ref, kv_ref, kpe_ref, fl_ref, pos_ref, invf_ref, gq_ref, gkv_ref, bf_ref,
             qn_ref, kvn_ref, kper_ref, logf_ref, cos_ref, sin_ref):
        ql = q_ref[...]
        qn_ref[...] = (ql * _rms(ql) * gq_ref[...]).astype(BF16)
        kl = kv_ref[...]
        kvn_ref[...] = (kl * _rms(kl) * gkv_ref[...]).astype(BF16)
        ang = pos_ref[...].astype(F32) * invf_ref[...]
        valid = _lane(ang.shape) < ROPE
        cs = jnp.where(valid, jnp.cos(ang), 0.0)
        sn = jnp.where(valid, jnp.sin(ang), 0.0)
        cos_ref[...] = cs
        sin_ref[...] = sn
        kp = jnp.where(valid, kpe_ref[...], 0.0)
        kper_ref[...] = (kp * cs + _rope_rot(kp) * sn).astype(BF16)
        z = fl_ref[...] + bf_ref[...]
        logf_ref[...] = jnp.minimum(z, 0.0) - jnp.log(1.0 + jnp.exp(-jnp.abs(z)))

    def col(width, off):
        return pl.BlockSpec((tm, width), lambda i: (i, off // width))

    def vec(width):
        return pl.BlockSpec((1, width), lambda i: (0, 0))

    def out(width):
        return pl.BlockSpec((tm, width), lambda i: (i, 0))

    return _call(
        body, [proj, proj, proj, proj, pos, invf, gq, gkv, bfor], name="split_prep", grid=(T // tm,),
        in_specs=[col(Q_LORA, lay["q"]), col(KV_LORA, lay["kv"]), col(LANES, lay["kpe"]), col(LANES, lay["fl"]),
                  pl.BlockSpec((tm, 1), lambda i: (i, 0)), vec(LANES), vec(Q_LORA), vec(KV_LORA), vec(LANES)],
        out_specs=[out(Q_LORA), out(KV_LORA), out(LANES), out(LANES), out(LANES), out(LANES)],
        out_shape=[_sds((T, Q_LORA), BF16), _sds((T, KV_LORA), BF16), _sds((T, LANES), BF16),
                   _sds((T, LANES), F32), _sds((T, LANES), F32), _sds((T, LANES), F32)],
        sem=("parallel",), comm=comm)


def _mla_prep(qraw, kvraw, kper, cosT, sinT, comm=None):
    H, T, _ = qraw.shape
    tm = _tile(T, HEAD_ROW_TILE, 16)

    def body(q_ref, kv_ref, kpe_ref, cos_ref, sin_ref, qo_ref, ko_ref, vo_ref):
        q = q_ref[...]
        pe = q[:, NOPE:]
        pe = jnp.where(_lane(pe.shape) < ROPE, pe, 0.0)
        qo_ref[:, :NOPE] = q[:, :NOPE].astype(BF16)
        qo_ref[:, NOPE:] = (pe * cos_ref[...] + _rope_rot(pe) * sin_ref[...]).astype(BF16)
        kv = kv_ref[...]
        ko_ref[:, :NOPE] = kv[:, :NOPE].astype(BF16)
        ko_ref[:, NOPE:] = kpe_ref[...]
        vo_ref[...] = kv[:, NOPE:].astype(BF16)

    head = pl.BlockSpec((None, tm, ATT_DK), lambda h, i: (h, i, 0))
    tok = pl.BlockSpec((tm, LANES), lambda h, i: (i, 0))
    return _call(
        body, [qraw, kvraw, kper, cosT, sinT], name="mla_prep", grid=(H, T // tm),
        in_specs=[head, head, tok, tok, tok],
        out_specs=[head, head, pl.BlockSpec((None, tm, VDIM), lambda h, i: (h, i, 0))],
        out_shape=[_sds((H, T, ATT_DK), BF16), _sds((H, T, ATT_DK), BF16), _sds((H, T, VDIM), BF16)],
        sem=("parallel", "parallel"), comm=comm)


def _fox_cumsum(logf, B, S, inv_scale):
    T = logf.shape[0]

    def body(l_ref, c_ref):
        c_ref[...] = _cumsum_rows(l_ref[...], reverse=False) * inv_scale

    seq = pl.BlockSpec((S, LANES), lambda b: (b, 0))
    return _call(body, [logf], name="fox_cumsum", grid=(B,), in_specs=[seq], out_specs=[seq],
                 out_shape=[_sds((T, LANES), F32)], sem=("parallel",))[0][0]


def _fox_prep(proj, cs, lay, comm=None):
    T = proj.shape[0]
    tm = _tile(T, HEAD_ROW_TILE, 16)

    def body(q_ref, k_ref, v_ref, cs_ref, qo_ref, ko_ref, vo_ref):
        h = pl.program_id(0)
        cv = cs_ref[...]
        lane = _lane(cv.shape)
        ccol = jnp.sum(jnp.where(lane == h, cv, 0.0), axis=1, keepdims=True)
        hi, mid, lo = _split3(ccol)
        one = jnp.where(lane < 6, 1.0, 0.0)
        augq = jnp.where(lane == 0, hi, jnp.where(lane == 1, mid, jnp.where(lane == 2, lo, one)))
        augk = jnp.where(lane < 3, 1.0, jnp.where(lane == 3, -hi, jnp.where(lane == 4, -mid, jnp.where(lane == 5, -lo, 0.0))))
        qo_ref[:, :FOX_DIM] = q_ref[...].astype(BF16)
        qo_ref[:, FOX_DIM:] = augq.astype(BF16)
        ko_ref[:, :FOX_DIM] = k_ref[...].astype(BF16)
        ko_ref[:, FOX_DIM:] = augk.astype(BF16)
        vo_ref[...] = v_ref[...].astype(BF16)

    def col(off):
        return pl.BlockSpec((tm, FOX_DIM), lambda h, i: (i, off // FOX_DIM + h))

    head = pl.BlockSpec((None, tm, ATT_DK), lambda h, i: (h, i, 0))
    return _call(
        body, [proj, proj, proj, cs], name="fox_prep", grid=(HEADS, T // tm),
        in_specs=[col(lay["fq"]), col(lay["fk"]), col(lay["fv"]), pl.BlockSpec((tm, LANES), lambda h, i: (i, 0))],
        out_specs=[head, head, pl.BlockSpec((None, tm, VDIM), lambda h, i: (h, i, 0))],
        out_shape=[_sds((HEADS, T, ATT_DK), BF16), _sds((HEADS, T, ATT_DK), BF16), _sds((HEADS, T, VDIM), BF16)],
        sem=("parallel", "parallel"), comm=comm)


def _visible(tq, tk, unit):
    r = lax.broadcasted_iota(jnp.int32, (tq, tk), 0)
    c = lax.broadcasted_iota(jnp.int32, (tq, tk), 1)
    sh = int(math.log2(unit))
    return lax.shift_right_logical(c, sh) <= lax.shift_right_logical(r, sh)


def _attn_fwd(streams, *, B, S, name, comm=None):
    n = len(streams)
    H, T, DK = streams[0][0].shape
    DV = streams[0][2].shape[2]
    tq = _tile(S, ATT_TILE)
    nq = S // tq
    sub = min(ATT_SUB, tq)
    NT = (((1,), (1,)), ((), ()))

    def body(*refs):
        ins, outs, (m_sc, acc_sc) = refs[:3 * n], refs[3 * n:5 * n], refs[5 * n:]
        i, j = pl.program_id(1), pl.program_id(2)

        @pl.when(j == 0)
        def _():
            m_sc[...] = jnp.full(m_sc.shape, NEG_INF, F32)
            acc_sc[...] = jnp.zeros(acc_sc.shape, F32)

        def step(diagonal):
            work = [(t, r) for r in range(tq // sub) for t in range(n)]

            def scores(t, r):
                q_ref, k_ref, _ = ins[3 * t:3 * t + 3]
                kc = (r + 1) * sub if diagonal else tq
                s = lax.dot_general(q_ref[r * sub:(r + 1) * sub, :], k_ref[0:kc, :], NT, preferred_element_type=F32)
                return s * (streams[t][4] * LOG2_E)

            ahead = [scores(*work[w]) for w in range(min(ATT_AHEAD, len(work)))]
            for w, (t, r) in enumerate(work):
                s = ahead.pop(0)
                if w + ATT_AHEAD < len(work):
                    ahead.append(scores(*work[w + ATT_AHEAD]))
                v_ref = ins[3 * t + 2]
                kc = s.shape[1]
                rows = slice(r * sub, (r + 1) * sub)
                if diagonal:
                    own = jnp.where(_visible(sub, sub, streams[t][3]), s[:, kc - sub:], NEG_INF)
                    s = own if kc == sub else jnp.concatenate([s[:, :kc - sub], own], axis=1)
                m_prev = m_sc[t, rows, :]
                mx = s[:, 0:LANES]
                for g in range(1, kc // LANES):
                    mx = jnp.maximum(mx, s[:, g * LANES:(g + 1) * LANES])
                m_new = jnp.maximum(m_prev, jnp.max(mx, axis=1, keepdims=True))
                alpha = jnp.exp2(m_prev - m_new)
                p = jnp.exp2(s - jnp.tile(m_new, (1, kc // LANES))).astype(BF16)
                v_aug = jnp.concatenate([v_ref[0:kc, :], jnp.ones((kc, LANES), BF16)], axis=1)
                acc_sc[t, rows, :] = jnp.tile(alpha, (1, 2)) * acc_sc[t, rows, :] + jnp.dot(
                    p, v_aug, preferred_element_type=F32)
                m_sc[t, rows, :] = m_new

        @pl.when(j < i)
        def _():
            step(False)

        @pl.when(j == i)
        def _():
            step(True)
            for t in range(n):
                o_ref, lse_ref = outs[2 * t:2 * t + 2]
                l = acc_sc[t, :, DV:]
                o_ref[...] = (acc_sc[t, :, :DV] / l).astype(BF16)
                lse_ref[...] = m_sc[t] + jnp.log2(l)

    def qmap(g, i, j):
        return (g % H, (g // H) * nq + i, 0)

    def kmap(g, i, j):
        return (g % H, (g // H) * nq + jnp.minimum(j, i), 0)

    args = [a for st in streams for a in st[:3]]
    outs, landed = _call(
        body, args, name=name, grid=(B * H, nq, nq),
        in_specs=[pl.BlockSpec((None, tq, DK), qmap), pl.BlockSpec((None, tq, DK), kmap),
                  pl.BlockSpec((None, tq, DV), kmap)] * n,
        out_specs=[pl.BlockSpec((tq, DV), lambda g, i, j: ((g // H) * nq + i, g % H)),
                   pl.BlockSpec((None, tq, LANES), qmap)] * n,
        out_shape=[_sds((T, H * DV), BF16), _sds((H, T, LANES), F32)] * n,
        scratch_shapes=[pltpu.VMEM((n, tq, LANES), F32), pltpu.VMEM((n, tq, DV + LANES), F32)],
        sem=("parallel", "parallel", "arbitrary"), comm=comm)
    return [(outs[2 * t], outs[2 * t + 1]) for t in range(n)], landed


def _attn_bwd(streams, *, B, S, name, comm=None):
    n = len(streams)
    H, T, DK = streams[0][0].shape
    DV = streams[0][2].shape[2]
    tq = _tile(S, ATT_TILE)
    nq = S // tq
    sub = min(ATT_SUB, tq)
    NT = (((1,), (1,)), ((), ()))
    TN = (((0,), (0,)), ((), ()))

    def body(*refs):
        ins, outs = refs[:6 * n], refs[6 * n:]
        j, i = pl.program_id(1), pl.program_id(2)

        @pl.when(jnp.logical_and(j == 0, i == 0))
        def _():
            for t in range(n):
                outs[3 * t][...] = jnp.zeros(outs[3 * t].shape, F32)

        @pl.when(i == 0)
        def _():
            for t in range(n):
                outs[3 * t + 1][...] = jnp.zeros(outs[3 * t + 1].shape, F32)
                outs[3 * t + 2][...] = jnp.zeros(outs[3 * t + 2].shape, F32)

        def step(diagonal):
            work = [(t, r) for r in range(tq // sub) for t in range(n)]

            def kcols(r):
                return (r + 1) * sub if diagonal else tq

            def scores(t, r):
                q_ref, k_ref, v_ref, _, do_ref, _ = ins[6 * t:6 * t + 6]
                rows, kc = slice(r * sub, (r + 1) * sub), kcols(r)
                s = lax.dot_general(q_ref[rows, :], k_ref[0:kc, :], NT, preferred_element_type=F32)
                dp = lax.dot_general(do_ref[rows, :], v_ref[0:kc, :], NT, preferred_element_type=F32)
                return s * (streams[t][7] * LOG2_E), dp

            def probs(t, r, s, dp):
                _, _, _, o_ref, do_ref, lse_ref = ins[6 * t:6 * t + 6]
                rows, kc = slice(r * sub, (r + 1) * sub), kcols(r)
                if diagonal:
                    own = jnp.where(_visible(sub, sub, streams[t][6]), s[:, kc - sub:], NEG_INF)
                    s = own if kc == sub else jnp.concatenate([s[:, :kc - sub], own], axis=1)
                p = jnp.exp2(s - jnp.tile(lse_ref[rows, :], (1, kc // LANES)))
                delta = jnp.sum(do_ref[rows, :].astype(F32) * o_ref[rows, :].astype(F32), axis=1, keepdims=True)
                return p.astype(BF16), (p * (dp - delta) * streams[t][7]).astype(BF16)

            def grads(t, r, p, ds):
                q_ref, k_ref, _, _, do_ref, _ = ins[6 * t:6 * t + 6]
                dq_ref, dk_ref, dv_ref = outs[3 * t:3 * t + 3]
                rows, kc = slice(r * sub, (r + 1) * sub), kcols(r)
                dv_ref[0:kc, :] += lax.dot_general(p, do_ref[rows, :], TN, preferred_element_type=F32)
                dk_ref[0:kc, :] += lax.dot_general(ds, q_ref[rows, :], TN, preferred_element_type=F32)
                qrows = pl.ds(pl.multiple_of(i * tq + r * sub, sub), sub)
                dq_ref[qrows, :] += jnp.dot(ds, k_ref[0:kc, :], preferred_element_type=F32)

            nw = len(work)
            sc = {w: scores(*work[w]) for w in range(min(2, nw))}
            pr = {0: probs(*work[0], *sc.pop(0))}
            for w in range(nw):
                if w + 2 < nw:
                    sc[w + 2] = scores(*work[w + 2])
                if w + 1 < nw:
                    pr[w + 1] = probs(*work[w + 1], *sc.pop(w + 1))
                grads(*work[w], *pr.pop(w))

        @pl.when(i > j)
        def _():
            step(False)

        @pl.when(i == j)
        def _():
            step(True)

    def qmap(g, j, i):
        return (g % H, (g // H) * nq + jnp.maximum(i, j), 0)

    def kmap(g, j, i):
        return (g % H, (g // H) * nq + j, 0)

    def omap(g, j, i):
        return ((g // H) * nq + jnp.maximum(i, j), g % H)

    args = [a for st in streams for a in st[:6]]
    outs, landed = _call(
        body, args, name=name, grid=(B * H, nq, nq),
        in_specs=[pl.BlockSpec((None, tq, DK), qmap), pl.BlockSpec((None, tq, DK), kmap),
                  pl.BlockSpec((None, tq, DV), kmap), pl.BlockSpec((tq, DV), omap), pl.BlockSpec((tq, DV), omap),
                  pl.BlockSpec((None, tq, LANES), qmap)] * n,
        out_specs=[pl.BlockSpec((None, S, DK), lambda g, j, i: (g % H, g // H, 0)),
                   pl.BlockSpec((None, tq, DK), kmap), pl.BlockSpec((None, tq, DV), kmap)] * n,
        out_shape=[_sds((H, T, DK), F32), _sds((H, T, DK), F32), _sds((H, T, DV), F32)] * n,
        sem=("parallel", "arbitrary", "arbitrary"), comm=comm)
    return [tuple(outs[3 * t:3 * t + 3]) for t in range(n)], landed


def _gate_merge(am, af, proj, bgate, lay, D, comm=None):
    T = am.shape[0]
    tm = _tile(T, ROW_TILE, 16)
    tn = _tile(D, 1024)

    def body(am_ref, af_ref, gm_ref, gf_ref, bm_ref, bf_ref, o_ref):
        sm = _sigmoid(gm_ref[...] + bm_ref[...])
        sf = _sigmoid(gf_ref[...] + bf_ref[...])
        o_ref[...] = (sm * am_ref[...] + sf * af_ref[...]).astype(BF16)

    og = lay["g"] // tn
    blk = pl.BlockSpec((tm, tn), lambda i, j: (i, j))
    return _call(
        body, [am, af, proj, proj, bgate, bgate], name="gate_merge", grid=(T // tm, D // tn),
        in_specs=[blk, blk, pl.BlockSpec((tm, tn), lambda i, j: (i, og + j)),
                  pl.BlockSpec((tm, tn), lambda i, j: (i, og + D // tn + j)),
                  pl.BlockSpec((1, tn), lambda i, j: (0, j)), pl.BlockSpec((1, tn), lambda i, j: (0, D // tn + j))],
        out_specs=[blk], out_shape=[_sds((T, D), BF16)], sem=("parallel", "parallel"), comm=comm)


def _mid(x, y1, g_pm, g_ffn):
    T, D = x.shape
    tm = _tile(T, ROW_TILE, 16)

    def body(x_ref, y_ref, gp_ref, gf_ref, x1_ref, h2_ref):
        y = y_ref[...]
        x1 = x_ref[...] + y * _rms(y) * gp_ref[...]
        x1_ref[...] = x1
        h2_ref[...] = (x1 * _rms(x1) * gf_ref[...]).astype(BF16)

    row = pl.BlockSpec((tm, D), lambda i: (i, 0))
    vec = pl.BlockSpec((1, D), lambda i: (0, 0))
    return _call(body, [x, y1, g_pm, g_ffn], name="mid", grid=(T // tm,), in_specs=[row, row, vec, vec],
                 out_specs=[row, row], out_shape=[_sds((T, D), F32), _sds((T, D), BF16)], sem=("parallel",))[0]


def _conv3(u, w_ref, bias):
    row = lax.broadcasted_iota(jnp.int32, u.shape, 0)
    u1 = jnp.where(row >= 1, pltpu.roll(u, 1, 0), 0.0)
    u2 = jnp.where(row >= 2, pltpu.roll(u, 2, 0), 0.0)
    return w_ref[0:1, :] * u2 + w_ref[1:2, :] * u1 + w_ref[2:3, :] * u + bias, u1, u2


def _convffn_fwd(u, cw, cb, B, S, F, comm=None):
    T = u.shape[0]
    tn = _tile(F, 256)
    nf = F // tn

    def body(ug_ref, uv_ref, wg_ref, wv_ref, bg_ref, bv_ref, a_ref):
        g, _, _ = _conv3(ug_ref[...], wg_ref, bg_ref[...])
        val, _, _ = _conv3(uv_ref[...], wv_ref, bv_ref[...])
        a_ref[...] = (_gelu_parts(g)[0] * val).astype(BF16)

    def seq(off):
        return pl.BlockSpec((S, tn), lambda b, j: (b, off + j))

    def par(rows, off):
        return pl.BlockSpec((rows, tn), lambda b, j: (0, off + j))

    return _call(body, [u, u, cw, cw, cb, cb], name="convffn_fwd", grid=(B, nf),
                 in_specs=[seq(0), seq(nf), par(3, 0), par(3, nf), par(1, 0), par(1, nf)],
                 out_specs=[seq(0)], out_shape=[_sds((T, F), BF16)], sem=("parallel", "parallel"), comm=comm)


def _convffn_bwd(u, dact, cw, cb, B, S, F, comm=None):
    T = u.shape[0]
    tn = _tile(F, 256)
    nf = F // tn

    def body(ug_ref, uv_ref, da_ref, wg_ref, wv_ref, bg_ref, bv_ref, dug_ref, duv_ref, dpg_ref, dpv_ref):
        b = pl.program_id(1)
        ug, uv, da = ug_ref[...], uv_ref[...], da_ref[...]
        g, ug1, ug2 = _conv3(ug, wg_ref, bg_ref[...])
        val, uv1, uv2 = _conv3(uv, wv_ref, bv_ref[...])
        gel, dgel = _gelu_parts(g)
        dg = da * val * dgel
        dval = da * gel
        row = lax.broadcasted_iota(jnp.int32, ug.shape, 0)

        def back(d, w_ref):
            d1 = jnp.where(row < S - 1, pltpu.roll(d, S - 1, 0), 0.0)
            d2 = jnp.where(row < S - 2, pltpu.roll(d, S - 2, 0), 0.0)
            return w_ref[2:3, :] * d + w_ref[1:2, :] * d1 + w_ref[0:1, :] * d2

        dug_ref[...] = back(dg, wg_ref).astype(BF16)
        duv_ref[...] = back(dval, wv_ref).astype(BF16)

        def sums(d, u0, u1, u2):
            r8 = lax.broadcasted_iota(jnp.int32, (8, d.shape[1]), 0)
            out = jnp.zeros((8, d.shape[1]), F32)
            for k, t in enumerate((d * u2, d * u1, d * u0, d)):
                out = jnp.where(r8 == k, jnp.sum(t, axis=0, keepdims=True), out)
            return out

        _accumulate(dpg_ref, sums(dg, ug, ug1, ug2), b == 0)
        _accumulate(dpv_ref, sums(dval, uv, uv1, uv2), b == 0)

    def seq(off):
        return pl.BlockSpec((S, tn), lambda j, b: (b, off + j))

    def par(rows, off):
        return pl.BlockSpec((rows, tn), lambda j, b: (0, off + j))

    outs, landed = _call(
        body, [u, u, dact, cw, cw, cb, cb], name="convffn_bwd", grid=(nf, B),
        in_specs=[seq(0), seq(nf), seq(0), par(3, 0), par(3, nf), par(1, 0), par(1, nf)],
        out_specs=[seq(0), seq(0), par(8, 0), par(8, 0)],
        out_shape=[_sds((T, F), BF16), _sds((T, F), BF16), _sds((8, F), F32), _sds((8, F), F32)],
        sem=("parallel", "arbitrary"), comm=comm)
    return outs, landed


def _tail(ff, x1, tgt, g):
    T, D = ff.shape
    tm = _tile(T, ROW_TILE, 16)

    def body(ff_ref, x1_ref, t_ref, g_ref, dy_ref, dff_ref, loss_ref, dg_ref):
        i = pl.program_id(0)
        f = ff_ref[...]
        gv = g_ref[...]
        r = _rms(f)
        n = f * r
        e = (x1_ref[...] + n * gv) - t_ref[...]
        dy = e * (1.0 / D)
        dy_ref[...] = dy
        dn = dy * gv
        dff_ref[...] = (r * (dn - n * jnp.mean(dn * n, axis=-1, keepdims=True))).astype(BF16)
        part = 0.5 * jnp.sum(jnp.mean(e * e, axis=-1, keepdims=True), axis=0, keepdims=True)
        _accumulate(loss_ref, jnp.broadcast_to(part, loss_ref.shape), i == 0)
        _accumulate(dg_ref, jnp.sum(dy * n, axis=0, keepdims=True), i == 0)

    row = pl.BlockSpec((tm, D), lambda i: (i, 0))
    vec = pl.BlockSpec((1, D), lambda i: (0, 0))
    return _call(body, [ff, x1, tgt, g], name="tail", grid=(T // tm,), in_specs=[row, row, row, vec],
                 out_specs=[row, row, pl.BlockSpec((8, LANES), lambda i: (0, 0)), vec],
                 out_shape=[_sds((T, D), F32), _sds((T, D), BF16), _sds((8, LANES), F32), _sds((1, D), F32)],
                 sem=("arbitrary",))[0]


def _mid_bwd(dy, dh2, x1, y1, g_ffn, g_pm, comm=None):
    T, D = dy.shape
    tm = _tile(T, ROW_TILE, 16)

    def body(dy_ref, dh_ref, x1_ref, y1_ref, gf_ref, gp_ref, dx1_ref, dy1_ref, dgf_ref, dgp_ref):
        i = pl.program_id(0)
        dh = dh_ref[...]
        d2, dgf = _rms_bwd(dh, x1_ref[...], gf_ref[...])
        dx1 = dy_ref[...] + d2
        dx1_ref[...] = dx1
        d1, dgp = _rms_bwd(dx1, y1_ref[...], gp_ref[...])
        dy1_ref[...] = d1.astype(BF16)
        _accumulate(dgf_ref, jnp.sum(dgf, axis=0, keepdims=True), i == 0)
        _accumulate(dgp_ref, jnp.sum(dgp, axis=0, keepdims=True), i == 0)

    row = pl.BlockSpec((tm, D), lambda i: (i, 0))
    vec = pl.BlockSpec((1, D), lambda i: (0, 0))
    return _call(body, [dy, dh2, x1, y1, g_ffn, g_pm], name="mid_bwd", grid=(T // tm,),
                 in_specs=[row, row, row, row, vec, vec], out_specs=[row, row, vec, vec],
                 out_shape=[_sds((T, D), F32), _sds((T, D), BF16), _sds((1, D), F32), _sds((1, D), F32)],
                 sem=("arbitrary",), comm=comm)


def _gate_bwd(dm, am, af, proj, bgate, lay, D, comm=None):
    T = dm.shape[0]
    tm = _tile(T, ROW_TILE, 16)
    tn = _tile(D, 512)

    def body(dm_ref, am_ref, af_ref, gm_ref, gf_ref, bm_ref, bf_ref,
             dam_ref, daf_ref, dgm_ref, dgf_ref, dbm_ref, dbf_ref):
        i = pl.program_id(1)
        d = dm_ref[...]
        sm = _sigmoid(gm_ref[...] + bm_ref[...])
        sf = _sigmoid(gf_ref[...] + bf_ref[...])
        dam_ref[...] = (d * sm).astype(BF16)
        daf_ref[...] = (d * sf).astype(BF16)
        dgm = d * am_ref[...] * (sm * (1.0 - sm))
        dgf = d * af_ref[...] * (sf * (1.0 - sf))
        dgm_ref[...] = dgm.astype(BF16)
        dgf_ref[...] = dgf.astype(BF16)
        _accumulate(dbm_ref, jnp.sum(dgm, axis=0, keepdims=True), i == 0)
        _accumulate(dbf_ref, jnp.sum(dgf, axis=0, keepdims=True), i == 0)

    og = lay["g"] // tn
    blk = pl.BlockSpec((tm, tn), lambda j, i: (i, j))
    vec = pl.BlockSpec((1, tn), lambda j, i: (0, j))
    return _call(
        body, [dm, am, af, proj, proj, bgate, bgate], name="gate_bwd", grid=(D // tn, T // tm),
        in_specs=[blk, blk, blk, pl.BlockSpec((tm, tn), lambda j, i: (i, og + j)),
                  pl.BlockSpec((tm, tn), lambda j, i: (i, og + D // tn + j)),
                  vec, pl.BlockSpec((1, tn), lambda j, i: (0, D // tn + j))],
        out_specs=[blk, blk, blk, blk, vec, vec],
        out_shape=[_sds((T, D), BF16)] * 4 + [_sds((1, D), F32)] * 2, sem=("parallel", "arbitrary"), comm=comm)


def _mla_bwd_prep(dq, dk, dv, cosT, sinT, comm=None):
    H, T, _ = dq.shape
    tm = _tile(T, HEAD_ROW_TILE, 16)

    def body(dq_ref, dk_ref, dv_ref, cos_ref, sin_ref, dqr_ref, dkv_ref, dkpe_ref):
        h = pl.program_id(1)
        cs, sn = cos_ref[...], sin_ref[...]
        valid = _lane(cs.shape) < ROPE

        def unrope(d):
            d = jnp.where(valid, d, 0.0)
            return d * cs - _rope_rot(d) * sn

        dqv = dq_ref[...]
        dqr_ref[:, :NOPE] = dqv[:, :NOPE].astype(BF16)
        dqr_ref[:, NOPE:] = unrope(dqv[:, NOPE:]).astype(BF16)
        dkv_ = dk_ref[...]
        dkv_ref[:, :NOPE] = dkv_[:, :NOPE].astype(BF16)
        dkv_ref[:, NOPE:] = dv_ref[...].astype(BF16)
        _accumulate(dkpe_ref, unrope(dkv_[:, NOPE:]), h == 0)

    head = pl.BlockSpec((None, tm, ATT_DK), lambda i, h: (h, i, 0))
    tok = pl.BlockSpec((tm, LANES), lambda i, h: (i, 0))
    return _call(
        body, [dq, dk, dv, cosT, sinT], name="mla_bwd_prep", grid=(T // tm, H),
        in_specs=[head, head, pl.BlockSpec((None, tm, VDIM), lambda i, h: (h, i, 0)), tok, tok],
        out_specs=[head, head, tok],
        out_shape=[_sds((H, T, ATT_DK), BF16), _sds((H, T, ATT_DK), BF16), _sds((T, LANES), F32)],
        sem=("parallel", "arbitrary"), comm=comm)


def _fox_bwd_prep(dq, dk, proj, bfor, lay, B, S, inv_scale):
    H, T, _ = dq.shape

    def body(dq_ref, dk_ref, fl_ref, bf_ref, dfl_ref, dbf_ref, dc_sc):
        b, h = pl.program_id(0), pl.program_id(1)
        lane = _lane(dc_sc.shape)
        col = jnp.sum(jnp.where(lane == 0, dq_ref[...], 0.0) - jnp.where(lane == 3, dk_ref[...], 0.0),
                      axis=1, keepdims=True)

        @pl.when(h == 0)
        def _():
            dc_sc[...] = jnp.zeros(dc_sc.shape, F32)

        dc_sc[...] = jnp.where(lane == h, col, dc_sc[...])

        @pl.when(h == H - 1)
        def _():
            dlogf = _cumsum_rows(dc_sc[...] * inv_scale, reverse=True)
            z = fl_ref[...] + bf_ref[...]
            dz = jnp.where(lane < H, dlogf * (1.0 / (1.0 + jnp.exp(z))), 0.0)
            dfl_ref[...] = dz
            _accumulate(dbf_ref, jnp.sum(dz, axis=0, keepdims=True), b == 0)

    aug = pl.BlockSpec((None, S, LANES), lambda b, h: (h, b, 1))
    seq = pl.BlockSpec((S, LANES), lambda b, h: (b, 0))
    vec = pl.BlockSpec((1, LANES), lambda b, h: (0, 0))
    return _call(
        body, [dq, dk, proj, bfor], name="fox_bwd_prep", grid=(B, H),
        in_specs=[aug, aug, pl.BlockSpec((S, LANES), lambda b, h: (b, lay["fl"] // LANES)), vec],
        out_specs=[seq, vec], out_shape=[_sds((T, LANES), F32), _sds((1, LANES), F32)],
        scratch_shapes=[pltpu.VMEM((S, LANES), F32)], sem=("arbitrary", "arbitrary"))[0]


def _heads_to_cols(dq, dk, dv):
    H, T, _ = dq.shape
    tm = _tile(T, HEAD_ROW_TILE, 16)

    def body(a_ref, b_ref, c_ref, ao_ref, bo_ref, co_ref):
        ao_ref[...] = a_ref[...].astype(BF16)
        bo_ref[...] = b_ref[...].astype(BF16)
        co_ref[...] = c_ref[...].astype(BF16)

    src = pl.BlockSpec((None, tm, FOX_DIM), lambda i, h: (h, i, 0))
    dst = pl.BlockSpec((tm, FOX_DIM), lambda i, h: (i, h))
    return _call(body, [dq, dk, dv], name="heads_to_cols", grid=(T // tm, H), in_specs=[src, src, src],
                 out_specs=[dst, dst, dst], out_shape=[_sds((T, H * FOX_DIM), BF16)] * 3,
                 sem=("parallel", "parallel"))[0]


def _lat_bwd(dqn, dkvn, proj, gq, gkv, lay):
    T = dqn.shape[0]
    tm = _tile(T, ROW_TILE, 16)

    def body(dq_ref, dkv_ref, q_ref, kv_ref, gq_ref, gkv_ref, dql_ref, dkl_ref, dgq_ref, dgkv_ref):
        i = pl.program_id(0)
        dql, dgq = _rms_bwd(dq_ref[...], q_ref[...], gq_ref[...])
        dkl, dgkv = _rms_bwd(dkv_ref[...], kv_ref[...], gkv_ref[...])
        dql_ref[...] = dql.astype(BF16)
        dkl_ref[...] = dkl.astype(BF16)
        _accumulate(dgq_ref, jnp.sum(dgq, axis=0, keepdims=True), i == 0)
        _accumulate(dgkv_ref, jnp.sum(dgkv, axis=0, keepdims=True), i == 0)

    def blk(width, off=0):
        return pl.BlockSpec((tm, width), lambda i: (i, off // width))

    def vec(width):
        return pl.BlockSpec((1, width), lambda i: (0, 0))

    return _call(
        body, [dqn, dkvn, proj, proj, gq, gkv], name="lat_bwd", grid=(T // tm,),
        in_specs=[blk(Q_LORA), blk(KV_LORA), blk(Q_LORA, lay["q"]), blk(KV_LORA, lay["kv"]), vec(Q_LORA), vec(KV_LORA)],
        out_specs=[blk(Q_LORA), blk(KV_LORA), vec(Q_LORA), vec(KV_LORA)],
        out_shape=[_sds((T, Q_LORA), BF16), _sds((T, KV_LORA), BF16), _sds((1, Q_LORA), F32), _sds((1, KV_LORA), F32)],
        sem=("arbitrary",))[0]


def _final_dx(dx1, dh, x, g, comm=None):
    T, D = x.shape
    tm = _tile(T, ROW_TILE, 16)

    def body(dx1_ref, dh_ref, x_ref, g_ref, dx_ref, dg_ref):
        i = pl.program_id(0)
        d, dg = _rms_bwd(dh_ref[...], x_ref[...], g_ref[...])
        dx_ref[...] = dx1_ref[...] + d
        _accumulate(dg_ref, jnp.sum(dg, axis=0, keepdims=True), i == 0)

    row = pl.BlockSpec((tm, D), lambda i: (i, 0))
    vec = pl.BlockSpec((1, D), lambda i: (0, 0))
    return _call(body, [dx1, dh, x, g], name="final_dx", grid=(T // tm,), in_specs=[row, row, row, vec],
                 out_specs=[row, vec], out_shape=[_sds((T, D), F32), _sds((1, D), F32)], sem=("arbitrary",), comm=comm)


def _chip_sum(pieces, paired, qc, name):
    G, R, C = pieces.shape
    tr = _tile(R, 256, 16)

    def body(qc_ref, g_ref, p_ref, keep_ref, send_ref):
        s = pl.program_id(1)
        tot = g_ref[...] + p_ref[...]

        @pl.when(s == 0)
        def _():
            keep_ref[...] = tot

        @pl.when(s > 0)
        def _():
            send_ref[...] = tot.astype(send_ref.dtype)

    grid_spec = pltpu.PrefetchScalarGridSpec(
        num_scalar_prefetch=1, grid=(R // tr, N_CHIP),
        in_specs=[pl.BlockSpec((None, tr, C), lambda i, s, qc: (2 * (qc[0] ^ s) + qc[1], i, 0)),
                  pl.BlockSpec((None, tr, C), lambda i, s, qc: (qc[0] ^ s, i, 0))],
        out_specs=[pl.BlockSpec((tr, C), lambda i, s, qc: (i, 0)),
                   pl.BlockSpec((None, tr, C), lambda i, s, qc: (jnp.maximum(s - 1, 0), i, 0))])
    send_dtype = BF16 if R >= 16 else pieces.dtype
    return pl.pallas_call(
        body, name=name, grid_spec=grid_spec,
        out_shape=[_sds((R, C), F32), _sds((3, R, C), send_dtype)],
        compiler_params=pltpu.CompilerParams(dimension_semantics=("arbitrary", "arbitrary"),
                                             vmem_limit_bytes=VMEM_LIMIT_BYTES),
    )(qc, pieces, paired)


def _adamw_math(w, g, m, v):
    m = ADAM_B1 * m + (1.0 - ADAM_B1) * g
    v = ADAM_B2 * v + (1.0 - ADAM_B2) * (g * g)
    m_hat = m / (1.0 - ADAM_B1 ** ADAM_STEP)
    v_hat = v / (1.0 - ADAM_B2 ** ADAM_STEP)
    delta = -ADAM_LR * (m_hat / (jnp.sqrt(v_hat) + ADAM_EPS) + ADAM_WD * w)
    return delta, m, v


def _sum_adamw(keep, pieces, w, m, v, name):
    R, C = w.shape
    P = pieces.shape[0]
    tr = _tile(R, 256, 16)

    def body(*refs):
        if keep is None:
            p_ref, w_ref, m_ref, v_ref, g_ref, d_ref, mo_ref, vo_ref = refs
            g = p_ref[0].astype(F32)
            rest = range(1, P)
        else:
            k_ref, p_ref, w_ref, m_ref, v_ref, g_ref, d_ref, mo_ref, vo_ref = refs
            g = k_ref[...]
            rest = range(P)
        for q in rest:
            g = g + p_ref[q].astype(F32)
        g_ref[...] = g
        d_ref[...], mo_ref[...], vo_ref[...] = _adamw_math(w_ref[...], g, m_ref[...], v_ref[...])

    blk = pl.BlockSpec((tr, C), lambda i: (i, 0))
    pblk = pl.BlockSpec((P, tr, C), lambda i: (0, i, 0))
    args = [pieces, w, m, v] if keep is None else [keep, pieces, w, m, v]
    specs = [pblk, blk, blk, blk] if keep is None else [blk, pblk, blk, blk, blk]
    return _call(body, args, name=name, grid=(R // tr,), in_specs=specs, out_specs=[blk] * 4,
                 out_shape=[_sds((R, C), F32)] * 4, sem=("parallel",))[0]


def _layout(D):
    lay = {"q": 0, "kv": Q_LORA, "kpe": Q_LORA + KV_LORA}
    lay["fq"] = lay["kpe"] + LANES
    lay["fk"] = lay["fq"] + HEADS * FOX_DIM
    lay["fv"] = lay["fk"] + HEADS * FOX_DIM
    lay["fl"] = lay["fv"] + HEADS * FOX_DIM
    lay["g"] = lay["fl"] + LANES
    lay["end"] = lay["g"] + 2 * D
    return lay


def kernel(x, positions, pre_mix_norm, w_in, q_a_norm, w_uq, kv_a_norm, w_ukv, b_forget, b_gate, w_branch_mla, w_branch_fox, w_out, post_mix_norm, pre_ffn_norm, w_up, conv_w, conv_b, w_down, post_ffn_norm, loss_target, m_pre_mix_norm, m_w_in, m_q_a_norm, m_w_uq, m_kv_a_norm, m_w_ukv, m_b_forget, m_b_gate, m_w_branch_mla, m_w_branch_fox, m_w_out, m_post_mix_norm, m_pre_ffn_norm, m_w_up, m_conv_w, m_conv_b, m_w_down, m_post_ffn_norm, v_pre_mix_norm, v_w_in, v_q_a_norm, v_w_uq, v_kv_a_norm, v_w_ukv, v_b_forget, v_b_gate, v_w_branch_mla, v_w_branch_fox, v_w_out, v_post_mix_norm, v_pre_ffn_norm, v_w_up, v_conv_w, v_conv_b, v_w_down, v_post_ffn_norm):
    B, S, D = x.shape
    T = B * S
    F = conv_b.shape[0] // 2
    lay = _layout(D)
    n_in = w_in.shape[1]
    d_in = N_DEV * n_in
    seg_a = Q_LORA + KV_LORA + ROPE
    seg_b = 3 * HEADS * FOX_DIM + HEADS
    mla_scale = (NOPE + ROPE) ** -0.5
    fox_scale = FOX_DIM ** -0.5
    ax, ay, ac = (lax.axis_index(a) for a in MESH_AXES)
    qc = jnp.stack([2 * ax + ay, ac]).astype(jnp.int32)

    def row(vec, width=None):
        vec = vec.reshape(1, -1)
        if width is not None and vec.shape[1] < width:
            vec = jnp.pad(vec, ((0, 0), (0, width - vec.shape[1])))
        return vec

    x2 = x.reshape(T, D)
    win_s = _cast_bf16(w_in, "cast_w_in")
    h, (win_g,) = _prenorm(x2, row(pre_mix_norm), comm=_Comm([_GatherPlan([win_s], mid_frac=0.3)]))
    small_s = [_cast_bf16(w, "cast_" + n) for w, n in
               [(w_uq, "w_uq"), (w_ukv, "w_ukv"), (w_branch_mla, "w_branch_mla"), (w_branch_fox, "w_branch_fox"), (w_out, "w_out")]]
    wup_s = _cast_bf16(w_up, "cast_w_up")
    wdown_s = _cast_bf16(w_down, "cast_w_down")

    def shard_cols(lo, hi):
        out = []
        for g in range(lo // n_in, (hi - 1) // n_in + 1):
            out.append(win_g[g][:, max(lo, g * n_in) - g * n_in:min(hi, (g + 1) * n_in) - g * n_in])
        return out

    w_perm = jnp.concatenate(
        shard_cols(0, seg_a) + [jnp.zeros((D, LANES - ROPE), BF16)] + shard_cols(seg_a, seg_a + seg_b)
        + [jnp.zeros((D, LANES - HEADS), BF16)] + shard_cols(seg_a + seg_b, d_in), axis=1)

    tgt = loss_target.reshape(T, D)
    pos = positions.reshape(T, 1)
    inv_freq = 1.0 / (ROPE_THETA ** (jnp.arange(0, ROPE, 2, dtype=F32) / ROPE))
    invf = row(jnp.concatenate([inv_freq, inv_freq]), LANES)
    g_pre, g_q, g_kv = row(pre_mix_norm), row(q_a_norm), row(kv_a_norm)
    g_pm, g_ffn, g_pf = row(post_mix_norm), row(pre_ffn_norm), row(post_ffn_norm)
    bfor = row(b_forget, LANES)
    bgate = row(b_gate)
    cb_full = row(conv_b)

    def own_plan(blocks):
        return _Comm([_GatherOwnPlan(blocks)])

    def pass_plan(gathered):
        return _Comm([_GatherPassPlan(gathered)])

    def pair_plan(gs):
        return _Comm([_PairScatterPlan(gs)])

    def chip_plan(gs):
        return _Comm([_ChipScatterPlan(gs)])

    half_d = D // 2
    proj, landed = _matmul(h, w_perm, mode="nn", name="mm_proj", comm=_Comm(
        [_GatherOwnPlan(small_s[:2] + [conv_w]), _GatherOwnPlan([wup_s], rows=(0, half_d))]))
    early_g, wup_part = landed[:-1], landed[-1:]
    (qn, kvn, kper, logf, cosT, sinT), (wuq_g, wukv_g, cw_g) = _split_prep(
        proj, pos, invf, g_q, g_kv, bfor, lay, comm=pass_plan(early_g))
    wuq_pad = jnp.pad(wuq_g, ((0, 0), (0, 0), (0, ATT_DK - NOPE - ROPE)))
    cw_full = jnp.transpose(cw_g, (1, 0, 2)).reshape(3, 2 * F)

    qraw = _matmul(qn, wuq_pad, mode="nn", name="mm_q", out_blocks=ATT_DK)
    kvraw = _matmul(kvn, wukv_g, mode="nn", name="mm_kv", out_blocks=NOPE + VDIM)
    (q_mla, k_mla, v_mla), branch_half = _mla_prep(qraw, kvraw, kper, cosT, sinT, comm=own_plan(small_s[2:4]))
    cs = _fox_cumsum(logf, B, S, 1.0 / fox_scale)
    (q_fox, k_fox, v_fox), wout_half = _fox_prep(proj, cs, lay, comm=own_plan(small_s[4:5]))
    ((o_mla, lse_mla), (o_fox, lse_fox)), landed = _attn_fwd(
        [(q_mla, k_mla, v_mla, MLA_UNIT, mla_scale), (q_fox, k_fox, v_fox, 1, fox_scale)], B=B, S=S,
        name="attn_fwd", comm=_Comm([_GatherOwnPlan([wup_s], rows=(half_d, D), into=wup_part),
                                     _GatherPassPlan(branch_half + wout_half)]))
    wup_half, (wbm_g, wbf_g, wout_g) = landed[:1], landed[1:]
    wbm = jnp.transpose(wbm_g, (1, 0, 2)).reshape(HEADS * VDIM, D)
    wbf = jnp.transpose(wbf_g, (1, 0, 2)).reshape(HEADS * FOX_DIM, D)
    wout = wout_g.reshape(D, D)
    a_m = _matmul(o_mla, wbm, mode="nn", name="mm_branch_mla")
    a_f = _matmul(o_fox, wbf, mode="nn", name="mm_branch_fox")
    (merged,), (wup_g,) = _gate_merge(a_m, a_f, proj, bgate, lay, D, comm=pass_plan(wup_half))
    n_up = wup_g.shape[2]
    y1 = _matmul(merged, wout, mode="nn", name="mm_out")
    x1, h2 = _mid(x2, y1, g_pm, g_ffn)
    u, wdown_half = _matmul(h2, wup_g, mode="nn", name="mm_up", tn=n_up, comm=own_plan([wdown_s]))
    (act,), (wdown_g,) = _convffn_fwd(u, cw_full, cb_full, B, S, F, comm=pass_plan(wdown_half))
    wdown = wdown_g.reshape(F, D)
    ff = _matmul(act, wdown, mode="nn", name="mm_down", tk=F // 2)
    dy, dff, loss_part, dg_pf = _tail(ff, x1, tgt, g_pf)

    dact = _matmul(dff, wdown, mode="nt", name="mm_dact", tn=F // 4)
    dw_down = _matmul(act, dff, mode="tn", name="mm_dw_down", tm=F // 4, tn=512).reshape(N_DEV, F // N_DEV, D)
    (du_g, du_v, dcp_g, dcp_v), (pa_down,) = _convffn_bwd(u, dact, cw_full, cb_full, B, S, F, comm=pair_plan([dw_down]))
    keep_down, sb_down = _chip_sum(dw_down, pa_down, qc, "chipsum_w_down")
    dh2, (rb_down,) = _matmul_halves((du_g, du_v), wup_g, mode="nt", name="mm_dh2", tm=MM_TILE, comm=chip_plan([sb_down]))
    dw_up = _matmul_halves(h2, (du_g, du_v), mode="tn", name="mm_dw_up", tm=512, tn=n_up, tk=T // 2)
    (dx1, dy1, dg_ffn, dg_pm), _ = _mid_bwd(dy, dh2, x1, y1, g_ffn, g_pm)
    dmerged = _matmul(dy1, wout, mode="nt", name="mm_dmerged")
    dw_out = _matmul(merged, dy1, mode="tn", name="mm_dw_out").reshape(N_DEV, D // N_DEV, D)
    (da_m, da_f, dgl_m, dgl_f, dbg_m, dbg_f), (pa_up,) = _gate_bwd(
        dmerged, a_m, a_f, proj, bgate, lay, D, comm=pair_plan([dw_up]))
    keep_up, sb_up = _chip_sum(dw_up, pa_up, qc, "chipsum_w_up")
    dw_bm = _matmul(o_mla, da_m, mode="tn", name="mm_dw_branch_mla", out_blocks=D // N_DEV)
    dw_bf = _matmul(o_fox, da_f, mode="tn", name="mm_dw_branch_fox", out_blocks=D // N_DEV)
    mix = [dw_out, dw_bm, dw_bf]
    do_mla, pa_mix = _matmul(da_m, wbm, mode="nt", name="mm_do_mla", out_dtype=BF16, comm=pair_plan(mix))
    do_fox = _matmul(da_f, wbf, mode="nt", name="mm_do_fox", out_dtype=BF16)
    mix_sums = [_chip_sum(g, p, qc, "chipsum_" + n) for g, p, n in zip(mix, pa_mix, ["w_out", "w_branch_mla", "w_branch_fox"])]
    ((dq_m, dk_m, dv_m), (dq_f, dk_f, dv_f)), (rb_up,) = _attn_bwd(
        [(q_mla, k_mla, v_mla, o_mla, do_mla, lse_mla, MLA_UNIT, mla_scale),
         (q_fox, k_fox, v_fox, o_fox, do_fox, lse_fox, 1, fox_scale)], B=B, S=S, name="attn_bwd",
        comm=chip_plan([sb_up]))
    (dqraw, dkvraw, dkpe), rb_mix = _mla_bwd_prep(dq_m, dk_m, dv_m, cosT, sinT, comm=chip_plan([s[1] for s in mix_sums]))
    dqn = _matmul(dqraw, wuq_pad, mode="nt", name="mm_dqn")
    dw_uq = _matmul(qn, dqraw, mode="tn", name="mm_dw_uq", out_blocks=ATT_DK)[:, :, :NOPE + ROPE]
    dkvn = _matmul(dkvraw, wukv_g, mode="nt", name="mm_dkvn")
    dw_ukv = _matmul(kvn, dkvraw, mode="tn", name="mm_dw_ukv", out_blocks=NOPE + VDIM)
    dqlat, dkvlat, dg_q, dg_kv = _lat_bwd(dqn, dkvn, proj, g_q, g_kv, lay)
    dfl, dbfor = _fox_bwd_prep(dq_f, dk_f, proj, bfor, lay, B, S, 1.0 / fox_scale)
    dfq, dfk, dfv = _heads_to_cols(dq_f, dk_f, dv_f)
    dproj = _concat_cols([dqlat, dkvlat, dkpe, dfq, dfk, dfv, dfl, dgl_m, dgl_f], "concat_dproj")
    dw_perm = _matmul(h, dproj, mode="tn", name="mm_dw_in")
    segs = [(0, seg_a, 0), (seg_a, seg_a + seg_b, lay["fq"] - seg_a), (seg_a + seg_b, d_in, lay["g"] - seg_a - seg_b)]

    def piece(g):
        lo, hi = g * n_in, (g + 1) * n_in
        parts = [dw_perm[:, max(lo, s0) + sh:min(hi, s1) + sh] for s0, s1, sh in segs if max(lo, s0) < min(hi, s1)]
        return parts[0] if len(parts) == 1 else jnp.concatenate(parts, axis=1)

    dw_in = jnp.stack([piece(g) for g in range(N_DEV)])
    dcw = jnp.transpose(jnp.concatenate([dcp_g[0:3], dcp_v[0:3]], axis=1).reshape(3, N_DEV, (2 * F) // N_DEV), (1, 0, 2))
    late = [dw_in, dw_uq, dw_ukv, dcw]
    pa_late = _exchange_alone(pair_plan(late), "pair_late")
    late_sums = [_chip_sum(g, p, qc, "chipsum_" + n) for g, p, n in zip(late, pa_late, ["w_in", "w_uq", "w_ukv", "conv_w"])]
    dh, rb_late = _matmul(dproj, w_perm, mode="nt", name="mm_dh", tn=2048, tk=2048, comm=chip_plan([s[1] for s in late_sums]))
    (grad_x, dg_pre), _ = _final_dx(dx1, dh, x2, g_pre)

    big_out = {}

    def finish(n, keep, pieces, w, m, v):
        big_out[n] = _sum_adamw(keep, pieces, w, m, v, "adamw_" + n)

    finish("w_down", keep_down, rb_down, w_down, m_w_down, v_w_down)
    finish("w_up", keep_up, rb_up, w_up, m_w_up, v_w_up)
    finish("w_out", mix_sums[0][0], rb_mix[0], w_out, m_w_out, v_w_out)
    finish("w_branch_mla", mix_sums[1][0], rb_mix[1], w_branch_mla, m_w_branch_mla, v_w_branch_mla)
    finish("w_branch_fox", mix_sums[2][0], rb_mix[2], w_branch_fox, m_w_branch_fox, v_w_branch_fox)
    finish("w_in", late_sums[0][0], rb_late[0], w_in, m_w_in, v_w_in)
    finish("w_uq", late_sums[1][0], rb_late[1], w_uq, m_w_uq, v_w_uq)
    finish("w_ukv", late_sums[2][0], rb_late[2], w_ukv, m_w_ukv, v_w_ukv)
    finish("conv_w", late_sums[3][0], rb_late[3], conv_w, m_conv_w, v_conv_w)

    widths = [D, Q_LORA, KV_LORA, LANES, 2 * D, D, D, 2 * F, D]
    small_names = ["pre_mix_norm", "q_a_norm", "kv_a_norm", "b_forget", "b_gate", "post_mix_norm", "pre_ffn_norm",
                   "conv_b", "post_ffn_norm"]
    true_w = [D, Q_LORA, KV_LORA, HEADS, 2 * D, D, D, 2 * F, D]
    dcb = jnp.concatenate([dcp_g[3:4], dcp_v[3:4]], axis=1)
    part = jnp.concatenate([dg_pre, dg_q, dg_kv, dbfor, dbg_m, dbg_f, dg_pm, dg_ffn, dcb, dg_pf], axis=1)

    def pack(vals):
        return jnp.concatenate([row(a, wd) for a, wd in zip(vals, widths)], axis=1)

    sw = pack([pre_mix_norm, q_a_norm, kv_a_norm, b_forget, b_gate, post_mix_norm, pre_ffn_norm, conv_b, post_ffn_norm])
    sm = pack([m_pre_mix_norm, m_q_a_norm, m_kv_a_norm, m_b_forget, m_b_gate, m_post_mix_norm, m_pre_ffn_norm,
               m_conv_b, m_post_ffn_norm])
    sv = pack([v_pre_mix_norm, v_q_a_norm, v_kv_a_norm, v_b_forget, v_b_gate, v_post_mix_norm, v_pre_ffn_norm,
               v_conv_b, v_post_ffn_norm])
    (parts_all,) = _exchange_alone(_Comm([_DirectGatherPlan([part])]), "gather_small")
    sg, sd, smo, svo = _sum_adamw(None, parts_all, sw, sm, sv, "adamw_small")
    small_out = {}
    off = 0
    for n, wd, tw in zip(small_names, widths, true_w):
        small_out[n] = tuple(a[0, off:off + tw] for a in (sg, sd, smo, svo))
        off += wd

    loss = lax.psum(loss_part[0, 0], MESH_AXES)
    order = ["pre_mix_norm", "w_in", "q_a_norm", "w_uq", "kv_a_norm", "w_ukv", "b_forget", "b_gate", "w_branch_mla",
             "w_branch_fox", "w_out", "post_mix_norm", "pre_ffn_norm", "w_up", "conv_w", "conv_b", "w_down",
             "post_ffn_norm"]
    res = {**big_out, **small_out}
    outs = [loss, grad_x.reshape(B, S, D)]
    for kind in range(4):
        outs += [res[n][kind] for n in order]
    return tuple(outs)
```

```python
import math

import jax
import jax.numpy as jnp
from jax import lax
from jax.experimental import pallas as pl
from jax.experimental.pallas import tpu as pltpu

F32 = jnp.float32
BF16 = jnp.bfloat16

N_DEV = 8
N_CHIP = 4
HEADS = 8
NOPE = 128
ROPE = 64
HALF_ROPE = ROPE // 2
VDIM = 128
Q_LORA = 512
KV_LORA = 256
FOX_DIM = 128
ATT_DK = 256
MLA_UNIT = 64
ROPE_THETA = 10000.0
EPS = 1e-6
NEG_INF = -1e30
LANES = 128
LOG2_E = 1.4426950408889634

ADAM_LR = 0.001
ADAM_B1 = 0.9
ADAM_B2 = 0.999
ADAM_EPS = 1e-08
ADAM_WD = 0.01
ADAM_STEP = 10

VMEM_LIMIT_BYTES = 56 * 1024 * 1024
ROW_TILE = 256
HEAD_ROW_TILE = 1024
ATT_TILE = 1024
ATT_SUB = 256
ATT_AHEAD = 3
MM_TILE = 1024

MESH_AXES = ("x", "y", "c")
ANY = pl.BlockSpec(memory_space=pl.ANY)


def _tile(n, pref, align=LANES):
    if n <= pref:
        return n
    t = (pref // align) * align
    while t >= align:
        if n % t == 0:
            return t
        t -= align
    return n


def _sds(shape, dtype):
    return jax.ShapeDtypeStruct(shape, dtype)


def _coords():
    x, y, c = (lax.axis_index(ax) for ax in MESH_AXES)
    return x, y, c


def _chip_rel(x, y, r):
    return (1 - x if r & 2 else x), (1 - y if r & 1 else y)


def _rcopy(src, dst, sems, w, k, dev):
    return pltpu.make_async_remote_copy(src_ref=src, dst_ref=dst, send_sem=sems[0].at[w, k], recv_sem=sems[1].at[w, k],
                                        device_id=dev, device_id_type=pl.DeviceIdType.MESH)


class _GatherPlan:
    def __init__(self, blocks, mid_frac=0.5):
        self.ins = list(blocks)
        self.out_shapes = [_sds((N_DEV,) + b.shape, b.dtype) for b in blocks]
        n = len(blocks)
        self.scratch = [pltpu.SemaphoreType.DMA((n, 7)), pltpu.SemaphoreType.DMA((n, 7)), pltpu.SemaphoreType.DMA((n,))]
        self.mid_frac = mid_frac

    def first(self, ins, outs, sems):
        x, y, c = _coords()
        me = 4 * x + 2 * y + c
        for w in range(len(ins)):
            pltpu.make_async_copy(ins[w], outs[w].at[me], sems[2].at[w]).start()
            _rcopy(ins[w], outs[w].at[me], sems, w, 0, (x, y, 1 - c)).start()
            for r in (1, 2, 3):
                px, py = _chip_rel(x, y, r)
                _rcopy(ins[w], outs[w].at[me], sems, w, r, (px, py, c)).start()

    def mid(self, ins, outs, sems):
        x, y, c = _coords()
        for w in range(len(ins)):
            for r in (1, 2, 3):
                px, py = _chip_rel(x, y, r)
                blk = outs[w].at[4 * px + 2 * py + c]
                _rcopy(ins[w], blk, sems, w, r, (px, py, c)).wait_recv()
                _rcopy(blk, blk, sems, w, 3 + r, (x, y, 1 - c)).start()

    def last(self, ins, outs, sems):
        x, y, c = _coords()
        me = 4 * x + 2 * y + c
        sib = (x, y, 1 - c)
        for w in range(len(ins)):
            _rcopy(ins[w], outs[w].at[4 * x + 2 * y + 1 - c], sems, w, 0, sib).wait_recv()
            for r in (1, 2, 3):
                px, py = _chip_rel(x, y, r)
                blk = outs[w].at[4 * px + 2 * py + 1 - c]
                _rcopy(blk, blk, sems, w, 3 + r, sib).wait_recv()
            for k in range(7):
                _rcopy(ins[w], outs[w].at[me], sems, w, k, sib).wait_send()
            pltpu.make_async_copy(ins[w], outs[w].at[me], sems[2].at[w]).wait()


class _GatherOwnPlan:
    mid = None

    def __init__(self, blocks, rows=None, into=None):
        self.n = len(blocks)
        self.rows = rows
        self.ins = list(blocks) + list(into or [])
        self.out_shapes = [_sds((N_DEV,) + b.shape, b.dtype) for b in blocks]
        self.aliases = [(self.n + i, i) for i in range(len(into or []))]
        n = self.n
        self.scratch = [pltpu.SemaphoreType.DMA((n, 4)), pltpu.SemaphoreType.DMA((n, 4)), pltpu.SemaphoreType.DMA((n,))]

    def _cut(self, ref):
        return ref if self.rows is None else ref.at[pl.ds(self.rows[0], self.rows[1] - self.rows[0])]

    def first(self, ins, outs, sems):
        x, y, c = _coords()
        me = 4 * x + 2 * y + c
        for w in range(self.n):
            src, dst = self._cut(ins[w]), self._cut(outs[w].at[me])
            pltpu.make_async_copy(src, dst, sems[2].at[w]).start()
            _rcopy(src, dst, sems, w, 0, (x, y, 1 - c)).start()
            for r in (1, 2, 3):
                px, py = _chip_rel(x, y, r)
                _rcopy(src, dst, sems, w, r, (px, py, c)).start()

    def last(self, ins, outs, sems):
        x, y, c = _coords()
        me = 4 * x + 2 * y + c
        for w in range(self.n):
            src = self._cut(ins[w])
            cp = _rcopy(src, self._cut(outs[w].at[4 * x + 2 * y + 1 - c]), sems, w, 0, (x, y, 1 - c))
            cp.wait_recv()
            cp.wait_send()
            for r in (1, 2, 3):
                px, py = _chip_rel(x, y, r)
                cp = _rcopy(src, self._cut(outs[w].at[4 * px + 2 * py + c]), sems, w, r, (px, py, c))
                cp.wait_recv()
                cp.wait_send()
            pltpu.make_async_copy(src, self._cut(outs[w].at[me]), sems[2].at[w]).wait()


class _GatherPassPlan:
    mid = None

    def __init__(self, gathered):
        self.ins = list(gathered)
        self.out_shapes = [_sds(g.shape, g.dtype) for g in gathered]
        self.aliases = [(i, i) for i in range(len(gathered))]
        n = len(gathered)
        self.scratch = [pltpu.SemaphoreType.DMA((n, 3)), pltpu.SemaphoreType.DMA((n, 3))]

    def first(self, ins, outs, sems):
        x, y, c = _coords()
        for w in range(len(ins)):
            for r in (1, 2, 3):
                px, py = _chip_rel(x, y, r)
                blk = 4 * px + 2 * py + c
                _rcopy(ins[w].at[blk], outs[w].at[blk], sems, w, r - 1, (x, y, 1 - c)).start()

    def last(self, ins, outs, sems):
        x, y, c = _coords()
        for w in range(len(ins)):
            for r in (1, 2, 3):
                px, py = _chip_rel(x, y, r)
                blk = 4 * px + 2 * py + 1 - c
                cp = _rcopy(ins[w].at[blk], outs[w].at[blk], sems, w, r - 1, (x, y, 1 - c))
                cp.wait_recv()
                cp.wait_send()


class _DirectGatherPlan:
    mid = None

    def __init__(self, blocks):
        self.ins = list(blocks)
        self.out_shapes = [_sds((N_DEV,) + b.shape, b.dtype) for b in blocks]
        n = len(blocks)
        self.scratch = [pltpu.SemaphoreType.DMA((n, 7)), pltpu.SemaphoreType.DMA((n, 7)), pltpu.SemaphoreType.DMA((n,))]

    @staticmethod
    def _peer(x, y, c, r):
        return (1 - x if r & 4 else x), (1 - y if r & 2 else y), (1 - c if r & 1 else c)

    def first(self, ins, outs, sems):
        x, y, c = _coords()
        me = 4 * x + 2 * y + c
        for w in range(len(ins)):
            pltpu.make_async_copy(ins[w], outs[w].at[me], sems[2].at[w]).start()
            for r in range(1, N_DEV):
                _rcopy(ins[w], outs[w].at[me], sems, w, r - 1, self._peer(x, y, c, r)).start()

    def last(self, ins, outs, sems):
        x, y, c = _coords()
        me = 4 * x + 2 * y + c
        for w in range(len(ins)):
            for r in range(1, N_DEV):
                px, py, pc = self._peer(x, y, c, r)
                cp = _rcopy(ins[w], outs[w].at[4 * px + 2 * py + pc], sems, w, r - 1, (px, py, pc))
                cp.wait_recv()
                cp.wait_send()
            pltpu.make_async_copy(ins[w], outs[w].at[me], sems[2].at[w]).wait()


class _PairScatterPlan:
    mid = None

    def __init__(self, pieces):
        self.ins = list(pieces)
        self.out_shapes = [_sds((N_CHIP,) + p.shape[1:], p.dtype) for p in pieces]
        n = len(pieces)
        self.scratch = [pltpu.SemaphoreType.DMA((n, N_CHIP)), pltpu.SemaphoreType.DMA((n, N_CHIP))]

    def _copies(self, ins, outs, sems):
        x, y, c = _coords()
        return [_rcopy(ins[w].at[2 * q + 1 - c], outs[w].at[q], sems, w, q, (x, y, 1 - c))
                for w in range(len(ins)) for q in range(N_CHIP)]

    def first(self, ins, outs, sems):
        for cp in self._copies(ins, outs, sems):
            cp.start()

    def last(self, ins, outs, sems):
        for cp in self._copies(ins, outs, sems):
            cp.wait_recv()
            cp.wait_send()


class _ChipScatterPlan:
    mid = None

    def __init__(self, sums, rows=None, into=None):
        self.n = len(sums)
        self.rows = rows
        self.ins = list(sums) + list(into or [])
        self.out_shapes = [_sds(s.shape, s.dtype) for s in sums]
        self.aliases = [(self.n + i, i) for i in range(len(into or []))]
        self.scratch = [pltpu.SemaphoreType.DMA((self.n, 3)), pltpu.SemaphoreType.DMA((self.n, 3))]

    def _copies(self, ins, outs, sems):
        x, y, c = _coords()
        cps = []
        for w in range(self.n):
            for r in (1, 2, 3):
                px, py = _chip_rel(x, y, r)
                src, dst = ins[w].at[r - 1], outs[w].at[r - 1]
                if self.rows is not None:
                    cut = pl.ds(self.rows[0], self.rows[1] - self.rows[0])
                    src, dst = src.at[cut], dst.at[cut]
                cps.append(_rcopy(src, dst, sems, w, r - 1, (px, py, c)))
        return cps

    def first(self, ins, outs, sems):
        for cp in self._copies(ins, outs, sems):
            cp.start()

    def last(self, ins, outs, sems):
        for cp in self._copies(ins, outs, sems):
            cp.wait_recv()
            cp.wait_send()


class _Comm:
    def __init__(self, plans):
        self.plans = list(plans)
        self.ins = [a for p in self.plans for a in p.ins]
        self.out_shapes = [s for p in self.plans for s in p.out_shapes]
        self.scratch = [s for p in self.plans for s in p.scratch]
        self.aliases = []
        i = o = 0
        for p in self.plans:
            self.aliases += [(i + a, o + b) for a, b in getattr(p, "aliases", [])]
            i, o = i + len(p.ins), o + len(p.out_shapes)

    def _parts(self, ins, outs, sems):
        i = o = s = 0
        for p in self.plans:
            yield p, ins[i:i + len(p.ins)], outs[o:o + len(p.out_shapes)], sems[s:s + len(p.scratch)]
            i, o, s = i + len(p.ins), o + len(p.out_shapes), s + len(p.scratch)

    def begin(self, step, nsteps, ins, outs, sems):
        @pl.when(step == 0)
        def _():
            for p, pi, po, ps in self._parts(ins, outs, sems):
                p.first(pi, po, ps)

        for p, pi, po, ps in self._parts(ins, outs, sems):
            if p.mid is not None:
                @pl.when(step == min(nsteps - 1, int(p.mid_frac * nsteps)))
                def _(p=p, pi=pi, po=po, ps=ps):
                    p.mid(pi, po, ps)

    def end(self, step, nsteps, ins, outs, sems):
        @pl.when(step == nsteps - 1)
        def _():
            for p, pi, po, ps in self._parts(ins, outs, sems):
                p.last(pi, po, ps)


def _call(body, args, *, name, grid, in_specs, out_specs, out_shape, scratch_shapes=(), sem=None, comm=None):
    in_specs, out_specs, out_shape, scratch_shapes = list(in_specs), list(out_specs), list(out_shape), list(scratch_shapes)
    if comm is None:
        res = pl.pallas_call(
            body, name=name, grid=grid, in_specs=in_specs, out_specs=out_specs, out_shape=out_shape,
            scratch_shapes=scratch_shapes,
            compiler_params=pltpu.CompilerParams(dimension_semantics=sem, vmem_limit_bytes=VMEM_LIMIT_BYTES),
        )(*args)
        return list(res), []
    n_in, n_out, n_sc = len(in_specs), len(out_specs), len(scratch_shapes)
    n_ci, n_co = len(comm.ins), len(comm.out_shapes)
    nsteps = math.prod(grid)

    def hosted(*refs):
        ins, cins = refs[:n_in], refs[n_in:n_in + n_ci]
        o0 = n_in + n_ci
        outs, couts = refs[o0:o0 + n_out], refs[o0 + n_out:o0 + n_out + n_co]
        s0 = o0 + n_out + n_co
        scr, csems = refs[s0:s0 + n_sc], refs[s0 + n_sc:]
        step = jnp.int32(0)
        for d in range(len(grid)):
            step = step * grid[d] + pl.program_id(d)
        comm.begin(step, nsteps, cins, couts, csems)
        body(*ins, *outs, *scr)
        comm.end(step, nsteps, cins, couts, csems)

    res = pl.pallas_call(
        hosted, name=name, grid=grid, in_specs=in_specs + [ANY] * n_ci, out_specs=out_specs + [ANY] * n_co,
        out_shape=out_shape + comm.out_shapes, scratch_shapes=scratch_shapes + comm.scratch,
        input_output_aliases={n_in + a: n_out + b for a, b in comm.aliases},
        compiler_params=pltpu.CompilerParams(dimension_semantics=("arbitrary",) * len(grid),
                                             vmem_limit_bytes=VMEM_LIMIT_BYTES, has_side_effects=True),
    )(*args, *comm.ins)
    return list(res[:n_out]), list(res[n_out:])


def _exchange_alone(comm, name):
    def body():
        pass

    return _call(body, [], name=name, grid=(), in_specs=[], out_specs=[], out_shape=[], comm=comm)[1]


def _matmul(a, b, *, mode, name, out_dtype=F32, out_blocks=None, tm=None, tn=None, tk=None, comm=None):
    tm = MM_TILE if tm is None else tm
    tn = MM_TILE if tn is None else tn
    a_blk = a.ndim == 3
    b_blk = b.ndim == 3
    if mode == "nn":
        M, K = a.shape
        N = b.shape[0] * b.shape[2] if b_blk else b.shape[1]
        dims = (((1,), (0,)), ((), ()))
    elif mode == "nt":
        M = a.shape[1] if a_blk else a.shape[0]
        K = a.shape[0] * a.shape[2] if a_blk else a.shape[1]
        N = b.shape[1] if b_blk else b.shape[0]
        dims = (((1,), (1,)), ((), ()))
    else:
        K, M = a.shape
        N = b.shape[0] * b.shape[2] if b_blk else b.shape[1]
        dims = (((0,), (0,)), ((), ()))

    tm = _tile(M, tm)
    tn = _tile(N, tn)
    if mode == "nt" and (a_blk or b_blk):
        tk = a.shape[2] if a_blk else b.shape[2]
    else:
        tk = _tile(K, K if tk is None else tk)
    if mode != "nt" and b_blk:
        tn = _tile(b.shape[2], tn)
    if out_blocks is not None:
        tn = _tile(out_blocks, tn)
    nk = K // tk
    grid = (M // tm, N // tn, nk)

    if mode == "nn":
        a_spec = pl.BlockSpec((tm, tk), lambda i, j, k: (i, k))
        if b_blk:
            rb = b.shape[2] // tn
            b_spec = pl.BlockSpec((None, tk, tn), lambda i, j, k: (j // rb, k, j % rb))
        else:
            b_spec = pl.BlockSpec((tk, tn), lambda i, j, k: (k, j))
    elif mode == "nt":
        if a_blk:
            a_spec = pl.BlockSpec((None, tm, tk), lambda i, j, k: (k, i, 0))
        else:
            a_spec = pl.BlockSpec((tm, tk), lambda i, j, k: (i, k))
        if b_blk:
            b_spec = pl.BlockSpec((None, tn, tk), lambda i, j, k: (k, j, 0))
        else:
            b_spec = pl.BlockSpec((tn, tk), lambda i, j, k: (j, k))
    else:
        a_spec = pl.BlockSpec((tk, tm), lambda i, j, k: (k, i))
        if b_blk:
            rb = b.shape[2] // tn
            b_spec = pl.BlockSpec((None, tk, tn), lambda i, j, k: (j // rb, k, j % rb))
        else:
            b_spec = pl.BlockSpec((tk, tn), lambda i, j, k: (k, j))

    if out_blocks is None:
        o_spec = pl.BlockSpec((tm, tn), lambda i, j, k: (i, j))
        o_shape = _sds((M, N), out_dtype)
    else:
        ro = out_blocks // tn
        o_spec = pl.BlockSpec((None, tm, tn), lambda i, j, k: (j // ro, i, j % ro))
        o_shape = _sds((N // out_blocks, M, out_blocks), out_dtype)

    direct = nk == 1 or out_dtype == F32

    def body(a_ref, b_ref, o_ref, *scratch):
        if nk == 1:
            o_ref[...] = lax.dot_general(a_ref[...], b_ref[...], dims, preferred_element_type=F32).astype(o_ref.dtype)
            return
        acc_ref = o_ref if direct else scratch[0]
        k = pl.program_id(2)

        @pl.when(k == 0)
        def _():
            acc_ref[...] = jnp.zeros(acc_ref.shape, F32)

        acc_ref[...] += lax.dot_general(a_ref[...], b_ref[...], dims, preferred_element_type=F32)
        if not direct:
            @pl.when(k == nk - 1)
            def _():
                o_ref[...] = acc_ref[...].astype(o_ref.dtype)

    scratch = [] if direct else [pltpu.VMEM((tm, tn), F32)]
    outs, landed = _call(body, [a, b], name=name, grid=grid, in_specs=[a_spec, b_spec], out_specs=[o_spec],
                         out_shape=[o_shape], scratch_shapes=scratch, sem=("parallel", "parallel", "arbitrary"), comm=comm)
    return outs[0] if comm is None else (outs[0], landed)


def _matmul_halves(a, b, *, mode, name, tm, tn=None, tk=None, comm=None):
    if mode == "nt":
        lo, hi = a
        M, kh = lo.shape
        G, N, kb = b.shape
        half = kh // kb
        tm, tn = _tile(M, tm), _tile(N, N if tn is None else tn)
        dims = (((1,), (1,)), ((), ()))

        def body(lo_ref, hi_ref, b_ref, o_ref):
            k = pl.program_id(2)

            @pl.when(k == 0)
            def _():
                o_ref[...] = jnp.zeros(o_ref.shape, F32)

            @pl.when(k < half)
            def _():
                o_ref[...] += lax.dot_general(lo_ref[...], b_ref[...], dims, preferred_element_type=F32)

            @pl.when(k >= half)
            def _():
                o_ref[...] += lax.dot_general(hi_ref[...], b_ref[...], dims, preferred_element_type=F32)

        outs, landed = _call(
            body, [lo, hi, b], name=name, grid=(M // tm, N // tn, G),
            in_specs=[pl.BlockSpec((tm, kb), lambda i, j, k: (i, jnp.minimum(k, half - 1))),
                      pl.BlockSpec((tm, kb), lambda i, j, k: (i, jnp.maximum(k - half, 0))),
                      pl.BlockSpec((None, tn, kb), lambda i, j, k: (k, j, 0))],
            out_specs=[pl.BlockSpec((tm, tn), lambda i, j, k: (i, j))], out_shape=[_sds((M, N), F32)],
            sem=("parallel", "parallel", "arbitrary"), comm=comm)
    else:
        lo, hi = b
        K, nh = lo.shape
        M = a.shape[1]
        n = tn
        half = nh // n
        tm, tk = _tile(M, tm), _tile(K, K if tk is None else tk)
        nk = K // tk
        dims = (((0,), (0,)), ((), ()))

        def body(a_ref, lo_ref, hi_ref, o_ref):
            j, k = pl.program_id(1), pl.program_id(2)

            @pl.when(k == 0)
            def _():
                o_ref[...] = jnp.zeros(o_ref.shape, F32)

            @pl.when(j < half)
            def _():
                o_ref[...] += lax.dot_general(a_ref[...], lo_ref[...], dims, preferred_element_type=F32)

            @pl.when(j >= half)
            def _():
                o_ref[...] += lax.dot_general(a_ref[...], hi_ref[...], dims, preferred_element_type=F32)

        outs, landed = _call(
            body, [a, lo, hi], name=name, grid=(M // tm, 2 * half, nk),
            in_specs=[pl.BlockSpec((tk, tm), lambda i, j, k: (k, i)),
                      pl.BlockSpec((tk, n), lambda i, j, k: (jnp.where(j < half, k, nk - 1), jnp.minimum(j, half - 1))),
                      pl.BlockSpec((tk, n), lambda i, j, k: (jnp.where(j >= half, k, 0), jnp.maximum(j - half, 0)))],
            out_specs=[pl.BlockSpec((None, tm, n), lambda i, j, k: (j, i, 0))],
            out_shape=[_sds((2 * half, M, n), F32)], sem=("parallel", "parallel", "arbitrary"), comm=comm)
    return outs[0] if comm is None else (outs[0], landed)


def _rms(x):
    return lax.rsqrt(jnp.mean(x * x, axis=-1, keepdims=True) + EPS)


def _rms_bwd(dy, x, g):
    r = _rms(x)
    n = x * r
    dn = dy * g
    dx = r * (dn - n * jnp.mean(dn * n, axis=-1, keepdims=True))
    return dx, dy * n


def _sigmoid(x):
    return 1.0 / (1.0 + jnp.exp(-x))


def _rope_rot(t):
    return pltpu.roll(t, HALF_ROPE, 1) - pltpu.roll(t, LANES - HALF_ROPE, 1)


def _lane(shape):
    return lax.broadcasted_iota(jnp.int32, shape, 1)


def _split3(x):
    hi = x.astype(BF16).astype(F32)
    r1 = x - hi
    mid = r1.astype(BF16).astype(F32)
    lo = (r1 - mid).astype(BF16).astype(F32)
    return hi, mid, lo


def _cumsum_rows(x, reverse):
    S = x.shape[0]
    bs = min(256, S)
    nb = S // bs
    r = lax.broadcasted_iota(jnp.int32, (bs, bs), 0)
    c = lax.broadcasted_iota(jnp.int32, (bs, bs), 1)
    tri = jnp.where((c >= r) if reverse else (c <= r), 1.0, 0.0).astype(BF16)
    edge = lax.broadcasted_iota(jnp.int32, (bs, x.shape[1]), 0) == (0 if reverse else bs - 1)
    carry = jnp.zeros((1, x.shape[1]), F32)
    outs = [None] * nb
    for bi in (range(nb - 1, -1, -1) if reverse else range(nb)):
        xb = x[bi * bs:(bi + 1) * bs, :]
        acc = carry
        for term in _split3(xb):
            acc = acc + jnp.dot(tri, term.astype(BF16), preferred_element_type=F32)
        outs[bi] = acc
        carry = jnp.sum(jnp.where(edge, acc, 0.0), axis=0, keepdims=True)
    return jnp.concatenate(outs, axis=0) if nb > 1 else outs[0]


def _gelu_parts(x):
    c0 = math.sqrt(2.0 / math.pi)
    inner = c0 * (x + 0.044715 * (x * x * x))
    t = jnp.tanh(inner)
    g = 0.5 * x * (1.0 + t)
    dg = 0.5 * (1.0 + t) + 0.5 * x * (1.0 - t * t) * (c0 * (1.0 + 3.0 * 0.044715 * (x * x)))
    return g, dg


def _accumulate(ref, value, first):
    @pl.when(first)
    def _():
        ref[...] = value

    @pl.when(jnp.logical_not(first))
    def _():
        ref[...] += value


def _cast_bf16(w, name):
    R, C = w.shape
    tr = _tile(R, 512, 16)

    def body(w_ref, o_ref):
        o_ref[...] = w_ref[...].astype(BF16)

    blk = pl.BlockSpec((tr, C), lambda i: (i, 0))
    return _call(body, [w], name=name, grid=(R // tr,), in_specs=[blk], out_specs=[blk],
                 out_shape=[_sds((R, C), BF16)], sem=("parallel",))[0][0]


def _concat_cols(parts, name):
    T = parts[0].shape[-2] if parts[0].ndim == 3 else parts[0].shape[0]
    widths = [p.shape[0] * LANES if p.ndim == 3 else p.shape[1] for p in parts]
    tm = _tile(T, ROW_TILE, 16)

    def body(*refs):
        o_ref = refs[-1]
        off = 0
        for p_ref, p, w in zip(refs[:-1], parts, widths):
            if p.ndim == 3:
                for hd in range(p.shape[0]):
                    o_ref[:, off + hd * LANES:off + (hd + 1) * LANES] = p_ref[hd].astype(BF16)
            else:
                o_ref[:, off:off + w] = p_ref[...].astype(BF16)
            off += w

    def spec(p, w):
        if p.ndim == 3:
            return pl.BlockSpec((p.shape[0], tm, LANES), lambda i: (0, i, 0))
        return pl.BlockSpec((tm, w), lambda i: (i, 0))

    return _call(body, parts, name=name, grid=(T // tm,),
                 in_specs=[spec(p, w) for p, w in zip(parts, widths)],
                 out_specs=[pl.BlockSpec((tm, sum(widths)), lambda i: (i, 0))],
                 out_shape=[_sds((T, sum(widths)), BF16)], sem=("parallel",))[0][0]


def _prenorm(x, g, comm=None):
    T, D = x.shape
    tm = _tile(T, ROW_TILE, 16)

    def body(x_ref, g_ref, h_ref):
        xv = x_ref[...]
        h_ref[...] = (xv * _rms(xv) * g_ref[...]).astype(BF16)

    row = pl.BlockSpec((tm, D), lambda i: (i, 0))
    (h,), landed = _call(body, [x, g], name="prenorm", grid=(T // tm,),
                         in_specs=[row, pl.BlockSpec((1, D), lambda i: (0, 0))], out_specs=[row],
                         out_shape=[_sds((T, D), BF16)], sem=("parallel",), comm=comm)
    return h, landed


def _split_prep(proj, pos, invf, gq, gkv, bfor, lay, comm=None):
    T = proj.shape[0]
    tm = _tile(T, ROW_TILE, 16)

    def body(q_ref, kv_ref, kpe_ref, fl_ref, pos_ref, invf_ref, gq_ref, gkv_ref, bf_ref,
             qn_ref, kvn_ref, kper_ref, logf_ref, cos_ref, sin_ref):
        ql = q_ref[...]
        qn_ref[...] = (ql * _rms(ql) * gq_ref[...]).astype(BF16)
        kl = kv_ref[...]
        kvn_ref[...] = (kl * _rms(kl) * gkv_ref[...]).astype(BF16)
        ang = pos_ref[...].astype(F32) * invf_ref[...]
        valid = _lane(ang.shape) < ROPE
        cs = jnp.where(valid, jnp.cos(ang), 0.0)
        sn = jnp.where(valid, jnp.sin(ang), 0.0)
        cos_ref[...] = cs
        sin_ref[...] = sn
        kp = jnp.where(valid, kpe_ref[...], 0.0)
        kper_ref[...] = (kp * cs + _rope_rot(kp) * sn).astype(BF16)
        z = fl_ref[...] + bf_ref[...]
        logf_ref[...] = jnp.minimum(z, 0.0) - jnp.log(1.0 + jnp.exp(-jnp.abs(z)))

    def col(width, off):
        return pl.BlockSpec((tm, width), lambda i: (i, off // width))

    def vec(width):
        return pl.BlockSpec((1, width), lambda i: (0, 0))

    def out(width):
        return pl.BlockSpec((tm, width), lambda i: (i, 0))

    return _call(
        body, [proj, proj, proj, proj, pos, invf, gq, gkv, bfor], name="split_prep", grid=(T // tm,),
        in_specs=[col(Q_LORA, lay["q"]), col(KV_LORA, lay["kv"]), col(LANES, lay["kpe"]), col(LANES, lay["fl"]),
                  pl.BlockSpec((tm, 1), lambda i: (i, 0)), vec(LANES), vec(Q_LORA), vec(KV_LORA), vec(LANES)],
        out_specs=[out(Q_LORA), out(KV_LORA), out(LANES), out(LANES), out(LANES), out(LANES)],
        out_shape=[_sds((T, Q_LORA), BF16), _sds((T, KV_LORA), BF16), _sds((T, LANES), BF16),
                   _sds((T, LANES), F32), _sds((T, LANES), F32), _sds((T, LANES), F32)],
        sem=("parallel",), comm=comm)


def _mla_prep(qraw, kvraw, kper, cosT, sinT, comm=None):
    H, T, _ = qraw.shape
    tm = _tile(T, HEAD_ROW_TILE, 16)

    def body(q_ref, kv_ref, kpe_ref, cos_ref, sin_ref, qo_ref, ko_ref, vo_ref):
        q = q_ref[...]
        pe = q[:, NOPE:]
        pe = jnp.where(_lane(pe.shape) < ROPE, pe, 0.0)
        qo_ref[:, :NOPE] = q[:, :NOPE].astype(BF16)
        qo_ref[:, NOPE:] = (pe * cos_ref[...] + _rope_rot(pe) * sin_ref[...]).astype(BF16)
        kv = kv_ref[...]
        ko_ref[:, :NOPE] = kv[:, :NOPE].astype(BF16)
        ko_ref[:, NOPE:] = kpe_ref[...]
        vo_ref[...] = kv[:, NOPE:].astype(BF16)

    head = pl.BlockSpec((None, tm, ATT_DK), lambda h, i: (h, i, 0))
    tok = pl.BlockSpec((tm, LANES), lambda h, i: (i, 0))
    return _call(
        body, [qraw, kvraw, kper, cosT, sinT], name="mla_prep", grid=(H, T // tm),
        in_specs=[head, head, tok, tok, tok],
        out_specs=[head, head, pl.BlockSpec((None, tm, VDIM), lambda h, i: (h, i, 0))],
        out_shape=[_sds((H, T, ATT_DK), BF16), _sds((H, T, ATT_DK), BF16), _sds((H, T, VDIM), BF16)],
        sem=("parallel", "parallel"), comm=comm)


def _fox_cumsum(logf, B, S, inv_scale):
    T = logf.shape[0]

    def body(l_ref, c_ref):
        c_ref[...] = _cumsum_rows(l_ref[...], reverse=False) * inv_scale

    seq = pl.BlockSpec((S, LANES), lambda b: (b, 0))
    return _call(body, [logf], name="fox_cumsum", grid=(B,), in_specs=[seq], out_specs=[seq],
                 out_shape=[_sds((T, LANES), F32)], sem=("parallel",))[0][0]


def _fox_prep(proj, cs, lay, comm=None):
    T = proj.shape[0]
    tm = _tile(T, HEAD_ROW_TILE, 16)

    def body(q_ref, k_ref, v_ref, cs_ref, qo_ref, ko_ref, vo_ref):
        h = pl.program_id(0)
        cv = cs_ref[...]
        lane = _lane(cv.shape)
        ccol = jnp.sum(jnp.where(lane == h, cv, 0.0), axis=1, keepdims=True)
        hi, mid, lo = _split3(ccol)
        one = jnp.where(lane < 6, 1.0, 0.0)
        augq = jnp.where(lane == 0, hi, jnp.where(lane == 1, mid, jnp.where(lane == 2, lo, one)))
        augk = jnp.where(lane < 3, 1.0, jnp.where(lane == 3, -hi, jnp.where(lane == 4, -mid, jnp.where(lane == 5, -lo, 0.0))))
        qo_ref[:, :FOX_DIM] = q_ref[...].astype(BF16)
        qo_ref[:, FOX_DIM:] = augq.astype(BF16)
        ko_ref[:, :FOX_DIM] = k_ref[...].astype(BF16)
        ko_ref[:, FOX_DIM:] = augk.astype(BF16)
        vo_ref[...] = v_ref[...].astype(BF16)

    def col(off):
        return pl.BlockSpec((tm, FOX_DIM), lambda h, i: (i, off // FOX_DIM + h))

    head = pl.BlockSpec((None, tm, ATT_DK), lambda h, i: (h, i, 0))
    return _call(
        body, [proj, proj, proj, cs], name="fox_prep", grid=(HEADS, T // tm),
        in_specs=[col(lay["fq"]), col(lay["fk"]), col(lay["fv"]), pl.BlockSpec((tm, LANES), lambda h, i: (i, 0))],
        out_specs=[head, head, pl.BlockSpec((None, tm, VDIM), lambda h, i: (h, i, 0))],
        out_shape=[_sds((HEADS, T, ATT_DK), BF16), _sds((HEADS, T, ATT_DK), BF16), _sds((HEADS, T, VDIM), BF16)],
        sem=("parallel", "parallel"), comm=comm)


def _visible(tq, tk, unit):
    r = lax.broadcasted_iota(jnp.int32, (tq, tk), 0)
    c = lax.broadcasted_iota(jnp.int32, (tq, tk), 1)
    sh = int(math.log2(unit))
    return lax.shift_right_logical(c, sh) <= lax.shift_right_logical(r, sh)


def _attn_fwd(streams, *, B, S, name, comm=None):
    n = len(streams)
    H, T, DK = streams[0][0].shape
    DV = streams[0][2].shape[2]
    tq = _tile(S, ATT_TILE)
    nq = S // tq
    sub = min(ATT_SUB, tq)
    NT = (((1,), (1,)), ((), ()))

    def body(*refs):
        ins, outs, (m_sc, acc_sc) = refs[:3 * n], refs[3 * n:5 * n], refs[5 * n:]
        i, j = pl.program_id(1), pl.program_id(2)

        @pl.when(j == 0)
        def _():
            m_sc[...] = jnp.full(m_sc.shape, NEG_INF, F32)
            acc_sc[...] = jnp.zeros(acc_sc.shape, F32)

        def step(diagonal):
            work = [(t, r) for r in range(tq // sub) for t in range(n)]

            def scores(t, r):
                q_ref, k_ref, _ = ins[3 * t:3 * t + 3]
                kc = (r + 1) * sub if diagonal else tq
                s = lax.dot_general(q_ref[r * sub:(r + 1) * sub, :], k_ref[0:kc, :], NT, preferred_element_type=F32)
                return s * (streams[t][4] * LOG2_E)

            ahead = [scores(*work[w]) for w in range(min(ATT_AHEAD, len(work)))]
            for w, (t, r) in enumerate(work):
                s = ahead.pop(0)
                if w + ATT_AHEAD < len(work):
                    ahead.append(scores(*work[w + ATT_AHEAD]))
                v_ref = ins[3 * t + 2]
                kc = s.shape[1]
                rows = slice(r * sub, (r + 1) * sub)
                if diagonal:
                    own = jnp.where(_visible(sub, sub, streams[t][3]), s[:, kc - sub:], NEG_INF)
                    s = own if kc == sub else jnp.concatenate([s[:, :kc - sub], own], axis=1)
                m_prev = m_sc[t, rows, :]
                mx = s[:, 0:LANES]
                for g in range(1, kc // LANES):
                    mx = jnp.maximum(mx, s[:, g * LANES:(g + 1) * LANES])
                m_new = jnp.maximum(m_prev, jnp.max(mx, axis=1, keepdims=True))
                alpha = jnp.exp2(m_prev - m_new)
                p = jnp.exp2(s - jnp.tile(m_new, (1, kc // LANES))).astype(BF16)
                v_aug = jnp.concatenate([v_ref[0:kc, :], jnp.ones((kc, LANES), BF16)], axis=1)
                acc_sc[t, rows, :] = jnp.tile(alpha, (1, 2)) * acc_sc[t, rows, :] + jnp.dot(
                    p, v_aug, preferred_element_type=F32)
                m_sc[t, rows, :] = m_new

        @pl.when(j < i)
        def _():
            step(False)

        @pl.when(j == i)
        def _():
            step(True)
            for t in range(n):
                o_ref, lse_ref = outs[2 * t:2 * t + 2]
                l = acc_sc[t, :, DV:]
                o_ref[...] = (acc_sc[t, :, :DV] / l).astype(BF16)
                lse_ref[...] = m_sc[t] + jnp.log2(l)

    def qmap(g, i, j):
        return (g % H, (g // H) * nq + i, 0)

    def kmap(g, i, j):
        return (g % H, (g // H) * nq + jnp.minimum(j, i), 0)

    args = [a for st in streams for a in st[:3]]
    outs, landed = _call(
        body, args, name=name, grid=(B * H, nq, nq),
        in_specs=[pl.BlockSpec((None, tq, DK), qmap), pl.BlockSpec((None, tq, DK), kmap),
                  pl.BlockSpec((None, tq, DV), kmap)] * n,
        out_specs=[pl.BlockSpec((tq, DV), lambda g, i, j: ((g // H) * nq + i, g % H)),
                   pl.BlockSpec((None, tq, LANES), qmap)] * n,
        out_shape=[_sds((T, H * DV), BF16), _sds((H, T, LANES), F32)] * n,
        scratch_shapes=[pltpu.VMEM((n, tq, LANES), F32), pltpu.VMEM((n, tq, DV + LANES), F32)],
        sem=("parallel", "parallel", "arbitrary"), comm=comm)
    return [(outs[2 * t], outs[2 * t + 1]) for t in range(n)], landed


def _attn_bwd(streams, *, B, S, name, comm=None):
    n = len(streams)
    H, T, DK = streams[0][0].shape
    DV = streams[0][2].shape[2]
    tq = _tile(S, ATT_TILE)
    nq = S // tq
    sub = min(ATT_SUB, tq)
    NT = (((1,), (1,)), ((), ()))
    TN = (((0,), (0,)), ((), ()))

    def body(*refs):
        ins, outs = refs[:6 * n], refs[6 * n:]
        j, i = pl.program_id(1), pl.program_id(2)

        @pl.when(jnp.logical_and(j == 0, i == 0))
        def _():
            for t in range(n):
                outs[3 * t][...] = jnp.zeros(outs[3 * t].shape, F32)

        @pl.when(i == 0)
        def _():
            for t in range(n):
                outs[3 * t + 1][...] = jnp.zeros(outs[3 * t + 1].shape, F32)
                outs[3 * t + 2][...] = jnp.zeros(outs[3 * t + 2].shape, F32)

        def step(diagonal):
            work = [(t, r) for r in range(tq // sub) for t in range(n)]

            def kcols(r):
                return (r + 1) * sub if diagonal else tq

            def scores(t, r):
                q_ref, k_ref, v_ref, _, do_ref, _ = ins[6 * t:6 * t + 6]
                rows, kc = slice(r * sub, (r + 1) * sub), kcols(r)
                s = lax.dot_general(q_ref[rows, :], k_ref[0:kc, :], NT, preferred_element_type=F32)
                dp = lax.dot_general(do_ref[rows, :], v_ref[0:kc, :], NT, preferred_element_type=F32)
                return s * (streams[t][7] * LOG2_E), dp

            def probs(t, r, s, dp):
                _, _, _, o_ref, do_ref, lse_ref = ins[6 * t:6 * t + 6]
                rows, kc = slice(r * sub, (r + 1) * sub), kcols(r)
                if diagonal:
                    own = jnp.where(_visible(sub, sub, streams[t][6]), s[:, kc - sub:], NEG_INF)
                    s = own if kc == sub else jnp.concatenate([s[:, :kc - sub], own], axis=1)
                p = jnp.exp2(s - jnp.tile(lse_ref[rows, :], (1, kc // LANES)))
                delta = jnp.sum(do_ref[rows, :].astype(F32) * o_ref[rows, :].astype(F32), axis=1, keepdims=True)
                return p.astype(BF16), (p * (dp - delta) * streams[t][7]).astype(BF16)

            def grads(t, r, p, ds):
                q_ref, k_ref, _, _, do_ref, _ = ins[6 * t:6 * t + 6]
                dq_ref, dk_ref, dv_ref = outs[3 * t:3 * t + 3]
                rows, kc = slice(r * sub, (r + 1) * sub), kcols(r)
                dv_ref[0:kc, :] += lax.dot_general(p, do_ref[rows, :], TN, preferred_element_type=F32)
                dk_ref[0:kc, :] += lax.dot_general(ds, q_ref[rows, :], TN, preferred_element_type=F32)
                qrows = pl.ds(pl.multiple_of(i * tq + r * sub, sub), sub)
                dq_ref[qrows, :] += jnp.dot(ds, k_ref[0:kc, :], preferred_element_type=F32)

            nw = len(work)
            sc = {w: scores(*work[w]) for w in range(min(2, nw))}
            pr = {0: probs(*work[0], *sc.pop(0))}
            for w in range(nw):
                if w + 2 < nw:
                    sc[w + 2] = scores(*work[w + 2])
                if w + 1 < nw:
                    pr[w + 1] = probs(*work[w + 1], *sc.pop(w + 1))
                grads(*work[w], *pr.pop(w))

        @pl.when(i > j)
        def _():
            step(False)

        @pl.when(i == j)
        def _():
            step(True)

    def qmap(g, j, i):
        return (g % H, (g // H) * nq + jnp.maximum(i, j), 0)

    def kmap(g, j, i):
        return (g % H, (g // H) * nq + j, 0)

    def omap(g, j, i):
        return ((g // H) * nq + jnp.maximum(i, j), g % H)

    args = [a for st in streams for a in st[:6]]
    outs, landed = _call(
        body, args, name=name, grid=(B * H, nq, nq),
        in_specs=[pl.BlockSpec((None, tq, DK), qmap), pl.BlockSpec((None, tq, DK), kmap),
                  pl.BlockSpec((None, tq, DV), kmap), pl.BlockSpec((tq, DV), omap), pl.BlockSpec((tq, DV), omap),
                  pl.BlockSpec((None, tq, LANES), qmap)] * n,
        out_specs=[pl.BlockSpec((None, S, DK), lambda g, j, i: (g % H, g // H, 0)),
                   pl.BlockSpec((None, tq, DK), kmap), pl.BlockSpec((None, tq, DV), kmap)] * n,
        out_shape=[_sds((H, T, DK), F32), _sds((H, T, DK), F32), _sds((H, T, DV), F32)] * n,
        sem=("parallel", "arbitrary", "arbitrary"), comm=comm)
    return [tuple(outs[3 * t:3 * t + 3]) for t in range(n)], landed


def _gate_merge(am, af, proj, bgate, lay, D, comm=None):
    T = am.shape[0]
    tm = _tile(T, ROW_TILE, 16)
    tn = _tile(D, 1024)

    def body(am_ref, af_ref, gm_ref, gf_ref, bm_ref, bf_ref, o_ref):
        sm = _sigmoid(gm_ref[...] + bm_ref[...])
        sf = _sigmoid(gf_ref[...] + bf_ref[...])
        o_ref[...] = (sm * am_ref[...] + sf * af_ref[...]).astype(BF16)

    og = lay["g"] // tn
    blk = pl.BlockSpec((tm, tn), lambda i, j: (i, j))
    return _call(
        body, [am, af, proj, proj, bgate, bgate], name="gate_merge", grid=(T // tm, D // tn),
        in_specs=[blk, blk, pl.BlockSpec((tm, tn), lambda i, j: (i, og + j)),
                  pl.BlockSpec((tm, tn), lambda i, j: (i, og + D // tn + j)),
                  pl.BlockSpec((1, tn), lambda i, j: (0, j)), pl.BlockSpec((1, tn), lambda i, j: (0, D // tn + j))],
        out_specs=[blk], out_shape=[_sds((T, D), BF16)], sem=("parallel", "parallel"), comm=comm)


def _mid(x, y1, g_pm, g_ffn):
    T, D = x.shape
    tm = _tile(T, ROW_TILE, 16)

    def body(x_ref, y_ref, gp_ref, gf_ref, x1_ref, h2_ref):
        y = y_ref[...]
        x1 = x_ref[...] + y * _rms(y) * gp_ref[...]
        x1_ref[...] = x1
        h2_ref[...] = (x1 * _rms(x1) * gf_ref[...]).astype(BF16)

    row = pl.BlockSpec((tm, D), lambda i: (i, 0))
    vec = pl.BlockSpec((1, D), lambda i: (0, 0))
    return _call(body, [x, y1, g_pm, g_ffn], name="mid", grid=(T // tm,), in_specs=[row, row, vec, vec],
                 out_specs=[row, row], out_shape=[_sds((T, D), F32), _sds((T, D), BF16)], sem=("parallel",))[0]


def _conv3(u, w_ref, bias):
    row = lax.broadcasted_iota(jnp.int32, u.shape, 0)
    u1 = jnp.where(row >= 1, pltpu.roll(u, 1, 0), 0.0)
    u2 = jnp.where(row >= 2, pltpu.roll(u, 2, 0), 0.0)
    return w_ref[0:1, :] * u2 + w_ref[1:2, :] * u1 + w_ref[2:3, :] * u + bias, u1, u2


def _convffn_fwd(u, cw, cb, B, S, F, comm=None):
    T = u.shape[0]
    tn = _tile(F, 256)
    nf = F // tn

    def body(ug_ref, uv_ref, wg_ref, wv_ref, bg_ref, bv_ref, a_ref):
        g, _, _ = _conv3(ug_ref[...], wg_ref, bg_ref[...])
        val, _, _ = _conv3(uv_ref[...], wv_ref, bv_ref[...])
        a_ref[...] = (_gelu_parts(g)[0] * val).astype(BF16)

    def seq(off):
        return pl.BlockSpec((S, tn), lambda b, j: (b, off + j))

    def par(rows, off):
        return pl.BlockSpec((rows, tn), lambda b, j: (0, off + j))

    return _call(body, [u, u, cw, cw, cb, cb], name="convffn_fwd", grid=(B, nf),
                 in_specs=[seq(0), seq(nf), par(3, 0), par(3, nf), par(1, 0), par(1, nf)],
                 out_specs=[seq(0)], out_shape=[_sds((T, F), BF16)], sem=("parallel", "parallel"), comm=comm)


def _convffn_bwd(u, dact, cw, cb, B, S, F, comm=None):
    T = u.shape[0]
    tn = _tile(F, 256)
    nf = F // tn

    def body(ug_ref, uv_ref, da_ref, wg_ref, wv_ref, bg_ref, bv_ref, dug_ref, duv_ref, dpg_ref, dpv_ref):
        b = pl.program_id(1)
        ug, uv, da = ug_ref[...], uv_ref[...], da_ref[...]
        g, ug1, ug2 = _conv3(ug, wg_ref, bg_ref[...])
        val, uv1, uv2 = _conv3(uv, wv_ref, bv_ref[...])
        gel, dgel = _gelu_parts(g)
        dg = da * val * dgel
        dval = da * gel
        row = lax.broadcasted_iota(jnp.int32, ug.shape, 0)

        def back(d, w_ref):
            d1 = jnp.where(row < S - 1, pltpu.roll(d, S - 1, 0), 0.0)
            d2 = jnp.where(row < S - 2, pltpu.roll(d, S - 2, 0), 0.0)
            return w_ref[2:3, :] * d + w_ref[1:2, :] * d1 + w_ref[0:1, :] * d2

        dug_ref[...] = back(dg, wg_ref).astype(BF16)
        duv_ref[...] = back(dval, wv_ref).astype(BF16)

        def sums(d, u0, u1, u2):
            r8 = lax.broadcasted_iota(jnp.int32, (8, d.shape[1]), 0)
            out = jnp.zeros((8, d.shape[1]), F32)
            for k, t in enumerate((d * u2, d * u1, d * u0, d)):
                out = jnp.where(r8 == k, jnp.sum(t, axis=0, keepdims=True), out)
            return out

        _accumulate(dpg_ref, sums(dg, ug, ug1, ug2), b == 0)
        _accumulate(dpv_ref, sums(dval, uv, uv1, uv2), b == 0)

    def seq(off):
        return pl.BlockSpec((S, tn), lambda j, b: (b, off + j))

    def par(rows, off):
        return pl.BlockSpec((rows, tn), lambda j, b: (0, off + j))

    outs, landed = _call(
        body, [u, u, dact, cw, cw, cb, cb], name="convffn_bwd", grid=(nf, B),
        in_specs=[seq(0), seq(nf), seq(0), par(3, 0), par(3, nf), par(1, 0), par(1, nf)],
        out_specs=[seq(0), seq(0), par(8, 0), par(8, 0)],
        out_shape=[_sds((T, F), BF16), _sds((T, F), BF16), _sds((8, F), F32), _sds((8, F), F32)],
        sem=("parallel", "arbitrary"), comm=comm)
    return outs, landed


def _tail(ff, x1, tgt, g):
    T, D = ff.shape
    tm = _tile(T, ROW_TILE, 16)

    def body(ff_ref, x1_ref, t_ref, g_ref, dy_ref, dff_ref, loss_ref, dg_ref):
        i = pl.program_id(0)
        f = ff_ref[...]
        gv = g_ref[...]
        r = _rms(f)
        n = f * r
        e = (x1_ref[...] + n * gv) - t_ref[...]
        dy = e * (1.0 / D)
        dy_ref[...] = dy
        dn = dy * gv
        dff_ref[...] = (r * (dn - n * jnp.mean(dn * n, axis=-1, keepdims=True))).astype(BF16)
        part = 0.5 * jnp.sum(jnp.mean(e * e, axis=-1, keepdims=True), axis=0, keepdims=True)
        _accumulate(loss_ref, jnp.broadcast_to(part, loss_ref.shape), i == 0)
        _accumulate(dg_ref, jnp.sum(dy * n, axis=0, keepdims=True), i == 0)

    row = pl.BlockSpec((tm, D), lambda i: (i, 0))
    vec = pl.BlockSpec((1, D), lambda i: (0, 0))
    return _call(body, [ff, x1, tgt, g], name="tail", grid=(T // tm,), in_specs=[row, row, row, vec],
                 out_specs=[row, row, pl.BlockSpec((8, LANES), lambda i: (0, 0)), vec],
                 out_shape=[_sds((T, D), F32), _sds((T, D), BF16), _sds((8, LANES), F32), _sds((1, D), F32)],
                 sem=("arbitrary",))[0]


def _mid_bwd(dy, dh2, x1, y1, g_ffn, g_pm, comm=None):
    T, D = dy.shape
    tm = _tile(T, ROW_TILE, 16)

    def body(dy_ref, dh_ref, x1_ref, y1_ref, gf_ref, gp_ref, dx1_ref, dy1_ref, dgf_ref, dgp_ref):
        i = pl.program_id(0)
        dh = dh_ref[...]
        d2, dgf = _rms_bwd(dh, x1_ref[...], gf_ref[...])
        dx1 = dy_ref[...] + d2
        dx1_ref[...] = dx1
        d1, dgp = _rms_bwd(dx1, y1_ref[...], gp_ref[...])
        dy1_ref[...] = d1.astype(BF16)
        _accumulate(dgf_ref, jnp.sum(dgf, axis=0, keepdims=True), i == 0)
        _accumulate(dgp_ref, jnp.sum(dgp, axis=0, keepdims=True), i == 0)

    row = pl.BlockSpec((tm, D), lambda i: (i, 0))
    vec = pl.BlockSpec((1, D), lambda i: (0, 0))
    return _call(body, [dy, dh2, x1, y1, g_ffn, g_pm], name="mid_bwd", grid=(T // tm,),
                 in_specs=[row, row, row, row, vec, vec], out_specs=[row, row, vec, vec],
                 out_shape=[_sds((T, D), F32), _sds((T, D), BF16), _sds((1, D), F32), _sds((1, D), F32)],
                 sem=("arbitrary",), comm=comm)


def _gate_bwd(dm, am, af, proj, bgate, lay, D, comm=None):
    T = dm.shape[0]
    tm = _tile(T, ROW_TILE, 16)
    tn = _tile(D, 512)

    def body(dm_ref, am_ref, af_ref, gm_ref, gf_ref, bm_ref, bf_ref,
             dam_ref, daf_ref, dgm_ref, dgf_ref, dbm_ref, dbf_ref):
        i = pl.program_id(1)
        d = dm_ref[...]
        sm = _sigmoid(gm_ref[...] + bm_ref[...])
        sf = _sigmoid(gf_ref[...] + bf_ref[...])
        dam_ref[...] = (d * sm).astype(BF16)
        daf_ref[...] = (d * sf).astype(BF16)
        dgm = d * am_ref[...] * (sm * (1.0 - sm))
        dgf = d * af_ref[...] * (sf * (1.0 - sf))
        dgm_ref[...] = dgm.astype(BF16)
        dgf_ref[...] = dgf.astype(BF16)
        _accumulate(dbm_ref, jnp.sum(dgm, axis=0, keepdims=True), i == 0)
        _accumulate(dbf_ref, jnp.sum(dgf, axis=0, keepdims=True), i == 0)

    og = lay["g"] // tn
    blk = pl.BlockSpec((tm, tn), lambda j, i: (i, j))
    vec = pl.BlockSpec((1, tn), lambda j, i: (0, j))
    return _call(
        body, [dm, am, af, proj, proj, bgate, bgate], name="gate_bwd", grid=(D // tn, T // tm),
        in_specs=[blk, blk, blk, pl.BlockSpec((tm, tn), lambda j, i: (i, og + j)),
                  pl.BlockSpec((tm, tn), lambda j, i: (i, og + D // tn + j)),
                  vec, pl.BlockSpec((1, tn), lambda j, i: (0, D // tn + j))],
        out_specs=[blk, blk, blk, blk, vec, vec],
        out_shape=[_sds((T, D), BF16)] * 4 + [_sds((1, D), F32)] * 2, sem=("parallel", "arbitrary"), comm=comm)


def _mla_bwd_prep(dq, dk, dv, cosT, sinT, comm=None):
    H, T, _ = dq.shape
    tm = _tile(T, HEAD_ROW_TILE, 16)

    def body(dq_ref, dk_ref, dv_ref, cos_ref, sin_ref, dqr_ref, dkv_ref, dkpe_ref):
        h = pl.program_id(1)
        cs, sn = cos_ref[...], sin_ref[...]
        valid = _lane(cs.shape) < ROPE

        def unrope(d):
            d = jnp.where(valid, d, 0.0)
            return d * cs - _rope_rot(d) * sn

        dqv = dq_ref[...]
        dqr_ref[:, :NOPE] = dqv[:, :NOPE].astype(BF16)
        dqr_ref[:, NOPE:] = unrope(dqv[:, NOPE:]).astype(BF16)
        dkv_ = dk_ref[...]
        dkv_ref[:, :NOPE] = dkv_[:, :NOPE].astype(BF16)
        dkv_ref[:, NOPE:] = dv_ref[...].astype(BF16)
        _accumulate(dkpe_ref, unrope(dkv_[:, NOPE:]), h == 0)

    head = pl.BlockSpec((None, tm, ATT_DK), lambda i, h: (h, i, 0))
    tok = pl.BlockSpec((tm, LANES), lambda i, h: (i, 0))
    return _call(
        body, [dq, dk, dv, cosT, sinT], name="mla_bwd_prep", grid=(T // tm, H),
        in_specs=[head, head, pl.BlockSpec((None, tm, VDIM), lambda i, h: (h, i, 0)), tok, tok],
        out_specs=[head, head, tok],
        out_shape=[_sds((H, T, ATT_DK), BF16), _sds((H, T, ATT_DK), BF16), _sds((T, LANES), F32)],
        sem=("parallel", "arbitrary"), comm=comm)


def _fox_bwd_prep(dq, dk, proj, bfor, lay, B, S, inv_scale):
    H, T, _ = dq.shape

    def body(dq_ref, dk_ref, fl_ref, bf_ref, dfl_ref, dbf_ref, dc_sc):
        b, h = pl.program_id(0), pl.program_id(1)
        lane = _lane(dc_sc.shape)
        col = jnp.sum(jnp.where(lane == 0, dq_ref[...], 0.0) - jnp.where(lane == 3, dk_ref[...], 0.0),
                      axis=1, keepdims=True)

        @pl.when(h == 0)
        def _():
            dc_sc[...] = jnp.zeros(dc_sc.shape, F32)

        dc_sc[...] = jnp.where(lane == h, col, dc_sc[...])

        @pl.when(h == H - 1)
        def _():
            dlogf = _cumsum_rows(dc_sc[...] * inv_scale, reverse=True)
            z = fl_ref[...] + bf_ref[...]
            dz = jnp.where(lane < H, dlogf * (1.0 / (1.0 + jnp.exp(z))), 0.0)
            dfl_ref[...] = dz
            _accumulate(dbf_ref, jnp.sum(dz, axis=0, keepdims=True), b == 0)

    aug = pl.BlockSpec((None, S, LANES), lambda b, h: (h, b, 1))
    seq = pl.BlockSpec((S, LANES), lambda b, h: (b, 0))
    vec = pl.BlockSpec((1, LANES), lambda b, h: (0, 0))
    return _call(
        body, [dq, dk, proj, bfor], name="fox_bwd_prep", grid=(B, H),
        in_specs=[aug, aug, pl.BlockSpec((S, LANES), lambda b, h: (b, lay["fl"] // LANES)), vec],
        out_specs=[seq, vec], out_shape=[_sds((T, LANES), F32), _sds((1, LANES), F32)],
        scratch_shapes=[pltpu.VMEM((S, LANES), F32)], sem=("arbitrary", "arbitrary"))[0]


def _lat_bwd(dqn, dkvn, proj, gq, gkv, lay):
    T = dqn.shape[0]
    tm = _tile(T, ROW_TILE, 16)

    def body(dq_ref, dkv_ref, q_ref, kv_ref, gq_ref, gkv_ref, dql_ref, dkl_ref, dgq_ref, dgkv_ref):
        i = pl.program_id(0)
        dql, dgq = _rms_bwd(dq_ref[...], q_ref[...], gq_ref[...])
        dkl, dgkv = _rms_bwd(dkv_ref[...], kv_ref[...], gkv_ref[...])
        dql_ref[...] = dql.astype(BF16)
        dkl_ref[...] = dkl.astype(BF16)
        _accumulate(dgq_ref, jnp.sum(dgq, axis=0, keepdims=True), i == 0)
        _accumulate(dgkv_ref, jnp.sum(dgkv, axis=0, keepdims=True), i == 0)

    def blk(width, off=0):
        return pl.BlockSpec((tm, width), lambda i: (i, off // width))

    def vec(width):
        return pl.BlockSpec((1, width), lambda i: (0, 0))

    return _call(
        body, [dqn, dkvn, proj, proj, gq, gkv], name="lat_bwd", grid=(T // tm,),
        in_specs=[blk(Q_LORA), blk(KV_LORA), blk(Q_LORA, lay["q"]), blk(KV_LORA, lay["kv"]), vec(Q_LORA), vec(KV_LORA)],
        out_specs=[blk(Q_LORA), blk(KV_LORA), vec(Q_LORA), vec(KV_LORA)],
        out_shape=[_sds((T, Q_LORA), BF16), _sds((T, KV_LORA), BF16), _sds((1, Q_LORA), F32), _sds((1, KV_LORA), F32)],
        sem=("arbitrary",))[0]


def _final_dx(dx1, dh, x, g, comm=None):
    T, D = x.shape
    tm = _tile(T, ROW_TILE, 16)

    def body(dx1_ref, dh_ref, x_ref, g_ref, dx_ref, dg_ref):
        i = pl.program_id(0)
        d, dg = _rms_bwd(dh_ref[...], x_ref[...], g_ref[...])
        dx_ref[...] = dx1_ref[...] + d
        _accumulate(dg_ref, jnp.sum(dg, axis=0, keepdims=True), i == 0)

    row = pl.BlockSpec((tm, D), lambda i: (i, 0))
    vec = pl.BlockSpec((1, D), lambda i: (0, 0))
    return _call(body, [dx1, dh, x, g], name="final_dx", grid=(T // tm,), in_specs=[row, row, row, vec],
                 out_specs=[row, vec], out_shape=[_sds((T, D), F32), _sds((1, D), F32)], sem=("arbitrary",), comm=comm)


def _chip_sum(pieces, paired, qc, name):
    G, R, C = pieces.shape
    tr = _tile(R, 256, 16)

    def body(qc_ref, g_ref, p_ref, keep_ref, send_ref):
        s = pl.program_id(1)
        tot = g_ref[...] + p_ref[...]

        @pl.when(s == 0)
        def _():
            keep_ref[...] = tot

        @pl.when(s > 0)
        def _():
            send_ref[...] = tot.astype(send_ref.dtype)

    grid_spec = pltpu.PrefetchScalarGridSpec(
        num_scalar_prefetch=1, grid=(R // tr, N_CHIP),
        in_specs=[pl.BlockSpec((None, tr, C), lambda i, s, qc: (2 * (qc[0] ^ s) + qc[1], i, 0)),
                  pl.BlockSpec((None, tr, C), lambda i, s, qc: (qc[0] ^ s, i, 0))],
        out_specs=[pl.BlockSpec((tr, C), lambda i, s, qc: (i, 0)),
                   pl.BlockSpec((None, tr, C), lambda i, s, qc: (jnp.maximum(s - 1, 0), i, 0))])
    send_dtype = BF16 if R >= 16 else pieces.dtype
    return pl.pallas_call(
        body, name=name, grid_spec=grid_spec,
        out_shape=[_sds((R, C), F32), _sds((3, R, C), send_dtype)],
        compiler_params=pltpu.CompilerParams(dimension_semantics=("arbitrary", "arbitrary"),
                                             vmem_limit_bytes=VMEM_LIMIT_BYTES),
    )(qc, pieces, paired)


def _adamw_math(w, g, m, v):
    m = ADAM_B1 * m + (1.0 - ADAM_B1) * g
    v = ADAM_B2 * v + (1.0 - ADAM_B2) * (g * g)
    m_hat = m / (1.0 - ADAM_B1 ** ADAM_STEP)
    v_hat = v / (1.0 - ADAM_B2 ** ADAM_STEP)
    delta = -ADAM_LR * (m_hat / (jnp.sqrt(v_hat) + ADAM_EPS) + ADAM_WD * w)
    return delta, m, v


def _sum_adamw(keep, pieces, w, m, v, name):
    R, C = w.shape
    P = pieces.shape[0]
    tr = _tile(R, 256, 16)

    def body(*refs):
        if keep is None:
            p_ref, w_ref, m_ref, v_ref, g_ref, d_ref, mo_ref, vo_ref = refs
            g = p_ref[0].astype(F32)
            rest = range(1, P)
        else:
            k_ref, p_ref, w_ref, m_ref, v_ref, g_ref, d_ref, mo_ref, vo_ref = refs
            g = k_ref[...]
            rest = range(P)
        for q in rest:
            g = g + p_ref[q].astype(F32)
        g_ref[...] = g
        d_ref[...], mo_ref[...], vo_ref[...] = _adamw_math(w_ref[...], g, m_ref[...], v_ref[...])

    blk = pl.BlockSpec((tr, C), lambda i: (i, 0))
    pblk = pl.BlockSpec((P, tr, C), lambda i: (0, i, 0))
    args = [pieces, w, m, v] if keep is None else [keep, pieces, w, m, v]
    specs = [pblk, blk, blk, blk] if keep is None else [blk, pblk, blk, blk, blk]
    return _call(body, args, name=name, grid=(R // tr,), in_specs=specs, out_specs=[blk] * 4,
                 out_shape=[_sds((R, C), F32)] * 4, sem=("parallel",))[0]


def _layout(D):
    lay = {"q": 0, "kv": Q_LORA, "kpe": Q_LORA + KV_LORA}
    lay["fq"] = lay["kpe"] + LANES
    lay["fk"] = lay["fq"] + HEADS * FOX_DIM
    lay["fv"] = lay["fk"] + HEADS * FOX_DIM
    lay["fl"] = lay["fv"] + HEADS * FOX_DIM
    lay["g"] = lay["fl"] + LANES
    lay["end"] = lay["g"] + 2 * D
    return lay


def kernel(x, positions, pre_mix_norm, w_in, q_a_norm, w_uq, kv_a_norm, w_ukv, b_forget, b_gate, w_branch_mla, w_branch_fox, w_out, post_mix_norm, pre_ffn_norm, w_up, conv_w, conv_b, w_down, post_ffn_norm, loss_target, m_pre_mix_norm, m_w_in, m_q_a_norm, m_w_uq, m_kv_a_norm, m_w_ukv, m_b_forget, m_b_gate, m_w_branch_mla, m_w_branch_fox, m_w_out, m_post_mix_norm, m_pre_ffn_norm, m_w_up, m_conv_w, m_conv_b, m_w_down, m_post_ffn_norm, v_pre_mix_norm, v_w_in, v_q_a_norm, v_w_uq, v_kv_a_norm, v_w_ukv, v_b_forget, v_b_gate, v_w_branch_mla, v_w_branch_fox, v_w_out, v_post_mix_norm, v_pre_ffn_norm, v_w_up, v_conv_w, v_conv_b, v_w_down, v_post_ffn_norm):
    B, S, D = x.shape
    T = B * S
    F = conv_b.shape[0] // 2
    lay = _layout(D)
    n_in = w_in.shape[1]
    d_in = N_DEV * n_in
    seg_a = Q_LORA + KV_LORA + ROPE
    seg_b = 3 * HEADS * FOX_DIM + HEADS
    mla_scale = (NOPE + ROPE) ** -0.5
    fox_scale = FOX_DIM ** -0.5
    ax, ay, ac = (lax.axis_index(a) for a in MESH_AXES)
    qc = jnp.stack([2 * ax + ay, ac]).astype(jnp.int32)

    def row(vec, width=None):
        vec = vec.reshape(1, -1)
        if width is not None and vec.shape[1] < width:
            vec = jnp.pad(vec, ((0, 0), (0, width - vec.shape[1])))
        return vec

    x2 = x.reshape(T, D)
    win_s = _cast_bf16(w_in, "cast_w_in")
    h, (win_g,) = _prenorm(x2, row(pre_mix_norm), comm=_Comm([_GatherPlan([win_s], mid_frac=0.3)]))
    small_s = [_cast_bf16(w, "cast_" + n) for w, n in
               [(w_uq, "w_uq"), (w_ukv, "w_ukv"), (w_branch_mla, "w_branch_mla"), (w_branch_fox, "w_branch_fox"), (w_out, "w_out")]]
    wup_s = _cast_bf16(w_up, "cast_w_up")
    wdown_s = _cast_bf16(w_down, "cast_w_down")

    def shard_cols(lo, hi):
        out = []
        for g in range(lo // n_in, (hi - 1) // n_in + 1):
            out.append(win_g[g][:, max(lo, g * n_in) - g * n_in:min(hi, (g + 1) * n_in) - g * n_in])
        return out

    w_perm = jnp.concatenate(
        shard_cols(0, seg_a) + [jnp.zeros((D, LANES - ROPE), BF16)] + shard_cols(seg_a, seg_a + seg_b)
        + [jnp.zeros((D, LANES - HEADS), BF16)] + shard_cols(seg_a + seg_b, d_in), axis=1)

    tgt = loss_target.reshape(T, D)
    pos = positions.reshape(T, 1)
    inv_freq = 1.0 / (ROPE_THETA ** (jnp.arange(0, ROPE, 2, dtype=F32) / ROPE))
    invf = row(jnp.concatenate([inv_freq, inv_freq]), LANES)
    g_pre, g_q, g_kv = row(pre_mix_norm), row(q_a_norm), row(kv_a_norm)
    g_pm, g_ffn, g_pf = row(post_mix_norm), row(pre_ffn_norm), row(post_ffn_norm)
    bfor = row(b_forget, LANES)
    bgate = row(b_gate)
    cb_full = row(conv_b)

    def own_plan(blocks):
        return _Comm([_GatherOwnPlan(blocks)])

    def pass_plan(gathered):
        return _Comm([_GatherPassPlan(gathered)])

    def pair_plan(gs):
        return _Comm([_PairScatterPlan(gs)])

    def chip_plan(gs):
        return _Comm([_ChipScatterPlan(gs)])

    half_d = D // 2
    proj, landed = _matmul(h, w_perm, mode="nn", name="mm_proj", comm=_Comm(
        [_GatherOwnPlan(small_s[:2] + [conv_w]), _GatherOwnPlan([wup_s], rows=(0, half_d))]))
    early_g, wup_part = landed[:-1], landed[-1:]
    (qn, kvn, kper, logf, cosT, sinT), (wuq_g, wukv_g, cw_g) = _split_prep(
        proj, pos, invf, g_q, g_kv, bfor, lay, comm=pass_plan(early_g))
    wuq_pad = jnp.pad(wuq_g, ((0, 0), (0, 0), (0, ATT_DK - NOPE - ROPE)))
    cw_full = jnp.transpose(cw_g, (1, 0, 2)).reshape(3, 2 * F)

    qraw = _matmul(qn, wuq_pad, mode="nn", name="mm_q", out_blocks=ATT_DK)
    kvraw = _matmul(kvn, wukv_g, mode="nn", name="mm_kv", out_blocks=NOPE + VDIM)
    (q_mla, k_mla, v_mla), branch_half = _mla_prep(qraw, kvraw, kper, cosT, sinT, comm=own_plan(small_s[2:4]))
    cs = _fox_cumsum(logf, B, S, 1.0 / fox_scale)
    (q_fox, k_fox, v_fox), wout_half = _fox_prep(proj, cs, lay, comm=own_plan(small_s[4:5]))
    ((o_mla, lse_mla), (o_fox, lse_fox)), landed = _attn_fwd(
        [(q_mla, k_mla, v_mla, MLA_UNIT, mla_scale), (q_fox, k_fox, v_fox, 1, fox_scale)], B=B, S=S,
        name="attn_fwd", comm=_Comm([_GatherOwnPlan([wup_s], rows=(half_d, D), into=wup_part),
                                     _GatherPassPlan(branch_half + wout_half)]))
    wup_half, (wbm_g, wbf_g, wout_g) = landed[:1], landed[1:]
    wbm = jnp.transpose(wbm_g, (1, 0, 2)).reshape(HEADS * VDIM, D)
    wbf = jnp.transpose(wbf_g, (1, 0, 2)).reshape(HEADS * FOX_DIM, D)
    wout = wout_g.reshape(D, D)
    a_m = _matmul(o_mla, wbm, mode="nn", name="mm_branch_mla")
    a_f = _matmul(o_fox, wbf, mode="nn", name="mm_branch_fox")
    (merged,), (wup_g,) = _gate_merge(a_m, a_f, proj, bgate, lay, D, comm=pass_plan(wup_half))
    n_up = wup_g.shape[2]
    y1 = _matmul(merged, wout, mode="nn", name="mm_out")
    x1, h2 = _mid(x2, y1, g_pm, g_ffn)
    u, wdown_half = _matmul(h2, wup_g, mode="nn", name="mm_up", tn=n_up, comm=own_plan([wdown_s]))
    (act,), (wdown_g,) = _convffn_fwd(u, cw_full, cb_full, B, S, F, comm=pass_plan(wdown_half))
    wdown = wdown_g.reshape(F, D)
    ff = _matmul(act, wdown, mode="nn", name="mm_down", tk=F // 2)
    dy, dff, loss_part, dg_pf = _tail(ff, x1, tgt, g_pf)

    dact = _matmul(dff, wdown, mode="nt", name="mm_dact", tn=F // 4)
    dw_down = _matmul(act, dff, mode="tn", name="mm_dw_down", tm=F // 4, tn=512).reshape(N_DEV, F // N_DEV, D)
    (du_g, du_v, dcp_g, dcp_v), (pa_down,) = _convffn_bwd(u, dact, cw_full, cb_full, B, S, F, comm=pair_plan([dw_down]))
    keep_down, sb_down = _chip_sum(dw_down, pa_down, qc, "chipsum_w_down")
    dh2, (rb_down,) = _matmul_halves((du_g, du_v), wup_g, mode="nt", name="mm_dh2", tm=MM_TILE, comm=chip_plan([sb_down]))
    dw_up = _matmul_halves(h2, (du_g, du_v), mode="tn", name="mm_dw_up", tm=MM_TILE, tn=n_up, tk=T // 2)
    (dx1, dy1, dg_ffn, dg_pm), _ = _mid_bwd(dy, dh2, x1, y1, g_ffn, g_pm)
    dmerged = _matmul(dy1, wout, mode="nt", name="mm_dmerged")
    dw_out = _matmul(merged, dy1, mode="tn", name="mm_dw_out").reshape(N_DEV, D // N_DEV, D)
    (da_m, da_f, dgl_m, dgl_f, dbg_m, dbg_f), (pa_up,) = _gate_bwd(
        dmerged, a_m, a_f, proj, bgate, lay, D, comm=pair_plan([dw_up]))
    keep_up, sb_up = _chip_sum(dw_up, pa_up, qc, "chipsum_w_up")
    dw_bm = _matmul(o_mla, da_m, mode="tn", name="mm_dw_branch_mla", out_blocks=D // N_DEV)
    dw_bf = _matmul(o_fox, da_f, mode="tn", name="mm_dw_branch_fox", out_blocks=D // N_DEV)
    mix = [dw_out, dw_bm, dw_bf]
    do_mla, pa_mix = _matmul(da_m, wbm, mode="nt", name="mm_do_mla", out_dtype=BF16, comm=pair_plan(mix))
    do_fox = _matmul(da_f, wbf, mode="nt", name="mm_do_fox", out_dtype=BF16)
    mix_sums = [_chip_sum(g, p, qc, "chipsum_" + n) for g, p, n in zip(mix, pa_mix, ["w_out", "w_branch_mla", "w_branch_fox"])]
    ((dq_m, dk_m, dv_m), (dq_f, dk_f, dv_f)), (rb_up,) = _attn_bwd(
        [(q_mla, k_mla, v_mla, o_mla, do_mla, lse_mla, MLA_UNIT, mla_scale),
         (q_fox, k_fox, v_fox, o_fox, do_fox, lse_fox, 1, fox_scale)], B=B, S=S, name="attn_bwd",
        comm=chip_plan([sb_up]))
    (dqraw, dkvraw, dkpe), rb_mix = _mla_bwd_prep(dq_m, dk_m, dv_m, cosT, sinT, comm=chip_plan([s[1] for s in mix_sums]))
    dqn = _matmul(dqraw, wuq_pad, mode="nt", name="mm_dqn")
    dw_uq = _matmul(qn, dqraw, mode="tn", name="mm_dw_uq", out_blocks=ATT_DK)[:, :, :NOPE + ROPE]
    dkvn = _matmul(dkvraw, wukv_g, mode="nt", name="mm_dkvn")
    dw_ukv = _matmul(kvn, dkvraw, mode="tn", name="mm_dw_ukv", out_blocks=NOPE + VDIM)
    dqlat, dkvlat, dg_q, dg_kv = _lat_bwd(dqn, dkvn, proj, g_q, g_kv, lay)
    dfl, dbfor = _fox_bwd_prep(dq_f, dk_f, proj, bfor, lay, B, S, 1.0 / fox_scale)
    dproj = _concat_cols([dqlat, dkvlat, dkpe, dq_f, dk_f, dv_f, dfl, dgl_m, dgl_f], "concat_dproj")
    dw_perm = _matmul(h, dproj, mode="tn", name="mm_dw_in")
    segs = [(0, seg_a, 0), (seg_a, seg_a + seg_b, lay["fq"] - seg_a), (seg_a + seg_b, d_in, lay["g"] - seg_a - seg_b)]

    def piece(g):
        lo, hi = g * n_in, (g + 1) * n_in
        parts = [dw_perm[:, max(lo, s0) + sh:min(hi, s1) + sh] for s0, s1, sh in segs if max(lo, s0) < min(hi, s1)]
        return parts[0] if len(parts) == 1 else jnp.concatenate(parts, axis=1)

    dw_in = jnp.stack([piece(g) for g in range(N_DEV)])
    dcw = jnp.transpose(jnp.concatenate([dcp_g[0:3], dcp_v[0:3]], axis=1).reshape(3, N_DEV, (2 * F) // N_DEV), (1, 0, 2))
    late = [dw_in, dw_uq, dw_ukv, dcw]
    pa_late = _exchange_alone(pair_plan(late), "pair_late")
    late_sums = [_chip_sum(g, p, qc, "chipsum_" + n) for g, p, n in zip(late, pa_late, ["w_in", "w_uq", "w_ukv", "conv_w"])]
    dh, rb_late = _matmul(dproj, w_perm, mode="nt", name="mm_dh", tn=2048, tk=2048, comm=chip_plan([s[1] for s in late_sums]))
    (grad_x, dg_pre), _ = _final_dx(dx1, dh, x2, g_pre)

    big_out = {}

    def finish(n, keep, pieces, w, m, v):
        big_out[n] = _sum_adamw(keep, pieces, w, m, v, "adamw_" + n)

    finish("w_down", keep_down, rb_down, w_down, m_w_down, v_w_down)
    finish("w_up", keep_up, rb_up, w_up, m_w_up, v_w_up)
    finish("w_out", mix_sums[0][0], rb_mix[0], w_out, m_w_out, v_w_out)
    finish("w_branch_mla", mix_sums[1][0], rb_mix[1], w_branch_mla, m_w_branch_mla, v_w_branch_mla)
    finish("w_branch_fox", mix_sums[2][0], rb_mix[2], w_branch_fox, m_w_branch_fox, v_w_branch_fox)
    finish("w_in", late_sums[0][0], rb_late[0], w_in, m_w_in, v_w_in)
    finish("w_uq", late_sums[1][0], rb_late[1], w_uq, m_w_uq, v_w_uq)
    finish("w_ukv", late_sums[2][0], rb_late[2], w_ukv, m_w_ukv, v_w_ukv)
    finish("conv_w", late_sums[3][0], rb_late[3], conv_w, m_conv_w, v_conv_w)

    widths = [D, Q_LORA, KV_LORA, LANES, 2 * D, D, D, 2 * F, D]
    small_names = ["pre_mix_norm", "q_a_norm", "kv_a_norm", "b_forget", "b_gate", "post_mix_norm", "pre_ffn_norm",
                   "conv_b", "post_ffn_norm"]
    true_w = [D, Q_LORA, KV_LORA, HEADS, 2 * D, D, D, 2 * F, D]
    dcb = jnp.concatenate([dcp_g[3:4], dcp_v[3:4]], axis=1)
    part = jnp.concatenate([dg_pre, dg_q, dg_kv, dbfor, dbg_m, dbg_f, dg_pm, dg_ffn, dcb, dg_pf], axis=1)

    def pack(vals):
        return jnp.concatenate([row(a, wd) for a, wd in zip(vals, widths)], axis=1)

    sw = pack([pre_mix_norm, q_a_norm, kv_a_norm, b_forget, b_gate, post_mix_norm, pre_ffn_norm, conv_b, post_ffn_norm])
    sm = pack([m_pre_mix_norm, m_q_a_norm, m_kv_a_norm, m_b_forget, m_b_gate, m_post_mix_norm, m_pre_ffn_norm,
               m_conv_b, m_post_ffn_norm])
    sv = pack([v_pre_mix_norm, v_q_a_norm, v_kv_a_norm, v_b_forget, v_b_gate, v_post_mix_norm, v_pre_ffn_norm,
               v_conv_b, v_post_ffn_norm])
    (parts_all,) = _exchange_alone(_Comm([_DirectGatherPlan([part])]), "gather_small")
    sg, sd, smo, svo = _sum_adamw(None, parts_all, sw, sm, sv, "adamw_small")
    small_out = {}
    off = 0
    for n, wd, tw in zip(small_names, widths, true_w):
        small_out[n] = tuple(a[0, off:off + tw] for a in (sg, sd, smo, svo))
        off += wd

    loss = lax.psum(loss_part[0, 0], MESH_AXES)
    order = ["pre_mix_norm", "w_in", "q_a_norm", "w_uq", "kv_a_norm", "w_ukv", "b_forget", "b_gate", "w_branch_mla",
             "w_branch_fox", "w_out", "post_mix_norm", "pre_ffn_norm", "w_up", "conv_w", "conv_b", "w_down",
             "post_ffn_norm"]
    res = {**big_out, **small_out}
    outs = [loss, grad_x.reshape(B, S, D)]
    for kind in range(4):
        outs += [res[n][kind] for n in order]
    return tuple(outs)
```

```python
import math

import jax
import jax.numpy as jnp
from jax import lax
from jax.experimental import pallas as pl
from jax.experimental.pallas import tpu as pltpu

F32 = jnp.float32
BF16 = jnp.bfloat16

N_DEV = 8
N_CHIP = 4
HEADS = 8
NOPE = 128
ROPE = 64
HALF_ROPE = ROPE // 2
VDIM = 128
Q_LORA = 512
KV_LORA = 256
FOX_DIM = 128
ATT_DK = 256
MLA_UNIT = 64
ROPE_THETA = 10000.0
EPS = 1e-6
NEG_INF = -1e30
LANES = 128
LOG2_E = 1.4426950408889634

ADAM_LR = 0.001
ADAM_B1 = 0.9
ADAM_B2 = 0.999
ADAM_EPS = 1e-08
ADAM_WD = 0.01
ADAM_STEP = 10

VMEM_LIMIT_BYTES = 56 * 1024 * 1024
ROW_TILE = 256
HEAD_ROW_TILE = 1024
ATT_TILE = 1024
ATT_SUB = 256
ATT_AHEAD = 3
MM_TILE = 1024

MESH_AXES = ("x", "y", "c")
ANY = pl.BlockSpec(memory_space=pl.ANY)


def _tile(n, pref, align=LANES):
    if n <= pref:
        return n
    t = (pref // align) * align
    while t >= align:
        if n % t == 0:
            return t
        t -= align
    return n


def _sds(shape, dtype):
    return jax.ShapeDtypeStruct(shape, dtype)


def _coords():
    x, y, c = (lax.axis_index(ax) for ax in MESH_AXES)
    return x, y, c


def _chip_rel(x, y, r):
    return (1 - x if r & 2 else x), (1 - y if r & 1 else y)


def _rcopy(src, dst, sems, w, k, dev):
    return pltpu.make_async_remote_copy(src_ref=src, dst_ref=dst, send_sem=sems[0].at[w, k], recv_sem=sems[1].at[w, k],
                                        device_id=dev, device_id_type=pl.DeviceIdType.MESH)


class _GatherRelayPlan:
    def __init__(self, blocks, mid_frac=0.5):
        self.ins = list(blocks)
        self.out_shapes = [_sds((N_DEV,) + b.shape, b.dtype) for b in blocks]
        n = len(blocks)
        self.scratch = [pltpu.SemaphoreType.DMA((n, 7)), pltpu.SemaphoreType.DMA((n, 7)), pltpu.SemaphoreType.DMA((n,))]
        self.mid_frac = mid_frac

    @staticmethod
    def _places():
        x, y, c = _coords()
        xn, yn = 4 * (1 - x) + 2 * y, 4 * x + 2 * (1 - y)
        relay_src = 4 * (x + c * (1 - 2 * x)) + 2 * (y + (1 - c) * (1 - 2 * y)) + c
        relay_to = (x + (1 - c) * (1 - 2 * x), y + c * (1 - 2 * y), c)
        return x, y, c, xn, yn, relay_src, relay_to, 4 * (1 - x) + 2 * (1 - y)

    def first(self, ins, outs, sems):
        x, y, c, _, _, _, _, _ = self._places()
        me = 4 * x + 2 * y + c
        for w in range(len(ins)):
            pltpu.make_async_copy(ins[w], outs[w].at[me], sems[2].at[w]).start()
            _rcopy(ins[w], outs[w].at[me], sems, w, 0, (x, y, 1 - c)).start()
            _rcopy(ins[w], outs[w].at[me], sems, w, 1, (1 - x, y, c)).start()
            _rcopy(ins[w], outs[w].at[me], sems, w, 2, (x, 1 - y, c)).start()

    def mid(self, ins, outs, sems):
        x, y, c, xn, yn, relay_src, relay_to, _ = self._places()
        sib = (x, y, 1 - c)
        for w in range(len(ins)):
            bx, by = outs[w].at[xn + c], outs[w].at[yn + c]
            _rcopy(ins[w], bx, sems, w, 1, (1 - x, y, c)).wait_recv()
            _rcopy(ins[w], by, sems, w, 2, (x, 1 - y, c)).wait_recv()
            _rcopy(outs[w].at[relay_src], outs[w].at[relay_src], sems, w, 3, relay_to).start()
            _rcopy(bx, bx, sems, w, 4, sib).start()
            _rcopy(by, by, sems, w, 5, sib).start()

    def last(self, ins, outs, sems):
        x, y, c, xn, yn, _, relay_to, dg = self._places()
        me = 4 * x + 2 * y + c
        sib = (x, y, 1 - c)
        for w in range(len(ins)):
            bd = outs[w].at[dg + c]
            _rcopy(ins[w], bd, sems, w, 3, relay_to).wait_recv()
            _rcopy(bd, bd, sems, w, 6, sib).start()
            for k, blk in ((0, 4 * x + 2 * y), (4, xn), (5, yn), (6, dg)):
                _rcopy(ins[w], outs[w].at[blk + 1 - c], sems, w, k, sib).wait_recv()
            for k in range(7):
                _rcopy(ins[w], outs[w].at[me], sems, w, k, sib).wait_send()
            pltpu.make_async_copy(ins[w], outs[w].at[me], sems[2].at[w]).wait()


class _GatherOwnPlan:
    mid = None

    def __init__(self, blocks, rows=None, into=None):
        self.n = len(blocks)
        self.rows = rows
        self.ins = list(blocks) + list(into or [])
        self.out_shapes = [_sds((N_DEV,) + b.shape, b.dtype) for b in blocks]
        self.aliases = [(self.n + i, i) for i in range(len(into or []))]
        n = self.n
        self.scratch = [pltpu.SemaphoreType.DMA((n, 4)), pltpu.SemaphoreType.DMA((n, 4)), pltpu.SemaphoreType.DMA((n,))]

    def _cut(self, ref):
        return ref if self.rows is None else ref.at[pl.ds(self.rows[0], self.rows[1] - self.rows[0])]

    def first(self, ins, outs, sems):
        x, y, c = _coords()
        me = 4 * x + 2 * y + c
        for w in range(self.n):
            src, dst = self._cut(ins[w]), self._cut(outs[w].at[me])
            pltpu.make_async_copy(src, dst, sems[2].at[w]).start()
            _rcopy(src, dst, sems, w, 0, (x, y, 1 - c)).start()
            for r in (1, 2, 3):
                px, py = _chip_rel(x, y, r)
                _rcopy(src, dst, sems, w, r, (px, py, c)).start()

    def last(self, ins, outs, sems):
        x, y, c = _coords()
        me = 4 * x + 2 * y + c
        for w in range(self.n):
            src = self._cut(ins[w])
            cp = _rcopy(src, self._cut(outs[w].at[4 * x + 2 * y + 1 - c]), sems, w, 0, (x, y, 1 - c))
            cp.wait_recv()
            cp.wait_send()
            for r in (1, 2, 3):
                px, py = _chip_rel(x, y, r)
                cp = _rcopy(src, self._cut(outs[w].at[4 * px + 2 * py + c]), sems, w, r, (px, py, c))
                cp.wait_recv()
                cp.wait_send()
            pltpu.make_async_copy(src, self._cut(outs[w].at[me]), sems[2].at[w]).wait()


class _GatherPassPlan:
    mid = None

    def __init__(self, gathered):
        self.ins = list(gathered)
        self.out_shapes = [_sds(g.shape, g.dtype) for g in gathered]
        self.aliases = [(i, i) for i in range(len(gathered))]
        n = len(gathered)
        self.scratch = [pltpu.SemaphoreType.DMA((n, 3)), pltpu.SemaphoreType.DMA((n, 3))]

    def first(self, ins, outs, sems):
        x, y, c = _coords()
        for w in range(len(ins)):
            for r in (1, 2, 3):
                px, py = _chip_rel(x, y, r)
                blk = 4 * px + 2 * py + c
                _rcopy(ins[w].at[blk], outs[w].at[blk], sems, w, r - 1, (x, y, 1 - c)).start()

    def last(self, ins, outs, sems):
        x, y, c = _coords()
        for w in range(len(ins)):
            for r in (1, 2, 3):
                px, py = _chip_rel(x, y, r)
                blk = 4 * px + 2 * py + 1 - c
                cp = _rcopy(ins[w].at[blk], outs[w].at[blk], sems, w, r - 1, (x, y, 1 - c))
                cp.wait_recv()
                cp.wait_send()


class _DirectGatherPlan:
    mid = None

    def __init__(self, blocks):
        self.ins = list(blocks)
        self.out_shapes = [_sds((N_DEV,) + b.shape, b.dtype) for b in blocks]
        n = len(blocks)
        self.scratch = [pltpu.SemaphoreType.DMA((n, 7)), pltpu.SemaphoreType.DMA((n, 7)), pltpu.SemaphoreType.DMA((n,))]

    @staticmethod
    def _peer(x, y, c, r):
        return (1 - x if r & 4 else x), (1 - y if r & 2 else y), (1 - c if r & 1 else c)

    def first(self, ins, outs, sems):
        x, y, c = _coords()
        me = 4 * x + 2 * y + c
        for w in range(len(ins)):
            pltpu.make_async_copy(ins[w], outs[w].at[me], sems[2].at[w]).start()
            for r in range(1, N_DEV):
                _rcopy(ins[w], outs[w].at[me], sems, w, r - 1, self._peer(x, y, c, r)).start()

    def last(self, ins, outs, sems):
        x, y, c = _coords()
        me = 4 * x + 2 * y + c
        for w in range(len(ins)):
            for r in range(1, N_DEV):
                px, py, pc = self._peer(x, y, c, r)
                cp = _rcopy(ins[w], outs[w].at[4 * px + 2 * py + pc], sems, w, r - 1, (px, py, pc))
                cp.wait_recv()
                cp.wait_send()
            pltpu.make_async_copy(ins[w], outs[w].at[me], sems[2].at[w]).wait()


class _PairScatterPlan:
    mid = None

    def __init__(self, pieces):
        self.ins = list(pieces)
        self.out_shapes = [_sds((N_CHIP,) + p.shape[1:], p.dtype) for p in pieces]
        n = len(pieces)
        self.scratch = [pltpu.SemaphoreType.DMA((n, N_CHIP)), pltpu.SemaphoreType.DMA((n, N_CHIP))]

    def _copies(self, ins, outs, sems):
        x, y, c = _coords()
        return [_rcopy(ins[w].at[2 * q + 1 - c], outs[w].at[q], sems, w, q, (x, y, 1 - c))
                for w in range(len(ins)) for q in range(N_CHIP)]

    def first(self, ins, outs, sems):
        for cp in self._copies(ins, outs, sems):
            cp.start()

    def last(self, ins, outs, sems):
        for cp in self._copies(ins, outs, sems):
            cp.wait_recv()
            cp.wait_send()


class _ChipScatterPlan:
    mid = None

    def __init__(self, sums, rows=None, into=None):
        self.n = len(sums)
        self.rows = rows
        self.ins = list(sums) + list(into or [])
        self.out_shapes = [_sds(s.shape, s.dtype) for s in sums]
        self.aliases = [(self.n + i, i) for i in range(len(into or []))]
        self.scratch = [pltpu.SemaphoreType.DMA((self.n, 3)), pltpu.SemaphoreType.DMA((self.n, 3))]

    def _copies(self, ins, outs, sems):
        x, y, c = _coords()
        cps = []
        for w in range(self.n):
            for r in (1, 2, 3):
                px, py = _chip_rel(x, y, r)
                src, dst = ins[w].at[r - 1], outs[w].at[r - 1]
                if self.rows is not None:
                    cut = pl.ds(self.rows[0], self.rows[1] - self.rows[0])
                    src, dst = src.at[cut], dst.at[cut]
                cps.append(_rcopy(src, dst, sems, w, r - 1, (px, py, c)))
        return cps

    def first(self, ins, outs, sems):
        for cp in self._copies(ins, outs, sems):
            cp.start()

    def last(self, ins, outs, sems):
        for cp in self._copies(ins, outs, sems):
            cp.wait_recv()
            cp.wait_send()


class _Comm:
    def __init__(self, plans):
        self.plans = list(plans)
        self.ins = [a for p in self.plans for a in p.ins]
        self.out_shapes = [s for p in self.plans for s in p.out_shapes]
        self.scratch = [s for p in self.plans for s in p.scratch]
        self.aliases = []
        i = o = 0
        for p in self.plans:
            self.aliases += [(i + a, o + b) for a, b in getattr(p, "aliases", [])]
            i, o = i + len(p.ins), o + len(p.out_shapes)

    def _parts(self, ins, outs, sems):
        i = o = s = 0
        for p in self.plans:
            yield p, ins[i:i + len(p.ins)], outs[o:o + len(p.out_shapes)], sems[s:s + len(p.scratch)]
            i, o, s = i + len(p.ins), o + len(p.out_shapes), s + len(p.scratch)

    def begin(self, step, nsteps, ins, outs, sems):
        @pl.when(step == 0)
        def _():
            for p, pi, po, ps in self._parts(ins, outs, sems):
                p.first(pi, po, ps)

        for p, pi, po, ps in self._parts(ins, outs, sems):
            if p.mid is not None:
                @pl.when(step == min(nsteps - 1, int(p.mid_frac * nsteps)))
                def _(p=p, pi=pi, po=po, ps=ps):
                    p.mid(pi, po, ps)

    def end(self, step, nsteps, ins, outs, sems):
        @pl.when(step == nsteps - 1)
        def _():
            for p, pi, po, ps in self._parts(ins, outs, sems):
                p.last(pi, po, ps)


def _call(body, args, *, name, grid, in_specs, out_specs, out_shape, scratch_shapes=(), sem=None, comm=None):
    in_specs, out_specs, out_shape, scratch_shapes = list(in_specs), list(out_specs), list(out_shape), list(scratch_shapes)
    if comm is None:
        res = pl.pallas_call(
            body, name=name, grid=grid, in_specs=in_specs, out_specs=out_specs, out_shape=out_shape,
            scratch_shapes=scratch_shapes,
            compiler_params=pltpu.CompilerParams(dimension_semantics=sem, vmem_limit_bytes=VMEM_LIMIT_BYTES),
        )(*args)
        return list(res), []
    n_in, n_out, n_sc = len(in_specs), len(out_specs), len(scratch_shapes)
    n_ci, n_co = len(comm.ins), len(comm.out_shapes)
    nsteps = math.prod(grid)

    def hosted(*refs):
        ins, cins = refs[:n_in], refs[n_in:n_in + n_ci]
        o0 = n_in + n_ci
        outs, couts = refs[o0:o0 + n_out], refs[o0 + n_out:o0 + n_out + n_co]
        s0 = o0 + n_out + n_co
        scr, csems = refs[s0:s0 + n_sc], refs[s0 + n_sc:]
        step = jnp.int32(0)
        for d in range(len(grid)):
            step = step * grid[d] + pl.program_id(d)
        comm.begin(step, nsteps, cins, couts, csems)
        body(*ins, *outs, *scr)
        comm.end(step, nsteps, cins, couts, csems)

    res = pl.pallas_call(
        hosted, name=name, grid=grid, in_specs=in_specs + [ANY] * n_ci, out_specs=out_specs + [ANY] * n_co,
        out_shape=out_shape + comm.out_shapes, scratch_shapes=scratch_shapes + comm.scratch,
        input_output_aliases={n_in + a: n_out + b for a, b in comm.aliases},
        compiler_params=pltpu.CompilerParams(dimension_semantics=("arbitrary",) * len(grid),
                                             vmem_limit_bytes=VMEM_LIMIT_BYTES, has_side_effects=True),
    )(*args, *comm.ins)
    return list(res[:n_out]), list(res[n_out:])


def _exchange_alone(comm, name):
    def body():
        pass

    return _call(body, [], name=name, grid=(), in_specs=[], out_specs=[], out_shape=[], comm=comm)[1]


def _matmul(a, b, *, mode, name, out_dtype=F32, out_blocks=None, tm=None, tn=None, tk=None, comm=None):
    tm = MM_TILE if tm is None else tm
    tn = MM_TILE if tn is None else tn
    a_blk = a.ndim == 3
    b_blk = b.ndim == 3
    if mode == "nn":
        M, K = a.shape
        N = b.shape[0] * b.shape[2] if b_blk else b.shape[1]
        dims = (((1,), (0,)), ((), ()))
    elif mode == "nt":
        M = a.shape[1] if a_blk else a.shape[0]
        K = a.shape[0] * a.shape[2] if a_blk else a.shape[1]
        N = b.shape[1] if b_blk else b.shape[0]
        dims = (((1,), (1,)), ((), ()))
    else:
        K, M = a.shape
        N = b.shape[0] * b.shape[2] if b_blk else b.shape[1]
        dims = (((0,), (0,)), ((), ()))

    tm = _tile(M, tm)
    tn = _tile(N, tn)
    if mode == "nt" and (a_blk or b_blk):
        tk = a.shape[2] if a_blk else b.shape[2]
    else:
        tk = _tile(K, K if tk is None else tk)
    if mode != "nt" and b_blk:
        tn = _tile(b.shape[2], tn)
    if out_blocks is not None:
        tn = _tile(out_blocks, tn)
    nk = K // tk
    grid = (M // tm, N // tn, nk)

    if mode == "nn":
        a_spec = pl.BlockSpec((tm, tk), lambda i, j, k: (i, k))
        if b_blk:
            rb = b.shape[2] // tn
            b_spec = pl.BlockSpec((None, tk, tn), lambda i, j, k: (j // rb, k, j % rb))
        else:
            b_spec = pl.BlockSpec((tk, tn), lambda i, j, k: (k, j))
    elif mode == "nt":
        if a_blk:
            a_spec = pl.BlockSpec((None, tm, tk), lambda i, j, k: (k, i, 0))
        else:
            a_spec = pl.BlockSpec((tm, tk), lambda i, j, k: (i, k))
        if b_blk:
            b_spec = pl.BlockSpec((None, tn, tk), lambda i, j, k: (k, j, 0))
        else:
            b_spec = pl.BlockSpec((tn, tk), lambda i, j, k: (j, k))
    else:
        a_spec = pl.BlockSpec((tk, tm), lambda i, j, k: (k, i))
        if b_blk:
            rb = b.shape[2] // tn
            b_spec = pl.BlockSpec((None, tk, tn), lambda i, j, k: (j // rb, k, j % rb))
        else:
            b_spec = pl.BlockSpec((tk, tn), lambda i, j, k: (k, j))

    if out_blocks is None:
        o_spec = pl.BlockSpec((tm, tn), lambda i, j, k: (i, j))
        o_shape = _sds((M, N), out_dtype)
    else:
        ro = out_blocks // tn
        o_spec = pl.BlockSpec((None, tm, tn), lambda i, j, k: (j // ro, i, j % ro))
        o_shape = _sds((N // out_blocks, M, out_blocks), out_dtype)

    direct = nk == 1 or out_dtype == F32

    def body(a_ref, b_ref, o_ref, *scratch):
        if nk == 1:
            o_ref[...] = lax.dot_general(a_ref[...], b_ref[...], dims, preferred_element_type=F32).astype(o_ref.dtype)
            return
        acc_ref = o_ref if direct else scratch[0]
        k = pl.program_id(2)

        @pl.when(k == 0)
        def _():
            acc_ref[...] = jnp.zeros(acc_ref.shape, F32)

        acc_ref[...] += lax.dot_general(a_ref[...], b_ref[...], dims, preferred_element_type=F32)
        if not direct:
            @pl.when(k == nk - 1)
            def _():
                o_ref[...] = acc_ref[...].astype(o_ref.dtype)

    scratch = [] if direct else [pltpu.VMEM((tm, tn), F32)]
    outs, landed = _call(body, [a, b], name=name, grid=grid, in_specs=[a_spec, b_spec], out_specs=[o_spec],
                         out_shape=[o_shape], scratch_shapes=scratch, sem=("parallel", "parallel", "arbitrary"), comm=comm)
    return outs[0] if comm is None else (outs[0], landed)


def _matmul_halves(a, b, *, mode, name, tm, tn=None, tk=None, comm=None):
    if mode == "nt":
        lo, hi = a
        M, kh = lo.shape
        G, N, kb = b.shape
        half = kh // kb
        tm, tn = _tile(M, tm), _tile(N, N if tn is None else tn)
        dims = (((1,), (1,)), ((), ()))

        def body(lo_ref, hi_ref, b_ref, o_ref):
            k = pl.program_id(2)

            @pl.when(k == 0)
            def _():
                o_ref[...] = jnp.zeros(o_ref.shape, F32)

            @pl.when(k < half)
            def _():
                o_ref[...] += lax.dot_general(lo_ref[...], b_ref[...], dims, preferred_element_type=F32)

            @pl.when(k >= half)
            def _():
                o_ref[...] += lax.dot_general(hi_ref[...], b_ref[...], dims, preferred_element_type=F32)

        outs, landed = _call(
            body, [lo, hi, b], name=name, grid=(M // tm, N // tn, G),
            in_specs=[pl.BlockSpec((tm, kb), lambda i, j, k: (i, jnp.minimum(k, half - 1))),
                      pl.BlockSpec((tm, kb), lambda i, j, k: (i, jnp.maximum(k - half, 0))),
                      pl.BlockSpec((None, tn, kb), lambda i, j, k: (k, j, 0))],
            out_specs=[pl.BlockSpec((tm, tn), lambda i, j, k: (i, j))], out_shape=[_sds((M, N), F32)],
            sem=("parallel", "parallel", "arbitrary"), comm=comm)
    else:
        lo, hi = b
        K, nh = lo.shape
        M = a.shape[1]
        n = tn
        half = nh // n
        tm, tk = _tile(M, tm), _tile(K, K if tk is None else tk)
        nk = K // tk
        dims = (((0,), (0,)), ((), ()))

        def body(a_ref, lo_ref, hi_ref, o_ref):
            j, k = pl.program_id(1), pl.program_id(2)

            @pl.when(k == 0)
            def _():
                o_ref[...] = jnp.zeros(o_ref.shape, F32)

            @pl.when(j < half)
            def _():
                o_ref[...] += lax.dot_general(a_ref[...], lo_ref[...], dims, preferred_element_type=F32)

            @pl.when(j >= half)
            def _():
                o_ref[...] += lax.dot_general(a_ref[...], hi_ref[...], dims, preferred_element_type=F32)

        outs, landed = _call(
            body, [a, lo, hi], name=name, grid=(M // tm, 2 * half, nk),
            in_specs=[pl.BlockSpec((tk, tm), lambda i, j, k: (k, i)),
                      pl.BlockSpec((tk, n), lambda i, j, k: (jnp.where(j < half, k, nk - 1), jnp.minimum(j, half - 1))),
                      pl.BlockSpec((tk, n), lambda i, j, k: (jnp.where(j >= half, k, 0), jnp.maximum(j - half, 0)))],
            out_specs=[pl.BlockSpec((None, tm, n), lambda i, j, k: (j, i, 0))],
            out_shape=[_sds((2 * half, M, n), F32)], sem=("parallel", "parallel", "arbitrary"), comm=comm)
    return outs[0] if comm is None else (outs[0], landed)


def _rms(x):
    return lax.rsqrt(jnp.mean(x * x, axis=-1, keepdims=True) + EPS)


def _rms_bwd(dy, x, g):
    r = _rms(x)
    n = x * r
    dn = dy * g
    dx = r * (dn - n * jnp.mean(dn * n, axis=-1, keepdims=True))
    return dx, dy * n


def _sigmoid(x):
    return 1.0 / (1.0 + jnp.exp(-x))


def _rope_rot(t):
    return pltpu.roll(t, HALF_ROPE, 1) - pltpu.roll(t, LANES - HALF_ROPE, 1)


def _lane(shape):
    return lax.broadcasted_iota(jnp.int32, shape, 1)


def _split3(x):
    hi = x.astype(BF16).astype(F32)
    r1 = x - hi
    mid = r1.astype(BF16).astype(F32)
    lo = (r1 - mid).astype(BF16).astype(F32)
    return hi, mid, lo


def _cumsum_rows(x, reverse):
    S = x.shape[0]
    bs = min(256, S)
    nb = S // bs
    r = lax.broadcasted_iota(jnp.int32, (bs, bs), 0)
    c = lax.broadcasted_iota(jnp.int32, (bs, bs), 1)
    tri = jnp.where((c >= r) if reverse else (c <= r), 1.0, 0.0).astype(BF16)
    edge = lax.broadcasted_iota(jnp.int32, (bs, x.shape[1]), 0) == (0 if reverse else bs - 1)
    carry = jnp.zeros((1, x.shape[1]), F32)
    outs = [None] * nb
    for bi in (range(nb - 1, -1, -1) if reverse else range(nb)):
        xb = x[bi * bs:(bi + 1) * bs, :]
        acc = carry
        for term in _split3(xb):
            acc = acc + jnp.dot(tri, term.astype(BF16), preferred_element_type=F32)
        outs[bi] = acc
        carry = jnp.sum(jnp.where(edge, acc, 0.0), axis=0, keepdims=True)
    return jnp.concatenate(outs, axis=0) if nb > 1 else outs[0]


def _gelu_parts(x):
    c0 = math.sqrt(2.0 / math.pi)
    inner = c0 * (x + 0.044715 * (x * x * x))
    t = jnp.tanh(inner)
    g = 0.5 * x * (1.0 + t)
    dg = 0.5 * (1.0 + t) + 0.5 * x * (1.0 - t * t) * (c0 * (1.0 + 3.0 * 0.044715 * (x * x)))
    return g, dg


def _accumulate(ref, value, first):
    @pl.when(first)
    def _():
        ref[...] = value

    @pl.when(jnp.logical_not(first))
    def _():
        ref[...] += value


def _cast_bf16(w, name):
    R, C = w.shape
    tr = _tile(R, 512, 16)

    def body(w_ref, o_ref):
        o_ref[...] = w_ref[...].astype(BF16)

    blk = pl.BlockSpec((tr, C), lambda i: (i, 0))
    return _call(body, [w], name=name, grid=(R // tr,), in_specs=[blk], out_specs=[blk],
                 out_shape=[_sds((R, C), BF16)], sem=("parallel",))[0][0]


def _concat_cols(parts, name):
    T = parts[0].shape[-2] if parts[0].ndim == 3 else parts[0].shape[0]
    widths = [p.shape[0] * LANES if p.ndim == 3 else p.shape[1] for p in parts]
    tm = _tile(T, ROW_TILE, 16)

    def body(*refs):
        o_ref = refs[-1]
        off = 0
        for p_ref, p, w in zip(refs[:-1], parts, widths):
            if p.ndim == 3:
                for hd in range(p.shape[0]):
                    o_ref[:, off + hd * LANES:off + (hd + 1) * LANES] = p_ref[hd].astype(BF16)
            else:
                o_ref[:, off:off + w] = p_ref[...].astype(BF16)
            off += w

    def spec(p, w):
        if p.ndim == 3:
            return pl.BlockSpec((p.shape[0], tm, LANES), lambda i: (0, i, 0))
        return pl.BlockSpec((tm, w), lambda i: (i, 0))

    return _call(body, parts, name=name, grid=(T // tm,),
                 in_specs=[spec(p, w) for p, w in zip(parts, widths)],
                 out_specs=[pl.BlockSpec((tm, sum(widths)), lambda i: (i, 0))],
                 out_shape=[_sds((T, sum(widths)), BF16)], sem=("parallel",))[0][0]


def _prenorm(x, g, comm=None):
    T, D = x.shape
    tm = _tile(T, ROW_TILE, 16)

    def body(x_ref, g_ref, h_ref):
        xv = x_ref[...]
        h_ref[...] = (xv * _rms(xv) * g_ref[...]).astype(BF16)

    row = pl.BlockSpec((tm, D), lambda i: (i, 0))
    (h,), landed = _call(body, [x, g], name="prenorm", grid=(T // tm,),
                         in_specs=[row, pl.BlockSpec((1, D), lambda i: (0, 0))], out_specs=[row],
                         out_shape=[_sds((T, D), BF16)], sem=("parallel",), comm=comm)
    return h, landed


def _split_prep(proj, pos, invf, gq, gkv, bfor, lay, comm=None):
    T = proj.shape[0]
    tm = _tile(T, ROW_TILE, 16)

    def body(q_ref, kv_ref, kpe_ref, fl_ref, pos_ref, invf_ref, gq_ref, gkv_ref, bf_ref,
             qn_ref, kvn_ref, kper_ref, logf_ref, cos_ref, sin_ref):
        ql = q_ref[...]
        qn_ref[...] = (ql * _rms(ql) * gq_ref[...]).astype(BF16)
        kl = kv_ref[...]
        kvn_ref[...] = (kl * _rms(kl) * gkv_ref[...]).astype(BF16)
        ang = pos_ref[...].astype(F32) * invf_ref[...]
        valid = _lane(ang.shape) < ROPE
        cs = jnp.where(valid, jnp.cos(ang), 0.0)
        sn = jnp.where(valid, jnp.sin(ang), 0.0)
        cos_ref[...] = cs
        sin_ref[...] = sn
        kp = jnp.where(valid, kpe_ref[...], 0.0)
        kper_ref[...] = (kp * cs + _rope_rot(kp) * sn).astype(BF16)
        z = fl_ref[...] + bf_ref[...]
        logf_ref[...] = jnp.minimum(z, 0.0) - jnp.log(1.0 + jnp.exp(-jnp.abs(z)))

    def col(width, off):
        return pl.BlockSpec((tm, width), lambda i: (i, off // width))

    def vec(width):
        return pl.BlockSpec((1, width), lambda i: (0, 0))

    def out(width):
        return pl.BlockSpec((tm, width), lambda i: (i, 0))

    return _call(
        body, [proj, proj, proj, proj, pos, invf, gq, gkv, bfor], name="split_prep", grid=(T // tm,),
        in_specs=[col(Q_LORA, lay["q"]), col(KV_LORA, lay["kv"]), col(LANES, lay["kpe"]), col(LANES, lay["fl"]),
                  pl.BlockSpec((tm, 1), lambda i: (i, 0)), vec(LANES), vec(Q_LORA), vec(KV_LORA), vec(LANES)],
        out_specs=[out(Q_LORA), out(KV_LORA), out(LANES), out(LANES), out(LANES), out(LANES)],
        out_shape=[_sds((T, Q_LORA), BF16), _sds((T, KV_LORA), BF16), _sds((T, LANES), BF16),
                   _sds((T, LANES), F32), _sds((T, LANES), F32), _sds((T, LANES), F32)],
        sem=("parallel",), comm=comm)


def _mla_prep(qraw, kvraw, kper, cosT, sinT, comm=None):
    H, T, _ = qraw.shape
    tm = _tile(T, HEAD_ROW_TILE, 16)

    def body(q_ref, kv_ref, kpe_ref, cos_ref, sin_ref, qo_ref, ko_ref, vo_ref):
        q = q_ref[...]
        pe = q[:, NOPE:]
        pe = jnp.where(_lane(pe.shape) < ROPE, pe, 0.0)
        qo_ref[:, :NOPE] = q[:, :NOPE].astype(BF16)
        qo_ref[:, NOPE:] = (pe * cos_ref[...] + _rope_rot(pe) * sin_ref[...]).astype(BF16)
        kv = kv_ref[...]
        ko_ref[:, :NOPE] = kv[:, :NOPE].astype(BF16)
        ko_ref[:, NOPE:] = kpe_ref[...]
        vo_ref[...] = kv[:, NOPE:].astype(BF16)

    head = pl.BlockSpec((None, tm, ATT_DK), lambda h, i: (h, i, 0))
    tok = pl.BlockSpec((tm, LANES), lambda h, i: (i, 0))
    return _call(
        body, [qraw, kvraw, kper, cosT, sinT], name="mla_prep", grid=(H, T // tm),
        in_specs=[head, head, tok, tok, tok],
        out_specs=[head, head, pl.BlockSpec((None, tm, VDIM), lambda h, i: (h, i, 0))],
        out_shape=[_sds((H, T, ATT_DK), BF16), _sds((H, T, ATT_DK), BF16), _sds((H, T, VDIM), BF16)],
        sem=("parallel", "parallel"), comm=comm)


def _fox_cumsum(logf, B, S, inv_scale):
    T = logf.shape[0]

    def body(l_ref, c_ref):
        c_ref[...] = _cumsum_rows(l_ref[...], reverse=False) * inv_scale

    seq = pl.BlockSpec((S, LANES), lambda b: (b, 0))
    return _call(body, [logf], name="fox_cumsum", grid=(B,), in_specs=[seq], out_specs=[seq],
                 out_shape=[_sds((T, LANES), F32)], sem=("parallel",))[0][0]


def _fox_prep(proj, cs, lay, comm=None):
    T = proj.shape[0]
    tm = _tile(T, HEAD_ROW_TILE, 16)

    def body(q_ref, k_ref, v_ref, cs_ref, qo_ref, ko_ref, vo_ref):
        h = pl.program_id(0)
        cv = cs_ref[...]
        lane = _lane(cv.shape)
        ccol = jnp.sum(jnp.where(lane == h, cv, 0.0), axis=1, keepdims=True)
        hi, mid, lo = _split3(ccol)
        one = jnp.where(lane < 6, 1.0, 0.0)
        augq = jnp.where(lane == 0, hi, jnp.where(lane == 1, mid, jnp.where(lane == 2, lo, one)))
        augk = jnp.where(lane < 3, 1.0, jnp.where(lane == 3, -hi, jnp.where(lane == 4, -mid, jnp.where(lane == 5, -lo, 0.0))))
        qo_ref[:, :FOX_DIM] = q_ref[...].astype(BF16)
        qo_ref[:, FOX_DIM:] = augq.astype(BF16)
        ko_ref[:, :FOX_DIM] = k_ref[...].astype(BF16)
        ko_ref[:, FOX_DIM:] = augk.astype(BF16)
        vo_ref[...] = v_ref[...].astype(BF16)

    def col(off):
        return pl.BlockSpec((tm, FOX_DIM), lambda h, i: (i, off // FOX_DIM + h))

    head = pl.BlockSpec((None, tm, ATT_DK), lambda h, i: (h, i, 0))
    return _call(
        body, [proj, proj, proj, cs], name="fox_prep", grid=(HEADS, T // tm),
        in_specs=[col(lay["fq"]), col(lay["fk"]), col(lay["fv"]), pl.BlockSpec((tm, LANES), lambda h, i: (i, 0))],
        out_specs=[head, head, pl.BlockSpec((None, tm, VDIM), lambda h, i: (h, i, 0))],
        out_shape=[_sds((HEADS, T, ATT_DK), BF16), _sds((HEADS, T, ATT_DK), BF16), _sds((HEADS, T, VDIM), BF16)],
        sem=("parallel", "parallel"), comm=comm)


def _visible(tq, tk, unit):
    r = lax.broadcasted_iota(jnp.int32, (tq, tk), 0)
    c = lax.broadcasted_iota(jnp.int32, (tq, tk), 1)
    sh = int(math.log2(unit))
    return lax.shift_right_logical(c, sh) <= lax.shift_right_logical(r, sh)


def _attn_fwd(streams, *, B, S, name, comm=None):
    n = len(streams)
    H, T, DK = streams[0][0].shape
    DV = streams[0][2].shape[2]
    tq = _tile(S, ATT_TILE)
    nq = S // tq
    sub = min(ATT_SUB, tq)
    NT = (((1,), (1,)), ((), ()))

    def body(*refs):
        ins, outs, (m_sc, acc_sc) = refs[:3 * n], refs[3 * n:5 * n], refs[5 * n:]
        i, j = pl.program_id(1), pl.program_id(2)

        @pl.when(j == 0)
        def _():
            m_sc[...] = jnp.full(m_sc.shape, NEG_INF, F32)
            acc_sc[...] = jnp.zeros(acc_sc.shape, F32)

        def step(diagonal):
            work = [(t, r) for r in range(tq // sub) for t in range(n)]

            def scores(t, r):
                q_ref, k_ref, _ = ins[3 * t:3 * t + 3]
                kc = (r + 1) * sub if diagonal else tq
                s = lax.dot_general(q_ref[r * sub:(r + 1) * sub, :], k_ref[0:kc, :], NT, preferred_element_type=F32)
                return s * (streams[t][4] * LOG2_E)

            ahead = [scores(*work[w]) for w in range(min(ATT_AHEAD, len(work)))]
            for w, (t, r) in enumerate(work):
                s = ahead.pop(0)
                if w + ATT_AHEAD < len(work):
                    ahead.append(scores(*work[w + ATT_AHEAD]))
                v_ref = ins[3 * t + 2]
                kc = s.shape[1]
                rows = slice(r * sub, (r + 1) * sub)
                if diagonal:
                    own = jnp.where(_visible(sub, sub, streams[t][3]), s[:, kc - sub:], NEG_INF)
                    s = own if kc == sub else jnp.concatenate([s[:, :kc - sub], own], axis=1)
                m_prev = m_sc[t, rows, :]
                mx = s[:, 0:LANES]
                for g in range(1, kc // LANES):
                    mx = jnp.maximum(mx, s[:, g * LANES:(g + 1) * LANES])
                m_new = jnp.maximum(m_prev, jnp.max(mx, axis=1, keepdims=True))
                alpha = jnp.exp2(m_prev - m_new)
                p = jnp.exp2(s - jnp.tile(m_new, (1, kc // LANES))).astype(BF16)
                v_aug = jnp.concatenate([v_ref[0:kc, :], jnp.ones((kc, LANES), BF16)], axis=1)
                acc_sc[t, rows, :] = jnp.tile(alpha, (1, 2)) * acc_sc[t, rows, :] + jnp.dot(
                    p, v_aug, preferred_element_type=F32)
                m_sc[t, rows, :] = m_new

        @pl.when(j < i)
        def _():
            step(False)

        @pl.when(j == i)
        def _():
            step(True)
            for t in range(n):
                o_ref, lse_ref = outs[2 * t:2 * t + 2]
                l = acc_sc[t, :, DV:]
                o_ref[...] = (acc_sc[t, :, :DV] / l).astype(BF16)
                lse_ref[...] = m_sc[t] + jnp.log2(l)

    def qmap(g, i, j):
        return (g % H, (g // H) * nq + i, 0)

    def kmap(g, i, j):
        return (g % H, (g // H) * nq + jnp.minimum(j, i), 0)

    args = [a for st in streams for a in st[:3]]
    outs, landed = _call(
        body, args, name=name, grid=(B * H, nq, nq),
        in_specs=[pl.BlockSpec((None, tq, DK), qmap), pl.BlockSpec((None, tq, DK), kmap),
                  pl.BlockSpec((None, tq, DV), kmap)] * n,
        out_specs=[pl.BlockSpec((tq, DV), lambda g, i, j: ((g // H) * nq + i, g % H)),
                   pl.BlockSpec((None, tq, LANES), qmap)] * n,
        out_shape=[_sds((T, H * DV), BF16), _sds((H, T, LANES), F32)] * n,
        scratch_shapes=[pltpu.VMEM((n, tq, LANES), F32), pltpu.VMEM((n, tq, DV + LANES), F32)],
        sem=("parallel", "parallel", "arbitrary"), comm=comm)
    return [(outs[2 * t], outs[2 * t + 1]) for t in range(n)], landed


def _attn_bwd(streams, *, B, S, name, comm=None):
    n = len(streams)
    H, T, DK = streams[0][0].shape
    DV = streams[0][2].shape[2]
    tq = _tile(S, ATT_TILE)
    nq = S // tq
    sub = min(ATT_SUB, tq)
    NT = (((1,), (1,)), ((), ()))
    TN = (((0,), (0,)), ((), ()))

    def body(*refs):
        ins, outs = refs[:6 * n], refs[6 * n:]
        j, i = pl.program_id(1), pl.program_id(2)

        @pl.when(jnp.logical_and(j == 0, i == 0))
        def _():
            for t in range(n):
                outs[3 * t][...] = jnp.zeros(outs[3 * t].shape, F32)

        @pl.when(i == 0)
        def _():
            for t in range(n):
                outs[3 * t + 1][...] = jnp.zeros(outs[3 * t + 1].shape, F32)
                outs[3 * t + 2][...] = jnp.zeros(outs[3 * t + 2].shape, F32)

        def step(diagonal):
            work = [(t, r) for r in range(tq // sub) for t in range(n)]

            def kcols(r):
                return (r + 1) * sub if diagonal else tq

            def scores(t, r):
                q_ref, k_ref, v_ref, _, do_ref, _ = ins[6 * t:6 * t + 6]
                rows, kc = slice(r * sub, (r + 1) * sub), kcols(r)
                s = lax.dot_general(q_ref[rows, :], k_ref[0:kc, :], NT, preferred_element_type=F32)
                dp = lax.dot_general(do_ref[rows, :], v_ref[0:kc, :], NT, preferred_element_type=F32)
                return s * (streams[t][7] * LOG2_E), dp

            def probs(t, r, s, dp):
                _, _, _, o_ref, do_ref, lse_ref = ins[6 * t:6 * t + 6]
                rows, kc = slice(r * sub, (r + 1) * sub), kcols(r)
                if diagonal:
                    own = jnp.where(_visible(sub, sub, streams[t][6]), s[:, kc - sub:], NEG_INF)
                    s = own if kc == sub else jnp.concatenate([s[:, :kc - sub], own], axis=1)
                p = jnp.exp2(s - jnp.tile(lse_ref[rows, :], (1, kc // LANES)))
                delta = jnp.sum(do_ref[rows, :].astype(F32) * o_ref[rows, :].astype(F32), axis=1, keepdims=True)
                return p.astype(BF16), (p * (dp - delta) * streams[t][7]).astype(BF16)

            def grads(t, r, p, ds):
                q_ref, k_ref, _, _, do_ref, _ = ins[6 * t:6 * t + 6]
                dq_ref, dk_ref, dv_ref = outs[3 * t:3 * t + 3]
                rows, kc = slice(r * sub, (r + 1) * sub), kcols(r)
                dv_ref[0:kc, :] += lax.dot_general(p, do_ref[rows, :], TN, preferred_element_type=F32)
                dk_ref[0:kc, :] += lax.dot_general(ds, q_ref[rows, :], TN, preferred_element_type=F32)
                qrows = pl.ds(pl.multiple_of(i * tq + r * sub, sub), sub)
                dq_ref[qrows, :] += jnp.dot(ds, k_ref[0:kc, :], preferred_element_type=F32)

            nw = len(work)
            sc = {w: scores(*work[w]) for w in range(min(2, nw))}
            pr = {0: probs(*work[0], *sc.pop(0))}
            for w in range(nw):
                if w + 2 < nw:
                    sc[w + 2] = scores(*work[w + 2])
                if w + 1 < nw:
                    pr[w + 1] = probs(*work[w + 1], *sc.pop(w + 1))
                grads(*work[w], *pr.pop(w))

        @pl.when(i > j)
        def _():
            step(False)

        @pl.when(i == j)
        def _():
            step(True)

    def qmap(g, j, i):
        return (g % H, (g // H) * nq + jnp.maximum(i, j), 0)

    def kmap(g, j, i):
        return (g % H, (g // H) * nq + j, 0)

    def omap(g, j, i):
        return ((g // H) * nq + jnp.maximum(i, j), g % H)

    args = [a for st in streams for a in st[:6]]
    outs, landed = _call(
        body, args, name=name, grid=(B * H, nq, nq),
        in_specs=[pl.BlockSpec((None, tq, DK), qmap), pl.BlockSpec((None, tq, DK), kmap),
                  pl.BlockSpec((None, tq, DV), kmap), pl.BlockSpec((tq, DV), omap), pl.BlockSpec((tq, DV), omap),
                  pl.BlockSpec((None, tq, LANES), qmap)] * n,
        out_specs=[pl.BlockSpec((None, S, DK), lambda g, j, i: (g % H, g // H, 0)),
                   pl.BlockSpec((None, tq, DK), kmap), pl.BlockSpec((None, tq, DV), kmap)] * n,
        out_shape=[_sds((H, T, DK), F32), _sds((H, T, DK), F32), _sds((H, T, DV), F32)] * n,
        sem=("parallel", "arbitrary", "arbitrary"), comm=comm)
    return [tuple(outs[3 * t:3 * t + 3]) for t in range(n)], landed


def _gate_merge(am, af, proj, bgate, lay, D, comm=None):
    T = am.shape[0]
    tm = _tile(T, ROW_TILE, 16)
    tn = _tile(D, 1024)

    def body(am_ref, af_ref, gm_ref, gf_ref, bm_ref, bf_ref, o_ref):
        sm = _sigmoid(gm_ref[...] + bm_ref[...])
        sf = _sigmoid(gf_ref[...] + bf_ref[...])
        o_ref[...] = (sm * am_ref[...] + sf * af_ref[...]).astype(BF16)

    og = lay["g"] // tn
    blk = pl.BlockSpec((tm, tn), lambda i, j: (i, j))
    return _call(
        body, [am, af, proj, proj, bgate, bgate], name="gate_merge", grid=(T // tm, D // tn),
        in_specs=[blk, blk, pl.BlockSpec((tm, tn), lambda i, j: (i, og + j)),
                  pl.BlockSpec((tm, tn), lambda i, j: (i, og + D // tn + j)),
                  pl.BlockSpec((1, tn), lambda i, j: (0, j)), pl.BlockSpec((1, tn), lambda i, j: (0, D // tn + j))],
        out_specs=[blk], out_shape=[_sds((T, D), BF16)], sem=("parallel", "parallel"), comm=comm)


def _mid(x, y1, g_pm, g_ffn):
    T, D = x.shape
    tm = _tile(T, ROW_TILE, 16)

    def body(x_ref, y_ref, gp_ref, gf_ref, x1_ref, h2_ref):
        y = y_ref[...]
        x1 = x_ref[...] + y * _rms(y) * gp_ref[...]
        x1_ref[...] = x1
        h2_ref[...] = (x1 * _rms(x1) * gf_ref[...]).astype(BF16)

    row = pl.BlockSpec((tm, D), lambda i: (i, 0))
    vec = pl.BlockSpec((1, D), lambda i: (0, 0))
    return _call(body, [x, y1, g_pm, g_ffn], name="mid", grid=(T // tm,), in_specs=[row, row, vec, vec],
                 out_specs=[row, row], out_shape=[_sds((T, D), F32), _sds((T, D), BF16)], sem=("parallel",))[0]


def _conv3(u, w_ref, bias):
    row = lax.broadcasted_iota(jnp.int32, u.shape, 0)
    u1 = jnp.where(row >= 1, pltpu.roll(u, 1, 0), 0.0)
    u2 = jnp.where(row >= 2, pltpu.roll(u, 2, 0), 0.0)
    return w_ref[0:1, :] * u2 + w_ref[1:2, :] * u1 + w_ref[2:3, :] * u + bias, u1, u2


def _convffn_fwd(u, cw, cb, B, S, F, comm=None):
    T = u.shape[0]
    tn = _tile(F, 256)
    nf = F // tn

    def body(ug_ref, uv_ref, wg_ref, wv_ref, bg_ref, bv_ref, a_ref):
        g, _, _ = _conv3(ug_ref[...], wg_ref, bg_ref[...])
        val, _, _ = _conv3(uv_ref[...], wv_ref, bv_ref[...])
        a_ref[...] = (_gelu_parts(g)[0] * val).astype(BF16)

    def seq(off):
        return pl.BlockSpec((S, tn), lambda b, j: (b, off + j))

    def par(rows, off):
        return pl.BlockSpec((rows, tn), lambda b, j: (0, off + j))

    return _call(body, [u, u, cw, cw, cb, cb], name="convffn_fwd", grid=(B, nf),
                 in_specs=[seq(0), seq(nf), par(3, 0), par(3, nf), par(1, 0), par(1, nf)],
                 out_specs=[seq(0)], out_shape=[_sds((T, F), BF16)], sem=("parallel", "parallel"), comm=comm)


def _convffn_bwd(u, dact, cw, cb, B, S, F, comm=None):
    T = u.shape[0]
    tn = _tile(F, 256)
    nf = F // tn

    def body(ug_ref, uv_ref, da_ref, wg_ref, wv_ref, bg_ref, bv_ref, dug_ref, duv_ref, dpg_ref, dpv_ref):
        b = pl.program_id(1)
        ug, uv, da = ug_ref[...], uv_ref[...], da_ref[...]
        g, ug1, ug2 = _conv3(ug, wg_ref, bg_ref[...])
        val, uv1, uv2 = _conv3(uv, wv_ref, bv_ref[...])
        gel, dgel = _gelu_parts(g)
        dg = da * val * dgel
        dval = da * gel
        row = lax.broadcasted_iota(jnp.int32, ug.shape, 0)

        def back(d, w_ref):
            d1 = jnp.where(row < S - 1, pltpu.roll(d, S - 1, 0), 0.0)
            d2 = jnp.where(row < S - 2, pltpu.roll(d, S - 2, 0), 0.0)
            return w_ref[2:3, :] * d + w_ref[1:2, :] * d1 + w_ref[0:1, :] * d2

        dug_ref[...] = back(dg, wg_ref).astype(BF16)
        duv_ref[...] = back(dval, wv_ref).astype(BF16)

        def sums(d, u0, u1, u2):
            r8 = lax.broadcasted_iota(jnp.int32, (8, d.shape[1]), 0)
            out = jnp.zeros((8, d.shape[1]), F32)
            for k, t in enumerate((d * u2, d * u1, d * u0, d)):
                out = jnp.where(r8 == k, jnp.sum(t, axis=0, keepdims=True), out)
            return out

        _accumulate(dpg_ref, sums(dg, ug, ug1, ug2), b == 0)
        _accumulate(dpv_ref, sums(dval, uv, uv1, uv2), b == 0)

    def seq(off):
        return pl.BlockSpec((S, tn), lambda j, b: (b, off + j))

    def par(rows, off):
        return pl.BlockSpec((rows, tn), lambda j, b: (0, off + j))

    outs, landed = _call(
        body, [u, u, dact, cw, cw, cb, cb], name="convffn_bwd", grid=(nf, B),
        in_specs=[seq(0), seq(nf), seq(0), par(3, 0), par(3, nf), par(1, 0), par(1, nf)],
        out_specs=[seq(0), seq(0), par(8, 0), par(8, 0)],
        out_shape=[_sds((T, F), BF16), _sds((T, F), BF16), _sds((8, F), F32), _sds((8, F), F32)],
        sem=("parallel", "arbitrary"), comm=comm)
    return outs, landed


def _tail(ff, x1, tgt, g):
    T, D = ff.shape
    tm = _tile(T, ROW_TILE, 16)

    def body(ff_ref, x1_ref, t_ref, g_ref, dy_ref, dff_ref, loss_ref, dg_ref):
        i = pl.program_id(0)
        f = ff_ref[...]
        gv = g_ref[...]
        r = _rms(f)
        n = f * r
        e = (x1_ref[...] + n * gv) - t_ref[...]
        dy = e * (1.0 / D)
        dy_ref[...] = dy
        dn = dy * gv
        dff_ref[...] = (r * (dn - n * jnp.mean(dn * n, axis=-1, keepdims=True))).astype(BF16)
        part = 0.5 * jnp.sum(jnp.mean(e * e, axis=-1, keepdims=True), axis=0, keepdims=True)
        _accumulate(loss_ref, jnp.broadcast_to(part, loss_ref.shape), i == 0)
        _accumulate(dg_ref, jnp.sum(dy * n, axis=0, keepdims=True), i == 0)

    row = pl.BlockSpec((tm, D), lambda i: (i, 0))
    vec = pl.BlockSpec((1, D), lambda i: (0, 0))
    return _call(body, [ff, x1, tgt, g], name="tail", grid=(T // tm,), in_specs=[row, row, row, vec],
                 out_specs=[row, row, pl.BlockSpec((8, LANES), lambda i: (0, 0)), vec],
                 out_shape=[_sds((T, D), F32), _sds((T, D), BF16), _sds((8, LANES), F32), _sds((1, D), F32)],
                 sem=("arbitrary",))[0]


def _mid_bwd(dy, dh2, x1, y1, g_ffn, g_pm, comm=None):
    T, D = dy.shape
    tm = _tile(T, ROW_TILE, 16)

    def body(dy_ref, dh_ref, x1_ref, y1_ref, gf_ref, gp_ref, dx1_ref, dy1_ref, dgf_ref, dgp_ref):
        i = pl.program_id(0)
        dh = dh_ref[...]
        d2, dgf = _rms_bwd(dh, x1_ref[...], gf_ref[...])
        dx1 = dy_ref[...] + d2
        dx1_ref[...] = dx1
        d1, dgp = _rms_bwd(dx1, y1_ref[...], gp_ref[...])
        dy1_ref[...] = d1.astype(BF16)
        _accumulate(dgf_ref, jnp.sum(dgf, axis=0, keepdims=True), i == 0)
        _accumulate(dgp_ref, jnp.sum(dgp, axis=0, keepdims=True), i == 0)

    row = pl.BlockSpec((tm, D), lambda i: (i, 0))
    vec = pl.BlockSpec((1, D), lambda i: (0, 0))
    return _call(body, [dy, dh2, x1, y1, g_ffn, g_pm], name="mid_bwd", grid=(T // tm,),
                 in_specs=[row, row, row, row, vec, vec], out_specs=[row, row, vec, vec],
                 out_shape=[_sds((T, D), F32), _sds((T, D), BF16), _sds((1, D), F32), _sds((1, D), F32)],
                 sem=("arbitrary",), comm=comm)


def _gate_bwd(dm, am, af, proj, bgate, lay, D, comm=None):
    T = dm.shape[0]
    tm = _tile(T, ROW_TILE, 16)
    tn = _tile(D, 512)

    def body(dm_ref, am_ref, af_ref, gm_ref, gf_ref, bm_ref, bf_ref,
             dam_ref, daf_ref, dgm_ref, dgf_ref, dbm_ref, dbf_ref):
        i = pl.program_id(1)
        d = dm_ref[...]
        sm = _sigmoid(gm_ref[...] + bm_ref[...])
        sf = _sigmoid(gf_ref[...] + bf_ref[...])
        dam_ref[...] = (d * sm).astype(BF16)
        daf_ref[...] = (d * sf).astype(BF16)
        dgm = d * am_ref[...] * (sm * (1.0 - sm))
        dgf = d * af_ref[...] * (sf * (1.0 - sf))
        dgm_ref[...] = dgm.astype(BF16)
        dgf_ref[...] = dgf.astype(BF16)
        _accumulate(dbm_ref, jnp.sum(dgm, axis=0, keepdims=True), i == 0)
        _accumulate(dbf_ref, jnp.sum(dgf, axis=0, keepdims=True), i == 0)

    og = lay["g"] // tn
    blk = pl.BlockSpec((tm, tn), lambda j, i: (i, j))
    vec = pl.BlockSpec((1, tn), lambda j, i: (0, j))
    return _call(
        body, [dm, am, af, proj, proj, bgate, bgate], name="gate_bwd", grid=(D // tn, T // tm),
        in_specs=[blk, blk, blk, pl.BlockSpec((tm, tn), lambda j, i: (i, og + j)),
                  pl.BlockSpec((tm, tn), lambda j, i: (i, og + D // tn + j)),
                  vec, pl.BlockSpec((1, tn), lambda j, i: (0, D // tn + j))],
        out_specs=[blk, blk, blk, blk, vec, vec],
        out_shape=[_sds((T, D), BF16)] * 4 + [_sds((1, D), F32)] * 2, sem=("parallel", "arbitrary"), comm=comm)


def _mla_bwd_prep(dq, dk, dv, cosT, sinT, comm=None):
    H, T, _ = dq.shape
    tm = _tile(T, HEAD_ROW_TILE, 16)

    def body(dq_ref, dk_ref, dv_ref, cos_ref, sin_ref, dqr_ref, dkv_ref, dkpe_ref):
        h = pl.program_id(1)
        cs, sn = cos_ref[...], sin_ref[...]
        valid = _lane(cs.shape) < ROPE

        def unrope(d):
            d = jnp.where(valid, d, 0.0)
            return d * cs - _rope_rot(d) * sn

        dqv = dq_ref[...]
        dqr_ref[:, :NOPE] = dqv[:, :NOPE].astype(BF16)
        dqr_ref[:, NOPE:] = unrope(dqv[:, NOPE:]).astype(BF16)
        dkv_ = dk_ref[...]
        dkv_ref[:, :NOPE] = dkv_[:, :NOPE].astype(BF16)
        dkv_ref[:, NOPE:] = dv_ref[...].astype(BF16)
        _accumulate(dkpe_ref, unrope(dkv_[:, NOPE:]), h == 0)

    head = pl.BlockSpec((None, tm, ATT_DK), lambda i, h: (h, i, 0))
    tok = pl.BlockSpec((tm, LANES), lambda i, h: (i, 0))
    return _call(
        body, [dq, dk, dv, cosT, sinT], name="mla_bwd_prep", grid=(T // tm, H),
        in_specs=[head, head, pl.BlockSpec((None, tm, VDIM), lambda i, h: (h, i, 0)), tok, tok],
        out_specs=[head, head, tok],
        out_shape=[_sds((H, T, ATT_DK), BF16), _sds((H, T, ATT_DK), BF16), _sds((T, LANES), F32)],
        sem=("parallel", "arbitrary"), comm=comm)


def _fox_bwd_prep(dq, dk, proj, bfor, lay, B, S, inv_scale):
    H, T, _ = dq.shape

    def body(dq_ref, dk_ref, fl_ref, bf_ref, dfl_ref, dbf_ref, dc_sc):
        b, h = pl.program_id(0), pl.program_id(1)
        lane = _lane(dc_sc.shape)
        col = jnp.sum(jnp.where(lane == 0, dq_ref[...], 0.0) - jnp.where(lane == 3, dk_ref[...], 0.0),
                      axis=1, keepdims=True)

        @pl.when(h == 0)
        def _():
            dc_sc[...] = jnp.zeros(dc_sc.shape, F32)

        dc_sc[...] = jnp.where(lane == h, col, dc_sc[...])

        @pl.when(h == H - 1)
        def _():
            dlogf = _cumsum_rows(dc_sc[...] * inv_scale, reverse=True)
            z = fl_ref[...] + bf_ref[...]
            dz = jnp.where(lane < H, dlogf * (1.0 / (1.0 + jnp.exp(z))), 0.0)
            dfl_ref[...] = dz
            _accumulate(dbf_ref, jnp.sum(dz, axis=0, keepdims=True), b == 0)

    aug = pl.BlockSpec((None, S, LANES), lambda b, h: (h, b, 1))
    seq = pl.BlockSpec((S, LANES), lambda b, h: (b, 0))
    vec = pl.BlockSpec((1, LANES), lambda b, h: (0, 0))
    return _call(
        body, [dq, dk, proj, bfor], name="fox_bwd_prep", grid=(B, H),
        in_specs=[aug, aug, pl.BlockSpec((S, LANES), lambda b, h: (b, lay["fl"] // LANES)), vec],
        out_specs=[seq, vec], out_shape=[_sds((T, LANES), F32), _sds((1, LANES), F32)],
        scratch_shapes=[pltpu.VMEM((S, LANES), F32)], sem=("arbitrary", "arbitrary"))[0]


def _lat_bwd(dqn, dkvn, proj, gq, gkv, lay):
    T = dqn.shape[0]
    tm = _tile(T, ROW_TILE, 16)

    def body(dq_ref, dkv_ref, q_ref, kv_ref, gq_ref, gkv_ref, dql_ref, dkl_ref, dgq_ref, dgkv_ref):
        i = pl.program_id(0)
        dql, dgq = _rms_bwd(dq_ref[...], q_ref[...], gq_ref[...])
        dkl, dgkv = _rms_bwd(dkv_ref[...], kv_ref[...], gkv_ref[...])
        dql_ref[...] = dql.astype(BF16)
        dkl_ref[...] = dkl.astype(BF16)
        _accumulate(dgq_ref, jnp.sum(dgq, axis=0, keepdims=True), i == 0)
        _accumulate(dgkv_ref, jnp.sum(dgkv, axis=0, keepdims=True), i == 0)

    def blk(width, off=0):
        return pl.BlockSpec((tm, width), lambda i: (i, off // width))

    def vec(width):
        return pl.BlockSpec((1, width), lambda i: (0, 0))

    return _call(
        body, [dqn, dkvn, proj, proj, gq, gkv], name="lat_bwd", grid=(T // tm,),
        in_specs=[blk(Q_LORA), blk(KV_LORA), blk(Q_LORA, lay["q"]), blk(KV_LORA, lay["kv"]), vec(Q_LORA), vec(KV_LORA)],
        out_specs=[blk(Q_LORA), blk(KV_LORA), vec(Q_LORA), vec(KV_LORA)],
        out_shape=[_sds((T, Q_LORA), BF16), _sds((T, KV_LORA), BF16), _sds((1, Q_LORA), F32), _sds((1, KV_LORA), F32)],
        sem=("arbitrary",))[0]


def _final_dx(dx1, dh, x, g, comm=None):
    T, D = x.shape
    tm = _tile(T, ROW_TILE, 16)

    def body(dx1_ref, dh_ref, x_ref, g_ref, dx_ref, dg_ref):
        i = pl.program_id(0)
        d, dg = _rms_bwd(dh_ref[...], x_ref[...], g_ref[...])
        dx_ref[...] = dx1_ref[...] + d
        _accumulate(dg_ref, jnp.sum(dg, axis=0, keepdims=True), i == 0)

    row = pl.BlockSpec((tm, D), lambda i: (i, 0))
    vec = pl.BlockSpec((1, D), lambda i: (0, 0))
    return _call(body, [dx1, dh, x, g], name="final_dx", grid=(T // tm,), in_specs=[row, row, row, vec],
                 out_specs=[row, vec], out_shape=[_sds((T, D), F32), _sds((1, D), F32)], sem=("arbitrary",), comm=comm)


def _chip_sum(pieces, paired, qc, name):
    G, R, C = pieces.shape
    tr = _tile(R, 256, 16)

    def body(qc_ref, g_ref, p_ref, keep_ref, send_ref):
        s = pl.program_id(1)
        tot = g_ref[...] + p_ref[...]

        @pl.when(s == 0)
        def _():
            keep_ref[...] = tot

        @pl.when(s > 0)
        def _():
            send_ref[...] = tot.astype(send_ref.dtype)

    grid_spec = pltpu.PrefetchScalarGridSpec(
        num_scalar_prefetch=1, grid=(R // tr, N_CHIP),
        in_specs=[pl.BlockSpec((None, tr, C), lambda i, s, qc: (2 * (qc[0] ^ s) + qc[1], i, 0)),
                  pl.BlockSpec((None, tr, C), lambda i, s, qc: (qc[0] ^ s, i, 0))],
        out_specs=[pl.BlockSpec((tr, C), lambda i, s, qc: (i, 0)),
                   pl.BlockSpec((None, tr, C), lambda i, s, qc: (jnp.maximum(s - 1, 0), i, 0))])
    send_dtype = BF16 if R >= 16 else pieces.dtype
    return pl.pallas_call(
        body, name=name, grid_spec=grid_spec,
        out_shape=[_sds((R, C), F32), _sds((3, R, C), send_dtype)],
        compiler_params=pltpu.CompilerParams(dimension_semantics=("arbitrary", "arbitrary"),
                                             vmem_limit_bytes=VMEM_LIMIT_BYTES),
    )(qc, pieces, paired)


def _adamw_math(w, g, m, v):
    m = ADAM_B1 * m + (1.0 - ADAM_B1) * g
    v = ADAM_B2 * v + (1.0 - ADAM_B2) * (g * g)
    m_hat = m / (1.0 - ADAM_B1 ** ADAM_STEP)
    v_hat = v / (1.0 - ADAM_B2 ** ADAM_STEP)
    delta = -ADAM_LR * (m_hat / (jnp.sqrt(v_hat) + ADAM_EPS) + ADAM_WD * w)
    return delta, m, v


def _sum_adamw(keep, pieces, w, m, v, name):
    R, C = w.shape
    P = pieces.shape[0]
    tr = _tile(R, 256, 16)

    def body(*refs):
        if keep is None:
            p_ref, w_ref, m_ref, v_ref, g_ref, d_ref, mo_ref, vo_ref = refs
            g = p_ref[0].astype(F32)
            rest = range(1, P)
        else:
            k_ref, p_ref, w_ref, m_ref, v_ref, g_ref, d_ref, mo_ref, vo_ref = refs
            g = k_ref[...]
            rest = range(P)
        for q in rest:
            g = g + p_ref[q].astype(F32)
        g_ref[...] = g
        d_ref[...], mo_ref[...], vo_ref[...] = _adamw_math(w_ref[...], g, m_ref[...], v_ref[...])

    blk = pl.BlockSpec((tr, C), lambda i: (i, 0))
    pblk = pl.BlockSpec((P, tr, C), lambda i: (0, i, 0))
    args = [pieces, w, m, v] if keep is None else [keep, pieces, w, m, v]
    specs = [pblk, blk, blk, blk] if keep is None else [blk, pblk, blk, blk, blk]
    return _call(body, args, name=name, grid=(R // tr,), in_specs=specs, out_specs=[blk] * 4,
                 out_shape=[_sds((R, C), F32)] * 4, sem=("parallel",))[0]


def _layout(D):
    lay = {"q": 0, "kv": Q_LORA, "kpe": Q_LORA + KV_LORA}
    lay["fq"] = lay["kpe"] + LANES
    lay["fk"] = lay["fq"] + HEADS * FOX_DIM
    lay["fv"] = lay["fk"] + HEADS * FOX_DIM
    lay["fl"] = lay["fv"] + HEADS * FOX_DIM
    lay["g"] = lay["fl"] + LANES
    lay["end"] = lay["g"] + 2 * D
    return lay


def kernel(x, positions, pre_mix_norm, w_in, q_a_norm, w_uq, kv_a_norm, w_ukv, b_forget, b_gate, w_branch_mla, w_branch_fox, w_out, post_mix_norm, pre_ffn_norm, w_up, conv_w, conv_b, w_down, post_ffn_norm, loss_target, m_pre_mix_norm, m_w_in, m_q_a_norm, m_w_uq, m_kv_a_norm, m_w_ukv, m_b_forget, m_b_gate, m_w_branch_mla, m_w_branch_fox, m_w_out, m_post_mix_norm, m_pre_ffn_norm, m_w_up, m_conv_w, m_conv_b, m_w_down, m_post_ffn_norm, v_pre_mix_norm, v_w_in, v_q_a_norm, v_w_uq, v_kv_a_norm, v_w_ukv, v_b_forget, v_b_gate, v_w_branch_mla, v_w_branch_fox, v_w_out, v_post_mix_norm, v_pre_ffn_norm, v_w_up, v_conv_w, v_conv_b, v_w_down, v_post_ffn_norm):
    B, S, D = x.shape
    T = B * S
    F = conv_b.shape[0] // 2
    lay = _layout(D)
    n_in = w_in.shape[1]
    d_in = N_DEV * n_in
    seg_a = Q_LORA + KV_LORA + ROPE
    seg_b = 3 * HEADS * FOX_DIM + HEADS
    mla_scale = (NOPE + ROPE) ** -0.5
    fox_scale = FOX_DIM ** -0.5
    ax, ay, ac = (lax.axis_index(a) for a in MESH_AXES)
    qc = jnp.stack([2 * ax + ay, ac]).astype(jnp.int32)

    def row(vec, width=None):
        vec = vec.reshape(1, -1)
        if width is not None and vec.shape[1] < width:
            vec = jnp.pad(vec, ((0, 0), (0, width - vec.shape[1])))
        return vec

    x2 = x.reshape(T, D)
    win_s = _cast_bf16(w_in, "cast_w_in")
    h, (win_g,) = _prenorm(x2, row(pre_mix_norm), comm=_Comm([_GatherRelayPlan([win_s], mid_frac=0.3)]))
    small_s = [_cast_bf16(w, "cast_" + n) for w, n in
               [(w_uq, "w_uq"), (w_ukv, "w_ukv"), (w_branch_mla, "w_branch_mla"), (w_branch_fox, "w_branch_fox"), (w_out, "w_out")]]
    wup_s = _cast_bf16(w_up, "cast_w_up")
    wdown_s = _cast_bf16(w_down, "cast_w_down")

    def shard_cols(lo, hi):
        out = []
        for g in range(lo // n_in, (hi - 1) // n_in + 1):
            out.append(win_g[g][:, max(lo, g * n_in) - g * n_in:min(hi, (g + 1) * n_in) - g * n_in])
        return out

    w_perm = jnp.concatenate(
        shard_cols(0, seg_a) + [jnp.zeros((D, LANES - ROPE), BF16)] + shard_cols(seg_a, seg_a + seg_b)
        + [jnp.zeros((D, LANES - HEADS), BF16)] + shard_cols(seg_a + seg_b, d_in), axis=1)

    tgt = loss_target.reshape(T, D)
    pos = positions.reshape(T, 1)
    inv_freq = 1.0 / (ROPE_THETA ** (jnp.arange(0, ROPE, 2, dtype=F32) / ROPE))
    invf = row(jnp.concatenate([inv_freq, inv_freq]), LANES)
    g_pre, g_q, g_kv = row(pre_mix_norm), row(q_a_norm), row(kv_a_norm)
    g_pm, g_ffn, g_pf = row(post_mix_norm), row(pre_ffn_norm), row(post_ffn_norm)
    bfor = row(b_forget, LANES)
    bgate = row(b_gate)
    cb_full = row(conv_b)

    def own_plan(blocks):
        return _Comm([_GatherOwnPlan(blocks)])

    def pass_plan(gathered):
        return _Comm([_GatherPassPlan(gathered)])

    def pair_plan(gs):
        return _Comm([_PairScatterPlan(gs)])

    def chip_plan(gs):
        return _Comm([_ChipScatterPlan(gs)])

    half_d = D // 2
    proj, landed = _matmul(h, w_perm, mode="nn", name="mm_proj", comm=_Comm(
        [_GatherOwnPlan(small_s[:2] + [conv_w]), _GatherOwnPlan([wup_s], rows=(0, half_d))]))
    early_g, wup_part = landed[:-1], landed[-1:]
    (qn, kvn, kper, logf, cosT, sinT), (wuq_g, wukv_g, cw_g) = _split_prep(
        proj, pos, invf, g_q, g_kv, bfor, lay, comm=pass_plan(early_g))
    wuq_pad = jnp.pad(wuq_g, ((0, 0), (0, 0), (0, ATT_DK - NOPE - ROPE)))
    cw_full = jnp.transpose(cw_g, (1, 0, 2)).reshape(3, 2 * F)

    qraw = _matmul(qn, wuq_pad, mode="nn", name="mm_q", out_blocks=ATT_DK)
    kvraw = _matmul(kvn, wukv_g, mode="nn", name="mm_kv", out_blocks=NOPE + VDIM)
    (q_mla, k_mla, v_mla), branch_half = _mla_prep(qraw, kvraw, kper, cosT, sinT, comm=own_plan(small_s[2:4]))
    cs = _fox_cumsum(logf, B, S, 1.0 / fox_scale)
    (q_fox, k_fox, v_fox), wout_half = _fox_prep(proj, cs, lay, comm=own_plan(small_s[4:5]))
    ((o_mla, lse_mla), (o_fox, lse_fox)), landed = _attn_fwd(
        [(q_mla, k_mla, v_mla, MLA_UNIT, mla_scale), (q_fox, k_fox, v_fox, 1, fox_scale)], B=B, S=S,
        name="attn_fwd", comm=_Comm([_GatherOwnPlan([wup_s], rows=(half_d, D), into=wup_part),
                                     _GatherPassPlan(branch_half + wout_half)]))
    wup_half, (wbm_g, wbf_g, wout_g) = landed[:1], landed[1:]
    wbm = jnp.transpose(wbm_g, (1, 0, 2)).reshape(HEADS * VDIM, D)
    wbf = jnp.transpose(wbf_g, (1, 0, 2)).reshape(HEADS * FOX_DIM, D)
    wout = wout_g.reshape(D, D)
    a_m = _matmul(o_mla, wbm, mode="nn", name="mm_branch_mla")
    a_f = _matmul(o_fox, wbf, mode="nn", name="mm_branch_fox")
    (merged,), (wup_g,) = _gate_merge(a_m, a_f, proj, bgate, lay, D, comm=pass_plan(wup_half))
    n_up = wup_g.shape[2]
    y1 = _matmul(merged, wout, mode="nn", name="mm_out")
    x1, h2 = _mid(x2, y1, g_pm, g_ffn)
    u, wdown_half = _matmul(h2, wup_g, mode="nn", name="mm_up", tn=n_up, comm=own_plan([wdown_s]))
    (act,), (wdown_g,) = _convffn_fwd(u, cw_full, cb_full, B, S, F, comm=pass_plan(wdown_half))
    wdown = wdown_g.reshape(F, D)
    ff = _matmul(act, wdown, mode="nn", name="mm_down", tk=F // 2)
    dy, dff, loss_part, dg_pf = _tail(ff, x1, tgt, g_pf)

    dact = _matmul(dff, wdown, mode="nt", name="mm_dact", tn=F // 4)
    dw_down = _matmul(act, dff, mode="tn", name="mm_dw_down", tm=F // 4, tn=512).reshape(N_DEV, F // N_DEV, D)
    (du_g, du_v, dcp_g, dcp_v), (pa_down,) = _convffn_bwd(u, dact, cw_full, cb_full, B, S, F, comm=pair_plan([dw_down]))
    keep_down, sb_down = _chip_sum(dw_down, pa_down, qc, "chipsum_w_down")
    dh2, (rb_down,) = _matmul_halves((du_g, du_v), wup_g, mode="nt", name="mm_dh2", tm=MM_TILE, comm=chip_plan([sb_down]))
    dw_up = _matmul_halves(h2, (du_g, du_v), mode="tn", name="mm_dw_up", tm=MM_TILE, tn=n_up, tk=T // 2)
    (dx1, dy1, dg_ffn, dg_pm), _ = _mid_bwd(dy, dh2, x1, y1, g_ffn, g_pm)
    dmerged = _matmul(dy1, wout, mode="nt", name="mm_dmerged")
    dw_out = _matmul(merged, dy1, mode="tn", name="mm_dw_out").reshape(N_DEV, D // N_DEV, D)
    (da_m, da_f, dgl_m, dgl_f, dbg_m, dbg_f), (pa_up,) = _gate_bwd(
        dmerged, a_m, a_f, proj, bgate, lay, D, comm=pair_plan([dw_up]))
    keep_up, sb_up = _chip_sum(dw_up, pa_up, qc, "chipsum_w_up")
    dw_bm = _matmul(o_mla, da_m, mode="tn", name="mm_dw_branch_mla", out_blocks=D // N_DEV)
    dw_bf = _matmul(o_fox, da_f, mode="tn", name="mm_dw_branch_fox", out_blocks=D // N_DEV)
    mix = [dw_out, dw_bm, dw_bf]
    do_mla, pa_mix = _matmul(da_m, wbm, mode="nt", name="mm_do_mla", out_dtype=BF16, comm=pair_plan(mix))
    do_fox = _matmul(da_f, wbf, mode="nt", name="mm_do_fox", out_dtype=BF16)
    mix_sums = [_chip_sum(g, p, qc, "chipsum_" + n) for g, p, n in zip(mix, pa_mix, ["w_out", "w_branch_mla", "w_branch_fox"])]
    ((dq_m, dk_m, dv_m), (dq_f, dk_f, dv_f)), (rb_up,) = _attn_bwd(
        [(q_mla, k_mla, v_mla, o_mla, do_mla, lse_mla, MLA_UNIT, mla_scale),
         (q_fox, k_fox, v_fox, o_fox, do_fox, lse_fox, 1, fox_scale)], B=B, S=S, name="attn_bwd",
        comm=chip_plan([sb_up]))
    (dqraw, dkvraw, dkpe), rb_mix = _mla_bwd_prep(dq_m, dk_m, dv_m, cosT, sinT, comm=chip_plan([s[1] for s in mix_sums]))
    dqn = _matmul(dqraw, wuq_pad, mode="nt", name="mm_dqn")
    dw_uq = _matmul(qn, dqraw, mode="tn", name="mm_dw_uq", out_blocks=ATT_DK)[:, :, :NOPE + ROPE]
    dkvn = _matmul(dkvraw, wukv_g, mode="nt", name="mm_dkvn")
    dw_ukv = _matmul(kvn, dkvraw, mode="tn", name="mm_dw_ukv", out_blocks=NOPE + VDIM)
    dqlat, dkvlat, dg_q, dg_kv = _lat_bwd(dqn, dkvn, proj, g_q, g_kv, lay)
    dfl, dbfor = _fox_bwd_prep(dq_f, dk_f, proj, bfor, lay, B, S, 1.0 / fox_scale)
    dproj = _concat_cols([dqlat, dkvlat, dkpe, dq_f, dk_f, dv_f, dfl, dgl_m, dgl_f], "concat_dproj")
    dw_perm = _matmul(h, dproj, mode="tn", name="mm_dw_in")
    segs = [(0, seg_a, 0), (seg_a, seg_a + seg_b, lay["fq"] - seg_a), (seg_a + seg_b, d_in, lay["g"] - seg_a - seg_b)]

    def piece(g):
        lo, hi = g * n_in, (g + 1) * n_in
        parts = [dw_perm[:, max(lo, s0) + sh:min(hi, s1) + sh] for s0, s1, sh in segs if max(lo, s0) < min(hi, s1)]
        return parts[0] if len(parts) == 1 else jnp.concatenate(parts, axis=1)

    dw_in = jnp.stack([piece(g) for g in range(N_DEV)])
    dcw = jnp.transpose(jnp.concatenate([dcp_g[0:3], dcp_v[0:3]], axis=1).reshape(3, N_DEV, (2 * F) // N_DEV), (1, 0, 2))
    late = [dw_in, dw_uq, dw_ukv, dcw]
    pa_late = _exchange_alone(pair_plan(late), "pair_late")
    late_sums = [_chip_sum(g, p, qc, "chipsum_" + n) for g, p, n in zip(late, pa_late, ["w_in", "w_uq", "w_ukv", "conv_w"])]
    dh, rb_late = _matmul(dproj, w_perm, mode="nt", name="mm_dh", tn=2048, tk=2048, comm=chip_plan([s[1] for s in late_sums]))
    (grad_x, dg_pre), _ = _final_dx(dx1, dh, x2, g_pre)

    big_out = {}

    def finish(n, keep, pieces, w, m, v):
        big_out[n] = _sum_adamw(keep, pieces, w, m, v, "adamw_" + n)

    finish("w_down", keep_down, rb_down, w_down, m_w_down, v_w_down)
    finish("w_up", keep_up, rb_up, w_up, m_w_up, v_w_up)
    finish("w_out", mix_sums[0][0], rb_mix[0], w_out, m_w_out, v_w_out)
    finish("w_branch_mla", mix_sums[1][0], rb_mix[1], w_branch_mla, m_w_branch_mla, v_w_branch_mla)
    finish("w_branch_fox", mix_sums[2][0], rb_mix[2], w_branch_fox, m_w_branch_fox, v_w_branch_fox)
    finish("w_in", late_sums[0][0], rb_late[0], w_in, m_w_in, v_w_in)
    finish("w_uq", late_sums[1][0], rb_late[1], w_uq, m_w_uq, v_w_uq)
    finish("w_ukv", late_sums[2][0], rb_late[2], w_ukv, m_w_ukv, v_w_ukv)
    finish("conv_w", late_sums[3][0], rb_late[3], conv_w, m_conv_w, v_conv_w)

    widths = [D, Q_LORA, KV_LORA, LANES, 2 * D, D, D, 2 * F, D]
    small_names = ["pre_mix_norm", "q_a_norm", "kv_a_norm", "b_forget", "b_gate", "post_mix_norm", "pre_ffn_norm",
                   "conv_b", "post_ffn_norm"]
    true_w = [D, Q_LORA, KV_LORA, HEADS, 2 * D, D, D, 2 * F, D]
    dcb = jnp.concatenate([dcp_g[3:4], dcp_v[3:4]], axis=1)
    part = jnp.concatenate([dg_pre, dg_q, dg_kv, dbfor, dbg_m, dbg_f, dg_pm, dg_ffn, dcb, dg_pf], axis=1)

    def pack(vals):
        return jnp.concatenate([row(a, wd) for a, wd in zip(vals, widths)], axis=1)

    sw = pack([pre_mix_norm, q_a_norm, kv_a_norm, b_forget, b_gate, post_mix_norm, pre_ffn_norm, conv_b, post_ffn_norm])
    sm = pack([m_pre_mix_norm, m_q_a_norm, m_kv_a_norm, m_b_forget, m_b_gate, m_post_mix_norm, m_pre_ffn_norm,
               m_conv_b, m_post_ffn_norm])
    sv = pack([v_pre_mix_norm, v_q_a_norm, v_kv_a_norm, v_b_forget, v_b_gate, v_post_mix_norm, v_pre_ffn_norm,
               v_conv_b, v_post_ffn_norm])
    (parts_all,) = _exchange_alone(_Comm([_DirectGatherPlan([part])]), "gather_small")
    sg, sd, smo, svo = _sum_adamw(None, parts_all, sw, sm, sv, "adamw_small")
    small_out = {}
    off = 0
    for n, wd, tw in zip(small_names, widths, true_w):
        small_out[n] = tuple(a[0, off:off + tw] for a in (sg, sd, smo, svo))
        off += wd

    loss = lax.psum(loss_part[0, 0], MESH_AXES)
    order = ["pre_mix_norm", "w_in", "q_a_norm", "w_uq", "kv_a_norm", "w_ukv", "b_forget", "b_gate", "w_branch_mla",
             "w_branch_fox", "w_out", "post_mix_norm", "pre_ffn_norm", "w_up", "conv_w", "conv_b", "w_down",
             "post_ffn_norm"]
    res = {**big_out, **small_out}
    outs = [loss, grad_x.reshape(B, S, D)]
    for kind in range(4):
        outs += [res[n][kind] for n in order]
    return tuple(outs)
```

```python
import math

import jax
import jax.numpy as jnp
from jax import lax
from jax.experimental import pallas as pl
from jax.experimental.pallas import tpu as pltpu

F32 = jnp.float32
BF16 = jnp.bfloat16

N_DEV = 8
N_CHIP = 4
HEADS = 8
NOPE = 128
ROPE = 64
HALF_ROPE = ROPE // 2
VDIM = 128
Q_LORA = 512
KV_LORA = 256
FOX_DIM = 128
ATT_DK = 256
MLA_UNIT = 64
ROPE_THETA = 10000.0
EPS = 1e-6
NEG_INF = -1e30
LANES = 128
LOG2_E = 1.4426950408889634

ADAM_LR = 0.001
ADAM_B1 = 0.9
ADAM_B2 = 0.999
ADAM_EPS = 1e-08
ADAM_WD = 0.01
ADAM_STEP = 10

VMEM_LIMIT_BYTES = 56 * 1024 * 1024
ROW_TILE = 256
HEAD_ROW_TILE = 1024
ATT_TILE = 1024
ATT_SUB = 256
ATT_AHEAD = 3
MM_TILE = 1024

MESH_AXES = ("x", "y", "c")
ANY = pl.BlockSpec(memory_space=pl.ANY)


def _tile(n, pref, align=LANES):
    if n <= pref:
        return n
    t = (pref // align) * align
    while t >= align:
        if n % t == 0:
            return t
        t -= align
    return n


def _sds(shape, dtype):
    return jax.ShapeDtypeStruct(shape, dtype)


def _coords():
    x, y, c = (lax.axis_index(ax) for ax in MESH_AXES)
    return x, y, c


def _chip_rel(x, y, r):
    return (1 - x if r & 2 else x), (1 - y if r & 1 else y)


def _rcopy(src, dst, sems, w, k, dev):
    return pltpu.make_async_remote_copy(src_ref=src, dst_ref=dst, send_sem=sems[0].at[w, k], recv_sem=sems[1].at[w, k],
                                        device_id=dev, device_id_type=pl.DeviceIdType.MESH)


class _GatherRelayPlan:
    def __init__(self, blocks, mid_frac=0.5):
        self.ins = list(blocks)
        self.out_shapes = [_sds((N_DEV,) + b.shape, b.dtype) for b in blocks]
        n = len(blocks)
        self.scratch = [pltpu.SemaphoreType.DMA((n, 7)), pltpu.SemaphoreType.DMA((n, 7)), pltpu.SemaphoreType.DMA((n,))]
        self.mid_frac = mid_frac

    @staticmethod
    def _places():
        x, y, c = _coords()
        xn, yn = 4 * (1 - x) + 2 * y, 4 * x + 2 * (1 - y)
        relay_src = 4 * (x + c * (1 - 2 * x)) + 2 * (y + (1 - c) * (1 - 2 * y)) + c
        relay_to = (x + (1 - c) * (1 - 2 * x), y + c * (1 - 2 * y), c)
        return x, y, c, xn, yn, relay_src, relay_to, 4 * (1 - x) + 2 * (1 - y)

    def first(self, ins, outs, sems):
        x, y, c, _, _, _, _, _ = self._places()
        me = 4 * x + 2 * y + c
        for w in range(len(ins)):
            pltpu.make_async_copy(ins[w], outs[w].at[me], sems[2].at[w]).start()
            _rcopy(ins[w], outs[w].at[me], sems, w, 0, (x, y, 1 - c)).start()
            _rcopy(ins[w], outs[w].at[me], sems, w, 1, (1 - x, y, c)).start()
            _rcopy(ins[w], outs[w].at[me], sems, w, 2, (x, 1 - y, c)).start()

    def mid(self, ins, outs, sems):
        x, y, c, xn, yn, relay_src, relay_to, _ = self._places()
        sib = (x, y, 1 - c)
        for w in range(len(ins)):
            bx, by = outs[w].at[xn + c], outs[w].at[yn + c]
            _rcopy(ins[w], bx, sems, w, 1, (1 - x, y, c)).wait_recv()
            _rcopy(ins[w], by, sems, w, 2, (x, 1 - y, c)).wait_recv()
            _rcopy(outs[w].at[relay_src], outs[w].at[relay_src], sems, w, 3, relay_to).start()
            _rcopy(bx, bx, sems, w, 4, sib).start()
            _rcopy(by, by, sems, w, 5, sib).start()

    def last(self, ins, outs, sems):
        x, y, c, xn, yn, _, relay_to, dg = self._places()
        me = 4 * x + 2 * y + c
        sib = (x, y, 1 - c)
        for w in range(len(ins)):
            bd = outs[w].at[dg + c]
            _rcopy(ins[w], bd, sems, w, 3, relay_to).wait_recv()
            _rcopy(bd, bd, sems, w, 6, sib).start()
            for k, blk in ((0, 4 * x + 2 * y), (4, xn), (5, yn), (6, dg)):
                _rcopy(ins[w], outs[w].at[blk + 1 - c], sems, w, k, sib).wait_recv()
            for k in range(7):
                _rcopy(ins[w], outs[w].at[me], sems, w, k, sib).wait_send()
            pltpu.make_async_copy(ins[w], outs[w].at[me], sems[2].at[w]).wait()


class _GatherOwnPlan:
    mid = None

    def __init__(self, blocks, rows=None, into=None):
        self.n = len(blocks)
        self.rows = rows
        self.ins = list(blocks) + list(into or [])
        self.out_shapes = [_sds((N_DEV,) + b.shape, b.dtype) for b in blocks]
        self.aliases = [(self.n + i, i) for i in range(len(into or []))]
        n = self.n
        self.scratch = [pltpu.SemaphoreType.DMA((n, 4)), pltpu.SemaphoreType.DMA((n, 4)), pltpu.SemaphoreType.DMA((n,))]

    def _cut(self, ref):
        return ref if self.rows is None else ref.at[pl.ds(self.rows[0], self.rows[1] - self.rows[0])]

    def first(self, ins, outs, sems):
        x, y, c = _coords()
        me = 4 * x + 2 * y + c
        for w in range(self.n):
            src, dst = self._cut(ins[w]), self._cut(outs[w].at[me])
            pltpu.make_async_copy(src, dst, sems[2].at[w]).start()
            _rcopy(src, dst, sems, w, 0, (x, y, 1 - c)).start()
            for r in (1, 2, 3):
                px, py = _chip_rel(x, y, r)
                _rcopy(src, dst, sems, w, r, (px, py, c)).start()

    def last(self, ins, outs, sems):
        x, y, c = _coords()
        me = 4 * x + 2 * y + c
        for w in range(self.n):
            src = self._cut(ins[w])
            cp = _rcopy(src, self._cut(outs[w].at[4 * x + 2 * y + 1 - c]), sems, w, 0, (x, y, 1 - c))
            cp.wait_recv()
            cp.wait_send()
            for r in (1, 2, 3):
                px, py = _chip_rel(x, y, r)
                cp = _rcopy(src, self._cut(outs[w].at[4 * px + 2 * py + c]), sems, w, r, (px, py, c))
                cp.wait_recv()
                cp.wait_send()
            pltpu.make_async_copy(src, self._cut(outs[w].at[me]), sems[2].at[w]).wait()


class _GatherPassPlan:
    mid = None

    def __init__(self, gathered):
        self.ins = list(gathered)
        self.out_shapes = [_sds(g.shape, g.dtype) for g in gathered]
        self.aliases = [(i, i) for i in range(len(gathered))]
        n = len(gathered)
        self.scratch = [pltpu.SemaphoreType.DMA((n, 3)), pltpu.SemaphoreType.DMA((n, 3))]

    def first(self, ins, outs, sems):
        x, y, c = _coords()
        for w in range(len(ins)):
            for r in (1, 2, 3):
                px, py = _chip_rel(x, y, r)
                blk = 4 * px + 2 * py + c
                _rcopy(ins[w].at[blk], outs[w].at[blk], sems, w, r - 1, (x, y, 1 - c)).start()

    def last(self, ins, outs, sems):
        x, y, c = _coords()
        for w in range(len(ins)):
            for r in (1, 2, 3):
                px, py = _chip_rel(x, y, r)
                blk = 4 * px + 2 * py + 1 - c
                cp = _rcopy(ins[w].at[blk], outs[w].at[blk], sems, w, r - 1, (x, y, 1 - c))
                cp.wait_recv()
                cp.wait_send()


class _DirectGatherPlan:
    mid = None

    def __init__(self, blocks):
        self.ins = list(blocks)
        self.out_shapes = [_sds((N_DEV,) + b.shape, b.dtype) for b in blocks]
        n = len(blocks)
        self.scratch = [pltpu.SemaphoreType.DMA((n, 7)), pltpu.SemaphoreType.DMA((n, 7)), pltpu.SemaphoreType.DMA((n,))]

    @staticmethod
    def _peer(x, y, c, r):
        return (1 - x if r & 4 else x), (1 - y if r & 2 else y), (1 - c if r & 1 else c)

    def first(self, ins, outs, sems):
        x, y, c = _coords()
        me = 4 * x + 2 * y + c
        for w in range(len(ins)):
            pltpu.make_async_copy(ins[w], outs[w].at[me], sems[2].at[w]).start()
            for r in range(1, N_DEV):
                _rcopy(ins[w], outs[w].at[me], sems, w, r - 1, self._peer(x, y, c, r)).start()

    def last(self, ins, outs, sems):
        x, y, c = _coords()
        me = 4 * x + 2 * y + c
        for w in range(len(ins)):
            for r in range(1, N_DEV):
                px, py, pc = self._peer(x, y, c, r)
                cp = _rcopy(ins[w], outs[w].at[4 * px + 2 * py + pc], sems, w, r - 1, (px, py, pc))
                cp.wait_recv()
                cp.wait_send()
            pltpu.make_async_copy(ins[w], outs[w].at[me], sems[2].at[w]).wait()


class _PairScatterPlan:
    mid = None

    def __init__(self, pieces, rows=None, into=None):
        self.n = len(pieces)
        self.rows = rows
        self.ins = list(pieces) + list(into or [])
        self.out_shapes = [_sds((N_CHIP,) + p.shape[1:], p.dtype) for p in pieces]
        self.aliases = [(self.n + i, i) for i in range(len(into or []))]
        self.scratch = [pltpu.SemaphoreType.DMA((self.n, N_CHIP)), pltpu.SemaphoreType.DMA((self.n, N_CHIP))]

    def _copies(self, ins, outs, sems):
        x, y, c = _coords()
        cps = []
        for w in range(self.n):
            for q in range(N_CHIP):
                src, dst = ins[w].at[2 * q + 1 - c], outs[w].at[q]
                if self.rows is not None:
                    cut = pl.ds(self.rows[0], self.rows[1] - self.rows[0])
                    src, dst = src.at[cut], dst.at[cut]
                cps.append(_rcopy(src, dst, sems, w, q, (x, y, 1 - c)))
        return cps

    def first(self, ins, outs, sems):
        for cp in self._copies(ins, outs, sems):
            cp.start()

    def last(self, ins, outs, sems):
        for cp in self._copies(ins, outs, sems):
            cp.wait_recv()
            cp.wait_send()


class _ChipScatterPlan:
    mid = None

    def __init__(self, sums, rows=None, into=None):
        self.n = len(sums)
        self.rows = rows
        self.ins = list(sums) + list(into or [])
        self.out_shapes = [_sds(s.shape, s.dtype) for s in sums]
        self.aliases = [(self.n + i, i) for i in range(len(into or []))]
        self.scratch = [pltpu.SemaphoreType.DMA((self.n, 3)), pltpu.SemaphoreType.DMA((self.n, 3))]

    def _copies(self, ins, outs, sems):
        x, y, c = _coords()
        cps = []
        for w in range(self.n):
            for r in (1, 2, 3):
                px, py = _chip_rel(x, y, r)
                src, dst = ins[w].at[r - 1], outs[w].at[r - 1]
                if self.rows is not None:
                    cut = pl.ds(self.rows[0], self.rows[1] - self.rows[0])
                    src, dst = src.at[cut], dst.at[cut]
                cps.append(_rcopy(src, dst, sems, w, r - 1, (px, py, c)))
        return cps

    def first(self, ins, outs, sems):
        for cp in self._copies(ins, outs, sems):
            cp.start()

    def last(self, ins, outs, sems):
        for cp in self._copies(ins, outs, sems):
            cp.wait_recv()
            cp.wait_send()


class _Comm:
    def __init__(self, plans):
        self.plans = list(plans)
        self.ins = [a for p in self.plans for a in p.ins]
        self.out_shapes = [s for p in self.plans for s in p.out_shapes]
        self.scratch = [s for p in self.plans for s in p.scratch]
        self.aliases = []
        i = o = 0
        for p in self.plans:
            self.aliases += [(i + a, o + b) for a, b in getattr(p, "aliases", [])]
            i, o = i + len(p.ins), o + len(p.out_shapes)

    def _parts(self, ins, outs, sems):
        i = o = s = 0
        for p in self.plans:
            yield p, ins[i:i + len(p.ins)], outs[o:o + len(p.out_shapes)], sems[s:s + len(p.scratch)]
            i, o, s = i + len(p.ins), o + len(p.out_shapes), s + len(p.scratch)

    def begin(self, step, nsteps, ins, outs, sems):
        @pl.when(step == 0)
        def _():
            for p, pi, po, ps in self._parts(ins, outs, sems):
                p.first(pi, po, ps)

        for p, pi, po, ps in self._parts(ins, outs, sems):
            if p.mid is not None:
                @pl.when(step == min(nsteps - 1, int(p.mid_frac * nsteps)))
                def _(p=p, pi=pi, po=po, ps=ps):
                    p.mid(pi, po, ps)

    def end(self, step, nsteps, ins, outs, sems):
        @pl.when(step == nsteps - 1)
        def _():
            for p, pi, po, ps in self._parts(ins, outs, sems):
                p.last(pi, po, ps)


def _call(body, args, *, name, grid, in_specs, out_specs, out_shape, scratch_shapes=(), sem=None, comm=None):
    in_specs, out_specs, out_shape, scratch_shapes = list(in_specs), list(out_specs), list(out_shape), list(scratch_shapes)
    if comm is None:
        res = pl.pallas_call(
            body, name=name, grid=grid, in_specs=in_specs, out_specs=out_specs, out_shape=out_shape,
            scratch_shapes=scratch_shapes,
            compiler_params=pltpu.CompilerParams(dimension_semantics=sem, vmem_limit_bytes=VMEM_LIMIT_BYTES),
        )(*args)
        return list(res), []
    n_in, n_out, n_sc = len(in_specs), len(out_specs), len(scratch_shapes)
    n_ci, n_co = len(comm.ins), len(comm.out_shapes)
    nsteps = math.prod(grid)

    def hosted(*refs):
        ins, cins = refs[:n_in], refs[n_in:n_in + n_ci]
        o0 = n_in + n_ci
        outs, couts = refs[o0:o0 + n_out], refs[o0 + n_out:o0 + n_out + n_co]
        s0 = o0 + n_out + n_co
        scr, csems = refs[s0:s0 + n_sc], refs[s0 + n_sc:]
        step = jnp.int32(0)
        for d in range(len(grid)):
            step = step * grid[d] + pl.program_id(d)
        comm.begin(step, nsteps, cins, couts, csems)
        body(*ins, *outs, *scr)
        comm.end(step, nsteps, cins, couts, csems)

    res = pl.pallas_call(
        hosted, name=name, grid=grid, in_specs=in_specs + [ANY] * n_ci, out_specs=out_specs + [ANY] * n_co,
        out_shape=out_shape + comm.out_shapes, scratch_shapes=scratch_shapes + comm.scratch,
        input_output_aliases={n_in + a: n_out + b for a, b in comm.aliases},
        compiler_params=pltpu.CompilerParams(dimension_semantics=("arbitrary",) * len(grid),
                                             vmem_limit_bytes=VMEM_LIMIT_BYTES, has_side_effects=True),
    )(*args, *comm.ins)
    return list(res[:n_out]), list(res[n_out:])


def _exchange_alone(comm, name):
    def body():
        pass

    return _call(body, [], name=name, grid=(), in_specs=[], out_specs=[], out_shape=[], comm=comm)[1]


def _matmul(a, b, *, mode, name, out_dtype=F32, out_blocks=None, tm=None, tn=None, tk=None, comm=None):
    tm = MM_TILE if tm is None else tm
    tn = MM_TILE if tn is None else tn
    a_blk = a.ndim == 3
    b_blk = b.ndim == 3
    if mode == "nn":
        M, K = a.shape
        N = b.shape[0] * b.shape[2] if b_blk else b.shape[1]
        dims = (((1,), (0,)), ((), ()))
    elif mode == "nt":
        M = a.shape[1] if a_blk else a.shape[0]
        K = a.shape[0] * a.shape[2] if a_blk else a.shape[1]
        N = b.shape[1] if b_blk else b.shape[0]
        dims = (((1,), (1,)), ((), ()))
    else:
        K, M = a.shape
        N = b.shape[0] * b.shape[2] if b_blk else b.shape[1]
        dims = (((0,), (0,)), ((), ()))

    tm = _tile(M, tm)
    tn = _tile(N, tn)
    if mode == "nt" and (a_blk or b_blk):
        tk = a.shape[2] if a_blk else b.shape[2]
    else:
        tk = _tile(K, K if tk is None else tk)
    if mode != "nt" and b_blk:
        tn = _tile(b.shape[2], tn)
    if out_blocks is not None:
        tn = _tile(out_blocks, tn)
    nk = K // tk
    grid = (M // tm, N // tn, nk)

    if mode == "nn":
        a_spec = pl.BlockSpec((tm, tk), lambda i, j, k: (i, k))
        if b_blk:
            rb = b.shape[2] // tn
            b_spec = pl.BlockSpec((None, tk, tn), lambda i, j, k: (j // rb, k, j % rb))
        else:
            b_spec = pl.BlockSpec((tk, tn), lambda i, j, k: (k, j))
    elif mode == "nt":
        if a_blk:
            a_spec = pl.BlockSpec((None, tm, tk), lambda i, j, k: (k, i, 0))
        else:
            a_spec = pl.BlockSpec((tm, tk), lambda i, j, k: (i, k))
        if b_blk:
            b_spec = pl.BlockSpec((None, tn, tk), lambda i, j, k: (k, j, 0))
        else:
            b_spec = pl.BlockSpec((tn, tk), lambda i, j, k: (j, k))
    else:
        a_spec = pl.BlockSpec((tk, tm), lambda i, j, k: (k, i))
        if b_blk:
            rb = b.shape[2] // tn
            b_spec = pl.BlockSpec((None, tk, tn), lambda i, j, k: (j // rb, k, j % rb))
        else:
            b_spec = pl.BlockSpec((tk, tn), lambda i, j, k: (k, j))

    if out_blocks is None:
        o_spec = pl.BlockSpec((tm, tn), lambda i, j, k: (i, j))
        o_shape = _sds((M, N), out_dtype)
    else:
        ro = out_blocks // tn
        o_spec = pl.BlockSpec((None, tm, tn), lambda i, j, k: (j // ro, i, j % ro))
        o_shape = _sds((N // out_blocks, M, out_blocks), out_dtype)

    direct = nk == 1 or out_dtype == F32

    def body(a_ref, b_ref, o_ref, *scratch):
        if nk == 1:
            o_ref[...] = lax.dot_general(a_ref[...], b_ref[...], dims, preferred_element_type=F32).astype(o_ref.dtype)
            return
        acc_ref = o_ref if direct else scratch[0]
        k = pl.program_id(2)

        @pl.when(k == 0)
        def _():
            acc_ref[...] = jnp.zeros(acc_ref.shape, F32)

        acc_ref[...] += lax.dot_general(a_ref[...], b_ref[...], dims, preferred_element_type=F32)
        if not direct:
            @pl.when(k == nk - 1)
            def _():
                o_ref[...] = acc_ref[...].astype(o_ref.dtype)

    scratch = [] if direct else [pltpu.VMEM((tm, tn), F32)]
    outs, landed = _call(body, [a, b], name=name, grid=grid, in_specs=[a_spec, b_spec], out_specs=[o_spec],
                         out_shape=[o_shape], scratch_shapes=scratch, sem=("parallel", "parallel", "arbitrary"), comm=comm)
    return outs[0] if comm is None else (outs[0], landed)


def _matmul_halves(a, b, *, mode, name, tm, tn=None, tk=None, comm=None):
    if mode == "nt":
        lo, hi = a
        M, kh = lo.shape
        G, N, kb = b.shape
        half = kh // kb
        tm, tn = _tile(M, tm), _tile(N, N if tn is None else tn)
        dims = (((1,), (1,)), ((), ()))

        def body(lo_ref, hi_ref, b_ref, o_ref):
            k = pl.program_id(2)

            @pl.when(k == 0)
            def _():
                o_ref[...] = jnp.zeros(o_ref.shape, F32)

            @pl.when(k < half)
            def _():
                o_ref[...] += lax.dot_general(lo_ref[...], b_ref[...], dims, preferred_element_type=F32)

            @pl.when(k >= half)
            def _():
                o_ref[...] += lax.dot_general(hi_ref[...], b_ref[...], dims, preferred_element_type=F32)

        outs, landed = _call(
            body, [lo, hi, b], name=name, grid=(M // tm, N // tn, G),
            in_specs=[pl.BlockSpec((tm, kb), lambda i, j, k: (i, jnp.minimum(k, half - 1))),
                      pl.BlockSpec((tm, kb), lambda i, j, k: (i, jnp.maximum(k - half, 0))),
                      pl.BlockSpec((None, tn, kb), lambda i, j, k: (k, j, 0))],
            out_specs=[pl.BlockSpec((tm, tn), lambda i, j, k: (i, j))], out_shape=[_sds((M, N), F32)],
            sem=("parallel", "parallel", "arbitrary"), comm=comm)
    else:
        lo, hi = b
        K, nh = lo.shape
        M = a.shape[1]
        n = tn
        half = nh // n
        tm, tk = _tile(M, tm), _tile(K, K if tk is None else tk)
        nk = K // tk
        dims = (((0,), (0,)), ((), ()))

        def body(a_ref, lo_ref, hi_ref, o_ref):
            j, k = pl.program_id(1), pl.program_id(2)

            @pl.when(k == 0)
            def _():
                o_ref[...] = jnp.zeros(o_ref.shape, F32)

            @pl.when(j < half)
            def _():
                o_ref[...] += lax.dot_general(a_ref[...], lo_ref[...], dims, preferred_element_type=F32)

            @pl.when(j >= half)
            def _():
                o_ref[...] += lax.dot_general(a_ref[...], hi_ref[...], dims, preferred_element_type=F32)

        outs, landed = _call(
            body, [a, lo, hi], name=name, grid=(M // tm, 2 * half, nk),
            in_specs=[pl.BlockSpec((tk, tm), lambda i, j, k: (k, i)),
                      pl.BlockSpec((tk, n), lambda i, j, k: (jnp.where(j < half, k, nk - 1), jnp.minimum(j, half - 1))),
                      pl.BlockSpec((tk, n), lambda i, j, k: (jnp.where(j >= half, k, 0), jnp.maximum(j - half, 0)))],
            out_specs=[pl.BlockSpec((None, tm, n), lambda i, j, k: (j, i, 0))],
            out_shape=[_sds((2 * half, M, n), F32)], sem=("parallel", "parallel", "arbitrary"), comm=comm)
    return outs[0] if comm is None else (outs[0], landed)


def _rms(x):
    return lax.rsqrt(jnp.mean(x * x, axis=-1, keepdims=True) + EPS)


def _rms_bwd(dy, x, g):
    r = _rms(x)
    n = x * r
    dn = dy * g
    dx = r * (dn - n * jnp.mean(dn * n, axis=-1, keepdims=True))
    return dx, dy * n


def _sigmoid(x):
    return 1.0 / (1.0 + jnp.exp(-x))


def _rope_rot(t):
    return pltpu.roll(t, HALF_ROPE, 1) - pltpu.roll(t, LANES - HALF_ROPE, 1)


def _lane(shape):
    return lax.broadcasted_iota(jnp.int32, shape, 1)


def _split3(x):
    hi = x.astype(BF16).astype(F32)
    r1 = x - hi
    mid = r1.astype(BF16).astype(F32)
    lo = (r1 - mid).astype(BF16).astype(F32)
    return hi, mid, lo


def _cumsum_rows(x, reverse):
    S = x.shape[0]
    bs = min(256, S)
    nb = S // bs
    r = lax.broadcasted_iota(jnp.int32, (bs, bs), 0)
    c = lax.broadcasted_iota(jnp.int32, (bs, bs), 1)
    tri = jnp.where((c >= r) if reverse else (c <= r), 1.0, 0.0).astype(BF16)
    edge = lax.broadcasted_iota(jnp.int32, (bs, x.shape[1]), 0) == (0 if reverse else bs - 1)
    carry = jnp.zeros((1, x.shape[1]), F32)
    outs = [None] * nb
    for bi in (range(nb - 1, -1, -1) if reverse else range(nb)):
        xb = x[bi * bs:(bi + 1) * bs, :]
        acc = carry
        for term in _split3(xb):
            acc = acc + jnp.dot(tri, term.astype(BF16), preferred_element_type=F32)
        outs[bi] = acc
        carry = jnp.sum(jnp.where(edge, acc, 0.0), axis=0, keepdims=True)
    return jnp.concatenate(outs, axis=0) if nb > 1 else outs[0]


def _gelu_parts(x):
    c0 = math.sqrt(2.0 / math.pi)
    inner = c0 * (x + 0.044715 * (x * x * x))
    t = jnp.tanh(inner)
    g = 0.5 * x * (1.0 + t)
    dg = 0.5 * (1.0 + t) + 0.5 * x * (1.0 - t * t) * (c0 * (1.0 + 3.0 * 0.044715 * (x * x)))
    return g, dg


def _accumulate(ref, value, first):
    @pl.when(first)
    def _():
        ref[...] = value

    @pl.when(jnp.logical_not(first))
    def _():
        ref[...] += value


def _cast_bf16(w, name):
    R, C = w.shape
    tr = _tile(R, 512, 16)

    def body(w_ref, o_ref):
        o_ref[...] = w_ref[...].astype(BF16)

    blk = pl.BlockSpec((tr, C), lambda i: (i, 0))
    return _call(body, [w], name=name, grid=(R // tr,), in_specs=[blk], out_specs=[blk],
                 out_shape=[_sds((R, C), BF16)], sem=("parallel",))[0][0]


def _concat_cols(parts, name):
    T = parts[0].shape[-2] if parts[0].ndim == 3 else parts[0].shape[0]
    widths = [p.shape[0] * LANES if p.ndim == 3 else p.shape[1] for p in parts]
    tm = _tile(T, ROW_TILE, 16)

    def body(*refs):
        o_ref = refs[-1]
        off = 0
        for p_ref, p, w in zip(refs[:-1], parts, widths):
            if p.ndim == 3:
                for hd in range(p.shape[0]):
                    o_ref[:, off + hd * LANES:off + (hd + 1) * LANES] = p_ref[hd].astype(BF16)
            else:
                o_ref[:, off:off + w] = p_ref[...].astype(BF16)
            off += w

    def spec(p, w):
        if p.ndim == 3:
            return pl.BlockSpec((p.shape[0], tm, LANES), lambda i: (0, i, 0))
        return pl.BlockSpec((tm, w), lambda i: (i, 0))

    return _call(body, parts, name=name, grid=(T // tm,),
                 in_specs=[spec(p, w) for p, w in zip(parts, widths)],
                 out_specs=[pl.BlockSpec((tm, sum(widths)), lambda i: (i, 0))],
                 out_shape=[_sds((T, sum(widths)), BF16)], sem=("parallel",))[0][0]


def _prenorm(x, g, comm=None):
    T, D = x.shape
    tm = _tile(T, ROW_TILE, 16)

    def body(x_ref, g_ref, h_ref):
        xv = x_ref[...]
        h_ref[...] = (xv * _rms(xv) * g_ref[...]).astype(BF16)

    row = pl.BlockSpec((tm, D), lambda i: (i, 0))
    (h,), landed = _call(body, [x, g], name="prenorm", grid=(T // tm,),
                         in_specs=[row, pl.BlockSpec((1, D), lambda i: (0, 0))], out_specs=[row],
                         out_shape=[_sds((T, D), BF16)], sem=("parallel",), comm=comm)
    return h, landed


def _split_prep(proj, pos, invf, gq, gkv, bfor, lay, comm=None):
    T = proj.shape[0]
    tm = _tile(T, ROW_TILE, 16)

    def body(q_ref, kv_ref, kpe_ref, fl_ref, pos_ref, invf_ref, gq_ref, gkv_ref, bf_ref,
             qn_ref, kvn_ref, kper_ref, logf_ref, cos_ref, sin_ref):
        ql = q_ref[...]
        qn_ref[...] = (ql * _rms(ql) * gq_ref[...]).astype(BF16)
        kl = kv_ref[...]
        kvn_ref[...] = (kl * _rms(kl) * gkv_ref[...]).astype(BF16)
        ang = pos_ref[...].astype(F32) * invf_ref[...]
        valid = _lane(ang.shape) < ROPE
        cs = jnp.where(valid, jnp.cos(ang), 0.0)
        sn = jnp.where(valid, jnp.sin(ang), 0.0)
        cos_ref[...] = cs
        sin_ref[...] = sn
        kp = jnp.where(valid, kpe_ref[...], 0.0)
        kper_ref[...] = (kp * cs + _rope_rot(kp) * sn).astype(BF16)
        z = fl_ref[...] + bf_ref[...]
        logf_ref[...] = jnp.minimum(z, 0.0) - jnp.log(1.0 + jnp.exp(-jnp.abs(z)))

    def col(width, off):
        return pl.BlockSpec((tm, width), lambda i: (i, off // width))

    def vec(width):
        return pl.BlockSpec((1, width), lambda i: (0, 0))

    def out(width):
        return pl.BlockSpec((tm, width), lambda i: (i, 0))

    return _call(
        body, [proj, proj, proj, proj, pos, invf, gq, gkv, bfor], name="split_prep", grid=(T // tm,),
        in_specs=[col(Q_LORA, lay["q"]), col(KV_LORA, lay["kv"]), col(LANES, lay["kpe"]), col(LANES, lay["fl"]),
                  pl.BlockSpec((tm, 1), lambda i: (i, 0)), vec(LANES), vec(Q_LORA), vec(KV_LORA), vec(LANES)],
        out_specs=[out(Q_LORA), out(KV_LORA), out(LANES), out(LANES), out(LANES), out(LANES)],
        out_shape=[_sds((T, Q_LORA), BF16), _sds((T, KV_LORA), BF16), _sds((T, LANES), BF16),
                   _sds((T, LANES), F32), _sds((T, LANES), F32), _sds((T, LANES), F32)],
        sem=("parallel",), comm=comm)


def _mla_prep(qraw, kvraw, kper, cosT, sinT, comm=None):
    H, T, _ = qraw.shape
    tm = _tile(T, HEAD_ROW_TILE, 16)

    def body(q_ref, kv_ref, kpe_ref, cos_ref, sin_ref, qo_ref, ko_ref, vo_ref):
        q = q_ref[...]
        pe = q[:, NOPE:]
        pe = jnp.where(_lane(pe.shape) < ROPE, pe, 0.0)
        qo_ref[:, :NOPE] = q[:, :NOPE].astype(BF16)
        qo_ref[:, NOPE:] = (pe * cos_ref[...] + _rope_rot(pe) * sin_ref[...]).astype(BF16)
        kv = kv_ref[...]
        ko_ref[:, :NOPE] = kv[:, :NOPE].astype(BF16)
        ko_ref[:, NOPE:] = kpe_ref[...]
        vo_ref[...] = kv[:, NOPE:].astype(BF16)

    head = pl.BlockSpec((None, tm, ATT_DK), lambda h, i: (h, i, 0))
    tok = pl.BlockSpec((tm, LANES), lambda h, i: (i, 0))
    return _call(
        body, [qraw, kvraw, kper, cosT, sinT], name="mla_prep", grid=(H, T // tm),
        in_specs=[head, head, tok, tok, tok],
        out_specs=[head, head, pl.BlockSpec((None, tm, VDIM), lambda h, i: (h, i, 0))],
        out_shape=[_sds((H, T, ATT_DK), BF16), _sds((H, T, ATT_DK), BF16), _sds((H, T, VDIM), BF16)],
        sem=("parallel", "parallel"), comm=comm)


def _fox_cumsum(logf, B, S, inv_scale):
    T = logf.shape[0]

    def body(l_ref, c_ref):
        c_ref[...] = _cumsum_rows(l_ref[...], reverse=False) * inv_scale

    seq = pl.BlockSpec((S, LANES), lambda b: (b, 0))
    return _call(body, [logf], name="fox_cumsum", grid=(B,), in_specs=[seq], out_specs=[seq],
                 out_shape=[_sds((T, LANES), F32)], sem=("parallel",))[0][0]


def _fox_prep(proj, cs, lay, comm=None):
    T = proj.shape[0]
    tm = _tile(T, HEAD_ROW_TILE, 16)

    def body(q_ref, k_ref, v_ref, cs_ref, qo_ref, ko_ref, vo_ref):
        h = pl.program_id(0)
        cv = cs_ref[...]
        lane = _lane(cv.shape)
        ccol = jnp.sum(jnp.where(lane == h, cv, 0.0), axis=1, keepdims=True)
        hi, mid, lo = _split3(ccol)
        one = jnp.where(lane < 6, 1.0, 0.0)
        augq = jnp.where(lane == 0, hi, jnp.where(lane == 1, mid, jnp.where(lane == 2, lo, one)))
        augk = jnp.where(lane < 3, 1.0, jnp.where(lane == 3, -hi, jnp.where(lane == 4, -mid, jnp.where(lane == 5, -lo, 0.0))))
        qo_ref[:, :FOX_DIM] = q_ref[...].astype(BF16)
        qo_ref[:, FOX_DIM:] = augq.astype(BF16)
        ko_ref[:, :FOX_DIM] = k_ref[...].astype(BF16)
        ko_ref[:, FOX_DIM:] = augk.astype(BF16)
        vo_ref[...] = v_ref[...].astype(BF16)

    def col(off):
        return pl.BlockSpec((tm, FOX_DIM), lambda h, i: (i, off // FOX_DIM + h))

    head = pl.BlockSpec((None, tm, ATT_DK), lambda h, i: (h, i, 0))
    return _call(
        body, [proj, proj, proj, cs], name="fox_prep", grid=(HEADS, T // tm),
        in_specs=[col(lay["fq"]), col(lay["fk"]), col(lay["fv"]), pl.BlockSpec((tm, LANES), lambda h, i: (i, 0))],
        out_specs=[head, head, pl.BlockSpec((None, tm, VDIM), lambda h, i: (h, i, 0))],
        out_shape=[_sds((HEADS, T, ATT_DK), BF16), _sds((HEADS, T, ATT_DK), BF16), _sds((HEADS, T, VDIM), BF16)],
        sem=("parallel", "parallel"), comm=comm)


def _visible(tq, tk, unit):
    r = lax.broadcasted_iota(jnp.int32, (tq, tk), 0)
    c = lax.broadcasted_iota(jnp.int32, (tq, tk), 1)
    sh = int(math.log2(unit))
    return lax.shift_right_logical(c, sh) <= lax.shift_right_logical(r, sh)


def _attn_fwd(streams, *, B, S, name, comm=None):
    n = len(streams)
    H, T, DK = streams[0][0].shape
    DV = streams[0][2].shape[2]
    tq = _tile(S, ATT_TILE)
    nq = S // tq
    sub = min(ATT_SUB, tq)
    NT = (((1,), (1,)), ((), ()))

    def body(*refs):
        ins, outs, (m_sc, acc_sc) = refs[:3 * n], refs[3 * n:5 * n], refs[5 * n:]
        i, j = pl.program_id(1), pl.program_id(2)

        @pl.when(j == 0)
        def _():
            m_sc[...] = jnp.full(m_sc.shape, NEG_INF, F32)
            acc_sc[...] = jnp.zeros(acc_sc.shape, F32)

        def step(diagonal):
            work = [(t, r) for r in range(tq // sub) for t in range(n)]

            def scores(t, r):
                q_ref, k_ref, _ = ins[3 * t:3 * t + 3]
                kc = (r + 1) * sub if diagonal else tq
                s = lax.dot_general(q_ref[r * sub:(r + 1) * sub, :], k_ref[0:kc, :], NT, preferred_element_type=F32)
                return s * (streams[t][4] * LOG2_E)

            ahead = [scores(*work[w]) for w in range(min(ATT_AHEAD, len(work)))]
            for w, (t, r) in enumerate(work):
                s = ahead.pop(0)
                if w + ATT_AHEAD < len(work):
                    ahead.append(scores(*work[w + ATT_AHEAD]))
                v_ref = ins[3 * t + 2]
                kc = s.shape[1]
                rows = slice(r * sub, (r + 1) * sub)
                if diagonal:
                    own = jnp.where(_visible(sub, sub, streams[t][3]), s[:, kc - sub:], NEG_INF)
                    s = own if kc == sub else jnp.concatenate([s[:, :kc - sub], own], axis=1)
                m_prev = m_sc[t, rows, :]
                mx = s[:, 0:LANES]
                for g in range(1, kc // LANES):
                    mx = jnp.maximum(mx, s[:, g * LANES:(g + 1) * LANES])
                m_new = jnp.maximum(m_prev, jnp.max(mx, axis=1, keepdims=True))
                alpha = jnp.exp2(m_prev - m_new)
                p = jnp.exp2(s - jnp.tile(m_new, (1, kc // LANES))).astype(BF16)
                v_aug = jnp.concatenate([v_ref[0:kc, :], jnp.ones((kc, LANES), BF16)], axis=1)
                acc_sc[t, rows, :] = jnp.tile(alpha, (1, 2)) * acc_sc[t, rows, :] + jnp.dot(
                    p, v_aug, preferred_element_type=F32)
                m_sc[t, rows, :] = m_new

        @pl.when(j < i)
        def _():
            step(False)

        @pl.when(j == i)
        def _():
            step(True)
            for t in range(n):
                o_ref, lse_ref = outs[2 * t:2 * t + 2]
                l = acc_sc[t, :, DV:]
                o_ref[...] = (acc_sc[t, :, :DV] / l).astype(BF16)
                lse_ref[...] = m_sc[t] + jnp.log2(l)

    def qmap(g, i, j):
        return (g % H, (g // H) * nq + i, 0)

    def kmap(g, i, j):
        return (g % H, (g // H) * nq + jnp.minimum(j, i), 0)

    args = [a for st in streams for a in st[:3]]
    outs, landed = _call(
        body, args, name=name, grid=(B * H, nq, nq),
        in_specs=[pl.BlockSpec((None, tq, DK), qmap), pl.BlockSpec((None, tq, DK), kmap),
                  pl.BlockSpec((None, tq, DV), kmap)] * n,
        out_specs=[pl.BlockSpec((tq, DV), lambda g, i, j: ((g // H) * nq + i, g % H)),
                   pl.BlockSpec((None, tq, LANES), qmap)] * n,
        out_shape=[_sds((T, H * DV), BF16), _sds((H, T, LANES), F32)] * n,
        scratch_shapes=[pltpu.VMEM((n, tq, LANES), F32), pltpu.VMEM((n, tq, DV + LANES), F32)],
        sem=("parallel", "parallel", "arbitrary"), comm=comm)
    return [(outs[2 * t], outs[2 * t + 1]) for t in range(n)], landed


def _attn_bwd(streams, *, B, S, name, comm=None):
    n = len(streams)
    H, T, DK = streams[0][0].shape
    DV = streams[0][2].shape[2]
    tq = _tile(S, ATT_TILE)
    nq = S // tq
    sub = min(ATT_SUB, tq)
    NT = (((1,), (1,)), ((), ()))
    TN = (((0,), (0,)), ((), ()))

    def body(*refs):
        ins, outs = refs[:6 * n], refs[6 * n:]
        j, i = pl.program_id(1), pl.program_id(2)

        @pl.when(jnp.logical_and(j == 0, i == 0))
        def _():
            for t in range(n):
                outs[3 * t][...] = jnp.zeros(outs[3 * t].shape, F32)

        @pl.when(i == 0)
        def _():
            for t in range(n):
                outs[3 * t + 1][...] = jnp.zeros(outs[3 * t + 1].shape, F32)
                outs[3 * t + 2][...] = jnp.zeros(outs[3 * t + 2].shape, F32)

        def step(diagonal):
            work = [(t, r) for r in range(tq // sub) for t in range(n)]

            def kcols(r):
                return (r + 1) * sub if diagonal else tq

            def scores(t, r):
                q_ref, k_ref, v_ref, _, do_ref, _ = ins[6 * t:6 * t + 6]
                rows, kc = slice(r * sub, (r + 1) * sub), kcols(r)
                s = lax.dot_general(q_ref[rows, :], k_ref[0:kc, :], NT, preferred_element_type=F32)
                dp = lax.dot_general(do_ref[rows, :], v_ref[0:kc, :], NT, preferred_element_type=F32)
                return s * (streams[t][7] * LOG2_E), dp

            def probs(t, r, s, dp):
                _, _, _, o_ref, do_ref, lse_ref = ins[6 * t:6 * t + 6]
                rows, kc = slice(r * sub, (r + 1) * sub), kcols(r)
                if diagonal:
                    own = jnp.where(_visible(sub, sub, streams[t][6]), s[:, kc - sub:], NEG_INF)
                    s = own if kc == sub else jnp.concatenate([s[:, :kc - sub], own], axis=1)
                p = jnp.exp2(s - jnp.tile(lse_ref[rows, :], (1, kc // LANES)))
                delta = jnp.sum(do_ref[rows, :].astype(F32) * o_ref[rows, :].astype(F32), axis=1, keepdims=True)
                return p.astype(BF16), (p * (dp - delta) * streams[t][7]).astype(BF16)

            def grads(t, r, p, ds):
                q_ref, k_ref, _, _, do_ref, _ = ins[6 * t:6 * t + 6]
                dq_ref, dk_ref, dv_ref = outs[3 * t:3 * t + 3]
                rows, kc = slice(r * sub, (r + 1) * sub), kcols(r)
                dv_ref[0:kc, :] += lax.dot_general(p, do_ref[rows, :], TN, preferred_element_type=F32)
                dk_ref[0:kc, :] += lax.dot_general(ds, q_ref[rows, :], TN, preferred_element_type=F32)
                qrows = pl.ds(pl.multiple_of(i * tq + r * sub, sub), sub)
                dq_ref[qrows, :] += jnp.dot(ds, k_ref[0:kc, :], preferred_element_type=F32)

            nw = len(work)
            sc = {w: scores(*work[w]) for w in range(min(2, nw))}
            pr = {0: probs(*work[0], *sc.pop(0))}
            for w in range(nw):
                if w + 2 < nw:
                    sc[w + 2] = scores(*work[w + 2])
                if w + 1 < nw:
                    pr[w + 1] = probs(*work[w + 1], *sc.pop(w + 1))
                grads(*work[w], *pr.pop(w))

        @pl.when(i > j)
        def _():
            step(False)

        @pl.when(i == j)
        def _():
            step(True)

    def qmap(g, j, i):
        return (g % H, (g // H) * nq + jnp.maximum(i, j), 0)

    def kmap(g, j, i):
        return (g % H, (g // H) * nq + j, 0)

    def omap(g, j, i):
        return ((g // H) * nq + jnp.maximum(i, j), g % H)

    args = [a for st in streams for a in st[:6]]
    outs, landed = _call(
        body, args, name=name, grid=(B * H, nq, nq),
        in_specs=[pl.BlockSpec((None, tq, DK), qmap), pl.BlockSpec((None, tq, DK), kmap),
                  pl.BlockSpec((None, tq, DV), kmap), pl.BlockSpec((tq, DV), omap), pl.BlockSpec((tq, DV), omap),
                  pl.BlockSpec((None, tq, LANES), qmap)] * n,
        out_specs=[pl.BlockSpec((None, S, DK), lambda g, j, i: (g % H, g // H, 0)),
                   pl.BlockSpec((None, tq, DK), kmap), pl.BlockSpec((None, tq, DV), kmap)] * n,
        out_shape=[_sds((H, T, DK), F32), _sds((H, T, DK), F32), _sds((H, T, DV), F32)] * n,
        sem=("parallel", "arbitrary", "arbitrary"), comm=comm)
    return [tuple(outs[3 * t:3 * t + 3]) for t in range(n)], landed


def _gate_merge(am, af, proj, bgate, lay, D, comm=None):
    T = am.shape[0]
    tm = _tile(T, ROW_TILE, 16)
    tn = _tile(D, 1024)

    def body(am_ref, af_ref, gm_ref, gf_ref, bm_ref, bf_ref, o_ref):
        sm = _sigmoid(gm_ref[...] + bm_ref[...])
        sf = _sigmoid(gf_ref[...] + bf_ref[...])
        o_ref[...] = (sm * am_ref[...] + sf * af_ref[...]).astype(BF16)

    og = lay["g"] // tn
    blk = pl.BlockSpec((tm, tn), lambda i, j: (i, j))
    return _call(
        body, [am, af, proj, proj, bgate, bgate], name="gate_merge", grid=(T // tm, D // tn),
        in_specs=[blk, blk, pl.BlockSpec((tm, tn), lambda i, j: (i, og + j)),
                  pl.BlockSpec((tm, tn), lambda i, j: (i, og + D // tn + j)),
                  pl.BlockSpec((1, tn), lambda i, j: (0, j)), pl.BlockSpec((1, tn), lambda i, j: (0, D // tn + j))],
        out_specs=[blk], out_shape=[_sds((T, D), BF16)], sem=("parallel", "parallel"), comm=comm)


def _mid(x, y1, g_pm, g_ffn):
    T, D = x.shape
    tm = _tile(T, ROW_TILE, 16)

    def body(x_ref, y_ref, gp_ref, gf_ref, x1_ref, h2_ref):
        y = y_ref[...]
        x1 = x_ref[...] + y * _rms(y) * gp_ref[...]
        x1_ref[...] = x1
        h2_ref[...] = (x1 * _rms(x1) * gf_ref[...]).astype(BF16)

    row = pl.BlockSpec((tm, D), lambda i: (i, 0))
    vec = pl.BlockSpec((1, D), lambda i: (0, 0))
    return _call(body, [x, y1, g_pm, g_ffn], name="mid", grid=(T // tm,), in_specs=[row, row, vec, vec],
                 out_specs=[row, row], out_shape=[_sds((T, D), F32), _sds((T, D), BF16)], sem=("parallel",))[0]


def _conv3(u, w_ref, bias):
    row = lax.broadcasted_iota(jnp.int32, u.shape, 0)
    u1 = jnp.where(row >= 1, pltpu.roll(u, 1, 0), 0.0)
    u2 = jnp.where(row >= 2, pltpu.roll(u, 2, 0), 0.0)
    return w_ref[0:1, :] * u2 + w_ref[1:2, :] * u1 + w_ref[2:3, :] * u + bias, u1, u2


def _convffn_fwd(u, cw, cb, B, S, F, comm=None):
    T = u.shape[0]
    tn = _tile(F, 256)
    nf = F // tn

    def body(ug_ref, uv_ref, wg_ref, wv_ref, bg_ref, bv_ref, a_ref):
        g, _, _ = _conv3(ug_ref[...], wg_ref, bg_ref[...])
        val, _, _ = _conv3(uv_ref[...], wv_ref, bv_ref[...])
        a_ref[...] = (_gelu_parts(g)[0] * val).astype(BF16)

    def seq(off):
        return pl.BlockSpec((S, tn), lambda b, j: (b, off + j))

    def par(rows, off):
        return pl.BlockSpec((rows, tn), lambda b, j: (0, off + j))

    return _call(body, [u, u, cw, cw, cb, cb], name="convffn_fwd", grid=(B, nf),
                 in_specs=[seq(0), seq(nf), par(3, 0), par(3, nf), par(1, 0), par(1, nf)],
                 out_specs=[seq(0)], out_shape=[_sds((T, F), BF16)], sem=("parallel", "parallel"), comm=comm)


def _convffn_bwd(u, dact, cw, cb, B, S, F, comm=None):
    T = u.shape[0]
    tn = _tile(F, 256)
    nf = F // tn

    def body(ug_ref, uv_ref, da_ref, wg_ref, wv_ref, bg_ref, bv_ref, dug_ref, duv_ref, dpg_ref, dpv_ref):
        b = pl.program_id(1)
        ug, uv, da = ug_ref[...], uv_ref[...], da_ref[...]
        g, ug1, ug2 = _conv3(ug, wg_ref, bg_ref[...])
        val, uv1, uv2 = _conv3(uv, wv_ref, bv_ref[...])
        gel, dgel = _gelu_parts(g)
        dg = da * val * dgel
        dval = da * gel
        row = lax.broadcasted_iota(jnp.int32, ug.shape, 0)

        def back(d, w_ref):
            d1 = jnp.where(row < S - 1, pltpu.roll(d, S - 1, 0), 0.0)
            d2 = jnp.where(row < S - 2, pltpu.roll(d, S - 2, 0), 0.0)
            return w_ref[2:3, :] * d + w_ref[1:2, :] * d1 + w_ref[0:1, :] * d2

        dug_ref[...] = back(dg, wg_ref).astype(BF16)
        duv_ref[...] = back(dval, wv_ref).astype(BF16)

        def sums(d, u0, u1, u2):
            r8 = lax.broadcasted_iota(jnp.int32, (8, d.shape[1]), 0)
            out = jnp.zeros((8, d.shape[1]), F32)
            for k, t in enumerate((d * u2, d * u1, d * u0, d)):
                out = jnp.where(r8 == k, jnp.sum(t, axis=0, keepdims=True), out)
            return out

        _accumulate(dpg_ref, sums(dg, ug, ug1, ug2), b == 0)
        _accumulate(dpv_ref, sums(dval, uv, uv1, uv2), b == 0)

    def seq(off):
        return pl.BlockSpec((S, tn), lambda j, b: (b, off + j))

    def par(rows, off):
        return pl.BlockSpec((rows, tn), lambda j, b: (0, off + j))

    outs, landed = _call(
        body, [u, u, dact, cw, cw, cb, cb], name="convffn_bwd", grid=(nf, B),
        in_specs=[seq(0), seq(nf), seq(0), par(3, 0), par(3, nf), par(1, 0), par(1, nf)],
        out_specs=[seq(0), seq(0), par(8, 0), par(8, 0)],
        out_shape=[_sds((T, F), BF16), _sds((T, F), BF16), _sds((8, F), F32), _sds((8, F), F32)],
        sem=("parallel", "arbitrary"), comm=comm)
    return outs, landed


def _tail(ff, x1, tgt, g):
    T, D = ff.shape
    tm = _tile(T, ROW_TILE, 16)

    def body(ff_ref, x1_ref, t_ref, g_ref, dy_ref, dff_ref, loss_ref, dg_ref):
        i = pl.program_id(0)
        f = ff_ref[...]
        gv = g_ref[...]
        r = _rms(f)
        n = f * r
        e = (x1_ref[...] + n * gv) - t_ref[...]
        dy = e * (1.0 / D)
        dy_ref[...] = dy
        dn = dy * gv
        dff_ref[...] = (r * (dn - n * jnp.mean(dn * n, axis=-1, keepdims=True))).astype(BF16)
        part = 0.5 * jnp.sum(jnp.mean(e * e, axis=-1, keepdims=True), axis=0, keepdims=True)
        _accumulate(loss_ref, jnp.broadcast_to(part, loss_ref.shape), i == 0)
        _accumulate(dg_ref, jnp.sum(dy * n, axis=0, keepdims=True), i == 0)

    row = pl.BlockSpec((tm, D), lambda i: (i, 0))
    vec = pl.BlockSpec((1, D), lambda i: (0, 0))
    return _call(body, [ff, x1, tgt, g], name="tail", grid=(T // tm,), in_specs=[row, row, row, vec],
                 out_specs=[row, row, pl.BlockSpec((8, LANES), lambda i: (0, 0)), vec],
                 out_shape=[_sds((T, D), F32), _sds((T, D), BF16), _sds((8, LANES), F32), _sds((1, D), F32)],
                 sem=("arbitrary",))[0]


def _mid_bwd(dy, dh2, x1, y1, g_ffn, g_pm, comm=None):
    T, D = dy.shape
    tm = _tile(T, ROW_TILE, 16)

    def body(dy_ref, dh_ref, x1_ref, y1_ref, gf_ref, gp_ref, dx1_ref, dy1_ref, dgf_ref, dgp_ref):
        i = pl.program_id(0)
        dh = dh_ref[...]
        d2, dgf = _rms_bwd(dh, x1_ref[...], gf_ref[...])
        dx1 = dy_ref[...] + d2
        dx1_ref[...] = dx1
        d1, dgp = _rms_bwd(dx1, y1_ref[...], gp_ref[...])
        dy1_ref[...] = d1.astype(BF16)
        _accumulate(dgf_ref, jnp.sum(dgf, axis=0, keepdims=True), i == 0)
        _accumulate(dgp_ref, jnp.sum(dgp, axis=0, keepdims=True), i == 0)

    row = pl.BlockSpec((tm, D), lambda i: (i, 0))
    vec = pl.BlockSpec((1, D), lambda i: (0, 0))
    return _call(body, [dy, dh2, x1, y1, g_ffn, g_pm], name="mid_bwd", grid=(T // tm,),
                 in_specs=[row, row, row, row, vec, vec], out_specs=[row, row, vec, vec],
                 out_shape=[_sds((T, D), F32), _sds((T, D), BF16), _sds((1, D), F32), _sds((1, D), F32)],
                 sem=("arbitrary",), comm=comm)


def _gate_bwd(dm, am, af, proj, bgate, lay, D, comm=None):
    T = dm.shape[0]
    tm = _tile(T, ROW_TILE, 16)
    tn = _tile(D, 512)

    def body(dm_ref, am_ref, af_ref, gm_ref, gf_ref, bm_ref, bf_ref,
             dam_ref, daf_ref, dgm_ref, dgf_ref, dbm_ref, dbf_ref):
        i = pl.program_id(1)
        d = dm_ref[...]
        sm = _sigmoid(gm_ref[...] + bm_ref[...])
        sf = _sigmoid(gf_ref[...] + bf_ref[...])
        dam_ref[...] = (d * sm).astype(BF16)
        daf_ref[...] = (d * sf).astype(BF16)
        dgm = d * am_ref[...] * (sm * (1.0 - sm))
        dgf = d * af_ref[...] * (sf * (1.0 - sf))
        dgm_ref[...] = dgm.astype(BF16)
        dgf_ref[...] = dgf.astype(BF16)
        _accumulate(dbm_ref, jnp.sum(dgm, axis=0, keepdims=True), i == 0)
        _accumulate(dbf_ref, jnp.sum(dgf, axis=0, keepdims=True), i == 0)

    og = lay["g"] // tn
    blk = pl.BlockSpec((tm, tn), lambda j, i: (i, j))
    vec = pl.BlockSpec((1, tn), lambda j, i: (0, j))
    return _call(
        body, [dm, am, af, proj, proj, bgate, bgate], name="gate_bwd", grid=(D // tn, T // tm),
        in_specs=[blk, blk, blk, pl.BlockSpec((tm, tn), lambda j, i: (i, og + j)),
                  pl.BlockSpec((tm, tn), lambda j, i: (i, og + D // tn + j)),
                  vec, pl.BlockSpec((1, tn), lambda j, i: (0, D // tn + j))],
        out_specs=[blk, blk, blk, blk, vec, vec],
        out_shape=[_sds((T, D), BF16)] * 4 + [_sds((1, D), F32)] * 2, sem=("parallel", "arbitrary"), comm=comm)


def _mla_bwd_prep(dq, dk, dv, cosT, sinT, comm=None):
    H, T, _ = dq.shape
    tm = _tile(T, HEAD_ROW_TILE, 16)

    def body(dq_ref, dk_ref, dv_ref, cos_ref, sin_ref, dqr_ref, dkv_ref, dkpe_ref):
        h = pl.program_id(1)
        cs, sn = cos_ref[...], sin_ref[...]
        valid = _lane(cs.shape) < ROPE

        def unrope(d):
            d = jnp.where(valid, d, 0.0)
            return d * cs - _rope_rot(d) * sn

        dqv = dq_ref[...]
        dqr_ref[:, :NOPE] = dqv[:, :NOPE].astype(BF16)
        dqr_ref[:, NOPE:] = unrope(dqv[:, NOPE:]).astype(BF16)
        dkv_ = dk_ref[...]
        dkv_ref[:, :NOPE] = dkv_[:, :NOPE].astype(BF16)
        dkv_ref[:, NOPE:] = dv_ref[...].astype(BF16)
        _accumulate(dkpe_ref, unrope(dkv_[:, NOPE:]), h == 0)

    head = pl.BlockSpec((None, tm, ATT_DK), lambda i, h: (h, i, 0))
    tok = pl.BlockSpec((tm, LANES), lambda i, h: (i, 0))
    return _call(
        body, [dq, dk, dv, cosT, sinT], name="mla_bwd_prep", grid=(T // tm, H),
        in_specs=[head, head, pl.BlockSpec((None, tm, VDIM), lambda i, h: (h, i, 0)), tok, tok],
        out_specs=[head, head, tok],
        out_shape=[_sds((H, T, ATT_DK), BF16), _sds((H, T, ATT_DK), BF16), _sds((T, LANES), F32)],
        sem=("parallel", "arbitrary"), comm=comm)


def _fox_bwd_prep(dq, dk, proj, bfor, lay, B, S, inv_scale):
    H, T, _ = dq.shape

    def body(dq_ref, dk_ref, fl_ref, bf_ref, dfl_ref, dbf_ref, dc_sc):
        b, h = pl.program_id(0), pl.program_id(1)
        lane = _lane(dc_sc.shape)
        col = jnp.sum(jnp.where(lane == 0, dq_ref[...], 0.0) - jnp.where(lane == 3, dk_ref[...], 0.0),
                      axis=1, keepdims=True)

        @pl.when(h == 0)
        def _():
            dc_sc[...] = jnp.zeros(dc_sc.shape, F32)

        dc_sc[...] = jnp.where(lane == h, col, dc_sc[...])

        @pl.when(h == H - 1)
        def _():
            dlogf = _cumsum_rows(dc_sc[...] * inv_scale, reverse=True)
            z = fl_ref[...] + bf_ref[...]
            dz = jnp.where(lane < H, dlogf * (1.0 / (1.0 + jnp.exp(z))), 0.0)
            dfl_ref[...] = dz
            _accumulate(dbf_ref, jnp.sum(dz, axis=0, keepdims=True), b == 0)

    aug = pl.BlockSpec((None, S, LANES), lambda b, h: (h, b, 1))
    seq = pl.BlockSpec((S, LANES), lambda b, h: (b, 0))
    vec = pl.BlockSpec((1, LANES), lambda b, h: (0, 0))
    return _call(
        body, [dq, dk, proj, bfor], name="fox_bwd_prep", grid=(B, H),
        in_specs=[aug, aug, pl.BlockSpec((S, LANES), lambda b, h: (b, lay["fl"] // LANES)), vec],
        out_specs=[seq, vec], out_shape=[_sds((T, LANES), F32), _sds((1, LANES), F32)],
        scratch_shapes=[pltpu.VMEM((S, LANES), F32)], sem=("arbitrary", "arbitrary"))[0]


def _lat_bwd(dqn, dkvn, proj, gq, gkv, lay):
    T = dqn.shape[0]
    tm = _tile(T, ROW_TILE, 16)

    def body(dq_ref, dkv_ref, q_ref, kv_ref, gq_ref, gkv_ref, dql_ref, dkl_ref, dgq_ref, dgkv_ref):
        i = pl.program_id(0)
        dql, dgq = _rms_bwd(dq_ref[...], q_ref[...], gq_ref[...])
        dkl, dgkv = _rms_bwd(dkv_ref[...], kv_ref[...], gkv_ref[...])
        dql_ref[...] = dql.astype(BF16)
        dkl_ref[...] = dkl.astype(BF16)
        _accumulate(dgq_ref, jnp.sum(dgq, axis=0, keepdims=True), i == 0)
        _accumulate(dgkv_ref, jnp.sum(dgkv, axis=0, keepdims=True), i == 0)

    def blk(width, off=0):
        return pl.BlockSpec((tm, width), lambda i: (i, off // width))

    def vec(width):
        return pl.BlockSpec((1, width), lambda i: (0, 0))

    return _call(
        body, [dqn, dkvn, proj, proj, gq, gkv], name="lat_bwd", grid=(T // tm,),
        in_specs=[blk(Q_LORA), blk(KV_LORA), blk(Q_LORA, lay["q"]), blk(KV_LORA, lay["kv"]), vec(Q_LORA), vec(KV_LORA)],
        out_specs=[blk(Q_LORA), blk(KV_LORA), vec(Q_LORA), vec(KV_LORA)],
        out_shape=[_sds((T, Q_LORA), BF16), _sds((T, KV_LORA), BF16), _sds((1, Q_LORA), F32), _sds((1, KV_LORA), F32)],
        sem=("arbitrary",))[0]


def _final_dx(dx1, dh, x, g, comm=None):
    T, D = x.shape
    tm = _tile(T, ROW_TILE, 16)

    def body(dx1_ref, dh_ref, x_ref, g_ref, dx_ref, dg_ref):
        i = pl.program_id(0)
        d, dg = _rms_bwd(dh_ref[...], x_ref[...], g_ref[...])
        dx_ref[...] = dx1_ref[...] + d
        _accumulate(dg_ref, jnp.sum(dg, axis=0, keepdims=True), i == 0)

    row = pl.BlockSpec((tm, D), lambda i: (i, 0))
    vec = pl.BlockSpec((1, D), lambda i: (0, 0))
    return _call(body, [dx1, dh, x, g], name="final_dx", grid=(T // tm,), in_specs=[row, row, row, vec],
                 out_specs=[row, vec], out_shape=[_sds((T, D), F32), _sds((1, D), F32)], sem=("arbitrary",), comm=comm)


def _chip_sum(pieces, paired, qc, name):
    G, R, C = pieces.shape
    tr = _tile(R, 256, 16)

    def body(qc_ref, g_ref, p_ref, keep_ref, send_ref):
        s = pl.program_id(1)
        tot = g_ref[...] + p_ref[...]

        @pl.when(s == 0)
        def _():
            keep_ref[...] = tot

        @pl.when(s > 0)
        def _():
            send_ref[...] = tot.astype(send_ref.dtype)

    grid_spec = pltpu.PrefetchScalarGridSpec(
        num_scalar_prefetch=1, grid=(R // tr, N_CHIP),
        in_specs=[pl.BlockSpec((None, tr, C), lambda i, s, qc: (2 * (qc[0] ^ s) + qc[1], i, 0)),
                  pl.BlockSpec((None, tr, C), lambda i, s, qc: (qc[0] ^ s, i, 0))],
        out_specs=[pl.BlockSpec((tr, C), lambda i, s, qc: (i, 0)),
                   pl.BlockSpec((None, tr, C), lambda i, s, qc: (jnp.maximum(s - 1, 0), i, 0))])
    send_dtype = BF16 if R >= 16 else pieces.dtype
    return pl.pallas_call(
        body, name=name, grid_spec=grid_spec,
        out_shape=[_sds((R, C), F32), _sds((3, R, C), send_dtype)],
        compiler_params=pltpu.CompilerParams(dimension_semantics=("arbitrary", "arbitrary"),
                                             vmem_limit_bytes=VMEM_LIMIT_BYTES),
    )(qc, pieces, paired)


def _adamw_math(w, g, m, v):
    m = ADAM_B1 * m + (1.0 - ADAM_B1) * g
    v = ADAM_B2 * v + (1.0 - ADAM_B2) * (g * g)
    m_hat = m / (1.0 - ADAM_B1 ** ADAM_STEP)
    v_hat = v / (1.0 - ADAM_B2 ** ADAM_STEP)
    delta = -ADAM_LR * (m_hat / (jnp.sqrt(v_hat) + ADAM_EPS) + ADAM_WD * w)
    return delta, m, v


def _sum_adamw(keep, pieces, w, m, v, name):
    R, C = w.shape
    P = pieces.shape[0]
    tr = _tile(R, 256, 16)

    def body(*refs):
        if keep is None:
            p_ref, w_ref, m_ref, v_ref, g_ref, d_ref, mo_ref, vo_ref = refs
            g = p_ref[0].astype(F32)
            rest = range(1, P)
        else:
            k_ref, p_ref, w_ref, m_ref, v_ref, g_ref, d_ref, mo_ref, vo_ref = refs
            g = k_ref[...]
            rest = range(P)
        for q in rest:
            g = g + p_ref[q].astype(F32)
        g_ref[...] = g
        d_ref[...], mo_ref[...], vo_ref[...] = _adamw_math(w_ref[...], g, m_ref[...], v_ref[...])

    blk = pl.BlockSpec((tr, C), lambda i: (i, 0))
    pblk = pl.BlockSpec((P, tr, C), lambda i: (0, i, 0))
    args = [pieces, w, m, v] if keep is None else [keep, pieces, w, m, v]
    specs = [pblk, blk, blk, blk] if keep is None else [blk, pblk, blk, blk, blk]
    return _call(body, args, name=name, grid=(R // tr,), in_specs=specs, out_specs=[blk] * 4,
                 out_shape=[_sds((R, C), F32)] * 4, sem=("parallel",))[0]


def _layout(D):
    lay = {"q": 0, "kv": Q_LORA, "kpe": Q_LORA + KV_LORA}
    lay["fq"] = lay["kpe"] + LANES
    lay["fk"] = lay["fq"] + HEADS * FOX_DIM
    lay["fv"] = lay["fk"] + HEADS * FOX_DIM
    lay["fl"] = lay["fv"] + HEADS * FOX_DIM
    lay["g"] = lay["fl"] + LANES
    lay["end"] = lay["g"] + 2 * D
    return lay


def kernel(x, positions, pre_mix_norm, w_in, q_a_norm, w_uq, kv_a_norm, w_ukv, b_forget, b_gate, w_branch_mla, w_branch_fox, w_out, post_mix_norm, pre_ffn_norm, w_up, conv_w, conv_b, w_down, post_ffn_norm, loss_target, m_pre_mix_norm, m_w_in, m_q_a_norm, m_w_uq, m_kv_a_norm, m_w_ukv, m_b_forget, m_b_gate, m_w_branch_mla, m_w_branch_fox, m_w_out, m_post_mix_norm, m_pre_ffn_norm, m_w_up, m_conv_w, m_conv_b, m_w_down, m_post_ffn_norm, v_pre_mix_norm, v_w_in, v_q_a_norm, v_w_uq, v_kv_a_norm, v_w_ukv, v_b_forget, v_b_gate, v_w_branch_mla, v_w_branch_fox, v_w_out, v_post_mix_norm, v_pre_ffn_norm, v_w_up, v_conv_w, v_conv_b, v_w_down, v_post_ffn_norm):
    B, S, D = x.shape
    T = B * S
    F = conv_b.shape[0] // 2
    lay = _layout(D)
    n_in = w_in.shape[1]
    d_in = N_DEV * n_in
    seg_a = Q_LORA + KV_LORA + ROPE
    seg_b = 3 * HEADS * FOX_DIM + HEADS
    mla_scale = (NOPE + ROPE) ** -0.5
    fox_scale = FOX_DIM ** -0.5
    ax, ay, ac = (lax.axis_index(a) for a in MESH_AXES)
    qc = jnp.stack([2 * ax + ay, ac]).astype(jnp.int32)

    def row(vec, width=None):
        vec = vec.reshape(1, -1)
        if width is not None and vec.shape[1] < width:
            vec = jnp.pad(vec, ((0, 0), (0, width - vec.shape[1])))
        return vec

    x2 = x.reshape(T, D)
    win_s = _cast_bf16(w_in, "cast_w_in")
    h, (win_g,) = _prenorm(x2, row(pre_mix_norm), comm=_Comm([_GatherRelayPlan([win_s], mid_frac=0.3)]))
    small_s = [_cast_bf16(w, "cast_" + n) for w, n in
               [(w_uq, "w_uq"), (w_ukv, "w_ukv"), (w_branch_mla, "w_branch_mla"), (w_branch_fox, "w_branch_fox"), (w_out, "w_out")]]
    wup_s = _cast_bf16(w_up, "cast_w_up")
    wdown_s = _cast_bf16(w_down, "cast_w_down")

    def shard_cols(lo, hi):
        out = []
        for g in range(lo // n_in, (hi - 1) // n_in + 1):
            out.append(win_g[g][:, max(lo, g * n_in) - g * n_in:min(hi, (g + 1) * n_in) - g * n_in])
        return out

    w_perm = jnp.concatenate(
        shard_cols(0, seg_a) + [jnp.zeros((D, LANES - ROPE), BF16)] + shard_cols(seg_a, seg_a + seg_b)
        + [jnp.zeros((D, LANES - HEADS), BF16)] + shard_cols(seg_a + seg_b, d_in), axis=1)

    tgt = loss_target.reshape(T, D)
    pos = positions.reshape(T, 1)
    inv_freq = 1.0 / (ROPE_THETA ** (jnp.arange(0, ROPE, 2, dtype=F32) / ROPE))
    invf = row(jnp.concatenate([inv_freq, inv_freq]), LANES)
    g_pre, g_q, g_kv = row(pre_mix_norm), row(q_a_norm), row(kv_a_norm)
    g_pm, g_ffn, g_pf = row(post_mix_norm), row(pre_ffn_norm), row(post_ffn_norm)
    bfor = row(b_forget, LANES)
    bgate = row(b_gate)
    cb_full = row(conv_b)

    def own_plan(blocks):
        return _Comm([_GatherOwnPlan(blocks)])

    def pass_plan(gathered):
        return _Comm([_GatherPassPlan(gathered)])

    def pair_plan(gs):
        return _Comm([_PairScatterPlan(gs)])

    def chip_plan(gs):
        return _Comm([_ChipScatterPlan(gs)])

    half_d = D // 2
    proj, landed = _matmul(h, w_perm, mode="nn", name="mm_proj", comm=_Comm(
        [_GatherOwnPlan(small_s[:2] + [conv_w]), _GatherOwnPlan([wup_s], rows=(0, half_d))]))
    early_g, wup_part = landed[:-1], landed[-1:]
    (qn, kvn, kper, logf, cosT, sinT), (wuq_g, wukv_g, cw_g) = _split_prep(
        proj, pos, invf, g_q, g_kv, bfor, lay, comm=pass_plan(early_g))
    wuq_pad = jnp.pad(wuq_g, ((0, 0), (0, 0), (0, ATT_DK - NOPE - ROPE)))
    cw_full = jnp.transpose(cw_g, (1, 0, 2)).reshape(3, 2 * F)

    qraw = _matmul(qn, wuq_pad, mode="nn", name="mm_q", out_blocks=ATT_DK, tm=T)
    kvraw = _matmul(kvn, wukv_g, mode="nn", name="mm_kv", out_blocks=NOPE + VDIM, tm=T)
    (q_mla, k_mla, v_mla), branch_half = _mla_prep(qraw, kvraw, kper, cosT, sinT, comm=own_plan(small_s[2:4]))
    cs = _fox_cumsum(logf, B, S, 1.0 / fox_scale)
    (q_fox, k_fox, v_fox), wout_half = _fox_prep(proj, cs, lay, comm=own_plan(small_s[4:5]))
    ((o_mla, lse_mla), (o_fox, lse_fox)), landed = _attn_fwd(
        [(q_mla, k_mla, v_mla, MLA_UNIT, mla_scale), (q_fox, k_fox, v_fox, 1, fox_scale)], B=B, S=S,
        name="attn_fwd", comm=_Comm([_GatherOwnPlan([wup_s], rows=(half_d, D), into=wup_part),
                                     _GatherPassPlan(branch_half + wout_half)]))
    wup_half, (wbm_g, wbf_g, wout_g) = landed[:1], landed[1:]
    wbm = jnp.transpose(wbm_g, (1, 0, 2)).reshape(HEADS * VDIM, D)
    wbf = jnp.transpose(wbf_g, (1, 0, 2)).reshape(HEADS * FOX_DIM, D)
    wout = wout_g.reshape(D, D)
    a_m = _matmul(o_mla, wbm, mode="nn", name="mm_branch_mla")
    a_f = _matmul(o_fox, wbf, mode="nn", name="mm_branch_fox")
    (merged,), (wup_g,) = _gate_merge(a_m, a_f, proj, bgate, lay, D, comm=pass_plan(wup_half))
    n_up = wup_g.shape[2]
    y1 = _matmul(merged, wout, mode="nn", name="mm_out")
    x1, h2 = _mid(x2, y1, g_pm, g_ffn)
    u, wdown_half = _matmul(h2, wup_g, mode="nn", name="mm_up", tn=n_up, comm=own_plan([wdown_s]))
    (act,), (wdown_g,) = _convffn_fwd(u, cw_full, cb_full, B, S, F, comm=pass_plan(wdown_half))
    wdown = wdown_g.reshape(F, D)
    ff = _matmul(act, wdown, mode="nn", name="mm_down", tk=F // 2)
    dy, dff, loss_part, dg_pf = _tail(ff, x1, tgt, g_pf)

    dact = _matmul(dff, wdown, mode="nt", name="mm_dact", tn=F // 4)
    dw_down = _matmul(act, dff, mode="tn", name="mm_dw_down", tm=F // 4, tn=512).reshape(N_DEV, F // N_DEV, D)
    (du_g, du_v, dcp_g, dcp_v), (pa_down,) = _convffn_bwd(u, dact, cw_full, cb_full, B, S, F, comm=pair_plan([dw_down]))
    keep_down, sb_down = _chip_sum(dw_down, pa_down, qc, "chipsum_w_down")
    dh2, (rb_down,) = _matmul_halves((du_g, du_v), wup_g, mode="nt", name="mm_dh2", tm=MM_TILE, comm=chip_plan([sb_down]))
    dw_up = _matmul_halves(h2, (du_g, du_v), mode="tn", name="mm_dw_up", tm=MM_TILE, tn=n_up, tk=T // 2)
    (dx1, dy1, dg_ffn, dg_pm), pa_up_part = _mid_bwd(
        dy, dh2, x1, y1, g_ffn, g_pm, comm=_Comm([_PairScatterPlan([dw_up], rows=(0, half_d))]))
    dmerged = _matmul(dy1, wout, mode="nt", name="mm_dmerged")
    dw_out = _matmul(merged, dy1, mode="tn", name="mm_dw_out").reshape(N_DEV, D // N_DEV, D)
    (da_m, da_f, dgl_m, dgl_f, dbg_m, dbg_f), (pa_up,) = _gate_bwd(
        dmerged, a_m, a_f, proj, bgate, lay, D,
        comm=_Comm([_PairScatterPlan([dw_up], rows=(half_d, D), into=pa_up_part)]))
    keep_up, sb_up = _chip_sum(dw_up, pa_up, qc, "chipsum_w_up")
    dw_bm = _matmul(o_mla, da_m, mode="tn", name="mm_dw_branch_mla", out_blocks=D // N_DEV)
    dw_bf = _matmul(o_fox, da_f, mode="tn", name="mm_dw_branch_fox", out_blocks=D // N_DEV)
    mix = [dw_out, dw_bm, dw_bf]
    do_mla, pa_mix = _matmul(da_m, wbm, mode="nt", name="mm_do_mla", out_dtype=BF16, comm=pair_plan(mix))
    do_fox = _matmul(da_f, wbf, mode="nt", name="mm_do_fox", out_dtype=BF16)
    mix_sums = [_chip_sum(g, p, qc, "chipsum_" + n) for g, p, n in zip(mix, pa_mix, ["w_out", "w_branch_mla", "w_branch_fox"])]
    ((dq_m, dk_m, dv_m), (dq_f, dk_f, dv_f)), (rb_up,) = _attn_bwd(
        [(q_mla, k_mla, v_mla, o_mla, do_mla, lse_mla, MLA_UNIT, mla_scale),
         (q_fox, k_fox, v_fox, o_fox, do_fox, lse_fox, 1, fox_scale)], B=B, S=S, name="attn_bwd",
        comm=chip_plan([sb_up]))
    (dqraw, dkvraw, dkpe), _ = _mla_bwd_prep(dq_m, dk_m, dv_m, cosT, sinT)
    dqn = _matmul(dqraw, wuq_pad, mode="nt", name="mm_dqn", tm=T)
    dw_uq = _matmul(qn, dqraw, mode="tn", name="mm_dw_uq", out_blocks=ATT_DK)[:, :, :NOPE + ROPE]
    dkvn = _matmul(dkvraw, wukv_g, mode="nt", name="mm_dkvn", tm=T)
    dw_ukv = _matmul(kvn, dkvraw, mode="tn", name="mm_dw_ukv", out_blocks=NOPE + VDIM)
    dqlat, dkvlat, dg_q, dg_kv = _lat_bwd(dqn, dkvn, proj, g_q, g_kv, lay)
    dfl, dbfor = _fox_bwd_prep(dq_f, dk_f, proj, bfor, lay, B, S, 1.0 / fox_scale)
    dproj = _concat_cols([dqlat, dkvlat, dkpe, dq_f, dk_f, dv_f, dfl, dgl_m, dgl_f], "concat_dproj")
    dw_perm, rb_mix = _matmul(h, dproj, mode="tn", name="mm_dw_in", comm=chip_plan([s[1] for s in mix_sums]))
    segs = [(0, seg_a, 0), (seg_a, seg_a + seg_b, lay["fq"] - seg_a), (seg_a + seg_b, d_in, lay["g"] - seg_a - seg_b)]

    def piece(g):
        lo, hi = g * n_in, (g + 1) * n_in
        parts = [dw_perm[:, max(lo, s0) + sh:min(hi, s1) + sh] for s0, s1, sh in segs if max(lo, s0) < min(hi, s1)]
        return parts[0] if len(parts) == 1 else jnp.concatenate(parts, axis=1)

    dw_in = jnp.stack([piece(g) for g in range(N_DEV)])
    dcw = jnp.transpose(jnp.concatenate([dcp_g[0:3], dcp_v[0:3]], axis=1).reshape(3, N_DEV, (2 * F) // N_DEV), (1, 0, 2))
    late = [dw_in, dw_uq, dw_ukv, dcw]
    pa_late = _exchange_alone(pair_plan(late), "pair_late")
    late_sums = [_chip_sum(g, p, qc, "chipsum_" + n) for g, p, n in zip(late, pa_late, ["w_in", "w_uq", "w_ukv", "conv_w"])]
    dh, rb_late = _matmul(dproj, w_perm, mode="nt", name="mm_dh", tn=2048, tk=2048, comm=chip_plan([s[1] for s in late_sums]))
    (grad_x, dg_pre), _ = _final_dx(dx1, dh, x2, g_pre)

    big_out = {}

    def finish(n, keep, pieces, w, m, v):
        big_out[n] = _sum_adamw(keep, pieces, w, m, v, "adamw_" + n)

    finish("w_down", keep_down, rb_down, w_down, m_w_down, v_w_down)
    finish("w_up", keep_up, rb_up, w_up, m_w_up, v_w_up)
    finish("w_out", mix_sums[0][0], rb_mix[0], w_out, m_w_out, v_w_out)
    finish("w_branch_mla", mix_sums[1][0], rb_mix[1], w_branch_mla, m_w_branch_mla, v_w_branch_mla)
    finish("w_branch_fox", mix_sums[2][0], rb_mix[2], w_branch_fox, m_w_branch_fox, v_w_branch_fox)
    finish("w_in", late_sums[0][0], rb_late[0], w_in, m_w_in, v_w_in)
    finish("w_uq", late_sums[1][0], rb_late[1], w_uq, m_w_uq, v_w_uq)
    finish("w_ukv", late_sums[2][0], rb_late[2], w_ukv, m_w_ukv, v_w_ukv)
    finish("conv_w", late_sums[3][0], rb_late[3], conv_w, m_conv_w, v_conv_w)

    widths = [D, Q_LORA, KV_LORA, LANES, 2 * D, D, D, 2 * F, D]
    small_names = ["pre_mix_norm", "q_a_norm", "kv_a_norm", "b_forget", "b_gate", "post_mix_norm", "pre_ffn_norm",
                   "conv_b", "post_ffn_norm"]
    true_w = [D, Q_LORA, KV_LORA, HEADS, 2 * D, D, D, 2 * F, D]
    dcb = jnp.concatenate([dcp_g[3:4], dcp_v[3:4]], axis=1)
    part = jnp.concatenate([dg_pre, dg_q, dg_kv, dbfor, dbg_m, dbg_f, dg_pm, dg_ffn, dcb, dg_pf], axis=1)

    def pack(vals):
        return jnp.concatenate([row(a, wd) for a, wd in zip(vals, widths)], axis=1)

    sw = pack([pre_mix_norm, q_a_norm, kv_a_norm, b_forget, b_gate, post_mix_norm, pre_ffn_norm, conv_b, post_ffn_norm])
    sm = pack([m_pre_mix_norm, m_q_a_norm, m_kv_a_norm, m_b_forget, m_b_gate, m_post_mix_norm, m_pre_ffn_norm,
               m_conv_b, m_post_ffn_norm])
    sv = pack([v_pre_mix_norm, v_q_a_norm, v_kv_a_norm, v_b_forget, v_b_gate, v_post_mix_norm, v_pre_ffn_norm,
               v_conv_b, v_post_ffn_norm])
    (parts_all,) = _exchange_alone(_Comm([_DirectGatherPlan([part])]), "gather_small")
    sg, sd, smo, svo = _sum_adamw(None, parts_all, sw, sm, sv, "adamw_small")
    small_out = {}
    off = 0
    for n, wd, tw in zip(small_names, widths, true_w):
        small_out[n] = tuple(a[0, off:off + tw] for a in (sg, sd, smo, svo))
        off += wd

    loss = lax.psum(loss_part[0, 0], MESH_AXES)
    order = ["pre_mix_norm", "w_in", "q_a_norm", "w_uq", "kv_a_norm", "w_ukv", "b_forget", "b_gate", "w_branch_mla",
             "w_branch_fox", "w_out", "post_mix_norm", "pre_ffn_norm", "w_up", "conv_w", "conv_b", "w_down",
             "post_ffn_norm"]
    res = {**big_out, **small_out}
    outs = [loss, grad_x.reshape(B, S, D)]
    for kind in range(4):
        outs += [res[n][kind] for n in order]
    return tuple(outs)
```

```python
import math

import jax
import jax.numpy as jnp
from jax import lax
from jax.experimental import pallas as pl
from jax.experimental.pallas import tpu as pltpu

F32 = jnp.float32
BF16 = jnp.bfloat16

N_DEV = 8
N_CHIP = 4
HEADS = 8
NOPE = 128
ROPE = 64
HALF_ROPE = ROPE // 2
VDIM = 128
Q_LORA = 512
KV_LORA = 256
FOX_DIM = 128
ATT_DK = 256
MLA_UNIT = 64
ROPE_THETA = 10000.0
EPS = 1e-6
NEG_INF = -1e30
LANES = 128
LOG2_E = 1.4426950408889634

ADAM_LR = 0.001
ADAM_B1 = 0.9
ADAM_B2 = 0.999
ADAM_EPS = 1e-08
ADAM_WD = 0.01
ADAM_STEP = 10

VMEM_LIMIT_BYTES = 56 * 1024 * 1024
ROW_TILE = 256
HEAD_ROW_TILE = 1024
ATT_TILE = 1024
ATT_SUB = 256
ATT_AHEAD = 3
MM_TILE = 1024

MESH_AXES = ("x", "y", "c")
ANY = pl.BlockSpec(memory_space=pl.ANY)


def _tile(n, pref, align=LANES):
    if n <= pref:
        return n
    t = (pref // align) * align
    while t >= align:
        if n % t == 0:
            return t
        t -= align
    return n


def _sds(shape, dtype):
    return jax.ShapeDtypeStruct(shape, dtype)


def _coords():
    x, y, c = (lax.axis_index(ax) for ax in MESH_AXES)
    return x, y, c


def _chip_rel(x, y, r):
    return (1 - x if r & 2 else x), (1 - y if r & 1 else y)


def _rcopy(src, dst, sems, w, k, dev):
    return pltpu.make_async_remote_copy(src_ref=src, dst_ref=dst, send_sem=sems[0].at[w, k], recv_sem=sems[1].at[w, k],
                                        device_id=dev, device_id_type=pl.DeviceIdType.MESH)


class _GatherRelayPlan:
    def __init__(self, blocks, mid_frac=0.5):
        self.ins = list(blocks)
        self.out_shapes = [_sds((N_DEV,) + b.shape, b.dtype) for b in blocks]
        n = len(blocks)
        self.scratch = [pltpu.SemaphoreType.DMA((n, 7)), pltpu.SemaphoreType.DMA((n, 7)), pltpu.SemaphoreType.DMA((n,))]
        self.mid_frac = mid_frac

    @staticmethod
    def _places():
        x, y, c = _coords()
        xn, yn = 4 * (1 - x) + 2 * y, 4 * x + 2 * (1 - y)
        relay_src = 4 * (x + c * (1 - 2 * x)) + 2 * (y + (1 - c) * (1 - 2 * y)) + c
        relay_to = (x + (1 - c) * (1 - 2 * x), y + c * (1 - 2 * y), c)
        return x, y, c, xn, yn, relay_src, relay_to, 4 * (1 - x) + 2 * (1 - y)

    def first(self, ins, outs, sems):
        x, y, c, _, _, _, _, _ = self._places()
        me = 4 * x + 2 * y + c
        for w in range(len(ins)):
            pltpu.make_async_copy(ins[w], outs[w].at[me], sems[2].at[w]).start()
            _rcopy(ins[w], outs[w].at[me], sems, w, 0, (x, y, 1 - c)).start()
            _rcopy(ins[w], outs[w].at[me], sems, w, 1, (1 - x, y, c)).start()
            _rcopy(ins[w], outs[w].at[me], sems, w, 2, (x, 1 - y, c)).start()

    def mid(self, ins, outs, sems):
        x, y, c, xn, yn, relay_src, relay_to, _ = self._places()
        sib = (x, y, 1 - c)
        for w in range(len(ins)):
            bx, by = outs[w].at[xn + c], outs[w].at[yn + c]
            _rcopy(ins[w], bx, sems, w, 1, (1 - x, y, c)).wait_recv()
            _rcopy(ins[w], by, sems, w, 2, (x, 1 - y, c)).wait_recv()
            _rcopy(outs[w].at[relay_src], outs[w].at[relay_src], sems, w, 3, relay_to).start()
            _rcopy(bx, bx, sems, w, 4, sib).start()
            _rcopy(by, by, sems, w, 5, sib).start()

    def last(self, ins, outs, sems):
        x, y, c, xn, yn, _, relay_to, dg = self._places()
        me = 4 * x + 2 * y + c
        sib = (x, y, 1 - c)
        for w in range(len(ins)):
            bd = outs[w].at[dg + c]
            _rcopy(ins[w], bd, sems, w, 3, relay_to).wait_recv()
            _rcopy(bd, bd, sems, w, 6, sib).start()
            for k, blk in ((0, 4 * x + 2 * y), (4, xn), (5, yn), (6, dg)):
                _rcopy(ins[w], outs[w].at[blk + 1 - c], sems, w, k, sib).wait_recv()
            for k in range(7):
                _rcopy(ins[w], outs[w].at[me], sems, w, k, sib).wait_send()
            pltpu.make_async_copy(ins[w], outs[w].at[me], sems[2].at[w]).wait()


class _GatherOwnPlan:
    mid = None

    def __init__(self, blocks, rows=None, into=None):
        self.n = len(blocks)
        self.rows = rows
        self.ins = list(blocks) + list(into or [])
        self.out_shapes = [_sds((N_DEV,) + b.shape, b.dtype) for b in blocks]
        self.aliases = [(self.n + i, i) for i in range(len(into or []))]
        n = self.n
        self.scratch = [pltpu.SemaphoreType.DMA((n, 4)), pltpu.SemaphoreType.DMA((n, 4)), pltpu.SemaphoreType.DMA((n,))]

    def _cut(self, ref):
        return ref if self.rows is None else ref.at[pl.ds(self.rows[0], self.rows[1] - self.rows[0])]

    def first(self, ins, outs, sems):
        x, y, c = _coords()
        me = 4 * x + 2 * y + c
        for w in range(self.n):
            src, dst = self._cut(ins[w]), self._cut(outs[w].at[me])
            pltpu.make_async_copy(src, dst, sems[2].at[w]).start()
            _rcopy(src, dst, sems, w, 0, (x, y, 1 - c)).start()
            for r in (1, 2, 3):
                px, py = _chip_rel(x, y, r)
                _rcopy(src, dst, sems, w, r, (px, py, c)).start()

    def last(self, ins, outs, sems):
        x, y, c = _coords()
        me = 4 * x + 2 * y + c
        for w in range(self.n):
            src = self._cut(ins[w])
            cp = _rcopy(src, self._cut(outs[w].at[4 * x + 2 * y + 1 - c]), sems, w, 0, (x, y, 1 - c))
            cp.wait_recv()
            cp.wait_send()
            for r in (1, 2, 3):
                px, py = _chip_rel(x, y, r)
                cp = _rcopy(src, self._cut(outs[w].at[4 * px + 2 * py + c]), sems, w, r, (px, py, c))
                cp.wait_recv()
                cp.wait_send()
            pltpu.make_async_copy(src, self._cut(outs[w].at[me]), sems[2].at[w]).wait()


class _GatherPassPlan:
    mid = None

    def __init__(self, gathered):
        self.ins = list(gathered)
        self.out_shapes = [_sds(g.shape, g.dtype) for g in gathered]
        self.aliases = [(i, i) for i in range(len(gathered))]
        n = len(gathered)
        self.scratch = [pltpu.SemaphoreType.DMA((n, 3)), pltpu.SemaphoreType.DMA((n, 3))]

    def first(self, ins, outs, sems):
        x, y, c = _coords()
        for w in range(len(ins)):
            for r in (1, 2, 3):
                px, py = _chip_rel(x, y, r)
                blk = 4 * px + 2 * py + c
                _rcopy(ins[w].at[blk], outs[w].at[blk], sems, w, r - 1, (x, y, 1 - c)).start()

    def last(self, ins, outs, sems):
        x, y, c = _coords()
        for w in range(len(ins)):
            for r in (1, 2, 3):
                px, py = _chip_rel(x, y, r)
                blk = 4 * px + 2 * py + 1 - c
                cp = _rcopy(ins[w].at[blk], outs[w].at[blk], sems, w, r - 1, (x, y, 1 - c))
                cp.wait_recv()
                cp.wait_send()


class _DirectGatherPlan:
    mid = None

    def __init__(self, blocks):
        self.ins = list(blocks)
        self.out_shapes = [_sds((N_DEV,) + b.shape, b.dtype) for b in blocks]
        n = len(blocks)
        self.scratch = [pltpu.SemaphoreType.DMA((n, 7)), pltpu.SemaphoreType.DMA((n, 7)), pltpu.SemaphoreType.DMA((n,))]

    @staticmethod
    def _peer(x, y, c, r):
        return (1 - x if r & 4 else x), (1 - y if r & 2 else y), (1 - c if r & 1 else c)

    def first(self, ins, outs, sems):
        x, y, c = _coords()
        me = 4 * x + 2 * y + c
        for w in range(len(ins)):
            pltpu.make_async_copy(ins[w], outs[w].at[me], sems[2].at[w]).start()
            for r in range(1, N_DEV):
                _rcopy(ins[w], outs[w].at[me], sems, w, r - 1, self._peer(x, y, c, r)).start()

    def last(self, ins, outs, sems):
        x, y, c = _coords()
        me = 4 * x + 2 * y + c
        for w in range(len(ins)):
            for r in range(1, N_DEV):
                px, py, pc = self._peer(x, y, c, r)
                cp = _rcopy(ins[w], outs[w].at[4 * px + 2 * py + pc], sems, w, r - 1, (px, py, pc))
                cp.wait_recv()
                cp.wait_send()
            pltpu.make_async_copy(ins[w], outs[w].at[me], sems[2].at[w]).wait()


class _PairScatterPlan:
    mid = None

    def __init__(self, pieces, rows=None, into=None):
        self.n = len(pieces)
        self.rows = rows
        self.ins = list(pieces) + list(into or [])
        self.out_shapes = [_sds((N_CHIP,) + p.shape[1:], p.dtype) for p in pieces]
        self.aliases = [(self.n + i, i) for i in range(len(into or []))]
        self.scratch = [pltpu.SemaphoreType.DMA((self.n, N_CHIP)), pltpu.SemaphoreType.DMA((self.n, N_CHIP))]

    def _copies(self, ins, outs, sems):
        x, y, c = _coords()
        cps = []
        for w in range(self.n):
            for q in range(N_CHIP):
                src, dst = ins[w].at[2 * q + 1 - c], outs[w].at[q]
                if self.rows is not None:
                    cut = pl.ds(self.rows[0], self.rows[1] - self.rows[0])
                    src, dst = src.at[cut], dst.at[cut]
                cps.append(_rcopy(src, dst, sems, w, q, (x, y, 1 - c)))
        return cps

    def first(self, ins, outs, sems):
        for cp in self._copies(ins, outs, sems):
            cp.start()

    def last(self, ins, outs, sems):
        for cp in self._copies(ins, outs, sems):
            cp.wait_recv()
            cp.wait_send()


class _ChipScatterPlan:
    mid = None

    def __init__(self, sums, rows=None, into=None):
        self.n = len(sums)
        self.rows = rows
        self.ins = list(sums) + list(into or [])
        self.out_shapes = [_sds(s.shape, s.dtype) for s in sums]
        self.aliases = [(self.n + i, i) for i in range(len(into or []))]
        self.scratch = [pltpu.SemaphoreType.DMA((self.n, 3)), pltpu.SemaphoreType.DMA((self.n, 3))]

    def _copies(self, ins, outs, sems):
        x, y, c = _coords()
        cps = []
        for w in range(self.n):
            for r in (1, 2, 3):
                px, py = _chip_rel(x, y, r)
                src, dst = ins[w].at[r - 1], outs[w].at[r - 1]
                if self.rows is not None:
                    cut = pl.ds(self.rows[0], self.rows[1] - self.rows[0])
                    src, dst = src.at[cut], dst.at[cut]
                cps.append(_rcopy(src, dst, sems, w, r - 1, (px, py, c)))
        return cps

    def first(self, ins, outs, sems):
        for cp in self._copies(ins, outs, sems):
            cp.start()

    def last(self, ins, outs, sems):
        for cp in self._copies(ins, outs, sems):
            cp.wait_recv()
            cp.wait_send()


class _Comm:
    def __init__(self, plans):
        self.plans = list(plans)
        self.ins = [a for p in self.plans for a in p.ins]
        self.out_shapes = [s for p in self.plans for s in p.out_shapes]
        self.scratch = [s for p in self.plans for s in p.scratch]
        self.aliases = []
        i = o = 0
        for p in self.plans:
            self.aliases += [(i + a, o + b) for a, b in getattr(p, "aliases", [])]
            i, o = i + len(p.ins), o + len(p.out_shapes)

    def _parts(self, ins, outs, sems):
        i = o = s = 0
        for p in self.plans:
            yield p, ins[i:i + len(p.ins)], outs[o:o + len(p.out_shapes)], sems[s:s + len(p.scratch)]
            i, o, s = i + len(p.ins), o + len(p.out_shapes), s + len(p.scratch)

    def begin(self, step, nsteps, ins, outs, sems):
        @pl.when(step == 0)
        def _():
            for p, pi, po, ps in self._parts(ins, outs, sems):
                p.first(pi, po, ps)

        for p, pi, po, ps in self._parts(ins, outs, sems):
            if p.mid is not None:
                @pl.when(step == min(nsteps - 1, int(p.mid_frac * nsteps)))
                def _(p=p, pi=pi, po=po, ps=ps):
                    p.mid(pi, po, ps)

    def end(self, step, nsteps, ins, outs, sems):
        @pl.when(step == nsteps - 1)
        def _():
            for p, pi, po, ps in self._parts(ins, outs, sems):
                p.last(pi, po, ps)


def _call(body, args, *, name, grid, in_specs, out_specs, out_shape, scratch_shapes=(), sem=None, comm=None):
    in_specs, out_specs, out_shape, scratch_shapes = list(in_specs), list(out_specs), list(out_shape), list(scratch_shapes)
    if comm is None:
        res = pl.pallas_call(
            body, name=name, grid=grid, in_specs=in_specs, out_specs=out_specs, out_shape=out_shape,
            scratch_shapes=scratch_shapes,
            compiler_params=pltpu.CompilerParams(dimension_semantics=sem, vmem_limit_bytes=VMEM_LIMIT_BYTES),
        )(*args)
        return list(res), []
    n_in, n_out, n_sc = len(in_specs), len(out_specs), len(scratch_shapes)
    n_ci, n_co = len(comm.ins), len(comm.out_shapes)
    nsteps = math.prod(grid)

    def hosted(*refs):
        ins, cins = refs[:n_in], refs[n_in:n_in + n_ci]
        o0 = n_in + n_ci
        outs, couts = refs[o0:o0 + n_out], refs[o0 + n_out:o0 + n_out + n_co]
        s0 = o0 + n_out + n_co
        scr, csems = refs[s0:s0 + n_sc], refs[s0 + n_sc:]
        step = jnp.int32(0)
        for d in range(len(grid)):
            step = step * grid[d] + pl.program_id(d)
        comm.begin(step, nsteps, cins, couts, csems)
        body(*ins, *outs, *scr)
        comm.end(step, nsteps, cins, couts, csems)

    res = pl.pallas_call(
        hosted, name=name, grid=grid, in_specs=in_specs + [ANY] * n_ci, out_specs=out_specs + [ANY] * n_co,
        out_shape=out_shape + comm.out_shapes, scratch_shapes=scratch_shapes + comm.scratch,
        input_output_aliases={n_in + a: n_out + b for a, b in comm.aliases},
        compiler_params=pltpu.CompilerParams(dimension_semantics=("arbitrary",) * len(grid),
                                             vmem_limit_bytes=VMEM_LIMIT_BYTES, has_side_effects=True),
    )(*args, *comm.ins)
    return list(res[:n_out]), list(res[n_out:])


def _exchange_alone(comm, name):
    def body():
        pass

    return _call(body, [], name=name, grid=(), in_specs=[], out_specs=[], out_shape=[], comm=comm)[1]


def _matmul(a, b, *, mode, name, out_dtype=F32, out_blocks=None, tm=None, tn=None, tk=None, comm=None):
    tm = MM_TILE if tm is None else tm
    tn = MM_TILE if tn is None else tn
    a_blk = a.ndim == 3
    b_blk = b.ndim == 3
    if mode == "nn":
        M, K = a.shape
        N = b.shape[0] * b.shape[2] if b_blk else b.shape[1]
        dims = (((1,), (0,)), ((), ()))
    elif mode == "nt":
        M = a.shape[1] if a_blk else a.shape[0]
        K = a.shape[0] * a.shape[2] if a_blk else a.shape[1]
        N = b.shape[1] if b_blk else b.shape[0]
        dims = (((1,), (1,)), ((), ()))
    else:
        K, M = a.shape
        N = b.shape[0] * b.shape[2] if b_blk else b.shape[1]
        dims = (((0,), (0,)), ((), ()))

    tm = _tile(M, tm)
    tn = _tile(N, tn)
    if mode == "nt" and (a_blk or b_blk):
        tk = a.shape[2] if a_blk else b.shape[2]
    else:
        tk = _tile(K, K if tk is None else tk)
    if mode != "nt" and b_blk:
        tn = _tile(b.shape[2], tn)
    if out_blocks is not None:
        tn = _tile(out_blocks, tn)
    nk = K // tk
    grid = (M // tm, N // tn, nk)

    if mode == "nn":
        a_spec = pl.BlockSpec((tm, tk), lambda i, j, k: (i, k))
        if b_blk:
            rb = b.shape[2] // tn
            b_spec = pl.BlockSpec((None, tk, tn), lambda i, j, k: (j // rb, k, j % rb))
        else:
            b_spec = pl.BlockSpec((tk, tn), lambda i, j, k: (k, j))
    elif mode == "nt":
        if a_blk:
            a_spec = pl.BlockSpec((None, tm, tk), lambda i, j, k: (k, i, 0))
        else:
            a_spec = pl.BlockSpec((tm, tk), lambda i, j, k: (i, k))
        if b_blk:
            b_spec = pl.BlockSpec((None, tn, tk), lambda i, j, k: (k, j, 0))
        else:
            b_spec = pl.BlockSpec((tn, tk), lambda i, j, k: (j, k))
    else:
        a_spec = pl.BlockSpec((tk, tm), lambda i, j, k: (k, i))
        if b_blk:
            rb = b.shape[2] // tn
            b_spec = pl.BlockSpec((None, tk, tn), lambda i, j, k: (j // rb, k, j % rb))
        else:
            b_spec = pl.BlockSpec((tk, tn), lambda i, j, k: (k, j))

    if out_blocks is None:
        o_spec = pl.BlockSpec((tm, tn), lambda i, j, k: (i, j))
        o_shape = _sds((M, N), out_dtype)
    else:
        ro = out_blocks // tn
        o_spec = pl.BlockSpec((None, tm, tn), lambda i, j, k: (j // ro, i, j % ro))
        o_shape = _sds((N // out_blocks, M, out_blocks), out_dtype)

    direct = nk == 1 or out_dtype == F32

    def body(a_ref, b_ref, o_ref, *scratch):
        if nk == 1:
            o_ref[...] = lax.dot_general(a_ref[...], b_ref[...], dims, preferred_element_type=F32).astype(o_ref.dtype)
            return
        acc_ref = o_ref if direct else scratch[0]
        k = pl.program_id(2)

        @pl.when(k == 0)
        def _():
            acc_ref[...] = jnp.zeros(acc_ref.shape, F32)

        acc_ref[...] += lax.dot_general(a_ref[...], b_ref[...], dims, preferred_element_type=F32)
        if not direct:
            @pl.when(k == nk - 1)
            def _():
                o_ref[...] = acc_ref[...].astype(o_ref.dtype)

    scratch = [] if direct else [pltpu.VMEM((tm, tn), F32)]
    outs, landed = _call(body, [a, b], name=name, grid=grid, in_specs=[a_spec, b_spec], out_specs=[o_spec],
                         out_shape=[o_shape], scratch_shapes=scratch, sem=("parallel", "parallel", "arbitrary"), comm=comm)
    return outs[0] if comm is None else (outs[0], landed)


def _matmul_halves(a, b, *, mode, name, tm, tn=None, tk=None, comm=None):
    if mode == "nt":
        lo, hi = a
        M, kh = lo.shape
        G, N, kb = b.shape
        half = kh // kb
        tm, tn = _tile(M, tm), _tile(N, N if tn is None else tn)
        dims = (((1,), (1,)), ((), ()))

        def body(lo_ref, hi_ref, b_ref, o_ref):
            k = pl.program_id(2)

            @pl.when(k == 0)
            def _():
                o_ref[...] = jnp.zeros(o_ref.shape, F32)

            @pl.when(k < half)
            def _():
                o_ref[...] += lax.dot_general(lo_ref[...], b_ref[...], dims, preferred_element_type=F32)

            @pl.when(k >= half)
            def _():
                o_ref[...] += lax.dot_general(hi_ref[...], b_ref[...], dims, preferred_element_type=F32)

        outs, landed = _call(
            body, [lo, hi, b], name=name, grid=(M // tm, N // tn, G),
            in_specs=[pl.BlockSpec((tm, kb), lambda i, j, k: (i, jnp.minimum(k, half - 1))),
                      pl.BlockSpec((tm, kb), lambda i, j, k: (i, jnp.maximum(k - half, 0))),
                      pl.BlockSpec((None, tn, kb), lambda i, j, k: (k, j, 0))],
            out_specs=[pl.BlockSpec((tm, tn), lambda i, j, k: (i, j))], out_shape=[_sds((M, N), F32)],
            sem=("parallel", "parallel", "arbitrary"), comm=comm)
    else:
        lo, hi = b
        K, nh = lo.shape
        M = a.shape[1]
        n = tn
        half = nh // n
        tm, tk = _tile(M, tm), _tile(K, K if tk is None else tk)
        nk = K // tk
        dims = (((0,), (0,)), ((), ()))

        def body(a_ref, lo_ref, hi_ref, o_ref):
            j, k = pl.program_id(1), pl.program_id(2)

            @pl.when(k == 0)
            def _():
                o_ref[...] = jnp.zeros(o_ref.shape, F32)

            @pl.when(j < half)
            def _():
                o_ref[...] += lax.dot_general(a_ref[...], lo_ref[...], dims, preferred_element_type=F32)

            @pl.when(j >= half)
            def _():
                o_ref[...] += lax.dot_general(a_ref[...], hi_ref[...], dims, preferred_element_type=F32)

        outs, landed = _call(
            body, [a, lo, hi], name=name, grid=(M // tm, 2 * half, nk),
            in_specs=[pl.BlockSpec((tk, tm), lambda i, j, k: (k, i)),
                      pl.BlockSpec((tk, n), lambda i, j, k: (jnp.where(j < half, k, nk - 1), jnp.minimum(j, half - 1))),
                      pl.BlockSpec((tk, n), lambda i, j, k: (jnp.where(j >= half, k, 0), jnp.maximum(j - half, 0)))],
            out_specs=[pl.BlockSpec((None, tm, n), lambda i, j, k: (j, i, 0))],
            out_shape=[_sds((2 * half, M, n), F32)], sem=("parallel", "parallel", "arbitrary"), comm=comm)
    return outs[0] if comm is None else (outs[0], landed)


def _rms(x):
    return lax.rsqrt(jnp.mean(x * x, axis=-1, keepdims=True) + EPS)


def _rms_bwd(dy, x, g):
    r = _rms(x)
    n = x * r
    dn = dy * g
    dx = r * (dn - n * jnp.mean(dn * n, axis=-1, keepdims=True))
    return dx, dy * n


def _sigmoid(x):
    return 1.0 / (1.0 + jnp.exp(-x))


def _rope_rot(t):
    return pltpu.roll(t, HALF_ROPE, 1) - pltpu.roll(t, LANES - HALF_ROPE, 1)


def _lane(shape):
    return lax.broadcasted_iota(jnp.int32, shape, 1)


def _split3(x):
    hi = x.astype(BF16).astype(F32)
    r1 = x - hi
    mid = r1.astype(BF16).astype(F32)
    lo = (r1 - mid).astype(BF16).astype(F32)
    return hi, mid, lo


def _cumsum_rows(x, reverse):
    S = x.shape[0]
    bs = min(256, S)
    nb = S // bs
    r = lax.broadcasted_iota(jnp.int32, (bs, bs), 0)
    c = lax.broadcasted_iota(jnp.int32, (bs, bs), 1)
    tri = jnp.where((c >= r) if reverse else (c <= r), 1.0, 0.0).astype(BF16)
    edge = lax.broadcasted_iota(jnp.int32, (bs, x.shape[1]), 0) == (0 if reverse else bs - 1)
    carry = jnp.zeros((1, x.shape[1]), F32)
    outs = [None] * nb
    for bi in (range(nb - 1, -1, -1) if reverse else range(nb)):
        xb = x[bi * bs:(bi + 1) * bs, :]
        acc = carry
        for term in _split3(xb):
            acc = acc + jnp.dot(tri, term.astype(BF16), preferred_element_type=F32)
        outs[bi] = acc
        carry = jnp.sum(jnp.where(edge, acc, 0.0), axis=0, keepdims=True)
    return jnp.concatenate(outs, axis=0) if nb > 1 else outs[0]


def _gelu_parts(x):
    c0 = math.sqrt(2.0 / math.pi)
    inner = c0 * (x + 0.044715 * (x * x * x))
    t = jnp.tanh(inner)
    g = 0.5 * x * (1.0 + t)
    dg = 0.5 * (1.0 + t) + 0.5 * x * (1.0 - t * t) * (c0 * (1.0 + 3.0 * 0.044715 * (x * x)))
    return g, dg


def _accumulate(ref, value, first):
    @pl.when(first)
    def _():
        ref[...] = value

    @pl.when(jnp.logical_not(first))
    def _():
        ref[...] += value


def _cast_bf16(w, name):
    R, C = w.shape
    tr = _tile(R, 512, 16)

    def body(w_ref, o_ref):
        o_ref[...] = w_ref[...].astype(BF16)

    blk = pl.BlockSpec((tr, C), lambda i: (i, 0))
    return _call(body, [w], name=name, grid=(R // tr,), in_specs=[blk], out_specs=[blk],
                 out_shape=[_sds((R, C), BF16)], sem=("parallel",))[0][0]


def _concat_cols(parts, name):
    T = parts[0].shape[-2] if parts[0].ndim == 3 else parts[0].shape[0]
    widths = [p.shape[0] * LANES if p.ndim == 3 else p.shape[1] for p in parts]
    tm = _tile(T, ROW_TILE, 16)

    def body(*refs):
        o_ref = refs[-1]
        off = 0
        for p_ref, p, w in zip(refs[:-1], parts, widths):
            if p.ndim == 3:
                for hd in range(p.shape[0]):
                    o_ref[:, off + hd * LANES:off + (hd + 1) * LANES] = p_ref[hd].astype(BF16)
            else:
                o_ref[:, off:off + w] = p_ref[...].astype(BF16)
            off += w

    def spec(p, w):
        if p.ndim == 3:
            return pl.BlockSpec((p.shape[0], tm, LANES), lambda i: (0, i, 0))
        return pl.BlockSpec((tm, w), lambda i: (i, 0))

    return _call(body, parts, name=name, grid=(T // tm,),
                 in_specs=[spec(p, w) for p, w in zip(parts, widths)],
                 out_specs=[pl.BlockSpec((tm, sum(widths)), lambda i: (i, 0))],
                 out_shape=[_sds((T, sum(widths)), BF16)], sem=("parallel",))[0][0]


def _prenorm(x, g, comm=None):
    T, D = x.shape
    tm = _tile(T, ROW_TILE, 16)

    def body(x_ref, g_ref, h_ref):
        xv = x_ref[...]
        h_ref[...] = (xv * _rms(xv) * g_ref[...]).astype(BF16)

    row = pl.BlockSpec((tm, D), lambda i: (i, 0))
    (h,), landed = _call(body, [x, g], name="prenorm", grid=(T // tm,),
                         in_specs=[row, pl.BlockSpec((1, D), lambda i: (0, 0))], out_specs=[row],
                         out_shape=[_sds((T, D), BF16)], sem=("parallel",), comm=comm)
    return h, landed


def _split_prep(proj, pos, invf, gq, gkv, bfor, lay, comm=None):
    T = proj.shape[0]
    tm = _tile(T, ROW_TILE, 16)

    def body(q_ref, kv_ref, kpe_ref, fl_ref, pos_ref, invf_ref, gq_ref, gkv_ref, bf_ref,
             qn_ref, kvn_ref, kper_ref, logf_ref, cos_ref, sin_ref):
        ql = q_ref[...]
        qn_ref[...] = (ql * _rms(ql) * gq_ref[...]).astype(BF16)
        kl = kv_ref[...]
        kvn_ref[...] = (kl * _rms(kl) * gkv_ref[...]).astype(BF16)
        ang = pos_ref[...].astype(F32) * invf_ref[...]
        valid = _lane(ang.shape) < ROPE
        cs = jnp.where(valid, jnp.cos(ang), 0.0)
        sn = jnp.where(valid, jnp.sin(ang), 0.0)
        cos_ref[...] = cs
        sin_ref[...] = sn
        kp = jnp.where(valid, kpe_ref[...], 0.0)
        kper_ref[...] = (kp * cs + _rope_rot(kp) * sn).astype(BF16)
        z = fl_ref[...] + bf_ref[...]
        logf_ref[...] = jnp.minimum(z, 0.0) - jnp.log(1.0 + jnp.exp(-jnp.abs(z)))

    def col(width, off):
        return pl.BlockSpec((tm, width), lambda i: (i, off // width))

    def vec(width):
        return pl.BlockSpec((1, width), lambda i: (0, 0))

    def out(width):
        return pl.BlockSpec((tm, width), lambda i: (i, 0))

    return _call(
        body, [proj, proj, proj, proj, pos, invf, gq, gkv, bfor], name="split_prep", grid=(T // tm,),
        in_specs=[col(Q_LORA, lay["q"]), col(KV_LORA, lay["kv"]), col(LANES, lay["kpe"]), col(LANES, lay["fl"]),
                  pl.BlockSpec((tm, 1), lambda i: (i, 0)), vec(LANES), vec(Q_LORA), vec(KV_LORA), vec(LANES)],
        out_specs=[out(Q_LORA), out(KV_LORA), out(LANES), out(LANES), out(LANES), out(LANES)],
        out_shape=[_sds((T, Q_LORA), BF16), _sds((T, KV_LORA), BF16), _sds((T, LANES), BF16),
                   _sds((T, LANES), F32), _sds((T, LANES), F32), _sds((T, LANES), F32)],
        sem=("parallel",), comm=comm)


def _mla_prep(qraw, kvraw, kper, cosT, sinT, comm=None):
    H, T, _ = qraw.shape
    tm = _tile(T, HEAD_ROW_TILE, 16)

    def body(q_ref, kv_ref, kpe_ref, cos_ref, sin_ref, qo_ref, ko_ref, vo_ref):
        q = q_ref[...]
        pe = q[:, NOPE:]
        pe = jnp.where(_lane(pe.shape) < ROPE, pe, 0.0)
        qo_ref[:, :NOPE] = q[:, :NOPE].astype(BF16)
        qo_ref[:, NOPE:] = (pe * cos_ref[...] + _rope_rot(pe) * sin_ref[...]).astype(BF16)
        kv = kv_ref[...]
        ko_ref[:, :NOPE] = kv[:, :NOPE].astype(BF16)
        ko_ref[:, NOPE:] = kpe_ref[...]
        vo_ref[...] = kv[:, NOPE:].astype(BF16)

    head = pl.BlockSpec((None, tm, ATT_DK), lambda h, i: (h, i, 0))
    tok = pl.BlockSpec((tm, LANES), lambda h, i: (i, 0))
    return _call(
        body, [qraw, kvraw, kper, cosT, sinT], name="mla_prep", grid=(H, T // tm),
        in_specs=[head, head, tok, tok, tok],
        out_specs=[head, head, pl.BlockSpec((None, tm, VDIM), lambda h, i: (h, i, 0))],
        out_shape=[_sds((H, T, ATT_DK), BF16), _sds((H, T, ATT_DK), BF16), _sds((H, T, VDIM), BF16)],
        sem=("parallel", "parallel"), comm=comm)


def _fox_cumsum(logf, B, S, inv_scale):
    T = logf.shape[0]

    def body(l_ref, c_ref):
        c_ref[...] = _cumsum_rows(l_ref[...], reverse=False) * inv_scale

    seq = pl.BlockSpec((S, LANES), lambda b: (b, 0))
    return _call(body, [logf], name="fox_cumsum", grid=(B,), in_specs=[seq], out_specs=[seq],
                 out_shape=[_sds((T, LANES), F32)], sem=("parallel",))[0][0]


def _fox_prep(proj, cs, lay, comm=None):
    T = proj.shape[0]
    tm = _tile(T, HEAD_ROW_TILE, 16)

    def body(q_ref, k_ref, v_ref, cs_ref, qo_ref, ko_ref, vo_ref):
        h = pl.program_id(0)
        cv = cs_ref[...]
        lane = _lane(cv.shape)
        ccol = jnp.sum(jnp.where(lane == h, cv, 0.0), axis=1, keepdims=True)
        hi, mid, lo = _split3(ccol)
        one = jnp.where(lane < 6, 1.0, 0.0)
        augq = jnp.where(lane == 0, hi, jnp.where(lane == 1, mid, jnp.where(lane == 2, lo, one)))
        augk = jnp.where(lane < 3, 1.0, jnp.where(lane == 3, -hi, jnp.where(lane == 4, -mid, jnp.where(lane == 5, -lo, 0.0))))
        qo_ref[:, :FOX_DIM] = q_ref[...].astype(BF16)
        qo_ref[:, FOX_DIM:] = augq.astype(BF16)
        ko_ref[:, :FOX_DIM] = k_ref[...].astype(BF16)
        ko_ref[:, FOX_DIM:] = augk.astype(BF16)
        vo_ref[...] = v_ref[...].astype(BF16)

    def col(off):
        return pl.BlockSpec((tm, FOX_DIM), lambda h, i: (i, off // FOX_DIM + h))

    head = pl.BlockSpec((None, tm, ATT_DK), lambda h, i: (h, i, 0))
    return _call(
        body, [proj, proj, proj, cs], name="fox_prep", grid=(HEADS, T // tm),
        in_specs=[col(lay["fq"]), col(lay["fk"]), col(lay["fv"]), pl.BlockSpec((tm, LANES), lambda h, i: (i, 0))],
        out_specs=[head, head, pl.BlockSpec((None, tm, VDIM), lambda h, i: (h, i, 0))],
        out_shape=[_sds((HEADS, T, ATT_DK), BF16), _sds((HEADS, T, ATT_DK), BF16), _sds((HEADS, T, VDIM), BF16)],
        sem=("parallel", "parallel"), comm=comm)


def _visible(tq, tk, unit):
    r = lax.broadcasted_iota(jnp.int32, (tq, tk), 0)
    c = lax.broadcasted_iota(jnp.int32, (tq, tk), 1)
    sh = int(math.log2(unit))
    return lax.shift_right_logical(c, sh) <= lax.shift_right_logical(r, sh)


def _attn_fwd(streams, *, B, S, name, comm=None):
    n = len(streams)
    H, T, DK = streams[0][0].shape
    DV = streams[0][2].shape[2]
    tq = _tile(S, ATT_TILE)
    nq = S // tq
    sub = min(ATT_SUB, tq)
    NT = (((1,), (1,)), ((), ()))

    def body(*refs):
        ins, outs, (m_sc, acc_sc) = refs[:3 * n], refs[3 * n:5 * n], refs[5 * n:]
        i, j = pl.program_id(1), pl.program_id(2)

        @pl.when(j == 0)
        def _():
            m_sc[...] = jnp.full(m_sc.shape, NEG_INF, F32)
            acc_sc[...] = jnp.zeros(acc_sc.shape, F32)

        def step(diagonal):
            work = [(t, r) for r in range(tq // sub) for t in range(n)]

            def scores(t, r):
                q_ref, k_ref, _ = ins[3 * t:3 * t + 3]
                kc = (r + 1) * sub if diagonal else tq
                s = lax.dot_general(q_ref[r * sub:(r + 1) * sub, :], k_ref[0:kc, :], NT, preferred_element_type=F32)
                return s * (streams[t][4] * LOG2_E)

            ahead = [scores(*work[w]) for w in range(min(ATT_AHEAD, len(work)))]
            for w, (t, r) in enumerate(work):
                s = ahead.pop(0)
                if w + ATT_AHEAD < len(work):
                    ahead.append(scores(*work[w + ATT_AHEAD]))
                v_ref = ins[3 * t + 2]
                kc = s.shape[1]
                rows = slice(r * sub, (r + 1) * sub)
                if diagonal:
                    own = jnp.where(_visible(sub, sub, streams[t][3]), s[:, kc - sub:], NEG_INF)
                    s = own if kc == sub else jnp.concatenate([s[:, :kc - sub], own], axis=1)
                m_prev = m_sc[t, rows, :]
                mx = s[:, 0:LANES]
                for g in range(1, kc // LANES):
                    mx = jnp.maximum(mx, s[:, g * LANES:(g + 1) * LANES])
                m_new = jnp.maximum(m_prev, jnp.max(mx, axis=1, keepdims=True))
                alpha = jnp.exp2(m_prev - m_new)
                p = jnp.exp2(s - jnp.tile(m_new, (1, kc // LANES))).astype(BF16)
                v_aug = jnp.concatenate([v_ref[0:kc, :], jnp.ones((kc, LANES), BF16)], axis=1)
                acc_sc[t, rows, :] = jnp.tile(alpha, (1, 2)) * acc_sc[t, rows, :] + jnp.dot(
                    p, v_aug, preferred_element_type=F32)
                m_sc[t, rows, :] = m_new

        @pl.when(j < i)
        def _():
            step(False)

        @pl.when(j == i)
        def _():
            step(True)
            for t in range(n):
                o_ref, lse_ref = outs[2 * t:2 * t + 2]
                l = acc_sc[t, :, DV:]
                o_ref[...] = (acc_sc[t, :, :DV] / l).astype(BF16)
                lse_ref[...] = m_sc[t] + jnp.log2(l)

    def qmap(g, i, j):
        return (g % H, (g // H) * nq + i, 0)

    def kmap(g, i, j):
        return (g % H, (g // H) * nq + jnp.minimum(j, i), 0)

    args = [a for st in streams for a in st[:3]]
    outs, landed = _call(
        body, args, name=name, grid=(B * H, nq, nq),
        in_specs=[pl.BlockSpec((None, tq, DK), qmap), pl.BlockSpec((None, tq, DK), kmap),
                  pl.BlockSpec((None, tq, DV), kmap)] * n,
        out_specs=[pl.BlockSpec((tq, DV), lambda g, i, j: ((g // H) * nq + i, g % H)),
                   pl.BlockSpec((None, tq, LANES), qmap)] * n,
        out_shape=[_sds((T, H * DV), BF16), _sds((H, T, LANES), F32)] * n,
        scratch_shapes=[pltpu.VMEM((n, tq, LANES), F32), pltpu.VMEM((n, tq, DV + LANES), F32)],
        sem=("parallel", "parallel", "arbitrary"), comm=comm)
    return [(outs[2 * t], outs[2 * t + 1]) for t in range(n)], landed


def _attn_bwd(streams, *, B, S, name, comm=None):
    n = len(streams)
    H, T, DK = streams[0][0].shape
    DV = streams[0][2].shape[2]
    tq = _tile(S, ATT_TILE)
    nq = S // tq
    sub = min(ATT_SUB, tq)
    NT = (((1,), (1,)), ((), ()))
    TN = (((0,), (0,)), ((), ()))

    def body(*refs):
        ins, outs = refs[:6 * n], refs[6 * n:]
        j, i = pl.program_id(1), pl.program_id(2)

        @pl.when(jnp.logical_and(j == 0, i == 0))
        def _():
            for t in range(n):
                outs[3 * t][...] = jnp.zeros(outs[3 * t].shape, F32)

        @pl.when(i == 0)
        def _():
            for t in range(n):
                outs[3 * t + 1][...] = jnp.zeros(outs[3 * t + 1].shape, F32)
                outs[3 * t + 2][...] = jnp.zeros(outs[3 * t + 2].shape, F32)

        def step(diagonal):
            work = [(t, r) for r in range(tq // sub) for t in range(n)]

            def kcols(r):
                return (r + 1) * sub if diagonal else tq

            def scores(t, r):
                q_ref, k_ref, v_ref, _, do_ref, _ = ins[6 * t:6 * t + 6]
                rows, kc = slice(r * sub, (r + 1) * sub), kcols(r)
                s = lax.dot_general(q_ref[rows, :], k_ref[0:kc, :], NT, preferred_element_type=F32)
                dp = lax.dot_general(do_ref[rows, :], v_ref[0:kc, :], NT, preferred_element_type=F32)
                return s * (streams[t][7] * LOG2_E), dp

            def probs(t, r, s, dp):
                _, _, _, o_ref, do_ref, lse_ref = ins[6 * t:6 * t + 6]
                rows, kc = slice(r * sub, (r + 1) * sub), kcols(r)
                if diagonal:
                    own = jnp.where(_visible(sub, sub, streams[t][6]), s[:, kc - sub:], NEG_INF)
                    s = own if kc == sub else jnp.concatenate([s[:, :kc - sub], own], axis=1)
                p = jnp.exp2(s - jnp.tile(lse_ref[rows, :], (1, kc // LANES)))
                delta = jnp.sum(do_ref[rows, :].astype(F32) * o_ref[rows, :].astype(F32), axis=1, keepdims=True)
                return p.astype(BF16), (p * (dp - delta) * streams[t][7]).astype(BF16)

            def grads(t, r, p, ds):
                q_ref, k_ref, _, _, do_ref, _ = ins[6 * t:6 * t + 6]
                dq_ref, dk_ref, dv_ref = outs[3 * t:3 * t + 3]
                rows, kc = slice(r * sub, (r + 1) * sub), kcols(r)
                dv_ref[0:kc, :] += lax.dot_general(p, do_ref[rows, :], TN, preferred_element_type=F32)
                dk_ref[0:kc, :] += lax.dot_general(ds, q_ref[rows, :], TN, preferred_element_type=F32)
                qrows = pl.ds(pl.multiple_of(i * tq + r * sub, sub), sub)
                dq_ref[qrows, :] += jnp.dot(ds, k_ref[0:kc, :], preferred_element_type=F32)

            nw = len(work)
            sc = {w: scores(*work[w]) for w in range(min(2, nw))}
            pr = {0: probs(*work[0], *sc.pop(0))}
            for w in range(nw):
                if w + 2 < nw:
                    sc[w + 2] = scores(*work[w + 2])
                if w + 1 < nw:
                    pr[w + 1] = probs(*work[w + 1], *sc.pop(w + 1))
                grads(*work[w], *pr.pop(w))

        @pl.when(i > j)
        def _():
            step(False)

        @pl.when(i == j)
        def _():
            step(True)

    def qmap(g, j, i):
        return (g % H, (g // H) * nq + jnp.maximum(i, j), 0)

    def kmap(g, j, i):
        return (g % H, (g // H) * nq + j, 0)

    def omap(g, j, i):
        return ((g // H) * nq + jnp.maximum(i, j), g % H)

    args = [a for st in streams for a in st[:6]]
    outs, landed = _call(
        body, args, name=name, grid=(B * H, nq, nq),
        in_specs=[pl.BlockSpec((None, tq, DK), qmap), pl.BlockSpec((None, tq, DK), kmap),
                  pl.BlockSpec((None, tq, DV), kmap), pl.BlockSpec((tq, DV), omap), pl.BlockSpec((tq, DV), omap),
                  pl.BlockSpec((None, tq, LANES), qmap)] * n,
        out_specs=[pl.BlockSpec((None, S, DK), lambda g, j, i: (g % H, g // H, 0)),
                   pl.BlockSpec((None, tq, DK), kmap), pl.BlockSpec((None, tq, DV), kmap)] * n,
        out_shape=[_sds((H, T, DK), F32), _sds((H, T, DK), F32), _sds((H, T, DV), F32)] * n,
        sem=("parallel", "arbitrary", "arbitrary"), comm=comm)
    return [tuple(outs[3 * t:3 * t + 3]) for t in range(n)], landed


def _gate_merge(am, af, proj, bgate, lay, D, comm=None):
    T = am.shape[0]
    tm = _tile(T, ROW_TILE, 16)
    tn = _tile(D, 1024)

    def body(am_ref, af_ref, gm_ref, gf_ref, bm_ref, bf_ref, o_ref):
        sm = _sigmoid(gm_ref[...] + bm_ref[...])
        sf = _sigmoid(gf_ref[...] + bf_ref[...])
        o_ref[...] = (sm * am_ref[...] + sf * af_ref[...]).astype(BF16)

    og = lay["g"] // tn
    blk = pl.BlockSpec((tm, tn), lambda i, j: (i, j))
    return _call(
        body, [am, af, proj, proj, bgate, bgate], name="gate_merge", grid=(T // tm, D // tn),
        in_specs=[blk, blk, pl.BlockSpec((tm, tn), lambda i, j: (i, og + j)),
                  pl.BlockSpec((tm, tn), lambda i, j: (i, og + D // tn + j)),
                  pl.BlockSpec((1, tn), lambda i, j: (0, j)), pl.BlockSpec((1, tn), lambda i, j: (0, D // tn + j))],
        out_specs=[blk], out_shape=[_sds((T, D), BF16)], sem=("parallel", "parallel"), comm=comm)


def _mid(x, y1, g_pm, g_ffn):
    T, D = x.shape
    tm = _tile(T, ROW_TILE, 16)

    def body(x_ref, y_ref, gp_ref, gf_ref, x1_ref, h2_ref):
        y = y_ref[...]
        x1 = x_ref[...] + y * _rms(y) * gp_ref[...]
        x1_ref[...] = x1
        h2_ref[...] = (x1 * _rms(x1) * gf_ref[...]).astype(BF16)

    row = pl.BlockSpec((tm, D), lambda i: (i, 0))
    vec = pl.BlockSpec((1, D), lambda i: (0, 0))
    return _call(body, [x, y1, g_pm, g_ffn], name="mid", grid=(T // tm,), in_specs=[row, row, vec, vec],
                 out_specs=[row, row], out_shape=[_sds((T, D), F32), _sds((T, D), BF16)], sem=("parallel",))[0]


def _conv3(u, w_ref, bias):
    row = lax.broadcasted_iota(jnp.int32, u.shape, 0)
    u1 = jnp.where(row >= 1, pltpu.roll(u, 1, 0), 0.0)
    u2 = jnp.where(row >= 2, pltpu.roll(u, 2, 0), 0.0)
    return w_ref[0:1, :] * u2 + w_ref[1:2, :] * u1 + w_ref[2:3, :] * u + bias, u1, u2


def _convffn_fwd(u, cw, cb, B, S, F, comm=None):
    T = u.shape[0]
    tn = _tile(F, 256)
    nf = F // tn

    def body(ug_ref, uv_ref, wg_ref, wv_ref, bg_ref, bv_ref, a_ref):
        g, _, _ = _conv3(ug_ref[...], wg_ref, bg_ref[...])
        val, _, _ = _conv3(uv_ref[...], wv_ref, bv_ref[...])
        a_ref[...] = (_gelu_parts(g)[0] * val).astype(BF16)

    def seq(off):
        return pl.BlockSpec((S, tn), lambda b, j: (b, off + j))

    def par(rows, off):
        return pl.BlockSpec((rows, tn), lambda b, j: (0, off + j))

    return _call(body, [u, u, cw, cw, cb, cb], name="convffn_fwd", grid=(B, nf),
                 in_specs=[seq(0), seq(nf), par(3, 0), par(3, nf), par(1, 0), par(1, nf)],
                 out_specs=[seq(0)], out_shape=[_sds((T, F), BF16)], sem=("parallel", "parallel"), comm=comm)


def _convffn_bwd(u, dact, cw, cb, B, S, F, comm=None):
    T = u.shape[0]
    tn = _tile(F, 256)
    nf = F // tn

    def body(ug_ref, uv_ref, da_ref, wg_ref, wv_ref, bg_ref, bv_ref, dug_ref, duv_ref, dpg_ref, dpv_ref):
        b = pl.program_id(1)
        ug, uv, da = ug_ref[...], uv_ref[...], da_ref[...]
        g, ug1, ug2 = _conv3(ug, wg_ref, bg_ref[...])
        val, uv1, uv2 = _conv3(uv, wv_ref, bv_ref[...])
        gel, dgel = _gelu_parts(g)
        dg = da * val * dgel
        dval = da * gel
        row = lax.broadcasted_iota(jnp.int32, ug.shape, 0)

        def back(d, w_ref):
            d1 = jnp.where(row < S - 1, pltpu.roll(d, S - 1, 0), 0.0)
            d2 = jnp.where(row < S - 2, pltpu.roll(d, S - 2, 0), 0.0)
            return w_ref[2:3, :] * d + w_ref[1:2, :] * d1 + w_ref[0:1, :] * d2

        dug_ref[...] = back(dg, wg_ref).astype(BF16)
        duv_ref[...] = back(dval, wv_ref).astype(BF16)

        def sums(d, u0, u1, u2):
            r8 = lax.broadcasted_iota(jnp.int32, (8, d.shape[1]), 0)
            out = jnp.zeros((8, d.shape[1]), F32)
            for k, t in enumerate((d * u2, d * u1, d * u0, d)):
                out = jnp.where(r8 == k, jnp.sum(t, axis=0, keepdims=True), out)
            return out

        _accumulate(dpg_ref, sums(dg, ug, ug1, ug2), b == 0)
        _accumulate(dpv_ref, sums(dval, uv, uv1, uv2), b == 0)

    def seq(off):
        return pl.BlockSpec((S, tn), lambda j, b: (b, off + j))

    def par(rows, off):
        return pl.BlockSpec((rows, tn), lambda j, b: (0, off + j))

    outs, landed = _call(
        body, [u, u, dact, cw, cw, cb, cb], name="convffn_bwd", grid=(nf, B),
        in_specs=[seq(0), seq(nf), seq(0), par(3, 0), par(3, nf), par(1, 0), par(1, nf)],
        out_specs=[seq(0), seq(0), par(8, 0), par(8, 0)],
        out_shape=[_sds((T, F), BF16), _sds((T, F), BF16), _sds((8, F), F32), _sds((8, F), F32)],
        sem=("parallel", "arbitrary"), comm=comm)
    return outs, landed


def _tail(ff, x1, tgt, g):
    T, D = ff.shape
    tm = _tile(T, ROW_TILE, 16)

    def body(ff_ref, x1_ref, t_ref, g_ref, dy_ref, dff_ref, loss_ref, dg_ref):
        i = pl.program_id(0)
        f = ff_ref[...]
        gv = g_ref[...]
        r = _rms(f)
        n = f * r
        e = (x1_ref[...] + n * gv) - t_ref[...]
        dy = e * (1.0 / D)
        dy_ref[...] = dy
        dn = dy * gv
        dff_ref[...] = (r * (dn - n * jnp.mean(dn * n, axis=-1, keepdims=True))).astype(BF16)
        part = 0.5 * jnp.sum(jnp.mean(e * e, axis=-1, keepdims=True), axis=0, keepdims=True)
        _accumulate(loss_ref, jnp.broadcast_to(part, loss_ref.shape), i == 0)
        _accumulate(dg_ref, jnp.sum(dy * n, axis=0, keepdims=True), i == 0)

    row = pl.BlockSpec((tm, D), lambda i: (i, 0))
    vec = pl.BlockSpec((1, D), lambda i: (0, 0))
    return _call(body, [ff, x1, tgt, g], name="tail", grid=(T // tm,), in_specs=[row, row, row, vec],
                 out_specs=[row, row, pl.BlockSpec((8, LANES), lambda i: (0, 0)), vec],
                 out_shape=[_sds((T, D), F32), _sds((T, D), BF16), _sds((8, LANES), F32), _sds((1, D), F32)],
                 sem=("arbitrary",))[0]


def _mid_bwd(dy, dh2, x1, y1, g_ffn, g_pm, comm=None):
    T, D = dy.shape
    tm = _tile(T, ROW_TILE, 16)

    def body(dy_ref, dh_ref, x1_ref, y1_ref, gf_ref, gp_ref, dx1_ref, dy1_ref, dgf_ref, dgp_ref):
        i = pl.program_id(0)
        dh = dh_ref[...]
        d2, dgf = _rms_bwd(dh, x1_ref[...], gf_ref[...])
        dx1 = dy_ref[...] + d2
        dx1_ref[...] = dx1
        d1, dgp = _rms_bwd(dx1, y1_ref[...], gp_ref[...])
        dy1_ref[...] = d1.astype(BF16)
        _accumulate(dgf_ref, jnp.sum(dgf, axis=0, keepdims=True), i == 0)
        _accumulate(dgp_ref, jnp.sum(dgp, axis=0, keepdims=True), i == 0)

    row = pl.BlockSpec((tm, D), lambda i: (i, 0))
    vec = pl.BlockSpec((1, D), lambda i: (0, 0))
    return _call(body, [dy, dh2, x1, y1, g_ffn, g_pm], name="mid_bwd", grid=(T // tm,),
                 in_specs=[row, row, row, row, vec, vec], out_specs=[row, row, vec, vec],
                 out_shape=[_sds((T, D), F32), _sds((T, D), BF16), _sds((1, D), F32), _sds((1, D), F32)],
                 sem=("arbitrary",), comm=comm)


def _gate_bwd(dm, am, af, proj, bgate, lay, D, comm=None):
    T = dm.shape[0]
    tm = _tile(T, ROW_TILE, 16)
    tn = _tile(D, 512)

    def body(dm_ref, am_ref, af_ref, gm_ref, gf_ref, bm_ref, bf_ref,
             dam_ref, daf_ref, dgm_ref, dgf_ref, dbm_ref, dbf_ref):
        i = pl.program_id(1)
        d = dm_ref[...]
        sm = _sigmoid(gm_ref[...] + bm_ref[...])
        sf = _sigmoid(gf_ref[...] + bf_ref[...])
        dam_ref[...] = (d * sm).astype(BF16)
        daf_ref[...] = (d * sf).astype(BF16)
        dgm = d * am_ref[...] * (sm * (1.0 - sm))
        dgf = d * af_ref[...] * (sf * (1.0 - sf))
        dgm_ref[...] = dgm.astype(BF16)
        dgf_ref[...] = dgf.astype(BF16)
        _accumulate(dbm_ref, jnp.sum(dgm, axis=0, keepdims=True), i == 0)
        _accumulate(dbf_ref, jnp.sum(dgf, axis=0, keepdims=True), i == 0)

    og = lay["g"] // tn
    blk = pl.BlockSpec((tm, tn), lambda j, i: (i, j))
    vec = pl.BlockSpec((1, tn), lambda j, i: (0, j))
    return _call(
        body, [dm, am, af, proj, proj, bgate, bgate], name="gate_bwd", grid=(D // tn, T // tm),
        in_specs=[blk, blk, blk, pl.BlockSpec((tm, tn), lambda j, i: (i, og + j)),
                  pl.BlockSpec((tm, tn), lambda j, i: (i, og + D // tn + j)),
                  vec, pl.BlockSpec((1, tn), lambda j, i: (0, D // tn + j))],
        out_specs=[blk, blk, blk, blk, vec, vec],
        out_shape=[_sds((T, D), BF16)] * 4 + [_sds((1, D), F32)] * 2, sem=("parallel", "arbitrary"), comm=comm)


def _mla_bwd_prep(dq, dk, dv, cosT, sinT, comm=None):
    H, T, _ = dq.shape
    tm = _tile(T, HEAD_ROW_TILE, 16)

    def body(dq_ref, dk_ref, dv_ref, cos_ref, sin_ref, dqr_ref, dkv_ref, dkpe_ref):
        h = pl.program_id(1)
        cs, sn = cos_ref[...], sin_ref[...]
        valid = _lane(cs.shape) < ROPE

        def unrope(d):
            d = jnp.where(valid, d, 0.0)
            return d * cs - _rope_rot(d) * sn

        dqv = dq_ref[...]
        dqr_ref[:, :NOPE] = dqv[:, :NOPE].astype(BF16)
        dqr_ref[:, NOPE:] = unrope(dqv[:, NOPE:]).astype(BF16)
        dkv_ = dk_ref[...]
        dkv_ref[:, :NOPE] = dkv_[:, :NOPE].astype(BF16)
        dkv_ref[:, NOPE:] = dv_ref[...].astype(BF16)
        _accumulate(dkpe_ref, unrope(dkv_[:, NOPE:]), h == 0)

    head = pl.BlockSpec((None, tm, ATT_DK), lambda i, h: (h, i, 0))
    tok = pl.BlockSpec((tm, LANES), lambda i, h: (i, 0))
    return _call(
        body, [dq, dk, dv, cosT, sinT], name="mla_bwd_prep", grid=(T // tm, H),
        in_specs=[head, head, pl.BlockSpec((None, tm, VDIM), lambda i, h: (h, i, 0)), tok, tok],
        out_specs=[head, head, tok],
        out_shape=[_sds((H, T, ATT_DK), BF16), _sds((H, T, ATT_DK), BF16), _sds((T, LANES), F32)],
        sem=("parallel", "arbitrary"), comm=comm)


def _fox_bwd_prep(dq, dk, proj, bfor, lay, B, S, inv_scale):
    H, T, _ = dq.shape

    def body(dq_ref, dk_ref, fl_ref, bf_ref, dfl_ref, dbf_ref, dc_sc):
        b, h = pl.program_id(0), pl.program_id(1)
        lane = _lane(dc_sc.shape)
        col = jnp.sum(jnp.where(lane == 0, dq_ref[...], 0.0) - jnp.where(lane == 3, dk_ref[...], 0.0),
                      axis=1, keepdims=True)

        @pl.when(h == 0)
        def _():
            dc_sc[...] = jnp.zeros(dc_sc.shape, F32)

        dc_sc[...] = jnp.where(lane == h, col, dc_sc[...])

        @pl.when(h == H - 1)
        def _():
            dlogf = _cumsum_rows(dc_sc[...] * inv_scale, reverse=True)
            z = fl_ref[...] + bf_ref[...]
            dz = jnp.where(lane < H, dlogf * (1.0 / (1.0 + jnp.exp(z))), 0.0)
            dfl_ref[...] = dz
            _accumulate(dbf_ref, jnp.sum(dz, axis=0, keepdims=True), b == 0)

    aug = pl.BlockSpec((None, S, LANES), lambda b, h: (h, b, 1))
    seq = pl.BlockSpec((S, LANES), lambda b, h: (b, 0))
    vec = pl.BlockSpec((1, LANES), lambda b, h: (0, 0))
    return _call(
        body, [dq, dk, proj, bfor], name="fox_bwd_prep", grid=(B, H),
        in_specs=[aug, aug, pl.BlockSpec((S, LANES), lambda b, h: (b, lay["fl"] // LANES)), vec],
        out_specs=[seq, vec], out_shape=[_sds((T, LANES), F32), _sds((1, LANES), F32)],
        scratch_shapes=[pltpu.VMEM((S, LANES), F32)], sem=("arbitrary", "arbitrary"))[0]


def _lat_bwd(dqn, dkvn, proj, gq, gkv, lay):
    T = dqn.shape[0]
    tm = _tile(T, ROW_TILE, 16)

    def body(dq_ref, dkv_ref, q_ref, kv_ref, gq_ref, gkv_ref, dql_ref, dkl_ref, dgq_ref, dgkv_ref):
        i = pl.program_id(0)
        dql, dgq = _rms_bwd(dq_ref[...], q_ref[...], gq_ref[...])
        dkl, dgkv = _rms_bwd(dkv_ref[...], kv_ref[...], gkv_ref[...])
        dql_ref[...] = dql.astype(BF16)
        dkl_ref[...] = dkl.astype(BF16)
        _accumulate(dgq_ref, jnp.sum(dgq, axis=0, keepdims=True), i == 0)
        _accumulate(dgkv_ref, jnp.sum(dgkv, axis=0, keepdims=True), i == 0)

    def blk(width, off=0):
        return pl.BlockSpec((tm, width), lambda i: (i, off // width))

    def vec(width):
        return pl.BlockSpec((1, width), lambda i: (0, 0))

    return _call(
        body, [dqn, dkvn, proj, proj, gq, gkv], name="lat_bwd", grid=(T // tm,),
        in_specs=[blk(Q_LORA), blk(KV_LORA), blk(Q_LORA, lay["q"]), blk(KV_LORA, lay["kv"]), vec(Q_LORA), vec(KV_LORA)],
        out_specs=[blk(Q_LORA), blk(KV_LORA), vec(Q_LORA), vec(KV_LORA)],
        out_shape=[_sds((T, Q_LORA), BF16), _sds((T, KV_LORA), BF16), _sds((1, Q_LORA), F32), _sds((1, KV_LORA), F32)],
        sem=("arbitrary",))[0]


def _final_dx(dx1, dh, x, g, comm=None):
    T, D = x.shape
    tm = _tile(T, ROW_TILE, 16)

    def body(dx1_ref, dh_ref, x_ref, g_ref, dx_ref, dg_ref):
        i = pl.program_id(0)
        d, dg = _rms_bwd(dh_ref[...], x_ref[...], g_ref[...])
        dx_ref[...] = dx1_ref[...] + d
        _accumulate(dg_ref, jnp.sum(dg, axis=0, keepdims=True), i == 0)

    row = pl.BlockSpec((tm, D), lambda i: (i, 0))
    vec = pl.BlockSpec((1, D), lambda i: (0, 0))
    return _call(body, [dx1, dh, x, g], name="final_dx", grid=(T // tm,), in_specs=[row, row, row, vec],
                 out_specs=[row, vec], out_shape=[_sds((T, D), F32), _sds((1, D), F32)], sem=("arbitrary",), comm=comm)


def _chip_sum(pieces, paired, qc, name):
    G, R, C = pieces.shape
    tr = _tile(R, 256, 16)

    def body(qc_ref, g_ref, p_ref, keep_ref, send_ref):
        s = pl.program_id(1)
        tot = g_ref[...] + p_ref[...]

        @pl.when(s == 0)
        def _():
            keep_ref[...] = tot

        @pl.when(s > 0)
        def _():
            send_ref[...] = tot.astype(send_ref.dtype)

    grid_spec = pltpu.PrefetchScalarGridSpec(
        num_scalar_prefetch=1, grid=(R // tr, N_CHIP),
        in_specs=[pl.BlockSpec((None, tr, C), lambda i, s, qc: (2 * (qc[0] ^ s) + qc[1], i, 0)),
                  pl.BlockSpec((None, tr, C), lambda i, s, qc: (qc[0] ^ s, i, 0))],
        out_specs=[pl.BlockSpec((tr, C), lambda i, s, qc: (i, 0)),
                   pl.BlockSpec((None, tr, C), lambda i, s, qc: (jnp.maximum(s - 1, 0), i, 0))])
    send_dtype = BF16 if R >= 16 else pieces.dtype
    return pl.pallas_call(
        body, name=name, grid_spec=grid_spec,
        out_shape=[_sds((R, C), F32), _sds((3, R, C), send_dtype)],
        compiler_params=pltpu.CompilerParams(dimension_semantics=("arbitrary", "arbitrary"),
                                             vmem_limit_bytes=VMEM_LIMIT_BYTES),
    )(qc, pieces, paired)


def _adamw_math(w, g, m, v):
    m = ADAM_B1 * m + (1.0 - ADAM_B1) * g
    v = ADAM_B2 * v + (1.0 - ADAM_B2) * (g * g)
    m_hat = m / (1.0 - ADAM_B1 ** ADAM_STEP)
    v_hat = v / (1.0 - ADAM_B2 ** ADAM_STEP)
    delta = -ADAM_LR * (m_hat / (jnp.sqrt(v_hat) + ADAM_EPS) + ADAM_WD * w)
    return delta, m, v


def _sum_adamw(keep, pieces, w, m, v, name):
    R, C = w.shape
    P = pieces.shape[0]
    tr = _tile(R, 256, 16)

    def body(*refs):
        if keep is None:
            p_ref, w_ref, m_ref, v_ref, g_ref, d_ref, mo_ref, vo_ref = refs
            g = p_ref[0].astype(F32)
            rest = range(1, P)
        else:
            k_ref, p_ref, w_ref, m_ref, v_ref, g_ref, d_ref, mo_ref, vo_ref = refs
            g = k_ref[...]
            rest = range(P)
        for q in rest:
            g = g + p_ref[q].astype(F32)
        g_ref[...] = g
        d_ref[...], mo_ref[...], vo_ref[...] = _adamw_math(w_ref[...], g, m_ref[...], v_ref[...])

    blk = pl.BlockSpec((tr, C), lambda i: (i, 0))
    pblk = pl.BlockSpec((P, tr, C), lambda i: (0, i, 0))
    args = [pieces, w, m, v] if keep is None else [keep, pieces, w, m, v]
    specs = [pblk, blk, blk, blk] if keep is None else [blk, pblk, blk, blk, blk]
    return _call(body, args, name=name, grid=(R // tr,), in_specs=specs, out_specs=[blk] * 4,
                 out_shape=[_sds((R, C), F32)] * 4, sem=("parallel",))[0]


def _layout(D):
    lay = {"q": 0, "kv": Q_LORA, "kpe": Q_LORA + KV_LORA}
    lay["fq"] = lay["kpe"] + LANES
    lay["fk"] = lay["fq"] + HEADS * FOX_DIM
    lay["fv"] = lay["fk"] + HEADS * FOX_DIM
    lay["fl"] = lay["fv"] + HEADS * FOX_DIM
    lay["g"] = lay["fl"] + LANES
    lay["end"] = lay["g"] + 2 * D
    return lay


def kernel(x, positions, pre_mix_norm, w_in, q_a_norm, w_uq, kv_a_norm, w_ukv, b_forget, b_gate, w_branch_mla, w_branch_fox, w_out, post_mix_norm, pre_ffn_norm, w_up, conv_w, conv_b, w_down, post_ffn_norm, loss_target, m_pre_mix_norm, m_w_in, m_q_a_norm, m_w_uq, m_kv_a_norm, m_w_ukv, m_b_forget, m_b_gate, m_w_branch_mla, m_w_branch_fox, m_w_out, m_post_mix_norm, m_pre_ffn_norm, m_w_up, m_conv_w, m_conv_b, m_w_down, m_post_ffn_norm, v_pre_mix_norm, v_w_in, v_q_a_norm, v_w_uq, v_kv_a_norm, v_w_ukv, v_b_forget, v_b_gate, v_w_branch_mla, v_w_branch_fox, v_w_out, v_post_mix_norm, v_pre_ffn_norm, v_w_up, v_conv_w, v_conv_b, v_w_down, v_post_ffn_norm):
    B, S, D = x.shape
    T = B * S
    F = conv_b.shape[0] // 2
    lay = _layout(D)
    n_in = w_in.shape[1]
    d_in = N_DEV * n_in
    seg_a = Q_LORA + KV_LORA + ROPE
    seg_b = 3 * HEADS * FOX_DIM + HEADS
    mla_scale = (NOPE + ROPE) ** -0.5
    fox_scale = FOX_DIM ** -0.5
    ax, ay, ac = (lax.axis_index(a) for a in MESH_AXES)
    qc = jnp.stack([2 * ax + ay, ac]).astype(jnp.int32)

    def row(vec, width=None):
        vec = vec.reshape(1, -1)
        if width is not None and vec.shape[1] < width:
            vec = jnp.pad(vec, ((0, 0), (0, width - vec.shape[1])))
        return vec

    x2 = x.reshape(T, D)
    win_s = _cast_bf16(w_in, "cast_w_in")
    h, (win_g,) = _prenorm(x2, row(pre_mix_norm), comm=_Comm([_GatherRelayPlan([win_s], mid_frac=0.3)]))
    small_s = [_cast_bf16(w, "cast_" + n) for w, n in
               [(w_uq, "w_uq"), (w_ukv, "w_ukv"), (w_branch_mla, "w_branch_mla"), (w_branch_fox, "w_branch_fox"), (w_out, "w_out")]]
    wup_s = _cast_bf16(w_up, "cast_w_up")
    wdown_s = _cast_bf16(w_down, "cast_w_down")

    def shard_cols(lo, hi):
        out = []
        for g in range(lo // n_in, (hi - 1) // n_in + 1):
            out.append(win_g[g][:, max(lo, g * n_in) - g * n_in:min(hi, (g + 1) * n_in) - g * n_in])
        return out

    w_perm = jnp.concatenate(
        shard_cols(0, seg_a) + [jnp.zeros((D, LANES - ROPE), BF16)] + shard_cols(seg_a, seg_a + seg_b)
        + [jnp.zeros((D, LANES - HEADS), BF16)] + shard_cols(seg_a + seg_b, d_in), axis=1)

    tgt = loss_target.reshape(T, D)
    pos = positions.reshape(T, 1)
    inv_freq = 1.0 / (ROPE_THETA ** (jnp.arange(0, ROPE, 2, dtype=F32) / ROPE))
    invf = row(jnp.concatenate([inv_freq, inv_freq]), LANES)
    g_pre, g_q, g_kv = row(pre_mix_norm), row(q_a_norm), row(kv_a_norm)
    g_pm, g_ffn, g_pf = row(post_mix_norm), row(pre_ffn_norm), row(post_ffn_norm)
    bfor = row(b_forget, LANES)
    bgate = row(b_gate)
    cb_full = row(conv_b)

    def own_plan(blocks):
        return _Comm([_GatherOwnPlan(blocks)])

    def pass_plan(gathered):
        return _Comm([_GatherPassPlan(gathered)])

    def pair_plan(gs):
        return _Comm([_PairScatterPlan(gs)])

    def chip_plan(gs):
        return _Comm([_ChipScatterPlan(gs)])

    half_d = D // 2
    proj, landed = _matmul(h, w_perm, mode="nn", name="mm_proj", comm=_Comm(
        [_GatherOwnPlan(small_s[:2] + [conv_w]), _GatherOwnPlan([wup_s], rows=(0, half_d))]))
    early_g, wup_part = landed[:-1], landed[-1:]
    (qn, kvn, kper, logf, cosT, sinT), (wuq_g, wukv_g, cw_g) = _split_prep(
        proj, pos, invf, g_q, g_kv, bfor, lay, comm=pass_plan(early_g))
    wuq_pad = jnp.pad(wuq_g, ((0, 0), (0, 0), (0, ATT_DK - NOPE - ROPE)))
    cw_full = jnp.transpose(cw_g, (1, 0, 2)).reshape(3, 2 * F)

    qraw = _matmul(qn, wuq_pad, mode="nn", name="mm_q", out_blocks=ATT_DK, tm=T)
    kvraw = _matmul(kvn, wukv_g, mode="nn", name="mm_kv", out_blocks=NOPE + VDIM, tm=T)
    (q_mla, k_mla, v_mla), branch_half = _mla_prep(qraw, kvraw, kper, cosT, sinT, comm=own_plan(small_s[2:4]))
    cs = _fox_cumsum(logf, B, S, 1.0 / fox_scale)
    (q_fox, k_fox, v_fox), wout_half = _fox_prep(proj, cs, lay, comm=own_plan(small_s[4:5]))
    ((o_mla, lse_mla), (o_fox, lse_fox)), landed = _attn_fwd(
        [(q_mla, k_mla, v_mla, MLA_UNIT, mla_scale), (q_fox, k_fox, v_fox, 1, fox_scale)], B=B, S=S,
        name="attn_fwd", comm=_Comm([_GatherOwnPlan([wup_s], rows=(half_d, D), into=wup_part),
                                     _GatherPassPlan(branch_half + wout_half)]))
    wup_half, (wbm_g, wbf_g, wout_g) = landed[:1], landed[1:]
    wbm = jnp.transpose(wbm_g, (1, 0, 2)).reshape(HEADS * VDIM, D)
    wbf = jnp.transpose(wbf_g, (1, 0, 2)).reshape(HEADS * FOX_DIM, D)
    wout = wout_g.reshape(D, D)
    a_m = _matmul(o_mla, wbm, mode="nn", name="mm_branch_mla", tm=2 * MM_TILE)
    a_f = _matmul(o_fox, wbf, mode="nn", name="mm_branch_fox", tm=2 * MM_TILE)
    (merged,), (wup_g,) = _gate_merge(a_m, a_f, proj, bgate, lay, D, comm=pass_plan(wup_half))
    n_up = wup_g.shape[2]
    y1 = _matmul(merged, wout, mode="nn", name="mm_out")
    x1, h2 = _mid(x2, y1, g_pm, g_ffn)
    u, wdown_half = _matmul(h2, wup_g, mode="nn", name="mm_up", tn=n_up, comm=own_plan([wdown_s]))
    (act,), (wdown_g,) = _convffn_fwd(u, cw_full, cb_full, B, S, F, comm=pass_plan(wdown_half))
    wdown = wdown_g.reshape(F, D)
    ff = _matmul(act, wdown, mode="nn", name="mm_down", tk=F // 2)
    dy, dff, loss_part, dg_pf = _tail(ff, x1, tgt, g_pf)

    dact = _matmul(dff, wdown, mode="nt", name="mm_dact", tn=F // 4)
    dw_down = _matmul(act, dff, mode="tn", name="mm_dw_down", tm=F // 4, tn=512).reshape(N_DEV, F // N_DEV, D)
    (du_g, du_v, dcp_g, dcp_v), (pa_down,) = _convffn_bwd(u, dact, cw_full, cb_full, B, S, F, comm=pair_plan([dw_down]))
    keep_down, sb_down = _chip_sum(dw_down, pa_down, qc, "chipsum_w_down")
    dh2, (rb_down,) = _matmul_halves((du_g, du_v), wup_g, mode="nt", name="mm_dh2", tm=MM_TILE, comm=chip_plan([sb_down]))
    dw_up = _matmul_halves(h2, (du_g, du_v), mode="tn", name="mm_dw_up", tm=MM_TILE, tn=n_up, tk=T // 2)
    (dx1, dy1, dg_ffn, dg_pm), _ = _mid_bwd(dy, dh2, x1, y1, g_ffn, g_pm)
    dmerged = _matmul(dy1, wout, mode="nt", name="mm_dmerged")
    dw_out = _matmul(merged, dy1, mode="tn", name="mm_dw_out").reshape(N_DEV, D // N_DEV, D)
    (da_m, da_f, dgl_m, dgl_f, dbg_m, dbg_f), (pa_up,) = _gate_bwd(
        dmerged, a_m, a_f, proj, bgate, lay, D, comm=pair_plan([dw_up]))
    keep_up, sb_up = _chip_sum(dw_up, pa_up, qc, "chipsum_w_up")
    dw_bm = _matmul(o_mla, da_m, mode="tn", name="mm_dw_branch_mla", out_blocks=D // N_DEV)
    dw_bf = _matmul(o_fox, da_f, mode="tn", name="mm_dw_branch_fox", out_blocks=D // N_DEV)
    mix = [dw_out, dw_bm, dw_bf]
    do_mla, pa_mix = _matmul(da_m, wbm, mode="nt", name="mm_do_mla", out_dtype=BF16, comm=pair_plan(mix))
    do_fox = _matmul(da_f, wbf, mode="nt", name="mm_do_fox", out_dtype=BF16)
    mix_sums = [_chip_sum(g, p, qc, "chipsum_" + n) for g, p, n in zip(mix, pa_mix, ["w_out", "w_branch_mla", "w_branch_fox"])]
    ((dq_m, dk_m, dv_m), (dq_f, dk_f, dv_f)), (rb_up,) = _attn_bwd(
        [(q_mla, k_mla, v_mla, o_mla, do_mla, lse_mla, MLA_UNIT, mla_scale),
         (q_fox, k_fox, v_fox, o_fox, do_fox, lse_fox, 1, fox_scale)], B=B, S=S, name="attn_bwd",
        comm=chip_plan([sb_up]))
    (dqraw, dkvraw, dkpe), _ = _mla_bwd_prep(dq_m, dk_m, dv_m, cosT, sinT)
    dqn = _matmul(dqraw, wuq_pad, mode="nt", name="mm_dqn", tm=T)
    dw_uq = _matmul(qn, dqraw, mode="tn", name="mm_dw_uq", out_blocks=ATT_DK)[:, :, :NOPE + ROPE]
    dkvn = _matmul(dkvraw, wukv_g, mode="nt", name="mm_dkvn", tm=T)
    dw_ukv = _matmul(kvn, dkvraw, mode="tn", name="mm_dw_ukv", out_blocks=NOPE + VDIM)
    dqlat, dkvlat, dg_q, dg_kv = _lat_bwd(dqn, dkvn, proj, g_q, g_kv, lay)
    dfl, dbfor = _fox_bwd_prep(dq_f, dk_f, proj, bfor, lay, B, S, 1.0 / fox_scale)
    dproj = _concat_cols([dqlat, dkvlat, dkpe, dq_f, dk_f, dv_f, dfl, dgl_m, dgl_f], "concat_dproj")
    dw_perm, rb_mix = _matmul(h, dproj, mode="tn", name="mm_dw_in", comm=chip_plan([s[1] for s in mix_sums]))
    segs = [(0, seg_a, 0), (seg_a, seg_a + seg_b, lay["fq"] - seg_a), (seg_a + seg_b, d_in, lay["g"] - seg_a - seg_b)]

    def piece(g):
        lo, hi = g * n_in, (g + 1) * n_in
        parts = [dw_perm[:, max(lo, s0) + sh:min(hi, s1) + sh] for s0, s1, sh in segs if max(lo, s0) < min(hi, s1)]
        return parts[0] if len(parts) == 1 else jnp.concatenate(parts, axis=1)

    dw_in = jnp.stack([piece(g) for g in range(N_DEV)])
    dcw = jnp.transpose(jnp.concatenate([dcp_g[0:3], dcp_v[0:3]], axis=1).reshape(3, N_DEV, (2 * F) // N_DEV), (1, 0, 2))
    late = [dw_in, dw_uq, dw_ukv, dcw]
    pa_late = _exchange_alone(pair_plan(late), "pair_late")
    late_sums = [_chip_sum(g, p, qc, "chipsum_" + n) for g, p, n in zip(late, pa_late, ["w_in", "w_uq", "w_ukv", "conv_w"])]
    dh, rb_late = _matmul(dproj, w_perm, mode="nt", name="mm_dh", tn=2048, tk=2048, comm=chip_plan([s[1] for s in late_sums]))
    (grad_x, dg_pre), _ = _final_dx(dx1, dh, x2, g_pre)

    big_out = {}

    def finish(n, keep, pieces, w, m, v):
        big_out[n] = _sum_adamw(keep, pieces, w, m, v, "adamw_" + n)

    finish("w_down", keep_down, rb_down, w_down, m_w_down, v_w_down)
    finish("w_up", keep_up, rb_up, w_up, m_w_up, v_w_up)
    finish("w_out", mix_sums[0][0], rb_mix[0], w_out, m_w_out, v_w_out)
    finish("w_branch_mla", mix_sums[1][0], rb_mix[1], w_branch_mla, m_w_branch_mla, v_w_branch_mla)
    finish("w_branch_fox", mix_sums[2][0], rb_mix[2], w_branch_fox, m_w_branch_fox, v_w_branch_fox)
    finish("w_in", late_sums[0][0], rb_late[0], w_in, m_w_in, v_w_in)
    finish("w_uq", late_sums[1][0], rb_late[1], w_uq, m_w_uq, v_w_uq)
    finish("w_ukv", late_sums[2][0], rb_late[2], w_ukv, m_w_ukv, v_w_ukv)
    finish("conv_w", late_sums[3][0], rb_late[3], conv_w, m_conv_w, v_conv_w)

    widths = [D, Q_LORA, KV_LORA, LANES, 2 * D, D, D, 2 * F, D]
    small_names = ["pre_mix_norm", "q_a_norm", "kv_a_norm", "b_forget", "b_gate", "post_mix_norm", "pre_ffn_norm",
                   "conv_b", "post_ffn_norm"]
    true_w = [D, Q_LORA, KV_LORA, HEADS, 2 * D, D, D, 2 * F, D]
    dcb = jnp.concatenate([dcp_g[3:4], dcp_v[3:4]], axis=1)
    part = jnp.concatenate([dg_pre, dg_q, dg_kv, dbfor, dbg_m, dbg_f, dg_pm, dg_ffn, dcb, dg_pf], axis=1)

    def pack(vals):
        return jnp.concatenate([row(a, wd) for a, wd in zip(vals, widths)], axis=1)

    sw = pack([pre_mix_norm, q_a_norm, kv_a_norm, b_forget, b_gate, post_mix_norm, pre_ffn_norm, conv_b, post_ffn_norm])
    sm = pack([m_pre_mix_norm, m_q_a_norm, m_kv_a_norm, m_b_forget, m_b_gate, m_post_mix_norm, m_pre_ffn_norm,
               m_conv_b, m_post_ffn_norm])
    sv = pack([v_pre_mix_norm, v_q_a_norm, v_kv_a_norm, v_b_forget, v_b_gate, v_post_mix_norm, v_pre_ffn_norm,
               v_conv_b, v_post_ffn_norm])
    (parts_all,) = _exchange_alone(_Comm([_DirectGatherPlan([part])]), "gather_small")
    sg, sd, smo, svo = _sum_adamw(None, parts_all, sw, sm, sv, "adamw_small")
    small_out = {}
    off = 0
    for n, wd, tw in zip(small_names, widths, true_w):
        small_out[n] = tuple(a[0, off:off + tw] for a in (sg, sd, smo, svo))
        off += wd

    loss = lax.psum(loss_part[0, 0], MESH_AXES)
    order = ["pre_mix_norm", "w_in", "q_a_norm", "w_uq", "kv_a_norm", "w_ukv", "b_forget", "b_gate", "w_branch_mla",
             "w_branch_fox", "w_out", "post_mix_norm", "pre_ffn_norm", "w_up", "conv_w", "conv_b", "w_down",
             "post_ffn_norm"]
    res = {**big_out, **small_out}
    outs = [loss, grad_x.reshape(B, S, D)]
    for kind in range(4):
        outs += [res[n][kind] for n in order]
    return tuple(outs)
```

```python
import math

import jax
import jax.numpy as jnp
from jax import lax
from jax.experimental import pallas as pl
from jax.experimental.pallas import tpu as pltpu

F32 = jnp.float32
BF16 = jnp.bfloat16

N_DEV = 8
N_CHIP = 4
HEADS = 8
NOPE = 128
ROPE = 64
HALF_ROPE = ROPE // 2
VDIM = 128
Q_LORA = 512
KV_LORA = 256
FOX_DIM = 128
ATT_DK = 256
MLA_UNIT = 64
ROPE_THETA = 10000.0
EPS = 1e-6
NEG_INF = -1e30
LANES = 128
LOG2_E = 1.4426950408889634

ADAM_LR = 0.001
ADAM_B1 = 0.9
ADAM_B2 = 0.999
ADAM_EPS = 1e-08
ADAM_WD = 0.01
ADAM_STEP = 10

VMEM_LIMIT_BYTES = 56 * 1024 * 1024
ROW_TILE = 256
HEAD_ROW_TILE = 1024
ATT_TILE = 1024
ATT_SUB = 256
ATT_AHEAD = 3
MM_TILE = 1024

MESH_AXES = ("x", "y", "c")
ANY = pl.BlockSpec(memory_space=pl.ANY)


def _tile(n, pref, align=LANES):
    if n <= pref:
        return n
    t = (pref // align) * align
    while t >= align:
        if n % t == 0:
            return t
        t -= align
    return n


def _sds(shape, dtype):
    return jax.ShapeDtypeStruct(shape, dtype)


def _coords():
    x, y, c = (lax.axis_index(ax) for ax in MESH_AXES)
    return x, y, c


def _chip_rel(x, y, r):
    return (1 - x if r & 2 else x), (1 - y if r & 1 else y)


def _rcopy(src, dst, sems, w, k, dev):
    return pltpu.make_async_remote_copy(src_ref=src, dst_ref=dst, send_sem=sems[0].at[w, k], recv_sem=sems[1].at[w, k],
                                        device_id=dev, device_id_type=pl.DeviceIdType.MESH)


class _GatherRelayPlan:
    def __init__(self, blocks, mid_frac=0.5):
        self.ins = list(blocks)
        self.out_shapes = [_sds((N_DEV,) + b.shape, b.dtype) for b in blocks]
        n = len(blocks)
        self.scratch = [pltpu.SemaphoreType.DMA((n, 7)), pltpu.SemaphoreType.DMA((n, 7)), pltpu.SemaphoreType.DMA((n,))]
        self.mid_frac = mid_frac

    @staticmethod
    def _places():
        x, y, c = _coords()
        xn, yn = 4 * (1 - x) + 2 * y, 4 * x + 2 * (1 - y)
        relay_src = 4 * (x + c * (1 - 2 * x)) + 2 * (y + (1 - c) * (1 - 2 * y)) + c
        relay_to = (x + (1 - c) * (1 - 2 * x), y + c * (1 - 2 * y), c)
        return x, y, c, xn, yn, relay_src, relay_to, 4 * (1 - x) + 2 * (1 - y)

    def first(self, ins, outs, sems):
        x, y, c, _, _, _, _, _ = self._places()
        me = 4 * x + 2 * y + c
        for w in range(len(ins)):
            pltpu.make_async_copy(ins[w], outs[w].at[me], sems[2].at[w]).start()
            _rcopy(ins[w], outs[w].at[me], sems, w, 0, (x, y, 1 - c)).start()
            _rcopy(ins[w], outs[w].at[me], sems, w, 1, (1 - x, y, c)).start()
            _rcopy(ins[w], outs[w].at[me], sems, w, 2, (x, 1 - y, c)).start()

    def mid(self, ins, outs, sems):
        x, y, c, xn, yn, relay_src, relay_to, _ = self._places()
        sib = (x, y, 1 - c)
        for w in range(len(ins)):
            bx, by = outs[w].at[xn + c], outs[w].at[yn + c]
            _rcopy(ins[w], bx, sems, w, 1, (1 - x, y, c)).wait_recv()
            _rcopy(ins[w], by, sems, w, 2, (x, 1 - y, c)).wait_recv()
            _rcopy(outs[w].at[relay_src], outs[w].at[relay_src], sems, w, 3, relay_to).start()
            _rcopy(bx, bx, sems, w, 4, sib).start()
            _rcopy(by, by, sems, w, 5, sib).start()

    def last(self, ins, outs, sems):
        x, y, c, xn, yn, _, relay_to, dg = self._places()
        me = 4 * x + 2 * y + c
        sib = (x, y, 1 - c)
        for w in range(len(ins)):
            bd = outs[w].at[dg + c]
            _rcopy(ins[w], bd, sems, w, 3, relay_to).wait_recv()
            _rcopy(bd, bd, sems, w, 6, sib).start()
            for k, blk in ((0, 4 * x + 2 * y), (4, xn), (5, yn), (6, dg)):
                _rcopy(ins[w], outs[w].at[blk + 1 - c], sems, w, k, sib).wait_recv()
            for k in range(7):
                _rcopy(ins[w], outs[w].at[me], sems, w, k, sib).wait_send()
            pltpu.make_async_copy(ins[w], outs[w].at[me], sems[2].at[w]).wait()


class _GatherOwnPlan:
    mid = None

    def __init__(self, blocks, rows=None, into=None):
        self.n = len(blocks)
        self.rows = rows
        self.ins = list(blocks) + list(into or [])
        self.out_shapes = [_sds((N_DEV,) + b.shape, b.dtype) for b in blocks]
        self.aliases = [(self.n + i, i) for i in range(len(into or []))]
        n = self.n
        self.scratch = [pltpu.SemaphoreType.DMA((n, 4)), pltpu.SemaphoreType.DMA((n, 4)), pltpu.SemaphoreType.DMA((n,))]

    def _cut(self, ref):
        return ref if self.rows is None else ref.at[pl.ds(self.rows[0], self.rows[1] - self.rows[0])]

    def first(self, ins, outs, sems):
        x, y, c = _coords()
        me = 4 * x + 2 * y + c
        for w in range(self.n):
            src, dst = self._cut(ins[w]), self._cut(outs[w].at[me])
            pltpu.make_async_copy(src, dst, sems[2].at[w]).start()
            _rcopy(src, dst, sems, w, 0, (x, y, 1 - c)).start()
            for r in (1, 2, 3):
                px, py = _chip_rel(x, y, r)
                _rcopy(src, dst, sems, w, r, (px, py, c)).start()

    def last(self, ins, outs, sems):
        x, y, c = _coords()
        me = 4 * x + 2 * y + c
        for w in range(self.n):
            src = self._cut(ins[w])
            cp = _rcopy(src, self._cut(outs[w].at[4 * x + 2 * y + 1 - c]), sems, w, 0, (x, y, 1 - c))
            cp.wait_recv()
            cp.wait_send()
            for r in (1, 2, 3):
                px, py = _chip_rel(x, y, r)
                cp = _rcopy(src, self._cut(outs[w].at[4 * px + 2 * py + c]), sems, w, r, (px, py, c))
                cp.wait_recv()
                cp.wait_send()
            pltpu.make_async_copy(src, self._cut(outs[w].at[me]), sems[2].at[w]).wait()


class _GatherPassPlan:
    mid = None

    def __init__(self, gathered):
        self.ins = list(gathered)
        self.out_shapes = [_sds(g.shape, g.dtype) for g in gathered]
        self.aliases = [(i, i) for i in range(len(gathered))]
        n = len(gathered)
        self.scratch = [pltpu.SemaphoreType.DMA((n, 3)), pltpu.SemaphoreType.DMA((n, 3))]

    def first(self, ins, outs, sems):
        x, y, c = _coords()
        for w in range(len(ins)):
            for r in (1, 2, 3):
                px, py = _chip_rel(x, y, r)
                blk = 4 * px + 2 * py + c
                _rcopy(ins[w].at[blk], outs[w].at[blk], sems, w, r - 1, (x, y, 1 - c)).start()

    def last(self, ins, outs, sems):
        x, y, c = _coords()
        for w in range(len(ins)):
            for r in (1, 2, 3):
                px, py = _chip_rel(x, y, r)
                blk = 4 * px + 2 * py + 1 - c
                cp = _rcopy(ins[w].at[blk], outs[w].at[blk], sems, w, r - 1, (x, y, 1 - c))
                cp.wait_recv()
                cp.wait_send()


class _DirectGatherPlan:
    mid = None

    def __init__(self, blocks):
        self.ins = list(blocks)
        self.out_shapes = [_sds((N_DEV,) + b.shape, b.dtype) for b in blocks]
        n = len(blocks)
        self.scratch = [pltpu.SemaphoreType.DMA((n, 7)), pltpu.SemaphoreType.DMA((n, 7)), pltpu.SemaphoreType.DMA((n,))]

    @staticmethod
    def _peer(x, y, c, r):
        return (1 - x if r & 4 else x), (1 - y if r & 2 else y), (1 - c if r & 1 else c)

    def first(self, ins, outs, sems):
        x, y, c = _coords()
        me = 4 * x + 2 * y + c
        for w in range(len(ins)):
            pltpu.make_async_copy(ins[w], outs[w].at[me], sems[2].at[w]).start()
            for r in range(1, N_DEV):
                _rcopy(ins[w], outs[w].at[me], sems, w, r - 1, self._peer(x, y, c, r)).start()

    def last(self, ins, outs, sems):
        x, y, c = _coords()
        me = 4 * x + 2 * y + c
        for w in range(len(ins)):
            for r in range(1, N_DEV):
                px, py, pc = self._peer(x, y, c, r)
                cp = _rcopy(ins[w], outs[w].at[4 * px + 2 * py + pc], sems, w, r - 1, (px, py, pc))
                cp.wait_recv()
                cp.wait_send()
            pltpu.make_async_copy(ins[w], outs[w].at[me], sems[2].at[w]).wait()


class _PairScatterPlan:
    mid = None

    def __init__(self, pieces, rows=None, into=None):
        self.n = len(pieces)
        self.rows = rows
        self.ins = list(pieces) + list(into or [])
        self.out_shapes = [_sds((N_CHIP,) + p.shape[1:], p.dtype) for p in pieces]
        self.aliases = [(self.n + i, i) for i in range(len(into or []))]
        self.scratch = [pltpu.SemaphoreType.DMA((self.n, N_CHIP)), pltpu.SemaphoreType.DMA((self.n, N_CHIP))]

    def _copies(self, ins, outs, sems):
        x, y, c = _coords()
        cps = []
        for w in range(self.n):
            for q in range(N_CHIP):
                src, dst = ins[w].at[2 * q + 1 - c], outs[w].at[q]
                if self.rows is not None:
                    cut = pl.ds(self.rows[0], self.rows[1] - self.rows[0])
                    src, dst = src.at[cut], dst.at[cut]
                cps.append(_rcopy(src, dst, sems, w, q, (x, y, 1 - c)))
        return cps

    def first(self, ins, outs, sems):
        for cp in self._copies(ins, outs, sems):
            cp.start()

    def last(self, ins, outs, sems):
        for cp in self._copies(ins, outs, sems):
            cp.wait_recv()
            cp.wait_send()


class _ChipScatterPlan:
    mid = None

    def __init__(self, sums, rows=None, into=None):
        self.n = len(sums)
        self.rows = rows
        self.ins = list(sums) + list(into or [])
        self.out_shapes = [_sds(s.shape, s.dtype) for s in sums]
        self.aliases = [(self.n + i, i) for i in range(len(into or []))]
        self.scratch = [pltpu.SemaphoreType.DMA((self.n, 3)), pltpu.SemaphoreType.DMA((self.n, 3))]

    def _copies(self, ins, outs, sems):
        x, y, c = _coords()
        cps = []
        for w in range(self.n):
            for r in (1, 2, 3):
                px, py = _chip_rel(x, y, r)
                src, dst = ins[w].at[r - 1], outs[w].at[r - 1]
                if self.rows is not None:
                    cut = pl.ds(self.rows[0], self.rows[1] - self.rows[0])
                    src, dst = src.at[cut], dst.at[cut]
                cps.append(_rcopy(src, dst, sems, w, r - 1, (px, py, c)))
        return cps

    def first(self, ins, outs, sems):
        for cp in self._copies(ins, outs, sems):
            cp.start()

    def last(self, ins, outs, sems):
        for cp in self._copies(ins, outs, sems):
            cp.wait_recv()
            cp.wait_send()


class _Comm:
    def __init__(self, plans):
        self.plans = list(plans)
        self.ins = [a for p in self.plans for a in p.ins]
        self.out_shapes = [s for p in self.plans for s in p.out_shapes]
        self.scratch = [s for p in self.plans for s in p.scratch]
        self.aliases = []
        i = o = 0
        for p in self.plans:
            self.aliases += [(i + a, o + b) for a, b in getattr(p, "aliases", [])]
            i, o = i + len(p.ins), o + len(p.out_shapes)

    def _parts(self, ins, outs, sems):
        i = o = s = 0
        for p in self.plans:
            yield p, ins[i:i + len(p.ins)], outs[o:o + len(p.out_shapes)], sems[s:s + len(p.scratch)]
            i, o, s = i + len(p.ins), o + len(p.out_shapes), s + len(p.scratch)

    def begin(self, step, nsteps, ins, outs, sems):
        @pl.when(step == 0)
        def _():
            for p, pi, po, ps in self._parts(ins, outs, sems):
                p.first(pi, po, ps)

        for p, pi, po, ps in self._parts(ins, outs, sems):
            if p.mid is not None:
                @pl.when(step == min(nsteps - 1, int(p.mid_frac * nsteps)))
                def _(p=p, pi=pi, po=po, ps=ps):
                    p.mid(pi, po, ps)

    def end(self, step, nsteps, ins, outs, sems):
        @pl.when(step == nsteps - 1)
        def _():
            for p, pi, po, ps in self._parts(ins, outs, sems):
                p.last(pi, po, ps)


def _call(body, args, *, name, grid, in_specs, out_specs, out_shape, scratch_shapes=(), sem=None, comm=None):
    in_specs, out_specs, out_shape, scratch_shapes = list(in_specs), list(out_specs), list(out_shape), list(scratch_shapes)
    if comm is None:
        res = pl.pallas_call(
            body, name=name, grid=grid, in_specs=in_specs, out_specs=out_specs, out_shape=out_shape,
            scratch_shapes=scratch_shapes,
            compiler_params=pltpu.CompilerParams(dimension_semantics=sem, vmem_limit_bytes=VMEM_LIMIT_BYTES),
        )(*args)
        return list(res), []
    n_in, n_out, n_sc = len(in_specs), len(out_specs), len(scratch_shapes)
    n_ci, n_co = len(comm.ins), len(comm.out_shapes)
    nsteps = math.prod(grid)

    def hosted(*refs):
        ins, cins = refs[:n_in], refs[n_in:n_in + n_ci]
        o0 = n_in + n_ci
        outs, couts = refs[o0:o0 + n_out], refs[o0 + n_out:o0 + n_out + n_co]
        s0 = o0 + n_out + n_co
        scr, csems = refs[s0:s0 + n_sc], refs[s0 + n_sc:]
        step = jnp.int32(0)
        for d in range(len(grid)):
            step = step * grid[d] + pl.program_id(d)
        comm.begin(step, nsteps, cins, couts, csems)
        body(*ins, *outs, *scr)
        comm.end(step, nsteps, cins, couts, csems)

    res = pl.pallas_call(
        hosted, name=name, grid=grid, in_specs=in_specs + [ANY] * n_ci, out_specs=out_specs + [ANY] * n_co,
        out_shape=out_shape + comm.out_shapes, scratch_shapes=scratch_shapes + comm.scratch,
        input_output_aliases={n_in + a: n_out + b for a, b in comm.aliases},
        compiler_params=pltpu.CompilerParams(dimension_semantics=("arbitrary",) * len(grid),
                                             vmem_limit_bytes=VMEM_LIMIT_BYTES, has_side_effects=True),
    )(*args, *comm.ins)
    return list(res[:n_out]), list(res[n_out:])


def _exchange_alone(comm, name):
    def body():
        pass

    return _call(body, [], name=name, grid=(), in_specs=[], out_specs=[], out_shape=[], comm=comm)[1]


def _matmul(a, b, *, mode, name, out_dtype=F32, out_blocks=None, tm=None, tn=None, tk=None, comm=None):
    tm = MM_TILE if tm is None else tm
    tn = MM_TILE if tn is None else tn
    a_blk = a.ndim == 3
    b_blk = b.ndim == 3
    if mode == "nn":
        M, K = a.shape
        N = b.shape[0] * b.shape[2] if b_blk else b.shape[1]
        dims = (((1,), (0,)), ((), ()))
    elif mode == "nt":
        M = a.shape[1] if a_blk else a.shape[0]
        K = a.shape[0] * a.shape[2] if a_blk else a.shape[1]
        N = b.shape[1] if b_blk else b.shape[0]
        dims = (((1,), (1,)), ((), ()))
    else:
        K, M = a.shape
        N = b.shape[0] * b.shape[2] if b_blk else b.shape[1]
        dims = (((0,), (0,)), ((), ()))

    tm = _tile(M, tm)
    tn = _tile(N, tn)
    if mode == "nt" and (a_blk or b_blk):
        tk = a.shape[2] if a_blk else b.shape[2]
    else:
        tk = _tile(K, K if tk is None else tk)
    if mode != "nt" and b_blk:
        tn = _tile(b.shape[2], tn)
    if out_blocks is not None:
        tn = _tile(out_blocks, tn)
    nk = K // tk
    grid = (M // tm, N // tn, nk)

    if mode == "nn":
        a_spec = pl.BlockSpec((tm, tk), lambda i, j, k: (i, k))
        if b_blk:
            rb = b.shape[2] // tn
            b_spec = pl.BlockSpec((None, tk, tn), lambda i, j, k: (j // rb, k, j % rb))
        else:
            b_spec = pl.BlockSpec((tk, tn), lambda i, j, k: (k, j))
    elif mode == "nt":
        if a_blk:
            a_spec = pl.BlockSpec((None, tm, tk), lambda i, j, k: (k, i, 0))
        else:
            a_spec = pl.BlockSpec((tm, tk), lambda i, j, k: (i, k))
        if b_blk:
            b_spec = pl.BlockSpec((None, tn, tk), lambda i, j, k: (k, j, 0))
        else:
            b_spec = pl.BlockSpec((tn, tk), lambda i, j, k: (j, k))
    else:
        a_spec = pl.BlockSpec((tk, tm), lambda i, j, k: (k, i))
        if b_blk:
            rb = b.shape[2] // tn
            b_spec = pl.BlockSpec((None, tk, tn), lambda i, j, k: (j // rb, k, j % rb))
        else:
            b_spec = pl.BlockSpec((tk, tn), lambda i, j, k: (k, j))

    if out_blocks is None:
        o_spec = pl.BlockSpec((tm, tn), lambda i, j, k: (i, j))
        o_shape = _sds((M, N), out_dtype)
    else:
        ro = out_blocks // tn
        o_spec = pl.BlockSpec((None, tm, tn), lambda i, j, k: (j // ro, i, j % ro))
        o_shape = _sds((N // out_blocks, M, out_blocks), out_dtype)

    direct = nk == 1 or out_dtype == F32

    def body(a_ref, b_ref, o_ref, *scratch):
        if nk == 1:
            o_ref[...] = lax.dot_general(a_ref[...], b_ref[...], dims, preferred_element_type=F32).astype(o_ref.dtype)
            return
        acc_ref = o_ref if direct else scratch[0]
        k = pl.program_id(2)

        @pl.when(k == 0)
        def _():
            acc_ref[...] = jnp.zeros(acc_ref.shape, F32)

        acc_ref[...] += lax.dot_general(a_ref[...], b_ref[...], dims, preferred_element_type=F32)
        if not direct:
            @pl.when(k == nk - 1)
            def _():
                o_ref[...] = acc_ref[...].astype(o_ref.dtype)

    scratch = [] if direct else [pltpu.VMEM((tm, tn), F32)]
    outs, landed = _call(body, [a, b], name=name, grid=grid, in_specs=[a_spec, b_spec], out_specs=[o_spec],
                         out_shape=[o_shape], scratch_shapes=scratch, sem=("parallel", "parallel", "arbitrary"), comm=comm)
    return outs[0] if comm is None else (outs[0], landed)


def _matmul_halves(a, b, *, mode, name, tm, tn=None, tk=None, comm=None):
    if mode == "nt":
        lo, hi = a
        M, kh = lo.shape
        G, N, kb = b.shape
        half = kh // kb
        tm, tn = _tile(M, tm), _tile(N, N if tn is None else tn)
        dims = (((1,), (1,)), ((), ()))

        def body(lo_ref, hi_ref, b_ref, o_ref):
            k = pl.program_id(2)

            @pl.when(k == 0)
            def _():
                o_ref[...] = jnp.zeros(o_ref.shape, F32)

            @pl.when(k < half)
            def _():
                o_ref[...] += lax.dot_general(lo_ref[...], b_ref[...], dims, preferred_element_type=F32)

            @pl.when(k >= half)
            def _():
                o_ref[...] += lax.dot_general(hi_ref[...], b_ref[...], dims, preferred_element_type=F32)

        outs, landed = _call(
            body, [lo, hi, b], name=name, grid=(M // tm, N // tn, G),
            in_specs=[pl.BlockSpec((tm, kb), lambda i, j, k: (i, jnp.minimum(k, half - 1))),
                      pl.BlockSpec((tm, kb), lambda i, j, k: (i, jnp.maximum(k - half, 0))),
                      pl.BlockSpec((None, tn, kb), lambda i, j, k: (k, j, 0))],
            out_specs=[pl.BlockSpec((tm, tn), lambda i, j, k: (i, j))], out_shape=[_sds((M, N), F32)],
            sem=("parallel", "parallel", "arbitrary"), comm=comm)
    else:
        lo, hi = b
        K, nh = lo.shape
        M = a.shape[1]
        n = tn
        half = nh // n
        tm, tk = _tile(M, tm), _tile(K, K if tk is None else tk)
        nk = K // tk
        dims = (((0,), (0,)), ((), ()))

        def body(a_ref, lo_ref, hi_ref, o_ref):
            j, k = pl.program_id(1), pl.program_id(2)

            @pl.when(k == 0)
            def _():
                o_ref[...] = jnp.zeros(o_ref.shape, F32)

            @pl.when(j < half)
            def _():
                o_ref[...] += lax.dot_general(a_ref[...], lo_ref[...], dims, preferred_element_type=F32)

            @pl.when(j >= half)
            def _():
                o_ref[...] += lax.dot_general(a_ref[...], hi_ref[...], dims, preferred_element_type=F32)

        outs, landed = _call(
            body, [a, lo, hi], name=name, grid=(M // tm, 2 * half, nk),
            in_specs=[pl.BlockSpec((tk, tm), lambda i, j, k: (k, i)),
                      pl.BlockSpec((tk, n), lambda i, j, k: (jnp.where(j < half, k, nk - 1), jnp.minimum(j, half - 1))),
                      pl.BlockSpec((tk, n), lambda i, j, k: (jnp.where(j >= half, k, 0), jnp.maximum(j - half, 0)))],
            out_specs=[pl.BlockSpec((None, tm, n), lambda i, j, k: (j, i, 0))],
            out_shape=[_sds((2 * half, M, n), F32)], sem=("parallel", "parallel", "arbitrary"), comm=comm)
    return outs[0] if comm is None else (outs[0], landed)


def _rms(x):
    return lax.rsqrt(jnp.mean(x * x, axis=-1, keepdims=True) + EPS)


def _rms_bwd(dy, x, g):
    r = _rms(x)
    n = x * r
    dn = dy * g
    dx = r * (dn - n * jnp.mean(dn * n, axis=-1, keepdims=True))
    return dx, dy * n


def _sigmoid(x):
    return 1.0 / (1.0 + jnp.exp(-x))


def _rope_rot(t):
    return pltpu.roll(t, HALF_ROPE, 1) - pltpu.roll(t, LANES - HALF_ROPE, 1)


def _lane(shape):
    return lax.broadcasted_iota(jnp.int32, shape, 1)


def _split3(x):
    hi = x.astype(BF16).astype(F32)
    r1 = x - hi
    mid = r1.astype(BF16).astype(F32)
    lo = (r1 - mid).astype(BF16).astype(F32)
    return hi, mid, lo


def _cumsum_rows(x, reverse):
    S = x.shape[0]
    bs = min(256, S)
    nb = S // bs
    r = lax.broadcasted_iota(jnp.int32, (bs, bs), 0)
    c = lax.broadcasted_iota(jnp.int32, (bs, bs), 1)
    tri = jnp.where((c >= r) if reverse else (c <= r), 1.0, 0.0).astype(BF16)
    edge = lax.broadcasted_iota(jnp.int32, (bs, x.shape[1]), 0) == (0 if reverse else bs - 1)
    carry = jnp.zeros((1, x.shape[1]), F32)
    outs = [None] * nb
    for bi in (range(nb - 1, -1, -1) if reverse else range(nb)):
        xb = x[bi * bs:(bi + 1) * bs, :]
        acc = carry
        for term in _split3(xb):
            acc = acc + jnp.dot(tri, term.astype(BF16), preferred_element_type=F32)
        outs[bi] = acc
        carry = jnp.sum(jnp.where(edge, acc, 0.0), axis=0, keepdims=True)
    return jnp.concatenate(outs, axis=0) if nb > 1 else outs[0]


def _gelu_parts(x):
    c0 = math.sqrt(2.0 / math.pi)
    inner = c0 * (x + 0.044715 * (x * x * x))
    t = jnp.tanh(inner)
    g = 0.5 * x * (1.0 + t)
    dg = 0.5 * (1.0 + t) + 0.5 * x * (1.0 - t * t) * (c0 * (1.0 + 3.0 * 0.044715 * (x * x)))
    return g, dg


def _accumulate(ref, value, first):
    @pl.when(first)
    def _():
        ref[...] = value

    @pl.when(jnp.logical_not(first))
    def _():
        ref[...] += value


def _cast_bf16(w, name):
    R, C = w.shape
    tr = _tile(R, 512, 16)

    def body(w_ref, o_ref):
        o_ref[...] = w_ref[...].astype(BF16)

    blk = pl.BlockSpec((tr, C), lambda i: (i, 0))
    return _call(body, [w], name=name, grid=(R // tr,), in_specs=[blk], out_specs=[blk],
                 out_shape=[_sds((R, C), BF16)], sem=("parallel",))[0][0]


def _concat_cols(parts, name):
    T = parts[0].shape[-2] if parts[0].ndim == 3 else parts[0].shape[0]
    widths = [p.shape[0] * LANES if p.ndim == 3 else p.shape[1] for p in parts]
    tm = _tile(T, ROW_TILE, 16)

    def body(*refs):
        o_ref = refs[-1]
        off = 0
        for p_ref, p, w in zip(refs[:-1], parts, widths):
            if p.ndim == 3:
                for hd in range(p.shape[0]):
                    o_ref[:, off + hd * LANES:off + (hd + 1) * LANES] = p_ref[hd].astype(BF16)
            else:
                o_ref[:, off:off + w] = p_ref[...].astype(BF16)
            off += w

    def spec(p, w):
        if p.ndim == 3:
            return pl.BlockSpec((p.shape[0], tm, LANES), lambda i: (0, i, 0))
        return pl.BlockSpec((tm, w), lambda i: (i, 0))

    return _call(body, parts, name=name, grid=(T // tm,),
                 in_specs=[spec(p, w) for p, w in zip(parts, widths)],
                 out_specs=[pl.BlockSpec((tm, sum(widths)), lambda i: (i, 0))],
                 out_shape=[_sds((T, sum(widths)), BF16)], sem=("parallel",))[0][0]


def _prenorm(x, g, comm=None):
    T, D = x.shape
    tm = _tile(T, ROW_TILE, 16)

    def body(x_ref, g_ref, h_ref):
        xv = x_ref[...]
        h_ref[...] = (xv * _rms(xv) * g_ref[...]).astype(BF16)

    row = pl.BlockSpec((tm, D), lambda i: (i, 0))
    (h,), landed = _call(body, [x, g], name="prenorm", grid=(T // tm,),
                         in_specs=[row, pl.BlockSpec((1, D), lambda i: (0, 0))], out_specs=[row],
                         out_shape=[_sds((T, D), BF16)], sem=("parallel",), comm=comm)
    return h, landed


def _split_prep(proj, pos, invf, gq, gkv, bfor, lay, comm=None):
    T = proj.shape[0]
    tm = _tile(T, ROW_TILE, 16)

    def body(q_ref, kv_ref, kpe_ref, fl_ref, pos_ref, invf_ref, gq_ref, gkv_ref, bf_ref,
             qn_ref, kvn_ref, kper_ref, logf_ref, cos_ref, sin_ref):
        ql = q_ref[...]
        qn_ref[...] = (ql * _rms(ql) * gq_ref[...]).astype(BF16)
        kl = kv_ref[...]
        kvn_ref[...] = (kl * _rms(kl) * gkv_ref[...]).astype(BF16)
        ang = pos_ref[...].astype(F32) * invf_ref[...]
        valid = _lane(ang.shape) < ROPE
        cs = jnp.where(valid, jnp.cos(ang), 0.0)
        sn = jnp.where(valid, jnp.sin(ang), 0.0)
        cos_ref[...] = cs
        sin_ref[...] = sn
        kp = jnp.where(valid, kpe_ref[...], 0.0)
        kper_ref[...] = (kp * cs + _rope_rot(kp) * sn).astype(BF16)
        z = fl_ref[...] + bf_ref[...]
        logf_ref[...] = jnp.minimum(z, 0.0) - jnp.log(1.0 + jnp.exp(-jnp.abs(z)))

    def col(width, off):
        return pl.BlockSpec((tm, width), lambda i: (i, off // width))

    def vec(width):
        return pl.BlockSpec((1, width), lambda i: (0, 0))

    def out(width):
        return pl.BlockSpec((tm, width), lambda i: (i, 0))

    return _call(
        body, [proj, proj, proj, proj, pos, invf, gq, gkv, bfor], name="split_prep", grid=(T // tm,),
        in_specs=[col(Q_LORA, lay["q"]), col(KV_LORA, lay["kv"]), col(LANES, lay["kpe"]), col(LANES, lay["fl"]),
                  pl.BlockSpec((tm, 1), lambda i: (i, 0)), vec(LANES), vec(Q_LORA), vec(KV_LORA), vec(LANES)],
        out_specs=[out(Q_LORA), out(KV_LORA), out(LANES), out(LANES), out(LANES), out(LANES)],
        out_shape=[_sds((T, Q_LORA), BF16), _sds((T, KV_LORA), BF16), _sds((T, LANES), BF16),
                   _sds((T, LANES), F32), _sds((T, LANES), F32), _sds((T, LANES), F32)],
        sem=("parallel",), comm=comm)


def _mla_prep(qraw, kvraw, kper, cosT, sinT, comm=None):
    H, T, _ = qraw.shape
    tm = _tile(T, HEAD_ROW_TILE, 16)

    def body(q_ref, kv_ref, kpe_ref, cos_ref, sin_ref, qo_ref, ko_ref, vo_ref):
        q = q_ref[...]
        pe = q[:, NOPE:]
        pe = jnp.where(_lane(pe.shape) < ROPE, pe, 0.0)
        qo_ref[:, :NOPE] = q[:, :NOPE].astype(BF16)
        qo_ref[:, NOPE:] = (pe * cos_ref[...] + _rope_rot(pe) * sin_ref[...]).astype(BF16)
        kv = kv_ref[...]
        ko_ref[:, :NOPE] = kv[:, :NOPE].astype(BF16)
        ko_ref[:, NOPE:] = kpe_ref[...]
        vo_ref[...] = kv[:, NOPE:].astype(BF16)

    head = pl.BlockSpec((None, tm, ATT_DK), lambda h, i: (h, i, 0))
    tok = pl.BlockSpec((tm, LANES), lambda h, i: (i, 0))
    return _call(
        body, [qraw, kvraw, kper, cosT, sinT], name="mla_prep", grid=(H, T // tm),
        in_specs=[head, head, tok, tok, tok],
        out_specs=[head, head, pl.BlockSpec((None, tm, VDIM), lambda h, i: (h, i, 0))],
        out_shape=[_sds((H, T, ATT_DK), BF16), _sds((H, T, ATT_DK), BF16), _sds((H, T, VDIM), BF16)],
        sem=("parallel", "parallel"), comm=comm)


def _fox_cumsum(logf, B, S, inv_scale):
    T = logf.shape[0]

    def body(l_ref, c_ref):
        c_ref[...] = _cumsum_rows(l_ref[...], reverse=False) * inv_scale

    seq = pl.BlockSpec((S, LANES), lambda b: (b, 0))
    return _call(body, [logf], name="fox_cumsum", grid=(B,), in_specs=[seq], out_specs=[seq],
                 out_shape=[_sds((T, LANES), F32)], sem=("parallel",))[0][0]


def _fox_prep(proj, cs, lay, comm=None):
    T = proj.shape[0]
    tm = _tile(T, HEAD_ROW_TILE, 16)

    def body(q_ref, k_ref, v_ref, cs_ref, qo_ref, ko_ref, vo_ref):
        h = pl.program_id(0)
        cv = cs_ref[...]
        lane = _lane(cv.shape)
        ccol = jnp.sum(jnp.where(lane == h, cv, 0.0), axis=1, keepdims=True)
        hi, mid, lo = _split3(ccol)
        one = jnp.where(lane < 6, 1.0, 0.0)
        augq = jnp.where(lane == 0, hi, jnp.where(lane == 1, mid, jnp.where(lane == 2, lo, one)))
        augk = jnp.where(lane < 3, 1.0, jnp.where(lane == 3, -hi, jnp.where(lane == 4, -mid, jnp.where(lane == 5, -lo, 0.0))))
        qo_ref[:, :FOX_DIM] = q_ref[...].astype(BF16)
        qo_ref[:, FOX_DIM:] = augq.astype(BF16)
        ko_ref[:, :FOX_DIM] = k_ref[...].astype(BF16)
        ko_ref[:, FOX_DIM:] = augk.astype(BF16)
        vo_ref[...] = v_ref[...].astype(BF16)

    def col(off):
        return pl.BlockSpec((tm, FOX_DIM), lambda h, i: (i, off // FOX_DIM + h))

    head = pl.BlockSpec((None, tm, ATT_DK), lambda h, i: (h, i, 0))
    return _call(
        body, [proj, proj, proj, cs], name="fox_prep", grid=(HEADS, T // tm),
        in_specs=[col(lay["fq"]), col(lay["fk"]), col(lay["fv"]), pl.BlockSpec((tm, LANES), lambda h, i: (i, 0))],
        out_specs=[head, head, pl.BlockSpec((None, tm, VDIM), lambda h, i: (h, i, 0))],
        out_shape=[_sds((HEADS, T, ATT_DK), BF16), _sds((HEADS, T, ATT_DK), BF16), _sds((HEADS, T, VDIM), BF16)],
        sem=("parallel", "parallel"), comm=comm)


def _visible(tq, tk, unit):
    r = lax.broadcasted_iota(jnp.int32, (tq, tk), 0)
    c = lax.broadcasted_iota(jnp.int32, (tq, tk), 1)
    sh = int(math.log2(unit))
    return lax.shift_right_logical(c, sh) <= lax.shift_right_logical(r, sh)


def _attn_fwd(streams, *, B, S, name, comm=None):
    n = len(streams)
    H, T, DK = streams[0][0].shape
    DV = streams[0][2].shape[2]
    tq = _tile(S, ATT_TILE)
    nq = S // tq
    sub = min(ATT_SUB, tq)
    NT = (((1,), (1,)), ((), ()))

    def body(*refs):
        ins, outs, (m_sc, acc_sc) = refs[:3 * n], refs[3 * n:5 * n], refs[5 * n:]
        i, j = pl.program_id(1), pl.program_id(2)

        @pl.when(j == 0)
        def _():
            m_sc[...] = jnp.full(m_sc.shape, NEG_INF, F32)
            acc_sc[...] = jnp.zeros(acc_sc.shape, F32)

        def step(diagonal):
            work = [(t, r) for r in range(tq // sub) for t in range(n)]

            def scores(t, r):
                q_ref, k_ref, _ = ins[3 * t:3 * t + 3]
                kc = (r + 1) * sub if diagonal else tq
                s = lax.dot_general(q_ref[r * sub:(r + 1) * sub, :], k_ref[0:kc, :], NT, preferred_element_type=F32)
                return s * (streams[t][4] * LOG2_E)

            ahead = [scores(*work[w]) for w in range(min(ATT_AHEAD, len(work)))]
            for w, (t, r) in enumerate(work):
                s = ahead.pop(0)
                if w + ATT_AHEAD < len(work):
                    ahead.append(scores(*work[w + ATT_AHEAD]))
                v_ref = ins[3 * t + 2]
                kc = s.shape[1]
                rows = slice(r * sub, (r + 1) * sub)
                if diagonal:
                    own = jnp.where(_visible(sub, sub, streams[t][3]), s[:, kc - sub:], NEG_INF)
                    s = own if kc == sub else jnp.concatenate([s[:, :kc - sub], own], axis=1)
                m_prev = m_sc[t, rows, :]
                mx = s[:, 0:LANES]
                for g in range(1, kc // LANES):
                    mx = jnp.maximum(mx, s[:, g * LANES:(g + 1) * LANES])
                m_new = jnp.maximum(m_prev, jnp.max(mx, axis=1, keepdims=True))
                alpha = jnp.exp2(m_prev - m_new)
                p = jnp.exp2(s - jnp.tile(m_new, (1, kc // LANES))).astype(BF16)
                v_aug = jnp.concatenate([v_ref[0:kc, :], jnp.ones((kc, LANES), BF16)], axis=1)
                acc_sc[t, rows, :] = jnp.tile(alpha, (1, 2)) * acc_sc[t, rows, :] + jnp.dot(
                    p, v_aug, preferred_element_type=F32)
                m_sc[t, rows, :] = m_new

        @pl.when(j < i)
        def _():
            step(False)

        @pl.when(j == i)
        def _():
            step(True)
            for t in range(n):
                o_ref, lse_ref = outs[2 * t:2 * t + 2]
                l = acc_sc[t, :, DV:]
                o_ref[...] = (acc_sc[t, :, :DV] / l).astype(BF16)
                lse_ref[...] = m_sc[t] + jnp.log2(l)

    def qmap(g, i, j):
        return (g % H, (g // H) * nq + i, 0)

    def kmap(g, i, j):
        return (g % H, (g // H) * nq + jnp.minimum(j, i), 0)

    args = [a for st in streams for a in st[:3]]
    outs, landed = _call(
        body, args, name=name, grid=(B * H, nq, nq),
        in_specs=[pl.BlockSpec((None, tq, DK), qmap), pl.BlockSpec((None, tq, DK), kmap),
                  pl.BlockSpec((None, tq, DV), kmap)] * n,
        out_specs=[pl.BlockSpec((tq, DV), lambda g, i, j: ((g // H) * nq + i, g % H)),
                   pl.BlockSpec((None, tq, LANES), qmap)] * n,
        out_shape=[_sds((T, H * DV), BF16), _sds((H, T, LANES), F32)] * n,
        scratch_shapes=[pltpu.VMEM((n, tq, LANES), F32), pltpu.VMEM((n, tq, DV + LANES), F32)],
        sem=("parallel", "parallel", "arbitrary"), comm=comm)
    return [(outs[2 * t], outs[2 * t + 1]) for t in range(n)], landed


def _attn_bwd(streams, *, B, S, name, comm=None):
    n = len(streams)
    H, T, DK = streams[0][0].shape
    DV = streams[0][2].shape[2]
    tq = _tile(S, ATT_TILE)
    nq = S // tq
    sub = min(ATT_SUB, tq)
    NT = (((1,), (1,)), ((), ()))
    TN = (((0,), (0,)), ((), ()))

    def body(*refs):
        ins, outs = refs[:6 * n], refs[6 * n:]
        j, i = pl.program_id(1), pl.program_id(2)

        @pl.when(jnp.logical_and(j == 0, i == 0))
        def _():
            for t in range(n):
                outs[3 * t][...] = jnp.zeros(outs[3 * t].shape, F32)

        @pl.when(i == 0)
        def _():
            for t in range(n):
                outs[3 * t + 1][...] = jnp.zeros(outs[3 * t + 1].shape, F32)
                outs[3 * t + 2][...] = jnp.zeros(outs[3 * t + 2].shape, F32)

        def step(diagonal):
            work = [(t, r) for r in range(tq // sub) for t in range(n)]

            def kcols(r):
                return (r + 1) * sub if diagonal else tq

            def scores(t, r):
                q_ref, k_ref, v_ref, _, do_ref, _ = ins[6 * t:6 * t + 6]
                rows, kc = slice(r * sub, (r + 1) * sub), kcols(r)
                s = lax.dot_general(q_ref[rows, :], k_ref[0:kc, :], NT, preferred_element_type=F32)
                dp = lax.dot_general(do_ref[rows, :], v_ref[0:kc, :], NT, preferred_element_type=F32)
                return s * (streams[t][7] * LOG2_E), dp

            def probs(t, r, s, dp):
                _, _, _, o_ref, do_ref, lse_ref = ins[6 * t:6 * t + 6]
                rows, kc = slice(r * sub, (r + 1) * sub), kcols(r)
                if diagonal:
                    own = jnp.where(_visible(sub, sub, streams[t][6]), s[:, kc - sub:], NEG_INF)
                    s = own if kc == sub else jnp.concatenate([s[:, :kc - sub], own], axis=1)
                p = jnp.exp2(s - jnp.tile(lse_ref[rows, :], (1, kc // LANES)))
                delta = jnp.sum(do_ref[rows, :].astype(F32) * o_ref[rows, :].astype(F32), axis=1, keepdims=True)
                return p.astype(BF16), (p * (dp - delta) * streams[t][7]).astype(BF16)

            def grads(t, r, p, ds):
                q_ref, k_ref, _, _, do_ref, _ = ins[6 * t:6 * t + 6]
                dq_ref, dk_ref, dv_ref = outs[3 * t:3 * t + 3]
                rows, kc = slice(r * sub, (r + 1) * sub), kcols(r)
                dv_ref[0:kc, :] += lax.dot_general(p, do_ref[rows, :], TN, preferred_element_type=F32)
                dk_ref[0:kc, :] += lax.dot_general(ds, q_ref[rows, :], TN, preferred_element_type=F32)
                qrows = pl.ds(pl.multiple_of(i * tq + r * sub, sub), sub)
                dq_ref[qrows, :] += jnp.dot(ds, k_ref[0:kc, :], preferred_element_type=F32)

            nw = len(work)
            sc = {w: scores(*work[w]) for w in range(min(2, nw))}
            pr = {0: probs(*work[0], *sc.pop(0))}
            for w in range(nw):
                if w + 2 < nw:
                    sc[w + 2] = scores(*work[w + 2])
                if w + 1 < nw:
                    pr[w + 1] = probs(*work[w + 1], *sc.pop(w + 1))
                grads(*work[w], *pr.pop(w))

        @pl.when(i > j)
        def _():
            step(False)

        @pl.when(i == j)
        def _():
            step(True)

    def qmap(g, j, i):
        return (g % H, (g // H) * nq + jnp.maximum(i, j), 0)

    def kmap(g, j, i):
        return (g % H, (g // H) * nq + j, 0)

    def omap(g, j, i):
        return ((g // H) * nq + jnp.maximum(i, j), g % H)

    args = [a for st in streams for a in st[:6]]
    outs, landed = _call(
        body, args, name=name, grid=(B * H, nq, nq),
        in_specs=[pl.BlockSpec((None, tq, DK), qmap), pl.BlockSpec((None, tq, DK), kmap),
                  pl.BlockSpec((None, tq, DV), kmap), pl.BlockSpec((tq, DV), omap), pl.BlockSpec((tq, DV), omap),
                  pl.BlockSpec((None, tq, LANES), qmap)] * n,
        out_specs=[pl.BlockSpec((None, S, DK), lambda g, j, i: (g % H, g // H, 0)),
                   pl.BlockSpec((None, tq, DK), kmap), pl.BlockSpec((None, tq, DV), kmap)] * n,
        out_shape=[_sds((H, T, DK), F32), _sds((H, T, DK), F32), _sds((H, T, DV), F32)] * n,
        sem=("parallel", "arbitrary", "arbitrary"), comm=comm)
    return [tuple(outs[3 * t:3 * t + 3]) for t in range(n)], landed


def _gate_merge(am, af, proj, bgate, lay, D, comm=None):
    T = am.shape[0]
    tm = _tile(T, ROW_TILE, 16)
    tn = _tile(D, 1024)

    def body(am_ref, af_ref, gm_ref, gf_ref, bm_ref, bf_ref, o_ref):
        sm = _sigmoid(gm_ref[...] + bm_ref[...])
        sf = _sigmoid(gf_ref[...] + bf_ref[...])
        o_ref[...] = (sm * am_ref[...] + sf * af_ref[...]).astype(BF16)

    og = lay["g"] // tn
    blk = pl.BlockSpec((tm, tn), lambda i, j: (i, j))
    return _call(
        body, [am, af, proj, proj, bgate, bgate], name="gate_merge", grid=(T // tm, D // tn),
        in_specs=[blk, blk, pl.BlockSpec((tm, tn), lambda i, j: (i, og + j)),
                  pl.BlockSpec((tm, tn), lambda i, j: (i, og + D // tn + j)),
                  pl.BlockSpec((1, tn), lambda i, j: (0, j)), pl.BlockSpec((1, tn), lambda i, j: (0, D // tn + j))],
        out_specs=[blk], out_shape=[_sds((T, D), BF16)], sem=("parallel", "parallel"), comm=comm)


def _mid(x, y1, g_pm, g_ffn):
    T, D = x.shape
    tm = _tile(T, ROW_TILE, 16)

    def body(x_ref, y_ref, gp_ref, gf_ref, x1_ref, h2_ref):
        y = y_ref[...]
        x1 = x_ref[...] + y * _rms(y) * gp_ref[...]
        x1_ref[...] = x1
        h2_ref[...] = (x1 * _rms(x1) * gf_ref[...]).astype(BF16)

    row = pl.BlockSpec((tm, D), lambda i: (i, 0))
    vec = pl.BlockSpec((1, D), lambda i: (0, 0))
    return _call(body, [x, y1, g_pm, g_ffn], name="mid", grid=(T // tm,), in_specs=[row, row, vec, vec],
                 out_specs=[row, row], out_shape=[_sds((T, D), F32), _sds((T, D), BF16)], sem=("parallel",))[0]


def _conv3(u, w_ref, bias):
    row = lax.broadcasted_iota(jnp.int32, u.shape, 0)
    u1 = jnp.where(row >= 1, pltpu.roll(u, 1, 0), 0.0)
    u2 = jnp.where(row >= 2, pltpu.roll(u, 2, 0), 0.0)
    return w_ref[0:1, :] * u2 + w_ref[1:2, :] * u1 + w_ref[2:3, :] * u + bias, u1, u2


def _convffn_fwd(u, cw, cb, B, S, F, comm=None):
    T = u.shape[0]
    tn = _tile(F, 256)
    nf = F // tn

    def body(ug_ref, uv_ref, wg_ref, wv_ref, bg_ref, bv_ref, a_ref):
        g, _, _ = _conv3(ug_ref[...], wg_ref, bg_ref[...])
        val, _, _ = _conv3(uv_ref[...], wv_ref, bv_ref[...])
        a_ref[...] = (_gelu_parts(g)[0] * val).astype(BF16)

    def seq(off):
        return pl.BlockSpec((S, tn), lambda b, j: (b, off + j))

    def par(rows, off):
        return pl.BlockSpec((rows, tn), lambda b, j: (0, off + j))

    return _call(body, [u, u, cw, cw, cb, cb], name="convffn_fwd", grid=(B, nf),
                 in_specs=[seq(0), seq(nf), par(3, 0), par(3, nf), par(1, 0), par(1, nf)],
                 out_specs=[seq(0)], out_shape=[_sds((T, F), BF16)], sem=("parallel", "parallel"), comm=comm)


def _convffn_bwd(u, dact, cw, cb, B, S, F, comm=None):
    T = u.shape[0]
    tn = _tile(F, 256)
    nf = F // tn

    def body(ug_ref, uv_ref, da_ref, wg_ref, wv_ref, bg_ref, bv_ref, dug_ref, duv_ref, dpg_ref, dpv_ref):
        b = pl.program_id(1)
        ug, uv, da = ug_ref[...], uv_ref[...], da_ref[...]
        g, ug1, ug2 = _conv3(ug, wg_ref, bg_ref[...])
        val, uv1, uv2 = _conv3(uv, wv_ref, bv_ref[...])
        gel, dgel = _gelu_parts(g)
        dg = da * val * dgel
        dval = da * gel
        row = lax.broadcasted_iota(jnp.int32, ug.shape, 0)

        def back(d, w_ref):
            d1 = jnp.where(row < S - 1, pltpu.roll(d, S - 1, 0), 0.0)
            d2 = jnp.where(row < S - 2, pltpu.roll(d, S - 2, 0), 0.0)
            return w_ref[2:3, :] * d + w_ref[1:2, :] * d1 + w_ref[0:1, :] * d2

        dug_ref[...] = back(dg, wg_ref).astype(BF16)
        duv_ref[...] = back(dval, wv_ref).astype(BF16)

        def sums(d, u0, u1, u2):
            r8 = lax.broadcasted_iota(jnp.int32, (8, d.shape[1]), 0)
            out = jnp.zeros((8, d.shape[1]), F32)
            for k, t in enumerate((d * u2, d * u1, d * u0, d)):
                out = jnp.where(r8 == k, jnp.sum(t, axis=0, keepdims=True), out)
            return out

        _accumulate(dpg_ref, sums(dg, ug, ug1, ug2), b == 0)
        _accumulate(dpv_ref, sums(dval, uv, uv1, uv2), b == 0)

    def seq(off):
        return pl.BlockSpec((S, tn), lambda j, b: (b, off + j))

    def par(rows, off):
        return pl.BlockSpec((rows, tn), lambda j, b: (0, off + j))

    outs, landed = _call(
        body, [u, u, dact, cw, cw, cb, cb], name="convffn_bwd", grid=(nf, B),
        in_specs=[seq(0), seq(nf), seq(0), par(3, 0), par(3, nf), par(1, 0), par(1, nf)],
        out_specs=[seq(0), seq(0), par(8, 0), par(8, 0)],
        out_shape=[_sds((T, F), BF16), _sds((T, F), BF16), _sds((8, F), F32), _sds((8, F), F32)],
        sem=("parallel", "arbitrary"), comm=comm)
    return outs, landed


def _tail(ff, x1, tgt, g):
    T, D = ff.shape
    tm = _tile(T, ROW_TILE, 16)

    def body(ff_ref, x1_ref, t_ref, g_ref, dy_ref, dff_ref, loss_ref, dg_ref):
        i = pl.program_id(0)
        f = ff_ref[...]
        gv = g_ref[...]
        r = _rms(f)
        n = f * r
        e = (x1_ref[...] + n * gv) - t_ref[...]
        dy = e * (1.0 / D)
        dy_ref[...] = dy
        dn = dy * gv
        dff_ref[...] = (r * (dn - n * jnp.mean(dn * n, axis=-1, keepdims=True))).astype(BF16)
        part = 0.5 * jnp.sum(jnp.mean(e * e, axis=-1, keepdims=True), axis=0, keepdims=True)
        _accumulate(loss_ref, jnp.broadcast_to(part, loss_ref.shape), i == 0)
        _accumulate(dg_ref, jnp.sum(dy * n, axis=0, keepdims=True), i == 0)

    row = pl.BlockSpec((tm, D), lambda i: (i, 0))
    vec = pl.BlockSpec((1, D), lambda i: (0, 0))
    return _call(body, [ff, x1, tgt, g], name="tail", grid=(T // tm,), in_specs=[row, row, row, vec],
                 out_specs=[row, row, pl.BlockSpec((8, LANES), lambda i: (0, 0)), vec],
                 out_shape=[_sds((T, D), F32), _sds((T, D), BF16), _sds((8, LANES), F32), _sds((1, D), F32)],
                 sem=("arbitrary",))[0]


def _mid_bwd(dy, dh2, x1, y1, g_ffn, g_pm, comm=None):
    T, D = dy.shape
    tm = _tile(T, ROW_TILE, 16)

    def body(dy_ref, dh_ref, x1_ref, y1_ref, gf_ref, gp_ref, dx1_ref, dy1_ref, dgf_ref, dgp_ref):
        i = pl.program_id(0)
        dh = dh_ref[...]
        d2, dgf = _rms_bwd(dh, x1_ref[...], gf_ref[...])
        dx1 = dy_ref[...] + d2
        dx1_ref[...] = dx1
        d1, dgp = _rms_bwd(dx1, y1_ref[...], gp_ref[...])
        dy1_ref[...] = d1.astype(BF16)
        _accumulate(dgf_ref, jnp.sum(dgf, axis=0, keepdims=True), i == 0)
        _accumulate(dgp_ref, jnp.sum(dgp, axis=0, keepdims=True), i == 0)

    row = pl.BlockSpec((tm, D), lambda i: (i, 0))
    vec = pl.BlockSpec((1, D), lambda i: (0, 0))
    return _call(body, [dy, dh2, x1, y1, g_ffn, g_pm], name="mid_bwd", grid=(T // tm,),
                 in_specs=[row, row, row, row, vec, vec], out_specs=[row, row, vec, vec],
                 out_shape=[_sds((T, D), F32), _sds((T, D), BF16), _sds((1, D), F32), _sds((1, D), F32)],
                 sem=("arbitrary",), comm=comm)


def _gate_bwd(dm, am, af, proj, bgate, lay, D, comm=None):
    T = dm.shape[0]
    tm = _tile(T, ROW_TILE, 16)
    tn = _tile(D, 512)

    def body(dm_ref, am_ref, af_ref, gm_ref, gf_ref, bm_ref, bf_ref,
             dam_ref, daf_ref, dgm_ref, dgf_ref, dbm_ref, dbf_ref):
        i = pl.program_id(1)
        d = dm_ref[...]
        sm = _sigmoid(gm_ref[...] + bm_ref[...])
        sf = _sigmoid(gf_ref[...] + bf_ref[...])
        dam_ref[...] = (d * sm).astype(BF16)
        daf_ref[...] = (d * sf).astype(BF16)
        dgm = d * am_ref[...] * (sm * (1.0 - sm))
        dgf = d * af_ref[...] * (sf * (1.0 - sf))
        dgm_ref[...] = dgm.astype(BF16)
        dgf_ref[...] = dgf.astype(BF16)
        _accumulate(dbm_ref, jnp.sum(dgm, axis=0, keepdims=True), i == 0)
        _accumulate(dbf_ref, jnp.sum(dgf, axis=0, keepdims=True), i == 0)

    og = lay["g"] // tn
    blk = pl.BlockSpec((tm, tn), lambda j, i: (i, j))
    vec = pl.BlockSpec((1, tn), lambda j, i: (0, j))
    return _call(
        body, [dm, am, af, proj, proj, bgate, bgate], name="gate_bwd", grid=(D // tn, T // tm),
        in_specs=[blk, blk, blk, pl.BlockSpec((tm, tn), lambda j, i: (i, og + j)),
                  pl.BlockSpec((tm, tn), lambda j, i: (i, og + D // tn + j)),
                  vec, pl.BlockSpec((1, tn), lambda j, i: (0, D // tn + j))],
        out_specs=[blk, blk, blk, blk, vec, vec],
        out_shape=[_sds((T, D), BF16)] * 4 + [_sds((1, D), F32)] * 2, sem=("parallel", "arbitrary"), comm=comm)


def _mla_bwd_prep(dq, dk, dv, cosT, sinT, comm=None):
    H, T, _ = dq.shape
    tm = _tile(T, HEAD_ROW_TILE, 16)

    def body(dq_ref, dk_ref, dv_ref, cos_ref, sin_ref, dqr_ref, dkv_ref, dkpe_ref):
        h = pl.program_id(1)
        cs, sn = cos_ref[...], sin_ref[...]
        valid = _lane(cs.shape) < ROPE

        def unrope(d):
            d = jnp.where(valid, d, 0.0)
            return d * cs - _rope_rot(d) * sn

        dqv = dq_ref[...]
        dqr_ref[:, :NOPE] = dqv[:, :NOPE].astype(BF16)
        dqr_ref[:, NOPE:] = unrope(dqv[:, NOPE:]).astype(BF16)
        dkv_ = dk_ref[...]
        dkv_ref[:, :NOPE] = dkv_[:, :NOPE].astype(BF16)
        dkv_ref[:, NOPE:] = dv_ref[...].astype(BF16)
        _accumulate(dkpe_ref, unrope(dkv_[:, NOPE:]), h == 0)

    head = pl.BlockSpec((None, tm, ATT_DK), lambda i, h: (h, i, 0))
    tok = pl.BlockSpec((tm, LANES), lambda i, h: (i, 0))
    return _call(
        body, [dq, dk, dv, cosT, sinT], name="mla_bwd_prep", grid=(T // tm, H),
        in_specs=[head, head, pl.BlockSpec((None, tm, VDIM), lambda i, h: (h, i, 0)), tok, tok],
        out_specs=[head, head, tok],
        out_shape=[_sds((H, T, ATT_DK), BF16), _sds((H, T, ATT_DK), BF16), _sds((T, LANES), F32)],
        sem=("parallel", "arbitrary"), comm=comm)


def _fox_bwd_prep(dq, dk, proj, bfor, lay, B, S, inv_scale):
    H, T, _ = dq.shape

    def body(dq_ref, dk_ref, fl_ref, bf_ref, dfl_ref, dbf_ref, dc_sc):
        b, h = pl.program_id(0), pl.program_id(1)
        lane = _lane(dc_sc.shape)
        col = jnp.sum(jnp.where(lane == 0, dq_ref[...], 0.0) - jnp.where(lane == 3, dk_ref[...], 0.0),
                      axis=1, keepdims=True)

        @pl.when(h == 0)
        def _():
            dc_sc[...] = jnp.zeros(dc_sc.shape, F32)

        dc_sc[...] = jnp.where(lane == h, col, dc_sc[...])

        @pl.when(h == H - 1)
        def _():
            dlogf = _cumsum_rows(dc_sc[...] * inv_scale, reverse=True)
            z = fl_ref[...] + bf_ref[...]
            dz = jnp.where(lane < H, dlogf * (1.0 / (1.0 + jnp.exp(z))), 0.0)
            dfl_ref[...] = dz
            _accumulate(dbf_ref, jnp.sum(dz, axis=0, keepdims=True), b == 0)

    aug = pl.BlockSpec((None, S, LANES), lambda b, h: (h, b, 1))
    seq = pl.BlockSpec((S, LANES), lambda b, h: (b, 0))
    vec = pl.BlockSpec((1, LANES), lambda b, h: (0, 0))
    return _call(
        body, [dq, dk, proj, bfor], name="fox_bwd_prep", grid=(B, H),
        in_specs=[aug, aug, pl.BlockSpec((S, LANES), lambda b, h: (b, lay["fl"] // LANES)), vec],
        out_specs=[seq, vec], out_shape=[_sds((T, LANES), F32), _sds((1, LANES), F32)],
        scratch_shapes=[pltpu.VMEM((S, LANES), F32)], sem=("arbitrary", "arbitrary"))[0]


def _lat_bwd(dqn, dkvn, proj, gq, gkv, lay):
    T = dqn.shape[0]
    tm = _tile(T, ROW_TILE, 16)

    def body(dq_ref, dkv_ref, q_ref, kv_ref, gq_ref, gkv_ref, dql_ref, dkl_ref, dgq_ref, dgkv_ref):
        i = pl.program_id(0)
        dql, dgq = _rms_bwd(dq_ref[...], q_ref[...], gq_ref[...])
        dkl, dgkv = _rms_bwd(dkv_ref[...], kv_ref[...], gkv_ref[...])
        dql_ref[...] = dql.astype(BF16)
        dkl_ref[...] = dkl.astype(BF16)
        _accumulate(dgq_ref, jnp.sum(dgq, axis=0, keepdims=True), i == 0)
        _accumulate(dgkv_ref, jnp.sum(dgkv, axis=0, keepdims=True), i == 0)

    def blk(width, off=0):
        return pl.BlockSpec((tm, width), lambda i: (i, off // width))

    def vec(width):
        return pl.BlockSpec((1, width), lambda i: (0, 0))

    return _call(
        body, [dqn, dkvn, proj, proj, gq, gkv], name="lat_bwd", grid=(T // tm,),
        in_specs=[blk(Q_LORA), blk(KV_LORA), blk(Q_LORA, lay["q"]), blk(KV_LORA, lay["kv"]), vec(Q_LORA), vec(KV_LORA)],
        out_specs=[blk(Q_LORA), blk(KV_LORA), vec(Q_LORA), vec(KV_LORA)],
        out_shape=[_sds((T, Q_LORA), BF16), _sds((T, KV_LORA), BF16), _sds((1, Q_LORA), F32), _sds((1, KV_LORA), F32)],
        sem=("arbitrary",))[0]


def _final_dx(dx1, dh, x, g, comm=None):
    T, D = x.shape
    tm = _tile(T, ROW_TILE, 16)

    def body(dx1_ref, dh_ref, x_ref, g_ref, dx_ref, dg_ref):
        i = pl.program_id(0)
        d, dg = _rms_bwd(dh_ref[...], x_ref[...], g_ref[...])
        dx_ref[...] = dx1_ref[...] + d
        _accumulate(dg_ref, jnp.sum(dg, axis=0, keepdims=True), i == 0)

    row = pl.BlockSpec((tm, D), lambda i: (i, 0))
    vec = pl.BlockSpec((1, D), lambda i: (0, 0))
    return _call(body, [dx1, dh, x, g], name="final_dx", grid=(T // tm,), in_specs=[row, row, row, vec],
                 out_specs=[row, vec], out_shape=[_sds((T, D), F32), _sds((1, D), F32)], sem=("arbitrary",), comm=comm)


def _chip_sum(pieces, paired, qc, name):
    G, R, C = pieces.shape
    tr = _tile(R, 256, 16)

    def body(qc_ref, g_ref, p_ref, keep_ref, send_ref):
        s = pl.program_id(1)
        tot = g_ref[...] + p_ref[...]

        @pl.when(s == 0)
        def _():
            keep_ref[...] = tot

        @pl.when(s > 0)
        def _():
            send_ref[...] = tot.astype(send_ref.dtype)

    grid_spec = pltpu.PrefetchScalarGridSpec(
        num_scalar_prefetch=1, grid=(R // tr, N_CHIP),
        in_specs=[pl.BlockSpec((None, tr, C), lambda i, s, qc: (2 * (qc[0] ^ s) + qc[1], i, 0)),
                  pl.BlockSpec((None, tr, C), lambda i, s, qc: (qc[0] ^ s, i, 0))],
        out_specs=[pl.BlockSpec((tr, C), lambda i, s, qc: (i, 0)),
                   pl.BlockSpec((None, tr, C), lambda i, s, qc: (jnp.maximum(s - 1, 0), i, 0))])
    send_dtype = BF16 if R >= 16 else pieces.dtype
    return pl.pallas_call(
        body, name=name, grid_spec=grid_spec,
        out_shape=[_sds((R, C), F32), _sds((3, R, C), send_dtype)],
        compiler_params=pltpu.CompilerParams(dimension_semantics=("arbitrary", "arbitrary"),
                                             vmem_limit_bytes=VMEM_LIMIT_BYTES),
    )(qc, pieces, paired)


def _adamw_math(w, g, m, v):
    m = ADAM_B1 * m + (1.0 - ADAM_B1) * g
    v = ADAM_B2 * v + (1.0 - ADAM_B2) * (g * g)
    m_hat = m / (1.0 - ADAM_B1 ** ADAM_STEP)
    v_hat = v / (1.0 - ADAM_B2 ** ADAM_STEP)
    delta = -ADAM_LR * (m_hat / (jnp.sqrt(v_hat) + ADAM_EPS) + ADAM_WD * w)
    return delta, m, v


def _sum_adamw(keep, pieces, w, m, v, name):
    R, C = w.shape
    P = pieces.shape[0]
    tr = _tile(R, 256, 16)

    def body(k_ref, p_ref, w_ref, m_ref, v_ref, g_ref, d_ref, mo_ref, vo_ref):
        g = k_ref[...]
        for q in range(P):
            g = g + p_ref[q].astype(F32)
        g_ref[...] = g
        d_ref[...], mo_ref[...], vo_ref[...] = _adamw_math(w_ref[...], g, m_ref[...], v_ref[...])

    blk = pl.BlockSpec((tr, C), lambda i: (i, 0))
    pblk = pl.BlockSpec((P, tr, C), lambda i: (0, i, 0))
    return _call(body, [keep, pieces, w, m, v], name=name, grid=(R // tr,), in_specs=[blk, pblk, blk, blk, blk],
                 out_specs=[blk] * 4, out_shape=[_sds((R, C), F32)] * 4, sem=("parallel",))[0]


def _adamw_small(parts, w, m, v, widths):
    n = len(widths)

    def body(p_ref, w_ref, m_ref, v_ref, *o_refs):
        g = p_ref[0]
        for q in range(1, N_DEV):
            g = g + p_ref[q]
        vals = (g,) + _adamw_math(w_ref[...], g, m_ref[...], v_ref[...])
        off = 0
        for i, wd in enumerate(widths):
            for kind in range(4):
                o_refs[4 * i + kind][...] = vals[kind][:, off:off + wd]
            off += wd

    whole = pl.BlockSpec(memory_space=pltpu.VMEM)
    outs = _call(body, [parts, w, m, v], name="adamw_small", grid=(), in_specs=[whole] * 4,
                 out_specs=[whole] * (4 * n), out_shape=[_sds((1, wd), F32) for wd in widths for _ in range(4)])[0]
    return [tuple(outs[4 * i:4 * i + 4]) for i in range(n)]


def _layout(D):
    lay = {"q": 0, "kv": Q_LORA, "kpe": Q_LORA + KV_LORA}
    lay["fq"] = lay["kpe"] + LANES
    lay["fk"] = lay["fq"] + HEADS * FOX_DIM
    lay["fv"] = lay["fk"] + HEADS * FOX_DIM
    lay["fl"] = lay["fv"] + HEADS * FOX_DIM
    lay["g"] = lay["fl"] + LANES
    lay["end"] = lay["g"] + 2 * D
    return lay


def kernel(x, positions, pre_mix_norm, w_in, q_a_norm, w_uq, kv_a_norm, w_ukv, b_forget, b_gate, w_branch_mla, w_branch_fox, w_out, post_mix_norm, pre_ffn_norm, w_up, conv_w, conv_b, w_down, post_ffn_norm, loss_target, m_pre_mix_norm, m_w_in, m_q_a_norm, m_w_uq, m_kv_a_norm, m_w_ukv, m_b_forget, m_b_gate, m_w_branch_mla, m_w_branch_fox, m_w_out, m_post_mix_norm, m_pre_ffn_norm, m_w_up, m_conv_w, m_conv_b, m_w_down, m_post_ffn_norm, v_pre_mix_norm, v_w_in, v_q_a_norm, v_w_uq, v_kv_a_norm, v_w_ukv, v_b_forget, v_b_gate, v_w_branch_mla, v_w_branch_fox, v_w_out, v_post_mix_norm, v_pre_ffn_norm, v_w_up, v_conv_w, v_conv_b, v_w_down, v_post_ffn_norm):
    B, S, D = x.shape
    T = B * S
    F = conv_b.shape[0] // 2
    lay = _layout(D)
    n_in = w_in.shape[1]
    d_in = N_DEV * n_in
    seg_a = Q_LORA + KV_LORA + ROPE
    seg_b = 3 * HEADS * FOX_DIM + HEADS
    mla_scale = (NOPE + ROPE) ** -0.5
    fox_scale = FOX_DIM ** -0.5
    ax, ay, ac = (lax.axis_index(a) for a in MESH_AXES)
    qc = jnp.stack([2 * ax + ay, ac]).astype(jnp.int32)

    def row(vec, width=None):
        vec = vec.reshape(1, -1)
        if width is not None and vec.shape[1] < width:
            vec = jnp.pad(vec, ((0, 0), (0, width - vec.shape[1])))
        return vec

    x2 = x.reshape(T, D)
    win_s = _cast_bf16(w_in, "cast_w_in")
    h, (win_g,) = _prenorm(x2, row(pre_mix_norm), comm=_Comm([_GatherRelayPlan([win_s], mid_frac=0.3)]))
    small_s = [_cast_bf16(w, "cast_" + n) for w, n in
               [(w_uq, "w_uq"), (w_ukv, "w_ukv"), (w_branch_mla, "w_branch_mla"), (w_branch_fox, "w_branch_fox"), (w_out, "w_out")]]
    wup_s = _cast_bf16(w_up, "cast_w_up")
    wdown_s = _cast_bf16(w_down, "cast_w_down")

    def shard_cols(lo, hi):
        out = []
        for g in range(lo // n_in, (hi - 1) // n_in + 1):
            out.append(win_g[g][:, max(lo, g * n_in) - g * n_in:min(hi, (g + 1) * n_in) - g * n_in])
        return out

    w_perm = jnp.concatenate(
        shard_cols(0, seg_a) + [jnp.zeros((D, LANES - ROPE), BF16)] + shard_cols(seg_a, seg_a + seg_b)
        + [jnp.zeros((D, LANES - HEADS), BF16)] + shard_cols(seg_a + seg_b, d_in), axis=1)

    tgt = loss_target.reshape(T, D)
    pos = positions.reshape(T, 1)
    inv_freq = 1.0 / (ROPE_THETA ** (jnp.arange(0, ROPE, 2, dtype=F32) / ROPE))
    invf = row(jnp.concatenate([inv_freq, inv_freq]), LANES)
    g_pre, g_q, g_kv = row(pre_mix_norm), row(q_a_norm), row(kv_a_norm)
    g_pm, g_ffn, g_pf = row(post_mix_norm), row(pre_ffn_norm), row(post_ffn_norm)
    bfor = row(b_forget, LANES)
    bgate = row(b_gate)
    cb_full = row(conv_b)

    def own_plan(blocks):
        return _Comm([_GatherOwnPlan(blocks)])

    def pass_plan(gathered):
        return _Comm([_GatherPassPlan(gathered)])

    def pair_plan(gs):
        return _Comm([_PairScatterPlan(gs)])

    def chip_plan(gs):
        return _Comm([_ChipScatterPlan(gs)])

    half_d = D // 2
    proj, landed = _matmul(h, w_perm, mode="nn", name="mm_proj", comm=_Comm(
        [_GatherOwnPlan(small_s[:2] + [conv_w]), _GatherOwnPlan([wup_s], rows=(0, half_d))]))
    early_g, wup_part = landed[:-1], landed[-1:]
    (qn, kvn, kper, logf, cosT, sinT), (wuq_g, wukv_g, cw_g) = _split_prep(
        proj, pos, invf, g_q, g_kv, bfor, lay, comm=pass_plan(early_g))
    wuq_pad = jnp.pad(wuq_g, ((0, 0), (0, 0), (0, ATT_DK - NOPE - ROPE)))
    cw_full = jnp.transpose(cw_g, (1, 0, 2)).reshape(3, 2 * F)

    qraw = _matmul(qn, wuq_pad, mode="nn", name="mm_q", out_blocks=ATT_DK, tm=T)
    kvraw = _matmul(kvn, wukv_g, mode="nn", name="mm_kv", out_blocks=NOPE + VDIM, tm=T)
    (q_mla, k_mla, v_mla), branch_half = _mla_prep(qraw, kvraw, kper, cosT, sinT, comm=own_plan(small_s[2:4]))
    cs = _fox_cumsum(logf, B, S, 1.0 / fox_scale)
    (q_fox, k_fox, v_fox), wout_half = _fox_prep(proj, cs, lay, comm=own_plan(small_s[4:5]))
    ((o_mla, lse_mla), (o_fox, lse_fox)), landed = _attn_fwd(
        [(q_mla, k_mla, v_mla, MLA_UNIT, mla_scale), (q_fox, k_fox, v_fox, 1, fox_scale)], B=B, S=S,
        name="attn_fwd", comm=_Comm([_GatherOwnPlan([wup_s], rows=(half_d, D), into=wup_part),
                                     _GatherPassPlan(branch_half + wout_half)]))
    wup_half, (wbm_g, wbf_g, wout_g) = landed[:1], landed[1:]
    wbm = jnp.transpose(wbm_g, (1, 0, 2)).reshape(HEADS * VDIM, D)
    wbf = jnp.transpose(wbf_g, (1, 0, 2)).reshape(HEADS * FOX_DIM, D)
    wout = wout_g.reshape(D, D)
    a_m = _matmul(o_mla, wbm, mode="nn", name="mm_branch_mla", tm=2 * MM_TILE)
    a_f = _matmul(o_fox, wbf, mode="nn", name="mm_branch_fox", tm=2 * MM_TILE)
    (merged,), (wup_g,) = _gate_merge(a_m, a_f, proj, bgate, lay, D, comm=pass_plan(wup_half))
    n_up = wup_g.shape[2]
    y1 = _matmul(merged, wout, mode="nn", name="mm_out")
    x1, h2 = _mid(x2, y1, g_pm, g_ffn)
    u, wdown_half = _matmul(h2, wup_g, mode="nn", name="mm_up", tn=n_up, comm=own_plan([wdown_s]))
    (act,), (wdown_g,) = _convffn_fwd(u, cw_full, cb_full, B, S, F, comm=pass_plan(wdown_half))
    wdown = wdown_g.reshape(F, D)
    ff = _matmul(act, wdown, mode="nn", name="mm_down", tk=F // 2)
    dy, dff, loss_part, dg_pf = _tail(ff, x1, tgt, g_pf)

    dact = _matmul(dff, wdown, mode="nt", name="mm_dact", tn=F // 4)
    dw_down = _matmul(act, dff, mode="tn", name="mm_dw_down", tm=F // 4, tn=512).reshape(N_DEV, F // N_DEV, D)
    (du_g, du_v, dcp_g, dcp_v), (pa_down,) = _convffn_bwd(u, dact, cw_full, cb_full, B, S, F, comm=pair_plan([dw_down]))
    keep_down, sb_down = _chip_sum(dw_down, pa_down, qc, "chipsum_w_down")
    dh2, (rb_down,) = _matmul_halves((du_g, du_v), wup_g, mode="nt", name="mm_dh2", tm=MM_TILE, comm=chip_plan([sb_down]))
    dw_up = _matmul_halves(h2, (du_g, du_v), mode="tn", name="mm_dw_up", tm=MM_TILE, tn=n_up, tk=T // 2)
    (dx1, dy1, dg_ffn, dg_pm), _ = _mid_bwd(dy, dh2, x1, y1, g_ffn, g_pm)
    dmerged = _matmul(dy1, wout, mode="nt", name="mm_dmerged")
    dw_out = _matmul(merged, dy1, mode="tn", name="mm_dw_out").reshape(N_DEV, D // N_DEV, D)
    (da_m, da_f, dgl_m, dgl_f, dbg_m, dbg_f), (pa_up,) = _gate_bwd(
        dmerged, a_m, a_f, proj, bgate, lay, D, comm=pair_plan([dw_up]))
    keep_up, sb_up = _chip_sum(dw_up, pa_up, qc, "chipsum_w_up")
    dw_bm = _matmul(o_mla, da_m, mode="tn", name="mm_dw_branch_mla", out_blocks=D // N_DEV)
    dw_bf = _matmul(o_fox, da_f, mode="tn", name="mm_dw_branch_fox", out_blocks=D // N_DEV)
    mix = [dw_out, dw_bm, dw_bf]
    do_mla, pa_mix = _matmul(da_m, wbm, mode="nt", name="mm_do_mla", out_dtype=BF16, comm=pair_plan(mix))
    do_fox = _matmul(da_f, wbf, mode="nt", name="mm_do_fox", out_dtype=BF16)
    mix_sums = [_chip_sum(g, p, qc, "chipsum_" + n) for g, p, n in zip(mix, pa_mix, ["w_out", "w_branch_mla", "w_branch_fox"])]
    ((dq_m, dk_m, dv_m), (dq_f, dk_f, dv_f)), (rb_up,) = _attn_bwd(
        [(q_mla, k_mla, v_mla, o_mla, do_mla, lse_mla, MLA_UNIT, mla_scale),
         (q_fox, k_fox, v_fox, o_fox, do_fox, lse_fox, 1, fox_scale)], B=B, S=S, name="attn_bwd",
        comm=chip_plan([sb_up]))
    (dqraw, dkvraw, dkpe), _ = _mla_bwd_prep(dq_m, dk_m, dv_m, cosT, sinT)
    dqn = _matmul(dqraw, wuq_pad, mode="nt", name="mm_dqn", tm=T)
    dw_uq = _matmul(qn, dqraw, mode="tn", name="mm_dw_uq", out_blocks=ATT_DK)[:, :, :NOPE + ROPE]
    dkvn = _matmul(dkvraw, wukv_g, mode="nt", name="mm_dkvn", tm=T)
    dw_ukv = _matmul(kvn, dkvraw, mode="tn", name="mm_dw_ukv", out_blocks=NOPE + VDIM)
    dqlat, dkvlat, dg_q, dg_kv = _lat_bwd(dqn, dkvn, proj, g_q, g_kv, lay)
    dfl, dbfor = _fox_bwd_prep(dq_f, dk_f, proj, bfor, lay, B, S, 1.0 / fox_scale)
    dproj = _concat_cols([dqlat, dkvlat, dkpe, dq_f, dk_f, dv_f, dfl, dgl_m, dgl_f], "concat_dproj")
    dw_perm, rb_mix = _matmul(h, dproj, mode="tn", name="mm_dw_in", comm=chip_plan([s[1] for s in mix_sums]))
    segs = [(0, seg_a, 0), (seg_a, seg_a + seg_b, lay["fq"] - seg_a), (seg_a + seg_b, d_in, lay["g"] - seg_a - seg_b)]

    def piece(g):
        lo, hi = g * n_in, (g + 1) * n_in
        parts = [dw_perm[:, max(lo, s0) + sh:min(hi, s1) + sh] for s0, s1, sh in segs if max(lo, s0) < min(hi, s1)]
        return parts[0] if len(parts) == 1 else jnp.concatenate(parts, axis=1)

    dw_in = jnp.stack([piece(g) for g in range(N_DEV)])
    dcw = jnp.transpose(jnp.concatenate([dcp_g[0:3], dcp_v[0:3]], axis=1).reshape(3, N_DEV, (2 * F) // N_DEV), (1, 0, 2))
    late = [dw_in, dw_uq, dw_ukv, dcw]
    pa_late = _exchange_alone(pair_plan(late), "pair_late")
    late_sums = [_chip_sum(g, p, qc, "chipsum_" + n) for g, p, n in zip(late, pa_late, ["w_in", "w_uq", "w_ukv", "conv_w"])]
    dh, rb_late = _matmul(dproj, w_perm, mode="nt", name="mm_dh", tn=2048, tk=2048, comm=chip_plan([s[1] for s in late_sums]))
    (grad_x, dg_pre), _ = _final_dx(dx1, dh, x2, g_pre)

    big_out = {}

    def finish(n, keep, pieces, w, m, v):
        big_out[n] = _sum_adamw(keep, pieces, w, m, v, "adamw_" + n)

    finish("w_down", keep_down, rb_down, w_down, m_w_down, v_w_down)
    finish("w_up", keep_up, rb_up, w_up, m_w_up, v_w_up)
    finish("w_out", mix_sums[0][0], rb_mix[0], w_out, m_w_out, v_w_out)
    finish("w_branch_mla", mix_sums[1][0], rb_mix[1], w_branch_mla, m_w_branch_mla, v_w_branch_mla)
    finish("w_branch_fox", mix_sums[2][0], rb_mix[2], w_branch_fox, m_w_branch_fox, v_w_branch_fox)
    finish("w_in", late_sums[0][0], rb_late[0], w_in, m_w_in, v_w_in)
    finish("w_uq", late_sums[1][0], rb_late[1], w_uq, m_w_uq, v_w_uq)
    finish("w_ukv", late_sums[2][0], rb_late[2], w_ukv, m_w_ukv, v_w_ukv)
    finish("conv_w", late_sums[3][0], rb_late[3], conv_w, m_conv_w, v_conv_w)

    widths = [D, Q_LORA, KV_LORA, LANES, 2 * D, D, D, 2 * F, D]
    small_names = ["pre_mix_norm", "q_a_norm", "kv_a_norm", "b_forget", "b_gate", "post_mix_norm", "pre_ffn_norm",
                   "conv_b", "post_ffn_norm"]
    true_w = [D, Q_LORA, KV_LORA, HEADS, 2 * D, D, D, 2 * F, D]
    dcb = jnp.concatenate([dcp_g[3:4], dcp_v[3:4]], axis=1)
    part = jnp.concatenate([dg_pre, dg_q, dg_kv, dbfor, dbg_m, dbg_f, dg_pm, dg_ffn, dcb, dg_pf], axis=1)

    def pack(vals):
        return jnp.concatenate([row(a, wd) for a, wd in zip(vals, widths)], axis=1)

    sw = pack([pre_mix_norm, q_a_norm, kv_a_norm, b_forget, b_gate, post_mix_norm, pre_ffn_norm, conv_b, post_ffn_norm])
    sm = pack([m_pre_mix_norm, m_q_a_norm, m_kv_a_norm, m_b_forget, m_b_gate, m_post_mix_norm, m_pre_ffn_norm,
               m_conv_b, m_post_ffn_norm])
    sv = pack([v_pre_mix_norm, v_q_a_norm, v_kv_a_norm, v_b_forget, v_b_gate, v_post_mix_norm, v_pre_ffn_norm,
               v_conv_b, v_post_ffn_norm])
    (parts_all,) = _exchange_alone(_Comm([_DirectGatherPlan([part])]), "gather_small")
    small = _adamw_small(parts_all, sw, sm, sv, widths)
    small_out = {n: tuple(a.reshape(-1)[:tw] for a in vals) for n, vals, tw in zip(small_names, small, true_w)}

    loss = lax.psum(loss_part[0, 0], MESH_AXES)
    order = ["pre_mix_norm", "w_in", "q_a_norm", "w_uq", "kv_a_norm", "w_ukv", "b_forget", "b_gate", "w_branch_mla",
             "w_branch_fox", "w_out", "post_mix_norm", "pre_ffn_norm", "w_up", "conv_w", "conv_b", "w_down",
             "post_ffn_norm"]
    res = {**big_out, **small_out}
    outs = [loss, grad_x.reshape(B, S, D)]
    for kind in range(4):
        outs += [res[n][kind] for n in order]
    return tuple(outs)
```

```python
import math

import jax
import jax.numpy as jnp
from jax import lax
from jax.experimental import pallas as pl
from jax.experimental.pallas import tpu as pltpu

F32 = jnp.float32
BF16 = jnp.bfloat16

N_DEV = 8
N_CHIP = 4
HEADS = 8
NOPE = 128
ROPE = 64
HALF_ROPE = ROPE // 2
VDIM = 128
Q_LORA = 512
KV_LORA = 256
FOX_DIM = 128
ATT_DK = 256
MLA_UNIT = 64
ROPE_THETA = 10000.0
EPS = 1e-6
NEG_INF = -1e30
LANES = 128
LOG2_E = 1.4426950408889634

ADAM_LR = 0.001
ADAM_B1 = 0.9
ADAM_B2 = 0.999
ADAM_EPS = 1e-08
ADAM_WD = 0.01
ADAM_STEP = 10

VMEM_LIMIT_BYTES = 56 * 1024 * 1024
ROW_TILE = 256
HEAD_ROW_TILE = 1024
ATT_TILE = 1024
ATT_SUB = 256
ATT_AHEAD = 3
MM_TILE = 1024

MESH_AXES = ("x", "y", "c")
ANY = pl.BlockSpec(memory_space=pl.ANY)


def _tile(n, pref, align=LANES):
    if n <= pref:
        return n
    t = (pref // align) * align
    while t >= align:
        if n % t == 0:
            return t
        t -= align
    return n


def _sds(shape, dtype):
    return jax.ShapeDtypeStruct(shape, dtype)


def _coords():
    x, y, c = (lax.axis_index(ax) for ax in MESH_AXES)
    return x, y, c


def _chip_rel(x, y, r):
    return (1 - x if r & 2 else x), (1 - y if r & 1 else y)


def _rcopy(src, dst, sems, w, k, dev):
    return pltpu.make_async_remote_copy(src_ref=src, dst_ref=dst, send_sem=sems[0].at[w, k], recv_sem=sems[1].at[w, k],
                                        device_id=dev, device_id_type=pl.DeviceIdType.MESH)


class _GatherRelayPlan:
    def __init__(self, blocks, mid_frac=0.5):
        self.ins = list(blocks)
        self.out_shapes = [_sds((N_DEV,) + b.shape, b.dtype) for b in blocks]
        n = len(blocks)
        self.scratch = [pltpu.SemaphoreType.DMA((n, 7)), pltpu.SemaphoreType.DMA((n, 7)), pltpu.SemaphoreType.DMA((n,))]
        self.mid_frac = mid_frac

    @staticmethod
    def _places():
        x, y, c = _coords()
        xn, yn = 4 * (1 - x) + 2 * y, 4 * x + 2 * (1 - y)
        relay_src = 4 * (x + c * (1 - 2 * x)) + 2 * (y + (1 - c) * (1 - 2 * y)) + c
        relay_to = (x + (1 - c) * (1 - 2 * x), y + c * (1 - 2 * y), c)
        return x, y, c, xn, yn, relay_src, relay_to, 4 * (1 - x) + 2 * (1 - y)

    def first(self, ins, outs, sems):
        x, y, c, _, _, _, _, _ = self._places()
        me = 4 * x + 2 * y + c
        for w in range(len(ins)):
            pltpu.make_async_copy(ins[w], outs[w].at[me], sems[2].at[w]).start()
            _rcopy(ins[w], outs[w].at[me], sems, w, 0, (x, y, 1 - c)).start()
            _rcopy(ins[w], outs[w].at[me], sems, w, 1, (1 - x, y, c)).start()
            _rcopy(ins[w], outs[w].at[me], sems, w, 2, (x, 1 - y, c)).start()

    def mid(self, ins, outs, sems):
        x, y, c, xn, yn, relay_src, relay_to, _ = self._places()
        sib = (x, y, 1 - c)
        for w in range(len(ins)):
            bx, by = outs[w].at[xn + c], outs[w].at[yn + c]
            _rcopy(ins[w], bx, sems, w, 1, (1 - x, y, c)).wait_recv()
            _rcopy(ins[w], by, sems, w, 2, (x, 1 - y, c)).wait_recv()
            _rcopy(outs[w].at[relay_src], outs[w].at[relay_src], sems, w, 3, relay_to).start()
            _rcopy(bx, bx, sems, w, 4, sib).start()
            _rcopy(by, by, sems, w, 5, sib).start()

    def last(self, ins, outs, sems):
        x, y, c, xn, yn, _, relay_to, dg = self._places()
        me = 4 * x + 2 * y + c
        sib = (x, y, 1 - c)
        for w in range(len(ins)):
            bd = outs[w].at[dg + c]
            _rcopy(ins[w], bd, sems, w, 3, relay_to).wait_recv()
            _rcopy(bd, bd, sems, w, 6, sib).start()
            for k, blk in ((0, 4 * x + 2 * y), (4, xn), (5, yn), (6, dg)):
                _rcopy(ins[w], outs[w].at[blk + 1 - c], sems, w, k, sib).wait_recv()
            for k in range(7):
                _rcopy(ins[w], outs[w].at[me], sems, w, k, sib).wait_send()
            pltpu.make_async_copy(ins[w], outs[w].at[me], sems[2].at[w]).wait()


class _GatherOwnPlan:
    mid = None

    def __init__(self, blocks, rows=None, into=None):
        self.n = len(blocks)
        self.rows = rows
        self.ins = list(blocks) + list(into or [])
        self.out_shapes = [_sds((N_DEV,) + b.shape, b.dtype) for b in blocks]
        self.aliases = [(self.n + i, i) for i in range(len(into or []))]
        n = self.n
        self.scratch = [pltpu.SemaphoreType.DMA((n, 4)), pltpu.SemaphoreType.DMA((n, 4)), pltpu.SemaphoreType.DMA((n,))]

    def _cut(self, ref):
        return ref if self.rows is None else ref.at[pl.ds(self.rows[0], self.rows[1] - self.rows[0])]

    def first(self, ins, outs, sems):
        x, y, c = _coords()
        me = 4 * x + 2 * y + c
        for w in range(self.n):
            src, dst = self._cut(ins[w]), self._cut(outs[w].at[me])
            pltpu.make_async_copy(src, dst, sems[2].at[w]).start()
            _rcopy(src, dst, sems, w, 0, (x, y, 1 - c)).start()
            for r in (1, 2, 3):
                px, py = _chip_rel(x, y, r)
                _rcopy(src, dst, sems, w, r, (px, py, c)).start()

    def last(self, ins, outs, sems):
        x, y, c = _coords()
        me = 4 * x + 2 * y + c
        for w in range(self.n):
            src = self._cut(ins[w])
            cp = _rcopy(src, self._cut(outs[w].at[4 * x + 2 * y + 1 - c]), sems, w, 0, (x, y, 1 - c))
            cp.wait_recv()
            cp.wait_send()
            for r in (1, 2, 3):
                px, py = _chip_rel(x, y, r)
                cp = _rcopy(src, self._cut(outs[w].at[4 * px + 2 * py + c]), sems, w, r, (px, py, c))
                cp.wait_recv()
                cp.wait_send()
            pltpu.make_async_copy(src, self._cut(outs[w].at[me]), sems[2].at[w]).wait()


class _GatherPassPlan:
    mid = None

    def __init__(self, gathered):
        self.ins = list(gathered)
        self.out_shapes = [_sds(g.shape, g.dtype) for g in gathered]
        self.aliases = [(i, i) for i in range(len(gathered))]
        n = len(gathered)
        self.scratch = [pltpu.SemaphoreType.DMA((n, 3)), pltpu.SemaphoreType.DMA((n, 3))]

    def first(self, ins, outs, sems):
        x, y, c = _coords()
        for w in range(len(ins)):
            for r in (1, 2, 3):
                px, py = _chip_rel(x, y, r)
                blk = 4 * px + 2 * py + c
                _rcopy(ins[w].at[blk], outs[w].at[blk], sems, w, r - 1, (x, y, 1 - c)).start()

    def last(self, ins, outs, sems):
        x, y, c = _coords()
        for w in range(len(ins)):
            for r in (1, 2, 3):
                px, py = _chip_rel(x, y, r)
                blk = 4 * px + 2 * py + 1 - c
                cp = _rcopy(ins[w].at[blk], outs[w].at[blk], sems, w, r - 1, (x, y, 1 - c))
                cp.wait_recv()
                cp.wait_send()


class _DirectGatherPlan:
    mid = None

    def __init__(self, blocks):
        self.ins = list(blocks)
        self.out_shapes = [_sds((N_DEV,) + b.shape, b.dtype) for b in blocks]
        n = len(blocks)
        self.scratch = [pltpu.SemaphoreType.DMA((n, 7)), pltpu.SemaphoreType.DMA((n, 7)), pltpu.SemaphoreType.DMA((n,))]

    @staticmethod
    def _peer(x, y, c, r):
        return (1 - x if r & 4 else x), (1 - y if r & 2 else y), (1 - c if r & 1 else c)

    def first(self, ins, outs, sems):
        x, y, c = _coords()
        me = 4 * x + 2 * y + c
        for w in range(len(ins)):
            pltpu.make_async_copy(ins[w], outs[w].at[me], sems[2].at[w]).start()
            for r in range(1, N_DEV):
                _rcopy(ins[w], outs[w].at[me], sems, w, r - 1, self._peer(x, y, c, r)).start()

    def last(self, ins, outs, sems):
        x, y, c = _coords()
        me = 4 * x + 2 * y + c
        for w in range(len(ins)):
            for r in range(1, N_DEV):
                px, py, pc = self._peer(x, y, c, r)
                cp = _rcopy(ins[w], outs[w].at[4 * px + 2 * py + pc], sems, w, r - 1, (px, py, pc))
                cp.wait_recv()
                cp.wait_send()
            pltpu.make_async_copy(ins[w], outs[w].at[me], sems[2].at[w]).wait()


class _PairScatterPlan:
    mid = None

    def __init__(self, pieces, rows=None, into=None):
        self.n = len(pieces)
        self.rows = rows
        self.ins = list(pieces) + list(into or [])
        self.out_shapes = [_sds((N_CHIP,) + p.shape[1:], p.dtype) for p in pieces]
        self.aliases = [(self.n + i, i) for i in range(len(into or []))]
        self.scratch = [pltpu.SemaphoreType.DMA((self.n, N_CHIP)), pltpu.SemaphoreType.DMA((self.n, N_CHIP))]

    def _copies(self, ins, outs, sems):
        x, y, c = _coords()
        cps = []
        for w in range(self.n):
            for q in range(N_CHIP):
                src, dst = ins[w].at[2 * q + 1 - c], outs[w].at[q]
                if self.rows is not None:
                    cut = pl.ds(self.rows[0], self.rows[1] - self.rows[0])
                    src, dst = src.at[cut], dst.at[cut]
                cps.append(_rcopy(src, dst, sems, w, q, (x, y, 1 - c)))
        return cps

    def first(self, ins, outs, sems):
        for cp in self._copies(ins, outs, sems):
            cp.start()

    def last(self, ins, outs, sems):
        for cp in self._copies(ins, outs, sems):
            cp.wait_recv()
            cp.wait_send()


class _ChipScatterPlan:
    mid = None

    def __init__(self, sums, rows=None, into=None):
        self.n = len(sums)
        self.rows = rows
        self.ins = list(sums) + list(into or [])
        self.out_shapes = [_sds(s.shape, s.dtype) for s in sums]
        self.aliases = [(self.n + i, i) for i in range(len(into or []))]
        self.scratch = [pltpu.SemaphoreType.DMA((self.n, 3)), pltpu.SemaphoreType.DMA((self.n, 3))]

    def _copies(self, ins, outs, sems):
        x, y, c = _coords()
        cps = []
        for w in range(self.n):
            for r in (1, 2, 3):
                px, py = _chip_rel(x, y, r)
                src, dst = ins[w].at[r - 1], outs[w].at[r - 1]
                if self.rows is not None:
                    cut = pl.ds(self.rows[0], self.rows[1] - self.rows[0])
                    src, dst = src.at[cut], dst.at[cut]
                cps.append(_rcopy(src, dst, sems, w, r - 1, (px, py, c)))
        return cps

    def first(self, ins, outs, sems):
        for cp in self._copies(ins, outs, sems):
            cp.start()

    def last(self, ins, outs, sems):
        for cp in self._copies(ins, outs, sems):
            cp.wait_recv()
            cp.wait_send()


class _Comm:
    def __init__(self, plans):
        self.plans = list(plans)
        self.ins = [a for p in self.plans for a in p.ins]
        self.out_shapes = [s for p in self.plans for s in p.out_shapes]
        self.scratch = [s for p in self.plans for s in p.scratch]
        self.aliases = []
        i = o = 0
        for p in self.plans:
            self.aliases += [(i + a, o + b) for a, b in getattr(p, "aliases", [])]
            i, o = i + len(p.ins), o + len(p.out_shapes)

    def _parts(self, ins, outs, sems):
        i = o = s = 0
        for p in self.plans:
            yield p, ins[i:i + len(p.ins)], outs[o:o + len(p.out_shapes)], sems[s:s + len(p.scratch)]
            i, o, s = i + len(p.ins), o + len(p.out_shapes), s + len(p.scratch)

    def begin(self, step, nsteps, ins, outs, sems):
        @pl.when(step == 0)
        def _():
            for p, pi, po, ps in self._parts(ins, outs, sems):
                p.first(pi, po, ps)

        for p, pi, po, ps in self._parts(ins, outs, sems):
            if p.mid is not None:
                @pl.when(step == min(nsteps - 1, int(p.mid_frac * nsteps)))
                def _(p=p, pi=pi, po=po, ps=ps):
                    p.mid(pi, po, ps)

    def end(self, step, nsteps, ins, outs, sems):
        @pl.when(step == nsteps - 1)
        def _():
            for p, pi, po, ps in self._parts(ins, outs, sems):
                p.last(pi, po, ps)


def _call(body, args, *, name, grid, in_specs, out_specs, out_shape, scratch_shapes=(), sem=None, comm=None):
    in_specs, out_specs, out_shape, scratch_shapes = list(in_specs), list(out_specs), list(out_shape), list(scratch_shapes)
    if comm is None:
        res = pl.pallas_call(
            body, name=name, grid=grid, in_specs=in_specs, out_specs=out_specs, out_shape=out_shape,
            scratch_shapes=scratch_shapes,
            compiler_params=pltpu.CompilerParams(dimension_semantics=sem, vmem_limit_bytes=VMEM_LIMIT_BYTES),
        )(*args)
        return list(res), []
    n_in, n_out, n_sc = len(in_specs), len(out_specs), len(scratch_shapes)
    n_ci, n_co = len(comm.ins), len(comm.out_shapes)
    nsteps = math.prod(grid)

    def hosted(*refs):
        ins, cins = refs[:n_in], refs[n_in:n_in + n_ci]
        o0 = n_in + n_ci
        outs, couts = refs[o0:o0 + n_out], refs[o0 + n_out:o0 + n_out + n_co]
        s0 = o0 + n_out + n_co
        scr, csems = refs[s0:s0 + n_sc], refs[s0 + n_sc:]
        step = jnp.int32(0)
        for d in range(len(grid)):
            step = step * grid[d] + pl.program_id(d)
        comm.begin(step, nsteps, cins, couts, csems)
        body(*ins, *outs, *scr)
        comm.end(step, nsteps, cins, couts, csems)

    res = pl.pallas_call(
        hosted, name=name, grid=grid, in_specs=in_specs + [ANY] * n_ci, out_specs=out_specs + [ANY] * n_co,
        out_shape=out_shape + comm.out_shapes, scratch_shapes=scratch_shapes + comm.scratch,
        input_output_aliases={n_in + a: n_out + b for a, b in comm.aliases},
        compiler_params=pltpu.CompilerParams(dimension_semantics=("arbitrary",) * len(grid),
                                             vmem_limit_bytes=VMEM_LIMIT_BYTES, has_side_effects=True),
    )(*args, *comm.ins)
    return list(res[:n_out]), list(res[n_out:])


def _exchange_alone(comm, name):
    def body():
        pass

    return _call(body, [], name=name, grid=(), in_specs=[], out_specs=[], out_shape=[], comm=comm)[1]


def _matmul(a, b, *, mode, name, out_dtype=F32, out_blocks=None, tm=None, tn=None, tk=None, comm=None):
    tm = MM_TILE if tm is None else tm
    tn = MM_TILE if tn is None else tn
    a_blk = a.ndim == 3
    b_blk = b.ndim == 3
    if mode == "nn":
        M, K = a.shape
        N = b.shape[0] * b.shape[2] if b_blk else b.shape[1]
        dims = (((1,), (0,)), ((), ()))
    elif mode == "nt":
        M = a.shape[1] if a_blk else a.shape[0]
        K = a.shape[0] * a.shape[2] if a_blk else a.shape[1]
        N = b.shape[1] if b_blk else b.shape[0]
        dims = (((1,), (1,)), ((), ()))
    else:
        K, M = a.shape
        N = b.shape[0] * b.shape[2] if b_blk else b.shape[1]
        dims = (((0,), (0,)), ((), ()))

    tm = _tile(M, tm)
    tn = _tile(N, tn)
    if mode == "nt" and (a_blk or b_blk):
        tk = a.shape[2] if a_blk else b.shape[2]
    else:
        tk = _tile(K, K if tk is None else tk)
    if mode != "nt" and b_blk:
        tn = _tile(b.shape[2], tn)
    if out_blocks is not None:
        tn = _tile(out_blocks, tn)
    nk = K // tk
    grid = (M // tm, N // tn, nk)

    if mode == "nn":
        a_spec = pl.BlockSpec((tm, tk), lambda i, j, k: (i, k))
        if b_blk:
            rb = b.shape[2] // tn
            b_spec = pl.BlockSpec((None, tk, tn), lambda i, j, k: (j // rb, k, j % rb))
        else:
            b_spec = pl.BlockSpec((tk, tn), lambda i, j, k: (k, j))
    elif mode == "nt":
        if a_blk:
            a_spec = pl.BlockSpec((None, tm, tk), lambda i, j, k: (k, i, 0))
        else:
            a_spec = pl.BlockSpec((tm, tk), lambda i, j, k: (i, k))
        if b_blk:
            b_spec = pl.BlockSpec((None, tn, tk), lambda i, j, k: (k, j, 0))
        else:
            b_spec = pl.BlockSpec((tn, tk), lambda i, j, k: (j, k))
    else:
        a_spec = pl.BlockSpec((tk, tm), lambda i, j, k: (k, i))
        if b_blk:
            rb = b.shape[2] // tn
            b_spec = pl.BlockSpec((None, tk, tn), lambda i, j, k: (j // rb, k, j % rb))
        else:
            b_spec = pl.BlockSpec((tk, tn), lambda i, j, k: (k, j))

    if out_blocks is None:
        o_spec = pl.BlockSpec((tm, tn), lambda i, j, k: (i, j))
        o_shape = _sds((M, N), out_dtype)
    else:
        ro = out_blocks // tn
        o_spec = pl.BlockSpec((None, tm, tn), lambda i, j, k: (j // ro, i, j % ro))
        o_shape = _sds((N // out_blocks, M, out_blocks), out_dtype)

    direct = nk == 1 or out_dtype == F32

    def body(a_ref, b_ref, o_ref, *scratch):
        if nk == 1:
            o_ref[...] = lax.dot_general(a_ref[...], b_ref[...], dims, preferred_element_type=F32).astype(o_ref.dtype)
            return
        acc_ref = o_ref if direct else scratch[0]
        k = pl.program_id(2)

        @pl.when(k == 0)
        def _():
            acc_ref[...] = jnp.zeros(acc_ref.shape, F32)

        acc_ref[...] += lax.dot_general(a_ref[...], b_ref[...], dims, preferred_element_type=F32)
        if not direct:
            @pl.when(k == nk - 1)
            def _():
                o_ref[...] = acc_ref[...].astype(o_ref.dtype)

    scratch = [] if direct else [pltpu.VMEM((tm, tn), F32)]
    outs, landed = _call(body, [a, b], name=name, grid=grid, in_specs=[a_spec, b_spec], out_specs=[o_spec],
                         out_shape=[o_shape], scratch_shapes=scratch, sem=("parallel", "parallel", "arbitrary"), comm=comm)
    return outs[0] if comm is None else (outs[0], landed)


def _matmul_halves(a, b, *, mode, name, tm, tn=None, tk=None, comm=None):
    if mode == "nt":
        lo, hi = a
        M, kh = lo.shape
        G, N, kb = b.shape
        half = kh // kb
        tm, tn = _tile(M, tm), _tile(N, N if tn is None else tn)
        dims = (((1,), (1,)), ((), ()))

        def body(lo_ref, hi_ref, b_ref, o_ref):
            k = pl.program_id(2)

            @pl.when(k == 0)
            def _():
                o_ref[...] = jnp.zeros(o_ref.shape, F32)

            @pl.when(k < half)
            def _():
                o_ref[...] += lax.dot_general(lo_ref[...], b_ref[...], dims, preferred_element_type=F32)

            @pl.when(k >= half)
            def _():
                o_ref[...] += lax.dot_general(hi_ref[...], b_ref[...], dims, preferred_element_type=F32)

        outs, landed = _call(
            body, [lo, hi, b], name=name, grid=(M // tm, N // tn, G),
            in_specs=[pl.BlockSpec((tm, kb), lambda i, j, k: (i, jnp.minimum(k, half - 1))),
                      pl.BlockSpec((tm, kb), lambda i, j, k: (i, jnp.maximum(k - half, 0))),
                      pl.BlockSpec((None, tn, kb), lambda i, j, k: (k, j, 0))],
            out_specs=[pl.BlockSpec((tm, tn), lambda i, j, k: (i, j))], out_shape=[_sds((M, N), F32)],
            sem=("parallel", "parallel", "arbitrary"), comm=comm)
    else:
        lo, hi = b
        K, nh = lo.shape
        M = a.shape[1]
        n = tn
        half = nh // n
        tm, tk = _tile(M, tm), _tile(K, K if tk is None else tk)
        nk = K // tk
        dims = (((0,), (0,)), ((), ()))

        def body(a_ref, lo_ref, hi_ref, o_ref, o16_ref):
            j, k = pl.program_id(1), pl.program_id(2)

            @pl.when(k == 0)
            def _():
                o_ref[...] = jnp.zeros(o_ref.shape, F32)

            @pl.when(j < half)
            def _():
                o_ref[...] += lax.dot_general(a_ref[...], lo_ref[...], dims, preferred_element_type=F32)

            @pl.when(j >= half)
            def _():
                o_ref[...] += lax.dot_general(a_ref[...], hi_ref[...], dims, preferred_element_type=F32)

            @pl.when(k == nk - 1)
            def _():
                o16_ref[...] = o_ref[...].astype(BF16)

        blocks = pl.BlockSpec((None, tm, n), lambda i, j, k: (j, i, 0))
        outs, landed = _call(
            body, [a, lo, hi], name=name, grid=(M // tm, 2 * half, nk),
            in_specs=[pl.BlockSpec((tk, tm), lambda i, j, k: (k, i)),
                      pl.BlockSpec((tk, n), lambda i, j, k: (jnp.where(j < half, k, nk - 1), jnp.minimum(j, half - 1))),
                      pl.BlockSpec((tk, n), lambda i, j, k: (jnp.where(j >= half, k, 0), jnp.maximum(j - half, 0)))],
            out_specs=[blocks, blocks],
            out_shape=[_sds((2 * half, M, n), F32), _sds((2 * half, M, n), BF16)],
            sem=("parallel", "parallel", "arbitrary"), comm=comm)
        return (outs[0], outs[1]) if comm is None else (outs[0], outs[1], landed)
    return outs[0] if comm is None else (outs[0], landed)


def _rms(x):
    return lax.rsqrt(jnp.mean(x * x, axis=-1, keepdims=True) + EPS)


def _rms_bwd(dy, x, g):
    r = _rms(x)
    n = x * r
    dn = dy * g
    dx = r * (dn - n * jnp.mean(dn * n, axis=-1, keepdims=True))
    return dx, dy * n


def _sigmoid(x):
    return 1.0 / (1.0 + jnp.exp(-x))


def _rope_rot(t):
    return pltpu.roll(t, HALF_ROPE, 1) - pltpu.roll(t, LANES - HALF_ROPE, 1)


def _lane(shape):
    return lax.broadcasted_iota(jnp.int32, shape, 1)


def _split3(x):
    hi = x.astype(BF16).astype(F32)
    r1 = x - hi
    mid = r1.astype(BF16).astype(F32)
    lo = (r1 - mid).astype(BF16).astype(F32)
    return hi, mid, lo


def _cumsum_rows(x, reverse):
    S = x.shape[0]
    bs = min(256, S)
    nb = S // bs
    r = lax.broadcasted_iota(jnp.int32, (bs, bs), 0)
    c = lax.broadcasted_iota(jnp.int32, (bs, bs), 1)
    tri = jnp.where((c >= r) if reverse else (c <= r), 1.0, 0.0).astype(BF16)
    edge = lax.broadcasted_iota(jnp.int32, (bs, x.shape[1]), 0) == (0 if reverse else bs - 1)
    carry = jnp.zeros((1, x.shape[1]), F32)
    outs = [None] * nb
    for bi in (range(nb - 1, -1, -1) if reverse else range(nb)):
        xb = x[bi * bs:(bi + 1) * bs, :]
        acc = carry
        for term in _split3(xb):
            acc = acc + jnp.dot(tri, term.astype(BF16), preferred_element_type=F32)
        outs[bi] = acc
        carry = jnp.sum(jnp.where(edge, acc, 0.0), axis=0, keepdims=True)
    return jnp.concatenate(outs, axis=0) if nb > 1 else outs[0]


def _gelu_parts(x):
    c0 = math.sqrt(2.0 / math.pi)
    inner = c0 * (x + 0.044715 * (x * x * x))
    t = jnp.tanh(inner)
    g = 0.5 * x * (1.0 + t)
    dg = 0.5 * (1.0 + t) + 0.5 * x * (1.0 - t * t) * (c0 * (1.0 + 3.0 * 0.044715 * (x * x)))
    return g, dg


def _accumulate(ref, value, first):
    @pl.when(first)
    def _():
        ref[...] = value

    @pl.when(jnp.logical_not(first))
    def _():
        ref[...] += value


def _cast_bf16(w, name):
    R, C = w.shape
    tr = _tile(R, 512, 16)

    def body(w_ref, o_ref):
        o_ref[...] = w_ref[...].astype(BF16)

    blk = pl.BlockSpec((tr, C), lambda i: (i, 0))
    return _call(body, [w], name=name, grid=(R // tr,), in_specs=[blk], out_specs=[blk],
                 out_shape=[_sds((R, C), BF16)], sem=("parallel",))[0][0]


def _concat_cols(parts, name):
    T = parts[0].shape[-2] if parts[0].ndim == 3 else parts[0].shape[0]
    widths = [p.shape[0] * LANES if p.ndim == 3 else p.shape[1] for p in parts]
    tm = _tile(T, ROW_TILE, 16)

    def body(*refs):
        o_ref = refs[-1]
        off = 0
        for p_ref, p, w in zip(refs[:-1], parts, widths):
            if p.ndim == 3:
                for hd in range(p.shape[0]):
                    o_ref[:, off + hd * LANES:off + (hd + 1) * LANES] = p_ref[hd].astype(BF16)
            else:
                o_ref[:, off:off + w] = p_ref[...].astype(BF16)
            off += w

    def spec(p, w):
        if p.ndim == 3:
            return pl.BlockSpec((p.shape[0], tm, LANES), lambda i: (0, i, 0))
        return pl.BlockSpec((tm, w), lambda i: (i, 0))

    return _call(body, parts, name=name, grid=(T // tm,),
                 in_specs=[spec(p, w) for p, w in zip(parts, widths)],
                 out_specs=[pl.BlockSpec((tm, sum(widths)), lambda i: (i, 0))],
                 out_shape=[_sds((T, sum(widths)), BF16)], sem=("parallel",))[0][0]


def _prenorm(x, g, comm=None):
    T, D = x.shape
    tm = _tile(T, ROW_TILE, 16)

    def body(x_ref, g_ref, h_ref):
        xv = x_ref[...]
        h_ref[...] = (xv * _rms(xv) * g_ref[...]).astype(BF16)

    row = pl.BlockSpec((tm, D), lambda i: (i, 0))
    (h,), landed = _call(body, [x, g], name="prenorm", grid=(T // tm,),
                         in_specs=[row, pl.BlockSpec((1, D), lambda i: (0, 0))], out_specs=[row],
                         out_shape=[_sds((T, D), BF16)], sem=("parallel",), comm=comm)
    return h, landed


def _split_prep(proj, pos, invf, gq, gkv, bfor, lay, comm=None):
    T = proj.shape[0]
    tm = _tile(T, ROW_TILE, 16)

    def body(q_ref, kv_ref, kpe_ref, fl_ref, pos_ref, invf_ref, gq_ref, gkv_ref, bf_ref,
             qn_ref, kvn_ref, kper_ref, logf_ref, cos_ref, sin_ref):
        ql = q_ref[...]
        qn_ref[...] = (ql * _rms(ql) * gq_ref[...]).astype(BF16)
        kl = kv_ref[...]
        kvn_ref[...] = (kl * _rms(kl) * gkv_ref[...]).astype(BF16)
        ang = pos_ref[...].astype(F32) * invf_ref[...]
        valid = _lane(ang.shape) < ROPE
        cs = jnp.where(valid, jnp.cos(ang), 0.0)
        sn = jnp.where(valid, jnp.sin(ang), 0.0)
        cos_ref[...] = cs
        sin_ref[...] = sn
        kp = jnp.where(valid, kpe_ref[...], 0.0)
        kper_ref[...] = (kp * cs + _rope_rot(kp) * sn).astype(BF16)
        z = fl_ref[...] + bf_ref[...]
        logf_ref[...] = jnp.minimum(z, 0.0) - jnp.log(1.0 + jnp.exp(-jnp.abs(z)))

    def col(width, off):
        return pl.BlockSpec((tm, width), lambda i: (i, off // width))

    def vec(width):
        return pl.BlockSpec((1, width), lambda i: (0, 0))

    def out(width):
        return pl.BlockSpec((tm, width), lambda i: (i, 0))

    return _call(
        body, [proj, proj, proj, proj, pos, invf, gq, gkv, bfor], name="split_prep", grid=(T // tm,),
        in_specs=[col(Q_LORA, lay["q"]), col(KV_LORA, lay["kv"]), col(LANES, lay["kpe"]), col(LANES, lay["fl"]),
                  pl.BlockSpec((tm, 1), lambda i: (i, 0)), vec(LANES), vec(Q_LORA), vec(KV_LORA), vec(LANES)],
        out_specs=[out(Q_LORA), out(KV_LORA), out(LANES), out(LANES), out(LANES), out(LANES)],
        out_shape=[_sds((T, Q_LORA), BF16), _sds((T, KV_LORA), BF16), _sds((T, LANES), BF16),
                   _sds((T, LANES), F32), _sds((T, LANES), F32), _sds((T, LANES), F32)],
        sem=("parallel",), comm=comm)


def _mla_prep(qraw, kvraw, kper, cosT, sinT, comm=None):
    H, T, _ = qraw.shape
    tm = _tile(T, HEAD_ROW_TILE, 16)

    def body(q_ref, kv_ref, kpe_ref, cos_ref, sin_ref, qo_ref, ko_ref, vo_ref):
        q = q_ref[...]
        pe = q[:, NOPE:]
        pe = jnp.where(_lane(pe.shape) < ROPE, pe, 0.0)
        qo_ref[:, :NOPE] = q[:, :NOPE].astype(BF16)
        qo_ref[:, NOPE:] = (pe * cos_ref[...] + _rope_rot(pe) * sin_ref[...]).astype(BF16)
        kv = kv_ref[...]
        ko_ref[:, :NOPE] = kv[:, :NOPE].astype(BF16)
        ko_ref[:, NOPE:] = kpe_ref[...]
        vo_ref[...] = kv[:, NOPE:].astype(BF16)

    head = pl.BlockSpec((None, tm, ATT_DK), lambda h, i: (h, i, 0))
    tok = pl.BlockSpec((tm, LANES), lambda h, i: (i, 0))
    return _call(
        body, [qraw, kvraw, kper, cosT, sinT], name="mla_prep", grid=(H, T // tm),
        in_specs=[head, head, tok, tok, tok],
        out_specs=[head, head, pl.BlockSpec((None, tm, VDIM), lambda h, i: (h, i, 0))],
        out_shape=[_sds((H, T, ATT_DK), BF16), _sds((H, T, ATT_DK), BF16), _sds((H, T, VDIM), BF16)],
        sem=("parallel", "parallel"), comm=comm)


def _fox_cumsum(logf, B, S, inv_scale):
    T = logf.shape[0]

    def body(l_ref, c_ref):
        c_ref[...] = _cumsum_rows(l_ref[...], reverse=False) * inv_scale

    seq = pl.BlockSpec((S, LANES), lambda b: (b, 0))
    return _call(body, [logf], name="fox_cumsum", grid=(B,), in_specs=[seq], out_specs=[seq],
                 out_shape=[_sds((T, LANES), F32)], sem=("parallel",))[0][0]


def _fox_prep(proj, cs, lay, comm=None):
    T = proj.shape[0]
    tm = _tile(T, HEAD_ROW_TILE, 16)

    def body(q_ref, k_ref, v_ref, cs_ref, qo_ref, ko_ref, vo_ref):
        h = pl.program_id(0)
        cv = cs_ref[...]
        lane = _lane(cv.shape)
        ccol = jnp.sum(jnp.where(lane == h, cv, 0.0), axis=1, keepdims=True)
        hi, mid, lo = _split3(ccol)
        one = jnp.where(lane < 6, 1.0, 0.0)
        augq = jnp.where(lane == 0, hi, jnp.where(lane == 1, mid, jnp.where(lane == 2, lo, one)))
        augk = jnp.where(lane < 3, 1.0, jnp.where(lane == 3, -hi, jnp.where(lane == 4, -mid, jnp.where(lane == 5, -lo, 0.0))))
        qo_ref[:, :FOX_DIM] = q_ref[...].astype(BF16)
        qo_ref[:, FOX_DIM:] = augq.astype(BF16)
        ko_ref[:, :FOX_DIM] = k_ref[...].astype(BF16)
        ko_ref[:, FOX_DIM:] = augk.astype(BF16)
        vo_ref[...] = v_ref[...].astype(BF16)

    def col(off):
        return pl.BlockSpec((tm, FOX_DIM), lambda h, i: (i, off // FOX_DIM + h))

    head = pl.BlockSpec((None, tm, ATT_DK), lambda h, i: (h, i, 0))
    return _call(
        body, [proj, proj, proj, cs], name="fox_prep", grid=(HEADS, T // tm),
        in_specs=[col(lay["fq"]), col(lay["fk"]), col(lay["fv"]), pl.BlockSpec((tm, LANES), lambda h, i: (i, 0))],
        out_specs=[head, head, pl.BlockSpec((None, tm, VDIM), lambda h, i: (h, i, 0))],
        out_shape=[_sds((HEADS, T, ATT_DK), BF16), _sds((HEADS, T, ATT_DK), BF16), _sds((HEADS, T, VDIM), BF16)],
        sem=("parallel", "parallel"), comm=comm)


def _visible(tq, tk, unit):
    r = lax.broadcasted_iota(jnp.int32, (tq, tk), 0)
    c = lax.broadcasted_iota(jnp.int32, (tq, tk), 1)
    sh = int(math.log2(unit))
    return lax.shift_right_logical(c, sh) <= lax.shift_right_logical(r, sh)


def _attn_fwd(streams, *, B, S, name, comm=None):
    n = len(streams)
    H, T, DK = streams[0][0].shape
    DV = streams[0][2].shape[2]
    tq = _tile(S, ATT_TILE)
    nq = S // tq
    sub = min(ATT_SUB, tq)
    NT = (((1,), (1,)), ((), ()))

    def body(*refs):
        ins, outs, (m_sc, acc_sc) = refs[:3 * n], refs[3 * n:5 * n], refs[5 * n:]
        i, j = pl.program_id(1), pl.program_id(2)

        @pl.when(j == 0)
        def _():
            m_sc[...] = jnp.full(m_sc.shape, NEG_INF, F32)
            acc_sc[...] = jnp.zeros(acc_sc.shape, F32)

        def step(diagonal):
            work = [(t, r) for r in range(tq // sub) for t in range(n)]

            def scores(t, r):
                q_ref, k_ref, _ = ins[3 * t:3 * t + 3]
                kc = (r + 1) * sub if diagonal else tq
                s = lax.dot_general(q_ref[r * sub:(r + 1) * sub, :], k_ref[0:kc, :], NT, preferred_element_type=F32)
                return s * (streams[t][4] * LOG2_E)

            ahead = [scores(*work[w]) for w in range(min(ATT_AHEAD, len(work)))]
            for w, (t, r) in enumerate(work):
                s = ahead.pop(0)
                if w + ATT_AHEAD < len(work):
                    ahead.append(scores(*work[w + ATT_AHEAD]))
                v_ref = ins[3 * t + 2]
                kc = s.shape[1]
                rows = slice(r * sub, (r + 1) * sub)
                if diagonal:
                    own = jnp.where(_visible(sub, sub, streams[t][3]), s[:, kc - sub:], NEG_INF)
                    s = own if kc == sub else jnp.concatenate([s[:, :kc - sub], own], axis=1)
                m_prev = m_sc[t, rows, :]
                mx = s[:, 0:LANES]
                for g in range(1, kc // LANES):
                    mx = jnp.maximum(mx, s[:, g * LANES:(g + 1) * LANES])
                m_new = jnp.maximum(m_prev, jnp.max(mx, axis=1, keepdims=True))
                alpha = jnp.exp2(m_prev - m_new)
                p = jnp.exp2(s - jnp.tile(m_new, (1, kc // LANES))).astype(BF16)
                v_aug = jnp.concatenate([v_ref[0:kc, :], jnp.ones((kc, LANES), BF16)], axis=1)
                acc_sc[t, rows, :] = jnp.tile(alpha, (1, 2)) * acc_sc[t, rows, :] + jnp.dot(
                    p, v_aug, preferred_element_type=F32)
                m_sc[t, rows, :] = m_new

        @pl.when(j < i)
        def _():
            step(False)

        @pl.when(j == i)
        def _():
            step(True)
            for t in range(n):
                o_ref, lse_ref = outs[2 * t:2 * t + 2]
                l = acc_sc[t, :, DV:]
                o_ref[...] = (acc_sc[t, :, :DV] / l).astype(BF16)
                lse_ref[...] = m_sc[t] + jnp.log2(l)

    def qmap(g, i, j):
        return (g % H, (g // H) * nq + i, 0)

    def kmap(g, i, j):
        return (g % H, (g // H) * nq + jnp.minimum(j, i), 0)

    args = [a for st in streams for a in st[:3]]
    outs, landed = _call(
        body, args, name=name, grid=(B * H, nq, nq),
        in_specs=[pl.BlockSpec((None, tq, DK), qmap), pl.BlockSpec((None, tq, DK), kmap),
                  pl.BlockSpec((None, tq, DV), kmap)] * n,
        out_specs=[pl.BlockSpec((tq, DV), lambda g, i, j: ((g // H) * nq + i, g % H)),
                   pl.BlockSpec((None, tq, LANES), qmap)] * n,
        out_shape=[_sds((T, H * DV), BF16), _sds((H, T, LANES), F32)] * n,
        scratch_shapes=[pltpu.VMEM((n, tq, LANES), F32), pltpu.VMEM((n, tq, DV + LANES), F32)],
        sem=("parallel", "parallel", "arbitrary"), comm=comm)
    return [(outs[2 * t], outs[2 * t + 1]) for t in range(n)], landed


def _attn_bwd(streams, *, B, S, name, comm=None):
    n = len(streams)
    H, T, DK = streams[0][0].shape
    DV = streams[0][2].shape[2]
    tq = _tile(S, ATT_TILE)
    nq = S // tq
    sub = min(ATT_SUB, tq)
    NT = (((1,), (1,)), ((), ()))
    TN = (((0,), (0,)), ((), ()))

    def body(*refs):
        ins, outs = refs[:6 * n], refs[6 * n:]
        j, i = pl.program_id(1), pl.program_id(2)

        @pl.when(jnp.logical_and(j == 0, i == 0))
        def _():
            for t in range(n):
                outs[3 * t][...] = jnp.zeros(outs[3 * t].shape, F32)

        @pl.when(i == 0)
        def _():
            for t in range(n):
                outs[3 * t + 1][...] = jnp.zeros(outs[3 * t + 1].shape, F32)
                outs[3 * t + 2][...] = jnp.zeros(outs[3 * t + 2].shape, F32)

        def step(diagonal):
            work = [(t, r) for r in range(tq // sub) for t in range(n)]

            def kcols(r):
                return (r + 1) * sub if diagonal else tq

            def scores(t, r):
                q_ref, k_ref, v_ref, _, do_ref, _ = ins[6 * t:6 * t + 6]
                rows, kc = slice(r * sub, (r + 1) * sub), kcols(r)
                s = lax.dot_general(q_ref[rows, :], k_ref[0:kc, :], NT, preferred_element_type=F32)
                dp = lax.dot_general(do_ref[rows, :], v_ref[0:kc, :], NT, preferred_element_type=F32)
                return s * (streams[t][7] * LOG2_E), dp

            def probs(t, r, s, dp):
                _, _, _, o_ref, do_ref, lse_ref = ins[6 * t:6 * t + 6]
                rows, kc = slice(r * sub, (r + 1) * sub), kcols(r)
                if diagonal:
                    own = jnp.where(_visible(sub, sub, streams[t][6]), s[:, kc - sub:], NEG_INF)
                    s = own if kc == sub else jnp.concatenate([s[:, :kc - sub], own], axis=1)
                p = jnp.exp2(s - jnp.tile(lse_ref[rows, :], (1, kc // LANES)))
                delta = jnp.sum(do_ref[rows, :].astype(F32) * o_ref[rows, :].astype(F32), axis=1, keepdims=True)
                return p.astype(BF16), (p * (dp - delta) * streams[t][7]).astype(BF16)

            def grads(t, r, p, ds):
                q_ref, k_ref, _, _, do_ref, _ = ins[6 * t:6 * t + 6]
                dq_ref, dk_ref, dv_ref = outs[3 * t:3 * t + 3]
                rows, kc = slice(r * sub, (r + 1) * sub), kcols(r)
                dv_ref[0:kc, :] += lax.dot_general(p, do_ref[rows, :], TN, preferred_element_type=F32)
                dk_ref[0:kc, :] += lax.dot_general(ds, q_ref[rows, :], TN, preferred_element_type=F32)
                qrows = pl.ds(pl.multiple_of(i * tq + r * sub, sub), sub)
                dq_ref[qrows, :] += jnp.dot(ds, k_ref[0:kc, :], preferred_element_type=F32)

            nw = len(work)
            sc = {w: scores(*work[w]) for w in range(min(2, nw))}
            pr = {0: probs(*work[0], *sc.pop(0))}
            for w in range(nw):
                if w + 2 < nw:
                    sc[w + 2] = scores(*work[w + 2])
                if w + 1 < nw:
                    pr[w + 1] = probs(*work[w + 1], *sc.pop(w + 1))
                grads(*work[w], *pr.pop(w))

        @pl.when(i > j)
        def _():
            step(False)

        @pl.when(i == j)
        def _():
            step(True)

    def qmap(g, j, i):
        return (g % H, (g // H) * nq + jnp.maximum(i, j), 0)

    def kmap(g, j, i):
        return (g % H, (g // H) * nq + j, 0)

    def omap(g, j, i):
        return ((g // H) * nq + jnp.maximum(i, j), g % H)

    args = [a for st in streams for a in st[:6]]
    outs, landed = _call(
        body, args, name=name, grid=(B * H, nq, nq),
        in_specs=[pl.BlockSpec((None, tq, DK), qmap), pl.BlockSpec((None, tq, DK), kmap),
                  pl.BlockSpec((None, tq, DV), kmap), pl.BlockSpec((tq, DV), omap), pl.BlockSpec((tq, DV), omap),
                  pl.BlockSpec((None, tq, LANES), qmap)] * n,
        out_specs=[pl.BlockSpec((None, S, DK), lambda g, j, i: (g % H, g // H, 0)),
                   pl.BlockSpec((None, tq, DK), kmap), pl.BlockSpec((None, tq, DV), kmap)] * n,
        out_shape=[_sds((H, T, DK), F32), _sds((H, T, DK), F32), _sds((H, T, DV), F32)] * n,
        sem=("parallel", "arbitrary", "arbitrary"), comm=comm)
    return [tuple(outs[3 * t:3 * t + 3]) for t in range(n)], landed


def _gate_merge(am, af, proj, bgate, lay, D, comm=None):
    T = am.shape[0]
    tm = _tile(T, ROW_TILE, 16)
    tn = _tile(D, 1024)

    def body(am_ref, af_ref, gm_ref, gf_ref, bm_ref, bf_ref, o_ref):
        sm = _sigmoid(gm_ref[...] + bm_ref[...])
        sf = _sigmoid(gf_ref[...] + bf_ref[...])
        o_ref[...] = (sm * am_ref[...] + sf * af_ref[...]).astype(BF16)

    og = lay["g"] // tn
    blk = pl.BlockSpec((tm, tn), lambda i, j: (i, j))
    return _call(
        body, [am, af, proj, proj, bgate, bgate], name="gate_merge", grid=(T // tm, D // tn),
        in_specs=[blk, blk, pl.BlockSpec((tm, tn), lambda i, j: (i, og + j)),
                  pl.BlockSpec((tm, tn), lambda i, j: (i, og + D // tn + j)),
                  pl.BlockSpec((1, tn), lambda i, j: (0, j)), pl.BlockSpec((1, tn), lambda i, j: (0, D // tn + j))],
        out_specs=[blk], out_shape=[_sds((T, D), BF16)], sem=("parallel", "parallel"), comm=comm)


def _mid(x, y1, g_pm, g_ffn):
    T, D = x.shape
    tm = _tile(T, ROW_TILE, 16)

    def body(x_ref, y_ref, gp_ref, gf_ref, x1_ref, h2_ref):
        y = y_ref[...]
        x1 = x_ref[...] + y * _rms(y) * gp_ref[...]
        x1_ref[...] = x1
        h2_ref[...] = (x1 * _rms(x1) * gf_ref[...]).astype(BF16)

    row = pl.BlockSpec((tm, D), lambda i: (i, 0))
    vec = pl.BlockSpec((1, D), lambda i: (0, 0))
    return _call(body, [x, y1, g_pm, g_ffn], name="mid", grid=(T // tm,), in_specs=[row, row, vec, vec],
                 out_specs=[row, row], out_shape=[_sds((T, D), F32), _sds((T, D), BF16)], sem=("parallel",))[0]


def _conv3(u, w_ref, bias):
    row = lax.broadcasted_iota(jnp.int32, u.shape, 0)
    u1 = jnp.where(row >= 1, pltpu.roll(u, 1, 0), 0.0)
    u2 = jnp.where(row >= 2, pltpu.roll(u, 2, 0), 0.0)
    return w_ref[0:1, :] * u2 + w_ref[1:2, :] * u1 + w_ref[2:3, :] * u + bias, u1, u2


def _convffn_fwd(u, cw, cb, B, S, F, comm=None):
    T = u.shape[0]
    tn = _tile(F, 256)
    nf = F // tn

    def body(ug_ref, uv_ref, wg_ref, wv_ref, bg_ref, bv_ref, a_ref):
        g, _, _ = _conv3(ug_ref[...], wg_ref, bg_ref[...])
        val, _, _ = _conv3(uv_ref[...], wv_ref, bv_ref[...])
        a_ref[...] = (_gelu_parts(g)[0] * val).astype(BF16)

    def seq(off):
        return pl.BlockSpec((S, tn), lambda b, j: (b, off + j))

    def par(rows, off):
        return pl.BlockSpec((rows, tn), lambda b, j: (0, off + j))

    return _call(body, [u, u, cw, cw, cb, cb], name="convffn_fwd", grid=(B, nf),
                 in_specs=[seq(0), seq(nf), par(3, 0), par(3, nf), par(1, 0), par(1, nf)],
                 out_specs=[seq(0)], out_shape=[_sds((T, F), BF16)], sem=("parallel", "parallel"), comm=comm)


def _convffn_bwd(u, dact, cw, cb, B, S, F, comm=None):
    T = u.shape[0]
    tn = _tile(F, 256)
    nf = F // tn

    def body(ug_ref, uv_ref, da_ref, wg_ref, wv_ref, bg_ref, bv_ref, dug_ref, duv_ref, dpg_ref, dpv_ref):
        b = pl.program_id(1)
        ug, uv, da = ug_ref[...], uv_ref[...], da_ref[...]
        g, ug1, ug2 = _conv3(ug, wg_ref, bg_ref[...])
        val, uv1, uv2 = _conv3(uv, wv_ref, bv_ref[...])
        gel, dgel = _gelu_parts(g)
        dg = da * val * dgel
        dval = da * gel
        row = lax.broadcasted_iota(jnp.int32, ug.shape, 0)

        def back(d, w_ref):
            d1 = jnp.where(row < S - 1, pltpu.roll(d, S - 1, 0), 0.0)
            d2 = jnp.where(row < S - 2, pltpu.roll(d, S - 2, 0), 0.0)
            return w_ref[2:3, :] * d + w_ref[1:2, :] * d1 + w_ref[0:1, :] * d2

        dug_ref[...] = back(dg, wg_ref).astype(BF16)
        duv_ref[...] = back(dval, wv_ref).astype(BF16)

        def sums(d, u0, u1, u2):
            r8 = lax.broadcasted_iota(jnp.int32, (8, d.shape[1]), 0)
            out = jnp.zeros((8, d.shape[1]), F32)
            for k, t in enumerate((d * u2, d * u1, d * u0, d)):
                out = jnp.where(r8 == k, jnp.sum(t, axis=0, keepdims=True), out)
            return out

        _accumulate(dpg_ref, sums(dg, ug, ug1, ug2), b == 0)
        _accumulate(dpv_ref, sums(dval, uv, uv1, uv2), b == 0)

    def seq(off):
        return pl.BlockSpec((S, tn), lambda j, b: (b, off + j))

    def par(rows, off):
        return pl.BlockSpec((rows, tn), lambda j, b: (0, off + j))

    outs, landed = _call(
        body, [u, u, dact, cw, cw, cb, cb], name="convffn_bwd", grid=(nf, B),
        in_specs=[seq(0), seq(nf), seq(0), par(3, 0), par(3, nf), par(1, 0), par(1, nf)],
        out_specs=[seq(0), seq(0), par(8, 0), par(8, 0)],
        out_shape=[_sds((T, F), BF16), _sds((T, F), BF16), _sds((8, F), F32), _sds((8, F), F32)],
        sem=("parallel", "arbitrary"), comm=comm)
    return outs, landed


def _tail(ff, x1, tgt, g):
    T, D = ff.shape
    tm = _tile(T, ROW_TILE, 16)

    def body(ff_ref, x1_ref, t_ref, g_ref, dy_ref, dff_ref, loss_ref, dg_ref):
        i = pl.program_id(0)
        f = ff_ref[...]
        gv = g_ref[...]
        r = _rms(f)
        n = f * r
        e = (x1_ref[...] + n * gv) - t_ref[...]
        dy = e * (1.0 / D)
        dy_ref[...] = dy
        dn = dy * gv
        dff_ref[...] = (r * (dn - n * jnp.mean(dn * n, axis=-1, keepdims=True))).astype(BF16)
        part = 0.5 * jnp.sum(jnp.mean(e * e, axis=-1, keepdims=True), axis=0, keepdims=True)
        _accumulate(loss_ref, jnp.broadcast_to(part, loss_ref.shape), i == 0)
        _accumulate(dg_ref, jnp.sum(dy * n, axis=0, keepdims=True), i == 0)

    row = pl.BlockSpec((tm, D), lambda i: (i, 0))
    vec = pl.BlockSpec((1, D), lambda i: (0, 0))
    return _call(body, [ff, x1, tgt, g], name="tail", grid=(T // tm,), in_specs=[row, row, row, vec],
                 out_specs=[row, row, pl.BlockSpec((8, LANES), lambda i: (0, 0)), vec],
                 out_shape=[_sds((T, D), F32), _sds((T, D), BF16), _sds((8, LANES), F32), _sds((1, D), F32)],
                 sem=("arbitrary",))[0]


def _mid_bwd(dy, dh2, x1, y1, g_ffn, g_pm, comm=None):
    T, D = dy.shape
    tm = _tile(T, ROW_TILE, 16)

    def body(dy_ref, dh_ref, x1_ref, y1_ref, gf_ref, gp_ref, dx1_ref, dy1_ref, dgf_ref, dgp_ref):
        i = pl.program_id(0)
        dh = dh_ref[...]
        d2, dgf = _rms_bwd(dh, x1_ref[...], gf_ref[...])
        dx1 = dy_ref[...] + d2
        dx1_ref[...] = dx1
        d1, dgp = _rms_bwd(dx1, y1_ref[...], gp_ref[...])
        dy1_ref[...] = d1.astype(BF16)
        _accumulate(dgf_ref, jnp.sum(dgf, axis=0, keepdims=True), i == 0)
        _accumulate(dgp_ref, jnp.sum(dgp, axis=0, keepdims=True), i == 0)

    row = pl.BlockSpec((tm, D), lambda i: (i, 0))
    vec = pl.BlockSpec((1, D), lambda i: (0, 0))
    return _call(body, [dy, dh2, x1, y1, g_ffn, g_pm], name="mid_bwd", grid=(T // tm,),
                 in_specs=[row, row, row, row, vec, vec], out_specs=[row, row, vec, vec],
                 out_shape=[_sds((T, D), F32), _sds((T, D), BF16), _sds((1, D), F32), _sds((1, D), F32)],
                 sem=("arbitrary",), comm=comm)


def _gate_bwd(dm, am, af, proj, bgate, lay, D, comm=None):
    T = dm.shape[0]
    tm = _tile(T, ROW_TILE, 16)
    tn = _tile(D, 512)

    def body(dm_ref, am_ref, af_ref, gm_ref, gf_ref, bm_ref, bf_ref,
             dam_ref, daf_ref, dgm_ref, dgf_ref, dbm_ref, dbf_ref):
        i = pl.program_id(1)
        d = dm_ref[...]
        sm = _sigmoid(gm_ref[...] + bm_ref[...])
        sf = _sigmoid(gf_ref[...] + bf_ref[...])
        dam_ref[...] = (d * sm).astype(BF16)
        daf_ref[...] = (d * sf).astype(BF16)
        dgm = d * am_ref[...] * (sm * (1.0 - sm))
        dgf = d * af_ref[...] * (sf * (1.0 - sf))
        dgm_ref[...] = dgm.astype(BF16)
        dgf_ref[...] = dgf.astype(BF16)
        _accumulate(dbm_ref, jnp.sum(dgm, axis=0, keepdims=True), i == 0)
        _accumulate(dbf_ref, jnp.sum(dgf, axis=0, keepdims=True), i == 0)

    og = lay["g"] // tn
    blk = pl.BlockSpec((tm, tn), lambda j, i: (i, j))
    vec = pl.BlockSpec((1, tn), lambda j, i: (0, j))
    return _call(
        body, [dm, am, af, proj, proj, bgate, bgate], name="gate_bwd", grid=(D // tn, T // tm),
        in_specs=[blk, blk, blk, pl.BlockSpec((tm, tn), lambda j, i: (i, og + j)),
                  pl.BlockSpec((tm, tn), lambda j, i: (i, og + D // tn + j)),
                  vec, pl.BlockSpec((1, tn), lambda j, i: (0, D // tn + j))],
        out_specs=[blk, blk, blk, blk, vec, vec],
        out_shape=[_sds((T, D), BF16)] * 4 + [_sds((1, D), F32)] * 2, sem=("parallel", "arbitrary"), comm=comm)


def _mla_bwd_prep(dq, dk, dv, cosT, sinT, comm=None):
    H, T, _ = dq.shape
    tm = _tile(T, HEAD_ROW_TILE, 16)

    def body(dq_ref, dk_ref, dv_ref, cos_ref, sin_ref, dqr_ref, dkv_ref, dkpe_ref):
        h = pl.program_id(1)
        cs, sn = cos_ref[...], sin_ref[...]
        valid = _lane(cs.shape) < ROPE

        def unrope(d):
            d = jnp.where(valid, d, 0.0)
            return d * cs - _rope_rot(d) * sn

        dqv = dq_ref[...]
        dqr_ref[:, :NOPE] = dqv[:, :NOPE].astype(BF16)
        dqr_ref[:, NOPE:] = unrope(dqv[:, NOPE:]).astype(BF16)
        dkv_ = dk_ref[...]
        dkv_ref[:, :NOPE] = dkv_[:, :NOPE].astype(BF16)
        dkv_ref[:, NOPE:] = dv_ref[...].astype(BF16)
        _accumulate(dkpe_ref, unrope(dkv_[:, NOPE:]), h == 0)

    head = pl.BlockSpec((None, tm, ATT_DK), lambda i, h: (h, i, 0))
    tok = pl.BlockSpec((tm, LANES), lambda i, h: (i, 0))
    return _call(
        body, [dq, dk, dv, cosT, sinT], name="mla_bwd_prep", grid=(T // tm, H),
        in_specs=[head, head, pl.BlockSpec((None, tm, VDIM), lambda i, h: (h, i, 0)), tok, tok],
        out_specs=[head, head, tok],
        out_shape=[_sds((H, T, ATT_DK), BF16), _sds((H, T, ATT_DK), BF16), _sds((T, LANES), F32)],
        sem=("parallel", "arbitrary"), comm=comm)


def _fox_bwd_prep(dq, dk, proj, bfor, lay, B, S, inv_scale):
    H, T, _ = dq.shape

    def body(dq_ref, dk_ref, fl_ref, bf_ref, dfl_ref, dbf_ref, dc_sc):
        b, h = pl.program_id(0), pl.program_id(1)
        lane = _lane(dc_sc.shape)
        col = jnp.sum(jnp.where(lane == 0, dq_ref[...], 0.0) - jnp.where(lane == 3, dk_ref[...], 0.0),
                      axis=1, keepdims=True)

        @pl.when(h == 0)
        def _():
            dc_sc[...] = jnp.zeros(dc_sc.shape, F32)

        dc_sc[...] = jnp.where(lane == h, col, dc_sc[...])

        @pl.when(h == H - 1)
        def _():
            dlogf = _cumsum_rows(dc_sc[...] * inv_scale, reverse=True)
            z = fl_ref[...] + bf_ref[...]
            dz = jnp.where(lane < H, dlogf * (1.0 / (1.0 + jnp.exp(z))), 0.0)
            dfl_ref[...] = dz
            _accumulate(dbf_ref, jnp.sum(dz, axis=0, keepdims=True), b == 0)

    aug = pl.BlockSpec((None, S, LANES), lambda b, h: (h, b, 1))
    seq = pl.BlockSpec((S, LANES), lambda b, h: (b, 0))
    vec = pl.BlockSpec((1, LANES), lambda b, h: (0, 0))
    return _call(
        body, [dq, dk, proj, bfor], name="fox_bwd_prep", grid=(B, H),
        in_specs=[aug, aug, pl.BlockSpec((S, LANES), lambda b, h: (b, lay["fl"] // LANES)), vec],
        out_specs=[seq, vec], out_shape=[_sds((T, LANES), F32), _sds((1, LANES), F32)],
        scratch_shapes=[pltpu.VMEM((S, LANES), F32)], sem=("arbitrary", "arbitrary"))[0]


def _lat_bwd(dqn, dkvn, proj, gq, gkv, lay):
    T = dqn.shape[0]
    tm = _tile(T, ROW_TILE, 16)

    def body(dq_ref, dkv_ref, q_ref, kv_ref, gq_ref, gkv_ref, dql_ref, dkl_ref, dgq_ref, dgkv_ref):
        i = pl.program_id(0)
        dql, dgq = _rms_bwd(dq_ref[...], q_ref[...], gq_ref[...])
        dkl, dgkv = _rms_bwd(dkv_ref[...], kv_ref[...], gkv_ref[...])
        dql_ref[...] = dql.astype(BF16)
        dkl_ref[...] = dkl.astype(BF16)
        _accumulate(dgq_ref, jnp.sum(dgq, axis=0, keepdims=True), i == 0)
        _accumulate(dgkv_ref, jnp.sum(dgkv, axis=0, keepdims=True), i == 0)

    def blk(width, off=0):
        return pl.BlockSpec((tm, width), lambda i: (i, off // width))

    def vec(width):
        return pl.BlockSpec((1, width), lambda i: (0, 0))

    return _call(
        body, [dqn, dkvn, proj, proj, gq, gkv], name="lat_bwd", grid=(T // tm,),
        in_specs=[blk(Q_LORA), blk(KV_LORA), blk(Q_LORA, lay["q"]), blk(KV_LORA, lay["kv"]), vec(Q_LORA), vec(KV_LORA)],
        out_specs=[blk(Q_LORA), blk(KV_LORA), vec(Q_LORA), vec(KV_LORA)],
        out_shape=[_sds((T, Q_LORA), BF16), _sds((T, KV_LORA), BF16), _sds((1, Q_LORA), F32), _sds((1, KV_LORA), F32)],
        sem=("arbitrary",))[0]


def _final_dx(dx1, dh, x, g, comm=None):
    T, D = x.shape
    tm = _tile(T, ROW_TILE, 16)

    def body(dx1_ref, dh_ref, x_ref, g_ref, dx_ref, dg_ref):
        i = pl.program_id(0)
        d, dg = _rms_bwd(dh_ref[...], x_ref[...], g_ref[...])
        dx_ref[...] = dx1_ref[...] + d
        _accumulate(dg_ref, jnp.sum(dg, axis=0, keepdims=True), i == 0)

    row = pl.BlockSpec((tm, D), lambda i: (i, 0))
    vec = pl.BlockSpec((1, D), lambda i: (0, 0))
    return _call(body, [dx1, dh, x, g], name="final_dx", grid=(T // tm,), in_specs=[row, row, row, vec],
                 out_specs=[row, vec], out_shape=[_sds((T, D), F32), _sds((1, D), F32)], sem=("arbitrary",), comm=comm)


def _chip_sum(pieces, paired, qc, name):
    G, R, C = pieces.shape
    tr = _tile(R, 256, 16)

    def body(qc_ref, g_ref, p_ref, keep_ref, send_ref):
        s = pl.program_id(1)
        tot = g_ref[...] + p_ref[...]

        @pl.when(s == 0)
        def _():
            keep_ref[...] = tot

        @pl.when(s > 0)
        def _():
            send_ref[...] = tot.astype(send_ref.dtype)

    grid_spec = pltpu.PrefetchScalarGridSpec(
        num_scalar_prefetch=1, grid=(R // tr, N_CHIP),
        in_specs=[pl.BlockSpec((None, tr, C), lambda i, s, qc: (2 * (qc[0] ^ s) + qc[1], i, 0)),
                  pl.BlockSpec((None, tr, C), lambda i, s, qc: (qc[0] ^ s, i, 0))],
        out_specs=[pl.BlockSpec((tr, C), lambda i, s, qc: (i, 0)),
                   pl.BlockSpec((None, tr, C), lambda i, s, qc: (jnp.maximum(s - 1, 0), i, 0))])
    send_dtype = BF16 if R >= 16 else pieces.dtype
    return pl.pallas_call(
        body, name=name, grid_spec=grid_spec,
        out_shape=[_sds((R, C), F32), _sds((3, R, C), send_dtype)],
        compiler_params=pltpu.CompilerParams(dimension_semantics=("arbitrary", "arbitrary"),
                                             vmem_limit_bytes=VMEM_LIMIT_BYTES),
    )(qc, pieces, paired)


def _adamw_math(w, g, m, v):
    m = ADAM_B1 * m + (1.0 - ADAM_B1) * g
    v = ADAM_B2 * v + (1.0 - ADAM_B2) * (g * g)
    m_hat = m / (1.0 - ADAM_B1 ** ADAM_STEP)
    v_hat = v / (1.0 - ADAM_B2 ** ADAM_STEP)
    delta = -ADAM_LR * (m_hat / (jnp.sqrt(v_hat) + ADAM_EPS) + ADAM_WD * w)
    return delta, m, v


def _sum_adamw(keep, pieces, w, m, v, name):
    R, C = w.shape
    P = pieces.shape[0]
    tr = _tile(R, 256, 16)

    def body(k_ref, p_ref, w_ref, m_ref, v_ref, g_ref, d_ref, mo_ref, vo_ref):
        g = k_ref[...]
        for q in range(P):
            g = g + p_ref[q].astype(F32)
        g_ref[...] = g
        d_ref[...], mo_ref[...], vo_ref[...] = _adamw_math(w_ref[...], g, m_ref[...], v_ref[...])

    blk = pl.BlockSpec((tr, C), lambda i: (i, 0))
    pblk = pl.BlockSpec((P, tr, C), lambda i: (0, i, 0))
    return _call(body, [keep, pieces, w, m, v], name=name, grid=(R // tr,), in_specs=[blk, pblk, blk, blk, blk],
                 out_specs=[blk] * 4, out_shape=[_sds((R, C), F32)] * 4, sem=("parallel",))[0]


def _adamw_small(parts, w, m, v, widths):
    n = len(widths)

    def body(p_ref, w_ref, m_ref, v_ref, *o_refs):
        g = p_ref[0]
        for q in range(1, N_DEV):
            g = g + p_ref[q]
        vals = (g,) + _adamw_math(w_ref[...], g, m_ref[...], v_ref[...])
        off = 0
        for i, wd in enumerate(widths):
            for kind in range(4):
                o_refs[4 * i + kind][...] = vals[kind][:, off:off + wd]
            off += wd

    whole = pl.BlockSpec(memory_space=pltpu.VMEM)
    outs = _call(body, [parts, w, m, v], name="adamw_small", grid=(), in_specs=[whole] * 4,
                 out_specs=[whole] * (4 * n), out_shape=[_sds((1, wd), F32) for wd in widths for _ in range(4)])[0]
    return [tuple(outs[4 * i:4 * i + 4]) for i in range(n)]


def _layout(D):
    lay = {"q": 0, "kv": Q_LORA, "kpe": Q_LORA + KV_LORA}
    lay["fq"] = lay["kpe"] + LANES
    lay["fk"] = lay["fq"] + HEADS * FOX_DIM
    lay["fv"] = lay["fk"] + HEADS * FOX_DIM
    lay["fl"] = lay["fv"] + HEADS * FOX_DIM
    lay["g"] = lay["fl"] + LANES
    lay["end"] = lay["g"] + 2 * D
    return lay


def kernel(x, positions, pre_mix_norm, w_in, q_a_norm, w_uq, kv_a_norm, w_ukv, b_forget, b_gate, w_branch_mla, w_branch_fox, w_out, post_mix_norm, pre_ffn_norm, w_up, conv_w, conv_b, w_down, post_ffn_norm, loss_target, m_pre_mix_norm, m_w_in, m_q_a_norm, m_w_uq, m_kv_a_norm, m_w_ukv, m_b_forget, m_b_gate, m_w_branch_mla, m_w_branch_fox, m_w_out, m_post_mix_norm, m_pre_ffn_norm, m_w_up, m_conv_w, m_conv_b, m_w_down, m_post_ffn_norm, v_pre_mix_norm, v_w_in, v_q_a_norm, v_w_uq, v_kv_a_norm, v_w_ukv, v_b_forget, v_b_gate, v_w_branch_mla, v_w_branch_fox, v_w_out, v_post_mix_norm, v_pre_ffn_norm, v_w_up, v_conv_w, v_conv_b, v_w_down, v_post_ffn_norm):
    B, S, D = x.shape
    T = B * S
    F = conv_b.shape[0] // 2
    lay = _layout(D)
    n_in = w_in.shape[1]
    d_in = N_DEV * n_in
    seg_a = Q_LORA + KV_LORA + ROPE
    seg_b = 3 * HEADS * FOX_DIM + HEADS
    mla_scale = (NOPE + ROPE) ** -0.5
    fox_scale = FOX_DIM ** -0.5
    ax, ay, ac = (lax.axis_index(a) for a in MESH_AXES)
    qc = jnp.stack([2 * ax + ay, ac]).astype(jnp.int32)

    def row(vec, width=None):
        vec = vec.reshape(1, -1)
        if width is not None and vec.shape[1] < width:
            vec = jnp.pad(vec, ((0, 0), (0, width - vec.shape[1])))
        return vec

    x2 = x.reshape(T, D)
    win_s = _cast_bf16(w_in, "cast_w_in")
    h, (win_g,) = _prenorm(x2, row(pre_mix_norm), comm=_Comm([_GatherRelayPlan([win_s], mid_frac=0.3)]))
    small_s = [_cast_bf16(w, "cast_" + n) for w, n in
               [(w_uq, "w_uq"), (w_ukv, "w_ukv"), (w_branch_mla, "w_branch_mla"), (w_branch_fox, "w_branch_fox"), (w_out, "w_out")]]
    wup_s = _cast_bf16(w_up, "cast_w_up")
    wdown_s = _cast_bf16(w_down, "cast_w_down")

    def shard_cols(lo, hi):
        out = []
        for g in range(lo // n_in, (hi - 1) // n_in + 1):
            out.append(win_g[g][:, max(lo, g * n_in) - g * n_in:min(hi, (g + 1) * n_in) - g * n_in])
        return out

    w_perm = jnp.concatenate(
        shard_cols(0, seg_a) + [jnp.zeros((D, LANES - ROPE), BF16)] + shard_cols(seg_a, seg_a + seg_b)
        + [jnp.zeros((D, LANES - HEADS), BF16)] + shard_cols(seg_a + seg_b, d_in), axis=1)

    tgt = loss_target.reshape(T, D)
    pos = positions.reshape(T, 1)
    inv_freq = 1.0 / (ROPE_THETA ** (jnp.arange(0, ROPE, 2, dtype=F32) / ROPE))
    invf = row(jnp.concatenate([inv_freq, inv_freq]), LANES)
    g_pre, g_q, g_kv = row(pre_mix_norm), row(q_a_norm), row(kv_a_norm)
    g_pm, g_ffn, g_pf = row(post_mix_norm), row(pre_ffn_norm), row(post_ffn_norm)
    bfor = row(b_forget, LANES)
    bgate = row(b_gate)
    cb_full = row(conv_b)

    def own_plan(blocks):
        return _Comm([_GatherOwnPlan(blocks)])

    def pass_plan(gathered):
        return _Comm([_GatherPassPlan(gathered)])

    def pair_plan(gs):
        return _Comm([_PairScatterPlan(gs)])

    def chip_plan(gs):
        return _Comm([_ChipScatterPlan(gs)])

    half_d = D // 2
    proj, landed = _matmul(h, w_perm, mode="nn", name="mm_proj", comm=_Comm(
        [_GatherOwnPlan(small_s[:2] + [conv_w]), _GatherOwnPlan([wup_s], rows=(0, half_d))]))
    early_g, wup_part = landed[:-1], landed[-1:]
    (qn, kvn, kper, logf, cosT, sinT), (wuq_g, wukv_g, cw_g) = _split_prep(
        proj, pos, invf, g_q, g_kv, bfor, lay, comm=pass_plan(early_g))
    wuq_pad = jnp.pad(wuq_g, ((0, 0), (0, 0), (0, ATT_DK - NOPE - ROPE)))
    cw_full = jnp.transpose(cw_g, (1, 0, 2)).reshape(3, 2 * F)

    qraw = _matmul(qn, wuq_pad, mode="nn", name="mm_q", out_blocks=ATT_DK, tm=T)
    kvraw = _matmul(kvn, wukv_g, mode="nn", name="mm_kv", out_blocks=NOPE + VDIM, tm=T)
    (q_mla, k_mla, v_mla), branch_half = _mla_prep(qraw, kvraw, kper, cosT, sinT, comm=own_plan(small_s[2:4]))
    cs = _fox_cumsum(logf, B, S, 1.0 / fox_scale)
    (q_fox, k_fox, v_fox), wout_half = _fox_prep(proj, cs, lay, comm=own_plan(small_s[4:5]))
    ((o_mla, lse_mla), (o_fox, lse_fox)), landed = _attn_fwd(
        [(q_mla, k_mla, v_mla, MLA_UNIT, mla_scale), (q_fox, k_fox, v_fox, 1, fox_scale)], B=B, S=S,
        name="attn_fwd", comm=_Comm([_GatherOwnPlan([wup_s], rows=(half_d, D), into=wup_part),
                                     _GatherPassPlan(branch_half + wout_half)]))
    wup_half, (wbm_g, wbf_g, wout_g) = landed[:1], landed[1:]
    wbm = jnp.transpose(wbm_g, (1, 0, 2)).reshape(HEADS * VDIM, D)
    wbf = jnp.transpose(wbf_g, (1, 0, 2)).reshape(HEADS * FOX_DIM, D)
    wout = wout_g.reshape(D, D)
    a_m = _matmul(o_mla, wbm, mode="nn", name="mm_branch_mla", tm=2 * MM_TILE)
    a_f = _matmul(o_fox, wbf, mode="nn", name="mm_branch_fox", tm=2 * MM_TILE)
    (merged,), (wup_g,) = _gate_merge(a_m, a_f, proj, bgate, lay, D, comm=pass_plan(wup_half))
    n_up = wup_g.shape[2]
    y1 = _matmul(merged, wout, mode="nn", name="mm_out")
    x1, h2 = _mid(x2, y1, g_pm, g_ffn)
    u, wdown_half = _matmul(h2, wup_g, mode="nn", name="mm_up", tn=n_up, comm=own_plan([wdown_s]))
    (act,), (wdown_g,) = _convffn_fwd(u, cw_full, cb_full, B, S, F, comm=pass_plan(wdown_half))
    wdown = wdown_g.reshape(F, D)
    ff = _matmul(act, wdown, mode="nn", name="mm_down", tk=F // 2)
    dy, dff, loss_part, dg_pf = _tail(ff, x1, tgt, g_pf)

    dact = _matmul(dff, wdown, mode="nt", name="mm_dact", tn=F // 4)
    dw_down = _matmul(act, dff, mode="tn", name="mm_dw_down", tm=F // 4, tn=512).reshape(N_DEV, F // N_DEV, D)
    (du_g, du_v, dcp_g, dcp_v), (pa_down,) = _convffn_bwd(u, dact, cw_full, cb_full, B, S, F, comm=pair_plan([dw_down]))
    keep_down, sb_down = _chip_sum(dw_down, pa_down, qc, "chipsum_w_down")
    dh2, (rb_down,) = _matmul_halves((du_g, du_v), wup_g, mode="nt", name="mm_dh2", tm=MM_TILE, comm=chip_plan([sb_down]))
    dw_up, dw_up16 = _matmul_halves(h2, (du_g, du_v), mode="tn", name="mm_dw_up", tm=MM_TILE, tn=n_up, tk=T // 2)
    (dx1, dy1, dg_ffn, dg_pm), _ = _mid_bwd(dy, dh2, x1, y1, g_ffn, g_pm)
    dmerged = _matmul(dy1, wout, mode="nt", name="mm_dmerged")
    dw_out = _matmul(merged, dy1, mode="tn", name="mm_dw_out").reshape(N_DEV, D // N_DEV, D)
    (da_m, da_f, dgl_m, dgl_f, dbg_m, dbg_f), (pa_up,) = _gate_bwd(
        dmerged, a_m, a_f, proj, bgate, lay, D, comm=pair_plan([dw_up16]))
    keep_up, sb_up = _chip_sum(dw_up, pa_up, qc, "chipsum_w_up")
    dw_bm = _matmul(o_mla, da_m, mode="tn", name="mm_dw_branch_mla", out_blocks=D // N_DEV)
    dw_bf = _matmul(o_fox, da_f, mode="tn", name="mm_dw_branch_fox", out_blocks=D // N_DEV)
    mix = [dw_out, dw_bm, dw_bf]
    do_mla, pa_mix = _matmul(da_m, wbm, mode="nt", name="mm_do_mla", out_dtype=BF16, comm=pair_plan(mix))
    do_fox = _matmul(da_f, wbf, mode="nt", name="mm_do_fox", out_dtype=BF16)
    mix_sums = [_chip_sum(g, p, qc, "chipsum_" + n) for g, p, n in zip(mix, pa_mix, ["w_out", "w_branch_mla", "w_branch_fox"])]
    ((dq_m, dk_m, dv_m), (dq_f, dk_f, dv_f)), (rb_up,) = _attn_bwd(
        [(q_mla, k_mla, v_mla, o_mla, do_mla, lse_mla, MLA_UNIT, mla_scale),
         (q_fox, k_fox, v_fox, o_fox, do_fox, lse_fox, 1, fox_scale)], B=B, S=S, name="attn_bwd",
        comm=chip_plan([sb_up]))
    (dqraw, dkvraw, dkpe), _ = _mla_bwd_prep(dq_m, dk_m, dv_m, cosT, sinT)
    dqn = _matmul(dqraw, wuq_pad, mode="nt", name="mm_dqn", tm=T)
    dw_uq = _matmul(qn, dqraw, mode="tn", name="mm_dw_uq", out_blocks=ATT_DK)[:, :, :NOPE + ROPE]
    dkvn = _matmul(dkvraw, wukv_g, mode="nt", name="mm_dkvn", tm=T)
    dw_ukv = _matmul(kvn, dkvraw, mode="tn", name="mm_dw_ukv", out_blocks=NOPE + VDIM)
    dqlat, dkvlat, dg_q, dg_kv = _lat_bwd(dqn, dkvn, proj, g_q, g_kv, lay)
    dfl, dbfor = _fox_bwd_prep(dq_f, dk_f, proj, bfor, lay, B, S, 1.0 / fox_scale)
    dproj = _concat_cols([dqlat, dkvlat, dkpe, dq_f, dk_f, dv_f, dfl, dgl_m, dgl_f], "concat_dproj")
    dw_perm, rb_mix = _matmul(h, dproj, mode="tn", name="mm_dw_in", comm=chip_plan([s[1] for s in mix_sums]))
    segs = [(0, seg_a, 0), (seg_a, seg_a + seg_b, lay["fq"] - seg_a), (seg_a + seg_b, d_in, lay["g"] - seg_a - seg_b)]

    def piece(g):
        lo, hi = g * n_in, (g + 1) * n_in
        parts = [dw_perm[:, max(lo, s0) + sh:min(hi, s1) + sh] for s0, s1, sh in segs if max(lo, s0) < min(hi, s1)]
        return parts[0] if len(parts) == 1 else jnp.concatenate(parts, axis=1)

    dw_in = jnp.stack([piece(g) for g in range(N_DEV)])
    dcw = jnp.transpose(jnp.concatenate([dcp_g[0:3], dcp_v[0:3]], axis=1).reshape(3, N_DEV, (2 * F) // N_DEV), (1, 0, 2))
    late = [dw_in, dw_uq, dw_ukv, dcw]
    pa_late = _exchange_alone(pair_plan([dw_in.astype(BF16), dw_uq, dw_ukv, dcw]), "pair_late")
    late_sums = [_chip_sum(g, p, qc, "chipsum_" + n) for g, p, n in zip(late, pa_late, ["w_in", "w_uq", "w_ukv", "conv_w"])]
    dh, rb_late = _matmul(dproj, w_perm, mode="nt", name="mm_dh", tn=2048, tk=2048, comm=chip_plan([s[1] for s in late_sums]))
    (grad_x, dg_pre), _ = _final_dx(dx1, dh, x2, g_pre)

    big_out = {}

    def finish(n, keep, pieces, w, m, v):
        big_out[n] = _sum_adamw(keep, pieces, w, m, v, "adamw_" + n)

    finish("w_down", keep_down, rb_down, w_down, m_w_down, v_w_down)
    finish("w_up", keep_up, rb_up, w_up, m_w_up, v_w_up)
    finish("w_out", mix_sums[0][0], rb_mix[0], w_out, m_w_out, v_w_out)
    finish("w_branch_mla", mix_sums[1][0], rb_mix[1], w_branch_mla, m_w_branch_mla, v_w_branch_mla)
    finish("w_branch_fox", mix_sums[2][0], rb_mix[2], w_branch_fox, m_w_branch_fox, v_w_branch_fox)
    finish("w_in", late_sums[0][0], rb_late[0], w_in, m_w_in, v_w_in)
    finish("w_uq", late_sums[1][0], rb_late[1], w_uq, m_w_uq, v_w_uq)
    finish("w_ukv", late_sums[2][0], rb_late[2], w_ukv, m_w_ukv, v_w_ukv)
    finish("conv_w", late_sums[3][0], rb_late[3], conv_w, m_conv_w, v_conv_w)

    widths = [D, Q_LORA, KV_LORA, LANES, 2 * D, D, D, 2 * F, D]
    small_names = ["pre_mix_norm", "q_a_norm", "kv_a_norm", "b_forget", "b_gate", "post_mix_norm", "pre_ffn_norm",
                   "conv_b", "post_ffn_norm"]
    true_w = [D, Q_LORA, KV_LORA, HEADS, 2 * D, D, D, 2 * F, D]
    dcb = jnp.concatenate([dcp_g[3:4], dcp_v[3:4]], axis=1)
    part = jnp.concatenate([dg_pre, dg_q, dg_kv, dbfor, dbg_m, dbg_f, dg_pm, dg_ffn, dcb, dg_pf], axis=1)

    def pack(vals):
        return jnp.concatenate([row(a, wd) for a, wd in zip(vals, widths)], axis=1)

    sw = pack([pre_mix_norm, q_a_norm, kv_a_norm, b_forget, b_gate, post_mix_norm, pre_ffn_norm, conv_b, post_ffn_norm])
    sm = pack([m_pre_mix_norm, m_q_a_norm, m_kv_a_norm, m_b_forget, m_b_gate, m_post_mix_norm, m_pre_ffn_norm,
               m_conv_b, m_post_ffn_norm])
    sv = pack([v_pre_mix_norm, v_q_a_norm, v_kv_a_norm, v_b_forget, v_b_gate, v_post_mix_norm, v_pre_ffn_norm,
               v_conv_b, v_post_ffn_norm])
    (parts_all,) = _exchange_alone(_Comm([_DirectGatherPlan([part])]), "gather_small")
    small = _adamw_small(parts_all, sw, sm, sv, widths)
    small_out = {n: tuple(a.reshape(-1)[:tw] for a in vals) for n, vals, tw in zip(small_names, small, true_w)}

    loss = lax.psum(loss_part[0, 0], MESH_AXES)
    order = ["pre_mix_norm", "w_in", "q_a_norm", "w_uq", "kv_a_norm", "w_ukv", "b_forget", "b_gate", "w_branch_mla",
             "w_branch_fox", "w_out", "post_mix_norm", "pre_ffn_norm", "w_up", "conv_w", "conv_b", "w_down",
             "post_ffn_norm"]
    res = {**big_out, **small_out}
    outs = [loss, grad_x.reshape(B, S, D)]
    for kind in range(4):
        outs += [res[n][kind] for n in order]
    return tuple(outs)
```

```python
import math

import jax
import jax.numpy as jnp
from jax import lax
from jax.experimental import pallas as pl
from jax.experimental.pallas import tpu as pltpu

F32 = jnp.float32
BF16 = jnp.bfloat16

N_DEV = 8
N_CHIP = 4
HEADS = 8
NOPE = 128
ROPE = 64
HALF_ROPE = ROPE // 2
VDIM = 128
Q_LORA = 512
KV_LORA = 256
FOX_DIM = 128
ATT_DK = 256
MLA_UNIT = 64
ROPE_THETA = 10000.0
EPS = 1e-6
NEG_INF = -1e30
LANES = 128
LOG2_E = 1.4426950408889634

ADAM_LR = 0.001
ADAM_B1 = 0.9
ADAM_B2 = 0.999
ADAM_EPS = 1e-08
ADAM_WD = 0.01
ADAM_STEP = 10

VMEM_LIMIT_BYTES = 56 * 1024 * 1024
ROW_TILE = 256
HEAD_ROW_TILE = 1024
ATT_TILE = 1024
ATT_SUB = 256
ATT_AHEAD = 3
MM_TILE = 1024

MESH_AXES = ("x", "y", "c")
ANY = pl.BlockSpec(memory_space=pl.ANY)


def _tile(n, pref, align=LANES):
    if n <= pref:
        return n
    t = (pref // align) * align
    while t >= align:
        if n % t == 0:
            return t
        t -= align
    return n


def _sds(shape, dtype):
    return jax.ShapeDtypeStruct(shape, dtype)


def _coords():
    x, y, c = (lax.axis_index(ax) for ax in MESH_AXES)
    return x, y, c


def _chip_rel(x, y, r):
    return (1 - x if r & 2 else x), (1 - y if r & 1 else y)


def _rcopy(src, dst, sems, w, k, dev):
    return pltpu.make_async_remote_copy(src_ref=src, dst_ref=dst, send_sem=sems[0].at[w, k], recv_sem=sems[1].at[w, k],
                                        device_id=dev, device_id_type=pl.DeviceIdType.MESH)


class _GatherRelayPlan:
    def __init__(self, blocks, mid_frac=0.5):
        self.ins = list(blocks)
        self.out_shapes = [_sds((N_DEV,) + b.shape, b.dtype) for b in blocks]
        n = len(blocks)
        self.scratch = [pltpu.SemaphoreType.DMA((n, 7)), pltpu.SemaphoreType.DMA((n, 7)), pltpu.SemaphoreType.DMA((n,))]
        self.mid_frac = mid_frac

    @staticmethod
    def _places():
        x, y, c = _coords()
        xn, yn = 4 * (1 - x) + 2 * y, 4 * x + 2 * (1 - y)
        relay_src = 4 * (x + c * (1 - 2 * x)) + 2 * (y + (1 - c) * (1 - 2 * y)) + c
        relay_to = (x + (1 - c) * (1 - 2 * x), y + c * (1 - 2 * y), c)
        return x, y, c, xn, yn, relay_src, relay_to, 4 * (1 - x) + 2 * (1 - y)

    def first(self, ins, outs, sems):
        x, y, c, _, _, _, _, _ = self._places()
        me = 4 * x + 2 * y + c
        for w in range(len(ins)):
            pltpu.make_async_copy(ins[w], outs[w].at[me], sems[2].at[w]).start()
            _rcopy(ins[w], outs[w].at[me], sems, w, 0, (x, y, 1 - c)).start()
            _rcopy(ins[w], outs[w].at[me], sems, w, 1, (1 - x, y, c)).start()
            _rcopy(ins[w], outs[w].at[me], sems, w, 2, (x, 1 - y, c)).start()

    def mid(self, ins, outs, sems):
        x, y, c, xn, yn, relay_src, relay_to, _ = self._places()
        sib = (x, y, 1 - c)
        for w in range(len(ins)):
            bx, by = outs[w].at[xn + c], outs[w].at[yn + c]
            _rcopy(ins[w], bx, sems, w, 1, (1 - x, y, c)).wait_recv()
            _rcopy(ins[w], by, sems, w, 2, (x, 1 - y, c)).wait_recv()
            _rcopy(outs[w].at[relay_src], outs[w].at[relay_src], sems, w, 3, relay_to).start()
            _rcopy(bx, bx, sems, w, 4, sib).start()
            _rcopy(by, by, sems, w, 5, sib).start()

    def last(self, ins, outs, sems):
        x, y, c, xn, yn, _, relay_to, dg = self._places()
        me = 4 * x + 2 * y + c
        sib = (x, y, 1 - c)
        for w in range(len(ins)):
            bd = outs[w].at[dg + c]
            _rcopy(ins[w], bd, sems, w, 3, relay_to).wait_recv()
            _rcopy(bd, bd, sems, w, 6, sib).start()
            for k, blk in ((0, 4 * x + 2 * y), (4, xn), (5, yn), (6, dg)):
                _rcopy(ins[w], outs[w].at[blk + 1 - c], sems, w, k, sib).wait_recv()
            for k in range(7):
                _rcopy(ins[w], outs[w].at[me], sems, w, k, sib).wait_send()
            pltpu.make_async_copy(ins[w], outs[w].at[me], sems[2].at[w]).wait()


class _GatherOwnPlan:
    mid = None

    def __init__(self, blocks, rows=None, into=None):
        self.n = len(blocks)
        self.rows = rows
        self.ins = list(blocks) + list(into or [])
        self.out_shapes = [_sds((N_DEV,) + b.shape, b.dtype) for b in blocks]
        self.aliases = [(self.n + i, i) for i in range(len(into or []))]
        n = self.n
        self.scratch = [pltpu.SemaphoreType.DMA((n, 4)), pltpu.SemaphoreType.DMA((n, 4)), pltpu.SemaphoreType.DMA((n,))]

    def _cut(self, ref):
        return ref if self.rows is None else ref.at[pl.ds(self.rows[0], self.rows[1] - self.rows[0])]

    def first(self, ins, outs, sems):
        x, y, c = _coords()
        me = 4 * x + 2 * y + c
        for w in range(self.n):
            src, dst = self._cut(ins[w]), self._cut(outs[w].at[me])
            pltpu.make_async_copy(src, dst, sems[2].at[w]).start()
            _rcopy(src, dst, sems, w, 0, (x, y, 1 - c)).start()
            for r in (1, 2, 3):
                px, py = _chip_rel(x, y, r)
                _rcopy(src, dst, sems, w, r, (px, py, c)).start()

    def last(self, ins, outs, sems):
        x, y, c = _coords()
        me = 4 * x + 2 * y + c
        for w in range(self.n):
            src = self._cut(ins[w])
            cp = _rcopy(src, self._cut(outs[w].at[4 * x + 2 * y + 1 - c]), sems, w, 0, (x, y, 1 - c))
            cp.wait_recv()
            cp.wait_send()
            for r in (1, 2, 3):
                px, py = _chip_rel(x, y, r)
                cp = _rcopy(src, self._cut(outs[w].at[4 * px + 2 * py + c]), sems, w, r, (px, py, c))
                cp.wait_recv()
                cp.wait_send()
            pltpu.make_async_copy(src, self._cut(outs[w].at[me]), sems[2].at[w]).wait()


class _GatherPassPlan:
    mid = None

    def __init__(self, gathered):
        self.ins = list(gathered)
        self.out_shapes = [_sds(g.shape, g.dtype) for g in gathered]
        self.aliases = [(i, i) for i in range(len(gathered))]
        n = len(gathered)
        self.scratch = [pltpu.SemaphoreType.DMA((n, 3)), pltpu.SemaphoreType.DMA((n, 3))]

    def first(self, ins, outs, sems):
        x, y, c = _coords()
        for w in range(len(ins)):
            for r in (1, 2, 3):
                px, py = _chip_rel(x, y, r)
                blk = 4 * px + 2 * py + c
                _rcopy(ins[w].at[blk], outs[w].at[blk], sems, w, r - 1, (x, y, 1 - c)).start()

    def last(self, ins, outs, sems):
        x, y, c = _coords()
        for w in range(len(ins)):
            for r in (1, 2, 3):
                px, py = _chip_rel(x, y, r)
                blk = 4 * px + 2 * py + 1 - c
                cp = _rcopy(ins[w].at[blk], outs[w].at[blk], sems, w, r - 1, (x, y, 1 - c))
                cp.wait_recv()
                cp.wait_send()


class _DirectGatherPlan:
    mid = None

    def __init__(self, blocks):
        self.ins = list(blocks)
        self.out_shapes = [_sds((N_DEV,) + b.shape, b.dtype) for b in blocks]
        n = len(blocks)
        self.scratch = [pltpu.SemaphoreType.DMA((n, 7)), pltpu.SemaphoreType.DMA((n, 7)), pltpu.SemaphoreType.DMA((n,))]

    @staticmethod
    def _peer(x, y, c, r):
        return (1 - x if r & 4 else x), (1 - y if r & 2 else y), (1 - c if r & 1 else c)

    def first(self, ins, outs, sems):
        x, y, c = _coords()
        me = 4 * x + 2 * y + c
        for w in range(len(ins)):
            pltpu.make_async_copy(ins[w], outs[w].at[me], sems[2].at[w]).start()
            for r in range(1, N_DEV):
                _rcopy(ins[w], outs[w].at[me], sems, w, r - 1, self._peer(x, y, c, r)).start()

    def last(self, ins, outs, sems):
        x, y, c = _coords()
        me = 4 * x + 2 * y + c
        for w in range(len(ins)):
            for r in range(1, N_DEV):
                px, py, pc = self._peer(x, y, c, r)
                cp = _rcopy(ins[w], outs[w].at[4 * px + 2 * py + pc], sems, w, r - 1, (px, py, pc))
                cp.wait_recv()
                cp.wait_send()
            pltpu.make_async_copy(ins[w], outs[w].at[me], sems[2].at[w]).wait()


class _PairScatterPlan:
    mid = None

    def __init__(self, pieces, rows=None, into=None):
        self.n = len(pieces)
        self.rows = rows
        self.ins = list(pieces) + list(into or [])
        self.out_shapes = [_sds((N_CHIP,) + p.shape[1:], p.dtype) for p in pieces]
        self.aliases = [(self.n + i, i) for i in range(len(into or []))]
        self.scratch = [pltpu.SemaphoreType.DMA((self.n, N_CHIP)), pltpu.SemaphoreType.DMA((self.n, N_CHIP))]

    def _copies(self, ins, outs, sems):
        x, y, c = _coords()
        cps = []
        for w in range(self.n):
            for q in range(N_CHIP):
                src, dst = ins[w].at[2 * q + 1 - c], outs[w].at[q]
                if self.rows is not None:
                    cut = pl.ds(self.rows[0], self.rows[1] - self.rows[0])
                    src, dst = src.at[cut], dst.at[cut]
                cps.append(_rcopy(src, dst, sems, w, q, (x, y, 1 - c)))
        return cps

    def first(self, ins, outs, sems):
        for cp in self._copies(ins, outs, sems):
            cp.start()

    def last(self, ins, outs, sems):
        for cp in self._copies(ins, outs, sems):
            cp.wait_recv()
            cp.wait_send()


class _ChipScatterPlan:
    mid = None

    def __init__(self, sums, rows=None, into=None):
        self.n = len(sums)
        self.rows = rows
        self.ins = list(sums) + list(into or [])
        self.out_shapes = [_sds(s.shape, s.dtype) for s in sums]
        self.aliases = [(self.n + i, i) for i in range(len(into or []))]
        self.scratch = [pltpu.SemaphoreType.DMA((self.n, 3)), pltpu.SemaphoreType.DMA((self.n, 3))]

    def _copies(self, ins, outs, sems):
        x, y, c = _coords()
        cps = []
        for w in range(self.n):
            for r in (1, 2, 3):
                px, py = _chip_rel(x, y, r)
                src, dst = ins[w].at[r - 1], outs[w].at[r - 1]
                if self.rows is not None:
                    cut = pl.ds(self.rows[0], self.rows[1] - self.rows[0])
                    src, dst = src.at[cut], dst.at[cut]
                cps.append(_rcopy(src, dst, sems, w, r - 1, (px, py, c)))
        return cps

    def first(self, ins, outs, sems):
        for cp in self._copies(ins, outs, sems):
            cp.start()

    def last(self, ins, outs, sems):
        for cp in self._copies(ins, outs, sems):
            cp.wait_recv()
            cp.wait_send()


class _Comm:
    def __init__(self, plans):
        self.plans = list(plans)
        self.ins = [a for p in self.plans for a in p.ins]
        self.out_shapes = [s for p in self.plans for s in p.out_shapes]
        self.scratch = [s for p in self.plans for s in p.scratch]
        self.aliases = []
        i = o = 0
        for p in self.plans:
            self.aliases += [(i + a, o + b) for a, b in getattr(p, "aliases", [])]
            i, o = i + len(p.ins), o + len(p.out_shapes)

    def _parts(self, ins, outs, sems):
        i = o = s = 0
        for p in self.plans:
            yield p, ins[i:i + len(p.ins)], outs[o:o + len(p.out_shapes)], sems[s:s + len(p.scratch)]
            i, o, s = i + len(p.ins), o + len(p.out_shapes), s + len(p.scratch)

    def begin(self, step, nsteps, ins, outs, sems):
        @pl.when(step == 0)
        def _():
            for p, pi, po, ps in self._parts(ins, outs, sems):
                p.first(pi, po, ps)

        for p, pi, po, ps in self._parts(ins, outs, sems):
            if p.mid is not None:
                @pl.when(step == min(nsteps - 1, int(p.mid_frac * nsteps)))
                def _(p=p, pi=pi, po=po, ps=ps):
                    p.mid(pi, po, ps)

    def end(self, step, nsteps, ins, outs, sems):
        @pl.when(step == nsteps - 1)
        def _():
            for p, pi, po, ps in self._parts(ins, outs, sems):
                p.last(pi, po, ps)


def _call(body, args, *, name, grid, in_specs, out_specs, out_shape, scratch_shapes=(), sem=None, comm=None):
    in_specs, out_specs, out_shape, scratch_shapes = list(in_specs), list(out_specs), list(out_shape), list(scratch_shapes)
    if comm is None:
        res = pl.pallas_call(
            body, name=name, grid=grid, in_specs=in_specs, out_specs=out_specs, out_shape=out_shape,
            scratch_shapes=scratch_shapes,
            compiler_params=pltpu.CompilerParams(dimension_semantics=sem, vmem_limit_bytes=VMEM_LIMIT_BYTES),
        )(*args)
        return list(res), []
    n_in, n_out, n_sc = len(in_specs), len(out_specs), len(scratch_shapes)
    n_ci, n_co = len(comm.ins), len(comm.out_shapes)
    nsteps = math.prod(grid)

    def hosted(*refs):
        ins, cins = refs[:n_in], refs[n_in:n_in + n_ci]
        o0 = n_in + n_ci
        outs, couts = refs[o0:o0 + n_out], refs[o0 + n_out:o0 + n_out + n_co]
        s0 = o0 + n_out + n_co
        scr, csems = refs[s0:s0 + n_sc], refs[s0 + n_sc:]
        step = jnp.int32(0)
        for d in range(len(grid)):
            step = step * grid[d] + pl.program_id(d)
        comm.begin(step, nsteps, cins, couts, csems)
        body(*ins, *outs, *scr)
        comm.end(step, nsteps, cins, couts, csems)

    res = pl.pallas_call(
        hosted, name=name, grid=grid, in_specs=in_specs + [ANY] * n_ci, out_specs=out_specs + [ANY] * n_co,
        out_shape=out_shape + comm.out_shapes, scratch_shapes=scratch_shapes + comm.scratch,
        input_output_aliases={n_in + a: n_out + b for a, b in comm.aliases},
        compiler_params=pltpu.CompilerParams(dimension_semantics=("arbitrary",) * len(grid),
                                             vmem_limit_bytes=VMEM_LIMIT_BYTES, has_side_effects=True),
    )(*args, *comm.ins)
    return list(res[:n_out]), list(res[n_out:])


def _exchange_alone(comm, name):
    def body():
        pass

    return _call(body, [], name=name, grid=(), in_specs=[], out_specs=[], out_shape=[], comm=comm)[1]


def _matmul(a, b, *, mode, name, out_dtype=F32, out_blocks=None, tm=None, tn=None, tk=None, comm=None):
    tm = MM_TILE if tm is None else tm
    tn = MM_TILE if tn is None else tn
    a_blk = a.ndim == 3
    b_blk = b.ndim == 3
    if mode == "nn":
        M, K = a.shape
        N = b.shape[0] * b.shape[2] if b_blk else b.shape[1]
        dims = (((1,), (0,)), ((), ()))
    elif mode == "nt":
        M = a.shape[1] if a_blk else a.shape[0]
        K = a.shape[0] * a.shape[2] if a_blk else a.shape[1]
        N = b.shape[1] if b_blk else b.shape[0]
        dims = (((1,), (1,)), ((), ()))
    else:
        K, M = a.shape
        N = b.shape[0] * b.shape[2] if b_blk else b.shape[1]
        dims = (((0,), (0,)), ((), ()))

    tm = _tile(M, tm)
    tn = _tile(N, tn)
    if mode == "nt" and (a_blk or b_blk):
        tk = a.shape[2] if a_blk else b.shape[2]
    else:
        tk = _tile(K, K if tk is None else tk)
    if mode != "nt" and b_blk:
        tn = _tile(b.shape[2], tn)
    if out_blocks is not None:
        tn = _tile(out_blocks, tn)
    nk = K // tk
    grid = (M // tm, N // tn, nk)

    if mode == "nn":
        a_spec = pl.BlockSpec((tm, tk), lambda i, j, k: (i, k))
        if b_blk:
            rb = b.shape[2] // tn
            b_spec = pl.BlockSpec((None, tk, tn), lambda i, j, k: (j // rb, k, j % rb))
        else:
            b_spec = pl.BlockSpec((tk, tn), lambda i, j, k: (k, j))
    elif mode == "nt":
        if a_blk:
            a_spec = pl.BlockSpec((None, tm, tk), lambda i, j, k: (k, i, 0))
        else:
            a_spec = pl.BlockSpec((tm, tk), lambda i, j, k: (i, k))
        if b_blk:
            b_spec = pl.BlockSpec((None, tn, tk), lambda i, j, k: (k, j, 0))
        else:
            b_spec = pl.BlockSpec((tn, tk), lambda i, j, k: (j, k))
    else:
        a_spec = pl.BlockSpec((tk, tm), lambda i, j, k: (k, i))
        if b_blk:
            rb = b.shape[2] // tn
            b_spec = pl.BlockSpec((None, tk, tn), lambda i, j, k: (j // rb, k, j % rb))
        else:
            b_spec = pl.BlockSpec((tk, tn), lambda i, j, k: (k, j))

    if out_blocks is None:
        o_spec = pl.BlockSpec((tm, tn), lambda i, j, k: (i, j))
        o_shape = _sds((M, N), out_dtype)
    else:
        ro = out_blocks // tn
        o_spec = pl.BlockSpec((None, tm, tn), lambda i, j, k: (j // ro, i, j % ro))
        o_shape = _sds((N // out_blocks, M, out_blocks), out_dtype)

    direct = nk == 1 or out_dtype == F32

    def body(a_ref, b_ref, o_ref, *scratch):
        if nk == 1:
            o_ref[...] = lax.dot_general(a_ref[...], b_ref[...], dims, preferred_element_type=F32).astype(o_ref.dtype)
            return
        acc_ref = o_ref if direct else scratch[0]
        k = pl.program_id(2)

        @pl.when(k == 0)
        def _():
            acc_ref[...] = jnp.zeros(acc_ref.shape, F32)

        acc_ref[...] += lax.dot_general(a_ref[...], b_ref[...], dims, preferred_element_type=F32)
        if not direct:
            @pl.when(k == nk - 1)
            def _():
                o_ref[...] = acc_ref[...].astype(o_ref.dtype)

    scratch = [] if direct else [pltpu.VMEM((tm, tn), F32)]
    outs, landed = _call(body, [a, b], name=name, grid=grid, in_specs=[a_spec, b_spec], out_specs=[o_spec],
                         out_shape=[o_shape], scratch_shapes=scratch, sem=("parallel", "parallel", "arbitrary"), comm=comm)
    return outs[0] if comm is None else (outs[0], landed)


def _matmul_halves(a, b, *, mode, name, tm, tn=None, tk=None, comm=None):
    if mode == "nt":
        lo, hi = a
        M, kh = lo.shape
        G, N, kb = b.shape
        half = kh // kb
        tm, tn = _tile(M, tm), _tile(N, N if tn is None else tn)
        dims = (((1,), (1,)), ((), ()))

        def body(lo_ref, hi_ref, b_ref, o_ref):
            k = pl.program_id(2)

            @pl.when(k == 0)
            def _():
                o_ref[...] = jnp.zeros(o_ref.shape, F32)

            @pl.when(k < half)
            def _():
                o_ref[...] += lax.dot_general(lo_ref[...], b_ref[...], dims, preferred_element_type=F32)

            @pl.when(k >= half)
            def _():
                o_ref[...] += lax.dot_general(hi_ref[...], b_ref[...], dims, preferred_element_type=F32)

        outs, landed = _call(
            body, [lo, hi, b], name=name, grid=(M // tm, N // tn, G),
            in_specs=[pl.BlockSpec((tm, kb), lambda i, j, k: (i, jnp.minimum(k, half - 1))),
                      pl.BlockSpec((tm, kb), lambda i, j, k: (i, jnp.maximum(k - half, 0))),
                      pl.BlockSpec((None, tn, kb), lambda i, j, k: (k, j, 0))],
            out_specs=[pl.BlockSpec((tm, tn), lambda i, j, k: (i, j))], out_shape=[_sds((M, N), F32)],
            sem=("parallel", "parallel", "arbitrary"), comm=comm)
    else:
        lo, hi = b
        K, nh = lo.shape
        M = a.shape[1]
        n = tn
        half = nh // n
        tm, tk = _tile(M, tm), _tile(K, K if tk is None else tk)
        nk = K // tk
        dims = (((0,), (0,)), ((), ()))

        def body(a_ref, lo_ref, hi_ref, o_ref):
            j, k = pl.program_id(1), pl.program_id(2)

            @pl.when(k == 0)
            def _():
                o_ref[...] = jnp.zeros(o_ref.shape, F32)

            @pl.when(j < half)
            def _():
                o_ref[...] += lax.dot_general(a_ref[...], lo_ref[...], dims, preferred_element_type=F32)

            @pl.when(j >= half)
            def _():
                o_ref[...] += lax.dot_general(a_ref[...], hi_ref[...], dims, preferred_element_type=F32)

        outs, landed = _call(
            body, [a, lo, hi], name=name, grid=(M // tm, 2 * half, nk),
            in_specs=[pl.BlockSpec((tk, tm), lambda i, j, k: (k, i)),
                      pl.BlockSpec((tk, n), lambda i, j, k: (jnp.where(j < half, k, nk - 1), jnp.minimum(j, half - 1))),
                      pl.BlockSpec((tk, n), lambda i, j, k: (jnp.where(j >= half, k, 0), jnp.maximum(j - half, 0)))],
            out_specs=[pl.BlockSpec((None, tm, n), lambda i, j, k: (j, i, 0))],
            out_shape=[_sds((2 * half, M, n), F32)], sem=("parallel", "parallel", "arbitrary"), comm=comm)
    return outs[0] if comm is None else (outs[0], landed)


def _rms(x):
    return lax.rsqrt(jnp.mean(x * x, axis=-1, keepdims=True) + EPS)


def _rms_bwd(dy, x, g):
    r = _rms(x)
    n = x * r
    dn = dy * g
    dx = r * (dn - n * jnp.mean(dn * n, axis=-1, keepdims=True))
    return dx, dy * n


def _sigmoid(x):
    return 1.0 / (1.0 + jnp.exp(-x))


def _rope_rot(t):
    return pltpu.roll(t, HALF_ROPE, 1) - pltpu.roll(t, LANES - HALF_ROPE, 1)


def _lane(shape):
    return lax.broadcasted_iota(jnp.int32, shape, 1)


def _split3(x):
    hi = x.astype(BF16).astype(F32)
    r1 = x - hi
    mid = r1.astype(BF16).astype(F32)
    lo = (r1 - mid).astype(BF16).astype(F32)
    return hi, mid, lo


def _cumsum_rows(x, reverse):
    S = x.shape[0]
    bs = min(256, S)
    nb = S // bs
    r = lax.broadcasted_iota(jnp.int32, (bs, bs), 0)
    c = lax.broadcasted_iota(jnp.int32, (bs, bs), 1)
    tri = jnp.where((c >= r) if reverse else (c <= r), 1.0, 0.0).astype(BF16)
    edge = lax.broadcasted_iota(jnp.int32, (bs, x.shape[1]), 0) == (0 if reverse else bs - 1)
    carry = jnp.zeros((1, x.shape[1]), F32)
    outs = [None] * nb
    for bi in (range(nb - 1, -1, -1) if reverse else range(nb)):
        xb = x[bi * bs:(bi + 1) * bs, :]
        acc = carry
        for term in _split3(xb):
            acc = acc + jnp.dot(tri, term.astype(BF16), preferred_element_type=F32)
        outs[bi] = acc
        carry = jnp.sum(jnp.where(edge, acc, 0.0), axis=0, keepdims=True)
    return jnp.concatenate(outs, axis=0) if nb > 1 else outs[0]


def _gelu_parts(x):
    c0 = math.sqrt(2.0 / math.pi)
    inner = c0 * (x + 0.044715 * (x * x * x))
    t = jnp.tanh(inner)
    g = 0.5 * x * (1.0 + t)
    dg = 0.5 * (1.0 + t) + 0.5 * x * (1.0 - t * t) * (c0 * (1.0 + 3.0 * 0.044715 * (x * x)))
    return g, dg


def _accumulate(ref, value, first):
    @pl.when(first)
    def _():
        ref[...] = value

    @pl.when(jnp.logical_not(first))
    def _():
        ref[...] += value


def _cast_bf16(w, name):
    R, C = w.shape
    tr = _tile(R, 512, 16)

    def body(w_ref, o_ref):
        o_ref[...] = w_ref[...].astype(BF16)

    blk = pl.BlockSpec((tr, C), lambda i: (i, 0))
    return _call(body, [w], name=name, grid=(R // tr,), in_specs=[blk], out_specs=[blk],
                 out_shape=[_sds((R, C), BF16)], sem=("parallel",))[0][0]


def _concat_cols(parts, name):
    T = parts[0].shape[-2] if parts[0].ndim == 3 else parts[0].shape[0]
    widths = [p.shape[0] * LANES if p.ndim == 3 else p.shape[1] for p in parts]
    tm = _tile(T, ROW_TILE, 16)

    def body(*refs):
        o_ref = refs[-1]
        off = 0
        for p_ref, p, w in zip(refs[:-1], parts, widths):
            if p.ndim == 3:
                for hd in range(p.shape[0]):
                    o_ref[:, off + hd * LANES:off + (hd + 1) * LANES] = p_ref[hd].astype(BF16)
            else:
                o_ref[:, off:off + w] = p_ref[...].astype(BF16)
            off += w

    def spec(p, w):
        if p.ndim == 3:
            return pl.BlockSpec((p.shape[0], tm, LANES), lambda i: (0, i, 0))
        return pl.BlockSpec((tm, w), lambda i: (i, 0))

    return _call(body, parts, name=name, grid=(T // tm,),
                 in_specs=[spec(p, w) for p, w in zip(parts, widths)],
                 out_specs=[pl.BlockSpec((tm, sum(widths)), lambda i: (i, 0))],
                 out_shape=[_sds((T, sum(widths)), BF16)], sem=("parallel",))[0][0]


def _prenorm(x, g, comm=None):
    T, D = x.shape
    tm = _tile(T, ROW_TILE, 16)

    def body(x_ref, g_ref, h_ref):
        xv = x_ref[...]
        h_ref[...] = (xv * _rms(xv) * g_ref[...]).astype(BF16)

    row = pl.BlockSpec((tm, D), lambda i: (i, 0))
    (h,), landed = _call(body, [x, g], name="prenorm", grid=(T // tm,),
                         in_specs=[row, pl.BlockSpec((1, D), lambda i: (0, 0))], out_specs=[row],
                         out_shape=[_sds((T, D), BF16)], sem=("parallel",), comm=comm)
    return h, landed


def _split_prep(proj, pos, invf, gq, gkv, bfor, lay, comm=None):
    T = proj.shape[0]
    tm = _tile(T, ROW_TILE, 16)

    def body(q_ref, kv_ref, kpe_ref, fl_ref, pos_ref, invf_ref, gq_ref, gkv_ref, bf_ref,
             qn_ref, kvn_ref, kper_ref, logf_ref, cos_ref, sin_ref):
        ql = q_ref[...]
        qn_ref[...] = (ql * _rms(ql) * gq_ref[...]).astype(BF16)
        kl = kv_ref[...]
        kvn_ref[...] = (kl * _rms(kl) * gkv_ref[...]).astype(BF16)
        ang = pos_ref[...].astype(F32) * invf_ref[...]
        valid = _lane(ang.shape) < ROPE
        cs = jnp.where(valid, jnp.cos(ang), 0.0)
        sn = jnp.where(valid, jnp.sin(ang), 0.0)
        cos_ref[...] = cs
        sin_ref[...] = sn
        kp = jnp.where(valid, kpe_ref[...], 0.0)
        kper_ref[...] = (kp * cs + _rope_rot(kp) * sn).astype(BF16)
        z = fl_ref[...] + bf_ref[...]
        logf_ref[...] = jnp.minimum(z, 0.0) - jnp.log(1.0 + jnp.exp(-jnp.abs(z)))

    def col(width, off):
        return pl.BlockSpec((tm, width), lambda i: (i, off // width))

    def vec(width):
        return pl.BlockSpec((1, width), lambda i: (0, 0))

    def out(width):
        return pl.BlockSpec((tm, width), lambda i: (i, 0))

    return _call(
        body, [proj, proj, proj, proj, pos, invf, gq, gkv, bfor], name="split_prep", grid=(T // tm,),
        in_specs=[col(Q_LORA, lay["q"]), col(KV_LORA, lay["kv"]), col(LANES, lay["kpe"]), col(LANES, lay["fl"]),
                  pl.BlockSpec((tm, 1), lambda i: (i, 0)), vec(LANES), vec(Q_LORA), vec(KV_LORA), vec(LANES)],
        out_specs=[out(Q_LORA), out(KV_LORA), out(LANES), out(LANES), out(LANES), out(LANES)],
        out_shape=[_sds((T, Q_LORA), BF16), _sds((T, KV_LORA), BF16), _sds((T, LANES), BF16),
                   _sds((T, LANES), F32), _sds((T, LANES), F32), _sds((T, LANES), F32)],
        sem=("parallel",), comm=comm)


def _mla_prep(qraw, kvraw, kper, cosT, sinT, comm=None):
    H, T, _ = qraw.shape
    tm = _tile(T, HEAD_ROW_TILE, 16)

    def body(q_ref, kv_ref, kpe_ref, cos_ref, sin_ref, qo_ref, ko_ref, vo_ref):
        q = q_ref[...]
        pe = q[:, NOPE:]
        pe = jnp.where(_lane(pe.shape) < ROPE, pe, 0.0)
        qo_ref[:, :NOPE] = q[:, :NOPE].astype(BF16)
        qo_ref[:, NOPE:] = (pe * cos_ref[...] + _rope_rot(pe) * sin_ref[...]).astype(BF16)
        kv = kv_ref[...]
        ko_ref[:, :NOPE] = kv[:, :NOPE].astype(BF16)
        ko_ref[:, NOPE:] = kpe_ref[...]
        vo_ref[...] = kv[:, NOPE:].astype(BF16)

    head = pl.BlockSpec((None, tm, ATT_DK), lambda h, i: (h, i, 0))
    tok = pl.BlockSpec((tm, LANES), lambda h, i: (i, 0))
    return _call(
        body, [qraw, kvraw, kper, cosT, sinT], name="mla_prep", grid=(H, T // tm),
        in_specs=[head, head, tok, tok, tok],
        out_specs=[head, head, pl.BlockSpec((None, tm, VDIM), lambda h, i: (h, i, 0))],
        out_shape=[_sds((H, T, ATT_DK), BF16), _sds((H, T, ATT_DK), BF16), _sds((H, T, VDIM), BF16)],
        sem=("parallel", "parallel"), comm=comm)


def _fox_cumsum(logf, B, S, inv_scale):
    T = logf.shape[0]

    def body(l_ref, c_ref):
        c_ref[...] = _cumsum_rows(l_ref[...], reverse=False) * inv_scale

    seq = pl.BlockSpec((S, LANES), lambda b: (b, 0))
    return _call(body, [logf], name="fox_cumsum", grid=(B,), in_specs=[seq], out_specs=[seq],
                 out_shape=[_sds((T, LANES), F32)], sem=("parallel",))[0][0]


def _fox_prep(proj, cs, lay, comm=None):
    T = proj.shape[0]
    tm = _tile(T, HEAD_ROW_TILE, 16)

    def body(q_ref, k_ref, v_ref, cs_ref, qo_ref, ko_ref, vo_ref):
        h = pl.program_id(0)
        cv = cs_ref[...]
        lane = _lane(cv.shape)
        ccol = jnp.sum(jnp.where(lane == h, cv, 0.0), axis=1, keepdims=True)
        hi, mid, lo = _split3(ccol)
        one = jnp.where(lane < 6, 1.0, 0.0)
        augq = jnp.where(lane == 0, hi, jnp.where(lane == 1, mid, jnp.where(lane == 2, lo, one)))
        augk = jnp.where(lane < 3, 1.0, jnp.where(lane == 3, -hi, jnp.where(lane == 4, -mid, jnp.where(lane == 5, -lo, 0.0))))
        qo_ref[:, :FOX_DIM] = q_ref[...].astype(BF16)
        qo_ref[:, FOX_DIM:] = augq.astype(BF16)
        ko_ref[:, :FOX_DIM] = k_ref[...].astype(BF16)
        ko_ref[:, FOX_DIM:] = augk.astype(BF16)
        vo_ref[...] = v_ref[...].astype(BF16)

    def col(off):
        return pl.BlockSpec((tm, FOX_DIM), lambda h, i: (i, off // FOX_DIM + h))

    head = pl.BlockSpec((None, tm, ATT_DK), lambda h, i: (h, i, 0))
    return _call(
        body, [proj, proj, proj, cs], name="fox_prep", grid=(HEADS, T // tm),
        in_specs=[col(lay["fq"]), col(lay["fk"]), col(lay["fv"]), pl.BlockSpec((tm, LANES), lambda h, i: (i, 0))],
        out_specs=[head, head, pl.BlockSpec((None, tm, VDIM), lambda h, i: (h, i, 0))],
        out_shape=[_sds((HEADS, T, ATT_DK), BF16), _sds((HEADS, T, ATT_DK), BF16), _sds((HEADS, T, VDIM), BF16)],
        sem=("parallel", "parallel"), comm=comm)


def _visible(tq, tk, unit):
    r = lax.broadcasted_iota(jnp.int32, (tq, tk), 0)
    c = lax.broadcasted_iota(jnp.int32, (tq, tk), 1)
    sh = int(math.log2(unit))
    return lax.shift_right_logical(c, sh) <= lax.shift_right_logical(r, sh)


def _attn_fwd(streams, *, B, S, name, comm=None):
    n = len(streams)
    H, T, DK = streams[0][0].shape
    DV = streams[0][2].shape[2]
    tq = _tile(S, ATT_TILE)
    nq = S // tq
    sub = min(ATT_SUB, tq)
    NT = (((1,), (1,)), ((), ()))

    def body(*refs):
        ins, outs, (m_sc, acc_sc) = refs[:3 * n], refs[3 * n:5 * n], refs[5 * n:]
        i, j = pl.program_id(1), pl.program_id(2)

        @pl.when(j == 0)
        def _():
            m_sc[...] = jnp.full(m_sc.shape, NEG_INF, F32)
            acc_sc[...] = jnp.zeros(acc_sc.shape, F32)

        def step(diagonal):
            work = [(t, r) for r in range(tq // sub) for t in range(n)]

            def scores(t, r):
                q_ref, k_ref, _ = ins[3 * t:3 * t + 3]
                kc = (r + 1) * sub if diagonal else tq
                s = lax.dot_general(q_ref[r * sub:(r + 1) * sub, :], k_ref[0:kc, :], NT, preferred_element_type=F32)
                return s * (streams[t][4] * LOG2_E)

            ahead = [scores(*work[w]) for w in range(min(ATT_AHEAD, len(work)))]
            for w, (t, r) in enumerate(work):
                s = ahead.pop(0)
                if w + ATT_AHEAD < len(work):
                    ahead.append(scores(*work[w + ATT_AHEAD]))
                v_ref = ins[3 * t + 2]
                kc = s.shape[1]
                rows = slice(r * sub, (r + 1) * sub)
                if diagonal:
                    own = jnp.where(_visible(sub, sub, streams[t][3]), s[:, kc - sub:], NEG_INF)
                    s = own if kc == sub else jnp.concatenate([s[:, :kc - sub], own], axis=1)
                m_prev = m_sc[t, rows, :]
                mx = s[:, 0:LANES]
                for g in range(1, kc // LANES):
                    mx = jnp.maximum(mx, s[:, g * LANES:(g + 1) * LANES])
                m_new = jnp.maximum(m_prev, jnp.max(mx, axis=1, keepdims=True))
                alpha = jnp.exp2(m_prev - m_new)
                p = jnp.exp2(s - jnp.tile(m_new, (1, kc // LANES))).astype(BF16)
                v_aug = jnp.concatenate([v_ref[0:kc, :], jnp.ones((kc, LANES), BF16)], axis=1)
                acc_sc[t, rows, :] = jnp.tile(alpha, (1, 2)) * acc_sc[t, rows, :] + jnp.dot(
                    p, v_aug, preferred_element_type=F32)
                m_sc[t, rows, :] = m_new

        @pl.when(j < i)
        def _():
            step(False)

        @pl.when(j == i)
        def _():
            step(True)
            for t in range(n):
                o_ref, lse_ref = outs[2 * t:2 * t + 2]
                l = acc_sc[t, :, DV:]
                o_ref[...] = (acc_sc[t, :, :DV] / l).astype(BF16)
                lse_ref[...] = m_sc[t] + jnp.log2(l)

    def qmap(g, i, j):
        return (g % H, (g // H) * nq + i, 0)

    def kmap(g, i, j):
        return (g % H, (g // H) * nq + jnp.minimum(j, i), 0)

    args = [a for st in streams for a in st[:3]]
    outs, landed = _call(
        body, args, name=name, grid=(B * H, nq, nq),
        in_specs=[pl.BlockSpec((None, tq, DK), qmap), pl.BlockSpec((None, tq, DK), kmap),
                  pl.BlockSpec((None, tq, DV), kmap)] * n,
        out_specs=[pl.BlockSpec((tq, DV), lambda g, i, j: ((g // H) * nq + i, g % H)),
                   pl.BlockSpec((None, tq, LANES), qmap)] * n,
        out_shape=[_sds((T, H * DV), BF16), _sds((H, T, LANES), F32)] * n,
        scratch_shapes=[pltpu.VMEM((n, tq, LANES), F32), pltpu.VMEM((n, tq, DV + LANES), F32)],
        sem=("parallel", "parallel", "arbitrary"), comm=comm)
    return [(outs[2 * t], outs[2 * t + 1]) for t in range(n)], landed


def _attn_bwd(streams, *, B, S, name, comm=None):
    n = len(streams)
    H, T, DK = streams[0][0].shape
    DV = streams[0][2].shape[2]
    tq = _tile(S, ATT_TILE)
    nq = S // tq
    sub = min(ATT_SUB, tq)
    NT = (((1,), (1,)), ((), ()))
    TN = (((0,), (0,)), ((), ()))

    def body(*refs):
        ins, outs = refs[:6 * n], refs[6 * n:]
        j, i = pl.program_id(1), pl.program_id(2)

        @pl.when(jnp.logical_and(j == 0, i == 0))
        def _():
            for t in range(n):
                outs[3 * t][...] = jnp.zeros(outs[3 * t].shape, F32)

        @pl.when(i == 0)
        def _():
            for t in range(n):
                outs[3 * t + 1][...] = jnp.zeros(outs[3 * t + 1].shape, F32)
                outs[3 * t + 2][...] = jnp.zeros(outs[3 * t + 2].shape, F32)

        def step(diagonal):
            work = [(t, r) for r in range(tq // sub) for t in range(n)]

            def kcols(r):
                return (r + 1) * sub if diagonal else tq

            def scores(t, r):
                q_ref, k_ref, v_ref, _, do_ref, _ = ins[6 * t:6 * t + 6]
                rows, kc = slice(r * sub, (r + 1) * sub), kcols(r)
                s = lax.dot_general(q_ref[rows, :], k_ref[0:kc, :], NT, preferred_element_type=F32)
                dp = lax.dot_general(do_ref[rows, :], v_ref[0:kc, :], NT, preferred_element_type=F32)
                return s * (streams[t][7] * LOG2_E), dp

            def probs(t, r, s, dp):
                _, _, _, o_ref, do_ref, lse_ref = ins[6 * t:6 * t + 6]
                rows, kc = slice(r * sub, (r + 1) * sub), kcols(r)
                if diagonal:
                    own = jnp.where(_visible(sub, sub, streams[t][6]), s[:, kc - sub:], NEG_INF)
                    s = own if kc == sub else jnp.concatenate([s[:, :kc - sub], own], axis=1)
                p = jnp.exp2(s - jnp.tile(lse_ref[rows, :], (1, kc // LANES)))
                delta = jnp.sum(do_ref[rows, :].astype(F32) * o_ref[rows, :].astype(F32), axis=1, keepdims=True)
                return p.astype(BF16), (p * (dp - delta) * streams[t][7]).astype(BF16)

            def grads(t, r, p, ds):
                q_ref, k_ref, _, _, do_ref, _ = ins[6 * t:6 * t + 6]
                dq_ref, dk_ref, dv_ref = outs[3 * t:3 * t + 3]
                rows, kc = slice(r * sub, (r + 1) * sub), kcols(r)
                dv_ref[0:kc, :] += lax.dot_general(p, do_ref[rows, :], TN, preferred_element_type=F32)
                dk_ref[0:kc, :] += lax.dot_general(ds, q_ref[rows, :], TN, preferred_element_type=F32)
                qrows = pl.ds(pl.multiple_of(i * tq + r * sub, sub), sub)
                dq_ref[qrows, :] += jnp.dot(ds, k_ref[0:kc, :], preferred_element_type=F32)

            nw = len(work)
            sc = {w: scores(*work[w]) for w in range(min(2, nw))}
            pr = {0: probs(*work[0], *sc.pop(0))}
            for w in range(nw):
                if w + 2 < nw:
                    sc[w + 2] = scores(*work[w + 2])
                if w + 1 < nw:
                    pr[w + 1] = probs(*work[w + 1], *sc.pop(w + 1))
                grads(*work[w], *pr.pop(w))

        @pl.when(i > j)
        def _():
            step(False)

        @pl.when(i == j)
        def _():
            step(True)

    def qmap(g, j, i):
        return (g % H, (g // H) * nq + jnp.maximum(i, j), 0)

    def kmap(g, j, i):
        return (g % H, (g // H) * nq + j, 0)

    def omap(g, j, i):
        return ((g // H) * nq + jnp.maximum(i, j), g % H)

    args = [a for st in streams for a in st[:6]]
    outs, landed = _call(
        body, args, name=name, grid=(B * H, nq, nq),
        in_specs=[pl.BlockSpec((None, tq, DK), qmap), pl.BlockSpec((None, tq, DK), kmap),
                  pl.BlockSpec((None, tq, DV), kmap), pl.BlockSpec((tq, DV), omap), pl.BlockSpec((tq, DV), omap),
                  pl.BlockSpec((None, tq, LANES), qmap)] * n,
        out_specs=[pl.BlockSpec((None, S, DK), lambda g, j, i: (g % H, g // H, 0)),
                   pl.BlockSpec((None, tq, DK), kmap), pl.BlockSpec((None, tq, DV), kmap)] * n,
        out_shape=[_sds((H, T, DK), F32), _sds((H, T, DK), F32), _sds((H, T, DV), F32)] * n,
        sem=("parallel", "arbitrary", "arbitrary"), comm=comm)
    return [tuple(outs[3 * t:3 * t + 3]) for t in range(n)], landed


def _gate_merge(am, af, proj, bgate, lay, D, comm=None):
    T = am.shape[0]
    tm = _tile(T, ROW_TILE, 16)
    tn = _tile(D, 1024)

    def body(am_ref, af_ref, gm_ref, gf_ref, bm_ref, bf_ref, o_ref):
        sm = _sigmoid(gm_ref[...] + bm_ref[...])
        sf = _sigmoid(gf_ref[...] + bf_ref[...])
        o_ref[...] = (sm * am_ref[...] + sf * af_ref[...]).astype(BF16)

    og = lay["g"] // tn
    blk = pl.BlockSpec((tm, tn), lambda i, j: (i, j))
    return _call(
        body, [am, af, proj, proj, bgate, bgate], name="gate_merge", grid=(T // tm, D // tn),
        in_specs=[blk, blk, pl.BlockSpec((tm, tn), lambda i, j: (i, og + j)),
                  pl.BlockSpec((tm, tn), lambda i, j: (i, og + D // tn + j)),
                  pl.BlockSpec((1, tn), lambda i, j: (0, j)), pl.BlockSpec((1, tn), lambda i, j: (0, D // tn + j))],
        out_specs=[blk], out_shape=[_sds((T, D), BF16)], sem=("parallel", "parallel"), comm=comm)


def _out_mid(merged, w_out, x, g_pm, g_ffn):
    T, D = x.shape
    tm = _tile(T, 2 * ROW_TILE, 16)

    def body(a_ref, w_ref, x_ref, gp_ref, gf_ref, y_ref, x1_ref, h2_ref):
        y = jnp.dot(a_ref[...], w_ref[...], preferred_element_type=F32)
        y_ref[...] = y
        x1 = x_ref[...] + y * _rms(y) * gp_ref[...]
        x1_ref[...] = x1
        h2_ref[...] = (x1 * _rms(x1) * gf_ref[...]).astype(BF16)

    row = pl.BlockSpec((tm, D), lambda i: (i, 0))
    vec = pl.BlockSpec((1, D), lambda i: (0, 0))
    return _call(body, [merged, w_out, x, g_pm, g_ffn], name="mm_out_mid", grid=(T // tm,),
                 in_specs=[row, pl.BlockSpec((D, D), lambda i: (0, 0)), row, vec, vec], out_specs=[row, row, row],
                 out_shape=[_sds((T, D), F32), _sds((T, D), F32), _sds((T, D), BF16)], sem=("parallel",))[0]


def _conv3(u, w_ref, bias):
    row = lax.broadcasted_iota(jnp.int32, u.shape, 0)
    u1 = jnp.where(row >= 1, pltpu.roll(u, 1, 0), 0.0)
    u2 = jnp.where(row >= 2, pltpu.roll(u, 2, 0), 0.0)
    return w_ref[0:1, :] * u2 + w_ref[1:2, :] * u1 + w_ref[2:3, :] * u + bias, u1, u2


def _convffn_fwd(u, cw, cb, B, S, F, comm=None):
    T = u.shape[0]
    tn = _tile(F, 256)
    nf = F // tn

    def body(ug_ref, uv_ref, wg_ref, wv_ref, bg_ref, bv_ref, a_ref):
        g, _, _ = _conv3(ug_ref[...], wg_ref, bg_ref[...])
        val, _, _ = _conv3(uv_ref[...], wv_ref, bv_ref[...])
        a_ref[...] = (_gelu_parts(g)[0] * val).astype(BF16)

    def seq(off):
        return pl.BlockSpec((S, tn), lambda b, j: (b, off + j))

    def par(rows, off):
        return pl.BlockSpec((rows, tn), lambda b, j: (0, off + j))

    return _call(body, [u, u, cw, cw, cb, cb], name="convffn_fwd", grid=(B, nf),
                 in_specs=[seq(0), seq(nf), par(3, 0), par(3, nf), par(1, 0), par(1, nf)],
                 out_specs=[seq(0)], out_shape=[_sds((T, F), BF16)], sem=("parallel", "parallel"), comm=comm)


def _convffn_bwd(u, dact, cw, cb, B, S, F, comm=None):
    T = u.shape[0]
    tn = _tile(F, 256)
    nf = F // tn

    def body(ug_ref, uv_ref, da_ref, wg_ref, wv_ref, bg_ref, bv_ref, dug_ref, duv_ref, dpg_ref, dpv_ref):
        b = pl.program_id(1)
        ug, uv, da = ug_ref[...], uv_ref[...], da_ref[...]
        g, ug1, ug2 = _conv3(ug, wg_ref, bg_ref[...])
        val, uv1, uv2 = _conv3(uv, wv_ref, bv_ref[...])
        gel, dgel = _gelu_parts(g)
        dg = da * val * dgel
        dval = da * gel
        row = lax.broadcasted_iota(jnp.int32, ug.shape, 0)

        def back(d, w_ref):
            d1 = jnp.where(row < S - 1, pltpu.roll(d, S - 1, 0), 0.0)
            d2 = jnp.where(row < S - 2, pltpu.roll(d, S - 2, 0), 0.0)
            return w_ref[2:3, :] * d + w_ref[1:2, :] * d1 + w_ref[0:1, :] * d2

        dug_ref[...] = back(dg, wg_ref).astype(BF16)
        duv_ref[...] = back(dval, wv_ref).astype(BF16)

        def sums(d, u0, u1, u2):
            r8 = lax.broadcasted_iota(jnp.int32, (8, d.shape[1]), 0)
            out = jnp.zeros((8, d.shape[1]), F32)
            for k, t in enumerate((d * u2, d * u1, d * u0, d)):
                out = jnp.where(r8 == k, jnp.sum(t, axis=0, keepdims=True), out)
            return out

        _accumulate(dpg_ref, sums(dg, ug, ug1, ug2), b == 0)
        _accumulate(dpv_ref, sums(dval, uv, uv1, uv2), b == 0)

    def seq(off):
        return pl.BlockSpec((S, tn), lambda j, b: (b, off + j))

    def par(rows, off):
        return pl.BlockSpec((rows, tn), lambda j, b: (0, off + j))

    outs, landed = _call(
        body, [u, u, dact, cw, cw, cb, cb], name="convffn_bwd", grid=(nf, B),
        in_specs=[seq(0), seq(nf), seq(0), par(3, 0), par(3, nf), par(1, 0), par(1, nf)],
        out_specs=[seq(0), seq(0), par(8, 0), par(8, 0)],
        out_shape=[_sds((T, F), BF16), _sds((T, F), BF16), _sds((8, F), F32), _sds((8, F), F32)],
        sem=("parallel", "arbitrary"), comm=comm)
    return outs, landed


def _tail(ff, x1, tgt, g):
    T, D = ff.shape
    tm = _tile(T, ROW_TILE, 16)

    def body(ff_ref, x1_ref, t_ref, g_ref, dy_ref, dff_ref, loss_ref, dg_ref):
        i = pl.program_id(0)
        f = ff_ref[...]
        gv = g_ref[...]
        r = _rms(f)
        n = f * r
        e = (x1_ref[...] + n * gv) - t_ref[...]
        dy = e * (1.0 / D)
        dy_ref[...] = dy
        dn = dy * gv
        dff_ref[...] = (r * (dn - n * jnp.mean(dn * n, axis=-1, keepdims=True))).astype(BF16)
        part = 0.5 * jnp.sum(jnp.mean(e * e, axis=-1, keepdims=True), axis=0, keepdims=True)
        _accumulate(loss_ref, jnp.broadcast_to(part, loss_ref.shape), i == 0)
        _accumulate(dg_ref, jnp.sum(dy * n, axis=0, keepdims=True), i == 0)

    row = pl.BlockSpec((tm, D), lambda i: (i, 0))
    vec = pl.BlockSpec((1, D), lambda i: (0, 0))
    return _call(body, [ff, x1, tgt, g], name="tail", grid=(T // tm,), in_specs=[row, row, row, vec],
                 out_specs=[row, row, pl.BlockSpec((8, LANES), lambda i: (0, 0)), vec],
                 out_shape=[_sds((T, D), F32), _sds((T, D), BF16), _sds((8, LANES), F32), _sds((1, D), F32)],
                 sem=("arbitrary",))[0]


def _mid_bwd(dy, dh2, x1, y1, g_ffn, g_pm, comm=None):
    T, D = dy.shape
    tm = _tile(T, ROW_TILE, 16)

    def body(dy_ref, dh_ref, x1_ref, y1_ref, gf_ref, gp_ref, dx1_ref, dy1_ref, dgf_ref, dgp_ref):
        i = pl.program_id(0)
        dh = dh_ref[...]
        d2, dgf = _rms_bwd(dh, x1_ref[...], gf_ref[...])
        dx1 = dy_ref[...] + d2
        dx1_ref[...] = dx1
        d1, dgp = _rms_bwd(dx1, y1_ref[...], gp_ref[...])
        dy1_ref[...] = d1.astype(BF16)
        _accumulate(dgf_ref, jnp.sum(dgf, axis=0, keepdims=True), i == 0)
        _accumulate(dgp_ref, jnp.sum(dgp, axis=0, keepdims=True), i == 0)

    row = pl.BlockSpec((tm, D), lambda i: (i, 0))
    vec = pl.BlockSpec((1, D), lambda i: (0, 0))
    return _call(body, [dy, dh2, x1, y1, g_ffn, g_pm], name="mid_bwd", grid=(T // tm,),
                 in_specs=[row, row, row, row, vec, vec], out_specs=[row, row, vec, vec],
                 out_shape=[_sds((T, D), F32), _sds((T, D), BF16), _sds((1, D), F32), _sds((1, D), F32)],
                 sem=("arbitrary",), comm=comm)


def _gate_bwd(dm, am, af, proj, bgate, lay, D, comm=None):
    T = dm.shape[0]
    tm = _tile(T, ROW_TILE, 16)
    tn = _tile(D, 512)

    def body(dm_ref, am_ref, af_ref, gm_ref, gf_ref, bm_ref, bf_ref,
             dam_ref, daf_ref, dgm_ref, dgf_ref, dbm_ref, dbf_ref):
        i = pl.program_id(1)
        d = dm_ref[...]
        sm = _sigmoid(gm_ref[...] + bm_ref[...])
        sf = _sigmoid(gf_ref[...] + bf_ref[...])
        dam_ref[...] = (d * sm).astype(BF16)
        daf_ref[...] = (d * sf).astype(BF16)
        dgm = d * am_ref[...] * (sm * (1.0 - sm))
        dgf = d * af_ref[...] * (sf * (1.0 - sf))
        dgm_ref[...] = dgm.astype(BF16)
        dgf_ref[...] = dgf.astype(BF16)
        _accumulate(dbm_ref, jnp.sum(dgm, axis=0, keepdims=True), i == 0)
        _accumulate(dbf_ref, jnp.sum(dgf, axis=0, keepdims=True), i == 0)

    og = lay["g"] // tn
    blk = pl.BlockSpec((tm, tn), lambda j, i: (i, j))
    vec = pl.BlockSpec((1, tn), lambda j, i: (0, j))
    return _call(
        body, [dm, am, af, proj, proj, bgate, bgate], name="gate_bwd", grid=(D // tn, T // tm),
        in_specs=[blk, blk, blk, pl.BlockSpec((tm, tn), lambda j, i: (i, og + j)),
                  pl.BlockSpec((tm, tn), lambda j, i: (i, og + D // tn + j)),
                  vec, pl.BlockSpec((1, tn), lambda j, i: (0, D // tn + j))],
        out_specs=[blk, blk, blk, blk, vec, vec],
        out_shape=[_sds((T, D), BF16)] * 4 + [_sds((1, D), F32)] * 2, sem=("parallel", "arbitrary"), comm=comm)


def _mla_bwd_prep(dq, dk, dv, cosT, sinT, comm=None):
    H, T, _ = dq.shape
    tm = _tile(T, HEAD_ROW_TILE, 16)

    def body(dq_ref, dk_ref, dv_ref, cos_ref, sin_ref, dqr_ref, dkv_ref, dkpe_ref):
        h = pl.program_id(1)
        cs, sn = cos_ref[...], sin_ref[...]
        valid = _lane(cs.shape) < ROPE

        def unrope(d):
            d = jnp.where(valid, d, 0.0)
            return d * cs - _rope_rot(d) * sn

        dqv = dq_ref[...]
        dqr_ref[:, :NOPE] = dqv[:, :NOPE].astype(BF16)
        dqr_ref[:, NOPE:] = unrope(dqv[:, NOPE:]).astype(BF16)
        dkv_ = dk_ref[...]
        dkv_ref[:, :NOPE] = dkv_[:, :NOPE].astype(BF16)
        dkv_ref[:, NOPE:] = dv_ref[...].astype(BF16)
        _accumulate(dkpe_ref, unrope(dkv_[:, NOPE:]), h == 0)

    head = pl.BlockSpec((None, tm, ATT_DK), lambda i, h: (h, i, 0))
    tok = pl.BlockSpec((tm, LANES), lambda i, h: (i, 0))
    return _call(
        body, [dq, dk, dv, cosT, sinT], name="mla_bwd_prep", grid=(T // tm, H),
        in_specs=[head, head, pl.BlockSpec((None, tm, VDIM), lambda i, h: (h, i, 0)), tok, tok],
        out_specs=[head, head, tok],
        out_shape=[_sds((H, T, ATT_DK), BF16), _sds((H, T, ATT_DK), BF16), _sds((T, LANES), F32)],
        sem=("parallel", "arbitrary"), comm=comm)


def _fox_bwd_prep(dq, dk, proj, bfor, lay, B, S, inv_scale):
    H, T, _ = dq.shape

    def body(dq_ref, dk_ref, fl_ref, bf_ref, dfl_ref, dbf_ref, dc_sc):
        b, h = pl.program_id(0), pl.program_id(1)
        lane = _lane(dc_sc.shape)
        col = jnp.sum(jnp.where(lane == 0, dq_ref[...], 0.0) - jnp.where(lane == 3, dk_ref[...], 0.0),
                      axis=1, keepdims=True)

        @pl.when(h == 0)
        def _():
            dc_sc[...] = jnp.zeros(dc_sc.shape, F32)

        dc_sc[...] = jnp.where(lane == h, col, dc_sc[...])

        @pl.when(h == H - 1)
        def _():
            dlogf = _cumsum_rows(dc_sc[...] * inv_scale, reverse=True)
            z = fl_ref[...] + bf_ref[...]
            dz = jnp.where(lane < H, dlogf * (1.0 / (1.0 + jnp.exp(z))), 0.0)
            dfl_ref[...] = dz
            _accumulate(dbf_ref, jnp.sum(dz, axis=0, keepdims=True), b == 0)

    aug = pl.BlockSpec((None, S, LANES), lambda b, h: (h, b, 1))
    seq = pl.BlockSpec((S, LANES), lambda b, h: (b, 0))
    vec = pl.BlockSpec((1, LANES), lambda b, h: (0, 0))
    return _call(
        body, [dq, dk, proj, bfor], name="fox_bwd_prep", grid=(B, H),
        in_specs=[aug, aug, pl.BlockSpec((S, LANES), lambda b, h: (b, lay["fl"] // LANES)), vec],
        out_specs=[seq, vec], out_shape=[_sds((T, LANES), F32), _sds((1, LANES), F32)],
        scratch_shapes=[pltpu.VMEM((S, LANES), F32)], sem=("arbitrary", "arbitrary"))[0]


def _lat_bwd(dqn, dkvn, proj, gq, gkv, lay):
    T = dqn.shape[0]
    tm = _tile(T, ROW_TILE, 16)

    def body(dq_ref, dkv_ref, q_ref, kv_ref, gq_ref, gkv_ref, dql_ref, dkl_ref, dgq_ref, dgkv_ref):
        i = pl.program_id(0)
        dql, dgq = _rms_bwd(dq_ref[...], q_ref[...], gq_ref[...])
        dkl, dgkv = _rms_bwd(dkv_ref[...], kv_ref[...], gkv_ref[...])
        dql_ref[...] = dql.astype(BF16)
        dkl_ref[...] = dkl.astype(BF16)
        _accumulate(dgq_ref, jnp.sum(dgq, axis=0, keepdims=True), i == 0)
        _accumulate(dgkv_ref, jnp.sum(dgkv, axis=0, keepdims=True), i == 0)

    def blk(width, off=0):
        return pl.BlockSpec((tm, width), lambda i: (i, off // width))

    def vec(width):
        return pl.BlockSpec((1, width), lambda i: (0, 0))

    return _call(
        body, [dqn, dkvn, proj, proj, gq, gkv], name="lat_bwd", grid=(T // tm,),
        in_specs=[blk(Q_LORA), blk(KV_LORA), blk(Q_LORA, lay["q"]), blk(KV_LORA, lay["kv"]), vec(Q_LORA), vec(KV_LORA)],
        out_specs=[blk(Q_LORA), blk(KV_LORA), vec(Q_LORA), vec(KV_LORA)],
        out_shape=[_sds((T, Q_LORA), BF16), _sds((T, KV_LORA), BF16), _sds((1, Q_LORA), F32), _sds((1, KV_LORA), F32)],
        sem=("arbitrary",))[0]


def _dh_final(dproj, w_perm, dx1, x, g, comm=None):
    T, D = x.shape
    K = dproj.shape[1]
    tm = _tile(T, 2 * ROW_TILE, 16)
    tk = _tile(K, 1024)
    nk = K // tk
    NT = (((1,), (1,)), ((), ()))

    def body(a_ref, b_ref, dx1_ref, x_ref, g_ref, dx_ref, dg_ref):
        i, k = pl.program_id(0), pl.program_id(1)

        @pl.when(k == 0)
        def _():
            dx_ref[...] = jnp.zeros(dx_ref.shape, F32)

        dx_ref[...] += lax.dot_general(a_ref[...], b_ref[...], NT, preferred_element_type=F32)

        @pl.when(k == nk - 1)
        def _():
            d, dg = _rms_bwd(dx_ref[...], x_ref[...], g_ref[...])
            dx_ref[...] = dx1_ref[...] + d
            _accumulate(dg_ref, jnp.sum(dg, axis=0, keepdims=True), i == 0)

    row = pl.BlockSpec((tm, D), lambda i, k: (i, 0))
    vec = pl.BlockSpec((1, D), lambda i, k: (0, 0))
    return _call(body, [dproj, w_perm, dx1, x, g], name="mm_dh_final", grid=(T // tm, nk),
                 in_specs=[pl.BlockSpec((tm, tk), lambda i, k: (i, k)), pl.BlockSpec((D, tk), lambda i, k: (0, k)),
                           row, row, vec],
                 out_specs=[row, vec], out_shape=[_sds((T, D), F32), _sds((1, D), F32)],
                 sem=("arbitrary", "arbitrary"), comm=comm)


def _chip_sum(pieces, paired, qc, name):
    G, R, C = pieces.shape
    tr = _tile(R, 256, 16)

    def body(qc_ref, g_ref, p_ref, keep_ref, send_ref):
        s = pl.program_id(1)
        tot = g_ref[...] + p_ref[...]

        @pl.when(s == 0)
        def _():
            keep_ref[...] = tot

        @pl.when(s > 0)
        def _():
            send_ref[...] = tot.astype(send_ref.dtype)

    grid_spec = pltpu.PrefetchScalarGridSpec(
        num_scalar_prefetch=1, grid=(R // tr, N_CHIP),
        in_specs=[pl.BlockSpec((None, tr, C), lambda i, s, qc: (2 * (qc[0] ^ s) + qc[1], i, 0)),
                  pl.BlockSpec((None, tr, C), lambda i, s, qc: (qc[0] ^ s, i, 0))],
        out_specs=[pl.BlockSpec((tr, C), lambda i, s, qc: (i, 0)),
                   pl.BlockSpec((None, tr, C), lambda i, s, qc: (jnp.maximum(s - 1, 0), i, 0))])
    send_dtype = BF16 if R >= 16 else pieces.dtype
    return pl.pallas_call(
        body, name=name, grid_spec=grid_spec,
        out_shape=[_sds((R, C), F32), _sds((3, R, C), send_dtype)],
        compiler_params=pltpu.CompilerParams(dimension_semantics=("arbitrary", "arbitrary"),
                                             vmem_limit_bytes=VMEM_LIMIT_BYTES),
    )(qc, pieces, paired)


def _adamw_math(w, g, m, v):
    m = ADAM_B1 * m + (1.0 - ADAM_B1) * g
    v = ADAM_B2 * v + (1.0 - ADAM_B2) * (g * g)
    m_hat = m / (1.0 - ADAM_B1 ** ADAM_STEP)
    v_hat = v / (1.0 - ADAM_B2 ** ADAM_STEP)
    delta = -ADAM_LR * (m_hat / (jnp.sqrt(v_hat) + ADAM_EPS) + ADAM_WD * w)
    return delta, m, v


def _sum_adamw(keep, pieces, w, m, v, name):
    R, C = w.shape
    P = pieces.shape[0]
    tr = _tile(R, 256, 16)

    def body(k_ref, p_ref, w_ref, m_ref, v_ref, g_ref, d_ref, mo_ref, vo_ref):
        g = k_ref[...]
        for q in range(P):
            g = g + p_ref[q].astype(F32)
        g_ref[...] = g
        d_ref[...], mo_ref[...], vo_ref[...] = _adamw_math(w_ref[...], g, m_ref[...], v_ref[...])

    blk = pl.BlockSpec((tr, C), lambda i: (i, 0))
    pblk = pl.BlockSpec((P, tr, C), lambda i: (0, i, 0))
    return _call(body, [keep, pieces, w, m, v], name=name, grid=(R // tr,), in_specs=[blk, pblk, blk, blk, blk],
                 out_specs=[blk] * 4, out_shape=[_sds((R, C), F32)] * 4, sem=("parallel",))[0]


def _adamw_small(parts, w, m, v, widths):
    n = len(widths)

    def body(p_ref, w_ref, m_ref, v_ref, *o_refs):
        g = p_ref[0]
        for q in range(1, N_DEV):
            g = g + p_ref[q]
        vals = (g,) + _adamw_math(w_ref[...], g, m_ref[...], v_ref[...])
        off = 0
        for i, wd in enumerate(widths):
            for kind in range(4):
                o_refs[4 * i + kind][...] = vals[kind][:, off:off + wd]
            off += wd

    whole = pl.BlockSpec(memory_space=pltpu.VMEM)
    outs = _call(body, [parts, w, m, v], name="adamw_small", grid=(), in_specs=[whole] * 4,
                 out_specs=[whole] * (4 * n), out_shape=[_sds((1, wd), F32) for wd in widths for _ in range(4)])[0]
    return [tuple(outs[4 * i:4 * i + 4]) for i in range(n)]


def _layout(D):
    lay = {"q": 0, "kv": Q_LORA, "kpe": Q_LORA + KV_LORA}
    lay["fq"] = lay["kpe"] + LANES
    lay["fk"] = lay["fq"] + HEADS * FOX_DIM
    lay["fv"] = lay["fk"] + HEADS * FOX_DIM
    lay["fl"] = lay["fv"] + HEADS * FOX_DIM
    lay["g"] = lay["fl"] + LANES
    lay["end"] = lay["g"] + 2 * D
    return lay


def kernel(x, positions, pre_mix_norm, w_in, q_a_norm, w_uq, kv_a_norm, w_ukv, b_forget, b_gate, w_branch_mla, w_branch_fox, w_out, post_mix_norm, pre_ffn_norm, w_up, conv_w, conv_b, w_down, post_ffn_norm, loss_target, m_pre_mix_norm, m_w_in, m_q_a_norm, m_w_uq, m_kv_a_norm, m_w_ukv, m_b_forget, m_b_gate, m_w_branch_mla, m_w_branch_fox, m_w_out, m_post_mix_norm, m_pre_ffn_norm, m_w_up, m_conv_w, m_conv_b, m_w_down, m_post_ffn_norm, v_pre_mix_norm, v_w_in, v_q_a_norm, v_w_uq, v_kv_a_norm, v_w_ukv, v_b_forget, v_b_gate, v_w_branch_mla, v_w_branch_fox, v_w_out, v_post_mix_norm, v_pre_ffn_norm, v_w_up, v_conv_w, v_conv_b, v_w_down, v_post_ffn_norm):
    B, S, D = x.shape
    T = B * S
    F = conv_b.shape[0] // 2
    lay = _layout(D)
    n_in = w_in.shape[1]
    d_in = N_DEV * n_in
    seg_a = Q_LORA + KV_LORA + ROPE
    seg_b = 3 * HEADS * FOX_DIM + HEADS
    mla_scale = (NOPE + ROPE) ** -0.5
    fox_scale = FOX_DIM ** -0.5
    ax, ay, ac = (lax.axis_index(a) for a in MESH_AXES)
    qc = jnp.stack([2 * ax + ay, ac]).astype(jnp.int32)

    def row(vec, width=None):
        vec = vec.reshape(1, -1)
        if width is not None and vec.shape[1] < width:
            vec = jnp.pad(vec, ((0, 0), (0, width - vec.shape[1])))
        return vec

    x2 = x.reshape(T, D)
    win_s = _cast_bf16(w_in, "cast_w_in")
    h, (win_g,) = _prenorm(x2, row(pre_mix_norm), comm=_Comm([_GatherRelayPlan([win_s], mid_frac=0.3)]))
    small_s = [_cast_bf16(w, "cast_" + n) for w, n in
               [(w_uq, "w_uq"), (w_ukv, "w_ukv"), (w_branch_mla, "w_branch_mla"), (w_branch_fox, "w_branch_fox"), (w_out, "w_out")]]
    wup_s = _cast_bf16(w_up, "cast_w_up")
    wdown_s = _cast_bf16(w_down, "cast_w_down")

    def shard_cols(lo, hi):
        out = []
        for g in range(lo // n_in, (hi - 1) // n_in + 1):
            out.append(win_g[g][:, max(lo, g * n_in) - g * n_in:min(hi, (g + 1) * n_in) - g * n_in])
        return out

    w_perm = jnp.concatenate(
        shard_cols(0, seg_a) + [jnp.zeros((D, LANES - ROPE), BF16)] + shard_cols(seg_a, seg_a + seg_b)
        + [jnp.zeros((D, LANES - HEADS), BF16)] + shard_cols(seg_a + seg_b, d_in), axis=1)

    tgt = loss_target.reshape(T, D)
    pos = positions.reshape(T, 1)
    inv_freq = 1.0 / (ROPE_THETA ** (jnp.arange(0, ROPE, 2, dtype=F32) / ROPE))
    invf = row(jnp.concatenate([inv_freq, inv_freq]), LANES)
    g_pre, g_q, g_kv = row(pre_mix_norm), row(q_a_norm), row(kv_a_norm)
    g_pm, g_ffn, g_pf = row(post_mix_norm), row(pre_ffn_norm), row(post_ffn_norm)
    bfor = row(b_forget, LANES)
    bgate = row(b_gate)
    cb_full = row(conv_b)

    def own_plan(blocks):
        return _Comm([_GatherOwnPlan(blocks)])

    def pass_plan(gathered):
        return _Comm([_GatherPassPlan(gathered)])

    def pair_plan(gs):
        return _Comm([_PairScatterPlan(gs)])

    def chip_plan(gs):
        return _Comm([_ChipScatterPlan(gs)])

    half_d = D // 2
    proj, landed = _matmul(h, w_perm, mode="nn", name="mm_proj", comm=_Comm(
        [_GatherOwnPlan(small_s[:2] + [conv_w]), _GatherOwnPlan([wup_s], rows=(0, half_d))]))
    early_g, wup_part = landed[:-1], landed[-1:]
    (qn, kvn, kper, logf, cosT, sinT), (wuq_g, wukv_g, cw_g) = _split_prep(
        proj, pos, invf, g_q, g_kv, bfor, lay, comm=pass_plan(early_g))
    wuq_pad = jnp.pad(wuq_g, ((0, 0), (0, 0), (0, ATT_DK - NOPE - ROPE)))
    cw_full = jnp.transpose(cw_g, (1, 0, 2)).reshape(3, 2 * F)

    qraw = _matmul(qn, wuq_pad, mode="nn", name="mm_q", out_blocks=ATT_DK, tm=T)
    kvraw = _matmul(kvn, wukv_g, mode="nn", name="mm_kv", out_blocks=NOPE + VDIM, tm=T)
    (q_mla, k_mla, v_mla), branch_half = _mla_prep(qraw, kvraw, kper, cosT, sinT, comm=own_plan(small_s[2:4]))
    cs = _fox_cumsum(logf, B, S, 1.0 / fox_scale)
    (q_fox, k_fox, v_fox), wout_half = _fox_prep(proj, cs, lay, comm=own_plan(small_s[4:5]))
    ((o_mla, lse_mla), (o_fox, lse_fox)), landed = _attn_fwd(
        [(q_mla, k_mla, v_mla, MLA_UNIT, mla_scale), (q_fox, k_fox, v_fox, 1, fox_scale)], B=B, S=S,
        name="attn_fwd", comm=_Comm([_GatherOwnPlan([wup_s], rows=(half_d, D), into=wup_part),
                                     _GatherPassPlan(branch_half + wout_half)]))
    wup_half, (wbm_g, wbf_g, wout_g) = landed[:1], landed[1:]
    wbm = jnp.transpose(wbm_g, (1, 0, 2)).reshape(HEADS * VDIM, D)
    wbf = jnp.transpose(wbf_g, (1, 0, 2)).reshape(HEADS * FOX_DIM, D)
    wout = wout_g.reshape(D, D)
    a_m = _matmul(o_mla, wbm, mode="nn", name="mm_branch_mla", tm=2 * MM_TILE)
    a_f = _matmul(o_fox, wbf, mode="nn", name="mm_branch_fox", tm=2 * MM_TILE)
    (merged,), (wup_g,) = _gate_merge(a_m, a_f, proj, bgate, lay, D, comm=pass_plan(wup_half))
    n_up = wup_g.shape[2]
    y1, x1, h2 = _out_mid(merged, wout, x2, g_pm, g_ffn)
    u, wdown_half = _matmul(h2, wup_g, mode="nn", name="mm_up", tn=n_up, comm=own_plan([wdown_s]))
    (act,), (wdown_g,) = _convffn_fwd(u, cw_full, cb_full, B, S, F, comm=pass_plan(wdown_half))
    wdown = wdown_g.reshape(F, D)
    ff = _matmul(act, wdown, mode="nn", name="mm_down", tk=F // 2)
    dy, dff, loss_part, dg_pf = _tail(ff, x1, tgt, g_pf)

    dact = _matmul(dff, wdown, mode="nt", name="mm_dact", tn=F // 4)
    dw_down = _matmul(act, dff, mode="tn", name="mm_dw_down", tm=F // 4, tn=512).reshape(N_DEV, F // N_DEV, D)
    (du_g, du_v, dcp_g, dcp_v), (pa_down,) = _convffn_bwd(u, dact, cw_full, cb_full, B, S, F, comm=pair_plan([dw_down]))
    keep_down, sb_down = _chip_sum(dw_down, pa_down, qc, "chipsum_w_down")
    dh2, (rb_down,) = _matmul_halves((du_g, du_v), wup_g, mode="nt", name="mm_dh2", tm=MM_TILE, comm=chip_plan([sb_down]))
    dw_up = _matmul_halves(h2, (du_g, du_v), mode="tn", name="mm_dw_up", tm=MM_TILE, tn=n_up, tk=T // 2)
    (dx1, dy1, dg_ffn, dg_pm), _ = _mid_bwd(dy, dh2, x1, y1, g_ffn, g_pm)
    dmerged = _matmul(dy1, wout, mode="nt", name="mm_dmerged")
    dw_out = _matmul(merged, dy1, mode="tn", name="mm_dw_out").reshape(N_DEV, D // N_DEV, D)
    (da_m, da_f, dgl_m, dgl_f, dbg_m, dbg_f), (pa_up,) = _gate_bwd(
        dmerged, a_m, a_f, proj, bgate, lay, D, comm=pair_plan([dw_up]))
    keep_up, sb_up = _chip_sum(dw_up, pa_up, qc, "chipsum_w_up")
    dw_bm = _matmul(o_mla, da_m, mode="tn", name="mm_dw_branch_mla", out_blocks=D // N_DEV)
    dw_bf = _matmul(o_fox, da_f, mode="tn", name="mm_dw_branch_fox", out_blocks=D // N_DEV)
    mix = [dw_out, dw_bm, dw_bf]
    do_mla, pa_mix = _matmul(da_m, wbm, mode="nt", name="mm_do_mla", out_dtype=BF16, comm=pair_plan(mix))
    do_fox = _matmul(da_f, wbf, mode="nt", name="mm_do_fox", out_dtype=BF16)
    mix_sums = [_chip_sum(g, p, qc, "chipsum_" + n) for g, p, n in zip(mix, pa_mix, ["w_out", "w_branch_mla", "w_branch_fox"])]
    ((dq_m, dk_m, dv_m), (dq_f, dk_f, dv_f)), (rb_up,) = _attn_bwd(
        [(q_mla, k_mla, v_mla, o_mla, do_mla, lse_mla, MLA_UNIT, mla_scale),
         (q_fox, k_fox, v_fox, o_fox, do_fox, lse_fox, 1, fox_scale)], B=B, S=S, name="attn_bwd",
        comm=chip_plan([sb_up]))
    (dqraw, dkvraw, dkpe), _ = _mla_bwd_prep(dq_m, dk_m, dv_m, cosT, sinT)
    dqn = _matmul(dqraw, wuq_pad, mode="nt", name="mm_dqn", tm=T)
    dw_uq = _matmul(qn, dqraw, mode="tn", name="mm_dw_uq", out_blocks=ATT_DK)[:, :, :NOPE + ROPE]
    dkvn = _matmul(dkvraw, wukv_g, mode="nt", name="mm_dkvn", tm=T)
    dw_ukv = _matmul(kvn, dkvraw, mode="tn", name="mm_dw_ukv", out_blocks=NOPE + VDIM)
    dqlat, dkvlat, dg_q, dg_kv = _lat_bwd(dqn, dkvn, proj, g_q, g_kv, lay)
    dfl, dbfor = _fox_bwd_prep(dq_f, dk_f, proj, bfor, lay, B, S, 1.0 / fox_scale)
    dproj = _concat_cols([dqlat, dkvlat, dkpe, dq_f, dk_f, dv_f, dfl, dgl_m, dgl_f], "concat_dproj")
    dw_perm, rb_mix = _matmul(h, dproj, mode="tn", name="mm_dw_in", comm=chip_plan([s[1] for s in mix_sums]))
    segs = [(0, seg_a, 0), (seg_a, seg_a + seg_b, lay["fq"] - seg_a), (seg_a + seg_b, d_in, lay["g"] - seg_a - seg_b)]

    def piece(g):
        lo, hi = g * n_in, (g + 1) * n_in
        parts = [dw_perm[:, max(lo, s0) + sh:min(hi, s1) + sh] for s0, s1, sh in segs if max(lo, s0) < min(hi, s1)]
        return parts[0] if len(parts) == 1 else jnp.concatenate(parts, axis=1)

    dw_in = jnp.stack([piece(g) for g in range(N_DEV)])
    dcw = jnp.transpose(jnp.concatenate([dcp_g[0:3], dcp_v[0:3]], axis=1).reshape(3, N_DEV, (2 * F) // N_DEV), (1, 0, 2))
    late = [dw_in, dw_uq, dw_ukv, dcw]
    pa_late = _exchange_alone(pair_plan(late), "pair_late")
    late_sums = [_chip_sum(g, p, qc, "chipsum_" + n) for g, p, n in zip(late, pa_late, ["w_in", "w_uq", "w_ukv", "conv_w"])]
    (grad_x, dg_pre), rb_late = _dh_final(dproj, w_perm, dx1, x2, g_pre, comm=chip_plan([s[1] for s in late_sums]))

    big_out = {}

    def finish(n, keep, pieces, w, m, v):
        big_out[n] = _sum_adamw(keep, pieces, w, m, v, "adamw_" + n)

    finish("w_down", keep_down, rb_down, w_down, m_w_down, v_w_down)
    finish("w_up", keep_up, rb_up, w_up, m_w_up, v_w_up)
    finish("w_out", mix_sums[0][0], rb_mix[0], w_out, m_w_out, v_w_out)
    finish("w_branch_mla", mix_sums[1][0], rb_mix[1], w_branch_mla, m_w_branch_mla, v_w_branch_mla)
    finish("w_branch_fox", mix_sums[2][0], rb_mix[2], w_branch_fox, m_w_branch_fox, v_w_branch_fox)
    finish("w_in", late_sums[0][0], rb_late[0], w_in, m_w_in, v_w_in)
    finish("w_uq", late_sums[1][0], rb_late[1], w_uq, m_w_uq, v_w_uq)
    finish("w_ukv", late_sums[2][0], rb_late[2], w_ukv, m_w_ukv, v_w_ukv)
    finish("conv_w", late_sums[3][0], rb_late[3], conv_w, m_conv_w, v_conv_w)

    widths = [D, Q_LORA, KV_LORA, LANES, 2 * D, D, D, 2 * F, D]
    small_names = ["pre_mix_norm", "q_a_norm", "kv_a_norm", "b_forget", "b_gate", "post_mix_norm", "pre_ffn_norm",
                   "conv_b", "post_ffn_norm"]
    true_w = [D, Q_LORA, KV_LORA, HEADS, 2 * D, D, D, 2 * F, D]
    dcb = jnp.concatenate([dcp_g[3:4], dcp_v[3:4]], axis=1)
    part = jnp.concatenate([dg_pre, dg_q, dg_kv, dbfor, dbg_m, dbg_f, dg_pm, dg_ffn, dcb, dg_pf], axis=1)

    def pack(vals):
        return jnp.concatenate([row(a, wd) for a, wd in zip(vals, widths)], axis=1)

    sw = pack([pre_mix_norm, q_a_norm, kv_a_norm, b_forget, b_gate, post_mix_norm, pre_ffn_norm, conv_b, post_ffn_norm])
    sm = pack([m_pre_mix_norm, m_q_a_norm, m_kv_a_norm, m_b_forget, m_b_gate, m_post_mix_norm, m_pre_ffn_norm,
               m_conv_b, m_post_ffn_norm])
    sv = pack([v_pre_mix_norm, v_q_a_norm, v_kv_a_norm, v_b_forget, v_b_gate, v_post_mix_norm, v_pre_ffn_norm,
               v_conv_b, v_post_ffn_norm])
    (parts_all,) = _exchange_alone(_Comm([_DirectGatherPlan([part])]), "gather_small")
    small = _adamw_small(parts_all, sw, sm, sv, widths)
    small_out = {n: tuple(a.reshape(-1)[:tw] for a in vals) for n, vals, tw in zip(small_names, small, true_w)}

    loss = lax.psum(loss_part[0, 0], MESH_AXES)
    order = ["pre_mix_norm", "w_in", "q_a_norm", "w_uq", "kv_a_norm", "w_ukv", "b_forget", "b_gate", "w_branch_mla",
             "w_branch_fox", "w_out", "post_mix_norm", "pre_ffn_norm", "w_up", "conv_w", "conv_b", "w_down",
             "post_ffn_norm"]
    res = {**big_out, **small_out}
    outs = [loss, grad_x.reshape(B, S, D)]
    for kind in range(4):
        outs += [res[n][kind] for n in order]
    return tuple(outs)
```

```python
import math

import jax
import jax.numpy as jnp
from jax import lax
from jax.experimental import pallas as pl
from jax.experimental.pallas import tpu as pltpu

F32 = jnp.float32
BF16 = jnp.bfloat16

N_DEV = 8
N_CHIP = 4
HEADS = 8
NOPE = 128
ROPE = 64
HALF_ROPE = ROPE // 2
VDIM = 128
Q_LORA = 512
KV_LORA = 256
FOX_DIM = 128
ATT_DK = 256
MLA_UNIT = 64
ROPE_THETA = 10000.0
EPS = 1e-6
NEG_INF = -1e30
LANES = 128
LOG2_E = 1.4426950408889634

ADAM_LR = 0.001
ADAM_B1 = 0.9
ADAM_B2 = 0.999
ADAM_EPS = 1e-08
ADAM_WD = 0.01
ADAM_STEP = 10

VMEM_LIMIT_BYTES = 56 * 1024 * 1024
ROW_TILE = 256
HEAD_ROW_TILE = 1024
ATT_TILE = 1024
ATT_SUB = 256
ATT_AHEAD = 3
MM_TILE = 1024

MESH_AXES = ("x", "y", "c")
ANY = pl.BlockSpec(memory_space=pl.ANY)


def _tile(n, pref, align=LANES):
    if n <= pref:
        return n
    t = (pref // align) * align
    while t >= align:
        if n % t == 0:
            return t
        t -= align
    return n


def _sds(shape, dtype):
    return jax.ShapeDtypeStruct(shape, dtype)


def _coords():
    x, y, c = (lax.axis_index(ax) for ax in MESH_AXES)
    return x, y, c


def _chip_rel(x, y, r):
    return (1 - x if r & 2 else x), (1 - y if r & 1 else y)


def _rcopy(src, dst, sems, w, k, dev):
    return pltpu.make_async_remote_copy(src_ref=src, dst_ref=dst, send_sem=sems[0].at[w, k], recv_sem=sems[1].at[w, k],
                                        device_id=dev, device_id_type=pl.DeviceIdType.MESH)


class _GatherRelayPlan:
    def __init__(self, blocks, mid_frac=0.5):
        self.ins = list(blocks)
        self.out_shapes = [_sds((N_DEV,) + b.shape, b.dtype) for b in blocks]
        n = len(blocks)
        self.scratch = [pltpu.SemaphoreType.DMA((n, 7)), pltpu.SemaphoreType.DMA((n, 7)), pltpu.SemaphoreType.DMA((n,))]
        self.mid_frac = mid_frac

    @staticmethod
    def _places():
        x, y, c = _coords()
        xn, yn = 4 * (1 - x) + 2 * y, 4 * x + 2 * (1 - y)
        relay_src = 4 * (x + c * (1 - 2 * x)) + 2 * (y + (1 - c) * (1 - 2 * y)) + c
        relay_to = (x + (1 - c) * (1 - 2 * x), y + c * (1 - 2 * y), c)
        return x, y, c, xn, yn, relay_src, relay_to, 4 * (1 - x) + 2 * (1 - y)

    def first(self, ins, outs, sems):
        x, y, c, _, _, _, _, _ = self._places()
        me = 4 * x + 2 * y + c
        for w in range(len(ins)):
            pltpu.make_async_copy(ins[w], outs[w].at[me], sems[2].at[w]).start()
            _rcopy(ins[w], outs[w].at[me], sems, w, 0, (x, y, 1 - c)).start()
            _rcopy(ins[w], outs[w].at[me], sems, w, 1, (1 - x, y, c)).start()
            _rcopy(ins[w], outs[w].at[me], sems, w, 2, (x, 1 - y, c)).start()

    def mid(self, ins, outs, sems):
        x, y, c, xn, yn, relay_src, relay_to, _ = self._places()
        sib = (x, y, 1 - c)
        for w in range(len(ins)):
            bx, by = outs[w].at[xn + c], outs[w].at[yn + c]
            _rcopy(ins[w], bx, sems, w, 1, (1 - x, y, c)).wait_recv()
            _rcopy(ins[w], by, sems, w, 2, (x, 1 - y, c)).wait_recv()
            _rcopy(outs[w].at[relay_src], outs[w].at[relay_src], sems, w, 3, relay_to).start()
            _rcopy(bx, bx, sems, w, 4, sib).start()
            _rcopy(by, by, sems, w, 5, sib).start()

    def last(self, ins, outs, sems):
        x, y, c, xn, yn, _, relay_to, dg = self._places()
        me = 4 * x + 2 * y + c
        sib = (x, y, 1 - c)
        for w in range(len(ins)):
            bd = outs[w].at[dg + c]
            _rcopy(ins[w], bd, sems, w, 3, relay_to).wait_recv()
            _rcopy(bd, bd, sems, w, 6, sib).start()
            for k, blk in ((0, 4 * x + 2 * y), (4, xn), (5, yn), (6, dg)):
                _rcopy(ins[w], outs[w].at[blk + 1 - c], sems, w, k, sib).wait_recv()
            for k in range(7):
                _rcopy(ins[w], outs[w].at[me], sems, w, k, sib).wait_send()
            pltpu.make_async_copy(ins[w], outs[w].at[me], sems[2].at[w]).wait()


class _GatherOwnPlan:
    mid = None

    def __init__(self, blocks, rows=None, into=None):
        self.n = len(blocks)
        self.rows = rows
        self.ins = list(blocks) + list(into or [])
        self.out_shapes = [_sds((N_DEV,) + b.shape, b.dtype) for b in blocks]
        self.aliases = [(self.n + i, i) for i in range(len(into or []))]
        n = self.n
        self.scratch = [pltpu.SemaphoreType.DMA((n, 4)), pltpu.SemaphoreType.DMA((n, 4)), pltpu.SemaphoreType.DMA((n,))]

    def _cut(self, ref):
        return ref if self.rows is None else ref.at[pl.ds(self.rows[0], self.rows[1] - self.rows[0])]

    def first(self, ins, outs, sems):
        x, y, c = _coords()
        me = 4 * x + 2 * y + c
        for w in range(self.n):
            src, dst = self._cut(ins[w]), self._cut(outs[w].at[me])
            pltpu.make_async_copy(src, dst, sems[2].at[w]).start()
            _rcopy(src, dst, sems, w, 0, (x, y, 1 - c)).start()
            for r in (1, 2, 3):
                px, py = _chip_rel(x, y, r)
                _rcopy(src, dst, sems, w, r, (px, py, c)).start()

    def last(self, ins, outs, sems):
        x, y, c = _coords()
        me = 4 * x + 2 * y + c
        for w in range(self.n):
            src = self._cut(ins[w])
            cp = _rcopy(src, self._cut(outs[w].at[4 * x + 2 * y + 1 - c]), sems, w, 0, (x, y, 1 - c))
            cp.wait_recv()
            cp.wait_send()
            for r in (1, 2, 3):
                px, py = _chip_rel(x, y, r)
                cp = _rcopy(src, self._cut(outs[w].at[4 * px + 2 * py + c]), sems, w, r, (px, py, c))
                cp.wait_recv()
                cp.wait_send()
            pltpu.make_async_copy(src, self._cut(outs[w].at[me]), sems[2].at[w]).wait()


class _GatherPassPlan:
    mid = None

    def __init__(self, gathered):
        self.ins = list(gathered)
        self.out_shapes = [_sds(g.shape, g.dtype) for g in gathered]
        self.aliases = [(i, i) for i in range(len(gathered))]
        n = len(gathered)
        self.scratch = [pltpu.SemaphoreType.DMA((n, 3)), pltpu.SemaphoreType.DMA((n, 3))]

    def first(self, ins, outs, sems):
        x, y, c = _coords()
        for w in range(len(ins)):
            for r in (1, 2, 3):
                px, py = _chip_rel(x, y, r)
                blk = 4 * px + 2 * py + c
                _rcopy(ins[w].at[blk], outs[w].at[blk], sems, w, r - 1, (x, y, 1 - c)).start()

    def last(self, ins, outs, sems):
        x, y, c = _coords()
        for w in range(len(ins)):
            for r in (1, 2, 3):
                px, py = _chip_rel(x, y, r)
                blk = 4 * px + 2 * py + 1 - c
                cp = _rcopy(ins[w].at[blk], outs[w].at[blk], sems, w, r - 1, (x, y, 1 - c))
                cp.wait_recv()
                cp.wait_send()


class _DirectGatherPlan:
    mid = None

    def __init__(self, blocks):
        self.ins = list(blocks)
        self.out_shapes = [_sds((N_DEV,) + b.shape, b.dtype) for b in blocks]
        n = len(blocks)
        self.scratch = [pltpu.SemaphoreType.DMA((n, 7)), pltpu.SemaphoreType.DMA((n, 7)), pltpu.SemaphoreType.DMA((n,))]

    @staticmethod
    def _peer(x, y, c, r):
        return (1 - x if r & 4 else x), (1 - y if r & 2 else y), (1 - c if r & 1 else c)

    def first(self, ins, outs, sems):
        x, y, c = _coords()
        me = 4 * x + 2 * y + c
        for w in range(len(ins)):
            pltpu.make_async_copy(ins[w], outs[w].at[me], sems[2].at[w]).start()
            for r in range(1, N_DEV):
                _rcopy(ins[w], outs[w].at[me], sems, w, r - 1, self._peer(x, y, c, r)).start()

    def last(self, ins, outs, sems):
        x, y, c = _coords()
        me = 4 * x + 2 * y + c
        for w in range(len(ins)):
            for r in range(1, N_DEV):
                px, py, pc = self._peer(x, y, c, r)
                cp = _rcopy(ins[w], outs[w].at[4 * px + 2 * py + pc], sems, w, r - 1, (px, py, pc))
                cp.wait_recv()
                cp.wait_send()
            pltpu.make_async_copy(ins[w], outs[w].at[me], sems[2].at[w]).wait()


class _PairScatterPlan:
    mid = None

    def __init__(self, pieces, rows=None, into=None):
        self.n = len(pieces)
        self.rows = rows
        self.ins = list(pieces) + list(into or [])
        self.out_shapes = [_sds((N_CHIP,) + p.shape[1:], p.dtype) for p in pieces]
        self.aliases = [(self.n + i, i) for i in range(len(into or []))]
        self.scratch = [pltpu.SemaphoreType.DMA((self.n, N_CHIP)), pltpu.SemaphoreType.DMA((self.n, N_CHIP))]

    def _copies(self, ins, outs, sems):
        x, y, c = _coords()
        cps = []
        for w in range(self.n):
            for q in range(N_CHIP):
                src, dst = ins[w].at[2 * q + 1 - c], outs[w].at[q]
                if self.rows is not None:
                    cut = pl.ds(self.rows[0], self.rows[1] - self.rows[0])
                    src, dst = src.at[cut], dst.at[cut]
                cps.append(_rcopy(src, dst, sems, w, q, (x, y, 1 - c)))
        return cps

    def first(self, ins, outs, sems):
        for cp in self._copies(ins, outs, sems):
            cp.start()

    def last(self, ins, outs, sems):
        for cp in self._copies(ins, outs, sems):
            cp.wait_recv()
            cp.wait_send()


class _ChipScatterPlan:
    mid = None

    def __init__(self, sums, rows=None, into=None):
        self.n = len(sums)
        self.rows = rows
        self.ins = list(sums) + list(into or [])
        self.out_shapes = [_sds(s.shape, s.dtype) for s in sums]
        self.aliases = [(self.n + i, i) for i in range(len(into or []))]
        self.scratch = [pltpu.SemaphoreType.DMA((self.n, 3)), pltpu.SemaphoreType.DMA((self.n, 3))]

    def _copies(self, ins, outs, sems):
        x, y, c = _coords()
        cps = []
        for w in range(self.n):
            for r in (1, 2, 3):
                px, py = _chip_rel(x, y, r)
                src, dst = ins[w].at[r - 1], outs[w].at[r - 1]
                if self.rows is not None:
                    cut = pl.ds(self.rows[0], self.rows[1] - self.rows[0])
                    src, dst = src.at[cut], dst.at[cut]
                cps.append(_rcopy(src, dst, sems, w, r - 1, (px, py, c)))
        return cps

    def first(self, ins, outs, sems):
        for cp in self._copies(ins, outs, sems):
            cp.start()

    def last(self, ins, outs, sems):
        for cp in self._copies(ins, outs, sems):
            cp.wait_recv()
            cp.wait_send()


class _Comm:
    def __init__(self, plans):
        self.plans = list(plans)
        self.ins = [a for p in self.plans for a in p.ins]
        self.out_shapes = [s for p in self.plans for s in p.out_shapes]
        self.scratch = [s for p in self.plans for s in p.scratch]
        self.aliases = []
        i = o = 0
        for p in self.plans:
            self.aliases += [(i + a, o + b) for a, b in getattr(p, "aliases", [])]
            i, o = i + len(p.ins), o + len(p.out_shapes)

    def _parts(self, ins, outs, sems):
        i = o = s = 0
        for p in self.plans:
            yield p, ins[i:i + len(p.ins)], outs[o:o + len(p.out_shapes)], sems[s:s + len(p.scratch)]
            i, o, s = i + len(p.ins), o + len(p.out_shapes), s + len(p.scratch)

    def begin(self, step, nsteps, ins, outs, sems):
        @pl.when(step == 0)
        def _():
            for p, pi, po, ps in self._parts(ins, outs, sems):
                p.first(pi, po, ps)

        for p, pi, po, ps in self._parts(ins, outs, sems):
            if p.mid is not None:
                @pl.when(step == min(nsteps - 1, int(p.mid_frac * nsteps)))
                def _(p=p, pi=pi, po=po, ps=ps):
                    p.mid(pi, po, ps)

    def end(self, step, nsteps, ins, outs, sems):
        @pl.when(step == nsteps - 1)
        def _():
            for p, pi, po, ps in self._parts(ins, outs, sems):
                p.last(pi, po, ps)


def _call(body, args, *, name, grid, in_specs, out_specs, out_shape, scratch_shapes=(), sem=None, comm=None):
    in_specs, out_specs, out_shape, scratch_shapes = list(in_specs), list(out_specs), list(out_shape), list(scratch_shapes)
    if comm is None:
        res = pl.pallas_call(
            body, name=name, grid=grid, in_specs=in_specs, out_specs=out_specs, out_shape=out_shape,
            scratch_shapes=scratch_shapes,
            compiler_params=pltpu.CompilerParams(dimension_semantics=sem, vmem_limit_bytes=VMEM_LIMIT_BYTES),
        )(*args)
        return list(res), []
    n_in, n_out, n_sc = len(in_specs), len(out_specs), len(scratch_shapes)
    n_ci, n_co = len(comm.ins), len(comm.out_shapes)
    nsteps = math.prod(grid)

    def hosted(*refs):
        ins, cins = refs[:n_in], refs[n_in:n_in + n_ci]
        o0 = n_in + n_ci
        outs, couts = refs[o0:o0 + n_out], refs[o0 + n_out:o0 + n_out + n_co]
        s0 = o0 + n_out + n_co
        scr, csems = refs[s0:s0 + n_sc], refs[s0 + n_sc:]
        step = jnp.int32(0)
        for d in range(len(grid)):
            step = step * grid[d] + pl.program_id(d)
        comm.begin(step, nsteps, cins, couts, csems)
        body(*ins, *outs, *scr)
        comm.end(step, nsteps, cins, couts, csems)

    res = pl.pallas_call(
        hosted, name=name, grid=grid, in_specs=in_specs + [ANY] * n_ci, out_specs=out_specs + [ANY] * n_co,
        out_shape=out_shape + comm.out_shapes, scratch_shapes=scratch_shapes + comm.scratch,
        input_output_aliases={n_in + a: n_out + b for a, b in comm.aliases},
        compiler_params=pltpu.CompilerParams(dimension_semantics=("arbitrary",) * len(grid),
                                             vmem_limit_bytes=VMEM_LIMIT_BYTES, has_side_effects=True),
    )(*args, *comm.ins)
    return list(res[:n_out]), list(res[n_out:])


def _exchange_alone(comm, name):
    def body():
        pass

    return _call(body, [], name=name, grid=(), in_specs=[], out_specs=[], out_shape=[], comm=comm)[1]


def _matmul(a, b, *, mode, name, out_dtype=F32, out_blocks=None, tm=None, tn=None, tk=None, comm=None):
    tm = MM_TILE if tm is None else tm
    tn = MM_TILE if tn is None else tn
    a_blk = a.ndim == 3
    b_blk = b.ndim == 3
    if mode == "nn":
        M, K = a.shape
        N = b.shape[0] * b.shape[2] if b_blk else b.shape[1]
        dims = (((1,), (0,)), ((), ()))
    elif mode == "nt":
        M = a.shape[1] if a_blk else a.shape[0]
        K = a.shape[0] * a.shape[2] if a_blk else a.shape[1]
        N = b.shape[1] if b_blk else b.shape[0]
        dims = (((1,), (1,)), ((), ()))
    else:
        K, M = a.shape
        N = b.shape[0] * b.shape[2] if b_blk else b.shape[1]
        dims = (((0,), (0,)), ((), ()))

    tm = _tile(M, tm)
    tn = _tile(N, tn)
    if mode == "nt" and (a_blk or b_blk):
        tk = a.shape[2] if a_blk else b.shape[2]
    else:
        tk = _tile(K, K if tk is None else tk)
    if mode != "nt" and b_blk:
        tn = _tile(b.shape[2], tn)
    if out_blocks is not None:
        tn = _tile(out_blocks, tn)
    nk = K // tk
    grid = (M // tm, N // tn, nk)

    if mode == "nn":
        a_spec = pl.BlockSpec((tm, tk), lambda i, j, k: (i, k))
        if b_blk:
            rb = b.shape[2] // tn
            b_spec = pl.BlockSpec((None, tk, tn), lambda i, j, k: (j // rb, k, j % rb))
        else:
            b_spec = pl.BlockSpec((tk, tn), lambda i, j, k: (k, j))
    elif mode == "nt":
        if a_blk:
            a_spec = pl.BlockSpec((None, tm, tk), lambda i, j, k: (k, i, 0))
        else:
            a_spec = pl.BlockSpec((tm, tk), lambda i, j, k: (i, k))
        if b_blk:
            b_spec = pl.BlockSpec((None, tn, tk), lambda i, j, k: (k, j, 0))
        else:
            b_spec = pl.BlockSpec((tn, tk), lambda i, j, k: (j, k))
    else:
        a_spec = pl.BlockSpec((tk, tm), lambda i, j, k: (k, i))
        if b_blk:
            rb = b.shape[2] // tn
            b_spec = pl.BlockSpec((None, tk, tn), lambda i, j, k: (j // rb, k, j % rb))
        else:
            b_spec = pl.BlockSpec((tk, tn), lambda i, j, k: (k, j))

    if out_blocks is None:
        o_spec = pl.BlockSpec((tm, tn), lambda i, j, k: (i, j))
        o_shape = _sds((M, N), out_dtype)
    else:
        ro = out_blocks // tn
        o_spec = pl.BlockSpec((None, tm, tn), lambda i, j, k: (j // ro, i, j % ro))
        o_shape = _sds((N // out_blocks, M, out_blocks), out_dtype)

    direct = nk == 1 or out_dtype == F32

    def body(a_ref, b_ref, o_ref, *scratch):
        if nk == 1:
            o_ref[...] = lax.dot_general(a_ref[...], b_ref[...], dims, preferred_element_type=F32).astype(o_ref.dtype)
            return
        acc_ref = o_ref if direct else scratch[0]
        k = pl.program_id(2)

        @pl.when(k == 0)
        def _():
            acc_ref[...] = jnp.zeros(acc_ref.shape, F32)

        acc_ref[...] += lax.dot_general(a_ref[...], b_ref[...], dims, preferred_element_type=F32)
        if not direct:
            @pl.when(k == nk - 1)
            def _():
                o_ref[...] = acc_ref[...].astype(o_ref.dtype)

    scratch = [] if direct else [pltpu.VMEM((tm, tn), F32)]
    outs, landed = _call(body, [a, b], name=name, grid=grid, in_specs=[a_spec, b_spec], out_specs=[o_spec],
                         out_shape=[o_shape], scratch_shapes=scratch, sem=("parallel", "parallel", "arbitrary"), comm=comm)
    return outs[0] if comm is None else (outs[0], landed)


def _matmul_halves(a, b, *, mode, name, tm, tn=None, tk=None, comm=None):
    if mode == "nt":
        lo, hi = a
        M, kh = lo.shape
        G, N, kb = b.shape
        half = kh // kb
        tm, tn = _tile(M, tm), _tile(N, N if tn is None else tn)
        dims = (((1,), (1,)), ((), ()))

        def body(lo_ref, hi_ref, b_ref, o_ref):
            k = pl.program_id(2)

            @pl.when(k == 0)
            def _():
                o_ref[...] = jnp.zeros(o_ref.shape, F32)

            @pl.when(k < half)
            def _():
                o_ref[...] += lax.dot_general(lo_ref[...], b_ref[...], dims, preferred_element_type=F32)

            @pl.when(k >= half)
            def _():
                o_ref[...] += lax.dot_general(hi_ref[...], b_ref[...], dims, preferred_element_type=F32)

        outs, landed = _call(
            body, [lo, hi, b], name=name, grid=(M // tm, N // tn, G),
            in_specs=[pl.BlockSpec((tm, kb), lambda i, j, k: (i, jnp.minimum(k, half - 1))),
                      pl.BlockSpec((tm, kb), lambda i, j, k: (i, jnp.maximum(k - half, 0))),
                      pl.BlockSpec((None, tn, kb), lambda i, j, k: (k, j, 0))],
            out_specs=[pl.BlockSpec((tm, tn), lambda i, j, k: (i, j))], out_shape=[_sds((M, N), F32)],
            sem=("parallel", "parallel", "arbitrary"), comm=comm)
    else:
        lo, hi = b
        K, nh = lo.shape
        M = a.shape[1]
        n = tn
        half = nh // n
        tm, tk = _tile(M, tm), _tile(K, K if tk is None else tk)
        nk = K // tk
        dims = (((0,), (0,)), ((), ()))

        def body(a_ref, lo_ref, hi_ref, o_ref):
            j, k = pl.program_id(1), pl.program_id(2)

            @pl.when(k == 0)
            def _():
                o_ref[...] = jnp.zeros(o_ref.shape, F32)

            @pl.when(j < half)
            def _():
                o_ref[...] += lax.dot_general(a_ref[...], lo_ref[...], dims, preferred_element_type=F32)

            @pl.when(j >= half)
            def _():
                o_ref[...] += lax.dot_general(a_ref[...], hi_ref[...], dims, preferred_element_type=F32)

        outs, landed = _call(
            body, [a, lo, hi], name=name, grid=(M // tm, 2 * half, nk),
            in_specs=[pl.BlockSpec((tk, tm), lambda i, j, k: (k, i)),
                      pl.BlockSpec((tk, n), lambda i, j, k: (jnp.where(j < half, k, nk - 1), jnp.minimum(j, half - 1))),
                      pl.BlockSpec((tk, n), lambda i, j, k: (jnp.where(j >= half, k, 0), jnp.maximum(j - half, 0)))],
            out_specs=[pl.BlockSpec((None, tm, n), lambda i, j, k: (j, i, 0))],
            out_shape=[_sds((2 * half, M, n), F32)], sem=("parallel", "parallel", "arbitrary"), comm=comm)
    return outs[0] if comm is None else (outs[0], landed)


def _rms(x):
    return lax.rsqrt(jnp.mean(x * x, axis=-1, keepdims=True) + EPS)


def _rms_bwd(dy, x, g):
    r = _rms(x)
    n = x * r
    dn = dy * g
    dx = r * (dn - n * jnp.mean(dn * n, axis=-1, keepdims=True))
    return dx, dy * n


def _sigmoid(x):
    return 1.0 / (1.0 + jnp.exp(-x))


def _rope_rot(t):
    return pltpu.roll(t, HALF_ROPE, 1) - pltpu.roll(t, LANES - HALF_ROPE, 1)


def _lane(shape):
    return lax.broadcasted_iota(jnp.int32, shape, 1)


def _split3(x):
    hi = x.astype(BF16).astype(F32)
    r1 = x - hi
    mid = r1.astype(BF16).astype(F32)
    lo = (r1 - mid).astype(BF16).astype(F32)
    return hi, mid, lo


def _cumsum_rows(x, reverse):
    S = x.shape[0]
    bs = min(256, S)
    nb = S // bs
    r = lax.broadcasted_iota(jnp.int32, (bs, bs), 0)
    c = lax.broadcasted_iota(jnp.int32, (bs, bs), 1)
    tri = jnp.where((c >= r) if reverse else (c <= r), 1.0, 0.0).astype(BF16)
    edge = lax.broadcasted_iota(jnp.int32, (bs, x.shape[1]), 0) == (0 if reverse else bs - 1)
    carry = jnp.zeros((1, x.shape[1]), F32)
    outs = [None] * nb
    for bi in (range(nb - 1, -1, -1) if reverse else range(nb)):
        xb = x[bi * bs:(bi + 1) * bs, :]
        acc = carry
        for term in _split3(xb):
            acc = acc + jnp.dot(tri, term.astype(BF16), preferred_element_type=F32)
        outs[bi] = acc
        carry = jnp.sum(jnp.where(edge, acc, 0.0), axis=0, keepdims=True)
    return jnp.concatenate(outs, axis=0) if nb > 1 else outs[0]


def _gelu_parts(x):
    c0 = math.sqrt(2.0 / math.pi)
    inner = c0 * (x + 0.044715 * (x * x * x))
    t = jnp.tanh(inner)
    g = 0.5 * x * (1.0 + t)
    dg = 0.5 * (1.0 + t) + 0.5 * x * (1.0 - t * t) * (c0 * (1.0 + 3.0 * 0.044715 * (x * x)))
    return g, dg


def _accumulate(ref, value, first):
    @pl.when(first)
    def _():
        ref[...] = value

    @pl.when(jnp.logical_not(first))
    def _():
        ref[...] += value


def _cast_bf16(w, name):
    R, C = w.shape
    tr = _tile(R, 512, 16)

    def body(w_ref, o_ref):
        o_ref[...] = w_ref[...].astype(BF16)

    blk = pl.BlockSpec((tr, C), lambda i: (i, 0))
    return _call(body, [w], name=name, grid=(R // tr,), in_specs=[blk], out_specs=[blk],
                 out_shape=[_sds((R, C), BF16)], sem=("parallel",))[0][0]


def _concat_cols(parts, name):
    T = parts[0].shape[-2] if parts[0].ndim == 3 else parts[0].shape[0]
    widths = [p.shape[0] * LANES if p.ndim == 3 else p.shape[1] for p in parts]
    tm = _tile(T, ROW_TILE, 16)

    def body(*refs):
        o_ref = refs[-1]
        off = 0
        for p_ref, p, w in zip(refs[:-1], parts, widths):
            if p.ndim == 3:
                for hd in range(p.shape[0]):
                    o_ref[:, off + hd * LANES:off + (hd + 1) * LANES] = p_ref[hd].astype(BF16)
            else:
                o_ref[:, off:off + w] = p_ref[...].astype(BF16)
            off += w

    def spec(p, w):
        if p.ndim == 3:
            return pl.BlockSpec((p.shape[0], tm, LANES), lambda i: (0, i, 0))
        return pl.BlockSpec((tm, w), lambda i: (i, 0))

    return _call(body, parts, name=name, grid=(T // tm,),
                 in_specs=[spec(p, w) for p, w in zip(parts, widths)],
                 out_specs=[pl.BlockSpec((tm, sum(widths)), lambda i: (i, 0))],
                 out_shape=[_sds((T, sum(widths)), BF16)], sem=("parallel",))[0][0]


def _prenorm(x, g, comm=None):
    T, D = x.shape
    tm = _tile(T, ROW_TILE, 16)

    def body(x_ref, g_ref, h_ref):
        xv = x_ref[...]
        h_ref[...] = (xv * _rms(xv) * g_ref[...]).astype(BF16)

    row = pl.BlockSpec((tm, D), lambda i: (i, 0))
    (h,), landed = _call(body, [x, g], name="prenorm", grid=(T // tm,),
                         in_specs=[row, pl.BlockSpec((1, D), lambda i: (0, 0))], out_specs=[row],
                         out_shape=[_sds((T, D), BF16)], sem=("parallel",), comm=comm)
    return h, landed


def _split_prep(proj, pos, invf, gq, gkv, bfor, lay, comm=None):
    T = proj.shape[0]
    tm = _tile(T, ROW_TILE, 16)

    def body(q_ref, kv_ref, kpe_ref, fl_ref, pos_ref, invf_ref, gq_ref, gkv_ref, bf_ref,
             qn_ref, kvn_ref, kper_ref, logf_ref, cos_ref, sin_ref):
        ql = q_ref[...]
        qn_ref[...] = (ql * _rms(ql) * gq_ref[...]).astype(BF16)
        kl = kv_ref[...]
        kvn_ref[...] = (kl * _rms(kl) * gkv_ref[...]).astype(BF16)
        ang = pos_ref[...].astype(F32) * invf_ref[...]
        valid = _lane(ang.shape) < ROPE
        cs = jnp.where(valid, jnp.cos(ang), 0.0)
        sn = jnp.where(valid, jnp.sin(ang), 0.0)
        cos_ref[...] = cs
        sin_ref[...] = sn
        kp = jnp.where(valid, kpe_ref[...], 0.0)
        kper_ref[...] = (kp * cs + _rope_rot(kp) * sn).astype(BF16)
        z = fl_ref[...] + bf_ref[...]
        logf_ref[...] = jnp.minimum(z, 0.0) - jnp.log(1.0 + jnp.exp(-jnp.abs(z)))

    def col(width, off):
        return pl.BlockSpec((tm, width), lambda i: (i, off // width))

    def vec(width):
        return pl.BlockSpec((1, width), lambda i: (0, 0))

    def out(width):
        return pl.BlockSpec((tm, width), lambda i: (i, 0))

    return _call(
        body, [proj, proj, proj, proj, pos, invf, gq, gkv, bfor], name="split_prep", grid=(T // tm,),
        in_specs=[col(Q_LORA, lay["q"]), col(KV_LORA, lay["kv"]), col(LANES, lay["kpe"]), col(LANES, lay["fl"]),
                  pl.BlockSpec((tm, 1), lambda i: (i, 0)), vec(LANES), vec(Q_LORA), vec(KV_LORA), vec(LANES)],
        out_specs=[out(Q_LORA), out(KV_LORA), out(LANES), out(LANES), out(LANES), out(LANES)],
        out_shape=[_sds((T, Q_LORA), BF16), _sds((T, KV_LORA), BF16), _sds((T, LANES), BF16),
                   _sds((T, LANES), F32), _sds((T, LANES), F32), _sds((T, LANES), F32)],
        sem=("parallel",), comm=comm)


def _mla_prep(qraw, kvraw, kper, cosT, sinT, comm=None):
    H, T, _ = qraw.shape
    tm = _tile(T, HEAD_ROW_TILE, 16)

    def body(q_ref, kv_ref, kpe_ref, cos_ref, sin_ref, qo_ref, ko_ref, vo_ref):
        q = q_ref[...]
        pe = q[:, NOPE:]
        pe = jnp.where(_lane(pe.shape) < ROPE, pe, 0.0)
        qo_ref[:, :NOPE] = q[:, :NOPE].astype(BF16)
        qo_ref[:, NOPE:] = (pe * cos_ref[...] + _rope_rot(pe) * sin_ref[...]).astype(BF16)
        kv = kv_ref[...]
        ko_ref[:, :NOPE] = kv[:, :NOPE].astype(BF16)
        ko_ref[:, NOPE:] = kpe_ref[...]
        vo_ref[...] = kv[:, NOPE:].astype(BF16)

    head = pl.BlockSpec((None, tm, ATT_DK), lambda h, i: (h, i, 0))
    tok = pl.BlockSpec((tm, LANES), lambda h, i: (i, 0))
    return _call(
        body, [qraw, kvraw, kper, cosT, sinT], name="mla_prep", grid=(H, T // tm),
        in_specs=[head, head, tok, tok, tok],
        out_specs=[head, head, pl.BlockSpec((None, tm, VDIM), lambda h, i: (h, i, 0))],
        out_shape=[_sds((H, T, ATT_DK), BF16), _sds((H, T, ATT_DK), BF16), _sds((H, T, VDIM), BF16)],
        sem=("parallel", "parallel"), comm=comm)


def _fox_cumsum(logf, B, S, inv_scale):
    T = logf.shape[0]

    def body(l_ref, c_ref):
        c_ref[...] = _cumsum_rows(l_ref[...], reverse=False) * inv_scale

    seq = pl.BlockSpec((S, LANES), lambda b: (b, 0))
    return _call(body, [logf], name="fox_cumsum", grid=(B,), in_specs=[seq], out_specs=[seq],
                 out_shape=[_sds((T, LANES), F32)], sem=("parallel",))[0][0]


def _fox_prep(proj, cs, lay, comm=None):
    T = proj.shape[0]
    tm = _tile(T, HEAD_ROW_TILE, 16)

    def body(q_ref, k_ref, v_ref, cs_ref, qo_ref, ko_ref, vo_ref):
        h = pl.program_id(0)
        cv = cs_ref[...]
        lane = _lane(cv.shape)
        ccol = jnp.sum(jnp.where(lane == h, cv, 0.0), axis=1, keepdims=True)
        hi, mid, lo = _split3(ccol)
        one = jnp.where(lane < 6, 1.0, 0.0)
        augq = jnp.where(lane == 0, hi, jnp.where(lane == 1, mid, jnp.where(lane == 2, lo, one)))
        augk = jnp.where(lane < 3, 1.0, jnp.where(lane == 3, -hi, jnp.where(lane == 4, -mid, jnp.where(lane == 5, -lo, 0.0))))
        qo_ref[:, :FOX_DIM] = q_ref[...].astype(BF16)
        qo_ref[:, FOX_DIM:] = augq.astype(BF16)
        ko_ref[:, :FOX_DIM] = k_ref[...].astype(BF16)
        ko_ref[:, FOX_DIM:] = augk.astype(BF16)
        vo_ref[...] = v_ref[...].astype(BF16)

    def col(off):
        return pl.BlockSpec((tm, FOX_DIM), lambda h, i: (i, off // FOX_DIM + h))

    head = pl.BlockSpec((None, tm, ATT_DK), lambda h, i: (h, i, 0))
    return _call(
        body, [proj, proj, proj, cs], name="fox_prep", grid=(HEADS, T // tm),
        in_specs=[col(lay["fq"]), col(lay["fk"]), col(lay["fv"]), pl.BlockSpec((tm, LANES), lambda h, i: (i, 0))],
        out_specs=[head, head, pl.BlockSpec((None, tm, VDIM), lambda h, i: (h, i, 0))],
        out_shape=[_sds((HEADS, T, ATT_DK), BF16), _sds((HEADS, T, ATT_DK), BF16), _sds((HEADS, T, VDIM), BF16)],
        sem=("parallel", "parallel"), comm=comm)


def _visible(tq, tk, unit):
    r = lax.broadcasted_iota(jnp.int32, (tq, tk), 0)
    c = lax.broadcasted_iota(jnp.int32, (tq, tk), 1)
    sh = int(math.log2(unit))
    return lax.shift_right_logical(c, sh) <= lax.shift_right_logical(r, sh)


def _attn_fwd(streams, *, B, S, name, comm=None):
    n = len(streams)
    H, T, DK = streams[0][0].shape
    DV = streams[0][2].shape[2]
    tq = _tile(S, ATT_TILE)
    nq = S // tq
    sub = min(ATT_SUB, tq)
    NT = (((1,), (1,)), ((), ()))

    def body(*refs):
        ins, outs, (m_sc, acc_sc) = refs[:3 * n], refs[3 * n:5 * n], refs[5 * n:]
        i, j = pl.program_id(1), pl.program_id(2)

        @pl.when(j == 0)
        def _():
            m_sc[...] = jnp.full(m_sc.shape, NEG_INF, F32)
            acc_sc[...] = jnp.zeros(acc_sc.shape, F32)

        def step(diagonal):
            work = [(t, r) for r in range(tq // sub) for t in range(n)]

            def scores(t, r):
                q_ref, k_ref, _ = ins[3 * t:3 * t + 3]
                kc = (r + 1) * sub if diagonal else tq
                s = lax.dot_general(q_ref[r * sub:(r + 1) * sub, :], k_ref[0:kc, :], NT, preferred_element_type=F32)
                return s * (streams[t][4] * LOG2_E)

            ahead = [scores(*work[w]) for w in range(min(ATT_AHEAD, len(work)))]
            for w, (t, r) in enumerate(work):
                s = ahead.pop(0)
                if w + ATT_AHEAD < len(work):
                    ahead.append(scores(*work[w + ATT_AHEAD]))
                v_ref = ins[3 * t + 2]
                kc = s.shape[1]
                rows = slice(r * sub, (r + 1) * sub)
                if diagonal:
                    own = jnp.where(_visible(sub, sub, streams[t][3]), s[:, kc - sub:], NEG_INF)
                    s = own if kc == sub else jnp.concatenate([s[:, :kc - sub], own], axis=1)
                m_prev = m_sc[t, rows, :]
                mx = s[:, 0:LANES]
                for g in range(1, kc // LANES):
                    mx = jnp.maximum(mx, s[:, g * LANES:(g + 1) * LANES])
                m_new = jnp.maximum(m_prev, jnp.max(mx, axis=1, keepdims=True))
                alpha = jnp.exp2(m_prev - m_new)
                p = jnp.exp2(s - jnp.tile(m_new, (1, kc // LANES))).astype(BF16)
                v_aug = jnp.concatenate([v_ref[0:kc, :], jnp.ones((kc, LANES), BF16)], axis=1)
                acc_sc[t, rows, :] = jnp.tile(alpha, (1, 2)) * acc_sc[t, rows, :] + jnp.dot(
                    p, v_aug, preferred_element_type=F32)
                m_sc[t, rows, :] = m_new

        @pl.when(j < i)
        def _():
            step(False)

        @pl.when(j == i)
        def _():
            step(True)
            for t in range(n):
                o_ref, lse_ref = outs[2 * t:2 * t + 2]
                l = acc_sc[t, :, DV:]
                o_ref[...] = (acc_sc[t, :, :DV] / l).astype(BF16)
                lse_ref[...] = m_sc[t] + jnp.log2(l)

    def qmap(g, i, j):
        return (g % H, (g // H) * nq + i, 0)

    def kmap(g, i, j):
        return (g % H, (g // H) * nq + jnp.minimum(j, i), 0)

    args = [a for st in streams for a in st[:3]]
    outs, landed = _call(
        body, args, name=name, grid=(B * H, nq, nq),
        in_specs=[pl.BlockSpec((None, tq, DK), qmap), pl.BlockSpec((None, tq, DK), kmap),
                  pl.BlockSpec((None, tq, DV), kmap)] * n,
        out_specs=[pl.BlockSpec((tq, DV), lambda g, i, j: ((g // H) * nq + i, g % H)),
                   pl.BlockSpec((None, tq, LANES), qmap)] * n,
        out_shape=[_sds((T, H * DV), BF16), _sds((H, T, LANES), F32)] * n,
        scratch_shapes=[pltpu.VMEM((n, tq, LANES), F32), pltpu.VMEM((n, tq, DV + LANES), F32)],
        sem=("parallel", "parallel", "arbitrary"), comm=comm)
    return [(outs[2 * t], outs[2 * t + 1]) for t in range(n)], landed


def _attn_bwd(streams, *, B, S, name, comm=None):
    n = len(streams)
    H, T, DK = streams[0][0].shape
    DV = streams[0][2].shape[2]
    tq = _tile(S, ATT_TILE)
    nq = S // tq
    sub = min(ATT_SUB, tq)
    NT = (((1,), (1,)), ((), ()))
    TN = (((0,), (0,)), ((), ()))

    def body(*refs):
        ins, outs = refs[:6 * n], refs[6 * n:]
        j, i = pl.program_id(1), pl.program_id(2)

        @pl.when(jnp.logical_and(j == 0, i == 0))
        def _():
            for t in range(n):
                outs[3 * t][...] = jnp.zeros(outs[3 * t].shape, F32)

        @pl.when(i == 0)
        def _():
            for t in range(n):
                outs[3 * t + 1][...] = jnp.zeros(outs[3 * t + 1].shape, F32)
                outs[3 * t + 2][...] = jnp.zeros(outs[3 * t + 2].shape, F32)

        def step(diagonal):
            work = [(t, r) for r in range(tq // sub) for t in range(n)]

            def kcols(r):
                return (r + 1) * sub if diagonal else tq

            def scores(t, r):
                q_ref, k_ref, v_ref, _, do_ref, _ = ins[6 * t:6 * t + 6]
                rows, kc = slice(r * sub, (r + 1) * sub), kcols(r)
                s = lax.dot_general(q_ref[rows, :], k_ref[0:kc, :], NT, preferred_element_type=F32)
                dp = lax.dot_general(do_ref[rows, :], v_ref[0:kc, :], NT, preferred_element_type=F32)
                return s * (streams[t][7] * LOG2_E), dp

            def probs(t, r, s, dp):
                _, _, _, o_ref, do_ref, lse_ref = ins[6 * t:6 * t + 6]
                rows, kc = slice(r * sub, (r + 1) * sub), kcols(r)
                if diagonal:
                    own = jnp.where(_visible(sub, sub, streams[t][6]), s[:, kc - sub:], NEG_INF)
                    s = own if kc == sub else jnp.concatenate([s[:, :kc - sub], own], axis=1)
                p = jnp.exp2(s - jnp.tile(lse_ref[rows, :], (1, kc // LANES)))
                delta = jnp.sum(do_ref[rows, :].astype(F32) * o_ref[rows, :].astype(F32), axis=1, keepdims=True)
                return p.astype(BF16), (p * (dp - delta) * streams[t][7]).astype(BF16)

            def grads(t, r, p, ds):
                q_ref, k_ref, _, _, do_ref, _ = ins[6 * t:6 * t + 6]
                dq_ref, dk_ref, dv_ref = outs[3 * t:3 * t + 3]
                rows, kc = slice(r * sub, (r + 1) * sub), kcols(r)
                dv_ref[0:kc, :] += lax.dot_general(p, do_ref[rows, :], TN, preferred_element_type=F32)
                dk_ref[0:kc, :] += lax.dot_general(ds, q_ref[rows, :], TN, preferred_element_type=F32)
                qrows = pl.ds(pl.multiple_of(i * tq + r * sub, sub), sub)
                dq_ref[qrows, :] += jnp.dot(ds, k_ref[0:kc, :], preferred_element_type=F32)

            nw = len(work)
            sc = {w: scores(*work[w]) for w in range(min(2, nw))}
            pr = {0: probs(*work[0], *sc.pop(0))}
            for w in range(nw):
                if w + 2 < nw:
                    sc[w + 2] = scores(*work[w + 2])
                if w + 1 < nw:
                    pr[w + 1] = probs(*work[w + 1], *sc.pop(w + 1))
                grads(*work[w], *pr.pop(w))

        @pl.when(i > j)
        def _():
            step(False)

        @pl.when(i == j)
        def _():
            step(True)

    def qmap(g, j, i):
        return (g % H, (g // H) * nq + jnp.maximum(i, j), 0)

    def kmap(g, j, i):
        return (g % H, (g // H) * nq + j, 0)

    def omap(g, j, i):
        return ((g // H) * nq + jnp.maximum(i, j), g % H)

    args = [a for st in streams for a in st[:6]]
    outs, landed = _call(
        body, args, name=name, grid=(B * H, nq, nq),
        in_specs=[pl.BlockSpec((None, tq, DK), qmap), pl.BlockSpec((None, tq, DK), kmap),
                  pl.BlockSpec((None, tq, DV), kmap), pl.BlockSpec((tq, DV), omap), pl.BlockSpec((tq, DV), omap),
                  pl.BlockSpec((None, tq, LANES), qmap)] * n,
        out_specs=[pl.BlockSpec((None, S, DK), lambda g, j, i: (g % H, g // H, 0)),
                   pl.BlockSpec((None, tq, DK), kmap), pl.BlockSpec((None, tq, DV), kmap)] * n,
        out_shape=[_sds((H, T, DK), F32), _sds((H, T, DK), F32), _sds((H, T, DV), F32)] * n,
        sem=("parallel", "arbitrary", "arbitrary"), comm=comm)
    return [tuple(outs[3 * t:3 * t + 3]) for t in range(n)], landed


def _gate_merge(am, af, proj, bgate, lay, D, comm=None):
    T = am.shape[0]
    tm = _tile(T, ROW_TILE, 16)
    tn = _tile(D, 1024)

    def body(am_ref, af_ref, gm_ref, gf_ref, bm_ref, bf_ref, o_ref):
        sm = _sigmoid(gm_ref[...] + bm_ref[...])
        sf = _sigmoid(gf_ref[...] + bf_ref[...])
        o_ref[...] = (sm * am_ref[...] + sf * af_ref[...]).astype(BF16)

    og = lay["g"] // tn
    blk = pl.BlockSpec((tm, tn), lambda i, j: (i, j))
    return _call(
        body, [am, af, proj, proj, bgate, bgate], name="gate_merge", grid=(T // tm, D // tn),
        in_specs=[blk, blk, pl.BlockSpec((tm, tn), lambda i, j: (i, og + j)),
                  pl.BlockSpec((tm, tn), lambda i, j: (i, og + D // tn + j)),
                  pl.BlockSpec((1, tn), lambda i, j: (0, j)), pl.BlockSpec((1, tn), lambda i, j: (0, D // tn + j))],
        out_specs=[blk], out_shape=[_sds((T, D), BF16)], sem=("parallel", "parallel"), comm=comm)


def _out_mid(merged, w_out, x, g_pm, g_ffn):
    T, D = x.shape
    tm = _tile(T, 2 * ROW_TILE, 16)

    def body(a_ref, w_ref, x_ref, gp_ref, gf_ref, y_ref, x1_ref, h2_ref):
        y = jnp.dot(a_ref[...], w_ref[...], preferred_element_type=F32)
        y_ref[...] = y
        x1 = x_ref[...] + y * _rms(y) * gp_ref[...]
        x1_ref[...] = x1
        h2_ref[...] = (x1 * _rms(x1) * gf_ref[...]).astype(BF16)

    row = pl.BlockSpec((tm, D), lambda i: (i, 0))
    vec = pl.BlockSpec((1, D), lambda i: (0, 0))
    return _call(body, [merged, w_out, x, g_pm, g_ffn], name="mm_out_mid", grid=(T // tm,),
                 in_specs=[row, pl.BlockSpec((D, D), lambda i: (0, 0)), row, vec, vec], out_specs=[row, row, row],
                 out_shape=[_sds((T, D), F32), _sds((T, D), F32), _sds((T, D), BF16)], sem=("parallel",))[0]


def _conv3(u, w_ref, bias):
    row = lax.broadcasted_iota(jnp.int32, u.shape, 0)
    u1 = jnp.where(row >= 1, pltpu.roll(u, 1, 0), 0.0)
    u2 = jnp.where(row >= 2, pltpu.roll(u, 2, 0), 0.0)
    return w_ref[0:1, :] * u2 + w_ref[1:2, :] * u1 + w_ref[2:3, :] * u + bias, u1, u2


def _convffn_fwd(u, cw, cb, B, S, F, comm=None):
    T = u.shape[0]
    tn = _tile(F, 256)
    nf = F // tn

    def body(ug_ref, uv_ref, wg_ref, wv_ref, bg_ref, bv_ref, a_ref):
        g, _, _ = _conv3(ug_ref[...], wg_ref, bg_ref[...])
        val, _, _ = _conv3(uv_ref[...], wv_ref, bv_ref[...])
        a_ref[...] = (_gelu_parts(g)[0] * val).astype(BF16)

    def seq(off):
        return pl.BlockSpec((S, tn), lambda b, j: (b, off + j))

    def par(rows, off):
        return pl.BlockSpec((rows, tn), lambda b, j: (0, off + j))

    return _call(body, [u, u, cw, cw, cb, cb], name="convffn_fwd", grid=(B, nf),
                 in_specs=[seq(0), seq(nf), par(3, 0), par(3, nf), par(1, 0), par(1, nf)],
                 out_specs=[seq(0)], out_shape=[_sds((T, F), BF16)], sem=("parallel", "parallel"), comm=comm)


def _convffn_bwd(u, dact, cw, cb, B, S, F, comm=None):
    T = u.shape[0]
    tn = _tile(F, 256)
    nf = F // tn

    def body(ug_ref, uv_ref, da_ref, wg_ref, wv_ref, bg_ref, bv_ref, dug_ref, duv_ref, dpg_ref, dpv_ref):
        b = pl.program_id(1)
        ug, uv, da = ug_ref[...], uv_ref[...], da_ref[...]
        g, ug1, ug2 = _conv3(ug, wg_ref, bg_ref[...])
        val, uv1, uv2 = _conv3(uv, wv_ref, bv_ref[...])
        gel, dgel = _gelu_parts(g)
        dg = da * val * dgel
        dval = da * gel
        row = lax.broadcasted_iota(jnp.int32, ug.shape, 0)

        def back(d, w_ref):
            d1 = jnp.where(row < S - 1, pltpu.roll(d, S - 1, 0), 0.0)
            d2 = jnp.where(row < S - 2, pltpu.roll(d, S - 2, 0), 0.0)
            return w_ref[2:3, :] * d + w_ref[1:2, :] * d1 + w_ref[0:1, :] * d2

        dug_ref[...] = back(dg, wg_ref).astype(BF16)
        duv_ref[...] = back(dval, wv_ref).astype(BF16)

        def sums(d, u0, u1, u2):
            r8 = lax.broadcasted_iota(jnp.int32, (8, d.shape[1]), 0)
            out = jnp.zeros((8, d.shape[1]), F32)
            for k, t in enumerate((d * u2, d * u1, d * u0, d)):
                out = jnp.where(r8 == k, jnp.sum(t, axis=0, keepdims=True), out)
            return out

        _accumulate(dpg_ref, sums(dg, ug, ug1, ug2), b == 0)
        _accumulate(dpv_ref, sums(dval, uv, uv1, uv2), b == 0)

    def seq(off):
        return pl.BlockSpec((S, tn), lambda j, b: (b, off + j))

    def par(rows, off):
        return pl.BlockSpec((rows, tn), lambda j, b: (0, off + j))

    outs, landed = _call(
        body, [u, u, dact, cw, cw, cb, cb], name="convffn_bwd", grid=(nf, B),
        in_specs=[seq(0), seq(nf), seq(0), par(3, 0), par(3, nf), par(1, 0), par(1, nf)],
        out_specs=[seq(0), seq(0), par(8, 0), par(8, 0)],
        out_shape=[_sds((T, F), BF16), _sds((T, F), BF16), _sds((8, F), F32), _sds((8, F), F32)],
        sem=("parallel", "arbitrary"), comm=comm)
    return outs, landed


def _tail(ff, x1, tgt, g):
    T, D = ff.shape
    tm = _tile(T, ROW_TILE, 16)

    def body(ff_ref, x1_ref, t_ref, g_ref, dy_ref, dff_ref, loss_ref, dg_ref):
        i = pl.program_id(0)
        f = ff_ref[...]
        gv = g_ref[...]
        r = _rms(f)
        n = f * r
        e = (x1_ref[...] + n * gv) - t_ref[...]
        dy = e * (1.0 / D)
        dy_ref[...] = dy
        dn = dy * gv
        dff_ref[...] = (r * (dn - n * jnp.mean(dn * n, axis=-1, keepdims=True))).astype(BF16)
        part = 0.5 * jnp.sum(jnp.mean(e * e, axis=-1, keepdims=True), axis=0, keepdims=True)
        _accumulate(loss_ref, jnp.broadcast_to(part, loss_ref.shape), i == 0)
        _accumulate(dg_ref, jnp.sum(dy * n, axis=0, keepdims=True), i == 0)

    row = pl.BlockSpec((tm, D), lambda i: (i, 0))
    vec = pl.BlockSpec((1, D), lambda i: (0, 0))
    return _call(body, [ff, x1, tgt, g], name="tail", grid=(T // tm,), in_specs=[row, row, row, vec],
                 out_specs=[row, row, pl.BlockSpec((8, LANES), lambda i: (0, 0)), vec],
                 out_shape=[_sds((T, D), F32), _sds((T, D), BF16), _sds((8, LANES), F32), _sds((1, D), F32)],
                 sem=("arbitrary",))[0]


def _mid_bwd(dy, dh2, x1, y1, g_ffn, g_pm, comm=None):
    T, D = dy.shape
    tm = _tile(T, ROW_TILE, 16)

    def body(dy_ref, dh_ref, x1_ref, y1_ref, gf_ref, gp_ref, dx1_ref, dy1_ref, dgf_ref, dgp_ref):
        i = pl.program_id(0)
        dh = dh_ref[...]
        d2, dgf = _rms_bwd(dh, x1_ref[...], gf_ref[...])
        dx1 = dy_ref[...] + d2
        dx1_ref[...] = dx1
        d1, dgp = _rms_bwd(dx1, y1_ref[...], gp_ref[...])
        dy1_ref[...] = d1.astype(BF16)
        _accumulate(dgf_ref, jnp.sum(dgf, axis=0, keepdims=True), i == 0)
        _accumulate(dgp_ref, jnp.sum(dgp, axis=0, keepdims=True), i == 0)

    row = pl.BlockSpec((tm, D), lambda i: (i, 0))
    vec = pl.BlockSpec((1, D), lambda i: (0, 0))
    return _call(body, [dy, dh2, x1, y1, g_ffn, g_pm], name="mid_bwd", grid=(T // tm,),
                 in_specs=[row, row, row, row, vec, vec], out_specs=[row, row, vec, vec],
                 out_shape=[_sds((T, D), F32), _sds((T, D), BF16), _sds((1, D), F32), _sds((1, D), F32)],
                 sem=("arbitrary",), comm=comm)


def _gate_bwd(dy1, w_out, am, af, proj, bgate, lay, D, comm=None):
    T = dy1.shape[0]
    tm = _tile(T, 2 * ROW_TILE, 16)
    tn = _tile(D, 512)
    NT = (((1,), (1,)), ((), ()))

    def body(dy_ref, w_ref, am_ref, af_ref, gm_ref, gf_ref, bm_ref, bf_ref,
             dam_ref, daf_ref, dgm_ref, dgf_ref, dbm_ref, dbf_ref):
        i = pl.program_id(1)
        d = lax.dot_general(dy_ref[...], w_ref[...], NT, preferred_element_type=F32)
        sm = _sigmoid(gm_ref[...] + bm_ref[...])
        sf = _sigmoid(gf_ref[...] + bf_ref[...])
        dam_ref[...] = (d * sm).astype(BF16)
        daf_ref[...] = (d * sf).astype(BF16)
        dgm = d * am_ref[...] * (sm * (1.0 - sm))
        dgf = d * af_ref[...] * (sf * (1.0 - sf))
        dgm_ref[...] = dgm.astype(BF16)
        dgf_ref[...] = dgf.astype(BF16)
        _accumulate(dbm_ref, jnp.sum(dgm, axis=0, keepdims=True), i == 0)
        _accumulate(dbf_ref, jnp.sum(dgf, axis=0, keepdims=True), i == 0)

    og = lay["g"] // tn
    blk = pl.BlockSpec((tm, tn), lambda j, i: (i, j))
    vec = pl.BlockSpec((1, tn), lambda j, i: (0, j))
    return _call(
        body, [dy1, w_out, am, af, proj, proj, bgate, bgate], name="mm_dmerged_gate_bwd", grid=(D // tn, T // tm),
        in_specs=[pl.BlockSpec((tm, D), lambda j, i: (i, 0)), pl.BlockSpec((tn, D), lambda j, i: (j, 0)),
                  blk, blk, pl.BlockSpec((tm, tn), lambda j, i: (i, og + j)),
                  pl.BlockSpec((tm, tn), lambda j, i: (i, og + D // tn + j)),
                  vec, pl.BlockSpec((1, tn), lambda j, i: (0, D // tn + j))],
        out_specs=[blk, blk, blk, blk, vec, vec],
        out_shape=[_sds((T, D), BF16)] * 4 + [_sds((1, D), F32)] * 2, sem=("parallel", "arbitrary"), comm=comm)


def _mla_bwd_prep(dq, dk, dv, cosT, sinT, comm=None):
    H, T, _ = dq.shape
    tm = _tile(T, HEAD_ROW_TILE, 16)

    def body(dq_ref, dk_ref, dv_ref, cos_ref, sin_ref, dqr_ref, dkv_ref, dkpe_ref):
        h = pl.program_id(1)
        cs, sn = cos_ref[...], sin_ref[...]
        valid = _lane(cs.shape) < ROPE

        def unrope(d):
            d = jnp.where(valid, d, 0.0)
            return d * cs - _rope_rot(d) * sn

        dqv = dq_ref[...]
        dqr_ref[:, :NOPE] = dqv[:, :NOPE].astype(BF16)
        dqr_ref[:, NOPE:] = unrope(dqv[:, NOPE:]).astype(BF16)
        dkv_ = dk_ref[...]
        dkv_ref[:, :NOPE] = dkv_[:, :NOPE].astype(BF16)
        dkv_ref[:, NOPE:] = dv_ref[...].astype(BF16)
        _accumulate(dkpe_ref, unrope(dkv_[:, NOPE:]), h == 0)

    head = pl.BlockSpec((None, tm, ATT_DK), lambda i, h: (h, i, 0))
    tok = pl.BlockSpec((tm, LANES), lambda i, h: (i, 0))
    return _call(
        body, [dq, dk, dv, cosT, sinT], name="mla_bwd_prep", grid=(T // tm, H),
        in_specs=[head, head, pl.BlockSpec((None, tm, VDIM), lambda i, h: (h, i, 0)), tok, tok],
        out_specs=[head, head, tok],
        out_shape=[_sds((H, T, ATT_DK), BF16), _sds((H, T, ATT_DK), BF16), _sds((T, LANES), F32)],
        sem=("parallel", "arbitrary"), comm=comm)


def _fox_bwd_prep(dq, dk, proj, bfor, lay, B, S, inv_scale):
    H, T, _ = dq.shape

    def body(dq_ref, dk_ref, fl_ref, bf_ref, dfl_ref, dbf_ref, dc_sc):
        b, h = pl.program_id(0), pl.program_id(1)
        lane = _lane(dc_sc.shape)
        col = jnp.sum(jnp.where(lane == 0, dq_ref[...], 0.0) - jnp.where(lane == 3, dk_ref[...], 0.0),
                      axis=1, keepdims=True)

        @pl.when(h == 0)
        def _():
            dc_sc[...] = jnp.zeros(dc_sc.shape, F32)

        dc_sc[...] = jnp.where(lane == h, col, dc_sc[...])

        @pl.when(h == H - 1)
        def _():
            dlogf = _cumsum_rows(dc_sc[...] * inv_scale, reverse=True)
            z = fl_ref[...] + bf_ref[...]
            dz = jnp.where(lane < H, dlogf * (1.0 / (1.0 + jnp.exp(z))), 0.0)
            dfl_ref[...] = dz
            _accumulate(dbf_ref, jnp.sum(dz, axis=0, keepdims=True), b == 0)

    aug = pl.BlockSpec((None, S, LANES), lambda b, h: (h, b, 1))
    seq = pl.BlockSpec((S, LANES), lambda b, h: (b, 0))
    vec = pl.BlockSpec((1, LANES), lambda b, h: (0, 0))
    return _call(
        body, [dq, dk, proj, bfor], name="fox_bwd_prep", grid=(B, H),
        in_specs=[aug, aug, pl.BlockSpec((S, LANES), lambda b, h: (b, lay["fl"] // LANES)), vec],
        out_specs=[seq, vec], out_shape=[_sds((T, LANES), F32), _sds((1, LANES), F32)],
        scratch_shapes=[pltpu.VMEM((S, LANES), F32)], sem=("arbitrary", "arbitrary"))[0]


def _lat_bwd(dqn, dkvn, proj, gq, gkv, lay):
    T = dqn.shape[0]
    tm = _tile(T, ROW_TILE, 16)

    def body(dq_ref, dkv_ref, q_ref, kv_ref, gq_ref, gkv_ref, dql_ref, dkl_ref, dgq_ref, dgkv_ref):
        i = pl.program_id(0)
        dql, dgq = _rms_bwd(dq_ref[...], q_ref[...], gq_ref[...])
        dkl, dgkv = _rms_bwd(dkv_ref[...], kv_ref[...], gkv_ref[...])
        dql_ref[...] = dql.astype(BF16)
        dkl_ref[...] = dkl.astype(BF16)
        _accumulate(dgq_ref, jnp.sum(dgq, axis=0, keepdims=True), i == 0)
        _accumulate(dgkv_ref, jnp.sum(dgkv, axis=0, keepdims=True), i == 0)

    def blk(width, off=0):
        return pl.BlockSpec((tm, width), lambda i: (i, off // width))

    def vec(width):
        return pl.BlockSpec((1, width), lambda i: (0, 0))

    return _call(
        body, [dqn, dkvn, proj, proj, gq, gkv], name="lat_bwd", grid=(T // tm,),
        in_specs=[blk(Q_LORA), blk(KV_LORA), blk(Q_LORA, lay["q"]), blk(KV_LORA, lay["kv"]), vec(Q_LORA), vec(KV_LORA)],
        out_specs=[blk(Q_LORA), blk(KV_LORA), vec(Q_LORA), vec(KV_LORA)],
        out_shape=[_sds((T, Q_LORA), BF16), _sds((T, KV_LORA), BF16), _sds((1, Q_LORA), F32), _sds((1, KV_LORA), F32)],
        sem=("arbitrary",))[0]


def _dh_final(dproj, w_perm, dx1, x, g, comm=None):
    T, D = x.shape
    K = dproj.shape[1]
    tm = _tile(T, 2 * ROW_TILE, 16)
    tk = _tile(K, 1024)
    nk = K // tk
    NT = (((1,), (1,)), ((), ()))

    def body(a_ref, b_ref, dx1_ref, x_ref, g_ref, dx_ref, dg_ref):
        i, k = pl.program_id(0), pl.program_id(1)

        @pl.when(k == 0)
        def _():
            dx_ref[...] = jnp.zeros(dx_ref.shape, F32)

        dx_ref[...] += lax.dot_general(a_ref[...], b_ref[...], NT, preferred_element_type=F32)

        @pl.when(k == nk - 1)
        def _():
            d, dg = _rms_bwd(dx_ref[...], x_ref[...], g_ref[...])
            dx_ref[...] = dx1_ref[...] + d
            _accumulate(dg_ref, jnp.sum(dg, axis=0, keepdims=True), i == 0)

    row = pl.BlockSpec((tm, D), lambda i, k: (i, 0))
    vec = pl.BlockSpec((1, D), lambda i, k: (0, 0))
    return _call(body, [dproj, w_perm, dx1, x, g], name="mm_dh_final", grid=(T // tm, nk),
                 in_specs=[pl.BlockSpec((tm, tk), lambda i, k: (i, k)), pl.BlockSpec((D, tk), lambda i, k: (0, k)),
                           row, row, vec],
                 out_specs=[row, vec], out_shape=[_sds((T, D), F32), _sds((1, D), F32)],
                 sem=("arbitrary", "arbitrary"), comm=comm)


def _chip_sum(pieces, paired, qc, name):
    G, R, C = pieces.shape
    tr = _tile(R, 256, 16)

    def body(qc_ref, g_ref, p_ref, keep_ref, send_ref):
        s = pl.program_id(1)
        tot = g_ref[...] + p_ref[...]

        @pl.when(s == 0)
        def _():
            keep_ref[...] = tot

        @pl.when(s > 0)
        def _():
            send_ref[...] = tot.astype(send_ref.dtype)

    grid_spec = pltpu.PrefetchScalarGridSpec(
        num_scalar_prefetch=1, grid=(R // tr, N_CHIP),
        in_specs=[pl.BlockSpec((None, tr, C), lambda i, s, qc: (2 * (qc[0] ^ s) + qc[1], i, 0)),
                  pl.BlockSpec((None, tr, C), lambda i, s, qc: (qc[0] ^ s, i, 0))],
        out_specs=[pl.BlockSpec((tr, C), lambda i, s, qc: (i, 0)),
                   pl.BlockSpec((None, tr, C), lambda i, s, qc: (jnp.maximum(s - 1, 0), i, 0))])
    send_dtype = BF16 if R >= 16 else pieces.dtype
    return pl.pallas_call(
        body, name=name, grid_spec=grid_spec,
        out_shape=[_sds((R, C), F32), _sds((3, R, C), send_dtype)],
        compiler_params=pltpu.CompilerParams(dimension_semantics=("arbitrary", "arbitrary"),
                                             vmem_limit_bytes=VMEM_LIMIT_BYTES),
    )(qc, pieces, paired)


def _adamw_math(w, g, m, v):
    m = ADAM_B1 * m + (1.0 - ADAM_B1) * g
    v = ADAM_B2 * v + (1.0 - ADAM_B2) * (g * g)
    m_hat = m / (1.0 - ADAM_B1 ** ADAM_STEP)
    v_hat = v / (1.0 - ADAM_B2 ** ADAM_STEP)
    delta = -ADAM_LR * (m_hat / (jnp.sqrt(v_hat) + ADAM_EPS) + ADAM_WD * w)
    return delta, m, v


def _sum_adamw(keep, pieces, w, m, v, name):
    R, C = w.shape
    P = pieces.shape[0]
    tr = _tile(R, 256, 16)

    def body(k_ref, p_ref, w_ref, m_ref, v_ref, g_ref, d_ref, mo_ref, vo_ref):
        g = k_ref[...]
        for q in range(P):
            g = g + p_ref[q].astype(F32)
        g_ref[...] = g
        d_ref[...], mo_ref[...], vo_ref[...] = _adamw_math(w_ref[...], g, m_ref[...], v_ref[...])

    blk = pl.BlockSpec((tr, C), lambda i: (i, 0))
    pblk = pl.BlockSpec((P, tr, C), lambda i: (0, i, 0))
    return _call(body, [keep, pieces, w, m, v], name=name, grid=(R // tr,), in_specs=[blk, pblk, blk, blk, blk],
                 out_specs=[blk] * 4, out_shape=[_sds((R, C), F32)] * 4, sem=("parallel",))[0]


def _adamw_small(parts, w, m, v, widths):
    n = len(widths)

    def body(p_ref, w_ref, m_ref, v_ref, *o_refs):
        g = p_ref[0]
        for q in range(1, N_DEV):
            g = g + p_ref[q]
        vals = (g,) + _adamw_math(w_ref[...], g, m_ref[...], v_ref[...])
        off = 0
        for i, wd in enumerate(widths):
            for kind in range(4):
                o_refs[4 * i + kind][...] = vals[kind][:, off:off + wd]
            off += wd

    whole = pl.BlockSpec(memory_space=pltpu.VMEM)
    outs = _call(body, [parts, w, m, v], name="adamw_small", grid=(), in_specs=[whole] * 4,
                 out_specs=[whole] * (4 * n), out_shape=[_sds((1, wd), F32) for wd in widths for _ in range(4)])[0]
    return [tuple(outs[4 * i:4 * i + 4]) for i in range(n)]


def _layout(D):
    lay = {"q": 0, "kv": Q_LORA, "kpe": Q_LORA + KV_LORA}
    lay["fq"] = lay["kpe"] + LANES
    lay["fk"] = lay["fq"] + HEADS * FOX_DIM
    lay["fv"] = lay["fk"] + HEADS * FOX_DIM
    lay["fl"] = lay["fv"] + HEADS * FOX_DIM
    lay["g"] = lay["fl"] + LANES
    lay["end"] = lay["g"] + 2 * D
    return lay


def kernel(x, positions, pre_mix_norm, w_in, q_a_norm, w_uq, kv_a_norm, w_ukv, b_forget, b_gate, w_branch_mla, w_branch_fox, w_out, post_mix_norm, pre_ffn_norm, w_up, conv_w, conv_b, w_down, post_ffn_norm, loss_target, m_pre_mix_norm, m_w_in, m_q_a_norm, m_w_uq, m_kv_a_norm, m_w_ukv, m_b_forget, m_b_gate, m_w_branch_mla, m_w_branch_fox, m_w_out, m_post_mix_norm, m_pre_ffn_norm, m_w_up, m_conv_w, m_conv_b, m_w_down, m_post_ffn_norm, v_pre_mix_norm, v_w_in, v_q_a_norm, v_w_uq, v_kv_a_norm, v_w_ukv, v_b_forget, v_b_gate, v_w_branch_mla, v_w_branch_fox, v_w_out, v_post_mix_norm, v_pre_ffn_norm, v_w_up, v_conv_w, v_conv_b, v_w_down, v_post_ffn_norm):
    B, S, D = x.shape
    T = B * S
    F = conv_b.shape[0] // 2
    lay = _layout(D)
    n_in = w_in.shape[1]
    d_in = N_DEV * n_in
    seg_a = Q_LORA + KV_LORA + ROPE
    seg_b = 3 * HEADS * FOX_DIM + HEADS
    mla_scale = (NOPE + ROPE) ** -0.5
    fox_scale = FOX_DIM ** -0.5
    ax, ay, ac = (lax.axis_index(a) for a in MESH_AXES)
    qc = jnp.stack([2 * ax + ay, ac]).astype(jnp.int32)

    def row(vec, width=None):
        vec = vec.reshape(1, -1)
        if width is not None and vec.shape[1] < width:
            vec = jnp.pad(vec, ((0, 0), (0, width - vec.shape[1])))
        return vec

    x2 = x.reshape(T, D)
    win_s = _cast_bf16(w_in, "cast_w_in")
    h, (win_g,) = _prenorm(x2, row(pre_mix_norm), comm=_Comm([_GatherRelayPlan([win_s], mid_frac=0.3)]))
    small_s = [_cast_bf16(w, "cast_" + n) for w, n in
               [(w_uq, "w_uq"), (w_ukv, "w_ukv"), (w_branch_mla, "w_branch_mla"), (w_branch_fox, "w_branch_fox"), (w_out, "w_out")]]
    wup_s = _cast_bf16(w_up, "cast_w_up")
    wdown_s = _cast_bf16(w_down, "cast_w_down")

    def shard_cols(lo, hi):
        out = []
        for g in range(lo // n_in, (hi - 1) // n_in + 1):
            out.append(win_g[g][:, max(lo, g * n_in) - g * n_in:min(hi, (g + 1) * n_in) - g * n_in])
        return out

    w_perm = jnp.concatenate(
        shard_cols(0, seg_a) + [jnp.zeros((D, LANES - ROPE), BF16)] + shard_cols(seg_a, seg_a + seg_b)
        + [jnp.zeros((D, LANES - HEADS), BF16)] + shard_cols(seg_a + seg_b, d_in), axis=1)

    tgt = loss_target.reshape(T, D)
    pos = positions.reshape(T, 1)
    inv_freq = 1.0 / (ROPE_THETA ** (jnp.arange(0, ROPE, 2, dtype=F32) / ROPE))
    invf = row(jnp.concatenate([inv_freq, inv_freq]), LANES)
    g_pre, g_q, g_kv = row(pre_mix_norm), row(q_a_norm), row(kv_a_norm)
    g_pm, g_ffn, g_pf = row(post_mix_norm), row(pre_ffn_norm), row(post_ffn_norm)
    bfor = row(b_forget, LANES)
    bgate = row(b_gate)
    cb_full = row(conv_b)

    def own_plan(blocks):
        return _Comm([_GatherOwnPlan(blocks)])

    def pass_plan(gathered):
        return _Comm([_GatherPassPlan(gathered)])

    def pair_plan(gs):
        return _Comm([_PairScatterPlan(gs)])

    def chip_plan(gs):
        return _Comm([_ChipScatterPlan(gs)])

    half_d = D // 2
    proj, landed = _matmul(h, w_perm, mode="nn", name="mm_proj", comm=_Comm(
        [_GatherOwnPlan(small_s[:2] + [conv_w]), _GatherOwnPlan([wup_s], rows=(0, half_d))]))
    early_g, wup_part = landed[:-1], landed[-1:]
    (qn, kvn, kper, logf, cosT, sinT), (wuq_g, wukv_g, cw_g) = _split_prep(
        proj, pos, invf, g_q, g_kv, bfor, lay, comm=pass_plan(early_g))
    wuq_pad = jnp.pad(wuq_g, ((0, 0), (0, 0), (0, ATT_DK - NOPE - ROPE)))
    cw_full = jnp.transpose(cw_g, (1, 0, 2)).reshape(3, 2 * F)

    qraw = _matmul(qn, wuq_pad, mode="nn", name="mm_q", out_blocks=ATT_DK, tm=T)
    kvraw = _matmul(kvn, wukv_g, mode="nn", name="mm_kv", out_blocks=NOPE + VDIM, tm=T)
    (q_mla, k_mla, v_mla), branch_half = _mla_prep(qraw, kvraw, kper, cosT, sinT, comm=own_plan(small_s[2:4]))
    cs = _fox_cumsum(logf, B, S, 1.0 / fox_scale)
    (q_fox, k_fox, v_fox), wout_half = _fox_prep(proj, cs, lay, comm=own_plan(small_s[4:5]))
    ((o_mla, lse_mla), (o_fox, lse_fox)), landed = _attn_fwd(
        [(q_mla, k_mla, v_mla, MLA_UNIT, mla_scale), (q_fox, k_fox, v_fox, 1, fox_scale)], B=B, S=S,
        name="attn_fwd", comm=_Comm([_GatherOwnPlan([wup_s], rows=(half_d, D), into=wup_part),
                                     _GatherPassPlan(branch_half + wout_half)]))
    wup_half, (wbm_g, wbf_g, wout_g) = landed[:1], landed[1:]
    wbm = jnp.transpose(wbm_g, (1, 0, 2)).reshape(HEADS * VDIM, D)
    wbf = jnp.transpose(wbf_g, (1, 0, 2)).reshape(HEADS * FOX_DIM, D)
    wout = wout_g.reshape(D, D)
    a_m = _matmul(o_mla, wbm, mode="nn", name="mm_branch_mla", tm=2 * MM_TILE)
    a_f = _matmul(o_fox, wbf, mode="nn", name="mm_branch_fox", tm=2 * MM_TILE)
    (merged,), (wup_g,) = _gate_merge(a_m, a_f, proj, bgate, lay, D, comm=pass_plan(wup_half))
    n_up = wup_g.shape[2]
    y1, x1, h2 = _out_mid(merged, wout, x2, g_pm, g_ffn)
    u, wdown_half = _matmul(h2, wup_g, mode="nn", name="mm_up", tn=n_up, comm=own_plan([wdown_s]))
    (act,), (wdown_g,) = _convffn_fwd(u, cw_full, cb_full, B, S, F, comm=pass_plan(wdown_half))
    wdown = wdown_g.reshape(F, D)
    ff = _matmul(act, wdown, mode="nn", name="mm_down", tk=F // 2)
    dy, dff, loss_part, dg_pf = _tail(ff, x1, tgt, g_pf)

    dact = _matmul(dff, wdown, mode="nt", name="mm_dact", tn=F // 4)
    dw_down = _matmul(act, dff, mode="tn", name="mm_dw_down", tm=F // 4, tn=512).reshape(N_DEV, F // N_DEV, D)
    (du_g, du_v, dcp_g, dcp_v), (pa_down,) = _convffn_bwd(u, dact, cw_full, cb_full, B, S, F, comm=pair_plan([dw_down]))
    keep_down, sb_down = _chip_sum(dw_down, pa_down, qc, "chipsum_w_down")
    dh2, (rb_down,) = _matmul_halves((du_g, du_v), wup_g, mode="nt", name="mm_dh2", tm=MM_TILE, comm=chip_plan([sb_down]))
    dw_up = _matmul_halves(h2, (du_g, du_v), mode="tn", name="mm_dw_up", tm=MM_TILE, tn=n_up, tk=T // 2)
    (dx1, dy1, dg_ffn, dg_pm), _ = _mid_bwd(dy, dh2, x1, y1, g_ffn, g_pm)
    dw_out = _matmul(merged, dy1, mode="tn", name="mm_dw_out").reshape(N_DEV, D // N_DEV, D)
    (da_m, da_f, dgl_m, dgl_f, dbg_m, dbg_f), (pa_up,) = _gate_bwd(
        dy1, wout, a_m, a_f, proj, bgate, lay, D, comm=pair_plan([dw_up]))
    keep_up, sb_up = _chip_sum(dw_up, pa_up, qc, "chipsum_w_up")
    dw_bm = _matmul(o_mla, da_m, mode="tn", name="mm_dw_branch_mla", out_blocks=D // N_DEV)
    dw_bf = _matmul(o_fox, da_f, mode="tn", name="mm_dw_branch_fox", out_blocks=D // N_DEV)
    mix = [dw_out, dw_bm, dw_bf]
    do_mla, pa_mix = _matmul(da_m, wbm, mode="nt", name="mm_do_mla", out_dtype=BF16, comm=pair_plan(mix))
    do_fox = _matmul(da_f, wbf, mode="nt", name="mm_do_fox", out_dtype=BF16)
    mix_sums = [_chip_sum(g, p, qc, "chipsum_" + n) for g, p, n in zip(mix, pa_mix, ["w_out", "w_branch_mla", "w_branch_fox"])]
    ((dq_m, dk_m, dv_m), (dq_f, dk_f, dv_f)), (rb_up,) = _attn_bwd(
        [(q_mla, k_mla, v_mla, o_mla, do_mla, lse_mla, MLA_UNIT, mla_scale),
         (q_fox, k_fox, v_fox, o_fox, do_fox, lse_fox, 1, fox_scale)], B=B, S=S, name="attn_bwd",
        comm=chip_plan([sb_up]))
    (dqraw, dkvraw, dkpe), _ = _mla_bwd_prep(dq_m, dk_m, dv_m, cosT, sinT)
    dqn = _matmul(dqraw, wuq_pad, mode="nt", name="mm_dqn", tm=T)
    dw_uq = _matmul(qn, dqraw, mode="tn", name="mm_dw_uq", out_blocks=ATT_DK)[:, :, :NOPE + ROPE]
    dkvn = _matmul(dkvraw, wukv_g, mode="nt", name="mm_dkvn", tm=T)
    dw_ukv = _matmul(kvn, dkvraw, mode="tn", name="mm_dw_ukv", out_blocks=NOPE + VDIM)
    dqlat, dkvlat, dg_q, dg_kv = _lat_bwd(dqn, dkvn, proj, g_q, g_kv, lay)
    dfl, dbfor = _fox_bwd_prep(dq_f, dk_f, proj, bfor, lay, B, S, 1.0 / fox_scale)
    dproj = _concat_cols([dqlat, dkvlat, dkpe, dq_f, dk_f, dv_f, dfl, dgl_m, dgl_f], "concat_dproj")
    dw_perm, rb_mix = _matmul(h, dproj, mode="tn", name="mm_dw_in", comm=chip_plan([s[1] for s in mix_sums]))
    segs = [(0, seg_a, 0), (seg_a, seg_a + seg_b, lay["fq"] - seg_a), (seg_a + seg_b, d_in, lay["g"] - seg_a - seg_b)]

    def piece(g):
        lo, hi = g * n_in, (g + 1) * n_in
        parts = [dw_perm[:, max(lo, s0) + sh:min(hi, s1) + sh] for s0, s1, sh in segs if max(lo, s0) < min(hi, s1)]
        return parts[0] if len(parts) == 1 else jnp.concatenate(parts, axis=1)

    dw_in = jnp.stack([piece(g) for g in range(N_DEV)])
    dcw = jnp.transpose(jnp.concatenate([dcp_g[0:3], dcp_v[0:3]], axis=1).reshape(3, N_DEV, (2 * F) // N_DEV), (1, 0, 2))
    late = [dw_in, dw_uq, dw_ukv, dcw]
    pa_late = _exchange_alone(pair_plan(late), "pair_late")
    late_sums = [_chip_sum(g, p, qc, "chipsum_" + n) for g, p, n in zip(late, pa_late, ["w_in", "w_uq", "w_ukv", "conv_w"])]
    (grad_x, dg_pre), rb_late = _dh_final(dproj, w_perm, dx1, x2, g_pre, comm=chip_plan([s[1] for s in late_sums]))

    big_out = {}

    def finish(n, keep, pieces, w, m, v):
        big_out[n] = _sum_adamw(keep, pieces, w, m, v, "adamw_" + n)

    finish("w_down", keep_down, rb_down, w_down, m_w_down, v_w_down)
    finish("w_up", keep_up, rb_up, w_up, m_w_up, v_w_up)
    finish("w_out", mix_sums[0][0], rb_mix[0], w_out, m_w_out, v_w_out)
    finish("w_branch_mla", mix_sums[1][0], rb_mix[1], w_branch_mla, m_w_branch_mla, v_w_branch_mla)
    finish("w_branch_fox", mix_sums[2][0], rb_mix[2], w_branch_fox, m_w_branch_fox, v_w_branch_fox)
    finish("w_in", late_sums[0][0], rb_late[0], w_in, m_w_in, v_w_in)
    finish("w_uq", late_sums[1][0], rb_late[1], w_uq, m_w_uq, v_w_uq)
    finish("w_ukv", late_sums[2][0], rb_late[2], w_ukv, m_w_ukv, v_w_ukv)
    finish("conv_w", late_sums[3][0], rb_late[3], conv_w, m_conv_w, v_conv_w)

    widths = [D, Q_LORA, KV_LORA, LANES, 2 * D, D, D, 2 * F, D]
    small_names = ["pre_mix_norm", "q_a_norm", "kv_a_norm", "b_forget", "b_gate", "post_mix_norm", "pre_ffn_norm",
                   "conv_b", "post_ffn_norm"]
    true_w = [D, Q_LORA, KV_LORA, HEADS, 2 * D, D, D, 2 * F, D]
    dcb = jnp.concatenate([dcp_g[3:4], dcp_v[3:4]], axis=1)
    part = jnp.concatenate([dg_pre, dg_q, dg_kv, dbfor, dbg_m, dbg_f, dg_pm, dg_ffn, dcb, dg_pf], axis=1)

    def pack(vals):
        return jnp.concatenate([row(a, wd) for a, wd in zip(vals, widths)], axis=1)

    sw = pack([pre_mix_norm, q_a_norm, kv_a_norm, b_forget, b_gate, post_mix_norm, pre_ffn_norm, conv_b, post_ffn_norm])
    sm = pack([m_pre_mix_norm, m_q_a_norm, m_kv_a_norm, m_b_forget, m_b_gate, m_post_mix_norm, m_pre_ffn_norm,
               m_conv_b, m_post_ffn_norm])
    sv = pack([v_pre_mix_norm, v_q_a_norm, v_kv_a_norm, v_b_forget, v_b_gate, v_post_mix_norm, v_pre_ffn_norm,
               v_conv_b, v_post_ffn_norm])
    (parts_all,) = _exchange_alone(_Comm([_DirectGatherPlan([part])]), "gather_small")
    small = _adamw_small(parts_all, sw, sm, sv, widths)
    small_out = {n: tuple(a.reshape(-1)[:tw] for a in vals) for n, vals, tw in zip(small_names, small, true_w)}

    loss = lax.psum(loss_part[0, 0], MESH_AXES)
    order = ["pre_mix_norm", "w_in", "q_a_norm", "w_uq", "kv_a_norm", "w_ukv", "b_forget", "b_gate", "w_branch_mla",
             "w_branch_fox", "w_out", "post_mix_norm", "pre_ffn_norm", "w_up", "conv_w", "conv_b", "w_down",
             "post_ffn_norm"]
    res = {**big_out, **small_out}
    outs = [loss, grad_x.reshape(B, S, D)]
    for kind in range(4):
        outs += [res[n][kind] for n in order]
    return tuple(outs)
```

```python
import math

import jax
import jax.numpy as jnp
from jax import lax
from jax.experimental import pallas as pl
from jax.experimental.pallas import tpu as pltpu

F32 = jnp.float32
BF16 = jnp.bfloat16

N_DEV = 8
N_CHIP = 4
HEADS = 8
NOPE = 128
ROPE = 64
HALF_ROPE = ROPE // 2
VDIM = 128
Q_LORA = 512
KV_LORA = 256
FOX_DIM = 128
ATT_DK = 256
MLA_UNIT = 64
ROPE_THETA = 10000.0
EPS = 1e-6
NEG_INF = -1e30
LANES = 128
LOG2_E = 1.4426950408889634

ADAM_LR = 0.001
ADAM_B1 = 0.9
ADAM_B2 = 0.999
ADAM_EPS = 1e-08
ADAM_WD = 0.01
ADAM_STEP = 10

VMEM_LIMIT_BYTES = 56 * 1024 * 1024
ROW_TILE = 256
HEAD_ROW_TILE = 1024
ATT_TILE = 1024
ATT_SUB = 256
ATT_AHEAD = 3
MM_TILE = 1024

MESH_AXES = ("x", "y", "c")
ANY = pl.BlockSpec(memory_space=pl.ANY)


def _tile(n, pref, align=LANES):
    if n <= pref:
        return n
    t = (pref // align) * align
    while t >= align:
        if n % t == 0:
            return t
        t -= align
    return n


def _sds(shape, dtype):
    return jax.ShapeDtypeStruct(shape, dtype)


def _coords():
    x, y, c = (lax.axis_index(ax) for ax in MESH_AXES)
    return x, y, c


def _chip_rel(x, y, r):
    return (1 - x if r & 2 else x), (1 - y if r & 1 else y)


def _rcopy(src, dst, sems, w, k, dev):
    return pltpu.make_async_remote_copy(src_ref=src, dst_ref=dst, send_sem=sems[0].at[w, k], recv_sem=sems[1].at[w, k],
                                        device_id=dev, device_id_type=pl.DeviceIdType.MESH)


class _GatherRelayPlan:
    def __init__(self, blocks, mid_frac=0.5):
        self.ins = list(blocks)
        self.out_shapes = [_sds((N_DEV,) + b.shape, b.dtype) for b in blocks]
        n = len(blocks)
        self.scratch = [pltpu.SemaphoreType.DMA((n, 7)), pltpu.SemaphoreType.DMA((n, 7)), pltpu.SemaphoreType.DMA((n,))]
        self.mid_frac = mid_frac

    @staticmethod
    def _places():
        x, y, c = _coords()
        xn, yn = 4 * (1 - x) + 2 * y, 4 * x + 2 * (1 - y)
        relay_src = 4 * (x + c * (1 - 2 * x)) + 2 * (y + (1 - c) * (1 - 2 * y)) + c
        relay_to = (x + (1 - c) * (1 - 2 * x), y + c * (1 - 2 * y), c)
        return x, y, c, xn, yn, relay_src, relay_to, 4 * (1 - x) + 2 * (1 - y)

    def first(self, ins, outs, sems):
        x, y, c, _, _, _, _, _ = self._places()
        me = 4 * x + 2 * y + c
        for w in range(len(ins)):
            pltpu.make_async_copy(ins[w], outs[w].at[me], sems[2].at[w]).start()
            _rcopy(ins[w], outs[w].at[me], sems, w, 0, (x, y, 1 - c)).start()
            _rcopy(ins[w], outs[w].at[me], sems, w, 1, (1 - x, y, c)).start()
            _rcopy(ins[w], outs[w].at[me], sems, w, 2, (x, 1 - y, c)).start()

    def mid(self, ins, outs, sems):
        x, y, c, xn, yn, relay_src, relay_to, _ = self._places()
        sib = (x, y, 1 - c)
        for w in range(len(ins)):
            bx, by = outs[w].at[xn + c], outs[w].at[yn + c]
            _rcopy(ins[w], bx, sems, w, 1, (1 - x, y, c)).wait_recv()
            _rcopy(ins[w], by, sems, w, 2, (x, 1 - y, c)).wait_recv()
            _rcopy(outs[w].at[relay_src], outs[w].at[relay_src], sems, w, 3, relay_to).start()
            _rcopy(bx, bx, sems, w, 4, sib).start()
            _rcopy(by, by, sems, w, 5, sib).start()

    def last(self, ins, outs, sems):
        x, y, c, xn, yn, _, relay_to, dg = self._places()
        me = 4 * x + 2 * y + c
        sib = (x, y, 1 - c)
        for w in range(len(ins)):
            bd = outs[w].at[dg + c]
            _rcopy(ins[w], bd, sems, w, 3, relay_to).wait_recv()
            _rcopy(bd, bd, sems, w, 6, sib).start()
            for k, blk in ((0, 4 * x + 2 * y), (4, xn), (5, yn), (6, dg)):
                _rcopy(ins[w], outs[w].at[blk + 1 - c], sems, w, k, sib).wait_recv()
            for k in range(7):
                _rcopy(ins[w], outs[w].at[me], sems, w, k, sib).wait_send()
            pltpu.make_async_copy(ins[w], outs[w].at[me], sems[2].at[w]).wait()


class _GatherOwnPlan:
    mid = None

    def __init__(self, blocks, rows=None, into=None):
        self.n = len(blocks)
        self.rows = rows
        self.ins = list(blocks) + list(into or [])
        self.out_shapes = [_sds((N_DEV,) + b.shape, b.dtype) for b in blocks]
        self.aliases = [(self.n + i, i) for i in range(len(into or []))]
        n = self.n
        self.scratch = [pltpu.SemaphoreType.DMA((n, 4)), pltpu.SemaphoreType.DMA((n, 4)), pltpu.SemaphoreType.DMA((n,))]

    def _cut(self, ref):
        return ref if self.rows is None else ref.at[pl.ds(self.rows[0], self.rows[1] - self.rows[0])]

    def first(self, ins, outs, sems):
        x, y, c = _coords()
        me = 4 * x + 2 * y + c
        for w in range(self.n):
            src, dst = self._cut(ins[w]), self._cut(outs[w].at[me])
            pltpu.make_async_copy(src, dst, sems[2].at[w]).start()
            _rcopy(src, dst, sems, w, 0, (x, y, 1 - c)).start()
            for r in (1, 2, 3):
                px, py = _chip_rel(x, y, r)
                _rcopy(src, dst, sems, w, r, (px, py, c)).start()

    def last(self, ins, outs, sems):
        x, y, c = _coords()
        me = 4 * x + 2 * y + c
        for w in range(self.n):
            src = self._cut(ins[w])
            cp = _rcopy(src, self._cut(outs[w].at[4 * x + 2 * y + 1 - c]), sems, w, 0, (x, y, 1 - c))
            cp.wait_recv()
            cp.wait_send()
            for r in (1, 2, 3):
                px, py = _chip_rel(x, y, r)
                cp = _rcopy(src, self._cut(outs[w].at[4 * px + 2 * py + c]), sems, w, r, (px, py, c))
                cp.wait_recv()
                cp.wait_send()
            pltpu.make_async_copy(src, self._cut(outs[w].at[me]), sems[2].at[w]).wait()


class _GatherPassPlan:
    mid = None

    def __init__(self, gathered):
        self.ins = list(gathered)
        self.out_shapes = [_sds(g.shape, g.dtype) for g in gathered]
        self.aliases = [(i, i) for i in range(len(gathered))]
        n = len(gathered)
        self.scratch = [pltpu.SemaphoreType.DMA((n, 3)), pltpu.SemaphoreType.DMA((n, 3))]

    def first(self, ins, outs, sems):
        x, y, c = _coords()
        for w in range(len(ins)):
            for r in (1, 2, 3):
                px, py = _chip_rel(x, y, r)
                blk = 4 * px + 2 * py + c
                _rcopy(ins[w].at[blk], outs[w].at[blk], sems, w, r - 1, (x, y, 1 - c)).start()

    def last(self, ins, outs, sems):
        x, y, c = _coords()
        for w in range(len(ins)):
            for r in (1, 2, 3):
                px, py = _chip_rel(x, y, r)
                blk = 4 * px + 2 * py + 1 - c
                cp = _rcopy(ins[w].at[blk], outs[w].at[blk], sems, w, r - 1, (x, y, 1 - c))
                cp.wait_recv()
                cp.wait_send()


class _DirectGatherPlan:
    mid = None

    def __init__(self, blocks):
        self.ins = list(blocks)
        self.out_shapes = [_sds((N_DEV,) + b.shape, b.dtype) for b in blocks]
        n = len(blocks)
        self.scratch = [pltpu.SemaphoreType.DMA((n, 7)), pltpu.SemaphoreType.DMA((n, 7)), pltpu.SemaphoreType.DMA((n,))]

    @staticmethod
    def _peer(x, y, c, r):
        return (1 - x if r & 4 else x), (1 - y if r & 2 else y), (1 - c if r & 1 else c)

    def first(self, ins, outs, sems):
        x, y, c = _coords()
        me = 4 * x + 2 * y + c
        for w in range(len(ins)):
            pltpu.make_async_copy(ins[w], outs[w].at[me], sems[2].at[w]).start()
            for r in range(1, N_DEV):
                _rcopy(ins[w], outs[w].at[me], sems, w, r - 1, self._peer(x, y, c, r)).start()

    def last(self, ins, outs, sems):
        x, y, c = _coords()
        me = 4 * x + 2 * y + c
        for w in range(len(ins)):
            for r in range(1, N_DEV):
                px, py, pc = self._peer(x, y, c, r)
                cp = _rcopy(ins[w], outs[w].at[4 * px + 2 * py + pc], sems, w, r - 1, (px, py, pc))
                cp.wait_recv()
                cp.wait_send()
            pltpu.make_async_copy(ins[w], outs[w].at[me], sems[2].at[w]).wait()


class _PairScatterPlan:
    mid = None

    def __init__(self, pieces, rows=None, into=None):
        self.n = len(pieces)
        self.rows = rows
        self.ins = list(pieces) + list(into or [])
        self.out_shapes = [_sds((N_CHIP,) + p.shape[1:], p.dtype) for p in pieces]
        self.aliases = [(self.n + i, i) for i in range(len(into or []))]
        self.scratch = [pltpu.SemaphoreType.DMA((self.n, N_CHIP)), pltpu.SemaphoreType.DMA((self.n, N_CHIP))]

    def _copies(self, ins, outs, sems):
        x, y, c = _coords()
        cps = []
        for w in range(self.n):
            for q in range(N_CHIP):
                src, dst = ins[w].at[2 * q + 1 - c], outs[w].at[q]
                if self.rows is not None:
                    cut = pl.ds(self.rows[0], self.rows[1] - self.rows[0])
                    src, dst = src.at[cut], dst.at[cut]
                cps.append(_rcopy(src, dst, sems, w, q, (x, y, 1 - c)))
        return cps

    def first(self, ins, outs, sems):
        for cp in self._copies(ins, outs, sems):
            cp.start()

    def last(self, ins, outs, sems):
        for cp in self._copies(ins, outs, sems):
            cp.wait_recv()
            cp.wait_send()


class _ChipScatterPlan:
    mid = None

    def __init__(self, sums, rows=None, into=None):
        self.n = len(sums)
        self.rows = rows
        self.ins = list(sums) + list(into or [])
        self.out_shapes = [_sds(s.shape, s.dtype) for s in sums]
        self.aliases = [(self.n + i, i) for i in range(len(into or []))]
        self.scratch = [pltpu.SemaphoreType.DMA((self.n, 3)), pltpu.SemaphoreType.DMA((self.n, 3))]

    def _copies(self, ins, outs, sems):
        x, y, c = _coords()
        cps = []
        for w in range(self.n):
            for r in (1, 2, 3):
                px, py = _chip_rel(x, y, r)
                src, dst = ins[w].at[r - 1], outs[w].at[r - 1]
                if self.rows is not None:
                    cut = pl.ds(self.rows[0], self.rows[1] - self.rows[0])
                    src, dst = src.at[cut], dst.at[cut]
                cps.append(_rcopy(src, dst, sems, w, r - 1, (px, py, c)))
        return cps

    def first(self, ins, outs, sems):
        for cp in self._copies(ins, outs, sems):
            cp.start()

    def last(self, ins, outs, sems):
        for cp in self._copies(ins, outs, sems):
            cp.wait_recv()
            cp.wait_send()


class _Comm:
    def __init__(self, plans):
        self.plans = list(plans)
        self.ins = [a for p in self.plans for a in p.ins]
        self.out_shapes = [s for p in self.plans for s in p.out_shapes]
        self.scratch = [s for p in self.plans for s in p.scratch]
        self.aliases = []
        i = o = 0
        for p in self.plans:
            self.aliases += [(i + a, o + b) for a, b in getattr(p, "aliases", [])]
            i, o = i + len(p.ins), o + len(p.out_shapes)

    def _parts(self, ins, outs, sems):
        i = o = s = 0
        for p in self.plans:
            yield p, ins[i:i + len(p.ins)], outs[o:o + len(p.out_shapes)], sems[s:s + len(p.scratch)]
            i, o, s = i + len(p.ins), o + len(p.out_shapes), s + len(p.scratch)

    def begin(self, step, nsteps, ins, outs, sems):
        @pl.when(step == 0)
        def _():
            for p, pi, po, ps in self._parts(ins, outs, sems):
                p.first(pi, po, ps)

        for p, pi, po, ps in self._parts(ins, outs, sems):
            if p.mid is not None:
                @pl.when(step == min(nsteps - 1, int(p.mid_frac * nsteps)))
                def _(p=p, pi=pi, po=po, ps=ps):
                    p.mid(pi, po, ps)

    def end(self, step, nsteps, ins, outs, sems):
        @pl.when(step == nsteps - 1)
        def _():
            for p, pi, po, ps in self._parts(ins, outs, sems):
                p.last(pi, po, ps)


def _call(body, args, *, name, grid, in_specs, out_specs, out_shape, scratch_shapes=(), sem=None, comm=None):
    in_specs, out_specs, out_shape, scratch_shapes = list(in_specs), list(out_specs), list(out_shape), list(scratch_shapes)
    if comm is None:
        res = pl.pallas_call(
            body, name=name, grid=grid, in_specs=in_specs, out_specs=out_specs, out_shape=out_shape,
            scratch_shapes=scratch_shapes,
            compiler_params=pltpu.CompilerParams(dimension_semantics=sem, vmem_limit_bytes=VMEM_LIMIT_BYTES),
        )(*args)
        return list(res), []
    n_in, n_out, n_sc = len(in_specs), len(out_specs), len(scratch_shapes)
    n_ci, n_co = len(comm.ins), len(comm.out_shapes)
    nsteps = math.prod(grid)

    def hosted(*refs):
        ins, cins = refs[:n_in], refs[n_in:n_in + n_ci]
        o0 = n_in + n_ci
        outs, couts = refs[o0:o0 + n_out], refs[o0 + n_out:o0 + n_out + n_co]
        s0 = o0 + n_out + n_co
        scr, csems = refs[s0:s0 + n_sc], refs[s0 + n_sc:]
        step = jnp.int32(0)
        for d in range(len(grid)):
            step = step * grid[d] + pl.program_id(d)
        comm.begin(step, nsteps, cins, couts, csems)
        body(*ins, *outs, *scr)
        comm.end(step, nsteps, cins, couts, csems)

    res = pl.pallas_call(
        hosted, name=name, grid=grid, in_specs=in_specs + [ANY] * n_ci, out_specs=out_specs + [ANY] * n_co,
        out_shape=out_shape + comm.out_shapes, scratch_shapes=scratch_shapes + comm.scratch,
        input_output_aliases={n_in + a: n_out + b for a, b in comm.aliases},
        compiler_params=pltpu.CompilerParams(dimension_semantics=("arbitrary",) * len(grid),
                                             vmem_limit_bytes=VMEM_LIMIT_BYTES, has_side_effects=True),
    )(*args, *comm.ins)
    return list(res[:n_out]), list(res[n_out:])


def _exchange_alone(comm, name):
    def body():
        pass

    return _call(body, [], name=name, grid=(), in_specs=[], out_specs=[], out_shape=[], comm=comm)[1]


def _matmul(a, b, *, mode, name, out_dtype=F32, out_blocks=None, tm=None, tn=None, tk=None, comm=None):
    tm = MM_TILE if tm is None else tm
    tn = MM_TILE if tn is None else tn
    a_blk = a.ndim == 3
    b_blk = b.ndim == 3
    if mode == "nn":
        M, K = a.shape
        N = b.shape[0] * b.shape[2] if b_blk else b.shape[1]
        dims = (((1,), (0,)), ((), ()))
    elif mode == "nt":
        M = a.shape[1] if a_blk else a.shape[0]
        K = a.shape[0] * a.shape[2] if a_blk else a.shape[1]
        N = b.shape[1] if b_blk else b.shape[0]
        dims = (((1,), (1,)), ((), ()))
    else:
        K, M = a.shape
        N = b.shape[0] * b.shape[2] if b_blk else b.shape[1]
        dims = (((0,), (0,)), ((), ()))

    tm = _tile(M, tm)
    tn = _tile(N, tn)
    if mode == "nt" and (a_blk or b_blk):
        tk = a.shape[2] if a_blk else b.shape[2]
    else:
        tk = _tile(K, K if tk is None else tk)
    if mode != "nt" and b_blk:
        tn = _tile(b.shape[2], tn)
    if out_blocks is not None:
        tn = _tile(out_blocks, tn)
    nk = K // tk
    grid = (M // tm, N // tn, nk)

    if mode == "nn":
        a_spec = pl.BlockSpec((tm, tk), lambda i, j, k: (i, k))
        if b_blk:
            rb = b.shape[2] // tn
            b_spec = pl.BlockSpec((None, tk, tn), lambda i, j, k: (j // rb, k, j % rb))
        else:
            b_spec = pl.BlockSpec((tk, tn), lambda i, j, k: (k, j))
    elif mode == "nt":
        if a_blk:
            a_spec = pl.BlockSpec((None, tm, tk), lambda i, j, k: (k, i, 0))
        else:
            a_spec = pl.BlockSpec((tm, tk), lambda i, j, k: (i, k))
        if b_blk:
            b_spec = pl.BlockSpec((None, tn, tk), lambda i, j, k: (k, j, 0))
        else:
            b_spec = pl.BlockSpec((tn, tk), lambda i, j, k: (j, k))
    else:
        a_spec = pl.BlockSpec((tk, tm), lambda i, j, k: (k, i))
        if b_blk:
            rb = b.shape[2] // tn
            b_spec = pl.BlockSpec((None, tk, tn), lambda i, j, k: (j // rb, k, j % rb))
        else:
            b_spec = pl.BlockSpec((tk, tn), lambda i, j, k: (k, j))

    if out_blocks is None:
        o_spec = pl.BlockSpec((tm, tn), lambda i, j, k: (i, j))
        o_shape = _sds((M, N), out_dtype)
    else:
        ro = out_blocks // tn
        o_spec = pl.BlockSpec((None, tm, tn), lambda i, j, k: (j // ro, i, j % ro))
        o_shape = _sds((N // out_blocks, M, out_blocks), out_dtype)

    direct = nk == 1 or out_dtype == F32

    def body(a_ref, b_ref, o_ref, *scratch):
        if nk == 1:
            o_ref[...] = lax.dot_general(a_ref[...], b_ref[...], dims, preferred_element_type=F32).astype(o_ref.dtype)
            return
        acc_ref = o_ref if direct else scratch[0]
        k = pl.program_id(2)

        @pl.when(k == 0)
        def _():
            acc_ref[...] = jnp.zeros(acc_ref.shape, F32)

        acc_ref[...] += lax.dot_general(a_ref[...], b_ref[...], dims, preferred_element_type=F32)
        if not direct:
            @pl.when(k == nk - 1)
            def _():
                o_ref[...] = acc_ref[...].astype(o_ref.dtype)

    scratch = [] if direct else [pltpu.VMEM((tm, tn), F32)]
    outs, landed = _call(body, [a, b], name=name, grid=grid, in_specs=[a_spec, b_spec], out_specs=[o_spec],
                         out_shape=[o_shape], scratch_shapes=scratch, sem=("parallel", "parallel", "arbitrary"), comm=comm)
    return outs[0] if comm is None else (outs[0], landed)


def _matmul_halves(a, b, *, mode, name, tm, tn=None, tk=None, comm=None):
    if mode == "nt":
        lo, hi = a
        M, kh = lo.shape
        G, N, kb = b.shape
        half = kh // kb
        tm, tn = _tile(M, tm), _tile(N, N if tn is None else tn)
        dims = (((1,), (1,)), ((), ()))

        def body(lo_ref, hi_ref, b_ref, o_ref):
            k = pl.program_id(2)

            @pl.when(k == 0)
            def _():
                o_ref[...] = jnp.zeros(o_ref.shape, F32)

            @pl.when(k < half)
            def _():
                o_ref[...] += lax.dot_general(lo_ref[...], b_ref[...], dims, preferred_element_type=F32)

            @pl.when(k >= half)
            def _():
                o_ref[...] += lax.dot_general(hi_ref[...], b_ref[...], dims, preferred_element_type=F32)

        outs, landed = _call(
            body, [lo, hi, b], name=name, grid=(M // tm, N // tn, G),
            in_specs=[pl.BlockSpec((tm, kb), lambda i, j, k: (i, jnp.minimum(k, half - 1))),
                      pl.BlockSpec((tm, kb), lambda i, j, k: (i, jnp.maximum(k - half, 0))),
                      pl.BlockSpec((None, tn, kb), lambda i, j, k: (k, j, 0))],
            out_specs=[pl.BlockSpec((tm, tn), lambda i, j, k: (i, j))], out_shape=[_sds((M, N), F32)],
            sem=("parallel", "parallel", "arbitrary"), comm=comm)
    else:
        lo, hi = b
        K, nh = lo.shape
        M = a.shape[1]
        n = tn
        half = nh // n
        tm, tk = _tile(M, tm), _tile(K, K if tk is None else tk)
        nk = K // tk
        dims = (((0,), (0,)), ((), ()))

        def body(a_ref, lo_ref, hi_ref, o_ref):
            j, k = pl.program_id(1), pl.program_id(2)

            @pl.when(k == 0)
            def _():
                o_ref[...] = jnp.zeros(o_ref.shape, F32)

            @pl.when(j < half)
            def _():
                o_ref[...] += lax.dot_general(a_ref[...], lo_ref[...], dims, preferred_element_type=F32)

            @pl.when(j >= half)
            def _():
                o_ref[...] += lax.dot_general(a_ref[...], hi_ref[...], dims, preferred_element_type=F32)

        outs, landed = _call(
            body, [a, lo, hi], name=name, grid=(M // tm, 2 * half, nk),
            in_specs=[pl.BlockSpec((tk, tm), lambda i, j, k: (k, i)),
                      pl.BlockSpec((tk, n), lambda i, j, k: (jnp.where(j < half, k, nk - 1), jnp.minimum(j, half - 1))),
                      pl.BlockSpec((tk, n), lambda i, j, k: (jnp.where(j >= half, k, 0), jnp.maximum(j - half, 0)))],
            out_specs=[pl.BlockSpec((None, tm, n), lambda i, j, k: (j, i, 0))],
            out_shape=[_sds((2 * half, M, n), F32)], sem=("parallel", "parallel", "arbitrary"), comm=comm)
    return outs[0] if comm is None else (outs[0], landed)


def _rms(x):
    return lax.rsqrt(jnp.mean(x * x, axis=-1, keepdims=True) + EPS)


def _rms_bwd(dy, x, g):
    r = _rms(x)
    n = x * r
    dn = dy * g
    dx = r * (dn - n * jnp.mean(dn * n, axis=-1, keepdims=True))
    return dx, dy * n


def _sigmoid(x):
    return 1.0 / (1.0 + jnp.exp(-x))


def _rope_rot(t):
    return pltpu.roll(t, HALF_ROPE, 1) - pltpu.roll(t, LANES - HALF_ROPE, 1)


def _lane(shape):
    return lax.broadcasted_iota(jnp.int32, shape, 1)


def _split3(x):
    hi = x.astype(BF16).astype(F32)
    r1 = x - hi
    mid = r1.astype(BF16).astype(F32)
    lo = (r1 - mid).astype(BF16).astype(F32)
    return hi, mid, lo


def _cumsum_rows(x, reverse):
    S = x.shape[0]
    bs = min(256, S)
    nb = S // bs
    r = lax.broadcasted_iota(jnp.int32, (bs, bs), 0)
    c = lax.broadcasted_iota(jnp.int32, (bs, bs), 1)
    tri = jnp.where((c >= r) if reverse else (c <= r), 1.0, 0.0).astype(BF16)
    edge = lax.broadcasted_iota(jnp.int32, (bs, x.shape[1]), 0) == (0 if reverse else bs - 1)
    carry = jnp.zeros((1, x.shape[1]), F32)
    outs = [None] * nb
    for bi in (range(nb - 1, -1, -1) if reverse else range(nb)):
        xb = x[bi * bs:(bi + 1) * bs, :]
        acc = carry
        for term in _split3(xb):
            acc = acc + jnp.dot(tri, term.astype(BF16), preferred_element_type=F32)
        outs[bi] = acc
        carry = jnp.sum(jnp.where(edge, acc, 0.0), axis=0, keepdims=True)
    return jnp.concatenate(outs, axis=0) if nb > 1 else outs[0]


def _gelu_parts(x):
    c0 = math.sqrt(2.0 / math.pi)
    inner = c0 * (x + 0.044715 * (x * x * x))
    t = jnp.tanh(inner)
    g = 0.5 * x * (1.0 + t)
    dg = 0.5 * (1.0 + t) + 0.5 * x * (1.0 - t * t) * (c0 * (1.0 + 3.0 * 0.044715 * (x * x)))
    return g, dg


def _accumulate(ref, value, first):
    @pl.when(first)
    def _():
        ref[...] = value

    @pl.when(jnp.logical_not(first))
    def _():
        ref[...] += value


def _cast_bf16(w, name):
    R, C = w.shape
    tr = _tile(R, 512, 16)

    def body(w_ref, o_ref):
        o_ref[...] = w_ref[...].astype(BF16)

    blk = pl.BlockSpec((tr, C), lambda i: (i, 0))
    return _call(body, [w], name=name, grid=(R // tr,), in_specs=[blk], out_specs=[blk],
                 out_shape=[_sds((R, C), BF16)], sem=("parallel",))[0][0]


def _concat_cols(parts, name):
    T = parts[0].shape[-2] if parts[0].ndim == 3 else parts[0].shape[0]
    widths = [p.shape[0] * LANES if p.ndim == 3 else p.shape[1] for p in parts]
    tm = _tile(T, ROW_TILE, 16)

    def body(*refs):
        o_ref = refs[-1]
        off = 0
        for p_ref, p, w in zip(refs[:-1], parts, widths):
            if p.ndim == 3:
                for hd in range(p.shape[0]):
                    o_ref[:, off + hd * LANES:off + (hd + 1) * LANES] = p_ref[hd].astype(BF16)
            else:
                o_ref[:, off:off + w] = p_ref[...].astype(BF16)
            off += w

    def spec(p, w):
        if p.ndim == 3:
            return pl.BlockSpec((p.shape[0], tm, LANES), lambda i: (0, i, 0))
        return pl.BlockSpec((tm, w), lambda i: (i, 0))

    return _call(body, parts, name=name, grid=(T // tm,),
                 in_specs=[spec(p, w) for p, w in zip(parts, widths)],
                 out_specs=[pl.BlockSpec((tm, sum(widths)), lambda i: (i, 0))],
                 out_shape=[_sds((T, sum(widths)), BF16)], sem=("parallel",))[0][0]


def _prenorm(x, g, comm=None):
    T, D = x.shape
    tm = _tile(T, ROW_TILE, 16)

    def body(x_ref, g_ref, h_ref):
        xv = x_ref[...]
        h_ref[...] = (xv * _rms(xv) * g_ref[...]).astype(BF16)

    row = pl.BlockSpec((tm, D), lambda i: (i, 0))
    (h,), landed = _call(body, [x, g], name="prenorm", grid=(T // tm,),
                         in_specs=[row, pl.BlockSpec((1, D), lambda i: (0, 0))], out_specs=[row],
                         out_shape=[_sds((T, D), BF16)], sem=("parallel",), comm=comm)
    return h, landed


def _split_prep(proj, pos, invf, gq, gkv, bfor, lay, comm=None):
    T = proj.shape[0]
    tm = _tile(T, ROW_TILE, 16)

    def body(q_ref, kv_ref, kpe_ref, fl_ref, pos_ref, invf_ref, gq_ref, gkv_ref, bf_ref,
             qn_ref, kvn_ref, kper_ref, logf_ref, cos_ref, sin_ref):
        ql = q_ref[...]
        qn_ref[...] = (ql * _rms(ql) * gq_ref[...]).astype(BF16)
        kl = kv_ref[...]
        kvn_ref[...] = (kl * _rms(kl) * gkv_ref[...]).astype(BF16)
        ang = pos_ref[...].astype(F32) * invf_ref[...]
        valid = _lane(ang.shape) < ROPE
        cs = jnp.where(valid, jnp.cos(ang), 0.0)
        sn = jnp.where(valid, jnp.sin(ang), 0.0)
        cos_ref[...] = cs
        sin_ref[...] = sn
        kp = jnp.where(valid, kpe_ref[...], 0.0)
        kper_ref[...] = (kp * cs + _rope_rot(kp) * sn).astype(BF16)
        z = fl_ref[...] + bf_ref[...]
        logf_ref[...] = jnp.minimum(z, 0.0) - jnp.log(1.0 + jnp.exp(-jnp.abs(z)))

    def col(width, off):
        return pl.BlockSpec((tm, width), lambda i: (i, off // width))

    def vec(width):
        return pl.BlockSpec((1, width), lambda i: (0, 0))

    def out(width):
        return pl.BlockSpec((tm, width), lambda i: (i, 0))

    return _call(
        body, [proj, proj, proj, proj, pos, invf, gq, gkv, bfor], name="split_prep", grid=(T // tm,),
        in_specs=[col(Q_LORA, lay["q"]), col(KV_LORA, lay["kv"]), col(LANES, lay["kpe"]), col(LANES, lay["fl"]),
                  pl.BlockSpec((tm, 1), lambda i: (i, 0)), vec(LANES), vec(Q_LORA), vec(KV_LORA), vec(LANES)],
        out_specs=[out(Q_LORA), out(KV_LORA), out(LANES), out(LANES), out(LANES), out(LANES)],
        out_shape=[_sds((T, Q_LORA), BF16), _sds((T, KV_LORA), BF16), _sds((T, LANES), BF16),
                   _sds((T, LANES), F32), _sds((T, LANES), F32), _sds((T, LANES), F32)],
        sem=("parallel",), comm=comm)


def _mla_prep(qraw, kvraw, kper, cosT, sinT, comm=None):
    H, T, _ = qraw.shape
    tm = _tile(T, HEAD_ROW_TILE, 16)

    def body(q_ref, kv_ref, kpe_ref, cos_ref, sin_ref, qo_ref, ko_ref, vo_ref):
        q = q_ref[...]
        pe = q[:, NOPE:]
        pe = jnp.where(_lane(pe.shape) < ROPE, pe, 0.0)
        qo_ref[:, :NOPE] = q[:, :NOPE].astype(BF16)
        qo_ref[:, NOPE:] = (pe * cos_ref[...] + _rope_rot(pe) * sin_ref[...]).astype(BF16)
        kv = kv_ref[...]
        ko_ref[:, :NOPE] = kv[:, :NOPE].astype(BF16)
        ko_ref[:, NOPE:] = kpe_ref[...]
        vo_ref[...] = kv[:, NOPE:].astype(BF16)

    head = pl.BlockSpec((None, tm, ATT_DK), lambda h, i: (h, i, 0))
    tok = pl.BlockSpec((tm, LANES), lambda h, i: (i, 0))
    return _call(
        body, [qraw, kvraw, kper, cosT, sinT], name="mla_prep", grid=(H, T // tm),
        in_specs=[head, head, tok, tok, tok],
        out_specs=[head, head, pl.BlockSpec((None, tm, VDIM), lambda h, i: (h, i, 0))],
        out_shape=[_sds((H, T, ATT_DK), BF16), _sds((H, T, ATT_DK), BF16), _sds((H, T, VDIM), BF16)],
        sem=("parallel", "parallel"), comm=comm)


def _fox_cumsum(logf, B, S, inv_scale):
    T = logf.shape[0]

    def body(l_ref, c_ref):
        c_ref[...] = _cumsum_rows(l_ref[...], reverse=False) * inv_scale

    seq = pl.BlockSpec((S, LANES), lambda b: (b, 0))
    return _call(body, [logf], name="fox_cumsum", grid=(B,), in_specs=[seq], out_specs=[seq],
                 out_shape=[_sds((T, LANES), F32)], sem=("parallel",))[0][0]


def _fox_prep(proj, cs, lay, comm=None):
    T = proj.shape[0]
    tm = _tile(T, HEAD_ROW_TILE, 16)

    def body(q_ref, k_ref, v_ref, cs_ref, qo_ref, ko_ref, vo_ref):
        h = pl.program_id(0)
        cv = cs_ref[...]
        lane = _lane(cv.shape)
        ccol = jnp.sum(jnp.where(lane == h, cv, 0.0), axis=1, keepdims=True)
        hi, mid, lo = _split3(ccol)
        one = jnp.where(lane < 6, 1.0, 0.0)
        augq = jnp.where(lane == 0, hi, jnp.where(lane == 1, mid, jnp.where(lane == 2, lo, one)))
        augk = jnp.where(lane < 3, 1.0, jnp.where(lane == 3, -hi, jnp.where(lane == 4, -mid, jnp.where(lane == 5, -lo, 0.0))))
        qo_ref[:, :FOX_DIM] = q_ref[...].astype(BF16)
        qo_ref[:, FOX_DIM:] = augq.astype(BF16)
        ko_ref[:, :FOX_DIM] = k_ref[...].astype(BF16)
        ko_ref[:, FOX_DIM:] = augk.astype(BF16)
        vo_ref[...] = v_ref[...].astype(BF16)

    def col(off):
        return pl.BlockSpec((tm, FOX_DIM), lambda h, i: (i, off // FOX_DIM + h))

    head = pl.BlockSpec((None, tm, ATT_DK), lambda h, i: (h, i, 0))
    return _call(
        body, [proj, proj, proj, cs], name="fox_prep", grid=(HEADS, T // tm),
        in_specs=[col(lay["fq"]), col(lay["fk"]), col(lay["fv"]), pl.BlockSpec((tm, LANES), lambda h, i: (i, 0))],
        out_specs=[head, head, pl.BlockSpec((None, tm, VDIM), lambda h, i: (h, i, 0))],
        out_shape=[_sds((HEADS, T, ATT_DK), BF16), _sds((HEADS, T, ATT_DK), BF16), _sds((HEADS, T, VDIM), BF16)],
        sem=("parallel", "parallel"), comm=comm)


def _visible(tq, tk, unit):
    r = lax.broadcasted_iota(jnp.int32, (tq, tk), 0)
    c = lax.broadcasted_iota(jnp.int32, (tq, tk), 1)
    sh = int(math.log2(unit))
    return lax.shift_right_logical(c, sh) <= lax.shift_right_logical(r, sh)


def _attn_fwd(streams, *, B, S, name, comm=None):
    n = len(streams)
    H, T, DK = streams[0][0].shape
    DV = streams[0][2].shape[2]
    tq = _tile(S, ATT_TILE)
    nq = S // tq
    sub = min(ATT_SUB, tq)
    NT = (((1,), (1,)), ((), ()))

    def body(*refs):
        ins, outs, (m_sc, acc_sc) = refs[:3 * n], refs[3 * n:5 * n], refs[5 * n:]
        i, j = pl.program_id(1), pl.program_id(2)

        @pl.when(j == 0)
        def _():
            m_sc[...] = jnp.full(m_sc.shape, NEG_INF, F32)
            acc_sc[...] = jnp.zeros(acc_sc.shape, F32)

        def step(diagonal):
            work = [(t, r) for r in range(tq // sub) for t in range(n)]

            def scores(t, r):
                q_ref, k_ref, _ = ins[3 * t:3 * t + 3]
                kc = (r + 1) * sub if diagonal else tq
                s = lax.dot_general(q_ref[r * sub:(r + 1) * sub, :], k_ref[0:kc, :], NT, preferred_element_type=F32)
                return s * (streams[t][4] * LOG2_E)

            ahead = [scores(*work[w]) for w in range(min(ATT_AHEAD, len(work)))]
            for w, (t, r) in enumerate(work):
                s = ahead.pop(0)
                if w + ATT_AHEAD < len(work):
                    ahead.append(scores(*work[w + ATT_AHEAD]))
                v_ref = ins[3 * t + 2]
                kc = s.shape[1]
                rows = slice(r * sub, (r + 1) * sub)
                if diagonal:
                    own = jnp.where(_visible(sub, sub, streams[t][3]), s[:, kc - sub:], NEG_INF)
                    s = own if kc == sub else jnp.concatenate([s[:, :kc - sub], own], axis=1)
                m_prev = m_sc[t, rows, :]
                mx = s[:, 0:LANES]
                for g in range(1, kc // LANES):
                    mx = jnp.maximum(mx, s[:, g * LANES:(g + 1) * LANES])
                m_new = jnp.maximum(m_prev, jnp.max(mx, axis=1, keepdims=True))
                alpha = jnp.exp2(m_prev - m_new)
                p = jnp.exp2(s - jnp.tile(m_new, (1, kc // LANES))).astype(BF16)
                v_aug = jnp.concatenate([v_ref[0:kc, :], jnp.ones((kc, LANES), BF16)], axis=1)
                acc_sc[t, rows, :] = jnp.tile(alpha, (1, 2)) * acc_sc[t, rows, :] + jnp.dot(
                    p, v_aug, preferred_element_type=F32)
                m_sc[t, rows, :] = m_new

        @pl.when(j < i)
        def _():
            step(False)

        @pl.when(j == i)
        def _():
            step(True)
            for t in range(n):
                o_ref, lse_ref = outs[2 * t:2 * t + 2]
                l = acc_sc[t, :, DV:]
                o_ref[...] = (acc_sc[t, :, :DV] / l).astype(BF16)
                lse_ref[...] = m_sc[t] + jnp.log2(l)

    def qmap(g, i, j):
        return (g % H, (g // H) * nq + i, 0)

    def kmap(g, i, j):
        return (g % H, (g // H) * nq + jnp.minimum(j, i), 0)

    args = [a for st in streams for a in st[:3]]
    outs, landed = _call(
        body, args, name=name, grid=(B * H, nq, nq),
        in_specs=[pl.BlockSpec((None, tq, DK), qmap), pl.BlockSpec((None, tq, DK), kmap),
                  pl.BlockSpec((None, tq, DV), kmap)] * n,
        out_specs=[pl.BlockSpec((tq, DV), lambda g, i, j: ((g // H) * nq + i, g % H)),
                   pl.BlockSpec((None, tq, LANES), qmap)] * n,
        out_shape=[_sds((T, H * DV), BF16), _sds((H, T, LANES), F32)] * n,
        scratch_shapes=[pltpu.VMEM((n, tq, LANES), F32), pltpu.VMEM((n, tq, DV + LANES), F32)],
        sem=("parallel", "parallel", "arbitrary"), comm=comm)
    return [(outs[2 * t], outs[2 * t + 1]) for t in range(n)], landed


def _attn_bwd(streams, *, B, S, name, comm=None):
    n = len(streams)
    H, T, DK = streams[0][0].shape
    DV = streams[0][2].shape[2]
    tq = _tile(S, ATT_TILE)
    nq = S // tq
    sub = min(ATT_SUB, tq)
    NT = (((1,), (1,)), ((), ()))
    TN = (((0,), (0,)), ((), ()))

    def body(*refs):
        ins, outs = refs[:6 * n], refs[6 * n:]
        j, i = pl.program_id(1), pl.program_id(2)

        @pl.when(jnp.logical_and(j == 0, i == 0))
        def _():
            for t in range(n):
                outs[3 * t][...] = jnp.zeros(outs[3 * t].shape, F32)

        @pl.when(i == 0)
        def _():
            for t in range(n):
                outs[3 * t + 1][...] = jnp.zeros(outs[3 * t + 1].shape, F32)
                outs[3 * t + 2][...] = jnp.zeros(outs[3 * t + 2].shape, F32)

        def step(diagonal):
            work = [(t, r) for r in range(tq // sub) for t in range(n)]

            def kcols(r):
                return (r + 1) * sub if diagonal else tq

            def scores(t, r):
                q_ref, k_ref, v_ref, _, do_ref, _ = ins[6 * t:6 * t + 6]
                rows, kc = slice(r * sub, (r + 1) * sub), kcols(r)
                s = lax.dot_general(q_ref[rows, :], k_ref[0:kc, :], NT, preferred_element_type=F32)
                dp = lax.dot_general(do_ref[rows, :], v_ref[0:kc, :], NT, preferred_element_type=F32)
                return s * (streams[t][7] * LOG2_E), dp

            def probs(t, r, s, dp):
                _, _, _, o_ref, do_ref, lse_ref = ins[6 * t:6 * t + 6]
                rows, kc = slice(r * sub, (r + 1) * sub), kcols(r)
                if diagonal:
                    own = jnp.where(_visible(sub, sub, streams[t][6]), s[:, kc - sub:], NEG_INF)
                    s = own if kc == sub else jnp.concatenate([s[:, :kc - sub], own], axis=1)
                p = jnp.exp2(s - jnp.tile(lse_ref[rows, :], (1, kc // LANES)))
                delta = jnp.sum(do_ref[rows, :].astype(F32) * o_ref[rows, :].astype(F32), axis=1, keepdims=True)
                return p.astype(BF16), (p * (dp - delta) * streams[t][7]).astype(BF16)

            def grads(t, r, p, ds):
                q_ref, k_ref, _, _, do_ref, _ = ins[6 * t:6 * t + 6]
                dq_ref, dk_ref, dv_ref = outs[3 * t:3 * t + 3]
                rows, kc = slice(r * sub, (r + 1) * sub), kcols(r)
                dv_ref[0:kc, :] += lax.dot_general(p, do_ref[rows, :], TN, preferred_element_type=F32)
                dk_ref[0:kc, :] += lax.dot_general(ds, q_ref[rows, :], TN, preferred_element_type=F32)
                qrows = pl.ds(pl.multiple_of(i * tq + r * sub, sub), sub)
                dq_ref[qrows, :] += jnp.dot(ds, k_ref[0:kc, :], preferred_element_type=F32)

            nw = len(work)
            sc = {w: scores(*work[w]) for w in range(min(2, nw))}
            pr = {0: probs(*work[0], *sc.pop(0))}
            for w in range(nw):
                if w + 2 < nw:
                    sc[w + 2] = scores(*work[w + 2])
                if w + 1 < nw:
                    pr[w + 1] = probs(*work[w + 1], *sc.pop(w + 1))
                grads(*work[w], *pr.pop(w))

        @pl.when(i > j)
        def _():
            step(False)

        @pl.when(i == j)
        def _():
            step(True)

    def qmap(g, j, i):
        return (g % H, (g // H) * nq + jnp.maximum(i, j), 0)

    def kmap(g, j, i):
        return (g % H, (g // H) * nq + j, 0)

    def omap(g, j, i):
        return ((g // H) * nq + jnp.maximum(i, j), g % H)

    args = [a for st in streams for a in st[:6]]
    outs, landed = _call(
        body, args, name=name, grid=(B * H, nq, nq),
        in_specs=[pl.BlockSpec((None, tq, DK), qmap), pl.BlockSpec((None, tq, DK), kmap),
                  pl.BlockSpec((None, tq, DV), kmap), pl.BlockSpec((tq, DV), omap), pl.BlockSpec((tq, DV), omap),
                  pl.BlockSpec((None, tq, LANES), qmap)] * n,
        out_specs=[pl.BlockSpec((None, S, DK), lambda g, j, i: (g % H, g // H, 0)),
                   pl.BlockSpec((None, tq, DK), kmap), pl.BlockSpec((None, tq, DV), kmap)] * n,
        out_shape=[_sds((H, T, DK), F32), _sds((H, T, DK), F32), _sds((H, T, DV), F32)] * n,
        sem=("parallel", "arbitrary", "arbitrary"), comm=comm)
    return [tuple(outs[3 * t:3 * t + 3]) for t in range(n)], landed


def _gate_merge(am, af, proj, bgate, lay, D, comm=None):
    T = am.shape[0]
    tm = _tile(T, ROW_TILE, 16)
    tn = _tile(D, 1024)

    def body(am_ref, af_ref, gm_ref, gf_ref, bm_ref, bf_ref, o_ref):
        sm = _sigmoid(gm_ref[...] + bm_ref[...])
        sf = _sigmoid(gf_ref[...] + bf_ref[...])
        o_ref[...] = (sm * am_ref[...] + sf * af_ref[...]).astype(BF16)

    og = lay["g"] // tn
    blk = pl.BlockSpec((tm, tn), lambda i, j: (i, j))
    return _call(
        body, [am, af, proj, proj, bgate, bgate], name="gate_merge", grid=(T // tm, D // tn),
        in_specs=[blk, blk, pl.BlockSpec((tm, tn), lambda i, j: (i, og + j)),
                  pl.BlockSpec((tm, tn), lambda i, j: (i, og + D // tn + j)),
                  pl.BlockSpec((1, tn), lambda i, j: (0, j)), pl.BlockSpec((1, tn), lambda i, j: (0, D // tn + j))],
        out_specs=[blk], out_shape=[_sds((T, D), BF16)], sem=("parallel", "parallel"), comm=comm)


def _out_mid(merged, w_out, x, g_pm, g_ffn):
    T, D = x.shape
    tm = _tile(T, 2 * ROW_TILE, 16)

    def body(a_ref, w_ref, x_ref, gp_ref, gf_ref, y_ref, x1_ref, h2_ref):
        y = jnp.dot(a_ref[...], w_ref[...], preferred_element_type=F32)
        y_ref[...] = y
        x1 = x_ref[...] + y * _rms(y) * gp_ref[...]
        x1_ref[...] = x1
        h2_ref[...] = (x1 * _rms(x1) * gf_ref[...]).astype(BF16)

    row = pl.BlockSpec((tm, D), lambda i: (i, 0))
    vec = pl.BlockSpec((1, D), lambda i: (0, 0))
    return _call(body, [merged, w_out, x, g_pm, g_ffn], name="mm_out_mid", grid=(T // tm,),
                 in_specs=[row, pl.BlockSpec((D, D), lambda i: (0, 0)), row, vec, vec], out_specs=[row, row, row],
                 out_shape=[_sds((T, D), F32), _sds((T, D), F32), _sds((T, D), BF16)], sem=("parallel",))[0]


def _conv3(u, w_ref, bias):
    row = lax.broadcasted_iota(jnp.int32, u.shape, 0)
    u1 = jnp.where(row >= 1, pltpu.roll(u, 1, 0), 0.0)
    u2 = jnp.where(row >= 2, pltpu.roll(u, 2, 0), 0.0)
    return w_ref[0:1, :] * u2 + w_ref[1:2, :] * u1 + w_ref[2:3, :] * u + bias, u1, u2


def _convffn_fwd(u, cw, cb, B, S, F, comm=None):
    T = u.shape[0]
    tn = _tile(F, 256)
    nf = F // tn

    def body(ug_ref, uv_ref, wg_ref, wv_ref, bg_ref, bv_ref, a_ref):
        g, _, _ = _conv3(ug_ref[...], wg_ref, bg_ref[...])
        val, _, _ = _conv3(uv_ref[...], wv_ref, bv_ref[...])
        a_ref[...] = (_gelu_parts(g)[0] * val).astype(BF16)

    def seq(off):
        return pl.BlockSpec((S, tn), lambda b, j: (b, off + j))

    def par(rows, off):
        return pl.BlockSpec((rows, tn), lambda b, j: (0, off + j))

    return _call(body, [u, u, cw, cw, cb, cb], name="convffn_fwd", grid=(B, nf),
                 in_specs=[seq(0), seq(nf), par(3, 0), par(3, nf), par(1, 0), par(1, nf)],
                 out_specs=[seq(0)], out_shape=[_sds((T, F), BF16)], sem=("parallel", "parallel"), comm=comm)


def _convffn_bwd(u, dact, cw, cb, B, S, F, comm=None):
    T = u.shape[0]
    tn = _tile(F, 256)
    nf = F // tn

    def body(ug_ref, uv_ref, da_ref, wg_ref, wv_ref, bg_ref, bv_ref, dug_ref, duv_ref, dpg_ref, dpv_ref):
        b = pl.program_id(1)
        ug, uv, da = ug_ref[...], uv_ref[...], da_ref[...]
        g, ug1, ug2 = _conv3(ug, wg_ref, bg_ref[...])
        val, uv1, uv2 = _conv3(uv, wv_ref, bv_ref[...])
        gel, dgel = _gelu_parts(g)
        dg = da * val * dgel
        dval = da * gel
        row = lax.broadcasted_iota(jnp.int32, ug.shape, 0)

        def back(d, w_ref):
            d1 = jnp.where(row < S - 1, pltpu.roll(d, S - 1, 0), 0.0)
            d2 = jnp.where(row < S - 2, pltpu.roll(d, S - 2, 0), 0.0)
            return w_ref[2:3, :] * d + w_ref[1:2, :] * d1 + w_ref[0:1, :] * d2

        dug_ref[...] = back(dg, wg_ref).astype(BF16)
        duv_ref[...] = back(dval, wv_ref).astype(BF16)

        def sums(d, u0, u1, u2):
            r8 = lax.broadcasted_iota(jnp.int32, (8, d.shape[1]), 0)
            out = jnp.zeros((8, d.shape[1]), F32)
            for k, t in enumerate((d * u2, d * u1, d * u0, d)):
                out = jnp.where(r8 == k, jnp.sum(t, axis=0, keepdims=True), out)
            return out

        _accumulate(dpg_ref, sums(dg, ug, ug1, ug2), b == 0)
        _accumulate(dpv_ref, sums(dval, uv, uv1, uv2), b == 0)

    def seq(off):
        return pl.BlockSpec((S, tn), lambda j, b: (b, off + j))

    def par(rows, off):
        return pl.BlockSpec((rows, tn), lambda j, b: (0, off + j))

    outs, landed = _call(
        body, [u, u, dact, cw, cw, cb, cb], name="convffn_bwd", grid=(nf, B),
        in_specs=[seq(0), seq(nf), seq(0), par(3, 0), par(3, nf), par(1, 0), par(1, nf)],
        out_specs=[seq(0), seq(0), par(8, 0), par(8, 0)],
        out_shape=[_sds((T, F), BF16), _sds((T, F), BF16), _sds((8, F), F32), _sds((8, F), F32)],
        sem=("parallel", "arbitrary"), comm=comm)
    return outs, landed


def _tail(ff, x1, tgt, g):
    T, D = ff.shape
    tm = _tile(T, ROW_TILE, 16)

    def body(ff_ref, x1_ref, t_ref, g_ref, dy_ref, dff_ref, loss_ref, dg_ref):
        i = pl.program_id(0)
        f = ff_ref[...]
        gv = g_ref[...]
        r = _rms(f)
        n = f * r
        e = (x1_ref[...] + n * gv) - t_ref[...]
        dy = e * (1.0 / D)
        dy_ref[...] = dy
        dn = dy * gv
        dff_ref[...] = (r * (dn - n * jnp.mean(dn * n, axis=-1, keepdims=True))).astype(BF16)
        part = 0.5 * jnp.sum(jnp.mean(e * e, axis=-1, keepdims=True), axis=0, keepdims=True)
        _accumulate(loss_ref, jnp.broadcast_to(part, loss_ref.shape), i == 0)
        _accumulate(dg_ref, jnp.sum(dy * n, axis=0, keepdims=True), i == 0)

    row = pl.BlockSpec((tm, D), lambda i: (i, 0))
    vec = pl.BlockSpec((1, D), lambda i: (0, 0))
    return _call(body, [ff, x1, tgt, g], name="tail", grid=(T // tm,), in_specs=[row, row, row, vec],
                 out_specs=[row, row, pl.BlockSpec((8, LANES), lambda i: (0, 0)), vec],
                 out_shape=[_sds((T, D), F32), _sds((T, D), BF16), _sds((8, LANES), F32), _sds((1, D), F32)],
                 sem=("arbitrary",))[0]


def _mid_bwd(dy, dh2, x1, y1, g_ffn, g_pm, comm=None):
    T, D = dy.shape
    tm = _tile(T, ROW_TILE, 16)

    def body(dy_ref, dh_ref, x1_ref, y1_ref, gf_ref, gp_ref, dx1_ref, dy1_ref, dgf_ref, dgp_ref):
        i = pl.program_id(0)
        dh = dh_ref[...]
        d2, dgf = _rms_bwd(dh, x1_ref[...], gf_ref[...])
        dx1 = dy_ref[...] + d2
        dx1_ref[...] = dx1
        d1, dgp = _rms_bwd(dx1, y1_ref[...], gp_ref[...])
        dy1_ref[...] = d1.astype(BF16)
        _accumulate(dgf_ref, jnp.sum(dgf, axis=0, keepdims=True), i == 0)
        _accumulate(dgp_ref, jnp.sum(dgp, axis=0, keepdims=True), i == 0)

    row = pl.BlockSpec((tm, D), lambda i: (i, 0))
    vec = pl.BlockSpec((1, D), lambda i: (0, 0))
    return _call(body, [dy, dh2, x1, y1, g_ffn, g_pm], name="mid_bwd", grid=(T // tm,),
                 in_specs=[row, row, row, row, vec, vec], out_specs=[row, row, vec, vec],
                 out_shape=[_sds((T, D), F32), _sds((T, D), BF16), _sds((1, D), F32), _sds((1, D), F32)],
                 sem=("arbitrary",), comm=comm)


def _gate_bwd(dy1, w_out, am, af, proj, bgate, lay, D, comm=None):
    T = dy1.shape[0]
    tm = _tile(T, 2 * ROW_TILE, 16)
    tn = _tile(D, 512)
    NT = (((1,), (1,)), ((), ()))

    def body(dy_ref, w_ref, am_ref, af_ref, gm_ref, gf_ref, bm_ref, bf_ref,
             dam_ref, daf_ref, dgm_ref, dgf_ref, dbm_ref, dbf_ref):
        i = pl.program_id(1)
        d = lax.dot_general(dy_ref[...], w_ref[...], NT, preferred_element_type=F32)
        sm = _sigmoid(gm_ref[...] + bm_ref[...])
        sf = _sigmoid(gf_ref[...] + bf_ref[...])
        dam_ref[...] = (d * sm).astype(BF16)
        daf_ref[...] = (d * sf).astype(BF16)
        dgm = d * am_ref[...] * (sm * (1.0 - sm))
        dgf = d * af_ref[...] * (sf * (1.0 - sf))
        dgm_ref[...] = dgm.astype(BF16)
        dgf_ref[...] = dgf.astype(BF16)
        _accumulate(dbm_ref, jnp.sum(dgm, axis=0, keepdims=True), i == 0)
        _accumulate(dbf_ref, jnp.sum(dgf, axis=0, keepdims=True), i == 0)

    og = lay["g"] // tn
    blk = pl.BlockSpec((tm, tn), lambda j, i: (i, j))
    vec = pl.BlockSpec((1, tn), lambda j, i: (0, j))
    return _call(
        body, [dy1, w_out, am, af, proj, proj, bgate, bgate], name="mm_dmerged_gate_bwd", grid=(D // tn, T // tm),
        in_specs=[pl.BlockSpec((tm, D), lambda j, i: (i, 0)), pl.BlockSpec((tn, D), lambda j, i: (j, 0)),
                  blk, blk, pl.BlockSpec((tm, tn), lambda j, i: (i, og + j)),
                  pl.BlockSpec((tm, tn), lambda j, i: (i, og + D // tn + j)),
                  vec, pl.BlockSpec((1, tn), lambda j, i: (0, D // tn + j))],
        out_specs=[blk, blk, blk, blk, vec, vec],
        out_shape=[_sds((T, D), BF16)] * 4 + [_sds((1, D), F32)] * 2, sem=("parallel", "arbitrary"), comm=comm)


def _mla_bwd_prep(dq, dk, dv, cosT, sinT, comm=None):
    H, T, _ = dq.shape
    tm = _tile(T, HEAD_ROW_TILE, 16)

    def body(dq_ref, dk_ref, dv_ref, cos_ref, sin_ref, dqr_ref, dkv_ref, dkpe_ref):
        h = pl.program_id(1)
        cs, sn = cos_ref[...], sin_ref[...]
        valid = _lane(cs.shape) < ROPE

        def unrope(d):
            d = jnp.where(valid, d, 0.0)
            return d * cs - _rope_rot(d) * sn

        dqv = dq_ref[...]
        dqr_ref[:, :NOPE] = dqv[:, :NOPE].astype(BF16)
        dqr_ref[:, NOPE:] = unrope(dqv[:, NOPE:]).astype(BF16)
        dkv_ = dk_ref[...]
        dkv_ref[:, :NOPE] = dkv_[:, :NOPE].astype(BF16)
        dkv_ref[:, NOPE:] = dv_ref[...].astype(BF16)
        _accumulate(dkpe_ref, unrope(dkv_[:, NOPE:]), h == 0)

    head = pl.BlockSpec((None, tm, ATT_DK), lambda i, h: (h, i, 0))
    tok = pl.BlockSpec((tm, LANES), lambda i, h: (i, 0))
    return _call(
        body, [dq, dk, dv, cosT, sinT], name="mla_bwd_prep", grid=(T // tm, H),
        in_specs=[head, head, pl.BlockSpec((None, tm, VDIM), lambda i, h: (h, i, 0)), tok, tok],
        out_specs=[head, head, tok],
        out_shape=[_sds((H, T, ATT_DK), BF16), _sds((H, T, ATT_DK), BF16), _sds((T, LANES), F32)],
        sem=("parallel", "arbitrary"), comm=comm)


def _fox_bwd_prep(dq, dk, proj, bfor, lay, B, S, inv_scale):
    H, T, _ = dq.shape

    def body(dq_ref, dk_ref, fl_ref, bf_ref, dfl_ref, dbf_ref, dc_sc):
        b, h = pl.program_id(0), pl.program_id(1)
        lane = _lane(dc_sc.shape)
        col = jnp.sum(jnp.where(lane == 0, dq_ref[...], 0.0) - jnp.where(lane == 3, dk_ref[...], 0.0),
                      axis=1, keepdims=True)

        @pl.when(h == 0)
        def _():
            dc_sc[...] = jnp.zeros(dc_sc.shape, F32)

        dc_sc[...] = jnp.where(lane == h, col, dc_sc[...])

        @pl.when(h == H - 1)
        def _():
            dlogf = _cumsum_rows(dc_sc[...] * inv_scale, reverse=True)
            z = fl_ref[...] + bf_ref[...]
            dz = jnp.where(lane < H, dlogf * (1.0 / (1.0 + jnp.exp(z))), 0.0)
            dfl_ref[...] = dz
            _accumulate(dbf_ref, jnp.sum(dz, axis=0, keepdims=True), b == 0)

    aug = pl.BlockSpec((None, S, LANES), lambda b, h: (h, b, 1))
    seq = pl.BlockSpec((S, LANES), lambda b, h: (b, 0))
    vec = pl.BlockSpec((1, LANES), lambda b, h: (0, 0))
    return _call(
        body, [dq, dk, proj, bfor], name="fox_bwd_prep", grid=(B, H),
        in_specs=[aug, aug, pl.BlockSpec((S, LANES), lambda b, h: (b, lay["fl"] // LANES)), vec],
        out_specs=[seq, vec], out_shape=[_sds((T, LANES), F32), _sds((1, LANES), F32)],
        scratch_shapes=[pltpu.VMEM((S, LANES), F32)], sem=("arbitrary", "arbitrary"))[0]


def _lat_bwd(dqn, dkvn, proj, gq, gkv, lay):
    T = dqn.shape[0]
    tm = _tile(T, ROW_TILE, 16)

    def body(dq_ref, dkv_ref, q_ref, kv_ref, gq_ref, gkv_ref, dql_ref, dkl_ref, dgq_ref, dgkv_ref):
        i = pl.program_id(0)
        dql, dgq = _rms_bwd(dq_ref[...], q_ref[...], gq_ref[...])
        dkl, dgkv = _rms_bwd(dkv_ref[...], kv_ref[...], gkv_ref[...])
        dql_ref[...] = dql.astype(BF16)
        dkl_ref[...] = dkl.astype(BF16)
        _accumulate(dgq_ref, jnp.sum(dgq, axis=0, keepdims=True), i == 0)
        _accumulate(dgkv_ref, jnp.sum(dgkv, axis=0, keepdims=True), i == 0)

    def blk(width, off=0):
        return pl.BlockSpec((tm, width), lambda i: (i, off // width))

    def vec(width):
        return pl.BlockSpec((1, width), lambda i: (0, 0))

    return _call(
        body, [dqn, dkvn, proj, proj, gq, gkv], name="lat_bwd", grid=(T // tm,),
        in_specs=[blk(Q_LORA), blk(KV_LORA), blk(Q_LORA, lay["q"]), blk(KV_LORA, lay["kv"]), vec(Q_LORA), vec(KV_LORA)],
        out_specs=[blk(Q_LORA), blk(KV_LORA), vec(Q_LORA), vec(KV_LORA)],
        out_shape=[_sds((T, Q_LORA), BF16), _sds((T, KV_LORA), BF16), _sds((1, Q_LORA), F32), _sds((1, KV_LORA), F32)],
        sem=("arbitrary",))[0]


def _dh_final(dproj, w_perm, dx1, x, g, comm=None):
    T, D = x.shape
    K = dproj.shape[1]
    tm = _tile(T, 2 * ROW_TILE, 16)
    tk = _tile(K, 1024)
    nk = K // tk
    NT = (((1,), (1,)), ((), ()))

    def body(a_ref, b_ref, dx1_ref, x_ref, g_ref, dx_ref, dg_ref):
        i, k = pl.program_id(0), pl.program_id(1)

        @pl.when(k == 0)
        def _():
            dx_ref[...] = jnp.zeros(dx_ref.shape, F32)

        dx_ref[...] += lax.dot_general(a_ref[...], b_ref[...], NT, preferred_element_type=F32)

        @pl.when(k == nk - 1)
        def _():
            d, dg = _rms_bwd(dx_ref[...], x_ref[...], g_ref[...])
            dx_ref[...] = dx1_ref[...] + d
            _accumulate(dg_ref, jnp.sum(dg, axis=0, keepdims=True), i == 0)

    row = pl.BlockSpec((tm, D), lambda i, k: (i, 0))
    vec = pl.BlockSpec((1, D), lambda i, k: (0, 0))
    return _call(body, [dproj, w_perm, dx1, x, g], name="mm_dh_final", grid=(T // tm, nk),
                 in_specs=[pl.BlockSpec((tm, tk), lambda i, k: (i, k)), pl.BlockSpec((D, tk), lambda i, k: (0, k)),
                           row, row, vec],
                 out_specs=[row, vec], out_shape=[_sds((T, D), F32), _sds((1, D), F32)],
                 sem=("arbitrary", "arbitrary"), comm=comm)


def _chip_sum(pieces, paired, qc, name):
    G, R, C = pieces.shape
    tr = _tile(R, 256, 16)

    def body(qc_ref, g_ref, p_ref, keep_ref, send_ref):
        s = pl.program_id(1)
        tot = g_ref[...] + p_ref[...]

        @pl.when(s == 0)
        def _():
            keep_ref[...] = tot

        @pl.when(s > 0)
        def _():
            send_ref[...] = tot.astype(send_ref.dtype)

    grid_spec = pltpu.PrefetchScalarGridSpec(
        num_scalar_prefetch=1, grid=(R // tr, N_CHIP),
        in_specs=[pl.BlockSpec((None, tr, C), lambda i, s, qc: (2 * (qc[0] ^ s) + qc[1], i, 0)),
                  pl.BlockSpec((None, tr, C), lambda i, s, qc: (qc[0] ^ s, i, 0))],
        out_specs=[pl.BlockSpec((tr, C), lambda i, s, qc: (i, 0)),
                   pl.BlockSpec((None, tr, C), lambda i, s, qc: (jnp.maximum(s - 1, 0), i, 0))])
    send_dtype = BF16 if R >= 16 else pieces.dtype
    return pl.pallas_call(
        body, name=name, grid_spec=grid_spec,
        out_shape=[_sds((R, C), F32), _sds((3, R, C), send_dtype)],
        compiler_params=pltpu.CompilerParams(dimension_semantics=("arbitrary", "arbitrary"),
                                             vmem_limit_bytes=VMEM_LIMIT_BYTES),
    )(qc, pieces, paired)


def _adamw_math(w, g, m, v):
    m = ADAM_B1 * m + (1.0 - ADAM_B1) * g
    v = ADAM_B2 * v + (1.0 - ADAM_B2) * (g * g)
    m_hat = m / (1.0 - ADAM_B1 ** ADAM_STEP)
    v_hat = v / (1.0 - ADAM_B2 ** ADAM_STEP)
    delta = -ADAM_LR * (m_hat / (jnp.sqrt(v_hat) + ADAM_EPS) + ADAM_WD * w)
    return delta, m, v


def _sum_adamw(keep, pieces, w, m, v, name):
    R, C = w.shape
    P = pieces.shape[0]
    tr = _tile(R, 256, 16)

    def body(k_ref, p_ref, w_ref, m_ref, v_ref, g_ref, d_ref, mo_ref, vo_ref):
        g = k_ref[...]
        for q in range(P):
            g = g + p_ref[q].astype(F32)
        g_ref[...] = g
        d_ref[...], mo_ref[...], vo_ref[...] = _adamw_math(w_ref[...], g, m_ref[...], v_ref[...])

    blk = pl.BlockSpec((tr, C), lambda i: (i, 0))
    pblk = pl.BlockSpec((P, tr, C), lambda i: (0, i, 0))
    return _call(body, [keep, pieces, w, m, v], name=name, grid=(R // tr,), in_specs=[blk, pblk, blk, blk, blk],
                 out_specs=[blk] * 4, out_shape=[_sds((R, C), F32)] * 4, sem=("parallel",))[0]


def _adamw_small(parts, w, m, v, widths):
    n = len(widths)

    def body(p_ref, w_ref, m_ref, v_ref, *o_refs):
        g = p_ref[0]
        for q in range(1, N_DEV):
            g = g + p_ref[q]
        vals = (g,) + _adamw_math(w_ref[...], g, m_ref[...], v_ref[...])
        off = 0
        for i, wd in enumerate(widths):
            for kind in range(4):
                o_refs[4 * i + kind][...] = vals[kind][:, off:off + wd]
            off += wd

    whole = pl.BlockSpec(memory_space=pltpu.VMEM)
    outs = _call(body, [parts, w, m, v], name="adamw_small", grid=(), in_specs=[whole] * 4,
                 out_specs=[whole] * (4 * n), out_shape=[_sds((1, wd), F32) for wd in widths for _ in range(4)])[0]
    return [tuple(outs[4 * i:4 * i + 4]) for i in range(n)]


def _layout(D):
    lay = {"q": 0, "kv": Q_LORA, "kpe": Q_LORA + KV_LORA}
    lay["fq"] = lay["kpe"] + LANES
    lay["fk"] = lay["fq"] + HEADS * FOX_DIM
    lay["fv"] = lay["fk"] + HEADS * FOX_DIM
    lay["fl"] = lay["fv"] + HEADS * FOX_DIM
    lay["g"] = lay["fl"] + LANES
    lay["end"] = lay["g"] + 2 * D
    return lay


def kernel(x, positions, pre_mix_norm, w_in, q_a_norm, w_uq, kv_a_norm, w_ukv, b_forget, b_gate, w_branch_mla, w_branch_fox, w_out, post_mix_norm, pre_ffn_norm, w_up, conv_w, conv_b, w_down, post_ffn_norm, loss_target, m_pre_mix_norm, m_w_in, m_q_a_norm, m_w_uq, m_kv_a_norm, m_w_ukv, m_b_forget, m_b_gate, m_w_branch_mla, m_w_branch_fox, m_w_out, m_post_mix_norm, m_pre_ffn_norm, m_w_up, m_conv_w, m_conv_b, m_w_down, m_post_ffn_norm, v_pre_mix_norm, v_w_in, v_q_a_norm, v_w_uq, v_kv_a_norm, v_w_ukv, v_b_forget, v_b_gate, v_w_branch_mla, v_w_branch_fox, v_w_out, v_post_mix_norm, v_pre_ffn_norm, v_w_up, v_conv_w, v_conv_b, v_w_down, v_post_ffn_norm):
    B, S, D = x.shape
    T = B * S
    F = conv_b.shape[0] // 2
    lay = _layout(D)
    n_in = w_in.shape[1]
    d_in = N_DEV * n_in
    seg_a = Q_LORA + KV_LORA + ROPE
    seg_b = 3 * HEADS * FOX_DIM + HEADS
    mla_scale = (NOPE + ROPE) ** -0.5
    fox_scale = FOX_DIM ** -0.5
    ax, ay, ac = (lax.axis_index(a) for a in MESH_AXES)
    qc = jnp.stack([2 * ax + ay, ac]).astype(jnp.int32)

    def row(vec, width=None):
        vec = vec.reshape(1, -1)
        if width is not None and vec.shape[1] < width:
            vec = jnp.pad(vec, ((0, 0), (0, width - vec.shape[1])))
        return vec

    x2 = x.reshape(T, D)
    win_s = _cast_bf16(w_in, "cast_w_in")
    h, (win_g,) = _prenorm(x2, row(pre_mix_norm), comm=_Comm([_GatherRelayPlan([win_s], mid_frac=0.3)]))
    small_s = [_cast_bf16(w, "cast_" + n) for w, n in
               [(w_uq, "w_uq"), (w_ukv, "w_ukv"), (w_branch_mla, "w_branch_mla"), (w_branch_fox, "w_branch_fox"), (w_out, "w_out")]]
    wup_s = _cast_bf16(w_up, "cast_w_up")
    wdown_s = _cast_bf16(w_down, "cast_w_down")

    def shard_cols(lo, hi):
        out = []
        for g in range(lo // n_in, (hi - 1) // n_in + 1):
            out.append(win_g[g][:, max(lo, g * n_in) - g * n_in:min(hi, (g + 1) * n_in) - g * n_in])
        return out

    w_perm = jnp.concatenate(
        shard_cols(0, seg_a) + [jnp.zeros((D, LANES - ROPE), BF16)] + shard_cols(seg_a, seg_a + seg_b)
        + [jnp.zeros((D, LANES - HEADS), BF16)] + shard_cols(seg_a + seg_b, d_in), axis=1)

    tgt = loss_target.reshape(T, D)
    pos = positions.reshape(T, 1)
    inv_freq = 1.0 / (ROPE_THETA ** (jnp.arange(0, ROPE, 2, dtype=F32) / ROPE))
    invf = row(jnp.concatenate([inv_freq, inv_freq]), LANES)
    g_pre, g_q, g_kv = row(pre_mix_norm), row(q_a_norm), row(kv_a_norm)
    g_pm, g_ffn, g_pf = row(post_mix_norm), row(pre_ffn_norm), row(post_ffn_norm)
    bfor = row(b_forget, LANES)
    bgate = row(b_gate)
    cb_full = row(conv_b)

    def own_plan(blocks):
        return _Comm([_GatherOwnPlan(blocks)])

    def pass_plan(gathered):
        return _Comm([_GatherPassPlan(gathered)])

    def pair_plan(gs):
        return _Comm([_PairScatterPlan(gs)])

    def chip_plan(gs):
        return _Comm([_ChipScatterPlan(gs)])

    half_d = D // 2
    proj, landed = _matmul(h, w_perm, mode="nn", name="mm_proj", comm=_Comm(
        [_GatherOwnPlan(small_s[:2] + [conv_w]), _GatherOwnPlan([wup_s], rows=(0, half_d))]))
    early_g, wup_part = landed[:-1], landed[-1:]
    (qn, kvn, kper, logf, cosT, sinT), (wuq_g, wukv_g, cw_g) = _split_prep(
        proj, pos, invf, g_q, g_kv, bfor, lay, comm=pass_plan(early_g))
    wuq_pad = jnp.pad(wuq_g, ((0, 0), (0, 0), (0, ATT_DK - NOPE - ROPE)))
    cw_full = jnp.transpose(cw_g, (1, 0, 2)).reshape(3, 2 * F)

    qraw = _matmul(qn, wuq_pad, mode="nn", name="mm_q", out_blocks=ATT_DK, tm=T)
    kvraw = _matmul(kvn, wukv_g, mode="nn", name="mm_kv", out_blocks=NOPE + VDIM, tm=T)
    (q_mla, k_mla, v_mla), branch_half = _mla_prep(qraw, kvraw, kper, cosT, sinT, comm=own_plan(small_s[2:4]))
    cs = _fox_cumsum(logf, B, S, 1.0 / fox_scale)
    (q_fox, k_fox, v_fox), wout_half = _fox_prep(proj, cs, lay, comm=own_plan(small_s[4:5]))
    ((o_mla, lse_mla), (o_fox, lse_fox)), landed = _attn_fwd(
        [(q_mla, k_mla, v_mla, MLA_UNIT, mla_scale), (q_fox, k_fox, v_fox, 1, fox_scale)], B=B, S=S,
        name="attn_fwd", comm=_Comm([_GatherOwnPlan([wup_s], rows=(half_d, D), into=wup_part),
                                     _GatherPassPlan(branch_half + wout_half)]))
    wup_half, (wbm_g, wbf_g, wout_g) = landed[:1], landed[1:]
    wbm = jnp.transpose(wbm_g, (1, 0, 2)).reshape(HEADS * VDIM, D)
    wbf = jnp.transpose(wbf_g, (1, 0, 2)).reshape(HEADS * FOX_DIM, D)
    wout = wout_g.reshape(D, D)
    a_m = _matmul(o_mla, wbm, mode="nn", name="mm_branch_mla", tm=2 * MM_TILE)
    a_f = _matmul(o_fox, wbf, mode="nn", name="mm_branch_fox", tm=2 * MM_TILE)
    (merged,), (wup_g,) = _gate_merge(a_m, a_f, proj, bgate, lay, D, comm=pass_plan(wup_half))
    n_up = wup_g.shape[2]
    y1, x1, h2 = _out_mid(merged, wout, x2, g_pm, g_ffn)
    u, wdown_half = _matmul(h2, wup_g, mode="nn", name="mm_up", tn=n_up, comm=own_plan([wdown_s]))
    (act,), (wdown_g,) = _convffn_fwd(u, cw_full, cb_full, B, S, F, comm=pass_plan(wdown_half))
    wdown = wdown_g.reshape(F, D)
    ff = _matmul(act, wdown, mode="nn", name="mm_down", tk=F // 2)
    dy, dff, loss_part, dg_pf = _tail(ff, x1, tgt, g_pf)

    dact = _matmul(dff, wdown, mode="nt", name="mm_dact", tn=F // 4)
    dw_down = _matmul(act, dff, mode="tn", name="mm_dw_down", tm=F // 4, tn=512).reshape(N_DEV, F // N_DEV, D)
    (du_g, du_v, dcp_g, dcp_v), (pa_down,) = _convffn_bwd(u, dact, cw_full, cb_full, B, S, F, comm=pair_plan([dw_down]))
    keep_down, sb_down = _chip_sum(dw_down, pa_down, qc, "chipsum_w_down")
    dh2, (rb_down,) = _matmul_halves((du_g, du_v), wup_g, mode="nt", name="mm_dh2", tm=MM_TILE, comm=chip_plan([sb_down]))
    dw_up = _matmul_halves(h2, (du_g, du_v), mode="tn", name="mm_dw_up", tm=MM_TILE, tn=n_up, tk=T // 2)
    (dx1, dy1, dg_ffn, dg_pm), _ = _mid_bwd(dy, dh2, x1, y1, g_ffn, g_pm)
    dw_out, pa_up_part = _matmul(merged, dy1, mode="tn", name="mm_dw_out",
                                 comm=_Comm([_PairScatterPlan([dw_up], rows=(0, half_d))]))
    dw_out = dw_out.reshape(N_DEV, D // N_DEV, D)
    (da_m, da_f, dgl_m, dgl_f, dbg_m, dbg_f), (pa_up,) = _gate_bwd(
        dy1, wout, a_m, a_f, proj, bgate, lay, D,
        comm=_Comm([_PairScatterPlan([dw_up], rows=(half_d, D), into=pa_up_part)]))
    keep_up, sb_up = _chip_sum(dw_up, pa_up, qc, "chipsum_w_up")
    dw_bm = _matmul(o_mla, da_m, mode="tn", name="mm_dw_branch_mla", out_blocks=D // N_DEV)
    dw_bf = _matmul(o_fox, da_f, mode="tn", name="mm_dw_branch_fox", out_blocks=D // N_DEV)
    mix = [dw_out, dw_bm, dw_bf]
    do_mla, pa_mix = _matmul(da_m, wbm, mode="nt", name="mm_do_mla", out_dtype=BF16, comm=pair_plan(mix))
    do_fox = _matmul(da_f, wbf, mode="nt", name="mm_do_fox", out_dtype=BF16)
    mix_sums = [_chip_sum(g, p, qc, "chipsum_" + n) for g, p, n in zip(mix, pa_mix, ["w_out", "w_branch_mla", "w_branch_fox"])]
    ((dq_m, dk_m, dv_m), (dq_f, dk_f, dv_f)), (rb_up,) = _attn_bwd(
        [(q_mla, k_mla, v_mla, o_mla, do_mla, lse_mla, MLA_UNIT, mla_scale),
         (q_fox, k_fox, v_fox, o_fox, do_fox, lse_fox, 1, fox_scale)], B=B, S=S, name="attn_bwd",
        comm=chip_plan([sb_up]))
    (dqraw, dkvraw, dkpe), _ = _mla_bwd_prep(dq_m, dk_m, dv_m, cosT, sinT)
    dqn = _matmul(dqraw, wuq_pad, mode="nt", name="mm_dqn", tm=T)
    dw_uq = _matmul(qn, dqraw, mode="tn", name="mm_dw_uq", out_blocks=ATT_DK)[:, :, :NOPE + ROPE]
    dkvn = _matmul(dkvraw, wukv_g, mode="nt", name="mm_dkvn", tm=T)
    dw_ukv = _matmul(kvn, dkvraw, mode="tn", name="mm_dw_ukv", out_blocks=NOPE + VDIM)
    dqlat, dkvlat, dg_q, dg_kv = _lat_bwd(dqn, dkvn, proj, g_q, g_kv, lay)
    dfl, dbfor = _fox_bwd_prep(dq_f, dk_f, proj, bfor, lay, B, S, 1.0 / fox_scale)
    dproj = _concat_cols([dqlat, dkvlat, dkpe, dq_f, dk_f, dv_f, dfl, dgl_m, dgl_f], "concat_dproj")
    dw_perm, rb_mix = _matmul(h, dproj, mode="tn", name="mm_dw_in", comm=chip_plan([s[1] for s in mix_sums]))
    segs = [(0, seg_a, 0), (seg_a, seg_a + seg_b, lay["fq"] - seg_a), (seg_a + seg_b, d_in, lay["g"] - seg_a - seg_b)]

    def piece(g):
        lo, hi = g * n_in, (g + 1) * n_in
        parts = [dw_perm[:, max(lo, s0) + sh:min(hi, s1) + sh] for s0, s1, sh in segs if max(lo, s0) < min(hi, s1)]
        return parts[0] if len(parts) == 1 else jnp.concatenate(parts, axis=1)

    dw_in = jnp.stack([piece(g) for g in range(N_DEV)])
    dcw = jnp.transpose(jnp.concatenate([dcp_g[0:3], dcp_v[0:3]], axis=1).reshape(3, N_DEV, (2 * F) // N_DEV), (1, 0, 2))
    late = [dw_in, dw_uq, dw_ukv, dcw]
    pa_late = _exchange_alone(pair_plan(late), "pair_late")
    late_sums = [_chip_sum(g, p, qc, "chipsum_" + n) for g, p, n in zip(late, pa_late, ["w_in", "w_uq", "w_ukv", "conv_w"])]
    (grad_x, dg_pre), rb_late = _dh_final(dproj, w_perm, dx1, x2, g_pre, comm=chip_plan([s[1] for s in late_sums]))

    big_out = {}

    def finish(n, keep, pieces, w, m, v):
        big_out[n] = _sum_adamw(keep, pieces, w, m, v, "adamw_" + n)

    finish("w_down", keep_down, rb_down, w_down, m_w_down, v_w_down)
    finish("w_up", keep_up, rb_up, w_up, m_w_up, v_w_up)
    finish("w_out", mix_sums[0][0], rb_mix[0], w_out, m_w_out, v_w_out)
    finish("w_branch_mla", mix_sums[1][0], rb_mix[1], w_branch_mla, m_w_branch_mla, v_w_branch_mla)
    finish("w_branch_fox", mix_sums[2][0], rb_mix[2], w_branch_fox, m_w_branch_fox, v_w_branch_fox)
    finish("w_in", late_sums[0][0], rb_late[0], w_in, m_w_in, v_w_in)
    finish("w_uq", late_sums[1][0], rb_late[1], w_uq, m_w_uq, v_w_uq)
    finish("w_ukv", late_sums[2][0], rb_late[2], w_ukv, m_w_ukv, v_w_ukv)
    finish("conv_w", late_sums[3][0], rb_late[3], conv_w, m_conv_w, v_conv_w)

    widths = [D, Q_LORA, KV_LORA, LANES, 2 * D, D, D, 2 * F, D]
    small_names = ["pre_mix_norm", "q_a_norm", "kv_a_norm", "b_forget", "b_gate", "post_mix_norm", "pre_ffn_norm",
                   "conv_b", "post_ffn_norm"]
    true_w = [D, Q_LORA, KV_LORA, HEADS, 2 * D, D, D, 2 * F, D]
    dcb = jnp.concatenate([dcp_g[3:4], dcp_v[3:4]], axis=1)
    part = jnp.concatenate([dg_pre, dg_q, dg_kv, dbfor, dbg_m, dbg_f, dg_pm, dg_ffn, dcb, dg_pf], axis=1)

    def pack(vals):
        return jnp.concatenate([row(a, wd) for a, wd in zip(vals, widths)], axis=1)

    sw = pack([pre_mix_norm, q_a_norm, kv_a_norm, b_forget, b_gate, post_mix_norm, pre_ffn_norm, conv_b, post_ffn_norm])
    sm = pack([m_pre_mix_norm, m_q_a_norm, m_kv_a_norm, m_b_forget, m_b_gate, m_post_mix_norm, m_pre_ffn_norm,
               m_conv_b, m_post_ffn_norm])
    sv = pack([v_pre_mix_norm, v_q_a_norm, v_kv_a_norm, v_b_forget, v_b_gate, v_post_mix_norm, v_pre_ffn_norm,
               v_conv_b, v_post_ffn_norm])
    (parts_all,) = _exchange_alone(_Comm([_DirectGatherPlan([part])]), "gather_small")
    small = _adamw_small(parts_all, sw, sm, sv, widths)
    small_out = {n: tuple(a.reshape(-1)[:tw] for a in vals) for n, vals, tw in zip(small_names, small, true_w)}

    loss = lax.psum(loss_part[0, 0], MESH_AXES)
    order = ["pre_mix_norm", "w_in", "q_a_norm", "w_uq", "kv_a_norm", "w_ukv", "b_forget", "b_gate", "w_branch_mla",
             "w_branch_fox", "w_out", "post_mix_norm", "pre_ffn_norm", "w_up", "conv_w", "conv_b", "w_down",
             "post_ffn_norm"]
    res = {**big_out, **small_out}
    outs = [loss, grad_x.reshape(B, S, D)]
    for kind in range(4):
        outs += [res[n][kind] for n in order]
    return tuple(outs)
```

```python
import math

import jax
import jax.numpy as jnp
from jax import lax
from jax.experimental import pallas as pl
from jax.experimental.pallas import tpu as pltpu

F32 = jnp.float32
BF16 = jnp.bfloat16

N_DEV = 8
N_CHIP = 4
HEADS = 8
NOPE = 128
ROPE = 64
HALF_ROPE = ROPE // 2
VDIM = 128
Q_LORA = 512
KV_LORA = 256
FOX_DIM = 128
ATT_DK = 256
MLA_UNIT = 64
ROPE_THETA = 10000.0
EPS = 1e-6
NEG_INF = -1e30
LANES = 128
LOG2_E = 1.4426950408889634

ADAM_LR = 0.001
ADAM_B1 = 0.9
ADAM_B2 = 0.999
ADAM_EPS = 1e-08
ADAM_WD = 0.01
ADAM_STEP = 10

VMEM_LIMIT_BYTES = 56 * 1024 * 1024
ROW_TILE = 256
HEAD_ROW_TILE = 1024
ATT_TILE = 1024
ATT_SUB = 256
ATT_AHEAD = 3
MM_TILE = 1024

MESH_AXES = ("x", "y", "c")
ANY = pl.BlockSpec(memory_space=pl.ANY)


def _tile(n, pref, align=LANES):
    if n <= pref:
        return n
    t = (pref // align) * align
    while t >= align:
        if n % t == 0:
            return t
        t -= align
    return n


def _sds(shape, dtype):
    return jax.ShapeDtypeStruct(shape, dtype)


def _coords():
    x, y, c = (lax.axis_index(ax) for ax in MESH_AXES)
    return x, y, c


def _chip_rel(x, y, r):
    return (1 - x if r & 2 else x), (1 - y if r & 1 else y)


def _rcopy(src, dst, sems, w, k, dev):
    return pltpu.make_async_remote_copy(src_ref=src, dst_ref=dst, send_sem=sems[0].at[w, k], recv_sem=sems[1].at[w, k],
                                        device_id=dev, device_id_type=pl.DeviceIdType.MESH)


class _GatherRelayPlan:
    def __init__(self, blocks, mid_frac=0.5):
        self.ins = list(blocks)
        self.out_shapes = [_sds((N_DEV,) + b.shape, b.dtype) for b in blocks]
        n = len(blocks)
        self.scratch = [pltpu.SemaphoreType.DMA((n, 7)), pltpu.SemaphoreType.DMA((n, 7)), pltpu.SemaphoreType.DMA((n,))]
        self.mid_frac = mid_frac

    @staticmethod
    def _places():
        x, y, c = _coords()
        xn, yn = 4 * (1 - x) + 2 * y, 4 * x + 2 * (1 - y)
        relay_src = 4 * (x + c * (1 - 2 * x)) + 2 * (y + (1 - c) * (1 - 2 * y)) + c
        relay_to = (x + (1 - c) * (1 - 2 * x), y + c * (1 - 2 * y), c)
        return x, y, c, xn, yn, relay_src, relay_to, 4 * (1 - x) + 2 * (1 - y)

    def first(self, ins, outs, sems):
        x, y, c, _, _, _, _, _ = self._places()
        me = 4 * x + 2 * y + c
        for w in range(len(ins)):
            pltpu.make_async_copy(ins[w], outs[w].at[me], sems[2].at[w]).start()
            _rcopy(ins[w], outs[w].at[me], sems, w, 0, (x, y, 1 - c)).start()
            _rcopy(ins[w], outs[w].at[me], sems, w, 1, (1 - x, y, c)).start()
            _rcopy(ins[w], outs[w].at[me], sems, w, 2, (x, 1 - y, c)).start()

    def mid(self, ins, outs, sems):
        x, y, c, xn, yn, relay_src, relay_to, _ = self._places()
        sib = (x, y, 1 - c)
        for w in range(len(ins)):
            bx, by = outs[w].at[xn + c], outs[w].at[yn + c]
            _rcopy(ins[w], bx, sems, w, 1, (1 - x, y, c)).wait_recv()
            _rcopy(ins[w], by, sems, w, 2, (x, 1 - y, c)).wait_recv()
            _rcopy(outs[w].at[relay_src], outs[w].at[relay_src], sems, w, 3, relay_to).start()
            _rcopy(bx, bx, sems, w, 4, sib).start()
            _rcopy(by, by, sems, w, 5, sib).start()

    def last(self, ins, outs, sems):
        x, y, c, xn, yn, _, relay_to, dg = self._places()
        me = 4 * x + 2 * y + c
        sib = (x, y, 1 - c)
        for w in range(len(ins)):
            bd = outs[w].at[dg + c]
            _rcopy(ins[w], bd, sems, w, 3, relay_to).wait_recv()
            _rcopy(bd, bd, sems, w, 6, sib).start()
            for k, blk in ((0, 4 * x + 2 * y), (4, xn), (5, yn), (6, dg)):
                _rcopy(ins[w], outs[w].at[blk + 1 - c], sems, w, k, sib).wait_recv()
            for k in range(7):
                _rcopy(ins[w], outs[w].at[me], sems, w, k, sib).wait_send()
            pltpu.make_async_copy(ins[w], outs[w].at[me], sems[2].at[w]).wait()


class _GatherOwnPlan:
    mid = None

    def __init__(self, blocks, rows=None, into=None):
        self.n = len(blocks)
        self.rows = rows
        self.ins = list(blocks) + list(into or [])
        self.out_shapes = [_sds((N_DEV,) + b.shape, b.dtype) for b in blocks]
        self.aliases = [(self.n + i, i) for i in range(len(into or []))]
        n = self.n
        self.scratch = [pltpu.SemaphoreType.DMA((n, 4)), pltpu.SemaphoreType.DMA((n, 4)), pltpu.SemaphoreType.DMA((n,))]

    def _cut(self, ref):
        return ref if self.rows is None else ref.at[pl.ds(self.rows[0], self.rows[1] - self.rows[0])]

    def first(self, ins, outs, sems):
        x, y, c = _coords()
        me = 4 * x + 2 * y + c
        for w in range(self.n):
            src, dst = self._cut(ins[w]), self._cut(outs[w].at[me])
            pltpu.make_async_copy(src, dst, sems[2].at[w]).start()
            _rcopy(src, dst, sems, w, 0, (x, y, 1 - c)).start()
            for r in (1, 2, 3):
                px, py = _chip_rel(x, y, r)
                _rcopy(src, dst, sems, w, r, (px, py, c)).start()

    def last(self, ins, outs, sems):
        x, y, c = _coords()
        me = 4 * x + 2 * y + c
        for w in range(self.n):
            src = self._cut(ins[w])
            cp = _rcopy(src, self._cut(outs[w].at[4 * x + 2 * y + 1 - c]), sems, w, 0, (x, y, 1 - c))
            cp.wait_recv()
            cp.wait_send()
            for r in (1, 2, 3):
                px, py = _chip_rel(x, y, r)
                cp = _rcopy(src, self._cut(outs[w].at[4 * px + 2 * py + c]), sems, w, r, (px, py, c))
                cp.wait_recv()
                cp.wait_send()
            pltpu.make_async_copy(src, self._cut(outs[w].at[me]), sems[2].at[w]).wait()


class _GatherPassPlan:
    mid = None

    def __init__(self, gathered):
        self.ins = list(gathered)
        self.out_shapes = [_sds(g.shape, g.dtype) for g in gathered]
        self.aliases = [(i, i) for i in range(len(gathered))]
        n = len(gathered)
        self.scratch = [pltpu.SemaphoreType.DMA((n, 3)), pltpu.SemaphoreType.DMA((n, 3))]

    def first(self, ins, outs, sems):
        x, y, c = _coords()
        for w in range(len(ins)):
            for r in (1, 2, 3):
                px, py = _chip_rel(x, y, r)
                blk = 4 * px + 2 * py + c
                _rcopy(ins[w].at[blk], outs[w].at[blk], sems, w, r - 1, (x, y, 1 - c)).start()

    def last(self, ins, outs, sems):
        x, y, c = _coords()
        for w in range(len(ins)):
            for r in (1, 2, 3):
                px, py = _chip_rel(x, y, r)
                blk = 4 * px + 2 * py + 1 - c
                cp = _rcopy(ins[w].at[blk], outs[w].at[blk], sems, w, r - 1, (x, y, 1 - c))
                cp.wait_recv()
                cp.wait_send()


class _DirectGatherPlan:
    mid = None

    def __init__(self, blocks):
        self.ins = list(blocks)
        self.out_shapes = [_sds((N_DEV,) + b.shape, b.dtype) for b in blocks]
        n = len(blocks)
        self.scratch = [pltpu.SemaphoreType.DMA((n, 7)), pltpu.SemaphoreType.DMA((n, 7)), pltpu.SemaphoreType.DMA((n,))]

    @staticmethod
    def _peer(x, y, c, r):
        return (1 - x if r & 4 else x), (1 - y if r & 2 else y), (1 - c if r & 1 else c)

    def first(self, ins, outs, sems):
        x, y, c = _coords()
        me = 4 * x + 2 * y + c
        for w in range(len(ins)):
            pltpu.make_async_copy(ins[w], outs[w].at[me], sems[2].at[w]).start()
            for r in range(1, N_DEV):
                _rcopy(ins[w], outs[w].at[me], sems, w, r - 1, self._peer(x, y, c, r)).start()

    def last(self, ins, outs, sems):
        x, y, c = _coords()
        me = 4 * x + 2 * y + c
        for w in range(len(ins)):
            for r in range(1, N_DEV):
                px, py, pc = self._peer(x, y, c, r)
                cp = _rcopy(ins[w], outs[w].at[4 * px + 2 * py + pc], sems, w, r - 1, (px, py, pc))
                cp.wait_recv()
                cp.wait_send()
            pltpu.make_async_copy(ins[w], outs[w].at[me], sems[2].at[w]).wait()


class _PairScatterPlan:
    mid = None

    def __init__(self, pieces, rows=None, into=None):
        self.n = len(pieces)
        self.rows = rows
        self.ins = list(pieces) + list(into or [])
        self.out_shapes = [_sds((N_CHIP,) + p.shape[1:], p.dtype) for p in pieces]
        self.aliases = [(self.n + i, i) for i in range(len(into or []))]
        self.scratch = [pltpu.SemaphoreType.DMA((self.n, N_CHIP)), pltpu.SemaphoreType.DMA((self.n, N_CHIP))]

    def _copies(self, ins, outs, sems):
        x, y, c = _coords()
        cps = []
        for w in range(self.n):
            for q in range(N_CHIP):
                src, dst = ins[w].at[2 * q + 1 - c], outs[w].at[q]
                if self.rows is not None:
                    cut = pl.ds(self.rows[0], self.rows[1] - self.rows[0])
                    src, dst = src.at[cut], dst.at[cut]
                cps.append(_rcopy(src, dst, sems, w, q, (x, y, 1 - c)))
        return cps

    def first(self, ins, outs, sems):
        for cp in self._copies(ins, outs, sems):
            cp.start()

    def last(self, ins, outs, sems):
        for cp in self._copies(ins, outs, sems):
            cp.wait_recv()
            cp.wait_send()


class _ChipScatterPlan:
    mid = None

    def __init__(self, sums, rows=None, into=None):
        self.n = len(sums)
        self.rows = rows
        self.ins = list(sums) + list(into or [])
        self.out_shapes = [_sds(s.shape, s.dtype) for s in sums]
        self.aliases = [(self.n + i, i) for i in range(len(into or []))]
        self.scratch = [pltpu.SemaphoreType.DMA((self.n, 3)), pltpu.SemaphoreType.DMA((self.n, 3))]

    def _copies(self, ins, outs, sems):
        x, y, c = _coords()
        cps = []
        for w in range(self.n):
            for r in (1, 2, 3):
                px, py = _chip_rel(x, y, r)
                src, dst = ins[w].at[r - 1], outs[w].at[r - 1]
                if self.rows is not None:
                    cut = pl.ds(self.rows[0], self.rows[1] - self.rows[0])
                    src, dst = src.at[cut], dst.at[cut]
                cps.append(_rcopy(src, dst, sems, w, r - 1, (px, py, c)))
        return cps

    def first(self, ins, outs, sems):
        for cp in self._copies(ins, outs, sems):
            cp.start()

    def last(self, ins, outs, sems):
        for cp in self._copies(ins, outs, sems):
            cp.wait_recv()
            cp.wait_send()


class _Comm:
    def __init__(self, plans):
        self.plans = list(plans)
        self.ins = [a for p in self.plans for a in p.ins]
        self.out_shapes = [s for p in self.plans for s in p.out_shapes]
        self.scratch = [s for p in self.plans for s in p.scratch]
        self.aliases = []
        i = o = 0
        for p in self.plans:
            self.aliases += [(i + a, o + b) for a, b in getattr(p, "aliases", [])]
            i, o = i + len(p.ins), o + len(p.out_shapes)

    def _parts(self, ins, outs, sems):
        i = o = s = 0
        for p in self.plans:
            yield p, ins[i:i + len(p.ins)], outs[o:o + len(p.out_shapes)], sems[s:s + len(p.scratch)]
            i, o, s = i + len(p.ins), o + len(p.out_shapes), s + len(p.scratch)

    def begin(self, step, nsteps, ins, outs, sems):
        @pl.when(step == 0)
        def _():
            for p, pi, po, ps in self._parts(ins, outs, sems):
                p.first(pi, po, ps)

        for p, pi, po, ps in self._parts(ins, outs, sems):
            if p.mid is not None:
                @pl.when(step == min(nsteps - 1, int(p.mid_frac * nsteps)))
                def _(p=p, pi=pi, po=po, ps=ps):
                    p.mid(pi, po, ps)

    def end(self, step, nsteps, ins, outs, sems):
        @pl.when(step == nsteps - 1)
        def _():
            for p, pi, po, ps in self._parts(ins, outs, sems):
                p.last(pi, po, ps)


def _call(body, args, *, name, grid, in_specs, out_specs, out_shape, scratch_shapes=(), sem=None, comm=None):
    in_specs, out_specs, out_shape, scratch_shapes = list(in_specs), list(out_specs), list(out_shape), list(scratch_shapes)
    if comm is None:
        res = pl.pallas_call(
            body, name=name, grid=grid, in_specs=in_specs, out_specs=out_specs, out_shape=out_shape,
            scratch_shapes=scratch_shapes,
            compiler_params=pltpu.CompilerParams(dimension_semantics=sem, vmem_limit_bytes=VMEM_LIMIT_BYTES),
        )(*args)
        return list(res), []
    n_in, n_out, n_sc = len(in_specs), len(out_specs), len(scratch_shapes)
    n_ci, n_co = len(comm.ins), len(comm.out_shapes)
    nsteps = math.prod(grid)

    def hosted(*refs):
        ins, cins = refs[:n_in], refs[n_in:n_in + n_ci]
        o0 = n_in + n_ci
        outs, couts = refs[o0:o0 + n_out], refs[o0 + n_out:o0 + n_out + n_co]
        s0 = o0 + n_out + n_co
        scr, csems = refs[s0:s0 + n_sc], refs[s0 + n_sc:]
        step = jnp.int32(0)
        for d in range(len(grid)):
            step = step * grid[d] + pl.program_id(d)
        comm.begin(step, nsteps, cins, couts, csems)
        body(*ins, *outs, *scr)
        comm.end(step, nsteps, cins, couts, csems)

    res = pl.pallas_call(
        hosted, name=name, grid=grid, in_specs=in_specs + [ANY] * n_ci, out_specs=out_specs + [ANY] * n_co,
        out_shape=out_shape + comm.out_shapes, scratch_shapes=scratch_shapes + comm.scratch,
        input_output_aliases={n_in + a: n_out + b for a, b in comm.aliases},
        compiler_params=pltpu.CompilerParams(dimension_semantics=("arbitrary",) * len(grid),
                                             vmem_limit_bytes=VMEM_LIMIT_BYTES, has_side_effects=True),
    )(*args, *comm.ins)
    return list(res[:n_out]), list(res[n_out:])


def _exchange_alone(comm, name):
    def body():
        pass

    return _call(body, [], name=name, grid=(), in_specs=[], out_specs=[], out_shape=[], comm=comm)[1]


def _matmul(a, b, *, mode, name, out_dtype=F32, out_blocks=None, tm=None, tn=None, tk=None, comm=None):
    tm = MM_TILE if tm is None else tm
    tn = MM_TILE if tn is None else tn
    a_blk = a.ndim == 3
    b_blk = b.ndim == 3
    if mode == "nn":
        M, K = a.shape
        N = b.shape[0] * b.shape[2] if b_blk else b.shape[1]
        dims = (((1,), (0,)), ((), ()))
    elif mode == "nt":
        M = a.shape[1] if a_blk else a.shape[0]
        K = a.shape[0] * a.shape[2] if a_blk else a.shape[1]
        N = b.shape[1] if b_blk else b.shape[0]
        dims = (((1,), (1,)), ((), ()))
    else:
        K, M = a.shape
        N = b.shape[0] * b.shape[2] if b_blk else b.shape[1]
        dims = (((0,), (0,)), ((), ()))

    tm = _tile(M, tm)
    tn = _tile(N, tn)
    if mode == "nt" and (a_blk or b_blk):
        tk = a.shape[2] if a_blk else b.shape[2]
    else:
        tk = _tile(K, K if tk is None else tk)
    if mode != "nt" and b_blk:
        tn = _tile(b.shape[2], tn)
    if out_blocks is not None:
        tn = _tile(out_blocks, tn)
    nk = K // tk
    grid = (M // tm, N // tn, nk)

    if mode == "nn":
        a_spec = pl.BlockSpec((tm, tk), lambda i, j, k: (i, k))
        if b_blk:
            rb = b.shape[2] // tn
            b_spec = pl.BlockSpec((None, tk, tn), lambda i, j, k: (j // rb, k, j % rb))
        else:
            b_spec = pl.BlockSpec((tk, tn), lambda i, j, k: (k, j))
    elif mode == "nt":
        if a_blk:
            a_spec = pl.BlockSpec((None, tm, tk), lambda i, j, k: (k, i, 0))
        else:
            a_spec = pl.BlockSpec((tm, tk), lambda i, j, k: (i, k))
        if b_blk:
            b_spec = pl.BlockSpec((None, tn, tk), lambda i, j, k: (k, j, 0))
        else:
            b_spec = pl.BlockSpec((tn, tk), lambda i, j, k: (j, k))
    else:
        a_spec = pl.BlockSpec((tk, tm), lambda i, j, k: (k, i))
        if b_blk:
            rb = b.shape[2] // tn
            b_spec = pl.BlockSpec((None, tk, tn), lambda i, j, k: (j // rb, k, j % rb))
        else:
            b_spec = pl.BlockSpec((tk, tn), lambda i, j, k: (k, j))

    if out_blocks is None:
        o_spec = pl.BlockSpec((tm, tn), lambda i, j, k: (i, j))
        o_shape = _sds((M, N), out_dtype)
    else:
        ro = out_blocks // tn
        o_spec = pl.BlockSpec((None, tm, tn), lambda i, j, k: (j // ro, i, j % ro))
        o_shape = _sds((N // out_blocks, M, out_blocks), out_dtype)

    direct = nk == 1 or out_dtype == F32

    def body(a_ref, b_ref, o_ref, *scratch):
        if nk == 1:
            o_ref[...] = lax.dot_general(a_ref[...], b_ref[...], dims, preferred_element_type=F32).astype(o_ref.dtype)
            return
        acc_ref = o_ref if direct else scratch[0]
        k = pl.program_id(2)

        @pl.when(k == 0)
        def _():
            acc_ref[...] = jnp.zeros(acc_ref.shape, F32)

        acc_ref[...] += lax.dot_general(a_ref[...], b_ref[...], dims, preferred_element_type=F32)
        if not direct:
            @pl.when(k == nk - 1)
            def _():
                o_ref[...] = acc_ref[...].astype(o_ref.dtype)

    scratch = [] if direct else [pltpu.VMEM((tm, tn), F32)]
    outs, landed = _call(body, [a, b], name=name, grid=grid, in_specs=[a_spec, b_spec], out_specs=[o_spec],
                         out_shape=[o_shape], scratch_shapes=scratch, sem=("parallel", "parallel", "arbitrary"), comm=comm)
    return outs[0] if comm is None else (outs[0], landed)


def _matmul_halves(a, b, *, mode, name, tm, tn=None, tk=None, comm=None):
    if mode == "nt":
        lo, hi = a
        M, kh = lo.shape
        G, N, kb = b.shape
        half = kh // kb
        tm, tn = _tile(M, tm), _tile(N, N if tn is None else tn)
        dims = (((1,), (1,)), ((), ()))

        def body(lo_ref, hi_ref, b_ref, o_ref):
            k = pl.program_id(2)

            @pl.when(k == 0)
            def _():
                o_ref[...] = jnp.zeros(o_ref.shape, F32)

            @pl.when(k < half)
            def _():
                o_ref[...] += lax.dot_general(lo_ref[...], b_ref[...], dims, preferred_element_type=F32)

            @pl.when(k >= half)
            def _():
                o_ref[...] += lax.dot_general(hi_ref[...], b_ref[...], dims, preferred_element_type=F32)

        outs, landed = _call(
            body, [lo, hi, b], name=name, grid=(M // tm, N // tn, G),
            in_specs=[pl.BlockSpec((tm, kb), lambda i, j, k: (i, jnp.minimum(k, half - 1))),
                      pl.BlockSpec((tm, kb), lambda i, j, k: (i, jnp.maximum(k - half, 0))),
                      pl.BlockSpec((None, tn, kb), lambda i, j, k: (k, j, 0))],
            out_specs=[pl.BlockSpec((tm, tn), lambda i, j, k: (i, j))], out_shape=[_sds((M, N), F32)],
            sem=("parallel", "parallel", "arbitrary"), comm=comm)
    else:
        lo, hi = b
        K, nh = lo.shape
        M = a.shape[1]
        n = tn
        half = nh // n
        tm, tk = _tile(M, tm), _tile(K, K if tk is None else tk)
        nk = K // tk
        dims = (((0,), (0,)), ((), ()))

        def body(a_ref, lo_ref, hi_ref, o_ref):
            j, k = pl.program_id(1), pl.program_id(2)

            @pl.when(k == 0)
            def _():
                o_ref[...] = jnp.zeros(o_ref.shape, F32)

            @pl.when(j < half)
            def _():
                o_ref[...] += lax.dot_general(a_ref[...], lo_ref[...], dims, preferred_element_type=F32)

            @pl.when(j >= half)
            def _():
                o_ref[...] += lax.dot_general(a_ref[...], hi_ref[...], dims, preferred_element_type=F32)

        outs, landed = _call(
            body, [a, lo, hi], name=name, grid=(M // tm, 2 * half, nk),
            in_specs=[pl.BlockSpec((tk, tm), lambda i, j, k: (k, i)),
                      pl.BlockSpec((tk, n), lambda i, j, k: (jnp.where(j < half, k, nk - 1), jnp.minimum(j, half - 1))),
                      pl.BlockSpec((tk, n), lambda i, j, k: (jnp.where(j >= half, k, 0), jnp.maximum(j - half, 0)))],
            out_specs=[pl.BlockSpec((None, tm, n), lambda i, j, k: (j, i, 0))],
            out_shape=[_sds((2 * half, M, n), F32)], sem=("parallel", "parallel", "arbitrary"), comm=comm)
    return outs[0] if comm is None else (outs[0], landed)


def _rms(x):
    return lax.rsqrt(jnp.mean(x * x, axis=-1, keepdims=True) + EPS)


def _rms_bwd(dy, x, g):
    r = _rms(x)
    n = x * r
    dn = dy * g
    dx = r * (dn - n * jnp.mean(dn * n, axis=-1, keepdims=True))
    return dx, dy * n


def _sigmoid(x):
    return 1.0 / (1.0 + jnp.exp(-x))


def _rope_rot(t):
    return pltpu.roll(t, HALF_ROPE, 1) - pltpu.roll(t, LANES - HALF_ROPE, 1)


def _lane(shape):
    return lax.broadcasted_iota(jnp.int32, shape, 1)


def _split3(x):
    hi = x.astype(BF16).astype(F32)
    r1 = x - hi
    mid = r1.astype(BF16).astype(F32)
    lo = (r1 - mid).astype(BF16).astype(F32)
    return hi, mid, lo


def _cumsum_rows(x, reverse):
    S = x.shape[0]
    bs = min(256, S)
    nb = S // bs
    r = lax.broadcasted_iota(jnp.int32, (bs, bs), 0)
    c = lax.broadcasted_iota(jnp.int32, (bs, bs), 1)
    tri = jnp.where((c >= r) if reverse else (c <= r), 1.0, 0.0).astype(BF16)
    edge = lax.broadcasted_iota(jnp.int32, (bs, x.shape[1]), 0) == (0 if reverse else bs - 1)
    carry = jnp.zeros((1, x.shape[1]), F32)
    outs = [None] * nb
    for bi in (range(nb - 1, -1, -1) if reverse else range(nb)):
        xb = x[bi * bs:(bi + 1) * bs, :]
        acc = carry
        for term in _split3(xb):
            acc = acc + jnp.dot(tri, term.astype(BF16), preferred_element_type=F32)
        outs[bi] = acc
        carry = jnp.sum(jnp.where(edge, acc, 0.0), axis=0, keepdims=True)
    return jnp.concatenate(outs, axis=0) if nb > 1 else outs[0]


def _gelu_parts(x):
    c0 = math.sqrt(2.0 / math.pi)
    inner = c0 * (x + 0.044715 * (x * x * x))
    t = jnp.tanh(inner)
    g = 0.5 * x * (1.0 + t)
    dg = 0.5 * (1.0 + t) + 0.5 * x * (1.0 - t * t) * (c0 * (1.0 + 3.0 * 0.044715 * (x * x)))
    return g, dg


def _accumulate(ref, value, first):
    @pl.when(first)
    def _():
        ref[...] = value

    @pl.when(jnp.logical_not(first))
    def _():
        ref[...] += value


def _cast_bf16(w, name):
    R, C = w.shape
    tr = _tile(R, 512, 16)

    def body(w_ref, o_ref):
        o_ref[...] = w_ref[...].astype(BF16)

    blk = pl.BlockSpec((tr, C), lambda i: (i, 0))
    return _call(body, [w], name=name, grid=(R // tr,), in_specs=[blk], out_specs=[blk],
                 out_shape=[_sds((R, C), BF16)], sem=("parallel",))[0][0]


def _concat_cols(parts, name):
    T = parts[0].shape[-2] if parts[0].ndim == 3 else parts[0].shape[0]
    widths = [p.shape[0] * LANES if p.ndim == 3 else p.shape[1] for p in parts]
    tm = _tile(T, ROW_TILE, 16)

    def body(*refs):
        o_ref = refs[-1]
        off = 0
        for p_ref, p, w in zip(refs[:-1], parts, widths):
            if p.ndim == 3:
                for hd in range(p.shape[0]):
                    o_ref[:, off + hd * LANES:off + (hd + 1) * LANES] = p_ref[hd].astype(BF16)
            else:
                o_ref[:, off:off + w] = p_ref[...].astype(BF16)
            off += w

    def spec(p, w):
        if p.ndim == 3:
            return pl.BlockSpec((p.shape[0], tm, LANES), lambda i: (0, i, 0))
        return pl.BlockSpec((tm, w), lambda i: (i, 0))

    return _call(body, parts, name=name, grid=(T // tm,),
                 in_specs=[spec(p, w) for p, w in zip(parts, widths)],
                 out_specs=[pl.BlockSpec((tm, sum(widths)), lambda i: (i, 0))],
                 out_shape=[_sds((T, sum(widths)), BF16)], sem=("parallel",))[0][0]


def _prenorm(x, g, comm=None):
    T, D = x.shape
    tm = _tile(T, ROW_TILE, 16)

    def body(x_ref, g_ref, h_ref):
        xv = x_ref[...]
        h_ref[...] = (xv * _rms(xv) * g_ref[...]).astype(BF16)

    row = pl.BlockSpec((tm, D), lambda i: (i, 0))
    (h,), landed = _call(body, [x, g], name="prenorm", grid=(T // tm,),
                         in_specs=[row, pl.BlockSpec((1, D), lambda i: (0, 0))], out_specs=[row],
                         out_shape=[_sds((T, D), BF16)], sem=("parallel",), comm=comm)
    return h, landed


def _split_prep(proj, pos, invf, gq, gkv, bfor, lay, comm=None):
    T = proj.shape[0]
    tm = _tile(T, ROW_TILE, 16)

    def body(q_ref, kv_ref, kpe_ref, fl_ref, pos_ref, invf_ref, gq_ref, gkv_ref, bf_ref,
             qn_ref, kvn_ref, kper_ref, logf_ref, cos_ref, sin_ref):
        ql = q_ref[...]
        qn_ref[...] = (ql * _rms(ql) * gq_ref[...]).astype(BF16)
        kl = kv_ref[...]
        kvn_ref[...] = (kl * _rms(kl) * gkv_ref[...]).astype(BF16)
        ang = pos_ref[...].astype(F32) * invf_ref[...]
        valid = _lane(ang.shape) < ROPE
        cs = jnp.where(valid, jnp.cos(ang), 0.0)
        sn = jnp.where(valid, jnp.sin(ang), 0.0)
        cos_ref[...] = cs
        sin_ref[...] = sn
        kp = jnp.where(valid, kpe_ref[...], 0.0)
        kper_ref[...] = (kp * cs + _rope_rot(kp) * sn).astype(BF16)
        z = fl_ref[...] + bf_ref[...]
        logf_ref[...] = jnp.minimum(z, 0.0) - jnp.log(1.0 + jnp.exp(-jnp.abs(z)))

    def col(width, off):
        return pl.BlockSpec((tm, width), lambda i: (i, off // width))

    def vec(width):
        return pl.BlockSpec((1, width), lambda i: (0, 0))

    def out(width):
        return pl.BlockSpec((tm, width), lambda i: (i, 0))

    return _call(
        body, [proj, proj, proj, proj, pos, invf, gq, gkv, bfor], name="split_prep", grid=(T // tm,),
        in_specs=[col(Q_LORA, lay["q"]), col(KV_LORA, lay["kv"]), col(LANES, lay["kpe"]), col(LANES, lay["fl"]),
                  pl.BlockSpec((tm, 1), lambda i: (i, 0)), vec(LANES), vec(Q_LORA), vec(KV_LORA), vec(LANES)],
        out_specs=[out(Q_LORA), out(KV_LORA), out(LANES), out(LANES), out(LANES), out(LANES)],
        out_shape=[_sds((T, Q_LORA), BF16), _sds((T, KV_LORA), BF16), _sds((T, LANES), BF16),
                   _sds((T, LANES), F32), _sds((T, LANES), F32), _sds((T, LANES), F32)],
        sem=("parallel",), comm=comm)


def _mla_prep(qraw, kvraw, kper, cosT, sinT, comm=None):
    H, T, _ = qraw.shape
    tm = _tile(T, HEAD_ROW_TILE, 16)

    def body(q_ref, kv_ref, kpe_ref, cos_ref, sin_ref, qo_ref, ko_ref, vo_ref):
        q = q_ref[...]
        pe = q[:, NOPE:]
        pe = jnp.where(_lane(pe.shape) < ROPE, pe, 0.0)
        qo_ref[:, :NOPE] = q[:, :NOPE].astype(BF16)
        qo_ref[:, NOPE:] = (pe * cos_ref[...] + _rope_rot(pe) * sin_ref[...]).astype(BF16)
        kv = kv_ref[...]
        ko_ref[:, :NOPE] = kv[:, :NOPE].astype(BF16)
        ko_ref[:, NOPE:] = kpe_ref[...]
        vo_ref[...] = kv[:, NOPE:].astype(BF16)

    head = pl.BlockSpec((None, tm, ATT_DK), lambda h, i: (h, i, 0))
    tok = pl.BlockSpec((tm, LANES), lambda h, i: (i, 0))
    return _call(
        body, [qraw, kvraw, kper, cosT, sinT], name="mla_prep", grid=(H, T // tm),
        in_specs=[head, head, tok, tok, tok],
        out_specs=[head, head, pl.BlockSpec((None, tm, VDIM), lambda h, i: (h, i, 0))],
        out_shape=[_sds((H, T, ATT_DK), BF16), _sds((H, T, ATT_DK), BF16), _sds((H, T, VDIM), BF16)],
        sem=("parallel", "parallel"), comm=comm)


def _fox_cumsum(logf, B, S, inv_scale):
    T = logf.shape[0]

    def body(l_ref, c_ref):
        c_ref[...] = _cumsum_rows(l_ref[...], reverse=False) * inv_scale

    seq = pl.BlockSpec((S, LANES), lambda b: (b, 0))
    return _call(body, [logf], name="fox_cumsum", grid=(B,), in_specs=[seq], out_specs=[seq],
                 out_shape=[_sds((T, LANES), F32)], sem=("parallel",))[0][0]


def _fox_prep(proj, cs, lay, comm=None):
    T = proj.shape[0]
    tm = _tile(T, HEAD_ROW_TILE, 16)

    def body(q_ref, k_ref, v_ref, cs_ref, qo_ref, ko_ref, vo_ref):
        h = pl.program_id(0)
        cv = cs_ref[...]
        lane = _lane(cv.shape)
        ccol = jnp.sum(jnp.where(lane == h, cv, 0.0), axis=1, keepdims=True)
        hi, mid, lo = _split3(ccol)
        one = jnp.where(lane < 6, 1.0, 0.0)
        augq = jnp.where(lane == 0, hi, jnp.where(lane == 1, mid, jnp.where(lane == 2, lo, one)))
        augk = jnp.where(lane < 3, 1.0, jnp.where(lane == 3, -hi, jnp.where(lane == 4, -mid, jnp.where(lane == 5, -lo, 0.0))))
        qo_ref[:, :FOX_DIM] = q_ref[...].astype(BF16)
        qo_ref[:, FOX_DIM:] = augq.astype(BF16)
        ko_ref[:, :FOX_DIM] = k_ref[...].astype(BF16)
        ko_ref[:, FOX_DIM:] = augk.astype(BF16)
        vo_ref[...] = v_ref[...].astype(BF16)

    def col(off):
        return pl.BlockSpec((tm, FOX_DIM), lambda h, i: (i, off // FOX_DIM + h))

    head = pl.BlockSpec((None, tm, ATT_DK), lambda h, i: (h, i, 0))
    return _call(
        body, [proj, proj, proj, cs], name="fox_prep", grid=(HEADS, T // tm),
        in_specs=[col(lay["fq"]), col(lay["fk"]), col(lay["fv"]), pl.BlockSpec((tm, LANES), lambda h, i: (i, 0))],
        out_specs=[head, head, pl.BlockSpec((None, tm, VDIM), lambda h, i: (h, i, 0))],
        out_shape=[_sds((HEADS, T, ATT_DK), BF16), _sds((HEADS, T, ATT_DK), BF16), _sds((HEADS, T, VDIM), BF16)],
        sem=("parallel", "parallel"), comm=comm)


def _visible(tq, tk, unit):
    r = lax.broadcasted_iota(jnp.int32, (tq, tk), 0)
    c = lax.broadcasted_iota(jnp.int32, (tq, tk), 1)
    sh = int(math.log2(unit))
    return lax.shift_right_logical(c, sh) <= lax.shift_right_logical(r, sh)


def _attn_fwd(streams, *, B, S, name, comm=None):
    n = len(streams)
    H, T, DK = streams[0][0].shape
    DV = streams[0][2].shape[2]
    tq = _tile(S, ATT_TILE)
    nq = S // tq
    sub = min(ATT_SUB, tq)
    NT = (((1,), (1,)), ((), ()))

    def body(*refs):
        ins, outs, (m_sc, acc_sc) = refs[:3 * n], refs[3 * n:5 * n], refs[5 * n:]
        i, j = pl.program_id(1), pl.program_id(2)

        @pl.when(j == 0)
        def _():
            m_sc[...] = jnp.full(m_sc.shape, NEG_INF, F32)
            acc_sc[...] = jnp.zeros(acc_sc.shape, F32)

        def step(diagonal):
            work = [(t, r) for r in range(tq // sub) for t in range(n)]

            def scores(t, r):
                q_ref, k_ref, _ = ins[3 * t:3 * t + 3]
                kc = (r + 1) * sub if diagonal else tq
                s = lax.dot_general(q_ref[r * sub:(r + 1) * sub, :], k_ref[0:kc, :], NT, preferred_element_type=F32)
                return s * (streams[t][4] * LOG2_E)

            ahead = [scores(*work[w]) for w in range(min(ATT_AHEAD, len(work)))]
            for w, (t, r) in enumerate(work):
                s = ahead.pop(0)
                if w + ATT_AHEAD < len(work):
                    ahead.append(scores(*work[w + ATT_AHEAD]))
                v_ref = ins[3 * t + 2]
                kc = s.shape[1]
                rows = slice(r * sub, (r + 1) * sub)
                if diagonal:
                    own = jnp.where(_visible(sub, sub, streams[t][3]), s[:, kc - sub:], NEG_INF)
                    s = own if kc == sub else jnp.concatenate([s[:, :kc - sub], own], axis=1)
                m_prev = m_sc[t, rows, :]
                mx = s[:, 0:LANES]
                for g in range(1, kc // LANES):
                    mx = jnp.maximum(mx, s[:, g * LANES:(g + 1) * LANES])
                m_new = jnp.maximum(m_prev, jnp.max(mx, axis=1, keepdims=True))
                alpha = jnp.exp2(m_prev - m_new)
                p = jnp.exp2(s - jnp.tile(m_new, (1, kc // LANES))).astype(BF16)
                v_aug = jnp.concatenate([v_ref[0:kc, :], jnp.ones((kc, LANES), BF16)], axis=1)
                acc_sc[t, rows, :] = jnp.tile(alpha, (1, 2)) * acc_sc[t, rows, :] + jnp.dot(
                    p, v_aug, preferred_element_type=F32)
                m_sc[t, rows, :] = m_new

        @pl.when(j < i)
        def _():
            step(False)

        @pl.when(j == i)
        def _():
            step(True)
            for t in range(n):
                o_ref, lse_ref = outs[2 * t:2 * t + 2]
                l = acc_sc[t, :, DV:]
                o_ref[...] = (acc_sc[t, :, :DV] / l).astype(BF16)
                lse_ref[...] = m_sc[t] + jnp.log2(l)

    def qmap(g, i, j):
        return (g % H, (g // H) * nq + i, 0)

    def kmap(g, i, j):
        return (g % H, (g // H) * nq + jnp.minimum(j, i), 0)

    args = [a for st in streams for a in st[:3]]
    outs, landed = _call(
        body, args, name=name, grid=(B * H, nq, nq),
        in_specs=[pl.BlockSpec((None, tq, DK), qmap), pl.BlockSpec((None, tq, DK), kmap),
                  pl.BlockSpec((None, tq, DV), kmap)] * n,
        out_specs=[pl.BlockSpec((tq, DV), lambda g, i, j: ((g // H) * nq + i, g % H)),
                   pl.BlockSpec((None, tq, LANES), qmap)] * n,
        out_shape=[_sds((T, H * DV), BF16), _sds((H, T, LANES), F32)] * n,
        scratch_shapes=[pltpu.VMEM((n, tq, LANES), F32), pltpu.VMEM((n, tq, DV + LANES), F32)],
        sem=("parallel", "parallel", "arbitrary"), comm=comm)
    return [(outs[2 * t], outs[2 * t + 1]) for t in range(n)], landed


def _attn_bwd(streams, *, B, S, name, comm=None):
    n = len(streams)
    H, T, DK = streams[0][0].shape
    DV = streams[0][2].shape[2]
    tq = _tile(S, ATT_TILE)
    nq = S // tq
    sub = min(ATT_SUB, tq)
    NT = (((1,), (1,)), ((), ()))
    TN = (((0,), (0,)), ((), ()))

    def body(*refs):
        ins, outs = refs[:6 * n], refs[6 * n:]
        j, i = pl.program_id(1), pl.program_id(2)

        @pl.when(jnp.logical_and(j == 0, i == 0))
        def _():
            for t in range(n):
                outs[3 * t][...] = jnp.zeros(outs[3 * t].shape, F32)

        @pl.when(i == 0)
        def _():
            for t in range(n):
                outs[3 * t + 1][...] = jnp.zeros(outs[3 * t + 1].shape, F32)
                outs[3 * t + 2][...] = jnp.zeros(outs[3 * t + 2].shape, F32)

        def step(diagonal):
            work = [(t, r) for r in range(tq // sub) for t in range(n)]

            def kcols(r):
                return (r + 1) * sub if diagonal else tq

            def scores(t, r):
                q_ref, k_ref, v_ref, _, do_ref, _ = ins[6 * t:6 * t + 6]
                rows, kc = slice(r * sub, (r + 1) * sub), kcols(r)
                s = lax.dot_general(q_ref[rows, :], k_ref[0:kc, :], NT, preferred_element_type=F32)
                dp = lax.dot_general(do_ref[rows, :], v_ref[0:kc, :], NT, preferred_element_type=F32)
                return s * (streams[t][7] * LOG2_E), dp

            def probs(t, r, s, dp):
                _, _, _, o_ref, do_ref, lse_ref = ins[6 * t:6 * t + 6]
                rows, kc = slice(r * sub, (r + 1) * sub), kcols(r)
                if diagonal:
                    own = jnp.where(_visible(sub, sub, streams[t][6]), s[:, kc - sub:], NEG_INF)
                    s = own if kc == sub else jnp.concatenate([s[:, :kc - sub], own], axis=1)
                p = jnp.exp2(s - jnp.tile(lse_ref[rows, :], (1, kc // LANES)))
                delta = jnp.sum(do_ref[rows, :].astype(F32) * o_ref[rows, :].astype(F32), axis=1, keepdims=True)
                return p.astype(BF16), (p * (dp - delta) * streams[t][7]).astype(BF16)

            def grads(t, r, p, ds):
                q_ref, k_ref, _, _, do_ref, _ = ins[6 * t:6 * t + 6]
                dq_ref, dk_ref, dv_ref = outs[3 * t:3 * t + 3]
                rows, kc = slice(r * sub, (r + 1) * sub), kcols(r)
                dv_ref[0:kc, :] += lax.dot_general(p, do_ref[rows, :], TN, preferred_element_type=F32)
                dk_ref[0:kc, :] += lax.dot_general(ds, q_ref[rows, :], TN, preferred_element_type=F32)
                qrows = pl.ds(pl.multiple_of(i * tq + r * sub, sub), sub)
                dq_ref[qrows, :] += jnp.dot(ds, k_ref[0:kc, :], preferred_element_type=F32)

            nw = len(work)
            sc = {w: scores(*work[w]) for w in range(min(2, nw))}
            pr = {0: probs(*work[0], *sc.pop(0))}
            for w in range(nw):
                if w + 2 < nw:
                    sc[w + 2] = scores(*work[w + 2])
                if w + 1 < nw:
                    pr[w + 1] = probs(*work[w + 1], *sc.pop(w + 1))
                grads(*work[w], *pr.pop(w))

        @pl.when(i > j)
        def _():
            step(False)

        @pl.when(i == j)
        def _():
            step(True)

    def qmap(g, j, i):
        return (g % H, (g // H) * nq + jnp.maximum(i, j), 0)

    def kmap(g, j, i):
        return (g % H, (g // H) * nq + j, 0)

    def omap(g, j, i):
        return ((g // H) * nq + jnp.maximum(i, j), g % H)

    args = [a for st in streams for a in st[:6]]
    outs, landed = _call(
        body, args, name=name, grid=(B * H, nq, nq),
        in_specs=[pl.BlockSpec((None, tq, DK), qmap), pl.BlockSpec((None, tq, DK), kmap),
                  pl.BlockSpec((None, tq, DV), kmap), pl.BlockSpec((tq, DV), omap), pl.BlockSpec((tq, DV), omap),
                  pl.BlockSpec((None, tq, LANES), qmap)] * n,
        out_specs=[pl.BlockSpec((None, S, DK), lambda g, j, i: (g % H, g // H, 0)),
                   pl.BlockSpec((None, tq, DK), kmap), pl.BlockSpec((None, tq, DV), kmap)] * n,
        out_shape=[_sds((H, T, DK), F32), _sds((H, T, DK), F32), _sds((H, T, DV), F32)] * n,
        sem=("parallel", "arbitrary", "arbitrary"), comm=comm)
    return [tuple(outs[3 * t:3 * t + 3]) for t in range(n)], landed


def _branch_merge(o_m, o_f, w_m, w_f, proj, bgate, lay, D, comm=None):
    T, K = o_m.shape
    tm = _tile(T, 2 * ROW_TILE, 16)
    tn = _tile(D, 512)

    def body(om_ref, of_ref, wm_ref, wf_ref, gm_ref, gf_ref, bm_ref, bf_ref, am_ref, af_ref, o_ref):
        am = jnp.dot(om_ref[...], wm_ref[...], preferred_element_type=F32)
        af = jnp.dot(of_ref[...], wf_ref[...], preferred_element_type=F32)
        am_ref[...] = am
        af_ref[...] = af
        sm = _sigmoid(gm_ref[...] + bm_ref[...])
        sf = _sigmoid(gf_ref[...] + bf_ref[...])
        o_ref[...] = (sm * am + sf * af).astype(BF16)

    og = lay["g"] // tn
    blk = pl.BlockSpec((tm, tn), lambda i, j: (i, j))
    lhs = pl.BlockSpec((tm, K), lambda i, j: (i, 0))
    rhs = pl.BlockSpec((K, tn), lambda i, j: (0, j))
    return _call(
        body, [o_m, o_f, w_m, w_f, proj, proj, bgate, bgate], name="mm_branch_merge", grid=(T // tm, D // tn),
        in_specs=[lhs, lhs, rhs, rhs, pl.BlockSpec((tm, tn), lambda i, j: (i, og + j)),
                  pl.BlockSpec((tm, tn), lambda i, j: (i, og + D // tn + j)),
                  pl.BlockSpec((1, tn), lambda i, j: (0, j)), pl.BlockSpec((1, tn), lambda i, j: (0, D // tn + j))],
        out_specs=[blk, blk, blk], out_shape=[_sds((T, D), F32), _sds((T, D), F32), _sds((T, D), BF16)],
        sem=("parallel", "parallel"), comm=comm)


def _out_mid(merged, w_out, x, g_pm, g_ffn):
    T, D = x.shape
    tm = _tile(T, 2 * ROW_TILE, 16)

    def body(a_ref, w_ref, x_ref, gp_ref, gf_ref, y_ref, x1_ref, h2_ref):
        y = jnp.dot(a_ref[...], w_ref[...], preferred_element_type=F32)
        y_ref[...] = y
        x1 = x_ref[...] + y * _rms(y) * gp_ref[...]
        x1_ref[...] = x1
        h2_ref[...] = (x1 * _rms(x1) * gf_ref[...]).astype(BF16)

    row = pl.BlockSpec((tm, D), lambda i: (i, 0))
    vec = pl.BlockSpec((1, D), lambda i: (0, 0))
    return _call(body, [merged, w_out, x, g_pm, g_ffn], name="mm_out_mid", grid=(T // tm,),
                 in_specs=[row, pl.BlockSpec((D, D), lambda i: (0, 0)), row, vec, vec], out_specs=[row, row, row],
                 out_shape=[_sds((T, D), F32), _sds((T, D), F32), _sds((T, D), BF16)], sem=("parallel",))[0]


def _conv3(u, w_ref, bias):
    row = lax.broadcasted_iota(jnp.int32, u.shape, 0)
    u1 = jnp.where(row >= 1, pltpu.roll(u, 1, 0), 0.0)
    u2 = jnp.where(row >= 2, pltpu.roll(u, 2, 0), 0.0)
    return w_ref[0:1, :] * u2 + w_ref[1:2, :] * u1 + w_ref[2:3, :] * u + bias, u1, u2


def _convffn_fwd(u, cw, cb, B, S, F, comm=None):
    T = u.shape[0]
    tn = _tile(F, 256)
    nf = F // tn

    def body(ug_ref, uv_ref, wg_ref, wv_ref, bg_ref, bv_ref, a_ref):
        g, _, _ = _conv3(ug_ref[...], wg_ref, bg_ref[...])
        val, _, _ = _conv3(uv_ref[...], wv_ref, bv_ref[...])
        a_ref[...] = (_gelu_parts(g)[0] * val).astype(BF16)

    def seq(off):
        return pl.BlockSpec((S, tn), lambda b, j: (b, off + j))

    def par(rows, off):
        return pl.BlockSpec((rows, tn), lambda b, j: (0, off + j))

    return _call(body, [u, u, cw, cw, cb, cb], name="convffn_fwd", grid=(B, nf),
                 in_specs=[seq(0), seq(nf), par(3, 0), par(3, nf), par(1, 0), par(1, nf)],
                 out_specs=[seq(0)], out_shape=[_sds((T, F), BF16)], sem=("parallel", "parallel"), comm=comm)


def _convffn_bwd(u, dact, cw, cb, B, S, F, comm=None):
    T = u.shape[0]
    tn = _tile(F, 256)
    nf = F // tn

    def body(ug_ref, uv_ref, da_ref, wg_ref, wv_ref, bg_ref, bv_ref, dug_ref, duv_ref, dpg_ref, dpv_ref):
        b = pl.program_id(1)
        ug, uv, da = ug_ref[...], uv_ref[...], da_ref[...]
        g, ug1, ug2 = _conv3(ug, wg_ref, bg_ref[...])
        val, uv1, uv2 = _conv3(uv, wv_ref, bv_ref[...])
        gel, dgel = _gelu_parts(g)
        dg = da * val * dgel
        dval = da * gel
        row = lax.broadcasted_iota(jnp.int32, ug.shape, 0)

        def back(d, w_ref):
            d1 = jnp.where(row < S - 1, pltpu.roll(d, S - 1, 0), 0.0)
            d2 = jnp.where(row < S - 2, pltpu.roll(d, S - 2, 0), 0.0)
            return w_ref[2:3, :] * d + w_ref[1:2, :] * d1 + w_ref[0:1, :] * d2

        dug_ref[...] = back(dg, wg_ref).astype(BF16)
        duv_ref[...] = back(dval, wv_ref).astype(BF16)

        def sums(d, u0, u1, u2):
            r8 = lax.broadcasted_iota(jnp.int32, (8, d.shape[1]), 0)
            out = jnp.zeros((8, d.shape[1]), F32)
            for k, t in enumerate((d * u2, d * u1, d * u0, d)):
                out = jnp.where(r8 == k, jnp.sum(t, axis=0, keepdims=True), out)
            return out

        _accumulate(dpg_ref, sums(dg, ug, ug1, ug2), b == 0)
        _accumulate(dpv_ref, sums(dval, uv, uv1, uv2), b == 0)

    def seq(off):
        return pl.BlockSpec((S, tn), lambda j, b: (b, off + j))

    def par(rows, off):
        return pl.BlockSpec((rows, tn), lambda j, b: (0, off + j))

    outs, landed = _call(
        body, [u, u, dact, cw, cw, cb, cb], name="convffn_bwd", grid=(nf, B),
        in_specs=[seq(0), seq(nf), seq(0), par(3, 0), par(3, nf), par(1, 0), par(1, nf)],
        out_specs=[seq(0), seq(0), par(8, 0), par(8, 0)],
        out_shape=[_sds((T, F), BF16), _sds((T, F), BF16), _sds((8, F), F32), _sds((8, F), F32)],
        sem=("parallel", "arbitrary"), comm=comm)
    return outs, landed


def _tail(ff, x1, tgt, g):
    T, D = ff.shape
    tm = _tile(T, ROW_TILE, 16)

    def body(ff_ref, x1_ref, t_ref, g_ref, dy_ref, dff_ref, loss_ref, dg_ref):
        i = pl.program_id(0)
        f = ff_ref[...]
        gv = g_ref[...]
        r = _rms(f)
        n = f * r
        e = (x1_ref[...] + n * gv) - t_ref[...]
        dy = e * (1.0 / D)
        dy_ref[...] = dy
        dn = dy * gv
        dff_ref[...] = (r * (dn - n * jnp.mean(dn * n, axis=-1, keepdims=True))).astype(BF16)
        part = 0.5 * jnp.sum(jnp.mean(e * e, axis=-1, keepdims=True), axis=0, keepdims=True)
        _accumulate(loss_ref, jnp.broadcast_to(part, loss_ref.shape), i == 0)
        _accumulate(dg_ref, jnp.sum(dy * n, axis=0, keepdims=True), i == 0)

    row = pl.BlockSpec((tm, D), lambda i: (i, 0))
    vec = pl.BlockSpec((1, D), lambda i: (0, 0))
    return _call(body, [ff, x1, tgt, g], name="tail", grid=(T // tm,), in_specs=[row, row, row, vec],
                 out_specs=[row, row, pl.BlockSpec((8, LANES), lambda i: (0, 0)), vec],
                 out_shape=[_sds((T, D), F32), _sds((T, D), BF16), _sds((8, LANES), F32), _sds((1, D), F32)],
                 sem=("arbitrary",))[0]


def _mid_bwd(dy, dh2, x1, y1, g_ffn, g_pm, comm=None):
    T, D = dy.shape
    tm = _tile(T, ROW_TILE, 16)

    def body(dy_ref, dh_ref, x1_ref, y1_ref, gf_ref, gp_ref, dx1_ref, dy1_ref, dgf_ref, dgp_ref):
        i = pl.program_id(0)
        dh = dh_ref[...]
        d2, dgf = _rms_bwd(dh, x1_ref[...], gf_ref[...])
        dx1 = dy_ref[...] + d2
        dx1_ref[...] = dx1
        d1, dgp = _rms_bwd(dx1, y1_ref[...], gp_ref[...])
        dy1_ref[...] = d1.astype(BF16)
        _accumulate(dgf_ref, jnp.sum(dgf, axis=0, keepdims=True), i == 0)
        _accumulate(dgp_ref, jnp.sum(dgp, axis=0, keepdims=True), i == 0)

    row = pl.BlockSpec((tm, D), lambda i: (i, 0))
    vec = pl.BlockSpec((1, D), lambda i: (0, 0))
    return _call(body, [dy, dh2, x1, y1, g_ffn, g_pm], name="mid_bwd", grid=(T // tm,),
                 in_specs=[row, row, row, row, vec, vec], out_specs=[row, row, vec, vec],
                 out_shape=[_sds((T, D), F32), _sds((T, D), BF16), _sds((1, D), F32), _sds((1, D), F32)],
                 sem=("arbitrary",), comm=comm)


def _gate_bwd(dy1, w_out, am, af, proj, bgate, lay, D, comm=None):
    T = dy1.shape[0]
    tm = _tile(T, 2 * ROW_TILE, 16)
    tn = _tile(D, 512)
    NT = (((1,), (1,)), ((), ()))

    def body(dy_ref, w_ref, am_ref, af_ref, gm_ref, gf_ref, bm_ref, bf_ref,
             dam_ref, daf_ref, dgm_ref, dgf_ref, dbm_ref, dbf_ref):
        i = pl.program_id(1)
        d = lax.dot_general(dy_ref[...], w_ref[...], NT, preferred_element_type=F32)
        sm = _sigmoid(gm_ref[...] + bm_ref[...])
        sf = _sigmoid(gf_ref[...] + bf_ref[...])
        dam_ref[...] = (d * sm).astype(BF16)
        daf_ref[...] = (d * sf).astype(BF16)
        dgm = d * am_ref[...] * (sm * (1.0 - sm))
        dgf = d * af_ref[...] * (sf * (1.0 - sf))
        dgm_ref[...] = dgm.astype(BF16)
        dgf_ref[...] = dgf.astype(BF16)
        _accumulate(dbm_ref, jnp.sum(dgm, axis=0, keepdims=True), i == 0)
        _accumulate(dbf_ref, jnp.sum(dgf, axis=0, keepdims=True), i == 0)

    og = lay["g"] // tn
    blk = pl.BlockSpec((tm, tn), lambda j, i: (i, j))
    vec = pl.BlockSpec((1, tn), lambda j, i: (0, j))
    return _call(
        body, [dy1, w_out, am, af, proj, proj, bgate, bgate], name="mm_dmerged_gate_bwd", grid=(D // tn, T // tm),
        in_specs=[pl.BlockSpec((tm, D), lambda j, i: (i, 0)), pl.BlockSpec((tn, D), lambda j, i: (j, 0)),
                  blk, blk, pl.BlockSpec((tm, tn), lambda j, i: (i, og + j)),
                  pl.BlockSpec((tm, tn), lambda j, i: (i, og + D // tn + j)),
                  vec, pl.BlockSpec((1, tn), lambda j, i: (0, D // tn + j))],
        out_specs=[blk, blk, blk, blk, vec, vec],
        out_shape=[_sds((T, D), BF16)] * 4 + [_sds((1, D), F32)] * 2, sem=("parallel", "arbitrary"), comm=comm)


def _mla_bwd_prep(dq, dk, dv, cosT, sinT, comm=None):
    H, T, _ = dq.shape
    tm = _tile(T, HEAD_ROW_TILE, 16)

    def body(dq_ref, dk_ref, dv_ref, cos_ref, sin_ref, dqr_ref, dkv_ref, dkpe_ref):
        h = pl.program_id(1)
        cs, sn = cos_ref[...], sin_ref[...]
        valid = _lane(cs.shape) < ROPE

        def unrope(d):
            d = jnp.where(valid, d, 0.0)
            return d * cs - _rope_rot(d) * sn

        dqv = dq_ref[...]
        dqr_ref[:, :NOPE] = dqv[:, :NOPE].astype(BF16)
        dqr_ref[:, NOPE:] = unrope(dqv[:, NOPE:]).astype(BF16)
        dkv_ = dk_ref[...]
        dkv_ref[:, :NOPE] = dkv_[:, :NOPE].astype(BF16)
        dkv_ref[:, NOPE:] = dv_ref[...].astype(BF16)
        _accumulate(dkpe_ref, unrope(dkv_[:, NOPE:]), h == 0)

    head = pl.BlockSpec((None, tm, ATT_DK), lambda i, h: (h, i, 0))
    tok = pl.BlockSpec((tm, LANES), lambda i, h: (i, 0))
    return _call(
        body, [dq, dk, dv, cosT, sinT], name="mla_bwd_prep", grid=(T // tm, H),
        in_specs=[head, head, pl.BlockSpec((None, tm, VDIM), lambda i, h: (h, i, 0)), tok, tok],
        out_specs=[head, head, tok],
        out_shape=[_sds((H, T, ATT_DK), BF16), _sds((H, T, ATT_DK), BF16), _sds((T, LANES), F32)],
        sem=("parallel", "arbitrary"), comm=comm)


def _fox_bwd_prep(dq, dk, proj, bfor, lay, B, S, inv_scale):
    H, T, _ = dq.shape

    def body(dq_ref, dk_ref, fl_ref, bf_ref, dfl_ref, dbf_ref, dc_sc):
        b, h = pl.program_id(0), pl.program_id(1)
        lane = _lane(dc_sc.shape)
        col = jnp.sum(jnp.where(lane == 0, dq_ref[...], 0.0) - jnp.where(lane == 3, dk_ref[...], 0.0),
                      axis=1, keepdims=True)

        @pl.when(h == 0)
        def _():
            dc_sc[...] = jnp.zeros(dc_sc.shape, F32)

        dc_sc[...] = jnp.where(lane == h, col, dc_sc[...])

        @pl.when(h == H - 1)
        def _():
            dlogf = _cumsum_rows(dc_sc[...] * inv_scale, reverse=True)
            z = fl_ref[...] + bf_ref[...]
            dz = jnp.where(lane < H, dlogf * (1.0 / (1.0 + jnp.exp(z))), 0.0)
            dfl_ref[...] = dz
            _accumulate(dbf_ref, jnp.sum(dz, axis=0, keepdims=True), b == 0)

    aug = pl.BlockSpec((None, S, LANES), lambda b, h: (h, b, 1))
    seq = pl.BlockSpec((S, LANES), lambda b, h: (b, 0))
    vec = pl.BlockSpec((1, LANES), lambda b, h: (0, 0))
    return _call(
        body, [dq, dk, proj, bfor], name="fox_bwd_prep", grid=(B, H),
        in_specs=[aug, aug, pl.BlockSpec((S, LANES), lambda b, h: (b, lay["fl"] // LANES)), vec],
        out_specs=[seq, vec], out_shape=[_sds((T, LANES), F32), _sds((1, LANES), F32)],
        scratch_shapes=[pltpu.VMEM((S, LANES), F32)], sem=("arbitrary", "arbitrary"))[0]


def _lat_bwd(dqn, dkvn, proj, gq, gkv, lay):
    T = dqn.shape[0]
    tm = _tile(T, ROW_TILE, 16)

    def body(dq_ref, dkv_ref, q_ref, kv_ref, gq_ref, gkv_ref, dql_ref, dkl_ref, dgq_ref, dgkv_ref):
        i = pl.program_id(0)
        dql, dgq = _rms_bwd(dq_ref[...], q_ref[...], gq_ref[...])
        dkl, dgkv = _rms_bwd(dkv_ref[...], kv_ref[...], gkv_ref[...])
        dql_ref[...] = dql.astype(BF16)
        dkl_ref[...] = dkl.astype(BF16)
        _accumulate(dgq_ref, jnp.sum(dgq, axis=0, keepdims=True), i == 0)
        _accumulate(dgkv_ref, jnp.sum(dgkv, axis=0, keepdims=True), i == 0)

    def blk(width, off=0):
        return pl.BlockSpec((tm, width), lambda i: (i, off // width))

    def vec(width):
        return pl.BlockSpec((1, width), lambda i: (0, 0))

    return _call(
        body, [dqn, dkvn, proj, proj, gq, gkv], name="lat_bwd", grid=(T // tm,),
        in_specs=[blk(Q_LORA), blk(KV_LORA), blk(Q_LORA, lay["q"]), blk(KV_LORA, lay["kv"]), vec(Q_LORA), vec(KV_LORA)],
        out_specs=[blk(Q_LORA), blk(KV_LORA), vec(Q_LORA), vec(KV_LORA)],
        out_shape=[_sds((T, Q_LORA), BF16), _sds((T, KV_LORA), BF16), _sds((1, Q_LORA), F32), _sds((1, KV_LORA), F32)],
        sem=("arbitrary",))[0]


def _dh_final(dproj, w_perm, dx1, x, g, comm=None):
    T, D = x.shape
    K = dproj.shape[1]
    tm = _tile(T, 2 * ROW_TILE, 16)
    tk = _tile(K, 1024)
    nk = K // tk
    NT = (((1,), (1,)), ((), ()))

    def body(a_ref, b_ref, dx1_ref, x_ref, g_ref, dx_ref, dg_ref):
        i, k = pl.program_id(0), pl.program_id(1)

        @pl.when(k == 0)
        def _():
            dx_ref[...] = jnp.zeros(dx_ref.shape, F32)

        dx_ref[...] += lax.dot_general(a_ref[...], b_ref[...], NT, preferred_element_type=F32)

        @pl.when(k == nk - 1)
        def _():
            d, dg = _rms_bwd(dx_ref[...], x_ref[...], g_ref[...])
            dx_ref[...] = dx1_ref[...] + d
            _accumulate(dg_ref, jnp.sum(dg, axis=0, keepdims=True), i == 0)

    row = pl.BlockSpec((tm, D), lambda i, k: (i, 0))
    vec = pl.BlockSpec((1, D), lambda i, k: (0, 0))
    return _call(body, [dproj, w_perm, dx1, x, g], name="mm_dh_final", grid=(T // tm, nk),
                 in_specs=[pl.BlockSpec((tm, tk), lambda i, k: (i, k)), pl.BlockSpec((D, tk), lambda i, k: (0, k)),
                           row, row, vec],
                 out_specs=[row, vec], out_shape=[_sds((T, D), F32), _sds((1, D), F32)],
                 sem=("arbitrary", "arbitrary"), comm=comm)


def _chip_sum(pieces, paired, qc, name):
    G, R, C = pieces.shape
    tr = _tile(R, 256, 16)

    def body(qc_ref, g_ref, p_ref, keep_ref, send_ref):
        s = pl.program_id(1)
        tot = g_ref[...] + p_ref[...]

        @pl.when(s == 0)
        def _():
            keep_ref[...] = tot

        @pl.when(s > 0)
        def _():
            send_ref[...] = tot.astype(send_ref.dtype)

    grid_spec = pltpu.PrefetchScalarGridSpec(
        num_scalar_prefetch=1, grid=(R // tr, N_CHIP),
        in_specs=[pl.BlockSpec((None, tr, C), lambda i, s, qc: (2 * (qc[0] ^ s) + qc[1], i, 0)),
                  pl.BlockSpec((None, tr, C), lambda i, s, qc: (qc[0] ^ s, i, 0))],
        out_specs=[pl.BlockSpec((tr, C), lambda i, s, qc: (i, 0)),
                   pl.BlockSpec((None, tr, C), lambda i, s, qc: (jnp.maximum(s - 1, 0), i, 0))])
    send_dtype = BF16 if R >= 16 else pieces.dtype
    return pl.pallas_call(
        body, name=name, grid_spec=grid_spec,
        out_shape=[_sds((R, C), F32), _sds((3, R, C), send_dtype)],
        compiler_params=pltpu.CompilerParams(dimension_semantics=("arbitrary", "arbitrary"),
                                             vmem_limit_bytes=VMEM_LIMIT_BYTES),
    )(qc, pieces, paired)


def _adamw_math(w, g, m, v):
    m = ADAM_B1 * m + (1.0 - ADAM_B1) * g
    v = ADAM_B2 * v + (1.0 - ADAM_B2) * (g * g)
    m_hat = m / (1.0 - ADAM_B1 ** ADAM_STEP)
    v_hat = v / (1.0 - ADAM_B2 ** ADAM_STEP)
    delta = -ADAM_LR * (m_hat / (jnp.sqrt(v_hat) + ADAM_EPS) + ADAM_WD * w)
    return delta, m, v


def _sum_adamw(keep, pieces, w, m, v, name):
    R, C = w.shape
    P = pieces.shape[0]
    tr = _tile(R, 256, 16)

    def body(k_ref, p_ref, w_ref, m_ref, v_ref, g_ref, d_ref, mo_ref, vo_ref):
        g = k_ref[...]
        for q in range(P):
            g = g + p_ref[q].astype(F32)
        g_ref[...] = g
        d_ref[...], mo_ref[...], vo_ref[...] = _adamw_math(w_ref[...], g, m_ref[...], v_ref[...])

    blk = pl.BlockSpec((tr, C), lambda i: (i, 0))
    pblk = pl.BlockSpec((P, tr, C), lambda i: (0, i, 0))
    return _call(body, [keep, pieces, w, m, v], name=name, grid=(R // tr,), in_specs=[blk, pblk, blk, blk, blk],
                 out_specs=[blk] * 4, out_shape=[_sds((R, C), F32)] * 4, sem=("parallel",))[0]


def _adamw_small(parts, w, m, v, widths):
    n = len(widths)

    def body(p_ref, w_ref, m_ref, v_ref, *o_refs):
        g = p_ref[0]
        for q in range(1, N_DEV):
            g = g + p_ref[q]
        vals = (g,) + _adamw_math(w_ref[...], g, m_ref[...], v_ref[...])
        off = 0
        for i, wd in enumerate(widths):
            for kind in range(4):
                o_refs[4 * i + kind][...] = vals[kind][:, off:off + wd]
            off += wd

    whole = pl.BlockSpec(memory_space=pltpu.VMEM)
    outs = _call(body, [parts, w, m, v], name="adamw_small", grid=(), in_specs=[whole] * 4,
                 out_specs=[whole] * (4 * n), out_shape=[_sds((1, wd), F32) for wd in widths for _ in range(4)])[0]
    return [tuple(outs[4 * i:4 * i + 4]) for i in range(n)]


def _layout(D):
    lay = {"q": 0, "kv": Q_LORA, "kpe": Q_LORA + KV_LORA}
    lay["fq"] = lay["kpe"] + LANES
    lay["fk"] = lay["fq"] + HEADS * FOX_DIM
    lay["fv"] = lay["fk"] + HEADS * FOX_DIM
    lay["fl"] = lay["fv"] + HEADS * FOX_DIM
    lay["g"] = lay["fl"] + LANES
    lay["end"] = lay["g"] + 2 * D
    return lay


def kernel(x, positions, pre_mix_norm, w_in, q_a_norm, w_uq, kv_a_norm, w_ukv, b_forget, b_gate, w_branch_mla, w_branch_fox, w_out, post_mix_norm, pre_ffn_norm, w_up, conv_w, conv_b, w_down, post_ffn_norm, loss_target, m_pre_mix_norm, m_w_in, m_q_a_norm, m_w_uq, m_kv_a_norm, m_w_ukv, m_b_forget, m_b_gate, m_w_branch_mla, m_w_branch_fox, m_w_out, m_post_mix_norm, m_pre_ffn_norm, m_w_up, m_conv_w, m_conv_b, m_w_down, m_post_ffn_norm, v_pre_mix_norm, v_w_in, v_q_a_norm, v_w_uq, v_kv_a_norm, v_w_ukv, v_b_forget, v_b_gate, v_w_branch_mla, v_w_branch_fox, v_w_out, v_post_mix_norm, v_pre_ffn_norm, v_w_up, v_conv_w, v_conv_b, v_w_down, v_post_ffn_norm):
    B, S, D = x.shape
    T = B * S
    F = conv_b.shape[0] // 2
    lay = _layout(D)
    n_in = w_in.shape[1]
    d_in = N_DEV * n_in
    seg_a = Q_LORA + KV_LORA + ROPE
    seg_b = 3 * HEADS * FOX_DIM + HEADS
    mla_scale = (NOPE + ROPE) ** -0.5
    fox_scale = FOX_DIM ** -0.5
    ax, ay, ac = (lax.axis_index(a) for a in MESH_AXES)
    qc = jnp.stack([2 * ax + ay, ac]).astype(jnp.int32)

    def row(vec, width=None):
        vec = vec.reshape(1, -1)
        if width is not None and vec.shape[1] < width:
            vec = jnp.pad(vec, ((0, 0), (0, width - vec.shape[1])))
        return vec

    x2 = x.reshape(T, D)
    win_s = _cast_bf16(w_in, "cast_w_in")
    h, (win_g,) = _prenorm(x2, row(pre_mix_norm), comm=_Comm([_GatherRelayPlan([win_s], mid_frac=0.3)]))
    small_s = [_cast_bf16(w, "cast_" + n) for w, n in
               [(w_uq, "w_uq"), (w_ukv, "w_ukv"), (w_branch_mla, "w_branch_mla"), (w_branch_fox, "w_branch_fox"), (w_out, "w_out")]]
    wup_s = _cast_bf16(w_up, "cast_w_up")
    wdown_s = _cast_bf16(w_down, "cast_w_down")

    def shard_cols(lo, hi):
        out = []
        for g in range(lo // n_in, (hi - 1) // n_in + 1):
            out.append(win_g[g][:, max(lo, g * n_in) - g * n_in:min(hi, (g + 1) * n_in) - g * n_in])
        return out

    w_perm = jnp.concatenate(
        shard_cols(0, seg_a) + [jnp.zeros((D, LANES - ROPE), BF16)] + shard_cols(seg_a, seg_a + seg_b)
        + [jnp.zeros((D, LANES - HEADS), BF16)] + shard_cols(seg_a + seg_b, d_in), axis=1)

    tgt = loss_target.reshape(T, D)
    pos = positions.reshape(T, 1)
    inv_freq = 1.0 / (ROPE_THETA ** (jnp.arange(0, ROPE, 2, dtype=F32) / ROPE))
    invf = row(jnp.concatenate([inv_freq, inv_freq]), LANES)
    g_pre, g_q, g_kv = row(pre_mix_norm), row(q_a_norm), row(kv_a_norm)
    g_pm, g_ffn, g_pf = row(post_mix_norm), row(pre_ffn_norm), row(post_ffn_norm)
    bfor = row(b_forget, LANES)
    bgate = row(b_gate)
    cb_full = row(conv_b)

    def own_plan(blocks):
        return _Comm([_GatherOwnPlan(blocks)])

    def pass_plan(gathered):
        return _Comm([_GatherPassPlan(gathered)])

    def pair_plan(gs):
        return _Comm([_PairScatterPlan(gs)])

    def chip_plan(gs):
        return _Comm([_ChipScatterPlan(gs)])

    half_d = D // 2
    proj, landed = _matmul(h, w_perm, mode="nn", name="mm_proj", comm=_Comm(
        [_GatherOwnPlan(small_s[:2] + [conv_w]), _GatherOwnPlan([wup_s], rows=(0, half_d))]))
    early_g, wup_part = landed[:-1], landed[-1:]
    (qn, kvn, kper, logf, cosT, sinT), (wuq_g, wukv_g, cw_g) = _split_prep(
        proj, pos, invf, g_q, g_kv, bfor, lay, comm=pass_plan(early_g))
    wuq_pad = jnp.pad(wuq_g, ((0, 0), (0, 0), (0, ATT_DK - NOPE - ROPE)))
    cw_full = jnp.transpose(cw_g, (1, 0, 2)).reshape(3, 2 * F)

    qraw = _matmul(qn, wuq_pad, mode="nn", name="mm_q", out_blocks=ATT_DK, tm=T)
    kvraw = _matmul(kvn, wukv_g, mode="nn", name="mm_kv", out_blocks=NOPE + VDIM, tm=T)
    (q_mla, k_mla, v_mla), branch_half = _mla_prep(qraw, kvraw, kper, cosT, sinT, comm=own_plan(small_s[2:4]))
    cs = _fox_cumsum(logf, B, S, 1.0 / fox_scale)
    (q_fox, k_fox, v_fox), wout_half = _fox_prep(proj, cs, lay, comm=own_plan(small_s[4:5]))
    ((o_mla, lse_mla), (o_fox, lse_fox)), landed = _attn_fwd(
        [(q_mla, k_mla, v_mla, MLA_UNIT, mla_scale), (q_fox, k_fox, v_fox, 1, fox_scale)], B=B, S=S,
        name="attn_fwd", comm=_Comm([_GatherOwnPlan([wup_s], rows=(half_d, D), into=wup_part),
                                     _GatherPassPlan(branch_half + wout_half)]))
    wup_half, (wbm_g, wbf_g, wout_g) = landed[:1], landed[1:]
    wbm = jnp.transpose(wbm_g, (1, 0, 2)).reshape(HEADS * VDIM, D)
    wbf = jnp.transpose(wbf_g, (1, 0, 2)).reshape(HEADS * FOX_DIM, D)
    wout = wout_g.reshape(D, D)
    (a_m, a_f, merged), (wup_g,) = _branch_merge(o_mla, o_fox, wbm, wbf, proj, bgate, lay, D,
                                                 comm=pass_plan(wup_half))
    n_up = wup_g.shape[2]
    y1, x1, h2 = _out_mid(merged, wout, x2, g_pm, g_ffn)
    u, wdown_half = _matmul(h2, wup_g, mode="nn", name="mm_up", tn=n_up, comm=own_plan([wdown_s]))
    (act,), (wdown_g,) = _convffn_fwd(u, cw_full, cb_full, B, S, F, comm=pass_plan(wdown_half))
    wdown = wdown_g.reshape(F, D)
    ff = _matmul(act, wdown, mode="nn", name="mm_down", tk=F // 2)
    dy, dff, loss_part, dg_pf = _tail(ff, x1, tgt, g_pf)

    dact = _matmul(dff, wdown, mode="nt", name="mm_dact", tn=F // 4)
    dw_down = _matmul(act, dff, mode="tn", name="mm_dw_down", tm=F // 4, tn=512).reshape(N_DEV, F // N_DEV, D)
    (du_g, du_v, dcp_g, dcp_v), (pa_down,) = _convffn_bwd(u, dact, cw_full, cb_full, B, S, F, comm=pair_plan([dw_down]))
    keep_down, sb_down = _chip_sum(dw_down, pa_down, qc, "chipsum_w_down")
    dh2, (rb_down,) = _matmul_halves((du_g, du_v), wup_g, mode="nt", name="mm_dh2", tm=MM_TILE, comm=chip_plan([sb_down]))
    dw_up = _matmul_halves(h2, (du_g, du_v), mode="tn", name="mm_dw_up", tm=MM_TILE, tn=n_up, tk=T // 2)
    (dx1, dy1, dg_ffn, dg_pm), _ = _mid_bwd(dy, dh2, x1, y1, g_ffn, g_pm)
    dw_out, pa_up_part = _matmul(merged, dy1, mode="tn", name="mm_dw_out",
                                 comm=_Comm([_PairScatterPlan([dw_up], rows=(0, half_d))]))
    dw_out = dw_out.reshape(N_DEV, D // N_DEV, D)
    (da_m, da_f, dgl_m, dgl_f, dbg_m, dbg_f), (pa_up,) = _gate_bwd(
        dy1, wout, a_m, a_f, proj, bgate, lay, D,
        comm=_Comm([_PairScatterPlan([dw_up], rows=(half_d, D), into=pa_up_part)]))
    keep_up, sb_up = _chip_sum(dw_up, pa_up, qc, "chipsum_w_up")
    dw_bm = _matmul(o_mla, da_m, mode="tn", name="mm_dw_branch_mla", out_blocks=D // N_DEV)
    dw_bf = _matmul(o_fox, da_f, mode="tn", name="mm_dw_branch_fox", out_blocks=D // N_DEV)
    mix = [dw_out, dw_bm, dw_bf]
    do_mla, pa_mix = _matmul(da_m, wbm, mode="nt", name="mm_do_mla", out_dtype=BF16, comm=pair_plan(mix))
    do_fox = _matmul(da_f, wbf, mode="nt", name="mm_do_fox", out_dtype=BF16)
    mix_sums = [_chip_sum(g, p, qc, "chipsum_" + n) for g, p, n in zip(mix, pa_mix, ["w_out", "w_branch_mla", "w_branch_fox"])]
    ((dq_m, dk_m, dv_m), (dq_f, dk_f, dv_f)), (rb_up,) = _attn_bwd(
        [(q_mla, k_mla, v_mla, o_mla, do_mla, lse_mla, MLA_UNIT, mla_scale),
         (q_fox, k_fox, v_fox, o_fox, do_fox, lse_fox, 1, fox_scale)], B=B, S=S, name="attn_bwd",
        comm=chip_plan([sb_up]))
    (dqraw, dkvraw, dkpe), _ = _mla_bwd_prep(dq_m, dk_m, dv_m, cosT, sinT)
    dqn = _matmul(dqraw, wuq_pad, mode="nt", name="mm_dqn", tm=T)
    dw_uq = _matmul(qn, dqraw, mode="tn", name="mm_dw_uq", out_blocks=ATT_DK)[:, :, :NOPE + ROPE]
    dkvn = _matmul(dkvraw, wukv_g, mode="nt", name="mm_dkvn", tm=T)
    dw_ukv = _matmul(kvn, dkvraw, mode="tn", name="mm_dw_ukv", out_blocks=NOPE + VDIM)
    dqlat, dkvlat, dg_q, dg_kv = _lat_bwd(dqn, dkvn, proj, g_q, g_kv, lay)
    dfl, dbfor = _fox_bwd_prep(dq_f, dk_f, proj, bfor, lay, B, S, 1.0 / fox_scale)
    dproj = _concat_cols([dqlat, dkvlat, dkpe, dq_f, dk_f, dv_f, dfl, dgl_m, dgl_f], "concat_dproj")
    dw_perm, rb_mix = _matmul(h, dproj, mode="tn", name="mm_dw_in", comm=chip_plan([s[1] for s in mix_sums]))
    segs = [(0, seg_a, 0), (seg_a, seg_a + seg_b, lay["fq"] - seg_a), (seg_a + seg_b, d_in, lay["g"] - seg_a - seg_b)]

    def piece(g):
        lo, hi = g * n_in, (g + 1) * n_in
        parts = [dw_perm[:, max(lo, s0) + sh:min(hi, s1) + sh] for s0, s1, sh in segs if max(lo, s0) < min(hi, s1)]
        return parts[0] if len(parts) == 1 else jnp.concatenate(parts, axis=1)

    dw_in = jnp.stack([piece(g) for g in range(N_DEV)])
    dcw = jnp.transpose(jnp.concatenate([dcp_g[0:3], dcp_v[0:3]], axis=1).reshape(3, N_DEV, (2 * F) // N_DEV), (1, 0, 2))
    late = [dw_in, dw_uq, dw_ukv, dcw]
    pa_late = _exchange_alone(pair_plan(late), "pair_late")
    late_sums = [_chip_sum(g, p, qc, "chipsum_" + n) for g, p, n in zip(late, pa_late, ["w_in", "w_uq", "w_ukv", "conv_w"])]
    (grad_x, dg_pre), rb_late = _dh_final(dproj, w_perm, dx1, x2, g_pre, comm=chip_plan([s[1] for s in late_sums]))

    big_out = {}

    def finish(n, keep, pieces, w, m, v):
        big_out[n] = _sum_adamw(keep, pieces, w, m, v, "adamw_" + n)

    finish("w_down", keep_down, rb_down, w_down, m_w_down, v_w_down)
    finish("w_up", keep_up, rb_up, w_up, m_w_up, v_w_up)
    finish("w_out", mix_sums[0][0], rb_mix[0], w_out, m_w_out, v_w_out)
    finish("w_branch_mla", mix_sums[1][0], rb_mix[1], w_branch_mla, m_w_branch_mla, v_w_branch_mla)
    finish("w_branch_fox", mix_sums[2][0], rb_mix[2], w_branch_fox, m_w_branch_fox, v_w_branch_fox)
    finish("w_in", late_sums[0][0], rb_late[0], w_in, m_w_in, v_w_in)
    finish("w_uq", late_sums[1][0], rb_late[1], w_uq, m_w_uq, v_w_uq)
    finish("w_ukv", late_sums[2][0], rb_late[2], w_ukv, m_w_ukv, v_w_ukv)
    finish("conv_w", late_sums[3][0], rb_late[3], conv_w, m_conv_w, v_conv_w)

    widths = [D, Q_LORA, KV_LORA, LANES, 2 * D, D, D, 2 * F, D]
    small_names = ["pre_mix_norm", "q_a_norm", "kv_a_norm", "b_forget", "b_gate", "post_mix_norm", "pre_ffn_norm",
                   "conv_b", "post_ffn_norm"]
    true_w = [D, Q_LORA, KV_LORA, HEADS, 2 * D, D, D, 2 * F, D]
    dcb = jnp.concatenate([dcp_g[3:4], dcp_v[3:4]], axis=1)
    part = jnp.concatenate([dg_pre, dg_q, dg_kv, dbfor, dbg_m, dbg_f, dg_pm, dg_ffn, dcb, dg_pf], axis=1)

    def pack(vals):
        return jnp.concatenate([row(a, wd) for a, wd in zip(vals, widths)], axis=1)

    sw = pack([pre_mix_norm, q_a_norm, kv_a_norm, b_forget, b_gate, post_mix_norm, pre_ffn_norm, conv_b, post_ffn_norm])
    sm = pack([m_pre_mix_norm, m_q_a_norm, m_kv_a_norm, m_b_forget, m_b_gate, m_post_mix_norm, m_pre_ffn_norm,
               m_conv_b, m_post_ffn_norm])
    sv = pack([v_pre_mix_norm, v_q_a_norm, v_kv_a_norm, v_b_forget, v_b_gate, v_post_mix_norm, v_pre_ffn_norm,
               v_conv_b, v_post_ffn_norm])
    (parts_all,) = _exchange_alone(_Comm([_DirectGatherPlan([part])]), "gather_small")
    small = _adamw_small(parts_all, sw, sm, sv, widths)
    small_out = {n: tuple(a.reshape(-1)[:tw] for a in vals) for n, vals, tw in zip(small_names, small, true_w)}

    loss = lax.psum(loss_part[0, 0], MESH_AXES)
    order = ["pre_mix_norm", "w_in", "q_a_norm", "w_uq", "kv_a_norm", "w_ukv", "b_forget", "b_gate", "w_branch_mla",
             "w_branch_fox", "w_out", "post_mix_norm", "pre_ffn_norm", "w_up", "conv_w", "conv_b", "w_down",
             "post_ffn_norm"]
    res = {**big_out, **small_out}
    outs = [loss, grad_x.reshape(B, S, D)]
    for kind in range(4):
        outs += [res[n][kind] for n in order]
    return tuple(outs)
```

```python
import math

import jax
import jax.numpy as jnp
from jax import lax
from jax.experimental import pallas as pl
from jax.experimental.pallas import tpu as pltpu

F32 = jnp.float32
BF16 = jnp.bfloat16

N_DEV = 8
N_CHIP = 4
HEADS = 8
NOPE = 128
ROPE = 64
HALF_ROPE = ROPE // 2
VDIM = 128
Q_LORA = 512
KV_LORA = 256
FOX_DIM = 128
ATT_DK = 256
MLA_UNIT = 64
ROPE_THETA = 10000.0
EPS = 1e-6
NEG_INF = -1e30
LANES = 128
LOG2_E = 1.4426950408889634

ADAM_LR = 0.001
ADAM_B1 = 0.9
ADAM_B2 = 0.999
ADAM_EPS = 1e-08
ADAM_WD = 0.01
ADAM_STEP = 10

VMEM_LIMIT_BYTES = 56 * 1024 * 1024
ROW_TILE = 256
HEAD_ROW_TILE = 1024
ATT_TILE = 1024
ATT_SUB = 256
ATT_AHEAD = 3
MM_TILE = 1024

MESH_AXES = ("x", "y", "c")
ANY = pl.BlockSpec(memory_space=pl.ANY)


def _tile(n, pref, align=LANES):
    if n <= pref:
        return n
    t = (pref // align) * align
    while t >= align:
        if n % t == 0:
            return t
        t -= align
    return n


def _sds(shape, dtype):
    return jax.ShapeDtypeStruct(shape, dtype)


def _coords():
    x, y, c = (lax.axis_index(ax) for ax in MESH_AXES)
    return x, y, c


def _chip_rel(x, y, r):
    return (1 - x if r & 2 else x), (1 - y if r & 1 else y)


def _rcopy(src, dst, sems, w, k, dev):
    return pltpu.make_async_remote_copy(src_ref=src, dst_ref=dst, send_sem=sems[0].at[w, k], recv_sem=sems[1].at[w, k],
                                        device_id=dev, device_id_type=pl.DeviceIdType.MESH)


class _GatherRelayPlan:
    def __init__(self, blocks, mid_frac=0.5):
        self.ins = list(blocks)
        self.out_shapes = [_sds((N_DEV,) + b.shape, b.dtype) for b in blocks]
        n = len(blocks)
        self.scratch = [pltpu.SemaphoreType.DMA((n, 7)), pltpu.SemaphoreType.DMA((n, 7)), pltpu.SemaphoreType.DMA((n,))]
        self.mid_frac = mid_frac

    @staticmethod
    def _places():
        x, y, c = _coords()
        xn, yn = 4 * (1 - x) + 2 * y, 4 * x + 2 * (1 - y)
        relay_src = 4 * (x + c * (1 - 2 * x)) + 2 * (y + (1 - c) * (1 - 2 * y)) + c
        relay_to = (x + (1 - c) * (1 - 2 * x), y + c * (1 - 2 * y), c)
        return x, y, c, xn, yn, relay_src, relay_to, 4 * (1 - x) + 2 * (1 - y)

    def first(self, ins, outs, sems):
        x, y, c, _, _, _, _, _ = self._places()
        me = 4 * x + 2 * y + c
        for w in range(len(ins)):
            pltpu.make_async_copy(ins[w], outs[w].at[me], sems[2].at[w]).start()
            _rcopy(ins[w], outs[w].at[me], sems, w, 0, (x, y, 1 - c)).start()
            _rcopy(ins[w], outs[w].at[me], sems, w, 1, (1 - x, y, c)).start()
            _rcopy(ins[w], outs[w].at[me], sems, w, 2, (x, 1 - y, c)).start()

    def mid(self, ins, outs, sems):
        x, y, c, xn, yn, relay_src, relay_to, _ = self._places()
        sib = (x, y, 1 - c)
        for w in range(len(ins)):
            bx, by = outs[w].at[xn + c], outs[w].at[yn + c]
            _rcopy(ins[w], bx, sems, w, 1, (1 - x, y, c)).wait_recv()
            _rcopy(ins[w], by, sems, w, 2, (x, 1 - y, c)).wait_recv()
            _rcopy(outs[w].at[relay_src], outs[w].at[relay_src], sems, w, 3, relay_to).start()
            _rcopy(bx, bx, sems, w, 4, sib).start()
            _rcopy(by, by, sems, w, 5, sib).start()

    def last(self, ins, outs, sems):
        x, y, c, xn, yn, _, relay_to, dg = self._places()
        me = 4 * x + 2 * y + c
        sib = (x, y, 1 - c)
        for w in range(len(ins)):
            bd = outs[w].at[dg + c]
            _rcopy(ins[w], bd, sems, w, 3, relay_to).wait_recv()
            _rcopy(bd, bd, sems, w, 6, sib).start()
            for k, blk in ((0, 4 * x + 2 * y), (4, xn), (5, yn), (6, dg)):
                _rcopy(ins[w], outs[w].at[blk + 1 - c], sems, w, k, sib).wait_recv()
            for k in range(7):
                _rcopy(ins[w], outs[w].at[me], sems, w, k, sib).wait_send()
            pltpu.make_async_copy(ins[w], outs[w].at[me], sems[2].at[w]).wait()


class _GatherOwnPlan:
    mid = None

    def __init__(self, blocks, rows=None, into=None):
        self.n = len(blocks)
        self.rows = rows
        self.ins = list(blocks) + list(into or [])
        self.out_shapes = [_sds((N_DEV,) + b.shape, b.dtype) for b in blocks]
        self.aliases = [(self.n + i, i) for i in range(len(into or []))]
        n = self.n
        self.scratch = [pltpu.SemaphoreType.DMA((n, 4)), pltpu.SemaphoreType.DMA((n, 4)), pltpu.SemaphoreType.DMA((n,))]

    def _cut(self, ref):
        return ref if self.rows is None else ref.at[pl.ds(self.rows[0], self.rows[1] - self.rows[0])]

    def first(self, ins, outs, sems):
        x, y, c = _coords()
        me = 4 * x + 2 * y + c
        for w in range(self.n):
            src, dst = self._cut(ins[w]), self._cut(outs[w].at[me])
            pltpu.make_async_copy(src, dst, sems[2].at[w]).start()
            _rcopy(src, dst, sems, w, 0, (x, y, 1 - c)).start()
            for r in (1, 2, 3):
                px, py = _chip_rel(x, y, r)
                _rcopy(src, dst, sems, w, r, (px, py, c)).start()

    def last(self, ins, outs, sems):
        x, y, c = _coords()
        me = 4 * x + 2 * y + c
        for w in range(self.n):
            src = self._cut(ins[w])
            cp = _rcopy(src, self._cut(outs[w].at[4 * x + 2 * y + 1 - c]), sems, w, 0, (x, y, 1 - c))
            cp.wait_recv()
            cp.wait_send()
            for r in (1, 2, 3):
                px, py = _chip_rel(x, y, r)
                cp = _rcopy(src, self._cut(outs[w].at[4 * px + 2 * py + c]), sems, w, r, (px, py, c))
                cp.wait_recv()
                cp.wait_send()
            pltpu.make_async_copy(src, self._cut(outs[w].at[me]), sems[2].at[w]).wait()


class _GatherPassPlan:
    mid = None

    def __init__(self, gathered):
        self.ins = list(gathered)
        self.out_shapes = [_sds(g.shape, g.dtype) for g in gathered]
        self.aliases = [(i, i) for i in range(len(gathered))]
        n = len(gathered)
        self.scratch = [pltpu.SemaphoreType.DMA((n, 3)), pltpu.SemaphoreType.DMA((n, 3))]

    def first(self, ins, outs, sems):
        x, y, c = _coords()
        for w in range(len(ins)):
            for r in (1, 2, 3):
                px, py = _chip_rel(x, y, r)
                blk = 4 * px + 2 * py + c
                _rcopy(ins[w].at[blk], outs[w].at[blk], sems, w, r - 1, (x, y, 1 - c)).start()

    def last(self, ins, outs, sems):
        x, y, c = _coords()
        for w in range(len(ins)):
            for r in (1, 2, 3):
                px, py = _chip_rel(x, y, r)
                blk = 4 * px + 2 * py + 1 - c
                cp = _rcopy(ins[w].at[blk], outs[w].at[blk], sems, w, r - 1, (x, y, 1 - c))
                cp.wait_recv()
                cp.wait_send()


class _DirectGatherPlan:
    mid = None

    def __init__(self, blocks):
        self.ins = list(blocks)
        self.out_shapes = [_sds((N_DEV,) + b.shape, b.dtype) for b in blocks]
        n = len(blocks)
        self.scratch = [pltpu.SemaphoreType.DMA((n, 7)), pltpu.SemaphoreType.DMA((n, 7)), pltpu.SemaphoreType.DMA((n,))]

    @staticmethod
    def _peer(x, y, c, r):
        return (1 - x if r & 4 else x), (1 - y if r & 2 else y), (1 - c if r & 1 else c)

    def first(self, ins, outs, sems):
        x, y, c = _coords()
        me = 4 * x + 2 * y + c
        for w in range(len(ins)):
            pltpu.make_async_copy(ins[w], outs[w].at[me], sems[2].at[w]).start()
            for r in range(1, N_DEV):
                _rcopy(ins[w], outs[w].at[me], sems, w, r - 1, self._peer(x, y, c, r)).start()

    def last(self, ins, outs, sems):
        x, y, c = _coords()
        me = 4 * x + 2 * y + c
        for w in range(len(ins)):
            for r in range(1, N_DEV):
                px, py, pc = self._peer(x, y, c, r)
                cp = _rcopy(ins[w], outs[w].at[4 * px + 2 * py + pc], sems, w, r - 1, (px, py, pc))
                cp.wait_recv()
                cp.wait_send()
            pltpu.make_async_copy(ins[w], outs[w].at[me], sems[2].at[w]).wait()


class _PairScatterPlan:
    mid = None

    def __init__(self, pieces, rows=None, into=None):
        self.n = len(pieces)
        self.rows = rows
        self.ins = list(pieces) + list(into or [])
        self.out_shapes = [_sds((N_CHIP,) + p.shape[1:], p.dtype) for p in pieces]
        self.aliases = [(self.n + i, i) for i in range(len(into or []))]
        self.scratch = [pltpu.SemaphoreType.DMA((self.n, N_CHIP)), pltpu.SemaphoreType.DMA((self.n, N_CHIP))]

    def _copies(self, ins, outs, sems):
        x, y, c = _coords()
        cps = []
        for w in range(self.n):
            for q in range(N_CHIP):
                src, dst = ins[w].at[2 * q + 1 - c], outs[w].at[q]
                if self.rows is not None:
                    cut = pl.ds(self.rows[0], self.rows[1] - self.rows[0])
                    src, dst = src.at[cut], dst.at[cut]
                cps.append(_rcopy(src, dst, sems, w, q, (x, y, 1 - c)))
        return cps

    def first(self, ins, outs, sems):
        for cp in self._copies(ins, outs, sems):
            cp.start()

    def last(self, ins, outs, sems):
        for cp in self._copies(ins, outs, sems):
            cp.wait_recv()
            cp.wait_send()


class _ChipScatterPlan:
    mid = None

    def __init__(self, sums, rows=None, into=None):
        self.n = len(sums)
        self.rows = rows
        self.ins = list(sums) + list(into or [])
        self.out_shapes = [_sds(s.shape, s.dtype) for s in sums]
        self.aliases = [(self.n + i, i) for i in range(len(into or []))]
        self.scratch = [pltpu.SemaphoreType.DMA((self.n, 3)), pltpu.SemaphoreType.DMA((self.n, 3))]

    def _copies(self, ins, outs, sems):
        x, y, c = _coords()
        cps = []
        for w in range(self.n):
            for r in (1, 2, 3):
                px, py = _chip_rel(x, y, r)
                src, dst = ins[w].at[r - 1], outs[w].at[r - 1]
                if self.rows is not None:
                    cut = pl.ds(self.rows[0], self.rows[1] - self.rows[0])
                    src, dst = src.at[cut], dst.at[cut]
                cps.append(_rcopy(src, dst, sems, w, r - 1, (px, py, c)))
        return cps

    def first(self, ins, outs, sems):
        for cp in self._copies(ins, outs, sems):
            cp.start()

    def last(self, ins, outs, sems):
        for cp in self._copies(ins, outs, sems):
            cp.wait_recv()
            cp.wait_send()


class _Comm:
    def __init__(self, plans):
        self.plans = list(plans)
        self.ins = [a for p in self.plans for a in p.ins]
        self.out_shapes = [s for p in self.plans for s in p.out_shapes]
        self.scratch = [s for p in self.plans for s in p.scratch]
        self.aliases = []
        i = o = 0
        for p in self.plans:
            self.aliases += [(i + a, o + b) for a, b in getattr(p, "aliases", [])]
            i, o = i + len(p.ins), o + len(p.out_shapes)

    def _parts(self, ins, outs, sems):
        i = o = s = 0
        for p in self.plans:
            yield p, ins[i:i + len(p.ins)], outs[o:o + len(p.out_shapes)], sems[s:s + len(p.scratch)]
            i, o, s = i + len(p.ins), o + len(p.out_shapes), s + len(p.scratch)

    def begin(self, step, nsteps, ins, outs, sems):
        @pl.when(step == 0)
        def _():
            for p, pi, po, ps in self._parts(ins, outs, sems):
                p.first(pi, po, ps)

        for p, pi, po, ps in self._parts(ins, outs, sems):
            if p.mid is not None:
                @pl.when(step == min(nsteps - 1, int(p.mid_frac * nsteps)))
                def _(p=p, pi=pi, po=po, ps=ps):
                    p.mid(pi, po, ps)

    def end(self, step, nsteps, ins, outs, sems):
        @pl.when(step == nsteps - 1)
        def _():
            for p, pi, po, ps in self._parts(ins, outs, sems):
                p.last(pi, po, ps)


def _call(body, args, *, name, grid, in_specs, out_specs, out_shape, scratch_shapes=(), sem=None, comm=None):
    in_specs, out_specs, out_shape, scratch_shapes = list(in_specs), list(out_specs), list(out_shape), list(scratch_shapes)
    if comm is None:
        res = pl.pallas_call(
            body, name=name, grid=grid, in_specs=in_specs, out_specs=out_specs, out_shape=out_shape,
            scratch_shapes=scratch_shapes,
            compiler_params=pltpu.CompilerParams(dimension_semantics=sem, vmem_limit_bytes=VMEM_LIMIT_BYTES),
        )(*args)
        return list(res), []
    n_in, n_out, n_sc = len(in_specs), len(out_specs), len(scratch_shapes)
    n_ci, n_co = len(comm.ins), len(comm.out_shapes)
    nsteps = math.prod(grid)

    def hosted(*refs):
        ins, cins = refs[:n_in], refs[n_in:n_in + n_ci]
        o0 = n_in + n_ci
        outs, couts = refs[o0:o0 + n_out], refs[o0 + n_out:o0 + n_out + n_co]
        s0 = o0 + n_out + n_co
        scr, csems = refs[s0:s0 + n_sc], refs[s0 + n_sc:]
        step = jnp.int32(0)
        for d in range(len(grid)):
            step = step * grid[d] + pl.program_id(d)
        comm.begin(step, nsteps, cins, couts, csems)
        body(*ins, *outs, *scr)
        comm.end(step, nsteps, cins, couts, csems)

    res = pl.pallas_call(
        hosted, name=name, grid=grid, in_specs=in_specs + [ANY] * n_ci, out_specs=out_specs + [ANY] * n_co,
        out_shape=out_shape + comm.out_shapes, scratch_shapes=scratch_shapes + comm.scratch,
        input_output_aliases={n_in + a: n_out + b for a, b in comm.aliases},
        compiler_params=pltpu.CompilerParams(dimension_semantics=("arbitrary",) * len(grid),
                                             vmem_limit_bytes=VMEM_LIMIT_BYTES, has_side_effects=True),
    )(*args, *comm.ins)
    return list(res[:n_out]), list(res[n_out:])


def _exchange_alone(comm, name):
    def body():
        pass

    return _call(body, [], name=name, grid=(), in_specs=[], out_specs=[], out_shape=[], comm=comm)[1]


def _matmul(a, b, *, mode, name, out_dtype=F32, out_blocks=None, tm=None, tn=None, tk=None, comm=None):
    tm = MM_TILE if tm is None else tm
    tn = MM_TILE if tn is None else tn
    a_blk = a.ndim == 3
    b_blk = b.ndim == 3
    if mode == "nn":
        M, K = a.shape
        N = b.shape[0] * b.shape[2] if b_blk else b.shape[1]
        dims = (((1,), (0,)), ((), ()))
    elif mode == "nt":
        M = a.shape[1] if a_blk else a.shape[0]
        K = a.shape[0] * a.shape[2] if a_blk else a.shape[1]
        N = b.shape[1] if b_blk else b.shape[0]
        dims = (((1,), (1,)), ((), ()))
    else:
        K, M = a.shape
        N = b.shape[0] * b.shape[2] if b_blk else b.shape[1]
        dims = (((0,), (0,)), ((), ()))

    tm = _tile(M, tm)
    tn = _tile(N, tn)
    if mode == "nt" and (a_blk or b_blk):
        tk = a.shape[2] if a_blk else b.shape[2]
    else:
        tk = _tile(K, K if tk is None else tk)
    if mode != "nt" and b_blk:
        tn = _tile(b.shape[2], tn)
    if out_blocks is not None:
        tn = _tile(out_blocks, tn)
    nk = K // tk
    grid = (M // tm, N // tn, nk)

    if mode == "nn":
        a_spec = pl.BlockSpec((tm, tk), lambda i, j, k: (i, k))
        if b_blk:
            rb = b.shape[2] // tn
            b_spec = pl.BlockSpec((None, tk, tn), lambda i, j, k: (j // rb, k, j % rb))
        else:
            b_spec = pl.BlockSpec((tk, tn), lambda i, j, k: (k, j))
    elif mode == "nt":
        if a_blk:
            a_spec = pl.BlockSpec((None, tm, tk), lambda i, j, k: (k, i, 0))
        else:
            a_spec = pl.BlockSpec((tm, tk), lambda i, j, k: (i, k))
        if b_blk:
            b_spec = pl.BlockSpec((None, tn, tk), lambda i, j, k: (k, j, 0))
        else:
            b_spec = pl.BlockSpec((tn, tk), lambda i, j, k: (j, k))
    else:
        a_spec = pl.BlockSpec((tk, tm), lambda i, j, k: (k, i))
        if b_blk:
            rb = b.shape[2] // tn
            b_spec = pl.BlockSpec((None, tk, tn), lambda i, j, k: (j // rb, k, j % rb))
        else:
            b_spec = pl.BlockSpec((tk, tn), lambda i, j, k: (k, j))

    if out_blocks is None:
        o_spec = pl.BlockSpec((tm, tn), lambda i, j, k: (i, j))
        o_shape = _sds((M, N), out_dtype)
    else:
        ro = out_blocks // tn
        o_spec = pl.BlockSpec((None, tm, tn), lambda i, j, k: (j // ro, i, j % ro))
        o_shape = _sds((N // out_blocks, M, out_blocks), out_dtype)

    direct = nk == 1 or out_dtype == F32

    def body(a_ref, b_ref, o_ref, *scratch):
        if nk == 1:
            o_ref[...] = lax.dot_general(a_ref[...], b_ref[...], dims, preferred_element_type=F32).astype(o_ref.dtype)
            return
        acc_ref = o_ref if direct else scratch[0]
        k = pl.program_id(2)

        @pl.when(k == 0)
        def _():
            acc_ref[...] = jnp.zeros(acc_ref.shape, F32)

        acc_ref[...] += lax.dot_general(a_ref[...], b_ref[...], dims, preferred_element_type=F32)
        if not direct:
            @pl.when(k == nk - 1)
            def _():
                o_ref[...] = acc_ref[...].astype(o_ref.dtype)

    scratch = [] if direct else [pltpu.VMEM((tm, tn), F32)]
    outs, landed = _call(body, [a, b], name=name, grid=grid, in_specs=[a_spec, b_spec], out_specs=[o_spec],
                         out_shape=[o_shape], scratch_shapes=scratch, sem=("parallel", "parallel", "arbitrary"), comm=comm)
    return outs[0] if comm is None else (outs[0], landed)


def _matmul_halves(a, b, *, mode, name, tm, tn=None, tk=None, comm=None):
    if mode == "nt":
        lo, hi = a
        M, kh = lo.shape
        G, N, kb = b.shape
        half = kh // kb
        tm, tn = _tile(M, tm), _tile(N, N if tn is None else tn)
        dims = (((1,), (1,)), ((), ()))

        def body(lo_ref, hi_ref, b_ref, o_ref):
            k = pl.program_id(2)

            @pl.when(k == 0)
            def _():
                o_ref[...] = jnp.zeros(o_ref.shape, F32)

            @pl.when(k < half)
            def _():
                o_ref[...] += lax.dot_general(lo_ref[...], b_ref[...], dims, preferred_element_type=F32)

            @pl.when(k >= half)
            def _():
                o_ref[...] += lax.dot_general(hi_ref[...], b_ref[...], dims, preferred_element_type=F32)

        outs, landed = _call(
            body, [lo, hi, b], name=name, grid=(M // tm, N // tn, G),
            in_specs=[pl.BlockSpec((tm, kb), lambda i, j, k: (i, jnp.minimum(k, half - 1))),
                      pl.BlockSpec((tm, kb), lambda i, j, k: (i, jnp.maximum(k - half, 0))),
                      pl.BlockSpec((None, tn, kb), lambda i, j, k: (k, j, 0))],
            out_specs=[pl.BlockSpec((tm, tn), lambda i, j, k: (i, j))], out_shape=[_sds((M, N), F32)],
            sem=("parallel", "parallel", "arbitrary"), comm=comm)
    else:
        lo, hi = b
        K, nh = lo.shape
        M = a.shape[1]
        n = tn
        half = nh // n
        tm, tk = _tile(M, tm), _tile(K, K if tk is None else tk)
        nk = K // tk
        dims = (((0,), (0,)), ((), ()))

        def body(a_ref, lo_ref, hi_ref, o_ref):
            j, k = pl.program_id(1), pl.program_id(2)

            @pl.when(k == 0)
            def _():
                o_ref[...] = jnp.zeros(o_ref.shape, F32)

            @pl.when(j < half)
            def _():
                o_ref[...] += lax.dot_general(a_ref[...], lo_ref[...], dims, preferred_element_type=F32)

            @pl.when(j >= half)
            def _():
                o_ref[...] += lax.dot_general(a_ref[...], hi_ref[...], dims, preferred_element_type=F32)

        outs, landed = _call(
            body, [a, lo, hi], name=name, grid=(M // tm, 2 * half, nk),
            in_specs=[pl.BlockSpec((tk, tm), lambda i, j, k: (k, i)),
                      pl.BlockSpec((tk, n), lambda i, j, k: (jnp.where(j < half, k, nk - 1), jnp.minimum(j, half - 1))),
                      pl.BlockSpec((tk, n), lambda i, j, k: (jnp.where(j >= half, k, 0), jnp.maximum(j - half, 0)))],
            out_specs=[pl.BlockSpec((None, tm, n), lambda i, j, k: (j, i, 0))],
            out_shape=[_sds((2 * half, M, n), F32)], sem=("parallel", "parallel", "arbitrary"), comm=comm)
    return outs[0] if comm is None else (outs[0], landed)


def _rms(x):
    return lax.rsqrt(jnp.mean(x * x, axis=-1, keepdims=True) + EPS)


def _rms_bwd(dy, x, g):
    r = _rms(x)
    n = x * r
    dn = dy * g
    dx = r * (dn - n * jnp.mean(dn * n, axis=-1, keepdims=True))
    return dx, dy * n


def _sigmoid(x):
    return 1.0 / (1.0 + jnp.exp(-x))


def _rope_rot(t):
    return pltpu.roll(t, HALF_ROPE, 1) - pltpu.roll(t, LANES - HALF_ROPE, 1)


def _lane(shape):
    return lax.broadcasted_iota(jnp.int32, shape, 1)


def _split3(x):
    hi = x.astype(BF16).astype(F32)
    r1 = x - hi
    mid = r1.astype(BF16).astype(F32)
    lo = (r1 - mid).astype(BF16).astype(F32)
    return hi, mid, lo


def _cumsum_rows(x, reverse):
    S = x.shape[0]
    bs = min(256, S)
    nb = S // bs
    r = lax.broadcasted_iota(jnp.int32, (bs, bs), 0)
    c = lax.broadcasted_iota(jnp.int32, (bs, bs), 1)
    tri = jnp.where((c >= r) if reverse else (c <= r), 1.0, 0.0).astype(BF16)
    edge = lax.broadcasted_iota(jnp.int32, (bs, x.shape[1]), 0) == (0 if reverse else bs - 1)
    carry = jnp.zeros((1, x.shape[1]), F32)
    outs = [None] * nb
    for bi in (range(nb - 1, -1, -1) if reverse else range(nb)):
        xb = x[bi * bs:(bi + 1) * bs, :]
        acc = carry
        for term in _split3(xb):
            acc = acc + jnp.dot(tri, term.astype(BF16), preferred_element_type=F32)
        outs[bi] = acc
        carry = jnp.sum(jnp.where(edge, acc, 0.0), axis=0, keepdims=True)
    return jnp.concatenate(outs, axis=0) if nb > 1 else outs[0]


def _gelu_parts(x):
    c0 = math.sqrt(2.0 / math.pi)
    inner = c0 * (x + 0.044715 * (x * x * x))
    t = jnp.tanh(inner)
    g = 0.5 * x * (1.0 + t)
    dg = 0.5 * (1.0 + t) + 0.5 * x * (1.0 - t * t) * (c0 * (1.0 + 3.0 * 0.044715 * (x * x)))
    return g, dg


def _accumulate(ref, value, first):
    @pl.when(first)
    def _():
        ref[...] = value

    @pl.when(jnp.logical_not(first))
    def _():
        ref[...] += value


def _cast_bf16(w, name):
    R, C = w.shape
    tr = _tile(R, 512, 16)

    def body(w_ref, o_ref):
        o_ref[...] = w_ref[...].astype(BF16)

    blk = pl.BlockSpec((tr, C), lambda i: (i, 0))
    return _call(body, [w], name=name, grid=(R // tr,), in_specs=[blk], out_specs=[blk],
                 out_shape=[_sds((R, C), BF16)], sem=("parallel",))[0][0]


def _concat_cols(parts, name):
    T = parts[0].shape[-2] if parts[0].ndim == 3 else parts[0].shape[0]
    widths = [p.shape[0] * LANES if p.ndim == 3 else p.shape[1] for p in parts]
    tm = _tile(T, ROW_TILE, 16)

    def body(*refs):
        o_ref = refs[-1]
        off = 0
        for p_ref, p, w in zip(refs[:-1], parts, widths):
            if p.ndim == 3:
                for hd in range(p.shape[0]):
                    o_ref[:, off + hd * LANES:off + (hd + 1) * LANES] = p_ref[hd].astype(BF16)
            else:
                o_ref[:, off:off + w] = p_ref[...].astype(BF16)
            off += w

    def spec(p, w):
        if p.ndim == 3:
            return pl.BlockSpec((p.shape[0], tm, LANES), lambda i: (0, i, 0))
        return pl.BlockSpec((tm, w), lambda i: (i, 0))

    return _call(body, parts, name=name, grid=(T // tm,),
                 in_specs=[spec(p, w) for p, w in zip(parts, widths)],
                 out_specs=[pl.BlockSpec((tm, sum(widths)), lambda i: (i, 0))],
                 out_shape=[_sds((T, sum(widths)), BF16)], sem=("parallel",))[0][0]


def _prenorm(x, g, comm=None):
    T, D = x.shape
    tm = _tile(T, ROW_TILE, 16)

    def body(x_ref, g_ref, h_ref):
        xv = x_ref[...]
        h_ref[...] = (xv * _rms(xv) * g_ref[...]).astype(BF16)

    row = pl.BlockSpec((tm, D), lambda i: (i, 0))
    (h,), landed = _call(body, [x, g], name="prenorm", grid=(T // tm,),
                         in_specs=[row, pl.BlockSpec((1, D), lambda i: (0, 0))], out_specs=[row],
                         out_shape=[_sds((T, D), BF16)], sem=("parallel",), comm=comm)
    return h, landed


def _split_prep(proj, pos, invf, gq, gkv, bfor, lay, comm=None):
    T = proj.shape[0]
    tm = _tile(T, ROW_TILE, 16)

    def body(q_ref, kv_ref, kpe_ref, fl_ref, pos_ref, invf_ref, gq_ref, gkv_ref, bf_ref,
             qn_ref, kvn_ref, kper_ref, logf_ref, cos_ref, sin_ref):
        ql = q_ref[...]
        qn_ref[...] = (ql * _rms(ql) * gq_ref[...]).astype(BF16)
        kl = kv_ref[...]
        kvn_ref[...] = (kl * _rms(kl) * gkv_ref[...]).astype(BF16)
        ang = pos_ref[...].astype(F32) * invf_ref[...]
        valid = _lane(ang.shape) < ROPE
        cs = jnp.where(valid, jnp.cos(ang), 0.0)
        sn = jnp.where(valid, jnp.sin(ang), 0.0)
        cos_ref[...] = cs
        sin_ref[...] = sn
        kp = jnp.where(valid, kpe_ref[...], 0.0)
        kper_ref[...] = (kp * cs + _rope_rot(kp) * sn).astype(BF16)
        z = fl_ref[...] + bf_ref[...]
        logf_ref[...] = jnp.minimum(z, 0.0) - jnp.log(1.0 + jnp.exp(-jnp.abs(z)))

    def col(width, off):
        return pl.BlockSpec((tm, width), lambda i: (i, off // width))

    def vec(width):
        return pl.BlockSpec((1, width), lambda i: (0, 0))

    def out(width):
        return pl.BlockSpec((tm, width), lambda i: (i, 0))

    return _call(
        body, [proj, proj, proj, proj, pos, invf, gq, gkv, bfor], name="split_prep", grid=(T // tm,),
        in_specs=[col(Q_LORA, lay["q"]), col(KV_LORA, lay["kv"]), col(LANES, lay["kpe"]), col(LANES, lay["fl"]),
                  pl.BlockSpec((tm, 1), lambda i: (i, 0)), vec(LANES), vec(Q_LORA), vec(KV_LORA), vec(LANES)],
        out_specs=[out(Q_LORA), out(KV_LORA), out(LANES), out(LANES), out(LANES), out(LANES)],
        out_shape=[_sds((T, Q_LORA), BF16), _sds((T, KV_LORA), BF16), _sds((T, LANES), BF16),
                   _sds((T, LANES), F32), _sds((T, LANES), F32), _sds((T, LANES), F32)],
        sem=("parallel",), comm=comm)


def _mla_prep(qraw, kvraw, kper, cosT, sinT, comm=None):
    H, T, _ = qraw.shape
    tm = _tile(T, HEAD_ROW_TILE, 16)

    def body(q_ref, kv_ref, kpe_ref, cos_ref, sin_ref, qo_ref, ko_ref, vo_ref):
        q = q_ref[...]
        pe = q[:, NOPE:]
        pe = jnp.where(_lane(pe.shape) < ROPE, pe, 0.0)
        qo_ref[:, :NOPE] = q[:, :NOPE].astype(BF16)
        qo_ref[:, NOPE:] = (pe * cos_ref[...] + _rope_rot(pe) * sin_ref[...]).astype(BF16)
        kv = kv_ref[...]
        ko_ref[:, :NOPE] = kv[:, :NOPE].astype(BF16)
        ko_ref[:, NOPE:] = kpe_ref[...]
        vo_ref[...] = kv[:, NOPE:].astype(BF16)

    head = pl.BlockSpec((None, tm, ATT_DK), lambda h, i: (h, i, 0))
    tok = pl.BlockSpec((tm, LANES), lambda h, i: (i, 0))
    return _call(
        body, [qraw, kvraw, kper, cosT, sinT], name="mla_prep", grid=(H, T // tm),
        in_specs=[head, head, tok, tok, tok],
        out_specs=[head, head, pl.BlockSpec((None, tm, VDIM), lambda h, i: (h, i, 0))],
        out_shape=[_sds((H, T, ATT_DK), BF16), _sds((H, T, ATT_DK), BF16), _sds((H, T, VDIM), BF16)],
        sem=("parallel", "parallel"), comm=comm)


def _fox_cumsum(logf, B, S, inv_scale):
    T = logf.shape[0]

    def body(l_ref, c_ref):
        c_ref[...] = _cumsum_rows(l_ref[...], reverse=False) * inv_scale

    seq = pl.BlockSpec((S, LANES), lambda b: (b, 0))
    return _call(body, [logf], name="fox_cumsum", grid=(B,), in_specs=[seq], out_specs=[seq],
                 out_shape=[_sds((T, LANES), F32)], sem=("parallel",))[0][0]


def _fox_prep(proj, cs, lay, comm=None):
    T = proj.shape[0]
    tm = _tile(T, HEAD_ROW_TILE, 16)

    def body(q_ref, k_ref, v_ref, cs_ref, qo_ref, ko_ref, vo_ref):
        h = pl.program_id(0)
        cv = cs_ref[...]
        lane = _lane(cv.shape)
        ccol = jnp.sum(jnp.where(lane == h, cv, 0.0), axis=1, keepdims=True)
        hi, mid, lo = _split3(ccol)
        one = jnp.where(lane < 6, 1.0, 0.0)
        augq = jnp.where(lane == 0, hi, jnp.where(lane == 1, mid, jnp.where(lane == 2, lo, one)))
        augk = jnp.where(lane < 3, 1.0, jnp.where(lane == 3, -hi, jnp.where(lane == 4, -mid, jnp.where(lane == 5, -lo, 0.0))))
        qo_ref[:, :FOX_DIM] = q_ref[...].astype(BF16)
        qo_ref[:, FOX_DIM:] = augq.astype(BF16)
        ko_ref[:, :FOX_DIM] = k_ref[...].astype(BF16)
        ko_ref[:, FOX_DIM:] = augk.astype(BF16)
        vo_ref[...] = v_ref[...].astype(BF16)

    def col(off):
        return pl.BlockSpec((tm, FOX_DIM), lambda h, i: (i, off // FOX_DIM + h))

    head = pl.BlockSpec((None, tm, ATT_DK), lambda h, i: (h, i, 0))
    return _call(
        body, [proj, proj, proj, cs], name="fox_prep", grid=(HEADS, T // tm),
        in_specs=[col(lay["fq"]), col(lay["fk"]), col(lay["fv"]), pl.BlockSpec((tm, LANES), lambda h, i: (i, 0))],
        out_specs=[head, head, pl.BlockSpec((None, tm, VDIM), lambda h, i: (h, i, 0))],
        out_shape=[_sds((HEADS, T, ATT_DK), BF16), _sds((HEADS, T, ATT_DK), BF16), _sds((HEADS, T, VDIM), BF16)],
        sem=("parallel", "parallel"), comm=comm)


def _visible(tq, tk, unit):
    r = lax.broadcasted_iota(jnp.int32, (tq, tk), 0)
    c = lax.broadcasted_iota(jnp.int32, (tq, tk), 1)
    sh = int(math.log2(unit))
    return lax.shift_right_logical(c, sh) <= lax.shift_right_logical(r, sh)


def _attn_fwd(streams, *, B, S, name, comm=None):
    n = len(streams)
    H, T, DK = streams[0][0].shape
    DV = streams[0][2].shape[2]
    tq = _tile(S, ATT_TILE)
    nq = S // tq
    sub = min(ATT_SUB, tq)
    NT = (((1,), (1,)), ((), ()))

    def body(*refs):
        ins, outs, (m_sc, acc_sc) = refs[:3 * n], refs[3 * n:5 * n], refs[5 * n:]
        i, j = pl.program_id(1), pl.program_id(2)

        @pl.when(j == 0)
        def _():
            m_sc[...] = jnp.full(m_sc.shape, NEG_INF, F32)
            acc_sc[...] = jnp.zeros(acc_sc.shape, F32)

        def step(diagonal):
            work = [(t, r) for r in range(tq // sub) for t in range(n)]

            def scores(t, r):
                q_ref, k_ref, _ = ins[3 * t:3 * t + 3]
                kc = (r + 1) * sub if diagonal else tq
                s = lax.dot_general(q_ref[r * sub:(r + 1) * sub, :], k_ref[0:kc, :], NT, preferred_element_type=F32)
                return s * (streams[t][4] * LOG2_E)

            ahead = [scores(*work[w]) for w in range(min(ATT_AHEAD, len(work)))]
            for w, (t, r) in enumerate(work):
                s = ahead.pop(0)
                if w + ATT_AHEAD < len(work):
                    ahead.append(scores(*work[w + ATT_AHEAD]))
                v_ref = ins[3 * t + 2]
                kc = s.shape[1]
                rows = slice(r * sub, (r + 1) * sub)
                if diagonal:
                    own = jnp.where(_visible(sub, sub, streams[t][3]), s[:, kc - sub:], NEG_INF)
                    s = own if kc == sub else jnp.concatenate([s[:, :kc - sub], own], axis=1)
                m_prev = m_sc[t, rows, :]
                mx = s[:, 0:LANES]
                for g in range(1, kc // LANES):
                    mx = jnp.maximum(mx, s[:, g * LANES:(g + 1) * LANES])
                m_new = jnp.maximum(m_prev, jnp.max(mx, axis=1, keepdims=True))
                alpha = jnp.exp2(m_prev - m_new)
                p = jnp.exp2(s - jnp.tile(m_new, (1, kc // LANES))).astype(BF16)
                v_aug = jnp.concatenate([v_ref[0:kc, :], jnp.ones((kc, LANES), BF16)], axis=1)
                acc_sc[t, rows, :] = jnp.tile(alpha, (1, 2)) * acc_sc[t, rows, :] + jnp.dot(
                    p, v_aug, preferred_element_type=F32)
                m_sc[t, rows, :] = m_new

        @pl.when(j < i)
        def _():
            step(False)

        @pl.when(j == i)
        def _():
            step(True)
            for t in range(n):
                o_ref, lse_ref = outs[2 * t:2 * t + 2]
                l = acc_sc[t, :, DV:]
                o_ref[...] = (acc_sc[t, :, :DV] / l).astype(BF16)
                lse_ref[...] = m_sc[t] + jnp.log2(l)

    def qmap(g, i, j):
        return (g % H, (g // H) * nq + i, 0)

    def kmap(g, i, j):
        return (g % H, (g // H) * nq + jnp.minimum(j, i), 0)

    args = [a for st in streams for a in st[:3]]
    outs, landed = _call(
        body, args, name=name, grid=(B * H, nq, nq),
        in_specs=[pl.BlockSpec((None, tq, DK), qmap), pl.BlockSpec((None, tq, DK), kmap),
                  pl.BlockSpec((None, tq, DV), kmap)] * n,
        out_specs=[pl.BlockSpec((tq, DV), lambda g, i, j: ((g // H) * nq + i, g % H)),
                   pl.BlockSpec((None, tq, LANES), qmap)] * n,
        out_shape=[_sds((T, H * DV), BF16), _sds((H, T, LANES), F32)] * n,
        scratch_shapes=[pltpu.VMEM((n, tq, LANES), F32), pltpu.VMEM((n, tq, DV + LANES), F32)],
        sem=("parallel", "parallel", "arbitrary"), comm=comm)
    return [(outs[2 * t], outs[2 * t + 1]) for t in range(n)], landed


def _attn_bwd(streams, *, B, S, name, comm=None):
    n = len(streams)
    H, T, DK = streams[0][0].shape
    DV = streams[0][2].shape[2]
    tq = _tile(S, ATT_TILE)
    nq = S // tq
    sub = min(ATT_SUB, tq)
    NT = (((1,), (1,)), ((), ()))
    TN = (((0,), (0,)), ((), ()))

    def body(*refs):
        ins, outs = refs[:6 * n], refs[6 * n:]
        j, i = pl.program_id(1), pl.program_id(2)

        @pl.when(jnp.logical_and(j == 0, i == 0))
        def _():
            for t in range(n):
                outs[3 * t][...] = jnp.zeros(outs[3 * t].shape, F32)

        @pl.when(i == 0)
        def _():
            for t in range(n):
                outs[3 * t + 1][...] = jnp.zeros(outs[3 * t + 1].shape, F32)
                outs[3 * t + 2][...] = jnp.zeros(outs[3 * t + 2].shape, F32)

        def step(diagonal):
            work = [(t, r) for r in range(tq // sub) for t in range(n)]

            def kcols(r):
                return (r + 1) * sub if diagonal else tq

            def scores(t, r):
                q_ref, k_ref, v_ref, _, do_ref, _ = ins[6 * t:6 * t + 6]
                rows, kc = slice(r * sub, (r + 1) * sub), kcols(r)
                s = lax.dot_general(q_ref[rows, :], k_ref[0:kc, :], NT, preferred_element_type=F32)
                dp = lax.dot_general(do_ref[rows, :], v_ref[0:kc, :], NT, preferred_element_type=F32)
                return s * (streams[t][7] * LOG2_E), dp

            def probs(t, r, s, dp):
                _, _, _, o_ref, do_ref, lse_ref = ins[6 * t:6 * t + 6]
                rows, kc = slice(r * sub, (r + 1) * sub), kcols(r)
                if diagonal:
                    own = jnp.where(_visible(sub, sub, streams[t][6]), s[:, kc - sub:], NEG_INF)
                    s = own if kc == sub else jnp.concatenate([s[:, :kc - sub], own], axis=1)
                p = jnp.exp2(s - jnp.tile(lse_ref[rows, :], (1, kc // LANES)))
                delta = jnp.sum(do_ref[rows, :].astype(F32) * o_ref[rows, :].astype(F32), axis=1, keepdims=True)
                return p.astype(BF16), (p * (dp - delta) * streams[t][7]).astype(BF16)

            def grads(t, r, p, ds):
                q_ref, k_ref, _, _, do_ref, _ = ins[6 * t:6 * t + 6]
                dq_ref, dk_ref, dv_ref = outs[3 * t:3 * t + 3]
                rows, kc = slice(r * sub, (r + 1) * sub), kcols(r)
                dv_ref[0:kc, :] += lax.dot_general(p, do_ref[rows, :], TN, preferred_element_type=F32)
                dk_ref[0:kc, :] += lax.dot_general(ds, q_ref[rows, :], TN, preferred_element_type=F32)
                qrows = pl.ds(pl.multiple_of(i * tq + r * sub, sub), sub)
                dq_ref[qrows, :] += jnp.dot(ds, k_ref[0:kc, :], preferred_element_type=F32)

            nw = len(work)
            sc = {w: scores(*work[w]) for w in range(min(2, nw))}
            pr = {0: probs(*work[0], *sc.pop(0))}
            for w in range(nw):
                if w + 2 < nw:
                    sc[w + 2] = scores(*work[w + 2])
                if w + 1 < nw:
                    pr[w + 1] = probs(*work[w + 1], *sc.pop(w + 1))
                grads(*work[w], *pr.pop(w))

        @pl.when(i > j)
        def _():
            step(False)

        @pl.when(i == j)
        def _():
            step(True)

    def qmap(g, j, i):
        return (g % H, (g // H) * nq + jnp.maximum(i, j), 0)

    def kmap(g, j, i):
        return (g % H, (g // H) * nq + j, 0)

    def omap(g, j, i):
        return ((g // H) * nq + jnp.maximum(i, j), g % H)

    args = [a for st in streams for a in st[:6]]
    outs, landed = _call(
        body, args, name=name, grid=(B * H, nq, nq),
        in_specs=[pl.BlockSpec((None, tq, DK), qmap), pl.BlockSpec((None, tq, DK), kmap),
                  pl.BlockSpec((None, tq, DV), kmap), pl.BlockSpec((tq, DV), omap), pl.BlockSpec((tq, DV), omap),
                  pl.BlockSpec((None, tq, LANES), qmap)] * n,
        out_specs=[pl.BlockSpec((None, S, DK), lambda g, j, i: (g % H, g // H, 0)),
                   pl.BlockSpec((None, tq, DK), kmap), pl.BlockSpec((None, tq, DV), kmap)] * n,
        out_shape=[_sds((H, T, DK), F32), _sds((H, T, DK), F32), _sds((H, T, DV), F32)] * n,
        sem=("parallel", "arbitrary", "arbitrary"), comm=comm)
    return [tuple(outs[3 * t:3 * t + 3]) for t in range(n)], landed


def _branch_merge(o_m, o_f, w_m, w_f, proj, bgate, lay, D, comm=None):
    T, K = o_m.shape
    tm = _tile(T, 2 * ROW_TILE, 16)
    tn = _tile(D, 1024)

    def body(om_ref, of_ref, wm_ref, wf_ref, gm_ref, gf_ref, bm_ref, bf_ref, am_ref, af_ref, o_ref):
        am = jnp.dot(om_ref[...], wm_ref[...], preferred_element_type=F32)
        af = jnp.dot(of_ref[...], wf_ref[...], preferred_element_type=F32)
        am_ref[...] = am
        af_ref[...] = af
        sm = _sigmoid(gm_ref[...] + bm_ref[...])
        sf = _sigmoid(gf_ref[...] + bf_ref[...])
        o_ref[...] = (sm * am + sf * af).astype(BF16)

    og = lay["g"] // tn
    blk = pl.BlockSpec((tm, tn), lambda i, j: (i, j))
    lhs = pl.BlockSpec((tm, K), lambda i, j: (i, 0))
    rhs = pl.BlockSpec((K, tn), lambda i, j: (0, j))
    return _call(
        body, [o_m, o_f, w_m, w_f, proj, proj, bgate, bgate], name="mm_branch_merge", grid=(T // tm, D // tn),
        in_specs=[lhs, lhs, rhs, rhs, pl.BlockSpec((tm, tn), lambda i, j: (i, og + j)),
                  pl.BlockSpec((tm, tn), lambda i, j: (i, og + D // tn + j)),
                  pl.BlockSpec((1, tn), lambda i, j: (0, j)), pl.BlockSpec((1, tn), lambda i, j: (0, D // tn + j))],
        out_specs=[blk, blk, blk], out_shape=[_sds((T, D), F32), _sds((T, D), F32), _sds((T, D), BF16)],
        sem=("parallel", "parallel"), comm=comm)


def _out_mid(merged, w_out, x, g_pm, g_ffn):
    T, D = x.shape
    tm = _tile(T, 2 * ROW_TILE, 16)

    def body(a_ref, w_ref, x_ref, gp_ref, gf_ref, y_ref, x1_ref, h2_ref):
        y = jnp.dot(a_ref[...], w_ref[...], preferred_element_type=F32)
        y_ref[...] = y
        x1 = x_ref[...] + y * _rms(y) * gp_ref[...]
        x1_ref[...] = x1
        h2_ref[...] = (x1 * _rms(x1) * gf_ref[...]).astype(BF16)

    row = pl.BlockSpec((tm, D), lambda i: (i, 0))
    vec = pl.BlockSpec((1, D), lambda i: (0, 0))
    return _call(body, [merged, w_out, x, g_pm, g_ffn], name="mm_out_mid", grid=(T // tm,),
                 in_specs=[row, pl.BlockSpec((D, D), lambda i: (0, 0)), row, vec, vec], out_specs=[row, row, row],
                 out_shape=[_sds((T, D), F32), _sds((T, D), F32), _sds((T, D), BF16)], sem=("parallel",))[0]


def _conv3(u, w_ref, bias):
    row = lax.broadcasted_iota(jnp.int32, u.shape, 0)
    u1 = jnp.where(row >= 1, pltpu.roll(u, 1, 0), 0.0)
    u2 = jnp.where(row >= 2, pltpu.roll(u, 2, 0), 0.0)
    return w_ref[0:1, :] * u2 + w_ref[1:2, :] * u1 + w_ref[2:3, :] * u + bias, u1, u2


def _convffn_fwd(u, cw, cb, B, S, F, comm=None):
    T = u.shape[0]
    tn = _tile(F, 256)
    nf = F // tn

    def body(ug_ref, uv_ref, wg_ref, wv_ref, bg_ref, bv_ref, a_ref):
        g, _, _ = _conv3(ug_ref[...], wg_ref, bg_ref[...])
        val, _, _ = _conv3(uv_ref[...], wv_ref, bv_ref[...])
        a_ref[...] = (_gelu_parts(g)[0] * val).astype(BF16)

    def seq(off):
        return pl.BlockSpec((S, tn), lambda b, j: (b, off + j))

    def par(rows, off):
        return pl.BlockSpec((rows, tn), lambda b, j: (0, off + j))

    return _call(body, [u, u, cw, cw, cb, cb], name="convffn_fwd", grid=(B, nf),
                 in_specs=[seq(0), seq(nf), par(3, 0), par(3, nf), par(1, 0), par(1, nf)],
                 out_specs=[seq(0)], out_shape=[_sds((T, F), BF16)], sem=("parallel", "parallel"), comm=comm)


def _convffn_bwd(u, dact, cw, cb, B, S, F, comm=None):
    T = u.shape[0]
    tn = _tile(F, 256)
    nf = F // tn

    def body(ug_ref, uv_ref, da_ref, wg_ref, wv_ref, bg_ref, bv_ref, dug_ref, duv_ref, dpg_ref, dpv_ref):
        b = pl.program_id(1)
        ug, uv, da = ug_ref[...], uv_ref[...], da_ref[...]
        g, ug1, ug2 = _conv3(ug, wg_ref, bg_ref[...])
        val, uv1, uv2 = _conv3(uv, wv_ref, bv_ref[...])
        gel, dgel = _gelu_parts(g)
        dg = da * val * dgel
        dval = da * gel
        row = lax.broadcasted_iota(jnp.int32, ug.shape, 0)

        def back(d, w_ref):
            d1 = jnp.where(row < S - 1, pltpu.roll(d, S - 1, 0), 0.0)
            d2 = jnp.where(row < S - 2, pltpu.roll(d, S - 2, 0), 0.0)
            return w_ref[2:3, :] * d + w_ref[1:2, :] * d1 + w_ref[0:1, :] * d2

        dug_ref[...] = back(dg, wg_ref).astype(BF16)
        duv_ref[...] = back(dval, wv_ref).astype(BF16)

        def sums(d, u0, u1, u2):
            r8 = lax.broadcasted_iota(jnp.int32, (8, d.shape[1]), 0)
            out = jnp.zeros((8, d.shape[1]), F32)
            for k, t in enumerate((d * u2, d * u1, d * u0, d)):
                out = jnp.where(r8 == k, jnp.sum(t, axis=0, keepdims=True), out)
            return out

        _accumulate(dpg_ref, sums(dg, ug, ug1, ug2), b == 0)
        _accumulate(dpv_ref, sums(dval, uv, uv1, uv2), b == 0)

    def seq(off):
        return pl.BlockSpec((S, tn), lambda j, b: (b, off + j))

    def par(rows, off):
        return pl.BlockSpec((rows, tn), lambda j, b: (0, off + j))

    outs, landed = _call(
        body, [u, u, dact, cw, cw, cb, cb], name="convffn_bwd", grid=(nf, B),
        in_specs=[seq(0), seq(nf), seq(0), par(3, 0), par(3, nf), par(1, 0), par(1, nf)],
        out_specs=[seq(0), seq(0), par(8, 0), par(8, 0)],
        out_shape=[_sds((T, F), BF16), _sds((T, F), BF16), _sds((8, F), F32), _sds((8, F), F32)],
        sem=("parallel", "arbitrary"), comm=comm)
    return outs, landed


def _tail(ff, x1, tgt, g):
    T, D = ff.shape
    tm = _tile(T, ROW_TILE, 16)

    def body(ff_ref, x1_ref, t_ref, g_ref, dy_ref, dff_ref, loss_ref, dg_ref):
        i = pl.program_id(0)
        f = ff_ref[...]
        gv = g_ref[...]
        r = _rms(f)
        n = f * r
        e = (x1_ref[...] + n * gv) - t_ref[...]
        dy = e * (1.0 / D)
        dy_ref[...] = dy
        dn = dy * gv
        dff_ref[...] = (r * (dn - n * jnp.mean(dn * n, axis=-1, keepdims=True))).astype(BF16)
        part = 0.5 * jnp.sum(jnp.mean(e * e, axis=-1, keepdims=True), axis=0, keepdims=True)
        _accumulate(loss_ref, jnp.broadcast_to(part, loss_ref.shape), i == 0)
        _accumulate(dg_ref, jnp.sum(dy * n, axis=0, keepdims=True), i == 0)

    row = pl.BlockSpec((tm, D), lambda i: (i, 0))
    vec = pl.BlockSpec((1, D), lambda i: (0, 0))
    return _call(body, [ff, x1, tgt, g], name="tail", grid=(T // tm,), in_specs=[row, row, row, vec],
                 out_specs=[row, row, pl.BlockSpec((8, LANES), lambda i: (0, 0)), vec],
                 out_shape=[_sds((T, D), F32), _sds((T, D), BF16), _sds((8, LANES), F32), _sds((1, D), F32)],
                 sem=("arbitrary",))[0]


def _mid_bwd(dy, dh2, x1, y1, g_ffn, g_pm, comm=None):
    T, D = dy.shape
    tm = _tile(T, ROW_TILE, 16)

    def body(dy_ref, dh_ref, x1_ref, y1_ref, gf_ref, gp_ref, dx1_ref, dy1_ref, dgf_ref, dgp_ref):
        i = pl.program_id(0)
        dh = dh_ref[...]
        d2, dgf = _rms_bwd(dh, x1_ref[...], gf_ref[...])
        dx1 = dy_ref[...] + d2
        dx1_ref[...] = dx1
        d1, dgp = _rms_bwd(dx1, y1_ref[...], gp_ref[...])
        dy1_ref[...] = d1.astype(BF16)
        _accumulate(dgf_ref, jnp.sum(dgf, axis=0, keepdims=True), i == 0)
        _accumulate(dgp_ref, jnp.sum(dgp, axis=0, keepdims=True), i == 0)

    row = pl.BlockSpec((tm, D), lambda i: (i, 0))
    vec = pl.BlockSpec((1, D), lambda i: (0, 0))
    return _call(body, [dy, dh2, x1, y1, g_ffn, g_pm], name="mid_bwd", grid=(T // tm,),
                 in_specs=[row, row, row, row, vec, vec], out_specs=[row, row, vec, vec],
                 out_shape=[_sds((T, D), F32), _sds((T, D), BF16), _sds((1, D), F32), _sds((1, D), F32)],
                 sem=("arbitrary",), comm=comm)


def _gate_bwd(dy1, w_out, am, af, proj, bgate, lay, D, comm=None):
    T = dy1.shape[0]
    tm = _tile(T, 2 * ROW_TILE, 16)
    tn = _tile(D, 1024)
    NT = (((1,), (1,)), ((), ()))

    def body(dy_ref, w_ref, am_ref, af_ref, gm_ref, gf_ref, bm_ref, bf_ref,
             dam_ref, daf_ref, dgm_ref, dgf_ref, dbm_ref, dbf_ref):
        i = pl.program_id(1)
        d = lax.dot_general(dy_ref[...], w_ref[...], NT, preferred_element_type=F32)
        sm = _sigmoid(gm_ref[...] + bm_ref[...])
        sf = _sigmoid(gf_ref[...] + bf_ref[...])
        dam_ref[...] = (d * sm).astype(BF16)
        daf_ref[...] = (d * sf).astype(BF16)
        dgm = d * am_ref[...] * (sm * (1.0 - sm))
        dgf = d * af_ref[...] * (sf * (1.0 - sf))
        dgm_ref[...] = dgm.astype(BF16)
        dgf_ref[...] = dgf.astype(BF16)
        _accumulate(dbm_ref, jnp.sum(dgm, axis=0, keepdims=True), i == 0)
        _accumulate(dbf_ref, jnp.sum(dgf, axis=0, keepdims=True), i == 0)

    og = lay["g"] // tn
    blk = pl.BlockSpec((tm, tn), lambda j, i: (i, j))
    vec = pl.BlockSpec((1, tn), lambda j, i: (0, j))
    return _call(
        body, [dy1, w_out, am, af, proj, proj, bgate, bgate], name="mm_dmerged_gate_bwd", grid=(D // tn, T // tm),
        in_specs=[pl.BlockSpec((tm, D), lambda j, i: (i, 0)), pl.BlockSpec((tn, D), lambda j, i: (j, 0)),
                  blk, blk, pl.BlockSpec((tm, tn), lambda j, i: (i, og + j)),
                  pl.BlockSpec((tm, tn), lambda j, i: (i, og + D // tn + j)),
                  vec, pl.BlockSpec((1, tn), lambda j, i: (0, D // tn + j))],
        out_specs=[blk, blk, blk, blk, vec, vec],
        out_shape=[_sds((T, D), BF16)] * 4 + [_sds((1, D), F32)] * 2, sem=("parallel", "arbitrary"), comm=comm)


def _mla_bwd_prep(dq, dk, dv, cosT, sinT, comm=None):
    H, T, _ = dq.shape
    tm = _tile(T, HEAD_ROW_TILE, 16)

    def body(dq_ref, dk_ref, dv_ref, cos_ref, sin_ref, dqr_ref, dkv_ref, dkpe_ref):
        h = pl.program_id(1)
        cs, sn = cos_ref[...], sin_ref[...]
        valid = _lane(cs.shape) < ROPE

        def unrope(d):
            d = jnp.where(valid, d, 0.0)
            return d * cs - _rope_rot(d) * sn

        dqv = dq_ref[...]
        dqr_ref[:, :NOPE] = dqv[:, :NOPE].astype(BF16)
        dqr_ref[:, NOPE:] = unrope(dqv[:, NOPE:]).astype(BF16)
        dkv_ = dk_ref[...]
        dkv_ref[:, :NOPE] = dkv_[:, :NOPE].astype(BF16)
        dkv_ref[:, NOPE:] = dv_ref[...].astype(BF16)
        _accumulate(dkpe_ref, unrope(dkv_[:, NOPE:]), h == 0)

    head = pl.BlockSpec((None, tm, ATT_DK), lambda i, h: (h, i, 0))
    tok = pl.BlockSpec((tm, LANES), lambda i, h: (i, 0))
    return _call(
        body, [dq, dk, dv, cosT, sinT], name="mla_bwd_prep", grid=(T // tm, H),
        in_specs=[head, head, pl.BlockSpec((None, tm, VDIM), lambda i, h: (h, i, 0)), tok, tok],
        out_specs=[head, head, tok],
        out_shape=[_sds((H, T, ATT_DK), BF16), _sds((H, T, ATT_DK), BF16), _sds((T, LANES), F32)],
        sem=("parallel", "arbitrary"), comm=comm)


def _fox_bwd_prep(dq, dk, proj, bfor, lay, B, S, inv_scale):
    H, T, _ = dq.shape

    def body(dq_ref, dk_ref, fl_ref, bf_ref, dfl_ref, dbf_ref, dc_sc):
        b, h = pl.program_id(0), pl.program_id(1)
        lane = _lane(dc_sc.shape)
        col = jnp.sum(jnp.where(lane == 0, dq_ref[...], 0.0) - jnp.where(lane == 3, dk_ref[...], 0.0),
                      axis=1, keepdims=True)

        @pl.when(h == 0)
        def _():
            dc_sc[...] = jnp.zeros(dc_sc.shape, F32)

        dc_sc[...] = jnp.where(lane == h, col, dc_sc[...])

        @pl.when(h == H - 1)
        def _():
            dlogf = _cumsum_rows(dc_sc[...] * inv_scale, reverse=True)
            z = fl_ref[...] + bf_ref[...]
            dz = jnp.where(lane < H, dlogf * (1.0 / (1.0 + jnp.exp(z))), 0.0)
            dfl_ref[...] = dz
            _accumulate(dbf_ref, jnp.sum(dz, axis=0, keepdims=True), b == 0)

    aug = pl.BlockSpec((None, S, LANES), lambda b, h: (h, b, 1))
    seq = pl.BlockSpec((S, LANES), lambda b, h: (b, 0))
    vec = pl.BlockSpec((1, LANES), lambda b, h: (0, 0))
    return _call(
        body, [dq, dk, proj, bfor], name="fox_bwd_prep", grid=(B, H),
        in_specs=[aug, aug, pl.BlockSpec((S, LANES), lambda b, h: (b, lay["fl"] // LANES)), vec],
        out_specs=[seq, vec], out_shape=[_sds((T, LANES), F32), _sds((1, LANES), F32)],
        scratch_shapes=[pltpu.VMEM((S, LANES), F32)], sem=("arbitrary", "arbitrary"))[0]


def _lat_bwd(dqn, dkvn, proj, gq, gkv, lay):
    T = dqn.shape[0]
    tm = _tile(T, ROW_TILE, 16)

    def body(dq_ref, dkv_ref, q_ref, kv_ref, gq_ref, gkv_ref, dql_ref, dkl_ref, dgq_ref, dgkv_ref):
        i = pl.program_id(0)
        dql, dgq = _rms_bwd(dq_ref[...], q_ref[...], gq_ref[...])
        dkl, dgkv = _rms_bwd(dkv_ref[...], kv_ref[...], gkv_ref[...])
        dql_ref[...] = dql.astype(BF16)
        dkl_ref[...] = dkl.astype(BF16)
        _accumulate(dgq_ref, jnp.sum(dgq, axis=0, keepdims=True), i == 0)
        _accumulate(dgkv_ref, jnp.sum(dgkv, axis=0, keepdims=True), i == 0)

    def blk(width, off=0):
        return pl.BlockSpec((tm, width), lambda i: (i, off // width))

    def vec(width):
        return pl.BlockSpec((1, width), lambda i: (0, 0))

    return _call(
        body, [dqn, dkvn, proj, proj, gq, gkv], name="lat_bwd", grid=(T // tm,),
        in_specs=[blk(Q_LORA), blk(KV_LORA), blk(Q_LORA, lay["q"]), blk(KV_LORA, lay["kv"]), vec(Q_LORA), vec(KV_LORA)],
        out_specs=[blk(Q_LORA), blk(KV_LORA), vec(Q_LORA), vec(KV_LORA)],
        out_shape=[_sds((T, Q_LORA), BF16), _sds((T, KV_LORA), BF16), _sds((1, Q_LORA), F32), _sds((1, KV_LORA), F32)],
        sem=("arbitrary",))[0]


def _dh_final(dproj, w_perm, dx1, x, g, comm=None):
    T, D = x.shape
    K = dproj.shape[1]
    tm = _tile(T, 2 * ROW_TILE, 16)
    tk = _tile(K, 1024)
    nk = K // tk
    NT = (((1,), (1,)), ((), ()))

    def body(a_ref, b_ref, dx1_ref, x_ref, g_ref, dx_ref, dg_ref):
        i, k = pl.program_id(0), pl.program_id(1)

        @pl.when(k == 0)
        def _():
            dx_ref[...] = jnp.zeros(dx_ref.shape, F32)

        dx_ref[...] += lax.dot_general(a_ref[...], b_ref[...], NT, preferred_element_type=F32)

        @pl.when(k == nk - 1)
        def _():
            d, dg = _rms_bwd(dx_ref[...], x_ref[...], g_ref[...])
            dx_ref[...] = dx1_ref[...] + d
            _accumulate(dg_ref, jnp.sum(dg, axis=0, keepdims=True), i == 0)

    row = pl.BlockSpec((tm, D), lambda i, k: (i, 0))
    vec = pl.BlockSpec((1, D), lambda i, k: (0, 0))
    return _call(body, [dproj, w_perm, dx1, x, g], name="mm_dh_final", grid=(T // tm, nk),
                 in_specs=[pl.BlockSpec((tm, tk), lambda i, k: (i, k)), pl.BlockSpec((D, tk), lambda i, k: (0, k)),
                           row, row, vec],
                 out_specs=[row, vec], out_shape=[_sds((T, D), F32), _sds((1, D), F32)],
                 sem=("arbitrary", "arbitrary"), comm=comm)


def _chip_sum(pieces, paired, qc, name):
    G, R, C = pieces.shape
    tr = _tile(R, 256, 16)

    def body(qc_ref, g_ref, p_ref, keep_ref, send_ref):
        s = pl.program_id(1)
        tot = g_ref[...] + p_ref[...]

        @pl.when(s == 0)
        def _():
            keep_ref[...] = tot

        @pl.when(s > 0)
        def _():
            send_ref[...] = tot.astype(send_ref.dtype)

    grid_spec = pltpu.PrefetchScalarGridSpec(
        num_scalar_prefetch=1, grid=(R // tr, N_CHIP),
        in_specs=[pl.BlockSpec((None, tr, C), lambda i, s, qc: (2 * (qc[0] ^ s) + qc[1], i, 0)),
                  pl.BlockSpec((None, tr, C), lambda i, s, qc: (qc[0] ^ s, i, 0))],
        out_specs=[pl.BlockSpec((tr, C), lambda i, s, qc: (i, 0)),
                   pl.BlockSpec((None, tr, C), lambda i, s, qc: (jnp.maximum(s - 1, 0), i, 0))])
    send_dtype = BF16 if R >= 16 else pieces.dtype
    return pl.pallas_call(
        body, name=name, grid_spec=grid_spec,
        out_shape=[_sds((R, C), F32), _sds((3, R, C), send_dtype)],
        compiler_params=pltpu.CompilerParams(dimension_semantics=("arbitrary", "arbitrary"),
                                             vmem_limit_bytes=VMEM_LIMIT_BYTES),
    )(qc, pieces, paired)


def _adamw_math(w, g, m, v):
    m = ADAM_B1 * m + (1.0 - ADAM_B1) * g
    v = ADAM_B2 * v + (1.0 - ADAM_B2) * (g * g)
    m_hat = m / (1.0 - ADAM_B1 ** ADAM_STEP)
    v_hat = v / (1.0 - ADAM_B2 ** ADAM_STEP)
    delta = -ADAM_LR * (m_hat / (jnp.sqrt(v_hat) + ADAM_EPS) + ADAM_WD * w)
    return delta, m, v


def _sum_adamw(keep, pieces, w, m, v, name):
    R, C = w.shape
    P = pieces.shape[0]
    tr = _tile(R, 256, 16)

    def body(k_ref, p_ref, w_ref, m_ref, v_ref, g_ref, d_ref, mo_ref, vo_ref):
        g = k_ref[...]
        for q in range(P):
            g = g + p_ref[q].astype(F32)
        g_ref[...] = g
        d_ref[...], mo_ref[...], vo_ref[...] = _adamw_math(w_ref[...], g, m_ref[...], v_ref[...])

    blk = pl.BlockSpec((tr, C), lambda i: (i, 0))
    pblk = pl.BlockSpec((P, tr, C), lambda i: (0, i, 0))
    return _call(body, [keep, pieces, w, m, v], name=name, grid=(R // tr,), in_specs=[blk, pblk, blk, blk, blk],
                 out_specs=[blk] * 4, out_shape=[_sds((R, C), F32)] * 4, sem=("parallel",))[0]


def _adamw_small(parts, w, m, v, widths):
    n = len(widths)

    def body(p_ref, w_ref, m_ref, v_ref, *o_refs):
        g = p_ref[0]
        for q in range(1, N_DEV):
            g = g + p_ref[q]
        vals = (g,) + _adamw_math(w_ref[...], g, m_ref[...], v_ref[...])
        off = 0
        for i, wd in enumerate(widths):
            for kind in range(4):
                o_refs[4 * i + kind][...] = vals[kind][:, off:off + wd]
            off += wd

    whole = pl.BlockSpec(memory_space=pltpu.VMEM)
    outs = _call(body, [parts, w, m, v], name="adamw_small", grid=(), in_specs=[whole] * 4,
                 out_specs=[whole] * (4 * n), out_shape=[_sds((1, wd), F32) for wd in widths for _ in range(4)])[0]
    return [tuple(outs[4 * i:4 * i + 4]) for i in range(n)]


def _layout(D):
    lay = {"q": 0, "kv": Q_LORA, "kpe": Q_LORA + KV_LORA}
    lay["fq"] = lay["kpe"] + LANES
    lay["fk"] = lay["fq"] + HEADS * FOX_DIM
    lay["fv"] = lay["fk"] + HEADS * FOX_DIM
    lay["fl"] = lay["fv"] + HEADS * FOX_DIM
    lay["g"] = lay["fl"] + LANES
    lay["end"] = lay["g"] + 2 * D
    return lay


def kernel(x, positions, pre_mix_norm, w_in, q_a_norm, w_uq, kv_a_norm, w_ukv, b_forget, b_gate, w_branch_mla, w_branch_fox, w_out, post_mix_norm, pre_ffn_norm, w_up, conv_w, conv_b, w_down, post_ffn_norm, loss_target, m_pre_mix_norm, m_w_in, m_q_a_norm, m_w_uq, m_kv_a_norm, m_w_ukv, m_b_forget, m_b_gate, m_w_branch_mla, m_w_branch_fox, m_w_out, m_post_mix_norm, m_pre_ffn_norm, m_w_up, m_conv_w, m_conv_b, m_w_down, m_post_ffn_norm, v_pre_mix_norm, v_w_in, v_q_a_norm, v_w_uq, v_kv_a_norm, v_w_ukv, v_b_forget, v_b_gate, v_w_branch_mla, v_w_branch_fox, v_w_out, v_post_mix_norm, v_pre_ffn_norm, v_w_up, v_conv_w, v_conv_b, v_w_down, v_post_ffn_norm):
    B, S, D = x.shape
    T = B * S
    F = conv_b.shape[0] // 2
    lay = _layout(D)
    n_in = w_in.shape[1]
    d_in = N_DEV * n_in
    seg_a = Q_LORA + KV_LORA + ROPE
    seg_b = 3 * HEADS * FOX_DIM + HEADS
    mla_scale = (NOPE + ROPE) ** -0.5
    fox_scale = FOX_DIM ** -0.5
    ax, ay, ac = (lax.axis_index(a) for a in MESH_AXES)
    qc = jnp.stack([2 * ax + ay, ac]).astype(jnp.int32)

    def row(vec, width=None):
        vec = vec.reshape(1, -1)
        if width is not None and vec.shape[1] < width:
            vec = jnp.pad(vec, ((0, 0), (0, width - vec.shape[1])))
        return vec

    x2 = x.reshape(T, D)
    win_s = _cast_bf16(w_in, "cast_w_in")
    h, (win_g,) = _prenorm(x2, row(pre_mix_norm), comm=_Comm([_GatherRelayPlan([win_s], mid_frac=0.3)]))
    small_s = [_cast_bf16(w, "cast_" + n) for w, n in
               [(w_uq, "w_uq"), (w_ukv, "w_ukv"), (w_branch_mla, "w_branch_mla"), (w_branch_fox, "w_branch_fox"), (w_out, "w_out")]]
    wup_s = _cast_bf16(w_up, "cast_w_up")
    wdown_s = _cast_bf16(w_down, "cast_w_down")

    def shard_cols(lo, hi):
        out = []
        for g in range(lo // n_in, (hi - 1) // n_in + 1):
            out.append(win_g[g][:, max(lo, g * n_in) - g * n_in:min(hi, (g + 1) * n_in) - g * n_in])
        return out

    w_perm = jnp.concatenate(
        shard_cols(0, seg_a) + [jnp.zeros((D, LANES - ROPE), BF16)] + shard_cols(seg_a, seg_a + seg_b)
        + [jnp.zeros((D, LANES - HEADS), BF16)] + shard_cols(seg_a + seg_b, d_in), axis=1)

    tgt = loss_target.reshape(T, D)
    pos = positions.reshape(T, 1)
    inv_freq = 1.0 / (ROPE_THETA ** (jnp.arange(0, ROPE, 2, dtype=F32) / ROPE))
    invf = row(jnp.concatenate([inv_freq, inv_freq]), LANES)
    g_pre, g_q, g_kv = row(pre_mix_norm), row(q_a_norm), row(kv_a_norm)
    g_pm, g_ffn, g_pf = row(post_mix_norm), row(pre_ffn_norm), row(post_ffn_norm)
    bfor = row(b_forget, LANES)
    bgate = row(b_gate)
    cb_full = row(conv_b)

    def own_plan(blocks):
        return _Comm([_GatherOwnPlan(blocks)])

    def pass_plan(gathered):
        return _Comm([_GatherPassPlan(gathered)])

    def pair_plan(gs):
        return _Comm([_PairScatterPlan(gs)])

    def chip_plan(gs):
        return _Comm([_ChipScatterPlan(gs)])

    half_d = D // 2
    proj, landed = _matmul(h, w_perm, mode="nn", name="mm_proj", comm=_Comm(
        [_GatherOwnPlan(small_s[:2] + [conv_w]), _GatherOwnPlan([wup_s], rows=(0, half_d))]))
    early_g, wup_part = landed[:-1], landed[-1:]
    (qn, kvn, kper, logf, cosT, sinT), (wuq_g, wukv_g, cw_g) = _split_prep(
        proj, pos, invf, g_q, g_kv, bfor, lay, comm=pass_plan(early_g))
    wuq_pad = jnp.pad(wuq_g, ((0, 0), (0, 0), (0, ATT_DK - NOPE - ROPE)))
    cw_full = jnp.transpose(cw_g, (1, 0, 2)).reshape(3, 2 * F)

    qraw = _matmul(qn, wuq_pad, mode="nn", name="mm_q", out_blocks=ATT_DK, tm=T)
    kvraw = _matmul(kvn, wukv_g, mode="nn", name="mm_kv", out_blocks=NOPE + VDIM, tm=T)
    (q_mla, k_mla, v_mla), branch_half = _mla_prep(qraw, kvraw, kper, cosT, sinT, comm=own_plan(small_s[2:4]))
    cs = _fox_cumsum(logf, B, S, 1.0 / fox_scale)
    (q_fox, k_fox, v_fox), wout_half = _fox_prep(proj, cs, lay, comm=own_plan(small_s[4:5]))
    ((o_mla, lse_mla), (o_fox, lse_fox)), landed = _attn_fwd(
        [(q_mla, k_mla, v_mla, MLA_UNIT, mla_scale), (q_fox, k_fox, v_fox, 1, fox_scale)], B=B, S=S,
        name="attn_fwd", comm=_Comm([_GatherOwnPlan([wup_s], rows=(half_d, D), into=wup_part),
                                     _GatherPassPlan(branch_half + wout_half)]))
    wup_half, (wbm_g, wbf_g, wout_g) = landed[:1], landed[1:]
    wbm = jnp.transpose(wbm_g, (1, 0, 2)).reshape(HEADS * VDIM, D)
    wbf = jnp.transpose(wbf_g, (1, 0, 2)).reshape(HEADS * FOX_DIM, D)
    wout = wout_g.reshape(D, D)
    (a_m, a_f, merged), (wup_g,) = _branch_merge(o_mla, o_fox, wbm, wbf, proj, bgate, lay, D,
                                                 comm=pass_plan(wup_half))
    n_up = wup_g.shape[2]
    y1, x1, h2 = _out_mid(merged, wout, x2, g_pm, g_ffn)
    u, wdown_half = _matmul(h2, wup_g, mode="nn", name="mm_up", tn=n_up, comm=own_plan([wdown_s]))
    (act,), (wdown_g,) = _convffn_fwd(u, cw_full, cb_full, B, S, F, comm=pass_plan(wdown_half))
    wdown = wdown_g.reshape(F, D)
    ff = _matmul(act, wdown, mode="nn", name="mm_down", tk=F // 2)
    dy, dff, loss_part, dg_pf = _tail(ff, x1, tgt, g_pf)

    dact = _matmul(dff, wdown, mode="nt", name="mm_dact", tn=F // 4)
    dw_down = _matmul(act, dff, mode="tn", name="mm_dw_down", tm=F // 4, tn=512).reshape(N_DEV, F // N_DEV, D)
    (du_g, du_v, dcp_g, dcp_v), (pa_down,) = _convffn_bwd(u, dact, cw_full, cb_full, B, S, F, comm=pair_plan([dw_down]))
    keep_down, sb_down = _chip_sum(dw_down, pa_down, qc, "chipsum_w_down")
    dh2, (rb_down,) = _matmul_halves((du_g, du_v), wup_g, mode="nt", name="mm_dh2", tm=MM_TILE, comm=chip_plan([sb_down]))
    dw_up = _matmul_halves(h2, (du_g, du_v), mode="tn", name="mm_dw_up", tm=MM_TILE, tn=n_up, tk=T // 2)
    (dx1, dy1, dg_ffn, dg_pm), _ = _mid_bwd(dy, dh2, x1, y1, g_ffn, g_pm)
    dw_out, pa_up_part = _matmul(merged, dy1, mode="tn", name="mm_dw_out",
                                 comm=_Comm([_PairScatterPlan([dw_up], rows=(0, half_d))]))
    dw_out = dw_out.reshape(N_DEV, D // N_DEV, D)
    (da_m, da_f, dgl_m, dgl_f, dbg_m, dbg_f), (pa_up,) = _gate_bwd(
        dy1, wout, a_m, a_f, proj, bgate, lay, D,
        comm=_Comm([_PairScatterPlan([dw_up], rows=(half_d, D), into=pa_up_part)]))
    keep_up, sb_up = _chip_sum(dw_up, pa_up, qc, "chipsum_w_up")
    dw_bm = _matmul(o_mla, da_m, mode="tn", name="mm_dw_branch_mla", out_blocks=D // N_DEV)
    dw_bf = _matmul(o_fox, da_f, mode="tn", name="mm_dw_branch_fox", out_blocks=D // N_DEV)
    mix = [dw_out, dw_bm, dw_bf]
    do_mla, pa_mix = _matmul(da_m, wbm, mode="nt", name="mm_do_mla", out_dtype=BF16, comm=pair_plan(mix))
    do_fox = _matmul(da_f, wbf, mode="nt", name="mm_do_fox", out_dtype=BF16)
    mix_sums = [_chip_sum(g, p, qc, "chipsum_" + n) for g, p, n in zip(mix, pa_mix, ["w_out", "w_branch_mla", "w_branch_fox"])]
    ((dq_m, dk_m, dv_m), (dq_f, dk_f, dv_f)), (rb_up,) = _attn_bwd(
        [(q_mla, k_mla, v_mla, o_mla, do_mla, lse_mla, MLA_UNIT, mla_scale),
         (q_fox, k_fox, v_fox, o_fox, do_fox, lse_fox, 1, fox_scale)], B=B, S=S, name="attn_bwd",
        comm=chip_plan([sb_up]))
    (dqraw, dkvraw, dkpe), _ = _mla_bwd_prep(dq_m, dk_m, dv_m, cosT, sinT)
    dqn = _matmul(dqraw, wuq_pad, mode="nt", name="mm_dqn", tm=T)
    dw_uq = _matmul(qn, dqraw, mode="tn", name="mm_dw_uq", out_blocks=ATT_DK)[:, :, :NOPE + ROPE]
    dkvn = _matmul(dkvraw, wukv_g, mode="nt", name="mm_dkvn", tm=T)
    dw_ukv = _matmul(kvn, dkvraw, mode="tn", name="mm_dw_ukv", out_blocks=NOPE + VDIM)
    dqlat, dkvlat, dg_q, dg_kv = _lat_bwd(dqn, dkvn, proj, g_q, g_kv, lay)
    dfl, dbfor = _fox_bwd_prep(dq_f, dk_f, proj, bfor, lay, B, S, 1.0 / fox_scale)
    dproj = _concat_cols([dqlat, dkvlat, dkpe, dq_f, dk_f, dv_f, dfl, dgl_m, dgl_f], "concat_dproj")
    dw_perm, rb_mix = _matmul(h, dproj, mode="tn", name="mm_dw_in", comm=chip_plan([s[1] for s in mix_sums]))
    segs = [(0, seg_a, 0), (seg_a, seg_a + seg_b, lay["fq"] - seg_a), (seg_a + seg_b, d_in, lay["g"] - seg_a - seg_b)]

    def piece(g):
        lo, hi = g * n_in, (g + 1) * n_in
        parts = [dw_perm[:, max(lo, s0) + sh:min(hi, s1) + sh] for s0, s1, sh in segs if max(lo, s0) < min(hi, s1)]
        return parts[0] if len(parts) == 1 else jnp.concatenate(parts, axis=1)

    dw_in = jnp.stack([piece(g) for g in range(N_DEV)])
    dcw = jnp.transpose(jnp.concatenate([dcp_g[0:3], dcp_v[0:3]], axis=1).reshape(3, N_DEV, (2 * F) // N_DEV), (1, 0, 2))
    late = [dw_in, dw_uq, dw_ukv, dcw]
    pa_late = _exchange_alone(pair_plan(late), "pair_late")
    late_sums = [_chip_sum(g, p, qc, "chipsum_" + n) for g, p, n in zip(late, pa_late, ["w_in", "w_uq", "w_ukv", "conv_w"])]
    (grad_x, dg_pre), rb_late = _dh_final(dproj, w_perm, dx1, x2, g_pre, comm=chip_plan([s[1] for s in late_sums]))

    big_out = {}

    def finish(n, keep, pieces, w, m, v):
        big_out[n] = _sum_adamw(keep, pieces, w, m, v, "adamw_" + n)

    finish("w_down", keep_down, rb_down, w_down, m_w_down, v_w_down)
    finish("w_up", keep_up, rb_up, w_up, m_w_up, v_w_up)
    finish("w_out", mix_sums[0][0], rb_mix[0], w_out, m_w_out, v_w_out)
    finish("w_branch_mla", mix_sums[1][0], rb_mix[1], w_branch_mla, m_w_branch_mla, v_w_branch_mla)
    finish("w_branch_fox", mix_sums[2][0], rb_mix[2], w_branch_fox, m_w_branch_fox, v_w_branch_fox)
    finish("w_in", late_sums[0][0], rb_late[0], w_in, m_w_in, v_w_in)
    finish("w_uq", late_sums[1][0], rb_late[1], w_uq, m_w_uq, v_w_uq)
    finish("w_ukv", late_sums[2][0], rb_late[2], w_ukv, m_w_ukv, v_w_ukv)
    finish("conv_w", late_sums[3][0], rb_late[3], conv_w, m_conv_w, v_conv_w)

    widths = [D, Q_LORA, KV_LORA, LANES, 2 * D, D, D, 2 * F, D]
    small_names = ["pre_mix_norm", "q_a_norm", "kv_a_norm", "b_forget", "b_gate", "post_mix_norm", "pre_ffn_norm",
                   "conv_b", "post_ffn_norm"]
    true_w = [D, Q_LORA, KV_LORA, HEADS, 2 * D, D, D, 2 * F, D]
    dcb = jnp.concatenate([dcp_g[3:4], dcp_v[3:4]], axis=1)
    part = jnp.concatenate([dg_pre, dg_q, dg_kv, dbfor, dbg_m, dbg_f, dg_pm, dg_ffn, dcb, dg_pf], axis=1)

    def pack(vals):
        return jnp.concatenate([row(a, wd) for a, wd in zip(vals, widths)], axis=1)

    sw = pack([pre_mix_norm, q_a_norm, kv_a_norm, b_forget, b_gate, post_mix_norm, pre_ffn_norm, conv_b, post_ffn_norm])
    sm = pack([m_pre_mix_norm, m_q_a_norm, m_kv_a_norm, m_b_forget, m_b_gate, m_post_mix_norm, m_pre_ffn_norm,
               m_conv_b, m_post_ffn_norm])
    sv = pack([v_pre_mix_norm, v_q_a_norm, v_kv_a_norm, v_b_forget, v_b_gate, v_post_mix_norm, v_pre_ffn_norm,
               v_conv_b, v_post_ffn_norm])
    (parts_all,) = _exchange_alone(_Comm([_DirectGatherPlan([part])]), "gather_small")
    small = _adamw_small(parts_all, sw, sm, sv, widths)
    small_out = {n: tuple(a.reshape(-1)[:tw] for a in vals) for n, vals, tw in zip(small_names, small, true_w)}

    loss = lax.psum(loss_part[0, 0], MESH_AXES)
    order = ["pre_mix_norm", "w_in", "q_a_norm", "w_uq", "kv_a_norm", "w_ukv", "b_forget", "b_gate", "w_branch_mla",
             "w_branch_fox", "w_out", "post_mix_norm", "pre_ffn_norm", "w_up", "conv_w", "conv_b", "w_down",
             "post_ffn_norm"]
    res = {**big_out, **small_out}
    outs = [loss, grad_x.reshape(B, S, D)]
    for kind in range(4):
        outs += [res[n][kind] for n in order]
    return tuple(outs)
```

```python
import math

import jax
import jax.numpy as jnp
from jax import lax
from jax.experimental import pallas as pl
from jax.experimental.pallas import tpu as pltpu

F32 = jnp.float32
BF16 = jnp.bfloat16

N_DEV = 8
N_CHIP = 4
HEADS = 8
NOPE = 128
ROPE = 64
HALF_ROPE = ROPE // 2
VDIM = 128
Q_LORA = 512
KV_LORA = 256
FOX_DIM = 128
ATT_DK = 256
MLA_UNIT = 64
ROPE_THETA = 10000.0
EPS = 1e-6
NEG_INF = -1e30
LANES = 128
LOG2_E = 1.4426950408889634

ADAM_LR = 0.001
ADAM_B1 = 0.9
ADAM_B2 = 0.999
ADAM_EPS = 1e-08
ADAM_WD = 0.01
ADAM_STEP = 10

VMEM_LIMIT_BYTES = 56 * 1024 * 1024
ROW_TILE = 256
HEAD_ROW_TILE = 2048
ATT_TILE = 1024
ATT_SUB = 256
ATT_AHEAD = 3
MM_TILE = 1024

MESH_AXES = ("x", "y", "c")
ANY = pl.BlockSpec(memory_space=pl.ANY)


def _tile(n, pref, align=LANES):
    if n <= pref:
        return n
    t = (pref // align) * align
    while t >= align:
        if n % t == 0:
            return t
        t -= align
    return n


def _sds(shape, dtype):
    return jax.ShapeDtypeStruct(shape, dtype)


def _coords():
    x, y, c = (lax.axis_index(ax) for ax in MESH_AXES)
    return x, y, c


def _chip_rel(x, y, r):
    return (1 - x if r & 2 else x), (1 - y if r & 1 else y)


def _rcopy(src, dst, sems, w, k, dev):
    return pltpu.make_async_remote_copy(src_ref=src, dst_ref=dst, send_sem=sems[0].at[w, k], recv_sem=sems[1].at[w, k],
                                        device_id=dev, device_id_type=pl.DeviceIdType.MESH)


class _GatherRelayPlan:
    def __init__(self, blocks, mid_frac=0.5):
        self.ins = list(blocks)
        self.out_shapes = [_sds((N_DEV,) + b.shape, b.dtype) for b in blocks]
        n = len(blocks)
        self.scratch = [pltpu.SemaphoreType.DMA((n, 7)), pltpu.SemaphoreType.DMA((n, 7)), pltpu.SemaphoreType.DMA((n,))]
        self.mid_frac = mid_frac

    @staticmethod
    def _places():
        x, y, c = _coords()
        xn, yn = 4 * (1 - x) + 2 * y, 4 * x + 2 * (1 - y)
        relay_src = 4 * (x + c * (1 - 2 * x)) + 2 * (y + (1 - c) * (1 - 2 * y)) + c
        relay_to = (x + (1 - c) * (1 - 2 * x), y + c * (1 - 2 * y), c)
        return x, y, c, xn, yn, relay_src, relay_to, 4 * (1 - x) + 2 * (1 - y)

    def first(self, ins, outs, sems):
        x, y, c, _, _, _, _, _ = self._places()
        me = 4 * x + 2 * y + c
        for w in range(len(ins)):
            pltpu.make_async_copy(ins[w], outs[w].at[me], sems[2].at[w]).start()
            _rcopy(ins[w], outs[w].at[me], sems, w, 0, (x, y, 1 - c)).start()
            _rcopy(ins[w], outs[w].at[me], sems, w, 1, (1 - x, y, c)).start()
            _rcopy(ins[w], outs[w].at[me], sems, w, 2, (x, 1 - y, c)).start()

    def mid(self, ins, outs, sems):
        x, y, c, xn, yn, relay_src, relay_to, _ = self._places()
        sib = (x, y, 1 - c)
        for w in range(len(ins)):
            bx, by = outs[w].at[xn + c], outs[w].at[yn + c]
            _rcopy(ins[w], bx, sems, w, 1, (1 - x, y, c)).wait_recv()
            _rcopy(ins[w], by, sems, w, 2, (x, 1 - y, c)).wait_recv()
            _rcopy(outs[w].at[relay_src], outs[w].at[relay_src], sems, w, 3, relay_to).start()
            _rcopy(bx, bx, sems, w, 4, sib).start()
            _rcopy(by, by, sems, w, 5, sib).start()

    def last(self, ins, outs, sems):
        x, y, c, xn, yn, _, relay_to, dg = self._places()
        me = 4 * x + 2 * y + c
        sib = (x, y, 1 - c)
        for w in range(len(ins)):
            bd = outs[w].at[dg + c]
            _rcopy(ins[w], bd, sems, w, 3, relay_to).wait_recv()
            _rcopy(bd, bd, sems, w, 6, sib).start()
            for k, blk in ((0, 4 * x + 2 * y), (4, xn), (5, yn), (6, dg)):
                _rcopy(ins[w], outs[w].at[blk + 1 - c], sems, w, k, sib).wait_recv()
            for k in range(7):
                _rcopy(ins[w], outs[w].at[me], sems, w, k, sib).wait_send()
            pltpu.make_async_copy(ins[w], outs[w].at[me], sems[2].at[w]).wait()


class _GatherOwnPlan:
    mid = None

    def __init__(self, blocks, rows=None, into=None):
        self.n = len(blocks)
        self.rows = rows
        self.ins = list(blocks) + list(into or [])
        self.out_shapes = [_sds((N_DEV,) + b.shape, b.dtype) for b in blocks]
        self.aliases = [(self.n + i, i) for i in range(len(into or []))]
        n = self.n
        self.scratch = [pltpu.SemaphoreType.DMA((n, 4)), pltpu.SemaphoreType.DMA((n, 4)), pltpu.SemaphoreType.DMA((n,))]

    def _cut(self, ref):
        return ref if self.rows is None else ref.at[pl.ds(self.rows[0], self.rows[1] - self.rows[0])]

    def first(self, ins, outs, sems):
        x, y, c = _coords()
        me = 4 * x + 2 * y + c
        for w in range(self.n):
            src, dst = self._cut(ins[w]), self._cut(outs[w].at[me])
            pltpu.make_async_copy(src, dst, sems[2].at[w]).start()
            _rcopy(src, dst, sems, w, 0, (x, y, 1 - c)).start()
            for r in (1, 2, 3):
                px, py = _chip_rel(x, y, r)
                _rcopy(src, dst, sems, w, r, (px, py, c)).start()

    def last(self, ins, outs, sems):
        x, y, c = _coords()
        me = 4 * x + 2 * y + c
        for w in range(self.n):
            src = self._cut(ins[w])
            cp = _rcopy(src, self._cut(outs[w].at[4 * x + 2 * y + 1 - c]), sems, w, 0, (x, y, 1 - c))
            cp.wait_recv()
            cp.wait_send()
            for r in (1, 2, 3):
                px, py = _chip_rel(x, y, r)
                cp = _rcopy(src, self._cut(outs[w].at[4 * px + 2 * py + c]), sems, w, r, (px, py, c))
                cp.wait_recv()
                cp.wait_send()
            pltpu.make_async_copy(src, self._cut(outs[w].at[me]), sems[2].at[w]).wait()


class _GatherPassPlan:
    mid = None

    def __init__(self, gathered):
        self.ins = list(gathered)
        self.out_shapes = [_sds(g.shape, g.dtype) for g in gathered]
        self.aliases = [(i, i) for i in range(len(gathered))]
        n = len(gathered)
        self.scratch = [pltpu.SemaphoreType.DMA((n, 3)), pltpu.SemaphoreType.DMA((n, 3))]

    def first(self, ins, outs, sems):
        x, y, c = _coords()
        for w in range(len(ins)):
            for r in (1, 2, 3):
                px, py = _chip_rel(x, y, r)
                blk = 4 * px + 2 * py + c
                _rcopy(ins[w].at[blk], outs[w].at[blk], sems, w, r - 1, (x, y, 1 - c)).start()

    def last(self, ins, outs, sems):
        x, y, c = _coords()
        for w in range(len(ins)):
            for r in (1, 2, 3):
                px, py = _chip_rel(x, y, r)
                blk = 4 * px + 2 * py + 1 - c
                cp = _rcopy(ins[w].at[blk], outs[w].at[blk], sems, w, r - 1, (x, y, 1 - c))
                cp.wait_recv()
                cp.wait_send()


class _DirectGatherPlan:
    mid = None

    def __init__(self, blocks):
        self.ins = list(blocks)
        self.out_shapes = [_sds((N_DEV,) + b.shape, b.dtype) for b in blocks]
        n = len(blocks)
        self.scratch = [pltpu.SemaphoreType.DMA((n, 7)), pltpu.SemaphoreType.DMA((n, 7)), pltpu.SemaphoreType.DMA((n,))]

    @staticmethod
    def _peer(x, y, c, r):
        return (1 - x if r & 4 else x), (1 - y if r & 2 else y), (1 - c if r & 1 else c)

    def first(self, ins, outs, sems):
        x, y, c = _coords()
        me = 4 * x + 2 * y + c
        for w in range(len(ins)):
            pltpu.make_async_copy(ins[w], outs[w].at[me], sems[2].at[w]).start()
            for r in range(1, N_DEV):
                _rcopy(ins[w], outs[w].at[me], sems, w, r - 1, self._peer(x, y, c, r)).start()

    def last(self, ins, outs, sems):
        x, y, c = _coords()
        me = 4 * x + 2 * y + c
        for w in range(len(ins)):
            for r in range(1, N_DEV):
                px, py, pc = self._peer(x, y, c, r)
                cp = _rcopy(ins[w], outs[w].at[4 * px + 2 * py + pc], sems, w, r - 1, (px, py, pc))
                cp.wait_recv()
                cp.wait_send()
            pltpu.make_async_copy(ins[w], outs[w].at[me], sems[2].at[w]).wait()


class _PairScatterPlan:
    mid = None

    def __init__(self, pieces, rows=None, into=None):
        self.n = len(pieces)
        self.rows = rows
        self.ins = list(pieces) + list(into or [])
        self.out_shapes = [_sds((N_CHIP,) + p.shape[1:], p.dtype) for p in pieces]
        self.aliases = [(self.n + i, i) for i in range(len(into or []))]
        self.scratch = [pltpu.SemaphoreType.DMA((self.n, N_CHIP)), pltpu.SemaphoreType.DMA((self.n, N_CHIP))]

    def _copies(self, ins, outs, sems):
        x, y, c = _coords()
        cps = []
        for w in range(self.n):
            for q in range(N_CHIP):
                src, dst = ins[w].at[2 * q + 1 - c], outs[w].at[q]
                if self.rows is not None:
                    cut = pl.ds(self.rows[0], self.rows[1] - self.rows[0])
                    src, dst = src.at[cut], dst.at[cut]
                cps.append(_rcopy(src, dst, sems, w, q, (x, y, 1 - c)))
        return cps

    def first(self, ins, outs, sems):
        for cp in self._copies(ins, outs, sems):
            cp.start()

    def last(self, ins, outs, sems):
        for cp in self._copies(ins, outs, sems):
            cp.wait_recv()
            cp.wait_send()


class _ChipScatterPlan:
    mid = None

    def __init__(self, sums, rows=None, into=None):
        self.n = len(sums)
        self.rows = rows
        self.ins = list(sums) + list(into or [])
        self.out_shapes = [_sds(s.shape, s.dtype) for s in sums]
        self.aliases = [(self.n + i, i) for i in range(len(into or []))]
        self.scratch = [pltpu.SemaphoreType.DMA((self.n, 3)), pltpu.SemaphoreType.DMA((self.n, 3))]

    def _copies(self, ins, outs, sems):
        x, y, c = _coords()
        cps = []
        for w in range(self.n):
            for r in (1, 2, 3):
                px, py = _chip_rel(x, y, r)
                src, dst = ins[w].at[r - 1], outs[w].at[r - 1]
                if self.rows is not None:
                    cut = pl.ds(self.rows[0], self.rows[1] - self.rows[0])
                    src, dst = src.at[cut], dst.at[cut]
                cps.append(_rcopy(src, dst, sems, w, r - 1, (px, py, c)))
        return cps

    def first(self, ins, outs, sems):
        for cp in self._copies(ins, outs, sems):
            cp.start()

    def last(self, ins, outs, sems):
        for cp in self._copies(ins, outs, sems):
            cp.wait_recv()
            cp.wait_send()


class _Comm:
    def __init__(self, plans):
        self.plans = list(plans)
        self.ins = [a for p in self.plans for a in p.ins]
        self.out_shapes = [s for p in self.plans for s in p.out_shapes]
        self.scratch = [s for p in self.plans for s in p.scratch]
        self.aliases = []
        i = o = 0
        for p in self.plans:
            self.aliases += [(i + a, o + b) for a, b in getattr(p, "aliases", [])]
            i, o = i + len(p.ins), o + len(p.out_shapes)

    def _parts(self, ins, outs, sems):
        i = o = s = 0
        for p in self.plans:
            yield p, ins[i:i + len(p.ins)], outs[o:o + len(p.out_shapes)], sems[s:s + len(p.scratch)]
            i, o, s = i + len(p.ins), o + len(p.out_shapes), s + len(p.scratch)

    def begin(self, step, nsteps, ins, outs, sems):
        @pl.when(step == 0)
        def _():
            for p, pi, po, ps in self._parts(ins, outs, sems):
                p.first(pi, po, ps)

        for p, pi, po, ps in self._parts(ins, outs, sems):
            if p.mid is not None:
                @pl.when(step == min(nsteps - 1, int(p.mid_frac * nsteps)))
                def _(p=p, pi=pi, po=po, ps=ps):
                    p.mid(pi, po, ps)

    def end(self, step, nsteps, ins, outs, sems):
        @pl.when(step == nsteps - 1)
        def _():
            for p, pi, po, ps in self._parts(ins, outs, sems):
                p.last(pi, po, ps)


def _call(body, args, *, name, grid, in_specs, out_specs, out_shape, scratch_shapes=(), sem=None, comm=None):
    in_specs, out_specs, out_shape, scratch_shapes = list(in_specs), list(out_specs), list(out_shape), list(scratch_shapes)
    if comm is None:
        res = pl.pallas_call(
            body, name=name, grid=grid, in_specs=in_specs, out_specs=out_specs, out_shape=out_shape,
            scratch_shapes=scratch_shapes,
            compiler_params=pltpu.CompilerParams(dimension_semantics=sem, vmem_limit_bytes=VMEM_LIMIT_BYTES),
        )(*args)
        return list(res), []
    n_in, n_out, n_sc = len(in_specs), len(out_specs), len(scratch_shapes)
    n_ci, n_co = len(comm.ins), len(comm.out_shapes)
    nsteps = math.prod(grid)

    def hosted(*refs):
        ins, cins = refs[:n_in], refs[n_in:n_in + n_ci]
        o0 = n_in + n_ci
        outs, couts = refs[o0:o0 + n_out], refs[o0 + n_out:o0 + n_out + n_co]
        s0 = o0 + n_out + n_co
        scr, csems = refs[s0:s0 + n_sc], refs[s0 + n_sc:]
        step = jnp.int32(0)
        for d in range(len(grid)):
            step = step * grid[d] + pl.program_id(d)
        comm.begin(step, nsteps, cins, couts, csems)
        body(*ins, *outs, *scr)
        comm.end(step, nsteps, cins, couts, csems)

    res = pl.pallas_call(
        hosted, name=name, grid=grid, in_specs=in_specs + [ANY] * n_ci, out_specs=out_specs + [ANY] * n_co,
        out_shape=out_shape + comm.out_shapes, scratch_shapes=scratch_shapes + comm.scratch,
        input_output_aliases={n_in + a: n_out + b for a, b in comm.aliases},
        compiler_params=pltpu.CompilerParams(dimension_semantics=("arbitrary",) * len(grid),
                                             vmem_limit_bytes=VMEM_LIMIT_BYTES, has_side_effects=True),
    )(*args, *comm.ins)
    return list(res[:n_out]), list(res[n_out:])


def _exchange_alone(comm, name):
    def body():
        pass

    return _call(body, [], name=name, grid=(), in_specs=[], out_specs=[], out_shape=[], comm=comm)[1]


def _matmul(a, b, *, mode, name, out_dtype=F32, out_blocks=None, tm=None, tn=None, tk=None, comm=None):
    tm = MM_TILE if tm is None else tm
    tn = MM_TILE if tn is None else tn
    a_blk = a.ndim == 3
    b_blk = b.ndim == 3
    if mode == "nn":
        M, K = a.shape
        N = b.shape[0] * b.shape[2] if b_blk else b.shape[1]
        dims = (((1,), (0,)), ((), ()))
    elif mode == "nt":
        M = a.shape[1] if a_blk else a.shape[0]
        K = a.shape[0] * a.shape[2] if a_blk else a.shape[1]
        N = b.shape[1] if b_blk else b.shape[0]
        dims = (((1,), (1,)), ((), ()))
    else:
        K, M = a.shape
        N = b.shape[0] * b.shape[2] if b_blk else b.shape[1]
        dims = (((0,), (0,)), ((), ()))

    tm = _tile(M, tm)
    tn = _tile(N, tn)
    if mode == "nt" and (a_blk or b_blk):
        tk = a.shape[2] if a_blk else b.shape[2]
    else:
        tk = _tile(K, K if tk is None else tk)
    if mode != "nt" and b_blk:
        tn = _tile(b.shape[2], tn)
    if out_blocks is not None:
        tn = _tile(out_blocks, tn)
    nk = K // tk
    grid = (M // tm, N // tn, nk)

    if mode == "nn":
        a_spec = pl.BlockSpec((tm, tk), lambda i, j, k: (i, k))
        if b_blk:
            rb = b.shape[2] // tn
            b_spec = pl.BlockSpec((None, tk, tn), lambda i, j, k: (j // rb, k, j % rb))
        else:
            b_spec = pl.BlockSpec((tk, tn), lambda i, j, k: (k, j))
    elif mode == "nt":
        if a_blk:
            a_spec = pl.BlockSpec((None, tm, tk), lambda i, j, k: (k, i, 0))
        else:
            a_spec = pl.BlockSpec((tm, tk), lambda i, j, k: (i, k))
        if b_blk:
            b_spec = pl.BlockSpec((None, tn, tk), lambda i, j, k: (k, j, 0))
        else:
            b_spec = pl.BlockSpec((tn, tk), lambda i, j, k: (j, k))
    else:
        a_spec = pl.BlockSpec((tk, tm), lambda i, j, k: (k, i))
        if b_blk:
            rb = b.shape[2] // tn
            b_spec = pl.BlockSpec((None, tk, tn), lambda i, j, k: (j // rb, k, j % rb))
        else:
            b_spec = pl.BlockSpec((tk, tn), lambda i, j, k: (k, j))

    if out_blocks is None:
        o_spec = pl.BlockSpec((tm, tn), lambda i, j, k: (i, j))
        o_shape = _sds((M, N), out_dtype)
    else:
        ro = out_blocks // tn
        o_spec = pl.BlockSpec((None, tm, tn), lambda i, j, k: (j // ro, i, j % ro))
        o_shape = _sds((N // out_blocks, M, out_blocks), out_dtype)

    direct = nk == 1 or out_dtype == F32

    def body(a_ref, b_ref, o_ref, *scratch):
        if nk == 1:
            o_ref[...] = lax.dot_general(a_ref[...], b_ref[...], dims, preferred_element_type=F32).astype(o_ref.dtype)
            return
        acc_ref = o_ref if direct else scratch[0]
        k = pl.program_id(2)

        @pl.when(k == 0)
        def _():
            acc_ref[...] = jnp.zeros(acc_ref.shape, F32)

        acc_ref[...] += lax.dot_general(a_ref[...], b_ref[...], dims, preferred_element_type=F32)
        if not direct:
            @pl.when(k == nk - 1)
            def _():
                o_ref[...] = acc_ref[...].astype(o_ref.dtype)

    scratch = [] if direct else [pltpu.VMEM((tm, tn), F32)]
    outs, landed = _call(body, [a, b], name=name, grid=grid, in_specs=[a_spec, b_spec], out_specs=[o_spec],
                         out_shape=[o_shape], scratch_shapes=scratch, sem=("parallel", "parallel", "arbitrary"), comm=comm)
    return outs[0] if comm is None else (outs[0], landed)


def _matmul_halves(a, b, *, mode, name, tm, tn=None, tk=None, comm=None):
    if mode == "nt":
        lo, hi = a
        M, kh = lo.shape
        G, N, kb = b.shape
        half = kh // kb
        tm, tn = _tile(M, tm), _tile(N, N if tn is None else tn)
        dims = (((1,), (1,)), ((), ()))

        def body(lo_ref, hi_ref, b_ref, o_ref):
            k = pl.program_id(2)

            @pl.when(k == 0)
            def _():
                o_ref[...] = jnp.zeros(o_ref.shape, F32)

            @pl.when(k < half)
            def _():
                o_ref[...] += lax.dot_general(lo_ref[...], b_ref[...], dims, preferred_element_type=F32)

            @pl.when(k >= half)
            def _():
                o_ref[...] += lax.dot_general(hi_ref[...], b_ref[...], dims, preferred_element_type=F32)

        outs, landed = _call(
            body, [lo, hi, b], name=name, grid=(M // tm, N // tn, G),
            in_specs=[pl.BlockSpec((tm, kb), lambda i, j, k: (i, jnp.minimum(k, half - 1))),
                      pl.BlockSpec((tm, kb), lambda i, j, k: (i, jnp.maximum(k - half, 0))),
                      pl.BlockSpec((None, tn, kb), lambda i, j, k: (k, j, 0))],
            out_specs=[pl.BlockSpec((tm, tn), lambda i, j, k: (i, j))], out_shape=[_sds((M, N), F32)],
            sem=("parallel", "parallel", "arbitrary"), comm=comm)
    else:
        lo, hi = b
        K, nh = lo.shape
        M = a.shape[1]
        n = tn
        half = nh // n
        tm, tk = _tile(M, tm), _tile(K, K if tk is None else tk)
        nk = K // tk
        dims = (((0,), (0,)), ((), ()))

        def body(a_ref, lo_ref, hi_ref, o_ref):
            j, k = pl.program_id(1), pl.program_id(2)

            @pl.when(k == 0)
            def _():
                o_ref[...] = jnp.zeros(o_ref.shape, F32)

            @pl.when(j < half)
            def _():
                o_ref[...] += lax.dot_general(a_ref[...], lo_ref[...], dims, preferred_element_type=F32)

            @pl.when(j >= half)
            def _():
                o_ref[...] += lax.dot_general(a_ref[...], hi_ref[...], dims, preferred_element_type=F32)

        outs, landed = _call(
            body, [a, lo, hi], name=name, grid=(M // tm, 2 * half, nk),
            in_specs=[pl.BlockSpec((tk, tm), lambda i, j, k: (k, i)),
                      pl.BlockSpec((tk, n), lambda i, j, k: (jnp.where(j < half, k, nk - 1), jnp.minimum(j, half - 1))),
                      pl.BlockSpec((tk, n), lambda i, j, k: (jnp.where(j >= half, k, 0), jnp.maximum(j - half, 0)))],
            out_specs=[pl.BlockSpec((None, tm, n), lambda i, j, k: (j, i, 0))],
            out_shape=[_sds((2 * half, M, n), F32)], sem=("parallel", "parallel", "arbitrary"), comm=comm)
    return outs[0] if comm is None else (outs[0], landed)


def _rms(x):
    return lax.rsqrt(jnp.mean(x * x, axis=-1, keepdims=True) + EPS)


def _rms_bwd(dy, x, g):
    r = _rms(x)
    n = x * r
    dn = dy * g
    dx = r * (dn - n * jnp.mean(dn * n, axis=-1, keepdims=True))
    return dx, dy * n


def _sigmoid(x):
    return 1.0 / (1.0 + jnp.exp(-x))


def _rope_rot(t):
    return pltpu.roll(t, HALF_ROPE, 1) - pltpu.roll(t, LANES - HALF_ROPE, 1)


def _lane(shape):
    return lax.broadcasted_iota(jnp.int32, shape, 1)


def _split3(x):
    hi = x.astype(BF16).astype(F32)
    r1 = x - hi
    mid = r1.astype(BF16).astype(F32)
    lo = (r1 - mid).astype(BF16).astype(F32)
    return hi, mid, lo


def _cumsum_rows(x, reverse):
    S = x.shape[0]
    bs = min(256, S)
    nb = S // bs
    r = lax.broadcasted_iota(jnp.int32, (bs, bs), 0)
    c = lax.broadcasted_iota(jnp.int32, (bs, bs), 1)
    tri = jnp.where((c >= r) if reverse else (c <= r), 1.0, 0.0).astype(BF16)
    edge = lax.broadcasted_iota(jnp.int32, (bs, x.shape[1]), 0) == (0 if reverse else bs - 1)
    carry = jnp.zeros((1, x.shape[1]), F32)
    outs = [None] * nb
    for bi in (range(nb - 1, -1, -1) if reverse else range(nb)):
        xb = x[bi * bs:(bi + 1) * bs, :]
        acc = carry
        for term in _split3(xb):
            acc = acc + jnp.dot(tri, term.astype(BF16), preferred_element_type=F32)
        outs[bi] = acc
        carry = jnp.sum(jnp.where(edge, acc, 0.0), axis=0, keepdims=True)
    return jnp.concatenate(outs, axis=0) if nb > 1 else outs[0]


def _gelu_parts(x):
    c0 = math.sqrt(2.0 / math.pi)
    inner = c0 * (x + 0.044715 * (x * x * x))
    t = jnp.tanh(inner)
    g = 0.5 * x * (1.0 + t)
    dg = 0.5 * (1.0 + t) + 0.5 * x * (1.0 - t * t) * (c0 * (1.0 + 3.0 * 0.044715 * (x * x)))
    return g, dg


def _accumulate(ref, value, first):
    @pl.when(first)
    def _():
        ref[...] = value

    @pl.when(jnp.logical_not(first))
    def _():
        ref[...] += value


def _cast_bf16(w, name):
    R, C = w.shape
    tr = _tile(R, 512, 16)

    def body(w_ref, o_ref):
        o_ref[...] = w_ref[...].astype(BF16)

    blk = pl.BlockSpec((tr, C), lambda i: (i, 0))
    return _call(body, [w], name=name, grid=(R // tr,), in_specs=[blk], out_specs=[blk],
                 out_shape=[_sds((R, C), BF16)], sem=("parallel",))[0][0]


def _concat_cols(parts, name):
    T = parts[0].shape[-2] if parts[0].ndim == 3 else parts[0].shape[0]
    widths = [p.shape[0] * LANES if p.ndim == 3 else p.shape[1] for p in parts]
    tm = _tile(T, ROW_TILE, 16)

    def body(*refs):
        o_ref = refs[-1]
        off = 0
        for p_ref, p, w in zip(refs[:-1], parts, widths):
            if p.ndim == 3:
                for hd in range(p.shape[0]):
                    o_ref[:, off + hd * LANES:off + (hd + 1) * LANES] = p_ref[hd].astype(BF16)
            else:
                o_ref[:, off:off + w] = p_ref[...].astype(BF16)
            off += w

    def spec(p, w):
        if p.ndim == 3:
            return pl.BlockSpec((p.shape[0], tm, LANES), lambda i: (0, i, 0))
        return pl.BlockSpec((tm, w), lambda i: (i, 0))

    return _call(body, parts, name=name, grid=(T // tm,),
                 in_specs=[spec(p, w) for p, w in zip(parts, widths)],
                 out_specs=[pl.BlockSpec((tm, sum(widths)), lambda i: (i, 0))],
                 out_shape=[_sds((T, sum(widths)), BF16)], sem=("parallel",))[0][0]


def _prenorm(x, g, comm=None):
    T, D = x.shape
    tm = _tile(T, ROW_TILE, 16)

    def body(x_ref, g_ref, h_ref):
        xv = x_ref[...]
        h_ref[...] = (xv * _rms(xv) * g_ref[...]).astype(BF16)

    row = pl.BlockSpec((tm, D), lambda i: (i, 0))
    (h,), landed = _call(body, [x, g], name="prenorm", grid=(T // tm,),
                         in_specs=[row, pl.BlockSpec((1, D), lambda i: (0, 0))], out_specs=[row],
                         out_shape=[_sds((T, D), BF16)], sem=("parallel",), comm=comm)
    return h, landed


def _split_prep(proj, pos, invf, gq, gkv, bfor, lay, comm=None):
    T = proj.shape[0]
    tm = _tile(T, ROW_TILE, 16)

    def body(q_ref, kv_ref, kpe_ref, fl_ref, pos_ref, invf_ref, gq_ref, gkv_ref, bf_ref,
             qn_ref, kvn_ref, kper_ref, logf_ref, cos_ref, sin_ref):
        ql = q_ref[...]
        qn_ref[...] = (ql * _rms(ql) * gq_ref[...]).astype(BF16)
        kl = kv_ref[...]
        kvn_ref[...] = (kl * _rms(kl) * gkv_ref[...]).astype(BF16)
        ang = pos_ref[...].astype(F32) * invf_ref[...]
        valid = _lane(ang.shape) < ROPE
        cs = jnp.where(valid, jnp.cos(ang), 0.0)
        sn = jnp.where(valid, jnp.sin(ang), 0.0)
        cos_ref[...] = cs
        sin_ref[...] = sn
        kp = jnp.where(valid, kpe_ref[...], 0.0)
        kper_ref[...] = (kp * cs + _rope_rot(kp) * sn).astype(BF16)
        z = fl_ref[...] + bf_ref[...]
        logf_ref[...] = jnp.minimum(z, 0.0) - jnp.log(1.0 + jnp.exp(-jnp.abs(z)))

    def col(width, off):
        return pl.BlockSpec((tm, width), lambda i: (i, off // width))

    def vec(width):
        return pl.BlockSpec((1, width), lambda i: (0, 0))

    def out(width):
        return pl.BlockSpec((tm, width), lambda i: (i, 0))

    return _call(
        body, [proj, proj, proj, proj, pos, invf, gq, gkv, bfor], name="split_prep", grid=(T // tm,),
        in_specs=[col(Q_LORA, lay["q"]), col(KV_LORA, lay["kv"]), col(LANES, lay["kpe"]), col(LANES, lay["fl"]),
                  pl.BlockSpec((tm, 1), lambda i: (i, 0)), vec(LANES), vec(Q_LORA), vec(KV_LORA), vec(LANES)],
        out_specs=[out(Q_LORA), out(KV_LORA), out(LANES), out(LANES), out(LANES), out(LANES)],
        out_shape=[_sds((T, Q_LORA), BF16), _sds((T, KV_LORA), BF16), _sds((T, LANES), BF16),
                   _sds((T, LANES), F32), _sds((T, LANES), F32), _sds((T, LANES), F32)],
        sem=("parallel",), comm=comm)


def _mla_prep(qraw, kvraw, kper, cosT, sinT, comm=None):
    H, T, _ = qraw.shape
    tm = _tile(T, HEAD_ROW_TILE, 16)

    def body(q_ref, kv_ref, kpe_ref, cos_ref, sin_ref, qo_ref, ko_ref, vo_ref):
        q = q_ref[...]
        pe = q[:, NOPE:]
        pe = jnp.where(_lane(pe.shape) < ROPE, pe, 0.0)
        qo_ref[:, :NOPE] = q[:, :NOPE].astype(BF16)
        qo_ref[:, NOPE:] = (pe * cos_ref[...] + _rope_rot(pe) * sin_ref[...]).astype(BF16)
        kv = kv_ref[...]
        ko_ref[:, :NOPE] = kv[:, :NOPE].astype(BF16)
        ko_ref[:, NOPE:] = kpe_ref[...]
        vo_ref[...] = kv[:, NOPE:].astype(BF16)

    head = pl.BlockSpec((None, tm, ATT_DK), lambda h, i: (h, i, 0))
    tok = pl.BlockSpec((tm, LANES), lambda h, i: (i, 0))
    return _call(
        body, [qraw, kvraw, kper, cosT, sinT], name="mla_prep", grid=(H, T // tm),
        in_specs=[head, head, tok, tok, tok],
        out_specs=[head, head, pl.BlockSpec((None, tm, VDIM), lambda h, i: (h, i, 0))],
        out_shape=[_sds((H, T, ATT_DK), BF16), _sds((H, T, ATT_DK), BF16), _sds((H, T, VDIM), BF16)],
        sem=("parallel", "parallel"), comm=comm)


def _fox_cumsum(logf, B, S, inv_scale):
    T = logf.shape[0]

    def body(l_ref, c_ref):
        c_ref[...] = _cumsum_rows(l_ref[...], reverse=False) * inv_scale

    seq = pl.BlockSpec((S, LANES), lambda b: (b, 0))
    return _call(body, [logf], name="fox_cumsum", grid=(B,), in_specs=[seq], out_specs=[seq],
                 out_shape=[_sds((T, LANES), F32)], sem=("parallel",))[0][0]


def _fox_prep(proj, cs, lay, comm=None):
    T = proj.shape[0]
    tm = _tile(T, HEAD_ROW_TILE, 16)

    def body(q_ref, k_ref, v_ref, cs_ref, qo_ref, ko_ref, vo_ref):
        h = pl.program_id(0)
        cv = cs_ref[...]
        lane = _lane(cv.shape)
        ccol = jnp.sum(jnp.where(lane == h, cv, 0.0), axis=1, keepdims=True)
        hi, mid, lo = _split3(ccol)
        one = jnp.where(lane < 6, 1.0, 0.0)
        augq = jnp.where(lane == 0, hi, jnp.where(lane == 1, mid, jnp.where(lane == 2, lo, one)))
        augk = jnp.where(lane < 3, 1.0, jnp.where(lane == 3, -hi, jnp.where(lane == 4, -mid, jnp.where(lane == 5, -lo, 0.0))))
        qo_ref[:, :FOX_DIM] = q_ref[...].astype(BF16)
        qo_ref[:, FOX_DIM:] = augq.astype(BF16)
        ko_ref[:, :FOX_DIM] = k_ref[...].astype(BF16)
        ko_ref[:, FOX_DIM:] = augk.astype(BF16)
        vo_ref[...] = v_ref[...].astype(BF16)

    def col(off):
        return pl.BlockSpec((tm, FOX_DIM), lambda h, i: (i, off // FOX_DIM + h))

    head = pl.BlockSpec((None, tm, ATT_DK), lambda h, i: (h, i, 0))
    return _call(
        body, [proj, proj, proj, cs], name="fox_prep", grid=(HEADS, T // tm),
        in_specs=[col(lay["fq"]), col(lay["fk"]), col(lay["fv"]), pl.BlockSpec((tm, LANES), lambda h, i: (i, 0))],
        out_specs=[head, head, pl.BlockSpec((None, tm, VDIM), lambda h, i: (h, i, 0))],
        out_shape=[_sds((HEADS, T, ATT_DK), BF16), _sds((HEADS, T, ATT_DK), BF16), _sds((HEADS, T, VDIM), BF16)],
        sem=("parallel", "parallel"), comm=comm)


def _visible(tq, tk, unit):
    r = lax.broadcasted_iota(jnp.int32, (tq, tk), 0)
    c = lax.broadcasted_iota(jnp.int32, (tq, tk), 1)
    sh = int(math.log2(unit))
    return lax.shift_right_logical(c, sh) <= lax.shift_right_logical(r, sh)


def _attn_fwd(streams, *, B, S, name, comm=None):
    n = len(streams)
    H, T, DK = streams[0][0].shape
    DV = streams[0][2].shape[2]
    tq = _tile(S, ATT_TILE)
    nq = S // tq
    sub = min(ATT_SUB, tq)
    NT = (((1,), (1,)), ((), ()))

    def body(*refs):
        ins, outs, (m_sc, acc_sc) = refs[:3 * n], refs[3 * n:5 * n], refs[5 * n:]
        i, j = pl.program_id(1), pl.program_id(2)

        @pl.when(j == 0)
        def _():
            m_sc[...] = jnp.full(m_sc.shape, NEG_INF, F32)
            acc_sc[...] = jnp.zeros(acc_sc.shape, F32)

        def step(diagonal):
            work = [(t, r) for r in range(tq // sub) for t in range(n)]

            def scores(t, r):
                q_ref, k_ref, _ = ins[3 * t:3 * t + 3]
                kc = (r + 1) * sub if diagonal else tq
                s = lax.dot_general(q_ref[r * sub:(r + 1) * sub, :], k_ref[0:kc, :], NT, preferred_element_type=F32)
                return s * (streams[t][4] * LOG2_E)

            ahead = [scores(*work[w]) for w in range(min(ATT_AHEAD, len(work)))]
            for w, (t, r) in enumerate(work):
                s = ahead.pop(0)
                if w + ATT_AHEAD < len(work):
                    ahead.append(scores(*work[w + ATT_AHEAD]))
                v_ref = ins[3 * t + 2]
                kc = s.shape[1]
                rows = slice(r * sub, (r + 1) * sub)
                if diagonal:
                    own = jnp.where(_visible(sub, sub, streams[t][3]), s[:, kc - sub:], NEG_INF)
                    s = own if kc == sub else jnp.concatenate([s[:, :kc - sub], own], axis=1)
                m_prev = m_sc[t, rows, :]
                mx = s[:, 0:LANES]
                for g in range(1, kc // LANES):
                    mx = jnp.maximum(mx, s[:, g * LANES:(g + 1) * LANES])
                m_new = jnp.maximum(m_prev, jnp.max(mx, axis=1, keepdims=True))
                alpha = jnp.exp2(m_prev - m_new)
                p = jnp.exp2(s - jnp.tile(m_new, (1, kc // LANES))).astype(BF16)
                v_aug = jnp.concatenate([v_ref[0:kc, :], jnp.ones((kc, LANES), BF16)], axis=1)
                acc_sc[t, rows, :] = jnp.tile(alpha, (1, 2)) * acc_sc[t, rows, :] + jnp.dot(
                    p, v_aug, preferred_element_type=F32)
                m_sc[t, rows, :] = m_new

        @pl.when(j < i)
        def _():
            step(False)

        @pl.when(j == i)
        def _():
            step(True)
            for t in range(n):
                o_ref, lse_ref = outs[2 * t:2 * t + 2]
                l = acc_sc[t, :, DV:]
                o_ref[...] = (acc_sc[t, :, :DV] / l).astype(BF16)
                lse_ref[...] = m_sc[t] + jnp.log2(l)

    def qmap(g, i, j):
        return (g % H, (g // H) * nq + i, 0)

    def kmap(g, i, j):
        return (g % H, (g // H) * nq + jnp.minimum(j, i), 0)

    args = [a for st in streams for a in st[:3]]
    outs, landed = _call(
        body, args, name=name, grid=(B * H, nq, nq),
        in_specs=[pl.BlockSpec((None, tq, DK), qmap), pl.BlockSpec((None, tq, DK), kmap),
                  pl.BlockSpec((None, tq, DV), kmap)] * n,
        out_specs=[pl.BlockSpec((tq, DV), lambda g, i, j: ((g // H) * nq + i, g % H)),
                   pl.BlockSpec((None, tq, LANES), qmap)] * n,
        out_shape=[_sds((T, H * DV), BF16), _sds((H, T, LANES), F32)] * n,
        scratch_shapes=[pltpu.VMEM((n, tq, LANES), F32), pltpu.VMEM((n, tq, DV + LANES), F32)],
        sem=("parallel", "parallel", "arbitrary"), comm=comm)
    return [(outs[2 * t], outs[2 * t + 1]) for t in range(n)], landed


def _attn_bwd(streams, *, B, S, name, comm=None):
    n = len(streams)
    H, T, DK = streams[0][0].shape
    DV = streams[0][2].shape[2]
    tq = _tile(S, ATT_TILE)
    nq = S // tq
    sub = min(ATT_SUB, tq)
    NT = (((1,), (1,)), ((), ()))
    TN = (((0,), (0,)), ((), ()))

    def body(*refs):
        ins, outs = refs[:6 * n], refs[6 * n:]
        j, i = pl.program_id(1), pl.program_id(2)

        @pl.when(jnp.logical_and(j == 0, i == 0))
        def _():
            for t in range(n):
                outs[3 * t][...] = jnp.zeros(outs[3 * t].shape, F32)

        @pl.when(i == 0)
        def _():
            for t in range(n):
                outs[3 * t + 1][...] = jnp.zeros(outs[3 * t + 1].shape, F32)
                outs[3 * t + 2][...] = jnp.zeros(outs[3 * t + 2].shape, F32)

        def step(diagonal):
            work = [(t, r) for r in range(tq // sub) for t in range(n)]

            def kcols(r):
                return (r + 1) * sub if diagonal else tq

            def scores(t, r):
                q_ref, k_ref, v_ref, _, do_ref, _ = ins[6 * t:6 * t + 6]
                rows, kc = slice(r * sub, (r + 1) * sub), kcols(r)
                s = lax.dot_general(q_ref[rows, :], k_ref[0:kc, :], NT, preferred_element_type=F32)
                dp = lax.dot_general(do_ref[rows, :], v_ref[0:kc, :], NT, preferred_element_type=F32)
                return s * (streams[t][7] * LOG2_E), dp

            def probs(t, r, s, dp):
                _, _, _, o_ref, do_ref, lse_ref = ins[6 * t:6 * t + 6]
                rows, kc = slice(r * sub, (r + 1) * sub), kcols(r)
                if diagonal:
                    own = jnp.where(_visible(sub, sub, streams[t][6]), s[:, kc - sub:], NEG_INF)
                    s = own if kc == sub else jnp.concatenate([s[:, :kc - sub], own], axis=1)
                p = jnp.exp2(s - jnp.tile(lse_ref[rows, :], (1, kc // LANES)))
                delta = jnp.sum(do_ref[rows, :].astype(F32) * o_ref[rows, :].astype(F32), axis=1, keepdims=True)
                return p.astype(BF16), (p * (dp - delta) * streams[t][7]).astype(BF16)

            def grads(t, r, p, ds):
                q_ref, k_ref, _, _, do_ref, _ = ins[6 * t:6 * t + 6]
                dq_ref, dk_ref, dv_ref = outs[3 * t:3 * t + 3]
                rows, kc = slice(r * sub, (r + 1) * sub), kcols(r)
                dv_ref[0:kc, :] += lax.dot_general(p, do_ref[rows, :], TN, preferred_element_type=F32)
                dk_ref[0:kc, :] += lax.dot_general(ds, q_ref[rows, :], TN, preferred_element_type=F32)
                qrows = pl.ds(pl.multiple_of(i * tq + r * sub, sub), sub)
                dq_ref[qrows, :] += jnp.dot(ds, k_ref[0:kc, :], preferred_element_type=F32)

            nw = len(work)
            sc = {w: scores(*work[w]) for w in range(min(2, nw))}
            pr = {0: probs(*work[0], *sc.pop(0))}
            for w in range(nw):
                if w + 2 < nw:
                    sc[w + 2] = scores(*work[w + 2])
                if w + 1 < nw:
                    pr[w + 1] = probs(*work[w + 1], *sc.pop(w + 1))
                grads(*work[w], *pr.pop(w))

        @pl.when(i > j)
        def _():
            step(False)

        @pl.when(i == j)
        def _():
            step(True)

    def qmap(g, j, i):
        return (g % H, (g // H) * nq + jnp.maximum(i, j), 0)

    def kmap(g, j, i):
        return (g % H, (g // H) * nq + j, 0)

    def omap(g, j, i):
        return ((g // H) * nq + jnp.maximum(i, j), g % H)

    args = [a for st in streams for a in st[:6]]
    outs, landed = _call(
        body, args, name=name, grid=(B * H, nq, nq),
        in_specs=[pl.BlockSpec((None, tq, DK), qmap), pl.BlockSpec((None, tq, DK), kmap),
                  pl.BlockSpec((None, tq, DV), kmap), pl.BlockSpec((tq, DV), omap), pl.BlockSpec((tq, DV), omap),
                  pl.BlockSpec((None, tq, LANES), qmap)] * n,
        out_specs=[pl.BlockSpec((None, S, DK), lambda g, j, i: (g % H, g // H, 0)),
                   pl.BlockSpec((None, tq, DK), kmap), pl.BlockSpec((None, tq, DV), kmap)] * n,
        out_shape=[_sds((H, T, DK), F32), _sds((H, T, DK), F32), _sds((H, T, DV), F32)] * n,
        sem=("parallel", "arbitrary", "arbitrary"), comm=comm)
    return [tuple(outs[3 * t:3 * t + 3]) for t in range(n)], landed


def _branch_merge(o_m, o_f, w_m, w_f, proj, bgate, lay, D, comm=None):
    T, K = o_m.shape
    tm = _tile(T, 2 * ROW_TILE, 16)
    tn = _tile(D, 1024)

    def body(om_ref, of_ref, wm_ref, wf_ref, gm_ref, gf_ref, bm_ref, bf_ref, am_ref, af_ref, o_ref):
        am = jnp.dot(om_ref[...], wm_ref[...], preferred_element_type=F32)
        af = jnp.dot(of_ref[...], wf_ref[...], preferred_element_type=F32)
        am_ref[...] = am
        af_ref[...] = af
        sm = _sigmoid(gm_ref[...] + bm_ref[...])
        sf = _sigmoid(gf_ref[...] + bf_ref[...])
        o_ref[...] = (sm * am + sf * af).astype(BF16)

    og = lay["g"] // tn
    blk = pl.BlockSpec((tm, tn), lambda i, j: (i, j))
    lhs = pl.BlockSpec((tm, K), lambda i, j: (i, 0))
    rhs = pl.BlockSpec((K, tn), lambda i, j: (0, j))
    return _call(
        body, [o_m, o_f, w_m, w_f, proj, proj, bgate, bgate], name="mm_branch_merge", grid=(T // tm, D // tn),
        in_specs=[lhs, lhs, rhs, rhs, pl.BlockSpec((tm, tn), lambda i, j: (i, og + j)),
                  pl.BlockSpec((tm, tn), lambda i, j: (i, og + D // tn + j)),
                  pl.BlockSpec((1, tn), lambda i, j: (0, j)), pl.BlockSpec((1, tn), lambda i, j: (0, D // tn + j))],
        out_specs=[blk, blk, blk], out_shape=[_sds((T, D), F32), _sds((T, D), F32), _sds((T, D), BF16)],
        sem=("parallel", "parallel"), comm=comm)


def _out_mid(merged, w_out, x, g_pm, g_ffn):
    T, D = x.shape
    tm = _tile(T, 2 * ROW_TILE, 16)

    def body(a_ref, w_ref, x_ref, gp_ref, gf_ref, y_ref, x1_ref, h2_ref):
        y = jnp.dot(a_ref[...], w_ref[...], preferred_element_type=F32)
        y_ref[...] = y
        x1 = x_ref[...] + y * _rms(y) * gp_ref[...]
        x1_ref[...] = x1
        h2_ref[...] = (x1 * _rms(x1) * gf_ref[...]).astype(BF16)

    row = pl.BlockSpec((tm, D), lambda i: (i, 0))
    vec = pl.BlockSpec((1, D), lambda i: (0, 0))
    return _call(body, [merged, w_out, x, g_pm, g_ffn], name="mm_out_mid", grid=(T // tm,),
                 in_specs=[row, pl.BlockSpec((D, D), lambda i: (0, 0)), row, vec, vec], out_specs=[row, row, row],
                 out_shape=[_sds((T, D), F32), _sds((T, D), F32), _sds((T, D), BF16)], sem=("parallel",))[0]


def _conv3(u, w_ref, bias):
    row = lax.broadcasted_iota(jnp.int32, u.shape, 0)
    u1 = jnp.where(row >= 1, pltpu.roll(u, 1, 0), 0.0)
    u2 = jnp.where(row >= 2, pltpu.roll(u, 2, 0), 0.0)
    return w_ref[0:1, :] * u2 + w_ref[1:2, :] * u1 + w_ref[2:3, :] * u + bias, u1, u2


def _convffn_fwd(u, cw, cb, B, S, F, comm=None):
    T = u.shape[0]
    tn = _tile(F, 256)
    nf = F // tn

    def body(ug_ref, uv_ref, wg_ref, wv_ref, bg_ref, bv_ref, a_ref):
        g, _, _ = _conv3(ug_ref[...], wg_ref, bg_ref[...])
        val, _, _ = _conv3(uv_ref[...], wv_ref, bv_ref[...])
        a_ref[...] = (_gelu_parts(g)[0] * val).astype(BF16)

    def seq(off):
        return pl.BlockSpec((S, tn), lambda b, j: (b, off + j))

    def par(rows, off):
        return pl.BlockSpec((rows, tn), lambda b, j: (0, off + j))

    return _call(body, [u, u, cw, cw, cb, cb], name="convffn_fwd", grid=(B, nf),
                 in_specs=[seq(0), seq(nf), par(3, 0), par(3, nf), par(1, 0), par(1, nf)],
                 out_specs=[seq(0)], out_shape=[_sds((T, F), BF16)], sem=("parallel", "parallel"), comm=comm)


def _convffn_bwd(u, dact, cw, cb, B, S, F, comm=None):
    T = u.shape[0]
    tn = _tile(F, 256)
    nf = F // tn

    def body(ug_ref, uv_ref, da_ref, wg_ref, wv_ref, bg_ref, bv_ref, dug_ref, duv_ref, dpg_ref, dpv_ref):
        b = pl.program_id(1)
        ug, uv, da = ug_ref[...], uv_ref[...], da_ref[...]
        g, ug1, ug2 = _conv3(ug, wg_ref, bg_ref[...])
        val, uv1, uv2 = _conv3(uv, wv_ref, bv_ref[...])
        gel, dgel = _gelu_parts(g)
        dg = da * val * dgel
        dval = da * gel
        row = lax.broadcasted_iota(jnp.int32, ug.shape, 0)

        def back(d, w_ref):
            d1 = jnp.where(row < S - 1, pltpu.roll(d, S - 1, 0), 0.0)
            d2 = jnp.where(row < S - 2, pltpu.roll(d, S - 2, 0), 0.0)
            return w_ref[2:3, :] * d + w_ref[1:2, :] * d1 + w_ref[0:1, :] * d2

        dug_ref[...] = back(dg, wg_ref).astype(BF16)
        duv_ref[...] = back(dval, wv_ref).astype(BF16)

        def sums(d, u0, u1, u2):
            r8 = lax.broadcasted_iota(jnp.int32, (8, d.shape[1]), 0)
            out = jnp.zeros((8, d.shape[1]), F32)
            for k, t in enumerate((d * u2, d * u1, d * u0, d)):
                out = jnp.where(r8 == k, jnp.sum(t, axis=0, keepdims=True), out)
            return out

        _accumulate(dpg_ref, sums(dg, ug, ug1, ug2), b == 0)
        _accumulate(dpv_ref, sums(dval, uv, uv1, uv2), b == 0)

    def seq(off):
        return pl.BlockSpec((S, tn), lambda j, b: (b, off + j))

    def par(rows, off):
        return pl.BlockSpec((rows, tn), lambda j, b: (0, off + j))

    outs, landed = _call(
        body, [u, u, dact, cw, cw, cb, cb], name="convffn_bwd", grid=(nf, B),
        in_specs=[seq(0), seq(nf), seq(0), par(3, 0), par(3, nf), par(1, 0), par(1, nf)],
        out_specs=[seq(0), seq(0), par(8, 0), par(8, 0)],
        out_shape=[_sds((T, F), BF16), _sds((T, F), BF16), _sds((8, F), F32), _sds((8, F), F32)],
        sem=("parallel", "arbitrary"), comm=comm)
    return outs, landed


def _tail(ff, x1, tgt, g):
    T, D = ff.shape
    tm = _tile(T, ROW_TILE, 16)

    def body(ff_ref, x1_ref, t_ref, g_ref, dy_ref, dff_ref, loss_ref, dg_ref):
        i = pl.program_id(0)
        f = ff_ref[...]
        gv = g_ref[...]
        r = _rms(f)
        n = f * r
        e = (x1_ref[...] + n * gv) - t_ref[...]
        dy = e * (1.0 / D)
        dy_ref[...] = dy
        dn = dy * gv
        dff_ref[...] = (r * (dn - n * jnp.mean(dn * n, axis=-1, keepdims=True))).astype(BF16)
        part = 0.5 * jnp.sum(jnp.mean(e * e, axis=-1, keepdims=True), axis=0, keepdims=True)
        _accumulate(loss_ref, jnp.broadcast_to(part, loss_ref.shape), i == 0)
        _accumulate(dg_ref, jnp.sum(dy * n, axis=0, keepdims=True), i == 0)

    row = pl.BlockSpec((tm, D), lambda i: (i, 0))
    vec = pl.BlockSpec((1, D), lambda i: (0, 0))
    return _call(body, [ff, x1, tgt, g], name="tail", grid=(T // tm,), in_specs=[row, row, row, vec],
                 out_specs=[row, row, pl.BlockSpec((8, LANES), lambda i: (0, 0)), vec],
                 out_shape=[_sds((T, D), F32), _sds((T, D), BF16), _sds((8, LANES), F32), _sds((1, D), F32)],
                 sem=("arbitrary",))[0]


def _mid_bwd(dy, dh2, x1, y1, g_ffn, g_pm, comm=None):
    T, D = dy.shape
    tm = _tile(T, ROW_TILE, 16)

    def body(dy_ref, dh_ref, x1_ref, y1_ref, gf_ref, gp_ref, dx1_ref, dy1_ref, dgf_ref, dgp_ref):
        i = pl.program_id(0)
        dh = dh_ref[...]
        d2, dgf = _rms_bwd(dh, x1_ref[...], gf_ref[...])
        dx1 = dy_ref[...] + d2
        dx1_ref[...] = dx1
        d1, dgp = _rms_bwd(dx1, y1_ref[...], gp_ref[...])
        dy1_ref[...] = d1.astype(BF16)
        _accumulate(dgf_ref, jnp.sum(dgf, axis=0, keepdims=True), i == 0)
        _accumulate(dgp_ref, jnp.sum(dgp, axis=0, keepdims=True), i == 0)

    row = pl.BlockSpec((tm, D), lambda i: (i, 0))
    vec = pl.BlockSpec((1, D), lambda i: (0, 0))
    return _call(body, [dy, dh2, x1, y1, g_ffn, g_pm], name="mid_bwd", grid=(T // tm,),
                 in_specs=[row, row, row, row, vec, vec], out_specs=[row, row, vec, vec],
                 out_shape=[_sds((T, D), F32), _sds((T, D), BF16), _sds((1, D), F32), _sds((1, D), F32)],
                 sem=("arbitrary",), comm=comm)


def _gate_bwd(dy1, w_out, am, af, proj, bgate, lay, D, comm=None):
    T = dy1.shape[0]
    tm = _tile(T, 2 * ROW_TILE, 16)
    tn = _tile(D, 1024)
    NT = (((1,), (1,)), ((), ()))

    def body(dy_ref, w_ref, am_ref, af_ref, gm_ref, gf_ref, bm_ref, bf_ref,
             dam_ref, daf_ref, dgm_ref, dgf_ref, dbm_ref, dbf_ref):
        i = pl.program_id(1)
        d = lax.dot_general(dy_ref[...], w_ref[...], NT, preferred_element_type=F32)
        sm = _sigmoid(gm_ref[...] + bm_ref[...])
        sf = _sigmoid(gf_ref[...] + bf_ref[...])
        dam_ref[...] = (d * sm).astype(BF16)
        daf_ref[...] = (d * sf).astype(BF16)
        dgm = d * am_ref[...] * (sm * (1.0 - sm))
        dgf = d * af_ref[...] * (sf * (1.0 - sf))
        dgm_ref[...] = dgm.astype(BF16)
        dgf_ref[...] = dgf.astype(BF16)
        _accumulate(dbm_ref, jnp.sum(dgm, axis=0, keepdims=True), i == 0)
        _accumulate(dbf_ref, jnp.sum(dgf, axis=0, keepdims=True), i == 0)

    og = lay["g"] // tn
    blk = pl.BlockSpec((tm, tn), lambda j, i: (i, j))
    vec = pl.BlockSpec((1, tn), lambda j, i: (0, j))
    return _call(
        body, [dy1, w_out, am, af, proj, proj, bgate, bgate], name="mm_dmerged_gate_bwd", grid=(D // tn, T // tm),
        in_specs=[pl.BlockSpec((tm, D), lambda j, i: (i, 0)), pl.BlockSpec((tn, D), lambda j, i: (j, 0)),
                  blk, blk, pl.BlockSpec((tm, tn), lambda j, i: (i, og + j)),
                  pl.BlockSpec((tm, tn), lambda j, i: (i, og + D // tn + j)),
                  vec, pl.BlockSpec((1, tn), lambda j, i: (0, D // tn + j))],
        out_specs=[blk, blk, blk, blk, vec, vec],
        out_shape=[_sds((T, D), BF16)] * 4 + [_sds((1, D), F32)] * 2, sem=("parallel", "arbitrary"), comm=comm)


def _mla_bwd_prep(dq, dk, dv, cosT, sinT, comm=None):
    H, T, _ = dq.shape
    tm = _tile(T, HEAD_ROW_TILE, 16)

    def body(dq_ref, dk_ref, dv_ref, cos_ref, sin_ref, dqr_ref, dkv_ref, dkpe_ref):
        h = pl.program_id(1)
        cs, sn = cos_ref[...], sin_ref[...]
        valid = _lane(cs.shape) < ROPE

        def unrope(d):
            d = jnp.where(valid, d, 0.0)
            return d * cs - _rope_rot(d) * sn

        dqv = dq_ref[...]
        dqr_ref[:, :NOPE] = dqv[:, :NOPE].astype(BF16)
        dqr_ref[:, NOPE:] = unrope(dqv[:, NOPE:]).astype(BF16)
        dkv_ = dk_ref[...]
        dkv_ref[:, :NOPE] = dkv_[:, :NOPE].astype(BF16)
        dkv_ref[:, NOPE:] = dv_ref[...].astype(BF16)
        _accumulate(dkpe_ref, unrope(dkv_[:, NOPE:]), h == 0)

    head = pl.BlockSpec((None, tm, ATT_DK), lambda i, h: (h, i, 0))
    tok = pl.BlockSpec((tm, LANES), lambda i, h: (i, 0))
    return _call(
        body, [dq, dk, dv, cosT, sinT], name="mla_bwd_prep", grid=(T // tm, H),
        in_specs=[head, head, pl.BlockSpec((None, tm, VDIM), lambda i, h: (h, i, 0)), tok, tok],
        out_specs=[head, head, tok],
        out_shape=[_sds((H, T, ATT_DK), BF16), _sds((H, T, ATT_DK), BF16), _sds((T, LANES), F32)],
        sem=("parallel", "arbitrary"), comm=comm)


def _fox_bwd_prep(dq, dk, proj, bfor, lay, B, S, inv_scale):
    H, T, _ = dq.shape

    def body(dq_ref, dk_ref, fl_ref, bf_ref, dfl_ref, dbf_ref, dc_sc):
        b, h = pl.program_id(0), pl.program_id(1)
        lane = _lane(dc_sc.shape)
        col = jnp.sum(jnp.where(lane == 0, dq_ref[...], 0.0) - jnp.where(lane == 3, dk_ref[...], 0.0),
                      axis=1, keepdims=True)

        @pl.when(h == 0)
        def _():
            dc_sc[...] = jnp.zeros(dc_sc.shape, F32)

        dc_sc[...] = jnp.where(lane == h, col, dc_sc[...])

        @pl.when(h == H - 1)
        def _():
            dlogf = _cumsum_rows(dc_sc[...] * inv_scale, reverse=True)
            z = fl_ref[...] + bf_ref[...]
            dz = jnp.where(lane < H, dlogf * (1.0 / (1.0 + jnp.exp(z))), 0.0)
            dfl_ref[...] = dz
            _accumulate(dbf_ref, jnp.sum(dz, axis=0, keepdims=True), b == 0)

    aug = pl.BlockSpec((None, S, LANES), lambda b, h: (h, b, 1))
    seq = pl.BlockSpec((S, LANES), lambda b, h: (b, 0))
    vec = pl.BlockSpec((1, LANES), lambda b, h: (0, 0))
    return _call(
        body, [dq, dk, proj, bfor], name="fox_bwd_prep", grid=(B, H),
        in_specs=[aug, aug, pl.BlockSpec((S, LANES), lambda b, h: (b, lay["fl"] // LANES)), vec],
        out_specs=[seq, vec], out_shape=[_sds((T, LANES), F32), _sds((1, LANES), F32)],
        scratch_shapes=[pltpu.VMEM((S, LANES), F32)], sem=("arbitrary", "arbitrary"))[0]


def _lat_bwd(dqn, dkvn, proj, gq, gkv, lay):
    T = dqn.shape[0]
    tm = _tile(T, ROW_TILE, 16)

    def body(dq_ref, dkv_ref, q_ref, kv_ref, gq_ref, gkv_ref, dql_ref, dkl_ref, dgq_ref, dgkv_ref):
        i = pl.program_id(0)
        dql, dgq = _rms_bwd(dq_ref[...], q_ref[...], gq_ref[...])
        dkl, dgkv = _rms_bwd(dkv_ref[...], kv_ref[...], gkv_ref[...])
        dql_ref[...] = dql.astype(BF16)
        dkl_ref[...] = dkl.astype(BF16)
        _accumulate(dgq_ref, jnp.sum(dgq, axis=0, keepdims=True), i == 0)
        _accumulate(dgkv_ref, jnp.sum(dgkv, axis=0, keepdims=True), i == 0)

    def blk(width, off=0):
        return pl.BlockSpec((tm, width), lambda i: (i, off // width))

    def vec(width):
        return pl.BlockSpec((1, width), lambda i: (0, 0))

    return _call(
        body, [dqn, dkvn, proj, proj, gq, gkv], name="lat_bwd", grid=(T // tm,),
        in_specs=[blk(Q_LORA), blk(KV_LORA), blk(Q_LORA, lay["q"]), blk(KV_LORA, lay["kv"]), vec(Q_LORA), vec(KV_LORA)],
        out_specs=[blk(Q_LORA), blk(KV_LORA), vec(Q_LORA), vec(KV_LORA)],
        out_shape=[_sds((T, Q_LORA), BF16), _sds((T, KV_LORA), BF16), _sds((1, Q_LORA), F32), _sds((1, KV_LORA), F32)],
        sem=("arbitrary",))[0]


def _dh_final(dproj, w_perm, dx1, x, g, comm=None):
    T, D = x.shape
    K = dproj.shape[1]
    tm = _tile(T, 2 * ROW_TILE, 16)
    tk = _tile(K, 1024)
    nk = K // tk
    NT = (((1,), (1,)), ((), ()))

    def body(a_ref, b_ref, dx1_ref, x_ref, g_ref, dx_ref, dg_ref):
        i, k = pl.program_id(0), pl.program_id(1)

        @pl.when(k == 0)
        def _():
            dx_ref[...] = jnp.zeros(dx_ref.shape, F32)

        dx_ref[...] += lax.dot_general(a_ref[...], b_ref[...], NT, preferred_element_type=F32)

        @pl.when(k == nk - 1)
        def _():
            d, dg = _rms_bwd(dx_ref[...], x_ref[...], g_ref[...])
            dx_ref[...] = dx1_ref[...] + d
            _accumulate(dg_ref, jnp.sum(dg, axis=0, keepdims=True), i == 0)

    row = pl.BlockSpec((tm, D), lambda i, k: (i, 0))
    vec = pl.BlockSpec((1, D), lambda i, k: (0, 0))
    return _call(body, [dproj, w_perm, dx1, x, g], name="mm_dh_final", grid=(T // tm, nk),
                 in_specs=[pl.BlockSpec((tm, tk), lambda i, k: (i, k)), pl.BlockSpec((D, tk), lambda i, k: (0, k)),
                           row, row, vec],
                 out_specs=[row, vec], out_shape=[_sds((T, D), F32), _sds((1, D), F32)],
                 sem=("arbitrary", "arbitrary"), comm=comm)


def _chip_sum(pieces, paired, qc, name):
    G, R, C = pieces.shape
    tr = _tile(R, 256, 16)

    def body(qc_ref, g_ref, p_ref, keep_ref, send_ref):
        s = pl.program_id(1)
        tot = g_ref[...] + p_ref[...]

        @pl.when(s == 0)
        def _():
            keep_ref[...] = tot

        @pl.when(s > 0)
        def _():
            send_ref[...] = tot.astype(send_ref.dtype)

    grid_spec = pltpu.PrefetchScalarGridSpec(
        num_scalar_prefetch=1, grid=(R // tr, N_CHIP),
        in_specs=[pl.BlockSpec((None, tr, C), lambda i, s, qc: (2 * (qc[0] ^ s) + qc[1], i, 0)),
                  pl.BlockSpec((None, tr, C), lambda i, s, qc: (qc[0] ^ s, i, 0))],
        out_specs=[pl.BlockSpec((tr, C), lambda i, s, qc: (i, 0)),
                   pl.BlockSpec((None, tr, C), lambda i, s, qc: (jnp.maximum(s - 1, 0), i, 0))])
    send_dtype = BF16 if R >= 16 else pieces.dtype
    return pl.pallas_call(
        body, name=name, grid_spec=grid_spec,
        out_shape=[_sds((R, C), F32), _sds((3, R, C), send_dtype)],
        compiler_params=pltpu.CompilerParams(dimension_semantics=("arbitrary", "arbitrary"),
                                             vmem_limit_bytes=VMEM_LIMIT_BYTES),
    )(qc, pieces, paired)


def _adamw_math(w, g, m, v):
    m = ADAM_B1 * m + (1.0 - ADAM_B1) * g
    v = ADAM_B2 * v + (1.0 - ADAM_B2) * (g * g)
    m_hat = m / (1.0 - ADAM_B1 ** ADAM_STEP)
    v_hat = v / (1.0 - ADAM_B2 ** ADAM_STEP)
    delta = -ADAM_LR * (m_hat / (jnp.sqrt(v_hat) + ADAM_EPS) + ADAM_WD * w)
    return delta, m, v


def _sum_adamw(keep, pieces, w, m, v, name):
    R, C = w.shape
    P = pieces.shape[0]
    tr = _tile(R, 256, 16)

    def body(k_ref, p_ref, w_ref, m_ref, v_ref, g_ref, d_ref, mo_ref, vo_ref):
        g = k_ref[...]
        for q in range(P):
            g = g + p_ref[q].astype(F32)
        g_ref[...] = g
        d_ref[...], mo_ref[...], vo_ref[...] = _adamw_math(w_ref[...], g, m_ref[...], v_ref[...])

    blk = pl.BlockSpec((tr, C), lambda i: (i, 0))
    pblk = pl.BlockSpec((P, tr, C), lambda i: (0, i, 0))
    return _call(body, [keep, pieces, w, m, v], name=name, grid=(R // tr,), in_specs=[blk, pblk, blk, blk, blk],
                 out_specs=[blk] * 4, out_shape=[_sds((R, C), F32)] * 4, sem=("parallel",))[0]


def _adamw_small(parts, w, m, v, widths):
    n = len(widths)

    def body(p_ref, w_ref, m_ref, v_ref, *o_refs):
        g = p_ref[0]
        for q in range(1, N_DEV):
            g = g + p_ref[q]
        vals = (g,) + _adamw_math(w_ref[...], g, m_ref[...], v_ref[...])
        off = 0
        for i, wd in enumerate(widths):
            for kind in range(4):
                o_refs[4 * i + kind][...] = vals[kind][:, off:off + wd]
            off += wd

    whole = pl.BlockSpec(memory_space=pltpu.VMEM)
    outs = _call(body, [parts, w, m, v], name="adamw_small", grid=(), in_specs=[whole] * 4,
                 out_specs=[whole] * (4 * n), out_shape=[_sds((1, wd), F32) for wd in widths for _ in range(4)])[0]
    return [tuple(outs[4 * i:4 * i + 4]) for i in range(n)]


def _layout(D):
    lay = {"q": 0, "kv": Q_LORA, "kpe": Q_LORA + KV_LORA}
    lay["fq"] = lay["kpe"] + LANES
    lay["fk"] = lay["fq"] + HEADS * FOX_DIM
    lay["fv"] = lay["fk"] + HEADS * FOX_DIM
    lay["fl"] = lay["fv"] + HEADS * FOX_DIM
    lay["g"] = lay["fl"] + LANES
    lay["end"] = lay["g"] + 2 * D
    return lay


def kernel(x, positions, pre_mix_norm, w_in, q_a_norm, w_uq, kv_a_norm, w_ukv, b_forget, b_gate, w_branch_mla, w_branch_fox, w_out, post_mix_norm, pre_ffn_norm, w_up, conv_w, conv_b, w_down, post_ffn_norm, loss_target, m_pre_mix_norm, m_w_in, m_q_a_norm, m_w_uq, m_kv_a_norm, m_w_ukv, m_b_forget, m_b_gate, m_w_branch_mla, m_w_branch_fox, m_w_out, m_post_mix_norm, m_pre_ffn_norm, m_w_up, m_conv_w, m_conv_b, m_w_down, m_post_ffn_norm, v_pre_mix_norm, v_w_in, v_q_a_norm, v_w_uq, v_kv_a_norm, v_w_ukv, v_b_forget, v_b_gate, v_w_branch_mla, v_w_branch_fox, v_w_out, v_post_mix_norm, v_pre_ffn_norm, v_w_up, v_conv_w, v_conv_b, v_w_down, v_post_ffn_norm):
    B, S, D = x.shape
    T = B * S
    F = conv_b.shape[0] // 2
    lay = _layout(D)
    n_in = w_in.shape[1]
    d_in = N_DEV * n_in
    seg_a = Q_LORA + KV_LORA + ROPE
    seg_b = 3 * HEADS * FOX_DIM + HEADS
    mla_scale = (NOPE + ROPE) ** -0.5
    fox_scale = FOX_DIM ** -0.5
    ax, ay, ac = (lax.axis_index(a) for a in MESH_AXES)
    qc = jnp.stack([2 * ax + ay, ac]).astype(jnp.int32)

    def row(vec, width=None):
        vec = vec.reshape(1, -1)
        if width is not None and vec.shape[1] < width:
            vec = jnp.pad(vec, ((0, 0), (0, width - vec.shape[1])))
        return vec

    x2 = x.reshape(T, D)
    win_s = _cast_bf16(w_in, "cast_w_in")
    h, (win_g,) = _prenorm(x2, row(pre_mix_norm), comm=_Comm([_GatherRelayPlan([win_s], mid_frac=0.3)]))
    small_s = [_cast_bf16(w, "cast_" + n) for w, n in
               [(w_uq, "w_uq"), (w_ukv, "w_ukv"), (w_branch_mla, "w_branch_mla"), (w_branch_fox, "w_branch_fox"), (w_out, "w_out")]]
    wup_s = _cast_bf16(w_up, "cast_w_up")
    wdown_s = _cast_bf16(w_down, "cast_w_down")

    def shard_cols(lo, hi):
        out = []
        for g in range(lo // n_in, (hi - 1) // n_in + 1):
            out.append(win_g[g][:, max(lo, g * n_in) - g * n_in:min(hi, (g + 1) * n_in) - g * n_in])
        return out

    w_perm = jnp.concatenate(
        shard_cols(0, seg_a) + [jnp.zeros((D, LANES - ROPE), BF16)] + shard_cols(seg_a, seg_a + seg_b)
        + [jnp.zeros((D, LANES - HEADS), BF16)] + shard_cols(seg_a + seg_b, d_in), axis=1)

    tgt = loss_target.reshape(T, D)
    pos = positions.reshape(T, 1)
    inv_freq = 1.0 / (ROPE_THETA ** (jnp.arange(0, ROPE, 2, dtype=F32) / ROPE))
    invf = row(jnp.concatenate([inv_freq, inv_freq]), LANES)
    g_pre, g_q, g_kv = row(pre_mix_norm), row(q_a_norm), row(kv_a_norm)
    g_pm, g_ffn, g_pf = row(post_mix_norm), row(pre_ffn_norm), row(post_ffn_norm)
    bfor = row(b_forget, LANES)
    bgate = row(b_gate)
    cb_full = row(conv_b)

    def own_plan(blocks):
        return _Comm([_GatherOwnPlan(blocks)])

    def pass_plan(gathered):
        return _Comm([_GatherPassPlan(gathered)])

    def pair_plan(gs):
        return _Comm([_PairScatterPlan(gs)])

    def chip_plan(gs):
        return _Comm([_ChipScatterPlan(gs)])

    half_d = D // 2
    proj, landed = _matmul(h, w_perm, mode="nn", name="mm_proj", comm=_Comm(
        [_GatherOwnPlan(small_s[:2] + [conv_w]), _GatherOwnPlan([wup_s], rows=(0, half_d))]))
    early_g, wup_part = landed[:-1], landed[-1:]
    (qn, kvn, kper, logf, cosT, sinT), (wuq_g, wukv_g, cw_g) = _split_prep(
        proj, pos, invf, g_q, g_kv, bfor, lay, comm=pass_plan(early_g))
    wuq_pad = jnp.pad(wuq_g, ((0, 0), (0, 0), (0, ATT_DK - NOPE - ROPE)))
    cw_full = jnp.transpose(cw_g, (1, 0, 2)).reshape(3, 2 * F)

    qraw = _matmul(qn, wuq_pad, mode="nn", name="mm_q", out_blocks=ATT_DK, tm=T)
    kvraw = _matmul(kvn, wukv_g, mode="nn", name="mm_kv", out_blocks=NOPE + VDIM, tm=T)
    (q_mla, k_mla, v_mla), branch_half = _mla_prep(qraw, kvraw, kper, cosT, sinT, comm=own_plan(small_s[2:4]))
    cs = _fox_cumsum(logf, B, S, 1.0 / fox_scale)
    (q_fox, k_fox, v_fox), wout_half = _fox_prep(proj, cs, lay, comm=own_plan(small_s[4:5]))
    ((o_mla, lse_mla), (o_fox, lse_fox)), landed = _attn_fwd(
        [(q_mla, k_mla, v_mla, MLA_UNIT, mla_scale), (q_fox, k_fox, v_fox, 1, fox_scale)], B=B, S=S,
        name="attn_fwd", comm=_Comm([_GatherOwnPlan([wup_s], rows=(half_d, D), into=wup_part),
                                     _GatherPassPlan(branch_half + wout_half)]))
    wup_half, (wbm_g, wbf_g, wout_g) = landed[:1], landed[1:]
    wbm = jnp.transpose(wbm_g, (1, 0, 2)).reshape(HEADS * VDIM, D)
    wbf = jnp.transpose(wbf_g, (1, 0, 2)).reshape(HEADS * FOX_DIM, D)
    wout = wout_g.reshape(D, D)
    (a_m, a_f, merged), (wup_g,) = _branch_merge(o_mla, o_fox, wbm, wbf, proj, bgate, lay, D,
                                                 comm=pass_plan(wup_half))
    n_up = wup_g.shape[2]
    y1, x1, h2 = _out_mid(merged, wout, x2, g_pm, g_ffn)
    u, wdown_half = _matmul(h2, wup_g, mode="nn", name="mm_up", tn=n_up, comm=own_plan([wdown_s]))
    (act,), (wdown_g,) = _convffn_fwd(u, cw_full, cb_full, B, S, F, comm=pass_plan(wdown_half))
    wdown = wdown_g.reshape(F, D)
    ff = _matmul(act, wdown, mode="nn", name="mm_down", tk=F // 2)
    dy, dff, loss_part, dg_pf = _tail(ff, x1, tgt, g_pf)

    dact = _matmul(dff, wdown, mode="nt", name="mm_dact", tn=F // 4)
    dw_down = _matmul(act, dff, mode="tn", name="mm_dw_down", tm=F // 4, tn=512).reshape(N_DEV, F // N_DEV, D)
    (du_g, du_v, dcp_g, dcp_v), (pa_down,) = _convffn_bwd(u, dact, cw_full, cb_full, B, S, F, comm=pair_plan([dw_down]))
    keep_down, sb_down = _chip_sum(dw_down, pa_down, qc, "chipsum_w_down")
    dh2, (rb_down,) = _matmul_halves((du_g, du_v), wup_g, mode="nt", name="mm_dh2", tm=MM_TILE, comm=chip_plan([sb_down]))
    dw_up = _matmul_halves(h2, (du_g, du_v), mode="tn", name="mm_dw_up", tm=MM_TILE, tn=n_up, tk=T // 2)
    (dx1, dy1, dg_ffn, dg_pm), _ = _mid_bwd(dy, dh2, x1, y1, g_ffn, g_pm)
    dw_out, pa_up_part = _matmul(merged, dy1, mode="tn", name="mm_dw_out",
                                 comm=_Comm([_PairScatterPlan([dw_up], rows=(0, half_d))]))
    dw_out = dw_out.reshape(N_DEV, D // N_DEV, D)
    (da_m, da_f, dgl_m, dgl_f, dbg_m, dbg_f), (pa_up,) = _gate_bwd(
        dy1, wout, a_m, a_f, proj, bgate, lay, D,
        comm=_Comm([_PairScatterPlan([dw_up], rows=(half_d, D), into=pa_up_part)]))
    keep_up, sb_up = _chip_sum(dw_up, pa_up, qc, "chipsum_w_up")
    dw_bm = _matmul(o_mla, da_m, mode="tn", name="mm_dw_branch_mla", out_blocks=D // N_DEV)
    dw_bf = _matmul(o_fox, da_f, mode="tn", name="mm_dw_branch_fox", out_blocks=D // N_DEV)
    mix = [dw_out, dw_bm, dw_bf]
    do_mla, pa_mix = _matmul(da_m, wbm, mode="nt", name="mm_do_mla", out_dtype=BF16, comm=pair_plan(mix))
    do_fox = _matmul(da_f, wbf, mode="nt", name="mm_do_fox", out_dtype=BF16)
    mix_sums = [_chip_sum(g, p, qc, "chipsum_" + n) for g, p, n in zip(mix, pa_mix, ["w_out", "w_branch_mla", "w_branch_fox"])]
    ((dq_m, dk_m, dv_m), (dq_f, dk_f, dv_f)), (rb_up,) = _attn_bwd(
        [(q_mla, k_mla, v_mla, o_mla, do_mla, lse_mla, MLA_UNIT, mla_scale),
         (q_fox, k_fox, v_fox, o_fox, do_fox, lse_fox, 1, fox_scale)], B=B, S=S, name="attn_bwd",
        comm=chip_plan([sb_up]))
    (dqraw, dkvraw, dkpe), _ = _mla_bwd_prep(dq_m, dk_m, dv_m, cosT, sinT)
    dqn = _matmul(dqraw, wuq_pad, mode="nt", name="mm_dqn", tm=T)
    dw_uq = _matmul(qn, dqraw, mode="tn", name="mm_dw_uq", out_blocks=ATT_DK)[:, :, :NOPE + ROPE]
    dkvn = _matmul(dkvraw, wukv_g, mode="nt", name="mm_dkvn", tm=T)
    dw_ukv = _matmul(kvn, dkvraw, mode="tn", name="mm_dw_ukv", out_blocks=NOPE + VDIM)
    dqlat, dkvlat, dg_q, dg_kv = _lat_bwd(dqn, dkvn, proj, g_q, g_kv, lay)
    dfl, dbfor = _fox_bwd_prep(dq_f, dk_f, proj, bfor, lay, B, S, 1.0 / fox_scale)
    dproj = _concat_cols([dqlat, dkvlat, dkpe, dq_f, dk_f, dv_f, dfl, dgl_m, dgl_f], "concat_dproj")
    dw_perm, rb_mix = _matmul(h, dproj, mode="tn", name="mm_dw_in", comm=chip_plan([s[1] for s in mix_sums]))
    segs = [(0, seg_a, 0), (seg_a, seg_a + seg_b, lay["fq"] - seg_a), (seg_a + seg_b, d_in, lay["g"] - seg_a - seg_b)]

    def piece(g):
        lo, hi = g * n_in, (g + 1) * n_in
        parts = [dw_perm[:, max(lo, s0) + sh:min(hi, s1) + sh] for s0, s1, sh in segs if max(lo, s0) < min(hi, s1)]
        return parts[0] if len(parts) == 1 else jnp.concatenate(parts, axis=1)

    dw_in = jnp.stack([piece(g) for g in range(N_DEV)])
    dcw = jnp.transpose(jnp.concatenate([dcp_g[0:3], dcp_v[0:3]], axis=1).reshape(3, N_DEV, (2 * F) // N_DEV), (1, 0, 2))
    late = [dw_in, dw_uq, dw_ukv, dcw]
    pa_late = _exchange_alone(pair_plan(late), "pair_late")
    late_sums = [_chip_sum(g, p, qc, "chipsum_" + n) for g, p, n in zip(late, pa_late, ["w_in", "w_uq", "w_ukv", "conv_w"])]
    (grad_x, dg_pre), rb_late = _dh_final(dproj, w_perm, dx1, x2, g_pre, comm=chip_plan([s[1] for s in late_sums]))

    big_out = {}

    def finish(n, keep, pieces, w, m, v):
        big_out[n] = _sum_adamw(keep, pieces, w, m, v, "adamw_" + n)

    finish("w_down", keep_down, rb_down, w_down, m_w_down, v_w_down)
    finish("w_up", keep_up, rb_up, w_up, m_w_up, v_w_up)
    finish("w_out", mix_sums[0][0], rb_mix[0], w_out, m_w_out, v_w_out)
    finish("w_branch_mla", mix_sums[1][0], rb_mix[1], w_branch_mla, m_w_branch_mla, v_w_branch_mla)
    finish("w_branch_fox", mix_sums[2][0], rb_mix[2], w_branch_fox, m_w_branch_fox, v_w_branch_fox)
    finish("w_in", late_sums[0][0], rb_late[0], w_in, m_w_in, v_w_in)
    finish("w_uq", late_sums[1][0], rb_late[1], w_uq, m_w_uq, v_w_uq)
    finish("w_ukv", late_sums[2][0], rb_late[2], w_ukv, m_w_ukv, v_w_ukv)
    finish("conv_w", late_sums[3][0], rb_late[3], conv_w, m_conv_w, v_conv_w)

    widths = [D, Q_LORA, KV_LORA, LANES, 2 * D, D, D, 2 * F, D]
    small_names = ["pre_mix_norm", "q_a_norm", "kv_a_norm", "b_forget", "b_gate", "post_mix_norm", "pre_ffn_norm",
                   "conv_b", "post_ffn_norm"]
    true_w = [D, Q_LORA, KV_LORA, HEADS, 2 * D, D, D, 2 * F, D]
    dcb = jnp.concatenate([dcp_g[3:4], dcp_v[3:4]], axis=1)
    part = jnp.concatenate([dg_pre, dg_q, dg_kv, dbfor, dbg_m, dbg_f, dg_pm, dg_ffn, dcb, dg_pf], axis=1)

    def pack(vals):
        return jnp.concatenate([row(a, wd) for a, wd in zip(vals, widths)], axis=1)

    sw = pack([pre_mix_norm, q_a_norm, kv_a_norm, b_forget, b_gate, post_mix_norm, pre_ffn_norm, conv_b, post_ffn_norm])
    sm = pack([m_pre_mix_norm, m_q_a_norm, m_kv_a_norm, m_b_forget, m_b_gate, m_post_mix_norm, m_pre_ffn_norm,
               m_conv_b, m_post_ffn_norm])
    sv = pack([v_pre_mix_norm, v_q_a_norm, v_kv_a_norm, v_b_forget, v_b_gate, v_post_mix_norm, v_pre_ffn_norm,
               v_conv_b, v_post_ffn_norm])
    (parts_all,) = _exchange_alone(_Comm([_DirectGatherPlan([part])]), "gather_small")
    small = _adamw_small(parts_all, sw, sm, sv, widths)
    small_out = {n: tuple(a.reshape(-1)[:tw] for a in vals) for n, vals, tw in zip(small_names, small, true_w)}

    loss = lax.psum(loss_part[0, 0], MESH_AXES)
    order = ["pre_mix_norm", "w_in", "q_a_norm", "w_uq", "kv_a_norm", "w_ukv", "b_forget", "b_gate", "w_branch_mla",
             "w_branch_fox", "w_out", "post_mix_norm", "pre_ffn_norm", "w_up", "conv_w", "conv_b", "w_down",
             "post_ffn_norm"]
    res = {**big_out, **small_out}
    outs = [loss, grad_x.reshape(B, S, D)]
    for kind in range(4):
        outs += [res[n][kind] for n in order]
    return tuple(outs)
```

```python
import math

import jax
import jax.numpy as jnp
from jax import lax
from jax.experimental import pallas as pl
from jax.experimental.pallas import tpu as pltpu

F32 = jnp.float32
BF16 = jnp.bfloat16

N_DEV = 8
N_CHIP = 4
HEADS = 8
NOPE = 128
ROPE = 64
HALF_ROPE = ROPE // 2
VDIM = 128
Q_LORA = 512
KV_LORA = 256
FOX_DIM = 128
ATT_DK = 256
MLA_UNIT = 64
ROPE_THETA = 10000.0
EPS = 1e-6
NEG_INF = -1e30
LANES = 128
LOG2_E = 1.4426950408889634

ADAM_LR = 0.001
ADAM_B1 = 0.9
ADAM_B2 = 0.999
ADAM_EPS = 1e-08
ADAM_WD = 0.01
ADAM_STEP = 10

VMEM_LIMIT_BYTES = 56 * 1024 * 1024
ROW_TILE = 256
HEAD_ROW_TILE = 2048
ATT_TILE = 1024
ATT_SUB = 256
ATT_AHEAD = 3
MM_TILE = 1024

MESH_AXES = ("x", "y", "c")
ANY = pl.BlockSpec(memory_space=pl.ANY)


def _tile(n, pref, align=LANES):
    if n <= pref:
        return n
    t = (pref // align) * align
    while t >= align:
        if n % t == 0:
            return t
        t -= align
    return n


def _sds(shape, dtype):
    return jax.ShapeDtypeStruct(shape, dtype)


def _coords():
    x, y, c = (lax.axis_index(ax) for ax in MESH_AXES)
    return x, y, c


def _chip_rel(x, y, r):
    return (1 - x if r & 2 else x), (1 - y if r & 1 else y)


def _rcopy(src, dst, sems, w, k, dev):
    return pltpu.make_async_remote_copy(src_ref=src, dst_ref=dst, send_sem=sems[0].at[w, k], recv_sem=sems[1].at[w, k],
                                        device_id=dev, device_id_type=pl.DeviceIdType.MESH)


class _GatherRelayPlan:
    def __init__(self, blocks, mid_frac=0.5):
        self.ins = list(blocks)
        self.out_shapes = [_sds((N_DEV,) + b.shape, b.dtype) for b in blocks]
        n = len(blocks)
        self.scratch = [pltpu.SemaphoreType.DMA((n, 7)), pltpu.SemaphoreType.DMA((n, 7)), pltpu.SemaphoreType.DMA((n,))]
        self.mid_frac = mid_frac

    @staticmethod
    def _places():
        x, y, c = _coords()
        xn, yn = 4 * (1 - x) + 2 * y, 4 * x + 2 * (1 - y)
        relay_src = 4 * (x + c * (1 - 2 * x)) + 2 * (y + (1 - c) * (1 - 2 * y)) + c
        relay_to = (x + (1 - c) * (1 - 2 * x), y + c * (1 - 2 * y), c)
        return x, y, c, xn, yn, relay_src, relay_to, 4 * (1 - x) + 2 * (1 - y)

    def first(self, ins, outs, sems):
        x, y, c, _, _, _, _, _ = self._places()
        me = 4 * x + 2 * y + c
        for w in range(len(ins)):
            pltpu.make_async_copy(ins[w], outs[w].at[me], sems[2].at[w]).start()
            _rcopy(ins[w], outs[w].at[me], sems, w, 0, (x, y, 1 - c)).start()
            _rcopy(ins[w], outs[w].at[me], sems, w, 1, (1 - x, y, c)).start()
            _rcopy(ins[w], outs[w].at[me], sems, w, 2, (x, 1 - y, c)).start()

    def mid(self, ins, outs, sems):
        x, y, c, xn, yn, relay_src, relay_to, _ = self._places()
        sib = (x, y, 1 - c)
        for w in range(len(ins)):
            bx, by = outs[w].at[xn + c], outs[w].at[yn + c]
            _rcopy(ins[w], bx, sems, w, 1, (1 - x, y, c)).wait_recv()
            _rcopy(ins[w], by, sems, w, 2, (x, 1 - y, c)).wait_recv()
            _rcopy(outs[w].at[relay_src], outs[w].at[relay_src], sems, w, 3, relay_to).start()
            _rcopy(bx, bx, sems, w, 4, sib).start()
            _rcopy(by, by, sems, w, 5, sib).start()

    def last(self, ins, outs, sems):
        x, y, c, xn, yn, _, relay_to, dg = self._places()
        me = 4 * x + 2 * y + c
        sib = (x, y, 1 - c)
        for w in range(len(ins)):
            bd = outs[w].at[dg + c]
            _rcopy(ins[w], bd, sems, w, 3, relay_to).wait_recv()
            _rcopy(bd, bd, sems, w, 6, sib).start()
            for k, blk in ((0, 4 * x + 2 * y), (4, xn), (5, yn), (6, dg)):
                _rcopy(ins[w], outs[w].at[blk + 1 - c], sems, w, k, sib).wait_recv()
            for k in range(7):
                _rcopy(ins[w], outs[w].at[me], sems, w, k, sib).wait_send()
            pltpu.make_async_copy(ins[w], outs[w].at[me], sems[2].at[w]).wait()


class _GatherOwnPlan:
    mid = None

    def __init__(self, blocks, rows=None, into=None):
        self.n = len(blocks)
        self.rows = rows
        self.ins = list(blocks) + list(into or [])
        self.out_shapes = [_sds((N_DEV,) + b.shape, b.dtype) for b in blocks]
        self.aliases = [(self.n + i, i) for i in range(len(into or []))]
        n = self.n
        self.scratch = [pltpu.SemaphoreType.DMA((n, 4)), pltpu.SemaphoreType.DMA((n, 4)), pltpu.SemaphoreType.DMA((n,))]

    def _cut(self, ref):
        return ref if self.rows is None else ref.at[pl.ds(self.rows[0], self.rows[1] - self.rows[0])]

    def first(self, ins, outs, sems):
        x, y, c = _coords()
        me = 4 * x + 2 * y + c
        for w in range(self.n):
            src, dst = self._cut(ins[w]), self._cut(outs[w].at[me])
            pltpu.make_async_copy(src, dst, sems[2].at[w]).start()
            _rcopy(src, dst, sems, w, 0, (x, y, 1 - c)).start()
            for r in (1, 2, 3):
                px, py = _chip_rel(x, y, r)
                _rcopy(src, dst, sems, w, r, (px, py, c)).start()

    def last(self, ins, outs, sems):
        x, y, c = _coords()
        me = 4 * x + 2 * y + c
        for w in range(self.n):
            src = self._cut(ins[w])
            cp = _rcopy(src, self._cut(outs[w].at[4 * x + 2 * y + 1 - c]), sems, w, 0, (x, y, 1 - c))
            cp.wait_recv()
            cp.wait_send()
            for r in (1, 2, 3):
                px, py = _chip_rel(x, y, r)
                cp = _rcopy(src, self._cut(outs[w].at[4 * px + 2 * py + c]), sems, w, r, (px, py, c))
                cp.wait_recv()
                cp.wait_send()
            pltpu.make_async_copy(src, self._cut(outs[w].at[me]), sems[2].at[w]).wait()


class _GatherPassPlan:
    mid = None

    def __init__(self, gathered):
        self.ins = list(gathered)
        self.out_shapes = [_sds(g.shape, g.dtype) for g in gathered]
        self.aliases = [(i, i) for i in range(len(gathered))]
        n = len(gathered)
        self.scratch = [pltpu.SemaphoreType.DMA((n, 3)), pltpu.SemaphoreType.DMA((n, 3))]

    def first(self, ins, outs, sems):
        x, y, c = _coords()
        for w in range(len(ins)):
            for r in (1, 2, 3):
                px, py = _chip_rel(x, y, r)
                blk = 4 * px + 2 * py + c
                _rcopy(ins[w].at[blk], outs[w].at[blk], sems, w, r - 1, (x, y, 1 - c)).start()

    def last(self, ins, outs, sems):
        x, y, c = _coords()
        for w in range(len(ins)):
            for r in (1, 2, 3):
                px, py = _chip_rel(x, y, r)
                blk = 4 * px + 2 * py + 1 - c
                cp = _rcopy(ins[w].at[blk], outs[w].at[blk], sems, w, r - 1, (x, y, 1 - c))
                cp.wait_recv()
                cp.wait_send()


class _DirectGatherPlan:
    mid = None

    def __init__(self, blocks):
        self.ins = list(blocks)
        self.out_shapes = [_sds((N_DEV,) + b.shape, b.dtype) for b in blocks]
        n = len(blocks)
        self.scratch = [pltpu.SemaphoreType.DMA((n, 7)), pltpu.SemaphoreType.DMA((n, 7)), pltpu.SemaphoreType.DMA((n,))]

    @staticmethod
    def _peer(x, y, c, r):
        return (1 - x if r & 4 else x), (1 - y if r & 2 else y), (1 - c if r & 1 else c)

    def first(self, ins, outs, sems):
        x, y, c = _coords()
        me = 4 * x + 2 * y + c
        for w in range(len(ins)):
            pltpu.make_async_copy(ins[w], outs[w].at[me], sems[2].at[w]).start()
            for r in range(1, N_DEV):
                _rcopy(ins[w], outs[w].at[me], sems, w, r - 1, self._peer(x, y, c, r)).start()

    def last(self, ins, outs, sems):
        x, y, c = _coords()
        me = 4 * x + 2 * y + c
        for w in range(len(ins)):
            for r in range(1, N_DEV):
                px, py, pc = self._peer(x, y, c, r)
                cp = _rcopy(ins[w], outs[w].at[4 * px + 2 * py + pc], sems, w, r - 1, (px, py, pc))
                cp.wait_recv()
                cp.wait_send()
            pltpu.make_async_copy(ins[w], outs[w].at[me], sems[2].at[w]).wait()


class _PairScatterPlan:
    mid = None

    def __init__(self, pieces, rows=None, into=None):
        self.n = len(pieces)
        self.rows = rows
        self.ins = list(pieces) + list(into or [])
        self.out_shapes = [_sds((N_CHIP,) + p.shape[1:], p.dtype) for p in pieces]
        self.aliases = [(self.n + i, i) for i in range(len(into or []))]
        self.scratch = [pltpu.SemaphoreType.DMA((self.n, N_CHIP)), pltpu.SemaphoreType.DMA((self.n, N_CHIP))]

    def _copies(self, ins, outs, sems):
        x, y, c = _coords()
        cps = []
        for w in range(self.n):
            for q in range(N_CHIP):
                src, dst = ins[w].at[2 * q + 1 - c], outs[w].at[q]
                if self.rows is not None:
                    cut = pl.ds(self.rows[0], self.rows[1] - self.rows[0])
                    src, dst = src.at[cut], dst.at[cut]
                cps.append(_rcopy(src, dst, sems, w, q, (x, y, 1 - c)))
        return cps

    def first(self, ins, outs, sems):
        for cp in self._copies(ins, outs, sems):
            cp.start()

    def last(self, ins, outs, sems):
        for cp in self._copies(ins, outs, sems):
            cp.wait_recv()
            cp.wait_send()


class _ChipScatterPlan:
    mid = None

    def __init__(self, sums, rows=None, into=None):
        self.n = len(sums)
        self.rows = rows
        self.ins = list(sums) + list(into or [])
        self.out_shapes = [_sds(s.shape, s.dtype) for s in sums]
        self.aliases = [(self.n + i, i) for i in range(len(into or []))]
        self.scratch = [pltpu.SemaphoreType.DMA((self.n, 3)), pltpu.SemaphoreType.DMA((self.n, 3))]

    def _copies(self, ins, outs, sems):
        x, y, c = _coords()
        cps = []
        for w in range(self.n):
            for r in (1, 2, 3):
                px, py = _chip_rel(x, y, r)
                src, dst = ins[w].at[r - 1], outs[w].at[r - 1]
                if self.rows is not None:
                    cut = pl.ds(self.rows[0], self.rows[1] - self.rows[0])
                    src, dst = src.at[cut], dst.at[cut]
                cps.append(_rcopy(src, dst, sems, w, r - 1, (px, py, c)))
        return cps

    def first(self, ins, outs, sems):
        for cp in self._copies(ins, outs, sems):
            cp.start()

    def last(self, ins, outs, sems):
        for cp in self._copies(ins, outs, sems):
            cp.wait_recv()
            cp.wait_send()


class _Comm:
    def __init__(self, plans):
        self.plans = list(plans)
        self.ins = [a for p in self.plans for a in p.ins]
        self.out_shapes = [s for p in self.plans for s in p.out_shapes]
        self.scratch = [s for p in self.plans for s in p.scratch]
        self.aliases = []
        i = o = 0
        for p in self.plans:
            self.aliases += [(i + a, o + b) for a, b in getattr(p, "aliases", [])]
            i, o = i + len(p.ins), o + len(p.out_shapes)

    def _parts(self, ins, outs, sems):
        i = o = s = 0
        for p in self.plans:
            yield p, ins[i:i + len(p.ins)], outs[o:o + len(p.out_shapes)], sems[s:s + len(p.scratch)]
            i, o, s = i + len(p.ins), o + len(p.out_shapes), s + len(p.scratch)

    def begin(self, step, nsteps, ins, outs, sems):
        @pl.when(step == 0)
        def _():
            for p, pi, po, ps in self._parts(ins, outs, sems):
                p.first(pi, po, ps)

        for p, pi, po, ps in self._parts(ins, outs, sems):
            if p.mid is not None:
                @pl.when(step == min(nsteps - 1, int(p.mid_frac * nsteps)))
                def _(p=p, pi=pi, po=po, ps=ps):
                    p.mid(pi, po, ps)

    def end(self, step, nsteps, ins, outs, sems):
        @pl.when(step == nsteps - 1)
        def _():
            for p, pi, po, ps in self._parts(ins, outs, sems):
                p.last(pi, po, ps)


def _call(body, args, *, name, grid, in_specs, out_specs, out_shape, scratch_shapes=(), sem=None, comm=None):
    in_specs, out_specs, out_shape, scratch_shapes = list(in_specs), list(out_specs), list(out_shape), list(scratch_shapes)
    if comm is None:
        res = pl.pallas_call(
            body, name=name, grid=grid, in_specs=in_specs, out_specs=out_specs, out_shape=out_shape,
            scratch_shapes=scratch_shapes,
            compiler_params=pltpu.CompilerParams(dimension_semantics=sem, vmem_limit_bytes=VMEM_LIMIT_BYTES),
        )(*args)
        return list(res), []
    n_in, n_out, n_sc = len(in_specs), len(out_specs), len(scratch_shapes)
    n_ci, n_co = len(comm.ins), len(comm.out_shapes)
    nsteps = math.prod(grid)

    def hosted(*refs):
        ins, cins = refs[:n_in], refs[n_in:n_in + n_ci]
        o0 = n_in + n_ci
        outs, couts = refs[o0:o0 + n_out], refs[o0 + n_out:o0 + n_out + n_co]
        s0 = o0 + n_out + n_co
        scr, csems = refs[s0:s0 + n_sc], refs[s0 + n_sc:]
        step = jnp.int32(0)
        for d in range(len(grid)):
            step = step * grid[d] + pl.program_id(d)
        comm.begin(step, nsteps, cins, couts, csems)
        body(*ins, *outs, *scr)
        comm.end(step, nsteps, cins, couts, csems)

    res = pl.pallas_call(
        hosted, name=name, grid=grid, in_specs=in_specs + [ANY] * n_ci, out_specs=out_specs + [ANY] * n_co,
        out_shape=out_shape + comm.out_shapes, scratch_shapes=scratch_shapes + comm.scratch,
        input_output_aliases={n_in + a: n_out + b for a, b in comm.aliases},
        compiler_params=pltpu.CompilerParams(dimension_semantics=("arbitrary",) * len(grid),
                                             vmem_limit_bytes=VMEM_LIMIT_BYTES, has_side_effects=True),
    )(*args, *comm.ins)
    return list(res[:n_out]), list(res[n_out:])


def _exchange_alone(comm, name):
    def body():
        pass

    return _call(body, [], name=name, grid=(), in_specs=[], out_specs=[], out_shape=[], comm=comm)[1]


def _matmul(a, b, *, mode, name, out_dtype=F32, out_blocks=None, tm=None, tn=None, tk=None, comm=None):
    tm = MM_TILE if tm is None else tm
    tn = MM_TILE if tn is None else tn
    a_blk = a.ndim == 3
    b_blk = b.ndim == 3
    if mode == "nn":
        M, K = a.shape
        N = b.shape[0] * b.shape[2] if b_blk else b.shape[1]
        dims = (((1,), (0,)), ((), ()))
    elif mode == "nt":
        M = a.shape[1] if a_blk else a.shape[0]
        K = a.shape[0] * a.shape[2] if a_blk else a.shape[1]
        N = b.shape[1] if b_blk else b.shape[0]
        dims = (((1,), (1,)), ((), ()))
    else:
        K, M = a.shape
        N = b.shape[0] * b.shape[2] if b_blk else b.shape[1]
        dims = (((0,), (0,)), ((), ()))

    tm = _tile(M, tm)
    tn = _tile(N, tn)
    if mode == "nt" and (a_blk or b_blk):
        tk = a.shape[2] if a_blk else b.shape[2]
    else:
        tk = _tile(K, K if tk is None else tk)
    if mode != "nt" and b_blk:
        tn = _tile(b.shape[2], tn)
    if out_blocks is not None:
        tn = _tile(out_blocks, tn)
    nk = K // tk
    grid = (M // tm, N // tn, nk)

    if mode == "nn":
        a_spec = pl.BlockSpec((tm, tk), lambda i, j, k: (i, k))
        if b_blk:
            rb = b.shape[2] // tn
            b_spec = pl.BlockSpec((None, tk, tn), lambda i, j, k: (j // rb, k, j % rb))
        else:
            b_spec = pl.BlockSpec((tk, tn), lambda i, j, k: (k, j))
    elif mode == "nt":
        if a_blk:
            a_spec = pl.BlockSpec((None, tm, tk), lambda i, j, k: (k, i, 0))
        else:
            a_spec = pl.BlockSpec((tm, tk), lambda i, j, k: (i, k))
        if b_blk:
            b_spec = pl.BlockSpec((None, tn, tk), lambda i, j, k: (k, j, 0))
        else:
            b_spec = pl.BlockSpec((tn, tk), lambda i, j, k: (j, k))
    else:
        a_spec = pl.BlockSpec((tk, tm), lambda i, j, k: (k, i))
        if b_blk:
            rb = b.shape[2] // tn
            b_spec = pl.BlockSpec((None, tk, tn), lambda i, j, k: (j // rb, k, j % rb))
        else:
            b_spec = pl.BlockSpec((tk, tn), lambda i, j, k: (k, j))

    if out_blocks is None:
        o_spec = pl.BlockSpec((tm, tn), lambda i, j, k: (i, j))
        o_shape = _sds((M, N), out_dtype)
    else:
        ro = out_blocks // tn
        o_spec = pl.BlockSpec((None, tm, tn), lambda i, j, k: (j // ro, i, j % ro))
        o_shape = _sds((N // out_blocks, M, out_blocks), out_dtype)

    direct = nk == 1 or out_dtype == F32

    def body(a_ref, b_ref, o_ref, *scratch):
        if nk == 1:
            o_ref[...] = lax.dot_general(a_ref[...], b_ref[...], dims, preferred_element_type=F32).astype(o_ref.dtype)
            return
        acc_ref = o_ref if direct else scratch[0]
        k = pl.program_id(2)

        @pl.when(k == 0)
        def _():
            acc_ref[...] = jnp.zeros(acc_ref.shape, F32)

        acc_ref[...] += lax.dot_general(a_ref[...], b_ref[...], dims, preferred_element_type=F32)
        if not direct:
            @pl.when(k == nk - 1)
            def _():
                o_ref[...] = acc_ref[...].astype(o_ref.dtype)

    scratch = [] if direct else [pltpu.VMEM((tm, tn), F32)]
    outs, landed = _call(body, [a, b], name=name, grid=grid, in_specs=[a_spec, b_spec], out_specs=[o_spec],
                         out_shape=[o_shape], scratch_shapes=scratch, sem=("parallel", "parallel", "arbitrary"), comm=comm)
    return outs[0] if comm is None else (outs[0], landed)


def _matmul_halves(a, b, *, mode, name, tm, tn=None, tk=None, comm=None):
    if mode == "nt":
        lo, hi = a
        M, kh = lo.shape
        G, N, kb = b.shape
        half = kh // kb
        tm, tn = _tile(M, tm), _tile(N, N if tn is None else tn)
        dims = (((1,), (1,)), ((), ()))

        def body(lo_ref, hi_ref, b_ref, o_ref):
            k = pl.program_id(2)

            @pl.when(k == 0)
            def _():
                o_ref[...] = jnp.zeros(o_ref.shape, F32)

            @pl.when(k < half)
            def _():
                o_ref[...] += lax.dot_general(lo_ref[...], b_ref[...], dims, preferred_element_type=F32)

            @pl.when(k >= half)
            def _():
                o_ref[...] += lax.dot_general(hi_ref[...], b_ref[...], dims, preferred_element_type=F32)

        outs, landed = _call(
            body, [lo, hi, b], name=name, grid=(M // tm, N // tn, G),
            in_specs=[pl.BlockSpec((tm, kb), lambda i, j, k: (i, jnp.minimum(k, half - 1))),
                      pl.BlockSpec((tm, kb), lambda i, j, k: (i, jnp.maximum(k - half, 0))),
                      pl.BlockSpec((None, tn, kb), lambda i, j, k: (k, j, 0))],
            out_specs=[pl.BlockSpec((tm, tn), lambda i, j, k: (i, j))], out_shape=[_sds((M, N), F32)],
            sem=("parallel", "parallel", "arbitrary"), comm=comm)
    else:
        lo, hi = b
        K, nh = lo.shape
        M = a.shape[1]
        n = tn
        half = nh // n
        tm, tk = _tile(M, tm), _tile(K, K if tk is None else tk)
        nk = K // tk
        dims = (((0,), (0,)), ((), ()))

        def body(a_ref, lo_ref, hi_ref, o_ref):
            j, k = pl.program_id(1), pl.program_id(2)

            @pl.when(k == 0)
            def _():
                o_ref[...] = jnp.zeros(o_ref.shape, F32)

            @pl.when(j < half)
            def _():
                o_ref[...] += lax.dot_general(a_ref[...], lo_ref[...], dims, preferred_element_type=F32)

            @pl.when(j >= half)
            def _():
                o_ref[...] += lax.dot_general(a_ref[...], hi_ref[...], dims, preferred_element_type=F32)

        outs, landed = _call(
            body, [a, lo, hi], name=name, grid=(M // tm, 2 * half, nk),
            in_specs=[pl.BlockSpec((tk, tm), lambda i, j, k: (k, i)),
                      pl.BlockSpec((tk, n), lambda i, j, k: (jnp.where(j < half, k, nk - 1), jnp.minimum(j, half - 1))),
                      pl.BlockSpec((tk, n), lambda i, j, k: (jnp.where(j >= half, k, 0), jnp.maximum(j - half, 0)))],
            out_specs=[pl.BlockSpec((None, tm, n), lambda i, j, k: (j, i, 0))],
            out_shape=[_sds((2 * half, M, n), F32)], sem=("parallel", "parallel", "arbitrary"), comm=comm)
    return outs[0] if comm is None else (outs[0], landed)


def _rms(x):
    return lax.rsqrt(jnp.mean(x * x, axis=-1, keepdims=True) + EPS)


def _rms_bwd(dy, x, g):
    r = _rms(x)
    n = x * r
    dn = dy * g
    dx = r * (dn - n * jnp.mean(dn * n, axis=-1, keepdims=True))
    return dx, dy * n


def _sigmoid(x):
    return 1.0 / (1.0 + jnp.exp(-x))


def _rope_rot(t):
    return pltpu.roll(t, HALF_ROPE, 1) - pltpu.roll(t, LANES - HALF_ROPE, 1)


def _lane(shape):
    return lax.broadcasted_iota(jnp.int32, shape, 1)


def _split3(x):
    hi = x.astype(BF16).astype(F32)
    r1 = x - hi
    mid = r1.astype(BF16).astype(F32)
    lo = (r1 - mid).astype(BF16).astype(F32)
    return hi, mid, lo


def _cumsum_rows(x, reverse):
    S = x.shape[0]
    bs = min(256, S)
    nb = S // bs
    r = lax.broadcasted_iota(jnp.int32, (bs, bs), 0)
    c = lax.broadcasted_iota(jnp.int32, (bs, bs), 1)
    tri = jnp.where((c >= r) if reverse else (c <= r), 1.0, 0.0).astype(BF16)
    edge = lax.broadcasted_iota(jnp.int32, (bs, x.shape[1]), 0) == (0 if reverse else bs - 1)
    carry = jnp.zeros((1, x.shape[1]), F32)
    outs = [None] * nb
    for bi in (range(nb - 1, -1, -1) if reverse else range(nb)):
        xb = x[bi * bs:(bi + 1) * bs, :]
        acc = carry
        for term in _split3(xb):
            acc = acc + jnp.dot(tri, term.astype(BF16), preferred_element_type=F32)
        outs[bi] = acc
        carry = jnp.sum(jnp.where(edge, acc, 0.0), axis=0, keepdims=True)
    return jnp.concatenate(outs, axis=0) if nb > 1 else outs[0]


def _gelu_parts(x):
    c0 = math.sqrt(2.0 / math.pi)
    inner = c0 * (x + 0.044715 * (x * x * x))
    t = jnp.tanh(inner)
    g = 0.5 * x * (1.0 + t)
    dg = 0.5 * (1.0 + t) + 0.5 * x * (1.0 - t * t) * (c0 * (1.0 + 3.0 * 0.044715 * (x * x)))
    return g, dg


def _accumulate(ref, value, first):
    @pl.when(first)
    def _():
        ref[...] = value

    @pl.when(jnp.logical_not(first))
    def _():
        ref[...] += value


def _cast_bf16(w, name):
    R, C = w.shape
    tr = _tile(R, 512, 16)

    def body(w_ref, o_ref):
        o_ref[...] = w_ref[...].astype(BF16)

    blk = pl.BlockSpec((tr, C), lambda i: (i, 0))
    return _call(body, [w], name=name, grid=(R // tr,), in_specs=[blk], out_specs=[blk],
                 out_shape=[_sds((R, C), BF16)], sem=("parallel",))[0][0]


def _concat_cols(parts, name):
    T = parts[0].shape[-2] if parts[0].ndim == 3 else parts[0].shape[0]
    widths = [p.shape[0] * LANES if p.ndim == 3 else p.shape[1] for p in parts]
    tm = _tile(T, ROW_TILE, 16)

    def body(*refs):
        o_ref = refs[-1]
        off = 0
        for p_ref, p, w in zip(refs[:-1], parts, widths):
            if p.ndim == 3:
                for hd in range(p.shape[0]):
                    o_ref[:, off + hd * LANES:off + (hd + 1) * LANES] = p_ref[hd].astype(BF16)
            else:
                o_ref[:, off:off + w] = p_ref[...].astype(BF16)
            off += w

    def spec(p, w):
        if p.ndim == 3:
            return pl.BlockSpec((p.shape[0], tm, LANES), lambda i: (0, i, 0))
        return pl.BlockSpec((tm, w), lambda i: (i, 0))

    return _call(body, parts, name=name, grid=(T // tm,),
                 in_specs=[spec(p, w) for p, w in zip(parts, widths)],
                 out_specs=[pl.BlockSpec((tm, sum(widths)), lambda i: (i, 0))],
                 out_shape=[_sds((T, sum(widths)), BF16)], sem=("parallel",))[0][0]


def _prenorm(x, g, comm=None):
    T, D = x.shape
    tm = _tile(T, ROW_TILE, 16)

    def body(x_ref, g_ref, h_ref):
        xv = x_ref[...]
        h_ref[...] = (xv * _rms(xv) * g_ref[...]).astype(BF16)

    row = pl.BlockSpec((tm, D), lambda i: (i, 0))
    (h,), landed = _call(body, [x, g], name="prenorm", grid=(T // tm,),
                         in_specs=[row, pl.BlockSpec((1, D), lambda i: (0, 0))], out_specs=[row],
                         out_shape=[_sds((T, D), BF16)], sem=("parallel",), comm=comm)
    return h, landed


def _split_prep(proj, pos, invf, gq, gkv, bfor, lay, comm=None):
    T = proj.shape[0]
    tm = _tile(T, ROW_TILE, 16)

    def body(q_ref, kv_ref, kpe_ref, fl_ref, pos_ref, invf_ref, gq_ref, gkv_ref, bf_ref,
             qn_ref, kvn_ref, kper_ref, logf_ref, cos_ref, sin_ref):
        ql = q_ref[...]
        qn_ref[...] = (ql * _rms(ql) * gq_ref[...]).astype(BF16)
        kl = kv_ref[...]
        kvn_ref[...] = (kl * _rms(kl) * gkv_ref[...]).astype(BF16)
        ang = pos_ref[...].astype(F32) * invf_ref[...]
        valid = _lane(ang.shape) < ROPE
        cs = jnp.where(valid, jnp.cos(ang), 0.0)
        sn = jnp.where(valid, jnp.sin(ang), 0.0)
        cos_ref[...] = cs
        sin_ref[...] = sn
        kp = jnp.where(valid, kpe_ref[...], 0.0)
        kper_ref[...] = (kp * cs + _rope_rot(kp) * sn).astype(BF16)
        z = fl_ref[...] + bf_ref[...]
        logf_ref[...] = jnp.minimum(z, 0.0) - jnp.log(1.0 + jnp.exp(-jnp.abs(z)))

    def col(width, off):
        return pl.BlockSpec((tm, width), lambda i: (i, off // width))

    def vec(width):
        return pl.BlockSpec((1, width), lambda i: (0, 0))

    def out(width):
        return pl.BlockSpec((tm, width), lambda i: (i, 0))

    return _call(
        body, [proj, proj, proj, proj, pos, invf, gq, gkv, bfor], name="split_prep", grid=(T // tm,),
        in_specs=[col(Q_LORA, lay["q"]), col(KV_LORA, lay["kv"]), col(LANES, lay["kpe"]), col(LANES, lay["fl"]),
                  pl.BlockSpec((tm, 1), lambda i: (i, 0)), vec(LANES), vec(Q_LORA), vec(KV_LORA), vec(LANES)],
        out_specs=[out(Q_LORA), out(KV_LORA), out(LANES), out(LANES), out(LANES), out(LANES)],
        out_shape=[_sds((T, Q_LORA), BF16), _sds((T, KV_LORA), BF16), _sds((T, LANES), BF16),
                   _sds((T, LANES), F32), _sds((T, LANES), F32), _sds((T, LANES), F32)],
        sem=("parallel",), comm=comm)


def _mla_prep(qraw, kvraw, kper, cosT, sinT, comm=None):
    H, T, _ = qraw.shape
    tm = _tile(T, HEAD_ROW_TILE, 16)

    def body(q_ref, kv_ref, kpe_ref, cos_ref, sin_ref, qo_ref, ko_ref, vo_ref):
        q = q_ref[...]
        pe = q[:, NOPE:]
        pe = jnp.where(_lane(pe.shape) < ROPE, pe, 0.0)
        qo_ref[:, :NOPE] = q[:, :NOPE].astype(BF16)
        qo_ref[:, NOPE:] = (pe * cos_ref[...] + _rope_rot(pe) * sin_ref[...]).astype(BF16)
        kv = kv_ref[...]
        ko_ref[:, :NOPE] = kv[:, :NOPE].astype(BF16)
        ko_ref[:, NOPE:] = kpe_ref[...]
        vo_ref[...] = kv[:, NOPE:].astype(BF16)

    head = pl.BlockSpec((None, tm, ATT_DK), lambda h, i: (h, i, 0))
    tok = pl.BlockSpec((tm, LANES), lambda h, i: (i, 0))
    return _call(
        body, [qraw, kvraw, kper, cosT, sinT], name="mla_prep", grid=(H, T // tm),
        in_specs=[head, head, tok, tok, tok],
        out_specs=[head, head, pl.BlockSpec((None, tm, VDIM), lambda h, i: (h, i, 0))],
        out_shape=[_sds((H, T, ATT_DK), BF16), _sds((H, T, ATT_DK), BF16), _sds((H, T, VDIM), BF16)],
        sem=("parallel", "parallel"), comm=comm)


def _fox_cumsum(logf, B, S, inv_scale):
    T = logf.shape[0]

    def body(l_ref, c_ref):
        c_ref[...] = _cumsum_rows(l_ref[...], reverse=False) * inv_scale

    seq = pl.BlockSpec((S, LANES), lambda b: (b, 0))
    return _call(body, [logf], name="fox_cumsum", grid=(B,), in_specs=[seq], out_specs=[seq],
                 out_shape=[_sds((T, LANES), F32)], sem=("parallel",))[0][0]


def _fox_prep(proj, cs, lay, comm=None):
    T = proj.shape[0]
    tm = _tile(T, HEAD_ROW_TILE, 16)

    def body(q_ref, k_ref, v_ref, cs_ref, qo_ref, ko_ref, vo_ref):
        h = pl.program_id(0)
        cv = cs_ref[...]
        lane = _lane(cv.shape)
        ccol = jnp.sum(jnp.where(lane == h, cv, 0.0), axis=1, keepdims=True)
        hi, mid, lo = _split3(ccol)
        one = jnp.where(lane < 6, 1.0, 0.0)
        augq = jnp.where(lane == 0, hi, jnp.where(lane == 1, mid, jnp.where(lane == 2, lo, one)))
        augk = jnp.where(lane < 3, 1.0, jnp.where(lane == 3, -hi, jnp.where(lane == 4, -mid, jnp.where(lane == 5, -lo, 0.0))))
        qo_ref[:, :FOX_DIM] = q_ref[...].astype(BF16)
        qo_ref[:, FOX_DIM:] = augq.astype(BF16)
        ko_ref[:, :FOX_DIM] = k_ref[...].astype(BF16)
        ko_ref[:, FOX_DIM:] = augk.astype(BF16)
        vo_ref[...] = v_ref[...].astype(BF16)

    def col(off):
        return pl.BlockSpec((tm, FOX_DIM), lambda h, i: (i, off // FOX_DIM + h))

    head = pl.BlockSpec((None, tm, ATT_DK), lambda h, i: (h, i, 0))
    return _call(
        body, [proj, proj, proj, cs], name="fox_prep", grid=(HEADS, T // tm),
        in_specs=[col(lay["fq"]), col(lay["fk"]), col(lay["fv"]), pl.BlockSpec((tm, LANES), lambda h, i: (i, 0))],
        out_specs=[head, head, pl.BlockSpec((None, tm, VDIM), lambda h, i: (h, i, 0))],
        out_shape=[_sds((HEADS, T, ATT_DK), BF16), _sds((HEADS, T, ATT_DK), BF16), _sds((HEADS, T, VDIM), BF16)],
        sem=("parallel", "parallel"), comm=comm)


def _visible(tq, tk, unit):
    r = lax.broadcasted_iota(jnp.int32, (tq, tk), 0)
    c = lax.broadcasted_iota(jnp.int32, (tq, tk), 1)
    sh = int(math.log2(unit))
    return lax.shift_right_logical(c, sh) <= lax.shift_right_logical(r, sh)


def _attn_fwd(streams, *, B, S, name, comm=None):
    n = len(streams)
    H, T, DK = streams[0][0].shape
    DV = streams[0][2].shape[2]
    tq = _tile(S, ATT_TILE)
    nq = S // tq
    sub = min(ATT_SUB, tq)
    NT = (((1,), (1,)), ((), ()))

    def body(*refs):
        ins, outs, (m_sc, acc_sc) = refs[:3 * n], refs[3 * n:5 * n], refs[5 * n:]
        i, j = pl.program_id(1), pl.program_id(2)

        @pl.when(j == 0)
        def _():
            m_sc[...] = jnp.full(m_sc.shape, NEG_INF, F32)
            acc_sc[...] = jnp.zeros(acc_sc.shape, F32)

        def step(diagonal):
            work = [(t, r) for r in range(tq // sub) for t in range(n)]

            def scores(t, r):
                q_ref, k_ref, _ = ins[3 * t:3 * t + 3]
                kc = (r + 1) * sub if diagonal else tq
                s = lax.dot_general(q_ref[r * sub:(r + 1) * sub, :], k_ref[0:kc, :], NT, preferred_element_type=F32)
                return s * (streams[t][4] * LOG2_E)

            ahead = [scores(*work[w]) for w in range(min(ATT_AHEAD, len(work)))]
            for w, (t, r) in enumerate(work):
                s = ahead.pop(0)
                if w + ATT_AHEAD < len(work):
                    ahead.append(scores(*work[w + ATT_AHEAD]))
                v_ref = ins[3 * t + 2]
                kc = s.shape[1]
                rows = slice(r * sub, (r + 1) * sub)
                if diagonal:
                    own = jnp.where(_visible(sub, sub, streams[t][3]), s[:, kc - sub:], NEG_INF)
                    s = own if kc == sub else jnp.concatenate([s[:, :kc - sub], own], axis=1)
                m_prev = m_sc[t, rows, :]
                mx = s[:, 0:LANES]
                for g in range(1, kc // LANES):
                    mx = jnp.maximum(mx, s[:, g * LANES:(g + 1) * LANES])
                m_new = jnp.maximum(m_prev, jnp.max(mx, axis=1, keepdims=True))
                alpha = jnp.exp2(m_prev - m_new)
                p = jnp.exp2(s - jnp.tile(m_new, (1, kc // LANES))).astype(BF16)
                v_aug = jnp.concatenate([v_ref[0:kc, :], jnp.ones((kc, LANES), BF16)], axis=1)
                acc_sc[t, rows, :] = jnp.tile(alpha, (1, 2)) * acc_sc[t, rows, :] + jnp.dot(
                    p, v_aug, preferred_element_type=F32)
                m_sc[t, rows, :] = m_new

        @pl.when(j < i)
        def _():
            step(False)

        @pl.when(j == i)
        def _():
            step(True)
            for t in range(n):
                o_ref, lse_ref = outs[2 * t:2 * t + 2]
                l = acc_sc[t, :, DV:]
                o_ref[...] = (acc_sc[t, :, :DV] / l).astype(BF16)
                lse_ref[...] = m_sc[t] + jnp.log2(l)

    def qmap(g, i, j):
        return (g % H, (g // H) * nq + i, 0)

    def kmap(g, i, j):
        return (g % H, (g // H) * nq + jnp.minimum(j, i), 0)

    args = [a for st in streams for a in st[:3]]
    outs, landed = _call(
        body, args, name=name, grid=(B * H, nq, nq),
        in_specs=[pl.BlockSpec((None, tq, DK), qmap), pl.BlockSpec((None, tq, DK), kmap),
                  pl.BlockSpec((None, tq, DV), kmap)] * n,
        out_specs=[pl.BlockSpec((tq, DV), lambda g, i, j: ((g // H) * nq + i, g % H)),
                   pl.BlockSpec((None, tq, LANES), qmap)] * n,
        out_shape=[_sds((T, H * DV), BF16), _sds((H, T, LANES), F32)] * n,
        scratch_shapes=[pltpu.VMEM((n, tq, LANES), F32), pltpu.VMEM((n, tq, DV + LANES), F32)],
        sem=("parallel", "parallel", "arbitrary"), comm=comm)
    return [(outs[2 * t], outs[2 * t + 1]) for t in range(n)], landed


def _attn_bwd(streams, *, B, S, name, comm=None):
    n = len(streams)
    H, T, DK = streams[0][0].shape
    DV = streams[0][2].shape[2]
    tq = _tile(S, ATT_TILE)
    nq = S // tq
    sub = min(ATT_SUB, tq)
    NT = (((1,), (1,)), ((), ()))
    TN = (((0,), (0,)), ((), ()))

    def body(*refs):
        ins, outs = refs[:6 * n], refs[6 * n:]
        j, i = pl.program_id(1), pl.program_id(2)

        @pl.when(jnp.logical_and(j == 0, i == 0))
        def _():
            for t in range(n):
                outs[3 * t][...] = jnp.zeros(outs[3 * t].shape, F32)

        @pl.when(i == 0)
        def _():
            for t in range(n):
                outs[3 * t + 1][...] = jnp.zeros(outs[3 * t + 1].shape, F32)
                outs[3 * t + 2][...] = jnp.zeros(outs[3 * t + 2].shape, F32)

        def step(diagonal):
            work = [(t, r) for r in range(tq // sub) for t in range(n)]

            def kcols(r):
                return (r + 1) * sub if diagonal else tq

            def scores(t, r):
                q_ref, k_ref, v_ref, _, do_ref, _ = ins[6 * t:6 * t + 6]
                rows, kc = slice(r * sub, (r + 1) * sub), kcols(r)
                s = lax.dot_general(q_ref[rows, :], k_ref[0:kc, :], NT, preferred_element_type=F32)
                dp = lax.dot_general(do_ref[rows, :], v_ref[0:kc, :], NT, preferred_element_type=F32)
                return s * (streams[t][7] * LOG2_E), dp

            def probs(t, r, s, dp):
                _, _, _, o_ref, do_ref, lse_ref = ins[6 * t:6 * t + 6]
                rows, kc = slice(r * sub, (r + 1) * sub), kcols(r)
                if diagonal:
                    own = jnp.where(_visible(sub, sub, streams[t][6]), s[:, kc - sub:], NEG_INF)
                    s = own if kc == sub else jnp.concatenate([s[:, :kc - sub], own], axis=1)
                p = jnp.exp2(s - jnp.tile(lse_ref[rows, :], (1, kc // LANES)))
                delta = jnp.sum(do_ref[rows, :].astype(F32) * o_ref[rows, :].astype(F32), axis=1, keepdims=True)
                return p.astype(BF16), (p * (dp - delta) * streams[t][7]).astype(BF16)

            def grads(t, r, p, ds):
                q_ref, k_ref, _, _, do_ref, _ = ins[6 * t:6 * t + 6]
                dq_ref, dk_ref, dv_ref = outs[3 * t:3 * t + 3]
                rows, kc = slice(r * sub, (r + 1) * sub), kcols(r)
                dv_ref[0:kc, :] += lax.dot_general(p, do_ref[rows, :], TN, preferred_element_type=F32)
                dk_ref[0:kc, :] += lax.dot_general(ds, q_ref[rows, :], TN, preferred_element_type=F32)
                qrows = pl.ds(pl.multiple_of(i * tq + r * sub, sub), sub)
                dq_ref[qrows, :] += jnp.dot(ds, k_ref[0:kc, :], preferred_element_type=F32)

            nw = len(work)
            sc = {w: scores(*work[w]) for w in range(min(2, nw))}
            pr = {0: probs(*work[0], *sc.pop(0))}
            for w in range(nw):
                if w + 2 < nw:
                    sc[w + 2] = scores(*work[w + 2])
                if w + 1 < nw:
                    pr[w + 1] = probs(*work[w + 1], *sc.pop(w + 1))
                grads(*work[w], *pr.pop(w))

        @pl.when(i > j)
        def _():
            step(False)

        @pl.when(i == j)
        def _():
            step(True)

    def qmap(g, j, i):
        return (g % H, (g // H) * nq + jnp.maximum(i, j), 0)

    def kmap(g, j, i):
        return (g % H, (g // H) * nq + j, 0)

    def omap(g, j, i):
        return ((g // H) * nq + jnp.maximum(i, j), g % H)

    args = [a for st in streams for a in st[:6]]
    outs, landed = _call(
        body, args, name=name, grid=(B * H, nq, nq),
        in_specs=[pl.BlockSpec((None, tq, DK), qmap), pl.BlockSpec((None, tq, DK), kmap),
                  pl.BlockSpec((None, tq, DV), kmap), pl.BlockSpec((tq, DV), omap), pl.BlockSpec((tq, DV), omap),
                  pl.BlockSpec((None, tq, LANES), qmap)] * n,
        out_specs=[pl.BlockSpec((None, S, DK), lambda g, j, i: (g % H, g // H, 0)),
                   pl.BlockSpec((None, tq, DK), kmap), pl.BlockSpec((None, tq, DV), kmap)] * n,
        out_shape=[_sds((H, T, DK), F32), _sds((H, T, DK), F32), _sds((H, T, DV), F32)] * n,
        sem=("parallel", "arbitrary", "arbitrary"), comm=comm)
    return [tuple(outs[3 * t:3 * t + 3]) for t in range(n)], landed


def _branch_merge(o_m, o_f, w_m, w_f, proj, bgate, lay, D, comm=None):
    T, K = o_m.shape
    tm = _tile(T, 2 * ROW_TILE, 16)
    tn = _tile(D, 1024)

    def body(om_ref, of_ref, wm_ref, wf_ref, gm_ref, gf_ref, bm_ref, bf_ref, am_ref, af_ref, o_ref):
        am = jnp.dot(om_ref[...], wm_ref[...], preferred_element_type=F32)
        af = jnp.dot(of_ref[...], wf_ref[...], preferred_element_type=F32)
        am_ref[...] = am
        af_ref[...] = af
        sm = _sigmoid(gm_ref[...] + bm_ref[...])
        sf = _sigmoid(gf_ref[...] + bf_ref[...])
        o_ref[...] = (sm * am + sf * af).astype(BF16)

    og = lay["g"] // tn
    blk = pl.BlockSpec((tm, tn), lambda i, j: (i, j))
    lhs = pl.BlockSpec((tm, K), lambda i, j: (i, 0))
    rhs = pl.BlockSpec((K, tn), lambda i, j: (0, j))
    return _call(
        body, [o_m, o_f, w_m, w_f, proj, proj, bgate, bgate], name="mm_branch_merge", grid=(T // tm, D // tn),
        in_specs=[lhs, lhs, rhs, rhs, pl.BlockSpec((tm, tn), lambda i, j: (i, og + j)),
                  pl.BlockSpec((tm, tn), lambda i, j: (i, og + D // tn + j)),
                  pl.BlockSpec((1, tn), lambda i, j: (0, j)), pl.BlockSpec((1, tn), lambda i, j: (0, D // tn + j))],
        out_specs=[blk, blk, blk], out_shape=[_sds((T, D), F32), _sds((T, D), F32), _sds((T, D), BF16)],
        sem=("parallel", "parallel"), comm=comm)


def _out_mid(merged, w_out, x, g_pm, g_ffn):
    T, D = x.shape
    tm = _tile(T, 2 * ROW_TILE, 16)

    def body(a_ref, w_ref, x_ref, gp_ref, gf_ref, y_ref, x1_ref, h2_ref):
        y = jnp.dot(a_ref[...], w_ref[...], preferred_element_type=F32)
        y_ref[...] = y
        x1 = x_ref[...] + y * _rms(y) * gp_ref[...]
        x1_ref[...] = x1
        h2_ref[...] = (x1 * _rms(x1) * gf_ref[...]).astype(BF16)

    row = pl.BlockSpec((tm, D), lambda i: (i, 0))
    vec = pl.BlockSpec((1, D), lambda i: (0, 0))
    return _call(body, [merged, w_out, x, g_pm, g_ffn], name="mm_out_mid", grid=(T // tm,),
                 in_specs=[row, pl.BlockSpec((D, D), lambda i: (0, 0)), row, vec, vec], out_specs=[row, row, row],
                 out_shape=[_sds((T, D), F32), _sds((T, D), F32), _sds((T, D), BF16)], sem=("parallel",))[0]


def _conv3(u, w_ref, bias):
    row = lax.broadcasted_iota(jnp.int32, u.shape, 0)
    u1 = jnp.where(row >= 1, pltpu.roll(u, 1, 0), 0.0)
    u2 = jnp.where(row >= 2, pltpu.roll(u, 2, 0), 0.0)
    return w_ref[0:1, :] * u2 + w_ref[1:2, :] * u1 + w_ref[2:3, :] * u + bias, u1, u2


def _convffn_fwd(u, cw, cb, B, S, F, comm=None):
    T = u.shape[0]
    tn = _tile(F, 256)
    nf = F // tn

    def body(ug_ref, uv_ref, wg_ref, wv_ref, bg_ref, bv_ref, a_ref):
        g, _, _ = _conv3(ug_ref[...], wg_ref, bg_ref[...])
        val, _, _ = _conv3(uv_ref[...], wv_ref, bv_ref[...])
        a_ref[...] = (_gelu_parts(g)[0] * val).astype(BF16)

    def seq(off):
        return pl.BlockSpec((S, tn), lambda b, j: (b, off + j))

    def par(rows, off):
        return pl.BlockSpec((rows, tn), lambda b, j: (0, off + j))

    return _call(body, [u, u, cw, cw, cb, cb], name="convffn_fwd", grid=(B, nf),
                 in_specs=[seq(0), seq(nf), par(3, 0), par(3, nf), par(1, 0), par(1, nf)],
                 out_specs=[seq(0)], out_shape=[_sds((T, F), BF16)], sem=("parallel", "parallel"), comm=comm)


def _convffn_bwd(u, dact, cw, cb, B, S, F, comm=None):
    T = u.shape[0]
    tn = _tile(F, 256)
    nf = F // tn
    TN = (((0,), (0,)), ((), ()))

    def body(ug_ref, uv_ref, da_ref, wg_ref, wv_ref, bg_ref, bv_ref, dug_ref, duv_ref, dpg_ref, dpv_ref):
        b = pl.program_id(1)
        ug, uv, da = ug_ref[...], uv_ref[...], da_ref[...]
        g, _, _ = _conv3(ug, wg_ref, bg_ref[...])
        val, _, _ = _conv3(uv, wv_ref, bv_ref[...])
        gel, dgel = _gelu_parts(g)
        dg = da * val * dgel
        dval = da * gel
        row = lax.broadcasted_iota(jnp.int32, ug.shape, 0)

        def back(d, w_ref):
            d1 = jnp.where(row < S - 1, pltpu.roll(d, S - 1, 0), 0.0)
            d2 = jnp.where(row < S - 2, pltpu.roll(d, S - 2, 0), 0.0)
            return w_ref[2:3, :] * d + w_ref[1:2, :] * d1 + w_ref[0:1, :] * d2, d1, d2

        dug, dg1, dg2 = back(dg, wg_ref)
        duv, dv1, dv2 = back(dval, wv_ref)
        dug_ref[...] = dug.astype(BF16)
        duv_ref[...] = duv.astype(BF16)

        def sums(d, d1, d2, u0):
            n = d.shape[1]
            eye = lax.broadcasted_iota(jnp.int32, (n, n), 0) == lax.broadcasted_iota(jnp.int32, (n, n), 1)
            r8 = lax.broadcasted_iota(jnp.int32, (8, n), 0)
            db = d.astype(BF16)
            out = jnp.zeros((8, n), F32)
            for k, t in enumerate((d2.astype(BF16), d1.astype(BF16), db)):
                m = lax.dot_general(t, u0.astype(BF16), TN, preferred_element_type=F32)
                out = jnp.where(r8 == k, jnp.sum(jnp.where(eye, m, 0.0), axis=0, keepdims=True), out)
            col = jnp.dot(jnp.ones((8, d.shape[0]), BF16), db, preferred_element_type=F32)
            return jnp.where(r8 == 3, col, out)

        _accumulate(dpg_ref, sums(dg, dg1, dg2, ug), b == 0)
        _accumulate(dpv_ref, sums(dval, dv1, dv2, uv), b == 0)

    def seq(off):
        return pl.BlockSpec((S, tn), lambda j, b: (b, off + j))

    def par(rows, off):
        return pl.BlockSpec((rows, tn), lambda j, b: (0, off + j))

    outs, landed = _call(
        body, [u, u, dact, cw, cw, cb, cb], name="convffn_bwd", grid=(nf, B),
        in_specs=[seq(0), seq(nf), seq(0), par(3, 0), par(3, nf), par(1, 0), par(1, nf)],
        out_specs=[seq(0), seq(0), par(8, 0), par(8, 0)],
        out_shape=[_sds((T, F), BF16), _sds((T, F), BF16), _sds((8, F), F32), _sds((8, F), F32)],
        sem=("parallel", "arbitrary"), comm=comm)
    return outs, landed


def _tail(ff, x1, tgt, g):
    T, D = ff.shape
    tm = _tile(T, ROW_TILE, 16)

    def body(ff_ref, x1_ref, t_ref, g_ref, dy_ref, dff_ref, loss_ref, dg_ref):
        i = pl.program_id(0)
        f = ff_ref[...]
        gv = g_ref[...]
        r = _rms(f)
        n = f * r
        e = (x1_ref[...] + n * gv) - t_ref[...]
        dy = e * (1.0 / D)
        dy_ref[...] = dy
        dn = dy * gv
        dff_ref[...] = (r * (dn - n * jnp.mean(dn * n, axis=-1, keepdims=True))).astype(BF16)
        part = 0.5 * jnp.sum(jnp.mean(e * e, axis=-1, keepdims=True), axis=0, keepdims=True)
        _accumulate(loss_ref, jnp.broadcast_to(part, loss_ref.shape), i == 0)
        _accumulate(dg_ref, jnp.sum(dy * n, axis=0, keepdims=True), i == 0)

    row = pl.BlockSpec((tm, D), lambda i: (i, 0))
    vec = pl.BlockSpec((1, D), lambda i: (0, 0))
    return _call(body, [ff, x1, tgt, g], name="tail", grid=(T // tm,), in_specs=[row, row, row, vec],
                 out_specs=[row, row, pl.BlockSpec((8, LANES), lambda i: (0, 0)), vec],
                 out_shape=[_sds((T, D), F32), _sds((T, D), BF16), _sds((8, LANES), F32), _sds((1, D), F32)],
                 sem=("arbitrary",))[0]


def _mid_bwd(dy, dh2, x1, y1, g_ffn, g_pm, comm=None):
    T, D = dy.shape
    tm = _tile(T, ROW_TILE, 16)

    def body(dy_ref, dh_ref, x1_ref, y1_ref, gf_ref, gp_ref, dx1_ref, dy1_ref, dgf_ref, dgp_ref):
        i = pl.program_id(0)
        dh = dh_ref[...]
        d2, dgf = _rms_bwd(dh, x1_ref[...], gf_ref[...])
        dx1 = dy_ref[...] + d2
        dx1_ref[...] = dx1
        d1, dgp = _rms_bwd(dx1, y1_ref[...], gp_ref[...])
        dy1_ref[...] = d1.astype(BF16)
        _accumulate(dgf_ref, jnp.sum(dgf, axis=0, keepdims=True), i == 0)
        _accumulate(dgp_ref, jnp.sum(dgp, axis=0, keepdims=True), i == 0)

    row = pl.BlockSpec((tm, D), lambda i: (i, 0))
    vec = pl.BlockSpec((1, D), lambda i: (0, 0))
    return _call(body, [dy, dh2, x1, y1, g_ffn, g_pm], name="mid_bwd", grid=(T // tm,),
                 in_specs=[row, row, row, row, vec, vec], out_specs=[row, row, vec, vec],
                 out_shape=[_sds((T, D), F32), _sds((T, D), BF16), _sds((1, D), F32), _sds((1, D), F32)],
                 sem=("arbitrary",), comm=comm)


def _gate_bwd(dy1, w_out, am, af, proj, bgate, lay, D, comm=None):
    T = dy1.shape[0]
    tm = _tile(T, 2 * ROW_TILE, 16)
    tn = _tile(D, 1024)
    NT = (((1,), (1,)), ((), ()))

    def body(dy_ref, w_ref, am_ref, af_ref, gm_ref, gf_ref, bm_ref, bf_ref,
             dam_ref, daf_ref, dgm_ref, dgf_ref, dbm_ref, dbf_ref):
        i = pl.program_id(1)
        d = lax.dot_general(dy_ref[...], w_ref[...], NT, preferred_element_type=F32)
        sm = _sigmoid(gm_ref[...] + bm_ref[...])
        sf = _sigmoid(gf_ref[...] + bf_ref[...])
        dam_ref[...] = (d * sm).astype(BF16)
        daf_ref[...] = (d * sf).astype(BF16)
        dgm = d * am_ref[...] * (sm * (1.0 - sm))
        dgf = d * af_ref[...] * (sf * (1.0 - sf))
        dgm_ref[...] = dgm.astype(BF16)
        dgf_ref[...] = dgf.astype(BF16)
        _accumulate(dbm_ref, jnp.sum(dgm, axis=0, keepdims=True), i == 0)
        _accumulate(dbf_ref, jnp.sum(dgf, axis=0, keepdims=True), i == 0)

    og = lay["g"] // tn
    blk = pl.BlockSpec((tm, tn), lambda j, i: (i, j))
    vec = pl.BlockSpec((1, tn), lambda j, i: (0, j))
    return _call(
        body, [dy1, w_out, am, af, proj, proj, bgate, bgate], name="mm_dmerged_gate_bwd", grid=(D // tn, T // tm),
        in_specs=[pl.BlockSpec((tm, D), lambda j, i: (i, 0)), pl.BlockSpec((tn, D), lambda j, i: (j, 0)),
                  blk, blk, pl.BlockSpec((tm, tn), lambda j, i: (i, og + j)),
                  pl.BlockSpec((tm, tn), lambda j, i: (i, og + D // tn + j)),
                  vec, pl.BlockSpec((1, tn), lambda j, i: (0, D // tn + j))],
        out_specs=[blk, blk, blk, blk, vec, vec],
        out_shape=[_sds((T, D), BF16)] * 4 + [_sds((1, D), F32)] * 2, sem=("parallel", "arbitrary"), comm=comm)


def _mla_bwd_prep(dq, dk, dv, cosT, sinT, comm=None):
    H, T, _ = dq.shape
    tm = _tile(T, HEAD_ROW_TILE, 16)

    def body(dq_ref, dk_ref, dv_ref, cos_ref, sin_ref, dqr_ref, dkv_ref, dkpe_ref):
        h = pl.program_id(1)
        cs, sn = cos_ref[...], sin_ref[...]
        valid = _lane(cs.shape) < ROPE

        def unrope(d):
            d = jnp.where(valid, d, 0.0)
            return d * cs - _rope_rot(d) * sn

        dqv = dq_ref[...]
        dqr_ref[:, :NOPE] = dqv[:, :NOPE].astype(BF16)
        dqr_ref[:, NOPE:] = unrope(dqv[:, NOPE:]).astype(BF16)
        dkv_ = dk_ref[...]
        dkv_ref[:, :NOPE] = dkv_[:, :NOPE].astype(BF16)
        dkv_ref[:, NOPE:] = dv_ref[...].astype(BF16)
        _accumulate(dkpe_ref, unrope(dkv_[:, NOPE:]), h == 0)

    head = pl.BlockSpec((None, tm, ATT_DK), lambda i, h: (h, i, 0))
    tok = pl.BlockSpec((tm, LANES), lambda i, h: (i, 0))
    return _call(
        body, [dq, dk, dv, cosT, sinT], name="mla_bwd_prep", grid=(T // tm, H),
        in_specs=[head, head, pl.BlockSpec((None, tm, VDIM), lambda i, h: (h, i, 0)), tok, tok],
        out_specs=[head, head, tok],
        out_shape=[_sds((H, T, ATT_DK), BF16), _sds((H, T, ATT_DK), BF16), _sds((T, LANES), F32)],
        sem=("parallel", "arbitrary"), comm=comm)


def _fox_bwd_prep(dq, dk, proj, bfor, lay, B, S, inv_scale):
    H, T, _ = dq.shape

    def body(dq_ref, dk_ref, fl_ref, bf_ref, dfl_ref, dbf_ref, dc_sc):
        b, h = pl.program_id(0), pl.program_id(1)
        lane = _lane(dc_sc.shape)
        col = jnp.sum(jnp.where(lane == 0, dq_ref[...], 0.0) - jnp.where(lane == 3, dk_ref[...], 0.0),
                      axis=1, keepdims=True)

        @pl.when(h == 0)
        def _():
            dc_sc[...] = jnp.zeros(dc_sc.shape, F32)

        dc_sc[...] = jnp.where(lane == h, col, dc_sc[...])

        @pl.when(h == H - 1)
        def _():
            dlogf = _cumsum_rows(dc_sc[...] * inv_scale, reverse=True)
            z = fl_ref[...] + bf_ref[...]
            dz = jnp.where(lane < H, dlogf * (1.0 / (1.0 + jnp.exp(z))), 0.0)
            dfl_ref[...] = dz
            _accumulate(dbf_ref, jnp.sum(dz, axis=0, keepdims=True), b == 0)

    aug = pl.BlockSpec((None, S, LANES), lambda b, h: (h, b, 1))
    seq = pl.BlockSpec((S, LANES), lambda b, h: (b, 0))
    vec = pl.BlockSpec((1, LANES), lambda b, h: (0, 0))
    return _call(
        body, [dq, dk, proj, bfor], name="fox_bwd_prep", grid=(B, H),
        in_specs=[aug, aug, pl.BlockSpec((S, LANES), lambda b, h: (b, lay["fl"] // LANES)), vec],
        out_specs=[seq, vec], out_shape=[_sds((T, LANES), F32), _sds((1, LANES), F32)],
        scratch_shapes=[pltpu.VMEM((S, LANES), F32)], sem=("arbitrary", "arbitrary"))[0]


def _lat_bwd(dqn, dkvn, proj, gq, gkv, lay):
    T = dqn.shape[0]
    tm = _tile(T, ROW_TILE, 16)

    def body(dq_ref, dkv_ref, q_ref, kv_ref, gq_ref, gkv_ref, dql_ref, dkl_ref, dgq_ref, dgkv_ref):
        i = pl.program_id(0)
        dql, dgq = _rms_bwd(dq_ref[...], q_ref[...], gq_ref[...])
        dkl, dgkv = _rms_bwd(dkv_ref[...], kv_ref[...], gkv_ref[...])
        dql_ref[...] = dql.astype(BF16)
        dkl_ref[...] = dkl.astype(BF16)
        _accumulate(dgq_ref, jnp.sum(dgq, axis=0, keepdims=True), i == 0)
        _accumulate(dgkv_ref, jnp.sum(dgkv, axis=0, keepdims=True), i == 0)

    def blk(width, off=0):
        return pl.BlockSpec((tm, width), lambda i: (i, off // width))

    def vec(width):
        return pl.BlockSpec((1, width), lambda i: (0, 0))

    return _call(
        body, [dqn, dkvn, proj, proj, gq, gkv], name="lat_bwd", grid=(T // tm,),
        in_specs=[blk(Q_LORA), blk(KV_LORA), blk(Q_LORA, lay["q"]), blk(KV_LORA, lay["kv"]), vec(Q_LORA), vec(KV_LORA)],
        out_specs=[blk(Q_LORA), blk(KV_LORA), vec(Q_LORA), vec(KV_LORA)],
        out_shape=[_sds((T, Q_LORA), BF16), _sds((T, KV_LORA), BF16), _sds((1, Q_LORA), F32), _sds((1, KV_LORA), F32)],
        sem=("arbitrary",))[0]


def _dh_final(dproj, w_perm, dx1, x, g, comm=None):
    T, D = x.shape
    K = dproj.shape[1]
    tm = _tile(T, 2 * ROW_TILE, 16)
    tk = _tile(K, 1024)
    nk = K // tk
    NT = (((1,), (1,)), ((), ()))

    def body(a_ref, b_ref, dx1_ref, x_ref, g_ref, dx_ref, dg_ref):
        i, k = pl.program_id(0), pl.program_id(1)

        @pl.when(k == 0)
        def _():
            dx_ref[...] = jnp.zeros(dx_ref.shape, F32)

        dx_ref[...] += lax.dot_general(a_ref[...], b_ref[...], NT, preferred_element_type=F32)

        @pl.when(k == nk - 1)
        def _():
            d, dg = _rms_bwd(dx_ref[...], x_ref[...], g_ref[...])
            dx_ref[...] = dx1_ref[...] + d
            _accumulate(dg_ref, jnp.sum(dg, axis=0, keepdims=True), i == 0)

    row = pl.BlockSpec((tm, D), lambda i, k: (i, 0))
    vec = pl.BlockSpec((1, D), lambda i, k: (0, 0))
    return _call(body, [dproj, w_perm, dx1, x, g], name="mm_dh_final", grid=(T // tm, nk),
                 in_specs=[pl.BlockSpec((tm, tk), lambda i, k: (i, k)), pl.BlockSpec((D, tk), lambda i, k: (0, k)),
                           row, row, vec],
                 out_specs=[row, vec], out_shape=[_sds((T, D), F32), _sds((1, D), F32)],
                 sem=("arbitrary", "arbitrary"), comm=comm)


def _chip_sum(pieces, paired, qc, name):
    G, R, C = pieces.shape
    tr = _tile(R, 256, 16)

    def body(qc_ref, g_ref, p_ref, keep_ref, send_ref):
        s = pl.program_id(1)
        tot = g_ref[...] + p_ref[...]

        @pl.when(s == 0)
        def _():
            keep_ref[...] = tot

        @pl.when(s > 0)
        def _():
            send_ref[...] = tot.astype(send_ref.dtype)

    grid_spec = pltpu.PrefetchScalarGridSpec(
        num_scalar_prefetch=1, grid=(R // tr, N_CHIP),
        in_specs=[pl.BlockSpec((None, tr, C), lambda i, s, qc: (2 * (qc[0] ^ s) + qc[1], i, 0)),
                  pl.BlockSpec((None, tr, C), lambda i, s, qc: (qc[0] ^ s, i, 0))],
        out_specs=[pl.BlockSpec((tr, C), lambda i, s, qc: (i, 0)),
                   pl.BlockSpec((None, tr, C), lambda i, s, qc: (jnp.maximum(s - 1, 0), i, 0))])
    send_dtype = BF16 if R >= 16 else pieces.dtype
    return pl.pallas_call(
        body, name=name, grid_spec=grid_spec,
        out_shape=[_sds((R, C), F32), _sds((3, R, C), send_dtype)],
        compiler_params=pltpu.CompilerParams(dimension_semantics=("arbitrary", "arbitrary"),
                                             vmem_limit_bytes=VMEM_LIMIT_BYTES),
    )(qc, pieces, paired)


def _adamw_math(w, g, m, v):
    m = ADAM_B1 * m + (1.0 - ADAM_B1) * g
    v = ADAM_B2 * v + (1.0 - ADAM_B2) * (g * g)
    m_hat = m / (1.0 - ADAM_B1 ** ADAM_STEP)
    v_hat = v / (1.0 - ADAM_B2 ** ADAM_STEP)
    delta = -ADAM_LR * (m_hat / (jnp.sqrt(v_hat) + ADAM_EPS) + ADAM_WD * w)
    return delta, m, v


def _sum_adamw(keep, pieces, w, m, v, name):
    R, C = w.shape
    P = pieces.shape[0]
    tr = _tile(R, 256, 16)

    def body(k_ref, p_ref, w_ref, m_ref, v_ref, g_ref, d_ref, mo_ref, vo_ref):
        g = k_ref[...]
        for q in range(P):
            g = g + p_ref[q].astype(F32)
        g_ref[...] = g
        d_ref[...], mo_ref[...], vo_ref[...] = _adamw_math(w_ref[...], g, m_ref[...], v_ref[...])

    blk = pl.BlockSpec((tr, C), lambda i: (i, 0))
    pblk = pl.BlockSpec((P, tr, C), lambda i: (0, i, 0))
    return _call(body, [keep, pieces, w, m, v], name=name, grid=(R // tr,), in_specs=[blk, pblk, blk, blk, blk],
                 out_specs=[blk] * 4, out_shape=[_sds((R, C), F32)] * 4, sem=("parallel",))[0]


def _adamw_small(parts, w, m, v, widths):
    n = len(widths)

    def body(p_ref, w_ref, m_ref, v_ref, *o_refs):
        g = p_ref[0]
        for q in range(1, N_DEV):
            g = g + p_ref[q]
        vals = (g,) + _adamw_math(w_ref[...], g, m_ref[...], v_ref[...])
        off = 0
        for i, wd in enumerate(widths):
            for kind in range(4):
                o_refs[4 * i + kind][...] = vals[kind][:, off:off + wd]
            off += wd

    whole = pl.BlockSpec(memory_space=pltpu.VMEM)
    outs = _call(body, [parts, w, m, v], name="adamw_small", grid=(), in_specs=[whole] * 4,
                 out_specs=[whole] * (4 * n), out_shape=[_sds((1, wd), F32) for wd in widths for _ in range(4)])[0]
    return [tuple(outs[4 * i:4 * i + 4]) for i in range(n)]


def _layout(D):
    lay = {"q": 0, "kv": Q_LORA, "kpe": Q_LORA + KV_LORA}
    lay["fq"] = lay["kpe"] + LANES
    lay["fk"] = lay["fq"] + HEADS * FOX_DIM
    lay["fv"] = lay["fk"] + HEADS * FOX_DIM
    lay["fl"] = lay["fv"] + HEADS * FOX_DIM
    lay["g"] = lay["fl"] + LANES
    lay["end"] = lay["g"] + 2 * D
    return lay


def kernel(x, positions, pre_mix_norm, w_in, q_a_norm, w_uq, kv_a_norm, w_ukv, b_forget, b_gate, w_branch_mla, w_branch_fox, w_out, post_mix_norm, pre_ffn_norm, w_up, conv_w, conv_b, w_down, post_ffn_norm, loss_target, m_pre_mix_norm, m_w_in, m_q_a_norm, m_w_uq, m_kv_a_norm, m_w_ukv, m_b_forget, m_b_gate, m_w_branch_mla, m_w_branch_fox, m_w_out, m_post_mix_norm, m_pre_ffn_norm, m_w_up, m_conv_w, m_conv_b, m_w_down, m_post_ffn_norm, v_pre_mix_norm, v_w_in, v_q_a_norm, v_w_uq, v_kv_a_norm, v_w_ukv, v_b_forget, v_b_gate, v_w_branch_mla, v_w_branch_fox, v_w_out, v_post_mix_norm, v_pre_ffn_norm, v_w_up, v_conv_w, v_conv_b, v_w_down, v_post_ffn_norm):
    B, S, D = x.shape
    T = B * S
    F = conv_b.shape[0] // 2
    lay = _layout(D)
    n_in = w_in.shape[1]
    d_in = N_DEV * n_in
    seg_a = Q_LORA + KV_LORA + ROPE
    seg_b = 3 * HEADS * FOX_DIM + HEADS
    mla_scale = (NOPE + ROPE) ** -0.5
    fox_scale = FOX_DIM ** -0.5
    ax, ay, ac = (lax.axis_index(a) for a in MESH_AXES)
    qc = jnp.stack([2 * ax + ay, ac]).astype(jnp.int32)

    def row(vec, width=None):
        vec = vec.reshape(1, -1)
        if width is not None and vec.shape[1] < width:
            vec = jnp.pad(vec, ((0, 0), (0, width - vec.shape[1])))
        return vec

    x2 = x.reshape(T, D)
    win_s = _cast_bf16(w_in, "cast_w_in")
    h, (win_g,) = _prenorm(x2, row(pre_mix_norm), comm=_Comm([_GatherRelayPlan([win_s], mid_frac=0.3)]))
    small_s = [_cast_bf16(w, "cast_" + n) for w, n in
               [(w_uq, "w_uq"), (w_ukv, "w_ukv"), (w_branch_mla, "w_branch_mla"), (w_branch_fox, "w_branch_fox"), (w_out, "w_out")]]
    wup_s = _cast_bf16(w_up, "cast_w_up")
    wdown_s = _cast_bf16(w_down, "cast_w_down")

    def shard_cols(lo, hi):
        out = []
        for g in range(lo // n_in, (hi - 1) // n_in + 1):
            out.append(win_g[g][:, max(lo, g * n_in) - g * n_in:min(hi, (g + 1) * n_in) - g * n_in])
        return out

    w_perm = jnp.concatenate(
        shard_cols(0, seg_a) + [jnp.zeros((D, LANES - ROPE), BF16)] + shard_cols(seg_a, seg_a + seg_b)
        + [jnp.zeros((D, LANES - HEADS), BF16)] + shard_cols(seg_a + seg_b, d_in), axis=1)

    tgt = loss_target.reshape(T, D)
    pos = positions.reshape(T, 1)
    inv_freq = 1.0 / (ROPE_THETA ** (jnp.arange(0, ROPE, 2, dtype=F32) / ROPE))
    invf = row(jnp.concatenate([inv_freq, inv_freq]), LANES)
    g_pre, g_q, g_kv = row(pre_mix_norm), row(q_a_norm), row(kv_a_norm)
    g_pm, g_ffn, g_pf = row(post_mix_norm), row(pre_ffn_norm), row(post_ffn_norm)
    bfor = row(b_forget, LANES)
    bgate = row(b_gate)
    cb_full = row(conv_b)

    def own_plan(blocks):
        return _Comm([_GatherOwnPlan(blocks)])

    def pass_plan(gathered):
        return _Comm([_GatherPassPlan(gathered)])

    def pair_plan(gs):
        return _Comm([_PairScatterPlan(gs)])

    def chip_plan(gs):
        return _Comm([_ChipScatterPlan(gs)])

    half_d = D // 2
    proj, landed = _matmul(h, w_perm, mode="nn", name="mm_proj", comm=_Comm(
        [_GatherOwnPlan(small_s[:2] + [conv_w]), _GatherOwnPlan([wup_s], rows=(0, half_d))]))
    early_g, wup_part = landed[:-1], landed[-1:]
    (qn, kvn, kper, logf, cosT, sinT), (wuq_g, wukv_g, cw_g) = _split_prep(
        proj, pos, invf, g_q, g_kv, bfor, lay, comm=pass_plan(early_g))
    wuq_pad = jnp.pad(wuq_g, ((0, 0), (0, 0), (0, ATT_DK - NOPE - ROPE)))
    cw_full = jnp.transpose(cw_g, (1, 0, 2)).reshape(3, 2 * F)

    qraw = _matmul(qn, wuq_pad, mode="nn", name="mm_q", out_blocks=ATT_DK, tm=T)
    kvraw = _matmul(kvn, wukv_g, mode="nn", name="mm_kv", out_blocks=NOPE + VDIM, tm=T)
    (q_mla, k_mla, v_mla), branch_half = _mla_prep(qraw, kvraw, kper, cosT, sinT, comm=own_plan(small_s[2:4]))
    cs = _fox_cumsum(logf, B, S, 1.0 / fox_scale)
    (q_fox, k_fox, v_fox), wout_half = _fox_prep(proj, cs, lay, comm=own_plan(small_s[4:5]))
    ((o_mla, lse_mla), (o_fox, lse_fox)), landed = _attn_fwd(
        [(q_mla, k_mla, v_mla, MLA_UNIT, mla_scale), (q_fox, k_fox, v_fox, 1, fox_scale)], B=B, S=S,
        name="attn_fwd", comm=_Comm([_GatherOwnPlan([wup_s], rows=(half_d, D), into=wup_part),
                                     _GatherPassPlan(branch_half + wout_half)]))
    wup_half, (wbm_g, wbf_g, wout_g) = landed[:1], landed[1:]
    wbm = jnp.transpose(wbm_g, (1, 0, 2)).reshape(HEADS * VDIM, D)
    wbf = jnp.transpose(wbf_g, (1, 0, 2)).reshape(HEADS * FOX_DIM, D)
    wout = wout_g.reshape(D, D)
    (a_m, a_f, merged), (wup_g,) = _branch_merge(o_mla, o_fox, wbm, wbf, proj, bgate, lay, D,
                                                 comm=pass_plan(wup_half))
    n_up = wup_g.shape[2]
    y1, x1, h2 = _out_mid(merged, wout, x2, g_pm, g_ffn)
    u, wdown_half = _matmul(h2, wup_g, mode="nn", name="mm_up", tn=n_up, comm=own_plan([wdown_s]))
    (act,), (wdown_g,) = _convffn_fwd(u, cw_full, cb_full, B, S, F, comm=pass_plan(wdown_half))
    wdown = wdown_g.reshape(F, D)
    ff = _matmul(act, wdown, mode="nn", name="mm_down", tk=F // 2)
    dy, dff, loss_part, dg_pf = _tail(ff, x1, tgt, g_pf)

    dact = _matmul(dff, wdown, mode="nt", name="mm_dact", tn=F // 4)
    dw_down = _matmul(act, dff, mode="tn", name="mm_dw_down", tm=F // 4, tn=512).reshape(N_DEV, F // N_DEV, D)
    (du_g, du_v, dcp_g, dcp_v), (pa_down,) = _convffn_bwd(u, dact, cw_full, cb_full, B, S, F, comm=pair_plan([dw_down]))
    keep_down, sb_down = _chip_sum(dw_down, pa_down, qc, "chipsum_w_down")
    dh2, (rb_down,) = _matmul_halves((du_g, du_v), wup_g, mode="nt", name="mm_dh2", tm=MM_TILE, comm=chip_plan([sb_down]))
    dw_up = _matmul_halves(h2, (du_g, du_v), mode="tn", name="mm_dw_up", tm=MM_TILE, tn=n_up, tk=T // 2)
    (dx1, dy1, dg_ffn, dg_pm), _ = _mid_bwd(dy, dh2, x1, y1, g_ffn, g_pm)
    dw_out, pa_up_part = _matmul(merged, dy1, mode="tn", name="mm_dw_out",
                                 comm=_Comm([_PairScatterPlan([dw_up], rows=(0, half_d))]))
    dw_out = dw_out.reshape(N_DEV, D // N_DEV, D)
    (da_m, da_f, dgl_m, dgl_f, dbg_m, dbg_f), (pa_up,) = _gate_bwd(
        dy1, wout, a_m, a_f, proj, bgate, lay, D,
        comm=_Comm([_PairScatterPlan([dw_up], rows=(half_d, D), into=pa_up_part)]))
    keep_up, sb_up = _chip_sum(dw_up, pa_up, qc, "chipsum_w_up")
    dw_bm = _matmul(o_mla, da_m, mode="tn", name="mm_dw_branch_mla", out_blocks=D // N_DEV)
    dw_bf = _matmul(o_fox, da_f, mode="tn", name="mm_dw_branch_fox", out_blocks=D // N_DEV)
    mix = [dw_out, dw_bm, dw_bf]
    do_mla, pa_mix = _matmul(da_m, wbm, mode="nt", name="mm_do_mla", out_dtype=BF16, comm=pair_plan(mix))
    do_fox = _matmul(da_f, wbf, mode="nt", name="mm_do_fox", out_dtype=BF16)
    mix_sums = [_chip_sum(g, p, qc, "chipsum_" + n) for g, p, n in zip(mix, pa_mix, ["w_out", "w_branch_mla", "w_branch_fox"])]
    ((dq_m, dk_m, dv_m), (dq_f, dk_f, dv_f)), (rb_up,) = _attn_bwd(
        [(q_mla, k_mla, v_mla, o_mla, do_mla, lse_mla, MLA_UNIT, mla_scale),
         (q_fox, k_fox, v_fox, o_fox, do_fox, lse_fox, 1, fox_scale)], B=B, S=S, name="attn_bwd",
        comm=chip_plan([sb_up]))
    (dqraw, dkvraw, dkpe), _ = _mla_bwd_prep(dq_m, dk_m, dv_m, cosT, sinT)
    dqn = _matmul(dqraw, wuq_pad, mode="nt", name="mm_dqn", tm=T)
    dw_uq = _matmul(qn, dqraw, mode="tn", name="mm_dw_uq", out_blocks=ATT_DK)[:, :, :NOPE + ROPE]
    dkvn = _matmul(dkvraw, wukv_g, mode="nt", name="mm_dkvn", tm=T)
    dw_ukv = _matmul(kvn, dkvraw, mode="tn", name="mm_dw_ukv", out_blocks=NOPE + VDIM)
    dqlat, dkvlat, dg_q, dg_kv = _lat_bwd(dqn, dkvn, proj, g_q, g_kv, lay)
    dfl, dbfor = _fox_bwd_prep(dq_f, dk_f, proj, bfor, lay, B, S, 1.0 / fox_scale)
    dproj = _concat_cols([dqlat, dkvlat, dkpe, dq_f, dk_f, dv_f, dfl, dgl_m, dgl_f], "concat_dproj")
    dw_perm, rb_mix = _matmul(h, dproj, mode="tn", name="mm_dw_in", comm=chip_plan([s[1] for s in mix_sums]))
    segs = [(0, seg_a, 0), (seg_a, seg_a + seg_b, lay["fq"] - seg_a), (seg_a + seg_b, d_in, lay["g"] - seg_a - seg_b)]

    def piece(g):
        lo, hi = g * n_in, (g + 1) * n_in
        parts = [dw_perm[:, max(lo, s0) + sh:min(hi, s1) + sh] for s0, s1, sh in segs if max(lo, s0) < min(hi, s1)]
        return parts[0] if len(parts) == 1 else jnp.concatenate(parts, axis=1)

    dw_in = jnp.stack([piece(g) for g in range(N_DEV)])
    dcw = jnp.transpose(jnp.concatenate([dcp_g[0:3], dcp_v[0:3]], axis=1).reshape(3, N_DEV, (2 * F) // N_DEV), (1, 0, 2))
    late = [dw_in, dw_uq, dw_ukv, dcw]
    pa_late = _exchange_alone(pair_plan(late), "pair_late")
    late_sums = [_chip_sum(g, p, qc, "chipsum_" + n) for g, p, n in zip(late, pa_late, ["w_in", "w_uq", "w_ukv", "conv_w"])]
    (grad_x, dg_pre), rb_late = _dh_final(dproj, w_perm, dx1, x2, g_pre, comm=chip_plan([s[1] for s in late_sums]))

    big_out = {}

    def finish(n, keep, pieces, w, m, v):
        big_out[n] = _sum_adamw(keep, pieces, w, m, v, "adamw_" + n)

    finish("w_down", keep_down, rb_down, w_down, m_w_down, v_w_down)
    finish("w_up", keep_up, rb_up, w_up, m_w_up, v_w_up)
    finish("w_out", mix_sums[0][0], rb_mix[0], w_out, m_w_out, v_w_out)
    finish("w_branch_mla", mix_sums[1][0], rb_mix[1], w_branch_mla, m_w_branch_mla, v_w_branch_mla)
    finish("w_branch_fox", mix_sums[2][0], rb_mix[2], w_branch_fox, m_w_branch_fox, v_w_branch_fox)
    finish("w_in", late_sums[0][0], rb_late[0], w_in, m_w_in, v_w_in)
    finish("w_uq", late_sums[1][0], rb_late[1], w_uq, m_w_uq, v_w_uq)
    finish("w_ukv", late_sums[2][0], rb_late[2], w_ukv, m_w_ukv, v_w_ukv)
    finish("conv_w", late_sums[3][0], rb_late[3], conv_w, m_conv_w, v_conv_w)

    widths = [D, Q_LORA, KV_LORA, LANES, 2 * D, D, D, 2 * F, D]
    small_names = ["pre_mix_norm", "q_a_norm", "kv_a_norm", "b_forget", "b_gate", "post_mix_norm", "pre_ffn_norm",
                   "conv_b", "post_ffn_norm"]
    true_w = [D, Q_LORA, KV_LORA, HEADS, 2 * D, D, D, 2 * F, D]
    dcb = jnp.concatenate([dcp_g[3:4], dcp_v[3:4]], axis=1)
    part = jnp.concatenate([dg_pre, dg_q, dg_kv, dbfor, dbg_m, dbg_f, dg_pm, dg_ffn, dcb, dg_pf], axis=1)

    def pack(vals):
        return jnp.concatenate([row(a, wd) for a, wd in zip(vals, widths)], axis=1)

    sw = pack([pre_mix_norm, q_a_norm, kv_a_norm, b_forget, b_gate, post_mix_norm, pre_ffn_norm, conv_b, post_ffn_norm])
    sm = pack([m_pre_mix_norm, m_q_a_norm, m_kv_a_norm, m_b_forget, m_b_gate, m_post_mix_norm, m_pre_ffn_norm,
               m_conv_b, m_post_ffn_norm])
    sv = pack([v_pre_mix_norm, v_q_a_norm, v_kv_a_norm, v_b_forget, v_b_gate, v_post_mix_norm, v_pre_ffn_norm,
               v_conv_b, v_post_ffn_norm])
    (parts_all,) = _exchange_alone(_Comm([_DirectGatherPlan([part])]), "gather_small")
    small = _adamw_small(parts_all, sw, sm, sv, widths)
    small_out = {n: tuple(a.reshape(-1)[:tw] for a in vals) for n, vals, tw in zip(small_names, small, true_w)}

    loss = lax.psum(loss_part[0, 0], MESH_AXES)
    order = ["pre_mix_norm", "w_in", "q_a_norm", "w_uq", "kv_a_norm", "w_ukv", "b_forget", "b_gate", "w_branch_mla",
             "w_branch_fox", "w_out", "post_mix_norm", "pre_ffn_norm", "w_up", "conv_w", "conv_b", "w_down",
             "post_ffn_norm"]
    res = {**big_out, **small_out}
    outs = [loss, grad_x.reshape(B, S, D)]
    for kind in range(4):
        outs += [res[n][kind] for n in order]
    return tuple(outs)
```

```python
import math

import jax
import jax.numpy as jnp
from jax import lax
from jax.experimental import pallas as pl
from jax.experimental.pallas import tpu as pltpu

F32 = jnp.float32
BF16 = jnp.bfloat16

N_DEV = 8
N_CHIP = 4
HEADS = 8
NOPE = 128
ROPE = 64
HALF_ROPE = ROPE // 2
VDIM = 128
Q_LORA = 512
KV_LORA = 256
FOX_DIM = 128
ATT_DK = 256
MLA_UNIT = 64
ROPE_THETA = 10000.0
EPS = 1e-6
NEG_INF = -1e30
LANES = 128
LOG2_E = 1.4426950408889634

ADAM_LR = 0.001
ADAM_B1 = 0.9
ADAM_B2 = 0.999
ADAM_EPS = 1e-08
ADAM_WD = 0.01
ADAM_STEP = 10

VMEM_LIMIT_BYTES = 56 * 1024 * 1024
ROW_TILE = 256
HEAD_ROW_TILE = 2048
ATT_TILE = 1024
ATT_SUB = 256
ATT_AHEAD = 3
MM_TILE = 1024

MESH_AXES = ("x", "y", "c")
ANY = pl.BlockSpec(memory_space=pl.ANY)


def _tile(n, pref, align=LANES):
    if n <= pref:
        return n
    t = (pref // align) * align
    while t >= align:
        if n % t == 0:
            return t
        t -= align
    return n


def _sds(shape, dtype):
    return jax.ShapeDtypeStruct(shape, dtype)


def _coords():
    x, y, c = (lax.axis_index(ax) for ax in MESH_AXES)
    return x, y, c


def _chip_rel(x, y, r):
    return (1 - x if r & 2 else x), (1 - y if r & 1 else y)


def _rcopy(src, dst, sems, w, k, dev):
    return pltpu.make_async_remote_copy(src_ref=src, dst_ref=dst, send_sem=sems[0].at[w, k], recv_sem=sems[1].at[w, k],
                                        device_id=dev, device_id_type=pl.DeviceIdType.MESH)


class _GatherRelayPlan:
    def __init__(self, blocks, mid_frac=0.5):
        self.ins = list(blocks)
        self.out_shapes = [_sds((N_DEV,) + b.shape, b.dtype) for b in blocks]
        n = len(blocks)
        self.scratch = [pltpu.SemaphoreType.DMA((n, 7)), pltpu.SemaphoreType.DMA((n, 7)), pltpu.SemaphoreType.DMA((n,))]
        self.mid_frac = mid_frac

    @staticmethod
    def _places():
        x, y, c = _coords()
        xn, yn = 4 * (1 - x) + 2 * y, 4 * x + 2 * (1 - y)
        relay_src = 4 * (x + c * (1 - 2 * x)) + 2 * (y + (1 - c) * (1 - 2 * y)) + c
        relay_to = (x + (1 - c) * (1 - 2 * x), y + c * (1 - 2 * y), c)
        return x, y, c, xn, yn, relay_src, relay_to, 4 * (1 - x) + 2 * (1 - y)

    def first(self, ins, outs, sems):
        x, y, c, _, _, _, _, _ = self._places()
        me = 4 * x + 2 * y + c
        for w in range(len(ins)):
            pltpu.make_async_copy(ins[w], outs[w].at[me], sems[2].at[w]).start()
            _rcopy(ins[w], outs[w].at[me], sems, w, 0, (x, y, 1 - c)).start()
            _rcopy(ins[w], outs[w].at[me], sems, w, 1, (1 - x, y, c)).start()
            _rcopy(ins[w], outs[w].at[me], sems, w, 2, (x, 1 - y, c)).start()

    def mid(self, ins, outs, sems):
        x, y, c, xn, yn, relay_src, relay_to, _ = self._places()
        sib = (x, y, 1 - c)
        for w in range(len(ins)):
            bx, by = outs[w].at[xn + c], outs[w].at[yn + c]
            _rcopy(ins[w], bx, sems, w, 1, (1 - x, y, c)).wait_recv()
            _rcopy(ins[w], by, sems, w, 2, (x, 1 - y, c)).wait_recv()
            _rcopy(outs[w].at[relay_src], outs[w].at[relay_src], sems, w, 3, relay_to).start()
            _rcopy(bx, bx, sems, w, 4, sib).start()
            _rcopy(by, by, sems, w, 5, sib).start()

    def last(self, ins, outs, sems):
        x, y, c, xn, yn, _, relay_to, dg = self._places()
        me = 4 * x + 2 * y + c
        sib = (x, y, 1 - c)
        for w in range(len(ins)):
            bd = outs[w].at[dg + c]
            _rcopy(ins[w], bd, sems, w, 3, relay_to).wait_recv()
            _rcopy(bd, bd, sems, w, 6, sib).start()
            for k, blk in ((0, 4 * x + 2 * y), (4, xn), (5, yn), (6, dg)):
                _rcopy(ins[w], outs[w].at[blk + 1 - c], sems, w, k, sib).wait_recv()
            for k in range(7):
                _rcopy(ins[w], outs[w].at[me], sems, w, k, sib).wait_send()
            pltpu.make_async_copy(ins[w], outs[w].at[me], sems[2].at[w]).wait()


class _GatherOwnPlan:
    mid = None

    def __init__(self, blocks, rows=None, into=None):
        self.n = len(blocks)
        self.rows = rows
        self.ins = list(blocks) + list(into or [])
        self.out_shapes = [_sds((N_DEV,) + b.shape, b.dtype) for b in blocks]
        self.aliases = [(self.n + i, i) for i in range(len(into or []))]
        n = self.n
        self.scratch = [pltpu.SemaphoreType.DMA((n, 4)), pltpu.SemaphoreType.DMA((n, 4)), pltpu.SemaphoreType.DMA((n,))]

    def _cut(self, ref):
        return ref if self.rows is None else ref.at[pl.ds(self.rows[0], self.rows[1] - self.rows[0])]

    def first(self, ins, outs, sems):
        x, y, c = _coords()
        me = 4 * x + 2 * y + c
        for w in range(self.n):
            src, dst = self._cut(ins[w]), self._cut(outs[w].at[me])
            pltpu.make_async_copy(src, dst, sems[2].at[w]).start()
            _rcopy(src, dst, sems, w, 0, (x, y, 1 - c)).start()
            for r in (1, 2, 3):
                px, py = _chip_rel(x, y, r)
                _rcopy(src, dst, sems, w, r, (px, py, c)).start()

    def last(self, ins, outs, sems):
        x, y, c = _coords()
        me = 4 * x + 2 * y + c
        for w in range(self.n):
            src = self._cut(ins[w])
            cp = _rcopy(src, self._cut(outs[w].at[4 * x + 2 * y + 1 - c]), sems, w, 0, (x, y, 1 - c))
            cp.wait_recv()
            cp.wait_send()
            for r in (1, 2, 3):
                px, py = _chip_rel(x, y, r)
                cp = _rcopy(src, self._cut(outs[w].at[4 * px + 2 * py + c]), sems, w, r, (px, py, c))
                cp.wait_recv()
                cp.wait_send()
            pltpu.make_async_copy(src, self._cut(outs[w].at[me]), sems[2].at[w]).wait()


class _GatherPassPlan:
    mid = None

    def __init__(self, gathered):
        self.ins = list(gathered)
        self.out_shapes = [_sds(g.shape, g.dtype) for g in gathered]
        self.aliases = [(i, i) for i in range(len(gathered))]
        n = len(gathered)
        self.scratch = [pltpu.SemaphoreType.DMA((n, 3)), pltpu.SemaphoreType.DMA((n, 3))]

    def first(self, ins, outs, sems):
        x, y, c = _coords()
        for w in range(len(ins)):
            for r in (1, 2, 3):
                px, py = _chip_rel(x, y, r)
                blk = 4 * px + 2 * py + c
                _rcopy(ins[w].at[blk], outs[w].at[blk], sems, w, r - 1, (x, y, 1 - c)).start()

    def last(self, ins, outs, sems):
        x, y, c = _coords()
        for w in range(len(ins)):
            for r in (1, 2, 3):
                px, py = _chip_rel(x, y, r)
                blk = 4 * px + 2 * py + 1 - c
                cp = _rcopy(ins[w].at[blk], outs[w].at[blk], sems, w, r - 1, (x, y, 1 - c))
                cp.wait_recv()
                cp.wait_send()


class _DirectGatherPlan:
    mid = None

    def __init__(self, blocks):
        self.ins = list(blocks)
        self.out_shapes = [_sds((N_DEV,) + b.shape, b.dtype) for b in blocks]
        n = len(blocks)
        self.scratch = [pltpu.SemaphoreType.DMA((n, 7)), pltpu.SemaphoreType.DMA((n, 7)), pltpu.SemaphoreType.DMA((n,))]

    @staticmethod
    def _peer(x, y, c, r):
        return (1 - x if r & 4 else x), (1 - y if r & 2 else y), (1 - c if r & 1 else c)

    def first(self, ins, outs, sems):
        x, y, c = _coords()
        me = 4 * x + 2 * y + c
        for w in range(len(ins)):
            pltpu.make_async_copy(ins[w], outs[w].at[me], sems[2].at[w]).start()
            for r in range(1, N_DEV):
                _rcopy(ins[w], outs[w].at[me], sems, w, r - 1, self._peer(x, y, c, r)).start()

    def last(self, ins, outs, sems):
        x, y, c = _coords()
        me = 4 * x + 2 * y + c
        for w in range(len(ins)):
            for r in range(1, N_DEV):
                px, py, pc = self._peer(x, y, c, r)
                cp = _rcopy(ins[w], outs[w].at[4 * px + 2 * py + pc], sems, w, r - 1, (px, py, pc))
                cp.wait_recv()
                cp.wait_send()
            pltpu.make_async_copy(ins[w], outs[w].at[me], sems[2].at[w]).wait()


class _PairScatterPlan:
    mid = None

    def __init__(self, pieces, rows=None, into=None):
        self.n = len(pieces)
        self.rows = rows
        self.ins = list(pieces) + list(into or [])
        self.out_shapes = [_sds((N_CHIP,) + p.shape[1:], p.dtype) for p in pieces]
        self.aliases = [(self.n + i, i) for i in range(len(into or []))]
        self.scratch = [pltpu.SemaphoreType.DMA((self.n, N_CHIP)), pltpu.SemaphoreType.DMA((self.n, N_CHIP))]

    def _copies(self, ins, outs, sems):
        x, y, c = _coords()
        cps = []
        for w in range(self.n):
            for q in range(N_CHIP):
                src, dst = ins[w].at[2 * q + 1 - c], outs[w].at[q]
                if self.rows is not None:
                    cut = pl.ds(self.rows[0], self.rows[1] - self.rows[0])
                    src, dst = src.at[cut], dst.at[cut]
                cps.append(_rcopy(src, dst, sems, w, q, (x, y, 1 - c)))
        return cps

    def first(self, ins, outs, sems):
        for cp in self._copies(ins, outs, sems):
            cp.start()

    def last(self, ins, outs, sems):
        for cp in self._copies(ins, outs, sems):
            cp.wait_recv()
            cp.wait_send()


class _ChipScatterPlan:
    mid = None

    def __init__(self, sums, rows=None, into=None):
        self.n = len(sums)
        self.rows = rows
        self.ins = list(sums) + list(into or [])
        self.out_shapes = [_sds(s.shape, s.dtype) for s in sums]
        self.aliases = [(self.n + i, i) for i in range(len(into or []))]
        self.scratch = [pltpu.SemaphoreType.DMA((self.n, 3)), pltpu.SemaphoreType.DMA((self.n, 3))]

    def _copies(self, ins, outs, sems):
        x, y, c = _coords()
        cps = []
        for w in range(self.n):
            for r in (1, 2, 3):
                px, py = _chip_rel(x, y, r)
                src, dst = ins[w].at[r - 1], outs[w].at[r - 1]
                if self.rows is not None:
                    cut = pl.ds(self.rows[0], self.rows[1] - self.rows[0])
                    src, dst = src.at[cut], dst.at[cut]
                cps.append(_rcopy(src, dst, sems, w, r - 1, (px, py, c)))
        return cps

    def first(self, ins, outs, sems):
        for cp in self._copies(ins, outs, sems):
            cp.start()

    def last(self, ins, outs, sems):
        for cp in self._copies(ins, outs, sems):
            cp.wait_recv()
            cp.wait_send()


class _Comm:
    def __init__(self, plans):
        self.plans = list(plans)
        self.ins = [a for p in self.plans for a in p.ins]
        self.out_shapes = [s for p in self.plans for s in p.out_shapes]
        self.scratch = [s for p in self.plans for s in p.scratch]
        self.aliases = []
        i = o = 0
        for p in self.plans:
            self.aliases += [(i + a, o + b) for a, b in getattr(p, "aliases", [])]
            i, o = i + len(p.ins), o + len(p.out_shapes)

    def _parts(self, ins, outs, sems):
        i = o = s = 0
        for p in self.plans:
            yield p, ins[i:i + len(p.ins)], outs[o:o + len(p.out_shapes)], sems[s:s + len(p.scratch)]
            i, o, s = i + len(p.ins), o + len(p.out_shapes), s + len(p.scratch)

    def begin(self, step, nsteps, ins, outs, sems):
        @pl.when(step == 0)
        def _():
            for p, pi, po, ps in self._parts(ins, outs, sems):
                p.first(pi, po, ps)

        for p, pi, po, ps in self._parts(ins, outs, sems):
            if p.mid is not None:
                @pl.when(step == min(nsteps - 1, int(p.mid_frac * nsteps)))
                def _(p=p, pi=pi, po=po, ps=ps):
                    p.mid(pi, po, ps)

    def end(self, step, nsteps, ins, outs, sems):
        @pl.when(step == nsteps - 1)
        def _():
            for p, pi, po, ps in self._parts(ins, outs, sems):
                p.last(pi, po, ps)


def _call(body, args, *, name, grid, in_specs, out_specs, out_shape, scratch_shapes=(), sem=None, comm=None):
    in_specs, out_specs, out_shape, scratch_shapes = list(in_specs), list(out_specs), list(out_shape), list(scratch_shapes)
    if comm is None:
        res = pl.pallas_call(
            body, name=name, grid=grid, in_specs=in_specs, out_specs=out_specs, out_shape=out_shape,
            scratch_shapes=scratch_shapes,
            compiler_params=pltpu.CompilerParams(dimension_semantics=sem, vmem_limit_bytes=VMEM_LIMIT_BYTES),
        )(*args)
        return list(res), []
    n_in, n_out, n_sc = len(in_specs), len(out_specs), len(scratch_shapes)
    n_ci, n_co = len(comm.ins), len(comm.out_shapes)
    nsteps = math.prod(grid)

    def hosted(*refs):
        ins, cins = refs[:n_in], refs[n_in:n_in + n_ci]
        o0 = n_in + n_ci
        outs, couts = refs[o0:o0 + n_out], refs[o0 + n_out:o0 + n_out + n_co]
        s0 = o0 + n_out + n_co
        scr, csems = refs[s0:s0 + n_sc], refs[s0 + n_sc:]
        step = jnp.int32(0)
        for d in range(len(grid)):
            step = step * grid[d] + pl.program_id(d)
        comm.begin(step, nsteps, cins, couts, csems)
        body(*ins, *outs, *scr)
        comm.end(step, nsteps, cins, couts, csems)

    res = pl.pallas_call(
        hosted, name=name, grid=grid, in_specs=in_specs + [ANY] * n_ci, out_specs=out_specs + [ANY] * n_co,
        out_shape=out_shape + comm.out_shapes, scratch_shapes=scratch_shapes + comm.scratch,
        input_output_aliases={n_in + a: n_out + b for a, b in comm.aliases},
        compiler_params=pltpu.CompilerParams(dimension_semantics=("arbitrary",) * len(grid),
                                             vmem_limit_bytes=VMEM_LIMIT_BYTES, has_side_effects=True),
    )(*args, *comm.ins)
    return list(res[:n_out]), list(res[n_out:])


def _exchange_alone(comm, name):
    def body():
        pass

    return _call(body, [], name=name, grid=(), in_specs=[], out_specs=[], out_shape=[], comm=comm)[1]


def _matmul(a, b, *, mode, name, out_dtype=F32, out_blocks=None, tm=None, tn=None, tk=None, comm=None):
    tm = MM_TILE if tm is None else tm
    tn = MM_TILE if tn is None else tn
    a_blk = a.ndim == 3
    b_blk = b.ndim == 3
    if mode == "nn":
        M, K = a.shape
        N = b.shape[0] * b.shape[2] if b_blk else b.shape[1]
        dims = (((1,), (0,)), ((), ()))
    elif mode == "nt":
        M = a.shape[1] if a_blk else a.shape[0]
        K = a.shape[0] * a.shape[2] if a_blk else a.shape[1]
        N = b.shape[1] if b_blk else b.shape[0]
        dims = (((1,), (1,)), ((), ()))
    else:
        K, M = a.shape
        N = b.shape[0] * b.shape[2] if b_blk else b.shape[1]
        dims = (((0,), (0,)), ((), ()))

    tm = _tile(M, tm)
    tn = _tile(N, tn)
    if mode == "nt" and (a_blk or b_blk):
        tk = a.shape[2] if a_blk else b.shape[2]
    else:
        tk = _tile(K, K if tk is None else tk)
    if mode != "nt" and b_blk:
        tn = _tile(b.shape[2], tn)
    if out_blocks is not None:
        tn = _tile(out_blocks, tn)
    nk = K // tk
    grid = (M // tm, N // tn, nk)

    if mode == "nn":
        a_spec = pl.BlockSpec((tm, tk), lambda i, j, k: (i, k))
        if b_blk:
            rb = b.shape[2] // tn
            b_spec = pl.BlockSpec((None, tk, tn), lambda i, j, k: (j // rb, k, j % rb))
        else:
            b_spec = pl.BlockSpec((tk, tn), lambda i, j, k: (k, j))
    elif mode == "nt":
        if a_blk:
            a_spec = pl.BlockSpec((None, tm, tk), lambda i, j, k: (k, i, 0))
        else:
            a_spec = pl.BlockSpec((tm, tk), lambda i, j, k: (i, k))
        if b_blk:
            b_spec = pl.BlockSpec((None, tn, tk), lambda i, j, k: (k, j, 0))
        else:
            b_spec = pl.BlockSpec((tn, tk), lambda i, j, k: (j, k))
    else:
        a_spec = pl.BlockSpec((tk, tm), lambda i, j, k: (k, i))
        if b_blk:
            rb = b.shape[2] // tn
            b_spec = pl.BlockSpec((None, tk, tn), lambda i, j, k: (j // rb, k, j % rb))
        else:
            b_spec = pl.BlockSpec((tk, tn), lambda i, j, k: (k, j))

    if out_blocks is None:
        o_spec = pl.BlockSpec((tm, tn), lambda i, j, k: (i, j))
        o_shape = _sds((M, N), out_dtype)
    else:
        ro = out_blocks // tn
        o_spec = pl.BlockSpec((None, tm, tn), lambda i, j, k: (j // ro, i, j % ro))
        o_shape = _sds((N // out_blocks, M, out_blocks), out_dtype)

    direct = nk == 1 or out_dtype == F32

    def body(a_ref, b_ref, o_ref, *scratch):
        if nk == 1:
            o_ref[...] = lax.dot_general(a_ref[...], b_ref[...], dims, preferred_element_type=F32).astype(o_ref.dtype)
            return
        acc_ref = o_ref if direct else scratch[0]
        k = pl.program_id(2)

        @pl.when(k == 0)
        def _():
            acc_ref[...] = jnp.zeros(acc_ref.shape, F32)

        acc_ref[...] += lax.dot_general(a_ref[...], b_ref[...], dims, preferred_element_type=F32)
        if not direct:
            @pl.when(k == nk - 1)
            def _():
                o_ref[...] = acc_ref[...].astype(o_ref.dtype)

    scratch = [] if direct else [pltpu.VMEM((tm, tn), F32)]
    outs, landed = _call(body, [a, b], name=name, grid=grid, in_specs=[a_spec, b_spec], out_specs=[o_spec],
                         out_shape=[o_shape], scratch_shapes=scratch, sem=("parallel", "parallel", "arbitrary"), comm=comm)
    return outs[0] if comm is None else (outs[0], landed)


def _matmul_halves(a, b, *, mode, name, tm, tn=None, tk=None, comm=None):
    if mode == "nt":
        lo, hi = a
        M, kh = lo.shape
        G, N, kb = b.shape
        half = kh // kb
        tm, tn = _tile(M, tm), _tile(N, N if tn is None else tn)
        dims = (((1,), (1,)), ((), ()))

        def body(lo_ref, hi_ref, b_ref, o_ref):
            k = pl.program_id(2)

            @pl.when(k == 0)
            def _():
                o_ref[...] = jnp.zeros(o_ref.shape, F32)

            @pl.when(k < half)
            def _():
                o_ref[...] += lax.dot_general(lo_ref[...], b_ref[...], dims, preferred_element_type=F32)

            @pl.when(k >= half)
            def _():
                o_ref[...] += lax.dot_general(hi_ref[...], b_ref[...], dims, preferred_element_type=F32)

        outs, landed = _call(
            body, [lo, hi, b], name=name, grid=(M // tm, N // tn, G),
            in_specs=[pl.BlockSpec((tm, kb), lambda i, j, k: (i, jnp.minimum(k, half - 1))),
                      pl.BlockSpec((tm, kb), lambda i, j, k: (i, jnp.maximum(k - half, 0))),
                      pl.BlockSpec((None, tn, kb), lambda i, j, k: (k, j, 0))],
            out_specs=[pl.BlockSpec((tm, tn), lambda i, j, k: (i, j))], out_shape=[_sds((M, N), F32)],
            sem=("parallel", "parallel", "arbitrary"), comm=comm)
    else:
        lo, hi = b
        K, nh = lo.shape
        M = a.shape[1]
        n = tn
        half = nh // n
        tm, tk = _tile(M, tm), _tile(K, K if tk is None else tk)
        nk = K // tk
        dims = (((0,), (0,)), ((), ()))

        def body(a_ref, lo_ref, hi_ref, o_ref):
            j, k = pl.program_id(1), pl.program_id(2)

            @pl.when(k == 0)
            def _():
                o_ref[...] = jnp.zeros(o_ref.shape, F32)

            @pl.when(j < half)
            def _():
                o_ref[...] += lax.dot_general(a_ref[...], lo_ref[...], dims, preferred_element_type=F32)

            @pl.when(j >= half)
            def _():
                o_ref[...] += lax.dot_general(a_ref[...], hi_ref[...], dims, preferred_element_type=F32)

        outs, landed = _call(
            body, [a, lo, hi], name=name, grid=(M // tm, 2 * half, nk),
            in_specs=[pl.BlockSpec((tk, tm), lambda i, j, k: (k, i)),
                      pl.BlockSpec((tk, n), lambda i, j, k: (jnp.where(j < half, k, nk - 1), jnp.minimum(j, half - 1))),
                      pl.BlockSpec((tk, n), lambda i, j, k: (jnp.where(j >= half, k, 0), jnp.maximum(j - half, 0)))],
            out_specs=[pl.BlockSpec((None, tm, n), lambda i, j, k: (j, i, 0))],
            out_shape=[_sds((2 * half, M, n), F32)], sem=("parallel", "parallel", "arbitrary"), comm=comm)
    return outs[0] if comm is None else (outs[0], landed)


def _rms(x):
    return lax.rsqrt(jnp.mean(x * x, axis=-1, keepdims=True) + EPS)


def _rms_bwd(dy, x, g):
    r = _rms(x)
    n = x * r
    dn = dy * g
    dx = r * (dn - n * jnp.mean(dn * n, axis=-1, keepdims=True))
    return dx, dy * n


def _sigmoid(x):
    return 1.0 / (1.0 + jnp.exp(-x))


def _rope_rot(t):
    return pltpu.roll(t, HALF_ROPE, 1) - pltpu.roll(t, LANES - HALF_ROPE, 1)


def _lane(shape):
    return lax.broadcasted_iota(jnp.int32, shape, 1)


def _split3(x):
    hi = x.astype(BF16).astype(F32)
    r1 = x - hi
    mid = r1.astype(BF16).astype(F32)
    lo = (r1 - mid).astype(BF16).astype(F32)
    return hi, mid, lo


def _cumsum_rows(x, reverse):
    S = x.shape[0]
    bs = min(256, S)
    nb = S // bs
    r = lax.broadcasted_iota(jnp.int32, (bs, bs), 0)
    c = lax.broadcasted_iota(jnp.int32, (bs, bs), 1)
    tri = jnp.where((c >= r) if reverse else (c <= r), 1.0, 0.0).astype(BF16)
    edge = lax.broadcasted_iota(jnp.int32, (bs, x.shape[1]), 0) == (0 if reverse else bs - 1)
    carry = jnp.zeros((1, x.shape[1]), F32)
    outs = [None] * nb
    for bi in (range(nb - 1, -1, -1) if reverse else range(nb)):
        xb = x[bi * bs:(bi + 1) * bs, :]
        acc = carry
        for term in _split3(xb):
            acc = acc + jnp.dot(tri, term.astype(BF16), preferred_element_type=F32)
        outs[bi] = acc
        carry = jnp.sum(jnp.where(edge, acc, 0.0), axis=0, keepdims=True)
    return jnp.concatenate(outs, axis=0) if nb > 1 else outs[0]


def _gelu_parts(x):
    c0 = math.sqrt(2.0 / math.pi)
    inner = c0 * (x + 0.044715 * (x * x * x))
    t = jnp.tanh(inner)
    g = 0.5 * x * (1.0 + t)
    dg = 0.5 * (1.0 + t) + 0.5 * x * (1.0 - t * t) * (c0 * (1.0 + 3.0 * 0.044715 * (x * x)))
    return g, dg


def _accumulate(ref, value, first):
    @pl.when(first)
    def _():
        ref[...] = value

    @pl.when(jnp.logical_not(first))
    def _():
        ref[...] += value


def _cast_bf16(w, name):
    R, C = w.shape
    tr = _tile(R, 512, 16)

    def body(w_ref, o_ref):
        o_ref[...] = w_ref[...].astype(BF16)

    blk = pl.BlockSpec((tr, C), lambda i: (i, 0))
    return _call(body, [w], name=name, grid=(R // tr,), in_specs=[blk], out_specs=[blk],
                 out_shape=[_sds((R, C), BF16)], sem=("parallel",))[0][0]


def _concat_cols(parts, name):
    T = parts[0].shape[-2] if parts[0].ndim == 3 else parts[0].shape[0]
    widths = [p.shape[0] * LANES if p.ndim == 3 else p.shape[1] for p in parts]
    tm = _tile(T, ROW_TILE, 16)

    def body(*refs):
        o_ref = refs[-1]
        off = 0
        for p_ref, p, w in zip(refs[:-1], parts, widths):
            if p.ndim == 3:
                for hd in range(p.shape[0]):
                    o_ref[:, off + hd * LANES:off + (hd + 1) * LANES] = p_ref[hd].astype(BF16)
            else:
                o_ref[:, off:off + w] = p_ref[...].astype(BF16)
            off += w

    def spec(p, w):
        if p.ndim == 3:
            return pl.BlockSpec((p.shape[0], tm, LANES), lambda i: (0, i, 0))
        return pl.BlockSpec((tm, w), lambda i: (i, 0))

    return _call(body, parts, name=name, grid=(T // tm,),
                 in_specs=[spec(p, w) for p, w in zip(parts, widths)],
                 out_specs=[pl.BlockSpec((tm, sum(widths)), lambda i: (i, 0))],
                 out_shape=[_sds((T, sum(widths)), BF16)], sem=("parallel",))[0][0]


def _prenorm(x, g, comm=None):
    T, D = x.shape
    tm = _tile(T, ROW_TILE, 16)

    def body(x_ref, g_ref, h_ref):
        xv = x_ref[...]
        h_ref[...] = (xv * _rms(xv) * g_ref[...]).astype(BF16)

    row = pl.BlockSpec((tm, D), lambda i: (i, 0))
    (h,), landed = _call(body, [x, g], name="prenorm", grid=(T // tm,),
                         in_specs=[row, pl.BlockSpec((1, D), lambda i: (0, 0))], out_specs=[row],
                         out_shape=[_sds((T, D), BF16)], sem=("parallel",), comm=comm)
    return h, landed


def _split_prep(proj, pos, invf, gq, gkv, bfor, lay, comm=None):
    T = proj.shape[0]
    tm = _tile(T, ROW_TILE, 16)

    def body(q_ref, kv_ref, kpe_ref, fl_ref, pos_ref, invf_ref, gq_ref, gkv_ref, bf_ref,
             qn_ref, kvn_ref, kper_ref, logf_ref, cos_ref, sin_ref):
        ql = q_ref[...]
        qn_ref[...] = (ql * _rms(ql) * gq_ref[...]).astype(BF16)
        kl = kv_ref[...]
        kvn_ref[...] = (kl * _rms(kl) * gkv_ref[...]).astype(BF16)
        ang = pos_ref[...].astype(F32) * invf_ref[...]
        valid = _lane(ang.shape) < ROPE
        cs = jnp.where(valid, jnp.cos(ang), 0.0)
        sn = jnp.where(valid, jnp.sin(ang), 0.0)
        cos_ref[...] = cs
        sin_ref[...] = sn
        kp = jnp.where(valid, kpe_ref[...], 0.0)
        kper_ref[...] = (kp * cs + _rope_rot(kp) * sn).astype(BF16)
        z = fl_ref[...] + bf_ref[...]
        logf_ref[...] = jnp.minimum(z, 0.0) - jnp.log(1.0 + jnp.exp(-jnp.abs(z)))

    def col(width, off):
        return pl.BlockSpec((tm, width), lambda i: (i, off // width))

    def vec(width):
        return pl.BlockSpec((1, width), lambda i: (0, 0))

    def out(width):
        return pl.BlockSpec((tm, width), lambda i: (i, 0))

    return _call(
        body, [proj, proj, proj, proj, pos, invf, gq, gkv, bfor], name="split_prep", grid=(T // tm,),
        in_specs=[col(Q_LORA, lay["q"]), col(KV_LORA, lay["kv"]), col(LANES, lay["kpe"]), col(LANES, lay["fl"]),
                  pl.BlockSpec((tm, 1), lambda i: (i, 0)), vec(LANES), vec(Q_LORA), vec(KV_LORA), vec(LANES)],
        out_specs=[out(Q_LORA), out(KV_LORA), out(LANES), out(LANES), out(LANES), out(LANES)],
        out_shape=[_sds((T, Q_LORA), BF16), _sds((T, KV_LORA), BF16), _sds((T, LANES), BF16),
                   _sds((T, LANES), F32), _sds((T, LANES), F32), _sds((T, LANES), F32)],
        sem=("parallel",), comm=comm)


def _mla_prep(qraw, kvraw, kper, cosT, sinT, comm=None):
    H, T, _ = qraw.shape
    tm = _tile(T, HEAD_ROW_TILE, 16)

    def body(q_ref, kv_ref, kpe_ref, cos_ref, sin_ref, qo_ref, ko_ref, vo_ref):
        q = q_ref[...]
        pe = q[:, NOPE:]
        pe = jnp.where(_lane(pe.shape) < ROPE, pe, 0.0)
        qo_ref[:, :NOPE] = q[:, :NOPE].astype(BF16)
        qo_ref[:, NOPE:] = (pe * cos_ref[...] + _rope_rot(pe) * sin_ref[...]).astype(BF16)
        kv = kv_ref[...]
        ko_ref[:, :NOPE] = kv[:, :NOPE].astype(BF16)
        ko_ref[:, NOPE:] = kpe_ref[...]
        vo_ref[...] = kv[:, NOPE:].astype(BF16)

    head = pl.BlockSpec((None, tm, ATT_DK), lambda h, i: (h, i, 0))
    tok = pl.BlockSpec((tm, LANES), lambda h, i: (i, 0))
    return _call(
        body, [qraw, kvraw, kper, cosT, sinT], name="mla_prep", grid=(H, T // tm),
        in_specs=[head, head, tok, tok, tok],
        out_specs=[head, head, pl.BlockSpec((None, tm, VDIM), lambda h, i: (h, i, 0))],
        out_shape=[_sds((H, T, ATT_DK), BF16), _sds((H, T, ATT_DK), BF16), _sds((H, T, VDIM), BF16)],
        sem=("parallel", "parallel"), comm=comm)


def _fox_cumsum(logf, B, S, inv_scale):
    T = logf.shape[0]

    def body(l_ref, c_ref):
        c_ref[...] = _cumsum_rows(l_ref[...], reverse=False) * inv_scale

    seq = pl.BlockSpec((S, LANES), lambda b: (b, 0))
    return _call(body, [logf], name="fox_cumsum", grid=(B,), in_specs=[seq], out_specs=[seq],
                 out_shape=[_sds((T, LANES), F32)], sem=("parallel",))[0][0]


def _fox_prep(proj, cs, lay, comm=None):
    T = proj.shape[0]
    tm = _tile(T, HEAD_ROW_TILE, 16)

    def body(q_ref, k_ref, v_ref, cs_ref, qo_ref, ko_ref, vo_ref):
        h = pl.program_id(0)
        cv = cs_ref[...]
        lane = _lane(cv.shape)
        ccol = jnp.sum(jnp.where(lane == h, cv, 0.0), axis=1, keepdims=True)
        hi, mid, lo = _split3(ccol)
        one = jnp.where(lane < 6, 1.0, 0.0)
        augq = jnp.where(lane == 0, hi, jnp.where(lane == 1, mid, jnp.where(lane == 2, lo, one)))
        augk = jnp.where(lane < 3, 1.0, jnp.where(lane == 3, -hi, jnp.where(lane == 4, -mid, jnp.where(lane == 5, -lo, 0.0))))
        qo_ref[:, :FOX_DIM] = q_ref[...].astype(BF16)
        qo_ref[:, FOX_DIM:] = augq.astype(BF16)
        ko_ref[:, :FOX_DIM] = k_ref[...].astype(BF16)
        ko_ref[:, FOX_DIM:] = augk.astype(BF16)
        vo_ref[...] = v_ref[...].astype(BF16)

    def col(off):
        return pl.BlockSpec((tm, FOX_DIM), lambda h, i: (i, off // FOX_DIM + h))

    head = pl.BlockSpec((None, tm, ATT_DK), lambda h, i: (h, i, 0))
    return _call(
        body, [proj, proj, proj, cs], name="fox_prep", grid=(HEADS, T // tm),
        in_specs=[col(lay["fq"]), col(lay["fk"]), col(lay["fv"]), pl.BlockSpec((tm, LANES), lambda h, i: (i, 0))],
        out_specs=[head, head, pl.BlockSpec((None, tm, VDIM), lambda h, i: (h, i, 0))],
        out_shape=[_sds((HEADS, T, ATT_DK), BF16), _sds((HEADS, T, ATT_DK), BF16), _sds((HEADS, T, VDIM), BF16)],
        sem=("parallel", "parallel"), comm=comm)


def _visible(tq, tk, unit):
    r = lax.broadcasted_iota(jnp.int32, (tq, tk), 0)
    c = lax.broadcasted_iota(jnp.int32, (tq, tk), 1)
    sh = int(math.log2(unit))
    return lax.shift_right_logical(c, sh) <= lax.shift_right_logical(r, sh)


def _attn_fwd(streams, *, B, S, name, comm=None):
    n = len(streams)
    H, T, DK = streams[0][0].shape
    DV = streams[0][2].shape[2]
    tq = _tile(S, ATT_TILE)
    nq = S // tq
    sub = min(ATT_SUB, tq)
    NT = (((1,), (1,)), ((), ()))

    def body(*refs):
        ins, outs, (m_sc, acc_sc) = refs[:3 * n], refs[3 * n:5 * n], refs[5 * n:]
        i, j = pl.program_id(1), pl.program_id(2)

        @pl.when(j == 0)
        def _():
            m_sc[...] = jnp.full(m_sc.shape, NEG_INF, F32)
            acc_sc[...] = jnp.zeros(acc_sc.shape, F32)

        def step(diagonal):
            work = [(t, r) for r in range(tq // sub) for t in range(n)]

            def scores(t, r):
                q_ref, k_ref, _ = ins[3 * t:3 * t + 3]
                kc = (r + 1) * sub if diagonal else tq
                s = lax.dot_general(q_ref[r * sub:(r + 1) * sub, :], k_ref[0:kc, :], NT, preferred_element_type=F32)
                return s * (streams[t][4] * LOG2_E)

            ahead = [scores(*work[w]) for w in range(min(ATT_AHEAD, len(work)))]
            for w, (t, r) in enumerate(work):
                s = ahead.pop(0)
                if w + ATT_AHEAD < len(work):
                    ahead.append(scores(*work[w + ATT_AHEAD]))
                v_ref = ins[3 * t + 2]
                kc = s.shape[1]
                rows = slice(r * sub, (r + 1) * sub)
                if diagonal:
                    own = jnp.where(_visible(sub, sub, streams[t][3]), s[:, kc - sub:], NEG_INF)
                    s = own if kc == sub else jnp.concatenate([s[:, :kc - sub], own], axis=1)
                m_prev = m_sc[t, rows, :]
                mx = s[:, 0:LANES]
                for g in range(1, kc // LANES):
                    mx = jnp.maximum(mx, s[:, g * LANES:(g + 1) * LANES])
                m_new = jnp.maximum(m_prev, jnp.max(mx, axis=1, keepdims=True))
                alpha = jnp.exp2(m_prev - m_new)
                p = jnp.exp2(s - jnp.tile(m_new, (1, kc // LANES))).astype(BF16)
                v_aug = jnp.concatenate([v_ref[0:kc, :], jnp.ones((kc, LANES), BF16)], axis=1)
                acc_sc[t, rows, :] = jnp.tile(alpha, (1, 2)) * acc_sc[t, rows, :] + jnp.dot(
                    p, v_aug, preferred_element_type=F32)
                m_sc[t, rows, :] = m_new

        @pl.when(j < i)
        def _():
            step(False)

        @pl.when(j == i)
        def _():
            step(True)
            for t in range(n):
                o_ref, lse_ref = outs[2 * t:2 * t + 2]
                l = acc_sc[t, :, DV:]
                o_ref[...] = (acc_sc[t, :, :DV] / l).astype(BF16)
                lse_ref[...] = m_sc[t] + jnp.log2(l)

    def qmap(g, i, j):
        return (g % H, (g // H) * nq + i, 0)

    def kmap(g, i, j):
        return (g % H, (g // H) * nq + jnp.minimum(j, i), 0)

    args = [a for st in streams for a in st[:3]]
    outs, landed = _call(
        body, args, name=name, grid=(B * H, nq, nq),
        in_specs=[pl.BlockSpec((None, tq, DK), qmap), pl.BlockSpec((None, tq, DK), kmap),
                  pl.BlockSpec((None, tq, DV), kmap)] * n,
        out_specs=[pl.BlockSpec((tq, DV), lambda g, i, j: ((g // H) * nq + i, g % H)),
                   pl.BlockSpec((None, tq, LANES), qmap)] * n,
        out_shape=[_sds((T, H * DV), BF16), _sds((H, T, LANES), F32)] * n,
        scratch_shapes=[pltpu.VMEM((n, tq, LANES), F32), pltpu.VMEM((n, tq, DV + LANES), F32)],
        sem=("parallel", "parallel", "arbitrary"), comm=comm)
    return [(outs[2 * t], outs[2 * t + 1]) for t in range(n)], landed


def _attn_bwd(streams, *, B, S, name, comm=None):
    n = len(streams)
    H, T, DK = streams[0][0].shape
    DV = streams[0][2].shape[2]
    tq = _tile(S, ATT_TILE)
    nq = S // tq
    sub = min(ATT_SUB, tq)
    NT = (((1,), (1,)), ((), ()))
    TN = (((0,), (0,)), ((), ()))

    def body(*refs):
        ins, outs = refs[:6 * n], refs[6 * n:]
        j, i = pl.program_id(1), pl.program_id(2)

        @pl.when(jnp.logical_and(j == 0, i == 0))
        def _():
            for t in range(n):
                outs[3 * t][...] = jnp.zeros(outs[3 * t].shape, F32)

        @pl.when(i == 0)
        def _():
            for t in range(n):
                outs[3 * t + 1][...] = jnp.zeros(outs[3 * t + 1].shape, F32)
                outs[3 * t + 2][...] = jnp.zeros(outs[3 * t + 2].shape, F32)

        def step(diagonal):
            work = [(t, r) for r in range(tq // sub) for t in range(n)]

            def kcols(r):
                return (r + 1) * sub if diagonal else tq

            def scores(t, r):
                q_ref, k_ref, v_ref, _, do_ref, _ = ins[6 * t:6 * t + 6]
                rows, kc = slice(r * sub, (r + 1) * sub), kcols(r)
                s = lax.dot_general(q_ref[rows, :], k_ref[0:kc, :], NT, preferred_element_type=F32)
                dp = lax.dot_general(do_ref[rows, :], v_ref[0:kc, :], NT, preferred_element_type=F32)
                return s * (streams[t][7] * LOG2_E), dp

            def probs(t, r, s, dp):
                _, _, _, o_ref, do_ref, lse_ref = ins[6 * t:6 * t + 6]
                rows, kc = slice(r * sub, (r + 1) * sub), kcols(r)
                if diagonal:
                    own = jnp.where(_visible(sub, sub, streams[t][6]), s[:, kc - sub:], NEG_INF)
                    s = own if kc == sub else jnp.concatenate([s[:, :kc - sub], own], axis=1)
                p = jnp.exp2(s - jnp.tile(lse_ref[rows, :], (1, kc // LANES)))
                delta = jnp.sum(do_ref[rows, :].astype(F32) * o_ref[rows, :].astype(F32), axis=1, keepdims=True)
                return p.astype(BF16), (p * (dp - delta) * streams[t][7]).astype(BF16)

            def grads(t, r, p, ds):
                q_ref, k_ref, _, _, do_ref, _ = ins[6 * t:6 * t + 6]
                dq_ref, dk_ref, dv_ref = outs[3 * t:3 * t + 3]
                rows, kc = slice(r * sub, (r + 1) * sub), kcols(r)
                dv_ref[0:kc, :] += lax.dot_general(p, do_ref[rows, :], TN, preferred_element_type=F32)
                dk_ref[0:kc, :] += lax.dot_general(ds, q_ref[rows, :], TN, preferred_element_type=F32)
                qrows = pl.ds(pl.multiple_of(i * tq + r * sub, sub), sub)
                dq_ref[qrows, :] += jnp.dot(ds, k_ref[0:kc, :], preferred_element_type=F32)

            nw = len(work)
            sc = {w: scores(*work[w]) for w in range(min(2, nw))}
            pr = {0: probs(*work[0], *sc.pop(0))}
            for w in range(nw):
                if w + 2 < nw:
                    sc[w + 2] = scores(*work[w + 2])
                if w + 1 < nw:
                    pr[w + 1] = probs(*work[w + 1], *sc.pop(w + 1))
                grads(*work[w], *pr.pop(w))

        @pl.when(i > j)
        def _():
            step(False)

        @pl.when(i == j)
        def _():
            step(True)

    def qmap(g, j, i):
        return (g % H, (g // H) * nq + jnp.maximum(i, j), 0)

    def kmap(g, j, i):
        return (g % H, (g // H) * nq + j, 0)

    def omap(g, j, i):
        return ((g // H) * nq + jnp.maximum(i, j), g % H)

    args = [a for st in streams for a in st[:6]]
    outs, landed = _call(
        body, args, name=name, grid=(B * H, nq, nq),
        in_specs=[pl.BlockSpec((None, tq, DK), qmap), pl.BlockSpec((None, tq, DK), kmap),
                  pl.BlockSpec((None, tq, DV), kmap), pl.BlockSpec((tq, DV), omap), pl.BlockSpec((tq, DV), omap),
                  pl.BlockSpec((None, tq, LANES), qmap)] * n,
        out_specs=[pl.BlockSpec((None, S, DK), lambda g, j, i: (g % H, g // H, 0)),
                   pl.BlockSpec((None, tq, DK), kmap), pl.BlockSpec((None, tq, DV), kmap)] * n,
        out_shape=[_sds((H, T, DK), F32), _sds((H, T, DK), F32), _sds((H, T, DV), F32)] * n,
        sem=("parallel", "arbitrary", "arbitrary"), comm=comm)
    return [tuple(outs[3 * t:3 * t + 3]) for t in range(n)], landed


def _branch_merge(o_m, o_f, w_m, w_f, proj, bgate, lay, D, comm=None):
    T, K = o_m.shape
    tm = _tile(T, 2 * ROW_TILE, 16)
    tn = _tile(D, 1024)

    def body(om_ref, of_ref, wm_ref, wf_ref, gm_ref, gf_ref, bm_ref, bf_ref, am_ref, af_ref, o_ref):
        am = jnp.dot(om_ref[...], wm_ref[...], preferred_element_type=F32)
        af = jnp.dot(of_ref[...], wf_ref[...], preferred_element_type=F32)
        am_ref[...] = am
        af_ref[...] = af
        sm = _sigmoid(gm_ref[...] + bm_ref[...])
        sf = _sigmoid(gf_ref[...] + bf_ref[...])
        o_ref[...] = (sm * am + sf * af).astype(BF16)

    og = lay["g"] // tn
    blk = pl.BlockSpec((tm, tn), lambda i, j: (i, j))
    lhs = pl.BlockSpec((tm, K), lambda i, j: (i, 0))
    rhs = pl.BlockSpec((K, tn), lambda i, j: (0, j))
    return _call(
        body, [o_m, o_f, w_m, w_f, proj, proj, bgate, bgate], name="mm_branch_merge", grid=(T // tm, D // tn),
        in_specs=[lhs, lhs, rhs, rhs, pl.BlockSpec((tm, tn), lambda i, j: (i, og + j)),
                  pl.BlockSpec((tm, tn), lambda i, j: (i, og + D // tn + j)),
                  pl.BlockSpec((1, tn), lambda i, j: (0, j)), pl.BlockSpec((1, tn), lambda i, j: (0, D // tn + j))],
        out_specs=[blk, blk, blk], out_shape=[_sds((T, D), F32), _sds((T, D), F32), _sds((T, D), BF16)],
        sem=("parallel", "parallel"), comm=comm)


def _out_mid(merged, w_out, x, g_pm, g_ffn):
    T, D = x.shape
    tm = _tile(T, 2 * ROW_TILE, 16)

    def body(a_ref, w_ref, x_ref, gp_ref, gf_ref, y_ref, x1_ref, h2_ref):
        y = jnp.dot(a_ref[...], w_ref[...], preferred_element_type=F32)
        y_ref[...] = y
        x1 = x_ref[...] + y * _rms(y) * gp_ref[...]
        x1_ref[...] = x1
        h2_ref[...] = (x1 * _rms(x1) * gf_ref[...]).astype(BF16)

    row = pl.BlockSpec((tm, D), lambda i: (i, 0))
    vec = pl.BlockSpec((1, D), lambda i: (0, 0))
    return _call(body, [merged, w_out, x, g_pm, g_ffn], name="mm_out_mid", grid=(T // tm,),
                 in_specs=[row, pl.BlockSpec((D, D), lambda i: (0, 0)), row, vec, vec], out_specs=[row, row, row],
                 out_shape=[_sds((T, D), F32), _sds((T, D), F32), _sds((T, D), BF16)], sem=("parallel",))[0]


def _conv3(u, w_ref, bias):
    row = lax.broadcasted_iota(jnp.int32, u.shape, 0)
    u1 = jnp.where(row >= 1, pltpu.roll(u, 1, 0), 0.0)
    u2 = jnp.where(row >= 2, pltpu.roll(u, 2, 0), 0.0)
    return w_ref[0:1, :] * u2 + w_ref[1:2, :] * u1 + w_ref[2:3, :] * u + bias, u1, u2


def _convffn_fwd(u, cw, cb, B, S, F, comm=None):
    T = u.shape[0]
    tn = _tile(F, 256)
    nf = F // tn

    def body(ug_ref, uv_ref, wg_ref, wv_ref, bg_ref, bv_ref, a_ref):
        g, _, _ = _conv3(ug_ref[...], wg_ref, bg_ref[...])
        val, _, _ = _conv3(uv_ref[...], wv_ref, bv_ref[...])
        a_ref[...] = (_gelu_parts(g)[0] * val).astype(BF16)

    def seq(off):
        return pl.BlockSpec((S, tn), lambda b, j: (b, off + j))

    def par(rows, off):
        return pl.BlockSpec((rows, tn), lambda b, j: (0, off + j))

    return _call(body, [u, u, cw, cw, cb, cb], name="convffn_fwd", grid=(B, nf),
                 in_specs=[seq(0), seq(nf), par(3, 0), par(3, nf), par(1, 0), par(1, nf)],
                 out_specs=[seq(0)], out_shape=[_sds((T, F), BF16)], sem=("parallel", "parallel"), comm=comm)


def _convffn_bwd(u, dact, cw, cb, B, S, F, comm=None):
    T = u.shape[0]
    tn = _tile(F, 256)
    nf = F // tn

    def body(ug_ref, uv_ref, da_ref, wg_ref, wv_ref, bg_ref, bv_ref, dug_ref, duv_ref, dpg_ref, dpv_ref):
        b = pl.program_id(1)
        ug, uv, da = ug_ref[...], uv_ref[...], da_ref[...]
        g, ug1, ug2 = _conv3(ug, wg_ref, bg_ref[...])
        val, uv1, uv2 = _conv3(uv, wv_ref, bv_ref[...])
        gel, dgel = _gelu_parts(g)
        dg = da * val * dgel
        dval = da * gel
        row = lax.broadcasted_iota(jnp.int32, ug.shape, 0)

        def back(d, w_ref):
            d1 = jnp.where(row < S - 1, pltpu.roll(d, S - 1, 0), 0.0)
            d2 = jnp.where(row < S - 2, pltpu.roll(d, S - 2, 0), 0.0)
            return w_ref[2:3, :] * d + w_ref[1:2, :] * d1 + w_ref[0:1, :] * d2

        dug_ref[...] = back(dg, wg_ref).astype(BF16)
        duv_ref[...] = back(dval, wv_ref).astype(BF16)

        def sums(d, u0, u1, u2):
            r8 = lax.broadcasted_iota(jnp.int32, (8, d.shape[1]), 0)
            out = jnp.zeros((8, d.shape[1]), F32)
            ones = jnp.ones((8, d.shape[0]), BF16)
            for k, t in enumerate((d * u2, d * u1, d * u0, d)):
                out = jnp.where(r8 == k, jnp.dot(ones, t.astype(BF16), preferred_element_type=F32), out)
            return out

        _accumulate(dpg_ref, sums(dg, ug, ug1, ug2), b == 0)
        _accumulate(dpv_ref, sums(dval, uv, uv1, uv2), b == 0)

    def seq(off):
        return pl.BlockSpec((S, tn), lambda j, b: (b, off + j))

    def par(rows, off):
        return pl.BlockSpec((rows, tn), lambda j, b: (0, off + j))

    outs, landed = _call(
        body, [u, u, dact, cw, cw, cb, cb], name="convffn_bwd", grid=(nf, B),
        in_specs=[seq(0), seq(nf), seq(0), par(3, 0), par(3, nf), par(1, 0), par(1, nf)],
        out_specs=[seq(0), seq(0), par(8, 0), par(8, 0)],
        out_shape=[_sds((T, F), BF16), _sds((T, F), BF16), _sds((8, F), F32), _sds((8, F), F32)],
        sem=("parallel", "arbitrary"), comm=comm)
    return outs, landed


def _tail(ff, x1, tgt, g):
    T, D = ff.shape
    tm = _tile(T, ROW_TILE, 16)

    def body(ff_ref, x1_ref, t_ref, g_ref, dy_ref, dff_ref, loss_ref, dg_ref):
        i = pl.program_id(0)
        f = ff_ref[...]
        gv = g_ref[...]
        r = _rms(f)
        n = f * r
        e = (x1_ref[...] + n * gv) - t_ref[...]
        dy = e * (1.0 / D)
        dy_ref[...] = dy
        dn = dy * gv
        dff_ref[...] = (r * (dn - n * jnp.mean(dn * n, axis=-1, keepdims=True))).astype(BF16)
        part = 0.5 * jnp.sum(jnp.mean(e * e, axis=-1, keepdims=True), axis=0, keepdims=True)
        _accumulate(loss_ref, jnp.broadcast_to(part, loss_ref.shape), i == 0)
        _accumulate(dg_ref, jnp.sum(dy * n, axis=0, keepdims=True), i == 0)

    row = pl.BlockSpec((tm, D), lambda i: (i, 0))
    vec = pl.BlockSpec((1, D), lambda i: (0, 0))
    return _call(body, [ff, x1, tgt, g], name="tail", grid=(T // tm,), in_specs=[row, row, row, vec],
                 out_specs=[row, row, pl.BlockSpec((8, LANES), lambda i: (0, 0)), vec],
                 out_shape=[_sds((T, D), F32), _sds((T, D), BF16), _sds((8, LANES), F32), _sds((1, D), F32)],
                 sem=("arbitrary",))[0]


def _mid_bwd(dy, dh2, x1, y1, g_ffn, g_pm, comm=None):
    T, D = dy.shape
    tm = _tile(T, ROW_TILE, 16)

    def body(dy_ref, dh_ref, x1_ref, y1_ref, gf_ref, gp_ref, dx1_ref, dy1_ref, dgf_ref, dgp_ref):
        i = pl.program_id(0)
        dh = dh_ref[...]
        d2, dgf = _rms_bwd(dh, x1_ref[...], gf_ref[...])
        dx1 = dy_ref[...] + d2
        dx1_ref[...] = dx1
        d1, dgp = _rms_bwd(dx1, y1_ref[...], gp_ref[...])
        dy1_ref[...] = d1.astype(BF16)
        _accumulate(dgf_ref, jnp.sum(dgf, axis=0, keepdims=True), i == 0)
        _accumulate(dgp_ref, jnp.sum(dgp, axis=0, keepdims=True), i == 0)

    row = pl.BlockSpec((tm, D), lambda i: (i, 0))
    vec = pl.BlockSpec((1, D), lambda i: (0, 0))
    return _call(body, [dy, dh2, x1, y1, g_ffn, g_pm], name="mid_bwd", grid=(T // tm,),
                 in_specs=[row, row, row, row, vec, vec], out_specs=[row, row, vec, vec],
                 out_shape=[_sds((T, D), F32), _sds((T, D), BF16), _sds((1, D), F32), _sds((1, D), F32)],
                 sem=("arbitrary",), comm=comm)


def _gate_bwd(dy1, w_out, am, af, proj, bgate, lay, D, comm=None):
    T = dy1.shape[0]
    tm = _tile(T, 2 * ROW_TILE, 16)
    tn = _tile(D, 1024)
    NT = (((1,), (1,)), ((), ()))

    def body(dy_ref, w_ref, am_ref, af_ref, gm_ref, gf_ref, bm_ref, bf_ref,
             dam_ref, daf_ref, dgm_ref, dgf_ref, dbm_ref, dbf_ref):
        i = pl.program_id(1)
        d = lax.dot_general(dy_ref[...], w_ref[...], NT, preferred_element_type=F32)
        sm = _sigmoid(gm_ref[...] + bm_ref[...])
        sf = _sigmoid(gf_ref[...] + bf_ref[...])
        dam_ref[...] = (d * sm).astype(BF16)
        daf_ref[...] = (d * sf).astype(BF16)
        dgm = d * am_ref[...] * (sm * (1.0 - sm))
        dgf = d * af_ref[...] * (sf * (1.0 - sf))
        dgm_ref[...] = dgm.astype(BF16)
        dgf_ref[...] = dgf.astype(BF16)
        _accumulate(dbm_ref, jnp.sum(dgm, axis=0, keepdims=True), i == 0)
        _accumulate(dbf_ref, jnp.sum(dgf, axis=0, keepdims=True), i == 0)

    og = lay["g"] // tn
    blk = pl.BlockSpec((tm, tn), lambda j, i: (i, j))
    vec = pl.BlockSpec((1, tn), lambda j, i: (0, j))
    return _call(
        body, [dy1, w_out, am, af, proj, proj, bgate, bgate], name="mm_dmerged_gate_bwd", grid=(D // tn, T // tm),
        in_specs=[pl.BlockSpec((tm, D), lambda j, i: (i, 0)), pl.BlockSpec((tn, D), lambda j, i: (j, 0)),
                  blk, blk, pl.BlockSpec((tm, tn), lambda j, i: (i, og + j)),
                  pl.BlockSpec((tm, tn), lambda j, i: (i, og + D // tn + j)),
                  vec, pl.BlockSpec((1, tn), lambda j, i: (0, D // tn + j))],
        out_specs=[blk, blk, blk, blk, vec, vec],
        out_shape=[_sds((T, D), BF16)] * 4 + [_sds((1, D), F32)] * 2, sem=("parallel", "arbitrary"), comm=comm)


def _mla_bwd_prep(dq, dk, dv, cosT, sinT, comm=None):
    H, T, _ = dq.shape
    tm = _tile(T, HEAD_ROW_TILE, 16)

    def body(dq_ref, dk_ref, dv_ref, cos_ref, sin_ref, dqr_ref, dkv_ref, dkpe_ref):
        h = pl.program_id(1)
        cs, sn = cos_ref[...], sin_ref[...]
        valid = _lane(cs.shape) < ROPE

        def unrope(d):
            d = jnp.where(valid, d, 0.0)
            return d * cs - _rope_rot(d) * sn

        dqv = dq_ref[...]
        dqr_ref[:, :NOPE] = dqv[:, :NOPE].astype(BF16)
        dqr_ref[:, NOPE:] = unrope(dqv[:, NOPE:]).astype(BF16)
        dkv_ = dk_ref[...]
        dkv_ref[:, :NOPE] = dkv_[:, :NOPE].astype(BF16)
        dkv_ref[:, NOPE:] = dv_ref[...].astype(BF16)
        _accumulate(dkpe_ref, unrope(dkv_[:, NOPE:]), h == 0)

    head = pl.BlockSpec((None, tm, ATT_DK), lambda i, h: (h, i, 0))
    tok = pl.BlockSpec((tm, LANES), lambda i, h: (i, 0))
    return _call(
        body, [dq, dk, dv, cosT, sinT], name="mla_bwd_prep", grid=(T // tm, H),
        in_specs=[head, head, pl.BlockSpec((None, tm, VDIM), lambda i, h: (h, i, 0)), tok, tok],
        out_specs=[head, head, tok],
        out_shape=[_sds((H, T, ATT_DK), BF16), _sds((H, T, ATT_DK), BF16), _sds((T, LANES), F32)],
        sem=("parallel", "arbitrary"), comm=comm)


def _fox_bwd_prep(dq, dk, proj, bfor, lay, B, S, inv_scale):
    H, T, _ = dq.shape

    def body(dq_ref, dk_ref, fl_ref, bf_ref, dfl_ref, dbf_ref, dc_sc):
        b, h = pl.program_id(0), pl.program_id(1)
        lane = _lane(dc_sc.shape)
        col = jnp.sum(jnp.where(lane == 0, dq_ref[...], 0.0) - jnp.where(lane == 3, dk_ref[...], 0.0),
                      axis=1, keepdims=True)

        @pl.when(h == 0)
        def _():
            dc_sc[...] = jnp.zeros(dc_sc.shape, F32)

        dc_sc[...] = jnp.where(lane == h, col, dc_sc[...])

        @pl.when(h == H - 1)
        def _():
            dlogf = _cumsum_rows(dc_sc[...] * inv_scale, reverse=True)
            z = fl_ref[...] + bf_ref[...]
            dz = jnp.where(lane < H, dlogf * (1.0 / (1.0 + jnp.exp(z))), 0.0)
            dfl_ref[...] = dz
            _accumulate(dbf_ref, jnp.sum(dz, axis=0, keepdims=True), b == 0)

    aug = pl.BlockSpec((None, S, LANES), lambda b, h: (h, b, 1))
    seq = pl.BlockSpec((S, LANES), lambda b, h: (b, 0))
    vec = pl.BlockSpec((1, LANES), lambda b, h: (0, 0))
    return _call(
        body, [dq, dk, proj, bfor], name="fox_bwd_prep", grid=(B, H),
        in_specs=[aug, aug, pl.BlockSpec((S, LANES), lambda b, h: (b, lay["fl"] // LANES)), vec],
        out_specs=[seq, vec], out_shape=[_sds((T, LANES), F32), _sds((1, LANES), F32)],
        scratch_shapes=[pltpu.VMEM((S, LANES), F32)], sem=("arbitrary", "arbitrary"))[0]


def _lat_bwd(dqn, dkvn, proj, gq, gkv, lay):
    T = dqn.shape[0]
    tm = _tile(T, ROW_TILE, 16)

    def body(dq_ref, dkv_ref, q_ref, kv_ref, gq_ref, gkv_ref, dql_ref, dkl_ref, dgq_ref, dgkv_ref):
        i = pl.program_id(0)
        dql, dgq = _rms_bwd(dq_ref[...], q_ref[...], gq_ref[...])
        dkl, dgkv = _rms_bwd(dkv_ref[...], kv_ref[...], gkv_ref[...])
        dql_ref[...] = dql.astype(BF16)
        dkl_ref[...] = dkl.astype(BF16)
        _accumulate(dgq_ref, jnp.sum(dgq, axis=0, keepdims=True), i == 0)
        _accumulate(dgkv_ref, jnp.sum(dgkv, axis=0, keepdims=True), i == 0)

    def blk(width, off=0):
        return pl.BlockSpec((tm, width), lambda i: (i, off // width))

    def vec(width):
        return pl.BlockSpec((1, width), lambda i: (0, 0))

    return _call(
        body, [dqn, dkvn, proj, proj, gq, gkv], name="lat_bwd", grid=(T // tm,),
        in_specs=[blk(Q_LORA), blk(KV_LORA), blk(Q_LORA, lay["q"]), blk(KV_LORA, lay["kv"]), vec(Q_LORA), vec(KV_LORA)],
        out_specs=[blk(Q_LORA), blk(KV_LORA), vec(Q_LORA), vec(KV_LORA)],
        out_shape=[_sds((T, Q_LORA), BF16), _sds((T, KV_LORA), BF16), _sds((1, Q_LORA), F32), _sds((1, KV_LORA), F32)],
        sem=("arbitrary",))[0]


def _dh_final(dproj, w_perm, dx1, x, g, comm=None):
    T, D = x.shape
    K = dproj.shape[1]
    tm = _tile(T, 2 * ROW_TILE, 16)
    tk = _tile(K, 1024)
    nk = K // tk
    NT = (((1,), (1,)), ((), ()))

    def body(a_ref, b_ref, dx1_ref, x_ref, g_ref, dx_ref, dg_ref):
        i, k = pl.program_id(0), pl.program_id(1)

        @pl.when(k == 0)
        def _():
            dx_ref[...] = jnp.zeros(dx_ref.shape, F32)

        dx_ref[...] += lax.dot_general(a_ref[...], b_ref[...], NT, preferred_element_type=F32)

        @pl.when(k == nk - 1)
        def _():
            d, dg = _rms_bwd(dx_ref[...], x_ref[...], g_ref[...])
            dx_ref[...] = dx1_ref[...] + d
            _accumulate(dg_ref, jnp.sum(dg, axis=0, keepdims=True), i == 0)

    row = pl.BlockSpec((tm, D), lambda i, k: (i, 0))
    vec = pl.BlockSpec((1, D), lambda i, k: (0, 0))
    return _call(body, [dproj, w_perm, dx1, x, g], name="mm_dh_final", grid=(T // tm, nk),
                 in_specs=[pl.BlockSpec((tm, tk), lambda i, k: (i, k)), pl.BlockSpec((D, tk), lambda i, k: (0, k)),
                           row, row, vec],
                 out_specs=[row, vec], out_shape=[_sds((T, D), F32), _sds((1, D), F32)],
                 sem=("arbitrary", "arbitrary"), comm=comm)


def _chip_sum(pieces, paired, qc, name):
    G, R, C = pieces.shape
    tr = _tile(R, 256, 16)

    def body(qc_ref, g_ref, p_ref, keep_ref, send_ref):
        s = pl.program_id(1)
        tot = g_ref[...] + p_ref[...]

        @pl.when(s == 0)
        def _():
            keep_ref[...] = tot

        @pl.when(s > 0)
        def _():
            send_ref[...] = tot.astype(send_ref.dtype)

    grid_spec = pltpu.PrefetchScalarGridSpec(
        num_scalar_prefetch=1, grid=(R // tr, N_CHIP),
        in_specs=[pl.BlockSpec((None, tr, C), lambda i, s, qc: (2 * (qc[0] ^ s) + qc[1], i, 0)),
                  pl.BlockSpec((None, tr, C), lambda i, s, qc: (qc[0] ^ s, i, 0))],
        out_specs=[pl.BlockSpec((tr, C), lambda i, s, qc: (i, 0)),
                   pl.BlockSpec((None, tr, C), lambda i, s, qc: (jnp.maximum(s - 1, 0), i, 0))])
    send_dtype = BF16 if R >= 16 else pieces.dtype
    return pl.pallas_call(
        body, name=name, grid_spec=grid_spec,
        out_shape=[_sds((R, C), F32), _sds((3, R, C), send_dtype)],
        compiler_params=pltpu.CompilerParams(dimension_semantics=("arbitrary", "arbitrary"),
                                             vmem_limit_bytes=VMEM_LIMIT_BYTES),
    )(qc, pieces, paired)


def _adamw_math(w, g, m, v):
    m = ADAM_B1 * m + (1.0 - ADAM_B1) * g
    v = ADAM_B2 * v + (1.0 - ADAM_B2) * (g * g)
    m_hat = m / (1.0 - ADAM_B1 ** ADAM_STEP)
    v_hat = v / (1.0 - ADAM_B2 ** ADAM_STEP)
    delta = -ADAM_LR * (m_hat / (jnp.sqrt(v_hat) + ADAM_EPS) + ADAM_WD * w)
    return delta, m, v


def _sum_adamw(keep, pieces, w, m, v, name):
    R, C = w.shape
    P = pieces.shape[0]
    tr = _tile(R, 256, 16)

    def body(k_ref, p_ref, w_ref, m_ref, v_ref, g_ref, d_ref, mo_ref, vo_ref):
        g = k_ref[...]
        for q in range(P):
            g = g + p_ref[q].astype(F32)
        g_ref[...] = g
        d_ref[...], mo_ref[...], vo_ref[...] = _adamw_math(w_ref[...], g, m_ref[...], v_ref[...])

    blk = pl.BlockSpec((tr, C), lambda i: (i, 0))
    pblk = pl.BlockSpec((P, tr, C), lambda i: (0, i, 0))
    return _call(body, [keep, pieces, w, m, v], name=name, grid=(R // tr,), in_specs=[blk, pblk, blk, blk, blk],
                 out_specs=[blk] * 4, out_shape=[_sds((R, C), F32)] * 4, sem=("parallel",))[0]


def _adamw_small(parts, w, m, v, widths):
    n = len(widths)

    def body(p_ref, w_ref, m_ref, v_ref, *o_refs):
        g = p_ref[0]
        for q in range(1, N_DEV):
            g = g + p_ref[q]
        vals = (g,) + _adamw_math(w_ref[...], g, m_ref[...], v_ref[...])
        off = 0
        for i, wd in enumerate(widths):
            for kind in range(4):
                o_refs[4 * i + kind][...] = vals[kind][:, off:off + wd]
            off += wd

    whole = pl.BlockSpec(memory_space=pltpu.VMEM)
    outs = _call(body, [parts, w, m, v], name="adamw_small", grid=(), in_specs=[whole] * 4,
                 out_specs=[whole] * (4 * n), out_shape=[_sds((1, wd), F32) for wd in widths for _ in range(4)])[0]
    return [tuple(outs[4 * i:4 * i + 4]) for i in range(n)]


def _layout(D):
    lay = {"q": 0, "kv": Q_LORA, "kpe": Q_LORA + KV_LORA}
    lay["fq"] = lay["kpe"] + LANES
    lay["fk"] = lay["fq"] + HEADS * FOX_DIM
    lay["fv"] = lay["fk"] + HEADS * FOX_DIM
    lay["fl"] = lay["fv"] + HEADS * FOX_DIM
    lay["g"] = lay["fl"] + LANES
    lay["end"] = lay["g"] + 2 * D
    return lay


def kernel(x, positions, pre_mix_norm, w_in, q_a_norm, w_uq, kv_a_norm, w_ukv, b_forget, b_gate, w_branch_mla, w_branch_fox, w_out, post_mix_norm, pre_ffn_norm, w_up, conv_w, conv_b, w_down, post_ffn_norm, loss_target, m_pre_mix_norm, m_w_in, m_q_a_norm, m_w_uq, m_kv_a_norm, m_w_ukv, m_b_forget, m_b_gate, m_w_branch_mla, m_w_branch_fox, m_w_out, m_post_mix_norm, m_pre_ffn_norm, m_w_up, m_conv_w, m_conv_b, m_w_down, m_post_ffn_norm, v_pre_mix_norm, v_w_in, v_q_a_norm, v_w_uq, v_kv_a_norm, v_w_ukv, v_b_forget, v_b_gate, v_w_branch_mla, v_w_branch_fox, v_w_out, v_post_mix_norm, v_pre_ffn_norm, v_w_up, v_conv_w, v_conv_b, v_w_down, v_post_ffn_norm):
    B, S, D = x.shape
    T = B * S
    F = conv_b.shape[0] // 2
    lay = _layout(D)
    n_in = w_in.shape[1]
    d_in = N_DEV * n_in
    seg_a = Q_LORA + KV_LORA + ROPE
    seg_b = 3 * HEADS * FOX_DIM + HEADS
    mla_scale = (NOPE + ROPE) ** -0.5
    fox_scale = FOX_DIM ** -0.5
    ax, ay, ac = (lax.axis_index(a) for a in MESH_AXES)
    qc = jnp.stack([2 * ax + ay, ac]).astype(jnp.int32)

    def row(vec, width=None):
        vec = vec.reshape(1, -1)
        if width is not None and vec.shape[1] < width:
            vec = jnp.pad(vec, ((0, 0), (0, width - vec.shape[1])))
        return vec

    x2 = x.reshape(T, D)
    win_s = _cast_bf16(w_in, "cast_w_in")
    h, (win_g,) = _prenorm(x2, row(pre_mix_norm), comm=_Comm([_GatherRelayPlan([win_s], mid_frac=0.3)]))
    small_s = [_cast_bf16(w, "cast_" + n) for w, n in
               [(w_uq, "w_uq"), (w_ukv, "w_ukv"), (w_branch_mla, "w_branch_mla"), (w_branch_fox, "w_branch_fox"), (w_out, "w_out")]]
    wup_s = _cast_bf16(w_up, "cast_w_up")
    wdown_s = _cast_bf16(w_down, "cast_w_down")

    def shard_cols(lo, hi):
        out = []
        for g in range(lo // n_in, (hi - 1) // n_in + 1):
            out.append(win_g[g][:, max(lo, g * n_in) - g * n_in:min(hi, (g + 1) * n_in) - g * n_in])
        return out

    w_perm = jnp.concatenate(
        shard_cols(0, seg_a) + [jnp.zeros((D, LANES - ROPE), BF16)] + shard_cols(seg_a, seg_a + seg_b)
        + [jnp.zeros((D, LANES - HEADS), BF16)] + shard_cols(seg_a + seg_b, d_in), axis=1)

    tgt = loss_target.reshape(T, D)
    pos = positions.reshape(T, 1)
    inv_freq = 1.0 / (ROPE_THETA ** (jnp.arange(0, ROPE, 2, dtype=F32) / ROPE))
    invf = row(jnp.concatenate([inv_freq, inv_freq]), LANES)
    g_pre, g_q, g_kv = row(pre_mix_norm), row(q_a_norm), row(kv_a_norm)
    g_pm, g_ffn, g_pf = row(post_mix_norm), row(pre_ffn_norm), row(post_ffn_norm)
    bfor = row(b_forget, LANES)
    bgate = row(b_gate)
    cb_full = row(conv_b)

    def own_plan(blocks):
        return _Comm([_GatherOwnPlan(blocks)])

    def pass_plan(gathered):
        return _Comm([_GatherPassPlan(gathered)])

    def pair_plan(gs):
        return _Comm([_PairScatterPlan(gs)])

    def chip_plan(gs):
        return _Comm([_ChipScatterPlan(gs)])

    half_d = D // 2
    proj, landed = _matmul(h, w_perm, mode="nn", name="mm_proj", comm=_Comm(
        [_GatherOwnPlan(small_s[:2] + [conv_w]), _GatherOwnPlan([wup_s], rows=(0, half_d))]))
    early_g, wup_part = landed[:-1], landed[-1:]
    (qn, kvn, kper, logf, cosT, sinT), (wuq_g, wukv_g, cw_g) = _split_prep(
        proj, pos, invf, g_q, g_kv, bfor, lay, comm=pass_plan(early_g))
    wuq_pad = jnp.pad(wuq_g, ((0, 0), (0, 0), (0, ATT_DK - NOPE - ROPE)))
    cw_full = jnp.transpose(cw_g, (1, 0, 2)).reshape(3, 2 * F)

    qraw = _matmul(qn, wuq_pad, mode="nn", name="mm_q", out_blocks=ATT_DK, tm=T)
    kvraw = _matmul(kvn, wukv_g, mode="nn", name="mm_kv", out_blocks=NOPE + VDIM, tm=T)
    (q_mla, k_mla, v_mla), branch_half = _mla_prep(qraw, kvraw, kper, cosT, sinT, comm=own_plan(small_s[2:4]))
    cs = _fox_cumsum(logf, B, S, 1.0 / fox_scale)
    (q_fox, k_fox, v_fox), wout_half = _fox_prep(proj, cs, lay, comm=own_plan(small_s[4:5]))
    ((o_mla, lse_mla), (o_fox, lse_fox)), landed = _attn_fwd(
        [(q_mla, k_mla, v_mla, MLA_UNIT, mla_scale), (q_fox, k_fox, v_fox, 1, fox_scale)], B=B, S=S,
        name="attn_fwd", comm=_Comm([_GatherOwnPlan([wup_s], rows=(half_d, D), into=wup_part),
                                     _GatherPassPlan(branch_half + wout_half)]))
    wup_half, (wbm_g, wbf_g, wout_g) = landed[:1], landed[1:]
    wbm = jnp.transpose(wbm_g, (1, 0, 2)).reshape(HEADS * VDIM, D)
    wbf = jnp.transpose(wbf_g, (1, 0, 2)).reshape(HEADS * FOX_DIM, D)
    wout = wout_g.reshape(D, D)
    (a_m, a_f, merged), (wup_g,) = _branch_merge(o_mla, o_fox, wbm, wbf, proj, bgate, lay, D,
                                                 comm=pass_plan(wup_half))
    n_up = wup_g.shape[2]
    y1, x1, h2 = _out_mid(merged, wout, x2, g_pm, g_ffn)
    u, wdown_half = _matmul(h2, wup_g, mode="nn", name="mm_up", tn=n_up, comm=own_plan([wdown_s]))
    (act,), (wdown_g,) = _convffn_fwd(u, cw_full, cb_full, B, S, F, comm=pass_plan(wdown_half))
    wdown = wdown_g.reshape(F, D)
    ff = _matmul(act, wdown, mode="nn", name="mm_down", tk=F // 2)
    dy, dff, loss_part, dg_pf = _tail(ff, x1, tgt, g_pf)

    dact = _matmul(dff, wdown, mode="nt", name="mm_dact", tn=F // 4)
    dw_down = _matmul(act, dff, mode="tn", name="mm_dw_down", tm=F // 4, tn=512).reshape(N_DEV, F // N_DEV, D)
    (du_g, du_v, dcp_g, dcp_v), (pa_down,) = _convffn_bwd(u, dact, cw_full, cb_full, B, S, F, comm=pair_plan([dw_down]))
    keep_down, sb_down = _chip_sum(dw_down, pa_down, qc, "chipsum_w_down")
    dh2, (rb_down,) = _matmul_halves((du_g, du_v), wup_g, mode="nt", name="mm_dh2", tm=MM_TILE, comm=chip_plan([sb_down]))
    dw_up = _matmul_halves(h2, (du_g, du_v), mode="tn", name="mm_dw_up", tm=MM_TILE, tn=n_up, tk=T // 2)
    (dx1, dy1, dg_ffn, dg_pm), _ = _mid_bwd(dy, dh2, x1, y1, g_ffn, g_pm)
    dw_out, pa_up_part = _matmul(merged, dy1, mode="tn", name="mm_dw_out",
                                 comm=_Comm([_PairScatterPlan([dw_up], rows=(0, half_d))]))
    dw_out = dw_out.reshape(N_DEV, D // N_DEV, D)
    (da_m, da_f, dgl_m, dgl_f, dbg_m, dbg_f), (pa_up,) = _gate_bwd(
        dy1, wout, a_m, a_f, proj, bgate, lay, D,
        comm=_Comm([_PairScatterPlan([dw_up], rows=(half_d, D), into=pa_up_part)]))
    keep_up, sb_up = _chip_sum(dw_up, pa_up, qc, "chipsum_w_up")
    dw_bm = _matmul(o_mla, da_m, mode="tn", name="mm_dw_branch_mla", out_blocks=D // N_DEV)
    dw_bf = _matmul(o_fox, da_f, mode="tn", name="mm_dw_branch_fox", out_blocks=D // N_DEV)
    mix = [dw_out, dw_bm, dw_bf]
    do_mla, pa_mix = _matmul(da_m, wbm, mode="nt", name="mm_do_mla", out_dtype=BF16, comm=pair_plan(mix))
    do_fox = _matmul(da_f, wbf, mode="nt", name="mm_do_fox", out_dtype=BF16)
    mix_sums = [_chip_sum(g, p, qc, "chipsum_" + n) for g, p, n in zip(mix, pa_mix, ["w_out", "w_branch_mla", "w_branch_fox"])]
    ((dq_m, dk_m, dv_m), (dq_f, dk_f, dv_f)), (rb_up,) = _attn_bwd(
        [(q_mla, k_mla, v_mla, o_mla, do_mla, lse_mla, MLA_UNIT, mla_scale),
         (q_fox, k_fox, v_fox, o_fox, do_fox, lse_fox, 1, fox_scale)], B=B, S=S, name="attn_bwd",
        comm=chip_plan([sb_up]))
    (dqraw, dkvraw, dkpe), _ = _mla_bwd_prep(dq_m, dk_m, dv_m, cosT, sinT)
    dqn = _matmul(dqraw, wuq_pad, mode="nt", name="mm_dqn", tm=T)
    dw_uq = _matmul(qn, dqraw, mode="tn", name="mm_dw_uq", out_blocks=ATT_DK)[:, :, :NOPE + ROPE]
    dkvn = _matmul(dkvraw, wukv_g, mode="nt", name="mm_dkvn", tm=T)
    dw_ukv = _matmul(kvn, dkvraw, mode="tn", name="mm_dw_ukv", out_blocks=NOPE + VDIM)
    dqlat, dkvlat, dg_q, dg_kv = _lat_bwd(dqn, dkvn, proj, g_q, g_kv, lay)
    dfl, dbfor = _fox_bwd_prep(dq_f, dk_f, proj, bfor, lay, B, S, 1.0 / fox_scale)
    dproj = _concat_cols([dqlat, dkvlat, dkpe, dq_f, dk_f, dv_f, dfl, dgl_m, dgl_f], "concat_dproj")
    dw_perm, rb_mix = _matmul(h, dproj, mode="tn", name="mm_dw_in", comm=chip_plan([s[1] for s in mix_sums]))
    segs = [(0, seg_a, 0), (seg_a, seg_a + seg_b, lay["fq"] - seg_a), (seg_a + seg_b, d_in, lay["g"] - seg_a - seg_b)]

    def piece(g):
        lo, hi = g * n_in, (g + 1) * n_in
        parts = [dw_perm[:, max(lo, s0) + sh:min(hi, s1) + sh] for s0, s1, sh in segs if max(lo, s0) < min(hi, s1)]
        return parts[0] if len(parts) == 1 else jnp.concatenate(parts, axis=1)

    dw_in = jnp.stack([piece(g) for g in range(N_DEV)])
    dcw = jnp.transpose(jnp.concatenate([dcp_g[0:3], dcp_v[0:3]], axis=1).reshape(3, N_DEV, (2 * F) // N_DEV), (1, 0, 2))
    late = [dw_in, dw_uq, dw_ukv, dcw]
    pa_late = _exchange_alone(pair_plan(late), "pair_late")
    late_sums = [_chip_sum(g, p, qc, "chipsum_" + n) for g, p, n in zip(late, pa_late, ["w_in", "w_uq", "w_ukv", "conv_w"])]
    (grad_x, dg_pre), rb_late = _dh_final(dproj, w_perm, dx1, x2, g_pre, comm=chip_plan([s[1] for s in late_sums]))

    big_out = {}

    def finish(n, keep, pieces, w, m, v):
        big_out[n] = _sum_adamw(keep, pieces, w, m, v, "adamw_" + n)

    finish("w_down", keep_down, rb_down, w_down, m_w_down, v_w_down)
    finish("w_up", keep_up, rb_up, w_up, m_w_up, v_w_up)
    finish("w_out", mix_sums[0][0], rb_mix[0], w_out, m_w_out, v_w_out)
    finish("w_branch_mla", mix_sums[1][0], rb_mix[1], w_branch_mla, m_w_branch_mla, v_w_branch_mla)
    finish("w_branch_fox", mix_sums[2][0], rb_mix[2], w_branch_fox, m_w_branch_fox, v_w_branch_fox)
    finish("w_in", late_sums[0][0], rb_late[0], w_in, m_w_in, v_w_in)
    finish("w_uq", late_sums[1][0], rb_late[1], w_uq, m_w_uq, v_w_uq)
    finish("w_ukv", late_sums[2][0], rb_late[2], w_ukv, m_w_ukv, v_w_ukv)
    finish("conv_w", late_sums[3][0], rb_late[3], conv_w, m_conv_w, v_conv_w)

    widths = [D, Q_LORA, KV_LORA, LANES, 2 * D, D, D, 2 * F, D]
    small_names = ["pre_mix_norm", "q_a_norm", "kv_a_norm", "b_forget", "b_gate", "post_mix_norm", "pre_ffn_norm",
                   "conv_b", "post_ffn_norm"]
    true_w = [D, Q_LORA, KV_LORA, HEADS, 2 * D, D, D, 2 * F, D]
    dcb = jnp.concatenate([dcp_g[3:4], dcp_v[3:4]], axis=1)
    part = jnp.concatenate([dg_pre, dg_q, dg_kv, dbfor, dbg_m, dbg_f, dg_pm, dg_ffn, dcb, dg_pf], axis=1)

    def pack(vals):
        return jnp.concatenate([row(a, wd) for a, wd in zip(vals, widths)], axis=1)

    sw = pack([pre_mix_norm, q_a_norm, kv_a_norm, b_forget, b_gate, post_mix_norm, pre_ffn_norm, conv_b, post_ffn_norm])
    sm = pack([m_pre_mix_norm, m_q_a_norm, m_kv_a_norm, m_b_forget, m_b_gate, m_post_mix_norm, m_pre_ffn_norm,
               m_conv_b, m_post_ffn_norm])
    sv = pack([v_pre_mix_norm, v_q_a_norm, v_kv_a_norm, v_b_forget, v_b_gate, v_post_mix_norm, v_pre_ffn_norm,
               v_conv_b, v_post_ffn_norm])
    (parts_all,) = _exchange_alone(_Comm([_DirectGatherPlan([part])]), "gather_small")
    small = _adamw_small(parts_all, sw, sm, sv, widths)
    small_out = {n: tuple(a.reshape(-1)[:tw] for a in vals) for n, vals, tw in zip(small_names, small, true_w)}

    loss = lax.psum(loss_part[0, 0], MESH_AXES)
    order = ["pre_mix_norm", "w_in", "q_a_norm", "w_uq", "kv_a_norm", "w_ukv", "b_forget", "b_gate", "w_branch_mla",
             "w_branch_fox", "w_out", "post_mix_norm", "pre_ffn_norm", "w_up", "conv_w", "conv_b", "w_down",
             "post_ffn_norm"]
    res = {**big_out, **small_out}
    outs = [loss, grad_x.reshape(B, S, D)]
    for kind in range(4):
        outs += [res[n][kind] for n in order]
    return tuple(outs)
```
